```python
import math
import jax, jax.numpy as jnp
from jax import lax
import numpy as np

D_MODEL = 2048
BATCH = 8
SEQ = 4096
DEPTH = 1

D_MIX = D_MODEL
D_S5 = D_MIX // 2
D_GLA = D_MIX - D_S5
S5_GROUP = 16
S5_GROUPS = D_S5 // S5_GROUP
S5_STATE = 64
GLA_HEADS = 4
GLA_DK = D_GLA // 2
GLA_DV = D_GLA
GLA_HK = GLA_DK // GLA_HEADS
GLA_HV = GLA_DV // GLA_HEADS
GLA_RANK = 16
GLA_TAU = 16.0
GLA_CHUNK = 64
EPS = 1e-6
D_IN = 2 * D_S5 + 2 * GLA_DK + 2 * GLA_DV + GLA_RANK

kernel_name = "hybrid_s5_gla_parallel_heads"


def rmsnorm(x, w):
    xf = x.astype(jnp.float32)
    return xf * lax.rsqrt(jnp.mean(xf * xf, axis=-1, keepdims=True) + EPS) * w.astype(jnp.float32)


def s5_branch(u, A_re, A_im, B_re, B_im, C_re, C_im, D, log_dt, glu_w, glu_b):
    bsz, L, _ = u.shape
    ug = u.reshape(bsz, L, S5_GROUPS, S5_GROUP)
    a_re = jnp.minimum(A_re.astype(jnp.float32), -1e-4)
    A = lax.complex(a_re, A_im.astype(jnp.float32))
    dt = jnp.exp(log_dt.astype(jnp.float32))[:, None]
    A_bar = jnp.exp(A * dt)
    Bc = lax.complex(B_re.astype(jnp.float32), B_im.astype(jnp.float32))
    B_bar = ((A_bar - 1.0) / A)[..., None] * Bc
    Bu = jnp.einsum('blgh,gph->blgp', ug.astype(jnp.complex64), B_bar)
    a_seq = jnp.broadcast_to(A_bar, Bu.shape)

    def combine(e1, e2):
        a1, b1 = e1
        a2, b2 = e2
        return a1 * a2, a2 * b1 + b2

    _, states = lax.associative_scan(combine, (a_seq, Bu), axis=1)
    Cc = lax.complex(C_re.astype(jnp.float32), C_im.astype(jnp.float32))
    y = jnp.real(jnp.einsum('blgp,ghp->blgh', states, Cc))
    y = y + D.astype(jnp.float32).reshape(S5_GROUPS, S5_GROUP) * ug
    y = jax.nn.gelu(y.reshape(bsz, L, D_S5))
    return y * jax.nn.sigmoid(y @ glu_w.astype(jnp.float32) + glu_b.astype(jnp.float32))


def gla_branch(q, k, v, g_low, gate_up, gate_bias, norm_w):
    bsz, L, _ = q.shape
    n = L // GLA_CHUNK

    def heads(t, d):
        return t.reshape(bsz, n, GLA_CHUNK, GLA_HEADS, d).transpose(0, 3, 1, 2, 4)

    log_g = jax.nn.log_sigmoid(g_low @ gate_up.astype(jnp.float32)
                               + gate_bias.astype(jnp.float32)) / GLA_TAU
    q = heads(q, GLA_HK) * (GLA_HK ** -0.5)
    k = heads(k, GLA_HK)
    v = heads(v, GLA_HV)
    log_g = heads(log_g, GLA_HK)
    b = jnp.cumsum(log_g, axis=3)
    b_last = b[:, :, :, -1:, :]
    q_e = q * jnp.exp(b)
    k_e = k * jnp.exp(-b)
    k_tail = k * jnp.exp(b_last - b)
    mask = jnp.tril(jnp.ones((GLA_CHUNK, GLA_CHUNK), dtype=bool))
    attn = jnp.where(mask, jnp.einsum('bhncd,bhnsd->bhncs', q_e, k_e), 0.0)
    o_intra = jnp.einsum('bhncs,bhnse->bhnce', attn, v)

    kv = jnp.einsum('bhncd,bhnce->bhnde', k_tail, v)
    decay = jnp.exp(b_last[:, :, :, 0, :])

    def step(S, inp):
        kv_n, dec_n = inp
        return dec_n[..., None] * S + kv_n, S

    S0 = jnp.zeros((bsz, GLA_HEADS, GLA_HK, GLA_HV), jnp.float32)
    _, S_prev = lax.scan(step, S0, (kv.transpose(2, 0, 1, 3, 4), decay.transpose(2, 0, 1, 3)))
    S_prev = S_prev.transpose(1, 2, 0, 3, 4)
    o = o_intra + jnp.einsum('bhncd,bhnde->bhnce', q_e, S_prev)
    o = rmsnorm(o, norm_w)
    return o.transpose(0, 2, 3, 1, 4).reshape(bsz, L, GLA_DV)


def _fwd_setup_inputs(seed: int = 0) -> dict:
    key = jax.random.key(seed)
    ks = jax.random.split(key, 20)
    f32 = jnp.float32
    nrm = lambda k, shape, s: jax.random.normal(k, shape, f32) * s
    x = jax.random.normal(ks[0], (BATCH, SEQ, D_MODEL), f32)
    pre_norm_w = 1.0 + nrm(ks[1], (DEPTH, D_MODEL), 0.02)
    w_in = nrm(ks[2], (DEPTH, D_MODEL, D_IN), D_MODEL ** -0.5)
    s5_A_re = -0.5 + nrm(ks[3], (DEPTH, S5_GROUPS, S5_STATE), 0.01)
    s5_A_im = (math.pi * jnp.arange(S5_STATE, dtype=f32))[None, None, :] \
        + nrm(ks[4], (DEPTH, S5_GROUPS, S5_STATE), 0.01)
    bs = (2.0 * S5_GROUP) ** -0.5
    cs = (2.0 * S5_STATE) ** -0.5
    s5_B_re = nrm(ks[5], (DEPTH, S5_GROUPS, S5_STATE, S5_GROUP), bs)
    s5_B_im = nrm(ks[6], (DEPTH, S5_GROUPS, S5_STATE, S5_GROUP), bs)
    s5_C_re = nrm(ks[7], (DEPTH, S5_GROUPS, S5_GROUP, S5_STATE), cs)
    s5_C_im = nrm(ks[8], (DEPTH, S5_GROUPS, S5_GROUP, S5_STATE), cs)
    s5_D = nrm(ks[9], (DEPTH, D_S5), 1.0)
    s5_log_dt = jax.random.uniform(ks[10], (DEPTH, S5_GROUPS), f32,
                                   math.log(1e-3), math.log(1e-1))
    s5_glu_w = nrm(ks[11], (DEPTH, D_S5, D_S5), D_S5 ** -0.5)
    s5_glu_b = nrm(ks[12], (DEPTH, D_S5), 0.01)
    gla_gate_up = nrm(ks[13], (DEPTH, GLA_RANK, GLA_DK), GLA_RANK ** -0.5)
    gla_gate_bias = nrm(ks[14], (DEPTH, GLA_DK), 0.1)
    gla_norm_w = 1.0 + nrm(ks[15], (DEPTH, GLA_HV), 0.02)
    w_out = nrm(ks[16], (DEPTH, D_MIX, D_MODEL), D_MIX ** -0.5)
    post_norm_w = 1.0 + nrm(ks[17], (DEPTH, D_MODEL), 0.02)
    return {"x": x, "pre_norm_w": pre_norm_w, "w_in": w_in,
            "s5_A_re": s5_A_re, "s5_A_im": s5_A_im, "s5_B_re": s5_B_re, "s5_B_im": s5_B_im,
            "s5_C_re": s5_C_re, "s5_C_im": s5_C_im, "s5_D": s5_D, "s5_log_dt": s5_log_dt,
            "s5_glu_w": s5_glu_w, "s5_glu_b": s5_glu_b,
            "gla_gate_up": gla_gate_up, "gla_gate_bias": gla_gate_bias, "gla_norm_w": gla_norm_w,
            "w_out": w_out, "post_norm_w": post_norm_w}


def _fwd_reference(x, pre_norm_w, w_in, s5_A_re, s5_A_im, s5_B_re, s5_B_im, s5_C_re, s5_C_im,
              s5_D, s5_log_dt, s5_glu_w, s5_glu_b, gla_gate_up, gla_gate_bias, gla_norm_w,
              w_out, post_norm_w):
    out_dtype = x.dtype
    resid = x.astype(jnp.float32)
    splits = [D_S5, 2 * D_S5, 2 * D_S5 + GLA_DK, 2 * D_S5 + 2 * GLA_DK,
              2 * D_S5 + 2 * GLA_DK + GLA_DV, 2 * D_S5 + 2 * GLA_DK + 2 * GLA_DV]
    for l in range(DEPTH):
        h = rmsnorm(resid, pre_norm_w[l])
        proj = h @ w_in[l].astype(jnp.float32)
        s5_u, s5_z, q, k, v, gla_z, g_low = jnp.split(proj, splits, axis=-1)
        y_s5 = s5_branch(s5_u, s5_A_re[l], s5_A_im[l], s5_B_re[l], s5_B_im[l],
                         s5_C_re[l], s5_C_im[l], s5_D[l], s5_log_dt[l],
                         s5_glu_w[l], s5_glu_b[l]) * jax.nn.silu(s5_z)
        y_gla = gla_branch(q, k, v, g_low, gla_gate_up[l], gla_gate_bias[l],
                           gla_norm_w[l]) * jax.nn.silu(gla_z)
        mixed = jnp.concatenate([y_s5, y_gla], axis=-1) @ w_out[l].astype(jnp.float32)
        resid = resid + rmsnorm(mixed, post_norm_w[l])
    return resid.astype(out_dtype)


import jax as _jax
import jax.numpy as _jnp

TWIN_FORMAT = 'train_step'
FWD_PARAMS = ['x', 'pre_norm_w', 'w_in', 's5_A_re', 's5_A_im', 's5_B_re', 's5_B_im', 's5_C_re', 's5_C_im', 's5_D', 's5_log_dt', 's5_glu_w', 's5_glu_b', 'gla_gate_up', 'gla_gate_bias', 'gla_norm_w', 'w_out', 'post_norm_w']
TWIN_WEIGHTS = ['pre_norm_w', 'w_in', 's5_A_re', 's5_A_im', 's5_B_re', 's5_B_im', 's5_C_re', 's5_C_im', 's5_D', 's5_log_dt', 's5_glu_w', 's5_glu_b', 'gla_gate_up', 'gla_gate_bias', 'gla_norm_w', 'w_out', 'post_norm_w']
TWIN_DIFF_INPUT = 'x'
TWIN_INPUTS = ['x', 'pre_norm_w', 'w_in', 's5_A_re', 's5_A_im', 's5_B_re', 's5_B_im', 's5_C_re', 's5_C_im', 's5_D', 's5_log_dt', 's5_glu_w', 's5_glu_b', 'gla_gate_up', 'gla_gate_bias', 'gla_norm_w', 'w_out', 'post_norm_w', 'loss_target', 'm_pre_norm_w', 'm_w_in', 'm_s5_A_re', 'm_s5_A_im', 'm_s5_B_re', 'm_s5_B_im', 'm_s5_C_re', 'm_s5_C_im', 'm_s5_D', 'm_s5_log_dt', 'm_s5_glu_w', 'm_s5_glu_b', 'm_gla_gate_up', 'm_gla_gate_bias', 'm_gla_norm_w', 'm_w_out', 'm_post_norm_w', 'v_pre_norm_w', 'v_w_in', 'v_s5_A_re', 'v_s5_A_im', 'v_s5_B_re', 'v_s5_B_im', 'v_s5_C_re', 'v_s5_C_im', 'v_s5_D', 'v_s5_log_dt', 'v_s5_glu_w', 'v_s5_glu_b', 'v_gla_gate_up', 'v_gla_gate_bias', 'v_gla_norm_w', 'v_w_out', 'v_post_norm_w']
TWIN_OUTPUTS = ['loss', 'grad_x', 'grad_pre_norm_w', 'grad_w_in', 'grad_s5_A_re', 'grad_s5_A_im', 'grad_s5_B_re', 'grad_s5_B_im', 'grad_s5_C_re', 'grad_s5_C_im', 'grad_s5_D', 'grad_s5_log_dt', 'grad_s5_glu_w', 'grad_s5_glu_b', 'grad_gla_gate_up', 'grad_gla_gate_bias', 'grad_gla_norm_w', 'grad_w_out', 'grad_post_norm_w', 'delta_pre_norm_w', 'delta_w_in', 'delta_s5_A_re', 'delta_s5_A_im', 'delta_s5_B_re', 'delta_s5_B_im', 'delta_s5_C_re', 'delta_s5_C_im', 'delta_s5_D', 'delta_s5_log_dt', 'delta_s5_glu_w', 'delta_s5_glu_b', 'delta_gla_gate_up', 'delta_gla_gate_bias', 'delta_gla_norm_w', 'delta_w_out', 'delta_post_norm_w', 'new_m_pre_norm_w', 'new_m_w_in', 'new_m_s5_A_re', 'new_m_s5_A_im', 'new_m_s5_B_re', 'new_m_s5_B_im', 'new_m_s5_C_re', 'new_m_s5_C_im', 'new_m_s5_D', 'new_m_s5_log_dt', 'new_m_s5_glu_w', 'new_m_s5_glu_b', 'new_m_gla_gate_up', 'new_m_gla_gate_bias', 'new_m_gla_norm_w', 'new_m_w_out', 'new_m_post_norm_w', 'new_v_pre_norm_w', 'new_v_w_in', 'new_v_s5_A_re', 'new_v_s5_A_im', 'new_v_s5_B_re', 'new_v_s5_B_im', 'new_v_s5_C_re', 'new_v_s5_C_im', 'new_v_s5_D', 'new_v_s5_log_dt', 'new_v_s5_glu_w', 'new_v_s5_glu_b', 'new_v_gla_gate_up', 'new_v_gla_gate_bias', 'new_v_gla_norm_w', 'new_v_w_out', 'new_v_post_norm_w']
TWIN_LEAF_KINDS = {'loss': 'loss', 'grad_x': 'grad_x', 'grad_pre_norm_w': 'grad_w', 'grad_w_in': 'grad_w', 'grad_s5_A_re': 'grad_w', 'grad_s5_A_im': 'grad_w', 'grad_s5_B_re': 'grad_w', 'grad_s5_B_im': 'grad_w', 'grad_s5_C_re': 'grad_w', 'grad_s5_C_im': 'grad_w', 'grad_s5_D': 'grad_w', 'grad_s5_log_dt': 'grad_w', 'grad_s5_glu_w': 'grad_w', 'grad_s5_glu_b': 'grad_w', 'grad_gla_gate_up': 'grad_w', 'grad_gla_gate_bias': 'grad_w', 'grad_gla_norm_w': 'grad_w', 'grad_w_out': 'grad_w', 'grad_post_norm_w': 'grad_w', 'delta_pre_norm_w': 'delta_w', 'delta_w_in': 'delta_w', 'delta_s5_A_re': 'delta_w', 'delta_s5_A_im': 'delta_w', 'delta_s5_B_re': 'delta_w', 'delta_s5_B_im': 'delta_w', 'delta_s5_C_re': 'delta_w', 'delta_s5_C_im': 'delta_w', 'delta_s5_D': 'delta_w', 'delta_s5_log_dt': 'delta_w', 'delta_s5_glu_w': 'delta_w', 'delta_s5_glu_b': 'delta_w', 'delta_gla_gate_up': 'delta_w', 'delta_gla_gate_bias': 'delta_w', 'delta_gla_norm_w': 'delta_w', 'delta_w_out': 'delta_w', 'delta_post_norm_w': 'delta_w', 'new_m_pre_norm_w': 'new_m', 'new_m_w_in': 'new_m', 'new_m_s5_A_re': 'new_m', 'new_m_s5_A_im': 'new_m', 'new_m_s5_B_re': 'new_m', 'new_m_s5_B_im': 'new_m', 'new_m_s5_C_re': 'new_m', 'new_m_s5_C_im': 'new_m', 'new_m_s5_D': 'new_m', 'new_m_s5_log_dt': 'new_m', 'new_m_s5_glu_w': 'new_m', 'new_m_s5_glu_b': 'new_m', 'new_m_gla_gate_up': 'new_m', 'new_m_gla_gate_bias': 'new_m', 'new_m_gla_norm_w': 'new_m', 'new_m_w_out': 'new_m', 'new_m_post_norm_w': 'new_m', 'new_v_pre_norm_w': 'new_v', 'new_v_w_in': 'new_v', 'new_v_s5_A_re': 'new_v', 'new_v_s5_A_im': 'new_v', 'new_v_s5_B_re': 'new_v', 'new_v_s5_B_im': 'new_v', 'new_v_s5_C_re': 'new_v', 'new_v_s5_C_im': 'new_v', 'new_v_s5_D': 'new_v', 'new_v_s5_log_dt': 'new_v', 'new_v_s5_glu_w': 'new_v', 'new_v_s5_glu_b': 'new_v', 'new_v_gla_gate_up': 'new_v', 'new_v_gla_gate_bias': 'new_v', 'new_v_gla_norm_w': 'new_v', 'new_v_w_out': 'new_v', 'new_v_post_norm_w': 'new_v'}


def _forward(args):
    return _fwd_reference(*[args[k] for k in FWD_PARAMS])


def _output_shape():
    def fwd():
        inp = _fwd_setup_inputs(0)
        return _fwd_reference(*[inp[k] for k in FWD_PARAMS])
    out = _jax.eval_shape(fwd)
    return out.shape, out.dtype

N_MICROBATCH = 1
ADAM_LR = 0.001
ADAM_B1 = 0.9
ADAM_B2 = 0.999
ADAM_EPS = 1e-08
ADAM_WD = 0.01
ADAM_STEP = 10
PER_EXAMPLE_BATCH_AXIS = {'x': 0, 'loss_target': 0}
SHARED_INPUTS = []
_WEIGHT_DTYPES = {'pre_norm_w': _jnp.float32, 'w_in': _jnp.float32, 's5_A_re': _jnp.float32, 's5_A_im': _jnp.float32, 's5_B_re': _jnp.float32, 's5_B_im': _jnp.float32, 's5_C_re': _jnp.float32, 's5_C_im': _jnp.float32, 's5_D': _jnp.float32, 's5_log_dt': _jnp.float32, 's5_glu_w': _jnp.float32, 's5_glu_b': _jnp.float32, 'gla_gate_up': _jnp.float32, 'gla_gate_bias': _jnp.float32, 'gla_norm_w': _jnp.float32, 'w_out': _jnp.float32, 'post_norm_w': _jnp.float32}
MOMENT_SCALE = {'pre_norm_w': 2.532785e-01, 'w_in': 1.593993e-01, 's5_A_re': 2.787580e-03, 's5_A_im': 2.950882e-03, 's5_B_re': 1.915282e-03, 's5_B_im': 1.952208e-03, 's5_C_re': 3.887801e-03, 's5_C_im': 3.979762e-03, 's5_D': 8.541276e-02, 's5_log_dt': 1.196729e+00, 's5_glu_w': 1.881723e-02, 's5_glu_b': 3.564852e-02, 'gla_gate_up': 2.717680e-02, 'gla_gate_bias': 1.072641e-01, 'gla_norm_w': 3.444627e-01, 'w_out': 1.310683e-01, 'post_norm_w': 1.600968e+01}


def _to_microbatches(a, axis):
    t = _jnp.moveaxis(a, axis, 0)
    t = t.reshape((N_MICROBATCH, t.shape[0] // N_MICROBATCH) + t.shape[1:])
    return _jnp.moveaxis(t, 1, axis + 1)


def setup_inputs(seed: int = 0) -> dict:
    inp = _fwd_setup_inputs(seed)
    key = _jax.random.fold_in(_jax.random.key(seed), 7919)
    shape, _ = _output_shape()
    out = dict(inp)
    out["loss_target"] = _jax.random.normal(_jax.random.fold_in(key, 0), shape, _jnp.float32)
    for i, name in enumerate(TWIN_WEIGHTS):
        w = inp[name].astype(_jnp.float32)
        if MOMENT_SCALE is None:
            s = _jnp.sqrt(_jnp.mean(_jnp.square(w)) + 1e-30)
        else:
            s = MOMENT_SCALE[name]
        km, kv = _jax.random.split(_jax.random.fold_in(key, i + 1))
        out[name] = w
        out["m_" + name] = s * _jax.random.normal(km, w.shape, _jnp.float32)
        out["v_" + name] = (s * s) * _jax.random.uniform(kv, w.shape, _jnp.float32, 0.5, 1.5)
    if N_MICROBATCH > 1:
        for name, axis in PER_EXAMPLE_BATCH_AXIS.items():
            out[name] = _to_microbatches(out[name], axis)
    return {'x': out['x'], 'pre_norm_w': out['pre_norm_w'], 'w_in': out['w_in'], 's5_A_re': out['s5_A_re'], 's5_A_im': out['s5_A_im'], 's5_B_re': out['s5_B_re'], 's5_B_im': out['s5_B_im'], 's5_C_re': out['s5_C_re'], 's5_C_im': out['s5_C_im'], 's5_D': out['s5_D'], 's5_log_dt': out['s5_log_dt'], 's5_glu_w': out['s5_glu_w'], 's5_glu_b': out['s5_glu_b'], 'gla_gate_up': out['gla_gate_up'], 'gla_gate_bias': out['gla_gate_bias'], 'gla_norm_w': out['gla_norm_w'], 'w_out': out['w_out'], 'post_norm_w': out['post_norm_w'], 'loss_target': out['loss_target'], 'm_pre_norm_w': out['m_pre_norm_w'], 'm_w_in': out['m_w_in'], 'm_s5_A_re': out['m_s5_A_re'], 'm_s5_A_im': out['m_s5_A_im'], 'm_s5_B_re': out['m_s5_B_re'], 'm_s5_B_im': out['m_s5_B_im'], 'm_s5_C_re': out['m_s5_C_re'], 'm_s5_C_im': out['m_s5_C_im'], 'm_s5_D': out['m_s5_D'], 'm_s5_log_dt': out['m_s5_log_dt'], 'm_s5_glu_w': out['m_s5_glu_w'], 'm_s5_glu_b': out['m_s5_glu_b'], 'm_gla_gate_up': out['m_gla_gate_up'], 'm_gla_gate_bias': out['m_gla_gate_bias'], 'm_gla_norm_w': out['m_gla_norm_w'], 'm_w_out': out['m_w_out'], 'm_post_norm_w': out['m_post_norm_w'], 'v_pre_norm_w': out['v_pre_norm_w'], 'v_w_in': out['v_w_in'], 'v_s5_A_re': out['v_s5_A_re'], 'v_s5_A_im': out['v_s5_A_im'], 'v_s5_B_re': out['v_s5_B_re'], 'v_s5_B_im': out['v_s5_B_im'], 'v_s5_C_re': out['v_s5_C_re'], 'v_s5_C_im': out['v_s5_C_im'], 'v_s5_D': out['v_s5_D'], 'v_s5_log_dt': out['v_s5_log_dt'], 'v_s5_glu_w': out['v_s5_glu_w'], 'v_s5_glu_b': out['v_s5_glu_b'], 'v_gla_gate_up': out['v_gla_gate_up'], 'v_gla_gate_bias': out['v_gla_gate_bias'], 'v_gla_norm_w': out['v_gla_norm_w'], 'v_w_out': out['v_w_out'], 'v_post_norm_w': out['v_post_norm_w']}


def _loss(weights, diff, rest, loss_target):
    with _jax.named_scope("forward"):
        args = {**rest, TWIN_DIFF_INPUT: diff, **{k: w.astype(_WEIGHT_DTYPES[k]) for k, w in weights.items()}}
        y = _forward(args)
    with _jax.named_scope("loss_head"):
        err = _jnp.square(y.astype(_jnp.float32) - loss_target)
        return 0.5 * _jnp.sum(_jnp.mean(err, axis=-1)) if err.ndim else 0.5 * err


def _adamw(w, g, m, v):
    m = ADAM_B1 * m + (1.0 - ADAM_B1) * g
    v = ADAM_B2 * v + (1.0 - ADAM_B2) * _jnp.square(g)
    m_hat = m / (1.0 - ADAM_B1 ** ADAM_STEP)
    v_hat = v / (1.0 - ADAM_B2 ** ADAM_STEP)
    delta = -ADAM_LR * (m_hat / (_jnp.sqrt(v_hat) + ADAM_EPS) + ADAM_WD * w)
    return delta, m, v


def reference(x, pre_norm_w, w_in, s5_A_re, s5_A_im, s5_B_re, s5_B_im, s5_C_re, s5_C_im, s5_D, s5_log_dt, s5_glu_w, s5_glu_b, gla_gate_up, gla_gate_bias, gla_norm_w, w_out, post_norm_w, loss_target, m_pre_norm_w, m_w_in, m_s5_A_re, m_s5_A_im, m_s5_B_re, m_s5_B_im, m_s5_C_re, m_s5_C_im, m_s5_D, m_s5_log_dt, m_s5_glu_w, m_s5_glu_b, m_gla_gate_up, m_gla_gate_bias, m_gla_norm_w, m_w_out, m_post_norm_w, v_pre_norm_w, v_w_in, v_s5_A_re, v_s5_A_im, v_s5_B_re, v_s5_B_im, v_s5_C_re, v_s5_C_im, v_s5_D, v_s5_log_dt, v_s5_glu_w, v_s5_glu_b, v_gla_gate_up, v_gla_gate_bias, v_gla_norm_w, v_w_out, v_post_norm_w):
    given = dict(x=x, pre_norm_w=pre_norm_w, w_in=w_in, s5_A_re=s5_A_re, s5_A_im=s5_A_im, s5_B_re=s5_B_re, s5_B_im=s5_B_im, s5_C_re=s5_C_re, s5_C_im=s5_C_im, s5_D=s5_D, s5_log_dt=s5_log_dt, s5_glu_w=s5_glu_w, s5_glu_b=s5_glu_b, gla_gate_up=gla_gate_up, gla_gate_bias=gla_gate_bias, gla_norm_w=gla_norm_w, w_out=w_out, post_norm_w=post_norm_w, loss_target=loss_target, m_pre_norm_w=m_pre_norm_w, m_w_in=m_w_in, m_s5_A_re=m_s5_A_re, m_s5_A_im=m_s5_A_im, m_s5_B_re=m_s5_B_re, m_s5_B_im=m_s5_B_im, m_s5_C_re=m_s5_C_re, m_s5_C_im=m_s5_C_im, m_s5_D=m_s5_D, m_s5_log_dt=m_s5_log_dt, m_s5_glu_w=m_s5_glu_w, m_s5_glu_b=m_s5_glu_b, m_gla_gate_up=m_gla_gate_up, m_gla_gate_bias=m_gla_gate_bias, m_gla_norm_w=m_gla_norm_w, m_w_out=m_w_out, m_post_norm_w=m_post_norm_w, v_pre_norm_w=v_pre_norm_w, v_w_in=v_w_in, v_s5_A_re=v_s5_A_re, v_s5_A_im=v_s5_A_im, v_s5_B_re=v_s5_B_re, v_s5_B_im=v_s5_B_im, v_s5_C_re=v_s5_C_re, v_s5_C_im=v_s5_C_im, v_s5_D=v_s5_D, v_s5_log_dt=v_s5_log_dt, v_s5_glu_w=v_s5_glu_w, v_s5_glu_b=v_s5_glu_b, v_gla_gate_up=v_gla_gate_up, v_gla_gate_bias=v_gla_gate_bias, v_gla_norm_w=v_gla_norm_w, v_w_out=v_w_out, v_post_norm_w=v_post_norm_w)
    weights = {n: given[n] for n in TWIN_WEIGHTS}
    shared = {n: given[n] for n in SHARED_INPUTS}
    per_example = {n: given[n] for n in ['x']}
    grad_fn = _jax.value_and_grad(_loss, argnums=(0, 1))

    def one_microbatch(ex, loss_target):
        ex = dict(ex)
        diff = ex.pop(TWIN_DIFF_INPUT)
        return grad_fn(weights, diff, {**shared, **ex}, loss_target)

    if N_MICROBATCH == 1:
        loss, (grad_w, grad_x) = one_microbatch(per_example, given["loss_target"])
    else:
        def body(carry, xs):
            loss_sum, grad_sum = carry
            l_k, (gw_k, gx_k) = one_microbatch(xs[0], xs[1])
            with _jax.named_scope("update"):
                return (loss_sum + l_k, _jax.tree.map(_jnp.add, grad_sum, gw_k)), gx_k

        init = (_jnp.zeros((), _jnp.float32), _jax.tree.map(_jnp.zeros_like, weights))
        (loss, grad_w), grad_x = _jax.lax.scan(body, init, (per_example, given["loss_target"]))
    with _jax.named_scope("update"):
        delta_w, new_m, new_v = {}, {}, {}
        for n in TWIN_WEIGHTS:
            delta_w[n], new_m[n], new_v[n] = _adamw(weights[n], grad_w[n], given["m_" + n], given["v_" + n])
    return (loss, grad_x, *[grad_w[n] for n in TWIN_WEIGHTS], *[delta_w[n] for n in TWIN_WEIGHTS],
            *[new_m[n] for n in TWIN_WEIGHTS], *[new_v[n] for n in TWIN_WEIGHTS])
```

```python
import functools
import math

import jax
import jax.numpy as jnp
from jax import lax
from jax.experimental import pallas as pl
from jax.experimental.pallas import tpu as pltpu

F32 = jnp.float32
BF16 = jnp.bfloat16
HI = lax.Precision.HIGHEST
MESH = pl.DeviceIdType.MESH

EPS = 1e-6
S5_GROUP = 16
S5_STATE = 64
GLA_HK = 128
GLA_HV = 256
GLA_RANK = 16
GLA_TAU = 16.0
GLA_CHUNK = 64
LANES = 128
SUBLANES = 8
S5_COLS = 128
S5_LANES = (S5_COLS // S5_GROUP) * S5_STATE

ADAM_LR = 0.001
ADAM_B1 = 0.9
ADAM_B2 = 0.999
ADAM_EPS = 1e-08
ADAM_WD = 0.01
ADAM_STEP = 10

GELU_K = math.sqrt(2.0 / math.pi)
GELU_C = 0.044715


def _blk(n, pref, unit=LANES):
    best = None
    b = unit
    while b <= min(n, pref):
        if n % b == 0:
            best = b
        b += unit
    return best if best is not None else n


def _dot(a, b, dn=(((1,), (0,)), ((), ()))):
    return lax.dot_general(a.astype(BF16), b.astype(BF16), dn, preferred_element_type=F32)


def _dot_hi(a, b, dn=(((1,), (0,)), ((), ()))):
    return lax.dot_general(a, b, dn, precision=HI, preferred_element_type=F32)


NN = (((1,), (0,)), ((), ()))
NT = (((1,), (1,)), ((), ()))
TN = (((0,), (0,)), ((), ()))


def _sigmoid(x):
    return 1.0 / (1.0 + jnp.exp(-x))


def _gelu(y):
    return 0.5 * y * (1.0 + jnp.tanh(GELU_K * (y + GELU_C * y * y * y)))


def _gelu_grad(y):
    th = jnp.tanh(GELU_K * (y + GELU_C * y * y * y))
    return 0.5 * (1.0 + th) + 0.5 * y * (1.0 - th * th) * GELU_K * (1.0 + 3.0 * GELU_C * y * y)


def _mm(a, b, *, name, ta=False, tb=False, out_dtype=F32, bm=1024, bn=1024, bk=512):
    if ta:
        K, M = a.shape
    else:
        M, K = a.shape
    if tb:
        N, K2 = b.shape
    else:
        K2, N = b.shape
    assert K == K2, (a.shape, b.shape, ta, tb)
    bm, bn, bk = _blk(M, bm), _blk(N, bn), _blk(K, bk)
    nk = K // bk
    dn = (((0 if ta else 1,), (1 if tb else 0,)), ((), ()))

    def body(a_ref, b_ref, o_ref, acc_ref):
        k = pl.program_id(2)

        @pl.when(k == 0)
        def _():
            acc_ref[...] = jnp.zeros_like(acc_ref)

        acc_ref[...] += _dot(a_ref[...], b_ref[...], dn)

        @pl.when(k == nk - 1)
        def _():
            o_ref[...] = acc_ref[...].astype(out_dtype)

    a_spec = pl.BlockSpec((bk, bm), lambda i, j, k: (k, i)) if ta else pl.BlockSpec((bm, bk), lambda i, j, k: (i, k))
    b_spec = pl.BlockSpec((bn, bk), lambda i, j, k: (j, k)) if tb else pl.BlockSpec((bk, bn), lambda i, j, k: (k, j))
    return pl.pallas_call(
        body,
        name=name,
        grid=(M // bm, N // bn, nk),
        in_specs=[a_spec, b_spec],
        out_specs=pl.BlockSpec((bm, bn), lambda i, j, k: (i, j)),
        out_shape=jax.ShapeDtypeStruct((M, N), out_dtype),
        scratch_shapes=[pltpu.VMEM((bm, bn), F32)],
        compiler_params=pltpu.CompilerParams(dimension_semantics=("parallel", "parallel", "arbitrary")),
    )(a, b)


def _prenorm_fwd(x, w):
    L, D = x.shape
    tr = _blk(L, 256, SUBLANES)

    def body(x_ref, w_ref, h_ref):
        xv = x_ref[...]
        r = lax.rsqrt(jnp.mean(xv * xv, axis=-1, keepdims=True) + EPS)
        h_ref[...] = (xv * r * w_ref[...]).astype(BF16)

    return pl.pallas_call(
        body, name="prenorm_fwd", grid=(L // tr,),
        in_specs=[pl.BlockSpec((tr, D), lambda i: (i, 0)), pl.BlockSpec((1, D), lambda i: (0, 0))],
        out_specs=pl.BlockSpec((tr, D), lambda i: (i, 0)),
        out_shape=jax.ShapeDtypeStruct((L, D), BF16),
        compiler_params=pltpu.CompilerParams(dimension_semantics=("parallel",)),
    )(x, w)


def _post_fwd_bwd(mixed, x, target, w):
    L, D = x.shape
    tr = _blk(L, 256, SUBLANES)
    nsteps = L // tr

    def body(mx_ref, x_ref, t_ref, w_ref, loss_ref, dm_ref, dout_ref, gw_ref, acc_ref):
        i = pl.program_id(0)

        @pl.when(i == 0)
        def _():
            acc_ref[...] = jnp.zeros_like(acc_ref)
            gw_ref[...] = jnp.zeros_like(gw_ref)

        mx = mx_ref[...]
        wv = w_ref[...]
        r = lax.rsqrt(jnp.mean(mx * mx, axis=-1, keepdims=True) + EPS)
        n = mx * r
        err = x_ref[...] + n * wv - t_ref[...]
        acc_ref[...] += jnp.sum(err * err, axis=0, keepdims=True)
        dout = err * (1.0 / D)
        dout_ref[...] = dout
        gw_ref[...] += jnp.sum(dout * n, axis=0, keepdims=True)
        dn = dout * wv
        dm_ref[...] = (r * (dn - n * jnp.mean(dn * n, axis=-1, keepdims=True))).astype(BF16)

        @pl.when(i == nsteps - 1)
        def _():
            loss_ref[...] = jnp.sum(acc_ref[...], axis=-1, keepdims=True) * (0.5 / D)

    row = pl.BlockSpec((tr, D), lambda i: (i, 0))
    vec = pl.BlockSpec((1, D), lambda i: (0, 0))
    return pl.pallas_call(
        body, name="post_fwd_bwd", grid=(nsteps,),
        in_specs=[row, row, row, vec],
        out_specs=[pl.BlockSpec((1, 1), lambda i: (0, 0)), row, row, vec],
        out_shape=[jax.ShapeDtypeStruct((1, 1), F32), jax.ShapeDtypeStruct((L, D), BF16),
                   jax.ShapeDtypeStruct((L, D), F32), jax.ShapeDtypeStruct((1, D), F32)],
        scratch_shapes=[pltpu.VMEM((1, D), F32)],
        compiler_params=pltpu.CompilerParams(dimension_semantics=("arbitrary",)),
    )(mixed, x, target, w)


def _prenorm_bwd(x, dh_main, dh_low, dout, w):
    L, D = x.shape
    tr = _blk(L, 256, SUBLANES)

    def body(x_ref, a_ref, b_ref, dout_ref, w_ref, gx_ref, gw_ref):
        i = pl.program_id(0)

        @pl.when(i == 0)
        def _():
            gw_ref[...] = jnp.zeros_like(gw_ref)

        xv = x_ref[...]
        r = lax.rsqrt(jnp.mean(xv * xv, axis=-1, keepdims=True) + EPS)
        n = xv * r
        dh = a_ref[...] + b_ref[...]
        gw_ref[...] += jnp.sum(dh * n, axis=0, keepdims=True)
        dn = dh * w_ref[...]
        gx_ref[...] = dout_ref[...] + r * (dn - n * jnp.mean(dn * n, axis=-1, keepdims=True))

    row = pl.BlockSpec((tr, D), lambda i: (i, 0))
    vec = pl.BlockSpec((1, D), lambda i: (0, 0))
    return pl.pallas_call(
        body, name="prenorm_bwd", grid=(L // tr,),
        in_specs=[row, row, row, row, vec],
        out_specs=[row, vec],
        out_shape=[jax.ShapeDtypeStruct((L, D), F32), jax.ShapeDtypeStruct((1, D), F32)],
        compiler_params=pltpu.CompilerParams(dimension_semantics=("arbitrary",)),
    )(x, dh_main, dh_low, dout, w)


def _s5_disc(a_re_raw, a_im, dt):
    a_re = jnp.minimum(a_re_raw, -1e-4)
    mag = jnp.exp(a_re * dt)
    ph = a_im * dt
    ab_re = mag * jnp.cos(ph)
    ab_im = mag * jnp.sin(ph)
    inv_n = 1.0 / (a_re * a_re + a_im * a_im)
    ia_re = a_re * inv_n
    ia_im = -a_im * inv_n
    n_re = ab_re - 1.0
    f_re = n_re * ia_re - ab_im * ia_im
    f_im = n_re * ia_im + ab_im * ia_re
    return a_re, ab_re, ab_im, f_re, f_im, ia_re, ia_im


def _s5_prep_fwd(a_re, a_im, log_dt, b_re, b_im, e16):
    G, P = a_re.shape
    PH = b_re.shape[1]

    def body(are_ref, aim_ref, ldt_ref, bre_ref, bim_ref, e_ref, bbre_ref, bbim_ref, pwre_ref, pwim_ref):
        dt = jnp.exp(ldt_ref[...])
        _, ab_re, ab_im, f_re, f_im, _, _ = _s5_disc(are_ref[...], aim_ref[...], dt)
        fx_re = _dot_hi(f_re, e_ref[...])
        fx_im = _dot_hi(f_im, e_ref[...])
        br, bi = bre_ref[...], bim_ref[...]
        bbre_ref[...] = fx_re * br - fx_im * bi
        bbim_ref[...] = fx_re * bi + fx_im * br
        pr, pi = ab_re, ab_im
        pwre_ref[0] = pr
        pwim_ref[0] = pi
        for k in range(1, SUBLANES):
            pr, pi = pr * ab_re - pi * ab_im, pr * ab_im + pi * ab_re
            pwre_ref[k] = pr
            pwim_ref[k] = pi

    vm = pl.BlockSpec(memory_space=pltpu.VMEM)
    return pl.pallas_call(
        body, name="s5_prep_fwd",
        in_specs=[vm] * 6, out_specs=[vm] * 4,
        out_shape=[jax.ShapeDtypeStruct((G, PH), F32), jax.ShapeDtypeStruct((G, PH), F32),
                   jax.ShapeDtypeStruct((SUBLANES, G, P), F32), jax.ShapeDtypeStruct((SUBLANES, G, P), F32)],
    )(a_re, a_im, log_dt, b_re, b_im, e16)


def _s5_prep_bwd(a_re, a_im, log_dt, b_re, b_im, e16, gbb_re, gbb_im, gab_re, gab_im):
    G, P = a_re.shape
    PH = b_re.shape[1]

    def body(are_ref, aim_ref, ldt_ref, bre_ref, bim_ref, e_ref, gbr_ref, gbi_ref, gar_ref, gai_ref,
             o_are, o_aim, o_bre, o_bim, o_ldt):
        dt = jnp.exp(ldt_ref[...])
        a_raw = are_ref[...]
        a_imv = aim_ref[...]
        a_re_c, ab_re, ab_im, f_re, f_im, ia_re, ia_im = _s5_disc(a_raw, a_imv, dt)
        ev = e_ref[...]
        fx_re = _dot_hi(f_re, ev)
        fx_im = _dot_hi(f_im, ev)
        gbr, gbi = gbr_ref[...], gbi_ref[...]
        br, bi = bre_ref[...], bim_ref[...]
        o_bre[...] = fx_re * gbr + fx_im * gbi
        o_bim[...] = fx_re * gbi - fx_im * gbr
        gf_re = _dot_hi(br * gbr + bi * gbi, ev, NT)
        gf_im = _dot_hi(br * gbi - bi * gbr, ev, NT)
        gab_r = gar_ref[...] + ia_re * gf_re + ia_im * gf_im
        gab_i = gai_ref[...] + ia_re * gf_im - ia_im * gf_re
        q_re = f_re * ia_re - f_im * ia_im
        q_im = f_re * ia_im + f_im * ia_re
        ga_re = -(q_re * gf_re + q_im * gf_im)
        ga_im = -(q_re * gf_im - q_im * gf_re)
        gth_re = ab_re * gab_r + ab_im * gab_i
        gth_im = ab_re * gab_i - ab_im * gab_r
        ga_re = ga_re + dt * gth_re
        ga_im = ga_im + dt * gth_im
        gdt = jnp.sum(a_re_c * gth_re + a_imv * gth_im, axis=-1, keepdims=True)
        o_ldt[...] = gdt * dt
        slope = jnp.where(a_raw < -1e-4, 1.0, jnp.where(a_raw == -1e-4, 0.5, 0.0))
        o_are[...] = ga_re * slope
        o_aim[...] = ga_im

    vm = pl.BlockSpec(memory_space=pltpu.VMEM)
    return pl.pallas_call(
        body, name="s5_prep_bwd",
        in_specs=[vm] * 10, out_specs=[vm] * 5,
        out_shape=[jax.ShapeDtypeStruct((G, P), F32), jax.ShapeDtypeStruct((G, P), F32),
                   jax.ShapeDtypeStruct((G, PH), F32), jax.ShapeDtypeStruct((G, PH), F32),
                   jax.ShapeDtypeStruct((G, 1), F32)],
    )(a_re, a_im, log_dt, b_re, b_im, e16, gbb_re, gbb_im, gab_re, gab_im)


def _block_diag(m, nb, rows, cols):
    g8 = S5_COLS // S5_GROUP
    m = m.reshape(nb, g8, rows, 1, cols) * jnp.eye(g8, dtype=m.dtype)[None, :, None, :, None]
    return m.reshape(nb, g8 * rows, g8 * cols)


def _block_diag_extract(m, nb, rows, cols):
    g8 = S5_COLS // S5_GROUP
    m = m.reshape(nb, g8, rows, g8, cols)
    idx = jnp.arange(g8)
    return m[:, idx, :, idx, :].transpose(1, 0, 2, 3).reshape(nb * g8, rows, cols)


def _scan_tables(pw_re, pw_im, nb):
    pw_re = pw_re.reshape(SUBLANES, nb, 1, S5_LANES)
    pw_im = pw_im.reshape(SUBLANES, nb, 1, S5_LANES)
    row = jnp.arange(SUBLANES, dtype=jnp.int32).reshape(1, SUBLANES, 1)
    tabs = []
    for k in (1, 2, 4):
        keep = (row >= k).astype(F32)
        tabs += [pw_re[k - 1] * keep, pw_im[k - 1] * keep]
    tabs += [jnp.moveaxis(pw_re[:, :, 0, :], 0, 1), jnp.moveaxis(pw_im[:, :, 0, :], 0, 1)]
    for k in (1, 2, 4):
        keep = (row < SUBLANES - k).astype(F32)
        tabs += [pw_re[k - 1] * keep, -pw_im[k - 1] * keep]
    tabs += [jnp.moveaxis(pw_re[::-1, :, 0, :], 0, 1), -jnp.moveaxis(pw_im[::-1, :, 0, :], 0, 1)]
    tabs = [jnp.broadcast_to(t, (nb, SUBLANES, S5_LANES)) for t in tabs]
    return jnp.stack(tabs, axis=1)


def _scan8(xr, xi, tab_ref, base, shifts):
    for lvl, sh in enumerate(shifts):
        mr = tab_ref[0, base + 2 * lvl]
        mi = tab_ref[0, base + 2 * lvl + 1]
        ar = pltpu.roll(xr, sh, 0)
        ai = pltpu.roll(xi, sh, 0)
        xr, xi = xr + mr * ar - mi * ai, xi + mr * ai + mi * ar
    return xr, xi


def _s5_scan_fwd(proj_main, bbd_re, bbd_im, cbd_re, cbd_im, dvec, tab, DS):
    L = proj_main.shape[0]
    nb = DS // S5_COLS
    tb = _blk(L, 512, SUBLANES)
    nt = L // tb
    ng = tb // SUBLANES

    def body(u_ref, bre_ref, bim_ref, cre_ref, cim_ref, d_ref, tab_ref, y_ref, sre_ref, sim_ref, car_ref):
        t = pl.program_id(1)

        @pl.when(t == 0)
        def _():
            car_ref[...] = jnp.zeros_like(car_ref)

        u = u_ref[...]
        sre_ref[...] = _dot_hi(u, bre_ref[0])
        sim_ref[...] = _dot_hi(u, bim_ref[0])

        def grp(r, carry):
            cr, ci = carry
            off = pl.multiple_of(r * SUBLANES, SUBLANES)
            xr, xi = _scan8(sre_ref[pl.ds(off, SUBLANES), :], sim_ref[pl.ds(off, SUBLANES), :], tab_ref, 0, (1, 2, 4))
            pr, pi = tab_ref[0, 6], tab_ref[0, 7]
            xr, xi = xr + pr * cr - pi * ci, xi + pr * ci + pi * cr
            sre_ref[pl.ds(off, SUBLANES), :] = xr
            sim_ref[pl.ds(off, SUBLANES), :] = xi
            return (jnp.broadcast_to(xr[SUBLANES - 1:SUBLANES, :], xr.shape),
                    jnp.broadcast_to(xi[SUBLANES - 1:SUBLANES, :], xi.shape))

        cr, ci = lax.fori_loop(0, ng, grp, (car_ref[0], car_ref[1]))
        car_ref[0] = cr
        car_ref[1] = ci
        y_ref[...] = _dot_hi(sre_ref[...], cre_ref[0]) - _dot_hi(sim_ref[...], cim_ref[0]) + d_ref[...] * u

    return pl.pallas_call(
        body, name="s5_scan_fwd", grid=(nb, nt),
        in_specs=[
            pl.BlockSpec((tb, S5_COLS), lambda j, t: (t, j)),
            pl.BlockSpec((1, S5_COLS, S5_LANES), lambda j, t: (j, 0, 0)),
            pl.BlockSpec((1, S5_COLS, S5_LANES), lambda j, t: (j, 0, 0)),
            pl.BlockSpec((1, S5_LANES, S5_COLS), lambda j, t: (j, 0, 0)),
            pl.BlockSpec((1, S5_LANES, S5_COLS), lambda j, t: (j, 0, 0)),
            pl.BlockSpec((1, S5_COLS), lambda j, t: (0, j)),
            pl.BlockSpec((1, 16, SUBLANES, S5_LANES), lambda j, t: (j, 0, 0, 0)),
        ],
        out_specs=[
            pl.BlockSpec((tb, S5_COLS), lambda j, t: (t, j)),
            pl.BlockSpec((tb, S5_LANES), lambda j, t: (t, j)),
            pl.BlockSpec((tb, S5_LANES), lambda j, t: (t, j)),
        ],
        out_shape=[jax.ShapeDtypeStruct((L, DS), F32),
                   jax.ShapeDtypeStruct((L, nb * S5_LANES), F32),
                   jax.ShapeDtypeStruct((L, nb * S5_LANES), F32)],
        scratch_shapes=[pltpu.VMEM((2, SUBLANES, S5_LANES), F32)],
        compiler_params=pltpu.CompilerParams(dimension_semantics=("parallel", "arbitrary")),
    )(proj_main, bbd_re, bbd_im, cbd_re, cbd_im, dvec, tab)


def _s5_scan_bwd(dy, proj_main, s_re, s_im, bbd_re, bbd_im, cbd_re, cbd_im, dvec, tab, DS):
    L = proj_main.shape[0]
    nb = DS // S5_COLS
    tb = _blk(L, 512, SUBLANES)
    nt = L // tb
    ng = tb // SUBLANES
    tb8 = tb // SUBLANES

    def body(dy_ref, u_ref, sre_ref, sim_ref, pre_ref, pim_ref, bre_ref, bim_ref, cre_ref, cim_ref, d_ref, tab_ref,
             du_ref, gd_ref, gcre_ref, gcim_ref, gbre_ref, gbim_ref, gare_ref, gaim_ref,
             lre_ref, lim_ref, car_ref):
        t = pl.program_id(1)

        @pl.when(t == 0)
        def _():
            car_ref[...] = jnp.zeros_like(car_ref)
            gd_ref[...] = jnp.zeros_like(gd_ref)
            gcre_ref[...] = jnp.zeros_like(gcre_ref)
            gcim_ref[...] = jnp.zeros_like(gcim_ref)
            gbre_ref[...] = jnp.zeros_like(gbre_ref)
            gbim_ref[...] = jnp.zeros_like(gbim_ref)
            gare_ref[...] = jnp.zeros_like(gare_ref)
            gaim_ref[...] = jnp.zeros_like(gaim_ref)

        dyv = dy_ref[...]
        u = u_ref[...]
        gd_ref[...] += jnp.sum(dyv * u, axis=0, keepdims=True)
        lre_ref[...] = _dot_hi(dyv, cre_ref[0], NT)
        lim_ref[...] = -_dot_hi(dyv, cim_ref[0], NT)
        gcre_ref[0] += _dot_hi(sre_ref[...], dyv, TN)
        gcim_ref[0] -= _dot_hi(sim_ref[...], dyv, TN)

        first = (t == nt - 1).astype(F32)
        head_re = pre_ref[...] * (1.0 - first)
        head_im = pim_ref[...] * (1.0 - first)
        row0 = lax.broadcasted_iota(jnp.int32, (SUBLANES, S5_LANES), 0) == 0

        def grp(i, carry):
            cr, ci, acc_re, acc_im = carry
            r = ng - 1 - i
            off = pl.multiple_of(r * SUBLANES, SUBLANES)
            xr, xi = _scan8(lre_ref[pl.ds(off, SUBLANES), :], lim_ref[pl.ds(off, SUBLANES), :], tab_ref, 8, (7, 6, 4))
            pr, pi = tab_ref[0, 14], tab_ref[0, 15]
            xr, xi = xr + pr * cr - pi * ci, xi + pr * ci + pi * cr
            lre_ref[pl.ds(off, SUBLANES), :] = xr
            lim_ref[pl.ds(off, SUBLANES), :] = xi
            poff = pl.multiple_of(jnp.maximum(r - 1, 0) * SUBLANES, SUBLANES)
            prev_re = jnp.where(r == 0, head_re, sre_ref[pl.ds(poff, SUBLANES), :])
            prev_im = jnp.where(r == 0, head_im, sim_ref[pl.ds(poff, SUBLANES), :])
            prev_re = jnp.broadcast_to(prev_re[SUBLANES - 1:SUBLANES, :], xr.shape)
            prev_im = jnp.broadcast_to(prev_im[SUBLANES - 1:SUBLANES, :], xi.shape)
            sp_re = jnp.where(row0, prev_re, pltpu.roll(sre_ref[pl.ds(off, SUBLANES), :], 1, 0))
            sp_im = jnp.where(row0, prev_im, pltpu.roll(sim_ref[pl.ds(off, SUBLANES), :], 1, 0))
            acc_re = acc_re + sp_re * xr + sp_im * xi
            acc_im = acc_im + sp_re * xi - sp_im * xr
            return (jnp.broadcast_to(xr[0:1, :], xr.shape), jnp.broadcast_to(xi[0:1, :], xi.shape), acc_re, acc_im)

        zero = jnp.zeros((SUBLANES, S5_LANES), F32)
        cr, ci, acc_re, acc_im = lax.fori_loop(0, ng, grp, (car_ref[0], car_ref[1], zero, zero))
        car_ref[0] = cr
        car_ref[1] = ci
        gare_ref[...] += jnp.sum(acc_re, axis=0, keepdims=True)
        gaim_ref[...] += jnp.sum(acc_im, axis=0, keepdims=True)
        lre = lre_ref[...]
        lim = lim_ref[...]
        du = dyv * d_ref[...] + _dot_hi(lre, bre_ref[0], NT) + _dot_hi(lim, bim_ref[0], NT)
        du_ref[...] = du.astype(BF16)
        gbre_ref[0] += _dot_hi(u, lre, TN)
        gbim_ref[0] += _dot_hi(u, lim, TN)

    rt = lambda t: nt - 1 - t
    col = pl.BlockSpec((tb, S5_COLS), lambda j, t: (rt(t), j))
    st = pl.BlockSpec((tb, S5_LANES), lambda j, t: (rt(t), j))
    prev = pl.BlockSpec((SUBLANES, S5_LANES), lambda j, t: (jnp.maximum(rt(t) * tb8 - 1, 0), j))
    bmat = pl.BlockSpec((1, S5_COLS, S5_LANES), lambda j, t: (j, 0, 0))
    cmat = pl.BlockSpec((1, S5_LANES, S5_COLS), lambda j, t: (j, 0, 0))
    return pl.pallas_call(
        body, name="s5_scan_bwd", grid=(nb, nt),
        in_specs=[col, col, st, st, prev, prev, bmat, bmat, cmat, cmat,
                  pl.BlockSpec((1, S5_COLS), lambda j, t: (0, j)),
                  pl.BlockSpec((1, 16, SUBLANES, S5_LANES), lambda j, t: (j, 0, 0, 0))],
        out_specs=[col, pl.BlockSpec((1, S5_COLS), lambda j, t: (0, j)), cmat, cmat, bmat, bmat,
                   pl.BlockSpec((1, S5_LANES), lambda j, t: (0, j)), pl.BlockSpec((1, S5_LANES), lambda j, t: (0, j))],
        out_shape=[jax.ShapeDtypeStruct((L, DS), BF16), jax.ShapeDtypeStruct((1, DS), F32),
                   jax.ShapeDtypeStruct((nb, S5_LANES, S5_COLS), F32), jax.ShapeDtypeStruct((nb, S5_LANES, S5_COLS), F32),
                   jax.ShapeDtypeStruct((nb, S5_COLS, S5_LANES), F32), jax.ShapeDtypeStruct((nb, S5_COLS, S5_LANES), F32),
                   jax.ShapeDtypeStruct((1, nb * S5_LANES), F32), jax.ShapeDtypeStruct((1, nb * S5_LANES), F32)],
        scratch_shapes=[pltpu.VMEM((tb, S5_LANES), F32), pltpu.VMEM((tb, S5_LANES), F32),
                        pltpu.VMEM((2, SUBLANES, S5_LANES), F32)],
        compiler_params=pltpu.CompilerParams(dimension_semantics=("parallel", "arbitrary")),
    )(dy, proj_main, s_re, s_im, s_re, s_im, bbd_re, bbd_im, cbd_re, cbd_im, dvec, tab)


def _s5_post_fwd(y_pre, proj_main, glu_w, glu_b, DS):
    L = y_pre.shape[0]
    tr = _blk(L, 256, SUBLANES)

    def body(y_ref, z_ref, w_ref, b_ref, o_ref, t_ref):
        y1 = _gelu(y_ref[...])
        t = _dot(y1, w_ref[...]) + b_ref[...]
        t_ref[...] = t
        z = z_ref[...]
        o_ref[...] = (y1 * _sigmoid(t) * (z * _sigmoid(z))).astype(BF16)

    row = pl.BlockSpec((tr, DS), lambda i: (i, 0))
    return pl.pallas_call(
        body, name="s5_post_fwd", grid=(L // tr,),
        in_specs=[row, pl.BlockSpec((tr, DS), lambda i: (i, 1)), pl.BlockSpec((DS, DS), lambda i: (0, 0)),
                  pl.BlockSpec((1, DS), lambda i: (0, 0))],
        out_specs=[row, row],
        out_shape=[jax.ShapeDtypeStruct((L, DS), BF16), jax.ShapeDtypeStruct((L, DS), F32)],
        compiler_params=pltpu.CompilerParams(dimension_semantics=("parallel",)),
    )(y_pre, proj_main, glu_w, glu_b)


def _s5_post_bwd(d_ycat, y_pre, proj_main, t_pre, glu_w, DS):
    L = y_pre.shape[0]
    tr = _blk(L, 256, SUBLANES)

    def body(dy_ref, y_ref, z_ref, t_ref, w_ref, dyp_ref, dz_ref, dt_ref, y1_ref, gb_ref):
        i = pl.program_id(0)

        @pl.when(i == 0)
        def _():
            gb_ref[...] = jnp.zeros_like(gb_ref)

        dy = dy_ref[...]
        yp = y_ref[...]
        z = z_ref[...]
        y1 = _gelu(yp)
        sg = _sigmoid(t_ref[...])
        sz = _sigmoid(z)
        c = y1 * sg
        d_c = dy * (z * sz)
        dz_ref[...] = (dy * c * (sz * (1.0 + z * (1.0 - sz)))).astype(BF16)
        d_t = d_c * y1 * sg * (1.0 - sg)
        gb_ref[...] += jnp.sum(d_t, axis=0, keepdims=True)
        dt_ref[...] = d_t.astype(BF16)
        y1_ref[...] = y1.astype(BF16)
        d_y1 = d_c * sg + _dot(d_t, w_ref[...], NT)
        dyp_ref[...] = d_y1 * _gelu_grad(yp)

    row = pl.BlockSpec((tr, DS), lambda i: (i, 0))
    return pl.pallas_call(
        body, name="s5_post_bwd", grid=(L // tr,),
        in_specs=[row, row, pl.BlockSpec((tr, DS), lambda i: (i, 1)), row, pl.BlockSpec((DS, DS), lambda i: (0, 0))],
        out_specs=[row, row, row, row, pl.BlockSpec((1, DS), lambda i: (0, 0))],
        out_shape=[jax.ShapeDtypeStruct((L, DS), F32), jax.ShapeDtypeStruct((L, DS), BF16),
                   jax.ShapeDtypeStruct((L, DS), BF16), jax.ShapeDtypeStruct((L, DS), BF16),
                   jax.ShapeDtypeStruct((1, DS), F32)],
        compiler_params=pltpu.CompilerParams(dimension_semantics=("arbitrary",)),
    )(d_ycat, y_pre, proj_main, t_pre, glu_w)


def _gla_gates(glow, gu_ref, gb_ref):
    a = _dot(glow, gu_ref[...]) + gb_ref[...]
    lg = (jnp.minimum(a, 0.0) - jnp.log(1.0 + jnp.exp(-jnp.abs(a)))) * (1.0 / GLA_TAU)
    ri = lax.broadcasted_iota(jnp.int32, (GLA_CHUNK, GLA_CHUNK), 0)
    ci = lax.broadcasted_iota(jnp.int32, (GLA_CHUNK, GLA_CHUNK), 1)
    b = _dot_hi((ri >= ci).astype(F32), lg)
    b_last = jnp.sum(lg, axis=0, keepdims=True)
    return a, b, b_last, ri >= ci


def _gla_specs(DS, DK, DV, cmap):
    c = GLA_CHUNK
    return [
        pl.BlockSpec((c, DK), lambda n: (cmap(n), 2 * DS // DK)),
        pl.BlockSpec((c, DK), lambda n: (cmap(n), 2 * DS // DK + 1)),
        pl.BlockSpec((c, DV), lambda n: (cmap(n), (2 * DS + 2 * DK) // DV)),
        pl.BlockSpec((c, DV), lambda n: (cmap(n), (2 * DS + 2 * DK) // DV + 1)),
    ]


def _gla_fwd(proj_main, proj_low, gate_up_pad, gate_bias, norm_w, DS, DK, DV):
    L = proj_main.shape[0]
    nc = L // GLA_CHUNK
    nh = DK // GLA_HK
    scale = GLA_HK ** -0.5

    def body(q_ref, k_ref, v_ref, z_ref, gl_ref, gu_ref, gb_ref, nw_ref, y_ref, sp_ref, st_ref):
        n = pl.program_id(0)

        @pl.when(n == 0)
        def _():
            st_ref[...] = jnp.zeros_like(st_ref)

        _, b, b_last, mask = _gla_gates(gl_ref[...], gu_ref, gb_ref)
        for h in range(nh):
            ks = slice(h * GLA_HK, (h + 1) * GLA_HK)
            vs = slice(h * GLA_HV, (h + 1) * GLA_HV)
            bh, bl = b[:, ks], b_last[:, ks]
            qe = (q_ref[:, ks] * scale) * jnp.exp(bh)
            kh = k_ref[:, ks]
            ke = kh * jnp.exp(-bh)
            ktail = kh * jnp.exp(bl - bh)
            vh = v_ref[:, vs]
            st = st_ref[h]
            sp_ref[0, h] = st
            attn = jnp.where(mask, _dot(qe, ke, NT), 0.0)
            o = _dot(attn, vh) + _dot(qe, st, NT)
            st_ref[h] = jnp.exp(bl) * st + _dot(vh, ktail, TN)
            r = lax.rsqrt(jnp.mean(o * o, axis=-1, keepdims=True) + EPS)
            z = z_ref[:, vs]
            y_ref[:, vs] = (o * r * nw_ref[...] * (z * _sigmoid(z))).astype(BF16)

    c = GLA_CHUNK
    return pl.pallas_call(
        body, name="gla_fwd", grid=(nc,),
        in_specs=_gla_specs(DS, DK, DV, lambda n: n) + [
            pl.BlockSpec((c, LANES), lambda n: (n, 0)),
            pl.BlockSpec((LANES, DK), lambda n: (0, 0)),
            pl.BlockSpec((1, DK), lambda n: (0, 0)),
            pl.BlockSpec((1, GLA_HV), lambda n: (0, 0)),
        ],
        out_specs=[pl.BlockSpec((c, DV), lambda n: (n, 0)),
                   pl.BlockSpec((1, nh, GLA_HV, GLA_HK), lambda n: (n, 0, 0, 0))],
        out_shape=[jax.ShapeDtypeStruct((L, DV), BF16), jax.ShapeDtypeStruct((nc, nh, GLA_HV, GLA_HK), F32)],
        scratch_shapes=[pltpu.VMEM((nh, GLA_HV, GLA_HK), F32)],
        compiler_params=pltpu.CompilerParams(dimension_semantics=("arbitrary",)),
    )(proj_main, proj_main, proj_main, proj_main, proj_low, gate_up_pad, gate_bias, norm_w)


def _gla_bwd(d_ycat, proj_main, proj_low, s_prev, gate_up_pad, gate_bias, norm_w, DS, DK, DV):
    L = proj_main.shape[0]
    nc = L // GLA_CHUNK
    nh = DK // GLA_HK
    scale = GLA_HK ** -0.5

    def body(dy_ref, q_ref, k_ref, v_ref, z_ref, gl_ref, sp_ref, gu_ref, gb_ref, nw_ref,
             dq_ref, dk_ref, dv_ref, dz_ref, da_ref, gnw_ref, ggb_ref, dst_ref):
        n = pl.program_id(0)

        @pl.when(n == 0)
        def _():
            dst_ref[...] = jnp.zeros_like(dst_ref)
            gnw_ref[...] = jnp.zeros_like(gnw_ref)
            ggb_ref[...] = jnp.zeros_like(ggb_ref)

        a, b, b_last, mask = _gla_gates(gl_ref[...], gu_ref, gb_ref)
        last_row = lax.broadcasted_iota(jnp.int32, (GLA_CHUNK, GLA_HK), 0) == GLA_CHUNK - 1
        ri = lax.broadcasted_iota(jnp.int32, (GLA_CHUNK, GLA_CHUNK), 0)
        ci = lax.broadcasted_iota(jnp.int32, (GLA_CHUNK, GLA_CHUNK), 1)
        upper = (ci >= ri).astype(F32)
        nw = nw_ref[...]
        for h in range(nh):
            ks = slice(h * GLA_HK, (h + 1) * GLA_HK)
            vs = slice(h * GLA_HV, (h + 1) * GLA_HV)
            bh, bl = b[:, ks], b_last[:, ks]
            e = jnp.exp(bh)
            einv = jnp.exp(-bh)
            etail = jnp.exp(bl - bh)
            dec = jnp.exp(bl)
            qe = (q_ref[:, ks] * scale) * e
            kh = k_ref[:, ks]
            ke = kh * einv
            ktail = kh * etail
            vh = v_ref[:, vs]
            st = sp_ref[0, h]
            dst = dst_ref[h]
            attn = jnp.where(mask, _dot(qe, ke, NT), 0.0)
            o = _dot(attn, vh) + _dot(qe, st, NT)
            r = lax.rsqrt(jnp.mean(o * o, axis=-1, keepdims=True) + EPS)
            nrm = o * r
            z = z_ref[:, vs]
            sz = _sigmoid(z)
            dy = dy_ref[:, vs]
            dz_ref[:, vs] = (dy * nrm * nw * (sz * (1.0 + z * (1.0 - sz)))).astype(BF16)
            d_on = dy * (z * sz)
            gnw_ref[...] += jnp.sum(d_on * nrm, axis=0, keepdims=True)
            d_n = d_on * nw
            d_o = r * (d_n - nrm * jnp.mean(d_n * nrm, axis=-1, keepdims=True))
            d_attn = jnp.where(mask, _dot(d_o, vh, NT), 0.0)
            dv_ref[:, vs] = (_dot(attn, d_o, TN) + _dot(ktail, dst, NT)).astype(BF16)
            d_qe = _dot(d_attn, ke) + _dot(d_o, st)
            d_ke = _dot(d_attn, qe, TN)
            d_kt = _dot(vh, dst)
            d_dec = jnp.sum(dst * st, axis=0, keepdims=True)
            dst_ref[h] = dec * dst + _dot(d_o, qe, TN)
            dq_ref[:, ks] = (d_qe * scale * e).astype(BF16)
            dk_ref[:, ks] = (d_ke * einv + d_kt * etail).astype(BF16)
            d_bl = jnp.sum(d_kt * ktail, axis=0, keepdims=True) + d_dec * dec
            d_b = d_qe * qe - d_ke * ke - d_kt * ktail + jnp.where(last_row, d_bl, 0.0)
            d_lg = _dot_hi(upper, d_b)
            d_a = d_lg * (1.0 / GLA_TAU) * _sigmoid(-a[:, ks])
            ggb_ref[:, ks] += jnp.sum(d_a, axis=0, keepdims=True)
            da_ref[:, ks] = d_a.astype(BF16)

    c = GLA_CHUNK
    rn = lambda n: nc - 1 - n
    return pl.pallas_call(
        body, name="gla_bwd", grid=(nc,),
        in_specs=[pl.BlockSpec((c, DV), lambda n: (rn(n), DS // DV))] + _gla_specs(DS, DK, DV, rn) + [
            pl.BlockSpec((c, LANES), lambda n: (rn(n), 0)),
            pl.BlockSpec((1, nh, GLA_HV, GLA_HK), lambda n: (rn(n), 0, 0, 0)),
            pl.BlockSpec((LANES, DK), lambda n: (0, 0)),
            pl.BlockSpec((1, DK), lambda n: (0, 0)),
            pl.BlockSpec((1, GLA_HV), lambda n: (0, 0)),
        ],
        out_specs=[pl.BlockSpec((c, DK), lambda n: (rn(n), 0)), pl.BlockSpec((c, DK), lambda n: (rn(n), 0)),
                   pl.BlockSpec((c, DV), lambda n: (rn(n), 0)), pl.BlockSpec((c, DV), lambda n: (rn(n), 0)),
                   pl.BlockSpec((c, DK), lambda n: (rn(n), 0)),
                   pl.BlockSpec((1, GLA_HV), lambda n: (0, 0)), pl.BlockSpec((1, DK), lambda n: (0, 0))],
        out_shape=[jax.ShapeDtypeStruct((L, DK), BF16), jax.ShapeDtypeStruct((L, DK), BF16),
                   jax.ShapeDtypeStruct((L, DV), BF16), jax.ShapeDtypeStruct((L, DV), BF16),
                   jax.ShapeDtypeStruct((L, DK), BF16),
                   jax.ShapeDtypeStruct((1, GLA_HV), F32), jax.ShapeDtypeStruct((1, DK), F32)],
        scratch_shapes=[pltpu.VMEM((nh, GLA_HV, GLA_HK), F32)],
        compiler_params=pltpu.CompilerParams(dimension_semantics=("arbitrary",)),
    )(d_ycat, proj_main, proj_main, proj_main, proj_main, proj_low, s_prev, gate_up_pad, gate_bias, norm_w)


def _adamw(w, g, m, v, name):
    R, C = w.shape
    tr = _blk(R, 256, SUBLANES)
    c1 = 1.0 - ADAM_B1 ** ADAM_STEP
    c2 = 1.0 - ADAM_B2 ** ADAM_STEP

    def body(w_ref, g_ref, m_ref, v_ref, d_ref, nm_ref, nv_ref):
        g_ = g_ref[...]
        m_ = ADAM_B1 * m_ref[...] + (1.0 - ADAM_B1) * g_
        v_ = ADAM_B2 * v_ref[...] + (1.0 - ADAM_B2) * (g_ * g_)
        nm_ref[...] = m_
        nv_ref[...] = v_
        d_ref[...] = -ADAM_LR * ((m_ / c1) / (jnp.sqrt(v_ / c2) + ADAM_EPS) + ADAM_WD * w_ref[...])

    blk = pl.BlockSpec((tr, C), lambda i: (i, 0))
    sd = jax.ShapeDtypeStruct((R, C), F32)
    return pl.pallas_call(
        body, name=name, grid=(R // tr,), in_specs=[blk] * 4, out_specs=[blk] * 3, out_shape=[sd] * 3,
        compiler_params=pltpu.CompilerParams(dimension_semantics=("parallel",)),
    )(w, g, m, v)


def _my_pos():
    return lax.axis_index("x"), lax.axis_index("y"), lax.axis_index("c")


def _gather_weights(shards):
    n = len(shards)
    halves = [s.shape[0] // 2 for s in shards]

    def body(*refs):
        ins, outs = refs[:n], refs[n:2 * n]
        send_sems, recv_sems, loc_sems = refs[2 * n:]
        x, y, c = _my_pos()
        me = 2 * x + y
        local = []
        for a in range(n):
            cp = pltpu.make_async_copy(ins[a], outs[a].at[me], loc_sems.at[a])
            cp.start()
            local.append(cp)

        def piece(a, chip, half):
            return outs[a].at[chip, pl.ds(half * halves[a], halves[a]), :]

        def copy(a, k, src_chip, half, to):
            sl = piece(a, src_chip, half)
            return pltpu.make_async_remote_copy(src_ref=sl, dst_ref=sl, send_sem=send_sems.at[a, k], recv_sem=recv_sems.at[a, k],
                                                device_id=to, device_id_type=MESH)

        def first(a, d, to):
            src = ins[a].at[pl.ds(c * halves[a], halves[a]), :]
            return pltpu.make_async_remote_copy(src_ref=src, dst_ref=piece(a, me, c), send_sem=send_sems.at[a, d - 1],
                                                recv_sem=recv_sems.at[a, d - 1], device_id=to, device_id_type=MESH)

        sent = []
        for d in (1, 2, 3):
            to = (x ^ (d >> 1), y ^ (d & 1), c)
            for a in range(n):
                cp = first(a, d, to)
                cp.start()
                sent.append(cp)
        for d in (1, 2, 3):
            chip = (x ^ (d >> 1)) * 2 + (y ^ (d & 1))
            for a in range(n):
                copy(a, d - 1, chip, c, (x, y, c)).wait_recv()
                fw = copy(a, 2 + d, chip, c, (x, y, 1 - c))
                fw.start()
                sent.append(fw)
        for d in (1, 2, 3):
            chip = (x ^ (d >> 1)) * 2 + (y ^ (d & 1))
            for a in range(n):
                copy(a, 2 + d, chip, 1 - c, (x, y, c)).wait_recv()
        for cp in sent:
            cp.wait_send()
        for cp in local:
            cp.wait()

    hbm = pl.BlockSpec(memory_space=pltpu.HBM)
    return pl.pallas_call(
        body, name="gather_weights",
        in_specs=[hbm] * n, out_specs=[hbm] * n,
        out_shape=[jax.ShapeDtypeStruct((4,) + s.shape, s.dtype) for s in shards],
        scratch_shapes=[pltpu.SemaphoreType.DMA((n, 6)), pltpu.SemaphoreType.DMA((n, 6)), pltpu.SemaphoreType.DMA((n,))],
    )(*shards)


def _pair_exchange(gs):
    nk, rows2, lanes = gs.shape
    hrows = rows2 // 2

    def body(g_ref, o_ref, send_sem, recv_sem):
        x, y, c = _my_pos()
        cp = pltpu.make_async_remote_copy(
            src_ref=g_ref.at[:, pl.ds((1 - c) * hrows, hrows), :], dst_ref=o_ref, send_sem=send_sem, recv_sem=recv_sem,
            device_id=(x, y, 1 - c), device_id_type=MESH)
        cp.start()
        cp.wait()

    hbm = pl.BlockSpec(memory_space=pltpu.HBM)
    return pl.pallas_call(
        body, name="grad_pair_exchange", in_specs=[hbm], out_specs=hbm,
        out_shape=jax.ShapeDtypeStruct((nk, hrows, lanes), gs.dtype),
        scratch_shapes=[pltpu.SemaphoreType.DMA, pltpu.SemaphoreType.DMA],
    )(gs)


def _pair_add(gs, got, c_arr):
    nk, rows2, lanes = gs.shape
    hrows = rows2 // 2
    tr = _blk(hrows, 2048, SUBLANES)
    nb = hrows // tr

    def body(c_ref, a_ref, b_ref, o_ref):
        o_ref[...] = a_ref[...] + b_ref[...]

    return pl.pallas_call(
        body, name="grad_pair_add",
        grid_spec=pltpu.PrefetchScalarGridSpec(
            num_scalar_prefetch=1, grid=(nk, nb),
            in_specs=[pl.BlockSpec((1, tr, lanes), lambda k, i, c_ref: (k, c_ref[0] * nb + i, 0)),
                      pl.BlockSpec((1, tr, lanes), lambda k, i, c_ref: (k, i, 0))],
            out_specs=pl.BlockSpec((1, tr, lanes), lambda k, i, c_ref: (k, i, 0))),
        out_shape=jax.ShapeDtypeStruct((nk, hrows, lanes), F32),
        compiler_params=pltpu.CompilerParams(dimension_semantics=("parallel", "parallel")),
    )(c_arr, gs, got)


def _chip_scatter(ps):
    nk, hrows, lanes = ps.shape

    def body(p_ref, o_ref, send_sems, recv_sems, loc_sem):
        x, y, c = _my_pos()
        me = 2 * x + y
        own = pltpu.make_async_copy(p_ref.at[me], o_ref.at[0], loc_sem)
        own.start()
        sent = []
        for d in (1, 2, 3):
            tx, ty = x ^ (d >> 1), y ^ (d & 1)
            cp = pltpu.make_async_remote_copy(src_ref=p_ref.at[2 * tx + ty], dst_ref=o_ref.at[d], send_sem=send_sems.at[d - 1],
                                              recv_sem=recv_sems.at[d - 1], device_id=(tx, ty, c), device_id_type=MESH)
            cp.start()
            sent.append(cp)
        for cp in sent:
            cp.wait()
        own.wait()

    hbm = pl.BlockSpec(memory_space=pltpu.HBM)
    return pl.pallas_call(
        body, name="grad_chip_scatter", in_specs=[hbm], out_specs=hbm,
        out_shape=jax.ShapeDtypeStruct((nk, hrows, lanes), ps.dtype),
        scratch_shapes=[pltpu.SemaphoreType.DMA((3,)), pltpu.SemaphoreType.DMA((3,)), pltpu.SemaphoreType.DMA],
    )(ps)


def _sum_slots(r, name):
    ns, hrows, lanes = r.shape
    tr = _blk(hrows, 2048, SUBLANES)

    def body(r_ref, o_ref):
        acc = r_ref[0]
        for s in range(1, ns):
            acc = acc + r_ref[s]
        o_ref[...] = acc

    return pl.pallas_call(
        body, name=name, grid=(hrows // tr,),
        in_specs=[pl.BlockSpec((ns, tr, lanes), lambda i: (0, i, 0))],
        out_specs=pl.BlockSpec((tr, lanes), lambda i: (i, 0)),
        out_shape=jax.ShapeDtypeStruct((hrows, lanes), F32),
        compiler_params=pltpu.CompilerParams(dimension_semantics=("parallel",)),
    )(r)


def _pair_gather(half):
    hrows, lanes = half.shape

    def body(h_ref, o_ref, send_sem, recv_sem, loc_sem):
        x, y, c = _my_pos()
        own = pltpu.make_async_copy(h_ref, o_ref.at[c], loc_sem)
        own.start()
        cp = pltpu.make_async_remote_copy(src_ref=h_ref, dst_ref=o_ref.at[c], send_sem=send_sem, recv_sem=recv_sem,
                                          device_id=(x, y, 1 - c), device_id_type=MESH)
        cp.start()
        cp.wait()
        own.wait()

    hbm = pl.BlockSpec(memory_space=pltpu.HBM)
    return pl.pallas_call(
        body, name="grad_pair_gather", in_specs=[hbm], out_specs=hbm,
        out_shape=jax.ShapeDtypeStruct((2, hrows, lanes), half.dtype),
        scratch_shapes=[pltpu.SemaphoreType.DMA, pltpu.SemaphoreType.DMA, pltpu.SemaphoreType.DMA],
    )(half)


def _allreduce_small(vec):
    rows, lanes = vec.shape
    r8 = rows // 8

    def body(v_ref, o_ref, got_ref, send_sems, recv_sems):
        x, y, c = _my_pos()
        me = 4 * x + 2 * y + c

        def peer(d):
            return (x ^ (d >> 2), y ^ ((d >> 1) & 1), c ^ (d & 1))

        def lin(p):
            return 4 * p[0] + 2 * p[1] + p[2]

        sent = []
        for d in range(1, 8):
            to = peer(d)
            cp = pltpu.make_async_remote_copy(
                src_ref=v_ref.at[pl.ds(pl.multiple_of(lin(to) * r8, SUBLANES), r8), :], dst_ref=got_ref.at[d],
                send_sem=send_sems.at[0, d], recv_sem=recv_sems.at[0, d], device_id=to, device_id_type=MESH)
            cp.start()
            sent.append(cp)
        mine = pl.ds(pl.multiple_of(me * r8, SUBLANES), r8)
        acc = v_ref[mine, :]
        for d in range(1, 8):
            sent[d - 1].wait_recv()
            acc = acc + got_ref[d]
        got_ref[0] = acc
        o_ref[mine, :] = acc
        for d in range(1, 8):
            cp = pltpu.make_async_remote_copy(
                src_ref=got_ref.at[0], dst_ref=o_ref.at[mine, :],
                send_sem=send_sems.at[1, d], recv_sem=recv_sems.at[1, d], device_id=peer(d), device_id_type=MESH)
            cp.start()
            sent.append(cp)
        for d in range(1, 8):
            src = pl.ds(pl.multiple_of(lin(peer(d)) * r8, SUBLANES), r8)
            pltpu.make_async_remote_copy(
                src_ref=got_ref.at[0], dst_ref=o_ref.at[src, :],
                send_sem=send_sems.at[1, d], recv_sem=recv_sems.at[1, d], device_id=peer(d), device_id_type=MESH).wait_recv()
        for cp in sent:
            cp.wait_send()

    vm = pl.BlockSpec(memory_space=pltpu.VMEM)
    return pl.pallas_call(
        body, name="allreduce_small", in_specs=[vm], out_specs=vm,
        out_shape=jax.ShapeDtypeStruct((rows, lanes), F32),
        scratch_shapes=[pltpu.VMEM((8, r8, lanes), F32), pltpu.SemaphoreType.DMA((2, 8)), pltpu.SemaphoreType.DMA((2, 8))],
    )(vec)


def _pack_rows(arrs, row_mult):
    flat = jnp.concatenate([a.reshape(-1) for a in arrs])
    n = flat.shape[0]
    rows = -(-n // (LANES * row_mult)) * row_mult
    flat = jnp.pad(flat, (0, rows * LANES - n))
    return flat.reshape(rows, LANES)


def _unpack_rows(packed, shapes):
    flat = packed.reshape(-1)
    out, off = [], 0
    for s in shapes:
        n = math.prod(s)
        out.append(flat[off:off + n].reshape(s))
        off += n
    return out


def kernel(x, pre_norm_w, w_in, s5_A_re, s5_A_im, s5_B_re, s5_B_im, s5_C_re, s5_C_im, s5_D, s5_log_dt, s5_glu_w, s5_glu_b, gla_gate_up, gla_gate_bias, gla_norm_w, w_out, post_norm_w, loss_target, m_pre_norm_w, m_w_in, m_s5_A_re, m_s5_A_im, m_s5_B_re, m_s5_B_im, m_s5_C_re, m_s5_C_im, m_s5_D, m_s5_log_dt, m_s5_glu_w, m_s5_glu_b, m_gla_gate_up, m_gla_gate_bias, m_gla_norm_w, m_w_out, m_post_norm_w, v_pre_norm_w, v_w_in, v_s5_A_re, v_s5_A_im, v_s5_B_re, v_s5_B_im, v_s5_C_re, v_s5_C_im, v_s5_D, v_s5_log_dt, v_s5_glu_w, v_s5_glu_b, v_gla_gate_up, v_gla_gate_bias, v_gla_norm_w, v_w_out, v_post_norm_w):
    names = ["pre_norm_w", "w_in", "s5_A_re", "s5_A_im", "s5_B_re", "s5_B_im", "s5_C_re", "s5_C_im", "s5_D", "s5_log_dt",
             "s5_glu_w", "s5_glu_b", "gla_gate_up", "gla_gate_bias", "gla_norm_w", "w_out", "post_norm_w"]
    W = dict(zip(names, (pre_norm_w, w_in, s5_A_re, s5_A_im, s5_B_re, s5_B_im, s5_C_re, s5_C_im, s5_D, s5_log_dt,
                         s5_glu_w, s5_glu_b, gla_gate_up, gla_gate_bias, gla_norm_w, w_out, post_norm_w)))
    M = dict(zip(names, (m_pre_norm_w, m_w_in, m_s5_A_re, m_s5_A_im, m_s5_B_re, m_s5_B_im, m_s5_C_re, m_s5_C_im, m_s5_D,
                         m_s5_log_dt, m_s5_glu_w, m_s5_glu_b, m_gla_gate_up, m_gla_gate_bias, m_gla_norm_w, m_w_out,
                         m_post_norm_w)))
    V = dict(zip(names, (v_pre_norm_w, v_w_in, v_s5_A_re, v_s5_A_im, v_s5_B_re, v_s5_B_im, v_s5_C_re, v_s5_C_im, v_s5_D,
                         v_s5_log_dt, v_s5_glu_w, v_s5_glu_b, v_gla_gate_up, v_gla_gate_bias, v_gla_norm_w, v_w_out,
                         v_post_norm_w)))
    sharded = ("w_in", "s5_glu_w", "w_out", "gla_gate_up")

    xb = x[0]
    tgt = loss_target[0]
    L, D = xb.shape
    DS = D // 2
    G = DS // S5_GROUP
    P = S5_STATE
    NB = DS // S5_COLS
    DV = D - DS
    DK = DV // 2
    WM = 2 * DS + 2 * DK + 2 * DV
    nsh = w_in.shape[2]

    g_win, g_glu, g_wout, g_gup = _gather_weights(
        [w_in[0].astype(BF16), s5_glu_w[0].astype(BF16), w_out[0].astype(BF16), gla_gate_up[0]])
    w_full = jnp.moveaxis(g_win, 0, 1).reshape(D, 4 * nsh)
    w_main = w_full[:, :WM]
    w_low = jnp.pad(w_full[:, WM:], ((0, 0), (0, LANES - GLA_RANK)))
    glu_w = g_glu.reshape(DS, DS)
    wout = g_wout.reshape(D, D)
    gup = jnp.moveaxis(g_gup, 0, 1).reshape(GLA_RANK, DK)
    gup_pad = jnp.pad(gup, ((0, LANES - GLA_RANK), (0, 0))).astype(BF16)

    e16 = jnp.repeat(jnp.eye(P, dtype=F32), S5_GROUP, axis=1)
    a_re, a_im = s5_A_re[0], s5_A_im[0]
    log_dt = s5_log_dt[0].reshape(G, 1)
    b_re = s5_B_re[0].reshape(G, P * S5_GROUP)
    b_im = s5_B_im[0].reshape(G, P * S5_GROUP)
    bb_re, bb_im, pw_re, pw_im = _s5_prep_fwd(a_re, a_im, log_dt, b_re, b_im, e16)
    to_hp = lambda m: m.reshape(G, P, S5_GROUP).transpose(0, 2, 1)
    bbd_re = _block_diag(to_hp(bb_re), NB, S5_GROUP, P)
    bbd_im = _block_diag(to_hp(bb_im), NB, S5_GROUP, P)
    cbd_re = _block_diag(s5_C_re[0].transpose(0, 2, 1), NB, P, S5_GROUP)
    cbd_im = _block_diag(s5_C_im[0].transpose(0, 2, 1), NB, P, S5_GROUP)
    tab = _scan_tables(pw_re, pw_im, NB)
    dvec = s5_D

    h = _prenorm_fwd(xb, pre_norm_w)
    proj_main = _mm(h, w_main, name="in_proj")
    proj_low = _mm(h, w_low, name="in_proj_low")
    y_pre, s_re, s_im = _s5_scan_fwd(proj_main, bbd_re, bbd_im, cbd_re, cbd_im, dvec, tab, DS)
    y_s5, t_pre = _s5_post_fwd(y_pre, proj_main, glu_w, s5_glu_b, DS)
    y_gla, s_prev = _gla_fwd(proj_main, proj_low, gup_pad, gla_gate_bias, gla_norm_w, DS, DK, DV)
    ycat = jnp.concatenate([y_s5, y_gla], axis=1)
    mixed = _mm(ycat, wout, name="out_proj")
    loss11, d_mixed, dout, g_post_w = _post_fwd_bwd(mixed, xb, tgt, post_norm_w)

    d_ycat = _mm(d_mixed, wout, tb=True, name="out_proj_dx")
    g_wout_full = _mm(ycat, d_mixed, ta=True, name="out_proj_dw")
    d_ypre, d_z, d_t, y1, g_glu_b = _s5_post_bwd(d_ycat, y_pre, proj_main, t_pre, glu_w, DS)
    g_glu_full = _mm(y1, d_t, ta=True, name="glu_dw")
    d_u, g_D, gcbd_re, gcbd_im, gbbd_re, gbbd_im, gab_re, gab_im = _s5_scan_bwd(
        d_ypre, proj_main, s_re, s_im, bbd_re, bbd_im, cbd_re, cbd_im, dvec, tab, DS)
    d_q, d_k, d_v, d_gz, d_a, g_norm_w, g_gate_bias = _gla_bwd(
        d_ycat, proj_main, proj_low, s_prev, gup_pad, gla_gate_bias, gla_norm_w, DS, DK, DV)
    d_low = _mm(d_a, gup_pad, tb=True, out_dtype=BF16, name="gate_dx")
    g_gup_pad = _mm(proj_low, d_a, ta=True, name="gate_dw")
    d_proj = jnp.concatenate([d_u, d_z, d_q, d_k, d_v, d_gz], axis=1)
    dh_main = _mm(d_proj, w_main, tb=True, name="in_proj_dx")
    dh_low = _mm(d_low, w_low, tb=True, name="in_proj_low_dx")
    g_wmain = _mm(h, d_proj, ta=True, name="in_proj_dw")
    g_wlow = _mm(h, d_low, ta=True, name="in_proj_low_dw")
    grad_x, g_pre_w = _prenorm_bwd(xb, dh_main, dh_low, dout, pre_norm_w)

    from_hp = lambda m: m.transpose(0, 2, 1).reshape(G, P * S5_GROUP)
    gbb_re = from_hp(_block_diag_extract(gbbd_re, NB, S5_GROUP, P))
    gbb_im = from_hp(_block_diag_extract(gbbd_im, NB, S5_GROUP, P))
    g_a_re, g_a_im, g_b_re, g_b_im, g_ldt = _s5_prep_bwd(
        a_re, a_im, log_dt, b_re, b_im, e16, gbb_re, gbb_im, gab_re.reshape(G, P), gab_im.reshape(G, P))
    g_c_re = _block_diag_extract(gcbd_re, NB, P, S5_GROUP).transpose(0, 2, 1)
    g_c_im = _block_diag_extract(gcbd_im, NB, P, S5_GROUP).transpose(0, 2, 1)

    loss = lax.psum(loss11[0, 0], ("x", "y", "c"))

    small = [n for n in names if n not in sharded]
    g_small = {"pre_norm_w": g_pre_w, "s5_A_re": g_a_re, "s5_A_im": g_a_im, "s5_B_re": g_b_re, "s5_B_im": g_b_im,
               "s5_C_re": g_c_re, "s5_C_im": g_c_im, "s5_D": g_D, "s5_log_dt": g_ldt, "s5_glu_b": g_glu_b,
               "gla_gate_bias": g_gate_bias, "gla_norm_w": g_norm_w, "post_norm_w": g_post_w}
    small_shapes = [W[n].shape for n in small]
    gsum = _allreduce_small(_pack_rows([g_small[n] for n in small], 8 * SUBLANES))
    pk = lambda d: _pack_rows([d[n] for n in small], 8 * SUBLANES)
    d_s, m_s, v_s = _adamw(pk(W), gsum, pk(M), pk(V), "adamw_small")
    G_out = dict(zip(small, _unpack_rows(gsum, small_shapes)))
    D_out = dict(zip(small, _unpack_rows(d_s, small_shapes)))
    M_out = dict(zip(small, _unpack_rows(m_s, small_shapes)))
    V_out = dict(zip(small, _unpack_rows(v_s, small_shapes)))

    g_win_full = jnp.concatenate([g_wmain, g_wlow[:, :GLA_RANK]], axis=1)
    gsh = DS // 4
    wsh = D // 4
    ksh = DK // 4
    packs = []
    for k in range(4):
        packs.append(_pack_rows([g_win_full[:, k * nsh:(k + 1) * nsh], g_glu_full[k * gsh:(k + 1) * gsh],
                                 g_wout_full[k * wsh:(k + 1) * wsh], g_gup_pad[:GLA_RANK, k * ksh:(k + 1) * ksh]],
                                2 * SUBLANES))
    gs = jnp.stack(packs)
    c_arr = lax.axis_index("c").astype(jnp.int32).reshape(1)
    got = _pair_exchange(gs)
    ps = _pair_add(gs, got, c_arr)
    half = _sum_slots(_chip_scatter(ps), "grad_chip_sum")
    full = _pair_gather(half)
    sh_shapes = [(D, nsh), (gsh, DS), (wsh, D), (GLA_RANK, ksh)]
    for n, g in zip(sharded, _unpack_rows(full, sh_shapes)):
        G_out[n] = g[None]
        d_, m_, v_ = _adamw(W[n][0], g, M[n][0], V[n][0], "adamw_" + n)
        D_out[n], M_out[n], V_out[n] = d_[None], m_[None], v_[None]

    return (loss, grad_x[None], *[G_out[n] for n in names], *[D_out[n] for n in names],
            *[M_out[n] for n in names], *[V_out[n] for n in names])
```

```python
import functools
import math

import jax
import jax.numpy as jnp
from jax import lax
from jax.experimental import pallas as pl
from jax.experimental.pallas import tpu as pltpu

F32 = jnp.float32
BF16 = jnp.bfloat16
HI = lax.Precision.HIGHEST
MESH = pl.DeviceIdType.MESH

EPS = 1e-6
S5_GROUP = 16
S5_STATE = 64
GLA_HK = 128
GLA_HV = 256
GLA_RANK = 16
GLA_TAU = 16.0
GLA_CHUNK = 64
LANES = 128
SUBLANES = 8
S5_COLS = 128
S5_LANES = (S5_COLS // S5_GROUP) * S5_STATE

ADAM_LR = 0.001
ADAM_B1 = 0.9
ADAM_B2 = 0.999
ADAM_EPS = 1e-08
ADAM_WD = 0.01
ADAM_STEP = 10

GELU_K = math.sqrt(2.0 / math.pi)
GELU_C = 0.044715


def _blk(n, pref, unit=LANES):
    best = None
    b = unit
    while b <= min(n, pref):
        if n % b == 0:
            best = b
        b += unit
    return best if best is not None else n


def _dot(a, b, dn=(((1,), (0,)), ((), ()))):
    return lax.dot_general(a.astype(BF16), b.astype(BF16), dn, preferred_element_type=F32)


def _dot_hi(a, b, dn=(((1,), (0,)), ((), ()))):
    return lax.dot_general(a, b, dn, precision=HI, preferred_element_type=F32)


NN = (((1,), (0,)), ((), ()))
NT = (((1,), (1,)), ((), ()))
TN = (((0,), (0,)), ((), ()))


def _sigmoid(x):
    return 1.0 / (1.0 + jnp.exp(-x))


def _gelu(y):
    return 0.5 * y * (1.0 + jnp.tanh(GELU_K * (y + GELU_C * y * y * y)))


def _gelu_grad(y):
    th = jnp.tanh(GELU_K * (y + GELU_C * y * y * y))
    return 0.5 * (1.0 + th) + 0.5 * y * (1.0 - th * th) * GELU_K * (1.0 + 3.0 * GELU_C * y * y)


def _mm(a, b, *, name, ta=False, tb=False, out_dtype=F32, bm=1024, bn=1024, bk=512):
    if ta:
        K, M = a.shape
    else:
        M, K = a.shape
    if tb:
        N, K2 = b.shape
    else:
        K2, N = b.shape
    assert K == K2, (a.shape, b.shape, ta, tb)
    bm, bn, bk = _blk(M, bm), _blk(N, bn), _blk(K, bk)
    nk = K // bk
    dn = (((0 if ta else 1,), (1 if tb else 0,)), ((), ()))

    def body(a_ref, b_ref, o_ref, acc_ref):
        k = pl.program_id(2)

        @pl.when(k == 0)
        def _():
            acc_ref[...] = jnp.zeros_like(acc_ref)

        acc_ref[...] += _dot(a_ref[...], b_ref[...], dn)

        @pl.when(k == nk - 1)
        def _():
            o_ref[...] = acc_ref[...].astype(out_dtype)

    a_spec = pl.BlockSpec((bk, bm), lambda i, j, k: (k, i)) if ta else pl.BlockSpec((bm, bk), lambda i, j, k: (i, k))
    b_spec = pl.BlockSpec((bn, bk), lambda i, j, k: (j, k)) if tb else pl.BlockSpec((bk, bn), lambda i, j, k: (k, j))
    return pl.pallas_call(
        body,
        name=name,
        grid=(M // bm, N // bn, nk),
        in_specs=[a_spec, b_spec],
        out_specs=pl.BlockSpec((bm, bn), lambda i, j, k: (i, j)),
        out_shape=jax.ShapeDtypeStruct((M, N), out_dtype),
        scratch_shapes=[pltpu.VMEM((bm, bn), F32)],
        compiler_params=pltpu.CompilerParams(dimension_semantics=("parallel", "parallel", "arbitrary")),
    )(a, b)


def _mm_ksplit(a1, a2, b, *, name, out_dtype=F32, bm=1024, bn=1024, bk=512):
    M, K1 = a1.shape
    K2 = a2.shape[1]
    N = b.shape[0]
    bm, bn = _blk(M, bm), _blk(N, bn)
    bk = _blk(math.gcd(K1, K2), bk)
    nk1, nk = K1 // bk, (K1 + K2) // bk

    def body(a1_ref, a2_ref, b_ref, o_ref, acc_ref):
        k = pl.program_id(2)

        @pl.when(k == 0)
        def _():
            acc_ref[...] = jnp.zeros_like(acc_ref)

        @pl.when(k < nk1)
        def _():
            acc_ref[...] += _dot(a1_ref[...], b_ref[...], NT)

        @pl.when(k >= nk1)
        def _():
            acc_ref[...] += _dot(a2_ref[...], b_ref[...], NT)

        @pl.when(k == nk - 1)
        def _():
            o_ref[...] = acc_ref[...].astype(out_dtype)

    return pl.pallas_call(
        body, name=name, grid=(M // bm, N // bn, nk),
        in_specs=[pl.BlockSpec((bm, bk), lambda i, j, k: (i, jnp.minimum(k, nk1 - 1))),
                  pl.BlockSpec((bm, bk), lambda i, j, k: (i, jnp.maximum(k - nk1, 0))),
                  pl.BlockSpec((bn, bk), lambda i, j, k: (j, k))],
        out_specs=pl.BlockSpec((bm, bn), lambda i, j, k: (i, j)),
        out_shape=jax.ShapeDtypeStruct((M, N), out_dtype),
        scratch_shapes=[pltpu.VMEM((bm, bn), F32)],
        compiler_params=pltpu.CompilerParams(dimension_semantics=("parallel", "parallel", "arbitrary")),
    )(a1, a2, b)


def _mm_nsplit(a, b1, b2, *, name, out_dtype=F32, bm=1024, bn=1024, bk=512):
    K, M = a.shape
    N1, N2 = b1.shape[1], b2.shape[1]
    bm, bk = _blk(M, bm), _blk(K, bk)
    bn = _blk(math.gcd(N1, N2), bn)
    nj1, nj = N1 // bn, (N1 + N2) // bn
    nk = K // bk

    def body(a_ref, b1_ref, b2_ref, o_ref, acc_ref):
        j = pl.program_id(1)
        k = pl.program_id(2)

        @pl.when(k == 0)
        def _():
            acc_ref[...] = jnp.zeros_like(acc_ref)

        @pl.when(j < nj1)
        def _():
            acc_ref[...] += _dot(a_ref[...], b1_ref[...], TN)

        @pl.when(j >= nj1)
        def _():
            acc_ref[...] += _dot(a_ref[...], b2_ref[...], TN)

        @pl.when(k == nk - 1)
        def _():
            o_ref[...] = acc_ref[...].astype(out_dtype)

    return pl.pallas_call(
        body, name=name, grid=(M // bm, nj, nk),
        in_specs=[pl.BlockSpec((bk, bm), lambda i, j, k: (k, i)),
                  pl.BlockSpec((bk, bn), lambda i, j, k: (jnp.where(j < nj1, k, nk - 1), jnp.minimum(j, nj1 - 1))),
                  pl.BlockSpec((bk, bn), lambda i, j, k: (jnp.where(j >= nj1, k, 0), jnp.maximum(j - nj1, 0)))],
        out_specs=pl.BlockSpec((bm, bn), lambda i, j, k: (i, j)),
        out_shape=jax.ShapeDtypeStruct((M, N1 + N2), out_dtype),
        scratch_shapes=[pltpu.VMEM((bm, bn), F32)],
        compiler_params=pltpu.CompilerParams(dimension_semantics=("parallel", "parallel", "arbitrary")),
    )(a, b1, b2)


def _prenorm_fwd(x, w):
    L, D = x.shape
    tr = _blk(L, 256, SUBLANES)

    def body(x_ref, w_ref, h_ref):
        xv = x_ref[...]
        r = lax.rsqrt(jnp.mean(xv * xv, axis=-1, keepdims=True) + EPS)
        h_ref[...] = (xv * r * w_ref[...]).astype(BF16)

    return pl.pallas_call(
        body, name="prenorm_fwd", grid=(L // tr,),
        in_specs=[pl.BlockSpec((tr, D), lambda i: (i, 0)), pl.BlockSpec((1, D), lambda i: (0, 0))],
        out_specs=pl.BlockSpec((tr, D), lambda i: (i, 0)),
        out_shape=jax.ShapeDtypeStruct((L, D), BF16),
        compiler_params=pltpu.CompilerParams(dimension_semantics=("parallel",)),
    )(x, w)


def _post_fwd_bwd(mixed, x, target, w):
    L, D = x.shape
    tr = _blk(L, 256, SUBLANES)
    nsteps = L // tr

    def body(mx_ref, x_ref, t_ref, w_ref, loss_ref, dm_ref, dout_ref, gw_ref, acc_ref):
        i = pl.program_id(0)

        @pl.when(i == 0)
        def _():
            acc_ref[...] = jnp.zeros_like(acc_ref)
            gw_ref[...] = jnp.zeros_like(gw_ref)

        mx = mx_ref[...]
        wv = w_ref[...]
        r = lax.rsqrt(jnp.mean(mx * mx, axis=-1, keepdims=True) + EPS)
        n = mx * r
        err = x_ref[...] + n * wv - t_ref[...]
        acc_ref[...] += jnp.sum(err * err, axis=0, keepdims=True)
        dout = err * (1.0 / D)
        dout_ref[...] = dout
        gw_ref[...] += jnp.sum(dout * n, axis=0, keepdims=True)
        dn = dout * wv
        dm_ref[...] = (r * (dn - n * jnp.mean(dn * n, axis=-1, keepdims=True))).astype(BF16)

        @pl.when(i == nsteps - 1)
        def _():
            loss_ref[...] = jnp.sum(acc_ref[...], axis=-1, keepdims=True) * (0.5 / D)

    row = pl.BlockSpec((tr, D), lambda i: (i, 0))
    vec = pl.BlockSpec((1, D), lambda i: (0, 0))
    return pl.pallas_call(
        body, name="post_fwd_bwd", grid=(nsteps,),
        in_specs=[row, row, row, vec],
        out_specs=[pl.BlockSpec((1, 1), lambda i: (0, 0)), row, row, vec],
        out_shape=[jax.ShapeDtypeStruct((1, 1), F32), jax.ShapeDtypeStruct((L, D), BF16),
                   jax.ShapeDtypeStruct((L, D), F32), jax.ShapeDtypeStruct((1, D), F32)],
        scratch_shapes=[pltpu.VMEM((1, D), F32)],
        compiler_params=pltpu.CompilerParams(dimension_semantics=("arbitrary",)),
    )(mixed, x, target, w)


def _prenorm_bwd(x, dh_main, dh_low, dout, w):
    L, D = x.shape
    tr = _blk(L, 256, SUBLANES)

    def body(x_ref, a_ref, b_ref, dout_ref, w_ref, gx_ref, gw_ref):
        i = pl.program_id(0)

        @pl.when(i == 0)
        def _():
            gw_ref[...] = jnp.zeros_like(gw_ref)

        xv = x_ref[...]
        r = lax.rsqrt(jnp.mean(xv * xv, axis=-1, keepdims=True) + EPS)
        n = xv * r
        dh = a_ref[...] + b_ref[...]
        gw_ref[...] += jnp.sum(dh * n, axis=0, keepdims=True)
        dn = dh * w_ref[...]
        gx_ref[...] = dout_ref[...] + r * (dn - n * jnp.mean(dn * n, axis=-1, keepdims=True))

    row = pl.BlockSpec((tr, D), lambda i: (i, 0))
    vec = pl.BlockSpec((1, D), lambda i: (0, 0))
    return pl.pallas_call(
        body, name="prenorm_bwd", grid=(L // tr,),
        in_specs=[row, row, row, row, vec],
        out_specs=[row, vec],
        out_shape=[jax.ShapeDtypeStruct((L, D), F32), jax.ShapeDtypeStruct((1, D), F32)],
        compiler_params=pltpu.CompilerParams(dimension_semantics=("arbitrary",)),
    )(x, dh_main, dh_low, dout, w)


def _s5_disc(a_re_raw, a_im, dt):
    a_re = jnp.minimum(a_re_raw, -1e-4)
    mag = jnp.exp(a_re * dt)
    ph = a_im * dt
    ab_re = mag * jnp.cos(ph)
    ab_im = mag * jnp.sin(ph)
    inv_n = 1.0 / (a_re * a_re + a_im * a_im)
    ia_re = a_re * inv_n
    ia_im = -a_im * inv_n
    n_re = ab_re - 1.0
    f_re = n_re * ia_re - ab_im * ia_im
    f_im = n_re * ia_im + ab_im * ia_re
    return a_re, ab_re, ab_im, f_re, f_im, ia_re, ia_im


def _s5_prep_fwd(a_re, a_im, log_dt, b_re, b_im, e16):
    G, P = a_re.shape
    PH = b_re.shape[1]

    def body(are_ref, aim_ref, ldt_ref, bre_ref, bim_ref, e_ref, bbre_ref, bbim_ref, pwre_ref, pwim_ref):
        dt = jnp.exp(ldt_ref[...])
        _, ab_re, ab_im, f_re, f_im, _, _ = _s5_disc(are_ref[...], aim_ref[...], dt)
        fx_re = _dot_hi(f_re, e_ref[...])
        fx_im = _dot_hi(f_im, e_ref[...])
        br, bi = bre_ref[...], bim_ref[...]
        bbre_ref[...] = fx_re * br - fx_im * bi
        bbim_ref[...] = fx_re * bi + fx_im * br
        pr, pi = ab_re, ab_im
        pwre_ref[0] = pr
        pwim_ref[0] = pi
        for k in range(1, SUBLANES):
            pr, pi = pr * ab_re - pi * ab_im, pr * ab_im + pi * ab_re
            pwre_ref[k] = pr
            pwim_ref[k] = pi

    vm = pl.BlockSpec(memory_space=pltpu.VMEM)
    return pl.pallas_call(
        body, name="s5_prep_fwd",
        in_specs=[vm] * 6, out_specs=[vm] * 4,
        out_shape=[jax.ShapeDtypeStruct((G, PH), F32), jax.ShapeDtypeStruct((G, PH), F32),
                   jax.ShapeDtypeStruct((SUBLANES, G, P), F32), jax.ShapeDtypeStruct((SUBLANES, G, P), F32)],
    )(a_re, a_im, log_dt, b_re, b_im, e16)


def _s5_prep_bwd(a_re, a_im, log_dt, b_re, b_im, e16, gbb_re, gbb_im, gab_re, gab_im):
    G, P = a_re.shape
    PH = b_re.shape[1]

    def body(are_ref, aim_ref, ldt_ref, bre_ref, bim_ref, e_ref, gbr_ref, gbi_ref, gar_ref, gai_ref,
             o_are, o_aim, o_bre, o_bim, o_ldt):
        dt = jnp.exp(ldt_ref[...])
        a_raw = are_ref[...]
        a_imv = aim_ref[...]
        a_re_c, ab_re, ab_im, f_re, f_im, ia_re, ia_im = _s5_disc(a_raw, a_imv, dt)
        ev = e_ref[...]
        fx_re = _dot_hi(f_re, ev)
        fx_im = _dot_hi(f_im, ev)
        gbr, gbi = gbr_ref[...], gbi_ref[...]
        br, bi = bre_ref[...], bim_ref[...]
        o_bre[...] = fx_re * gbr + fx_im * gbi
        o_bim[...] = fx_re * gbi - fx_im * gbr
        gf_re = _dot_hi(br * gbr + bi * gbi, ev, NT)
        gf_im = _dot_hi(br * gbi - bi * gbr, ev, NT)
        gab_r = gar_ref[...] + ia_re * gf_re + ia_im * gf_im
        gab_i = gai_ref[...] + ia_re * gf_im - ia_im * gf_re
        q_re = f_re * ia_re - f_im * ia_im
        q_im = f_re * ia_im + f_im * ia_re
        ga_re = -(q_re * gf_re + q_im * gf_im)
        ga_im = -(q_re * gf_im - q_im * gf_re)
        gth_re = ab_re * gab_r + ab_im * gab_i
        gth_im = ab_re * gab_i - ab_im * gab_r
        ga_re = ga_re + dt * gth_re
        ga_im = ga_im + dt * gth_im
        gdt = jnp.sum(a_re_c * gth_re + a_imv * gth_im, axis=-1, keepdims=True)
        o_ldt[...] = gdt * dt
        slope = jnp.where(a_raw < -1e-4, 1.0, jnp.where(a_raw == -1e-4, 0.5, 0.0))
        o_are[...] = ga_re * slope
        o_aim[...] = ga_im

    vm = pl.BlockSpec(memory_space=pltpu.VMEM)
    return pl.pallas_call(
        body, name="s5_prep_bwd",
        in_specs=[vm] * 10, out_specs=[vm] * 5,
        out_shape=[jax.ShapeDtypeStruct((G, P), F32), jax.ShapeDtypeStruct((G, P), F32),
                   jax.ShapeDtypeStruct((G, PH), F32), jax.ShapeDtypeStruct((G, PH), F32),
                   jax.ShapeDtypeStruct((G, 1), F32)],
    )(a_re, a_im, log_dt, b_re, b_im, e16, gbb_re, gbb_im, gab_re, gab_im)


def _block_diag(m, nb, rows, cols):
    g8 = S5_COLS // S5_GROUP
    m = m.reshape(nb, g8, rows, 1, cols) * jnp.eye(g8, dtype=m.dtype)[None, :, None, :, None]
    return m.reshape(nb, g8 * rows, g8 * cols)


def _block_diag_extract(m, nb, rows, cols):
    g8 = S5_COLS // S5_GROUP
    m = m.reshape(nb, g8, rows, g8, cols)
    idx = jnp.arange(g8)
    return m[:, idx, :, idx, :].transpose(1, 0, 2, 3).reshape(nb * g8, rows, cols)


def _scan_tables(pw_re, pw_im, nb):
    pw_re = pw_re.reshape(SUBLANES, nb, 1, S5_LANES)
    pw_im = pw_im.reshape(SUBLANES, nb, 1, S5_LANES)
    row = jnp.arange(SUBLANES, dtype=jnp.int32).reshape(1, SUBLANES, 1)
    tabs = []
    for k in (1, 2, 4):
        keep = (row >= k).astype(F32)
        tabs += [pw_re[k - 1] * keep, pw_im[k - 1] * keep]
    tabs += [jnp.moveaxis(pw_re[:, :, 0, :], 0, 1), jnp.moveaxis(pw_im[:, :, 0, :], 0, 1)]
    for k in (1, 2, 4):
        keep = (row < SUBLANES - k).astype(F32)
        tabs += [pw_re[k - 1] * keep, -pw_im[k - 1] * keep]
    tabs += [jnp.moveaxis(pw_re[::-1, :, 0, :], 0, 1), -jnp.moveaxis(pw_im[::-1, :, 0, :], 0, 1)]
    tabs = [jnp.broadcast_to(t, (nb, SUBLANES, S5_LANES)) for t in tabs]
    return jnp.stack(tabs, axis=1)


def _scan8(xr, xi, tab_ref, base, shifts):
    for lvl, sh in enumerate(shifts):
        mr = tab_ref[0, base + 2 * lvl]
        mi = tab_ref[0, base + 2 * lvl + 1]
        ar = pltpu.roll(xr, sh, 0)
        ai = pltpu.roll(xi, sh, 0)
        xr, xi = xr + mr * ar - mi * ai, xi + mr * ai + mi * ar
    return xr, xi


def _s5_scan_fwd(proj_main, bbd_re, bbd_im, cbd_re, cbd_im, dvec, tab, DS):
    L = proj_main.shape[0]
    nb = DS // S5_COLS
    tb = _blk(L, 512, SUBLANES)
    nt = L // tb
    ng = tb // SUBLANES

    def body(u_ref, bre_ref, bim_ref, cre_ref, cim_ref, d_ref, tab_ref, y_ref, sre_ref, sim_ref, car_ref):
        t = pl.program_id(1)

        @pl.when(t == 0)
        def _():
            car_ref[...] = jnp.zeros_like(car_ref)

        u = u_ref[...]
        sre_ref[...] = _dot(u, bre_ref[0])
        sim_ref[...] = _dot(u, bim_ref[0])

        def grp(r, carry):
            cr, ci = carry
            off = pl.multiple_of(r * SUBLANES, SUBLANES)
            xr, xi = _scan8(sre_ref[pl.ds(off, SUBLANES), :], sim_ref[pl.ds(off, SUBLANES), :], tab_ref, 0, (1, 2, 4))
            pr, pi = tab_ref[0, 6], tab_ref[0, 7]
            xr, xi = xr + pr * cr - pi * ci, xi + pr * ci + pi * cr
            sre_ref[pl.ds(off, SUBLANES), :] = xr
            sim_ref[pl.ds(off, SUBLANES), :] = xi
            return (jnp.broadcast_to(xr[SUBLANES - 1:SUBLANES, :], xr.shape),
                    jnp.broadcast_to(xi[SUBLANES - 1:SUBLANES, :], xi.shape))

        cr, ci = lax.fori_loop(0, ng, grp, (car_ref[0], car_ref[1]))
        car_ref[0] = cr
        car_ref[1] = ci
        y_ref[...] = _dot(sre_ref[...], cre_ref[0]) - _dot(sim_ref[...], cim_ref[0]) + d_ref[...] * u

    return pl.pallas_call(
        body, name="s5_scan_fwd", grid=(nb, nt),
        in_specs=[
            pl.BlockSpec((tb, S5_COLS), lambda j, t: (t, j)),
            pl.BlockSpec((1, S5_COLS, S5_LANES), lambda j, t: (j, 0, 0)),
            pl.BlockSpec((1, S5_COLS, S5_LANES), lambda j, t: (j, 0, 0)),
            pl.BlockSpec((1, S5_LANES, S5_COLS), lambda j, t: (j, 0, 0)),
            pl.BlockSpec((1, S5_LANES, S5_COLS), lambda j, t: (j, 0, 0)),
            pl.BlockSpec((1, S5_COLS), lambda j, t: (0, j)),
            pl.BlockSpec((1, 16, SUBLANES, S5_LANES), lambda j, t: (j, 0, 0, 0)),
        ],
        out_specs=[
            pl.BlockSpec((tb, S5_COLS), lambda j, t: (t, j)),
            pl.BlockSpec((tb, S5_LANES), lambda j, t: (t, j)),
            pl.BlockSpec((tb, S5_LANES), lambda j, t: (t, j)),
        ],
        out_shape=[jax.ShapeDtypeStruct((L, DS), F32),
                   jax.ShapeDtypeStruct((L, nb * S5_LANES), F32),
                   jax.ShapeDtypeStruct((L, nb * S5_LANES), F32)],
        scratch_shapes=[pltpu.VMEM((2, SUBLANES, S5_LANES), F32)],
        compiler_params=pltpu.CompilerParams(dimension_semantics=("parallel", "arbitrary")),
    )(proj_main, bbd_re, bbd_im, cbd_re, cbd_im, dvec, tab)


def _s5_scan_bwd(dy, proj_main, s_re, s_im, bbd_re, bbd_im, cbd_re, cbd_im, dvec, tab, d_s5, DS):
    L = proj_main.shape[0]
    nb = DS // S5_COLS
    tb = _blk(L, 512, SUBLANES)
    nt = L // tb
    ng = tb // SUBLANES
    tb8 = tb // SUBLANES

    def body(dy_ref, u_ref, sre_ref, sim_ref, pre_ref, pim_ref, bre_ref, bim_ref, cre_ref, cim_ref, d_ref, tab_ref, _ds5_ref,
             du_ref, gd_ref, gcre_ref, gcim_ref, gbre_ref, gbim_ref, gare_ref, gaim_ref,
             lre_ref, lim_ref, car_ref):
        t = pl.program_id(1)

        @pl.when(t == 0)
        def _():
            car_ref[...] = jnp.zeros_like(car_ref)
            gd_ref[...] = jnp.zeros_like(gd_ref)
            gcre_ref[...] = jnp.zeros_like(gcre_ref)
            gcim_ref[...] = jnp.zeros_like(gcim_ref)
            gbre_ref[...] = jnp.zeros_like(gbre_ref)
            gbim_ref[...] = jnp.zeros_like(gbim_ref)
            gare_ref[...] = jnp.zeros_like(gare_ref)
            gaim_ref[...] = jnp.zeros_like(gaim_ref)

        dyv = dy_ref[...]
        u = u_ref[...]
        gd_ref[...] += jnp.sum(dyv * u, axis=0, keepdims=True)
        lre_ref[...] = _dot(dyv, cre_ref[0], NT)
        lim_ref[...] = -_dot(dyv, cim_ref[0], NT)
        gcre_ref[0] += _dot(sre_ref[...], dyv, TN)
        gcim_ref[0] -= _dot(sim_ref[...], dyv, TN)

        first = (t == nt - 1).astype(F32)
        head_re = pre_ref[...] * (1.0 - first)
        head_im = pim_ref[...] * (1.0 - first)
        row0 = lax.broadcasted_iota(jnp.int32, (SUBLANES, S5_LANES), 0) == 0

        def grp(i, carry):
            cr, ci, acc_re, acc_im = carry
            r = ng - 1 - i
            off = pl.multiple_of(r * SUBLANES, SUBLANES)
            xr, xi = _scan8(lre_ref[pl.ds(off, SUBLANES), :], lim_ref[pl.ds(off, SUBLANES), :], tab_ref, 8, (7, 6, 4))
            pr, pi = tab_ref[0, 14], tab_ref[0, 15]
            xr, xi = xr + pr * cr - pi * ci, xi + pr * ci + pi * cr
            lre_ref[pl.ds(off, SUBLANES), :] = xr
            lim_ref[pl.ds(off, SUBLANES), :] = xi
            poff = pl.multiple_of(jnp.maximum(r - 1, 0) * SUBLANES, SUBLANES)
            prev_re = jnp.where(r == 0, head_re, sre_ref[pl.ds(poff, SUBLANES), :])
            prev_im = jnp.where(r == 0, head_im, sim_ref[pl.ds(poff, SUBLANES), :])
            prev_re = jnp.broadcast_to(prev_re[SUBLANES - 1:SUBLANES, :], xr.shape)
            prev_im = jnp.broadcast_to(prev_im[SUBLANES - 1:SUBLANES, :], xi.shape)
            sp_re = jnp.where(row0, prev_re, pltpu.roll(sre_ref[pl.ds(off, SUBLANES), :], 1, 0))
            sp_im = jnp.where(row0, prev_im, pltpu.roll(sim_ref[pl.ds(off, SUBLANES), :], 1, 0))
            acc_re = acc_re + sp_re * xr + sp_im * xi
            acc_im = acc_im + sp_re * xi - sp_im * xr
            return (jnp.broadcast_to(xr[0:1, :], xr.shape), jnp.broadcast_to(xi[0:1, :], xi.shape), acc_re, acc_im)

        zero = jnp.zeros((SUBLANES, S5_LANES), F32)
        cr, ci, acc_re, acc_im = lax.fori_loop(0, ng, grp, (car_ref[0], car_ref[1], zero, zero))
        car_ref[0] = cr
        car_ref[1] = ci
        gare_ref[...] += jnp.sum(acc_re, axis=0, keepdims=True)
        gaim_ref[...] += jnp.sum(acc_im, axis=0, keepdims=True)
        lre = lre_ref[...]
        lim = lim_ref[...]
        du = dyv * d_ref[...] + _dot(lre, bre_ref[0], NT) + _dot(lim, bim_ref[0], NT)
        du_ref[...] = du.astype(BF16)
        gbre_ref[0] += _dot(u, lre, TN)
        gbim_ref[0] += _dot(u, lim, TN)

    rt = lambda t: nt - 1 - t
    col = pl.BlockSpec((tb, S5_COLS), lambda j, t: (rt(t), j))
    st = pl.BlockSpec((tb, S5_LANES), lambda j, t: (rt(t), j))
    prev = pl.BlockSpec((SUBLANES, S5_LANES), lambda j, t: (jnp.maximum(rt(t) * tb8 - 1, 0), j))
    bmat = pl.BlockSpec((1, S5_COLS, S5_LANES), lambda j, t: (j, 0, 0))
    cmat = pl.BlockSpec((1, S5_LANES, S5_COLS), lambda j, t: (j, 0, 0))
    return pl.pallas_call(
        body, name="s5_scan_bwd", grid=(nb, nt),
        in_specs=[col, col, st, st, prev, prev, bmat, bmat, cmat, cmat,
                  pl.BlockSpec((1, S5_COLS), lambda j, t: (0, j)),
                  pl.BlockSpec((1, 16, SUBLANES, S5_LANES), lambda j, t: (j, 0, 0, 0)),
                  pl.BlockSpec(memory_space=pl.ANY)],
        out_specs=[col, pl.BlockSpec((1, S5_COLS), lambda j, t: (0, j)), cmat, cmat, bmat, bmat,
                   pl.BlockSpec((1, S5_LANES), lambda j, t: (0, j)), pl.BlockSpec((1, S5_LANES), lambda j, t: (0, j))],
        input_output_aliases={12: 0},
        out_shape=[jax.ShapeDtypeStruct((L, 2 * DS), BF16), jax.ShapeDtypeStruct((1, DS), F32),
                   jax.ShapeDtypeStruct((nb, S5_LANES, S5_COLS), F32), jax.ShapeDtypeStruct((nb, S5_LANES, S5_COLS), F32),
                   jax.ShapeDtypeStruct((nb, S5_COLS, S5_LANES), F32), jax.ShapeDtypeStruct((nb, S5_COLS, S5_LANES), F32),
                   jax.ShapeDtypeStruct((1, nb * S5_LANES), F32), jax.ShapeDtypeStruct((1, nb * S5_LANES), F32)],
        scratch_shapes=[pltpu.VMEM((tb, S5_LANES), F32), pltpu.VMEM((tb, S5_LANES), F32),
                        pltpu.VMEM((2, SUBLANES, S5_LANES), F32)],
        compiler_params=pltpu.CompilerParams(dimension_semantics=("parallel", "arbitrary")),
    )(dy, proj_main, s_re, s_im, s_re, s_im, bbd_re, bbd_im, cbd_re, cbd_im, dvec, tab, d_s5)


def _s5_post_fwd(y_pre, proj_main, glu_w, glu_b, DS):
    L = y_pre.shape[0]
    tr = _blk(L, 256, SUBLANES)

    def body(y_ref, z_ref, w_ref, b_ref, o_ref, t_ref):
        y1 = _gelu(y_ref[...])
        t = _dot(y1, w_ref[...]) + b_ref[...]
        t_ref[...] = t
        z = z_ref[...]
        o_ref[...] = (y1 * _sigmoid(t) * (z * _sigmoid(z))).astype(BF16)

    row = pl.BlockSpec((tr, DS), lambda i: (i, 0))
    return pl.pallas_call(
        body, name="s5_post_fwd", grid=(L // tr,),
        in_specs=[row, pl.BlockSpec((tr, DS), lambda i: (i, 1)), pl.BlockSpec((DS, DS), lambda i: (0, 0)),
                  pl.BlockSpec((1, DS), lambda i: (0, 0))],
        out_specs=[row, row],
        out_shape=[jax.ShapeDtypeStruct((L, 2 * DS), BF16), jax.ShapeDtypeStruct((L, DS), F32)],
        compiler_params=pltpu.CompilerParams(dimension_semantics=("parallel",)),
    )(y_pre, proj_main, glu_w, glu_b)


def _s5_post_bwd(d_ycat, y_pre, proj_main, t_pre, glu_w, DS):
    L = y_pre.shape[0]
    tr = _blk(L, 256, SUBLANES)

    def body(dy_ref, y_ref, z_ref, t_ref, w_ref, dyp_ref, dz_ref, dt_ref, y1_ref, gb_ref):
        i = pl.program_id(0)

        @pl.when(i == 0)
        def _():
            gb_ref[...] = jnp.zeros_like(gb_ref)

        dy = dy_ref[...]
        yp = y_ref[...]
        z = z_ref[...]
        y1 = _gelu(yp)
        sg = _sigmoid(t_ref[...])
        sz = _sigmoid(z)
        c = y1 * sg
        d_c = dy * (z * sz)
        dz_ref[...] = (dy * c * (sz * (1.0 + z * (1.0 - sz)))).astype(BF16)
        d_t = d_c * y1 * sg * (1.0 - sg)
        gb_ref[...] += jnp.sum(d_t, axis=0, keepdims=True)
        dt_ref[...] = d_t.astype(BF16)
        y1_ref[...] = y1.astype(BF16)
        d_y1 = d_c * sg + _dot(d_t, w_ref[...], NT)
        dyp_ref[...] = d_y1 * _gelu_grad(yp)

    row = pl.BlockSpec((tr, DS), lambda i: (i, 0))
    return pl.pallas_call(
        body, name="s5_post_bwd", grid=(L // tr,),
        in_specs=[row, row, pl.BlockSpec((tr, DS), lambda i: (i, 1)), row, pl.BlockSpec((DS, DS), lambda i: (0, 0))],
        out_specs=[row, pl.BlockSpec((tr, DS), lambda i: (i, 1)), row, row, pl.BlockSpec((1, DS), lambda i: (0, 0))],
        out_shape=[jax.ShapeDtypeStruct((L, DS), F32), jax.ShapeDtypeStruct((L, 2 * DS), BF16),
                   jax.ShapeDtypeStruct((L, DS), BF16), jax.ShapeDtypeStruct((L, DS), BF16),
                   jax.ShapeDtypeStruct((1, DS), F32)],
        compiler_params=pltpu.CompilerParams(dimension_semantics=("arbitrary",)),
    )(d_ycat, y_pre, proj_main, t_pre, glu_w)


def _gla_gates(glow, gu_ref, gb_ref):
    a = _dot(glow, gu_ref[...]) + gb_ref[...]
    lg = (jnp.minimum(a, 0.0) - jnp.log(1.0 + jnp.exp(-jnp.abs(a)))) * (1.0 / GLA_TAU)
    ri = lax.broadcasted_iota(jnp.int32, (GLA_CHUNK, GLA_CHUNK), 0)
    ci = lax.broadcasted_iota(jnp.int32, (GLA_CHUNK, GLA_CHUNK), 1)
    b = _dot_hi((ri >= ci).astype(F32), lg)
    b_last = jnp.sum(lg, axis=0, keepdims=True)
    return a, b, b_last, ri >= ci


def _gla_specs(DS, DK, DV, cmap):
    c = GLA_CHUNK
    return [
        pl.BlockSpec((c, DK), lambda n: (cmap(n), 2 * DS // DK)),
        pl.BlockSpec((c, DK), lambda n: (cmap(n), 2 * DS // DK + 1)),
        pl.BlockSpec((c, DV), lambda n: (cmap(n), (2 * DS + 2 * DK) // DV)),
        pl.BlockSpec((c, DV), lambda n: (cmap(n), (2 * DS + 2 * DK) // DV + 1)),
    ]


def _gla_fwd(proj_main, proj_low, gate_up_pad, gate_bias, norm_w, ycat, DS, DK, DV):
    L = proj_main.shape[0]
    nc = L // GLA_CHUNK
    nh = DK // GLA_HK
    scale = GLA_HK ** -0.5

    def body(q_ref, k_ref, v_ref, z_ref, gl_ref, gu_ref, gb_ref, nw_ref, _yc_ref, y_ref, sp_ref, st_ref):
        n = pl.program_id(0)

        @pl.when(n == 0)
        def _():
            st_ref[...] = jnp.zeros_like(st_ref)

        _, b, b_last, mask = _gla_gates(gl_ref[...], gu_ref, gb_ref)
        for h in range(nh):
            ks = slice(h * GLA_HK, (h + 1) * GLA_HK)
            vs = slice(h * GLA_HV, (h + 1) * GLA_HV)
            bh, bl = b[:, ks], b_last[:, ks]
            qe = (q_ref[:, ks] * scale) * jnp.exp(bh)
            kh = k_ref[:, ks]
            ke = kh * jnp.exp(-bh)
            ktail = kh * jnp.exp(bl - bh)
            vh = v_ref[:, vs]
            st = st_ref[h]
            sp_ref[0, h] = st
            attn = jnp.where(mask, _dot(qe, ke, NT), 0.0)
            o = _dot(attn, vh) + _dot(qe, st, NT)
            st_ref[h] = jnp.exp(bl) * st + _dot(vh, ktail, TN)
            r = lax.rsqrt(jnp.mean(o * o, axis=-1, keepdims=True) + EPS)
            z = z_ref[:, vs]
            y_ref[:, vs] = (o * r * nw_ref[...] * (z * _sigmoid(z))).astype(BF16)

    c = GLA_CHUNK
    return pl.pallas_call(
        body, name="gla_fwd", grid=(nc,),
        in_specs=_gla_specs(DS, DK, DV, lambda n: n) + [
            pl.BlockSpec((c, LANES), lambda n: (n, 0)),
            pl.BlockSpec((LANES, DK), lambda n: (0, 0)),
            pl.BlockSpec((1, DK), lambda n: (0, 0)),
            pl.BlockSpec((1, GLA_HV), lambda n: (0, 0)),
            pl.BlockSpec(memory_space=pl.ANY),
        ],
        out_specs=[pl.BlockSpec((c, DV), lambda n: (n, DS // DV)),
                   pl.BlockSpec((1, nh, GLA_HV, GLA_HK), lambda n: (n, 0, 0, 0))],
        input_output_aliases={8: 0},
        out_shape=[jax.ShapeDtypeStruct(ycat.shape, BF16), jax.ShapeDtypeStruct((nc, nh, GLA_HV, GLA_HK), F32)],
        scratch_shapes=[pltpu.VMEM((nh, GLA_HV, GLA_HK), F32)],
        compiler_params=pltpu.CompilerParams(dimension_semantics=("arbitrary",)),
    )(proj_main, proj_main, proj_main, proj_main, proj_low, gate_up_pad, gate_bias, norm_w, ycat)


def _gla_bwd(d_ycat, proj_main, proj_low, s_prev, gate_up_pad, gate_bias, norm_w, DS, DK, DV):
    L = proj_main.shape[0]
    nc = L // GLA_CHUNK
    nh = DK // GLA_HK
    scale = GLA_HK ** -0.5

    def body(dy_ref, q_ref, k_ref, v_ref, z_ref, gl_ref, sp_ref, gu_ref, gb_ref, nw_ref,
             dg_ref, da_ref, gnw_ref, ggb_ref, dst_ref):
        n = pl.program_id(0)

        @pl.when(n == 0)
        def _():
            dst_ref[...] = jnp.zeros_like(dst_ref)
            gnw_ref[...] = jnp.zeros_like(gnw_ref)
            ggb_ref[...] = jnp.zeros_like(ggb_ref)

        a, b, b_last, mask = _gla_gates(gl_ref[...], gu_ref, gb_ref)
        last_row = lax.broadcasted_iota(jnp.int32, (GLA_CHUNK, GLA_HK), 0) == GLA_CHUNK - 1
        ri = lax.broadcasted_iota(jnp.int32, (GLA_CHUNK, GLA_CHUNK), 0)
        ci = lax.broadcasted_iota(jnp.int32, (GLA_CHUNK, GLA_CHUNK), 1)
        upper = (ci >= ri).astype(F32)
        nw = nw_ref[...]
        for h in range(nh):
            ks = slice(h * GLA_HK, (h + 1) * GLA_HK)
            vs = slice(h * GLA_HV, (h + 1) * GLA_HV)
            bh, bl = b[:, ks], b_last[:, ks]
            e = jnp.exp(bh)
            einv = jnp.exp(-bh)
            etail = jnp.exp(bl - bh)
            dec = jnp.exp(bl)
            qe = (q_ref[:, ks] * scale) * e
            kh = k_ref[:, ks]
            ke = kh * einv
            ktail = kh * etail
            vh = v_ref[:, vs]
            st = sp_ref[0, h]
            dst = dst_ref[h]
            attn = jnp.where(mask, _dot(qe, ke, NT), 0.0)
            o = _dot(attn, vh) + _dot(qe, st, NT)
            r = lax.rsqrt(jnp.mean(o * o, axis=-1, keepdims=True) + EPS)
            nrm = o * r
            z = z_ref[:, vs]
            sz = _sigmoid(z)
            dy = dy_ref[:, vs]
            dg_ref[:, 2 * DK + DV + h * GLA_HV:2 * DK + DV + (h + 1) * GLA_HV] = (
                dy * nrm * nw * (sz * (1.0 + z * (1.0 - sz)))).astype(BF16)
            d_on = dy * (z * sz)
            gnw_ref[...] += jnp.sum(d_on * nrm, axis=0, keepdims=True)
            d_n = d_on * nw
            d_o = r * (d_n - nrm * jnp.mean(d_n * nrm, axis=-1, keepdims=True))
            d_attn = jnp.where(mask, _dot(d_o, vh, NT), 0.0)
            dg_ref[:, 2 * DK + h * GLA_HV:2 * DK + (h + 1) * GLA_HV] = (
                _dot(attn, d_o, TN) + _dot(ktail, dst, NT)).astype(BF16)
            d_qe = _dot(d_attn, ke) + _dot(d_o, st)
            d_ke = _dot(d_attn, qe, TN)
            d_kt = _dot(vh, dst)
            d_dec = jnp.sum(dst * st, axis=0, keepdims=True)
            dst_ref[h] = dec * dst + _dot(d_o, qe, TN)
            dg_ref[:, ks] = (d_qe * scale * e).astype(BF16)
            dg_ref[:, DK + h * GLA_HK:DK + (h + 1) * GLA_HK] = (d_ke * einv + d_kt * etail).astype(BF16)
            d_bl = jnp.sum(d_kt * ktail, axis=0, keepdims=True) + d_dec * dec
            d_b = d_qe * qe - d_ke * ke - d_kt * ktail + jnp.where(last_row, d_bl, 0.0)
            d_lg = _dot_hi(upper, d_b)
            d_a = d_lg * (1.0 / GLA_TAU) * _sigmoid(-a[:, ks])
            ggb_ref[:, ks] += jnp.sum(d_a, axis=0, keepdims=True)
            da_ref[:, ks] = d_a.astype(BF16)

    c = GLA_CHUNK
    rn = lambda n: nc - 1 - n
    return pl.pallas_call(
        body, name="gla_bwd", grid=(nc,),
        in_specs=[pl.BlockSpec((c, DV), lambda n: (rn(n), DS // DV))] + _gla_specs(DS, DK, DV, rn) + [
            pl.BlockSpec((c, LANES), lambda n: (rn(n), 0)),
            pl.BlockSpec((1, nh, GLA_HV, GLA_HK), lambda n: (rn(n), 0, 0, 0)),
            pl.BlockSpec((LANES, DK), lambda n: (0, 0)),
            pl.BlockSpec((1, DK), lambda n: (0, 0)),
            pl.BlockSpec((1, GLA_HV), lambda n: (0, 0)),
        ],
        out_specs=[pl.BlockSpec((c, 2 * DK + 2 * DV), lambda n: (rn(n), 0)),
                   pl.BlockSpec((c, DK), lambda n: (rn(n), 0)),
                   pl.BlockSpec((1, GLA_HV), lambda n: (0, 0)), pl.BlockSpec((1, DK), lambda n: (0, 0))],
        out_shape=[jax.ShapeDtypeStruct((L, 2 * DK + 2 * DV), BF16),
                   jax.ShapeDtypeStruct((L, DK), BF16),
                   jax.ShapeDtypeStruct((1, GLA_HV), F32), jax.ShapeDtypeStruct((1, DK), F32)],
        scratch_shapes=[pltpu.VMEM((nh, GLA_HV, GLA_HK), F32)],
        compiler_params=pltpu.CompilerParams(dimension_semantics=("arbitrary",)),
    )(d_ycat, proj_main, proj_main, proj_main, proj_main, proj_low, s_prev, gate_up_pad, gate_bias, norm_w)


def _adamw(w, g, m, v, name):
    R, C = w.shape
    tr = _blk(R, 256, SUBLANES)
    c1 = 1.0 - ADAM_B1 ** ADAM_STEP
    c2 = 1.0 - ADAM_B2 ** ADAM_STEP

    def body(w_ref, g_ref, m_ref, v_ref, d_ref, nm_ref, nv_ref):
        g_ = g_ref[...]
        m_ = ADAM_B1 * m_ref[...] + (1.0 - ADAM_B1) * g_
        v_ = ADAM_B2 * v_ref[...] + (1.0 - ADAM_B2) * (g_ * g_)
        nm_ref[...] = m_
        nv_ref[...] = v_
        d_ref[...] = -ADAM_LR * ((m_ / c1) / (jnp.sqrt(v_ / c2) + ADAM_EPS) + ADAM_WD * w_ref[...])

    blk = pl.BlockSpec((tr, C), lambda i: (i, 0))
    sd = jax.ShapeDtypeStruct((R, C), F32)
    return pl.pallas_call(
        body, name=name, grid=(R // tr,), in_specs=[blk] * 4, out_specs=[blk] * 3, out_shape=[sd] * 3,
        compiler_params=pltpu.CompilerParams(dimension_semantics=("parallel",)),
    )(w, g, m, v)


def _my_pos():
    return lax.axis_index("x"), lax.axis_index("y"), lax.axis_index("c")


def _gather_weights(shards):
    n = len(shards)
    halves = [s.shape[0] // 2 for s in shards]

    def body(*refs):
        ins, outs = refs[:n], refs[n:2 * n]
        send_sems, recv_sems = refs[2 * n:]
        x, y, c = _my_pos()
        me = 2 * x + y

        def piece(a, chip, half):
            return outs[a].at[chip, pl.ds(half * halves[a], halves[a]), :]

        def copy(a, k, src_chip, half, to):
            sl = piece(a, src_chip, half)
            return pltpu.make_async_remote_copy(src_ref=sl, dst_ref=sl, send_sem=send_sems.at[a, k], recv_sem=recv_sems.at[a, k],
                                                device_id=to, device_id_type=MESH)

        def first(a, d, to):
            src = ins[a].at[pl.ds(c * halves[a], halves[a]), :]
            return pltpu.make_async_remote_copy(src_ref=src, dst_ref=piece(a, me, c), send_sem=send_sems.at[a, d - 1],
                                                recv_sem=recv_sems.at[a, d - 1], device_id=to, device_id_type=MESH)

        sent = []
        for d in (1, 2, 3):
            to = (x ^ (d >> 1), y ^ (d & 1), c)
            for a in range(n):
                cp = first(a, d, to)
                cp.start()
                sent.append(cp)
        for d in (1, 2, 3):
            chip = (x ^ (d >> 1)) * 2 + (y ^ (d & 1))
            for a in range(n):
                copy(a, d - 1, chip, c, (x, y, c)).wait_recv()
                fw = copy(a, 2 + d, chip, c, (x, y, 1 - c))
                fw.start()
                sent.append(fw)
        for d in (1, 2, 3):
            chip = (x ^ (d >> 1)) * 2 + (y ^ (d & 1))
            for a in range(n):
                copy(a, 2 + d, chip, 1 - c, (x, y, c)).wait_recv()
        for cp in sent:
            cp.wait_send()

    hbm = pl.BlockSpec(memory_space=pltpu.HBM)
    return pl.pallas_call(
        body, name="gather_weights",
        in_specs=[hbm] * n, out_specs=[hbm] * n,
        out_shape=[jax.ShapeDtypeStruct((4,) + s.shape, s.dtype) for s in shards],
        scratch_shapes=[pltpu.SemaphoreType.DMA((n, 6)), pltpu.SemaphoreType.DMA((n, 6))],
    )(*shards)


def _pair_exchange(gs):
    n = len(gs)

    def body(*refs):
        ins, outs = refs[:n], refs[n:2 * n]
        send_sems, recv_sems = refs[2 * n:]
        x, y, c = _my_pos()
        sent = []
        for a in range(n):
            hrows = gs[a].shape[1] // 2
            cp = pltpu.make_async_remote_copy(
                src_ref=ins[a].at[:, pl.ds((1 - c) * hrows, hrows), :], dst_ref=outs[a], send_sem=send_sems.at[a],
                recv_sem=recv_sems.at[a], device_id=(x, y, 1 - c), device_id_type=MESH)
            cp.start()
            sent.append(cp)
        for cp in sent:
            cp.wait()

    hbm = pl.BlockSpec(memory_space=pltpu.HBM)
    return pl.pallas_call(
        body, name="grad_pair_exchange", in_specs=[hbm] * n, out_specs=[hbm] * n,
        out_shape=[jax.ShapeDtypeStruct((g.shape[0], g.shape[1] // 2, g.shape[2]), g.dtype) for g in gs],
        scratch_shapes=[pltpu.SemaphoreType.DMA((n,)), pltpu.SemaphoreType.DMA((n,))],
    )(*gs)


def _pair_add(g, got, c_arr, name):
    nk, rows2, cols = g.shape
    hrows = rows2 // 2
    tr = _blk(hrows, 256, 2 * SUBLANES)
    nb = hrows // tr

    def body(c_ref, a_ref, b_ref, o_ref):
        o_ref[...] = (a_ref[...].astype(F32) + b_ref[...].astype(F32)).astype(o_ref.dtype)

    return pl.pallas_call(
        body, name=name,
        grid_spec=pltpu.PrefetchScalarGridSpec(
            num_scalar_prefetch=1, grid=(nk, nb),
            in_specs=[pl.BlockSpec((1, tr, cols), lambda k, i, c_ref: (k, c_ref[0] * nb + i, 0)),
                      pl.BlockSpec((1, tr, cols), lambda k, i, c_ref: (k, i, 0))],
            out_specs=pl.BlockSpec((1, tr, cols), lambda k, i, c_ref: (k, i, 0))),
        out_shape=jax.ShapeDtypeStruct((nk, hrows, cols), g.dtype),
        compiler_params=pltpu.CompilerParams(dimension_semantics=("parallel", "parallel")),
    )(c_arr, g, got)


def _chip_scatter(pss):
    n = len(pss)

    def body(*refs):
        ins, outs = refs[:n], refs[n:2 * n]
        send_sems, recv_sems = refs[2 * n:]
        x, y, c = _my_pos()
        sent = []
        for d in (1, 2, 3):
            tx, ty = x ^ (d >> 1), y ^ (d & 1)
            for a in range(n):
                cp = pltpu.make_async_remote_copy(
                    src_ref=ins[a].at[2 * tx + ty], dst_ref=outs[a].at[d - 1], send_sem=send_sems.at[a, d - 1],
                    recv_sem=recv_sems.at[a, d - 1], device_id=(tx, ty, c), device_id_type=MESH)
                cp.start()
                sent.append(cp)
        for cp in sent:
            cp.wait()

    hbm = pl.BlockSpec(memory_space=pltpu.HBM)
    return pl.pallas_call(
        body, name="grad_chip_scatter", in_specs=[hbm] * n, out_specs=[hbm] * n,
        out_shape=[jax.ShapeDtypeStruct((3,) + p.shape[1:], p.dtype) for p in pss],
        scratch_shapes=[pltpu.SemaphoreType.DMA((n, 3)), pltpu.SemaphoreType.DMA((n, 3))],
    )(*pss)


def _chip_sum(ps, got, me_arr, name):
    _, hrows, cols = ps.shape
    tr = _blk(hrows, 256, 2 * SUBLANES)

    def body(me_ref, p_ref, g_ref, o_ref):
        acc = p_ref[0].astype(F32)
        for s in range(3):
            acc = acc + g_ref[s].astype(F32)
        o_ref[...] = acc

    return pl.pallas_call(
        body, name=name,
        grid_spec=pltpu.PrefetchScalarGridSpec(
            num_scalar_prefetch=1, grid=(hrows // tr,),
            in_specs=[pl.BlockSpec((1, tr, cols), lambda i, me_ref: (me_ref[0], i, 0)),
                      pl.BlockSpec((3, tr, cols), lambda i, me_ref: (0, i, 0))],
            out_specs=pl.BlockSpec((tr, cols), lambda i, me_ref: (i, 0))),
        out_shape=jax.ShapeDtypeStruct((hrows, cols), F32),
        compiler_params=pltpu.CompilerParams(dimension_semantics=("parallel",)),
    )(me_arr, ps, got)


def _pair_swap(halves):
    n = len(halves)

    def body(*refs):
        ins, outs = refs[:n], refs[n:2 * n]
        send_sems, recv_sems = refs[2 * n:]
        x, y, c = _my_pos()
        sent = []
        for a in range(n):
            cp = pltpu.make_async_remote_copy(src_ref=ins[a], dst_ref=outs[a], send_sem=send_sems.at[a], recv_sem=recv_sems.at[a],
                                              device_id=(x, y, 1 - c), device_id_type=MESH)
            cp.start()
            sent.append(cp)
        for cp in sent:
            cp.wait()

    hbm = pl.BlockSpec(memory_space=pltpu.HBM)
    return pl.pallas_call(
        body, name="grad_pair_swap", in_specs=[hbm] * n, out_specs=[hbm] * n,
        out_shape=[jax.ShapeDtypeStruct(h.shape, h.dtype) for h in halves],
        scratch_shapes=[pltpu.SemaphoreType.DMA((n,)), pltpu.SemaphoreType.DMA((n,))],
    )(*halves)


def _adamw_sharded(w, g_own, g_other, m, v, c_arr, name):
    R, C = w.shape
    hrows = R // 2
    tr = _blk(hrows, 256, SUBLANES)
    nbh = hrows // tr
    c1 = 1.0 - ADAM_B1 ** ADAM_STEP
    c2 = 1.0 - ADAM_B2 ** ADAM_STEP

    def body(c_ref, w_ref, go_ref, gx_ref, m_ref, v_ref, g_ref, d_ref, nm_ref, nv_ref):
        mine = (pl.program_id(0) // nbh) == c_ref[0]
        g_ = jnp.where(mine, go_ref[...], gx_ref[...])
        g_ref[...] = g_
        m_ = ADAM_B1 * m_ref[...] + (1.0 - ADAM_B1) * g_
        v_ = ADAM_B2 * v_ref[...] + (1.0 - ADAM_B2) * (g_ * g_)
        nm_ref[...] = m_
        nv_ref[...] = v_
        d_ref[...] = -ADAM_LR * ((m_ / c1) / (jnp.sqrt(v_ / c2) + ADAM_EPS) + ADAM_WD * w_ref[...])

    blk = pl.BlockSpec((tr, C), lambda i, c_ref: (i, 0))
    hblk = pl.BlockSpec((tr, C), lambda i, c_ref: (i % nbh, 0))
    sd = jax.ShapeDtypeStruct((R, C), F32)
    return pl.pallas_call(
        body, name=name,
        grid_spec=pltpu.PrefetchScalarGridSpec(
            num_scalar_prefetch=1, grid=(2 * nbh,),
            in_specs=[blk, hblk, hblk, blk, blk], out_specs=[blk] * 4),
        out_shape=[sd] * 4,
        compiler_params=pltpu.CompilerParams(dimension_semantics=("parallel",)),
    )(c_arr, w, g_own, g_other, m, v)


def _allreduce_small(vec):
    rows, lanes = vec.shape
    r8 = rows // 8

    def body(v_ref, o_ref, got_ref, send_sems, recv_sems):
        x, y, c = _my_pos()
        me = 4 * x + 2 * y + c

        def peer(d):
            return (x ^ (d >> 2), y ^ ((d >> 1) & 1), c ^ (d & 1))

        def lin(p):
            return 4 * p[0] + 2 * p[1] + p[2]

        sent = []
        for d in range(1, 8):
            to = peer(d)
            cp = pltpu.make_async_remote_copy(
                src_ref=v_ref.at[pl.ds(pl.multiple_of(lin(to) * r8, SUBLANES), r8), :], dst_ref=got_ref.at[d],
                send_sem=send_sems.at[0, d], recv_sem=recv_sems.at[0, d], device_id=to, device_id_type=MESH)
            cp.start()
            sent.append(cp)
        mine = pl.ds(pl.multiple_of(me * r8, SUBLANES), r8)
        acc = v_ref[mine, :]
        for d in range(1, 8):
            sent[d - 1].wait_recv()
            acc = acc + got_ref[d]
        got_ref[0] = acc
        o_ref[mine, :] = acc
        for d in range(1, 8):
            cp = pltpu.make_async_remote_copy(
                src_ref=got_ref.at[0], dst_ref=o_ref.at[mine, :],
                send_sem=send_sems.at[1, d], recv_sem=recv_sems.at[1, d], device_id=peer(d), device_id_type=MESH)
            cp.start()
            sent.append(cp)
        for d in range(1, 8):
            src = pl.ds(pl.multiple_of(lin(peer(d)) * r8, SUBLANES), r8)
            pltpu.make_async_remote_copy(
                src_ref=got_ref.at[0], dst_ref=o_ref.at[src, :],
                send_sem=send_sems.at[1, d], recv_sem=recv_sems.at[1, d], device_id=peer(d), device_id_type=MESH).wait_recv()
        for cp in sent:
            cp.wait_send()

    vm = pl.BlockSpec(memory_space=pltpu.VMEM)
    return pl.pallas_call(
        body, name="allreduce_small", in_specs=[vm], out_specs=vm,
        out_shape=jax.ShapeDtypeStruct((rows, lanes), F32),
        scratch_shapes=[pltpu.VMEM((8, r8, lanes), F32), pltpu.SemaphoreType.DMA((2, 8)), pltpu.SemaphoreType.DMA((2, 8))],
    )(vec)


def _pack_rows(arrs, row_mult):
    flat = jnp.concatenate([a.reshape(-1) for a in arrs])
    n = flat.shape[0]
    rows = -(-n // (LANES * row_mult)) * row_mult
    flat = jnp.pad(flat, (0, rows * LANES - n))
    return flat.reshape(rows, LANES)


def _unpack_rows(packed, shapes):
    flat = packed.reshape(-1)
    out, off = [], 0
    for s in shapes:
        n = math.prod(s)
        out.append(flat[off:off + n].reshape(s))
        off += n
    return out


def kernel(x, pre_norm_w, w_in, s5_A_re, s5_A_im, s5_B_re, s5_B_im, s5_C_re, s5_C_im, s5_D, s5_log_dt, s5_glu_w, s5_glu_b, gla_gate_up, gla_gate_bias, gla_norm_w, w_out, post_norm_w, loss_target, m_pre_norm_w, m_w_in, m_s5_A_re, m_s5_A_im, m_s5_B_re, m_s5_B_im, m_s5_C_re, m_s5_C_im, m_s5_D, m_s5_log_dt, m_s5_glu_w, m_s5_glu_b, m_gla_gate_up, m_gla_gate_bias, m_gla_norm_w, m_w_out, m_post_norm_w, v_pre_norm_w, v_w_in, v_s5_A_re, v_s5_A_im, v_s5_B_re, v_s5_B_im, v_s5_C_re, v_s5_C_im, v_s5_D, v_s5_log_dt, v_s5_glu_w, v_s5_glu_b, v_gla_gate_up, v_gla_gate_bias, v_gla_norm_w, v_w_out, v_post_norm_w):
    names = ["pre_norm_w", "w_in", "s5_A_re", "s5_A_im", "s5_B_re", "s5_B_im", "s5_C_re", "s5_C_im", "s5_D", "s5_log_dt",
             "s5_glu_w", "s5_glu_b", "gla_gate_up", "gla_gate_bias", "gla_norm_w", "w_out", "post_norm_w"]
    W = dict(zip(names, (pre_norm_w, w_in, s5_A_re, s5_A_im, s5_B_re, s5_B_im, s5_C_re, s5_C_im, s5_D, s5_log_dt,
                         s5_glu_w, s5_glu_b, gla_gate_up, gla_gate_bias, gla_norm_w, w_out, post_norm_w)))
    M = dict(zip(names, (m_pre_norm_w, m_w_in, m_s5_A_re, m_s5_A_im, m_s5_B_re, m_s5_B_im, m_s5_C_re, m_s5_C_im, m_s5_D,
                         m_s5_log_dt, m_s5_glu_w, m_s5_glu_b, m_gla_gate_up, m_gla_gate_bias, m_gla_norm_w, m_w_out,
                         m_post_norm_w)))
    V = dict(zip(names, (v_pre_norm_w, v_w_in, v_s5_A_re, v_s5_A_im, v_s5_B_re, v_s5_B_im, v_s5_C_re, v_s5_C_im, v_s5_D,
                         v_s5_log_dt, v_s5_glu_w, v_s5_glu_b, v_gla_gate_up, v_gla_gate_bias, v_gla_norm_w, v_w_out,
                         v_post_norm_w)))
    sharded = ("w_in", "s5_glu_w", "w_out", "gla_gate_up")

    xb = x[0]
    tgt = loss_target[0]
    L, D = xb.shape
    DS = D // 2
    G = DS // S5_GROUP
    P = S5_STATE
    NB = DS // S5_COLS
    DV = D - DS
    DK = DV // 2
    WM = 2 * DS + 2 * DK + 2 * DV
    nsh = w_in.shape[2]

    chip = 2 * lax.axis_index("x") + lax.axis_index("y")
    own = [w_in[0].astype(BF16), s5_glu_w[0].astype(BF16), w_out[0].astype(BF16), gla_gate_up[0]]
    g_win, g_glu, g_wout, g_gup = [lax.dynamic_update_index_in_dim(g, o, chip, 0)
                                   for g, o in zip(_gather_weights(own), own)]
    w_full = jnp.moveaxis(g_win, 0, 1).reshape(D, 4 * nsh)
    w_main = w_full[:, :WM]
    w_low = jnp.pad(w_full[:, WM:], ((0, 0), (0, LANES - GLA_RANK)))
    glu_w = g_glu.reshape(DS, DS)
    wout = g_wout.reshape(D, D)
    gup = jnp.moveaxis(g_gup, 0, 1).reshape(GLA_RANK, DK)
    gup_pad = jnp.pad(gup, ((0, LANES - GLA_RANK), (0, 0))).astype(BF16)

    e16 = jnp.repeat(jnp.eye(P, dtype=F32), S5_GROUP, axis=1)
    a_re, a_im = s5_A_re[0], s5_A_im[0]
    log_dt = s5_log_dt[0].reshape(G, 1)
    b_re = s5_B_re[0].reshape(G, P * S5_GROUP)
    b_im = s5_B_im[0].reshape(G, P * S5_GROUP)
    bb_re, bb_im, pw_re, pw_im = _s5_prep_fwd(a_re, a_im, log_dt, b_re, b_im, e16)
    to_hp = lambda m: m.reshape(G, P, S5_GROUP).transpose(0, 2, 1)
    bbd_re = _block_diag(to_hp(bb_re), NB, S5_GROUP, P).astype(BF16)
    bbd_im = _block_diag(to_hp(bb_im), NB, S5_GROUP, P).astype(BF16)
    cbd_re = _block_diag(s5_C_re[0].transpose(0, 2, 1), NB, P, S5_GROUP).astype(BF16)
    cbd_im = _block_diag(s5_C_im[0].transpose(0, 2, 1), NB, P, S5_GROUP).astype(BF16)
    tab = _scan_tables(pw_re, pw_im, NB)
    dvec = s5_D

    h = _prenorm_fwd(xb, pre_norm_w)
    proj_main = _mm(h, w_main, name="in_proj")
    proj_low = _mm(h, w_low, name="in_proj_low")
    y_pre, s_re, s_im = _s5_scan_fwd(proj_main, bbd_re, bbd_im, cbd_re, cbd_im, dvec, tab, DS)
    ycat, t_pre = _s5_post_fwd(y_pre, proj_main, glu_w, s5_glu_b, DS)
    ycat, s_prev = _gla_fwd(proj_main, proj_low, gup_pad, gla_gate_bias, gla_norm_w, ycat, DS, DK, DV)
    mixed = _mm(ycat, wout, name="out_proj")
    loss11, d_mixed, dout, g_post_w = _post_fwd_bwd(mixed, xb, tgt, post_norm_w)

    d_ycat = _mm(d_mixed, wout, tb=True, name="out_proj_dx")
    g_wout_full = _mm(ycat, d_mixed, ta=True, out_dtype=BF16, name="out_proj_dw")
    d_ypre, d_s5, d_t, y1, g_glu_b = _s5_post_bwd(d_ycat, y_pre, proj_main, t_pre, glu_w, DS)
    g_glu_full = _mm(y1, d_t, ta=True, out_dtype=BF16, name="glu_dw")
    d_s5, g_D, gcbd_re, gcbd_im, gbbd_re, gbbd_im, gab_re, gab_im = _s5_scan_bwd(
        d_ypre, proj_main, s_re, s_im, bbd_re, bbd_im, cbd_re, cbd_im, dvec, tab, d_s5, DS)
    d_gla, d_a, g_norm_w, g_gate_bias = _gla_bwd(
        d_ycat, proj_main, proj_low, s_prev, gup_pad, gla_gate_bias, gla_norm_w, DS, DK, DV)
    d_low = _mm(d_a, gup_pad, tb=True, out_dtype=BF16, name="gate_dx")
    g_gup_pad = _mm(proj_low, d_a, ta=True, name="gate_dw")
    dh_main = _mm_ksplit(d_s5, d_gla, w_main, name="in_proj_dx")
    dh_low = _mm(d_low, w_low, tb=True, name="in_proj_low_dx")
    g_wmain = _mm_nsplit(h, d_s5, d_gla, out_dtype=BF16, name="in_proj_dw")
    g_wlow = _mm(h, d_low, ta=True, out_dtype=BF16, name="in_proj_low_dw")
    grad_x, g_pre_w = _prenorm_bwd(xb, dh_main, dh_low, dout, pre_norm_w)

    from_hp = lambda m: m.transpose(0, 2, 1).reshape(G, P * S5_GROUP)
    gbb_re = from_hp(_block_diag_extract(gbbd_re, NB, S5_GROUP, P))
    gbb_im = from_hp(_block_diag_extract(gbbd_im, NB, S5_GROUP, P))
    g_a_re, g_a_im, g_b_re, g_b_im, g_ldt = _s5_prep_bwd(
        a_re, a_im, log_dt, b_re, b_im, e16, gbb_re, gbb_im, gab_re.reshape(G, P), gab_im.reshape(G, P))
    g_c_re = _block_diag_extract(gcbd_re, NB, P, S5_GROUP).transpose(0, 2, 1)
    g_c_im = _block_diag_extract(gcbd_im, NB, P, S5_GROUP).transpose(0, 2, 1)

    loss = lax.psum(loss11[0, 0], ("x", "y", "c"))

    small = [n for n in names if n not in sharded]
    g_small = {"pre_norm_w": g_pre_w, "s5_A_re": g_a_re, "s5_A_im": g_a_im, "s5_B_re": g_b_re, "s5_B_im": g_b_im,
               "s5_C_re": g_c_re, "s5_C_im": g_c_im, "s5_D": g_D, "s5_log_dt": g_ldt, "s5_glu_b": g_glu_b,
               "gla_gate_bias": g_gate_bias, "gla_norm_w": g_norm_w, "post_norm_w": g_post_w}
    small_shapes = [W[n].shape for n in small]
    gsum = _allreduce_small(_pack_rows([g_small[n] for n in small], 8 * SUBLANES))
    pk = lambda d: _pack_rows([d[n] for n in small], 8 * SUBLANES)
    d_s, m_s, v_s = _adamw(pk(W), gsum, pk(M), pk(V), "adamw_small")
    G_out = dict(zip(small, _unpack_rows(gsum, small_shapes)))
    D_out = dict(zip(small, _unpack_rows(d_s, small_shapes)))
    M_out = dict(zip(small, _unpack_rows(m_s, small_shapes)))
    V_out = dict(zip(small, _unpack_rows(v_s, small_shapes)))

    g_win_full = jnp.concatenate([g_wmain, g_wlow[:, :GLA_RANK]], axis=1)
    gs = [jnp.moveaxis(g_win_full.reshape(D, 4, nsh), 1, 0),
          g_glu_full.reshape(4, DS // 4, DS),
          g_wout_full.reshape(4, D // 4, D),
          jnp.moveaxis(g_gup_pad[:GLA_RANK].reshape(GLA_RANK, 4, DK // 4), 1, 0)]
    c_arr = lax.axis_index("c").astype(jnp.int32).reshape(1)
    me_arr = chip.astype(jnp.int32).reshape(1)
    got = _pair_exchange(gs)
    pss = [_pair_add(g, r, c_arr, "grad_pair_add_" + n) for n, g, r in zip(sharded, gs, got)]
    rcv = _chip_scatter(pss)
    halves = [_chip_sum(p, r, me_arr, "grad_chip_sum_" + n) for n, p, r in zip(sharded, pss, rcv)]
    others = _pair_swap(halves)
    for n, g_own, g_other in zip(sharded, halves, others):
        g_, d_, m_, v_ = _adamw_sharded(W[n][0], g_own, g_other, M[n][0], V[n][0], c_arr, "adamw_" + n)
        G_out[n], D_out[n], M_out[n], V_out[n] = g_[None], d_[None], m_[None], v_[None]

    return (loss, grad_x[None], *[G_out[n] for n in names], *[D_out[n] for n in names],
            *[M_out[n] for n in names], *[V_out[n] for n in names])
```

```python
import functools
import math

import jax
import jax.numpy as jnp
from jax import lax
from jax.experimental import pallas as pl
from jax.experimental.pallas import tpu as pltpu

F32 = jnp.float32
BF16 = jnp.bfloat16
HI = lax.Precision.HIGHEST
MESH = pl.DeviceIdType.MESH

EPS = 1e-6
S5_GROUP = 16
S5_STATE = 64
GLA_HK = 128
GLA_HV = 256
GLA_RANK = 16
GLA_TAU = 16.0
GLA_CHUNK = 64
GLA_STEP_CHUNKS = 2
LANES = 128
SUBLANES = 8
S5_COLS = 128
S5_LANES = (S5_COLS // S5_GROUP) * S5_STATE

ADAM_LR = 0.001
ADAM_B1 = 0.9
ADAM_B2 = 0.999
ADAM_EPS = 1e-08
ADAM_WD = 0.01
ADAM_STEP = 10

GELU_K = math.sqrt(2.0 / math.pi)
GELU_C = 0.044715


def _blk(n, pref, unit=LANES):
    best = None
    b = unit
    while b <= min(n, pref):
        if n % b == 0:
            best = b
        b += unit
    return best if best is not None else n


def _dot(a, b, dn=(((1,), (0,)), ((), ()))):
    return lax.dot_general(a.astype(BF16), b.astype(BF16), dn, preferred_element_type=F32)


def _dot_hi(a, b, dn=(((1,), (0,)), ((), ()))):
    return lax.dot_general(a, b, dn, precision=HI, preferred_element_type=F32)


NN = (((1,), (0,)), ((), ()))
NT = (((1,), (1,)), ((), ()))
TN = (((0,), (0,)), ((), ()))


def _sigmoid(x):
    return 1.0 / (1.0 + jnp.exp(-x))


def _gelu(y):
    return 0.5 * y * (1.0 + jnp.tanh(GELU_K * (y + GELU_C * y * y * y)))


def _gelu_grad(y):
    th = jnp.tanh(GELU_K * (y + GELU_C * y * y * y))
    return 0.5 * (1.0 + th) + 0.5 * y * (1.0 - th * th) * GELU_K * (1.0 + 3.0 * GELU_C * y * y)


def _mm(a, b, *, name, ta=False, tb=False, out_dtype=F32, bm=1024, bn=1024, bk=512):
    if ta:
        K, M = a.shape
    else:
        M, K = a.shape
    if tb:
        N, K2 = b.shape
    else:
        K2, N = b.shape
    assert K == K2, (a.shape, b.shape, ta, tb)
    bm, bn, bk = _blk(M, bm), _blk(N, bn), _blk(K, bk)
    nk = K // bk
    dn = (((0 if ta else 1,), (1 if tb else 0,)), ((), ()))

    def body(a_ref, b_ref, o_ref, acc_ref):
        k = pl.program_id(2)

        @pl.when(k == 0)
        def _():
            acc_ref[...] = jnp.zeros_like(acc_ref)

        acc_ref[...] += _dot(a_ref[...], b_ref[...], dn)

        @pl.when(k == nk - 1)
        def _():
            o_ref[...] = acc_ref[...].astype(out_dtype)

    a_spec = pl.BlockSpec((bk, bm), lambda i, j, k: (k, i)) if ta else pl.BlockSpec((bm, bk), lambda i, j, k: (i, k))
    b_spec = pl.BlockSpec((bn, bk), lambda i, j, k: (j, k)) if tb else pl.BlockSpec((bk, bn), lambda i, j, k: (k, j))
    return pl.pallas_call(
        body,
        name=name,
        grid=(M // bm, N // bn, nk),
        in_specs=[a_spec, b_spec],
        out_specs=pl.BlockSpec((bm, bn), lambda i, j, k: (i, j)),
        out_shape=jax.ShapeDtypeStruct((M, N), out_dtype),
        scratch_shapes=[pltpu.VMEM((bm, bn), F32)],
        compiler_params=pltpu.CompilerParams(dimension_semantics=("parallel", "parallel", "arbitrary")),
    )(a, b)


def _mm_ksplit(a1, a2, b, after, *, name, out_dtype=F32, bm=1024, bn=1024, bk=512):
    M, K1 = a1.shape
    K2 = a2.shape[1]
    N = b.shape[0]
    bm, bn = _blk(M, bm), _blk(N, bn)
    bk = _blk(math.gcd(K1, K2), bk)
    nk1, nk = K1 // bk, (K1 + K2) // bk

    def body(a1_ref, a2_ref, b_ref, _after_ref, o_ref, acc_ref):
        k = pl.program_id(2)

        @pl.when(k == 0)
        def _():
            acc_ref[...] = jnp.zeros_like(acc_ref)

        @pl.when(k < nk1)
        def _():
            acc_ref[...] += _dot(a1_ref[...], b_ref[...], NT)

        @pl.when(k >= nk1)
        def _():
            acc_ref[...] += _dot(a2_ref[...], b_ref[...], NT)

        @pl.when(k == nk - 1)
        def _():
            o_ref[...] = acc_ref[...].astype(out_dtype)

    return pl.pallas_call(
        body, name=name, grid=(M // bm, N // bn, nk),
        in_specs=[pl.BlockSpec((bm, bk), lambda i, j, k: (i, jnp.minimum(k, nk1 - 1))),
                  pl.BlockSpec((bm, bk), lambda i, j, k: (i, jnp.maximum(k - nk1, 0))),
                  pl.BlockSpec((bn, bk), lambda i, j, k: (j, k)),
                  pl.BlockSpec(memory_space=pl.ANY)],
        out_specs=pl.BlockSpec((bm, bn), lambda i, j, k: (i, j)),
        out_shape=jax.ShapeDtypeStruct((M, N), out_dtype),
        scratch_shapes=[pltpu.VMEM((bm, bn), F32)],
        compiler_params=pltpu.CompilerParams(dimension_semantics=("parallel", "parallel", "arbitrary")),
    )(a1, a2, b, after)


def _mm_nsplit(a, b1, b2, *, name, out_dtype=F32, bm=1024, bn=1024, bk=512):
    K, M = a.shape
    N1, N2 = b1.shape[1], b2.shape[1]
    bm, bk = _blk(M, bm), _blk(K, bk)
    bn = _blk(math.gcd(N1, N2), bn)
    nj1, nj = N1 // bn, (N1 + N2) // bn
    nk = K // bk

    def body(a_ref, b1_ref, b2_ref, o_ref, acc_ref):
        j = pl.program_id(1)
        k = pl.program_id(2)

        @pl.when(k == 0)
        def _():
            acc_ref[...] = jnp.zeros_like(acc_ref)

        @pl.when(j < nj1)
        def _():
            acc_ref[...] += _dot(a_ref[...], b1_ref[...], TN)

        @pl.when(j >= nj1)
        def _():
            acc_ref[...] += _dot(a_ref[...], b2_ref[...], TN)

        @pl.when(k == nk - 1)
        def _():
            o_ref[...] = acc_ref[...].astype(out_dtype)

    return pl.pallas_call(
        body, name=name, grid=(M // bm, nj, nk),
        in_specs=[pl.BlockSpec((bk, bm), lambda i, j, k: (k, i)),
                  pl.BlockSpec((bk, bn), lambda i, j, k: (jnp.where(j < nj1, k, nk - 1), jnp.minimum(j, nj1 - 1))),
                  pl.BlockSpec((bk, bn), lambda i, j, k: (jnp.where(j >= nj1, k, 0), jnp.maximum(j - nj1, 0)))],
        out_specs=pl.BlockSpec((bm, bn), lambda i, j, k: (i, j)),
        out_shape=jax.ShapeDtypeStruct((M, N1 + N2), out_dtype),
        scratch_shapes=[pltpu.VMEM((bm, bn), F32)],
        compiler_params=pltpu.CompilerParams(dimension_semantics=("parallel", "parallel", "arbitrary")),
    )(a, b1, b2)


def _prenorm_fwd(x, w):
    L, D = x.shape
    tr = _blk(L, 256, SUBLANES)

    def body(x_ref, w_ref, h_ref):
        xv = x_ref[...]
        r = lax.rsqrt(jnp.mean(xv * xv, axis=-1, keepdims=True) + EPS)
        h_ref[...] = (xv * r * w_ref[...]).astype(BF16)

    return pl.pallas_call(
        body, name="prenorm_fwd", grid=(L // tr,),
        in_specs=[pl.BlockSpec((tr, D), lambda i: (i, 0)), pl.BlockSpec((1, D), lambda i: (0, 0))],
        out_specs=pl.BlockSpec((tr, D), lambda i: (i, 0)),
        out_shape=jax.ShapeDtypeStruct((L, D), BF16),
        compiler_params=pltpu.CompilerParams(dimension_semantics=("parallel",)),
    )(x, w)


def _post_fwd_bwd(mixed, x, target, w):
    L, D = x.shape
    tr = _blk(L, 256, SUBLANES)
    nsteps = L // tr

    def body(mx_ref, x_ref, t_ref, w_ref, loss_ref, dm_ref, dout_ref, gw_ref, acc_ref):
        i = pl.program_id(0)

        @pl.when(i == 0)
        def _():
            acc_ref[...] = jnp.zeros_like(acc_ref)
            gw_ref[...] = jnp.zeros_like(gw_ref)

        mx = mx_ref[...]
        wv = w_ref[...]
        r = lax.rsqrt(jnp.mean(mx * mx, axis=-1, keepdims=True) + EPS)
        n = mx * r
        err = x_ref[...] + n * wv - t_ref[...]
        acc_ref[...] += jnp.sum(err * err, axis=0, keepdims=True)
        dout = err * (1.0 / D)
        dout_ref[...] = dout
        gw_ref[...] += jnp.sum(dout * n, axis=0, keepdims=True)
        dn = dout * wv
        dm_ref[...] = (r * (dn - n * jnp.mean(dn * n, axis=-1, keepdims=True))).astype(BF16)

        @pl.when(i == nsteps - 1)
        def _():
            loss_ref[...] = jnp.sum(acc_ref[...], axis=-1, keepdims=True) * (0.5 / D)

    row = pl.BlockSpec((tr, D), lambda i: (i, 0))
    vec = pl.BlockSpec((1, D), lambda i: (0, 0))
    return pl.pallas_call(
        body, name="post_fwd_bwd", grid=(nsteps,),
        in_specs=[row, row, row, vec],
        out_specs=[pl.BlockSpec((1, 1), lambda i: (0, 0)), row, row, vec],
        out_shape=[jax.ShapeDtypeStruct((1, 1), F32), jax.ShapeDtypeStruct((L, D), BF16),
                   jax.ShapeDtypeStruct((L, D), F32), jax.ShapeDtypeStruct((1, D), F32)],
        scratch_shapes=[pltpu.VMEM((1, D), F32)],
        compiler_params=pltpu.CompilerParams(dimension_semantics=("arbitrary",)),
    )(mixed, x, target, w)


def _prenorm_bwd(x, dh_main, dh_low, dout, w):
    L, D = x.shape
    tr = _blk(L, 256, SUBLANES)

    def body(x_ref, a_ref, b_ref, dout_ref, w_ref, gx_ref, gw_ref):
        i = pl.program_id(0)

        @pl.when(i == 0)
        def _():
            gw_ref[...] = jnp.zeros_like(gw_ref)

        xv = x_ref[...]
        r = lax.rsqrt(jnp.mean(xv * xv, axis=-1, keepdims=True) + EPS)
        n = xv * r
        dh = a_ref[...] + b_ref[...]
        gw_ref[...] += jnp.sum(dh * n, axis=0, keepdims=True)
        dn = dh * w_ref[...]
        gx_ref[...] = dout_ref[...] + r * (dn - n * jnp.mean(dn * n, axis=-1, keepdims=True))

    row = pl.BlockSpec((tr, D), lambda i: (i, 0))
    vec = pl.BlockSpec((1, D), lambda i: (0, 0))
    return pl.pallas_call(
        body, name="prenorm_bwd", grid=(L // tr,),
        in_specs=[row, row, row, row, vec],
        out_specs=[row, vec],
        out_shape=[jax.ShapeDtypeStruct((L, D), F32), jax.ShapeDtypeStruct((1, D), F32)],
        compiler_params=pltpu.CompilerParams(dimension_semantics=("arbitrary",)),
    )(x, dh_main, dh_low, dout, w)


def _s5_disc(a_re_raw, a_im, dt):
    a_re = jnp.minimum(a_re_raw, -1e-4)
    mag = jnp.exp(a_re * dt)
    ph = a_im * dt
    ab_re = mag * jnp.cos(ph)
    ab_im = mag * jnp.sin(ph)
    inv_n = 1.0 / (a_re * a_re + a_im * a_im)
    ia_re = a_re * inv_n
    ia_im = -a_im * inv_n
    n_re = ab_re - 1.0
    f_re = n_re * ia_re - ab_im * ia_im
    f_im = n_re * ia_im + ab_im * ia_re
    return a_re, ab_re, ab_im, f_re, f_im, ia_re, ia_im


def _s5_prep_fwd(a_re, a_im, log_dt, b_re, b_im, e16):
    G, P = a_re.shape
    PH = b_re.shape[1]

    def body(are_ref, aim_ref, ldt_ref, bre_ref, bim_ref, e_ref, bbre_ref, bbim_ref, pwre_ref, pwim_ref):
        dt = jnp.exp(ldt_ref[...])
        _, ab_re, ab_im, f_re, f_im, _, _ = _s5_disc(are_ref[...], aim_ref[...], dt)
        fx_re = _dot_hi(f_re, e_ref[...])
        fx_im = _dot_hi(f_im, e_ref[...])
        br, bi = bre_ref[...], bim_ref[...]
        bbre_ref[...] = fx_re * br - fx_im * bi
        bbim_ref[...] = fx_re * bi + fx_im * br
        pr, pi = ab_re, ab_im
        pwre_ref[0] = pr
        pwim_ref[0] = pi
        for k in range(1, SUBLANES):
            pr, pi = pr * ab_re - pi * ab_im, pr * ab_im + pi * ab_re
            pwre_ref[k] = pr
            pwim_ref[k] = pi

    vm = pl.BlockSpec(memory_space=pltpu.VMEM)
    return pl.pallas_call(
        body, name="s5_prep_fwd",
        in_specs=[vm] * 6, out_specs=[vm] * 4,
        out_shape=[jax.ShapeDtypeStruct((G, PH), F32), jax.ShapeDtypeStruct((G, PH), F32),
                   jax.ShapeDtypeStruct((SUBLANES, G, P), F32), jax.ShapeDtypeStruct((SUBLANES, G, P), F32)],
    )(a_re, a_im, log_dt, b_re, b_im, e16)


def _s5_prep_bwd(a_re, a_im, log_dt, b_re, b_im, e16, gbb_re, gbb_im, gab_re, gab_im):
    G, P = a_re.shape
    PH = b_re.shape[1]

    def body(are_ref, aim_ref, ldt_ref, bre_ref, bim_ref, e_ref, gbr_ref, gbi_ref, gar_ref, gai_ref,
             o_are, o_aim, o_bre, o_bim, o_ldt):
        dt = jnp.exp(ldt_ref[...])
        a_raw = are_ref[...]
        a_imv = aim_ref[...]
        a_re_c, ab_re, ab_im, f_re, f_im, ia_re, ia_im = _s5_disc(a_raw, a_imv, dt)
        ev = e_ref[...]
        fx_re = _dot_hi(f_re, ev)
        fx_im = _dot_hi(f_im, ev)
        gbr, gbi = gbr_ref[...], gbi_ref[...]
        br, bi = bre_ref[...], bim_ref[...]
        o_bre[...] = fx_re * gbr + fx_im * gbi
        o_bim[...] = fx_re * gbi - fx_im * gbr
        gf_re = _dot_hi(br * gbr + bi * gbi, ev, NT)
        gf_im = _dot_hi(br * gbi - bi * gbr, ev, NT)
        gab_r = gar_ref[...] + ia_re * gf_re + ia_im * gf_im
        gab_i = gai_ref[...] + ia_re * gf_im - ia_im * gf_re
        q_re = f_re * ia_re - f_im * ia_im
        q_im = f_re * ia_im + f_im * ia_re
        ga_re = -(q_re * gf_re + q_im * gf_im)
        ga_im = -(q_re * gf_im - q_im * gf_re)
        gth_re = ab_re * gab_r + ab_im * gab_i
        gth_im = ab_re * gab_i - ab_im * gab_r
        ga_re = ga_re + dt * gth_re
        ga_im = ga_im + dt * gth_im
        gdt = jnp.sum(a_re_c * gth_re + a_imv * gth_im, axis=-1, keepdims=True)
        o_ldt[...] = gdt * dt
        slope = jnp.where(a_raw < -1e-4, 1.0, jnp.where(a_raw == -1e-4, 0.5, 0.0))
        o_are[...] = ga_re * slope
        o_aim[...] = ga_im

    vm = pl.BlockSpec(memory_space=pltpu.VMEM)
    return pl.pallas_call(
        body, name="s5_prep_bwd",
        in_specs=[vm] * 10, out_specs=[vm] * 5,
        out_shape=[jax.ShapeDtypeStruct((G, P), F32), jax.ShapeDtypeStruct((G, P), F32),
                   jax.ShapeDtypeStruct((G, PH), F32), jax.ShapeDtypeStruct((G, PH), F32),
                   jax.ShapeDtypeStruct((G, 1), F32)],
    )(a_re, a_im, log_dt, b_re, b_im, e16, gbb_re, gbb_im, gab_re, gab_im)


def _block_diag(m, nb, rows, cols):
    g8 = S5_COLS // S5_GROUP
    m = m.reshape(nb, g8, rows, 1, cols) * jnp.eye(g8, dtype=m.dtype)[None, :, None, :, None]
    return m.reshape(nb, g8 * rows, g8 * cols)


def _block_diag_extract(m, nb, rows, cols):
    g8 = S5_COLS // S5_GROUP
    m = m.reshape(nb, g8, rows, g8, cols)
    idx = jnp.arange(g8)
    return m[:, idx, :, idx, :].transpose(1, 0, 2, 3).reshape(nb * g8, rows, cols)


def _scan_tables(pw_re, pw_im, nb):
    pw_re = pw_re.reshape(SUBLANES, nb, 1, S5_LANES)
    pw_im = pw_im.reshape(SUBLANES, nb, 1, S5_LANES)
    row = jnp.arange(SUBLANES, dtype=jnp.int32).reshape(1, SUBLANES, 1)
    tabs = []
    for k in (1, 2, 4):
        keep = (row >= k).astype(F32)
        tabs += [pw_re[k - 1] * keep, pw_im[k - 1] * keep]
    tabs += [jnp.moveaxis(pw_re[:, :, 0, :], 0, 1), jnp.moveaxis(pw_im[:, :, 0, :], 0, 1)]
    for k in (1, 2, 4):
        keep = (row < SUBLANES - k).astype(F32)
        tabs += [pw_re[k - 1] * keep, -pw_im[k - 1] * keep]
    tabs += [jnp.moveaxis(pw_re[::-1, :, 0, :], 0, 1), -jnp.moveaxis(pw_im[::-1, :, 0, :], 0, 1)]
    tabs = [jnp.broadcast_to(t, (nb, SUBLANES, S5_LANES)) for t in tabs]
    return jnp.stack(tabs, axis=1)


def _scan8(xr, xi, tab_ref, base, shifts):
    for lvl, sh in enumerate(shifts):
        mr = tab_ref[0, base + 2 * lvl]
        mi = tab_ref[0, base + 2 * lvl + 1]
        ar = pltpu.roll(xr, sh, 0)
        ai = pltpu.roll(xi, sh, 0)
        xr, xi = xr + mr * ar - mi * ai, xi + mr * ai + mi * ar
    return xr, xi


def _s5_scan_fwd(proj_main, bbd_re, bbd_im, cbd_re, cbd_im, dvec, tab, DS):
    L = proj_main.shape[0]
    nb = DS // S5_COLS
    tb = _blk(L, 512, SUBLANES)
    nt = L // tb
    ng = tb // SUBLANES

    def body(u_ref, bre_ref, bim_ref, cre_ref, cim_ref, d_ref, tab_ref, y_ref, sre_ref, sim_ref, car_ref):
        t = pl.program_id(1)

        @pl.when(t == 0)
        def _():
            car_ref[...] = jnp.zeros_like(car_ref)

        u = u_ref[...]
        sre_ref[...] = _dot(u, bre_ref[0])
        sim_ref[...] = _dot(u, bim_ref[0])

        def grp(r, carry):
            cr, ci = carry
            off = pl.multiple_of(r * SUBLANES, SUBLANES)
            xr, xi = _scan8(sre_ref[pl.ds(off, SUBLANES), :], sim_ref[pl.ds(off, SUBLANES), :], tab_ref, 0, (1, 2, 4))
            pr, pi = tab_ref[0, 6], tab_ref[0, 7]
            xr, xi = xr + pr * cr - pi * ci, xi + pr * ci + pi * cr
            sre_ref[pl.ds(off, SUBLANES), :] = xr
            sim_ref[pl.ds(off, SUBLANES), :] = xi
            return (jnp.broadcast_to(xr[SUBLANES - 1:SUBLANES, :], xr.shape),
                    jnp.broadcast_to(xi[SUBLANES - 1:SUBLANES, :], xi.shape))

        cr, ci = lax.fori_loop(0, ng, grp, (car_ref[0], car_ref[1]))
        car_ref[0] = cr
        car_ref[1] = ci
        y_ref[...] = _dot(sre_ref[...], cre_ref[0]) - _dot(sim_ref[...], cim_ref[0]) + d_ref[...] * u

    return pl.pallas_call(
        body, name="s5_scan_fwd", grid=(nb, nt),
        in_specs=[
            pl.BlockSpec((tb, S5_COLS), lambda j, t: (t, j)),
            pl.BlockSpec((1, S5_COLS, S5_LANES), lambda j, t: (j, 0, 0)),
            pl.BlockSpec((1, S5_COLS, S5_LANES), lambda j, t: (j, 0, 0)),
            pl.BlockSpec((1, S5_LANES, S5_COLS), lambda j, t: (j, 0, 0)),
            pl.BlockSpec((1, S5_LANES, S5_COLS), lambda j, t: (j, 0, 0)),
            pl.BlockSpec((1, S5_COLS), lambda j, t: (0, j)),
            pl.BlockSpec((1, 16, SUBLANES, S5_LANES), lambda j, t: (j, 0, 0, 0)),
        ],
        out_specs=[
            pl.BlockSpec((tb, S5_COLS), lambda j, t: (t, j)),
            pl.BlockSpec((tb, S5_LANES), lambda j, t: (t, j)),
            pl.BlockSpec((tb, S5_LANES), lambda j, t: (t, j)),
        ],
        out_shape=[jax.ShapeDtypeStruct((L, DS), F32),
                   jax.ShapeDtypeStruct((L, nb * S5_LANES), F32),
                   jax.ShapeDtypeStruct((L, nb * S5_LANES), F32)],
        scratch_shapes=[pltpu.VMEM((2, SUBLANES, S5_LANES), F32)],
        compiler_params=pltpu.CompilerParams(dimension_semantics=("parallel", "arbitrary")),
    )(proj_main, bbd_re, bbd_im, cbd_re, cbd_im, dvec, tab)


def _s5_scan_bwd(dy, proj_main, s_re, s_im, bbd_re, bbd_im, cbd_re, cbd_im, dvec, tab, d_s5, DS):
    L = proj_main.shape[0]
    nb = DS // S5_COLS
    tb = _blk(L, 512, SUBLANES)
    nt = L // tb
    ng = tb // SUBLANES
    tb8 = tb // SUBLANES

    def body(dy_ref, u_ref, sre_ref, sim_ref, pre_ref, pim_ref, bre_ref, bim_ref, cre_ref, cim_ref, d_ref, tab_ref, _ds5_ref,
             du_ref, gd_ref, gcre_ref, gcim_ref, gbre_ref, gbim_ref, gare_ref, gaim_ref,
             lre_ref, lim_ref, car_ref):
        t = pl.program_id(1)

        @pl.when(t == 0)
        def _():
            car_ref[...] = jnp.zeros_like(car_ref)
            gd_ref[...] = jnp.zeros_like(gd_ref)
            gcre_ref[...] = jnp.zeros_like(gcre_ref)
            gcim_ref[...] = jnp.zeros_like(gcim_ref)
            gbre_ref[...] = jnp.zeros_like(gbre_ref)
            gbim_ref[...] = jnp.zeros_like(gbim_ref)
            gare_ref[...] = jnp.zeros_like(gare_ref)
            gaim_ref[...] = jnp.zeros_like(gaim_ref)

        dyv = dy_ref[...]
        u = u_ref[...]
        gd_ref[...] += jnp.sum(dyv * u, axis=0, keepdims=True)
        lre_ref[...] = _dot(dyv, cre_ref[0], NT)
        lim_ref[...] = -_dot(dyv, cim_ref[0], NT)
        gcre_ref[0] += _dot(sre_ref[...], dyv, TN)
        gcim_ref[0] -= _dot(sim_ref[...], dyv, TN)

        first = (t == nt - 1).astype(F32)
        head_re = pre_ref[...] * (1.0 - first)
        head_im = pim_ref[...] * (1.0 - first)
        row0 = lax.broadcasted_iota(jnp.int32, (SUBLANES, S5_LANES), 0) == 0

        def grp(i, carry):
            cr, ci, acc_re, acc_im = carry
            r = ng - 1 - i
            off = pl.multiple_of(r * SUBLANES, SUBLANES)
            xr, xi = _scan8(lre_ref[pl.ds(off, SUBLANES), :], lim_ref[pl.ds(off, SUBLANES), :], tab_ref, 8, (7, 6, 4))
            pr, pi = tab_ref[0, 14], tab_ref[0, 15]
            xr, xi = xr + pr * cr - pi * ci, xi + pr * ci + pi * cr
            lre_ref[pl.ds(off, SUBLANES), :] = xr
            lim_ref[pl.ds(off, SUBLANES), :] = xi
            poff = pl.multiple_of(jnp.maximum(r - 1, 0) * SUBLANES, SUBLANES)
            prev_re = jnp.where(r == 0, head_re, sre_ref[pl.ds(poff, SUBLANES), :])
            prev_im = jnp.where(r == 0, head_im, sim_ref[pl.ds(poff, SUBLANES), :])
            prev_re = jnp.broadcast_to(prev_re[SUBLANES - 1:SUBLANES, :], xr.shape)
            prev_im = jnp.broadcast_to(prev_im[SUBLANES - 1:SUBLANES, :], xi.shape)
            sp_re = jnp.where(row0, prev_re, pltpu.roll(sre_ref[pl.ds(off, SUBLANES), :], 1, 0))
            sp_im = jnp.where(row0, prev_im, pltpu.roll(sim_ref[pl.ds(off, SUBLANES), :], 1, 0))
            acc_re = acc_re + sp_re * xr + sp_im * xi
            acc_im = acc_im + sp_re * xi - sp_im * xr
            return (jnp.broadcast_to(xr[0:1, :], xr.shape), jnp.broadcast_to(xi[0:1, :], xi.shape), acc_re, acc_im)

        zero = jnp.zeros((SUBLANES, S5_LANES), F32)
        cr, ci, acc_re, acc_im = lax.fori_loop(0, ng, grp, (car_ref[0], car_ref[1], zero, zero))
        car_ref[0] = cr
        car_ref[1] = ci
        gare_ref[...] += jnp.sum(acc_re, axis=0, keepdims=True)
        gaim_ref[...] += jnp.sum(acc_im, axis=0, keepdims=True)
        lre = lre_ref[...]
        lim = lim_ref[...]
        du = dyv * d_ref[...] + _dot(lre, bre_ref[0], NT) + _dot(lim, bim_ref[0], NT)
        du_ref[...] = du.astype(BF16)
        gbre_ref[0] += _dot(u, lre, TN)
        gbim_ref[0] += _dot(u, lim, TN)

    rt = lambda t: nt - 1 - t
    col = pl.BlockSpec((tb, S5_COLS), lambda j, t: (rt(t), j))
    st = pl.BlockSpec((tb, S5_LANES), lambda j, t: (rt(t), j))
    prev = pl.BlockSpec((SUBLANES, S5_LANES), lambda j, t: (jnp.maximum(rt(t) * tb8 - 1, 0), j))
    bmat = pl.BlockSpec((1, S5_COLS, S5_LANES), lambda j, t: (j, 0, 0))
    cmat = pl.BlockSpec((1, S5_LANES, S5_COLS), lambda j, t: (j, 0, 0))
    return pl.pallas_call(
        body, name="s5_scan_bwd", grid=(nb, nt),
        in_specs=[col, col, st, st, prev, prev, bmat, bmat, cmat, cmat,
                  pl.BlockSpec((1, S5_COLS), lambda j, t: (0, j)),
                  pl.BlockSpec((1, 16, SUBLANES, S5_LANES), lambda j, t: (j, 0, 0, 0)),
                  pl.BlockSpec(memory_space=pl.ANY)],
        out_specs=[col, pl.BlockSpec((1, S5_COLS), lambda j, t: (0, j)), cmat, cmat, bmat, bmat,
                   pl.BlockSpec((1, S5_LANES), lambda j, t: (0, j)), pl.BlockSpec((1, S5_LANES), lambda j, t: (0, j))],
        input_output_aliases={12: 0},
        out_shape=[jax.ShapeDtypeStruct((L, 2 * DS), BF16), jax.ShapeDtypeStruct((1, DS), F32),
                   jax.ShapeDtypeStruct((nb, S5_LANES, S5_COLS), F32), jax.ShapeDtypeStruct((nb, S5_LANES, S5_COLS), F32),
                   jax.ShapeDtypeStruct((nb, S5_COLS, S5_LANES), F32), jax.ShapeDtypeStruct((nb, S5_COLS, S5_LANES), F32),
                   jax.ShapeDtypeStruct((1, nb * S5_LANES), F32), jax.ShapeDtypeStruct((1, nb * S5_LANES), F32)],
        scratch_shapes=[pltpu.VMEM((tb, S5_LANES), F32), pltpu.VMEM((tb, S5_LANES), F32),
                        pltpu.VMEM((2, SUBLANES, S5_LANES), F32)],
        compiler_params=pltpu.CompilerParams(dimension_semantics=("parallel", "arbitrary")),
    )(dy, proj_main, s_re, s_im, s_re, s_im, bbd_re, bbd_im, cbd_re, cbd_im, dvec, tab, d_s5)


def _s5_post_fwd(y_pre, proj_main, glu_w, glu_b, DS):
    L = y_pre.shape[0]
    tr = _blk(L, 256, SUBLANES)

    def body(y_ref, z_ref, w_ref, b_ref, o_ref, t_ref):
        y1 = _gelu(y_ref[...])
        t = _dot(y1, w_ref[...]) + b_ref[...]
        t_ref[...] = t
        z = z_ref[...]
        o_ref[...] = (y1 * _sigmoid(t) * (z * _sigmoid(z))).astype(BF16)

    row = pl.BlockSpec((tr, DS), lambda i: (i, 0))
    return pl.pallas_call(
        body, name="s5_post_fwd", grid=(L // tr,),
        in_specs=[row, pl.BlockSpec((tr, DS), lambda i: (i, 1)), pl.BlockSpec((DS, DS), lambda i: (0, 0)),
                  pl.BlockSpec((1, DS), lambda i: (0, 0))],
        out_specs=[row, row],
        out_shape=[jax.ShapeDtypeStruct((L, 2 * DS), BF16), jax.ShapeDtypeStruct((L, DS), F32)],
        compiler_params=pltpu.CompilerParams(dimension_semantics=("parallel",)),
    )(y_pre, proj_main, glu_w, glu_b)


def _s5_post_bwd(d_ycat, y_pre, proj_main, t_pre, glu_w, DS):
    L = y_pre.shape[0]
    tr = _blk(L, 256, SUBLANES)

    def body(dy_ref, y_ref, z_ref, t_ref, w_ref, dyp_ref, dz_ref, dt_ref, y1_ref, gb_ref):
        i = pl.program_id(0)

        @pl.when(i == 0)
        def _():
            gb_ref[...] = jnp.zeros_like(gb_ref)

        dy = dy_ref[...]
        yp = y_ref[...]
        z = z_ref[...]
        y1 = _gelu(yp)
        sg = _sigmoid(t_ref[...])
        sz = _sigmoid(z)
        c = y1 * sg
        d_c = dy * (z * sz)
        dz_ref[...] = (dy * c * (sz * (1.0 + z * (1.0 - sz)))).astype(BF16)
        d_t = d_c * y1 * sg * (1.0 - sg)
        gb_ref[...] += jnp.sum(d_t, axis=0, keepdims=True)
        dt_ref[...] = d_t.astype(BF16)
        y1_ref[...] = y1.astype(BF16)
        d_y1 = d_c * sg + _dot(d_t, w_ref[...], NT)
        dyp_ref[...] = d_y1 * _gelu_grad(yp)

    row = pl.BlockSpec((tr, DS), lambda i: (i, 0))
    return pl.pallas_call(
        body, name="s5_post_bwd", grid=(L // tr,),
        in_specs=[row, row, pl.BlockSpec((tr, DS), lambda i: (i, 1)), row, pl.BlockSpec((DS, DS), lambda i: (0, 0))],
        out_specs=[row, pl.BlockSpec((tr, DS), lambda i: (i, 1)), row, row, pl.BlockSpec((1, DS), lambda i: (0, 0))],
        out_shape=[jax.ShapeDtypeStruct((L, DS), F32), jax.ShapeDtypeStruct((L, 2 * DS), BF16),
                   jax.ShapeDtypeStruct((L, DS), BF16), jax.ShapeDtypeStruct((L, DS), BF16),
                   jax.ShapeDtypeStruct((1, DS), F32)],
        compiler_params=pltpu.CompilerParams(dimension_semantics=("arbitrary",)),
    )(d_ycat, y_pre, proj_main, t_pre, glu_w)


def _gla_gates(glow, gu_ref, gb_ref):
    a = _dot(glow, gu_ref[...]) + gb_ref[...]
    lg = (jnp.minimum(a, 0.0) - jnp.log(1.0 + jnp.exp(-jnp.abs(a)))) * (1.0 / GLA_TAU)
    ri = lax.broadcasted_iota(jnp.int32, (GLA_CHUNK, GLA_CHUNK), 0)
    ci = lax.broadcasted_iota(jnp.int32, (GLA_CHUNK, GLA_CHUNK), 1)
    b = _dot_hi((ri >= ci).astype(F32), lg)
    b_last = jnp.sum(lg, axis=0, keepdims=True)
    return a, b, b_last, ri >= ci


def _gla_specs(DS, DK, DV, c, cmap):
    return [
        pl.BlockSpec((c, DK), lambda n: (cmap(n), 2 * DS // DK)),
        pl.BlockSpec((c, DK), lambda n: (cmap(n), 2 * DS // DK + 1)),
        pl.BlockSpec((c, DV), lambda n: (cmap(n), (2 * DS + 2 * DK) // DV)),
        pl.BlockSpec((c, DV), lambda n: (cmap(n), (2 * DS + 2 * DK) // DV + 1)),
    ]


def _gla_fwd(proj_main, proj_low, gate_up_pad, gate_bias, norm_w, ycat, DS, DK, DV):
    L = proj_main.shape[0]
    nc = L // GLA_CHUNK
    cps = math.gcd(GLA_STEP_CHUNKS, nc)
    nh = DK // GLA_HK
    scale = GLA_HK ** -0.5

    def body(q_ref, k_ref, v_ref, z_ref, gl_ref, gu_ref, gb_ref, nw_ref, _yc_ref, y_ref, sp_ref, st_ref):
        n = pl.program_id(0)

        @pl.when(n == 0)
        def _():
            st_ref[...] = jnp.zeros_like(st_ref)

        for sc in range(cps):
            rs = slice(sc * GLA_CHUNK, (sc + 1) * GLA_CHUNK)
            _, b, b_last, mask = _gla_gates(gl_ref[rs, :], gu_ref, gb_ref)
            for h in range(nh):
                ks = slice(h * GLA_HK, (h + 1) * GLA_HK)
                vs = slice(h * GLA_HV, (h + 1) * GLA_HV)
                bh, bl = b[:, ks], b_last[:, ks]
                qe = (q_ref[rs, ks] * scale) * jnp.exp(bh)
                kh = k_ref[rs, ks]
                ke = kh * jnp.exp(-bh)
                ktail = kh * jnp.exp(bl - bh)
                vh = v_ref[rs, vs]
                st = st_ref[h]
                sp_ref[sc, h] = st
                attn = jnp.where(mask, _dot(qe, ke, NT), 0.0)
                o = _dot(attn, vh) + _dot(qe, st, NT)
                st_ref[h] = jnp.exp(bl) * st + _dot(vh, ktail, TN)
                r = lax.rsqrt(jnp.mean(o * o, axis=-1, keepdims=True) + EPS)
                z = z_ref[rs, vs]
                y_ref[rs, vs] = (o * r * nw_ref[...] * (z * _sigmoid(z))).astype(BF16)

    c = cps * GLA_CHUNK
    return pl.pallas_call(
        body, name="gla_fwd", grid=(nc // cps,),
        in_specs=_gla_specs(DS, DK, DV, c, lambda n: n) + [
            pl.BlockSpec((c, LANES), lambda n: (n, 0)),
            pl.BlockSpec((LANES, DK), lambda n: (0, 0)),
            pl.BlockSpec((1, DK), lambda n: (0, 0)),
            pl.BlockSpec((1, GLA_HV), lambda n: (0, 0)),
            pl.BlockSpec(memory_space=pl.ANY),
        ],
        out_specs=[pl.BlockSpec((c, DV), lambda n: (n, DS // DV)),
                   pl.BlockSpec((cps, nh, GLA_HV, GLA_HK), lambda n: (n, 0, 0, 0))],
        input_output_aliases={8: 0},
        out_shape=[jax.ShapeDtypeStruct(ycat.shape, BF16), jax.ShapeDtypeStruct((nc, nh, GLA_HV, GLA_HK), F32)],
        scratch_shapes=[pltpu.VMEM((nh, GLA_HV, GLA_HK), F32)],
        compiler_params=pltpu.CompilerParams(dimension_semantics=("arbitrary",)),
    )(proj_main, proj_main, proj_main, proj_main, proj_low, gate_up_pad, gate_bias, norm_w, ycat)


def _gla_bwd(d_ycat, proj_main, proj_low, s_prev, gate_up_pad, gate_bias, norm_w, DS, DK, DV):
    L = proj_main.shape[0]
    nc = L // GLA_CHUNK
    cps = math.gcd(GLA_STEP_CHUNKS, nc)
    nh = DK // GLA_HK
    scale = GLA_HK ** -0.5

    def body(dy_ref, q_ref, k_ref, v_ref, z_ref, gl_ref, sp_ref, gu_ref, gb_ref, nw_ref,
             dg_ref, da_ref, gnw_ref, ggb_ref, dst_ref):
        n = pl.program_id(0)

        @pl.when(n == 0)
        def _():
            dst_ref[...] = jnp.zeros_like(dst_ref)
            gnw_ref[...] = jnp.zeros_like(gnw_ref)
            ggb_ref[...] = jnp.zeros_like(ggb_ref)

        last_row = lax.broadcasted_iota(jnp.int32, (GLA_CHUNK, GLA_HK), 0) == GLA_CHUNK - 1
        ri = lax.broadcasted_iota(jnp.int32, (GLA_CHUNK, GLA_CHUNK), 0)
        ci = lax.broadcasted_iota(jnp.int32, (GLA_CHUNK, GLA_CHUNK), 1)
        upper = (ci >= ri).astype(F32)
        nw = nw_ref[...]
        for sc in reversed(range(cps)):
            rs = slice(sc * GLA_CHUNK, (sc + 1) * GLA_CHUNK)
            a, b, b_last, mask = _gla_gates(gl_ref[rs, :], gu_ref, gb_ref)
            for h in range(nh):
                ks = slice(h * GLA_HK, (h + 1) * GLA_HK)
                vs = slice(h * GLA_HV, (h + 1) * GLA_HV)
                bh, bl = b[:, ks], b_last[:, ks]
                e = jnp.exp(bh)
                einv = jnp.exp(-bh)
                etail = jnp.exp(bl - bh)
                dec = jnp.exp(bl)
                qe = (q_ref[rs, ks] * scale) * e
                kh = k_ref[rs, ks]
                ke = kh * einv
                ktail = kh * etail
                vh = v_ref[rs, vs]
                st = sp_ref[sc, h]
                dst = dst_ref[h]
                attn = jnp.where(mask, _dot(qe, ke, NT), 0.0)
                o = _dot(attn, vh) + _dot(qe, st, NT)
                r = lax.rsqrt(jnp.mean(o * o, axis=-1, keepdims=True) + EPS)
                nrm = o * r
                z = z_ref[rs, vs]
                sz = _sigmoid(z)
                dy = dy_ref[rs, vs]
                dg_ref[rs, 2 * DK + DV + h * GLA_HV:2 * DK + DV + (h + 1) * GLA_HV] = (
                    dy * nrm * nw * (sz * (1.0 + z * (1.0 - sz)))).astype(BF16)
                d_on = dy * (z * sz)
                gnw_ref[...] += jnp.sum(d_on * nrm, axis=0, keepdims=True)
                d_n = d_on * nw
                d_o = r * (d_n - nrm * jnp.mean(d_n * nrm, axis=-1, keepdims=True))
                d_attn = jnp.where(mask, _dot(d_o, vh, NT), 0.0)
                dg_ref[rs, 2 * DK + h * GLA_HV:2 * DK + (h + 1) * GLA_HV] = (
                    _dot(attn, d_o, TN) + _dot(ktail, dst, NT)).astype(BF16)
                d_qe = _dot(d_attn, ke) + _dot(d_o, st)
                d_ke = _dot(d_attn, qe, TN)
                d_kt = _dot(vh, dst)
                d_dec = jnp.sum(dst * st, axis=0, keepdims=True)
                dst_ref[h] = dec * dst + _dot(d_o, qe, TN)
                dg_ref[rs, ks] = (d_qe * scale * e).astype(BF16)
                dg_ref[rs, DK + h * GLA_HK:DK + (h + 1) * GLA_HK] = (d_ke * einv + d_kt * etail).astype(BF16)
                d_bl = jnp.sum(d_kt * ktail, axis=0, keepdims=True) + d_dec * dec
                d_b = d_qe * qe - d_ke * ke - d_kt * ktail + jnp.where(last_row, d_bl, 0.0)
                d_lg = _dot_hi(upper, d_b)
                d_a = d_lg * (1.0 / GLA_TAU) * _sigmoid(-a[:, ks])
                ggb_ref[:, ks] += jnp.sum(d_a, axis=0, keepdims=True)
                da_ref[rs, ks] = d_a.astype(BF16)

    c = cps * GLA_CHUNK
    ns = nc // cps
    rn = lambda n: ns - 1 - n
    return pl.pallas_call(
        body, name="gla_bwd", grid=(ns,),
        in_specs=[pl.BlockSpec((c, DV), lambda n: (rn(n), DS // DV))] + _gla_specs(DS, DK, DV, c, rn) + [
            pl.BlockSpec((c, LANES), lambda n: (rn(n), 0)),
            pl.BlockSpec((cps, nh, GLA_HV, GLA_HK), lambda n: (rn(n), 0, 0, 0)),
            pl.BlockSpec((LANES, DK), lambda n: (0, 0)),
            pl.BlockSpec((1, DK), lambda n: (0, 0)),
            pl.BlockSpec((1, GLA_HV), lambda n: (0, 0)),
        ],
        out_specs=[pl.BlockSpec((c, 2 * DK + 2 * DV), lambda n: (rn(n), 0)),
                   pl.BlockSpec((c, DK), lambda n: (rn(n), 0)),
                   pl.BlockSpec((1, GLA_HV), lambda n: (0, 0)), pl.BlockSpec((1, DK), lambda n: (0, 0))],
        out_shape=[jax.ShapeDtypeStruct((L, 2 * DK + 2 * DV), BF16),
                   jax.ShapeDtypeStruct((L, DK), BF16),
                   jax.ShapeDtypeStruct((1, GLA_HV), F32), jax.ShapeDtypeStruct((1, DK), F32)],
        scratch_shapes=[pltpu.VMEM((nh, GLA_HV, GLA_HK), F32)],
        compiler_params=pltpu.CompilerParams(dimension_semantics=("arbitrary",)),
    )(d_ycat, proj_main, proj_main, proj_main, proj_main, proj_low, s_prev, gate_up_pad, gate_bias, norm_w)


def _adamw(w, g, m, v, name):
    R, C = w.shape
    tr = _blk(R, 256, SUBLANES)
    c1 = 1.0 - ADAM_B1 ** ADAM_STEP
    c2 = 1.0 - ADAM_B2 ** ADAM_STEP

    def body(w_ref, g_ref, m_ref, v_ref, d_ref, nm_ref, nv_ref):
        g_ = g_ref[...]
        m_ = ADAM_B1 * m_ref[...] + (1.0 - ADAM_B1) * g_
        v_ = ADAM_B2 * v_ref[...] + (1.0 - ADAM_B2) * (g_ * g_)
        nm_ref[...] = m_
        nv_ref[...] = v_
        d_ref[...] = -ADAM_LR * ((m_ / c1) / (jnp.sqrt(v_ / c2) + ADAM_EPS) + ADAM_WD * w_ref[...])

    blk = pl.BlockSpec((tr, C), lambda i: (i, 0))
    sd = jax.ShapeDtypeStruct((R, C), F32)
    return pl.pallas_call(
        body, name=name, grid=(R // tr,), in_specs=[blk] * 4, out_specs=[blk] * 3, out_shape=[sd] * 3,
        compiler_params=pltpu.CompilerParams(dimension_semantics=("parallel",)),
    )(w, g, m, v)


def _my_pos():
    return lax.axis_index("x"), lax.axis_index("y"), lax.axis_index("c")


def _gather_weights(shards):
    n = len(shards)
    halves = [s.shape[0] // 2 for s in shards]

    def body(*refs):
        ins, outs = refs[:n], refs[n:2 * n]
        send_sems, recv_sems = refs[2 * n:]
        x, y, c = _my_pos()
        me = 2 * x + y

        def piece(a, chip, half):
            return outs[a].at[chip, pl.ds(half * halves[a], halves[a]), :]

        def copy(a, k, src_chip, half, to):
            sl = piece(a, src_chip, half)
            return pltpu.make_async_remote_copy(src_ref=sl, dst_ref=sl, send_sem=send_sems.at[a, k], recv_sem=recv_sems.at[a, k],
                                                device_id=to, device_id_type=MESH)

        def first(a, d, to):
            src = ins[a].at[pl.ds(c * halves[a], halves[a]), :]
            return pltpu.make_async_remote_copy(src_ref=src, dst_ref=piece(a, me, c), send_sem=send_sems.at[a, d - 1],
                                                recv_sem=recv_sems.at[a, d - 1], device_id=to, device_id_type=MESH)

        sent = []
        for d in (1, 2, 3):
            to = (x ^ (d >> 1), y ^ (d & 1), c)
            for a in range(n):
                cp = first(a, d, to)
                cp.start()
                sent.append(cp)
        for d in (1, 2, 3):
            chip = (x ^ (d >> 1)) * 2 + (y ^ (d & 1))
            for a in range(n):
                copy(a, d - 1, chip, c, (x, y, c)).wait_recv()
                fw = copy(a, 2 + d, chip, c, (x, y, 1 - c))
                fw.start()
                sent.append(fw)
        for d in (1, 2, 3):
            chip = (x ^ (d >> 1)) * 2 + (y ^ (d & 1))
            for a in range(n):
                copy(a, 2 + d, chip, 1 - c, (x, y, c)).wait_recv()
        for cp in sent:
            cp.wait_send()

    hbm = pl.BlockSpec(memory_space=pltpu.HBM)
    return pl.pallas_call(
        body, name="gather_weights",
        in_specs=[hbm] * n, out_specs=[hbm] * n,
        out_shape=[jax.ShapeDtypeStruct((4,) + s.shape, s.dtype) for s in shards],
        scratch_shapes=[pltpu.SemaphoreType.DMA((n, 6)), pltpu.SemaphoreType.DMA((n, 6))],
    )(*shards)


def _pair_exchange(gs):
    n = len(gs)

    def body(*refs):
        ins, outs = refs[:n], refs[n:2 * n]
        send_sems, recv_sems = refs[2 * n:]
        x, y, c = _my_pos()
        sent = []
        for a in range(n):
            hrows = gs[a].shape[1] // 2
            cp = pltpu.make_async_remote_copy(
                src_ref=ins[a].at[:, pl.ds((1 - c) * hrows, hrows), :], dst_ref=outs[a], send_sem=send_sems.at[a],
                recv_sem=recv_sems.at[a], device_id=(x, y, 1 - c), device_id_type=MESH)
            cp.start()
            sent.append(cp)
        for cp in sent:
            cp.wait()

    hbm = pl.BlockSpec(memory_space=pltpu.HBM)
    return pl.pallas_call(
        body, name="grad_pair_exchange", in_specs=[hbm] * n, out_specs=[hbm] * n,
        out_shape=[jax.ShapeDtypeStruct((g.shape[0], g.shape[1] // 2, g.shape[2]), g.dtype) for g in gs],
        scratch_shapes=[pltpu.SemaphoreType.DMA((n,)), pltpu.SemaphoreType.DMA((n,))],
    )(*gs)


def _pair_add(g, got, c_arr, name):
    nk, rows2, cols = g.shape
    hrows = rows2 // 2
    tr = _blk(hrows, 256, 2 * SUBLANES)
    nb = hrows // tr

    def body(c_ref, a_ref, b_ref, o_ref):
        o_ref[...] = (a_ref[...].astype(F32) + b_ref[...].astype(F32)).astype(o_ref.dtype)

    return pl.pallas_call(
        body, name=name,
        grid_spec=pltpu.PrefetchScalarGridSpec(
            num_scalar_prefetch=1, grid=(nk, nb),
            in_specs=[pl.BlockSpec((1, tr, cols), lambda k, i, c_ref: (k, c_ref[0] * nb + i, 0)),
                      pl.BlockSpec((1, tr, cols), lambda k, i, c_ref: (k, i, 0))],
            out_specs=pl.BlockSpec((1, tr, cols), lambda k, i, c_ref: (k, i, 0))),
        out_shape=jax.ShapeDtypeStruct((nk, hrows, cols), g.dtype),
        compiler_params=pltpu.CompilerParams(dimension_semantics=("parallel", "parallel")),
    )(c_arr, g, got)


def _chip_scatter_copies(srcs, lands, send_sems, recv_sems):
    x, y, c = _my_pos()
    copies = []
    for d in (1, 2, 3):
        tx, ty = x ^ (d >> 1), y ^ (d & 1)
        for a in range(len(srcs)):
            copies.append(pltpu.make_async_remote_copy(
                src_ref=srcs[a].at[2 * tx + ty], dst_ref=lands[a].at[d - 1], send_sem=send_sems.at[3 * a + d - 1],
                recv_sem=recv_sems.at[3 * a + d - 1], device_id=(tx, ty, c), device_id_type=MESH))
    return copies


def _chip_scatter_start(pss):
    n = len(pss)

    def body(*refs):
        srcs, lands = refs[:n], refs[n:2 * n]
        send_sems, recv_sems = refs[2 * n], refs[2 * n + 1]
        token = refs[-1]
        for cp in _chip_scatter_copies(srcs, lands, send_sems, recv_sems):
            cp.start()
        token[...] = jnp.zeros_like(token)

    hbm = pl.BlockSpec(memory_space=pltpu.HBM)
    sem = pl.BlockSpec(memory_space=pltpu.SEMAPHORE)
    land_shapes = [(3,) + p.shape[1:] for p in pss]
    outs = pl.pallas_call(
        body, name="grad_chip_scatter_start",
        in_specs=[hbm] * (2 * n),
        out_specs=[sem, sem] + [hbm] * (2 * n) + [pl.BlockSpec(memory_space=pltpu.VMEM)],
        out_shape=[pltpu.SemaphoreType.DMA((3 * n,)), pltpu.SemaphoreType.DMA((3 * n,))]
        + [pltpu.HBM(p.shape, p.dtype) for p in pss]
        + [pltpu.HBM(s, p.dtype) for s, p in zip(land_shapes, pss)]
        + [jax.ShapeDtypeStruct((SUBLANES, LANES), F32)],
        input_output_aliases={i: 2 + i for i in range(2 * n)},
        compiler_params=pltpu.CompilerParams(has_side_effects=pltpu.SideEffectType.DATAFLOW_SIDE_EFFECTING),
    )(*[pltpu.with_memory_space_constraint(p, pltpu.HBM) for p in pss],
      *[pltpu.with_memory_space_constraint(lax.empty(s, p.dtype), pltpu.HBM) for s, p in zip(land_shapes, pss)])
    return outs[0], outs[1], outs[2:2 + n], outs[2 + n:2 + 2 * n], outs[-1]


def _chip_scatter_wait(send_sems, recv_sems, srcs, lands, after):
    n = len(srcs)

    def body(*refs):
        src_refs, land_refs = refs[:n], refs[n:2 * n]
        ssem, rsem = refs[2 * n], refs[2 * n + 1]
        for cp in _chip_scatter_copies(src_refs, land_refs, ssem, rsem):
            cp.wait_send()
            cp.wait_recv()

    hbm = pl.BlockSpec(memory_space=pltpu.HBM)
    sem = pl.BlockSpec(memory_space=pltpu.SEMAPHORE)
    outs = pl.pallas_call(
        body, name="grad_chip_scatter_wait",
        in_specs=[hbm] * (2 * n) + [sem, sem, pl.BlockSpec(memory_space=pl.ANY)],
        out_specs=[hbm] * (2 * n),
        out_shape=[pltpu.HBM(p.shape, p.dtype) for p in srcs] + [pltpu.HBM(p.shape, p.dtype) for p in lands],
        input_output_aliases={i: i for i in range(2 * n)},
        compiler_params=pltpu.CompilerParams(has_side_effects=pltpu.SideEffectType.DATAFLOW_SIDE_EFFECTING),
    )(*srcs, *lands, send_sems, recv_sems, after)
    return outs[:n], outs[n:]


def _chip_sum(ps, got, me_arr, name):
    _, hrows, cols = ps.shape
    tr = _blk(hrows, 256, 2 * SUBLANES)

    def body(me_ref, p_ref, g_ref, o_ref):
        acc = p_ref[0].astype(F32)
        for s in range(3):
            acc = acc + g_ref[s].astype(F32)
        o_ref[...] = acc

    return pl.pallas_call(
        body, name=name,
        grid_spec=pltpu.PrefetchScalarGridSpec(
            num_scalar_prefetch=1, grid=(hrows // tr,),
            in_specs=[pl.BlockSpec((1, tr, cols), lambda i, me_ref: (me_ref[0], i, 0)),
                      pl.BlockSpec((3, tr, cols), lambda i, me_ref: (0, i, 0))],
            out_specs=pl.BlockSpec((tr, cols), lambda i, me_ref: (i, 0))),
        out_shape=jax.ShapeDtypeStruct((hrows, cols), F32),
        compiler_params=pltpu.CompilerParams(dimension_semantics=("parallel",)),
    )(me_arr, ps, got)


def _pair_swap(halves):
    n = len(halves)

    def body(*refs):
        ins, outs = refs[:n], refs[n:2 * n]
        send_sems, recv_sems = refs[2 * n:]
        x, y, c = _my_pos()
        sent = []
        for a in range(n):
            cp = pltpu.make_async_remote_copy(src_ref=ins[a], dst_ref=outs[a], send_sem=send_sems.at[a], recv_sem=recv_sems.at[a],
                                              device_id=(x, y, 1 - c), device_id_type=MESH)
            cp.start()
            sent.append(cp)
        for cp in sent:
            cp.wait()

    hbm = pl.BlockSpec(memory_space=pltpu.HBM)
    return pl.pallas_call(
        body, name="grad_pair_swap", in_specs=[hbm] * n, out_specs=[hbm] * n,
        out_shape=[jax.ShapeDtypeStruct(h.shape, h.dtype) for h in halves],
        scratch_shapes=[pltpu.SemaphoreType.DMA((n,)), pltpu.SemaphoreType.DMA((n,))],
    )(*halves)


def _adamw_sharded(w, g_own, g_other, m, v, c_arr, name):
    R, C = w.shape
    hrows = R // 2
    tr = _blk(hrows, 256, SUBLANES)
    nbh = hrows // tr
    c1 = 1.0 - ADAM_B1 ** ADAM_STEP
    c2 = 1.0 - ADAM_B2 ** ADAM_STEP

    def body(c_ref, w_ref, go_ref, gx_ref, m_ref, v_ref, g_ref, d_ref, nm_ref, nv_ref):
        mine = (pl.program_id(0) // nbh) == c_ref[0]
        g_ = jnp.where(mine, go_ref[...], gx_ref[...])
        g_ref[...] = g_
        m_ = ADAM_B1 * m_ref[...] + (1.0 - ADAM_B1) * g_
        v_ = ADAM_B2 * v_ref[...] + (1.0 - ADAM_B2) * (g_ * g_)
        nm_ref[...] = m_
        nv_ref[...] = v_
        d_ref[...] = -ADAM_LR * ((m_ / c1) / (jnp.sqrt(v_ / c2) + ADAM_EPS) + ADAM_WD * w_ref[...])

    blk = pl.BlockSpec((tr, C), lambda i, c_ref: (i, 0))
    hblk = pl.BlockSpec((tr, C), lambda i, c_ref: (i % nbh, 0))
    sd = jax.ShapeDtypeStruct((R, C), F32)
    return pl.pallas_call(
        body, name=name,
        grid_spec=pltpu.PrefetchScalarGridSpec(
            num_scalar_prefetch=1, grid=(2 * nbh,),
            in_specs=[blk, hblk, hblk, blk, blk], out_specs=[blk] * 4),
        out_shape=[sd] * 4,
        compiler_params=pltpu.CompilerParams(dimension_semantics=("parallel",)),
    )(c_arr, w, g_own, g_other, m, v)


def _allreduce_small(vec):
    rows, lanes = vec.shape
    r8 = rows // 8

    def body(v_ref, o_ref, got_ref, send_sems, recv_sems):
        x, y, c = _my_pos()
        me = 4 * x + 2 * y + c

        def peer(d):
            return (x ^ (d >> 2), y ^ ((d >> 1) & 1), c ^ (d & 1))

        def lin(p):
            return 4 * p[0] + 2 * p[1] + p[2]

        sent = []
        for d in range(1, 8):
            to = peer(d)
            cp = pltpu.make_async_remote_copy(
                src_ref=v_ref.at[pl.ds(pl.multiple_of(lin(to) * r8, SUBLANES), r8), :], dst_ref=got_ref.at[d],
                send_sem=send_sems.at[0, d], recv_sem=recv_sems.at[0, d], device_id=to, device_id_type=MESH)
            cp.start()
            sent.append(cp)
        mine = pl.ds(pl.multiple_of(me * r8, SUBLANES), r8)
        acc = v_ref[mine, :]
        for d in range(1, 8):
            sent[d - 1].wait_recv()
            acc = acc + got_ref[d]
        got_ref[0] = acc
        o_ref[mine, :] = acc
        for d in range(1, 8):
            cp = pltpu.make_async_remote_copy(
                src_ref=got_ref.at[0], dst_ref=o_ref.at[mine, :],
                send_sem=send_sems.at[1, d], recv_sem=recv_sems.at[1, d], device_id=peer(d), device_id_type=MESH)
            cp.start()
            sent.append(cp)
        for d in range(1, 8):
            src = pl.ds(pl.multiple_of(lin(peer(d)) * r8, SUBLANES), r8)
            pltpu.make_async_remote_copy(
                src_ref=got_ref.at[0], dst_ref=o_ref.at[src, :],
                send_sem=send_sems.at[1, d], recv_sem=recv_sems.at[1, d], device_id=peer(d), device_id_type=MESH).wait_recv()
        for cp in sent:
            cp.wait_send()

    vm = pl.BlockSpec(memory_space=pltpu.VMEM)
    return pl.pallas_call(
        body, name="allreduce_small", in_specs=[vm], out_specs=vm,
        out_shape=jax.ShapeDtypeStruct((rows, lanes), F32),
        scratch_shapes=[pltpu.VMEM((8, r8, lanes), F32), pltpu.SemaphoreType.DMA((2, 8)), pltpu.SemaphoreType.DMA((2, 8))],
    )(vec)


def _pack_rows(arrs, row_mult):
    flat = jnp.concatenate([a.reshape(-1) for a in arrs])
    n = flat.shape[0]
    rows = -(-n // (LANES * row_mult)) * row_mult
    flat = jnp.pad(flat, (0, rows * LANES - n))
    return flat.reshape(rows, LANES)


def _unpack_rows(packed, shapes):
    flat = packed.reshape(-1)
    out, off = [], 0
    for s in shapes:
        n = math.prod(s)
        out.append(flat[off:off + n].reshape(s))
        off += n
    return out


def kernel(x, pre_norm_w, w_in, s5_A_re, s5_A_im, s5_B_re, s5_B_im, s5_C_re, s5_C_im, s5_D, s5_log_dt, s5_glu_w, s5_glu_b, gla_gate_up, gla_gate_bias, gla_norm_w, w_out, post_norm_w, loss_target, m_pre_norm_w, m_w_in, m_s5_A_re, m_s5_A_im, m_s5_B_re, m_s5_B_im, m_s5_C_re, m_s5_C_im, m_s5_D, m_s5_log_dt, m_s5_glu_w, m_s5_glu_b, m_gla_gate_up, m_gla_gate_bias, m_gla_norm_w, m_w_out, m_post_norm_w, v_pre_norm_w, v_w_in, v_s5_A_re, v_s5_A_im, v_s5_B_re, v_s5_B_im, v_s5_C_re, v_s5_C_im, v_s5_D, v_s5_log_dt, v_s5_glu_w, v_s5_glu_b, v_gla_gate_up, v_gla_gate_bias, v_gla_norm_w, v_w_out, v_post_norm_w):
    names = ["pre_norm_w", "w_in", "s5_A_re", "s5_A_im", "s5_B_re", "s5_B_im", "s5_C_re", "s5_C_im", "s5_D", "s5_log_dt",
             "s5_glu_w", "s5_glu_b", "gla_gate_up", "gla_gate_bias", "gla_norm_w", "w_out", "post_norm_w"]
    W = dict(zip(names, (pre_norm_w, w_in, s5_A_re, s5_A_im, s5_B_re, s5_B_im, s5_C_re, s5_C_im, s5_D, s5_log_dt,
                         s5_glu_w, s5_glu_b, gla_gate_up, gla_gate_bias, gla_norm_w, w_out, post_norm_w)))
    M = dict(zip(names, (m_pre_norm_w, m_w_in, m_s5_A_re, m_s5_A_im, m_s5_B_re, m_s5_B_im, m_s5_C_re, m_s5_C_im, m_s5_D,
                         m_s5_log_dt, m_s5_glu_w, m_s5_glu_b, m_gla_gate_up, m_gla_gate_bias, m_gla_norm_w, m_w_out,
                         m_post_norm_w)))
    V = dict(zip(names, (v_pre_norm_w, v_w_in, v_s5_A_re, v_s5_A_im, v_s5_B_re, v_s5_B_im, v_s5_C_re, v_s5_C_im, v_s5_D,
                         v_s5_log_dt, v_s5_glu_w, v_s5_glu_b, v_gla_gate_up, v_gla_gate_bias, v_gla_norm_w, v_w_out,
                         v_post_norm_w)))
    sharded = ("w_in", "s5_glu_w", "w_out", "gla_gate_up")

    xb = x[0]
    tgt = loss_target[0]
    L, D = xb.shape
    DS = D // 2
    G = DS // S5_GROUP
    P = S5_STATE
    NB = DS // S5_COLS
    DV = D - DS
    DK = DV // 2
    WM = 2 * DS + 2 * DK + 2 * DV
    nsh = w_in.shape[2]

    chip = 2 * lax.axis_index("x") + lax.axis_index("y")
    own = [w_in[0].astype(BF16), s5_glu_w[0].astype(BF16), w_out[0].astype(BF16), gla_gate_up[0]]
    g_win, g_glu, g_wout, g_gup = [lax.dynamic_update_index_in_dim(g, o, chip, 0)
                                   for g, o in zip(_gather_weights(own), own)]
    w_full = jnp.moveaxis(g_win, 0, 1).reshape(D, 4 * nsh)
    w_main = w_full[:, :WM]
    w_low = jnp.pad(w_full[:, WM:], ((0, 0), (0, LANES - GLA_RANK)))
    glu_w = g_glu.reshape(DS, DS)
    wout = g_wout.reshape(D, D)
    gup = jnp.moveaxis(g_gup, 0, 1).reshape(GLA_RANK, DK)
    gup_pad = jnp.pad(gup, ((0, LANES - GLA_RANK), (0, 0))).astype(BF16)

    e16 = jnp.repeat(jnp.eye(P, dtype=F32), S5_GROUP, axis=1)
    a_re, a_im = s5_A_re[0], s5_A_im[0]
    log_dt = s5_log_dt[0].reshape(G, 1)
    b_re = s5_B_re[0].reshape(G, P * S5_GROUP)
    b_im = s5_B_im[0].reshape(G, P * S5_GROUP)
    bb_re, bb_im, pw_re, pw_im = _s5_prep_fwd(a_re, a_im, log_dt, b_re, b_im, e16)
    to_hp = lambda m: m.reshape(G, P, S5_GROUP).transpose(0, 2, 1)
    bbd_re = _block_diag(to_hp(bb_re), NB, S5_GROUP, P).astype(BF16)
    bbd_im = _block_diag(to_hp(bb_im), NB, S5_GROUP, P).astype(BF16)
    cbd_re = _block_diag(s5_C_re[0].transpose(0, 2, 1), NB, P, S5_GROUP).astype(BF16)
    cbd_im = _block_diag(s5_C_im[0].transpose(0, 2, 1), NB, P, S5_GROUP).astype(BF16)
    tab = _scan_tables(pw_re, pw_im, NB)
    dvec = s5_D

    h = _prenorm_fwd(xb, pre_norm_w)
    proj_main = _mm(h, w_main, name="in_proj")
    proj_low = _mm(h, w_low, name="in_proj_low")
    y_pre, s_re, s_im = _s5_scan_fwd(proj_main, bbd_re, bbd_im, cbd_re, cbd_im, dvec, tab, DS)
    ycat, t_pre = _s5_post_fwd(y_pre, proj_main, glu_w, s5_glu_b, DS)
    ycat, s_prev = _gla_fwd(proj_main, proj_low, gup_pad, gla_gate_bias, gla_norm_w, ycat, DS, DK, DV)
    mixed = _mm(ycat, wout, name="out_proj")
    loss11, d_mixed, dout, g_post_w = _post_fwd_bwd(mixed, xb, tgt, post_norm_w)

    d_ycat = _mm(d_mixed, wout, tb=True, name="out_proj_dx")
    g_wout_full = _mm(ycat, d_mixed, ta=True, out_dtype=BF16, name="out_proj_dw")
    d_ypre, d_s5, d_t, y1, g_glu_b = _s5_post_bwd(d_ycat, y_pre, proj_main, t_pre, glu_w, DS)
    g_glu_full = _mm(y1, d_t, ta=True, out_dtype=BF16, name="glu_dw")
    d_s5, g_D, gcbd_re, gcbd_im, gbbd_re, gbbd_im, gab_re, gab_im = _s5_scan_bwd(
        d_ypre, proj_main, s_re, s_im, bbd_re, bbd_im, cbd_re, cbd_im, dvec, tab, d_s5, DS)
    d_gla, d_a, g_norm_w, g_gate_bias = _gla_bwd(
        d_ycat, proj_main, proj_low, s_prev, gup_pad, gla_gate_bias, gla_norm_w, DS, DK, DV)
    d_low = _mm(d_a, gup_pad, tb=True, out_dtype=BF16, name="gate_dx")
    g_gup_pad = _mm(proj_low, d_a, ta=True, name="gate_dw")
    g_wmain = _mm_nsplit(h, d_s5, d_gla, out_dtype=BF16, name="in_proj_dw")
    g_wlow = _mm(h, d_low, ta=True, out_dtype=BF16, name="in_proj_low_dw")

    g_win_full = jnp.concatenate([g_wmain, g_wlow[:, :GLA_RANK]], axis=1)
    gs = [jnp.moveaxis(g_win_full.reshape(D, 4, nsh), 1, 0),
          g_glu_full.reshape(4, DS // 4, DS),
          g_wout_full.reshape(4, D // 4, D),
          jnp.moveaxis(g_gup_pad[:GLA_RANK].reshape(GLA_RANK, 4, DK // 4), 1, 0)]
    c_arr = lax.axis_index("c").astype(jnp.int32).reshape(1)
    me_arr = chip.astype(jnp.int32).reshape(1)
    got = _pair_exchange(gs)
    pss = [_pair_add(g, r, c_arr, "grad_pair_add_" + n) for n, g, r in zip(sharded, gs, got)]
    send_sems, recv_sems, pss, lands, token = _chip_scatter_start(pss)

    dh_main = _mm_ksplit(d_s5, d_gla, w_main, token, name="in_proj_dx")
    dh_low = _mm(d_low, w_low, tb=True, name="in_proj_low_dx")
    grad_x, g_pre_w = _prenorm_bwd(xb, dh_main, dh_low, dout, pre_norm_w)
    pss, rcv = _chip_scatter_wait(send_sems, recv_sems, pss, lands, g_pre_w)

    from_hp = lambda m: m.transpose(0, 2, 1).reshape(G, P * S5_GROUP)
    gbb_re = from_hp(_block_diag_extract(gbbd_re, NB, S5_GROUP, P))
    gbb_im = from_hp(_block_diag_extract(gbbd_im, NB, S5_GROUP, P))
    g_a_re, g_a_im, g_b_re, g_b_im, g_ldt = _s5_prep_bwd(
        a_re, a_im, log_dt, b_re, b_im, e16, gbb_re, gbb_im, gab_re.reshape(G, P), gab_im.reshape(G, P))
    g_c_re = _block_diag_extract(gcbd_re, NB, P, S5_GROUP).transpose(0, 2, 1)
    g_c_im = _block_diag_extract(gcbd_im, NB, P, S5_GROUP).transpose(0, 2, 1)

    loss = lax.psum(loss11[0, 0], ("x", "y", "c"))

    small = [n for n in names if n not in sharded]
    g_small = {"pre_norm_w": g_pre_w, "s5_A_re": g_a_re, "s5_A_im": g_a_im, "s5_B_re": g_b_re, "s5_B_im": g_b_im,
               "s5_C_re": g_c_re, "s5_C_im": g_c_im, "s5_D": g_D, "s5_log_dt": g_ldt, "s5_glu_b": g_glu_b,
               "gla_gate_bias": g_gate_bias, "gla_norm_w": g_norm_w, "post_norm_w": g_post_w}
    small_shapes = [W[n].shape for n in small]
    gsum = _allreduce_small(_pack_rows([g_small[n] for n in small], 8 * SUBLANES))
    pk = lambda d: _pack_rows([d[n] for n in small], 8 * SUBLANES)
    d_s, m_s, v_s = _adamw(pk(W), gsum, pk(M), pk(V), "adamw_small")
    G_out = dict(zip(small, _unpack_rows(gsum, small_shapes)))
    D_out = dict(zip(small, _unpack_rows(d_s, small_shapes)))
    M_out = dict(zip(small, _unpack_rows(m_s, small_shapes)))
    V_out = dict(zip(small, _unpack_rows(v_s, small_shapes)))

    halves = [_chip_sum(p, r, me_arr, "grad_chip_sum_" + n) for n, p, r in zip(sharded, pss, rcv)]
    others = _pair_swap(halves)
    for n, g_own, g_other in zip(sharded, halves, others):
        g_, d_, m_, v_ = _adamw_sharded(W[n][0], g_own, g_other, M[n][0], V[n][0], c_arr, "adamw_" + n)
        G_out[n], D_out[n], M_out[n], V_out[n] = g_[None], d_[None], m_[None], v_[None]

    return (loss, grad_x[None], *[G_out[n] for n in names], *[D_out[n] for n in names],
            *[M_out[n] for n in names], *[V_out[n] for n in names])
```

```python
import functools
import math

import jax
import jax.numpy as jnp
from jax import lax
from jax.experimental import pallas as pl
from jax.experimental.pallas import tpu as pltpu

F32 = jnp.float32
BF16 = jnp.bfloat16
HI = lax.Precision.HIGHEST
MESH = pl.DeviceIdType.MESH

EPS = 1e-6
S5_GROUP = 16
S5_STATE = 64
GLA_HK = 128
GLA_HV = 256
GLA_RANK = 16
GLA_TAU = 16.0
GLA_CHUNK = 64
GLA_STEP_CHUNKS = 2
LANES = 128
SUBLANES = 8
S5_COLS = 128
S5_LANES = (S5_COLS // S5_GROUP) * S5_STATE

ADAM_LR = 0.001
ADAM_B1 = 0.9
ADAM_B2 = 0.999
ADAM_EPS = 1e-08
ADAM_WD = 0.01
ADAM_STEP = 10

GELU_K = math.sqrt(2.0 / math.pi)
GELU_C = 0.044715


def _blk(n, pref, unit=LANES):
    best = None
    b = unit
    while b <= min(n, pref):
        if n % b == 0:
            best = b
        b += unit
    return best if best is not None else n


def _dot(a, b, dn=(((1,), (0,)), ((), ()))):
    return lax.dot_general(a.astype(BF16), b.astype(BF16), dn, preferred_element_type=F32)


def _dot_hi(a, b, dn=(((1,), (0,)), ((), ()))):
    return lax.dot_general(a, b, dn, precision=HI, preferred_element_type=F32)


NN = (((1,), (0,)), ((), ()))
NT = (((1,), (1,)), ((), ()))
TN = (((0,), (0,)), ((), ()))


def _sigmoid(x):
    return 1.0 / (1.0 + jnp.exp(-x))


def _gelu(y):
    return 0.5 * y * (1.0 + jnp.tanh(GELU_K * (y + GELU_C * y * y * y)))


def _gelu_grad(y):
    th = jnp.tanh(GELU_K * (y + GELU_C * y * y * y))
    return 0.5 * (1.0 + th) + 0.5 * y * (1.0 - th * th) * GELU_K * (1.0 + 3.0 * GELU_C * y * y)


def _mm(a, b, *, name, ta=False, tb=False, out_dtype=F32, bm=1024, bn=1024, bk=512):
    if ta:
        K, M = a.shape
    else:
        M, K = a.shape
    if tb:
        N, K2 = b.shape
    else:
        K2, N = b.shape
    assert K == K2, (a.shape, b.shape, ta, tb)
    bm, bn, bk = _blk(M, bm), _blk(N, bn), _blk(K, bk)
    nk = K // bk
    dn = (((0 if ta else 1,), (1 if tb else 0,)), ((), ()))

    def body(a_ref, b_ref, o_ref, acc_ref):
        k = pl.program_id(2)

        @pl.when(k == 0)
        def _():
            acc_ref[...] = jnp.zeros_like(acc_ref)

        acc_ref[...] += _dot(a_ref[...], b_ref[...], dn)

        @pl.when(k == nk - 1)
        def _():
            o_ref[...] = acc_ref[...].astype(out_dtype)

    a_spec = pl.BlockSpec((bk, bm), lambda i, j, k: (k, i)) if ta else pl.BlockSpec((bm, bk), lambda i, j, k: (i, k))
    b_spec = pl.BlockSpec((bn, bk), lambda i, j, k: (j, k)) if tb else pl.BlockSpec((bk, bn), lambda i, j, k: (k, j))
    return pl.pallas_call(
        body,
        name=name,
        grid=(M // bm, N // bn, nk),
        in_specs=[a_spec, b_spec],
        out_specs=pl.BlockSpec((bm, bn), lambda i, j, k: (i, j)),
        out_shape=jax.ShapeDtypeStruct((M, N), out_dtype),
        scratch_shapes=[pltpu.VMEM((bm, bn), F32)],
        compiler_params=pltpu.CompilerParams(dimension_semantics=("parallel", "parallel", "arbitrary")),
    )(a, b)


def _mm_ksplit(a1, a2, b, after, *, name, out_dtype=F32, bm=1024, bn=1024, bk=512):
    M, K1 = a1.shape
    K2 = a2.shape[1]
    N = b.shape[0]
    bm, bn = _blk(M, bm), _blk(N, bn)
    bk = _blk(math.gcd(K1, K2), bk)
    nk1, nk = K1 // bk, (K1 + K2) // bk

    def body(a1_ref, a2_ref, b_ref, _after_ref, o_ref, acc_ref):
        k = pl.program_id(2)

        @pl.when(k == 0)
        def _():
            acc_ref[...] = jnp.zeros_like(acc_ref)

        @pl.when(k < nk1)
        def _():
            acc_ref[...] += _dot(a1_ref[...], b_ref[...], NT)

        @pl.when(k >= nk1)
        def _():
            acc_ref[...] += _dot(a2_ref[...], b_ref[...], NT)

        @pl.when(k == nk - 1)
        def _():
            o_ref[...] = acc_ref[...].astype(out_dtype)

    return pl.pallas_call(
        body, name=name, grid=(M // bm, N // bn, nk),
        in_specs=[pl.BlockSpec((bm, bk), lambda i, j, k: (i, jnp.minimum(k, nk1 - 1))),
                  pl.BlockSpec((bm, bk), lambda i, j, k: (i, jnp.maximum(k - nk1, 0))),
                  pl.BlockSpec((bn, bk), lambda i, j, k: (j, k)),
                  pl.BlockSpec(memory_space=pl.ANY)],
        out_specs=pl.BlockSpec((bm, bn), lambda i, j, k: (i, j)),
        out_shape=jax.ShapeDtypeStruct((M, N), out_dtype),
        scratch_shapes=[pltpu.VMEM((bm, bn), F32)],
        compiler_params=pltpu.CompilerParams(dimension_semantics=("parallel", "parallel", "arbitrary")),
    )(a1, a2, b, after)


def _mm_nsplit(a, b1, b2, *, name, out_dtype=F32, bm=1024, bn=1024, bk=512):
    K, M = a.shape
    N1, N2 = b1.shape[1], b2.shape[1]
    bm, bk = _blk(M, bm), _blk(K, bk)
    bn = _blk(math.gcd(N1, N2), bn)
    nj1, nj = N1 // bn, (N1 + N2) // bn
    nk = K // bk

    def body(a_ref, b1_ref, b2_ref, o_ref, acc_ref):
        j = pl.program_id(1)
        k = pl.program_id(2)

        @pl.when(k == 0)
        def _():
            acc_ref[...] = jnp.zeros_like(acc_ref)

        @pl.when(j < nj1)
        def _():
            acc_ref[...] += _dot(a_ref[...], b1_ref[...], TN)

        @pl.when(j >= nj1)
        def _():
            acc_ref[...] += _dot(a_ref[...], b2_ref[...], TN)

        @pl.when(k == nk - 1)
        def _():
            o_ref[...] = acc_ref[...].astype(out_dtype)

    return pl.pallas_call(
        body, name=name, grid=(M // bm, nj, nk),
        in_specs=[pl.BlockSpec((bk, bm), lambda i, j, k: (k, i)),
                  pl.BlockSpec((bk, bn), lambda i, j, k: (jnp.where(j < nj1, k, nk - 1), jnp.minimum(j, nj1 - 1))),
                  pl.BlockSpec((bk, bn), lambda i, j, k: (jnp.where(j >= nj1, k, 0), jnp.maximum(j - nj1, 0)))],
        out_specs=pl.BlockSpec((bm, bn), lambda i, j, k: (i, j)),
        out_shape=jax.ShapeDtypeStruct((M, N1 + N2), out_dtype),
        scratch_shapes=[pltpu.VMEM((bm, bn), F32)],
        compiler_params=pltpu.CompilerParams(dimension_semantics=("parallel", "parallel", "arbitrary")),
    )(a, b1, b2)


def _prenorm_fwd(x, w):
    L, D = x.shape
    tr = _blk(L, 256, SUBLANES)

    def body(x_ref, w_ref, h_ref):
        xv = x_ref[...]
        r = lax.rsqrt(jnp.mean(xv * xv, axis=-1, keepdims=True) + EPS)
        h_ref[...] = (xv * r * w_ref[...]).astype(BF16)

    return pl.pallas_call(
        body, name="prenorm_fwd", grid=(L // tr,),
        in_specs=[pl.BlockSpec((tr, D), lambda i: (i, 0)), pl.BlockSpec((1, D), lambda i: (0, 0))],
        out_specs=pl.BlockSpec((tr, D), lambda i: (i, 0)),
        out_shape=jax.ShapeDtypeStruct((L, D), BF16),
        compiler_params=pltpu.CompilerParams(dimension_semantics=("parallel",)),
    )(x, w)


def _post_fwd_bwd(mixed, x, target, w):
    L, D = x.shape
    tr = _blk(L, 256, SUBLANES)
    nsteps = L // tr

    def body(mx_ref, x_ref, t_ref, w_ref, loss_ref, dm_ref, dout_ref, gw_ref, acc_ref):
        i = pl.program_id(0)

        @pl.when(i == 0)
        def _():
            acc_ref[...] = jnp.zeros_like(acc_ref)
            gw_ref[...] = jnp.zeros_like(gw_ref)

        mx = mx_ref[...]
        wv = w_ref[...]
        r = lax.rsqrt(jnp.mean(mx * mx, axis=-1, keepdims=True) + EPS)
        n = mx * r
        err = x_ref[...] + n * wv - t_ref[...]
        acc_ref[...] += jnp.sum(err * err, axis=0, keepdims=True)
        dout = err * (1.0 / D)
        dout_ref[...] = dout
        gw_ref[...] += jnp.sum(dout * n, axis=0, keepdims=True)
        dn = dout * wv
        dm_ref[...] = (r * (dn - n * jnp.mean(dn * n, axis=-1, keepdims=True))).astype(BF16)

        @pl.when(i == nsteps - 1)
        def _():
            loss_ref[...] = jnp.sum(acc_ref[...], axis=-1, keepdims=True) * (0.5 / D)

    row = pl.BlockSpec((tr, D), lambda i: (i, 0))
    vec = pl.BlockSpec((1, D), lambda i: (0, 0))
    return pl.pallas_call(
        body, name="post_fwd_bwd", grid=(nsteps,),
        in_specs=[row, row, row, vec],
        out_specs=[pl.BlockSpec((1, 1), lambda i: (0, 0)), row, row, vec],
        out_shape=[jax.ShapeDtypeStruct((1, 1), F32), jax.ShapeDtypeStruct((L, D), BF16),
                   jax.ShapeDtypeStruct((L, D), F32), jax.ShapeDtypeStruct((1, D), F32)],
        scratch_shapes=[pltpu.VMEM((1, D), F32)],
        compiler_params=pltpu.CompilerParams(dimension_semantics=("arbitrary",)),
    )(mixed, x, target, w)


def _prenorm_bwd(x, dh_main, dh_low, dout, w):
    L, D = x.shape
    tr = _blk(L, 256, SUBLANES)

    def body(x_ref, a_ref, b_ref, dout_ref, w_ref, gx_ref, gw_ref):
        i = pl.program_id(0)

        @pl.when(i == 0)
        def _():
            gw_ref[...] = jnp.zeros_like(gw_ref)

        xv = x_ref[...]
        r = lax.rsqrt(jnp.mean(xv * xv, axis=-1, keepdims=True) + EPS)
        n = xv * r
        dh = a_ref[...] + b_ref[...]
        gw_ref[...] += jnp.sum(dh * n, axis=0, keepdims=True)
        dn = dh * w_ref[...]
        gx_ref[...] = dout_ref[...] + r * (dn - n * jnp.mean(dn * n, axis=-1, keepdims=True))

    row = pl.BlockSpec((tr, D), lambda i: (i, 0))
    vec = pl.BlockSpec((1, D), lambda i: (0, 0))
    return pl.pallas_call(
        body, name="prenorm_bwd", grid=(L // tr,),
        in_specs=[row, row, row, row, vec],
        out_specs=[row, vec],
        out_shape=[jax.ShapeDtypeStruct((L, D), F32), jax.ShapeDtypeStruct((1, D), F32)],
        compiler_params=pltpu.CompilerParams(dimension_semantics=("arbitrary",)),
    )(x, dh_main, dh_low, dout, w)


def _s5_disc(a_re_raw, a_im, dt):
    a_re = jnp.minimum(a_re_raw, -1e-4)
    mag = jnp.exp(a_re * dt)
    ph = a_im * dt
    ab_re = mag * jnp.cos(ph)
    ab_im = mag * jnp.sin(ph)
    inv_n = 1.0 / (a_re * a_re + a_im * a_im)
    ia_re = a_re * inv_n
    ia_im = -a_im * inv_n
    n_re = ab_re - 1.0
    f_re = n_re * ia_re - ab_im * ia_im
    f_im = n_re * ia_im + ab_im * ia_re
    return a_re, ab_re, ab_im, f_re, f_im, ia_re, ia_im


def _iota2(shape, dim):
    return lax.broadcasted_iota(jnp.int32, shape, dim)


def _group_mask(rows, rows_per_group):
    shift = rows_per_group.bit_length() - 1
    return (_iota2((rows, S5_LANES), 0) >> shift) == (_iota2((rows, S5_LANES), 1) >> (S5_STATE.bit_length() - 1))


def _lane_tiler(dtype):
    return ((_iota2((S5_STATE, S5_LANES), 1) & (S5_STATE - 1)) == _iota2((S5_STATE, S5_LANES), 0)).astype(dtype)


def _row_to_col(row, n):
    eye = (_iota2((n, n), 0) == _iota2((n, n), 1)).astype(F32)
    return jnp.sum(eye * row, axis=1, keepdims=True)


def _group_repeat(G):
    return ((_iota2((G * S5_GROUP, G), 0) >> (S5_GROUP.bit_length() - 1)) == _iota2((G * S5_GROUP, G), 1)).astype(F32)


def _s5_prep_fwd(a_re, a_im, log_dt, b_re, b_im, c_re, c_im):
    G, P = a_re.shape
    nb = G * S5_GROUP // S5_COLS
    g8 = S5_COLS // S5_GROUP

    def body(are_ref, aim_ref, ldt_ref, bre_ref, bim_ref, cre_ref, cim_ref,
             bbre_ref, bbim_ref, ctre_ref, ctim_ref, tab_ref):
        dt = jnp.exp(_row_to_col(ldt_ref[...], G))
        _, ab_re, ab_im, f_re, f_im, _, _ = _s5_disc(are_ref[...], aim_ref[...], dt)
        rep = _group_repeat(G)
        fx_re = _dot_hi(rep, f_re)
        fx_im = _dot_hi(rep, f_im)
        br, bi = bre_ref[...], bim_ref[...]
        bb_re = fx_re * br - fx_im * bi
        bb_im = fx_re * bi + fx_im * br
        tile_bf = _lane_tiler(BF16)
        mask = _group_mask(S5_COLS, S5_GROUP)
        for jb in range(nb):
            rs = slice(jb * S5_COLS, (jb + 1) * S5_COLS)
            for src, dst in ((bb_re[rs], bbre_ref), (bb_im[rs], bbim_ref), (cre_ref[rs, :], ctre_ref), (cim_ref[rs, :], ctim_ref)):
                dst[jb] = jnp.where(mask, _dot(src, tile_bf), 0.0).astype(BF16)

        pw = [(ab_re, ab_im)]
        for _ in range(1, SUBLANES):
            pr, pi = pw[-1]
            pw.append((pr * ab_re - pi * ab_im, pr * ab_im + pi * ab_re))
        tile_f = _lane_tiler(F32)
        mask8 = _group_mask(g8, 1)
        row = _iota2((SUBLANES, S5_LANES), 0)
        for jb in range(nb):
            gs = slice(jb * g8, (jb + 1) * g8)

            def lanes(m):
                return jnp.sum(jnp.where(mask8, _dot_hi(m[gs], tile_f), 0.0), axis=0, keepdims=True)

            vec = [(lanes(r), lanes(i)) for r, i in pw]
            for lvl, k in enumerate((1, 2, 4)):
                tab_ref[jb, 2 * lvl] = jnp.where(row >= k, vec[k - 1][0], 0.0)
                tab_ref[jb, 2 * lvl + 1] = jnp.where(row >= k, vec[k - 1][1], 0.0)
                tab_ref[jb, 8 + 2 * lvl] = jnp.where(row < SUBLANES - k, vec[k - 1][0], 0.0)
                tab_ref[jb, 9 + 2 * lvl] = jnp.where(row < SUBLANES - k, -vec[k - 1][1], 0.0)
            f_r = f_i = r_r = r_i = jnp.zeros((SUBLANES, S5_LANES), F32)
            for i in range(SUBLANES):
                f_r = jnp.where(row == i, vec[i][0], f_r)
                f_i = jnp.where(row == i, vec[i][1], f_i)
                r_r = jnp.where(row == i, vec[SUBLANES - 1 - i][0], r_r)
                r_i = jnp.where(row == i, -vec[SUBLANES - 1 - i][1], r_i)
            tab_ref[jb, 6] = f_r
            tab_ref[jb, 7] = f_i
            tab_ref[jb, 14] = r_r
            tab_ref[jb, 15] = r_i

    vm = pl.BlockSpec(memory_space=pltpu.VMEM)
    bd = jax.ShapeDtypeStruct((nb, S5_COLS, S5_LANES), BF16)
    return pl.pallas_call(
        body, name="s5_prep_fwd",
        in_specs=[vm] * 7, out_specs=[vm] * 5,
        out_shape=[bd, bd, bd, bd, jax.ShapeDtypeStruct((nb, 16, SUBLANES, S5_LANES), F32)],
    )(a_re, a_im, log_dt, b_re, b_im, c_re, c_im)


def _s5_prep_bwd(a_re, a_im, log_dt, b_re, b_im, gbb_re, gbb_im, gct_re, gct_im, gab_re, gab_im):
    G, P = a_re.shape
    nb = G * S5_GROUP // S5_COLS
    g8 = S5_COLS // S5_GROUP

    def body(are_ref, aim_ref, ldt_ref, bre_ref, bim_ref, gbr_ref, gbi_ref, gcr_ref, gci_ref, gar_ref, gai_ref,
             o_a, o_bc, o_ldt):
        dt = jnp.exp(_row_to_col(ldt_ref[...], G))
        a_raw = are_ref[...]
        a_imv = aim_ref[...]
        a_re_c, ab_re, ab_im, f_re, f_im, ia_re, ia_im = _s5_disc(a_raw, a_imv, dt)
        tile_f = _lane_tiler(F32)
        mask = _group_mask(S5_COLS, S5_GROUP)
        mask8 = _group_mask(g8, 1)
        for jb in range(nb):
            rs = slice(jb * S5_COLS, (jb + 1) * S5_COLS)
            gs = slice(jb * g8, (jb + 1) * g8)
            ls = slice(jb * S5_LANES, (jb + 1) * S5_LANES)
            for k, src in enumerate((gbr_ref, gbi_ref, gcr_ref, gci_ref)):
                o_bc[k, rs, :] = _dot_hi(jnp.where(mask, src[jb], 0.0), tile_f, NT)
            for k, src in enumerate((gar_ref, gai_ref)):
                o_a[k, gs, :] = _dot_hi(jnp.where(mask8, src[:, ls], 0.0), tile_f, NT)
        rep = _group_repeat(G)
        fx_re = _dot_hi(rep, f_re)
        fx_im = _dot_hi(rep, f_im)
        gbr, gbi = o_bc[0], o_bc[1]
        br, bi = bre_ref[...], bim_ref[...]
        o_bc[0] = fx_re * gbr + fx_im * gbi
        o_bc[1] = fx_re * gbi - fx_im * gbr
        gf_re = _dot_hi(rep, br * gbr + bi * gbi, TN)
        gf_im = _dot_hi(rep, br * gbi - bi * gbr, TN)
        gab_r = o_a[0] + ia_re * gf_re + ia_im * gf_im
        gab_i = o_a[1] + ia_re * gf_im - ia_im * gf_re
        q_re = f_re * ia_re - f_im * ia_im
        q_im = f_re * ia_im + f_im * ia_re
        ga_re = -(q_re * gf_re + q_im * gf_im)
        ga_im = -(q_re * gf_im - q_im * gf_re)
        gth_re = ab_re * gab_r + ab_im * gab_i
        gth_im = ab_re * gab_i - ab_im * gab_r
        ga_re = ga_re + dt * gth_re
        ga_im = ga_im + dt * gth_im
        gdt = jnp.sum(a_re_c * gth_re + a_imv * gth_im, axis=-1, keepdims=True)
        eye = (_iota2((G, G), 0) == _iota2((G, G), 1)).astype(F32)
        o_ldt[...] = jnp.sum(eye * (gdt * dt), axis=0, keepdims=True)
        slope = jnp.where(a_raw < -1e-4, 1.0, jnp.where(a_raw == -1e-4, 0.5, 0.0))
        o_a[0] = ga_re * slope
        o_a[1] = ga_im

    vm = pl.BlockSpec(memory_space=pltpu.VMEM)
    return pl.pallas_call(
        body, name="s5_prep_bwd",
        in_specs=[vm] * 11, out_specs=[vm] * 3,
        out_shape=[jax.ShapeDtypeStruct((2, G, P), F32), jax.ShapeDtypeStruct((4, G * S5_GROUP, P), F32),
                   jax.ShapeDtypeStruct((1, G), F32)],
    )(a_re, a_im, log_dt, b_re, b_im, gbb_re, gbb_im, gct_re, gct_im, gab_re, gab_im)


def _scan8(xr, xi, tab_ref, base, shifts):
    for lvl, sh in enumerate(shifts):
        mr = tab_ref[0, base + 2 * lvl]
        mi = tab_ref[0, base + 2 * lvl + 1]
        ar = pltpu.roll(xr, sh, 0)
        ai = pltpu.roll(xi, sh, 0)
        xr, xi = xr + mr * ar - mi * ai, xi + mr * ai + mi * ar
    return xr, xi


def _s5_scan_fwd(proj_main, bbd_re, bbd_im, cbd_re, cbd_im, dvec, tab, DS):
    L = proj_main.shape[0]
    nb = DS // S5_COLS
    tb = _blk(L, 512, SUBLANES)
    nt = L // tb
    ng = tb // SUBLANES

    def body(u_ref, bre_ref, bim_ref, cre_ref, cim_ref, d_ref, tab_ref, y_ref, sre_ref, sim_ref, car_ref):
        t = pl.program_id(1)

        @pl.when(t == 0)
        def _():
            car_ref[...] = jnp.zeros_like(car_ref)

        u = u_ref[...]
        sre_ref[...] = _dot(u, bre_ref[0])
        sim_ref[...] = _dot(u, bim_ref[0])

        def grp(r, carry):
            cr, ci = carry
            off = pl.multiple_of(r * SUBLANES, SUBLANES)
            xr, xi = _scan8(sre_ref[pl.ds(off, SUBLANES), :], sim_ref[pl.ds(off, SUBLANES), :], tab_ref, 0, (1, 2, 4))
            pr, pi = tab_ref[0, 6], tab_ref[0, 7]
            xr, xi = xr + pr * cr - pi * ci, xi + pr * ci + pi * cr
            sre_ref[pl.ds(off, SUBLANES), :] = xr
            sim_ref[pl.ds(off, SUBLANES), :] = xi
            return (jnp.broadcast_to(xr[SUBLANES - 1:SUBLANES, :], xr.shape),
                    jnp.broadcast_to(xi[SUBLANES - 1:SUBLANES, :], xi.shape))

        cr, ci = lax.fori_loop(0, ng, grp, (car_ref[0], car_ref[1]))
        car_ref[0] = cr
        car_ref[1] = ci
        y_ref[...] = _dot(sre_ref[...], cre_ref[0], NT) - _dot(sim_ref[...], cim_ref[0], NT) + d_ref[...] * u

    return pl.pallas_call(
        body, name="s5_scan_fwd", grid=(nb, nt),
        in_specs=[
            pl.BlockSpec((tb, S5_COLS), lambda j, t: (t, j)),
            pl.BlockSpec((1, S5_COLS, S5_LANES), lambda j, t: (j, 0, 0)),
            pl.BlockSpec((1, S5_COLS, S5_LANES), lambda j, t: (j, 0, 0)),
            pl.BlockSpec((1, S5_COLS, S5_LANES), lambda j, t: (j, 0, 0)),
            pl.BlockSpec((1, S5_COLS, S5_LANES), lambda j, t: (j, 0, 0)),
            pl.BlockSpec((1, S5_COLS), lambda j, t: (0, j)),
            pl.BlockSpec((1, 16, SUBLANES, S5_LANES), lambda j, t: (j, 0, 0, 0)),
        ],
        out_specs=[
            pl.BlockSpec((tb, S5_COLS), lambda j, t: (t, j)),
            pl.BlockSpec((tb, S5_LANES), lambda j, t: (t, j)),
            pl.BlockSpec((tb, S5_LANES), lambda j, t: (t, j)),
        ],
        out_shape=[jax.ShapeDtypeStruct((L, DS), F32),
                   jax.ShapeDtypeStruct((L, nb * S5_LANES), F32),
                   jax.ShapeDtypeStruct((L, nb * S5_LANES), F32)],
        scratch_shapes=[pltpu.VMEM((2, SUBLANES, S5_LANES), F32)],
        compiler_params=pltpu.CompilerParams(dimension_semantics=("parallel", "arbitrary")),
    )(proj_main, bbd_re, bbd_im, cbd_re, cbd_im, dvec, tab)


def _s5_scan_bwd(dy, proj_main, s_re, s_im, bbd_re, bbd_im, cbd_re, cbd_im, dvec, tab, d_s5, DS):
    L = proj_main.shape[0]
    nb = DS // S5_COLS
    tb = _blk(L, 512, SUBLANES)
    nt = L // tb
    ng = tb // SUBLANES
    tb8 = tb // SUBLANES

    def body(dy_ref, u_ref, sre_ref, sim_ref, pre_ref, pim_ref, bre_ref, bim_ref, cre_ref, cim_ref, d_ref, tab_ref, _ds5_ref,
             du_ref, gd_ref, gcre_ref, gcim_ref, gbre_ref, gbim_ref, gare_ref, gaim_ref,
             lre_ref, lim_ref, car_ref):
        t = pl.program_id(1)

        @pl.when(t == 0)
        def _():
            car_ref[...] = jnp.zeros_like(car_ref)
            gd_ref[...] = jnp.zeros_like(gd_ref)
            gcre_ref[...] = jnp.zeros_like(gcre_ref)
            gcim_ref[...] = jnp.zeros_like(gcim_ref)
            gbre_ref[...] = jnp.zeros_like(gbre_ref)
            gbim_ref[...] = jnp.zeros_like(gbim_ref)
            gare_ref[...] = jnp.zeros_like(gare_ref)
            gaim_ref[...] = jnp.zeros_like(gaim_ref)

        dyv = dy_ref[...]
        u = u_ref[...]
        gd_ref[...] += jnp.sum(dyv * u, axis=0, keepdims=True)
        lre_ref[...] = _dot(dyv, cre_ref[0])
        lim_ref[...] = -_dot(dyv, cim_ref[0])
        gcre_ref[0] += _dot(dyv, sre_ref[...], TN)
        gcim_ref[0] -= _dot(dyv, sim_ref[...], TN)

        first = (t == nt - 1).astype(F32)
        head_re = pre_ref[...] * (1.0 - first)
        head_im = pim_ref[...] * (1.0 - first)
        row0 = lax.broadcasted_iota(jnp.int32, (SUBLANES, S5_LANES), 0) == 0

        def grp(i, carry):
            cr, ci, acc_re, acc_im = carry
            r = ng - 1 - i
            off = pl.multiple_of(r * SUBLANES, SUBLANES)
            xr, xi = _scan8(lre_ref[pl.ds(off, SUBLANES), :], lim_ref[pl.ds(off, SUBLANES), :], tab_ref, 8, (7, 6, 4))
            pr, pi = tab_ref[0, 14], tab_ref[0, 15]
            xr, xi = xr + pr * cr - pi * ci, xi + pr * ci + pi * cr
            lre_ref[pl.ds(off, SUBLANES), :] = xr
            lim_ref[pl.ds(off, SUBLANES), :] = xi
            poff = pl.multiple_of(jnp.maximum(r - 1, 0) * SUBLANES, SUBLANES)
            prev_re = jnp.where(r == 0, head_re, sre_ref[pl.ds(poff, SUBLANES), :])
            prev_im = jnp.where(r == 0, head_im, sim_ref[pl.ds(poff, SUBLANES), :])
            prev_re = jnp.broadcast_to(prev_re[SUBLANES - 1:SUBLANES, :], xr.shape)
            prev_im = jnp.broadcast_to(prev_im[SUBLANES - 1:SUBLANES, :], xi.shape)
            sp_re = jnp.where(row0, prev_re, pltpu.roll(sre_ref[pl.ds(off, SUBLANES), :], 1, 0))
            sp_im = jnp.where(row0, prev_im, pltpu.roll(sim_ref[pl.ds(off, SUBLANES), :], 1, 0))
            acc_re = acc_re + sp_re * xr + sp_im * xi
            acc_im = acc_im + sp_re * xi - sp_im * xr
            return (jnp.broadcast_to(xr[0:1, :], xr.shape), jnp.broadcast_to(xi[0:1, :], xi.shape), acc_re, acc_im)

        zero = jnp.zeros((SUBLANES, S5_LANES), F32)
        cr, ci, acc_re, acc_im = lax.fori_loop(0, ng, grp, (car_ref[0], car_ref[1], zero, zero))
        car_ref[0] = cr
        car_ref[1] = ci
        gare_ref[...] += jnp.sum(acc_re, axis=0, keepdims=True)
        gaim_ref[...] += jnp.sum(acc_im, axis=0, keepdims=True)
        lre = lre_ref[...]
        lim = lim_ref[...]
        du = dyv * d_ref[...] + _dot(lre, bre_ref[0], NT) + _dot(lim, bim_ref[0], NT)
        du_ref[...] = du.astype(BF16)
        gbre_ref[0] += _dot(u, lre, TN)
        gbim_ref[0] += _dot(u, lim, TN)

    rt = lambda t: nt - 1 - t
    col = pl.BlockSpec((tb, S5_COLS), lambda j, t: (rt(t), j))
    st = pl.BlockSpec((tb, S5_LANES), lambda j, t: (rt(t), j))
    prev = pl.BlockSpec((SUBLANES, S5_LANES), lambda j, t: (jnp.maximum(rt(t) * tb8 - 1, 0), j))
    bmat = pl.BlockSpec((1, S5_COLS, S5_LANES), lambda j, t: (j, 0, 0))
    cmat = bmat
    return pl.pallas_call(
        body, name="s5_scan_bwd", grid=(nb, nt),
        in_specs=[col, col, st, st, prev, prev, bmat, bmat, cmat, cmat,
                  pl.BlockSpec((1, S5_COLS), lambda j, t: (0, j)),
                  pl.BlockSpec((1, 16, SUBLANES, S5_LANES), lambda j, t: (j, 0, 0, 0)),
                  pl.BlockSpec(memory_space=pl.ANY)],
        out_specs=[col, pl.BlockSpec((1, S5_COLS), lambda j, t: (0, j)), cmat, cmat, bmat, bmat,
                   pl.BlockSpec((1, S5_LANES), lambda j, t: (0, j)), pl.BlockSpec((1, S5_LANES), lambda j, t: (0, j))],
        input_output_aliases={12: 0},
        out_shape=[jax.ShapeDtypeStruct((L, 2 * DS), BF16), jax.ShapeDtypeStruct((1, DS), F32),
                   jax.ShapeDtypeStruct((nb, S5_COLS, S5_LANES), F32), jax.ShapeDtypeStruct((nb, S5_COLS, S5_LANES), F32),
                   jax.ShapeDtypeStruct((nb, S5_COLS, S5_LANES), F32), jax.ShapeDtypeStruct((nb, S5_COLS, S5_LANES), F32),
                   jax.ShapeDtypeStruct((1, nb * S5_LANES), F32), jax.ShapeDtypeStruct((1, nb * S5_LANES), F32)],
        scratch_shapes=[pltpu.VMEM((tb, S5_LANES), F32), pltpu.VMEM((tb, S5_LANES), F32),
                        pltpu.VMEM((2, SUBLANES, S5_LANES), F32)],
        compiler_params=pltpu.CompilerParams(dimension_semantics=("parallel", "arbitrary")),
    )(dy, proj_main, s_re, s_im, s_re, s_im, bbd_re, bbd_im, cbd_re, cbd_im, dvec, tab, d_s5)


def _s5_post_fwd(y_pre, proj_main, glu_w, glu_b, DS):
    L = y_pre.shape[0]
    tr = _blk(L, 256, SUBLANES)

    def body(y_ref, z_ref, w_ref, b_ref, o_ref, t_ref):
        y1 = _gelu(y_ref[...])
        t = _dot(y1, w_ref[...]) + b_ref[...]
        t_ref[...] = t
        z = z_ref[...]
        o_ref[...] = (y1 * _sigmoid(t) * (z * _sigmoid(z))).astype(BF16)

    row = pl.BlockSpec((tr, DS), lambda i: (i, 0))
    return pl.pallas_call(
        body, name="s5_post_fwd", grid=(L // tr,),
        in_specs=[row, pl.BlockSpec((tr, DS), lambda i: (i, 1)), pl.BlockSpec((DS, DS), lambda i: (0, 0)),
                  pl.BlockSpec((1, DS), lambda i: (0, 0))],
        out_specs=[row, row],
        out_shape=[jax.ShapeDtypeStruct((L, 2 * DS), BF16), jax.ShapeDtypeStruct((L, DS), F32)],
        compiler_params=pltpu.CompilerParams(dimension_semantics=("parallel",)),
    )(y_pre, proj_main, glu_w, glu_b)


def _s5_post_bwd(d_ycat, y_pre, proj_main, t_pre, glu_w, DS):
    L = y_pre.shape[0]
    tr = _blk(L, 256, SUBLANES)

    def body(dy_ref, y_ref, z_ref, t_ref, w_ref, dyp_ref, dz_ref, dt_ref, y1_ref, gb_ref):
        i = pl.program_id(0)

        @pl.when(i == 0)
        def _():
            gb_ref[...] = jnp.zeros_like(gb_ref)

        dy = dy_ref[...]
        yp = y_ref[...]
        z = z_ref[...]
        y1 = _gelu(yp)
        sg = _sigmoid(t_ref[...])
        sz = _sigmoid(z)
        c = y1 * sg
        d_c = dy * (z * sz)
        dz_ref[...] = (dy * c * (sz * (1.0 + z * (1.0 - sz)))).astype(BF16)
        d_t = d_c * y1 * sg * (1.0 - sg)
        gb_ref[...] += jnp.sum(d_t, axis=0, keepdims=True)
        dt_ref[...] = d_t.astype(BF16)
        y1_ref[...] = y1.astype(BF16)
        d_y1 = d_c * sg + _dot(d_t, w_ref[...], NT)
        dyp_ref[...] = d_y1 * _gelu_grad(yp)

    row = pl.BlockSpec((tr, DS), lambda i: (i, 0))
    return pl.pallas_call(
        body, name="s5_post_bwd", grid=(L // tr,),
        in_specs=[row, row, pl.BlockSpec((tr, DS), lambda i: (i, 1)), row, pl.BlockSpec((DS, DS), lambda i: (0, 0))],
        out_specs=[row, pl.BlockSpec((tr, DS), lambda i: (i, 1)), row, row, pl.BlockSpec((1, DS), lambda i: (0, 0))],
        out_shape=[jax.ShapeDtypeStruct((L, DS), F32), jax.ShapeDtypeStruct((L, 2 * DS), BF16),
                   jax.ShapeDtypeStruct((L, DS), BF16), jax.ShapeDtypeStruct((L, DS), BF16),
                   jax.ShapeDtypeStruct((1, DS), F32)],
        compiler_params=pltpu.CompilerParams(dimension_semantics=("arbitrary",)),
    )(d_ycat, y_pre, proj_main, t_pre, glu_w)


def _gla_gates(glow, gu_ref, gb_ref):
    a = _dot(glow, gu_ref[...]) + gb_ref[...]
    lg = (jnp.minimum(a, 0.0) - jnp.log(1.0 + jnp.exp(-jnp.abs(a)))) * (1.0 / GLA_TAU)
    ri = lax.broadcasted_iota(jnp.int32, (GLA_CHUNK, GLA_CHUNK), 0)
    ci = lax.broadcasted_iota(jnp.int32, (GLA_CHUNK, GLA_CHUNK), 1)
    b = _dot_hi((ri >= ci).astype(F32), lg)
    b_last = jnp.sum(lg, axis=0, keepdims=True)
    return a, b, b_last, ri >= ci


def _gla_specs(DS, DK, DV, c, cmap):
    return [
        pl.BlockSpec((c, DK), lambda n: (cmap(n), 2 * DS // DK)),
        pl.BlockSpec((c, DK), lambda n: (cmap(n), 2 * DS // DK + 1)),
        pl.BlockSpec((c, DV), lambda n: (cmap(n), (2 * DS + 2 * DK) // DV)),
        pl.BlockSpec((c, DV), lambda n: (cmap(n), (2 * DS + 2 * DK) // DV + 1)),
    ]


def _gla_fwd(proj_main, proj_low, gate_up_pad, gate_bias, norm_w, ycat, DS, DK, DV):
    L = proj_main.shape[0]
    nc = L // GLA_CHUNK
    cps = math.gcd(GLA_STEP_CHUNKS, nc)
    nh = DK // GLA_HK
    scale = GLA_HK ** -0.5

    def body(q_ref, k_ref, v_ref, z_ref, gl_ref, gu_ref, gb_ref, nw_ref, _yc_ref, y_ref, sp_ref, st_ref):
        n = pl.program_id(0)

        @pl.when(n == 0)
        def _():
            st_ref[...] = jnp.zeros_like(st_ref)

        for sc in range(cps):
            rs = slice(sc * GLA_CHUNK, (sc + 1) * GLA_CHUNK)
            _, b, b_last, mask = _gla_gates(gl_ref[rs, :], gu_ref, gb_ref)
            for h in range(nh):
                ks = slice(h * GLA_HK, (h + 1) * GLA_HK)
                vs = slice(h * GLA_HV, (h + 1) * GLA_HV)
                bh, bl = b[:, ks], b_last[:, ks]
                qe = (q_ref[rs, ks] * scale) * jnp.exp(bh)
                kh = k_ref[rs, ks]
                ke = kh * jnp.exp(-bh)
                ktail = kh * jnp.exp(bl - bh)
                vh = v_ref[rs, vs]
                st = st_ref[h]
                sp_ref[sc, h] = st
                attn = jnp.where(mask, _dot(qe, ke, NT), 0.0)
                o = _dot(attn, vh) + _dot(qe, st, NT)
                st_ref[h] = jnp.exp(bl) * st + _dot(vh, ktail, TN)
                r = lax.rsqrt(jnp.mean(o * o, axis=-1, keepdims=True) + EPS)
                z = z_ref[rs, vs]
                y_ref[rs, vs] = (o * r * nw_ref[...] * (z * _sigmoid(z))).astype(BF16)

    c = cps * GLA_CHUNK
    return pl.pallas_call(
        body, name="gla_fwd", grid=(nc // cps,),
        in_specs=_gla_specs(DS, DK, DV, c, lambda n: n) + [
            pl.BlockSpec((c, LANES), lambda n: (n, 0)),
            pl.BlockSpec((LANES, DK), lambda n: (0, 0)),
            pl.BlockSpec((1, DK), lambda n: (0, 0)),
            pl.BlockSpec((1, GLA_HV), lambda n: (0, 0)),
            pl.BlockSpec(memory_space=pl.ANY),
        ],
        out_specs=[pl.BlockSpec((c, DV), lambda n: (n, DS // DV)),
                   pl.BlockSpec((cps, nh, GLA_HV, GLA_HK), lambda n: (n, 0, 0, 0))],
        input_output_aliases={8: 0},
        out_shape=[jax.ShapeDtypeStruct(ycat.shape, BF16), jax.ShapeDtypeStruct((nc, nh, GLA_HV, GLA_HK), F32)],
        scratch_shapes=[pltpu.VMEM((nh, GLA_HV, GLA_HK), F32)],
        compiler_params=pltpu.CompilerParams(dimension_semantics=("arbitrary",)),
    )(proj_main, proj_main, proj_main, proj_main, proj_low, gate_up_pad, gate_bias, norm_w, ycat)


def _gla_bwd(d_ycat, proj_main, proj_low, s_prev, gate_up_pad, gate_bias, norm_w, DS, DK, DV):
    L = proj_main.shape[0]
    nc = L // GLA_CHUNK
    cps = math.gcd(GLA_STEP_CHUNKS, nc)
    nh = DK // GLA_HK
    scale = GLA_HK ** -0.5

    def body(dy_ref, q_ref, k_ref, v_ref, z_ref, gl_ref, sp_ref, gu_ref, gb_ref, nw_ref,
             dg_ref, da_ref, gnw_ref, ggb_ref, dst_ref):
        n = pl.program_id(0)

        @pl.when(n == 0)
        def _():
            dst_ref[...] = jnp.zeros_like(dst_ref)
            gnw_ref[...] = jnp.zeros_like(gnw_ref)
            ggb_ref[...] = jnp.zeros_like(ggb_ref)

        last_row = lax.broadcasted_iota(jnp.int32, (GLA_CHUNK, GLA_HK), 0) == GLA_CHUNK - 1
        ri = lax.broadcasted_iota(jnp.int32, (GLA_CHUNK, GLA_CHUNK), 0)
        ci = lax.broadcasted_iota(jnp.int32, (GLA_CHUNK, GLA_CHUNK), 1)
        upper = (ci >= ri).astype(F32)
        nw = nw_ref[...]
        for sc in reversed(range(cps)):
            rs = slice(sc * GLA_CHUNK, (sc + 1) * GLA_CHUNK)
            a, b, b_last, mask = _gla_gates(gl_ref[rs, :], gu_ref, gb_ref)
            for h in range(nh):
                ks = slice(h * GLA_HK, (h + 1) * GLA_HK)
                vs = slice(h * GLA_HV, (h + 1) * GLA_HV)
                bh, bl = b[:, ks], b_last[:, ks]
                e = jnp.exp(bh)
                einv = jnp.exp(-bh)
                etail = jnp.exp(bl - bh)
                dec = jnp.exp(bl)
                qe = (q_ref[rs, ks] * scale) * e
                kh = k_ref[rs, ks]
                ke = kh * einv
                ktail = kh * etail
                vh = v_ref[rs, vs]
                st = sp_ref[sc, h]
                dst = dst_ref[h]
                attn = jnp.where(mask, _dot(qe, ke, NT), 0.0)
                o = _dot(attn, vh) + _dot(qe, st, NT)
                r = lax.rsqrt(jnp.mean(o * o, axis=-1, keepdims=True) + EPS)
                nrm = o * r
                z = z_ref[rs, vs]
                sz = _sigmoid(z)
                dy = dy_ref[rs, vs]
                dg_ref[rs, 2 * DK + DV + h * GLA_HV:2 * DK + DV + (h + 1) * GLA_HV] = (
                    dy * nrm * nw * (sz * (1.0 + z * (1.0 - sz)))).astype(BF16)
                d_on = dy * (z * sz)
                gnw_ref[...] += jnp.sum(d_on * nrm, axis=0, keepdims=True)
                d_n = d_on * nw
                d_o = r * (d_n - nrm * jnp.mean(d_n * nrm, axis=-1, keepdims=True))
                d_attn = jnp.where(mask, _dot(d_o, vh, NT), 0.0)
                dg_ref[rs, 2 * DK + h * GLA_HV:2 * DK + (h + 1) * GLA_HV] = (
                    _dot(attn, d_o, TN) + _dot(ktail, dst, NT)).astype(BF16)
                d_qe = _dot(d_attn, ke) + _dot(d_o, st)
                d_ke = _dot(d_attn, qe, TN)
                d_kt = _dot(vh, dst)
                d_dec = jnp.sum(dst * st, axis=0, keepdims=True)
                dst_ref[h] = dec * dst + _dot(d_o, qe, TN)
                dg_ref[rs, ks] = (d_qe * scale * e).astype(BF16)
                dg_ref[rs, DK + h * GLA_HK:DK + (h + 1) * GLA_HK] = (d_ke * einv + d_kt * etail).astype(BF16)
                d_bl = jnp.sum(d_kt * ktail, axis=0, keepdims=True) + d_dec * dec
                d_b = d_qe * qe - d_ke * ke - d_kt * ktail + jnp.where(last_row, d_bl, 0.0)
                d_lg = _dot_hi(upper, d_b)
                d_a = d_lg * (1.0 / GLA_TAU) * _sigmoid(-a[:, ks])
                ggb_ref[:, ks] += jnp.sum(d_a, axis=0, keepdims=True)
                da_ref[rs, ks] = d_a.astype(BF16)

    c = cps * GLA_CHUNK
    ns = nc // cps
    rn = lambda n: ns - 1 - n
    return pl.pallas_call(
        body, name="gla_bwd", grid=(ns,),
        in_specs=[pl.BlockSpec((c, DV), lambda n: (rn(n), DS // DV))] + _gla_specs(DS, DK, DV, c, rn) + [
            pl.BlockSpec((c, LANES), lambda n: (rn(n), 0)),
            pl.BlockSpec((cps, nh, GLA_HV, GLA_HK), lambda n: (rn(n), 0, 0, 0)),
            pl.BlockSpec((LANES, DK), lambda n: (0, 0)),
            pl.BlockSpec((1, DK), lambda n: (0, 0)),
            pl.BlockSpec((1, GLA_HV), lambda n: (0, 0)),
        ],
        out_specs=[pl.BlockSpec((c, 2 * DK + 2 * DV), lambda n: (rn(n), 0)),
                   pl.BlockSpec((c, DK), lambda n: (rn(n), 0)),
                   pl.BlockSpec((1, GLA_HV), lambda n: (0, 0)), pl.BlockSpec((1, DK), lambda n: (0, 0))],
        out_shape=[jax.ShapeDtypeStruct((L, 2 * DK + 2 * DV), BF16),
                   jax.ShapeDtypeStruct((L, DK), BF16),
                   jax.ShapeDtypeStruct((1, GLA_HV), F32), jax.ShapeDtypeStruct((1, DK), F32)],
        scratch_shapes=[pltpu.VMEM((nh, GLA_HV, GLA_HK), F32)],
        compiler_params=pltpu.CompilerParams(dimension_semantics=("arbitrary",)),
    )(d_ycat, proj_main, proj_main, proj_main, proj_main, proj_low, s_prev, gate_up_pad, gate_bias, norm_w)


def _adamw_math(w, g, m, v):
    c1 = 1.0 - ADAM_B1 ** ADAM_STEP
    c2 = 1.0 - ADAM_B2 ** ADAM_STEP
    m_ = ADAM_B1 * m + (1.0 - ADAM_B1) * g
    v_ = ADAM_B2 * v + (1.0 - ADAM_B2) * (g * g)
    return -ADAM_LR * ((m_ / c1) / (jnp.sqrt(v_ / c2) + ADAM_EPS) + ADAM_WD * w), m_, v_


def _adamw_small(g_row, g_a, g_bc, ws, ms, vs):
    n = len(ws)
    nvec = n - 6

    def body(*refs):
        grow_ref, ga_ref, gbc_ref = refs[:3]
        w_refs, m_refs, v_refs = refs[3:3 + n], refs[3 + n:3 + 2 * n], refs[3 + 2 * n:3 + 3 * n]
        outs = refs[3 + 3 * n:]
        off = 0
        for i in range(n):
            if i < nvec:
                width = ws[i].shape[1]
                g = grow_ref[:, off:off + width]
                off += width
            elif i < nvec + 2:
                g = ga_ref[i - nvec]
            else:
                g = gbc_ref[i - nvec - 2]
            d, m_, v_ = _adamw_math(w_refs[i][...], g, m_refs[i][...], v_refs[i][...])
            outs[i][...] = g
            outs[n + i][...] = d
            outs[2 * n + i][...] = m_
            outs[3 * n + i][...] = v_

    vm = pl.BlockSpec(memory_space=pltpu.VMEM)
    outs = pl.pallas_call(
        body, name="adamw_small",
        in_specs=[vm] * (3 + 3 * n), out_specs=[vm] * (4 * n),
        out_shape=[jax.ShapeDtypeStruct(w.shape, F32) for w in ws] * 4,
    )(g_row, g_a, g_bc, *ws, *ms, *vs)
    return [outs[k * n:(k + 1) * n] for k in range(4)]


def _my_pos():
    return lax.axis_index("x"), lax.axis_index("y"), lax.axis_index("c")


def _gather_weights(shards):
    n = len(shards)
    halves = [s.shape[0] // 2 for s in shards]

    def body(*refs):
        ins, outs = refs[:n], refs[n:2 * n]
        send_sems, recv_sems = refs[2 * n:]
        x, y, c = _my_pos()
        me = 2 * x + y

        def piece(a, chip, half):
            return outs[a].at[chip, pl.ds(half * halves[a], halves[a]), :]

        def copy(a, k, src_chip, half, to):
            sl = piece(a, src_chip, half)
            return pltpu.make_async_remote_copy(src_ref=sl, dst_ref=sl, send_sem=send_sems.at[a, k], recv_sem=recv_sems.at[a, k],
                                                device_id=to, device_id_type=MESH)

        def first(a, d, to):
            src = ins[a].at[pl.ds(c * halves[a], halves[a]), :]
            return pltpu.make_async_remote_copy(src_ref=src, dst_ref=piece(a, me, c), send_sem=send_sems.at[a, d - 1],
                                                recv_sem=recv_sems.at[a, d - 1], device_id=to, device_id_type=MESH)

        sent = []
        for d in (1, 2, 3):
            to = (x ^ (d >> 1), y ^ (d & 1), c)
            for a in range(n):
                cp = first(a, d, to)
                cp.start()
                sent.append(cp)
        for d in (1, 2, 3):
            chip = (x ^ (d >> 1)) * 2 + (y ^ (d & 1))
            for a in range(n):
                copy(a, d - 1, chip, c, (x, y, c)).wait_recv()
                fw = copy(a, 2 + d, chip, c, (x, y, 1 - c))
                fw.start()
                sent.append(fw)
        for d in (1, 2, 3):
            chip = (x ^ (d >> 1)) * 2 + (y ^ (d & 1))
            for a in range(n):
                copy(a, 2 + d, chip, 1 - c, (x, y, c)).wait_recv()
        for cp in sent:
            cp.wait_send()

    hbm = pl.BlockSpec(memory_space=pltpu.HBM)
    return pl.pallas_call(
        body, name="gather_weights",
        in_specs=[hbm] * n, out_specs=[hbm] * n,
        out_shape=[jax.ShapeDtypeStruct((4,) + s.shape, s.dtype) for s in shards],
        scratch_shapes=[pltpu.SemaphoreType.DMA((n, 6)), pltpu.SemaphoreType.DMA((n, 6))],
    )(*shards)


def _pair_exchange(gs):
    n = len(gs)

    def body(*refs):
        ins, outs = refs[:n], refs[n:2 * n]
        send_sems, recv_sems = refs[2 * n:]
        x, y, c = _my_pos()
        sent = []
        for a in range(n):
            hrows = gs[a].shape[1] // 2
            cp = pltpu.make_async_remote_copy(
                src_ref=ins[a].at[:, pl.ds((1 - c) * hrows, hrows), :], dst_ref=outs[a], send_sem=send_sems.at[a],
                recv_sem=recv_sems.at[a], device_id=(x, y, 1 - c), device_id_type=MESH)
            cp.start()
            sent.append(cp)
        for cp in sent:
            cp.wait()

    hbm = pl.BlockSpec(memory_space=pltpu.HBM)
    return pl.pallas_call(
        body, name="grad_pair_exchange", in_specs=[hbm] * n, out_specs=[hbm] * n,
        out_shape=[jax.ShapeDtypeStruct((g.shape[0], g.shape[1] // 2, g.shape[2]), g.dtype) for g in gs],
        scratch_shapes=[pltpu.SemaphoreType.DMA((n,)), pltpu.SemaphoreType.DMA((n,))],
    )(*gs)


def _pair_add(g, got, c_arr, name):
    nk, rows2, cols = g.shape
    hrows = rows2 // 2
    tr = _blk(hrows, 256, 2 * SUBLANES)
    nb = hrows // tr

    def body(c_ref, a_ref, b_ref, o_ref):
        o_ref[...] = (a_ref[...].astype(F32) + b_ref[...].astype(F32)).astype(o_ref.dtype)

    return pl.pallas_call(
        body, name=name,
        grid_spec=pltpu.PrefetchScalarGridSpec(
            num_scalar_prefetch=1, grid=(nk, nb),
            in_specs=[pl.BlockSpec((1, tr, cols), lambda k, i, c_ref: (k, c_ref[0] * nb + i, 0)),
                      pl.BlockSpec((1, tr, cols), lambda k, i, c_ref: (k, i, 0))],
            out_specs=pl.BlockSpec((1, tr, cols), lambda k, i, c_ref: (k, i, 0))),
        out_shape=jax.ShapeDtypeStruct((nk, hrows, cols), g.dtype),
        compiler_params=pltpu.CompilerParams(dimension_semantics=("parallel", "parallel")),
    )(c_arr, g, got)


def _chip_scatter_copies(srcs, lands, send_sems, recv_sems):
    x, y, c = _my_pos()
    copies = []
    for d in (1, 2, 3):
        tx, ty = x ^ (d >> 1), y ^ (d & 1)
        for a in range(len(srcs)):
            copies.append(pltpu.make_async_remote_copy(
                src_ref=srcs[a].at[2 * tx + ty], dst_ref=lands[a].at[d - 1], send_sem=send_sems.at[3 * a + d - 1],
                recv_sem=recv_sems.at[3 * a + d - 1], device_id=(tx, ty, c), device_id_type=MESH))
    return copies


def _chip_scatter_start(pss):
    n = len(pss)

    def body(*refs):
        srcs, lands = refs[:n], refs[n:2 * n]
        send_sems, recv_sems = refs[2 * n], refs[2 * n + 1]
        token = refs[-1]
        for cp in _chip_scatter_copies(srcs, lands, send_sems, recv_sems):
            cp.start()
        token[...] = jnp.zeros_like(token)

    hbm = pl.BlockSpec(memory_space=pltpu.HBM)
    sem = pl.BlockSpec(memory_space=pltpu.SEMAPHORE)
    land_shapes = [(3,) + p.shape[1:] for p in pss]
    outs = pl.pallas_call(
        body, name="grad_chip_scatter_start",
        in_specs=[hbm] * (2 * n),
        out_specs=[sem, sem] + [hbm] * (2 * n) + [pl.BlockSpec(memory_space=pltpu.VMEM)],
        out_shape=[pltpu.SemaphoreType.DMA((3 * n,)), pltpu.SemaphoreType.DMA((3 * n,))]
        + [pltpu.HBM(p.shape, p.dtype) for p in pss]
        + [pltpu.HBM(s, p.dtype) for s, p in zip(land_shapes, pss)]
        + [jax.ShapeDtypeStruct((SUBLANES, LANES), F32)],
        input_output_aliases={i: 2 + i for i in range(2 * n)},
        compiler_params=pltpu.CompilerParams(has_side_effects=pltpu.SideEffectType.DATAFLOW_SIDE_EFFECTING),
    )(*[pltpu.with_memory_space_constraint(p, pltpu.HBM) for p in pss],
      *[pltpu.with_memory_space_constraint(lax.empty(s, p.dtype), pltpu.HBM) for s, p in zip(land_shapes, pss)])
    return outs[0], outs[1], outs[2:2 + n], outs[2 + n:2 + 2 * n], outs[-1]


def _chip_scatter_wait(send_sems, recv_sems, srcs, lands, after):
    n = len(srcs)

    def body(*refs):
        src_refs, land_refs = refs[:n], refs[n:2 * n]
        ssem, rsem = refs[2 * n], refs[2 * n + 1]
        for cp in _chip_scatter_copies(src_refs, land_refs, ssem, rsem):
            cp.wait_send()
            cp.wait_recv()

    hbm = pl.BlockSpec(memory_space=pltpu.HBM)
    sem = pl.BlockSpec(memory_space=pltpu.SEMAPHORE)
    outs = pl.pallas_call(
        body, name="grad_chip_scatter_wait",
        in_specs=[hbm] * (2 * n) + [sem, sem, pl.BlockSpec(memory_space=pl.ANY)],
        out_specs=[hbm] * (2 * n),
        out_shape=[pltpu.HBM(p.shape, p.dtype) for p in srcs] + [pltpu.HBM(p.shape, p.dtype) for p in lands],
        input_output_aliases={i: i for i in range(2 * n)},
        compiler_params=pltpu.CompilerParams(has_side_effects=pltpu.SideEffectType.DATAFLOW_SIDE_EFFECTING),
    )(*srcs, *lands, send_sems, recv_sems, after)
    return outs[:n], outs[n:]


def _chip_sum(ps, got, me_arr, name):
    _, hrows, cols = ps.shape
    tr = _blk(hrows, 256, 2 * SUBLANES)

    def body(me_ref, p_ref, g_ref, o_ref):
        acc = p_ref[0].astype(F32)
        for s in range(3):
            acc = acc + g_ref[s].astype(F32)
        o_ref[...] = acc

    return pl.pallas_call(
        body, name=name,
        grid_spec=pltpu.PrefetchScalarGridSpec(
            num_scalar_prefetch=1, grid=(hrows // tr,),
            in_specs=[pl.BlockSpec((1, tr, cols), lambda i, me_ref: (me_ref[0], i, 0)),
                      pl.BlockSpec((3, tr, cols), lambda i, me_ref: (0, i, 0))],
            out_specs=pl.BlockSpec((tr, cols), lambda i, me_ref: (i, 0))),
        out_shape=jax.ShapeDtypeStruct((hrows, cols), F32),
        compiler_params=pltpu.CompilerParams(dimension_semantics=("parallel",)),
    )(me_arr, ps, got)


def _pair_swap(halves):
    n = len(halves)

    def body(*refs):
        ins, outs = refs[:n], refs[n:2 * n]
        send_sems, recv_sems = refs[2 * n:]
        x, y, c = _my_pos()
        sent = []
        for a in range(n):
            cp = pltpu.make_async_remote_copy(src_ref=ins[a], dst_ref=outs[a], send_sem=send_sems.at[a], recv_sem=recv_sems.at[a],
                                              device_id=(x, y, 1 - c), device_id_type=MESH)
            cp.start()
            sent.append(cp)
        for cp in sent:
            cp.wait()

    hbm = pl.BlockSpec(memory_space=pltpu.HBM)
    return pl.pallas_call(
        body, name="grad_pair_swap", in_specs=[hbm] * n, out_specs=[hbm] * n,
        out_shape=[jax.ShapeDtypeStruct(h.shape, h.dtype) for h in halves],
        scratch_shapes=[pltpu.SemaphoreType.DMA((n,)), pltpu.SemaphoreType.DMA((n,))],
    )(*halves)


def _adamw_sharded(w, g_own, g_other, m, v, c_arr, name):
    R, C = w.shape
    hrows = R // 2
    tr = _blk(hrows, 256, SUBLANES)
    nbh = hrows // tr
    c1 = 1.0 - ADAM_B1 ** ADAM_STEP
    c2 = 1.0 - ADAM_B2 ** ADAM_STEP

    def body(c_ref, w_ref, go_ref, gx_ref, m_ref, v_ref, g_ref, d_ref, nm_ref, nv_ref):
        mine = (pl.program_id(0) // nbh) == c_ref[0]
        g_ = jnp.where(mine, go_ref[...], gx_ref[...])
        g_ref[...] = g_
        m_ = ADAM_B1 * m_ref[...] + (1.0 - ADAM_B1) * g_
        v_ = ADAM_B2 * v_ref[...] + (1.0 - ADAM_B2) * (g_ * g_)
        nm_ref[...] = m_
        nv_ref[...] = v_
        d_ref[...] = -ADAM_LR * ((m_ / c1) / (jnp.sqrt(v_ / c2) + ADAM_EPS) + ADAM_WD * w_ref[...])

    blk = pl.BlockSpec((tr, C), lambda i, c_ref: (i, 0))
    hblk = pl.BlockSpec((tr, C), lambda i, c_ref: (i % nbh, 0))
    sd = jax.ShapeDtypeStruct((R, C), F32)
    return pl.pallas_call(
        body, name=name,
        grid_spec=pltpu.PrefetchScalarGridSpec(
            num_scalar_prefetch=1, grid=(2 * nbh,),
            in_specs=[blk, hblk, hblk, blk, blk], out_specs=[blk] * 4),
        out_shape=[sd] * 4,
        compiler_params=pltpu.CompilerParams(dimension_semantics=("parallel",)),
    )(c_arr, w, g_own, g_other, m, v)


def _allreduce_small(arrs):
    n = len(arrs)
    rows = [a.shape[-2] // 8 for a in arrs]

    def piece(ref, a, p):
        start = p * rows[a]
        if rows[a] % SUBLANES == 0:
            start = pl.multiple_of(start, SUBLANES)
        return ref.at[..., pl.ds(start, rows[a]), :]

    def body(*refs):
        v_refs, o_refs, got_refs = refs[:n], refs[n:2 * n], refs[2 * n:3 * n]
        send_sems, recv_sems = refs[3 * n:]
        x, y, c = _my_pos()
        me = 4 * x + 2 * y + c

        def peer(d):
            return (x ^ (d >> 2), y ^ ((d >> 1) & 1), c ^ (d & 1))

        def lin(p):
            return 4 * p[0] + 2 * p[1] + p[2]

        sent = []
        for d in range(1, 8):
            to = peer(d)
            for a in range(n):
                cp = pltpu.make_async_remote_copy(
                    src_ref=piece(v_refs[a], a, lin(to)), dst_ref=got_refs[a].at[d],
                    send_sem=send_sems.at[0, d * n + a], recv_sem=recv_sems.at[0, d * n + a], device_id=to, device_id_type=MESH)
                cp.start()
                sent.append(cp)
        for a in range(n):
            acc = piece(v_refs[a], a, me)[...]
            for d in range(1, 8):
                sent[(d - 1) * n + a].wait_recv()
                acc = acc + got_refs[a][d]
            got_refs[a][0] = acc
            piece(o_refs[a], a, me)[...] = acc
        for d in range(1, 8):
            for a in range(n):
                cp = pltpu.make_async_remote_copy(
                    src_ref=got_refs[a].at[0], dst_ref=piece(o_refs[a], a, me),
                    send_sem=send_sems.at[1, d * n + a], recv_sem=recv_sems.at[1, d * n + a], device_id=peer(d), device_id_type=MESH)
                cp.start()
                sent.append(cp)
        for d in range(1, 8):
            for a in range(n):
                pltpu.make_async_remote_copy(
                    src_ref=got_refs[a].at[0], dst_ref=piece(o_refs[a], a, lin(peer(d))),
                    send_sem=send_sems.at[1, d * n + a], recv_sem=recv_sems.at[1, d * n + a], device_id=peer(d),
                    device_id_type=MESH).wait_recv()
        for cp in sent:
            cp.wait_send()

    vm = pl.BlockSpec(memory_space=pltpu.VMEM)
    return pl.pallas_call(
        body, name="allreduce_small", in_specs=[vm] * n, out_specs=[vm] * n,
        out_shape=[jax.ShapeDtypeStruct(a.shape, F32) for a in arrs],
        scratch_shapes=[pltpu.VMEM((8,) + a.shape[:-2] + (r, a.shape[-1]), F32) for a, r in zip(arrs, rows)]
        + [pltpu.SemaphoreType.DMA((2, 8 * n)), pltpu.SemaphoreType.DMA((2, 8 * n))],
    )(*arrs)


def kernel(x, pre_norm_w, w_in, s5_A_re, s5_A_im, s5_B_re, s5_B_im, s5_C_re, s5_C_im, s5_D, s5_log_dt, s5_glu_w, s5_glu_b, gla_gate_up, gla_gate_bias, gla_norm_w, w_out, post_norm_w, loss_target, m_pre_norm_w, m_w_in, m_s5_A_re, m_s5_A_im, m_s5_B_re, m_s5_B_im, m_s5_C_re, m_s5_C_im, m_s5_D, m_s5_log_dt, m_s5_glu_w, m_s5_glu_b, m_gla_gate_up, m_gla_gate_bias, m_gla_norm_w, m_w_out, m_post_norm_w, v_pre_norm_w, v_w_in, v_s5_A_re, v_s5_A_im, v_s5_B_re, v_s5_B_im, v_s5_C_re, v_s5_C_im, v_s5_D, v_s5_log_dt, v_s5_glu_w, v_s5_glu_b, v_gla_gate_up, v_gla_gate_bias, v_gla_norm_w, v_w_out, v_post_norm_w):
    names = ["pre_norm_w", "w_in", "s5_A_re", "s5_A_im", "s5_B_re", "s5_B_im", "s5_C_re", "s5_C_im", "s5_D", "s5_log_dt",
             "s5_glu_w", "s5_glu_b", "gla_gate_up", "gla_gate_bias", "gla_norm_w", "w_out", "post_norm_w"]
    W = dict(zip(names, (pre_norm_w, w_in, s5_A_re, s5_A_im, s5_B_re, s5_B_im, s5_C_re, s5_C_im, s5_D, s5_log_dt,
                         s5_glu_w, s5_glu_b, gla_gate_up, gla_gate_bias, gla_norm_w, w_out, post_norm_w)))
    M = dict(zip(names, (m_pre_norm_w, m_w_in, m_s5_A_re, m_s5_A_im, m_s5_B_re, m_s5_B_im, m_s5_C_re, m_s5_C_im, m_s5_D,
                         m_s5_log_dt, m_s5_glu_w, m_s5_glu_b, m_gla_gate_up, m_gla_gate_bias, m_gla_norm_w, m_w_out,
                         m_post_norm_w)))
    V = dict(zip(names, (v_pre_norm_w, v_w_in, v_s5_A_re, v_s5_A_im, v_s5_B_re, v_s5_B_im, v_s5_C_re, v_s5_C_im, v_s5_D,
                         v_s5_log_dt, v_s5_glu_w, v_s5_glu_b, v_gla_gate_up, v_gla_gate_bias, v_gla_norm_w, v_w_out,
                         v_post_norm_w)))
    sharded = ("w_in", "s5_glu_w", "w_out", "gla_gate_up")

    xb = x[0]
    tgt = loss_target[0]
    L, D = xb.shape
    DS = D // 2
    G = DS // S5_GROUP
    P = S5_STATE
    NB = DS // S5_COLS
    DV = D - DS
    DK = DV // 2
    WM = 2 * DS + 2 * DK + 2 * DV
    nsh = w_in.shape[2]

    chip = 2 * lax.axis_index("x") + lax.axis_index("y")
    own = [w_in[0].astype(BF16), s5_glu_w[0].astype(BF16), w_out[0].astype(BF16), gla_gate_up[0]]
    g_win, g_glu, g_wout, g_gup = [lax.dynamic_update_index_in_dim(g, o, chip, 0)
                                   for g, o in zip(_gather_weights(own), own)]
    w_full = jnp.moveaxis(g_win, 0, 1).reshape(D, 4 * nsh)
    w_main = w_full[:, :WM]
    w_low = jnp.pad(w_full[:, WM:], ((0, 0), (0, LANES - GLA_RANK)))
    glu_w = g_glu.reshape(DS, DS)
    wout = g_wout.reshape(D, D)
    gup = jnp.moveaxis(g_gup, 0, 1).reshape(GLA_RANK, DK)
    gup_pad = jnp.pad(gup, ((0, LANES - GLA_RANK), (0, 0))).astype(BF16)

    b_view = lambda t: jnp.transpose(t[0], (0, 2, 1)).reshape(G * S5_GROUP, P)
    b_back = lambda t: jnp.transpose(t.reshape(G, S5_GROUP, P), (0, 2, 1))[None]
    c_view = lambda t: t[0].reshape(G * S5_GROUP, P)
    c_back = lambda t: t.reshape(1, G, S5_GROUP, P)
    small = ["pre_norm_w", "post_norm_w", "s5_D", "s5_glu_b", "gla_gate_bias", "gla_norm_w", "s5_log_dt",
             "s5_A_re", "s5_A_im", "s5_B_re", "s5_B_im", "s5_C_re", "s5_C_im"]
    view = {n: (lambda t: t) for n in small[:7]}
    back = dict(view)
    view.update(s5_A_re=lambda t: t[0], s5_A_im=lambda t: t[0], s5_B_re=b_view, s5_B_im=b_view, s5_C_re=c_view, s5_C_im=c_view)
    back.update(s5_A_re=lambda t: t[None], s5_A_im=lambda t: t[None], s5_B_re=b_back, s5_B_im=b_back, s5_C_re=c_back,
                s5_C_im=c_back)
    Wv = {n: view[n](W[n]) for n in small}
    bbd_re, bbd_im, ct_re, ct_im, tab = _s5_prep_fwd(Wv["s5_A_re"], Wv["s5_A_im"], s5_log_dt, Wv["s5_B_re"], Wv["s5_B_im"],
                                                     Wv["s5_C_re"], Wv["s5_C_im"])
    dvec = s5_D

    h = _prenorm_fwd(xb, pre_norm_w)
    proj_main = _mm(h, w_main, name="in_proj")
    proj_low = _mm(h, w_low, name="in_proj_low")
    y_pre, s_re, s_im = _s5_scan_fwd(proj_main, bbd_re, bbd_im, ct_re, ct_im, dvec, tab, DS)
    ycat, t_pre = _s5_post_fwd(y_pre, proj_main, glu_w, s5_glu_b, DS)
    ycat, s_prev = _gla_fwd(proj_main, proj_low, gup_pad, gla_gate_bias, gla_norm_w, ycat, DS, DK, DV)
    mixed = _mm(ycat, wout, name="out_proj")
    loss11, d_mixed, dout, g_post_w = _post_fwd_bwd(mixed, xb, tgt, post_norm_w)

    d_ycat = _mm(d_mixed, wout, tb=True, name="out_proj_dx")
    g_wout_full = _mm(ycat, d_mixed, ta=True, out_dtype=BF16, name="out_proj_dw")
    d_ypre, d_s5, d_t, y1, g_glu_b = _s5_post_bwd(d_ycat, y_pre, proj_main, t_pre, glu_w, DS)
    g_glu_full = _mm(y1, d_t, ta=True, out_dtype=BF16, name="glu_dw")
    d_s5, g_D, gct_re, gct_im, gbbd_re, gbbd_im, gab_re, gab_im = _s5_scan_bwd(
        d_ypre, proj_main, s_re, s_im, bbd_re, bbd_im, ct_re, ct_im, dvec, tab, d_s5, DS)
    d_gla, d_a, g_norm_w, g_gate_bias = _gla_bwd(
        d_ycat, proj_main, proj_low, s_prev, gup_pad, gla_gate_bias, gla_norm_w, DS, DK, DV)
    d_low = _mm(d_a, gup_pad, tb=True, out_dtype=BF16, name="gate_dx")
    g_gup_pad = _mm(proj_low, d_a, ta=True, name="gate_dw")
    g_wmain = _mm_nsplit(h, d_s5, d_gla, out_dtype=BF16, name="in_proj_dw")
    g_wlow = _mm(h, d_low, ta=True, out_dtype=BF16, name="in_proj_low_dw")

    g_win_full = jnp.concatenate([g_wmain, g_wlow[:, :GLA_RANK]], axis=1)
    gs = [jnp.moveaxis(g_win_full.reshape(D, 4, nsh), 1, 0),
          g_glu_full.reshape(4, DS // 4, DS),
          g_wout_full.reshape(4, D // 4, D),
          jnp.moveaxis(g_gup_pad[:GLA_RANK].reshape(GLA_RANK, 4, DK // 4), 1, 0)]
    c_arr = lax.axis_index("c").astype(jnp.int32).reshape(1)
    me_arr = chip.astype(jnp.int32).reshape(1)
    got = _pair_exchange(gs)
    pss = [_pair_add(g, r, c_arr, "grad_pair_add_" + n) for n, g, r in zip(sharded, gs, got)]
    send_sems, recv_sems, pss, lands, token = _chip_scatter_start(pss)

    dh_main = _mm_ksplit(d_s5, d_gla, w_main, token, name="in_proj_dx")
    dh_low = _mm(d_low, w_low, tb=True, name="in_proj_low_dx")
    grad_x, g_pre_w = _prenorm_bwd(xb, dh_main, dh_low, dout, pre_norm_w)
    pss, rcv = _chip_scatter_wait(send_sems, recv_sems, pss, lands, g_pre_w)

    g_a, g_bc, g_ldt = _s5_prep_bwd(Wv["s5_A_re"], Wv["s5_A_im"], s5_log_dt, Wv["s5_B_re"], Wv["s5_B_im"],
                                    gbbd_re, gbbd_im, gct_re, gct_im, gab_re, gab_im)

    loss = lax.psum(loss11[0, 0], ("x", "y", "c"))

    g_vecs = jnp.concatenate([g_pre_w, g_post_w, g_D, g_glu_b, g_gate_bias, g_norm_w, g_ldt], axis=1)
    lanes_pad = -g_vecs.shape[1] % (8 * SUBLANES * LANES)
    g_vecs = jnp.pad(g_vecs, ((0, 0), (0, lanes_pad))).reshape(-1, LANES)
    r_vecs, r_a, r_bc = _allreduce_small([g_vecs, g_a, g_bc])
    outs4 = _adamw_small(r_vecs.reshape(1, -1), r_a, r_bc, [Wv[n] for n in small],
                         [view[n](M[n]) for n in small], [view[n](V[n]) for n in small])
    G_out, D_out, M_out, V_out = [{n: back[n](t) for n, t in zip(small, o)} for o in outs4]

    halves = [_chip_sum(p, r, me_arr, "grad_chip_sum_" + n) for n, p, r in zip(sharded, pss, rcv)]
    others = _pair_swap(halves)
    for n, g_own, g_other in zip(sharded, halves, others):
        g_, d_, m_, v_ = _adamw_sharded(W[n][0], g_own, g_other, M[n][0], V[n][0], c_arr, "adamw_" + n)
        G_out[n], D_out[n], M_out[n], V_out[n] = g_[None], d_[None], m_[None], v_[None]

    return (loss, grad_x[None], *[G_out[n] for n in names], *[D_out[n] for n in names],
            *[M_out[n] for n in names], *[V_out[n] for n in names])
```

```python
import functools
import math

import jax
import jax.numpy as jnp
from jax import lax
from jax.experimental import pallas as pl
from jax.experimental.pallas import tpu as pltpu

F32 = jnp.float32
BF16 = jnp.bfloat16
HI = lax.Precision.HIGHEST
MESH = pl.DeviceIdType.MESH

EPS = 1e-6
S5_GROUP = 16
S5_STATE = 64
GLA_HK = 128
GLA_HV = 256
GLA_RANK = 16
GLA_TAU = 16.0
GLA_CHUNK = 64
GLA_STEP_CHUNKS = 2
LANES = 128
SUBLANES = 8
S5_COLS = 128
S5_LANES = (S5_COLS // S5_GROUP) * S5_STATE

ADAM_LR = 0.001
ADAM_B1 = 0.9
ADAM_B2 = 0.999
ADAM_EPS = 1e-08
ADAM_WD = 0.01
ADAM_STEP = 10

GELU_K = math.sqrt(2.0 / math.pi)
GELU_C = 0.044715


def _blk(n, pref, unit=LANES):
    best = None
    b = unit
    while b <= min(n, pref):
        if n % b == 0:
            best = b
        b += unit
    return best if best is not None else n


def _dot(a, b, dn=(((1,), (0,)), ((), ()))):
    return lax.dot_general(a.astype(BF16), b.astype(BF16), dn, preferred_element_type=F32)


def _dot_hi(a, b, dn=(((1,), (0,)), ((), ()))):
    return lax.dot_general(a, b, dn, precision=HI, preferred_element_type=F32)


NN = (((1,), (0,)), ((), ()))
NT = (((1,), (1,)), ((), ()))
TN = (((0,), (0,)), ((), ()))


def _sigmoid(x):
    return 1.0 / (1.0 + jnp.exp(-x))


def _gelu(y):
    return 0.5 * y * (1.0 + jnp.tanh(GELU_K * (y + GELU_C * y * y * y)))


def _gelu_grad(y):
    th = jnp.tanh(GELU_K * (y + GELU_C * y * y * y))
    return 0.5 * (1.0 + th) + 0.5 * y * (1.0 - th * th) * GELU_K * (1.0 + 3.0 * GELU_C * y * y)


def _mm(a, b, *, name, ta=False, tb=False, out_dtype=F32, bm=1024, bn=1024, bk=2048):
    if ta:
        K, M = a.shape
    else:
        M, K = a.shape
    if tb:
        N, K2 = b.shape
    else:
        K2, N = b.shape
    assert K == K2, (a.shape, b.shape, ta, tb)
    bm, bn, bk = _blk(M, bm), _blk(N, bn), _blk(K, bk)
    nk = K // bk
    dn = (((0 if ta else 1,), (1 if tb else 0,)), ((), ()))

    def body(a_ref, b_ref, o_ref, *acc):
        if nk == 1:
            o_ref[...] = _dot(a_ref[...], b_ref[...], dn).astype(out_dtype)
            return
        acc_ref, = acc
        k = pl.program_id(2)

        @pl.when(k == 0)
        def _():
            acc_ref[...] = jnp.zeros_like(acc_ref)

        acc_ref[...] += _dot(a_ref[...], b_ref[...], dn)

        @pl.when(k == nk - 1)
        def _():
            o_ref[...] = acc_ref[...].astype(out_dtype)

    a_spec = pl.BlockSpec((bk, bm), lambda i, j, k: (k, i)) if ta else pl.BlockSpec((bm, bk), lambda i, j, k: (i, k))
    b_spec = pl.BlockSpec((bn, bk), lambda i, j, k: (j, k)) if tb else pl.BlockSpec((bk, bn), lambda i, j, k: (k, j))
    return pl.pallas_call(
        body,
        name=name,
        grid=(M // bm, N // bn, nk),
        in_specs=[a_spec, b_spec],
        out_specs=pl.BlockSpec((bm, bn), lambda i, j, k: (i, j)),
        out_shape=jax.ShapeDtypeStruct((M, N), out_dtype),
        scratch_shapes=[pltpu.VMEM((bm, bn), F32)] if nk > 1 else [],
        compiler_params=pltpu.CompilerParams(dimension_semantics=("parallel", "parallel", "arbitrary")),
    )(a, b)


def _mm_ksplit(a1, a2, b, after, *, name, out_dtype=F32, bm=1024, bn=1024, bk=2048):
    M, K1 = a1.shape
    K2 = a2.shape[1]
    N = b.shape[0]
    bm, bn = _blk(M, bm), _blk(N, bn)
    bk = _blk(math.gcd(K1, K2), bk)
    nk1, nk = K1 // bk, (K1 + K2) // bk

    def body(a1_ref, a2_ref, b_ref, _after_ref, o_ref, acc_ref):
        k = pl.program_id(2)

        @pl.when(k == 0)
        def _():
            acc_ref[...] = jnp.zeros_like(acc_ref)

        @pl.when(k < nk1)
        def _():
            acc_ref[...] += _dot(a1_ref[...], b_ref[...], NT)

        @pl.when(k >= nk1)
        def _():
            acc_ref[...] += _dot(a2_ref[...], b_ref[...], NT)

        @pl.when(k == nk - 1)
        def _():
            o_ref[...] = acc_ref[...].astype(out_dtype)

    return pl.pallas_call(
        body, name=name, grid=(M // bm, N // bn, nk),
        in_specs=[pl.BlockSpec((bm, bk), lambda i, j, k: (i, jnp.minimum(k, nk1 - 1))),
                  pl.BlockSpec((bm, bk), lambda i, j, k: (i, jnp.maximum(k - nk1, 0))),
                  pl.BlockSpec((bn, bk), lambda i, j, k: (j, k)),
                  pl.BlockSpec(memory_space=pl.ANY)],
        out_specs=pl.BlockSpec((bm, bn), lambda i, j, k: (i, j)),
        out_shape=jax.ShapeDtypeStruct((M, N), out_dtype),
        scratch_shapes=[pltpu.VMEM((bm, bn), F32)],
        compiler_params=pltpu.CompilerParams(dimension_semantics=("parallel", "parallel", "arbitrary")),
    )(a1, a2, b, after)


def _mm_nsplit(a, b1, b2, *, name, out_dtype=F32, bm=1024, bn=1024, bk=2048):
    K, M = a.shape
    N1, N2 = b1.shape[1], b2.shape[1]
    bm, bk = _blk(M, bm), _blk(K, bk)
    bn = _blk(math.gcd(N1, N2), bn)
    nj1, nj = N1 // bn, (N1 + N2) // bn
    nk = K // bk

    def body(a_ref, b1_ref, b2_ref, o_ref, acc_ref):
        j = pl.program_id(1)
        k = pl.program_id(2)

        @pl.when(k == 0)
        def _():
            acc_ref[...] = jnp.zeros_like(acc_ref)

        @pl.when(j < nj1)
        def _():
            acc_ref[...] += _dot(a_ref[...], b1_ref[...], TN)

        @pl.when(j >= nj1)
        def _():
            acc_ref[...] += _dot(a_ref[...], b2_ref[...], TN)

        @pl.when(k == nk - 1)
        def _():
            o_ref[...] = acc_ref[...].astype(out_dtype)

    return pl.pallas_call(
        body, name=name, grid=(M // bm, nj, nk),
        in_specs=[pl.BlockSpec((bk, bm), lambda i, j, k: (k, i)),
                  pl.BlockSpec((bk, bn), lambda i, j, k: (jnp.where(j < nj1, k, nk - 1), jnp.minimum(j, nj1 - 1))),
                  pl.BlockSpec((bk, bn), lambda i, j, k: (jnp.where(j >= nj1, k, 0), jnp.maximum(j - nj1, 0)))],
        out_specs=pl.BlockSpec((bm, bn), lambda i, j, k: (i, j)),
        out_shape=jax.ShapeDtypeStruct((M, N1 + N2), out_dtype),
        scratch_shapes=[pltpu.VMEM((bm, bn), F32)],
        compiler_params=pltpu.CompilerParams(dimension_semantics=("parallel", "parallel", "arbitrary")),
    )(a, b1, b2)


def _prenorm_fwd(x, w):
    L, D = x.shape
    tr = _blk(L, 256, SUBLANES)

    def body(x_ref, w_ref, h_ref):
        xv = x_ref[...]
        r = lax.rsqrt(jnp.mean(xv * xv, axis=-1, keepdims=True) + EPS)
        h_ref[...] = (xv * r * w_ref[...]).astype(BF16)

    return pl.pallas_call(
        body, name="prenorm_fwd", grid=(L // tr,),
        in_specs=[pl.BlockSpec((tr, D), lambda i: (i, 0)), pl.BlockSpec((1, D), lambda i: (0, 0))],
        out_specs=pl.BlockSpec((tr, D), lambda i: (i, 0)),
        out_shape=jax.ShapeDtypeStruct((L, D), BF16),
        compiler_params=pltpu.CompilerParams(dimension_semantics=("parallel",)),
    )(x, w)


def _post_fwd_bwd(mixed, x, target, w):
    L, D = x.shape
    tr = _blk(L, 256, SUBLANES)
    nsteps = L // tr

    def body(mx_ref, x_ref, t_ref, w_ref, loss_ref, dm_ref, dout_ref, gw_ref, acc_ref):
        i = pl.program_id(0)

        @pl.when(i == 0)
        def _():
            acc_ref[...] = jnp.zeros_like(acc_ref)
            gw_ref[...] = jnp.zeros_like(gw_ref)

        mx = mx_ref[...]
        wv = w_ref[...]
        r = lax.rsqrt(jnp.mean(mx * mx, axis=-1, keepdims=True) + EPS)
        n = mx * r
        err = x_ref[...] + n * wv - t_ref[...]
        acc_ref[...] += jnp.sum(err * err, axis=0, keepdims=True)
        dout = err * (1.0 / D)
        dout_ref[...] = dout
        gw_ref[...] += jnp.sum(dout * n, axis=0, keepdims=True)
        dn = dout * wv
        dm_ref[...] = (r * (dn - n * jnp.mean(dn * n, axis=-1, keepdims=True))).astype(BF16)

        @pl.when(i == nsteps - 1)
        def _():
            loss_ref[...] = jnp.sum(acc_ref[...], axis=-1, keepdims=True) * (0.5 / D)

    row = pl.BlockSpec((tr, D), lambda i: (i, 0))
    vec = pl.BlockSpec((1, D), lambda i: (0, 0))
    return pl.pallas_call(
        body, name="post_fwd_bwd", grid=(nsteps,),
        in_specs=[row, row, row, vec],
        out_specs=[pl.BlockSpec((1, 1), lambda i: (0, 0)), row, row, vec],
        out_shape=[jax.ShapeDtypeStruct((1, 1), F32), jax.ShapeDtypeStruct((L, D), BF16),
                   jax.ShapeDtypeStruct((L, D), F32), jax.ShapeDtypeStruct((1, D), F32)],
        scratch_shapes=[pltpu.VMEM((1, D), F32)],
        compiler_params=pltpu.CompilerParams(dimension_semantics=("arbitrary",)),
    )(mixed, x, target, w)


def _prenorm_bwd(x, dh_main, dh_low, dout, w):
    L, D = x.shape
    tr = _blk(L, 256, SUBLANES)

    def body(x_ref, a_ref, b_ref, dout_ref, w_ref, gx_ref, gw_ref):
        i = pl.program_id(0)

        @pl.when(i == 0)
        def _():
            gw_ref[...] = jnp.zeros_like(gw_ref)

        xv = x_ref[...]
        r = lax.rsqrt(jnp.mean(xv * xv, axis=-1, keepdims=True) + EPS)
        n = xv * r
        dh = a_ref[...] + b_ref[...]
        gw_ref[...] += jnp.sum(dh * n, axis=0, keepdims=True)
        dn = dh * w_ref[...]
        gx_ref[...] = dout_ref[...] + r * (dn - n * jnp.mean(dn * n, axis=-1, keepdims=True))

    row = pl.BlockSpec((tr, D), lambda i: (i, 0))
    vec = pl.BlockSpec((1, D), lambda i: (0, 0))
    return pl.pallas_call(
        body, name="prenorm_bwd", grid=(L // tr,),
        in_specs=[row, row, row, row, vec],
        out_specs=[row, vec],
        out_shape=[jax.ShapeDtypeStruct((L, D), F32), jax.ShapeDtypeStruct((1, D), F32)],
        compiler_params=pltpu.CompilerParams(dimension_semantics=("arbitrary",)),
    )(x, dh_main, dh_low, dout, w)


def _s5_disc(a_re_raw, a_im, dt):
    a_re = jnp.minimum(a_re_raw, -1e-4)
    mag = jnp.exp(a_re * dt)
    ph = a_im * dt
    ab_re = mag * jnp.cos(ph)
    ab_im = mag * jnp.sin(ph)
    inv_n = 1.0 / (a_re * a_re + a_im * a_im)
    ia_re = a_re * inv_n
    ia_im = -a_im * inv_n
    n_re = ab_re - 1.0
    f_re = n_re * ia_re - ab_im * ia_im
    f_im = n_re * ia_im + ab_im * ia_re
    return a_re, ab_re, ab_im, f_re, f_im, ia_re, ia_im


def _iota2(shape, dim):
    return lax.broadcasted_iota(jnp.int32, shape, dim)


def _group_mask(rows, rows_per_group):
    shift = rows_per_group.bit_length() - 1
    return (_iota2((rows, S5_LANES), 0) >> shift) == (_iota2((rows, S5_LANES), 1) >> (S5_STATE.bit_length() - 1))


def _lane_tiler(dtype):
    return ((_iota2((S5_STATE, S5_LANES), 1) & (S5_STATE - 1)) == _iota2((S5_STATE, S5_LANES), 0)).astype(dtype)


def _row_to_col(row, n):
    eye = (_iota2((n, n), 0) == _iota2((n, n), 1)).astype(F32)
    return jnp.sum(eye * row, axis=1, keepdims=True)


def _group_repeat(G):
    return ((_iota2((G * S5_GROUP, G), 0) >> (S5_GROUP.bit_length() - 1)) == _iota2((G * S5_GROUP, G), 1)).astype(F32)


def _s5_prep_fwd(a_re, a_im, log_dt, b_re, b_im, c_re, c_im):
    G, P = a_re.shape
    nb = G * S5_GROUP // S5_COLS
    g8 = S5_COLS // S5_GROUP

    def body(are_ref, aim_ref, ldt_ref, bre_ref, bim_ref, cre_ref, cim_ref,
             bbre_ref, bbim_ref, ctre_ref, ctim_ref, tab_ref):
        dt = jnp.exp(_row_to_col(ldt_ref[...], G))
        _, ab_re, ab_im, f_re, f_im, _, _ = _s5_disc(are_ref[...], aim_ref[...], dt)
        rep = _group_repeat(G)
        fx_re = _dot_hi(rep, f_re)
        fx_im = _dot_hi(rep, f_im)
        br, bi = bre_ref[...], bim_ref[...]
        bb_re = fx_re * br - fx_im * bi
        bb_im = fx_re * bi + fx_im * br
        tile_bf = _lane_tiler(BF16)
        mask = _group_mask(S5_COLS, S5_GROUP)
        for jb in range(nb):
            rs = slice(jb * S5_COLS, (jb + 1) * S5_COLS)
            for src, dst in ((bb_re[rs], bbre_ref), (bb_im[rs], bbim_ref), (cre_ref[rs, :], ctre_ref), (cim_ref[rs, :], ctim_ref)):
                dst[jb] = jnp.where(mask, _dot(src, tile_bf), 0.0).astype(BF16)

        pw = [(ab_re, ab_im)]
        for _ in range(1, SUBLANES):
            pr, pi = pw[-1]
            pw.append((pr * ab_re - pi * ab_im, pr * ab_im + pi * ab_re))
        tile_f = _lane_tiler(F32)
        mask8 = _group_mask(g8, 1)
        row = _iota2((SUBLANES, S5_LANES), 0)
        for jb in range(nb):
            gs = slice(jb * g8, (jb + 1) * g8)

            def lanes(m):
                return jnp.sum(jnp.where(mask8, _dot_hi(m[gs], tile_f), 0.0), axis=0, keepdims=True)

            vec = [(lanes(r), lanes(i)) for r, i in pw]
            for lvl, k in enumerate((1, 2, 4)):
                tab_ref[jb, 2 * lvl] = jnp.where(row >= k, vec[k - 1][0], 0.0)
                tab_ref[jb, 2 * lvl + 1] = jnp.where(row >= k, vec[k - 1][1], 0.0)
                tab_ref[jb, 8 + 2 * lvl] = jnp.where(row < SUBLANES - k, vec[k - 1][0], 0.0)
                tab_ref[jb, 9 + 2 * lvl] = jnp.where(row < SUBLANES - k, -vec[k - 1][1], 0.0)
            f_r = f_i = r_r = r_i = jnp.zeros((SUBLANES, S5_LANES), F32)
            for i in range(SUBLANES):
                f_r = jnp.where(row == i, vec[i][0], f_r)
                f_i = jnp.where(row == i, vec[i][1], f_i)
                r_r = jnp.where(row == i, vec[SUBLANES - 1 - i][0], r_r)
                r_i = jnp.where(row == i, -vec[SUBLANES - 1 - i][1], r_i)
            tab_ref[jb, 6] = f_r
            tab_ref[jb, 7] = f_i
            tab_ref[jb, 14] = r_r
            tab_ref[jb, 15] = r_i

    vm = pl.BlockSpec(memory_space=pltpu.VMEM)
    bd = jax.ShapeDtypeStruct((nb, S5_COLS, S5_LANES), BF16)
    return pl.pallas_call(
        body, name="s5_prep_fwd",
        in_specs=[vm] * 7, out_specs=[vm] * 5,
        out_shape=[bd, bd, bd, bd, jax.ShapeDtypeStruct((nb, 16, SUBLANES, S5_LANES), F32)],
    )(a_re, a_im, log_dt, b_re, b_im, c_re, c_im)


def _s5_prep_bwd(a_re, a_im, log_dt, b_re, b_im, gbb_re, gbb_im, gct_re, gct_im, gab_re, gab_im):
    G, P = a_re.shape
    nb = G * S5_GROUP // S5_COLS
    g8 = S5_COLS // S5_GROUP

    def body(are_ref, aim_ref, ldt_ref, bre_ref, bim_ref, gbr_ref, gbi_ref, gcr_ref, gci_ref, gar_ref, gai_ref,
             o_a, o_bc, o_ldt):
        dt = jnp.exp(_row_to_col(ldt_ref[...], G))
        a_raw = are_ref[...]
        a_imv = aim_ref[...]
        a_re_c, ab_re, ab_im, f_re, f_im, ia_re, ia_im = _s5_disc(a_raw, a_imv, dt)
        tile_f = _lane_tiler(F32)
        mask = _group_mask(S5_COLS, S5_GROUP)
        mask8 = _group_mask(g8, 1)
        for jb in range(nb):
            rs = slice(jb * S5_COLS, (jb + 1) * S5_COLS)
            gs = slice(jb * g8, (jb + 1) * g8)
            ls = slice(jb * S5_LANES, (jb + 1) * S5_LANES)
            for k, src in enumerate((gbr_ref, gbi_ref, gcr_ref, gci_ref)):
                o_bc[k, rs, :] = _dot_hi(jnp.where(mask, src[jb], 0.0), tile_f, NT)
            for k, src in enumerate((gar_ref, gai_ref)):
                o_a[k, gs, :] = _dot_hi(jnp.where(mask8, src[:, ls], 0.0), tile_f, NT)
        rep = _group_repeat(G)
        fx_re = _dot_hi(rep, f_re)
        fx_im = _dot_hi(rep, f_im)
        gbr, gbi = o_bc[0], o_bc[1]
        br, bi = bre_ref[...], bim_ref[...]
        o_bc[0] = fx_re * gbr + fx_im * gbi
        o_bc[1] = fx_re * gbi - fx_im * gbr
        gf_re = _dot_hi(rep, br * gbr + bi * gbi, TN)
        gf_im = _dot_hi(rep, br * gbi - bi * gbr, TN)
        gab_r = o_a[0] + ia_re * gf_re + ia_im * gf_im
        gab_i = o_a[1] + ia_re * gf_im - ia_im * gf_re
        q_re = f_re * ia_re - f_im * ia_im
        q_im = f_re * ia_im + f_im * ia_re
        ga_re = -(q_re * gf_re + q_im * gf_im)
        ga_im = -(q_re * gf_im - q_im * gf_re)
        gth_re = ab_re * gab_r + ab_im * gab_i
        gth_im = ab_re * gab_i - ab_im * gab_r
        ga_re = ga_re + dt * gth_re
        ga_im = ga_im + dt * gth_im
        gdt = jnp.sum(a_re_c * gth_re + a_imv * gth_im, axis=-1, keepdims=True)
        eye = (_iota2((G, G), 0) == _iota2((G, G), 1)).astype(F32)
        o_ldt[...] = jnp.sum(eye * (gdt * dt), axis=0, keepdims=True)
        slope = jnp.where(a_raw < -1e-4, 1.0, jnp.where(a_raw == -1e-4, 0.5, 0.0))
        o_a[0] = ga_re * slope
        o_a[1] = ga_im

    vm = pl.BlockSpec(memory_space=pltpu.VMEM)
    return pl.pallas_call(
        body, name="s5_prep_bwd",
        in_specs=[vm] * 11, out_specs=[vm] * 3,
        out_shape=[jax.ShapeDtypeStruct((2, G, P), F32), jax.ShapeDtypeStruct((4, G * S5_GROUP, P), F32),
                   jax.ShapeDtypeStruct((1, G), F32)],
    )(a_re, a_im, log_dt, b_re, b_im, gbb_re, gbb_im, gct_re, gct_im, gab_re, gab_im)


def _scan8(xr, xi, tab_ref, base, shifts):
    for lvl, sh in enumerate(shifts):
        mr = tab_ref[0, base + 2 * lvl]
        mi = tab_ref[0, base + 2 * lvl + 1]
        ar = pltpu.roll(xr, sh, 0)
        ai = pltpu.roll(xi, sh, 0)
        xr, xi = xr + mr * ar - mi * ai, xi + mr * ai + mi * ar
    return xr, xi


def _s5_scan_fwd(proj_main, bbd_re, bbd_im, cbd_re, cbd_im, dvec, tab, DS):
    L = proj_main.shape[0]
    nb = DS // S5_COLS
    tb = _blk(L, 512, SUBLANES)
    nt = L // tb
    ng = tb // SUBLANES

    def body(u_ref, bre_ref, bim_ref, cre_ref, cim_ref, d_ref, tab_ref, y_ref, sre_ref, sim_ref, car_ref):
        t = pl.program_id(1)

        @pl.when(t == 0)
        def _():
            car_ref[...] = jnp.zeros_like(car_ref)

        u = u_ref[...]
        sre_ref[...] = _dot(u, bre_ref[0])
        sim_ref[...] = _dot(u, bim_ref[0])

        def grp(r, carry):
            cr, ci = carry
            off = pl.multiple_of(r * SUBLANES, SUBLANES)
            xr, xi = _scan8(sre_ref[pl.ds(off, SUBLANES), :], sim_ref[pl.ds(off, SUBLANES), :], tab_ref, 0, (1, 2, 4))
            pr, pi = tab_ref[0, 6], tab_ref[0, 7]
            xr, xi = xr + pr * cr - pi * ci, xi + pr * ci + pi * cr
            sre_ref[pl.ds(off, SUBLANES), :] = xr
            sim_ref[pl.ds(off, SUBLANES), :] = xi
            return (jnp.broadcast_to(xr[SUBLANES - 1:SUBLANES, :], xr.shape),
                    jnp.broadcast_to(xi[SUBLANES - 1:SUBLANES, :], xi.shape))

        cr, ci = lax.fori_loop(0, ng, grp, (car_ref[0], car_ref[1]))
        car_ref[0] = cr
        car_ref[1] = ci
        y_ref[...] = _dot(sre_ref[...], cre_ref[0], NT) - _dot(sim_ref[...], cim_ref[0], NT) + d_ref[...] * u

    return pl.pallas_call(
        body, name="s5_scan_fwd", grid=(nb, nt),
        in_specs=[
            pl.BlockSpec((tb, S5_COLS), lambda j, t: (t, j)),
            pl.BlockSpec((1, S5_COLS, S5_LANES), lambda j, t: (j, 0, 0)),
            pl.BlockSpec((1, S5_COLS, S5_LANES), lambda j, t: (j, 0, 0)),
            pl.BlockSpec((1, S5_COLS, S5_LANES), lambda j, t: (j, 0, 0)),
            pl.BlockSpec((1, S5_COLS, S5_LANES), lambda j, t: (j, 0, 0)),
            pl.BlockSpec((1, S5_COLS), lambda j, t: (0, j)),
            pl.BlockSpec((1, 16, SUBLANES, S5_LANES), lambda j, t: (j, 0, 0, 0)),
        ],
        out_specs=[
            pl.BlockSpec((tb, S5_COLS), lambda j, t: (t, j)),
            pl.BlockSpec((tb, S5_LANES), lambda j, t: (t, j)),
            pl.BlockSpec((tb, S5_LANES), lambda j, t: (t, j)),
        ],
        out_shape=[jax.ShapeDtypeStruct((L, DS), F32),
                   jax.ShapeDtypeStruct((L, nb * S5_LANES), F32),
                   jax.ShapeDtypeStruct((L, nb * S5_LANES), F32)],
        scratch_shapes=[pltpu.VMEM((2, SUBLANES, S5_LANES), F32)],
        compiler_params=pltpu.CompilerParams(dimension_semantics=("parallel", "arbitrary")),
    )(proj_main, bbd_re, bbd_im, cbd_re, cbd_im, dvec, tab)


def _s5_scan_bwd(dy, proj_main, s_re, s_im, bbd_re, bbd_im, cbd_re, cbd_im, dvec, tab, d_s5, DS):
    L = proj_main.shape[0]
    nb = DS // S5_COLS
    tb = _blk(L, 512, SUBLANES)
    nt = L // tb
    ng = tb // SUBLANES
    tb8 = tb // SUBLANES

    def body(dy_ref, u_ref, sre_ref, sim_ref, pre_ref, pim_ref, bre_ref, bim_ref, cre_ref, cim_ref, d_ref, tab_ref, _ds5_ref,
             du_ref, gd_ref, gcre_ref, gcim_ref, gbre_ref, gbim_ref, gare_ref, gaim_ref,
             lre_ref, lim_ref, car_ref):
        t = pl.program_id(1)

        @pl.when(t == 0)
        def _():
            car_ref[...] = jnp.zeros_like(car_ref)
            gd_ref[...] = jnp.zeros_like(gd_ref)
            gcre_ref[...] = jnp.zeros_like(gcre_ref)
            gcim_ref[...] = jnp.zeros_like(gcim_ref)
            gbre_ref[...] = jnp.zeros_like(gbre_ref)
            gbim_ref[...] = jnp.zeros_like(gbim_ref)
            gare_ref[...] = jnp.zeros_like(gare_ref)
            gaim_ref[...] = jnp.zeros_like(gaim_ref)

        dyv = dy_ref[...]
        u = u_ref[...]
        gd_ref[...] += jnp.sum(dyv * u, axis=0, keepdims=True)
        lre_ref[...] = _dot(dyv, cre_ref[0])
        lim_ref[...] = -_dot(dyv, cim_ref[0])
        gcre_ref[0] += _dot(dyv, sre_ref[...], TN)
        gcim_ref[0] -= _dot(dyv, sim_ref[...], TN)

        first = (t == nt - 1).astype(F32)
        head_re = pre_ref[...] * (1.0 - first)
        head_im = pim_ref[...] * (1.0 - first)
        row0 = lax.broadcasted_iota(jnp.int32, (SUBLANES, S5_LANES), 0) == 0

        def grp(i, carry):
            cr, ci, acc_re, acc_im = carry
            r = ng - 1 - i
            off = pl.multiple_of(r * SUBLANES, SUBLANES)
            xr, xi = _scan8(lre_ref[pl.ds(off, SUBLANES), :], lim_ref[pl.ds(off, SUBLANES), :], tab_ref, 8, (7, 6, 4))
            pr, pi = tab_ref[0, 14], tab_ref[0, 15]
            xr, xi = xr + pr * cr - pi * ci, xi + pr * ci + pi * cr
            lre_ref[pl.ds(off, SUBLANES), :] = xr
            lim_ref[pl.ds(off, SUBLANES), :] = xi
            poff = pl.multiple_of(jnp.maximum(r - 1, 0) * SUBLANES, SUBLANES)
            prev_re = jnp.where(r == 0, head_re, sre_ref[pl.ds(poff, SUBLANES), :])
            prev_im = jnp.where(r == 0, head_im, sim_ref[pl.ds(poff, SUBLANES), :])
            prev_re = jnp.broadcast_to(prev_re[SUBLANES - 1:SUBLANES, :], xr.shape)
            prev_im = jnp.broadcast_to(prev_im[SUBLANES - 1:SUBLANES, :], xi.shape)
            sp_re = jnp.where(row0, prev_re, pltpu.roll(sre_ref[pl.ds(off, SUBLANES), :], 1, 0))
            sp_im = jnp.where(row0, prev_im, pltpu.roll(sim_ref[pl.ds(off, SUBLANES), :], 1, 0))
            acc_re = acc_re + sp_re * xr + sp_im * xi
            acc_im = acc_im + sp_re * xi - sp_im * xr
            return (jnp.broadcast_to(xr[0:1, :], xr.shape), jnp.broadcast_to(xi[0:1, :], xi.shape), acc_re, acc_im)

        zero = jnp.zeros((SUBLANES, S5_LANES), F32)
        cr, ci, acc_re, acc_im = lax.fori_loop(0, ng, grp, (car_ref[0], car_ref[1], zero, zero))
        car_ref[0] = cr
        car_ref[1] = ci
        gare_ref[...] += jnp.sum(acc_re, axis=0, keepdims=True)
        gaim_ref[...] += jnp.sum(acc_im, axis=0, keepdims=True)
        lre = lre_ref[...]
        lim = lim_ref[...]
        du = dyv * d_ref[...] + _dot(lre, bre_ref[0], NT) + _dot(lim, bim_ref[0], NT)
        du_ref[...] = du.astype(BF16)
        gbre_ref[0] += _dot(u, lre, TN)
        gbim_ref[0] += _dot(u, lim, TN)

    rt = lambda t: nt - 1 - t
    col = pl.BlockSpec((tb, S5_COLS), lambda j, t: (rt(t), j))
    st = pl.BlockSpec((tb, S5_LANES), lambda j, t: (rt(t), j))
    prev = pl.BlockSpec((SUBLANES, S5_LANES), lambda j, t: (jnp.maximum(rt(t) * tb8 - 1, 0), j))
    bmat = pl.BlockSpec((1, S5_COLS, S5_LANES), lambda j, t: (j, 0, 0))
    cmat = bmat
    return pl.pallas_call(
        body, name="s5_scan_bwd", grid=(nb, nt),
        in_specs=[col, col, st, st, prev, prev, bmat, bmat, cmat, cmat,
                  pl.BlockSpec((1, S5_COLS), lambda j, t: (0, j)),
                  pl.BlockSpec((1, 16, SUBLANES, S5_LANES), lambda j, t: (j, 0, 0, 0)),
                  pl.BlockSpec(memory_space=pl.ANY)],
        out_specs=[col, pl.BlockSpec((1, S5_COLS), lambda j, t: (0, j)), cmat, cmat, bmat, bmat,
                   pl.BlockSpec((1, S5_LANES), lambda j, t: (0, j)), pl.BlockSpec((1, S5_LANES), lambda j, t: (0, j))],
        input_output_aliases={12: 0},
        out_shape=[jax.ShapeDtypeStruct((L, 2 * DS), BF16), jax.ShapeDtypeStruct((1, DS), F32),
                   jax.ShapeDtypeStruct((nb, S5_COLS, S5_LANES), F32), jax.ShapeDtypeStruct((nb, S5_COLS, S5_LANES), F32),
                   jax.ShapeDtypeStruct((nb, S5_COLS, S5_LANES), F32), jax.ShapeDtypeStruct((nb, S5_COLS, S5_LANES), F32),
                   jax.ShapeDtypeStruct((1, nb * S5_LANES), F32), jax.ShapeDtypeStruct((1, nb * S5_LANES), F32)],
        scratch_shapes=[pltpu.VMEM((tb, S5_LANES), F32), pltpu.VMEM((tb, S5_LANES), F32),
                        pltpu.VMEM((2, SUBLANES, S5_LANES), F32)],
        compiler_params=pltpu.CompilerParams(dimension_semantics=("parallel", "arbitrary")),
    )(dy, proj_main, s_re, s_im, s_re, s_im, bbd_re, bbd_im, cbd_re, cbd_im, dvec, tab, d_s5)


def _s5_post_fwd(y_pre, proj_main, glu_w, glu_b, DS):
    L = y_pre.shape[0]
    tr = _blk(L, 256, SUBLANES)

    def body(y_ref, z_ref, w_ref, b_ref, o_ref, t_ref):
        y1 = _gelu(y_ref[...])
        t = _dot(y1, w_ref[...]) + b_ref[...]
        t_ref[...] = t
        z = z_ref[...]
        o_ref[...] = (y1 * _sigmoid(t) * (z * _sigmoid(z))).astype(BF16)

    row = pl.BlockSpec((tr, DS), lambda i: (i, 0))
    return pl.pallas_call(
        body, name="s5_post_fwd", grid=(L // tr,),
        in_specs=[row, pl.BlockSpec((tr, DS), lambda i: (i, 1)), pl.BlockSpec((DS, DS), lambda i: (0, 0)),
                  pl.BlockSpec((1, DS), lambda i: (0, 0))],
        out_specs=[row, row],
        out_shape=[jax.ShapeDtypeStruct((L, 2 * DS), BF16), jax.ShapeDtypeStruct((L, DS), F32)],
        compiler_params=pltpu.CompilerParams(dimension_semantics=("parallel",)),
    )(y_pre, proj_main, glu_w, glu_b)


def _s5_post_bwd(d_ycat, y_pre, proj_main, t_pre, glu_w, DS):
    L = y_pre.shape[0]
    tr = _blk(L, 256, SUBLANES)

    def body(dy_ref, y_ref, z_ref, t_ref, w_ref, dyp_ref, dz_ref, dt_ref, y1_ref, gb_ref):
        i = pl.program_id(0)

        @pl.when(i == 0)
        def _():
            gb_ref[...] = jnp.zeros_like(gb_ref)

        dy = dy_ref[...]
        yp = y_ref[...]
        z = z_ref[...]
        y1 = _gelu(yp)
        sg = _sigmoid(t_ref[...])
        sz = _sigmoid(z)
        c = y1 * sg
        d_c = dy * (z * sz)
        dz_ref[...] = (dy * c * (sz * (1.0 + z * (1.0 - sz)))).astype(BF16)
        d_t = d_c * y1 * sg * (1.0 - sg)
        gb_ref[...] += jnp.sum(d_t, axis=0, keepdims=True)
        dt_ref[...] = d_t.astype(BF16)
        y1_ref[...] = y1.astype(BF16)
        d_y1 = d_c * sg + _dot(d_t, w_ref[...], NT)
        dyp_ref[...] = d_y1 * _gelu_grad(yp)

    row = pl.BlockSpec((tr, DS), lambda i: (i, 0))
    return pl.pallas_call(
        body, name="s5_post_bwd", grid=(L // tr,),
        in_specs=[row, row, pl.BlockSpec((tr, DS), lambda i: (i, 1)), row, pl.BlockSpec((DS, DS), lambda i: (0, 0))],
        out_specs=[row, pl.BlockSpec((tr, DS), lambda i: (i, 1)), row, row, pl.BlockSpec((1, DS), lambda i: (0, 0))],
        out_shape=[jax.ShapeDtypeStruct((L, DS), F32), jax.ShapeDtypeStruct((L, 2 * DS), BF16),
                   jax.ShapeDtypeStruct((L, DS), BF16), jax.ShapeDtypeStruct((L, DS), BF16),
                   jax.ShapeDtypeStruct((1, DS), F32)],
        compiler_params=pltpu.CompilerParams(dimension_semantics=("arbitrary",)),
    )(d_ycat, y_pre, proj_main, t_pre, glu_w)


def _gla_gates(glow, gu_ref, gb_ref):
    a = _dot(glow, gu_ref[...]) + gb_ref[...]
    lg = (jnp.minimum(a, 0.0) - jnp.log(1.0 + jnp.exp(-jnp.abs(a)))) * (1.0 / GLA_TAU)
    ri = lax.broadcasted_iota(jnp.int32, (GLA_CHUNK, GLA_CHUNK), 0)
    ci = lax.broadcasted_iota(jnp.int32, (GLA_CHUNK, GLA_CHUNK), 1)
    b = _dot_hi((ri >= ci).astype(F32), lg)
    b_last = jnp.sum(lg, axis=0, keepdims=True)
    return a, b, b_last, ri >= ci


def _gla_specs(DS, DK, DV, c, cmap):
    return [
        pl.BlockSpec((c, DK), lambda n: (cmap(n), 2 * DS // DK)),
        pl.BlockSpec((c, DK), lambda n: (cmap(n), 2 * DS // DK + 1)),
        pl.BlockSpec((c, DV), lambda n: (cmap(n), (2 * DS + 2 * DK) // DV)),
        pl.BlockSpec((c, DV), lambda n: (cmap(n), (2 * DS + 2 * DK) // DV + 1)),
    ]


def _gla_fwd(proj_main, proj_low, gate_up_pad, gate_bias, norm_w, ycat, DS, DK, DV):
    L = proj_main.shape[0]
    nc = L // GLA_CHUNK
    cps = math.gcd(GLA_STEP_CHUNKS, nc)
    nh = DK // GLA_HK
    scale = GLA_HK ** -0.5

    def body(q_ref, k_ref, v_ref, z_ref, gl_ref, gu_ref, gb_ref, nw_ref, _yc_ref, y_ref, sp_ref, st_ref):
        n = pl.program_id(0)

        @pl.when(n == 0)
        def _():
            st_ref[...] = jnp.zeros_like(st_ref)

        pairs = [(sc, h) for sc in range(cps) for h in range(nh)]
        rows = lambda sc: slice(sc * GLA_CHUNK, (sc + 1) * GLA_CHUNK)
        kcol = lambda h: slice(h * GLA_HK, (h + 1) * GLA_HK)
        vcol = lambda h: slice(h * GLA_HV, (h + 1) * GLA_HV)
        gates = [_gla_gates(gl_ref[rows(sc), :], gu_ref, gb_ref) for sc in range(cps)]
        qe, dec, o_in, kv = {}, {}, {}, {}
        for sc, h in pairs:
            _, b, b_last, mask = gates[sc]
            bh, bl = b[:, kcol(h)], b_last[:, kcol(h)]
            qe[sc, h] = (q_ref[rows(sc), kcol(h)] * scale) * jnp.exp(bh)
            kh = k_ref[rows(sc), kcol(h)]
            vh = v_ref[rows(sc), vcol(h)]
            attn = jnp.where(mask, _dot(qe[sc, h], kh * jnp.exp(-bh), NT), 0.0)
            o_in[sc, h] = _dot(attn, vh)
            kv[sc, h] = _dot(vh, kh * jnp.exp(bl - bh), TN)
            dec[sc, h] = jnp.exp(bl)
        for sc, h in pairs:
            st = st_ref[h]
            sp_ref[sc, h] = st
            o = o_in[sc, h] + _dot(qe[sc, h], st, NT)
            st_ref[h] = dec[sc, h] * st + kv[sc, h]
            r = lax.rsqrt(jnp.mean(o * o, axis=-1, keepdims=True) + EPS)
            z = z_ref[rows(sc), vcol(h)]
            y_ref[rows(sc), vcol(h)] = (o * r * nw_ref[...] * (z * _sigmoid(z))).astype(BF16)

    c = cps * GLA_CHUNK
    return pl.pallas_call(
        body, name="gla_fwd", grid=(nc // cps,),
        in_specs=_gla_specs(DS, DK, DV, c, lambda n: n) + [
            pl.BlockSpec((c, LANES), lambda n: (n, 0)),
            pl.BlockSpec((LANES, DK), lambda n: (0, 0)),
            pl.BlockSpec((1, DK), lambda n: (0, 0)),
            pl.BlockSpec((1, GLA_HV), lambda n: (0, 0)),
            pl.BlockSpec(memory_space=pl.ANY),
        ],
        out_specs=[pl.BlockSpec((c, DV), lambda n: (n, DS // DV)),
                   pl.BlockSpec((cps, nh, GLA_HV, GLA_HK), lambda n: (n, 0, 0, 0))],
        input_output_aliases={8: 0},
        out_shape=[jax.ShapeDtypeStruct(ycat.shape, BF16), jax.ShapeDtypeStruct((nc, nh, GLA_HV, GLA_HK), F32)],
        scratch_shapes=[pltpu.VMEM((nh, GLA_HV, GLA_HK), F32)],
        compiler_params=pltpu.CompilerParams(dimension_semantics=("arbitrary",)),
    )(proj_main, proj_main, proj_main, proj_main, proj_low, gate_up_pad, gate_bias, norm_w, ycat)


def _gla_bwd(d_ycat, proj_main, proj_low, s_prev, gate_up_pad, gate_bias, norm_w, DS, DK, DV):
    L = proj_main.shape[0]
    nc = L // GLA_CHUNK
    cps = math.gcd(GLA_STEP_CHUNKS, nc)
    nh = DK // GLA_HK
    scale = GLA_HK ** -0.5

    def body(dy_ref, q_ref, k_ref, v_ref, z_ref, gl_ref, sp_ref, gu_ref, gb_ref, nw_ref,
             dg_ref, da_ref, gnw_ref, ggb_ref, dst_ref):
        n = pl.program_id(0)

        @pl.when(n == 0)
        def _():
            dst_ref[...] = jnp.zeros_like(dst_ref)
            gnw_ref[...] = jnp.zeros_like(gnw_ref)
            ggb_ref[...] = jnp.zeros_like(ggb_ref)

        last_row = lax.broadcasted_iota(jnp.int32, (GLA_CHUNK, GLA_HK), 0) == GLA_CHUNK - 1
        ri = lax.broadcasted_iota(jnp.int32, (GLA_CHUNK, GLA_CHUNK), 0)
        ci = lax.broadcasted_iota(jnp.int32, (GLA_CHUNK, GLA_CHUNK), 1)
        upper = (ci >= ri).astype(F32)
        nw = nw_ref[...]
        for sc in reversed(range(cps)):
            rs = slice(sc * GLA_CHUNK, (sc + 1) * GLA_CHUNK)
            a, b, b_last, mask = _gla_gates(gl_ref[rs, :], gu_ref, gb_ref)
            for h in range(nh):
                ks = slice(h * GLA_HK, (h + 1) * GLA_HK)
                vs = slice(h * GLA_HV, (h + 1) * GLA_HV)
                bh, bl = b[:, ks], b_last[:, ks]
                e = jnp.exp(bh)
                einv = jnp.exp(-bh)
                etail = jnp.exp(bl - bh)
                dec = jnp.exp(bl)
                qe = (q_ref[rs, ks] * scale) * e
                kh = k_ref[rs, ks]
                ke = kh * einv
                ktail = kh * etail
                vh = v_ref[rs, vs]
                st = sp_ref[sc, h]
                dst = dst_ref[h]
                attn = jnp.where(mask, _dot(qe, ke, NT), 0.0)
                o = _dot(attn, vh) + _dot(qe, st, NT)
                r = lax.rsqrt(jnp.mean(o * o, axis=-1, keepdims=True) + EPS)
                nrm = o * r
                z = z_ref[rs, vs]
                sz = _sigmoid(z)
                dy = dy_ref[rs, vs]
                dg_ref[rs, 2 * DK + DV + h * GLA_HV:2 * DK + DV + (h + 1) * GLA_HV] = (
                    dy * nrm * nw * (sz * (1.0 + z * (1.0 - sz)))).astype(BF16)
                d_on = dy * (z * sz)
                gnw_ref[...] += jnp.sum(d_on * nrm, axis=0, keepdims=True)
                d_n = d_on * nw
                d_o = r * (d_n - nrm * jnp.mean(d_n * nrm, axis=-1, keepdims=True))
                d_attn = jnp.where(mask, _dot(d_o, vh, NT), 0.0)
                dg_ref[rs, 2 * DK + h * GLA_HV:2 * DK + (h + 1) * GLA_HV] = (
                    _dot(attn, d_o, TN) + _dot(ktail, dst, NT)).astype(BF16)
                d_qe = _dot(d_attn, ke) + _dot(d_o, st)
                d_ke = _dot(d_attn, qe, TN)
                d_kt = _dot(vh, dst)
                d_dec = jnp.sum(dst * st, axis=0, keepdims=True)
                dst_ref[h] = dec * dst + _dot(d_o, qe, TN)
                dg_ref[rs, ks] = (d_qe * scale * e).astype(BF16)
                dg_ref[rs, DK + h * GLA_HK:DK + (h + 1) * GLA_HK] = (d_ke * einv + d_kt * etail).astype(BF16)
                d_bl = jnp.sum(d_kt * ktail, axis=0, keepdims=True) + d_dec * dec
                d_b = d_qe * qe - d_ke * ke - d_kt * ktail + jnp.where(last_row, d_bl, 0.0)
                d_lg = _dot_hi(upper, d_b)
                d_a = d_lg * (1.0 / GLA_TAU) * _sigmoid(-a[:, ks])
                ggb_ref[:, ks] += jnp.sum(d_a, axis=0, keepdims=True)
                da_ref[rs, ks] = d_a.astype(BF16)

    c = cps * GLA_CHUNK
    ns = nc // cps
    rn = lambda n: ns - 1 - n
    return pl.pallas_call(
        body, name="gla_bwd", grid=(ns,),
        in_specs=[pl.BlockSpec((c, DV), lambda n: (rn(n), DS // DV))] + _gla_specs(DS, DK, DV, c, rn) + [
            pl.BlockSpec((c, LANES), lambda n: (rn(n), 0)),
            pl.BlockSpec((cps, nh, GLA_HV, GLA_HK), lambda n: (rn(n), 0, 0, 0)),
            pl.BlockSpec((LANES, DK), lambda n: (0, 0)),
            pl.BlockSpec((1, DK), lambda n: (0, 0)),
            pl.BlockSpec((1, GLA_HV), lambda n: (0, 0)),
        ],
        out_specs=[pl.BlockSpec((c, 2 * DK + 2 * DV), lambda n: (rn(n), 0)),
                   pl.BlockSpec((c, DK), lambda n: (rn(n), 0)),
                   pl.BlockSpec((1, GLA_HV), lambda n: (0, 0)), pl.BlockSpec((1, DK), lambda n: (0, 0))],
        out_shape=[jax.ShapeDtypeStruct((L, 2 * DK + 2 * DV), BF16),
                   jax.ShapeDtypeStruct((L, DK), BF16),
                   jax.ShapeDtypeStruct((1, GLA_HV), F32), jax.ShapeDtypeStruct((1, DK), F32)],
        scratch_shapes=[pltpu.VMEM((nh, GLA_HV, GLA_HK), F32)],
        compiler_params=pltpu.CompilerParams(dimension_semantics=("arbitrary",)),
    )(d_ycat, proj_main, proj_main, proj_main, proj_main, proj_low, s_prev, gate_up_pad, gate_bias, norm_w)


def _adamw_math(w, g, m, v):
    c1 = 1.0 - ADAM_B1 ** ADAM_STEP
    c2 = 1.0 - ADAM_B2 ** ADAM_STEP
    m_ = ADAM_B1 * m + (1.0 - ADAM_B1) * g
    v_ = ADAM_B2 * v + (1.0 - ADAM_B2) * (g * g)
    return -ADAM_LR * ((m_ / c1) / (jnp.sqrt(v_ / c2) + ADAM_EPS) + ADAM_WD * w), m_, v_


def _adamw_small(g_row, g_a, g_bc, ws, ms, vs):
    n = len(ws)
    nvec = n - 6

    def body(*refs):
        grow_ref, ga_ref, gbc_ref = refs[:3]
        w_refs, m_refs, v_refs = refs[3:3 + n], refs[3 + n:3 + 2 * n], refs[3 + 2 * n:3 + 3 * n]
        outs = refs[3 + 3 * n:]
        off = 0
        for i in range(n):
            if i < nvec:
                width = ws[i].shape[1]
                g = grow_ref[:, off:off + width]
                off += width
            elif i < nvec + 2:
                g = ga_ref[i - nvec]
            else:
                g = gbc_ref[i - nvec - 2]
            d, m_, v_ = _adamw_math(w_refs[i][...], g, m_refs[i][...], v_refs[i][...])
            outs[i][...] = g
            outs[n + i][...] = d
            outs[2 * n + i][...] = m_
            outs[3 * n + i][...] = v_

    vm = pl.BlockSpec(memory_space=pltpu.VMEM)
    outs = pl.pallas_call(
        body, name="adamw_small",
        in_specs=[vm] * (3 + 3 * n), out_specs=[vm] * (4 * n),
        out_shape=[jax.ShapeDtypeStruct(w.shape, F32) for w in ws] * 4,
    )(g_row, g_a, g_bc, *ws, *ms, *vs)
    return [outs[k * n:(k + 1) * n] for k in range(4)]


def _my_pos():
    return lax.axis_index("x"), lax.axis_index("y"), lax.axis_index("c")


def _gather_weights(shards):
    n = len(shards)
    halves = [s.shape[0] // 2 for s in shards]

    def body(*refs):
        ins, outs = refs[:n], refs[n:2 * n]
        send_sems, recv_sems = refs[2 * n:]
        x, y, c = _my_pos()
        me = 2 * x + y

        def piece(a, chip, half):
            return outs[a].at[chip, pl.ds(half * halves[a], halves[a]), :]

        def copy(a, k, src_chip, half, to):
            sl = piece(a, src_chip, half)
            return pltpu.make_async_remote_copy(src_ref=sl, dst_ref=sl, send_sem=send_sems.at[a, k], recv_sem=recv_sems.at[a, k],
                                                device_id=to, device_id_type=MESH)

        def first(a, d, to):
            src = ins[a].at[pl.ds(c * halves[a], halves[a]), :]
            return pltpu.make_async_remote_copy(src_ref=src, dst_ref=piece(a, me, c), send_sem=send_sems.at[a, d - 1],
                                                recv_sem=recv_sems.at[a, d - 1], device_id=to, device_id_type=MESH)

        sent = []
        for d in (1, 2, 3):
            to = (x ^ (d >> 1), y ^ (d & 1), c)
            for a in range(n):
                cp = first(a, d, to)
                cp.start()
                sent.append(cp)
        for d in (1, 2, 3):
            chip = (x ^ (d >> 1)) * 2 + (y ^ (d & 1))
            for a in range(n):
                copy(a, d - 1, chip, c, (x, y, c)).wait_recv()
                fw = copy(a, 2 + d, chip, c, (x, y, 1 - c))
                fw.start()
                sent.append(fw)
        for d in (1, 2, 3):
            chip = (x ^ (d >> 1)) * 2 + (y ^ (d & 1))
            for a in range(n):
                copy(a, 2 + d, chip, 1 - c, (x, y, c)).wait_recv()
        for cp in sent:
            cp.wait_send()

    hbm = pl.BlockSpec(memory_space=pltpu.HBM)
    return pl.pallas_call(
        body, name="gather_weights",
        in_specs=[hbm] * n, out_specs=[hbm] * n,
        out_shape=[jax.ShapeDtypeStruct((4,) + s.shape, s.dtype) for s in shards],
        scratch_shapes=[pltpu.SemaphoreType.DMA((n, 6)), pltpu.SemaphoreType.DMA((n, 6))],
    )(*shards)


def _pair_exchange(gs):
    n = len(gs)

    def body(*refs):
        ins, outs = refs[:n], refs[n:2 * n]
        send_sems, recv_sems = refs[2 * n:]
        x, y, c = _my_pos()
        sent = []
        for a in range(n):
            hrows = gs[a].shape[1] // 2
            cp = pltpu.make_async_remote_copy(
                src_ref=ins[a].at[:, pl.ds((1 - c) * hrows, hrows), :], dst_ref=outs[a], send_sem=send_sems.at[a],
                recv_sem=recv_sems.at[a], device_id=(x, y, 1 - c), device_id_type=MESH)
            cp.start()
            sent.append(cp)
        for cp in sent:
            cp.wait()

    hbm = pl.BlockSpec(memory_space=pltpu.HBM)
    return pl.pallas_call(
        body, name="grad_pair_exchange", in_specs=[hbm] * n, out_specs=[hbm] * n,
        out_shape=[jax.ShapeDtypeStruct((g.shape[0], g.shape[1] // 2, g.shape[2]), g.dtype) for g in gs],
        scratch_shapes=[pltpu.SemaphoreType.DMA((n,)), pltpu.SemaphoreType.DMA((n,))],
    )(*gs)


def _pair_add(g, got, c_arr, name):
    nk, rows2, cols = g.shape
    hrows = rows2 // 2
    tr = _blk(hrows, 256, 2 * SUBLANES)
    nb = hrows // tr

    def body(c_ref, a_ref, b_ref, o_ref):
        o_ref[...] = (a_ref[...].astype(F32) + b_ref[...].astype(F32)).astype(o_ref.dtype)

    return pl.pallas_call(
        body, name=name,
        grid_spec=pltpu.PrefetchScalarGridSpec(
            num_scalar_prefetch=1, grid=(nk, nb),
            in_specs=[pl.BlockSpec((1, tr, cols), lambda k, i, c_ref: (k, c_ref[0] * nb + i, 0)),
                      pl.BlockSpec((1, tr, cols), lambda k, i, c_ref: (k, i, 0))],
            out_specs=pl.BlockSpec((1, tr, cols), lambda k, i, c_ref: (k, i, 0))),
        out_shape=jax.ShapeDtypeStruct((nk, hrows, cols), g.dtype),
        compiler_params=pltpu.CompilerParams(dimension_semantics=("parallel", "parallel")),
    )(c_arr, g, got)


def _chip_scatter_copies(srcs, lands, send_sems, recv_sems):
    x, y, c = _my_pos()
    copies = []
    for d in (1, 2, 3):
        tx, ty = x ^ (d >> 1), y ^ (d & 1)
        for a in range(len(srcs)):
            copies.append(pltpu.make_async_remote_copy(
                src_ref=srcs[a].at[2 * tx + ty], dst_ref=lands[a].at[d - 1], send_sem=send_sems.at[3 * a + d - 1],
                recv_sem=recv_sems.at[3 * a + d - 1], device_id=(tx, ty, c), device_id_type=MESH))
    return copies


def _chip_scatter_start(pss):
    n = len(pss)

    def body(*refs):
        srcs, lands = refs[:n], refs[n:2 * n]
        send_sems, recv_sems = refs[2 * n], refs[2 * n + 1]
        token = refs[-1]
        for cp in _chip_scatter_copies(srcs, lands, send_sems, recv_sems):
            cp.start()
        token[...] = jnp.zeros_like(token)

    hbm = pl.BlockSpec(memory_space=pltpu.HBM)
    sem = pl.BlockSpec(memory_space=pltpu.SEMAPHORE)
    land_shapes = [(3,) + p.shape[1:] for p in pss]
    outs = pl.pallas_call(
        body, name="grad_chip_scatter_start",
        in_specs=[hbm] * (2 * n),
        out_specs=[sem, sem] + [hbm] * (2 * n) + [pl.BlockSpec(memory_space=pltpu.VMEM)],
        out_shape=[pltpu.SemaphoreType.DMA((3 * n,)), pltpu.SemaphoreType.DMA((3 * n,))]
        + [pltpu.HBM(p.shape, p.dtype) for p in pss]
        + [pltpu.HBM(s, p.dtype) for s, p in zip(land_shapes, pss)]
        + [jax.ShapeDtypeStruct((SUBLANES, LANES), F32)],
        input_output_aliases={i: 2 + i for i in range(2 * n)},
        compiler_params=pltpu.CompilerParams(has_side_effects=pltpu.SideEffectType.DATAFLOW_SIDE_EFFECTING),
    )(*[pltpu.with_memory_space_constraint(p, pltpu.HBM) for p in pss],
      *[pltpu.with_memory_space_constraint(lax.empty(s, p.dtype), pltpu.HBM) for s, p in zip(land_shapes, pss)])
    return outs[0], outs[1], outs[2:2 + n], outs[2 + n:2 + 2 * n], outs[-1]


def _chip_scatter_wait(send_sems, recv_sems, srcs, lands, after):
    n = len(srcs)

    def body(*refs):
        src_refs, land_refs = refs[:n], refs[n:2 * n]
        ssem, rsem = refs[2 * n], refs[2 * n + 1]
        for cp in _chip_scatter_copies(src_refs, land_refs, ssem, rsem):
            cp.wait_send()
            cp.wait_recv()

    hbm = pl.BlockSpec(memory_space=pltpu.HBM)
    sem = pl.BlockSpec(memory_space=pltpu.SEMAPHORE)
    outs = pl.pallas_call(
        body, name="grad_chip_scatter_wait",
        in_specs=[hbm] * (2 * n) + [sem, sem, pl.BlockSpec(memory_space=pl.ANY)],
        out_specs=[hbm] * (2 * n),
        out_shape=[pltpu.HBM(p.shape, p.dtype) for p in srcs] + [pltpu.HBM(p.shape, p.dtype) for p in lands],
        input_output_aliases={i: i for i in range(2 * n)},
        compiler_params=pltpu.CompilerParams(has_side_effects=pltpu.SideEffectType.DATAFLOW_SIDE_EFFECTING),
    )(*srcs, *lands, send_sems, recv_sems, after)
    return outs[:n], outs[n:]


def _chip_sum(ps, got, me_arr, name):
    _, hrows, cols = ps.shape
    tr = _blk(hrows, 256, 2 * SUBLANES)

    def body(me_ref, p_ref, g_ref, o_ref):
        acc = p_ref[0].astype(F32)
        for s in range(3):
            acc = acc + g_ref[s].astype(F32)
        o_ref[...] = acc

    return pl.pallas_call(
        body, name=name,
        grid_spec=pltpu.PrefetchScalarGridSpec(
            num_scalar_prefetch=1, grid=(hrows // tr,),
            in_specs=[pl.BlockSpec((1, tr, cols), lambda i, me_ref: (me_ref[0], i, 0)),
                      pl.BlockSpec((3, tr, cols), lambda i, me_ref: (0, i, 0))],
            out_specs=pl.BlockSpec((tr, cols), lambda i, me_ref: (i, 0))),
        out_shape=jax.ShapeDtypeStruct((hrows, cols), F32),
        compiler_params=pltpu.CompilerParams(dimension_semantics=("parallel",)),
    )(me_arr, ps, got)


def _pair_swap(halves):
    n = len(halves)

    def body(*refs):
        ins, outs = refs[:n], refs[n:2 * n]
        send_sems, recv_sems = refs[2 * n:]
        x, y, c = _my_pos()
        sent = []
        for a in range(n):
            cp = pltpu.make_async_remote_copy(src_ref=ins[a], dst_ref=outs[a], send_sem=send_sems.at[a], recv_sem=recv_sems.at[a],
                                              device_id=(x, y, 1 - c), device_id_type=MESH)
            cp.start()
            sent.append(cp)
        for cp in sent:
            cp.wait()

    hbm = pl.BlockSpec(memory_space=pltpu.HBM)
    return pl.pallas_call(
        body, name="grad_pair_swap", in_specs=[hbm] * n, out_specs=[hbm] * n,
        out_shape=[jax.ShapeDtypeStruct(h.shape, h.dtype) for h in halves],
        scratch_shapes=[pltpu.SemaphoreType.DMA((n,)), pltpu.SemaphoreType.DMA((n,))],
    )(*halves)


def _adamw_sharded(w, g_own, g_other, m, v, c_arr, name):
    R, C = w.shape
    hrows = R // 2
    tr = _blk(hrows, 256, SUBLANES)
    nbh = hrows // tr
    c1 = 1.0 - ADAM_B1 ** ADAM_STEP
    c2 = 1.0 - ADAM_B2 ** ADAM_STEP

    def body(c_ref, w_ref, go_ref, gx_ref, m_ref, v_ref, g_ref, d_ref, nm_ref, nv_ref):
        mine = (pl.program_id(0) // nbh) == c_ref[0]
        g_ = jnp.where(mine, go_ref[...], gx_ref[...])
        g_ref[...] = g_
        m_ = ADAM_B1 * m_ref[...] + (1.0 - ADAM_B1) * g_
        v_ = ADAM_B2 * v_ref[...] + (1.0 - ADAM_B2) * (g_ * g_)
        nm_ref[...] = m_
        nv_ref[...] = v_
        d_ref[...] = -ADAM_LR * ((m_ / c1) / (jnp.sqrt(v_ / c2) + ADAM_EPS) + ADAM_WD * w_ref[...])

    blk = pl.BlockSpec((tr, C), lambda i, c_ref: (i, 0))
    hblk = pl.BlockSpec((tr, C), lambda i, c_ref: (i % nbh, 0))
    sd = jax.ShapeDtypeStruct((R, C), F32)
    return pl.pallas_call(
        body, name=name,
        grid_spec=pltpu.PrefetchScalarGridSpec(
            num_scalar_prefetch=1, grid=(2 * nbh,),
            in_specs=[blk, hblk, hblk, blk, blk], out_specs=[blk] * 4),
        out_shape=[sd] * 4,
        compiler_params=pltpu.CompilerParams(dimension_semantics=("parallel",)),
    )(c_arr, w, g_own, g_other, m, v)


def _allreduce_small(arrs):
    n = len(arrs)
    rows = [a.shape[-2] // 8 for a in arrs]

    def piece(ref, a, p):
        start = p * rows[a]
        if rows[a] % SUBLANES == 0:
            start = pl.multiple_of(start, SUBLANES)
        return ref.at[..., pl.ds(start, rows[a]), :]

    def body(*refs):
        v_refs, o_refs, got_refs = refs[:n], refs[n:2 * n], refs[2 * n:3 * n]
        send_sems, recv_sems = refs[3 * n:]
        x, y, c = _my_pos()
        me = 4 * x + 2 * y + c

        def peer(d):
            return (x ^ (d >> 2), y ^ ((d >> 1) & 1), c ^ (d & 1))

        def lin(p):
            return 4 * p[0] + 2 * p[1] + p[2]

        sent = []
        for d in range(1, 8):
            to = peer(d)
            for a in range(n):
                cp = pltpu.make_async_remote_copy(
                    src_ref=piece(v_refs[a], a, lin(to)), dst_ref=got_refs[a].at[d],
                    send_sem=send_sems.at[0, d * n + a], recv_sem=recv_sems.at[0, d * n + a], device_id=to, device_id_type=MESH)
                cp.start()
                sent.append(cp)
        for a in range(n):
            acc = piece(v_refs[a], a, me)[...]
            for d in range(1, 8):
                sent[(d - 1) * n + a].wait_recv()
                acc = acc + got_refs[a][d]
            got_refs[a][0] = acc
            piece(o_refs[a], a, me)[...] = acc
        for d in range(1, 8):
            for a in range(n):
                cp = pltpu.make_async_remote_copy(
                    src_ref=got_refs[a].at[0], dst_ref=piece(o_refs[a], a, me),
                    send_sem=send_sems.at[1, d * n + a], recv_sem=recv_sems.at[1, d * n + a], device_id=peer(d), device_id_type=MESH)
                cp.start()
                sent.append(cp)
        for d in range(1, 8):
            for a in range(n):
                pltpu.make_async_remote_copy(
                    src_ref=got_refs[a].at[0], dst_ref=piece(o_refs[a], a, lin(peer(d))),
                    send_sem=send_sems.at[1, d * n + a], recv_sem=recv_sems.at[1, d * n + a], device_id=peer(d),
                    device_id_type=MESH).wait_recv()
        for cp in sent:
            cp.wait_send()

    vm = pl.BlockSpec(memory_space=pltpu.VMEM)
    return pl.pallas_call(
        body, name="allreduce_small", in_specs=[vm] * n, out_specs=[vm] * n,
        out_shape=[jax.ShapeDtypeStruct(a.shape, F32) for a in arrs],
        scratch_shapes=[pltpu.VMEM((8,) + a.shape[:-2] + (r, a.shape[-1]), F32) for a, r in zip(arrs, rows)]
        + [pltpu.SemaphoreType.DMA((2, 8 * n)), pltpu.SemaphoreType.DMA((2, 8 * n))],
    )(*arrs)


def kernel(x, pre_norm_w, w_in, s5_A_re, s5_A_im, s5_B_re, s5_B_im, s5_C_re, s5_C_im, s5_D, s5_log_dt, s5_glu_w, s5_glu_b, gla_gate_up, gla_gate_bias, gla_norm_w, w_out, post_norm_w, loss_target, m_pre_norm_w, m_w_in, m_s5_A_re, m_s5_A_im, m_s5_B_re, m_s5_B_im, m_s5_C_re, m_s5_C_im, m_s5_D, m_s5_log_dt, m_s5_glu_w, m_s5_glu_b, m_gla_gate_up, m_gla_gate_bias, m_gla_norm_w, m_w_out, m_post_norm_w, v_pre_norm_w, v_w_in, v_s5_A_re, v_s5_A_im, v_s5_B_re, v_s5_B_im, v_s5_C_re, v_s5_C_im, v_s5_D, v_s5_log_dt, v_s5_glu_w, v_s5_glu_b, v_gla_gate_up, v_gla_gate_bias, v_gla_norm_w, v_w_out, v_post_norm_w):
    names = ["pre_norm_w", "w_in", "s5_A_re", "s5_A_im", "s5_B_re", "s5_B_im", "s5_C_re", "s5_C_im", "s5_D", "s5_log_dt",
             "s5_glu_w", "s5_glu_b", "gla_gate_up", "gla_gate_bias", "gla_norm_w", "w_out", "post_norm_w"]
    W = dict(zip(names, (pre_norm_w, w_in, s5_A_re, s5_A_im, s5_B_re, s5_B_im, s5_C_re, s5_C_im, s5_D, s5_log_dt,
                         s5_glu_w, s5_glu_b, gla_gate_up, gla_gate_bias, gla_norm_w, w_out, post_norm_w)))
    M = dict(zip(names, (m_pre_norm_w, m_w_in, m_s5_A_re, m_s5_A_im, m_s5_B_re, m_s5_B_im, m_s5_C_re, m_s5_C_im, m_s5_D,
                         m_s5_log_dt, m_s5_glu_w, m_s5_glu_b, m_gla_gate_up, m_gla_gate_bias, m_gla_norm_w, m_w_out,
                         m_post_norm_w)))
    V = dict(zip(names, (v_pre_norm_w, v_w_in, v_s5_A_re, v_s5_A_im, v_s5_B_re, v_s5_B_im, v_s5_C_re, v_s5_C_im, v_s5_D,
                         v_s5_log_dt, v_s5_glu_w, v_s5_glu_b, v_gla_gate_up, v_gla_gate_bias, v_gla_norm_w, v_w_out,
                         v_post_norm_w)))
    sharded = ("w_in", "s5_glu_w", "w_out", "gla_gate_up")

    xb = x[0]
    tgt = loss_target[0]
    L, D = xb.shape
    DS = D // 2
    G = DS // S5_GROUP
    P = S5_STATE
    NB = DS // S5_COLS
    DV = D - DS
    DK = DV // 2
    WM = 2 * DS + 2 * DK + 2 * DV
    nsh = w_in.shape[2]

    chip = 2 * lax.axis_index("x") + lax.axis_index("y")
    own = [w_in[0].astype(BF16), s5_glu_w[0].astype(BF16), w_out[0].astype(BF16), gla_gate_up[0]]
    g_win, g_glu, g_wout, g_gup = [lax.dynamic_update_index_in_dim(g, o, chip, 0)
                                   for g, o in zip(_gather_weights(own), own)]
    w_full = jnp.moveaxis(g_win, 0, 1).reshape(D, 4 * nsh)
    w_main = w_full[:, :WM]
    w_low = jnp.pad(w_full[:, WM:], ((0, 0), (0, LANES - GLA_RANK)))
    glu_w = g_glu.reshape(DS, DS)
    wout = g_wout.reshape(D, D)
    gup = jnp.moveaxis(g_gup, 0, 1).reshape(GLA_RANK, DK)
    gup_pad = jnp.pad(gup, ((0, LANES - GLA_RANK), (0, 0))).astype(BF16)

    b_view = lambda t: jnp.transpose(t[0], (0, 2, 1)).reshape(G * S5_GROUP, P)
    b_back = lambda t: jnp.transpose(t.reshape(G, S5_GROUP, P), (0, 2, 1))[None]
    c_view = lambda t: t[0].reshape(G * S5_GROUP, P)
    c_back = lambda t: t.reshape(1, G, S5_GROUP, P)
    small = ["pre_norm_w", "post_norm_w", "s5_D", "s5_glu_b", "gla_gate_bias", "gla_norm_w", "s5_log_dt",
             "s5_A_re", "s5_A_im", "s5_B_re", "s5_B_im", "s5_C_re", "s5_C_im"]
    view = {n: (lambda t: t) for n in small[:7]}
    back = dict(view)
    view.update(s5_A_re=lambda t: t[0], s5_A_im=lambda t: t[0], s5_B_re=b_view, s5_B_im=b_view, s5_C_re=c_view, s5_C_im=c_view)
    back.update(s5_A_re=lambda t: t[None], s5_A_im=lambda t: t[None], s5_B_re=b_back, s5_B_im=b_back, s5_C_re=c_back,
                s5_C_im=c_back)
    Wv = {n: view[n](W[n]) for n in small}
    bbd_re, bbd_im, ct_re, ct_im, tab = _s5_prep_fwd(Wv["s5_A_re"], Wv["s5_A_im"], s5_log_dt, Wv["s5_B_re"], Wv["s5_B_im"],
                                                     Wv["s5_C_re"], Wv["s5_C_im"])
    dvec = s5_D

    h = _prenorm_fwd(xb, pre_norm_w)
    proj_main = _mm(h, w_main, name="in_proj")
    proj_low = _mm(h, w_low, name="in_proj_low")
    y_pre, s_re, s_im = _s5_scan_fwd(proj_main, bbd_re, bbd_im, ct_re, ct_im, dvec, tab, DS)
    ycat, t_pre = _s5_post_fwd(y_pre, proj_main, glu_w, s5_glu_b, DS)
    ycat, s_prev = _gla_fwd(proj_main, proj_low, gup_pad, gla_gate_bias, gla_norm_w, ycat, DS, DK, DV)
    mixed = _mm(ycat, wout, name="out_proj")
    loss11, d_mixed, dout, g_post_w = _post_fwd_bwd(mixed, xb, tgt, post_norm_w)

    d_ycat = _mm(d_mixed, wout, tb=True, name="out_proj_dx")
    g_wout_full = _mm(ycat, d_mixed, ta=True, out_dtype=BF16, name="out_proj_dw")
    d_ypre, d_s5, d_t, y1, g_glu_b = _s5_post_bwd(d_ycat, y_pre, proj_main, t_pre, glu_w, DS)
    g_glu_full = _mm(y1, d_t, ta=True, out_dtype=BF16, name="glu_dw")
    d_s5, g_D, gct_re, gct_im, gbbd_re, gbbd_im, gab_re, gab_im = _s5_scan_bwd(
        d_ypre, proj_main, s_re, s_im, bbd_re, bbd_im, ct_re, ct_im, dvec, tab, d_s5, DS)
    d_gla, d_a, g_norm_w, g_gate_bias = _gla_bwd(
        d_ycat, proj_main, proj_low, s_prev, gup_pad, gla_gate_bias, gla_norm_w, DS, DK, DV)
    d_low = _mm(d_a, gup_pad, tb=True, out_dtype=BF16, name="gate_dx")
    g_gup_pad = _mm(proj_low, d_a, ta=True, name="gate_dw")
    g_wmain = _mm_nsplit(h, d_s5, d_gla, out_dtype=BF16, name="in_proj_dw")
    g_wlow = _mm(h, d_low, ta=True, out_dtype=BF16, name="in_proj_low_dw")

    g_win_full = jnp.concatenate([g_wmain, g_wlow[:, :GLA_RANK]], axis=1)
    gs = [jnp.moveaxis(g_win_full.reshape(D, 4, nsh), 1, 0),
          g_glu_full.reshape(4, DS // 4, DS),
          g_wout_full.reshape(4, D // 4, D),
          jnp.moveaxis(g_gup_pad[:GLA_RANK].reshape(GLA_RANK, 4, DK // 4), 1, 0)]
    c_arr = lax.axis_index("c").astype(jnp.int32).reshape(1)
    me_arr = chip.astype(jnp.int32).reshape(1)
    got = _pair_exchange(gs)
    pss = [_pair_add(g, r, c_arr, "grad_pair_add_" + n) for n, g, r in zip(sharded, gs, got)]
    send_sems, recv_sems, pss, lands, token = _chip_scatter_start(pss)

    dh_main = _mm_ksplit(d_s5, d_gla, w_main, token, name="in_proj_dx")
    dh_low = _mm(d_low, w_low, tb=True, name="in_proj_low_dx")
    grad_x, g_pre_w = _prenorm_bwd(xb, dh_main, dh_low, dout, pre_norm_w)
    pss, rcv = _chip_scatter_wait(send_sems, recv_sems, pss, lands, g_pre_w)

    g_a, g_bc, g_ldt = _s5_prep_bwd(Wv["s5_A_re"], Wv["s5_A_im"], s5_log_dt, Wv["s5_B_re"], Wv["s5_B_im"],
                                    gbbd_re, gbbd_im, gct_re, gct_im, gab_re, gab_im)

    loss = lax.psum(loss11[0, 0], ("x", "y", "c"))

    g_vecs = jnp.concatenate([g_pre_w, g_post_w, g_D, g_glu_b, g_gate_bias, g_norm_w, g_ldt], axis=1)
    lanes_pad = -g_vecs.shape[1] % (8 * SUBLANES * LANES)
    g_vecs = jnp.pad(g_vecs, ((0, 0), (0, lanes_pad))).reshape(-1, LANES)
    r_vecs, r_a, r_bc = _allreduce_small([g_vecs, g_a, g_bc])
    outs4 = _adamw_small(r_vecs.reshape(1, -1), r_a, r_bc, [Wv[n] for n in small],
                         [view[n](M[n]) for n in small], [view[n](V[n]) for n in small])
    G_out, D_out, M_out, V_out = [{n: back[n](t) for n, t in zip(small, o)} for o in outs4]

    halves = [_chip_sum(p, r, me_arr, "grad_chip_sum_" + n) for n, p, r in zip(sharded, pss, rcv)]
    others = _pair_swap(halves)
    for n, g_own, g_other in zip(sharded, halves, others):
        g_, d_, m_, v_ = _adamw_sharded(W[n][0], g_own, g_other, M[n][0], V[n][0], c_arr, "adamw_" + n)
        G_out[n], D_out[n], M_out[n], V_out[n] = g_[None], d_[None], m_[None], v_[None]

    return (loss, grad_x[None], *[G_out[n] for n in names], *[D_out[n] for n in names],
            *[M_out[n] for n in names], *[V_out[n] for n in names])
```

```python
import functools
import math

import jax
import jax.numpy as jnp
from jax import lax
from jax.experimental import pallas as pl
from jax.experimental.pallas import tpu as pltpu

F32 = jnp.float32
BF16 = jnp.bfloat16
HI = lax.Precision.HIGHEST
MESH = pl.DeviceIdType.MESH

EPS = 1e-6
S5_GROUP = 16
S5_STATE = 64
GLA_HK = 128
GLA_HV = 256
GLA_RANK = 16
GLA_TAU = 16.0
GLA_CHUNK = 64
GLA_STEP_CHUNKS = 2
LANES = 128
SUBLANES = 8
S5_COLS = 128
S5_LANES = (S5_COLS // S5_GROUP) * S5_STATE

ADAM_LR = 0.001
ADAM_B1 = 0.9
ADAM_B2 = 0.999
ADAM_EPS = 1e-08
ADAM_WD = 0.01
ADAM_STEP = 10

GELU_K = math.sqrt(2.0 / math.pi)
GELU_C = 0.044715


def _blk(n, pref, unit=LANES):
    best = None
    b = unit
    while b <= min(n, pref):
        if n % b == 0:
            best = b
        b += unit
    return best if best is not None else n


def _dot(a, b, dn=(((1,), (0,)), ((), ()))):
    return lax.dot_general(a.astype(BF16), b.astype(BF16), dn, preferred_element_type=F32)


def _dot_hi(a, b, dn=(((1,), (0,)), ((), ()))):
    return lax.dot_general(a, b, dn, precision=HI, preferred_element_type=F32)


NN = (((1,), (0,)), ((), ()))
NT = (((1,), (1,)), ((), ()))
TN = (((0,), (0,)), ((), ()))


def _sigmoid(x):
    return 1.0 / (1.0 + jnp.exp(-x))


def _gelu(y):
    return 0.5 * y * (1.0 + jnp.tanh(GELU_K * (y + GELU_C * y * y * y)))


def _gelu_grad(y):
    th = jnp.tanh(GELU_K * (y + GELU_C * y * y * y))
    return 0.5 * (1.0 + th) + 0.5 * y * (1.0 - th * th) * GELU_K * (1.0 + 3.0 * GELU_C * y * y)


def _mm(a, b, *, name, ta=False, tb=False, out_dtype=F32, bm=1024, bn=1024, bk=2048):
    if ta:
        K, M = a.shape
    else:
        M, K = a.shape
    if tb:
        N, K2 = b.shape
    else:
        K2, N = b.shape
    assert K == K2, (a.shape, b.shape, ta, tb)
    bm, bn, bk = _blk(M, bm), _blk(N, bn), _blk(K, bk)
    nk = K // bk
    dn = (((0 if ta else 1,), (1 if tb else 0,)), ((), ()))

    def body(a_ref, b_ref, o_ref, *acc):
        if nk == 1:
            o_ref[...] = _dot(a_ref[...], b_ref[...], dn).astype(out_dtype)
            return
        acc_ref, = acc
        k = pl.program_id(2)

        @pl.when(k == 0)
        def _():
            acc_ref[...] = jnp.zeros_like(acc_ref)

        acc_ref[...] += _dot(a_ref[...], b_ref[...], dn)

        @pl.when(k == nk - 1)
        def _():
            o_ref[...] = acc_ref[...].astype(out_dtype)

    a_spec = pl.BlockSpec((bk, bm), lambda i, j, k: (k, i)) if ta else pl.BlockSpec((bm, bk), lambda i, j, k: (i, k))
    b_spec = pl.BlockSpec((bn, bk), lambda i, j, k: (j, k)) if tb else pl.BlockSpec((bk, bn), lambda i, j, k: (k, j))
    return pl.pallas_call(
        body,
        name=name,
        grid=(M // bm, N // bn, nk),
        in_specs=[a_spec, b_spec],
        out_specs=pl.BlockSpec((bm, bn), lambda i, j, k: (i, j)),
        out_shape=jax.ShapeDtypeStruct((M, N), out_dtype),
        scratch_shapes=[pltpu.VMEM((bm, bn), F32)] if nk > 1 else [],
        compiler_params=pltpu.CompilerParams(dimension_semantics=("parallel", "parallel", "arbitrary")),
    )(a, b)


def _mm_ksplit(a1, a2, b, after, *, name, out_dtype=F32, bm=1024, bn=1024, bk=2048):
    M, K1 = a1.shape
    K2 = a2.shape[1]
    N = b.shape[0]
    bm, bn = _blk(M, bm), _blk(N, bn)
    bk = _blk(math.gcd(K1, K2), bk)
    nk1, nk = K1 // bk, (K1 + K2) // bk

    def body(a1_ref, a2_ref, b_ref, _after_ref, o_ref, acc_ref):
        k = pl.program_id(2)

        @pl.when(k == 0)
        def _():
            acc_ref[...] = jnp.zeros_like(acc_ref)

        @pl.when(k < nk1)
        def _():
            acc_ref[...] += _dot(a1_ref[...], b_ref[...], NT)

        @pl.when(k >= nk1)
        def _():
            acc_ref[...] += _dot(a2_ref[...], b_ref[...], NT)

        @pl.when(k == nk - 1)
        def _():
            o_ref[...] = acc_ref[...].astype(out_dtype)

    return pl.pallas_call(
        body, name=name, grid=(M // bm, N // bn, nk),
        in_specs=[pl.BlockSpec((bm, bk), lambda i, j, k: (i, jnp.minimum(k, nk1 - 1))),
                  pl.BlockSpec((bm, bk), lambda i, j, k: (i, jnp.maximum(k - nk1, 0))),
                  pl.BlockSpec((bn, bk), lambda i, j, k: (j, k)),
                  pl.BlockSpec(memory_space=pl.ANY)],
        out_specs=pl.BlockSpec((bm, bn), lambda i, j, k: (i, j)),
        out_shape=jax.ShapeDtypeStruct((M, N), out_dtype),
        scratch_shapes=[pltpu.VMEM((bm, bn), F32)],
        compiler_params=pltpu.CompilerParams(dimension_semantics=("parallel", "parallel", "arbitrary")),
    )(a1, a2, b, after)


def _mm_nsplit(a, b1, b2, *, name, out_dtype=F32, bm=1024, bn=1024, bk=2048):
    K, M = a.shape
    N1, N2 = b1.shape[1], b2.shape[1]
    bm, bk = _blk(M, bm), _blk(K, bk)
    bn = _blk(math.gcd(N1, N2), bn)
    nj1, nj = N1 // bn, (N1 + N2) // bn
    nk = K // bk

    def body(a_ref, b1_ref, b2_ref, o_ref, acc_ref):
        j = pl.program_id(1)
        k = pl.program_id(2)

        @pl.when(k == 0)
        def _():
            acc_ref[...] = jnp.zeros_like(acc_ref)

        @pl.when(j < nj1)
        def _():
            acc_ref[...] += _dot(a_ref[...], b1_ref[...], TN)

        @pl.when(j >= nj1)
        def _():
            acc_ref[...] += _dot(a_ref[...], b2_ref[...], TN)

        @pl.when(k == nk - 1)
        def _():
            o_ref[...] = acc_ref[...].astype(out_dtype)

    return pl.pallas_call(
        body, name=name, grid=(M // bm, nj, nk),
        in_specs=[pl.BlockSpec((bk, bm), lambda i, j, k: (k, i)),
                  pl.BlockSpec((bk, bn), lambda i, j, k: (jnp.where(j < nj1, k, nk - 1), jnp.minimum(j, nj1 - 1))),
                  pl.BlockSpec((bk, bn), lambda i, j, k: (jnp.where(j >= nj1, k, 0), jnp.maximum(j - nj1, 0)))],
        out_specs=pl.BlockSpec((bm, bn), lambda i, j, k: (i, j)),
        out_shape=jax.ShapeDtypeStruct((M, N1 + N2), out_dtype),
        scratch_shapes=[pltpu.VMEM((bm, bn), F32)],
        compiler_params=pltpu.CompilerParams(dimension_semantics=("parallel", "parallel", "arbitrary")),
    )(a, b1, b2)


def _prenorm_fwd(x, w):
    L, D = x.shape
    tr = _blk(L, 256, SUBLANES)

    def body(x_ref, w_ref, h_ref):
        xv = x_ref[...]
        r = lax.rsqrt(jnp.mean(xv * xv, axis=-1, keepdims=True) + EPS)
        h_ref[...] = (xv * r * w_ref[...]).astype(BF16)

    return pl.pallas_call(
        body, name="prenorm_fwd", grid=(L // tr,),
        in_specs=[pl.BlockSpec((tr, D), lambda i: (i, 0)), pl.BlockSpec((1, D), lambda i: (0, 0))],
        out_specs=pl.BlockSpec((tr, D), lambda i: (i, 0)),
        out_shape=jax.ShapeDtypeStruct((L, D), BF16),
        compiler_params=pltpu.CompilerParams(dimension_semantics=("parallel",)),
    )(x, w)


def _post_fwd_bwd(mixed, x, target, w):
    L, D = x.shape
    tr = _blk(L, 256, SUBLANES)
    nsteps = L // tr

    def body(mx_ref, x_ref, t_ref, w_ref, loss_ref, dm_ref, dout_ref, gw_ref, acc_ref):
        i = pl.program_id(0)

        @pl.when(i == 0)
        def _():
            acc_ref[...] = jnp.zeros_like(acc_ref)
            gw_ref[...] = jnp.zeros_like(gw_ref)

        mx = mx_ref[...]
        wv = w_ref[...]
        r = lax.rsqrt(jnp.mean(mx * mx, axis=-1, keepdims=True) + EPS)
        n = mx * r
        err = x_ref[...] + n * wv - t_ref[...]
        acc_ref[...] += jnp.sum(err * err, axis=0, keepdims=True)
        dout = err * (1.0 / D)
        dout_ref[...] = dout
        gw_ref[...] += jnp.sum(dout * n, axis=0, keepdims=True)
        dn = dout * wv
        dm_ref[...] = (r * (dn - n * jnp.mean(dn * n, axis=-1, keepdims=True))).astype(BF16)

        @pl.when(i == nsteps - 1)
        def _():
            loss_ref[...] = jnp.sum(acc_ref[...], axis=-1, keepdims=True) * (0.5 / D)

    row = pl.BlockSpec((tr, D), lambda i: (i, 0))
    vec = pl.BlockSpec((1, D), lambda i: (0, 0))
    return pl.pallas_call(
        body, name="post_fwd_bwd", grid=(nsteps,),
        in_specs=[row, row, row, vec],
        out_specs=[pl.BlockSpec((1, 1), lambda i: (0, 0)), row, row, vec],
        out_shape=[jax.ShapeDtypeStruct((1, 1), F32), jax.ShapeDtypeStruct((L, D), BF16),
                   jax.ShapeDtypeStruct((L, D), F32), jax.ShapeDtypeStruct((1, D), F32)],
        scratch_shapes=[pltpu.VMEM((1, D), F32)],
        compiler_params=pltpu.CompilerParams(dimension_semantics=("arbitrary",)),
    )(mixed, x, target, w)


def _prenorm_bwd(x, dh_main, dh_low, dout, w):
    L, D = x.shape
    tr = _blk(L, 256, SUBLANES)

    def body(x_ref, a_ref, b_ref, dout_ref, w_ref, gx_ref, gw_ref):
        i = pl.program_id(0)

        @pl.when(i == 0)
        def _():
            gw_ref[...] = jnp.zeros_like(gw_ref)

        xv = x_ref[...]
        r = lax.rsqrt(jnp.mean(xv * xv, axis=-1, keepdims=True) + EPS)
        n = xv * r
        dh = a_ref[...] + b_ref[...]
        gw_ref[...] += jnp.sum(dh * n, axis=0, keepdims=True)
        dn = dh * w_ref[...]
        gx_ref[...] = dout_ref[...] + r * (dn - n * jnp.mean(dn * n, axis=-1, keepdims=True))

    row = pl.BlockSpec((tr, D), lambda i: (i, 0))
    vec = pl.BlockSpec((1, D), lambda i: (0, 0))
    return pl.pallas_call(
        body, name="prenorm_bwd", grid=(L // tr,),
        in_specs=[row, row, row, row, vec],
        out_specs=[row, vec],
        out_shape=[jax.ShapeDtypeStruct((L, D), F32), jax.ShapeDtypeStruct((1, D), F32)],
        compiler_params=pltpu.CompilerParams(dimension_semantics=("arbitrary",)),
    )(x, dh_main, dh_low, dout, w)


def _s5_disc(a_re_raw, a_im, dt):
    a_re = jnp.minimum(a_re_raw, -1e-4)
    mag = jnp.exp(a_re * dt)
    ph = a_im * dt
    ab_re = mag * jnp.cos(ph)
    ab_im = mag * jnp.sin(ph)
    inv_n = 1.0 / (a_re * a_re + a_im * a_im)
    ia_re = a_re * inv_n
    ia_im = -a_im * inv_n
    n_re = ab_re - 1.0
    f_re = n_re * ia_re - ab_im * ia_im
    f_im = n_re * ia_im + ab_im * ia_re
    return a_re, ab_re, ab_im, f_re, f_im, ia_re, ia_im


def _iota2(shape, dim):
    return lax.broadcasted_iota(jnp.int32, shape, dim)


def _group_mask(rows, rows_per_group):
    shift = rows_per_group.bit_length() - 1
    return (_iota2((rows, S5_LANES), 0) >> shift) == (_iota2((rows, S5_LANES), 1) >> (S5_STATE.bit_length() - 1))


def _lane_tiler(dtype):
    return ((_iota2((S5_STATE, S5_LANES), 1) & (S5_STATE - 1)) == _iota2((S5_STATE, S5_LANES), 0)).astype(dtype)


def _row_to_col(row, n):
    eye = (_iota2((n, n), 0) == _iota2((n, n), 1)).astype(F32)
    return jnp.sum(eye * row, axis=1, keepdims=True)


def _group_repeat(G):
    return ((_iota2((G * S5_GROUP, G), 0) >> (S5_GROUP.bit_length() - 1)) == _iota2((G * S5_GROUP, G), 1)).astype(F32)


S5_TABS = 18


def _s5_prep_fwd(a_re, a_im, log_dt, b_re, b_im, c_re, c_im, seg):
    G, P = a_re.shape
    nb = G * S5_GROUP // S5_COLS
    g8 = S5_COLS // S5_GROUP
    assert seg & (seg - 1) == 0, seg

    def body(are_ref, aim_ref, ldt_ref, bre_ref, bim_ref, cre_ref, cim_ref,
             bbre_ref, bbim_ref, ctre_ref, ctim_ref, tab_ref, pt_ref):
        dt = jnp.exp(_row_to_col(ldt_ref[...], G))
        _, ab_re, ab_im, f_re, f_im, _, _ = _s5_disc(are_ref[...], aim_ref[...], dt)
        rep = _group_repeat(G)
        fx_re = _dot_hi(rep, f_re)
        fx_im = _dot_hi(rep, f_im)
        br, bi = bre_ref[...], bim_ref[...]
        bb_re = fx_re * br - fx_im * bi
        bb_im = fx_re * bi + fx_im * br
        tile_bf = _lane_tiler(BF16)
        mask = _group_mask(S5_COLS, S5_GROUP)
        for jb in range(nb):
            rs = slice(jb * S5_COLS, (jb + 1) * S5_COLS)
            for src, dst in ((bb_re[rs], bbre_ref), (bb_im[rs], bbim_ref), (cre_ref[rs, :], ctre_ref), (cim_ref[rs, :], ctim_ref)):
                dst[jb] = jnp.where(mask, _dot(src, tile_bf), 0.0).astype(BF16)

        tile_f = _lane_tiler(F32)
        mask8 = _group_mask(g8, 1)
        row = _iota2((SUBLANES, S5_LANES), 0)
        slab = (SUBLANES, S5_LANES)
        cmul = lambda p, q: (p[0] * q[0] - p[1] * q[1], p[0] * q[1] + p[1] * q[0])
        for jb in range(nb):
            gs = slice(jb * g8, (jb + 1) * g8)

            def lanes(m):
                v = jnp.sum(jnp.where(mask8, _dot_hi(m[gs], tile_f), 0.0), axis=0, keepdims=True)
                return jnp.broadcast_to(v, slab)

            a1 = (lanes(ab_re), lanes(ab_im))
            tab_ref[jb, 0], tab_ref[jb, 1] = a1

            def powers(i, p):
                off = pl.multiple_of(i * SUBLANES, SUBLANES)
                pt_ref[jb, 0, pl.ds(off, SUBLANES), :] = p[0]
                pt_ref[jb, 1, pl.ds(off, SUBLANES), :] = p[1]
                return cmul(p, a1)

            lax.fori_loop(0, seg, powers, a1)
            aseg = a1
            for _ in range(seg.bit_length() - 1):
                aseg = cmul(aseg, aseg)
            pw = [aseg]
            for _ in range(1, SUBLANES):
                pw.append(cmul(pw[-1], aseg))
            for lvl, k in enumerate((1, 2, 4)):
                tab_ref[jb, 2 + 2 * lvl] = jnp.where(row >= k, pw[k - 1][0], 0.0)
                tab_ref[jb, 3 + 2 * lvl] = jnp.where(row >= k, pw[k - 1][1], 0.0)
                tab_ref[jb, 10 + 2 * lvl] = jnp.where(row < SUBLANES - k, pw[k - 1][0], 0.0)
                tab_ref[jb, 11 + 2 * lvl] = jnp.where(row < SUBLANES - k, -pw[k - 1][1], 0.0)
            f_r = f_i = r_r = r_i = jnp.zeros(slab, F32)
            for i in range(SUBLANES):
                f_r = jnp.where(row == i, pw[i][0], f_r)
                f_i = jnp.where(row == i, pw[i][1], f_i)
                r_r = jnp.where(row == i, pw[SUBLANES - 1 - i][0], r_r)
                r_i = jnp.where(row == i, -pw[SUBLANES - 1 - i][1], r_i)
            tab_ref[jb, 8] = f_r
            tab_ref[jb, 9] = f_i
            tab_ref[jb, 16] = r_r
            tab_ref[jb, 17] = r_i

    vm = pl.BlockSpec(memory_space=pltpu.VMEM)
    bd = jax.ShapeDtypeStruct((nb, S5_COLS, S5_LANES), BF16)
    return pl.pallas_call(
        body, name="s5_prep_fwd",
        in_specs=[vm] * 7, out_specs=[vm] * 6,
        out_shape=[bd, bd, bd, bd, jax.ShapeDtypeStruct((nb, S5_TABS, SUBLANES, S5_LANES), F32),
                   jax.ShapeDtypeStruct((nb, 2, seg * SUBLANES, S5_LANES), F32)],
    )(a_re, a_im, log_dt, b_re, b_im, c_re, c_im)


def _s5_prep_bwd(a_re, a_im, log_dt, b_re, b_im, gbb_re, gbb_im, gct_re, gct_im, gab_re, gab_im):
    G, P = a_re.shape
    nb = G * S5_GROUP // S5_COLS
    g8 = S5_COLS // S5_GROUP

    def body(are_ref, aim_ref, ldt_ref, bre_ref, bim_ref, gbr_ref, gbi_ref, gcr_ref, gci_ref, gar_ref, gai_ref,
             o_a, o_bc, o_ldt):
        dt = jnp.exp(_row_to_col(ldt_ref[...], G))
        a_raw = are_ref[...]
        a_imv = aim_ref[...]
        a_re_c, ab_re, ab_im, f_re, f_im, ia_re, ia_im = _s5_disc(a_raw, a_imv, dt)
        tile_f = _lane_tiler(F32)
        mask = _group_mask(S5_COLS, S5_GROUP)
        mask8 = _group_mask(g8, 1)
        for jb in range(nb):
            rs = slice(jb * S5_COLS, (jb + 1) * S5_COLS)
            gs = slice(jb * g8, (jb + 1) * g8)
            ls = slice(jb * S5_LANES, (jb + 1) * S5_LANES)
            for k, src in enumerate((gbr_ref, gbi_ref, gcr_ref, gci_ref)):
                o_bc[k, rs, :] = _dot_hi(jnp.where(mask, src[jb], 0.0), tile_f, NT)
            for k, src in enumerate((gar_ref, gai_ref)):
                o_a[k, gs, :] = _dot_hi(jnp.where(mask8, src[:, ls], 0.0), tile_f, NT)
        rep = _group_repeat(G)
        fx_re = _dot_hi(rep, f_re)
        fx_im = _dot_hi(rep, f_im)
        gbr, gbi = o_bc[0], o_bc[1]
        br, bi = bre_ref[...], bim_ref[...]
        o_bc[0] = fx_re * gbr + fx_im * gbi
        o_bc[1] = fx_re * gbi - fx_im * gbr
        gf_re = _dot_hi(rep, br * gbr + bi * gbi, TN)
        gf_im = _dot_hi(rep, br * gbi - bi * gbr, TN)
        gab_r = o_a[0] + ia_re * gf_re + ia_im * gf_im
        gab_i = o_a[1] + ia_re * gf_im - ia_im * gf_re
        q_re = f_re * ia_re - f_im * ia_im
        q_im = f_re * ia_im + f_im * ia_re
        ga_re = -(q_re * gf_re + q_im * gf_im)
        ga_im = -(q_re * gf_im - q_im * gf_re)
        gth_re = ab_re * gab_r + ab_im * gab_i
        gth_im = ab_re * gab_i - ab_im * gab_r
        ga_re = ga_re + dt * gth_re
        ga_im = ga_im + dt * gth_im
        gdt = jnp.sum(a_re_c * gth_re + a_imv * gth_im, axis=-1, keepdims=True)
        eye = (_iota2((G, G), 0) == _iota2((G, G), 1)).astype(F32)
        o_ldt[...] = jnp.sum(eye * (gdt * dt), axis=0, keepdims=True)
        slope = jnp.where(a_raw < -1e-4, 1.0, jnp.where(a_raw == -1e-4, 0.5, 0.0))
        o_a[0] = ga_re * slope
        o_a[1] = ga_im

    vm = pl.BlockSpec(memory_space=pltpu.VMEM)
    return pl.pallas_call(
        body, name="s5_prep_bwd",
        in_specs=[vm] * 11, out_specs=[vm] * 3,
        out_shape=[jax.ShapeDtypeStruct((2, G, P), F32), jax.ShapeDtypeStruct((4, G * S5_GROUP, P), F32),
                   jax.ShapeDtypeStruct((1, G), F32)],
    )(a_re, a_im, log_dt, b_re, b_im, gbb_re, gbb_im, gct_re, gct_im, gab_re, gab_im)


def _scan8(xr, xi, tab_ref, base, shifts):
    for lvl, sh in enumerate(shifts):
        mr = tab_ref[0, base + 2 * lvl]
        mi = tab_ref[0, base + 2 * lvl + 1]
        ar = pltpu.roll(xr, sh, 0)
        ai = pltpu.roll(xi, sh, 0)
        xr, xi = xr + mr * ar - mi * ai, xi + mr * ai + mi * ar
    return xr, xi


def _to_segments(src_ref, dst_ref, seg):
    for i in range(seg):
        dst_ref[i * SUBLANES:(i + 1) * SUBLANES, :] = src_ref[pl.ds(i, SUBLANES, stride=seg), :]


def _from_segments(src_ref, dst_ref, seg):
    for i in range(seg):
        dst_ref[pl.ds(i, SUBLANES, stride=seg), :] = src_ref[i * SUBLANES:(i + 1) * SUBLANES, :]


def _slab(i):
    return pl.ds(pl.multiple_of(i * SUBLANES, SUBLANES), SUBLANES)


def _s5_scan_fwd(proj_main, bbd_re, bbd_im, cbd_re, cbd_im, dvec, tab, ptab, DS):
    L = proj_main.shape[0]
    nb = DS // S5_COLS
    tb = _blk(L, 512, SUBLANES)
    nt = L // tb
    seg = tb // SUBLANES

    def body(u_ref, bre_ref, bim_ref, cre_ref, cim_ref, d_ref, tab_ref, pt_ref, y_ref, sre_ref, sim_ref,
             up_ref, yp_ref, car_ref):
        t = pl.program_id(1)

        @pl.when(t == 0)
        def _():
            car_ref[...] = jnp.zeros_like(car_ref)

        _to_segments(u_ref, up_ref, seg)
        up = up_ref[...]
        sre_ref[...] = _dot(up, bre_ref[0])
        sim_ref[...] = _dot(up, bim_ref[0])
        ar, ai = tab_ref[0, 0], tab_ref[0, 1]

        def pass1(i, x):
            xr = ar * x[0] - ai * x[1] + sre_ref[_slab(i), :]
            xi = ar * x[1] + ai * x[0] + sim_ref[_slab(i), :]
            sre_ref[_slab(i), :] = xr
            sim_ref[_slab(i), :] = xi
            return xr, xi

        zero = jnp.zeros((SUBLANES, S5_LANES), F32)
        er, ei = lax.fori_loop(0, seg, pass1, (zero, zero))
        cin_r, cin_i = car_ref[0], car_ref[1]
        sr, si = _scan8(er, ei, tab_ref, 2, (1, 2, 4))
        pr, pi = tab_ref[0, 8], tab_ref[0, 9]
        sr, si = sr + pr * cin_r - pi * cin_i, si + pr * cin_i + pi * cin_r
        row0 = _iota2((SUBLANES, S5_LANES), 0) == 0
        cr = jnp.where(row0, cin_r, pltpu.roll(sr, 1, 0))
        ci = jnp.where(row0, cin_i, pltpu.roll(si, 1, 0))
        car_ref[0] = jnp.broadcast_to(sr[SUBLANES - 1:SUBLANES, :], sr.shape)
        car_ref[1] = jnp.broadcast_to(si[SUBLANES - 1:SUBLANES, :], si.shape)

        def pass2(i, _):
            qr, qi = pt_ref[0, 0, _slab(i), :], pt_ref[0, 1, _slab(i), :]
            sre_ref[_slab(i), :] += qr * cr - qi * ci
            sim_ref[_slab(i), :] += qr * ci + qi * cr
            return 0

        lax.fori_loop(0, seg, pass2, 0, unroll=4)
        yp_ref[...] = _dot(sre_ref[...], cre_ref[0], NT) - _dot(sim_ref[...], cim_ref[0], NT) + d_ref[...] * up
        _from_segments(yp_ref, y_ref, seg)

    return pl.pallas_call(
        body, name="s5_scan_fwd", grid=(nb, nt),
        in_specs=[
            pl.BlockSpec((tb, S5_COLS), lambda j, t: (t, j)),
            pl.BlockSpec((1, S5_COLS, S5_LANES), lambda j, t: (j, 0, 0)),
            pl.BlockSpec((1, S5_COLS, S5_LANES), lambda j, t: (j, 0, 0)),
            pl.BlockSpec((1, S5_COLS, S5_LANES), lambda j, t: (j, 0, 0)),
            pl.BlockSpec((1, S5_COLS, S5_LANES), lambda j, t: (j, 0, 0)),
            pl.BlockSpec((1, S5_COLS), lambda j, t: (0, j)),
            pl.BlockSpec((1, S5_TABS, SUBLANES, S5_LANES), lambda j, t: (j, 0, 0, 0)),
            pl.BlockSpec((1, 2, tb, S5_LANES), lambda j, t: (j, 0, 0, 0)),
        ],
        out_specs=[
            pl.BlockSpec((tb, S5_COLS), lambda j, t: (t, j)),
            pl.BlockSpec((tb, S5_LANES), lambda j, t: (t, j)),
            pl.BlockSpec((tb, S5_LANES), lambda j, t: (t, j)),
        ],
        out_shape=[jax.ShapeDtypeStruct((L, DS), F32),
                   jax.ShapeDtypeStruct((L, nb * S5_LANES), F32),
                   jax.ShapeDtypeStruct((L, nb * S5_LANES), F32)],
        scratch_shapes=[pltpu.VMEM((tb, S5_COLS), F32), pltpu.VMEM((tb, S5_COLS), F32),
                        pltpu.VMEM((2, SUBLANES, S5_LANES), F32)],
        compiler_params=pltpu.CompilerParams(dimension_semantics=("parallel", "arbitrary")),
    )(proj_main, bbd_re, bbd_im, cbd_re, cbd_im, dvec, tab, ptab)


def _s5_scan_bwd(dy, proj_main, s_re, s_im, bbd_re, bbd_im, cbd_re, cbd_im, dvec, tab, ptab, d_s5, DS):
    L = proj_main.shape[0]
    nb = DS // S5_COLS
    tb = _blk(L, 512, SUBLANES)
    nt = L // tb
    seg = tb // SUBLANES
    tb8 = tb // SUBLANES

    def body(dy_ref, u_ref, sre_ref, sim_ref, pre_ref, pim_ref, bre_ref, bim_ref, cre_ref, cim_ref, d_ref, tab_ref, pt_ref,
             _ds5_ref, du_ref, gd_ref, gcre_ref, gcim_ref, gbre_ref, gbim_ref, gare_ref, gaim_ref,
             lre_ref, lim_ref, up_ref, dyp_ref, dup_ref, duo_ref, car_ref):
        t = pl.program_id(1)

        @pl.when(t == 0)
        def _():
            car_ref[...] = jnp.zeros_like(car_ref)
            gd_ref[...] = jnp.zeros_like(gd_ref)
            gcre_ref[...] = jnp.zeros_like(gcre_ref)
            gcim_ref[...] = jnp.zeros_like(gcim_ref)
            gbre_ref[...] = jnp.zeros_like(gbre_ref)
            gbim_ref[...] = jnp.zeros_like(gbim_ref)
            gare_ref[...] = jnp.zeros_like(gare_ref)
            gaim_ref[...] = jnp.zeros_like(gaim_ref)

        _to_segments(dy_ref, dyp_ref, seg)
        _to_segments(u_ref, up_ref, seg)
        dyv = dyp_ref[...]
        u = up_ref[...]
        gd_ref[...] += jnp.sum(dyv * u, axis=0, keepdims=True)
        lre_ref[...] = _dot(dyv, cre_ref[0])
        lim_ref[...] = -_dot(dyv, cim_ref[0])
        gcre_ref[0] += _dot(dyv, sre_ref[...], TN)
        gcim_ref[0] -= _dot(dyv, sim_ref[...], TN)
        ar, ai = tab_ref[0, 0], -tab_ref[0, 1]

        def pass1(k, x):
            i = seg - 1 - k
            xr = ar * x[0] - ai * x[1] + lre_ref[_slab(i), :]
            xi = ar * x[1] + ai * x[0] + lim_ref[_slab(i), :]
            lre_ref[_slab(i), :] = xr
            lim_ref[_slab(i), :] = xi
            return xr, xi

        zero = jnp.zeros((SUBLANES, S5_LANES), F32)
        er, ei = lax.fori_loop(0, seg, pass1, (zero, zero))
        cin_r, cin_i = car_ref[0], car_ref[1]
        lr, li = _scan8(er, ei, tab_ref, 10, (7, 6, 4))
        pr, pi = tab_ref[0, 16], tab_ref[0, 17]
        lr, li = lr + pr * cin_r - pi * cin_i, li + pr * cin_i + pi * cin_r
        rows = _iota2((SUBLANES, S5_LANES), 0)
        cr = jnp.where(rows == SUBLANES - 1, cin_r, pltpu.roll(lr, SUBLANES - 1, 0))
        ci = jnp.where(rows == SUBLANES - 1, cin_i, pltpu.roll(li, SUBLANES - 1, 0))
        car_ref[0] = jnp.broadcast_to(lr[0:1, :], lr.shape)
        car_ref[1] = jnp.broadcast_to(li[0:1, :], li.shape)

        first = (t == nt - 1).astype(F32)
        head_re = jnp.broadcast_to(pre_ref[SUBLANES - 1:SUBLANES, :], zero.shape) * (1.0 - first)
        head_im = jnp.broadcast_to(pim_ref[SUBLANES - 1:SUBLANES, :], zero.shape) * (1.0 - first)
        last = _slab(seg - 1)
        sp0_re = jnp.where(rows == 0, head_re, pltpu.roll(sre_ref[last, :], 1, 0))
        sp0_im = jnp.where(rows == 0, head_im, pltpu.roll(sim_ref[last, :], 1, 0))

        def pass2(i, acc):
            j = seg - 1 - i
            qr, qi = pt_ref[0, 0, _slab(j), :], -pt_ref[0, 1, _slab(j), :]
            xr = lre_ref[_slab(i), :] + qr * cr - qi * ci
            xi = lim_ref[_slab(i), :] + qr * ci + qi * cr
            lre_ref[_slab(i), :] = xr
            lim_ref[_slab(i), :] = xi
            prev = _slab(jnp.maximum(i - 1, 0))
            sp_re = jnp.where(i == 0, sp0_re, sre_ref[prev, :])
            sp_im = jnp.where(i == 0, sp0_im, sim_ref[prev, :])
            return acc[0] + sp_re * xr + sp_im * xi, acc[1] + sp_re * xi - sp_im * xr

        acc_re, acc_im = lax.fori_loop(0, seg, pass2, (zero, zero), unroll=2)
        gare_ref[...] += jnp.sum(acc_re, axis=0, keepdims=True)
        gaim_ref[...] += jnp.sum(acc_im, axis=0, keepdims=True)
        lre = lre_ref[...]
        lim = lim_ref[...]
        dup_ref[...] = dyv * d_ref[...] + _dot(lre, bre_ref[0], NT) + _dot(lim, bim_ref[0], NT)
        _from_segments(dup_ref, duo_ref, seg)
        du_ref[...] = duo_ref[...].astype(BF16)
        gbre_ref[0] += _dot(u, lre, TN)
        gbim_ref[0] += _dot(u, lim, TN)

    rt = lambda t: nt - 1 - t
    col = pl.BlockSpec((tb, S5_COLS), lambda j, t: (rt(t), j))
    st = pl.BlockSpec((tb, S5_LANES), lambda j, t: (rt(t), j))
    prev = pl.BlockSpec((SUBLANES, S5_LANES), lambda j, t: (jnp.maximum(rt(t) * tb8 - 1, 0), j))
    bmat = pl.BlockSpec((1, S5_COLS, S5_LANES), lambda j, t: (j, 0, 0))
    cmat = bmat
    return pl.pallas_call(
        body, name="s5_scan_bwd", grid=(nb, nt),
        in_specs=[col, col, st, st, prev, prev, bmat, bmat, cmat, cmat,
                  pl.BlockSpec((1, S5_COLS), lambda j, t: (0, j)),
                  pl.BlockSpec((1, S5_TABS, SUBLANES, S5_LANES), lambda j, t: (j, 0, 0, 0)),
                  pl.BlockSpec((1, 2, tb, S5_LANES), lambda j, t: (j, 0, 0, 0)),
                  pl.BlockSpec(memory_space=pl.ANY)],
        out_specs=[col, pl.BlockSpec((1, S5_COLS), lambda j, t: (0, j)), cmat, cmat, bmat, bmat,
                   pl.BlockSpec((1, S5_LANES), lambda j, t: (0, j)), pl.BlockSpec((1, S5_LANES), lambda j, t: (0, j))],
        input_output_aliases={13: 0},
        out_shape=[jax.ShapeDtypeStruct((L, 2 * DS), BF16), jax.ShapeDtypeStruct((1, DS), F32),
                   jax.ShapeDtypeStruct((nb, S5_COLS, S5_LANES), F32), jax.ShapeDtypeStruct((nb, S5_COLS, S5_LANES), F32),
                   jax.ShapeDtypeStruct((nb, S5_COLS, S5_LANES), F32), jax.ShapeDtypeStruct((nb, S5_COLS, S5_LANES), F32),
                   jax.ShapeDtypeStruct((1, nb * S5_LANES), F32), jax.ShapeDtypeStruct((1, nb * S5_LANES), F32)],
        scratch_shapes=[pltpu.VMEM((tb, S5_LANES), F32), pltpu.VMEM((tb, S5_LANES), F32)]
        + [pltpu.VMEM((tb, S5_COLS), F32)] * 4 + [pltpu.VMEM((2, SUBLANES, S5_LANES), F32)],
        compiler_params=pltpu.CompilerParams(dimension_semantics=("parallel", "arbitrary")),
    )(dy, proj_main, s_re, s_im, s_re, s_im, bbd_re, bbd_im, cbd_re, cbd_im, dvec, tab, ptab, d_s5)


def _s5_post_fwd(y_pre, proj_main, glu_w, glu_b, DS):
    L = y_pre.shape[0]
    tr = _blk(L, 256, SUBLANES)

    def body(y_ref, z_ref, w_ref, b_ref, o_ref, t_ref):
        y1 = _gelu(y_ref[...])
        t = _dot(y1, w_ref[...]) + b_ref[...]
        t_ref[...] = t
        z = z_ref[...]
        o_ref[...] = (y1 * _sigmoid(t) * (z * _sigmoid(z))).astype(BF16)

    row = pl.BlockSpec((tr, DS), lambda i: (i, 0))
    return pl.pallas_call(
        body, name="s5_post_fwd", grid=(L // tr,),
        in_specs=[row, pl.BlockSpec((tr, DS), lambda i: (i, 1)), pl.BlockSpec((DS, DS), lambda i: (0, 0)),
                  pl.BlockSpec((1, DS), lambda i: (0, 0))],
        out_specs=[row, row],
        out_shape=[jax.ShapeDtypeStruct((L, 2 * DS), BF16), jax.ShapeDtypeStruct((L, DS), F32)],
        compiler_params=pltpu.CompilerParams(dimension_semantics=("parallel",)),
    )(y_pre, proj_main, glu_w, glu_b)


def _s5_post_bwd(d_ycat, y_pre, proj_main, t_pre, glu_w, DS):
    L = y_pre.shape[0]
    tr = _blk(L, 256, SUBLANES)

    def body(dy_ref, y_ref, z_ref, t_ref, w_ref, dyp_ref, dz_ref, dt_ref, y1_ref, gb_ref):
        i = pl.program_id(0)

        @pl.when(i == 0)
        def _():
            gb_ref[...] = jnp.zeros_like(gb_ref)

        dy = dy_ref[...]
        yp = y_ref[...]
        z = z_ref[...]
        y1 = _gelu(yp)
        sg = _sigmoid(t_ref[...])
        sz = _sigmoid(z)
        c = y1 * sg
        d_c = dy * (z * sz)
        dz_ref[...] = (dy * c * (sz * (1.0 + z * (1.0 - sz)))).astype(BF16)
        d_t = d_c * y1 * sg * (1.0 - sg)
        gb_ref[...] += jnp.sum(d_t, axis=0, keepdims=True)
        dt_ref[...] = d_t.astype(BF16)
        y1_ref[...] = y1.astype(BF16)
        d_y1 = d_c * sg + _dot(d_t, w_ref[...], NT)
        dyp_ref[...] = d_y1 * _gelu_grad(yp)

    row = pl.BlockSpec((tr, DS), lambda i: (i, 0))
    return pl.pallas_call(
        body, name="s5_post_bwd", grid=(L // tr,),
        in_specs=[row, row, pl.BlockSpec((tr, DS), lambda i: (i, 1)), row, pl.BlockSpec((DS, DS), lambda i: (0, 0))],
        out_specs=[row, pl.BlockSpec((tr, DS), lambda i: (i, 1)), row, row, pl.BlockSpec((1, DS), lambda i: (0, 0))],
        out_shape=[jax.ShapeDtypeStruct((L, DS), F32), jax.ShapeDtypeStruct((L, 2 * DS), BF16),
                   jax.ShapeDtypeStruct((L, DS), BF16), jax.ShapeDtypeStruct((L, DS), BF16),
                   jax.ShapeDtypeStruct((1, DS), F32)],
        compiler_params=pltpu.CompilerParams(dimension_semantics=("arbitrary",)),
    )(d_ycat, y_pre, proj_main, t_pre, glu_w)


def _gla_gates(glow, gu_ref, gb_ref):
    a = _dot(glow, gu_ref[...]) + gb_ref[...]
    lg = (jnp.minimum(a, 0.0) - jnp.log(1.0 + jnp.exp(-jnp.abs(a)))) * (1.0 / GLA_TAU)
    ri = lax.broadcasted_iota(jnp.int32, (GLA_CHUNK, GLA_CHUNK), 0)
    ci = lax.broadcasted_iota(jnp.int32, (GLA_CHUNK, GLA_CHUNK), 1)
    b = _dot_hi((ri >= ci).astype(F32), lg)
    b_last = jnp.sum(lg, axis=0, keepdims=True)
    return a, b, b_last, ri >= ci


def _gla_specs(DS, DK, DV, c, cmap):
    return [
        pl.BlockSpec((c, DK), lambda n: (cmap(n), 2 * DS // DK)),
        pl.BlockSpec((c, DK), lambda n: (cmap(n), 2 * DS // DK + 1)),
        pl.BlockSpec((c, DV), lambda n: (cmap(n), (2 * DS + 2 * DK) // DV)),
        pl.BlockSpec((c, DV), lambda n: (cmap(n), (2 * DS + 2 * DK) // DV + 1)),
    ]


def _gla_fwd(proj_main, proj_low, gate_up_pad, gate_bias, norm_w, ycat, DS, DK, DV):
    L = proj_main.shape[0]
    nc = L // GLA_CHUNK
    cps = math.gcd(GLA_STEP_CHUNKS, nc)
    nh = DK // GLA_HK
    scale = GLA_HK ** -0.5

    def body(q_ref, k_ref, v_ref, z_ref, gl_ref, gu_ref, gb_ref, nw_ref, _yc_ref, y_ref, sp_ref, st_ref):
        n = pl.program_id(0)

        @pl.when(n == 0)
        def _():
            st_ref[...] = jnp.zeros_like(st_ref)

        pairs = [(sc, h) for sc in range(cps) for h in range(nh)]
        rows = lambda sc: slice(sc * GLA_CHUNK, (sc + 1) * GLA_CHUNK)
        kcol = lambda h: slice(h * GLA_HK, (h + 1) * GLA_HK)
        vcol = lambda h: slice(h * GLA_HV, (h + 1) * GLA_HV)
        gates = [_gla_gates(gl_ref[rows(sc), :], gu_ref, gb_ref) for sc in range(cps)]
        qe, dec, o_in, kv = {}, {}, {}, {}
        for sc, h in pairs:
            _, b, b_last, mask = gates[sc]
            bh, bl = b[:, kcol(h)], b_last[:, kcol(h)]
            qe[sc, h] = (q_ref[rows(sc), kcol(h)] * scale) * jnp.exp(bh)
            kh = k_ref[rows(sc), kcol(h)]
            vh = v_ref[rows(sc), vcol(h)]
            attn = jnp.where(mask, _dot(qe[sc, h], kh * jnp.exp(-bh), NT), 0.0)
            o_in[sc, h] = _dot(attn, vh)
            kv[sc, h] = _dot(vh, kh * jnp.exp(bl - bh), TN)
            dec[sc, h] = jnp.exp(bl)
        for sc, h in pairs:
            st = st_ref[h]
            sp_ref[sc, h] = st
            o = o_in[sc, h] + _dot(qe[sc, h], st, NT)
            st_ref[h] = dec[sc, h] * st + kv[sc, h]
            r = lax.rsqrt(jnp.mean(o * o, axis=-1, keepdims=True) + EPS)
            z = z_ref[rows(sc), vcol(h)]
            y_ref[rows(sc), vcol(h)] = (o * r * nw_ref[...] * (z * _sigmoid(z))).astype(BF16)

    c = cps * GLA_CHUNK
    return pl.pallas_call(
        body, name="gla_fwd", grid=(nc // cps,),
        in_specs=_gla_specs(DS, DK, DV, c, lambda n: n) + [
            pl.BlockSpec((c, LANES), lambda n: (n, 0)),
            pl.BlockSpec((LANES, DK), lambda n: (0, 0)),
            pl.BlockSpec((1, DK), lambda n: (0, 0)),
            pl.BlockSpec((1, GLA_HV), lambda n: (0, 0)),
            pl.BlockSpec(memory_space=pl.ANY),
        ],
        out_specs=[pl.BlockSpec((c, DV), lambda n: (n, DS // DV)),
                   pl.BlockSpec((cps, nh, GLA_HV, GLA_HK), lambda n: (n, 0, 0, 0))],
        input_output_aliases={8: 0},
        out_shape=[jax.ShapeDtypeStruct(ycat.shape, BF16), jax.ShapeDtypeStruct((nc, nh, GLA_HV, GLA_HK), F32)],
        scratch_shapes=[pltpu.VMEM((nh, GLA_HV, GLA_HK), F32)],
        compiler_params=pltpu.CompilerParams(dimension_semantics=("arbitrary",)),
    )(proj_main, proj_main, proj_main, proj_main, proj_low, gate_up_pad, gate_bias, norm_w, ycat)


def _gla_bwd(d_ycat, proj_main, proj_low, s_prev, gate_up_pad, gate_bias, norm_w, DS, DK, DV):
    L = proj_main.shape[0]
    nc = L // GLA_CHUNK
    cps = math.gcd(GLA_STEP_CHUNKS, nc)
    nh = DK // GLA_HK
    scale = GLA_HK ** -0.5

    def body(dy_ref, q_ref, k_ref, v_ref, z_ref, gl_ref, sp_ref, gu_ref, gb_ref, nw_ref,
             dg_ref, da_ref, gnw_ref, ggb_ref, dst_ref):
        n = pl.program_id(0)

        @pl.when(n == 0)
        def _():
            dst_ref[...] = jnp.zeros_like(dst_ref)
            gnw_ref[...] = jnp.zeros_like(gnw_ref)
            ggb_ref[...] = jnp.zeros_like(ggb_ref)

        last_row = lax.broadcasted_iota(jnp.int32, (GLA_CHUNK, GLA_HK), 0) == GLA_CHUNK - 1
        ri = lax.broadcasted_iota(jnp.int32, (GLA_CHUNK, GLA_CHUNK), 0)
        ci = lax.broadcasted_iota(jnp.int32, (GLA_CHUNK, GLA_CHUNK), 1)
        upper = (ci >= ri).astype(F32)
        nw = nw_ref[...]
        for sc in reversed(range(cps)):
            rs = slice(sc * GLA_CHUNK, (sc + 1) * GLA_CHUNK)
            a, b, b_last, mask = _gla_gates(gl_ref[rs, :], gu_ref, gb_ref)
            for h in range(nh):
                ks = slice(h * GLA_HK, (h + 1) * GLA_HK)
                vs = slice(h * GLA_HV, (h + 1) * GLA_HV)
                bh, bl = b[:, ks], b_last[:, ks]
                e = jnp.exp(bh)
                einv = jnp.exp(-bh)
                etail = jnp.exp(bl - bh)
                dec = jnp.exp(bl)
                qe = (q_ref[rs, ks] * scale) * e
                kh = k_ref[rs, ks]
                ke = kh * einv
                ktail = kh * etail
                vh = v_ref[rs, vs]
                st = sp_ref[sc, h]
                dst = dst_ref[h]
                attn = jnp.where(mask, _dot(qe, ke, NT), 0.0)
                o = _dot(attn, vh) + _dot(qe, st, NT)
                r = lax.rsqrt(jnp.mean(o * o, axis=-1, keepdims=True) + EPS)
                nrm = o * r
                z = z_ref[rs, vs]
                sz = _sigmoid(z)
                dy = dy_ref[rs, vs]
                dg_ref[rs, 2 * DK + DV + h * GLA_HV:2 * DK + DV + (h + 1) * GLA_HV] = (
                    dy * nrm * nw * (sz * (1.0 + z * (1.0 - sz)))).astype(BF16)
                d_on = dy * (z * sz)
                gnw_ref[...] += jnp.sum(d_on * nrm, axis=0, keepdims=True)
                d_n = d_on * nw
                d_o = r * (d_n - nrm * jnp.mean(d_n * nrm, axis=-1, keepdims=True))
                d_attn = jnp.where(mask, _dot(d_o, vh, NT), 0.0)
                dg_ref[rs, 2 * DK + h * GLA_HV:2 * DK + (h + 1) * GLA_HV] = (
                    _dot(attn, d_o, TN) + _dot(ktail, dst, NT)).astype(BF16)
                d_qe = _dot(d_attn, ke) + _dot(d_o, st)
                d_ke = _dot(d_attn, qe, TN)
                d_kt = _dot(vh, dst)
                d_dec = jnp.sum(dst * st, axis=0, keepdims=True)
                dst_ref[h] = dec * dst + _dot(d_o, qe, TN)
                dg_ref[rs, ks] = (d_qe * scale * e).astype(BF16)
                dg_ref[rs, DK + h * GLA_HK:DK + (h + 1) * GLA_HK] = (d_ke * einv + d_kt * etail).astype(BF16)
                d_bl = jnp.sum(d_kt * ktail, axis=0, keepdims=True) + d_dec * dec
                d_b = d_qe * qe - d_ke * ke - d_kt * ktail + jnp.where(last_row, d_bl, 0.0)
                d_lg = _dot_hi(upper, d_b)
                d_a = d_lg * (1.0 / GLA_TAU) * _sigmoid(-a[:, ks])
                ggb_ref[:, ks] += jnp.sum(d_a, axis=0, keepdims=True)
                da_ref[rs, ks] = d_a.astype(BF16)

    c = cps * GLA_CHUNK
    ns = nc // cps
    rn = lambda n: ns - 1 - n
    return pl.pallas_call(
        body, name="gla_bwd", grid=(ns,),
        in_specs=[pl.BlockSpec((c, DV), lambda n: (rn(n), DS // DV))] + _gla_specs(DS, DK, DV, c, rn) + [
            pl.BlockSpec((c, LANES), lambda n: (rn(n), 0)),
            pl.BlockSpec((cps, nh, GLA_HV, GLA_HK), lambda n: (rn(n), 0, 0, 0)),
            pl.BlockSpec((LANES, DK), lambda n: (0, 0)),
            pl.BlockSpec((1, DK), lambda n: (0, 0)),
            pl.BlockSpec((1, GLA_HV), lambda n: (0, 0)),
        ],
        out_specs=[pl.BlockSpec((c, 2 * DK + 2 * DV), lambda n: (rn(n), 0)),
                   pl.BlockSpec((c, DK), lambda n: (rn(n), 0)),
                   pl.BlockSpec((1, GLA_HV), lambda n: (0, 0)), pl.BlockSpec((1, DK), lambda n: (0, 0))],
        out_shape=[jax.ShapeDtypeStruct((L, 2 * DK + 2 * DV), BF16),
                   jax.ShapeDtypeStruct((L, DK), BF16),
                   jax.ShapeDtypeStruct((1, GLA_HV), F32), jax.ShapeDtypeStruct((1, DK), F32)],
        scratch_shapes=[pltpu.VMEM((nh, GLA_HV, GLA_HK), F32)],
        compiler_params=pltpu.CompilerParams(dimension_semantics=("arbitrary",)),
    )(d_ycat, proj_main, proj_main, proj_main, proj_main, proj_low, s_prev, gate_up_pad, gate_bias, norm_w)


def _adamw_math(w, g, m, v):
    c1 = 1.0 - ADAM_B1 ** ADAM_STEP
    c2 = 1.0 - ADAM_B2 ** ADAM_STEP
    m_ = ADAM_B1 * m + (1.0 - ADAM_B1) * g
    v_ = ADAM_B2 * v + (1.0 - ADAM_B2) * (g * g)
    return -ADAM_LR * ((m_ / c1) / (jnp.sqrt(v_ / c2) + ADAM_EPS) + ADAM_WD * w), m_, v_


def _adamw_small(g_row, g_a, g_bc, ws, ms, vs):
    n = len(ws)
    nvec = n - 6

    def body(*refs):
        grow_ref, ga_ref, gbc_ref = refs[:3]
        w_refs, m_refs, v_refs = refs[3:3 + n], refs[3 + n:3 + 2 * n], refs[3 + 2 * n:3 + 3 * n]
        outs = refs[3 + 3 * n:]
        off = 0
        for i in range(n):
            if i < nvec:
                width = ws[i].shape[1]
                g = grow_ref[:, off:off + width]
                off += width
            elif i < nvec + 2:
                g = ga_ref[i - nvec]
            else:
                g = gbc_ref[i - nvec - 2]
            d, m_, v_ = _adamw_math(w_refs[i][...], g, m_refs[i][...], v_refs[i][...])
            outs[i][...] = g
            outs[n + i][...] = d
            outs[2 * n + i][...] = m_
            outs[3 * n + i][...] = v_

    vm = pl.BlockSpec(memory_space=pltpu.VMEM)
    outs = pl.pallas_call(
        body, name="adamw_small",
        in_specs=[vm] * (3 + 3 * n), out_specs=[vm] * (4 * n),
        out_shape=[jax.ShapeDtypeStruct(w.shape, F32) for w in ws] * 4,
    )(g_row, g_a, g_bc, *ws, *ms, *vs)
    return [outs[k * n:(k + 1) * n] for k in range(4)]


def _my_pos():
    return lax.axis_index("x"), lax.axis_index("y"), lax.axis_index("c")


def _gather_weights(shards):
    n = len(shards)
    halves = [s.shape[0] // 2 for s in shards]

    def body(*refs):
        ins, outs = refs[:n], refs[n:2 * n]
        send_sems, recv_sems = refs[2 * n:]
        x, y, c = _my_pos()
        me = 2 * x + y

        def piece(a, chip, half):
            return outs[a].at[chip, pl.ds(half * halves[a], halves[a]), :]

        def copy(a, k, src_chip, half, to):
            sl = piece(a, src_chip, half)
            return pltpu.make_async_remote_copy(src_ref=sl, dst_ref=sl, send_sem=send_sems.at[a, k], recv_sem=recv_sems.at[a, k],
                                                device_id=to, device_id_type=MESH)

        def first(a, d, to):
            src = ins[a].at[pl.ds(c * halves[a], halves[a]), :]
            return pltpu.make_async_remote_copy(src_ref=src, dst_ref=piece(a, me, c), send_sem=send_sems.at[a, d - 1],
                                                recv_sem=recv_sems.at[a, d - 1], device_id=to, device_id_type=MESH)

        sent = []
        for d in (1, 2, 3):
            to = (x ^ (d >> 1), y ^ (d & 1), c)
            for a in range(n):
                cp = first(a, d, to)
                cp.start()
                sent.append(cp)
        for d in (1, 2, 3):
            chip = (x ^ (d >> 1)) * 2 + (y ^ (d & 1))
            for a in range(n):
                copy(a, d - 1, chip, c, (x, y, c)).wait_recv()
                fw = copy(a, 2 + d, chip, c, (x, y, 1 - c))
                fw.start()
                sent.append(fw)
        for d in (1, 2, 3):
            chip = (x ^ (d >> 1)) * 2 + (y ^ (d & 1))
            for a in range(n):
                copy(a, 2 + d, chip, 1 - c, (x, y, c)).wait_recv()
        for cp in sent:
            cp.wait_send()

    hbm = pl.BlockSpec(memory_space=pltpu.HBM)
    return pl.pallas_call(
        body, name="gather_weights",
        in_specs=[hbm] * n, out_specs=[hbm] * n,
        out_shape=[jax.ShapeDtypeStruct((4,) + s.shape, s.dtype) for s in shards],
        scratch_shapes=[pltpu.SemaphoreType.DMA((n, 6)), pltpu.SemaphoreType.DMA((n, 6))],
    )(*shards)


def _pair_exchange(gs):
    n = len(gs)

    def body(*refs):
        ins, outs = refs[:n], refs[n:2 * n]
        send_sems, recv_sems = refs[2 * n:]
        x, y, c = _my_pos()
        sent = []
        for a in range(n):
            hrows = gs[a].shape[1] // 2
            cp = pltpu.make_async_remote_copy(
                src_ref=ins[a].at[:, pl.ds((1 - c) * hrows, hrows), :], dst_ref=outs[a], send_sem=send_sems.at[a],
                recv_sem=recv_sems.at[a], device_id=(x, y, 1 - c), device_id_type=MESH)
            cp.start()
            sent.append(cp)
        for cp in sent:
            cp.wait()

    hbm = pl.BlockSpec(memory_space=pltpu.HBM)
    return pl.pallas_call(
        body, name="grad_pair_exchange", in_specs=[hbm] * n, out_specs=[hbm] * n,
        out_shape=[jax.ShapeDtypeStruct((g.shape[0], g.shape[1] // 2, g.shape[2]), g.dtype) for g in gs],
        scratch_shapes=[pltpu.SemaphoreType.DMA((n,)), pltpu.SemaphoreType.DMA((n,))],
    )(*gs)


def _pair_add(g, got, c_arr, name):
    nk, rows2, cols = g.shape
    hrows = rows2 // 2
    tr = _blk(hrows, 256, 2 * SUBLANES)
    nb = hrows // tr

    def body(c_ref, a_ref, b_ref, o_ref):
        o_ref[...] = (a_ref[...].astype(F32) + b_ref[...].astype(F32)).astype(o_ref.dtype)

    return pl.pallas_call(
        body, name=name,
        grid_spec=pltpu.PrefetchScalarGridSpec(
            num_scalar_prefetch=1, grid=(nk, nb),
            in_specs=[pl.BlockSpec((1, tr, cols), lambda k, i, c_ref: (k, c_ref[0] * nb + i, 0)),
                      pl.BlockSpec((1, tr, cols), lambda k, i, c_ref: (k, i, 0))],
            out_specs=pl.BlockSpec((1, tr, cols), lambda k, i, c_ref: (k, i, 0))),
        out_shape=jax.ShapeDtypeStruct((nk, hrows, cols), g.dtype),
        compiler_params=pltpu.CompilerParams(dimension_semantics=("parallel", "parallel")),
    )(c_arr, g, got)


def _chip_scatter_copies(srcs, lands, send_sems, recv_sems):
    x, y, c = _my_pos()
    copies = []
    for d in (1, 2, 3):
        tx, ty = x ^ (d >> 1), y ^ (d & 1)
        for a in range(len(srcs)):
            copies.append(pltpu.make_async_remote_copy(
                src_ref=srcs[a].at[2 * tx + ty], dst_ref=lands[a].at[d - 1], send_sem=send_sems.at[3 * a + d - 1],
                recv_sem=recv_sems.at[3 * a + d - 1], device_id=(tx, ty, c), device_id_type=MESH))
    return copies


def _chip_scatter_start(pss):
    n = len(pss)

    def body(*refs):
        srcs, lands = refs[:n], refs[n:2 * n]
        send_sems, recv_sems = refs[2 * n], refs[2 * n + 1]
        token = refs[-1]
        for cp in _chip_scatter_copies(srcs, lands, send_sems, recv_sems):
            cp.start()
        token[...] = jnp.zeros_like(token)

    hbm = pl.BlockSpec(memory_space=pltpu.HBM)
    sem = pl.BlockSpec(memory_space=pltpu.SEMAPHORE)
    land_shapes = [(3,) + p.shape[1:] for p in pss]
    outs = pl.pallas_call(
        body, name="grad_chip_scatter_start",
        in_specs=[hbm] * (2 * n),
        out_specs=[sem, sem] + [hbm] * (2 * n) + [pl.BlockSpec(memory_space=pltpu.VMEM)],
        out_shape=[pltpu.SemaphoreType.DMA((3 * n,)), pltpu.SemaphoreType.DMA((3 * n,))]
        + [pltpu.HBM(p.shape, p.dtype) for p in pss]
        + [pltpu.HBM(s, p.dtype) for s, p in zip(land_shapes, pss)]
        + [jax.ShapeDtypeStruct((SUBLANES, LANES), F32)],
        input_output_aliases={i: 2 + i for i in range(2 * n)},
        compiler_params=pltpu.CompilerParams(has_side_effects=pltpu.SideEffectType.DATAFLOW_SIDE_EFFECTING),
    )(*[pltpu.with_memory_space_constraint(p, pltpu.HBM) for p in pss],
      *[pltpu.with_memory_space_constraint(lax.empty(s, p.dtype), pltpu.HBM) for s, p in zip(land_shapes, pss)])
    return outs[0], outs[1], outs[2:2 + n], outs[2 + n:2 + 2 * n], outs[-1]


def _chip_scatter_wait(send_sems, recv_sems, srcs, lands, after):
    n = len(srcs)

    def body(*refs):
        src_refs, land_refs = refs[:n], refs[n:2 * n]
        ssem, rsem = refs[2 * n], refs[2 * n + 1]
        for cp in _chip_scatter_copies(src_refs, land_refs, ssem, rsem):
            cp.wait_send()
            cp.wait_recv()

    hbm = pl.BlockSpec(memory_space=pltpu.HBM)
    sem = pl.BlockSpec(memory_space=pltpu.SEMAPHORE)
    outs = pl.pallas_call(
        body, name="grad_chip_scatter_wait",
        in_specs=[hbm] * (2 * n) + [sem, sem, pl.BlockSpec(memory_space=pl.ANY)],
        out_specs=[hbm] * (2 * n),
        out_shape=[pltpu.HBM(p.shape, p.dtype) for p in srcs] + [pltpu.HBM(p.shape, p.dtype) for p in lands],
        input_output_aliases={i: i for i in range(2 * n)},
        compiler_params=pltpu.CompilerParams(has_side_effects=pltpu.SideEffectType.DATAFLOW_SIDE_EFFECTING),
    )(*srcs, *lands, send_sems, recv_sems, after)
    return outs[:n], outs[n:]


def _chip_sum(ps, got, me_arr, name):
    _, hrows, cols = ps.shape
    tr = _blk(hrows, 256, 2 * SUBLANES)

    def body(me_ref, p_ref, g_ref, o_ref):
        acc = p_ref[0].astype(F32)
        for s in range(3):
            acc = acc + g_ref[s].astype(F32)
        o_ref[...] = acc

    return pl.pallas_call(
        body, name=name,
        grid_spec=pltpu.PrefetchScalarGridSpec(
            num_scalar_prefetch=1, grid=(hrows // tr,),
            in_specs=[pl.BlockSpec((1, tr, cols), lambda i, me_ref: (me_ref[0], i, 0)),
                      pl.BlockSpec((3, tr, cols), lambda i, me_ref: (0, i, 0))],
            out_specs=pl.BlockSpec((tr, cols), lambda i, me_ref: (i, 0))),
        out_shape=jax.ShapeDtypeStruct((hrows, cols), F32),
        compiler_params=pltpu.CompilerParams(dimension_semantics=("parallel",)),
    )(me_arr, ps, got)


def _pair_swap(halves):
    n = len(halves)

    def body(*refs):
        ins, outs = refs[:n], refs[n:2 * n]
        send_sems, recv_sems = refs[2 * n:]
        x, y, c = _my_pos()
        sent = []
        for a in range(n):
            cp = pltpu.make_async_remote_copy(src_ref=ins[a], dst_ref=outs[a], send_sem=send_sems.at[a], recv_sem=recv_sems.at[a],
                                              device_id=(x, y, 1 - c), device_id_type=MESH)
            cp.start()
            sent.append(cp)
        for cp in sent:
            cp.wait()

    hbm = pl.BlockSpec(memory_space=pltpu.HBM)
    return pl.pallas_call(
        body, name="grad_pair_swap", in_specs=[hbm] * n, out_specs=[hbm] * n,
        out_shape=[jax.ShapeDtypeStruct(h.shape, h.dtype) for h in halves],
        scratch_shapes=[pltpu.SemaphoreType.DMA((n,)), pltpu.SemaphoreType.DMA((n,))],
    )(*halves)


def _adamw_sharded(w, g_own, g_other, m, v, c_arr, name):
    R, C = w.shape
    hrows = R // 2
    tr = _blk(hrows, 256, SUBLANES)
    nbh = hrows // tr
    c1 = 1.0 - ADAM_B1 ** ADAM_STEP
    c2 = 1.0 - ADAM_B2 ** ADAM_STEP

    def body(c_ref, w_ref, go_ref, gx_ref, m_ref, v_ref, g_ref, d_ref, nm_ref, nv_ref):
        mine = (pl.program_id(0) // nbh) == c_ref[0]
        g_ = jnp.where(mine, go_ref[...], gx_ref[...])
        g_ref[...] = g_
        m_ = ADAM_B1 * m_ref[...] + (1.0 - ADAM_B1) * g_
        v_ = ADAM_B2 * v_ref[...] + (1.0 - ADAM_B2) * (g_ * g_)
        nm_ref[...] = m_
        nv_ref[...] = v_
        d_ref[...] = -ADAM_LR * ((m_ / c1) / (jnp.sqrt(v_ / c2) + ADAM_EPS) + ADAM_WD * w_ref[...])

    blk = pl.BlockSpec((tr, C), lambda i, c_ref: (i, 0))
    hblk = pl.BlockSpec((tr, C), lambda i, c_ref: (i % nbh, 0))
    sd = jax.ShapeDtypeStruct((R, C), F32)
    return pl.pallas_call(
        body, name=name,
        grid_spec=pltpu.PrefetchScalarGridSpec(
            num_scalar_prefetch=1, grid=(2 * nbh,),
            in_specs=[blk, hblk, hblk, blk, blk], out_specs=[blk] * 4),
        out_shape=[sd] * 4,
        compiler_params=pltpu.CompilerParams(dimension_semantics=("parallel",)),
    )(c_arr, w, g_own, g_other, m, v)


def _allreduce_small(arrs):
    n = len(arrs)
    rows = [a.shape[-2] // 8 for a in arrs]

    def piece(ref, a, p):
        start = p * rows[a]
        if rows[a] % SUBLANES == 0:
            start = pl.multiple_of(start, SUBLANES)
        return ref.at[..., pl.ds(start, rows[a]), :]

    def body(*refs):
        v_refs, o_refs, got_refs = refs[:n], refs[n:2 * n], refs[2 * n:3 * n]
        send_sems, recv_sems = refs[3 * n:]
        x, y, c = _my_pos()
        me = 4 * x + 2 * y + c

        def peer(d):
            return (x ^ (d >> 2), y ^ ((d >> 1) & 1), c ^ (d & 1))

        def lin(p):
            return 4 * p[0] + 2 * p[1] + p[2]

        sent = []
        for d in range(1, 8):
            to = peer(d)
            for a in range(n):
                cp = pltpu.make_async_remote_copy(
                    src_ref=piece(v_refs[a], a, lin(to)), dst_ref=got_refs[a].at[d],
                    send_sem=send_sems.at[0, d * n + a], recv_sem=recv_sems.at[0, d * n + a], device_id=to, device_id_type=MESH)
                cp.start()
                sent.append(cp)
        for a in range(n):
            acc = piece(v_refs[a], a, me)[...]
            for d in range(1, 8):
                sent[(d - 1) * n + a].wait_recv()
                acc = acc + got_refs[a][d]
            got_refs[a][0] = acc
            piece(o_refs[a], a, me)[...] = acc
        for d in range(1, 8):
            for a in range(n):
                cp = pltpu.make_async_remote_copy(
                    src_ref=got_refs[a].at[0], dst_ref=piece(o_refs[a], a, me),
                    send_sem=send_sems.at[1, d * n + a], recv_sem=recv_sems.at[1, d * n + a], device_id=peer(d), device_id_type=MESH)
                cp.start()
                sent.append(cp)
        for d in range(1, 8):
            for a in range(n):
                pltpu.make_async_remote_copy(
                    src_ref=got_refs[a].at[0], dst_ref=piece(o_refs[a], a, lin(peer(d))),
                    send_sem=send_sems.at[1, d * n + a], recv_sem=recv_sems.at[1, d * n + a], device_id=peer(d),
                    device_id_type=MESH).wait_recv()
        for cp in sent:
            cp.wait_send()

    vm = pl.BlockSpec(memory_space=pltpu.VMEM)
    return pl.pallas_call(
        body, name="allreduce_small", in_specs=[vm] * n, out_specs=[vm] * n,
        out_shape=[jax.ShapeDtypeStruct(a.shape, F32) for a in arrs],
        scratch_shapes=[pltpu.VMEM((8,) + a.shape[:-2] + (r, a.shape[-1]), F32) for a, r in zip(arrs, rows)]
        + [pltpu.SemaphoreType.DMA((2, 8 * n)), pltpu.SemaphoreType.DMA((2, 8 * n))],
    )(*arrs)


def kernel(x, pre_norm_w, w_in, s5_A_re, s5_A_im, s5_B_re, s5_B_im, s5_C_re, s5_C_im, s5_D, s5_log_dt, s5_glu_w, s5_glu_b, gla_gate_up, gla_gate_bias, gla_norm_w, w_out, post_norm_w, loss_target, m_pre_norm_w, m_w_in, m_s5_A_re, m_s5_A_im, m_s5_B_re, m_s5_B_im, m_s5_C_re, m_s5_C_im, m_s5_D, m_s5_log_dt, m_s5_glu_w, m_s5_glu_b, m_gla_gate_up, m_gla_gate_bias, m_gla_norm_w, m_w_out, m_post_norm_w, v_pre_norm_w, v_w_in, v_s5_A_re, v_s5_A_im, v_s5_B_re, v_s5_B_im, v_s5_C_re, v_s5_C_im, v_s5_D, v_s5_log_dt, v_s5_glu_w, v_s5_glu_b, v_gla_gate_up, v_gla_gate_bias, v_gla_norm_w, v_w_out, v_post_norm_w):
    names = ["pre_norm_w", "w_in", "s5_A_re", "s5_A_im", "s5_B_re", "s5_B_im", "s5_C_re", "s5_C_im", "s5_D", "s5_log_dt",
             "s5_glu_w", "s5_glu_b", "gla_gate_up", "gla_gate_bias", "gla_norm_w", "w_out", "post_norm_w"]
    W = dict(zip(names, (pre_norm_w, w_in, s5_A_re, s5_A_im, s5_B_re, s5_B_im, s5_C_re, s5_C_im, s5_D, s5_log_dt,
                         s5_glu_w, s5_glu_b, gla_gate_up, gla_gate_bias, gla_norm_w, w_out, post_norm_w)))
    M = dict(zip(names, (m_pre_norm_w, m_w_in, m_s5_A_re, m_s5_A_im, m_s5_B_re, m_s5_B_im, m_s5_C_re, m_s5_C_im, m_s5_D,
                         m_s5_log_dt, m_s5_glu_w, m_s5_glu_b, m_gla_gate_up, m_gla_gate_bias, m_gla_norm_w, m_w_out,
                         m_post_norm_w)))
    V = dict(zip(names, (v_pre_norm_w, v_w_in, v_s5_A_re, v_s5_A_im, v_s5_B_re, v_s5_B_im, v_s5_C_re, v_s5_C_im, v_s5_D,
                         v_s5_log_dt, v_s5_glu_w, v_s5_glu_b, v_gla_gate_up, v_gla_gate_bias, v_gla_norm_w, v_w_out,
                         v_post_norm_w)))
    sharded = ("w_in", "s5_glu_w", "w_out", "gla_gate_up")

    xb = x[0]
    tgt = loss_target[0]
    L, D = xb.shape
    DS = D // 2
    G = DS // S5_GROUP
    P = S5_STATE
    NB = DS // S5_COLS
    DV = D - DS
    DK = DV // 2
    WM = 2 * DS + 2 * DK + 2 * DV
    nsh = w_in.shape[2]

    chip = 2 * lax.axis_index("x") + lax.axis_index("y")
    own = [w_in[0].astype(BF16), s5_glu_w[0].astype(BF16), w_out[0].astype(BF16), gla_gate_up[0]]
    g_win, g_glu, g_wout, g_gup = [lax.dynamic_update_index_in_dim(g, o, chip, 0)
                                   for g, o in zip(_gather_weights(own), own)]
    w_full = jnp.moveaxis(g_win, 0, 1).reshape(D, 4 * nsh)
    w_main = w_full[:, :WM]
    w_low = jnp.pad(w_full[:, WM:], ((0, 0), (0, LANES - GLA_RANK)))
    glu_w = g_glu.reshape(DS, DS)
    wout = g_wout.reshape(D, D)
    gup = jnp.moveaxis(g_gup, 0, 1).reshape(GLA_RANK, DK)
    gup_pad = jnp.pad(gup, ((0, LANES - GLA_RANK), (0, 0))).astype(BF16)

    b_view = lambda t: jnp.transpose(t[0], (0, 2, 1)).reshape(G * S5_GROUP, P)
    b_back = lambda t: jnp.transpose(t.reshape(G, S5_GROUP, P), (0, 2, 1))[None]
    c_view = lambda t: t[0].reshape(G * S5_GROUP, P)
    c_back = lambda t: t.reshape(1, G, S5_GROUP, P)
    small = ["pre_norm_w", "post_norm_w", "s5_D", "s5_glu_b", "gla_gate_bias", "gla_norm_w", "s5_log_dt",
             "s5_A_re", "s5_A_im", "s5_B_re", "s5_B_im", "s5_C_re", "s5_C_im"]
    view = {n: (lambda t: t) for n in small[:7]}
    back = dict(view)
    view.update(s5_A_re=lambda t: t[0], s5_A_im=lambda t: t[0], s5_B_re=b_view, s5_B_im=b_view, s5_C_re=c_view, s5_C_im=c_view)
    back.update(s5_A_re=lambda t: t[None], s5_A_im=lambda t: t[None], s5_B_re=b_back, s5_B_im=b_back, s5_C_re=c_back,
                s5_C_im=c_back)
    Wv = {n: view[n](W[n]) for n in small}
    bbd_re, bbd_im, ct_re, ct_im, tab, ptab = _s5_prep_fwd(
        Wv["s5_A_re"], Wv["s5_A_im"], s5_log_dt, Wv["s5_B_re"], Wv["s5_B_im"], Wv["s5_C_re"], Wv["s5_C_im"],
        _blk(L, 512, SUBLANES) // SUBLANES)
    dvec = s5_D

    h = _prenorm_fwd(xb, pre_norm_w)
    proj_main = _mm(h, w_main, name="in_proj")
    proj_low = _mm(h, w_low, name="in_proj_low")
    y_pre, s_re, s_im = _s5_scan_fwd(proj_main, bbd_re, bbd_im, ct_re, ct_im, dvec, tab, ptab, DS)
    ycat, t_pre = _s5_post_fwd(y_pre, proj_main, glu_w, s5_glu_b, DS)
    ycat, s_prev = _gla_fwd(proj_main, proj_low, gup_pad, gla_gate_bias, gla_norm_w, ycat, DS, DK, DV)
    mixed = _mm(ycat, wout, name="out_proj")
    loss11, d_mixed, dout, g_post_w = _post_fwd_bwd(mixed, xb, tgt, post_norm_w)

    d_ycat = _mm(d_mixed, wout, tb=True, name="out_proj_dx")
    g_wout_full = _mm(ycat, d_mixed, ta=True, out_dtype=BF16, name="out_proj_dw")
    d_ypre, d_s5, d_t, y1, g_glu_b = _s5_post_bwd(d_ycat, y_pre, proj_main, t_pre, glu_w, DS)
    g_glu_full = _mm(y1, d_t, ta=True, out_dtype=BF16, name="glu_dw")
    d_s5, g_D, gct_re, gct_im, gbbd_re, gbbd_im, gab_re, gab_im = _s5_scan_bwd(
        d_ypre, proj_main, s_re, s_im, bbd_re, bbd_im, ct_re, ct_im, dvec, tab, ptab, d_s5, DS)
    d_gla, d_a, g_norm_w, g_gate_bias = _gla_bwd(
        d_ycat, proj_main, proj_low, s_prev, gup_pad, gla_gate_bias, gla_norm_w, DS, DK, DV)
    d_low = _mm(d_a, gup_pad, tb=True, out_dtype=BF16, name="gate_dx")
    g_gup_pad = _mm(proj_low, d_a, ta=True, name="gate_dw")
    g_wmain = _mm_nsplit(h, d_s5, d_gla, out_dtype=BF16, name="in_proj_dw")
    g_wlow = _mm(h, d_low, ta=True, out_dtype=BF16, name="in_proj_low_dw")

    g_win_full = jnp.concatenate([g_wmain, g_wlow[:, :GLA_RANK]], axis=1)
    gs = [jnp.moveaxis(g_win_full.reshape(D, 4, nsh), 1, 0),
          g_glu_full.reshape(4, DS // 4, DS),
          g_wout_full.reshape(4, D // 4, D),
          jnp.moveaxis(g_gup_pad[:GLA_RANK].reshape(GLA_RANK, 4, DK // 4), 1, 0)]
    c_arr = lax.axis_index("c").astype(jnp.int32).reshape(1)
    me_arr = chip.astype(jnp.int32).reshape(1)
    got = _pair_exchange(gs)
    pss = [_pair_add(g, r, c_arr, "grad_pair_add_" + n) for n, g, r in zip(sharded, gs, got)]
    send_sems, recv_sems, pss, lands, token = _chip_scatter_start(pss)

    dh_main = _mm_ksplit(d_s5, d_gla, w_main, token, name="in_proj_dx")
    dh_low = _mm(d_low, w_low, tb=True, name="in_proj_low_dx")
    grad_x, g_pre_w = _prenorm_bwd(xb, dh_main, dh_low, dout, pre_norm_w)
    pss, rcv = _chip_scatter_wait(send_sems, recv_sems, pss, lands, g_pre_w)

    g_a, g_bc, g_ldt = _s5_prep_bwd(Wv["s5_A_re"], Wv["s5_A_im"], s5_log_dt, Wv["s5_B_re"], Wv["s5_B_im"],
                                    gbbd_re, gbbd_im, gct_re, gct_im, gab_re, gab_im)

    loss = lax.psum(loss11[0, 0], ("x", "y", "c"))

    g_vecs = jnp.concatenate([g_pre_w, g_post_w, g_D, g_glu_b, g_gate_bias, g_norm_w, g_ldt], axis=1)
    lanes_pad = -g_vecs.shape[1] % (8 * SUBLANES * LANES)
    g_vecs = jnp.pad(g_vecs, ((0, 0), (0, lanes_pad))).reshape(-1, LANES)
    r_vecs, r_a, r_bc = _allreduce_small([g_vecs, g_a, g_bc])
    outs4 = _adamw_small(r_vecs.reshape(1, -1), r_a, r_bc, [Wv[n] for n in small],
                         [view[n](M[n]) for n in small], [view[n](V[n]) for n in small])
    G_out, D_out, M_out, V_out = [{n: back[n](t) for n, t in zip(small, o)} for o in outs4]

    halves = [_chip_sum(p, r, me_arr, "grad_chip_sum_" + n) for n, p, r in zip(sharded, pss, rcv)]
    others = _pair_swap(halves)
    for n, g_own, g_other in zip(sharded, halves, others):
        g_, d_, m_, v_ = _adamw_sharded(W[n][0], g_own, g_other, M[n][0], V[n][0], c_arr, "adamw_" + n)
        G_out[n], D_out[n], M_out[n], V_out[n] = g_[None], d_[None], m_[None], v_[None]

    return (loss, grad_x[None], *[G_out[n] for n in names], *[D_out[n] for n in names],
            *[M_out[n] for n in names], *[V_out[n] for n in names])
```

```python
import functools
import math

import jax
import jax.numpy as jnp
from jax import lax
from jax.experimental import pallas as pl
from jax.experimental.pallas import tpu as pltpu

F32 = jnp.float32
BF16 = jnp.bfloat16
HI = lax.Precision.HIGHEST
MESH = pl.DeviceIdType.MESH

EPS = 1e-6
S5_GROUP = 16
S5_STATE = 64
GLA_HK = 128
GLA_HV = 256
GLA_RANK = 16
GLA_TAU = 16.0
GLA_CHUNK = 64
GLA_STEP_CHUNKS = 2
LANES = 128
SUBLANES = 8
S5_COLS = 128
S5_LANES = (S5_COLS // S5_GROUP) * S5_STATE

ADAM_LR = 0.001
ADAM_B1 = 0.9
ADAM_B2 = 0.999
ADAM_EPS = 1e-08
ADAM_WD = 0.01
ADAM_STEP = 10

GELU_K = math.sqrt(2.0 / math.pi)
GELU_C = 0.044715


def _blk(n, pref, unit=LANES):
    best = None
    b = unit
    while b <= min(n, pref):
        if n % b == 0:
            best = b
        b += unit
    return best if best is not None else n


def _dot(a, b, dn=(((1,), (0,)), ((), ()))):
    return lax.dot_general(a.astype(BF16), b.astype(BF16), dn, preferred_element_type=F32)


def _dot_hi(a, b, dn=(((1,), (0,)), ((), ()))):
    return lax.dot_general(a, b, dn, precision=HI, preferred_element_type=F32)


NN = (((1,), (0,)), ((), ()))
NT = (((1,), (1,)), ((), ()))
TN = (((0,), (0,)), ((), ()))


def _sigmoid(x):
    return 1.0 / (1.0 + jnp.exp(-x))


def _gelu(y):
    return 0.5 * y * (1.0 + jnp.tanh(GELU_K * (y + GELU_C * y * y * y)))


def _gelu_grad(y):
    th = jnp.tanh(GELU_K * (y + GELU_C * y * y * y))
    return 0.5 * (1.0 + th) + 0.5 * y * (1.0 - th * th) * GELU_K * (1.0 + 3.0 * GELU_C * y * y)


def _mm(a, b, *, name, ta=False, tb=False, out_dtype=F32, bm=1024, bn=1024, bk=2048):
    if ta:
        K, M = a.shape
    else:
        M, K = a.shape
    if tb:
        N, K2 = b.shape
    else:
        K2, N = b.shape
    assert K == K2, (a.shape, b.shape, ta, tb)
    bm, bn, bk = _blk(M, bm), _blk(N, bn), _blk(K, bk)
    nk = K // bk
    dn = (((0 if ta else 1,), (1 if tb else 0,)), ((), ()))

    def body(a_ref, b_ref, o_ref, *acc):
        if nk == 1:
            o_ref[...] = _dot(a_ref[...], b_ref[...], dn).astype(out_dtype)
            return
        acc_ref, = acc
        k = pl.program_id(2)

        @pl.when(k == 0)
        def _():
            acc_ref[...] = jnp.zeros_like(acc_ref)

        acc_ref[...] += _dot(a_ref[...], b_ref[...], dn)

        @pl.when(k == nk - 1)
        def _():
            o_ref[...] = acc_ref[...].astype(out_dtype)

    a_spec = pl.BlockSpec((bk, bm), lambda i, j, k: (k, i)) if ta else pl.BlockSpec((bm, bk), lambda i, j, k: (i, k))
    b_spec = pl.BlockSpec((bn, bk), lambda i, j, k: (j, k)) if tb else pl.BlockSpec((bk, bn), lambda i, j, k: (k, j))
    return pl.pallas_call(
        body,
        name=name,
        grid=(M // bm, N // bn, nk),
        in_specs=[a_spec, b_spec],
        out_specs=pl.BlockSpec((bm, bn), lambda i, j, k: (i, j)),
        out_shape=jax.ShapeDtypeStruct((M, N), out_dtype),
        scratch_shapes=[pltpu.VMEM((bm, bn), F32)] if nk > 1 else [],
        compiler_params=pltpu.CompilerParams(dimension_semantics=("parallel", "parallel", "arbitrary")),
    )(a, b)


def _in_proj(h, w_main, w_low, after):
    M, K = h.shape
    N = w_main.shape[1]
    bm, bn = _blk(M, 1024), _blk(N, 1024)

    def body(h_ref, w_ref, wl_ref, _after_ref, o_ref, ol_ref):
        hv = h_ref[...]
        o_ref[...] = _dot(hv, w_ref[...])

        @pl.when(pl.program_id(1) == 0)
        def _():
            ol_ref[...] = _dot(hv, wl_ref[...])

    return pl.pallas_call(
        body, name="in_proj", grid=(M // bm, N // bn),
        in_specs=[pl.BlockSpec((bm, K), lambda i, j: (i, 0)), pl.BlockSpec((K, bn), lambda i, j: (0, j)),
                  pl.BlockSpec((K, LANES), lambda i, j: (0, 0)), pl.BlockSpec(memory_space=pl.ANY)],
        out_specs=[pl.BlockSpec((bm, bn), lambda i, j: (i, j)), pl.BlockSpec((bm, LANES), lambda i, j: (i, 0))],
        out_shape=[jax.ShapeDtypeStruct((M, N), F32), jax.ShapeDtypeStruct((M, LANES), F32)],
        compiler_params=pltpu.CompilerParams(dimension_semantics=("parallel", "arbitrary")),
    )(h, w_main, w_low, after)


def _in_proj_dx(a1, a2, al, b, bl, after, *, bm=1024, bn=1024, bk=2048):
    M, K1 = a1.shape
    K2 = a2.shape[1]
    N = b.shape[0]
    bm, bn = _blk(M, bm), _blk(N, bn)
    bk = _blk(math.gcd(K1, K2), bk)
    nk1, nk = K1 // bk, (K1 + K2) // bk

    def body(a1_ref, a2_ref, al_ref, b_ref, bl_ref, _after_ref, o_ref, acc_ref):
        k = pl.program_id(2)

        @pl.when(k == 0)
        def _():
            acc_ref[...] = _dot(al_ref[...], bl_ref[...], NT)

        @pl.when(k < nk1)
        def _():
            acc_ref[...] += _dot(a1_ref[...], b_ref[...], NT)

        @pl.when(k >= nk1)
        def _():
            acc_ref[...] += _dot(a2_ref[...], b_ref[...], NT)

        @pl.when(k == nk - 1)
        def _():
            o_ref[...] = acc_ref[...]

    return pl.pallas_call(
        body, name="in_proj_dx", grid=(M // bm, N // bn, nk),
        in_specs=[pl.BlockSpec((bm, bk), lambda i, j, k: (i, jnp.minimum(k, nk1 - 1))),
                  pl.BlockSpec((bm, bk), lambda i, j, k: (i, jnp.maximum(k - nk1, 0))),
                  pl.BlockSpec((bm, LANES), lambda i, j, k: (i, 0)),
                  pl.BlockSpec((bn, bk), lambda i, j, k: (j, k)),
                  pl.BlockSpec((bn, LANES), lambda i, j, k: (j, 0)),
                  pl.BlockSpec(memory_space=pl.ANY)],
        out_specs=pl.BlockSpec((bm, bn), lambda i, j, k: (i, j)),
        out_shape=jax.ShapeDtypeStruct((M, N), F32),
        scratch_shapes=[pltpu.VMEM((bm, bn), F32)],
        compiler_params=pltpu.CompilerParams(dimension_semantics=("parallel", "parallel", "arbitrary")),
    )(a1, a2, al, b, bl, after)


def _in_proj_dw(a, b1, b2, bl, *, bm=1024, bn=1024, bk=2048):
    K, M = a.shape
    N1, N2 = b1.shape[1], b2.shape[1]
    bm, bk = _blk(M, bm), _blk(K, bk)
    bn = _blk(math.gcd(N1, N2), bn)
    nj1, nj = N1 // bn, (N1 + N2) // bn
    nk = K // bk

    def body(a_ref, b1_ref, b2_ref, bl_ref, o_ref, ol_ref, acc_ref, accl_ref):
        j = pl.program_id(1)
        k = pl.program_id(2)

        @pl.when(k == 0)
        def _():
            acc_ref[...] = jnp.zeros_like(acc_ref)

        @pl.when(j < nj1)
        def _():
            acc_ref[...] += _dot(a_ref[...], b1_ref[...], TN)

        @pl.when(j >= nj1)
        def _():
            acc_ref[...] += _dot(a_ref[...], b2_ref[...], TN)

        @pl.when(k == nk - 1)
        def _():
            o_ref[...] = acc_ref[...].astype(BF16)

        @pl.when(j == 0)
        def _():
            low = _dot(a_ref[...], bl_ref[...], TN)

            @pl.when(k == 0)
            def _():
                accl_ref[...] = low

            @pl.when(k > 0)
            def _():
                accl_ref[...] += low

            @pl.when(k == nk - 1)
            def _():
                ol_ref[...] = accl_ref[...].astype(BF16)

    return pl.pallas_call(
        body, name="in_proj_dw", grid=(M // bm, nj, nk),
        in_specs=[pl.BlockSpec((bk, bm), lambda i, j, k: (k, i)),
                  pl.BlockSpec((bk, bn), lambda i, j, k: (jnp.where(j < nj1, k, nk - 1), jnp.minimum(j, nj1 - 1))),
                  pl.BlockSpec((bk, bn), lambda i, j, k: (jnp.where(j >= nj1, k, 0), jnp.maximum(j - nj1, 0))),
                  pl.BlockSpec((bk, LANES), lambda i, j, k: (jnp.where(j == 0, k, nk - 1), 0))],
        out_specs=[pl.BlockSpec((bm, bn), lambda i, j, k: (i, j)), pl.BlockSpec((bm, LANES), lambda i, j, k: (i, 0))],
        out_shape=[jax.ShapeDtypeStruct((M, N1 + N2), BF16), jax.ShapeDtypeStruct((M, LANES), BF16)],
        scratch_shapes=[pltpu.VMEM((bm, bn), F32), pltpu.VMEM((bm, LANES), F32)],
        compiler_params=pltpu.CompilerParams(dimension_semantics=("parallel", "arbitrary", "arbitrary")),
    )(a, b1, b2, bl)


def _prenorm_fwd(x, w):
    L, D = x.shape
    tr = _blk(L, 256, SUBLANES)

    def body(x_ref, w_ref, h_ref):
        xv = x_ref[...]
        r = lax.rsqrt(jnp.mean(xv * xv, axis=-1, keepdims=True) + EPS)
        h_ref[...] = (xv * r * w_ref[...]).astype(BF16)

    return pl.pallas_call(
        body, name="prenorm_fwd", grid=(L // tr,),
        in_specs=[pl.BlockSpec((tr, D), lambda i: (i, 0)), pl.BlockSpec((1, D), lambda i: (0, 0))],
        out_specs=pl.BlockSpec((tr, D), lambda i: (i, 0)),
        out_shape=jax.ShapeDtypeStruct((L, D), BF16),
        compiler_params=pltpu.CompilerParams(dimension_semantics=("parallel",)),
    )(x, w)


def _post_fwd_bwd(mixed, x, target, w):
    L, D = x.shape
    tr = _blk(L, 256, SUBLANES)
    nsteps = L // tr

    def body(mx_ref, x_ref, t_ref, w_ref, loss_ref, dm_ref, dout_ref, gw_ref, acc_ref):
        i = pl.program_id(0)

        @pl.when(i == 0)
        def _():
            acc_ref[...] = jnp.zeros_like(acc_ref)
            gw_ref[...] = jnp.zeros_like(gw_ref)

        mx = mx_ref[...]
        wv = w_ref[...]
        r = lax.rsqrt(jnp.mean(mx * mx, axis=-1, keepdims=True) + EPS)
        n = mx * r
        err = x_ref[...] + n * wv - t_ref[...]
        acc_ref[...] += jnp.sum(err * err, axis=0, keepdims=True)
        dout = err * (1.0 / D)
        dout_ref[...] = dout
        gw_ref[...] += jnp.sum(dout * n, axis=0, keepdims=True)
        dn = dout * wv
        dm_ref[...] = (r * (dn - n * jnp.mean(dn * n, axis=-1, keepdims=True))).astype(BF16)

        @pl.when(i == nsteps - 1)
        def _():
            loss_ref[...] = jnp.sum(acc_ref[...], axis=-1, keepdims=True) * (0.5 / D)

    row = pl.BlockSpec((tr, D), lambda i: (i, 0))
    vec = pl.BlockSpec((1, D), lambda i: (0, 0))
    return pl.pallas_call(
        body, name="post_fwd_bwd", grid=(nsteps,),
        in_specs=[row, row, row, vec],
        out_specs=[pl.BlockSpec((1, 1), lambda i: (0, 0)), row, row, vec],
        out_shape=[jax.ShapeDtypeStruct((1, 1), F32), jax.ShapeDtypeStruct((L, D), BF16),
                   jax.ShapeDtypeStruct((L, D), F32), jax.ShapeDtypeStruct((1, D), F32)],
        scratch_shapes=[pltpu.VMEM((1, D), F32)],
        compiler_params=pltpu.CompilerParams(dimension_semantics=("arbitrary",)),
    )(mixed, x, target, w)


def _prenorm_bwd(x, dh, dout, w):
    L, D = x.shape
    tr = _blk(L, 256, SUBLANES)

    def body(x_ref, a_ref, dout_ref, w_ref, gx_ref, gw_ref):
        i = pl.program_id(0)

        @pl.when(i == 0)
        def _():
            gw_ref[...] = jnp.zeros_like(gw_ref)

        xv = x_ref[...]
        r = lax.rsqrt(jnp.mean(xv * xv, axis=-1, keepdims=True) + EPS)
        n = xv * r
        dh = a_ref[...]
        gw_ref[...] += jnp.sum(dh * n, axis=0, keepdims=True)
        dn = dh * w_ref[...]
        gx_ref[...] = dout_ref[...] + r * (dn - n * jnp.mean(dn * n, axis=-1, keepdims=True))

    row = pl.BlockSpec((tr, D), lambda i: (i, 0))
    vec = pl.BlockSpec((1, D), lambda i: (0, 0))
    return pl.pallas_call(
        body, name="prenorm_bwd", grid=(L // tr,),
        in_specs=[row, row, row, vec],
        out_specs=[row, vec],
        out_shape=[jax.ShapeDtypeStruct((L, D), F32), jax.ShapeDtypeStruct((1, D), F32)],
        compiler_params=pltpu.CompilerParams(dimension_semantics=("arbitrary",)),
    )(x, dh, dout, w)


def _s5_disc(a_re_raw, a_im, dt):
    a_re = jnp.minimum(a_re_raw, -1e-4)
    mag = jnp.exp(a_re * dt)
    ph = a_im * dt
    ab_re = mag * jnp.cos(ph)
    ab_im = mag * jnp.sin(ph)
    inv_n = 1.0 / (a_re * a_re + a_im * a_im)
    ia_re = a_re * inv_n
    ia_im = -a_im * inv_n
    n_re = ab_re - 1.0
    f_re = n_re * ia_re - ab_im * ia_im
    f_im = n_re * ia_im + ab_im * ia_re
    return a_re, ab_re, ab_im, f_re, f_im, ia_re, ia_im


def _iota2(shape, dim):
    return lax.broadcasted_iota(jnp.int32, shape, dim)


def _group_mask(rows, rows_per_group):
    shift = rows_per_group.bit_length() - 1
    return (_iota2((rows, S5_LANES), 0) >> shift) == (_iota2((rows, S5_LANES), 1) >> (S5_STATE.bit_length() - 1))


def _lane_tiler(dtype):
    return ((_iota2((S5_STATE, S5_LANES), 1) & (S5_STATE - 1)) == _iota2((S5_STATE, S5_LANES), 0)).astype(dtype)


def _row_to_col(row, n):
    eye = (_iota2((n, n), 0) == _iota2((n, n), 1)).astype(F32)
    return jnp.sum(eye * row, axis=1, keepdims=True)


def _group_repeat(G):
    return ((_iota2((G * S5_GROUP, G), 0) >> (S5_GROUP.bit_length() - 1)) == _iota2((G * S5_GROUP, G), 1)).astype(F32)


S5_TABS = 18


def _s5_prep_fwd(a_re, a_im, log_dt, b_re, b_im, c_re, c_im, seg):
    G, P = a_re.shape
    nb = G * S5_GROUP // S5_COLS
    g8 = S5_COLS // S5_GROUP
    assert seg & (seg - 1) == 0, seg

    def body(are_ref, aim_ref, ldt_ref, bre_ref, bim_ref, cre_ref, cim_ref,
             bbre_ref, bbim_ref, ctre_ref, ctim_ref, tab_ref, pt_ref):
        dt = jnp.exp(_row_to_col(ldt_ref[...], G))
        _, ab_re, ab_im, f_re, f_im, _, _ = _s5_disc(are_ref[...], aim_ref[...], dt)
        rep = _group_repeat(G)
        fx_re = _dot_hi(rep, f_re)
        fx_im = _dot_hi(rep, f_im)
        br, bi = bre_ref[...], bim_ref[...]
        bb_re = fx_re * br - fx_im * bi
        bb_im = fx_re * bi + fx_im * br
        tile_bf = _lane_tiler(BF16)
        mask = _group_mask(S5_COLS, S5_GROUP)
        for jb in range(nb):
            rs = slice(jb * S5_COLS, (jb + 1) * S5_COLS)
            for src, dst in ((bb_re[rs], bbre_ref), (bb_im[rs], bbim_ref), (cre_ref[rs, :], ctre_ref), (cim_ref[rs, :], ctim_ref)):
                dst[jb] = jnp.where(mask, _dot(src, tile_bf), 0.0).astype(BF16)

        tile_f = _lane_tiler(F32)
        mask8 = _group_mask(g8, 1)
        row = _iota2((SUBLANES, S5_LANES), 0)
        slab = (SUBLANES, S5_LANES)
        cmul = lambda p, q: (p[0] * q[0] - p[1] * q[1], p[0] * q[1] + p[1] * q[0])
        for jb in range(nb):
            gs = slice(jb * g8, (jb + 1) * g8)

            def lanes(m):
                v = jnp.sum(jnp.where(mask8, _dot_hi(m[gs], tile_f), 0.0), axis=0, keepdims=True)
                return jnp.broadcast_to(v, slab)

            a1 = (lanes(ab_re), lanes(ab_im))
            tab_ref[jb, 0], tab_ref[jb, 1] = a1

            def powers(i, p):
                off = pl.multiple_of(i * SUBLANES, SUBLANES)
                pt_ref[jb, 0, pl.ds(off, SUBLANES), :] = p[0]
                pt_ref[jb, 1, pl.ds(off, SUBLANES), :] = p[1]
                return cmul(p, a1)

            lax.fori_loop(0, seg, powers, a1)
            aseg = a1
            for _ in range(seg.bit_length() - 1):
                aseg = cmul(aseg, aseg)
            pw = [aseg]
            for _ in range(1, SUBLANES):
                pw.append(cmul(pw[-1], aseg))
            for lvl, k in enumerate((1, 2, 4)):
                tab_ref[jb, 2 + 2 * lvl] = jnp.where(row >= k, pw[k - 1][0], 0.0)
                tab_ref[jb, 3 + 2 * lvl] = jnp.where(row >= k, pw[k - 1][1], 0.0)
                tab_ref[jb, 10 + 2 * lvl] = jnp.where(row < SUBLANES - k, pw[k - 1][0], 0.0)
                tab_ref[jb, 11 + 2 * lvl] = jnp.where(row < SUBLANES - k, -pw[k - 1][1], 0.0)
            f_r = f_i = r_r = r_i = jnp.zeros(slab, F32)
            for i in range(SUBLANES):
                f_r = jnp.where(row == i, pw[i][0], f_r)
                f_i = jnp.where(row == i, pw[i][1], f_i)
                r_r = jnp.where(row == i, pw[SUBLANES - 1 - i][0], r_r)
                r_i = jnp.where(row == i, -pw[SUBLANES - 1 - i][1], r_i)
            tab_ref[jb, 8] = f_r
            tab_ref[jb, 9] = f_i
            tab_ref[jb, 16] = r_r
            tab_ref[jb, 17] = r_i

    vm = pl.BlockSpec(memory_space=pltpu.VMEM)
    bd = jax.ShapeDtypeStruct((nb, S5_COLS, S5_LANES), BF16)
    return pl.pallas_call(
        body, name="s5_prep_fwd",
        in_specs=[vm] * 7, out_specs=[vm] * 6,
        out_shape=[bd, bd, bd, bd, jax.ShapeDtypeStruct((nb, S5_TABS, SUBLANES, S5_LANES), F32),
                   jax.ShapeDtypeStruct((nb, 2, seg * SUBLANES, S5_LANES), F32)],
    )(a_re, a_im, log_dt, b_re, b_im, c_re, c_im)


def _s5_prep_bwd(a_re, a_im, log_dt, b_re, b_im, gbb_re, gbb_im, gct_re, gct_im, gab_re, gab_im):
    G, P = a_re.shape
    nb = G * S5_GROUP // S5_COLS
    g8 = S5_COLS // S5_GROUP

    def body(are_ref, aim_ref, ldt_ref, bre_ref, bim_ref, gbr_ref, gbi_ref, gcr_ref, gci_ref, gar_ref, gai_ref,
             o_a, o_bc, o_ldt):
        dt = jnp.exp(_row_to_col(ldt_ref[...], G))
        a_raw = are_ref[...]
        a_imv = aim_ref[...]
        a_re_c, ab_re, ab_im, f_re, f_im, ia_re, ia_im = _s5_disc(a_raw, a_imv, dt)
        tile_f = _lane_tiler(F32)
        mask = _group_mask(S5_COLS, S5_GROUP)
        mask8 = _group_mask(g8, 1)
        for jb in range(nb):
            rs = slice(jb * S5_COLS, (jb + 1) * S5_COLS)
            gs = slice(jb * g8, (jb + 1) * g8)
            ls = slice(jb * S5_LANES, (jb + 1) * S5_LANES)
            for k, src in enumerate((gbr_ref, gbi_ref, gcr_ref, gci_ref)):
                o_bc[k, rs, :] = _dot_hi(jnp.where(mask, src[jb], 0.0), tile_f, NT)
            for k, src in enumerate((gar_ref, gai_ref)):
                o_a[k, gs, :] = _dot_hi(jnp.where(mask8, src[:, ls], 0.0), tile_f, NT)
        rep = _group_repeat(G)
        fx_re = _dot_hi(rep, f_re)
        fx_im = _dot_hi(rep, f_im)
        gbr, gbi = o_bc[0], o_bc[1]
        br, bi = bre_ref[...], bim_ref[...]
        o_bc[0] = fx_re * gbr + fx_im * gbi
        o_bc[1] = fx_re * gbi - fx_im * gbr
        gf_re = _dot_hi(rep, br * gbr + bi * gbi, TN)
        gf_im = _dot_hi(rep, br * gbi - bi * gbr, TN)
        gab_r = o_a[0] + ia_re * gf_re + ia_im * gf_im
        gab_i = o_a[1] + ia_re * gf_im - ia_im * gf_re
        q_re = f_re * ia_re - f_im * ia_im
        q_im = f_re * ia_im + f_im * ia_re
        ga_re = -(q_re * gf_re + q_im * gf_im)
        ga_im = -(q_re * gf_im - q_im * gf_re)
        gth_re = ab_re * gab_r + ab_im * gab_i
        gth_im = ab_re * gab_i - ab_im * gab_r
        ga_re = ga_re + dt * gth_re
        ga_im = ga_im + dt * gth_im
        gdt = jnp.sum(a_re_c * gth_re + a_imv * gth_im, axis=-1, keepdims=True)
        eye = (_iota2((G, G), 0) == _iota2((G, G), 1)).astype(F32)
        o_ldt[...] = jnp.sum(eye * (gdt * dt), axis=0, keepdims=True)
        slope = jnp.where(a_raw < -1e-4, 1.0, jnp.where(a_raw == -1e-4, 0.5, 0.0))
        o_a[0] = ga_re * slope
        o_a[1] = ga_im

    vm = pl.BlockSpec(memory_space=pltpu.VMEM)
    return pl.pallas_call(
        body, name="s5_prep_bwd",
        in_specs=[vm] * 11, out_specs=[vm] * 3,
        out_shape=[jax.ShapeDtypeStruct((2, G, P), F32), jax.ShapeDtypeStruct((4, G * S5_GROUP, P), F32),
                   jax.ShapeDtypeStruct((1, G), F32)],
    )(a_re, a_im, log_dt, b_re, b_im, gbb_re, gbb_im, gct_re, gct_im, gab_re, gab_im)


def _scan8(xr, xi, tab_ref, base, shifts):
    for lvl, sh in enumerate(shifts):
        mr = tab_ref[0, base + 2 * lvl]
        mi = tab_ref[0, base + 2 * lvl + 1]
        ar = pltpu.roll(xr, sh, 0)
        ai = pltpu.roll(xi, sh, 0)
        xr, xi = xr + mr * ar - mi * ai, xi + mr * ai + mi * ar
    return xr, xi


def _to_segments(src_ref, dst_ref, seg):
    for i in range(seg):
        dst_ref[i * SUBLANES:(i + 1) * SUBLANES, :] = src_ref[pl.ds(i, SUBLANES, stride=seg), :]


def _from_segments(src_ref, dst_ref, seg):
    for i in range(seg):
        dst_ref[pl.ds(i, SUBLANES, stride=seg), :] = src_ref[i * SUBLANES:(i + 1) * SUBLANES, :]


def _slab(i):
    return pl.ds(pl.multiple_of(i * SUBLANES, SUBLANES), SUBLANES)


def _s5_scan_fwd(proj_main, bbd_re, bbd_im, cbd_re, cbd_im, dvec, tab, ptab, DS):
    L = proj_main.shape[0]
    nb = DS // S5_COLS
    tb = _blk(L, 512, SUBLANES)
    nt = L // tb
    seg = tb // SUBLANES

    def body(u_ref, bre_ref, bim_ref, cre_ref, cim_ref, d_ref, tab_ref, pt_ref, y_ref, sre_ref, sim_ref,
             up_ref, yp_ref, car_ref):
        t = pl.program_id(1)

        @pl.when(t == 0)
        def _():
            car_ref[...] = jnp.zeros_like(car_ref)

        _to_segments(u_ref, up_ref, seg)
        up = up_ref[...]
        sre_ref[...] = _dot(up, bre_ref[0])
        sim_ref[...] = _dot(up, bim_ref[0])
        ar, ai = tab_ref[0, 0], tab_ref[0, 1]

        def pass1(i, x):
            xr = ar * x[0] - ai * x[1] + sre_ref[_slab(i), :]
            xi = ar * x[1] + ai * x[0] + sim_ref[_slab(i), :]
            sre_ref[_slab(i), :] = xr
            sim_ref[_slab(i), :] = xi
            return xr, xi

        zero = jnp.zeros((SUBLANES, S5_LANES), F32)
        er, ei = lax.fori_loop(0, seg, pass1, (zero, zero))
        cin_r, cin_i = car_ref[0], car_ref[1]
        sr, si = _scan8(er, ei, tab_ref, 2, (1, 2, 4))
        pr, pi = tab_ref[0, 8], tab_ref[0, 9]
        sr, si = sr + pr * cin_r - pi * cin_i, si + pr * cin_i + pi * cin_r
        row0 = _iota2((SUBLANES, S5_LANES), 0) == 0
        cr = jnp.where(row0, cin_r, pltpu.roll(sr, 1, 0))
        ci = jnp.where(row0, cin_i, pltpu.roll(si, 1, 0))
        car_ref[0] = jnp.broadcast_to(sr[SUBLANES - 1:SUBLANES, :], sr.shape)
        car_ref[1] = jnp.broadcast_to(si[SUBLANES - 1:SUBLANES, :], si.shape)

        def pass2(i, _):
            qr, qi = pt_ref[0, 0, _slab(i), :], pt_ref[0, 1, _slab(i), :]
            sre_ref[_slab(i), :] += qr * cr - qi * ci
            sim_ref[_slab(i), :] += qr * ci + qi * cr
            return 0

        lax.fori_loop(0, seg, pass2, 0, unroll=4)
        yp_ref[...] = _dot(sre_ref[...], cre_ref[0], NT) - _dot(sim_ref[...], cim_ref[0], NT) + d_ref[...] * up
        _from_segments(yp_ref, y_ref, seg)

    return pl.pallas_call(
        body, name="s5_scan_fwd", grid=(nb, nt),
        in_specs=[
            pl.BlockSpec((tb, S5_COLS), lambda j, t: (t, j)),
            pl.BlockSpec((1, S5_COLS, S5_LANES), lambda j, t: (j, 0, 0)),
            pl.BlockSpec((1, S5_COLS, S5_LANES), lambda j, t: (j, 0, 0)),
            pl.BlockSpec((1, S5_COLS, S5_LANES), lambda j, t: (j, 0, 0)),
            pl.BlockSpec((1, S5_COLS, S5_LANES), lambda j, t: (j, 0, 0)),
            pl.BlockSpec((1, S5_COLS), lambda j, t: (0, j)),
            pl.BlockSpec((1, S5_TABS, SUBLANES, S5_LANES), lambda j, t: (j, 0, 0, 0)),
            pl.BlockSpec((1, 2, tb, S5_LANES), lambda j, t: (j, 0, 0, 0)),
        ],
        out_specs=[
            pl.BlockSpec((tb, S5_COLS), lambda j, t: (t, j)),
            pl.BlockSpec((tb, S5_LANES), lambda j, t: (t, j)),
            pl.BlockSpec((tb, S5_LANES), lambda j, t: (t, j)),
        ],
        out_shape=[jax.ShapeDtypeStruct((L, DS), F32),
                   jax.ShapeDtypeStruct((L, nb * S5_LANES), F32),
                   jax.ShapeDtypeStruct((L, nb * S5_LANES), F32)],
        scratch_shapes=[pltpu.VMEM((tb, S5_COLS), F32), pltpu.VMEM((tb, S5_COLS), F32),
                        pltpu.VMEM((2, SUBLANES, S5_LANES), F32)],
        compiler_params=pltpu.CompilerParams(dimension_semantics=("parallel", "arbitrary")),
    )(proj_main, bbd_re, bbd_im, cbd_re, cbd_im, dvec, tab, ptab)


def _s5_scan_bwd(dy, proj_main, s_re, s_im, bbd_re, bbd_im, cbd_re, cbd_im, dvec, tab, ptab, d_s5, DS):
    L = proj_main.shape[0]
    nb = DS // S5_COLS
    tb = _blk(L, 512, SUBLANES)
    nt = L // tb
    seg = tb // SUBLANES
    tb8 = tb // SUBLANES

    def body(dy_ref, u_ref, sre_ref, sim_ref, pre_ref, pim_ref, bre_ref, bim_ref, cre_ref, cim_ref, d_ref, tab_ref, pt_ref,
             _ds5_ref, du_ref, gd_ref, gcre_ref, gcim_ref, gbre_ref, gbim_ref, gare_ref, gaim_ref,
             lre_ref, lim_ref, up_ref, dyp_ref, dup_ref, duo_ref, car_ref):
        t = pl.program_id(1)

        @pl.when(t == 0)
        def _():
            car_ref[...] = jnp.zeros_like(car_ref)
            gd_ref[...] = jnp.zeros_like(gd_ref)
            gcre_ref[...] = jnp.zeros_like(gcre_ref)
            gcim_ref[...] = jnp.zeros_like(gcim_ref)
            gbre_ref[...] = jnp.zeros_like(gbre_ref)
            gbim_ref[...] = jnp.zeros_like(gbim_ref)
            gare_ref[...] = jnp.zeros_like(gare_ref)
            gaim_ref[...] = jnp.zeros_like(gaim_ref)

        _to_segments(dy_ref, dyp_ref, seg)
        _to_segments(u_ref, up_ref, seg)
        dyv = dyp_ref[...]
        u = up_ref[...]
        gd_ref[...] += jnp.sum(dyv * u, axis=0, keepdims=True)
        lre_ref[...] = _dot(dyv, cre_ref[0])
        lim_ref[...] = -_dot(dyv, cim_ref[0])
        gcre_ref[0] += _dot(dyv, sre_ref[...], TN)
        gcim_ref[0] -= _dot(dyv, sim_ref[...], TN)
        ar, ai = tab_ref[0, 0], -tab_ref[0, 1]

        def pass1(k, x):
            i = seg - 1 - k
            xr = ar * x[0] - ai * x[1] + lre_ref[_slab(i), :]
            xi = ar * x[1] + ai * x[0] + lim_ref[_slab(i), :]
            lre_ref[_slab(i), :] = xr
            lim_ref[_slab(i), :] = xi
            return xr, xi

        zero = jnp.zeros((SUBLANES, S5_LANES), F32)
        er, ei = lax.fori_loop(0, seg, pass1, (zero, zero))
        cin_r, cin_i = car_ref[0], car_ref[1]
        lr, li = _scan8(er, ei, tab_ref, 10, (7, 6, 4))
        pr, pi = tab_ref[0, 16], tab_ref[0, 17]
        lr, li = lr + pr * cin_r - pi * cin_i, li + pr * cin_i + pi * cin_r
        rows = _iota2((SUBLANES, S5_LANES), 0)
        cr = jnp.where(rows == SUBLANES - 1, cin_r, pltpu.roll(lr, SUBLANES - 1, 0))
        ci = jnp.where(rows == SUBLANES - 1, cin_i, pltpu.roll(li, SUBLANES - 1, 0))
        car_ref[0] = jnp.broadcast_to(lr[0:1, :], lr.shape)
        car_ref[1] = jnp.broadcast_to(li[0:1, :], li.shape)

        first = (t == nt - 1).astype(F32)
        head_re = jnp.broadcast_to(pre_ref[SUBLANES - 1:SUBLANES, :], zero.shape) * (1.0 - first)
        head_im = jnp.broadcast_to(pim_ref[SUBLANES - 1:SUBLANES, :], zero.shape) * (1.0 - first)
        last = _slab(seg - 1)
        sp0_re = jnp.where(rows == 0, head_re, pltpu.roll(sre_ref[last, :], 1, 0))
        sp0_im = jnp.where(rows == 0, head_im, pltpu.roll(sim_ref[last, :], 1, 0))

        def pass2(i, acc):
            j = seg - 1 - i
            qr, qi = pt_ref[0, 0, _slab(j), :], -pt_ref[0, 1, _slab(j), :]
            xr = lre_ref[_slab(i), :] + qr * cr - qi * ci
            xi = lim_ref[_slab(i), :] + qr * ci + qi * cr
            lre_ref[_slab(i), :] = xr
            lim_ref[_slab(i), :] = xi
            prev = _slab(jnp.maximum(i - 1, 0))
            sp_re = jnp.where(i == 0, sp0_re, sre_ref[prev, :])
            sp_im = jnp.where(i == 0, sp0_im, sim_ref[prev, :])
            return acc[0] + sp_re * xr + sp_im * xi, acc[1] + sp_re * xi - sp_im * xr

        acc_re, acc_im = lax.fori_loop(0, seg, pass2, (zero, zero), unroll=2)
        gare_ref[...] += jnp.sum(acc_re, axis=0, keepdims=True)
        gaim_ref[...] += jnp.sum(acc_im, axis=0, keepdims=True)
        lre = lre_ref[...]
        lim = lim_ref[...]
        dup_ref[...] = dyv * d_ref[...] + _dot(lre, bre_ref[0], NT) + _dot(lim, bim_ref[0], NT)
        _from_segments(dup_ref, duo_ref, seg)
        du_ref[...] = duo_ref[...].astype(BF16)
        gbre_ref[0] += _dot(u, lre, TN)
        gbim_ref[0] += _dot(u, lim, TN)

    rt = lambda t: nt - 1 - t
    col = pl.BlockSpec((tb, S5_COLS), lambda j, t: (rt(t), j))
    st = pl.BlockSpec((tb, S5_LANES), lambda j, t: (rt(t), j))
    prev = pl.BlockSpec((SUBLANES, S5_LANES), lambda j, t: (jnp.maximum(rt(t) * tb8 - 1, 0), j))
    bmat = pl.BlockSpec((1, S5_COLS, S5_LANES), lambda j, t: (j, 0, 0))
    cmat = bmat
    return pl.pallas_call(
        body, name="s5_scan_bwd", grid=(nb, nt),
        in_specs=[col, col, st, st, prev, prev, bmat, bmat, cmat, cmat,
                  pl.BlockSpec((1, S5_COLS), lambda j, t: (0, j)),
                  pl.BlockSpec((1, S5_TABS, SUBLANES, S5_LANES), lambda j, t: (j, 0, 0, 0)),
                  pl.BlockSpec((1, 2, tb, S5_LANES), lambda j, t: (j, 0, 0, 0)),
                  pl.BlockSpec(memory_space=pl.ANY)],
        out_specs=[col, pl.BlockSpec((1, S5_COLS), lambda j, t: (0, j)), cmat, cmat, bmat, bmat,
                   pl.BlockSpec((1, S5_LANES), lambda j, t: (0, j)), pl.BlockSpec((1, S5_LANES), lambda j, t: (0, j))],
        input_output_aliases={13: 0},
        out_shape=[jax.ShapeDtypeStruct((L, 2 * DS), BF16), jax.ShapeDtypeStruct((1, DS), F32),
                   jax.ShapeDtypeStruct((nb, S5_COLS, S5_LANES), F32), jax.ShapeDtypeStruct((nb, S5_COLS, S5_LANES), F32),
                   jax.ShapeDtypeStruct((nb, S5_COLS, S5_LANES), F32), jax.ShapeDtypeStruct((nb, S5_COLS, S5_LANES), F32),
                   jax.ShapeDtypeStruct((1, nb * S5_LANES), F32), jax.ShapeDtypeStruct((1, nb * S5_LANES), F32)],
        scratch_shapes=[pltpu.VMEM((tb, S5_LANES), F32), pltpu.VMEM((tb, S5_LANES), F32)]
        + [pltpu.VMEM((tb, S5_COLS), F32)] * 4 + [pltpu.VMEM((2, SUBLANES, S5_LANES), F32)],
        compiler_params=pltpu.CompilerParams(dimension_semantics=("parallel", "arbitrary")),
    )(dy, proj_main, s_re, s_im, s_re, s_im, bbd_re, bbd_im, cbd_re, cbd_im, dvec, tab, ptab, d_s5)


def _s5_post_fwd(y_pre, proj_main, glu_w, glu_b, DS):
    L = y_pre.shape[0]
    tr = _blk(L, 256, SUBLANES)

    def body(y_ref, z_ref, w_ref, b_ref, o_ref, t_ref):
        y1 = _gelu(y_ref[...])
        t = _dot(y1, w_ref[...]) + b_ref[...]
        t_ref[...] = t
        z = z_ref[...]
        o_ref[...] = (y1 * _sigmoid(t) * (z * _sigmoid(z))).astype(BF16)

    row = pl.BlockSpec((tr, DS), lambda i: (i, 0))
    return pl.pallas_call(
        body, name="s5_post_fwd", grid=(L // tr,),
        in_specs=[row, pl.BlockSpec((tr, DS), lambda i: (i, 1)), pl.BlockSpec((DS, DS), lambda i: (0, 0)),
                  pl.BlockSpec((1, DS), lambda i: (0, 0))],
        out_specs=[row, row],
        out_shape=[jax.ShapeDtypeStruct((L, 2 * DS), BF16), jax.ShapeDtypeStruct((L, DS), F32)],
        compiler_params=pltpu.CompilerParams(dimension_semantics=("parallel",)),
    )(y_pre, proj_main, glu_w, glu_b)


def _s5_post_bwd(d_ycat, y_pre, proj_main, t_pre, glu_w, DS):
    L = y_pre.shape[0]
    tr = _blk(L, 256, SUBLANES)

    def body(dy_ref, y_ref, z_ref, t_ref, w_ref, dyp_ref, dz_ref, dt_ref, y1_ref, gb_ref):
        i = pl.program_id(0)

        @pl.when(i == 0)
        def _():
            gb_ref[...] = jnp.zeros_like(gb_ref)

        dy = dy_ref[...]
        yp = y_ref[...]
        z = z_ref[...]
        y1 = _gelu(yp)
        sg = _sigmoid(t_ref[...])
        sz = _sigmoid(z)
        c = y1 * sg
        d_c = dy * (z * sz)
        dz_ref[...] = (dy * c * (sz * (1.0 + z * (1.0 - sz)))).astype(BF16)
        d_t = d_c * y1 * sg * (1.0 - sg)
        gb_ref[...] += jnp.sum(d_t, axis=0, keepdims=True)
        dt_ref[...] = d_t.astype(BF16)
        y1_ref[...] = y1.astype(BF16)
        d_y1 = d_c * sg + _dot(d_t, w_ref[...], NT)
        dyp_ref[...] = d_y1 * _gelu_grad(yp)

    row = pl.BlockSpec((tr, DS), lambda i: (i, 0))
    return pl.pallas_call(
        body, name="s5_post_bwd", grid=(L // tr,),
        in_specs=[row, row, pl.BlockSpec((tr, DS), lambda i: (i, 1)), row, pl.BlockSpec((DS, DS), lambda i: (0, 0))],
        out_specs=[row, pl.BlockSpec((tr, DS), lambda i: (i, 1)), row, row, pl.BlockSpec((1, DS), lambda i: (0, 0))],
        out_shape=[jax.ShapeDtypeStruct((L, DS), F32), jax.ShapeDtypeStruct((L, 2 * DS), BF16),
                   jax.ShapeDtypeStruct((L, DS), BF16), jax.ShapeDtypeStruct((L, DS), BF16),
                   jax.ShapeDtypeStruct((1, DS), F32)],
        compiler_params=pltpu.CompilerParams(dimension_semantics=("arbitrary",)),
    )(d_ycat, y_pre, proj_main, t_pre, glu_w)


def _gla_gates(glow, gu_ref, gb_ref):
    a = _dot(glow, gu_ref[...]) + gb_ref[...]
    lg = (jnp.minimum(a, 0.0) - jnp.log(1.0 + jnp.exp(-jnp.abs(a)))) * (1.0 / GLA_TAU)
    ri = lax.broadcasted_iota(jnp.int32, (GLA_CHUNK, GLA_CHUNK), 0)
    ci = lax.broadcasted_iota(jnp.int32, (GLA_CHUNK, GLA_CHUNK), 1)
    b = _dot_hi((ri >= ci).astype(F32), lg)
    b_last = jnp.sum(lg, axis=0, keepdims=True)
    return a, b, b_last, ri >= ci


def _gla_specs(DS, DK, DV, c, cmap):
    return [
        pl.BlockSpec((c, DK), lambda n: (cmap(n), 2 * DS // DK)),
        pl.BlockSpec((c, DK), lambda n: (cmap(n), 2 * DS // DK + 1)),
        pl.BlockSpec((c, DV), lambda n: (cmap(n), (2 * DS + 2 * DK) // DV)),
        pl.BlockSpec((c, DV), lambda n: (cmap(n), (2 * DS + 2 * DK) // DV + 1)),
    ]


def _gla_fwd(proj_main, proj_low, gate_up_pad, gate_bias, norm_w, ycat, DS, DK, DV):
    L = proj_main.shape[0]
    nc = L // GLA_CHUNK
    cps = math.gcd(GLA_STEP_CHUNKS, nc)
    nh = DK // GLA_HK
    scale = GLA_HK ** -0.5

    def body(q_ref, k_ref, v_ref, z_ref, gl_ref, gu_ref, gb_ref, nw_ref, _yc_ref, y_ref, sp_ref, st_ref):
        n = pl.program_id(0)

        @pl.when(n == 0)
        def _():
            st_ref[...] = jnp.zeros_like(st_ref)

        pairs = [(sc, h) for sc in range(cps) for h in range(nh)]
        rows = lambda sc: slice(sc * GLA_CHUNK, (sc + 1) * GLA_CHUNK)
        kcol = lambda h: slice(h * GLA_HK, (h + 1) * GLA_HK)
        vcol = lambda h: slice(h * GLA_HV, (h + 1) * GLA_HV)
        gates = [_gla_gates(gl_ref[rows(sc), :], gu_ref, gb_ref) for sc in range(cps)]
        qe, dec, o_in, kv = {}, {}, {}, {}
        for sc, h in pairs:
            _, b, b_last, mask = gates[sc]
            bh, bl = b[:, kcol(h)], b_last[:, kcol(h)]
            qe[sc, h] = (q_ref[rows(sc), kcol(h)] * scale) * jnp.exp(bh)
            kh = k_ref[rows(sc), kcol(h)]
            vh = v_ref[rows(sc), vcol(h)]
            attn = jnp.where(mask, _dot(qe[sc, h], kh * jnp.exp(-bh), NT), 0.0)
            o_in[sc, h] = _dot(attn, vh)
            kv[sc, h] = _dot(vh, kh * jnp.exp(bl - bh), TN)
            dec[sc, h] = jnp.exp(bl)
        for sc, h in pairs:
            st = st_ref[h]
            sp_ref[sc, h] = st
            o = o_in[sc, h] + _dot(qe[sc, h], st, NT)
            st_ref[h] = dec[sc, h] * st + kv[sc, h]
            r = lax.rsqrt(jnp.mean(o * o, axis=-1, keepdims=True) + EPS)
            z = z_ref[rows(sc), vcol(h)]
            y_ref[rows(sc), vcol(h)] = (o * r * nw_ref[...] * (z * _sigmoid(z))).astype(BF16)

    c = cps * GLA_CHUNK
    return pl.pallas_call(
        body, name="gla_fwd", grid=(nc // cps,),
        in_specs=_gla_specs(DS, DK, DV, c, lambda n: n) + [
            pl.BlockSpec((c, LANES), lambda n: (n, 0)),
            pl.BlockSpec((LANES, DK), lambda n: (0, 0)),
            pl.BlockSpec((1, DK), lambda n: (0, 0)),
            pl.BlockSpec((1, GLA_HV), lambda n: (0, 0)),
            pl.BlockSpec(memory_space=pl.ANY),
        ],
        out_specs=[pl.BlockSpec((c, DV), lambda n: (n, DS // DV)),
                   pl.BlockSpec((cps, nh, GLA_HV, GLA_HK), lambda n: (n, 0, 0, 0))],
        input_output_aliases={8: 0},
        out_shape=[jax.ShapeDtypeStruct(ycat.shape, BF16), jax.ShapeDtypeStruct((nc, nh, GLA_HV, GLA_HK), F32)],
        scratch_shapes=[pltpu.VMEM((nh, GLA_HV, GLA_HK), F32)],
        compiler_params=pltpu.CompilerParams(dimension_semantics=("arbitrary",)),
    )(proj_main, proj_main, proj_main, proj_main, proj_low, gate_up_pad, gate_bias, norm_w, ycat)


def _gla_bwd(d_ycat, proj_main, proj_low, s_prev, gate_up_pad, gate_bias, norm_w, DS, DK, DV):
    L = proj_main.shape[0]
    nc = L // GLA_CHUNK
    cps = math.gcd(GLA_STEP_CHUNKS, nc)
    nh = DK // GLA_HK
    scale = GLA_HK ** -0.5

    def body(dy_ref, q_ref, k_ref, v_ref, z_ref, gl_ref, sp_ref, gu_ref, gb_ref, nw_ref,
             dg_ref, da_ref, gnw_ref, ggb_ref, dst_ref):
        n = pl.program_id(0)

        @pl.when(n == 0)
        def _():
            dst_ref[...] = jnp.zeros_like(dst_ref)
            gnw_ref[...] = jnp.zeros_like(gnw_ref)
            ggb_ref[...] = jnp.zeros_like(ggb_ref)

        last_row = lax.broadcasted_iota(jnp.int32, (GLA_CHUNK, GLA_HK), 0) == GLA_CHUNK - 1
        ri = lax.broadcasted_iota(jnp.int32, (GLA_CHUNK, GLA_CHUNK), 0)
        ci = lax.broadcasted_iota(jnp.int32, (GLA_CHUNK, GLA_CHUNK), 1)
        upper = (ci >= ri).astype(F32)
        nw = nw_ref[...]
        for sc in reversed(range(cps)):
            rs = slice(sc * GLA_CHUNK, (sc + 1) * GLA_CHUNK)
            a, b, b_last, mask = _gla_gates(gl_ref[rs, :], gu_ref, gb_ref)
            for h in range(nh):
                ks = slice(h * GLA_HK, (h + 1) * GLA_HK)
                vs = slice(h * GLA_HV, (h + 1) * GLA_HV)
                bh, bl = b[:, ks], b_last[:, ks]
                e = jnp.exp(bh)
                einv = jnp.exp(-bh)
                etail = jnp.exp(bl - bh)
                dec = jnp.exp(bl)
                qe = (q_ref[rs, ks] * scale) * e
                kh = k_ref[rs, ks]
                ke = kh * einv
                ktail = kh * etail
                vh = v_ref[rs, vs]
                st = sp_ref[sc, h]
                dst = dst_ref[h]
                attn = jnp.where(mask, _dot(qe, ke, NT), 0.0)
                o = _dot(attn, vh) + _dot(qe, st, NT)
                r = lax.rsqrt(jnp.mean(o * o, axis=-1, keepdims=True) + EPS)
                nrm = o * r
                z = z_ref[rs, vs]
                sz = _sigmoid(z)
                dy = dy_ref[rs, vs]
                dg_ref[rs, 2 * DK + DV + h * GLA_HV:2 * DK + DV + (h + 1) * GLA_HV] = (
                    dy * nrm * nw * (sz * (1.0 + z * (1.0 - sz)))).astype(BF16)
                d_on = dy * (z * sz)
                gnw_ref[...] += jnp.sum(d_on * nrm, axis=0, keepdims=True)
                d_n = d_on * nw
                d_o = r * (d_n - nrm * jnp.mean(d_n * nrm, axis=-1, keepdims=True))
                d_attn = jnp.where(mask, _dot(d_o, vh, NT), 0.0)
                dg_ref[rs, 2 * DK + h * GLA_HV:2 * DK + (h + 1) * GLA_HV] = (
                    _dot(attn, d_o, TN) + _dot(ktail, dst, NT)).astype(BF16)
                d_qe = _dot(d_attn, ke) + _dot(d_o, st)
                d_ke = _dot(d_attn, qe, TN)
                d_kt = _dot(vh, dst)
                d_dec = jnp.sum(dst * st, axis=0, keepdims=True)
                dst_ref[h] = dec * dst + _dot(d_o, qe, TN)
                dg_ref[rs, ks] = (d_qe * scale * e).astype(BF16)
                dg_ref[rs, DK + h * GLA_HK:DK + (h + 1) * GLA_HK] = (d_ke * einv + d_kt * etail).astype(BF16)
                d_bl = jnp.sum(d_kt * ktail, axis=0, keepdims=True) + d_dec * dec
                d_b = d_qe * qe - d_ke * ke - d_kt * ktail + jnp.where(last_row, d_bl, 0.0)
                d_lg = _dot_hi(upper, d_b)
                d_a = d_lg * (1.0 / GLA_TAU) * _sigmoid(-a[:, ks])
                ggb_ref[:, ks] += jnp.sum(d_a, axis=0, keepdims=True)
                da_ref[rs, ks] = d_a.astype(BF16)

    c = cps * GLA_CHUNK
    ns = nc // cps
    rn = lambda n: ns - 1 - n
    return pl.pallas_call(
        body, name="gla_bwd", grid=(ns,),
        in_specs=[pl.BlockSpec((c, DV), lambda n: (rn(n), DS // DV))] + _gla_specs(DS, DK, DV, c, rn) + [
            pl.BlockSpec((c, LANES), lambda n: (rn(n), 0)),
            pl.BlockSpec((cps, nh, GLA_HV, GLA_HK), lambda n: (rn(n), 0, 0, 0)),
            pl.BlockSpec((LANES, DK), lambda n: (0, 0)),
            pl.BlockSpec((1, DK), lambda n: (0, 0)),
            pl.BlockSpec((1, GLA_HV), lambda n: (0, 0)),
        ],
        out_specs=[pl.BlockSpec((c, 2 * DK + 2 * DV), lambda n: (rn(n), 0)),
                   pl.BlockSpec((c, DK), lambda n: (rn(n), 0)),
                   pl.BlockSpec((1, GLA_HV), lambda n: (0, 0)), pl.BlockSpec((1, DK), lambda n: (0, 0))],
        out_shape=[jax.ShapeDtypeStruct((L, 2 * DK + 2 * DV), BF16),
                   jax.ShapeDtypeStruct((L, DK), BF16),
                   jax.ShapeDtypeStruct((1, GLA_HV), F32), jax.ShapeDtypeStruct((1, DK), F32)],
        scratch_shapes=[pltpu.VMEM((nh, GLA_HV, GLA_HK), F32)],
        compiler_params=pltpu.CompilerParams(dimension_semantics=("arbitrary",)),
    )(d_ycat, proj_main, proj_main, proj_main, proj_main, proj_low, s_prev, gate_up_pad, gate_bias, norm_w)


def _adamw_math(w, g, m, v):
    c1 = 1.0 - ADAM_B1 ** ADAM_STEP
    c2 = 1.0 - ADAM_B2 ** ADAM_STEP
    m_ = ADAM_B1 * m + (1.0 - ADAM_B1) * g
    v_ = ADAM_B2 * v + (1.0 - ADAM_B2) * (g * g)
    return -ADAM_LR * ((m_ / c1) / (jnp.sqrt(v_ / c2) + ADAM_EPS) + ADAM_WD * w), m_, v_


def _adamw_small(g_row, g_a, g_bc, ws, ms, vs):
    n = len(ws)
    nvec = n - 6

    def body(*refs):
        grow_ref, ga_ref, gbc_ref = refs[:3]
        w_refs, m_refs, v_refs = refs[3:3 + n], refs[3 + n:3 + 2 * n], refs[3 + 2 * n:3 + 3 * n]
        outs = refs[3 + 3 * n:]
        off = 0
        for i in range(n):
            if i < nvec:
                width = ws[i].shape[1]
                g = grow_ref[:, off:off + width]
                off += width
            elif i < nvec + 2:
                g = ga_ref[i - nvec]
            else:
                g = gbc_ref[i - nvec - 2]
            d, m_, v_ = _adamw_math(w_refs[i][...], g, m_refs[i][...], v_refs[i][...])
            outs[i][...] = g
            outs[n + i][...] = d
            outs[2 * n + i][...] = m_
            outs[3 * n + i][...] = v_

    vm = pl.BlockSpec(memory_space=pltpu.VMEM)
    outs = pl.pallas_call(
        body, name="adamw_small",
        in_specs=[vm] * (3 + 3 * n), out_specs=[vm] * (4 * n),
        out_shape=[jax.ShapeDtypeStruct(w.shape, F32) for w in ws] * 4,
    )(g_row, g_a, g_bc, *ws, *ms, *vs)
    return [outs[k * n:(k + 1) * n] for k in range(4)]


def _my_pos():
    return lax.axis_index("x"), lax.axis_index("y"), lax.axis_index("c")


def _gather_weights(shards):
    n = len(shards)
    halves = [s.shape[0] // 2 for s in shards]

    def body(*refs):
        ins, outs = refs[:n], refs[n:2 * n]
        send_sems, recv_sems = refs[2 * n:]
        x, y, c = _my_pos()
        me = 2 * x + y

        def piece(a, chip, half):
            return outs[a].at[chip, pl.ds(half * halves[a], halves[a]), :]

        def copy(a, k, src_chip, half, to):
            sl = piece(a, src_chip, half)
            return pltpu.make_async_remote_copy(src_ref=sl, dst_ref=sl, send_sem=send_sems.at[a, k], recv_sem=recv_sems.at[a, k],
                                                device_id=to, device_id_type=MESH)

        def first(a, d, to):
            src = ins[a].at[pl.ds(c * halves[a], halves[a]), :]
            return pltpu.make_async_remote_copy(src_ref=src, dst_ref=piece(a, me, c), send_sem=send_sems.at[a, d - 1],
                                                recv_sem=recv_sems.at[a, d - 1], device_id=to, device_id_type=MESH)

        sent = []
        for d in (1, 2, 3):
            to = (x ^ (d >> 1), y ^ (d & 1), c)
            for a in range(n):
                cp = first(a, d, to)
                cp.start()
                sent.append(cp)
        for d in (1, 2, 3):
            chip = (x ^ (d >> 1)) * 2 + (y ^ (d & 1))
            for a in range(n):
                copy(a, d - 1, chip, c, (x, y, c)).wait_recv()
                fw = copy(a, 2 + d, chip, c, (x, y, 1 - c))
                fw.start()
                sent.append(fw)
        for d in (1, 2, 3):
            chip = (x ^ (d >> 1)) * 2 + (y ^ (d & 1))
            for a in range(n):
                copy(a, 2 + d, chip, 1 - c, (x, y, c)).wait_recv()
        for cp in sent:
            cp.wait_send()

    hbm = pl.BlockSpec(memory_space=pltpu.HBM)
    return pl.pallas_call(
        body, name="gather_weights",
        in_specs=[hbm] * n, out_specs=[hbm] * n,
        out_shape=[jax.ShapeDtypeStruct((4,) + s.shape, s.dtype) for s in shards],
        scratch_shapes=[pltpu.SemaphoreType.DMA((n, 6)), pltpu.SemaphoreType.DMA((n, 6))],
    )(*shards)


def _late_gather_copies(srcs, lands, send_sems, recv_sems):
    x, y, c = _my_pos()
    me = 2 * x + y
    copies = []
    for d in (1, 2, 3):
        to = (x ^ (d >> 1), y ^ (d & 1), c)
        for a in range(len(srcs)):
            hrows = srcs[a].shape[0] // 2
            rows = pl.ds(c * hrows, hrows)
            copies.append(pltpu.make_async_remote_copy(
                src_ref=srcs[a].at[rows, :], dst_ref=lands[a].at[me, rows, :], send_sem=send_sems.at[3 * a + d - 1],
                recv_sem=recv_sems.at[3 * a + d - 1], device_id=to, device_id_type=MESH))
    return copies


def _late_gather_start(shards, after):
    n = len(shards)

    def body(*refs):
        srcs, lands = refs[:n], refs[n:2 * n]
        send_sems, recv_sems = refs[2 * n + 1], refs[2 * n + 2]
        token = refs[-1]
        for cp in _late_gather_copies(srcs, lands, send_sems, recv_sems):
            cp.start()
        token[...] = jnp.zeros_like(token)

    hbm = pl.BlockSpec(memory_space=pltpu.HBM)
    sem = pl.BlockSpec(memory_space=pltpu.SEMAPHORE)
    outs = pl.pallas_call(
        body, name="late_gather_start",
        in_specs=[hbm] * (2 * n) + [pl.BlockSpec(memory_space=pl.ANY)],
        out_specs=[sem, sem] + [hbm] * (2 * n) + [pl.BlockSpec(memory_space=pltpu.VMEM)],
        out_shape=[pltpu.SemaphoreType.DMA((3 * n,)), pltpu.SemaphoreType.DMA((3 * n,))]
        + [pltpu.HBM(s.shape, s.dtype) for s in shards]
        + [pltpu.HBM((4,) + s.shape, s.dtype) for s in shards]
        + [jax.ShapeDtypeStruct((SUBLANES, LANES), F32)],
        input_output_aliases={i: 2 + i for i in range(2 * n)},
        compiler_params=pltpu.CompilerParams(has_side_effects=pltpu.SideEffectType.DATAFLOW_SIDE_EFFECTING),
    )(*[pltpu.with_memory_space_constraint(s, pltpu.HBM) for s in shards],
      *[pltpu.with_memory_space_constraint(lax.empty((4,) + s.shape, s.dtype), pltpu.HBM) for s in shards], after)
    return outs[0], outs[1], outs[2:2 + n], outs[2 + n:2 + 2 * n], outs[-1]


def _late_gather_wait(send_sems, recv_sems, shards, lands, after):
    n = len(shards)

    def body(*refs):
        src_refs, land_refs = refs[:n], refs[n:2 * n]
        ssem, rsem = refs[2 * n], refs[2 * n + 1]
        for cp in _late_gather_copies(src_refs, land_refs, ssem, rsem):
            cp.wait_send()
            cp.wait_recv()

    hbm = pl.BlockSpec(memory_space=pltpu.HBM)
    sem = pl.BlockSpec(memory_space=pltpu.SEMAPHORE)
    outs = pl.pallas_call(
        body, name="late_gather_wait",
        in_specs=[hbm] * (2 * n) + [sem, sem, pl.BlockSpec(memory_space=pl.ANY)],
        out_specs=[hbm] * (2 * n),
        out_shape=[pltpu.HBM(s.shape, s.dtype) for s in shards] + [pltpu.HBM(p.shape, p.dtype) for p in lands],
        input_output_aliases={i: i for i in range(2 * n)},
        compiler_params=pltpu.CompilerParams(has_side_effects=pltpu.SideEffectType.DATAFLOW_SIDE_EFFECTING),
    )(*shards, *lands, send_sems, recv_sems, after)
    return outs[n:]


def _late_gather_pair(lands):
    n = len(lands)

    def body(*refs):
        outs = refs[n:2 * n]
        send_sems, recv_sems = refs[2 * n:]
        x, y, c = _my_pos()

        def copy(a, d, half):
            chip = 2 * (x ^ (d >> 1)) + (y ^ (d & 1))
            hrows = lands[a].shape[1] // 2
            sl = outs[a].at[chip, pl.ds(half * hrows, hrows), :]
            return pltpu.make_async_remote_copy(src_ref=sl, dst_ref=sl, send_sem=send_sems.at[3 * a + d - 1],
                                                recv_sem=recv_sems.at[3 * a + d - 1], device_id=(x, y, 1 - c),
                                                device_id_type=MESH)

        pairs = [(a, d) for d in (1, 2, 3) for a in range(n)]
        for a, d in pairs:
            copy(a, d, c).start()
        for a, d in pairs:
            copy(a, d, c).wait_send()
            copy(a, d, 1 - c).wait_recv()

    hbm = pl.BlockSpec(memory_space=pltpu.HBM)
    return pl.pallas_call(
        body, name="late_gather_pair", in_specs=[hbm] * n, out_specs=[hbm] * n,
        out_shape=[jax.ShapeDtypeStruct(p.shape, p.dtype) for p in lands],
        input_output_aliases={i: i for i in range(n)},
        scratch_shapes=[pltpu.SemaphoreType.DMA((3 * n,)), pltpu.SemaphoreType.DMA((3 * n,))],
    )(*lands)


def _pair_exchange(gs):
    n = len(gs)

    def body(*refs):
        ins, outs = refs[:n], refs[n:2 * n]
        send_sems, recv_sems = refs[2 * n:]
        x, y, c = _my_pos()
        sent = []
        for a in range(n):
            hrows = gs[a].shape[1] // 2
            cp = pltpu.make_async_remote_copy(
                src_ref=ins[a].at[:, pl.ds((1 - c) * hrows, hrows), :], dst_ref=outs[a], send_sem=send_sems.at[a],
                recv_sem=recv_sems.at[a], device_id=(x, y, 1 - c), device_id_type=MESH)
            cp.start()
            sent.append(cp)
        for cp in sent:
            cp.wait()

    hbm = pl.BlockSpec(memory_space=pltpu.HBM)
    return pl.pallas_call(
        body, name="grad_pair_exchange", in_specs=[hbm] * n, out_specs=[hbm] * n,
        out_shape=[jax.ShapeDtypeStruct((g.shape[0], g.shape[1] // 2, g.shape[2]), g.dtype) for g in gs],
        scratch_shapes=[pltpu.SemaphoreType.DMA((n,)), pltpu.SemaphoreType.DMA((n,))],
    )(*gs)


def _pair_add(g, got, c_arr, name):
    nk, rows2, cols = g.shape
    hrows = rows2 // 2
    tr = _blk(hrows, 256, 2 * SUBLANES)
    nb = hrows // tr

    def body(c_ref, a_ref, b_ref, o_ref):
        o_ref[...] = (a_ref[...].astype(F32) + b_ref[...].astype(F32)).astype(o_ref.dtype)

    return pl.pallas_call(
        body, name=name,
        grid_spec=pltpu.PrefetchScalarGridSpec(
            num_scalar_prefetch=1, grid=(nk, nb),
            in_specs=[pl.BlockSpec((1, tr, cols), lambda k, i, c_ref: (k, c_ref[0] * nb + i, 0)),
                      pl.BlockSpec((1, tr, cols), lambda k, i, c_ref: (k, i, 0))],
            out_specs=pl.BlockSpec((1, tr, cols), lambda k, i, c_ref: (k, i, 0))),
        out_shape=jax.ShapeDtypeStruct((nk, hrows, cols), g.dtype),
        compiler_params=pltpu.CompilerParams(dimension_semantics=("parallel", "parallel")),
    )(c_arr, g, got)


def _chip_scatter_copies(srcs, lands, send_sems, recv_sems):
    x, y, c = _my_pos()
    copies = []
    for d in (1, 2, 3):
        tx, ty = x ^ (d >> 1), y ^ (d & 1)
        for a in range(len(srcs)):
            copies.append(pltpu.make_async_remote_copy(
                src_ref=srcs[a].at[2 * tx + ty], dst_ref=lands[a].at[d - 1], send_sem=send_sems.at[3 * a + d - 1],
                recv_sem=recv_sems.at[3 * a + d - 1], device_id=(tx, ty, c), device_id_type=MESH))
    return copies


def _chip_scatter_start(pss):
    n = len(pss)

    def body(*refs):
        srcs, lands = refs[:n], refs[n:2 * n]
        send_sems, recv_sems = refs[2 * n], refs[2 * n + 1]
        token = refs[-1]
        for cp in _chip_scatter_copies(srcs, lands, send_sems, recv_sems):
            cp.start()
        token[...] = jnp.zeros_like(token)

    hbm = pl.BlockSpec(memory_space=pltpu.HBM)
    sem = pl.BlockSpec(memory_space=pltpu.SEMAPHORE)
    land_shapes = [(3,) + p.shape[1:] for p in pss]
    outs = pl.pallas_call(
        body, name="grad_chip_scatter_start",
        in_specs=[hbm] * (2 * n),
        out_specs=[sem, sem] + [hbm] * (2 * n) + [pl.BlockSpec(memory_space=pltpu.VMEM)],
        out_shape=[pltpu.SemaphoreType.DMA((3 * n,)), pltpu.SemaphoreType.DMA((3 * n,))]
        + [pltpu.HBM(p.shape, p.dtype) for p in pss]
        + [pltpu.HBM(s, p.dtype) for s, p in zip(land_shapes, pss)]
        + [jax.ShapeDtypeStruct((SUBLANES, LANES), F32)],
        input_output_aliases={i: 2 + i for i in range(2 * n)},
        compiler_params=pltpu.CompilerParams(has_side_effects=pltpu.SideEffectType.DATAFLOW_SIDE_EFFECTING),
    )(*[pltpu.with_memory_space_constraint(p, pltpu.HBM) for p in pss],
      *[pltpu.with_memory_space_constraint(lax.empty(s, p.dtype), pltpu.HBM) for s, p in zip(land_shapes, pss)])
    return outs[0], outs[1], outs[2:2 + n], outs[2 + n:2 + 2 * n], outs[-1]


def _chip_scatter_wait(send_sems, recv_sems, srcs, lands, after):
    n = len(srcs)

    def body(*refs):
        src_refs, land_refs = refs[:n], refs[n:2 * n]
        ssem, rsem = refs[2 * n], refs[2 * n + 1]
        for cp in _chip_scatter_copies(src_refs, land_refs, ssem, rsem):
            cp.wait_send()
            cp.wait_recv()

    hbm = pl.BlockSpec(memory_space=pltpu.HBM)
    sem = pl.BlockSpec(memory_space=pltpu.SEMAPHORE)
    outs = pl.pallas_call(
        body, name="grad_chip_scatter_wait",
        in_specs=[hbm] * (2 * n) + [sem, sem, pl.BlockSpec(memory_space=pl.ANY)],
        out_specs=[hbm] * (2 * n),
        out_shape=[pltpu.HBM(p.shape, p.dtype) for p in srcs] + [pltpu.HBM(p.shape, p.dtype) for p in lands],
        input_output_aliases={i: i for i in range(2 * n)},
        compiler_params=pltpu.CompilerParams(has_side_effects=pltpu.SideEffectType.DATAFLOW_SIDE_EFFECTING),
    )(*srcs, *lands, send_sems, recv_sems, after)
    return outs[:n], outs[n:]


def _chip_sum(ps, got, me_arr, name):
    _, hrows, cols = ps.shape
    tr = _blk(hrows, 256, 2 * SUBLANES)

    def body(me_ref, p_ref, g_ref, o_ref):
        acc = p_ref[0].astype(F32)
        for s in range(3):
            acc = acc + g_ref[s].astype(F32)
        o_ref[...] = acc

    return pl.pallas_call(
        body, name=name,
        grid_spec=pltpu.PrefetchScalarGridSpec(
            num_scalar_prefetch=1, grid=(hrows // tr,),
            in_specs=[pl.BlockSpec((1, tr, cols), lambda i, me_ref: (me_ref[0], i, 0)),
                      pl.BlockSpec((3, tr, cols), lambda i, me_ref: (0, i, 0))],
            out_specs=pl.BlockSpec((tr, cols), lambda i, me_ref: (i, 0))),
        out_shape=jax.ShapeDtypeStruct((hrows, cols), F32),
        compiler_params=pltpu.CompilerParams(dimension_semantics=("parallel",)),
    )(me_arr, ps, got)


def _pair_swap(halves):
    n = len(halves)

    def body(*refs):
        ins, outs = refs[:n], refs[n:2 * n]
        send_sems, recv_sems = refs[2 * n:]
        x, y, c = _my_pos()
        sent = []
        for a in range(n):
            cp = pltpu.make_async_remote_copy(src_ref=ins[a], dst_ref=outs[a], send_sem=send_sems.at[a], recv_sem=recv_sems.at[a],
                                              device_id=(x, y, 1 - c), device_id_type=MESH)
            cp.start()
            sent.append(cp)
        for cp in sent:
            cp.wait()

    hbm = pl.BlockSpec(memory_space=pltpu.HBM)
    return pl.pallas_call(
        body, name="grad_pair_swap", in_specs=[hbm] * n, out_specs=[hbm] * n,
        out_shape=[jax.ShapeDtypeStruct(h.shape, h.dtype) for h in halves],
        scratch_shapes=[pltpu.SemaphoreType.DMA((n,)), pltpu.SemaphoreType.DMA((n,))],
    )(*halves)


def _adamw_sharded(w, g_own, g_other, m, v, c_arr, name):
    R, C = w.shape
    hrows = R // 2
    tr = _blk(hrows, 256, SUBLANES)
    nbh = hrows // tr
    c1 = 1.0 - ADAM_B1 ** ADAM_STEP
    c2 = 1.0 - ADAM_B2 ** ADAM_STEP

    def body(c_ref, w_ref, go_ref, gx_ref, m_ref, v_ref, g_ref, d_ref, nm_ref, nv_ref):
        mine = (pl.program_id(0) // nbh) == c_ref[0]
        g_ = jnp.where(mine, go_ref[...], gx_ref[...])
        g_ref[...] = g_
        m_ = ADAM_B1 * m_ref[...] + (1.0 - ADAM_B1) * g_
        v_ = ADAM_B2 * v_ref[...] + (1.0 - ADAM_B2) * (g_ * g_)
        nm_ref[...] = m_
        nv_ref[...] = v_
        d_ref[...] = -ADAM_LR * ((m_ / c1) / (jnp.sqrt(v_ / c2) + ADAM_EPS) + ADAM_WD * w_ref[...])

    blk = pl.BlockSpec((tr, C), lambda i, c_ref: (i, 0))
    hblk = pl.BlockSpec((tr, C), lambda i, c_ref: (i % nbh, 0))
    sd = jax.ShapeDtypeStruct((R, C), F32)
    return pl.pallas_call(
        body, name=name,
        grid_spec=pltpu.PrefetchScalarGridSpec(
            num_scalar_prefetch=1, grid=(2 * nbh,),
            in_specs=[blk, hblk, hblk, blk, blk], out_specs=[blk] * 4),
        out_shape=[sd] * 4,
        compiler_params=pltpu.CompilerParams(dimension_semantics=("parallel",)),
    )(c_arr, w, g_own, g_other, m, v)


def _allreduce_small(arrs):
    n = len(arrs)
    rows = [a.shape[-2] // 8 for a in arrs]

    def piece(ref, a, p):
        start = p * rows[a]
        if rows[a] % SUBLANES == 0:
            start = pl.multiple_of(start, SUBLANES)
        return ref.at[..., pl.ds(start, rows[a]), :]

    def body(*refs):
        v_refs, o_refs, got_refs = refs[:n], refs[n:2 * n], refs[2 * n:3 * n]
        send_sems, recv_sems = refs[3 * n:]
        x, y, c = _my_pos()
        me = 4 * x + 2 * y + c

        def peer(d):
            return (x ^ (d >> 2), y ^ ((d >> 1) & 1), c ^ (d & 1))

        def lin(p):
            return 4 * p[0] + 2 * p[1] + p[2]

        sent = []
        for d in range(1, 8):
            to = peer(d)
            for a in range(n):
                cp = pltpu.make_async_remote_copy(
                    src_ref=piece(v_refs[a], a, lin(to)), dst_ref=got_refs[a].at[d],
                    send_sem=send_sems.at[0, d * n + a], recv_sem=recv_sems.at[0, d * n + a], device_id=to, device_id_type=MESH)
                cp.start()
                sent.append(cp)
        for a in range(n):
            acc = piece(v_refs[a], a, me)[...]
            for d in range(1, 8):
                sent[(d - 1) * n + a].wait_recv()
                acc = acc + got_refs[a][d]
            got_refs[a][0] = acc
            piece(o_refs[a], a, me)[...] = acc
        for d in range(1, 8):
            for a in range(n):
                cp = pltpu.make_async_remote_copy(
                    src_ref=got_refs[a].at[0], dst_ref=piece(o_refs[a], a, me),
                    send_sem=send_sems.at[1, d * n + a], recv_sem=recv_sems.at[1, d * n + a], device_id=peer(d), device_id_type=MESH)
                cp.start()
                sent.append(cp)
        for d in range(1, 8):
            for a in range(n):
                pltpu.make_async_remote_copy(
                    src_ref=got_refs[a].at[0], dst_ref=piece(o_refs[a], a, lin(peer(d))),
                    send_sem=send_sems.at[1, d * n + a], recv_sem=recv_sems.at[1, d * n + a], device_id=peer(d),
                    device_id_type=MESH).wait_recv()
        for cp in sent:
            cp.wait_send()

    vm = pl.BlockSpec(memory_space=pltpu.VMEM)
    return pl.pallas_call(
        body, name="allreduce_small", in_specs=[vm] * n, out_specs=[vm] * n,
        out_shape=[jax.ShapeDtypeStruct(a.shape, F32) for a in arrs],
        scratch_shapes=[pltpu.VMEM((8,) + a.shape[:-2] + (r, a.shape[-1]), F32) for a, r in zip(arrs, rows)]
        + [pltpu.SemaphoreType.DMA((2, 8 * n)), pltpu.SemaphoreType.DMA((2, 8 * n))],
    )(*arrs)


def kernel(x, pre_norm_w, w_in, s5_A_re, s5_A_im, s5_B_re, s5_B_im, s5_C_re, s5_C_im, s5_D, s5_log_dt, s5_glu_w, s5_glu_b, gla_gate_up, gla_gate_bias, gla_norm_w, w_out, post_norm_w, loss_target, m_pre_norm_w, m_w_in, m_s5_A_re, m_s5_A_im, m_s5_B_re, m_s5_B_im, m_s5_C_re, m_s5_C_im, m_s5_D, m_s5_log_dt, m_s5_glu_w, m_s5_glu_b, m_gla_gate_up, m_gla_gate_bias, m_gla_norm_w, m_w_out, m_post_norm_w, v_pre_norm_w, v_w_in, v_s5_A_re, v_s5_A_im, v_s5_B_re, v_s5_B_im, v_s5_C_re, v_s5_C_im, v_s5_D, v_s5_log_dt, v_s5_glu_w, v_s5_glu_b, v_gla_gate_up, v_gla_gate_bias, v_gla_norm_w, v_w_out, v_post_norm_w):
    names = ["pre_norm_w", "w_in", "s5_A_re", "s5_A_im", "s5_B_re", "s5_B_im", "s5_C_re", "s5_C_im", "s5_D", "s5_log_dt",
             "s5_glu_w", "s5_glu_b", "gla_gate_up", "gla_gate_bias", "gla_norm_w", "w_out", "post_norm_w"]
    W = dict(zip(names, (pre_norm_w, w_in, s5_A_re, s5_A_im, s5_B_re, s5_B_im, s5_C_re, s5_C_im, s5_D, s5_log_dt,
                         s5_glu_w, s5_glu_b, gla_gate_up, gla_gate_bias, gla_norm_w, w_out, post_norm_w)))
    M = dict(zip(names, (m_pre_norm_w, m_w_in, m_s5_A_re, m_s5_A_im, m_s5_B_re, m_s5_B_im, m_s5_C_re, m_s5_C_im, m_s5_D,
                         m_s5_log_dt, m_s5_glu_w, m_s5_glu_b, m_gla_gate_up, m_gla_gate_bias, m_gla_norm_w, m_w_out,
                         m_post_norm_w)))
    V = dict(zip(names, (v_pre_norm_w, v_w_in, v_s5_A_re, v_s5_A_im, v_s5_B_re, v_s5_B_im, v_s5_C_re, v_s5_C_im, v_s5_D,
                         v_s5_log_dt, v_s5_glu_w, v_s5_glu_b, v_gla_gate_up, v_gla_gate_bias, v_gla_norm_w, v_w_out,
                         v_post_norm_w)))
    sharded = ("w_in", "s5_glu_w", "w_out", "gla_gate_up")

    xb = x[0]
    tgt = loss_target[0]
    L, D = xb.shape
    DS = D // 2
    G = DS // S5_GROUP
    P = S5_STATE
    NB = DS // S5_COLS
    DV = D - DS
    DK = DV // 2
    WM = 2 * DS + 2 * DK + 2 * DV
    nsh = w_in.shape[2]

    chip = 2 * lax.axis_index("x") + lax.axis_index("y")
    own = [w_in[0].astype(BF16), s5_glu_w[0].astype(BF16), w_out[0].astype(BF16), gla_gate_up[0]]
    fill = lambda g, o: lax.dynamic_update_index_in_dim(g, o, chip, 0)
    g_win = fill(_gather_weights(own[:1])[0], own[0])
    w_full = jnp.moveaxis(g_win, 0, 1).reshape(D, 4 * nsh)
    w_main = w_full[:, :WM]
    w_low = jnp.pad(w_full[:, WM:], ((0, 0), (0, LANES - GLA_RANK)))
    late_ss, late_rs, late_src, late_lands, late_token = _late_gather_start(own[1:], g_win)

    b_view = lambda t: jnp.transpose(t[0], (0, 2, 1)).reshape(G * S5_GROUP, P)
    b_back = lambda t: jnp.transpose(t.reshape(G, S5_GROUP, P), (0, 2, 1))[None]
    c_view = lambda t: t[0].reshape(G * S5_GROUP, P)
    c_back = lambda t: t.reshape(1, G, S5_GROUP, P)
    small = ["pre_norm_w", "post_norm_w", "s5_D", "s5_glu_b", "gla_gate_bias", "gla_norm_w", "s5_log_dt",
             "s5_A_re", "s5_A_im", "s5_B_re", "s5_B_im", "s5_C_re", "s5_C_im"]
    view = {n: (lambda t: t) for n in small[:7]}
    back = dict(view)
    view.update(s5_A_re=lambda t: t[0], s5_A_im=lambda t: t[0], s5_B_re=b_view, s5_B_im=b_view, s5_C_re=c_view, s5_C_im=c_view)
    back.update(s5_A_re=lambda t: t[None], s5_A_im=lambda t: t[None], s5_B_re=b_back, s5_B_im=b_back, s5_C_re=c_back,
                s5_C_im=c_back)
    Wv = {n: view[n](W[n]) for n in small}
    bbd_re, bbd_im, ct_re, ct_im, tab, ptab = _s5_prep_fwd(
        Wv["s5_A_re"], Wv["s5_A_im"], s5_log_dt, Wv["s5_B_re"], Wv["s5_B_im"], Wv["s5_C_re"], Wv["s5_C_im"],
        _blk(L, 512, SUBLANES) // SUBLANES)
    dvec = s5_D

    h = _prenorm_fwd(xb, pre_norm_w)
    proj_main, proj_low = _in_proj(h, w_main, w_low, late_token)
    y_pre, s_re, s_im = _s5_scan_fwd(proj_main, bbd_re, bbd_im, ct_re, ct_im, dvec, tab, ptab, DS)
    late = _late_gather_pair(_late_gather_wait(late_ss, late_rs, late_src, late_lands, y_pre))
    g_glu, g_wout, g_gup = [fill(g, o) for g, o in zip(late, own[1:])]
    glu_w = g_glu.reshape(DS, DS)
    wout = g_wout.reshape(D, D)
    gup = jnp.moveaxis(g_gup, 0, 1).reshape(GLA_RANK, DK)
    gup_pad = jnp.pad(gup, ((0, LANES - GLA_RANK), (0, 0))).astype(BF16)
    ycat, t_pre = _s5_post_fwd(y_pre, proj_main, glu_w, s5_glu_b, DS)
    ycat, s_prev = _gla_fwd(proj_main, proj_low, gup_pad, gla_gate_bias, gla_norm_w, ycat, DS, DK, DV)
    mixed = _mm(ycat, wout, name="out_proj")
    loss11, d_mixed, dout, g_post_w = _post_fwd_bwd(mixed, xb, tgt, post_norm_w)

    d_ycat = _mm(d_mixed, wout, tb=True, name="out_proj_dx")
    g_wout_full = _mm(ycat, d_mixed, ta=True, out_dtype=BF16, name="out_proj_dw")
    d_ypre, d_s5, d_t, y1, g_glu_b = _s5_post_bwd(d_ycat, y_pre, proj_main, t_pre, glu_w, DS)
    g_glu_full = _mm(y1, d_t, ta=True, out_dtype=BF16, name="glu_dw")
    d_s5, g_D, gct_re, gct_im, gbbd_re, gbbd_im, gab_re, gab_im = _s5_scan_bwd(
        d_ypre, proj_main, s_re, s_im, bbd_re, bbd_im, ct_re, ct_im, dvec, tab, ptab, d_s5, DS)
    d_gla, d_a, g_norm_w, g_gate_bias = _gla_bwd(
        d_ycat, proj_main, proj_low, s_prev, gup_pad, gla_gate_bias, gla_norm_w, DS, DK, DV)
    d_low = _mm(d_a, gup_pad, tb=True, out_dtype=BF16, name="gate_dx")
    g_gup_pad = _mm(proj_low, d_a, ta=True, name="gate_dw")
    g_wmain, g_wlow = _in_proj_dw(h, d_s5, d_gla, d_low)

    g_win_full = jnp.concatenate([g_wmain, g_wlow[:, :GLA_RANK]], axis=1)
    gs = [jnp.moveaxis(g_win_full.reshape(D, 4, nsh), 1, 0),
          g_glu_full.reshape(4, DS // 4, DS),
          g_wout_full.reshape(4, D // 4, D),
          jnp.moveaxis(g_gup_pad[:GLA_RANK].reshape(GLA_RANK, 4, DK // 4), 1, 0)]
    c_arr = lax.axis_index("c").astype(jnp.int32).reshape(1)
    me_arr = chip.astype(jnp.int32).reshape(1)
    got = _pair_exchange(gs)
    pss = [_pair_add(g, r, c_arr, "grad_pair_add_" + n) for n, g, r in zip(sharded, gs, got)]
    send_sems, recv_sems, pss, lands, token = _chip_scatter_start(pss)

    dh = _in_proj_dx(d_s5, d_gla, d_low, w_main, w_low, token)
    grad_x, g_pre_w = _prenorm_bwd(xb, dh, dout, pre_norm_w)
    pss, rcv = _chip_scatter_wait(send_sems, recv_sems, pss, lands, g_pre_w)

    g_a, g_bc, g_ldt = _s5_prep_bwd(Wv["s5_A_re"], Wv["s5_A_im"], s5_log_dt, Wv["s5_B_re"], Wv["s5_B_im"],
                                    gbbd_re, gbbd_im, gct_re, gct_im, gab_re, gab_im)

    loss = lax.psum(loss11[0, 0], ("x", "y", "c"))

    g_vecs = jnp.concatenate([g_pre_w, g_post_w, g_D, g_glu_b, g_gate_bias, g_norm_w, g_ldt], axis=1)
    lanes_pad = -g_vecs.shape[1] % (8 * SUBLANES * LANES)
    g_vecs = jnp.pad(g_vecs, ((0, 0), (0, lanes_pad))).reshape(-1, LANES)
    r_vecs, r_a, r_bc = _allreduce_small([g_vecs, g_a, g_bc])
    outs4 = _adamw_small(r_vecs.reshape(1, -1), r_a, r_bc, [Wv[n] for n in small],
                         [view[n](M[n]) for n in small], [view[n](V[n]) for n in small])
    G_out, D_out, M_out, V_out = [{n: back[n](t) for n, t in zip(small, o)} for o in outs4]

    halves = [_chip_sum(p, r, me_arr, "grad_chip_sum_" + n) for n, p, r in zip(sharded, pss, rcv)]
    others = _pair_swap(halves)
    for n, g_own, g_other in zip(sharded, halves, others):
        g_, d_, m_, v_ = _adamw_sharded(W[n][0], g_own, g_other, M[n][0], V[n][0], c_arr, "adamw_" + n)
        G_out[n], D_out[n], M_out[n], V_out[n] = g_[None], d_[None], m_[None], v_[None]

    return (loss, grad_x[None], *[G_out[n] for n in names], *[D_out[n] for n in names],
            *[M_out[n] for n in names], *[V_out[n] for n in names])
```

```python
import functools
import math

import jax
import jax.numpy as jnp
from jax import lax
from jax.experimental import pallas as pl
from jax.experimental.pallas import tpu as pltpu

F32 = jnp.float32
BF16 = jnp.bfloat16
HI = lax.Precision.HIGHEST
MESH = pl.DeviceIdType.MESH

EPS = 1e-6
S5_GROUP = 16
S5_STATE = 64
GLA_HK = 128
GLA_HV = 256
GLA_RANK = 16
GLA_TAU = 16.0
GLA_CHUNK = 64
GLA_STEP_CHUNKS = 2
LANES = 128
SUBLANES = 8
S5_COLS = 128
S5_LANES = (S5_COLS // S5_GROUP) * S5_STATE

ADAM_LR = 0.001
ADAM_B1 = 0.9
ADAM_B2 = 0.999
ADAM_EPS = 1e-08
ADAM_WD = 0.01
ADAM_STEP = 10

GELU_K = math.sqrt(2.0 / math.pi)
GELU_C = 0.044715


def _blk(n, pref, unit=LANES):
    best = None
    b = unit
    while b <= min(n, pref):
        if n % b == 0:
            best = b
        b += unit
    return best if best is not None else n


def _dot(a, b, dn=(((1,), (0,)), ((), ()))):
    return lax.dot_general(a.astype(BF16), b.astype(BF16), dn, preferred_element_type=F32)


def _dot_hi(a, b, dn=(((1,), (0,)), ((), ()))):
    return lax.dot_general(a, b, dn, precision=HI, preferred_element_type=F32)


NN = (((1,), (0,)), ((), ()))
NT = (((1,), (1,)), ((), ()))
TN = (((0,), (0,)), ((), ()))


def _sigmoid(x):
    return 1.0 / (1.0 + jnp.exp(-x))


def _gelu(y):
    return 0.5 * y * (1.0 + jnp.tanh(GELU_K * (y + GELU_C * y * y * y)))


def _gelu_grad(y):
    th = jnp.tanh(GELU_K * (y + GELU_C * y * y * y))
    return 0.5 * (1.0 + th) + 0.5 * y * (1.0 - th * th) * GELU_K * (1.0 + 3.0 * GELU_C * y * y)


def _mm(a, b, *, name, ta=False, tb=False, out_dtype=F32, bm=1024, bn=1024, bk=2048):
    if ta:
        K, M = a.shape
    else:
        M, K = a.shape
    if tb:
        N, K2 = b.shape
    else:
        K2, N = b.shape
    assert K == K2, (a.shape, b.shape, ta, tb)
    bm, bn, bk = _blk(M, bm), _blk(N, bn), _blk(K, bk)
    nk = K // bk
    dn = (((0 if ta else 1,), (1 if tb else 0,)), ((), ()))

    def body(a_ref, b_ref, o_ref, *acc):
        if nk == 1:
            o_ref[...] = _dot(a_ref[...], b_ref[...], dn).astype(out_dtype)
            return
        acc_ref, = acc
        k = pl.program_id(2)

        @pl.when(k == 0)
        def _():
            acc_ref[...] = jnp.zeros_like(acc_ref)

        acc_ref[...] += _dot(a_ref[...], b_ref[...], dn)

        @pl.when(k == nk - 1)
        def _():
            o_ref[...] = acc_ref[...].astype(out_dtype)

    a_spec = pl.BlockSpec((bk, bm), lambda i, j, k: (k, i)) if ta else pl.BlockSpec((bm, bk), lambda i, j, k: (i, k))
    b_spec = pl.BlockSpec((bn, bk), lambda i, j, k: (j, k)) if tb else pl.BlockSpec((bk, bn), lambda i, j, k: (k, j))
    return pl.pallas_call(
        body,
        name=name,
        grid=(M // bm, N // bn, nk),
        in_specs=[a_spec, b_spec],
        out_specs=pl.BlockSpec((bm, bn), lambda i, j, k: (i, j)),
        out_shape=jax.ShapeDtypeStruct((M, N), out_dtype),
        scratch_shapes=[pltpu.VMEM((bm, bn), F32)] if nk > 1 else [],
        compiler_params=pltpu.CompilerParams(dimension_semantics=("parallel", "parallel", "arbitrary")),
    )(a, b)


def _in_proj(h, w_main, w_low, after):
    M, K = h.shape
    N = w_main.shape[1]
    bm, bn = _blk(M, 1024), _blk(N, 1024)

    def body(h_ref, w_ref, wl_ref, _after_ref, o_ref, ol_ref):
        hv = h_ref[...]
        o_ref[...] = _dot(hv, w_ref[...])

        @pl.when(pl.program_id(1) == 0)
        def _():
            ol_ref[...] = _dot(hv, wl_ref[...])

    return pl.pallas_call(
        body, name="in_proj", grid=(M // bm, N // bn),
        in_specs=[pl.BlockSpec((bm, K), lambda i, j: (i, 0)), pl.BlockSpec((K, bn), lambda i, j: (0, j)),
                  pl.BlockSpec((K, LANES), lambda i, j: (0, 0)), pl.BlockSpec(memory_space=pl.ANY)],
        out_specs=[pl.BlockSpec((bm, bn), lambda i, j: (i, j)), pl.BlockSpec((bm, LANES), lambda i, j: (i, 0))],
        out_shape=[jax.ShapeDtypeStruct((M, N), F32), jax.ShapeDtypeStruct((M, LANES), F32)],
        compiler_params=pltpu.CompilerParams(dimension_semantics=("parallel", "arbitrary")),
    )(h, w_main, w_low, after)


def _in_proj_dx(a1, a2, al, b, bl, after, *, bm=1024, bn=1024, bk=2048):
    M, K1 = a1.shape
    K2 = a2.shape[1]
    N = b.shape[0]
    bm, bn = _blk(M, bm), _blk(N, bn)
    bk = _blk(math.gcd(K1, K2), bk)
    nk1, nk = K1 // bk, (K1 + K2) // bk

    def body(a1_ref, a2_ref, al_ref, b_ref, bl_ref, _after_ref, o_ref, acc_ref):
        k = pl.program_id(2)

        @pl.when(k == 0)
        def _():
            acc_ref[...] = _dot(al_ref[...], bl_ref[...], NT)

        @pl.when(k < nk1)
        def _():
            acc_ref[...] += _dot(a1_ref[...], b_ref[...], NT)

        @pl.when(k >= nk1)
        def _():
            acc_ref[...] += _dot(a2_ref[...], b_ref[...], NT)

        @pl.when(k == nk - 1)
        def _():
            o_ref[...] = acc_ref[...]

    return pl.pallas_call(
        body, name="in_proj_dx", grid=(M // bm, N // bn, nk),
        in_specs=[pl.BlockSpec((bm, bk), lambda i, j, k: (i, jnp.minimum(k, nk1 - 1))),
                  pl.BlockSpec((bm, bk), lambda i, j, k: (i, jnp.maximum(k - nk1, 0))),
                  pl.BlockSpec((bm, LANES), lambda i, j, k: (i, 0)),
                  pl.BlockSpec((bn, bk), lambda i, j, k: (j, k)),
                  pl.BlockSpec((bn, LANES), lambda i, j, k: (j, 0)),
                  pl.BlockSpec(memory_space=pl.ANY)],
        out_specs=pl.BlockSpec((bm, bn), lambda i, j, k: (i, j)),
        out_shape=jax.ShapeDtypeStruct((M, N), F32),
        scratch_shapes=[pltpu.VMEM((bm, bn), F32)],
        compiler_params=pltpu.CompilerParams(dimension_semantics=("parallel", "parallel", "arbitrary")),
    )(a1, a2, al, b, bl, after)


def _in_proj_dw(a, b1, b2, bl, *, bm=1024, bn=1024, bk=2048):
    K, M = a.shape
    N1, N2 = b1.shape[1], b2.shape[1]
    bm, bk = _blk(M, bm), _blk(K, bk)
    bn = _blk(math.gcd(N1, N2), bn)
    nj1, nj = N1 // bn, (N1 + N2) // bn
    nk = K // bk

    def body(a_ref, b1_ref, b2_ref, bl_ref, o_ref, ol_ref, acc_ref, accl_ref):
        j = pl.program_id(1)
        k = pl.program_id(2)

        @pl.when(k == 0)
        def _():
            acc_ref[...] = jnp.zeros_like(acc_ref)

        @pl.when(j < nj1)
        def _():
            acc_ref[...] += _dot(a_ref[...], b1_ref[...], TN)

        @pl.when(j >= nj1)
        def _():
            acc_ref[...] += _dot(a_ref[...], b2_ref[...], TN)

        @pl.when(k == nk - 1)
        def _():
            o_ref[...] = acc_ref[...].astype(BF16)

        @pl.when(j == 0)
        def _():
            low = _dot(a_ref[...], bl_ref[...], TN)

            @pl.when(k == 0)
            def _():
                accl_ref[...] = low

            @pl.when(k > 0)
            def _():
                accl_ref[...] += low

            @pl.when(k == nk - 1)
            def _():
                ol_ref[...] = accl_ref[...].astype(BF16)

    return pl.pallas_call(
        body, name="in_proj_dw", grid=(M // bm, nj, nk),
        in_specs=[pl.BlockSpec((bk, bm), lambda i, j, k: (k, i)),
                  pl.BlockSpec((bk, bn), lambda i, j, k: (jnp.where(j < nj1, k, nk - 1), jnp.minimum(j, nj1 - 1))),
                  pl.BlockSpec((bk, bn), lambda i, j, k: (jnp.where(j >= nj1, k, 0), jnp.maximum(j - nj1, 0))),
                  pl.BlockSpec((bk, LANES), lambda i, j, k: (jnp.where(j == 0, k, nk - 1), 0))],
        out_specs=[pl.BlockSpec((bm, bn), lambda i, j, k: (i, j)), pl.BlockSpec((bm, LANES), lambda i, j, k: (i, 0))],
        out_shape=[jax.ShapeDtypeStruct((M, N1 + N2), BF16), jax.ShapeDtypeStruct((M, LANES), BF16)],
        scratch_shapes=[pltpu.VMEM((bm, bn), F32), pltpu.VMEM((bm, LANES), F32)],
        compiler_params=pltpu.CompilerParams(dimension_semantics=("parallel", "arbitrary", "arbitrary")),
    )(a, b1, b2, bl)


def _prenorm_fwd(x, w, after):
    L, D = x.shape
    tr = _blk(L, 256, SUBLANES)

    def body(x_ref, w_ref, _after_ref, h_ref):
        xv = x_ref[...]
        r = lax.rsqrt(jnp.mean(xv * xv, axis=-1, keepdims=True) + EPS)
        h_ref[...] = (xv * r * w_ref[...]).astype(BF16)

    return pl.pallas_call(
        body, name="prenorm_fwd", grid=(L // tr,),
        in_specs=[pl.BlockSpec((tr, D), lambda i: (i, 0)), pl.BlockSpec((1, D), lambda i: (0, 0)),
                  pl.BlockSpec(memory_space=pl.ANY)],
        out_specs=pl.BlockSpec((tr, D), lambda i: (i, 0)),
        out_shape=jax.ShapeDtypeStruct((L, D), BF16),
        compiler_params=pltpu.CompilerParams(dimension_semantics=("parallel",)),
    )(x, w, after)


def _post_fwd_bwd(mixed, x, target, w):
    L, D = x.shape
    tr = _blk(L, 256, SUBLANES)
    nsteps = L // tr

    def body(mx_ref, x_ref, t_ref, w_ref, loss_ref, dm_ref, dout_ref, gw_ref, acc_ref):
        i = pl.program_id(0)

        @pl.when(i == 0)
        def _():
            acc_ref[...] = jnp.zeros_like(acc_ref)
            gw_ref[...] = jnp.zeros_like(gw_ref)

        mx = mx_ref[...]
        wv = w_ref[...]
        r = lax.rsqrt(jnp.mean(mx * mx, axis=-1, keepdims=True) + EPS)
        n = mx * r
        err = x_ref[...] + n * wv - t_ref[...]
        acc_ref[...] += jnp.sum(err * err, axis=0, keepdims=True)
        dout = err * (1.0 / D)
        dout_ref[...] = dout
        gw_ref[...] += jnp.sum(dout * n, axis=0, keepdims=True)
        dn = dout * wv
        dm_ref[...] = (r * (dn - n * jnp.mean(dn * n, axis=-1, keepdims=True))).astype(BF16)

        @pl.when(i == nsteps - 1)
        def _():
            loss_ref[...] = jnp.sum(acc_ref[...], axis=-1, keepdims=True) * (0.5 / D)

    row = pl.BlockSpec((tr, D), lambda i: (i, 0))
    vec = pl.BlockSpec((1, D), lambda i: (0, 0))
    return pl.pallas_call(
        body, name="post_fwd_bwd", grid=(nsteps,),
        in_specs=[row, row, row, vec],
        out_specs=[pl.BlockSpec((1, 1), lambda i: (0, 0)), row, row, vec],
        out_shape=[jax.ShapeDtypeStruct((1, 1), F32), jax.ShapeDtypeStruct((L, D), BF16),
                   jax.ShapeDtypeStruct((L, D), F32), jax.ShapeDtypeStruct((1, D), F32)],
        scratch_shapes=[pltpu.VMEM((1, D), F32)],
        compiler_params=pltpu.CompilerParams(dimension_semantics=("arbitrary",)),
    )(mixed, x, target, w)


def _prenorm_bwd(x, dh, dout, w):
    L, D = x.shape
    tr = _blk(L, 256, SUBLANES)

    def body(x_ref, a_ref, dout_ref, w_ref, gx_ref, gw_ref):
        i = pl.program_id(0)

        @pl.when(i == 0)
        def _():
            gw_ref[...] = jnp.zeros_like(gw_ref)

        xv = x_ref[...]
        r = lax.rsqrt(jnp.mean(xv * xv, axis=-1, keepdims=True) + EPS)
        n = xv * r
        dh = a_ref[...]
        gw_ref[...] += jnp.sum(dh * n, axis=0, keepdims=True)
        dn = dh * w_ref[...]
        gx_ref[...] = dout_ref[...] + r * (dn - n * jnp.mean(dn * n, axis=-1, keepdims=True))

    row = pl.BlockSpec((tr, D), lambda i: (i, 0))
    vec = pl.BlockSpec((1, D), lambda i: (0, 0))
    return pl.pallas_call(
        body, name="prenorm_bwd", grid=(L // tr,),
        in_specs=[row, row, row, vec],
        out_specs=[row, vec],
        out_shape=[jax.ShapeDtypeStruct((L, D), F32), jax.ShapeDtypeStruct((1, D), F32)],
        compiler_params=pltpu.CompilerParams(dimension_semantics=("arbitrary",)),
    )(x, dh, dout, w)


def _s5_disc(a_re_raw, a_im, dt):
    a_re = jnp.minimum(a_re_raw, -1e-4)
    mag = jnp.exp(a_re * dt)
    ph = a_im * dt
    ab_re = mag * jnp.cos(ph)
    ab_im = mag * jnp.sin(ph)
    inv_n = 1.0 / (a_re * a_re + a_im * a_im)
    ia_re = a_re * inv_n
    ia_im = -a_im * inv_n
    n_re = ab_re - 1.0
    f_re = n_re * ia_re - ab_im * ia_im
    f_im = n_re * ia_im + ab_im * ia_re
    return a_re, ab_re, ab_im, f_re, f_im, ia_re, ia_im


def _iota2(shape, dim):
    return lax.broadcasted_iota(jnp.int32, shape, dim)


def _group_mask(rows, rows_per_group):
    shift = rows_per_group.bit_length() - 1
    return (_iota2((rows, S5_LANES), 0) >> shift) == (_iota2((rows, S5_LANES), 1) >> (S5_STATE.bit_length() - 1))


def _lane_tiler(dtype):
    return ((_iota2((S5_STATE, S5_LANES), 1) & (S5_STATE - 1)) == _iota2((S5_STATE, S5_LANES), 0)).astype(dtype)


def _row_to_col(row, n):
    eye = (_iota2((n, n), 0) == _iota2((n, n), 1)).astype(F32)
    return jnp.sum(eye * row, axis=1, keepdims=True)


def _group_repeat(G):
    return ((_iota2((G * S5_GROUP, G), 0) >> (S5_GROUP.bit_length() - 1)) == _iota2((G * S5_GROUP, G), 1)).astype(F32)


S5_TABS = 18


def _s5_prep_fwd(a_re, a_im, log_dt, b_re, b_im, c_re, c_im, seg):
    G, P = a_re.shape
    nb = G * S5_GROUP // S5_COLS
    g8 = S5_COLS // S5_GROUP
    assert seg & (seg - 1) == 0, seg

    def body(are_ref, aim_ref, ldt_ref, bre_ref, bim_ref, cre_ref, cim_ref,
             bbre_ref, bbim_ref, ctre_ref, ctim_ref, tab_ref, pt_ref):
        dt = jnp.exp(_row_to_col(ldt_ref[...], G))
        _, ab_re, ab_im, f_re, f_im, _, _ = _s5_disc(are_ref[...], aim_ref[...], dt)
        rep = _group_repeat(G)
        fx_re = _dot_hi(rep, f_re)
        fx_im = _dot_hi(rep, f_im)
        br, bi = bre_ref[...], bim_ref[...]
        bb_re = fx_re * br - fx_im * bi
        bb_im = fx_re * bi + fx_im * br
        tile_bf = _lane_tiler(BF16)
        mask = _group_mask(S5_COLS, S5_GROUP)
        for jb in range(nb):
            rs = slice(jb * S5_COLS, (jb + 1) * S5_COLS)
            for src, dst in ((bb_re[rs], bbre_ref), (bb_im[rs], bbim_ref), (cre_ref[rs, :], ctre_ref), (cim_ref[rs, :], ctim_ref)):
                dst[jb] = jnp.where(mask, _dot(src, tile_bf), 0.0).astype(BF16)

        tile_f = _lane_tiler(F32)
        mask8 = _group_mask(g8, 1)
        row = _iota2((SUBLANES, S5_LANES), 0)
        slab = (SUBLANES, S5_LANES)
        cmul = lambda p, q: (p[0] * q[0] - p[1] * q[1], p[0] * q[1] + p[1] * q[0])
        for jb in range(nb):
            gs = slice(jb * g8, (jb + 1) * g8)

            def lanes(m):
                v = jnp.sum(jnp.where(mask8, _dot_hi(m[gs], tile_f), 0.0), axis=0, keepdims=True)
                return jnp.broadcast_to(v, slab)

            a1 = (lanes(ab_re), lanes(ab_im))
            tab_ref[jb, 0], tab_ref[jb, 1] = a1

            def powers(i, p):
                off = pl.multiple_of(i * SUBLANES, SUBLANES)
                pt_ref[jb, 0, pl.ds(off, SUBLANES), :] = p[0]
                pt_ref[jb, 1, pl.ds(off, SUBLANES), :] = p[1]
                return cmul(p, a1)

            lax.fori_loop(0, seg, powers, a1)
            aseg = a1
            for _ in range(seg.bit_length() - 1):
                aseg = cmul(aseg, aseg)
            pw = [aseg]
            for _ in range(1, SUBLANES):
                pw.append(cmul(pw[-1], aseg))
            for lvl, k in enumerate((1, 2, 4)):
                tab_ref[jb, 2 + 2 * lvl] = jnp.where(row >= k, pw[k - 1][0], 0.0)
                tab_ref[jb, 3 + 2 * lvl] = jnp.where(row >= k, pw[k - 1][1], 0.0)
                tab_ref[jb, 10 + 2 * lvl] = jnp.where(row < SUBLANES - k, pw[k - 1][0], 0.0)
                tab_ref[jb, 11 + 2 * lvl] = jnp.where(row < SUBLANES - k, -pw[k - 1][1], 0.0)
            f_r = f_i = r_r = r_i = jnp.zeros(slab, F32)
            for i in range(SUBLANES):
                f_r = jnp.where(row == i, pw[i][0], f_r)
                f_i = jnp.where(row == i, pw[i][1], f_i)
                r_r = jnp.where(row == i, pw[SUBLANES - 1 - i][0], r_r)
                r_i = jnp.where(row == i, -pw[SUBLANES - 1 - i][1], r_i)
            tab_ref[jb, 8] = f_r
            tab_ref[jb, 9] = f_i
            tab_ref[jb, 16] = r_r
            tab_ref[jb, 17] = r_i

    vm = pl.BlockSpec(memory_space=pltpu.VMEM)
    bd = jax.ShapeDtypeStruct((nb, S5_COLS, S5_LANES), BF16)
    return pl.pallas_call(
        body, name="s5_prep_fwd",
        in_specs=[vm] * 7, out_specs=[vm] * 6,
        out_shape=[bd, bd, bd, bd, jax.ShapeDtypeStruct((nb, S5_TABS, SUBLANES, S5_LANES), F32),
                   jax.ShapeDtypeStruct((nb, 2, seg * SUBLANES, S5_LANES), F32)],
    )(a_re, a_im, log_dt, b_re, b_im, c_re, c_im)


def _s5_prep_bwd(a_re, a_im, log_dt, b_re, b_im, gbb_re, gbb_im, gct_re, gct_im, gab_re, gab_im):
    G, P = a_re.shape
    nb = G * S5_GROUP // S5_COLS
    g8 = S5_COLS // S5_GROUP

    def body(are_ref, aim_ref, ldt_ref, bre_ref, bim_ref, gbr_ref, gbi_ref, gcr_ref, gci_ref, gar_ref, gai_ref,
             o_a, o_bc, o_ldt):
        dt = jnp.exp(_row_to_col(ldt_ref[...], G))
        a_raw = are_ref[...]
        a_imv = aim_ref[...]
        a_re_c, ab_re, ab_im, f_re, f_im, ia_re, ia_im = _s5_disc(a_raw, a_imv, dt)
        tile_f = _lane_tiler(F32)
        mask = _group_mask(S5_COLS, S5_GROUP)
        mask8 = _group_mask(g8, 1)
        for jb in range(nb):
            rs = slice(jb * S5_COLS, (jb + 1) * S5_COLS)
            gs = slice(jb * g8, (jb + 1) * g8)
            ls = slice(jb * S5_LANES, (jb + 1) * S5_LANES)
            for k, src in enumerate((gbr_ref, gbi_ref, gcr_ref, gci_ref)):
                o_bc[k, rs, :] = _dot_hi(jnp.where(mask, src[jb], 0.0), tile_f, NT)
            for k, src in enumerate((gar_ref, gai_ref)):
                o_a[k, gs, :] = _dot_hi(jnp.where(mask8, src[:, ls], 0.0), tile_f, NT)
        rep = _group_repeat(G)
        fx_re = _dot_hi(rep, f_re)
        fx_im = _dot_hi(rep, f_im)
        gbr, gbi = o_bc[0], o_bc[1]
        br, bi = bre_ref[...], bim_ref[...]
        o_bc[0] = fx_re * gbr + fx_im * gbi
        o_bc[1] = fx_re * gbi - fx_im * gbr
        gf_re = _dot_hi(rep, br * gbr + bi * gbi, TN)
        gf_im = _dot_hi(rep, br * gbi - bi * gbr, TN)
        gab_r = o_a[0] + ia_re * gf_re + ia_im * gf_im
        gab_i = o_a[1] + ia_re * gf_im - ia_im * gf_re
        q_re = f_re * ia_re - f_im * ia_im
        q_im = f_re * ia_im + f_im * ia_re
        ga_re = -(q_re * gf_re + q_im * gf_im)
        ga_im = -(q_re * gf_im - q_im * gf_re)
        gth_re = ab_re * gab_r + ab_im * gab_i
        gth_im = ab_re * gab_i - ab_im * gab_r
        ga_re = ga_re + dt * gth_re
        ga_im = ga_im + dt * gth_im
        gdt = jnp.sum(a_re_c * gth_re + a_imv * gth_im, axis=-1, keepdims=True)
        eye = (_iota2((G, G), 0) == _iota2((G, G), 1)).astype(F32)
        o_ldt[...] = jnp.sum(eye * (gdt * dt), axis=0, keepdims=True)
        slope = jnp.where(a_raw < -1e-4, 1.0, jnp.where(a_raw == -1e-4, 0.5, 0.0))
        o_a[0] = ga_re * slope
        o_a[1] = ga_im

    vm = pl.BlockSpec(memory_space=pltpu.VMEM)
    return pl.pallas_call(
        body, name="s5_prep_bwd",
        in_specs=[vm] * 11, out_specs=[vm] * 3,
        out_shape=[jax.ShapeDtypeStruct((2, G, P), F32), jax.ShapeDtypeStruct((4, G * S5_GROUP, P), F32),
                   jax.ShapeDtypeStruct((1, G), F32)],
    )(a_re, a_im, log_dt, b_re, b_im, gbb_re, gbb_im, gct_re, gct_im, gab_re, gab_im)


def _scan8(xr, xi, tab_ref, base, shifts):
    for lvl, sh in enumerate(shifts):
        mr = tab_ref[0, base + 2 * lvl]
        mi = tab_ref[0, base + 2 * lvl + 1]
        ar = pltpu.roll(xr, sh, 0)
        ai = pltpu.roll(xi, sh, 0)
        xr, xi = xr + mr * ar - mi * ai, xi + mr * ai + mi * ar
    return xr, xi


def _to_segments(src_ref, dst_ref, seg):
    for i in range(seg):
        dst_ref[i * SUBLANES:(i + 1) * SUBLANES, :] = src_ref[pl.ds(i, SUBLANES, stride=seg), :]


def _from_segments(src_ref, dst_ref, seg):
    for i in range(seg):
        dst_ref[pl.ds(i, SUBLANES, stride=seg), :] = src_ref[i * SUBLANES:(i + 1) * SUBLANES, :]


def _slab(i):
    return pl.ds(pl.multiple_of(i * SUBLANES, SUBLANES), SUBLANES)


def _s5_scan_fwd(proj_main, bbd_re, bbd_im, cbd_re, cbd_im, dvec, tab, ptab, DS):
    L = proj_main.shape[0]
    nb = DS // S5_COLS
    tb = _blk(L, 512, SUBLANES)
    nt = L // tb
    seg = tb // SUBLANES

    def body(u_ref, bre_ref, bim_ref, cre_ref, cim_ref, d_ref, tab_ref, pt_ref, y_ref, sre_ref, sim_ref,
             up_ref, yp_ref, car_ref):
        t = pl.program_id(1)

        @pl.when(t == 0)
        def _():
            car_ref[...] = jnp.zeros_like(car_ref)

        _to_segments(u_ref, up_ref, seg)
        up = up_ref[...]
        sre_ref[...] = _dot(up, bre_ref[0])
        sim_ref[...] = _dot(up, bim_ref[0])
        ar, ai = tab_ref[0, 0], tab_ref[0, 1]

        def pass1(i, x):
            xr = ar * x[0] - ai * x[1] + sre_ref[_slab(i), :]
            xi = ar * x[1] + ai * x[0] + sim_ref[_slab(i), :]
            sre_ref[_slab(i), :] = xr
            sim_ref[_slab(i), :] = xi
            return xr, xi

        zero = jnp.zeros((SUBLANES, S5_LANES), F32)
        er, ei = lax.fori_loop(0, seg, pass1, (zero, zero))
        cin_r, cin_i = car_ref[0], car_ref[1]
        sr, si = _scan8(er, ei, tab_ref, 2, (1, 2, 4))
        pr, pi = tab_ref[0, 8], tab_ref[0, 9]
        sr, si = sr + pr * cin_r - pi * cin_i, si + pr * cin_i + pi * cin_r
        row0 = _iota2((SUBLANES, S5_LANES), 0) == 0
        cr = jnp.where(row0, cin_r, pltpu.roll(sr, 1, 0))
        ci = jnp.where(row0, cin_i, pltpu.roll(si, 1, 0))
        car_ref[0] = jnp.broadcast_to(sr[SUBLANES - 1:SUBLANES, :], sr.shape)
        car_ref[1] = jnp.broadcast_to(si[SUBLANES - 1:SUBLANES, :], si.shape)

        def pass2(i, _):
            qr, qi = pt_ref[0, 0, _slab(i), :], pt_ref[0, 1, _slab(i), :]
            sre_ref[_slab(i), :] += qr * cr - qi * ci
            sim_ref[_slab(i), :] += qr * ci + qi * cr
            return 0

        lax.fori_loop(0, seg, pass2, 0, unroll=4)
        yp_ref[...] = _dot(sre_ref[...], cre_ref[0], NT) - _dot(sim_ref[...], cim_ref[0], NT) + d_ref[...] * up
        _from_segments(yp_ref, y_ref, seg)

    return pl.pallas_call(
        body, name="s5_scan_fwd", grid=(nb, nt),
        in_specs=[
            pl.BlockSpec((tb, S5_COLS), lambda j, t: (t, j)),
            pl.BlockSpec((1, S5_COLS, S5_LANES), lambda j, t: (j, 0, 0)),
            pl.BlockSpec((1, S5_COLS, S5_LANES), lambda j, t: (j, 0, 0)),
            pl.BlockSpec((1, S5_COLS, S5_LANES), lambda j, t: (j, 0, 0)),
            pl.BlockSpec((1, S5_COLS, S5_LANES), lambda j, t: (j, 0, 0)),
            pl.BlockSpec((1, S5_COLS), lambda j, t: (0, j)),
            pl.BlockSpec((1, S5_TABS, SUBLANES, S5_LANES), lambda j, t: (j, 0, 0, 0)),
            pl.BlockSpec((1, 2, tb, S5_LANES), lambda j, t: (j, 0, 0, 0)),
        ],
        out_specs=[
            pl.BlockSpec((tb, S5_COLS), lambda j, t: (t, j)),
            pl.BlockSpec((tb, S5_LANES), lambda j, t: (t, j)),
            pl.BlockSpec((tb, S5_LANES), lambda j, t: (t, j)),
        ],
        out_shape=[jax.ShapeDtypeStruct((L, DS), F32),
                   jax.ShapeDtypeStruct((L, nb * S5_LANES), F32),
                   jax.ShapeDtypeStruct((L, nb * S5_LANES), F32)],
        scratch_shapes=[pltpu.VMEM((tb, S5_COLS), F32), pltpu.VMEM((tb, S5_COLS), F32),
                        pltpu.VMEM((2, SUBLANES, S5_LANES), F32)],
        compiler_params=pltpu.CompilerParams(dimension_semantics=("parallel", "arbitrary")),
    )(proj_main, bbd_re, bbd_im, cbd_re, cbd_im, dvec, tab, ptab)


def _s5_scan_bwd(dy, proj_main, s_re, s_im, bbd_re, bbd_im, cbd_re, cbd_im, dvec, tab, ptab, d_s5, DS):
    L = proj_main.shape[0]
    nb = DS // S5_COLS
    tb = _blk(L, 512, SUBLANES)
    nt = L // tb
    seg = tb // SUBLANES
    tb8 = tb // SUBLANES

    def body(dy_ref, u_ref, sre_ref, sim_ref, pre_ref, pim_ref, bre_ref, bim_ref, cre_ref, cim_ref, d_ref, tab_ref, pt_ref,
             _ds5_ref, du_ref, gd_ref, gcre_ref, gcim_ref, gbre_ref, gbim_ref, gare_ref, gaim_ref,
             lre_ref, lim_ref, up_ref, dyp_ref, dup_ref, duo_ref, car_ref):
        t = pl.program_id(1)

        @pl.when(t == 0)
        def _():
            car_ref[...] = jnp.zeros_like(car_ref)
            gd_ref[...] = jnp.zeros_like(gd_ref)
            gcre_ref[...] = jnp.zeros_like(gcre_ref)
            gcim_ref[...] = jnp.zeros_like(gcim_ref)
            gbre_ref[...] = jnp.zeros_like(gbre_ref)
            gbim_ref[...] = jnp.zeros_like(gbim_ref)
            gare_ref[...] = jnp.zeros_like(gare_ref)
            gaim_ref[...] = jnp.zeros_like(gaim_ref)

        _to_segments(dy_ref, dyp_ref, seg)
        _to_segments(u_ref, up_ref, seg)
        dyv = dyp_ref[...]
        u = up_ref[...]
        gd_ref[...] += jnp.sum(dyv * u, axis=0, keepdims=True)
        lre_ref[...] = _dot(dyv, cre_ref[0])
        lim_ref[...] = -_dot(dyv, cim_ref[0])
        gcre_ref[0] += _dot(dyv, sre_ref[...], TN)
        gcim_ref[0] -= _dot(dyv, sim_ref[...], TN)
        ar, ai = tab_ref[0, 0], -tab_ref[0, 1]

        def pass1(k, x):
            i = seg - 1 - k
            xr = ar * x[0] - ai * x[1] + lre_ref[_slab(i), :]
            xi = ar * x[1] + ai * x[0] + lim_ref[_slab(i), :]
            lre_ref[_slab(i), :] = xr
            lim_ref[_slab(i), :] = xi
            return xr, xi

        zero = jnp.zeros((SUBLANES, S5_LANES), F32)
        er, ei = lax.fori_loop(0, seg, pass1, (zero, zero))
        cin_r, cin_i = car_ref[0], car_ref[1]
        lr, li = _scan8(er, ei, tab_ref, 10, (7, 6, 4))
        pr, pi = tab_ref[0, 16], tab_ref[0, 17]
        lr, li = lr + pr * cin_r - pi * cin_i, li + pr * cin_i + pi * cin_r
        rows = _iota2((SUBLANES, S5_LANES), 0)
        cr = jnp.where(rows == SUBLANES - 1, cin_r, pltpu.roll(lr, SUBLANES - 1, 0))
        ci = jnp.where(rows == SUBLANES - 1, cin_i, pltpu.roll(li, SUBLANES - 1, 0))
        car_ref[0] = jnp.broadcast_to(lr[0:1, :], lr.shape)
        car_ref[1] = jnp.broadcast_to(li[0:1, :], li.shape)

        first = (t == nt - 1).astype(F32)
        head_re = jnp.broadcast_to(pre_ref[SUBLANES - 1:SUBLANES, :], zero.shape) * (1.0 - first)
        head_im = jnp.broadcast_to(pim_ref[SUBLANES - 1:SUBLANES, :], zero.shape) * (1.0 - first)
        last = _slab(seg - 1)
        sp0_re = jnp.where(rows == 0, head_re, pltpu.roll(sre_ref[last, :], 1, 0))
        sp0_im = jnp.where(rows == 0, head_im, pltpu.roll(sim_ref[last, :], 1, 0))

        def pass2(i, acc):
            j = seg - 1 - i
            qr, qi = pt_ref[0, 0, _slab(j), :], -pt_ref[0, 1, _slab(j), :]
            xr = lre_ref[_slab(i), :] + qr * cr - qi * ci
            xi = lim_ref[_slab(i), :] + qr * ci + qi * cr
            lre_ref[_slab(i), :] = xr
            lim_ref[_slab(i), :] = xi
            prev = _slab(jnp.maximum(i - 1, 0))
            sp_re = jnp.where(i == 0, sp0_re, sre_ref[prev, :])
            sp_im = jnp.where(i == 0, sp0_im, sim_ref[prev, :])
            return acc[0] + sp_re * xr + sp_im * xi, acc[1] + sp_re * xi - sp_im * xr

        acc_re, acc_im = lax.fori_loop(0, seg, pass2, (zero, zero), unroll=2)
        gare_ref[...] += jnp.sum(acc_re, axis=0, keepdims=True)
        gaim_ref[...] += jnp.sum(acc_im, axis=0, keepdims=True)
        lre = lre_ref[...]
        lim = lim_ref[...]
        dup_ref[...] = dyv * d_ref[...] + _dot(lre, bre_ref[0], NT) + _dot(lim, bim_ref[0], NT)
        _from_segments(dup_ref, duo_ref, seg)
        du_ref[...] = duo_ref[...].astype(BF16)
        gbre_ref[0] += _dot(u, lre, TN)
        gbim_ref[0] += _dot(u, lim, TN)

    rt = lambda t: nt - 1 - t
    col = pl.BlockSpec((tb, S5_COLS), lambda j, t: (rt(t), j))
    st = pl.BlockSpec((tb, S5_LANES), lambda j, t: (rt(t), j))
    prev = pl.BlockSpec((SUBLANES, S5_LANES), lambda j, t: (jnp.maximum(rt(t) * tb8 - 1, 0), j))
    bmat = pl.BlockSpec((1, S5_COLS, S5_LANES), lambda j, t: (j, 0, 0))
    cmat = bmat
    return pl.pallas_call(
        body, name="s5_scan_bwd", grid=(nb, nt),
        in_specs=[col, col, st, st, prev, prev, bmat, bmat, cmat, cmat,
                  pl.BlockSpec((1, S5_COLS), lambda j, t: (0, j)),
                  pl.BlockSpec((1, S5_TABS, SUBLANES, S5_LANES), lambda j, t: (j, 0, 0, 0)),
                  pl.BlockSpec((1, 2, tb, S5_LANES), lambda j, t: (j, 0, 0, 0)),
                  pl.BlockSpec(memory_space=pl.ANY)],
        out_specs=[col, pl.BlockSpec((1, S5_COLS), lambda j, t: (0, j)), cmat, cmat, bmat, bmat,
                   pl.BlockSpec((1, S5_LANES), lambda j, t: (0, j)), pl.BlockSpec((1, S5_LANES), lambda j, t: (0, j))],
        input_output_aliases={13: 0},
        out_shape=[jax.ShapeDtypeStruct((L, 2 * DS), BF16), jax.ShapeDtypeStruct((1, DS), F32),
                   jax.ShapeDtypeStruct((nb, S5_COLS, S5_LANES), F32), jax.ShapeDtypeStruct((nb, S5_COLS, S5_LANES), F32),
                   jax.ShapeDtypeStruct((nb, S5_COLS, S5_LANES), F32), jax.ShapeDtypeStruct((nb, S5_COLS, S5_LANES), F32),
                   jax.ShapeDtypeStruct((1, nb * S5_LANES), F32), jax.ShapeDtypeStruct((1, nb * S5_LANES), F32)],
        scratch_shapes=[pltpu.VMEM((tb, S5_LANES), F32), pltpu.VMEM((tb, S5_LANES), F32)]
        + [pltpu.VMEM((tb, S5_COLS), F32)] * 4 + [pltpu.VMEM((2, SUBLANES, S5_LANES), F32)],
        compiler_params=pltpu.CompilerParams(dimension_semantics=("parallel", "arbitrary")),
    )(dy, proj_main, s_re, s_im, s_re, s_im, bbd_re, bbd_im, cbd_re, cbd_im, dvec, tab, ptab, d_s5)


def _s5_post_fwd(y_pre, proj_main, glu_w, glu_b, DS):
    L = y_pre.shape[0]
    tr = _blk(L, 256, SUBLANES)

    def body(y_ref, z_ref, w_ref, b_ref, o_ref, t_ref):
        y1 = _gelu(y_ref[...])
        t = _dot(y1, w_ref[...]) + b_ref[...]
        t_ref[...] = t
        z = z_ref[...]
        o_ref[...] = (y1 * _sigmoid(t) * (z * _sigmoid(z))).astype(BF16)

    row = pl.BlockSpec((tr, DS), lambda i: (i, 0))
    return pl.pallas_call(
        body, name="s5_post_fwd", grid=(L // tr,),
        in_specs=[row, pl.BlockSpec((tr, DS), lambda i: (i, 1)), pl.BlockSpec((DS, DS), lambda i: (0, 0)),
                  pl.BlockSpec((1, DS), lambda i: (0, 0))],
        out_specs=[row, row],
        out_shape=[jax.ShapeDtypeStruct((L, 2 * DS), BF16), jax.ShapeDtypeStruct((L, DS), F32)],
        compiler_params=pltpu.CompilerParams(dimension_semantics=("parallel",)),
    )(y_pre, proj_main, glu_w, glu_b)


def _s5_post_bwd(d_ycat, y_pre, proj_main, t_pre, glu_w, DS):
    L = y_pre.shape[0]
    tr = _blk(L, 256, SUBLANES)

    def body(dy_ref, y_ref, z_ref, t_ref, w_ref, dyp_ref, dz_ref, dt_ref, y1_ref, gb_ref):
        i = pl.program_id(0)

        @pl.when(i == 0)
        def _():
            gb_ref[...] = jnp.zeros_like(gb_ref)

        dy = dy_ref[...]
        yp = y_ref[...]
        z = z_ref[...]
        y1 = _gelu(yp)
        sg = _sigmoid(t_ref[...])
        sz = _sigmoid(z)
        c = y1 * sg
        d_c = dy * (z * sz)
        dz_ref[...] = (dy * c * (sz * (1.0 + z * (1.0 - sz)))).astype(BF16)
        d_t = d_c * y1 * sg * (1.0 - sg)
        gb_ref[...] += jnp.sum(d_t, axis=0, keepdims=True)
        dt_ref[...] = d_t.astype(BF16)
        y1_ref[...] = y1.astype(BF16)
        d_y1 = d_c * sg + _dot(d_t, w_ref[...], NT)
        dyp_ref[...] = d_y1 * _gelu_grad(yp)

    row = pl.BlockSpec((tr, DS), lambda i: (i, 0))
    return pl.pallas_call(
        body, name="s5_post_bwd", grid=(L // tr,),
        in_specs=[row, row, pl.BlockSpec((tr, DS), lambda i: (i, 1)), row, pl.BlockSpec((DS, DS), lambda i: (0, 0))],
        out_specs=[row, pl.BlockSpec((tr, DS), lambda i: (i, 1)), row, row, pl.BlockSpec((1, DS), lambda i: (0, 0))],
        out_shape=[jax.ShapeDtypeStruct((L, DS), F32), jax.ShapeDtypeStruct((L, 2 * DS), BF16),
                   jax.ShapeDtypeStruct((L, DS), BF16), jax.ShapeDtypeStruct((L, DS), BF16),
                   jax.ShapeDtypeStruct((1, DS), F32)],
        compiler_params=pltpu.CompilerParams(dimension_semantics=("arbitrary",)),
    )(d_ycat, y_pre, proj_main, t_pre, glu_w)


def _gla_gates(glow, gu_ref, gb_ref):
    a = _dot(glow, gu_ref[...]) + gb_ref[...]
    lg = (jnp.minimum(a, 0.0) - jnp.log(1.0 + jnp.exp(-jnp.abs(a)))) * (1.0 / GLA_TAU)
    ri = lax.broadcasted_iota(jnp.int32, (GLA_CHUNK, GLA_CHUNK), 0)
    ci = lax.broadcasted_iota(jnp.int32, (GLA_CHUNK, GLA_CHUNK), 1)
    b = _dot_hi((ri >= ci).astype(F32), lg)
    b_last = jnp.sum(lg, axis=0, keepdims=True)
    return a, b, b_last, ri >= ci


def _gla_specs(DS, DK, DV, c, cmap):
    return [
        pl.BlockSpec((c, DK), lambda n: (cmap(n), 2 * DS // DK)),
        pl.BlockSpec((c, DK), lambda n: (cmap(n), 2 * DS // DK + 1)),
        pl.BlockSpec((c, DV), lambda n: (cmap(n), (2 * DS + 2 * DK) // DV)),
        pl.BlockSpec((c, DV), lambda n: (cmap(n), (2 * DS + 2 * DK) // DV + 1)),
    ]


def _gla_fwd(proj_main, proj_low, gate_up_pad, gate_bias, norm_w, ycat, DS, DK, DV):
    L = proj_main.shape[0]
    nc = L // GLA_CHUNK
    cps = math.gcd(GLA_STEP_CHUNKS, nc)
    nh = DK // GLA_HK
    scale = GLA_HK ** -0.5

    def body(q_ref, k_ref, v_ref, z_ref, gl_ref, gu_ref, gb_ref, nw_ref, _yc_ref, y_ref, sp_ref, st_ref):
        n = pl.program_id(0)

        @pl.when(n == 0)
        def _():
            st_ref[...] = jnp.zeros_like(st_ref)

        pairs = [(sc, h) for sc in range(cps) for h in range(nh)]
        rows = lambda sc: slice(sc * GLA_CHUNK, (sc + 1) * GLA_CHUNK)
        kcol = lambda h: slice(h * GLA_HK, (h + 1) * GLA_HK)
        vcol = lambda h: slice(h * GLA_HV, (h + 1) * GLA_HV)
        gates = [_gla_gates(gl_ref[rows(sc), :], gu_ref, gb_ref) for sc in range(cps)]
        qe, dec, o_in, kv = {}, {}, {}, {}
        for sc, h in pairs:
            _, b, b_last, mask = gates[sc]
            bh, bl = b[:, kcol(h)], b_last[:, kcol(h)]
            qe[sc, h] = (q_ref[rows(sc), kcol(h)] * scale) * jnp.exp(bh)
            kh = k_ref[rows(sc), kcol(h)]
            vh = v_ref[rows(sc), vcol(h)]
            attn = jnp.where(mask, _dot(qe[sc, h], kh * jnp.exp(-bh), NT), 0.0)
            o_in[sc, h] = _dot(attn, vh)
            kv[sc, h] = _dot(vh, kh * jnp.exp(bl - bh), TN)
            dec[sc, h] = jnp.exp(bl)
        for sc, h in pairs:
            st = st_ref[h]
            sp_ref[sc, h] = st
            o = o_in[sc, h] + _dot(qe[sc, h], st, NT)
            st_ref[h] = dec[sc, h] * st + kv[sc, h]
            r = lax.rsqrt(jnp.mean(o * o, axis=-1, keepdims=True) + EPS)
            z = z_ref[rows(sc), vcol(h)]
            y_ref[rows(sc), vcol(h)] = (o * r * nw_ref[...] * (z * _sigmoid(z))).astype(BF16)

    c = cps * GLA_CHUNK
    return pl.pallas_call(
        body, name="gla_fwd", grid=(nc // cps,),
        in_specs=_gla_specs(DS, DK, DV, c, lambda n: n) + [
            pl.BlockSpec((c, LANES), lambda n: (n, 0)),
            pl.BlockSpec((LANES, DK), lambda n: (0, 0)),
            pl.BlockSpec((1, DK), lambda n: (0, 0)),
            pl.BlockSpec((1, GLA_HV), lambda n: (0, 0)),
            pl.BlockSpec(memory_space=pl.ANY),
        ],
        out_specs=[pl.BlockSpec((c, DV), lambda n: (n, DS // DV)),
                   pl.BlockSpec((cps, nh, GLA_HV, GLA_HK), lambda n: (n, 0, 0, 0))],
        input_output_aliases={8: 0},
        out_shape=[jax.ShapeDtypeStruct(ycat.shape, BF16), jax.ShapeDtypeStruct((nc, nh, GLA_HV, GLA_HK), F32)],
        scratch_shapes=[pltpu.VMEM((nh, GLA_HV, GLA_HK), F32)],
        compiler_params=pltpu.CompilerParams(dimension_semantics=("arbitrary",)),
    )(proj_main, proj_main, proj_main, proj_main, proj_low, gate_up_pad, gate_bias, norm_w, ycat)


def _gla_bwd(d_ycat, proj_main, proj_low, s_prev, gate_up_pad, gate_bias, norm_w, DS, DK, DV):
    L = proj_main.shape[0]
    nc = L // GLA_CHUNK
    cps = math.gcd(GLA_STEP_CHUNKS, nc)
    nh = DK // GLA_HK
    scale = GLA_HK ** -0.5

    def body(dy_ref, q_ref, k_ref, v_ref, z_ref, gl_ref, sp_ref, gu_ref, gb_ref, nw_ref,
             dg_ref, da_ref, gnw_ref, ggb_ref, dst_ref):
        n = pl.program_id(0)

        @pl.when(n == 0)
        def _():
            dst_ref[...] = jnp.zeros_like(dst_ref)
            gnw_ref[...] = jnp.zeros_like(gnw_ref)
            ggb_ref[...] = jnp.zeros_like(ggb_ref)

        last_row = lax.broadcasted_iota(jnp.int32, (GLA_CHUNK, GLA_HK), 0) == GLA_CHUNK - 1
        ri = lax.broadcasted_iota(jnp.int32, (GLA_CHUNK, GLA_CHUNK), 0)
        ci = lax.broadcasted_iota(jnp.int32, (GLA_CHUNK, GLA_CHUNK), 1)
        upper = (ci >= ri).astype(F32)
        nw = nw_ref[...]
        for sc in reversed(range(cps)):
            rs = slice(sc * GLA_CHUNK, (sc + 1) * GLA_CHUNK)
            a, b, b_last, mask = _gla_gates(gl_ref[rs, :], gu_ref, gb_ref)
            for h in range(nh):
                ks = slice(h * GLA_HK, (h + 1) * GLA_HK)
                vs = slice(h * GLA_HV, (h + 1) * GLA_HV)
                bh, bl = b[:, ks], b_last[:, ks]
                e = jnp.exp(bh)
                einv = jnp.exp(-bh)
                etail = jnp.exp(bl - bh)
                dec = jnp.exp(bl)
                qe = (q_ref[rs, ks] * scale) * e
                kh = k_ref[rs, ks]
                ke = kh * einv
                ktail = kh * etail
                vh = v_ref[rs, vs]
                st = sp_ref[sc, h]
                dst = dst_ref[h]
                attn = jnp.where(mask, _dot(qe, ke, NT), 0.0)
                o = _dot(attn, vh) + _dot(qe, st, NT)
                r = lax.rsqrt(jnp.mean(o * o, axis=-1, keepdims=True) + EPS)
                nrm = o * r
                z = z_ref[rs, vs]
                sz = _sigmoid(z)
                dy = dy_ref[rs, vs]
                dg_ref[rs, 2 * DK + DV + h * GLA_HV:2 * DK + DV + (h + 1) * GLA_HV] = (
                    dy * nrm * nw * (sz * (1.0 + z * (1.0 - sz)))).astype(BF16)
                d_on = dy * (z * sz)
                gnw_ref[...] += jnp.sum(d_on * nrm, axis=0, keepdims=True)
                d_n = d_on * nw
                d_o = r * (d_n - nrm * jnp.mean(d_n * nrm, axis=-1, keepdims=True))
                d_attn = jnp.where(mask, _dot(d_o, vh, NT), 0.0)
                dg_ref[rs, 2 * DK + h * GLA_HV:2 * DK + (h + 1) * GLA_HV] = (
                    _dot(attn, d_o, TN) + _dot(ktail, dst, NT)).astype(BF16)
                d_qe = _dot(d_attn, ke) + _dot(d_o, st)
                d_ke = _dot(d_attn, qe, TN)
                d_kt = _dot(vh, dst)
                d_dec = jnp.sum(dst * st, axis=0, keepdims=True)
                dst_ref[h] = dec * dst + _dot(d_o, qe, TN)
                dg_ref[rs, ks] = (d_qe * scale * e).astype(BF16)
                dg_ref[rs, DK + h * GLA_HK:DK + (h + 1) * GLA_HK] = (d_ke * einv + d_kt * etail).astype(BF16)
                d_bl = jnp.sum(d_kt * ktail, axis=0, keepdims=True) + d_dec * dec
                d_b = d_qe * qe - d_ke * ke - d_kt * ktail + jnp.where(last_row, d_bl, 0.0)
                d_lg = _dot_hi(upper, d_b)
                d_a = d_lg * (1.0 / GLA_TAU) * _sigmoid(-a[:, ks])
                ggb_ref[:, ks] += jnp.sum(d_a, axis=0, keepdims=True)
                da_ref[rs, ks] = d_a.astype(BF16)

    c = cps * GLA_CHUNK
    ns = nc // cps
    rn = lambda n: ns - 1 - n
    return pl.pallas_call(
        body, name="gla_bwd", grid=(ns,),
        in_specs=[pl.BlockSpec((c, DV), lambda n: (rn(n), DS // DV))] + _gla_specs(DS, DK, DV, c, rn) + [
            pl.BlockSpec((c, LANES), lambda n: (rn(n), 0)),
            pl.BlockSpec((cps, nh, GLA_HV, GLA_HK), lambda n: (rn(n), 0, 0, 0)),
            pl.BlockSpec((LANES, DK), lambda n: (0, 0)),
            pl.BlockSpec((1, DK), lambda n: (0, 0)),
            pl.BlockSpec((1, GLA_HV), lambda n: (0, 0)),
        ],
        out_specs=[pl.BlockSpec((c, 2 * DK + 2 * DV), lambda n: (rn(n), 0)),
                   pl.BlockSpec((c, DK), lambda n: (rn(n), 0)),
                   pl.BlockSpec((1, GLA_HV), lambda n: (0, 0)), pl.BlockSpec((1, DK), lambda n: (0, 0))],
        out_shape=[jax.ShapeDtypeStruct((L, 2 * DK + 2 * DV), BF16),
                   jax.ShapeDtypeStruct((L, DK), BF16),
                   jax.ShapeDtypeStruct((1, GLA_HV), F32), jax.ShapeDtypeStruct((1, DK), F32)],
        scratch_shapes=[pltpu.VMEM((nh, GLA_HV, GLA_HK), F32)],
        compiler_params=pltpu.CompilerParams(dimension_semantics=("arbitrary",)),
    )(d_ycat, proj_main, proj_main, proj_main, proj_main, proj_low, s_prev, gate_up_pad, gate_bias, norm_w)


def _adamw_math(w, g, m, v):
    c1 = 1.0 - ADAM_B1 ** ADAM_STEP
    c2 = 1.0 - ADAM_B2 ** ADAM_STEP
    m_ = ADAM_B1 * m + (1.0 - ADAM_B1) * g
    v_ = ADAM_B2 * v + (1.0 - ADAM_B2) * (g * g)
    return -ADAM_LR * ((m_ / c1) / (jnp.sqrt(v_ / c2) + ADAM_EPS) + ADAM_WD * w), m_, v_


def _adamw_small(g_row, g_a, g_bc, ws, ms, vs):
    n = len(ws)
    nvec = n - 6

    def body(*refs):
        grow_ref, ga_ref, gbc_ref = refs[:3]
        w_refs, m_refs, v_refs = refs[3:3 + n], refs[3 + n:3 + 2 * n], refs[3 + 2 * n:3 + 3 * n]
        outs = refs[3 + 3 * n:]
        off = 0
        for i in range(n):
            if i < nvec:
                width = ws[i].shape[1]
                g = grow_ref[:, off:off + width]
                off += width
            elif i < nvec + 2:
                g = ga_ref[i - nvec]
            else:
                g = gbc_ref[i - nvec - 2]
            d, m_, v_ = _adamw_math(w_refs[i][...], g, m_refs[i][...], v_refs[i][...])
            outs[i][...] = g
            outs[n + i][...] = d
            outs[2 * n + i][...] = m_
            outs[3 * n + i][...] = v_

    vm = pl.BlockSpec(memory_space=pltpu.VMEM)
    outs = pl.pallas_call(
        body, name="adamw_small",
        in_specs=[vm] * (3 + 3 * n), out_specs=[vm] * (4 * n),
        out_shape=[jax.ShapeDtypeStruct(w.shape, F32) for w in ws] * 4,
    )(g_row, g_a, g_bc, *ws, *ms, *vs)
    return [outs[k * n:(k + 1) * n] for k in range(4)]


def _my_pos():
    return lax.axis_index("x"), lax.axis_index("y"), lax.axis_index("c")


def _late_gather_copies(srcs, lands, send_sems, recv_sems):
    x, y, c = _my_pos()
    me = 2 * x + y
    copies = []
    for d in (1, 2, 3):
        to = (x ^ (d >> 1), y ^ (d & 1), c)
        for a in range(len(srcs)):
            hrows = srcs[a].shape[0] // 2
            rows = pl.ds(c * hrows, hrows)
            copies.append(pltpu.make_async_remote_copy(
                src_ref=srcs[a].at[rows, :], dst_ref=lands[a].at[me, rows, :], send_sem=send_sems.at[3 * a + d - 1],
                recv_sem=recv_sems.at[3 * a + d - 1], device_id=to, device_id_type=MESH))
    return copies


def _late_gather_start(shards, after, name):
    n = len(shards)

    def body(*refs):
        srcs, lands = refs[:n], refs[n:2 * n]
        send_sems, recv_sems = refs[2 * n + 1], refs[2 * n + 2]
        token = refs[-1]
        for cp in _late_gather_copies(srcs, lands, send_sems, recv_sems):
            cp.start()
        token[...] = jnp.zeros_like(token)

    hbm = pl.BlockSpec(memory_space=pltpu.HBM)
    sem = pl.BlockSpec(memory_space=pltpu.SEMAPHORE)
    outs = pl.pallas_call(
        body, name=name,
        in_specs=[hbm] * (2 * n) + [pl.BlockSpec(memory_space=pl.ANY)],
        out_specs=[sem, sem] + [hbm] * (2 * n) + [pl.BlockSpec(memory_space=pltpu.VMEM)],
        out_shape=[pltpu.SemaphoreType.DMA((3 * n,)), pltpu.SemaphoreType.DMA((3 * n,))]
        + [pltpu.HBM(s.shape, s.dtype) for s in shards]
        + [pltpu.HBM((4,) + s.shape, s.dtype) for s in shards]
        + [jax.ShapeDtypeStruct((SUBLANES, LANES), F32)],
        input_output_aliases={i: 2 + i for i in range(2 * n)},
        compiler_params=pltpu.CompilerParams(has_side_effects=pltpu.SideEffectType.DATAFLOW_SIDE_EFFECTING),
    )(*[pltpu.with_memory_space_constraint(s, pltpu.HBM) for s in shards],
      *[pltpu.with_memory_space_constraint(lax.empty((4,) + s.shape, s.dtype), pltpu.HBM) for s in shards], after)
    return outs[0], outs[1], outs[2:2 + n], outs[2 + n:2 + 2 * n], outs[-1]


def _late_gather_wait(send_sems, recv_sems, shards, lands, after, name):
    n = len(shards)

    def body(*refs):
        src_refs, land_refs = refs[:n], refs[n:2 * n]
        ssem, rsem = refs[2 * n], refs[2 * n + 1]
        for cp in _late_gather_copies(src_refs, land_refs, ssem, rsem):
            cp.wait_send()
            cp.wait_recv()

    hbm = pl.BlockSpec(memory_space=pltpu.HBM)
    sem = pl.BlockSpec(memory_space=pltpu.SEMAPHORE)
    outs = pl.pallas_call(
        body, name=name,
        in_specs=[hbm] * (2 * n) + [sem, sem, pl.BlockSpec(memory_space=pl.ANY)],
        out_specs=[hbm] * (2 * n),
        out_shape=[pltpu.HBM(s.shape, s.dtype) for s in shards] + [pltpu.HBM(p.shape, p.dtype) for p in lands],
        input_output_aliases={i: i for i in range(2 * n)},
        compiler_params=pltpu.CompilerParams(has_side_effects=pltpu.SideEffectType.DATAFLOW_SIDE_EFFECTING),
    )(*shards, *lands, send_sems, recv_sems, after)
    return outs[n:]


def _late_gather_pair(lands, name):
    n = len(lands)

    def body(*refs):
        outs = refs[n:2 * n]
        send_sems, recv_sems = refs[2 * n:]
        x, y, c = _my_pos()

        def copy(a, d, half):
            chip = 2 * (x ^ (d >> 1)) + (y ^ (d & 1))
            hrows = lands[a].shape[1] // 2
            sl = outs[a].at[chip, pl.ds(half * hrows, hrows), :]
            return pltpu.make_async_remote_copy(src_ref=sl, dst_ref=sl, send_sem=send_sems.at[3 * a + d - 1],
                                                recv_sem=recv_sems.at[3 * a + d - 1], device_id=(x, y, 1 - c),
                                                device_id_type=MESH)

        pairs = [(a, d) for d in (1, 2, 3) for a in range(n)]
        for a, d in pairs:
            copy(a, d, c).start()
        for a, d in pairs:
            copy(a, d, c).wait_send()
            copy(a, d, 1 - c).wait_recv()

    hbm = pl.BlockSpec(memory_space=pltpu.HBM)
    return pl.pallas_call(
        body, name=name, in_specs=[hbm] * n, out_specs=[hbm] * n,
        out_shape=[jax.ShapeDtypeStruct(p.shape, p.dtype) for p in lands],
        input_output_aliases={i: i for i in range(n)},
        scratch_shapes=[pltpu.SemaphoreType.DMA((3 * n,)), pltpu.SemaphoreType.DMA((3 * n,))],
    )(*lands)


def _pair_exchange(gs):
    n = len(gs)

    def body(*refs):
        ins, outs = refs[:n], refs[n:2 * n]
        send_sems, recv_sems = refs[2 * n:]
        x, y, c = _my_pos()
        sent = []
        for a in range(n):
            hrows = gs[a].shape[1] // 2
            cp = pltpu.make_async_remote_copy(
                src_ref=ins[a].at[:, pl.ds((1 - c) * hrows, hrows), :], dst_ref=outs[a], send_sem=send_sems.at[a],
                recv_sem=recv_sems.at[a], device_id=(x, y, 1 - c), device_id_type=MESH)
            cp.start()
            sent.append(cp)
        for cp in sent:
            cp.wait()

    hbm = pl.BlockSpec(memory_space=pltpu.HBM)
    return pl.pallas_call(
        body, name="grad_pair_exchange", in_specs=[hbm] * n, out_specs=[hbm] * n,
        out_shape=[jax.ShapeDtypeStruct((g.shape[0], g.shape[1] // 2, g.shape[2]), g.dtype) for g in gs],
        scratch_shapes=[pltpu.SemaphoreType.DMA((n,)), pltpu.SemaphoreType.DMA((n,))],
    )(*gs)


def _pair_add(g, got, c_arr, name):
    nk, rows2, cols = g.shape
    hrows = rows2 // 2
    tr = _blk(hrows, 256, 2 * SUBLANES)
    nb = hrows // tr

    def body(c_ref, a_ref, b_ref, o_ref):
        o_ref[...] = (a_ref[...].astype(F32) + b_ref[...].astype(F32)).astype(o_ref.dtype)

    return pl.pallas_call(
        body, name=name,
        grid_spec=pltpu.PrefetchScalarGridSpec(
            num_scalar_prefetch=1, grid=(nk, nb),
            in_specs=[pl.BlockSpec((1, tr, cols), lambda k, i, c_ref: (k, c_ref[0] * nb + i, 0)),
                      pl.BlockSpec((1, tr, cols), lambda k, i, c_ref: (k, i, 0))],
            out_specs=pl.BlockSpec((1, tr, cols), lambda k, i, c_ref: (k, i, 0))),
        out_shape=jax.ShapeDtypeStruct((nk, hrows, cols), g.dtype),
        compiler_params=pltpu.CompilerParams(dimension_semantics=("parallel", "parallel")),
    )(c_arr, g, got)


def _chip_scatter_copies(srcs, lands, send_sems, recv_sems):
    x, y, c = _my_pos()
    copies = []
    for d in (1, 2, 3):
        tx, ty = x ^ (d >> 1), y ^ (d & 1)
        for a in range(len(srcs)):
            copies.append(pltpu.make_async_remote_copy(
                src_ref=srcs[a].at[2 * tx + ty], dst_ref=lands[a].at[d - 1], send_sem=send_sems.at[3 * a + d - 1],
                recv_sem=recv_sems.at[3 * a + d - 1], device_id=(tx, ty, c), device_id_type=MESH))
    return copies


def _chip_scatter_start(pss):
    n = len(pss)

    def body(*refs):
        srcs, lands = refs[:n], refs[n:2 * n]
        send_sems, recv_sems = refs[2 * n], refs[2 * n + 1]
        token = refs[-1]
        for cp in _chip_scatter_copies(srcs, lands, send_sems, recv_sems):
            cp.start()
        token[...] = jnp.zeros_like(token)

    hbm = pl.BlockSpec(memory_space=pltpu.HBM)
    sem = pl.BlockSpec(memory_space=pltpu.SEMAPHORE)
    land_shapes = [(3,) + p.shape[1:] for p in pss]
    outs = pl.pallas_call(
        body, name="grad_chip_scatter_start",
        in_specs=[hbm] * (2 * n),
        out_specs=[sem, sem] + [hbm] * (2 * n) + [pl.BlockSpec(memory_space=pltpu.VMEM)],
        out_shape=[pltpu.SemaphoreType.DMA((3 * n,)), pltpu.SemaphoreType.DMA((3 * n,))]
        + [pltpu.HBM(p.shape, p.dtype) for p in pss]
        + [pltpu.HBM(s, p.dtype) for s, p in zip(land_shapes, pss)]
        + [jax.ShapeDtypeStruct((SUBLANES, LANES), F32)],
        input_output_aliases={i: 2 + i for i in range(2 * n)},
        compiler_params=pltpu.CompilerParams(has_side_effects=pltpu.SideEffectType.DATAFLOW_SIDE_EFFECTING),
    )(*[pltpu.with_memory_space_constraint(p, pltpu.HBM) for p in pss],
      *[pltpu.with_memory_space_constraint(lax.empty(s, p.dtype), pltpu.HBM) for s, p in zip(land_shapes, pss)])
    return outs[0], outs[1], outs[2:2 + n], outs[2 + n:2 + 2 * n], outs[-1]


def _chip_scatter_wait(send_sems, recv_sems, srcs, lands, after):
    n = len(srcs)

    def body(*refs):
        src_refs, land_refs = refs[:n], refs[n:2 * n]
        ssem, rsem = refs[2 * n], refs[2 * n + 1]
        for cp in _chip_scatter_copies(src_refs, land_refs, ssem, rsem):
            cp.wait_send()
            cp.wait_recv()

    hbm = pl.BlockSpec(memory_space=pltpu.HBM)
    sem = pl.BlockSpec(memory_space=pltpu.SEMAPHORE)
    outs = pl.pallas_call(
        body, name="grad_chip_scatter_wait",
        in_specs=[hbm] * (2 * n) + [sem, sem, pl.BlockSpec(memory_space=pl.ANY)],
        out_specs=[hbm] * (2 * n),
        out_shape=[pltpu.HBM(p.shape, p.dtype) for p in srcs] + [pltpu.HBM(p.shape, p.dtype) for p in lands],
        input_output_aliases={i: i for i in range(2 * n)},
        compiler_params=pltpu.CompilerParams(has_side_effects=pltpu.SideEffectType.DATAFLOW_SIDE_EFFECTING),
    )(*srcs, *lands, send_sems, recv_sems, after)
    return outs[:n], outs[n:]


def _chip_sum(ps, got, me_arr, name):
    _, hrows, cols = ps.shape
    tr = _blk(hrows, 256, 2 * SUBLANES)

    def body(me_ref, p_ref, g_ref, o_ref):
        acc = p_ref[0].astype(F32)
        for s in range(3):
            acc = acc + g_ref[s].astype(F32)
        o_ref[...] = acc

    return pl.pallas_call(
        body, name=name,
        grid_spec=pltpu.PrefetchScalarGridSpec(
            num_scalar_prefetch=1, grid=(hrows // tr,),
            in_specs=[pl.BlockSpec((1, tr, cols), lambda i, me_ref: (me_ref[0], i, 0)),
                      pl.BlockSpec((3, tr, cols), lambda i, me_ref: (0, i, 0))],
            out_specs=pl.BlockSpec((tr, cols), lambda i, me_ref: (i, 0))),
        out_shape=jax.ShapeDtypeStruct((hrows, cols), F32),
        compiler_params=pltpu.CompilerParams(dimension_semantics=("parallel",)),
    )(me_arr, ps, got)


def _pair_swap(halves):
    n = len(halves)

    def body(*refs):
        ins, outs = refs[:n], refs[n:2 * n]
        send_sems, recv_sems = refs[2 * n:]
        x, y, c = _my_pos()
        sent = []
        for a in range(n):
            cp = pltpu.make_async_remote_copy(src_ref=ins[a], dst_ref=outs[a], send_sem=send_sems.at[a], recv_sem=recv_sems.at[a],
                                              device_id=(x, y, 1 - c), device_id_type=MESH)
            cp.start()
            sent.append(cp)
        for cp in sent:
            cp.wait()

    hbm = pl.BlockSpec(memory_space=pltpu.HBM)
    return pl.pallas_call(
        body, name="grad_pair_swap", in_specs=[hbm] * n, out_specs=[hbm] * n,
        out_shape=[jax.ShapeDtypeStruct(h.shape, h.dtype) for h in halves],
        scratch_shapes=[pltpu.SemaphoreType.DMA((n,)), pltpu.SemaphoreType.DMA((n,))],
    )(*halves)


def _adamw_sharded(w, g_own, g_other, m, v, c_arr, name):
    R, C = w.shape
    hrows = R // 2
    tr = _blk(hrows, 256, SUBLANES)
    nbh = hrows // tr
    c1 = 1.0 - ADAM_B1 ** ADAM_STEP
    c2 = 1.0 - ADAM_B2 ** ADAM_STEP

    def body(c_ref, w_ref, go_ref, gx_ref, m_ref, v_ref, g_ref, d_ref, nm_ref, nv_ref):
        mine = (pl.program_id(0) // nbh) == c_ref[0]
        g_ = jnp.where(mine, go_ref[...], gx_ref[...])
        g_ref[...] = g_
        m_ = ADAM_B1 * m_ref[...] + (1.0 - ADAM_B1) * g_
        v_ = ADAM_B2 * v_ref[...] + (1.0 - ADAM_B2) * (g_ * g_)
        nm_ref[...] = m_
        nv_ref[...] = v_
        d_ref[...] = -ADAM_LR * ((m_ / c1) / (jnp.sqrt(v_ / c2) + ADAM_EPS) + ADAM_WD * w_ref[...])

    blk = pl.BlockSpec((tr, C), lambda i, c_ref: (i, 0))
    hblk = pl.BlockSpec((tr, C), lambda i, c_ref: (i % nbh, 0))
    sd = jax.ShapeDtypeStruct((R, C), F32)
    return pl.pallas_call(
        body, name=name,
        grid_spec=pltpu.PrefetchScalarGridSpec(
            num_scalar_prefetch=1, grid=(2 * nbh,),
            in_specs=[blk, hblk, hblk, blk, blk], out_specs=[blk] * 4),
        out_shape=[sd] * 4,
        compiler_params=pltpu.CompilerParams(dimension_semantics=("parallel",)),
    )(c_arr, w, g_own, g_other, m, v)


def _allreduce_small(arrs):
    n = len(arrs)
    rows = [a.shape[-2] // 8 for a in arrs]

    def piece(ref, a, p):
        start = p * rows[a]
        if rows[a] % SUBLANES == 0:
            start = pl.multiple_of(start, SUBLANES)
        return ref.at[..., pl.ds(start, rows[a]), :]

    def body(*refs):
        v_refs, o_refs, got_refs = refs[:n], refs[n:2 * n], refs[2 * n:3 * n]
        send_sems, recv_sems = refs[3 * n:]
        x, y, c = _my_pos()
        me = 4 * x + 2 * y + c

        def peer(d):
            return (x ^ (d >> 2), y ^ ((d >> 1) & 1), c ^ (d & 1))

        def lin(p):
            return 4 * p[0] + 2 * p[1] + p[2]

        sent = []
        for d in range(1, 8):
            to = peer(d)
            for a in range(n):
                cp = pltpu.make_async_remote_copy(
                    src_ref=piece(v_refs[a], a, lin(to)), dst_ref=got_refs[a].at[d],
                    send_sem=send_sems.at[0, d * n + a], recv_sem=recv_sems.at[0, d * n + a], device_id=to, device_id_type=MESH)
                cp.start()
                sent.append(cp)
        for a in range(n):
            acc = piece(v_refs[a], a, me)[...]
            for d in range(1, 8):
                sent[(d - 1) * n + a].wait_recv()
                acc = acc + got_refs[a][d]
            got_refs[a][0] = acc
            piece(o_refs[a], a, me)[...] = acc
        for d in range(1, 8):
            for a in range(n):
                cp = pltpu.make_async_remote_copy(
                    src_ref=got_refs[a].at[0], dst_ref=piece(o_refs[a], a, me),
                    send_sem=send_sems.at[1, d * n + a], recv_sem=recv_sems.at[1, d * n + a], device_id=peer(d), device_id_type=MESH)
                cp.start()
                sent.append(cp)
        for d in range(1, 8):
            for a in range(n):
                pltpu.make_async_remote_copy(
                    src_ref=got_refs[a].at[0], dst_ref=piece(o_refs[a], a, lin(peer(d))),
                    send_sem=send_sems.at[1, d * n + a], recv_sem=recv_sems.at[1, d * n + a], device_id=peer(d),
                    device_id_type=MESH).wait_recv()
        for cp in sent:
            cp.wait_send()

    vm = pl.BlockSpec(memory_space=pltpu.VMEM)
    return pl.pallas_call(
        body, name="allreduce_small", in_specs=[vm] * n, out_specs=[vm] * n,
        out_shape=[jax.ShapeDtypeStruct(a.shape, F32) for a in arrs],
        scratch_shapes=[pltpu.VMEM((8,) + a.shape[:-2] + (r, a.shape[-1]), F32) for a, r in zip(arrs, rows)]
        + [pltpu.SemaphoreType.DMA((2, 8 * n)), pltpu.SemaphoreType.DMA((2, 8 * n))],
    )(*arrs)


def kernel(x, pre_norm_w, w_in, s5_A_re, s5_A_im, s5_B_re, s5_B_im, s5_C_re, s5_C_im, s5_D, s5_log_dt, s5_glu_w, s5_glu_b, gla_gate_up, gla_gate_bias, gla_norm_w, w_out, post_norm_w, loss_target, m_pre_norm_w, m_w_in, m_s5_A_re, m_s5_A_im, m_s5_B_re, m_s5_B_im, m_s5_C_re, m_s5_C_im, m_s5_D, m_s5_log_dt, m_s5_glu_w, m_s5_glu_b, m_gla_gate_up, m_gla_gate_bias, m_gla_norm_w, m_w_out, m_post_norm_w, v_pre_norm_w, v_w_in, v_s5_A_re, v_s5_A_im, v_s5_B_re, v_s5_B_im, v_s5_C_re, v_s5_C_im, v_s5_D, v_s5_log_dt, v_s5_glu_w, v_s5_glu_b, v_gla_gate_up, v_gla_gate_bias, v_gla_norm_w, v_w_out, v_post_norm_w):
    names = ["pre_norm_w", "w_in", "s5_A_re", "s5_A_im", "s5_B_re", "s5_B_im", "s5_C_re", "s5_C_im", "s5_D", "s5_log_dt",
             "s5_glu_w", "s5_glu_b", "gla_gate_up", "gla_gate_bias", "gla_norm_w", "w_out", "post_norm_w"]
    W = dict(zip(names, (pre_norm_w, w_in, s5_A_re, s5_A_im, s5_B_re, s5_B_im, s5_C_re, s5_C_im, s5_D, s5_log_dt,
                         s5_glu_w, s5_glu_b, gla_gate_up, gla_gate_bias, gla_norm_w, w_out, post_norm_w)))
    M = dict(zip(names, (m_pre_norm_w, m_w_in, m_s5_A_re, m_s5_A_im, m_s5_B_re, m_s5_B_im, m_s5_C_re, m_s5_C_im, m_s5_D,
                         m_s5_log_dt, m_s5_glu_w, m_s5_glu_b, m_gla_gate_up, m_gla_gate_bias, m_gla_norm_w, m_w_out,
                         m_post_norm_w)))
    V = dict(zip(names, (v_pre_norm_w, v_w_in, v_s5_A_re, v_s5_A_im, v_s5_B_re, v_s5_B_im, v_s5_C_re, v_s5_C_im, v_s5_D,
                         v_s5_log_dt, v_s5_glu_w, v_s5_glu_b, v_gla_gate_up, v_gla_gate_bias, v_gla_norm_w, v_w_out,
                         v_post_norm_w)))
    sharded = ("w_in", "s5_glu_w", "w_out", "gla_gate_up")

    xb = x[0]
    tgt = loss_target[0]
    L, D = xb.shape
    DS = D // 2
    G = DS // S5_GROUP
    P = S5_STATE
    NB = DS // S5_COLS
    DV = D - DS
    DK = DV // 2
    WM = 2 * DS + 2 * DK + 2 * DV
    nsh = w_in.shape[2]

    chip = 2 * lax.axis_index("x") + lax.axis_index("y")
    own = [w_in[0].astype(BF16), s5_glu_w[0].astype(BF16), w_out[0].astype(BF16), gla_gate_up[0]]
    fill = lambda g, o: lax.dynamic_update_index_in_dim(g, o, chip, 0)
    win_ss, win_rs, win_src, win_lands, win_token = _late_gather_start(own[:1], pre_norm_w, "w_in_gather_start")
    h = _prenorm_fwd(xb, pre_norm_w, win_token)

    b_view = lambda t: jnp.transpose(t[0], (0, 2, 1)).reshape(G * S5_GROUP, P)
    b_back = lambda t: jnp.transpose(t.reshape(G, S5_GROUP, P), (0, 2, 1))[None]
    c_view = lambda t: t[0].reshape(G * S5_GROUP, P)
    c_back = lambda t: t.reshape(1, G, S5_GROUP, P)
    small = ["pre_norm_w", "post_norm_w", "s5_D", "s5_glu_b", "gla_gate_bias", "gla_norm_w", "s5_log_dt",
             "s5_A_re", "s5_A_im", "s5_B_re", "s5_B_im", "s5_C_re", "s5_C_im"]
    view = {n: (lambda t: t) for n in small[:7]}
    back = dict(view)
    view.update(s5_A_re=lambda t: t[0], s5_A_im=lambda t: t[0], s5_B_re=b_view, s5_B_im=b_view, s5_C_re=c_view, s5_C_im=c_view)
    back.update(s5_A_re=lambda t: t[None], s5_A_im=lambda t: t[None], s5_B_re=b_back, s5_B_im=b_back, s5_C_re=c_back,
                s5_C_im=c_back)
    Wv = {n: view[n](W[n]) for n in small}
    bbd_re, bbd_im, ct_re, ct_im, tab, ptab = _s5_prep_fwd(
        Wv["s5_A_re"], Wv["s5_A_im"], s5_log_dt, Wv["s5_B_re"], Wv["s5_B_im"], Wv["s5_C_re"], Wv["s5_C_im"],
        _blk(L, 512, SUBLANES) // SUBLANES)
    dvec = s5_D

    g_win = _late_gather_wait(win_ss, win_rs, win_src, win_lands, h, "w_in_gather_wait")
    g_win = fill(_late_gather_pair(g_win, "w_in_gather_pair")[0], own[0])
    w_full = jnp.moveaxis(g_win, 0, 1).reshape(D, 4 * nsh)
    w_main = w_full[:, :WM]
    w_low = jnp.pad(w_full[:, WM:], ((0, 0), (0, LANES - GLA_RANK)))
    late_ss, late_rs, late_src, late_lands, late_token = _late_gather_start(own[1:], g_win, "late_gather_start")
    proj_main, proj_low = _in_proj(h, w_main, w_low, late_token)
    y_pre, s_re, s_im = _s5_scan_fwd(proj_main, bbd_re, bbd_im, ct_re, ct_im, dvec, tab, ptab, DS)
    late = _late_gather_wait(late_ss, late_rs, late_src, late_lands, y_pre, "late_gather_wait")
    late = _late_gather_pair(late, "late_gather_pair")
    g_glu, g_wout, g_gup = [fill(g, o) for g, o in zip(late, own[1:])]
    glu_w = g_glu.reshape(DS, DS)
    wout = g_wout.reshape(D, D)
    gup = jnp.moveaxis(g_gup, 0, 1).reshape(GLA_RANK, DK)
    gup_pad = jnp.pad(gup, ((0, LANES - GLA_RANK), (0, 0))).astype(BF16)
    ycat, t_pre = _s5_post_fwd(y_pre, proj_main, glu_w, s5_glu_b, DS)
    ycat, s_prev = _gla_fwd(proj_main, proj_low, gup_pad, gla_gate_bias, gla_norm_w, ycat, DS, DK, DV)
    mixed = _mm(ycat, wout, name="out_proj")
    loss11, d_mixed, dout, g_post_w = _post_fwd_bwd(mixed, xb, tgt, post_norm_w)

    d_ycat = _mm(d_mixed, wout, tb=True, name="out_proj_dx")
    g_wout_full = _mm(ycat, d_mixed, ta=True, out_dtype=BF16, name="out_proj_dw")
    d_ypre, d_s5, d_t, y1, g_glu_b = _s5_post_bwd(d_ycat, y_pre, proj_main, t_pre, glu_w, DS)
    g_glu_full = _mm(y1, d_t, ta=True, out_dtype=BF16, name="glu_dw")
    d_s5, g_D, gct_re, gct_im, gbbd_re, gbbd_im, gab_re, gab_im = _s5_scan_bwd(
        d_ypre, proj_main, s_re, s_im, bbd_re, bbd_im, ct_re, ct_im, dvec, tab, ptab, d_s5, DS)
    d_gla, d_a, g_norm_w, g_gate_bias = _gla_bwd(
        d_ycat, proj_main, proj_low, s_prev, gup_pad, gla_gate_bias, gla_norm_w, DS, DK, DV)
    d_low = _mm(d_a, gup_pad, tb=True, out_dtype=BF16, name="gate_dx")
    g_gup_pad = _mm(proj_low, d_a, ta=True, name="gate_dw")
    g_wmain, g_wlow = _in_proj_dw(h, d_s5, d_gla, d_low)

    g_win_full = jnp.concatenate([g_wmain, g_wlow[:, :GLA_RANK]], axis=1)
    gs = [jnp.moveaxis(g_win_full.reshape(D, 4, nsh), 1, 0),
          g_glu_full.reshape(4, DS // 4, DS),
          g_wout_full.reshape(4, D // 4, D),
          jnp.moveaxis(g_gup_pad[:GLA_RANK].reshape(GLA_RANK, 4, DK // 4), 1, 0)]
    c_arr = lax.axis_index("c").astype(jnp.int32).reshape(1)
    me_arr = chip.astype(jnp.int32).reshape(1)
    got = _pair_exchange(gs)
    pss = [_pair_add(g, r, c_arr, "grad_pair_add_" + n) for n, g, r in zip(sharded, gs, got)]
    send_sems, recv_sems, pss, lands, token = _chip_scatter_start(pss)

    dh = _in_proj_dx(d_s5, d_gla, d_low, w_main, w_low, token)
    grad_x, g_pre_w = _prenorm_bwd(xb, dh, dout, pre_norm_w)
    pss, rcv = _chip_scatter_wait(send_sems, recv_sems, pss, lands, g_pre_w)

    g_a, g_bc, g_ldt = _s5_prep_bwd(Wv["s5_A_re"], Wv["s5_A_im"], s5_log_dt, Wv["s5_B_re"], Wv["s5_B_im"],
                                    gbbd_re, gbbd_im, gct_re, gct_im, gab_re, gab_im)

    loss = lax.psum(loss11[0, 0], ("x", "y", "c"))

    g_vecs = jnp.concatenate([g_pre_w, g_post_w, g_D, g_glu_b, g_gate_bias, g_norm_w, g_ldt], axis=1)
    lanes_pad = -g_vecs.shape[1] % (8 * SUBLANES * LANES)
    g_vecs = jnp.pad(g_vecs, ((0, 0), (0, lanes_pad))).reshape(-1, LANES)
    r_vecs, r_a, r_bc = _allreduce_small([g_vecs, g_a, g_bc])
    outs4 = _adamw_small(r_vecs.reshape(1, -1), r_a, r_bc, [Wv[n] for n in small],
                         [view[n](M[n]) for n in small], [view[n](V[n]) for n in small])
    G_out, D_out, M_out, V_out = [{n: back[n](t) for n, t in zip(small, o)} for o in outs4]

    halves = [_chip_sum(p, r, me_arr, "grad_chip_sum_" + n) for n, p, r in zip(sharded, pss, rcv)]
    others = _pair_swap(halves)
    for n, g_own, g_other in zip(sharded, halves, others):
        g_, d_, m_, v_ = _adamw_sharded(W[n][0], g_own, g_other, M[n][0], V[n][0], c_arr, "adamw_" + n)
        G_out[n], D_out[n], M_out[n], V_out[n] = g_[None], d_[None], m_[None], v_[None]

    return (loss, grad_x[None], *[G_out[n] for n in names], *[D_out[n] for n in names],
            *[M_out[n] for n in names], *[V_out[n] for n in names])
```

```python
import functools
import math

import jax
import jax.numpy as jnp
from jax import lax
from jax.experimental import pallas as pl
from jax.experimental.pallas import tpu as pltpu

F32 = jnp.float32
BF16 = jnp.bfloat16
HI = lax.Precision.HIGHEST
MESH = pl.DeviceIdType.MESH

EPS = 1e-6
S5_GROUP = 16
S5_STATE = 64
GLA_HK = 128
GLA_HV = 256
GLA_RANK = 16
GLA_TAU = 16.0
GLA_CHUNK = 64
GLA_STEP_CHUNKS = 2
LANES = 128
SUBLANES = 8
S5_COLS = 128
S5_LANES = (S5_COLS // S5_GROUP) * S5_STATE

ADAM_LR = 0.001
ADAM_B1 = 0.9
ADAM_B2 = 0.999
ADAM_EPS = 1e-08
ADAM_WD = 0.01
ADAM_STEP = 10

GELU_K = math.sqrt(2.0 / math.pi)
GELU_C = 0.044715


def _blk(n, pref, unit=LANES):
    best = None
    b = unit
    while b <= min(n, pref):
        if n % b == 0:
            best = b
        b += unit
    return best if best is not None else n


def _dot(a, b, dn=(((1,), (0,)), ((), ()))):
    return lax.dot_general(a.astype(BF16), b.astype(BF16), dn, preferred_element_type=F32)


def _dot_hi(a, b, dn=(((1,), (0,)), ((), ()))):
    return lax.dot_general(a, b, dn, precision=HI, preferred_element_type=F32)


NN = (((1,), (0,)), ((), ()))
NT = (((1,), (1,)), ((), ()))
TN = (((0,), (0,)), ((), ()))


def _sigmoid(x):
    return 1.0 / (1.0 + jnp.exp(-x))


def _gelu(y):
    return 0.5 * y * (1.0 + jnp.tanh(GELU_K * (y + GELU_C * y * y * y)))


def _gelu_grad(y):
    th = jnp.tanh(GELU_K * (y + GELU_C * y * y * y))
    return 0.5 * (1.0 + th) + 0.5 * y * (1.0 - th * th) * GELU_K * (1.0 + 3.0 * GELU_C * y * y)


def _mm(a, b, *, name, ta=False, tb=False, out_dtype=F32, bm=1024, bn=1024, bk=2048):
    if ta:
        K, M = a.shape
    else:
        M, K = a.shape
    if tb:
        N, K2 = b.shape
    else:
        K2, N = b.shape
    assert K == K2, (a.shape, b.shape, ta, tb)
    bm, bn, bk = _blk(M, bm), _blk(N, bn), _blk(K, bk)
    nk = K // bk
    dn = (((0 if ta else 1,), (1 if tb else 0,)), ((), ()))

    def body(a_ref, b_ref, o_ref, *acc):
        if nk == 1:
            o_ref[...] = _dot(a_ref[...], b_ref[...], dn).astype(out_dtype)
            return
        acc_ref, = acc
        k = pl.program_id(2)

        @pl.when(k == 0)
        def _():
            acc_ref[...] = jnp.zeros_like(acc_ref)

        acc_ref[...] += _dot(a_ref[...], b_ref[...], dn)

        @pl.when(k == nk - 1)
        def _():
            o_ref[...] = acc_ref[...].astype(out_dtype)

    a_spec = pl.BlockSpec((bk, bm), lambda i, j, k: (k, i)) if ta else pl.BlockSpec((bm, bk), lambda i, j, k: (i, k))
    b_spec = pl.BlockSpec((bn, bk), lambda i, j, k: (j, k)) if tb else pl.BlockSpec((bk, bn), lambda i, j, k: (k, j))
    return pl.pallas_call(
        body,
        name=name,
        grid=(M // bm, N // bn, nk),
        in_specs=[a_spec, b_spec],
        out_specs=pl.BlockSpec((bm, bn), lambda i, j, k: (i, j)),
        out_shape=jax.ShapeDtypeStruct((M, N), out_dtype),
        scratch_shapes=[pltpu.VMEM((bm, bn), F32)] if nk > 1 else [],
        compiler_params=pltpu.CompilerParams(dimension_semantics=("parallel", "parallel", "arbitrary")),
    )(a, b)


def _in_proj(h, w_main, w_low, after):
    M, K = h.shape
    N = w_main.shape[1]
    bm, bn = _blk(M, 1024), _blk(N, 1024)

    def body(h_ref, w_ref, wl_ref, _after_ref, o_ref, ol_ref):
        hv = h_ref[...]
        o_ref[...] = _dot(hv, w_ref[...])

        @pl.when(pl.program_id(1) == 0)
        def _():
            ol_ref[...] = _dot(hv, wl_ref[...])

    return pl.pallas_call(
        body, name="in_proj", grid=(M // bm, N // bn),
        in_specs=[pl.BlockSpec((bm, K), lambda i, j: (i, 0)), pl.BlockSpec((K, bn), lambda i, j: (0, j)),
                  pl.BlockSpec((K, LANES), lambda i, j: (0, 0)), pl.BlockSpec(memory_space=pl.ANY)],
        out_specs=[pl.BlockSpec((bm, bn), lambda i, j: (i, j)), pl.BlockSpec((bm, LANES), lambda i, j: (i, 0))],
        out_shape=[jax.ShapeDtypeStruct((M, N), F32), jax.ShapeDtypeStruct((M, LANES), F32)],
        compiler_params=pltpu.CompilerParams(dimension_semantics=("parallel", "arbitrary")),
    )(h, w_main, w_low, after)


def _in_proj_dx(a1, a2, al, b, bl, after, *, bm=1024, bn=1024, bk=2048):
    M, K1 = a1.shape
    K2 = a2.shape[1]
    N = b.shape[0]
    bm, bn = _blk(M, bm), _blk(N, bn)
    bk = _blk(math.gcd(K1, K2), bk)
    nk1, nk = K1 // bk, (K1 + K2) // bk

    def body(a1_ref, a2_ref, al_ref, b_ref, bl_ref, _after_ref, o_ref, acc_ref):
        k = pl.program_id(2)

        @pl.when(k == 0)
        def _():
            acc_ref[...] = _dot(al_ref[...], bl_ref[...], NT)

        @pl.when(k < nk1)
        def _():
            acc_ref[...] += _dot(a1_ref[...], b_ref[...], NT)

        @pl.when(k >= nk1)
        def _():
            acc_ref[...] += _dot(a2_ref[...], b_ref[...], NT)

        @pl.when(k == nk - 1)
        def _():
            o_ref[...] = acc_ref[...]

    return pl.pallas_call(
        body, name="in_proj_dx", grid=(M // bm, N // bn, nk),
        in_specs=[pl.BlockSpec((bm, bk), lambda i, j, k: (i, jnp.minimum(k, nk1 - 1))),
                  pl.BlockSpec((bm, bk), lambda i, j, k: (i, jnp.maximum(k - nk1, 0))),
                  pl.BlockSpec((bm, LANES), lambda i, j, k: (i, 0)),
                  pl.BlockSpec((bn, bk), lambda i, j, k: (j, k)),
                  pl.BlockSpec((bn, LANES), lambda i, j, k: (j, 0)),
                  pl.BlockSpec(memory_space=pl.ANY)],
        out_specs=pl.BlockSpec((bm, bn), lambda i, j, k: (i, j)),
        out_shape=jax.ShapeDtypeStruct((M, N), F32),
        scratch_shapes=[pltpu.VMEM((bm, bn), F32)],
        compiler_params=pltpu.CompilerParams(dimension_semantics=("parallel", "parallel", "arbitrary")),
    )(a1, a2, al, b, bl, after)


def _in_proj_dw(a, b1, b2, bl, *, bm=1024, bn=1024, bk=2048):
    K, M = a.shape
    N1, N2 = b1.shape[1], b2.shape[1]
    bm, bk = _blk(M, bm), _blk(K, bk)
    bn = _blk(math.gcd(N1, N2), bn)
    nj1, nj = N1 // bn, (N1 + N2) // bn
    nk = K // bk

    def body(a_ref, b1_ref, b2_ref, bl_ref, o_ref, ol_ref, acc_ref, accl_ref):
        j = pl.program_id(1)
        k = pl.program_id(2)

        @pl.when(k == 0)
        def _():
            acc_ref[...] = jnp.zeros_like(acc_ref)

        @pl.when(j < nj1)
        def _():
            acc_ref[...] += _dot(a_ref[...], b1_ref[...], TN)

        @pl.when(j >= nj1)
        def _():
            acc_ref[...] += _dot(a_ref[...], b2_ref[...], TN)

        @pl.when(k == nk - 1)
        def _():
            o_ref[...] = acc_ref[...].astype(BF16)

        @pl.when(j == 0)
        def _():
            low = _dot(a_ref[...], bl_ref[...], TN)

            @pl.when(k == 0)
            def _():
                accl_ref[...] = low

            @pl.when(k > 0)
            def _():
                accl_ref[...] += low

            @pl.when(k == nk - 1)
            def _():
                ol_ref[...] = accl_ref[...].astype(BF16)

    return pl.pallas_call(
        body, name="in_proj_dw", grid=(M // bm, nj, nk),
        in_specs=[pl.BlockSpec((bk, bm), lambda i, j, k: (k, i)),
                  pl.BlockSpec((bk, bn), lambda i, j, k: (jnp.where(j < nj1, k, nk - 1), jnp.minimum(j, nj1 - 1))),
                  pl.BlockSpec((bk, bn), lambda i, j, k: (jnp.where(j >= nj1, k, 0), jnp.maximum(j - nj1, 0))),
                  pl.BlockSpec((bk, LANES), lambda i, j, k: (jnp.where(j == 0, k, nk - 1), 0))],
        out_specs=[pl.BlockSpec((bm, bn), lambda i, j, k: (i, j)), pl.BlockSpec((bm, LANES), lambda i, j, k: (i, 0))],
        out_shape=[jax.ShapeDtypeStruct((M, N1 + N2), BF16), jax.ShapeDtypeStruct((M, LANES), BF16)],
        scratch_shapes=[pltpu.VMEM((bm, bn), F32), pltpu.VMEM((bm, LANES), F32)],
        compiler_params=pltpu.CompilerParams(dimension_semantics=("parallel", "arbitrary", "arbitrary")),
    )(a, b1, b2, bl)


def _prenorm_fwd(x, w, after):
    L, D = x.shape
    tr = _blk(L, 256, SUBLANES)

    def body(x_ref, w_ref, _after_ref, h_ref):
        xv = x_ref[...]
        r = lax.rsqrt(jnp.mean(xv * xv, axis=-1, keepdims=True) + EPS)
        h_ref[...] = (xv * r * w_ref[...]).astype(BF16)

    return pl.pallas_call(
        body, name="prenorm_fwd", grid=(L // tr,),
        in_specs=[pl.BlockSpec((tr, D), lambda i: (i, 0)), pl.BlockSpec((1, D), lambda i: (0, 0)),
                  pl.BlockSpec(memory_space=pl.ANY)],
        out_specs=pl.BlockSpec((tr, D), lambda i: (i, 0)),
        out_shape=jax.ShapeDtypeStruct((L, D), BF16),
        compiler_params=pltpu.CompilerParams(dimension_semantics=("parallel",)),
    )(x, w, after)


def _post_fwd_bwd(mixed, x, target, w):
    L, D = x.shape
    tr = _blk(L, 256, SUBLANES)
    nsteps = L // tr

    def body(mx_ref, x_ref, t_ref, w_ref, loss_ref, dm_ref, dout_ref, gw_ref, acc_ref):
        i = pl.program_id(0)

        @pl.when(i == 0)
        def _():
            acc_ref[...] = jnp.zeros_like(acc_ref)
            gw_ref[...] = jnp.zeros_like(gw_ref)

        mx = mx_ref[...]
        wv = w_ref[...]
        r = lax.rsqrt(jnp.mean(mx * mx, axis=-1, keepdims=True) + EPS)
        n = mx * r
        err = x_ref[...] + n * wv - t_ref[...]
        acc_ref[...] += jnp.sum(err * err, axis=0, keepdims=True)
        dout = err * (1.0 / D)
        dout_ref[...] = dout
        gw_ref[...] += jnp.sum(dout * n, axis=0, keepdims=True)
        dn = dout * wv
        dm_ref[...] = (r * (dn - n * jnp.mean(dn * n, axis=-1, keepdims=True))).astype(BF16)

        @pl.when(i == nsteps - 1)
        def _():
            loss_ref[...] = jnp.sum(acc_ref[...], axis=-1, keepdims=True) * (0.5 / D)

    row = pl.BlockSpec((tr, D), lambda i: (i, 0))
    vec = pl.BlockSpec((1, D), lambda i: (0, 0))
    return pl.pallas_call(
        body, name="post_fwd_bwd", grid=(nsteps,),
        in_specs=[row, row, row, vec],
        out_specs=[pl.BlockSpec((1, 1), lambda i: (0, 0)), row, row, vec],
        out_shape=[jax.ShapeDtypeStruct((1, 1), F32), jax.ShapeDtypeStruct((L, D), BF16),
                   jax.ShapeDtypeStruct((L, D), F32), jax.ShapeDtypeStruct((1, D), F32)],
        scratch_shapes=[pltpu.VMEM((1, D), F32)],
        compiler_params=pltpu.CompilerParams(dimension_semantics=("arbitrary",)),
    )(mixed, x, target, w)


def _prenorm_bwd(x, dh, dout, w):
    L, D = x.shape
    tr = _blk(L, 256, SUBLANES)

    def body(x_ref, a_ref, dout_ref, w_ref, gx_ref, gw_ref):
        i = pl.program_id(0)

        @pl.when(i == 0)
        def _():
            gw_ref[...] = jnp.zeros_like(gw_ref)

        xv = x_ref[...]
        r = lax.rsqrt(jnp.mean(xv * xv, axis=-1, keepdims=True) + EPS)
        n = xv * r
        dh = a_ref[...]
        gw_ref[...] += jnp.sum(dh * n, axis=0, keepdims=True)
        dn = dh * w_ref[...]
        gx_ref[...] = dout_ref[...] + r * (dn - n * jnp.mean(dn * n, axis=-1, keepdims=True))

    row = pl.BlockSpec((tr, D), lambda i: (i, 0))
    vec = pl.BlockSpec((1, D), lambda i: (0, 0))
    return pl.pallas_call(
        body, name="prenorm_bwd", grid=(L // tr,),
        in_specs=[row, row, row, vec],
        out_specs=[row, vec],
        out_shape=[jax.ShapeDtypeStruct((L, D), F32), jax.ShapeDtypeStruct((1, D), F32)],
        compiler_params=pltpu.CompilerParams(dimension_semantics=("arbitrary",)),
    )(x, dh, dout, w)


def _s5_disc(a_re_raw, a_im, dt):
    a_re = jnp.minimum(a_re_raw, -1e-4)
    mag = jnp.exp(a_re * dt)
    ph = a_im * dt
    ab_re = mag * jnp.cos(ph)
    ab_im = mag * jnp.sin(ph)
    inv_n = 1.0 / (a_re * a_re + a_im * a_im)
    ia_re = a_re * inv_n
    ia_im = -a_im * inv_n
    n_re = ab_re - 1.0
    f_re = n_re * ia_re - ab_im * ia_im
    f_im = n_re * ia_im + ab_im * ia_re
    return a_re, ab_re, ab_im, f_re, f_im, ia_re, ia_im


def _iota2(shape, dim):
    return lax.broadcasted_iota(jnp.int32, shape, dim)


def _group_mask(rows, rows_per_group):
    shift = rows_per_group.bit_length() - 1
    return (_iota2((rows, S5_LANES), 0) >> shift) == (_iota2((rows, S5_LANES), 1) >> (S5_STATE.bit_length() - 1))


def _lane_tiler(dtype):
    return ((_iota2((S5_STATE, S5_LANES), 1) & (S5_STATE - 1)) == _iota2((S5_STATE, S5_LANES), 0)).astype(dtype)


def _row_to_col(row, n):
    eye = (_iota2((n, n), 0) == _iota2((n, n), 1)).astype(F32)
    return jnp.sum(eye * row, axis=1, keepdims=True)


def _group_repeat(G):
    return ((_iota2((G * S5_GROUP, G), 0) >> (S5_GROUP.bit_length() - 1)) == _iota2((G * S5_GROUP, G), 1)).astype(F32)


S5_TABS = 18


def _s5_prep_fwd(a_re, a_im, log_dt, b_re, b_im, c_re, c_im, after, seg):
    G, P = a_re.shape
    nb = G * S5_GROUP // S5_COLS
    g8 = S5_COLS // S5_GROUP
    assert seg & (seg - 1) == 0, seg

    def body(are_ref, aim_ref, ldt_ref, bre_ref, bim_ref, cre_ref, cim_ref, _after_ref,
             bbre_ref, bbim_ref, ctre_ref, ctim_ref, tab_ref, pt_ref):
        dt = jnp.exp(_row_to_col(ldt_ref[...], G))
        _, ab_re, ab_im, f_re, f_im, _, _ = _s5_disc(are_ref[...], aim_ref[...], dt)
        rep = _group_repeat(G)
        fx_re = _dot_hi(rep, f_re)
        fx_im = _dot_hi(rep, f_im)
        br, bi = bre_ref[...], bim_ref[...]
        bb_re = fx_re * br - fx_im * bi
        bb_im = fx_re * bi + fx_im * br
        tile_bf = _lane_tiler(BF16)
        mask = _group_mask(S5_COLS, S5_GROUP)
        for jb in range(nb):
            rs = slice(jb * S5_COLS, (jb + 1) * S5_COLS)
            for src, dst in ((bb_re[rs], bbre_ref), (bb_im[rs], bbim_ref), (cre_ref[rs, :], ctre_ref), (cim_ref[rs, :], ctim_ref)):
                dst[jb] = jnp.where(mask, _dot(src, tile_bf), 0.0).astype(BF16)

        tile_f = _lane_tiler(F32)
        mask8 = _group_mask(g8, 1)
        row = _iota2((SUBLANES, S5_LANES), 0)
        slab = (SUBLANES, S5_LANES)
        cmul = lambda p, q: (p[0] * q[0] - p[1] * q[1], p[0] * q[1] + p[1] * q[0])
        for jb in range(nb):
            gs = slice(jb * g8, (jb + 1) * g8)

            def lanes(m):
                v = jnp.sum(jnp.where(mask8, _dot_hi(m[gs], tile_f), 0.0), axis=0, keepdims=True)
                return jnp.broadcast_to(v, slab)

            a1 = (lanes(ab_re), lanes(ab_im))
            tab_ref[jb, 0], tab_ref[jb, 1] = a1

            def powers(i, p):
                off = pl.multiple_of(i * SUBLANES, SUBLANES)
                pt_ref[jb, 0, pl.ds(off, SUBLANES), :] = p[0]
                pt_ref[jb, 1, pl.ds(off, SUBLANES), :] = p[1]
                return cmul(p, a1)

            lax.fori_loop(0, seg, powers, a1)
            aseg = a1
            for _ in range(seg.bit_length() - 1):
                aseg = cmul(aseg, aseg)
            pw = [aseg]
            for _ in range(1, SUBLANES):
                pw.append(cmul(pw[-1], aseg))
            for lvl, k in enumerate((1, 2, 4)):
                tab_ref[jb, 2 + 2 * lvl] = jnp.where(row >= k, pw[k - 1][0], 0.0)
                tab_ref[jb, 3 + 2 * lvl] = jnp.where(row >= k, pw[k - 1][1], 0.0)
                tab_ref[jb, 10 + 2 * lvl] = jnp.where(row < SUBLANES - k, pw[k - 1][0], 0.0)
                tab_ref[jb, 11 + 2 * lvl] = jnp.where(row < SUBLANES - k, -pw[k - 1][1], 0.0)
            f_r = f_i = r_r = r_i = jnp.zeros(slab, F32)
            for i in range(SUBLANES):
                f_r = jnp.where(row == i, pw[i][0], f_r)
                f_i = jnp.where(row == i, pw[i][1], f_i)
                r_r = jnp.where(row == i, pw[SUBLANES - 1 - i][0], r_r)
                r_i = jnp.where(row == i, -pw[SUBLANES - 1 - i][1], r_i)
            tab_ref[jb, 8] = f_r
            tab_ref[jb, 9] = f_i
            tab_ref[jb, 16] = r_r
            tab_ref[jb, 17] = r_i

    vm = pl.BlockSpec(memory_space=pltpu.VMEM)
    bd = jax.ShapeDtypeStruct((nb, S5_COLS, S5_LANES), BF16)
    return pl.pallas_call(
        body, name="s5_prep_fwd",
        in_specs=[vm] * 7 + [pl.BlockSpec(memory_space=pl.ANY)], out_specs=[vm] * 6,
        out_shape=[bd, bd, bd, bd, jax.ShapeDtypeStruct((nb, S5_TABS, SUBLANES, S5_LANES), F32),
                   jax.ShapeDtypeStruct((nb, 2, seg * SUBLANES, S5_LANES), F32)],
    )(a_re, a_im, log_dt, b_re, b_im, c_re, c_im, after)


def _s5_prep_bwd(a_re, a_im, log_dt, b_re, b_im, gbb_re, gbb_im, gct_re, gct_im, gab_re, gab_im):
    G, P = a_re.shape
    nb = G * S5_GROUP // S5_COLS
    g8 = S5_COLS // S5_GROUP

    def body(are_ref, aim_ref, ldt_ref, bre_ref, bim_ref, gbr_ref, gbi_ref, gcr_ref, gci_ref, gar_ref, gai_ref,
             o_a, o_bc, o_ldt):
        dt = jnp.exp(_row_to_col(ldt_ref[...], G))
        a_raw = are_ref[...]
        a_imv = aim_ref[...]
        a_re_c, ab_re, ab_im, f_re, f_im, ia_re, ia_im = _s5_disc(a_raw, a_imv, dt)
        tile_f = _lane_tiler(F32)
        mask = _group_mask(S5_COLS, S5_GROUP)
        mask8 = _group_mask(g8, 1)
        for jb in range(nb):
            rs = slice(jb * S5_COLS, (jb + 1) * S5_COLS)
            gs = slice(jb * g8, (jb + 1) * g8)
            ls = slice(jb * S5_LANES, (jb + 1) * S5_LANES)
            for k, src in enumerate((gbr_ref, gbi_ref, gcr_ref, gci_ref)):
                o_bc[k, rs, :] = _dot_hi(jnp.where(mask, src[jb], 0.0), tile_f, NT)
            for k, src in enumerate((gar_ref, gai_ref)):
                o_a[k, gs, :] = _dot_hi(jnp.where(mask8, src[:, ls], 0.0), tile_f, NT)
        rep = _group_repeat(G)
        fx_re = _dot_hi(rep, f_re)
        fx_im = _dot_hi(rep, f_im)
        gbr, gbi = o_bc[0], o_bc[1]
        br, bi = bre_ref[...], bim_ref[...]
        o_bc[0] = fx_re * gbr + fx_im * gbi
        o_bc[1] = fx_re * gbi - fx_im * gbr
        gf_re = _dot_hi(rep, br * gbr + bi * gbi, TN)
        gf_im = _dot_hi(rep, br * gbi - bi * gbr, TN)
        gab_r = o_a[0] + ia_re * gf_re + ia_im * gf_im
        gab_i = o_a[1] + ia_re * gf_im - ia_im * gf_re
        q_re = f_re * ia_re - f_im * ia_im
        q_im = f_re * ia_im + f_im * ia_re
        ga_re = -(q_re * gf_re + q_im * gf_im)
        ga_im = -(q_re * gf_im - q_im * gf_re)
        gth_re = ab_re * gab_r + ab_im * gab_i
        gth_im = ab_re * gab_i - ab_im * gab_r
        ga_re = ga_re + dt * gth_re
        ga_im = ga_im + dt * gth_im
        gdt = jnp.sum(a_re_c * gth_re + a_imv * gth_im, axis=-1, keepdims=True)
        eye = (_iota2((G, G), 0) == _iota2((G, G), 1)).astype(F32)
        o_ldt[...] = jnp.sum(eye * (gdt * dt), axis=0, keepdims=True)
        slope = jnp.where(a_raw < -1e-4, 1.0, jnp.where(a_raw == -1e-4, 0.5, 0.0))
        o_a[0] = ga_re * slope
        o_a[1] = ga_im

    vm = pl.BlockSpec(memory_space=pltpu.VMEM)
    return pl.pallas_call(
        body, name="s5_prep_bwd",
        in_specs=[vm] * 11, out_specs=[vm] * 3,
        out_shape=[jax.ShapeDtypeStruct((2, G, P), F32), jax.ShapeDtypeStruct((4, G * S5_GROUP, P), F32),
                   jax.ShapeDtypeStruct((1, G), F32)],
    )(a_re, a_im, log_dt, b_re, b_im, gbb_re, gbb_im, gct_re, gct_im, gab_re, gab_im)


def _scan8(xr, xi, tab_ref, base, shifts):
    for lvl, sh in enumerate(shifts):
        mr = tab_ref[0, base + 2 * lvl]
        mi = tab_ref[0, base + 2 * lvl + 1]
        ar = pltpu.roll(xr, sh, 0)
        ai = pltpu.roll(xi, sh, 0)
        xr, xi = xr + mr * ar - mi * ai, xi + mr * ai + mi * ar
    return xr, xi


def _to_segments(src_ref, dst_ref, seg):
    for i in range(seg):
        dst_ref[i * SUBLANES:(i + 1) * SUBLANES, :] = src_ref[pl.ds(i, SUBLANES, stride=seg), :]


def _from_segments(src_ref, dst_ref, seg):
    for i in range(seg):
        dst_ref[pl.ds(i, SUBLANES, stride=seg), :] = src_ref[i * SUBLANES:(i + 1) * SUBLANES, :]


def _slab(i):
    return pl.ds(pl.multiple_of(i * SUBLANES, SUBLANES), SUBLANES)


def _s5_scan_fwd(proj_main, bbd_re, bbd_im, cbd_re, cbd_im, dvec, tab, ptab, DS):
    L = proj_main.shape[0]
    nb = DS // S5_COLS
    tb = _blk(L, 512, SUBLANES)
    nt = L // tb
    seg = tb // SUBLANES

    def body(u_ref, bre_ref, bim_ref, cre_ref, cim_ref, d_ref, tab_ref, pt_ref, y_ref, sre_ref, sim_ref,
             up_ref, yp_ref, car_ref):
        t = pl.program_id(1)

        @pl.when(t == 0)
        def _():
            car_ref[...] = jnp.zeros_like(car_ref)

        _to_segments(u_ref, up_ref, seg)
        up = up_ref[...]
        sre_ref[...] = _dot(up, bre_ref[0])
        sim_ref[...] = _dot(up, bim_ref[0])
        ar, ai = tab_ref[0, 0], tab_ref[0, 1]

        def pass1(i, x):
            xr = ar * x[0] - ai * x[1] + sre_ref[_slab(i), :]
            xi = ar * x[1] + ai * x[0] + sim_ref[_slab(i), :]
            sre_ref[_slab(i), :] = xr
            sim_ref[_slab(i), :] = xi
            return xr, xi

        zero = jnp.zeros((SUBLANES, S5_LANES), F32)
        er, ei = lax.fori_loop(0, seg, pass1, (zero, zero))
        cin_r, cin_i = car_ref[0], car_ref[1]
        sr, si = _scan8(er, ei, tab_ref, 2, (1, 2, 4))
        pr, pi = tab_ref[0, 8], tab_ref[0, 9]
        sr, si = sr + pr * cin_r - pi * cin_i, si + pr * cin_i + pi * cin_r
        row0 = _iota2((SUBLANES, S5_LANES), 0) == 0
        cr = jnp.where(row0, cin_r, pltpu.roll(sr, 1, 0))
        ci = jnp.where(row0, cin_i, pltpu.roll(si, 1, 0))
        car_ref[0] = jnp.broadcast_to(sr[SUBLANES - 1:SUBLANES, :], sr.shape)
        car_ref[1] = jnp.broadcast_to(si[SUBLANES - 1:SUBLANES, :], si.shape)

        def pass2(i, _):
            qr, qi = pt_ref[0, 0, _slab(i), :], pt_ref[0, 1, _slab(i), :]
            sre_ref[_slab(i), :] += qr * cr - qi * ci
            sim_ref[_slab(i), :] += qr * ci + qi * cr
            return 0

        lax.fori_loop(0, seg, pass2, 0, unroll=4)
        yp_ref[...] = _dot(sre_ref[...], cre_ref[0], NT) - _dot(sim_ref[...], cim_ref[0], NT) + d_ref[...] * up
        _from_segments(yp_ref, y_ref, seg)

    return pl.pallas_call(
        body, name="s5_scan_fwd", grid=(nb, nt),
        in_specs=[
            pl.BlockSpec((tb, S5_COLS), lambda j, t: (t, j)),
            pl.BlockSpec((1, S5_COLS, S5_LANES), lambda j, t: (j, 0, 0)),
            pl.BlockSpec((1, S5_COLS, S5_LANES), lambda j, t: (j, 0, 0)),
            pl.BlockSpec((1, S5_COLS, S5_LANES), lambda j, t: (j, 0, 0)),
            pl.BlockSpec((1, S5_COLS, S5_LANES), lambda j, t: (j, 0, 0)),
            pl.BlockSpec((1, S5_COLS), lambda j, t: (0, j)),
            pl.BlockSpec((1, S5_TABS, SUBLANES, S5_LANES), lambda j, t: (j, 0, 0, 0)),
            pl.BlockSpec((1, 2, tb, S5_LANES), lambda j, t: (j, 0, 0, 0)),
        ],
        out_specs=[
            pl.BlockSpec((tb, S5_COLS), lambda j, t: (t, j)),
            pl.BlockSpec((tb, S5_LANES), lambda j, t: (t, j)),
            pl.BlockSpec((tb, S5_LANES), lambda j, t: (t, j)),
        ],
        out_shape=[jax.ShapeDtypeStruct((L, DS), F32),
                   jax.ShapeDtypeStruct((L, nb * S5_LANES), F32),
                   jax.ShapeDtypeStruct((L, nb * S5_LANES), F32)],
        scratch_shapes=[pltpu.VMEM((tb, S5_COLS), F32), pltpu.VMEM((tb, S5_COLS), F32),
                        pltpu.VMEM((2, SUBLANES, S5_LANES), F32)],
        compiler_params=pltpu.CompilerParams(dimension_semantics=("parallel", "arbitrary")),
    )(proj_main, bbd_re, bbd_im, cbd_re, cbd_im, dvec, tab, ptab)


def _s5_scan_bwd(dy, proj_main, s_re, s_im, bbd_re, bbd_im, cbd_re, cbd_im, dvec, tab, ptab, d_s5, DS):
    L = proj_main.shape[0]
    nb = DS // S5_COLS
    tb = _blk(L, 512, SUBLANES)
    nt = L // tb
    seg = tb // SUBLANES
    tb8 = tb // SUBLANES

    def body(dy_ref, u_ref, sre_ref, sim_ref, pre_ref, pim_ref, bre_ref, bim_ref, cre_ref, cim_ref, d_ref, tab_ref, pt_ref,
             _ds5_ref, du_ref, gd_ref, gcre_ref, gcim_ref, gbre_ref, gbim_ref, gare_ref, gaim_ref,
             lre_ref, lim_ref, up_ref, dyp_ref, dup_ref, duo_ref, car_ref):
        t = pl.program_id(1)

        @pl.when(t == 0)
        def _():
            car_ref[...] = jnp.zeros_like(car_ref)
            gd_ref[...] = jnp.zeros_like(gd_ref)
            gcre_ref[...] = jnp.zeros_like(gcre_ref)
            gcim_ref[...] = jnp.zeros_like(gcim_ref)
            gbre_ref[...] = jnp.zeros_like(gbre_ref)
            gbim_ref[...] = jnp.zeros_like(gbim_ref)
            gare_ref[...] = jnp.zeros_like(gare_ref)
            gaim_ref[...] = jnp.zeros_like(gaim_ref)

        _to_segments(dy_ref, dyp_ref, seg)
        _to_segments(u_ref, up_ref, seg)
        dyv = dyp_ref[...]
        u = up_ref[...]
        gd_ref[...] += jnp.sum(dyv * u, axis=0, keepdims=True)
        lre_ref[...] = _dot(dyv, cre_ref[0])
        lim_ref[...] = -_dot(dyv, cim_ref[0])
        gcre_ref[0] += _dot(dyv, sre_ref[...], TN)
        gcim_ref[0] -= _dot(dyv, sim_ref[...], TN)
        ar, ai = tab_ref[0, 0], -tab_ref[0, 1]

        def pass1(k, x):
            i = seg - 1 - k
            xr = ar * x[0] - ai * x[1] + lre_ref[_slab(i), :]
            xi = ar * x[1] + ai * x[0] + lim_ref[_slab(i), :]
            lre_ref[_slab(i), :] = xr
            lim_ref[_slab(i), :] = xi
            return xr, xi

        zero = jnp.zeros((SUBLANES, S5_LANES), F32)
        er, ei = lax.fori_loop(0, seg, pass1, (zero, zero))
        cin_r, cin_i = car_ref[0], car_ref[1]
        lr, li = _scan8(er, ei, tab_ref, 10, (7, 6, 4))
        pr, pi = tab_ref[0, 16], tab_ref[0, 17]
        lr, li = lr + pr * cin_r - pi * cin_i, li + pr * cin_i + pi * cin_r
        rows = _iota2((SUBLANES, S5_LANES), 0)
        cr = jnp.where(rows == SUBLANES - 1, cin_r, pltpu.roll(lr, SUBLANES - 1, 0))
        ci = jnp.where(rows == SUBLANES - 1, cin_i, pltpu.roll(li, SUBLANES - 1, 0))
        car_ref[0] = jnp.broadcast_to(lr[0:1, :], lr.shape)
        car_ref[1] = jnp.broadcast_to(li[0:1, :], li.shape)

        first = (t == nt - 1).astype(F32)
        head_re = jnp.broadcast_to(pre_ref[SUBLANES - 1:SUBLANES, :], zero.shape) * (1.0 - first)
        head_im = jnp.broadcast_to(pim_ref[SUBLANES - 1:SUBLANES, :], zero.shape) * (1.0 - first)
        last = _slab(seg - 1)
        sp0_re = jnp.where(rows == 0, head_re, pltpu.roll(sre_ref[last, :], 1, 0))
        sp0_im = jnp.where(rows == 0, head_im, pltpu.roll(sim_ref[last, :], 1, 0))

        def pass2(i, acc):
            j = seg - 1 - i
            qr, qi = pt_ref[0, 0, _slab(j), :], -pt_ref[0, 1, _slab(j), :]
            xr = lre_ref[_slab(i), :] + qr * cr - qi * ci
            xi = lim_ref[_slab(i), :] + qr * ci + qi * cr
            lre_ref[_slab(i), :] = xr
            lim_ref[_slab(i), :] = xi
            prev = _slab(jnp.maximum(i - 1, 0))
            sp_re = jnp.where(i == 0, sp0_re, sre_ref[prev, :])
            sp_im = jnp.where(i == 0, sp0_im, sim_ref[prev, :])
            return acc[0] + sp_re * xr + sp_im * xi, acc[1] + sp_re * xi - sp_im * xr

        acc_re, acc_im = lax.fori_loop(0, seg, pass2, (zero, zero), unroll=2)
        gare_ref[...] += jnp.sum(acc_re, axis=0, keepdims=True)
        gaim_ref[...] += jnp.sum(acc_im, axis=0, keepdims=True)
        lre = lre_ref[...]
        lim = lim_ref[...]
        dup_ref[...] = dyv * d_ref[...] + _dot(lre, bre_ref[0], NT) + _dot(lim, bim_ref[0], NT)
        _from_segments(dup_ref, duo_ref, seg)
        du_ref[...] = duo_ref[...].astype(BF16)
        gbre_ref[0] += _dot(u, lre, TN)
        gbim_ref[0] += _dot(u, lim, TN)

    rt = lambda t: nt - 1 - t
    col = pl.BlockSpec((tb, S5_COLS), lambda j, t: (rt(t), j))
    st = pl.BlockSpec((tb, S5_LANES), lambda j, t: (rt(t), j))
    prev = pl.BlockSpec((SUBLANES, S5_LANES), lambda j, t: (jnp.maximum(rt(t) * tb8 - 1, 0), j))
    bmat = pl.BlockSpec((1, S5_COLS, S5_LANES), lambda j, t: (j, 0, 0))
    cmat = bmat
    return pl.pallas_call(
        body, name="s5_scan_bwd", grid=(nb, nt),
        in_specs=[col, col, st, st, prev, prev, bmat, bmat, cmat, cmat,
                  pl.BlockSpec((1, S5_COLS), lambda j, t: (0, j)),
                  pl.BlockSpec((1, S5_TABS, SUBLANES, S5_LANES), lambda j, t: (j, 0, 0, 0)),
                  pl.BlockSpec((1, 2, tb, S5_LANES), lambda j, t: (j, 0, 0, 0)),
                  pl.BlockSpec(memory_space=pl.ANY)],
        out_specs=[col, pl.BlockSpec((1, S5_COLS), lambda j, t: (0, j)), cmat, cmat, bmat, bmat,
                   pl.BlockSpec((1, S5_LANES), lambda j, t: (0, j)), pl.BlockSpec((1, S5_LANES), lambda j, t: (0, j))],
        input_output_aliases={13: 0},
        out_shape=[jax.ShapeDtypeStruct((L, 2 * DS), BF16), jax.ShapeDtypeStruct((1, DS), F32),
                   jax.ShapeDtypeStruct((nb, S5_COLS, S5_LANES), F32), jax.ShapeDtypeStruct((nb, S5_COLS, S5_LANES), F32),
                   jax.ShapeDtypeStruct((nb, S5_COLS, S5_LANES), F32), jax.ShapeDtypeStruct((nb, S5_COLS, S5_LANES), F32),
                   jax.ShapeDtypeStruct((1, nb * S5_LANES), F32), jax.ShapeDtypeStruct((1, nb * S5_LANES), F32)],
        scratch_shapes=[pltpu.VMEM((tb, S5_LANES), F32), pltpu.VMEM((tb, S5_LANES), F32)]
        + [pltpu.VMEM((tb, S5_COLS), F32)] * 4 + [pltpu.VMEM((2, SUBLANES, S5_LANES), F32)],
        compiler_params=pltpu.CompilerParams(dimension_semantics=("parallel", "arbitrary")),
    )(dy, proj_main, s_re, s_im, s_re, s_im, bbd_re, bbd_im, cbd_re, cbd_im, dvec, tab, ptab, d_s5)


def _s5_post_fwd(y_pre, proj_main, glu_w, glu_b, DS):
    L = y_pre.shape[0]
    tr = _blk(L, 256, SUBLANES)

    def body(y_ref, z_ref, w_ref, b_ref, o_ref, t_ref):
        y1 = _gelu(y_ref[...])
        t = _dot(y1, w_ref[...]) + b_ref[...]
        t_ref[...] = t
        z = z_ref[...]
        o_ref[...] = (y1 * _sigmoid(t) * (z * _sigmoid(z))).astype(BF16)

    row = pl.BlockSpec((tr, DS), lambda i: (i, 0))
    return pl.pallas_call(
        body, name="s5_post_fwd", grid=(L // tr,),
        in_specs=[row, pl.BlockSpec((tr, DS), lambda i: (i, 1)), pl.BlockSpec((DS, DS), lambda i: (0, 0)),
                  pl.BlockSpec((1, DS), lambda i: (0, 0))],
        out_specs=[row, row],
        out_shape=[jax.ShapeDtypeStruct((L, 2 * DS), BF16), jax.ShapeDtypeStruct((L, DS), F32)],
        compiler_params=pltpu.CompilerParams(dimension_semantics=("parallel",)),
    )(y_pre, proj_main, glu_w, glu_b)


def _s5_post_bwd(d_ycat, y_pre, proj_main, t_pre, glu_w, DS):
    L = y_pre.shape[0]
    tr = _blk(L, 256, SUBLANES)

    def body(dy_ref, y_ref, z_ref, t_ref, w_ref, dyp_ref, dz_ref, dt_ref, y1_ref, gb_ref):
        i = pl.program_id(0)

        @pl.when(i == 0)
        def _():
            gb_ref[...] = jnp.zeros_like(gb_ref)

        dy = dy_ref[...]
        yp = y_ref[...]
        z = z_ref[...]
        y1 = _gelu(yp)
        sg = _sigmoid(t_ref[...])
        sz = _sigmoid(z)
        c = y1 * sg
        d_c = dy * (z * sz)
        dz_ref[...] = (dy * c * (sz * (1.0 + z * (1.0 - sz)))).astype(BF16)
        d_t = d_c * y1 * sg * (1.0 - sg)
        gb_ref[...] += jnp.sum(d_t, axis=0, keepdims=True)
        dt_ref[...] = d_t.astype(BF16)
        y1_ref[...] = y1.astype(BF16)
        d_y1 = d_c * sg + _dot(d_t, w_ref[...], NT)
        dyp_ref[...] = d_y1 * _gelu_grad(yp)

    row = pl.BlockSpec((tr, DS), lambda i: (i, 0))
    return pl.pallas_call(
        body, name="s5_post_bwd", grid=(L // tr,),
        in_specs=[row, row, pl.BlockSpec((tr, DS), lambda i: (i, 1)), row, pl.BlockSpec((DS, DS), lambda i: (0, 0))],
        out_specs=[row, pl.BlockSpec((tr, DS), lambda i: (i, 1)), row, row, pl.BlockSpec((1, DS), lambda i: (0, 0))],
        out_shape=[jax.ShapeDtypeStruct((L, DS), F32), jax.ShapeDtypeStruct((L, 2 * DS), BF16),
                   jax.ShapeDtypeStruct((L, DS), BF16), jax.ShapeDtypeStruct((L, DS), BF16),
                   jax.ShapeDtypeStruct((1, DS), F32)],
        compiler_params=pltpu.CompilerParams(dimension_semantics=("arbitrary",)),
    )(d_ycat, y_pre, proj_main, t_pre, glu_w)


def _gla_gates(glow, gu_ref, gb_ref):
    a = _dot(glow, gu_ref[...]) + gb_ref[...]
    lg = (jnp.minimum(a, 0.0) - jnp.log(1.0 + jnp.exp(-jnp.abs(a)))) * (1.0 / GLA_TAU)
    ri = lax.broadcasted_iota(jnp.int32, (GLA_CHUNK, GLA_CHUNK), 0)
    ci = lax.broadcasted_iota(jnp.int32, (GLA_CHUNK, GLA_CHUNK), 1)
    b = _dot_hi((ri >= ci).astype(F32), lg)
    b_last = jnp.sum(lg, axis=0, keepdims=True)
    return a, b, b_last, ri >= ci


def _gla_specs(DS, DK, DV, c, cmap):
    return [
        pl.BlockSpec((c, DK), lambda n: (cmap(n), 2 * DS // DK)),
        pl.BlockSpec((c, DK), lambda n: (cmap(n), 2 * DS // DK + 1)),
        pl.BlockSpec((c, DV), lambda n: (cmap(n), (2 * DS + 2 * DK) // DV)),
        pl.BlockSpec((c, DV), lambda n: (cmap(n), (2 * DS + 2 * DK) // DV + 1)),
    ]


def _gla_fwd(proj_main, proj_low, gate_up_pad, gate_bias, norm_w, ycat, DS, DK, DV):
    L = proj_main.shape[0]
    nc = L // GLA_CHUNK
    cps = math.gcd(GLA_STEP_CHUNKS, nc)
    nh = DK // GLA_HK
    scale = GLA_HK ** -0.5

    def body(q_ref, k_ref, v_ref, z_ref, gl_ref, gu_ref, gb_ref, nw_ref, _yc_ref, y_ref, sp_ref, st_ref):
        n = pl.program_id(0)

        @pl.when(n == 0)
        def _():
            st_ref[...] = jnp.zeros_like(st_ref)

        pairs = [(sc, h) for sc in range(cps) for h in range(nh)]
        rows = lambda sc: slice(sc * GLA_CHUNK, (sc + 1) * GLA_CHUNK)
        kcol = lambda h: slice(h * GLA_HK, (h + 1) * GLA_HK)
        vcol = lambda h: slice(h * GLA_HV, (h + 1) * GLA_HV)
        gates = [_gla_gates(gl_ref[rows(sc), :], gu_ref, gb_ref) for sc in range(cps)]
        qe, dec, o_in, kv = {}, {}, {}, {}
        for sc, h in pairs:
            _, b, b_last, mask = gates[sc]
            bh, bl = b[:, kcol(h)], b_last[:, kcol(h)]
            qe[sc, h] = (q_ref[rows(sc), kcol(h)] * scale) * jnp.exp(bh)
            kh = k_ref[rows(sc), kcol(h)]
            vh = v_ref[rows(sc), vcol(h)]
            attn = jnp.where(mask, _dot(qe[sc, h], kh * jnp.exp(-bh), NT), 0.0)
            o_in[sc, h] = _dot(attn, vh)
            kv[sc, h] = _dot(vh, kh * jnp.exp(bl - bh), TN)
            dec[sc, h] = jnp.exp(bl)
        for sc, h in pairs:
            st = st_ref[h]
            sp_ref[sc, h] = st
            o = o_in[sc, h] + _dot(qe[sc, h], st, NT)
            st_ref[h] = dec[sc, h] * st + kv[sc, h]
            r = lax.rsqrt(jnp.mean(o * o, axis=-1, keepdims=True) + EPS)
            z = z_ref[rows(sc), vcol(h)]
            y_ref[rows(sc), vcol(h)] = (o * r * nw_ref[...] * (z * _sigmoid(z))).astype(BF16)

    c = cps * GLA_CHUNK
    return pl.pallas_call(
        body, name="gla_fwd", grid=(nc // cps,),
        in_specs=_gla_specs(DS, DK, DV, c, lambda n: n) + [
            pl.BlockSpec((c, LANES), lambda n: (n, 0)),
            pl.BlockSpec((LANES, DK), lambda n: (0, 0)),
            pl.BlockSpec((1, DK), lambda n: (0, 0)),
            pl.BlockSpec((1, GLA_HV), lambda n: (0, 0)),
            pl.BlockSpec(memory_space=pl.ANY),
        ],
        out_specs=[pl.BlockSpec((c, DV), lambda n: (n, DS // DV)),
                   pl.BlockSpec((cps, nh, GLA_HV, GLA_HK), lambda n: (n, 0, 0, 0))],
        input_output_aliases={8: 0},
        out_shape=[jax.ShapeDtypeStruct(ycat.shape, BF16), jax.ShapeDtypeStruct((nc, nh, GLA_HV, GLA_HK), F32)],
        scratch_shapes=[pltpu.VMEM((nh, GLA_HV, GLA_HK), F32)],
        compiler_params=pltpu.CompilerParams(dimension_semantics=("arbitrary",)),
    )(proj_main, proj_main, proj_main, proj_main, proj_low, gate_up_pad, gate_bias, norm_w, ycat)


def _gla_bwd(d_ycat, proj_main, proj_low, s_prev, gate_up_pad, gate_bias, norm_w, DS, DK, DV):
    L = proj_main.shape[0]
    nc = L // GLA_CHUNK
    cps = math.gcd(GLA_STEP_CHUNKS, nc)
    nh = DK // GLA_HK
    scale = GLA_HK ** -0.5

    def body(dy_ref, q_ref, k_ref, v_ref, z_ref, gl_ref, sp_ref, gu_ref, gb_ref, nw_ref,
             dg_ref, da_ref, gnw_ref, ggb_ref, dst_ref):
        n = pl.program_id(0)

        @pl.when(n == 0)
        def _():
            dst_ref[...] = jnp.zeros_like(dst_ref)
            gnw_ref[...] = jnp.zeros_like(gnw_ref)
            ggb_ref[...] = jnp.zeros_like(ggb_ref)

        last_row = lax.broadcasted_iota(jnp.int32, (GLA_CHUNK, GLA_HK), 0) == GLA_CHUNK - 1
        ri = lax.broadcasted_iota(jnp.int32, (GLA_CHUNK, GLA_CHUNK), 0)
        ci = lax.broadcasted_iota(jnp.int32, (GLA_CHUNK, GLA_CHUNK), 1)
        upper = (ci >= ri).astype(F32)
        nw = nw_ref[...]
        for sc in reversed(range(cps)):
            rs = slice(sc * GLA_CHUNK, (sc + 1) * GLA_CHUNK)
            a, b, b_last, mask = _gla_gates(gl_ref[rs, :], gu_ref, gb_ref)
            for h in range(nh):
                ks = slice(h * GLA_HK, (h + 1) * GLA_HK)
                vs = slice(h * GLA_HV, (h + 1) * GLA_HV)
                bh, bl = b[:, ks], b_last[:, ks]
                e = jnp.exp(bh)
                einv = jnp.exp(-bh)
                etail = jnp.exp(bl - bh)
                dec = jnp.exp(bl)
                qe = (q_ref[rs, ks] * scale) * e
                kh = k_ref[rs, ks]
                ke = kh * einv
                ktail = kh * etail
                vh = v_ref[rs, vs]
                st = sp_ref[sc, h]
                dst = dst_ref[h]
                attn = jnp.where(mask, _dot(qe, ke, NT), 0.0)
                o = _dot(attn, vh) + _dot(qe, st, NT)
                r = lax.rsqrt(jnp.mean(o * o, axis=-1, keepdims=True) + EPS)
                nrm = o * r
                z = z_ref[rs, vs]
                sz = _sigmoid(z)
                dy = dy_ref[rs, vs]
                dg_ref[rs, 2 * DK + DV + h * GLA_HV:2 * DK + DV + (h + 1) * GLA_HV] = (
                    dy * nrm * nw * (sz * (1.0 + z * (1.0 - sz)))).astype(BF16)
                d_on = dy * (z * sz)
                gnw_ref[...] += jnp.sum(d_on * nrm, axis=0, keepdims=True)
                d_n = d_on * nw
                d_o = r * (d_n - nrm * jnp.mean(d_n * nrm, axis=-1, keepdims=True))
                d_attn = jnp.where(mask, _dot(d_o, vh, NT), 0.0)
                dg_ref[rs, 2 * DK + h * GLA_HV:2 * DK + (h + 1) * GLA_HV] = (
                    _dot(attn, d_o, TN) + _dot(ktail, dst, NT)).astype(BF16)
                d_qe = _dot(d_attn, ke) + _dot(d_o, st)
                d_ke = _dot(d_attn, qe, TN)
                d_kt = _dot(vh, dst)
                d_dec = jnp.sum(dst * st, axis=0, keepdims=True)
                dst_ref[h] = dec * dst + _dot(d_o, qe, TN)
                dg_ref[rs, ks] = (d_qe * scale * e).astype(BF16)
                dg_ref[rs, DK + h * GLA_HK:DK + (h + 1) * GLA_HK] = (d_ke * einv + d_kt * etail).astype(BF16)
                d_bl = jnp.sum(d_kt * ktail, axis=0, keepdims=True) + d_dec * dec
                d_b = d_qe * qe - d_ke * ke - d_kt * ktail + jnp.where(last_row, d_bl, 0.0)
                d_lg = _dot_hi(upper, d_b)
                d_a = d_lg * (1.0 / GLA_TAU) * _sigmoid(-a[:, ks])
                ggb_ref[:, ks] += jnp.sum(d_a, axis=0, keepdims=True)
                da_ref[rs, ks] = d_a.astype(BF16)

    c = cps * GLA_CHUNK
    ns = nc // cps
    rn = lambda n: ns - 1 - n
    return pl.pallas_call(
        body, name="gla_bwd", grid=(ns,),
        in_specs=[pl.BlockSpec((c, DV), lambda n: (rn(n), DS // DV))] + _gla_specs(DS, DK, DV, c, rn) + [
            pl.BlockSpec((c, LANES), lambda n: (rn(n), 0)),
            pl.BlockSpec((cps, nh, GLA_HV, GLA_HK), lambda n: (rn(n), 0, 0, 0)),
            pl.BlockSpec((LANES, DK), lambda n: (0, 0)),
            pl.BlockSpec((1, DK), lambda n: (0, 0)),
            pl.BlockSpec((1, GLA_HV), lambda n: (0, 0)),
        ],
        out_specs=[pl.BlockSpec((c, 2 * DK + 2 * DV), lambda n: (rn(n), 0)),
                   pl.BlockSpec((c, DK), lambda n: (rn(n), 0)),
                   pl.BlockSpec((1, GLA_HV), lambda n: (0, 0)), pl.BlockSpec((1, DK), lambda n: (0, 0))],
        out_shape=[jax.ShapeDtypeStruct((L, 2 * DK + 2 * DV), BF16),
                   jax.ShapeDtypeStruct((L, DK), BF16),
                   jax.ShapeDtypeStruct((1, GLA_HV), F32), jax.ShapeDtypeStruct((1, DK), F32)],
        scratch_shapes=[pltpu.VMEM((nh, GLA_HV, GLA_HK), F32)],
        compiler_params=pltpu.CompilerParams(dimension_semantics=("arbitrary",)),
    )(d_ycat, proj_main, proj_main, proj_main, proj_main, proj_low, s_prev, gate_up_pad, gate_bias, norm_w)


def _adamw_math(w, g, m, v):
    c1 = 1.0 - ADAM_B1 ** ADAM_STEP
    c2 = 1.0 - ADAM_B2 ** ADAM_STEP
    m_ = ADAM_B1 * m + (1.0 - ADAM_B1) * g
    v_ = ADAM_B2 * v + (1.0 - ADAM_B2) * (g * g)
    return -ADAM_LR * ((m_ / c1) / (jnp.sqrt(v_ / c2) + ADAM_EPS) + ADAM_WD * w), m_, v_


def _adamw_small(g_row, g_a, g_bc, ws, ms, vs):
    n = len(ws)
    nvec = n - 6

    def body(*refs):
        grow_ref, ga_ref, gbc_ref = refs[:3]
        w_refs, m_refs, v_refs = refs[3:3 + n], refs[3 + n:3 + 2 * n], refs[3 + 2 * n:3 + 3 * n]
        outs = refs[3 + 3 * n:]
        off = 0
        for i in range(n):
            if i < nvec:
                width = ws[i].shape[1]
                g = grow_ref[:, off:off + width]
                off += width
            elif i < nvec + 2:
                g = ga_ref[i - nvec]
            else:
                g = gbc_ref[i - nvec - 2]
            d, m_, v_ = _adamw_math(w_refs[i][...], g, m_refs[i][...], v_refs[i][...])
            outs[i][...] = g
            outs[n + i][...] = d
            outs[2 * n + i][...] = m_
            outs[3 * n + i][...] = v_

    vm = pl.BlockSpec(memory_space=pltpu.VMEM)
    outs = pl.pallas_call(
        body, name="adamw_small",
        in_specs=[vm] * (3 + 3 * n), out_specs=[vm] * (4 * n),
        out_shape=[jax.ShapeDtypeStruct(w.shape, F32) for w in ws] * 4,
    )(g_row, g_a, g_bc, *ws, *ms, *vs)
    return [outs[k * n:(k + 1) * n] for k in range(4)]


def _my_pos():
    return lax.axis_index("x"), lax.axis_index("y"), lax.axis_index("c")


def _late_gather_copies(srcs, lands, send_sems, recv_sems):
    x, y, c = _my_pos()
    me = 2 * x + y
    copies = []
    for d in (1, 2, 3):
        to = (x ^ (d >> 1), y ^ (d & 1), c)
        for a in range(len(srcs)):
            hrows = srcs[a].shape[0] // 2
            rows = pl.ds(c * hrows, hrows)
            copies.append(pltpu.make_async_remote_copy(
                src_ref=srcs[a].at[rows, :], dst_ref=lands[a].at[me, rows, :], send_sem=send_sems.at[3 * a + d - 1],
                recv_sem=recv_sems.at[3 * a + d - 1], device_id=to, device_id_type=MESH))
    return copies


def _late_gather_start(shards, after, name):
    n = len(shards)

    def body(*refs):
        srcs, lands = refs[:n], refs[n:2 * n]
        send_sems, recv_sems = refs[2 * n + 1], refs[2 * n + 2]
        token = refs[-1]
        for cp in _late_gather_copies(srcs, lands, send_sems, recv_sems):
            cp.start()
        token[...] = jnp.zeros_like(token)

    hbm = pl.BlockSpec(memory_space=pltpu.HBM)
    sem = pl.BlockSpec(memory_space=pltpu.SEMAPHORE)
    outs = pl.pallas_call(
        body, name=name,
        in_specs=[hbm] * (2 * n) + [pl.BlockSpec(memory_space=pl.ANY)],
        out_specs=[sem, sem] + [hbm] * (2 * n) + [pl.BlockSpec(memory_space=pltpu.VMEM)],
        out_shape=[pltpu.SemaphoreType.DMA((3 * n,)), pltpu.SemaphoreType.DMA((3 * n,))]
        + [pltpu.HBM(s.shape, s.dtype) for s in shards]
        + [pltpu.HBM((4,) + s.shape, s.dtype) for s in shards]
        + [jax.ShapeDtypeStruct((SUBLANES, LANES), F32)],
        input_output_aliases={i: 2 + i for i in range(2 * n)},
        compiler_params=pltpu.CompilerParams(has_side_effects=pltpu.SideEffectType.DATAFLOW_SIDE_EFFECTING),
    )(*[pltpu.with_memory_space_constraint(s, pltpu.HBM) for s in shards],
      *[pltpu.with_memory_space_constraint(lax.empty((4,) + s.shape, s.dtype), pltpu.HBM) for s in shards], after)
    return outs[0], outs[1], outs[2:2 + n], outs[2 + n:2 + 2 * n], outs[-1]


def _late_gather_wait(send_sems, recv_sems, shards, lands, after, name):
    n = len(shards)

    def body(*refs):
        src_refs, land_refs = refs[:n], refs[n:2 * n]
        ssem, rsem = refs[2 * n], refs[2 * n + 1]
        for cp in _late_gather_copies(src_refs, land_refs, ssem, rsem):
            cp.wait_send()
            cp.wait_recv()

    hbm = pl.BlockSpec(memory_space=pltpu.HBM)
    sem = pl.BlockSpec(memory_space=pltpu.SEMAPHORE)
    outs = pl.pallas_call(
        body, name=name,
        in_specs=[hbm] * (2 * n) + [sem, sem] + [pl.BlockSpec(memory_space=pl.ANY)] * len(after),
        out_specs=[hbm] * (2 * n),
        out_shape=[pltpu.HBM(s.shape, s.dtype) for s in shards] + [pltpu.HBM(p.shape, p.dtype) for p in lands],
        input_output_aliases={i: i for i in range(2 * n)},
        compiler_params=pltpu.CompilerParams(has_side_effects=pltpu.SideEffectType.DATAFLOW_SIDE_EFFECTING),
    )(*shards, *lands, send_sems, recv_sems, *after)
    return outs[n:]


def _late_gather_pair(lands, name):
    n = len(lands)

    def body(*refs):
        outs = refs[n:2 * n]
        send_sems, recv_sems = refs[2 * n:]
        x, y, c = _my_pos()

        def copy(a, d, half):
            chip = 2 * (x ^ (d >> 1)) + (y ^ (d & 1))
            hrows = lands[a].shape[1] // 2
            sl = outs[a].at[chip, pl.ds(half * hrows, hrows), :]
            return pltpu.make_async_remote_copy(src_ref=sl, dst_ref=sl, send_sem=send_sems.at[3 * a + d - 1],
                                                recv_sem=recv_sems.at[3 * a + d - 1], device_id=(x, y, 1 - c),
                                                device_id_type=MESH)

        pairs = [(a, d) for d in (1, 2, 3) for a in range(n)]
        for a, d in pairs:
            copy(a, d, c).start()
        for a, d in pairs:
            copy(a, d, c).wait_send()
            copy(a, d, 1 - c).wait_recv()

    hbm = pl.BlockSpec(memory_space=pltpu.HBM)
    return pl.pallas_call(
        body, name=name, in_specs=[hbm] * n, out_specs=[hbm] * n,
        out_shape=[jax.ShapeDtypeStruct(p.shape, p.dtype) for p in lands],
        input_output_aliases={i: i for i in range(n)},
        scratch_shapes=[pltpu.SemaphoreType.DMA((3 * n,)), pltpu.SemaphoreType.DMA((3 * n,))],
    )(*lands)


def _pair_exchange(gs):
    n = len(gs)

    def body(*refs):
        ins, outs = refs[:n], refs[n:2 * n]
        send_sems, recv_sems = refs[2 * n:]
        x, y, c = _my_pos()
        sent = []
        for a in range(n):
            hrows = gs[a].shape[1] // 2
            cp = pltpu.make_async_remote_copy(
                src_ref=ins[a].at[:, pl.ds((1 - c) * hrows, hrows), :], dst_ref=outs[a], send_sem=send_sems.at[a],
                recv_sem=recv_sems.at[a], device_id=(x, y, 1 - c), device_id_type=MESH)
            cp.start()
            sent.append(cp)
        for cp in sent:
            cp.wait()

    hbm = pl.BlockSpec(memory_space=pltpu.HBM)
    return pl.pallas_call(
        body, name="grad_pair_exchange", in_specs=[hbm] * n, out_specs=[hbm] * n,
        out_shape=[jax.ShapeDtypeStruct((g.shape[0], g.shape[1] // 2, g.shape[2]), g.dtype) for g in gs],
        scratch_shapes=[pltpu.SemaphoreType.DMA((n,)), pltpu.SemaphoreType.DMA((n,))],
    )(*gs)


def _pair_add(g, got, c_arr, name):
    nk, rows2, cols = g.shape
    hrows = rows2 // 2
    tr = _blk(hrows, 256, 2 * SUBLANES)
    nb = hrows // tr

    def body(c_ref, a_ref, b_ref, o_ref):
        o_ref[...] = (a_ref[...].astype(F32) + b_ref[...].astype(F32)).astype(o_ref.dtype)

    return pl.pallas_call(
        body, name=name,
        grid_spec=pltpu.PrefetchScalarGridSpec(
            num_scalar_prefetch=1, grid=(nk, nb),
            in_specs=[pl.BlockSpec((1, tr, cols), lambda k, i, c_ref: (k, c_ref[0] * nb + i, 0)),
                      pl.BlockSpec((1, tr, cols), lambda k, i, c_ref: (k, i, 0))],
            out_specs=pl.BlockSpec((1, tr, cols), lambda k, i, c_ref: (k, i, 0))),
        out_shape=jax.ShapeDtypeStruct((nk, hrows, cols), g.dtype),
        compiler_params=pltpu.CompilerParams(dimension_semantics=("parallel", "parallel")),
    )(c_arr, g, got)


def _chip_scatter_copies(srcs, lands, send_sems, recv_sems):
    x, y, c = _my_pos()
    copies = []
    for d in (1, 2, 3):
        tx, ty = x ^ (d >> 1), y ^ (d & 1)
        for a in range(len(srcs)):
            copies.append(pltpu.make_async_remote_copy(
                src_ref=srcs[a].at[2 * tx + ty], dst_ref=lands[a].at[d - 1], send_sem=send_sems.at[3 * a + d - 1],
                recv_sem=recv_sems.at[3 * a + d - 1], device_id=(tx, ty, c), device_id_type=MESH))
    return copies


def _chip_scatter_start(pss):
    n = len(pss)

    def body(*refs):
        srcs, lands = refs[:n], refs[n:2 * n]
        send_sems, recv_sems = refs[2 * n], refs[2 * n + 1]
        token = refs[-1]
        for cp in _chip_scatter_copies(srcs, lands, send_sems, recv_sems):
            cp.start()
        token[...] = jnp.zeros_like(token)

    hbm = pl.BlockSpec(memory_space=pltpu.HBM)
    sem = pl.BlockSpec(memory_space=pltpu.SEMAPHORE)
    land_shapes = [(3,) + p.shape[1:] for p in pss]
    outs = pl.pallas_call(
        body, name="grad_chip_scatter_start",
        in_specs=[hbm] * (2 * n),
        out_specs=[sem, sem] + [hbm] * (2 * n) + [pl.BlockSpec(memory_space=pltpu.VMEM)],
        out_shape=[pltpu.SemaphoreType.DMA((3 * n,)), pltpu.SemaphoreType.DMA((3 * n,))]
        + [pltpu.HBM(p.shape, p.dtype) for p in pss]
        + [pltpu.HBM(s, p.dtype) for s, p in zip(land_shapes, pss)]
        + [jax.ShapeDtypeStruct((SUBLANES, LANES), F32)],
        input_output_aliases={i: 2 + i for i in range(2 * n)},
        compiler_params=pltpu.CompilerParams(has_side_effects=pltpu.SideEffectType.DATAFLOW_SIDE_EFFECTING),
    )(*[pltpu.with_memory_space_constraint(p, pltpu.HBM) for p in pss],
      *[pltpu.with_memory_space_constraint(lax.empty(s, p.dtype), pltpu.HBM) for s, p in zip(land_shapes, pss)])
    return outs[0], outs[1], outs[2:2 + n], outs[2 + n:2 + 2 * n], outs[-1]


def _chip_scatter_wait(send_sems, recv_sems, srcs, lands, after):
    n = len(srcs)

    def body(*refs):
        src_refs, land_refs = refs[:n], refs[n:2 * n]
        ssem, rsem = refs[2 * n], refs[2 * n + 1]
        for cp in _chip_scatter_copies(src_refs, land_refs, ssem, rsem):
            cp.wait_send()
            cp.wait_recv()

    hbm = pl.BlockSpec(memory_space=pltpu.HBM)
    sem = pl.BlockSpec(memory_space=pltpu.SEMAPHORE)
    outs = pl.pallas_call(
        body, name="grad_chip_scatter_wait",
        in_specs=[hbm] * (2 * n) + [sem, sem, pl.BlockSpec(memory_space=pl.ANY)],
        out_specs=[hbm] * (2 * n),
        out_shape=[pltpu.HBM(p.shape, p.dtype) for p in srcs] + [pltpu.HBM(p.shape, p.dtype) for p in lands],
        input_output_aliases={i: i for i in range(2 * n)},
        compiler_params=pltpu.CompilerParams(has_side_effects=pltpu.SideEffectType.DATAFLOW_SIDE_EFFECTING),
    )(*srcs, *lands, send_sems, recv_sems, after)
    return outs[:n], outs[n:]


def _chip_sum(ps, got, me_arr, name):
    _, hrows, cols = ps.shape
    tr = _blk(hrows, 256, 2 * SUBLANES)

    def body(me_ref, p_ref, g_ref, o_ref):
        acc = p_ref[0].astype(F32)
        for s in range(3):
            acc = acc + g_ref[s].astype(F32)
        o_ref[...] = acc

    return pl.pallas_call(
        body, name=name,
        grid_spec=pltpu.PrefetchScalarGridSpec(
            num_scalar_prefetch=1, grid=(hrows // tr,),
            in_specs=[pl.BlockSpec((1, tr, cols), lambda i, me_ref: (me_ref[0], i, 0)),
                      pl.BlockSpec((3, tr, cols), lambda i, me_ref: (0, i, 0))],
            out_specs=pl.BlockSpec((tr, cols), lambda i, me_ref: (i, 0))),
        out_shape=jax.ShapeDtypeStruct((hrows, cols), F32),
        compiler_params=pltpu.CompilerParams(dimension_semantics=("parallel",)),
    )(me_arr, ps, got)


def _pair_swap(halves):
    n = len(halves)

    def body(*refs):
        ins, outs = refs[:n], refs[n:2 * n]
        send_sems, recv_sems = refs[2 * n:]
        x, y, c = _my_pos()
        sent = []
        for a in range(n):
            cp = pltpu.make_async_remote_copy(src_ref=ins[a], dst_ref=outs[a], send_sem=send_sems.at[a], recv_sem=recv_sems.at[a],
                                              device_id=(x, y, 1 - c), device_id_type=MESH)
            cp.start()
            sent.append(cp)
        for cp in sent:
            cp.wait()

    hbm = pl.BlockSpec(memory_space=pltpu.HBM)
    return pl.pallas_call(
        body, name="grad_pair_swap", in_specs=[hbm] * n, out_specs=[hbm] * n,
        out_shape=[jax.ShapeDtypeStruct(h.shape, h.dtype) for h in halves],
        scratch_shapes=[pltpu.SemaphoreType.DMA((n,)), pltpu.SemaphoreType.DMA((n,))],
    )(*halves)


def _adamw_sharded(w, g_own, g_other, m, v, c_arr, name):
    R, C = w.shape
    hrows = R // 2
    tr = _blk(hrows, 256, SUBLANES)
    nbh = hrows // tr
    c1 = 1.0 - ADAM_B1 ** ADAM_STEP
    c2 = 1.0 - ADAM_B2 ** ADAM_STEP

    def body(c_ref, w_ref, go_ref, gx_ref, m_ref, v_ref, g_ref, d_ref, nm_ref, nv_ref):
        mine = (pl.program_id(0) // nbh) == c_ref[0]
        g_ = jnp.where(mine, go_ref[...], gx_ref[...])
        g_ref[...] = g_
        m_ = ADAM_B1 * m_ref[...] + (1.0 - ADAM_B1) * g_
        v_ = ADAM_B2 * v_ref[...] + (1.0 - ADAM_B2) * (g_ * g_)
        nm_ref[...] = m_
        nv_ref[...] = v_
        d_ref[...] = -ADAM_LR * ((m_ / c1) / (jnp.sqrt(v_ / c2) + ADAM_EPS) + ADAM_WD * w_ref[...])

    blk = pl.BlockSpec((tr, C), lambda i, c_ref: (i, 0))
    hblk = pl.BlockSpec((tr, C), lambda i, c_ref: (i % nbh, 0))
    sd = jax.ShapeDtypeStruct((R, C), F32)
    return pl.pallas_call(
        body, name=name,
        grid_spec=pltpu.PrefetchScalarGridSpec(
            num_scalar_prefetch=1, grid=(2 * nbh,),
            in_specs=[blk, hblk, hblk, blk, blk], out_specs=[blk] * 4),
        out_shape=[sd] * 4,
        compiler_params=pltpu.CompilerParams(dimension_semantics=("parallel",)),
    )(c_arr, w, g_own, g_other, m, v)


def _allreduce_small(arrs):
    n = len(arrs)
    rows = [a.shape[-2] // 8 for a in arrs]

    def piece(ref, a, p):
        start = p * rows[a]
        if rows[a] % SUBLANES == 0:
            start = pl.multiple_of(start, SUBLANES)
        return ref.at[..., pl.ds(start, rows[a]), :]

    def body(*refs):
        v_refs, o_refs, got_refs = refs[:n], refs[n:2 * n], refs[2 * n:3 * n]
        send_sems, recv_sems = refs[3 * n:]
        x, y, c = _my_pos()
        me = 4 * x + 2 * y + c

        def peer(d):
            return (x ^ (d >> 2), y ^ ((d >> 1) & 1), c ^ (d & 1))

        def lin(p):
            return 4 * p[0] + 2 * p[1] + p[2]

        sent = []
        for d in range(1, 8):
            to = peer(d)
            for a in range(n):
                cp = pltpu.make_async_remote_copy(
                    src_ref=piece(v_refs[a], a, lin(to)), dst_ref=got_refs[a].at[d],
                    send_sem=send_sems.at[0, d * n + a], recv_sem=recv_sems.at[0, d * n + a], device_id=to, device_id_type=MESH)
                cp.start()
                sent.append(cp)
        for a in range(n):
            acc = piece(v_refs[a], a, me)[...]
            for d in range(1, 8):
                sent[(d - 1) * n + a].wait_recv()
                acc = acc + got_refs[a][d]
            got_refs[a][0] = acc
            piece(o_refs[a], a, me)[...] = acc
        for d in range(1, 8):
            for a in range(n):
                cp = pltpu.make_async_remote_copy(
                    src_ref=got_refs[a].at[0], dst_ref=piece(o_refs[a], a, me),
                    send_sem=send_sems.at[1, d * n + a], recv_sem=recv_sems.at[1, d * n + a], device_id=peer(d), device_id_type=MESH)
                cp.start()
                sent.append(cp)
        for d in range(1, 8):
            for a in range(n):
                pltpu.make_async_remote_copy(
                    src_ref=got_refs[a].at[0], dst_ref=piece(o_refs[a], a, lin(peer(d))),
                    send_sem=send_sems.at[1, d * n + a], recv_sem=recv_sems.at[1, d * n + a], device_id=peer(d),
                    device_id_type=MESH).wait_recv()
        for cp in sent:
            cp.wait_send()

    vm = pl.BlockSpec(memory_space=pltpu.VMEM)
    return pl.pallas_call(
        body, name="allreduce_small", in_specs=[vm] * n, out_specs=[vm] * n,
        out_shape=[jax.ShapeDtypeStruct(a.shape, F32) for a in arrs],
        scratch_shapes=[pltpu.VMEM((8,) + a.shape[:-2] + (r, a.shape[-1]), F32) for a, r in zip(arrs, rows)]
        + [pltpu.SemaphoreType.DMA((2, 8 * n)), pltpu.SemaphoreType.DMA((2, 8 * n))],
    )(*arrs)


def kernel(x, pre_norm_w, w_in, s5_A_re, s5_A_im, s5_B_re, s5_B_im, s5_C_re, s5_C_im, s5_D, s5_log_dt, s5_glu_w, s5_glu_b, gla_gate_up, gla_gate_bias, gla_norm_w, w_out, post_norm_w, loss_target, m_pre_norm_w, m_w_in, m_s5_A_re, m_s5_A_im, m_s5_B_re, m_s5_B_im, m_s5_C_re, m_s5_C_im, m_s5_D, m_s5_log_dt, m_s5_glu_w, m_s5_glu_b, m_gla_gate_up, m_gla_gate_bias, m_gla_norm_w, m_w_out, m_post_norm_w, v_pre_norm_w, v_w_in, v_s5_A_re, v_s5_A_im, v_s5_B_re, v_s5_B_im, v_s5_C_re, v_s5_C_im, v_s5_D, v_s5_log_dt, v_s5_glu_w, v_s5_glu_b, v_gla_gate_up, v_gla_gate_bias, v_gla_norm_w, v_w_out, v_post_norm_w):
    names = ["pre_norm_w", "w_in", "s5_A_re", "s5_A_im", "s5_B_re", "s5_B_im", "s5_C_re", "s5_C_im", "s5_D", "s5_log_dt",
             "s5_glu_w", "s5_glu_b", "gla_gate_up", "gla_gate_bias", "gla_norm_w", "w_out", "post_norm_w"]
    W = dict(zip(names, (pre_norm_w, w_in, s5_A_re, s5_A_im, s5_B_re, s5_B_im, s5_C_re, s5_C_im, s5_D, s5_log_dt,
                         s5_glu_w, s5_glu_b, gla_gate_up, gla_gate_bias, gla_norm_w, w_out, post_norm_w)))
    M = dict(zip(names, (m_pre_norm_w, m_w_in, m_s5_A_re, m_s5_A_im, m_s5_B_re, m_s5_B_im, m_s5_C_re, m_s5_C_im, m_s5_D,
                         m_s5_log_dt, m_s5_glu_w, m_s5_glu_b, m_gla_gate_up, m_gla_gate_bias, m_gla_norm_w, m_w_out,
                         m_post_norm_w)))
    V = dict(zip(names, (v_pre_norm_w, v_w_in, v_s5_A_re, v_s5_A_im, v_s5_B_re, v_s5_B_im, v_s5_C_re, v_s5_C_im, v_s5_D,
                         v_s5_log_dt, v_s5_glu_w, v_s5_glu_b, v_gla_gate_up, v_gla_gate_bias, v_gla_norm_w, v_w_out,
                         v_post_norm_w)))
    sharded = ("w_in", "s5_glu_w", "w_out", "gla_gate_up")

    xb = x[0]
    tgt = loss_target[0]
    L, D = xb.shape
    DS = D // 2
    G = DS // S5_GROUP
    P = S5_STATE
    NB = DS // S5_COLS
    DV = D - DS
    DK = DV // 2
    WM = 2 * DS + 2 * DK + 2 * DV
    nsh = w_in.shape[2]

    chip = 2 * lax.axis_index("x") + lax.axis_index("y")
    own = [w_in[0].astype(BF16), s5_glu_w[0].astype(BF16), w_out[0].astype(BF16), gla_gate_up[0]]
    fill = lambda g, o: lax.dynamic_update_index_in_dim(g, o, chip, 0)
    win_ss, win_rs, win_src, win_lands, win_token = _late_gather_start(own[:1], pre_norm_w, "w_in_gather_start")
    h = _prenorm_fwd(xb, pre_norm_w, win_token)

    b_view = lambda t: jnp.transpose(t[0], (0, 2, 1)).reshape(G * S5_GROUP, P)
    b_back = lambda t: jnp.transpose(t.reshape(G, S5_GROUP, P), (0, 2, 1))[None]
    c_view = lambda t: t[0].reshape(G * S5_GROUP, P)
    c_back = lambda t: t.reshape(1, G, S5_GROUP, P)
    small = ["pre_norm_w", "post_norm_w", "s5_D", "s5_glu_b", "gla_gate_bias", "gla_norm_w", "s5_log_dt",
             "s5_A_re", "s5_A_im", "s5_B_re", "s5_B_im", "s5_C_re", "s5_C_im"]
    view = {n: (lambda t: t) for n in small[:7]}
    back = dict(view)
    view.update(s5_A_re=lambda t: t[0], s5_A_im=lambda t: t[0], s5_B_re=b_view, s5_B_im=b_view, s5_C_re=c_view, s5_C_im=c_view)
    back.update(s5_A_re=lambda t: t[None], s5_A_im=lambda t: t[None], s5_B_re=b_back, s5_B_im=b_back, s5_C_re=c_back,
                s5_C_im=c_back)
    Wv = {n: view[n](W[n]) for n in small}
    bbd_re, bbd_im, ct_re, ct_im, tab, ptab = _s5_prep_fwd(
        Wv["s5_A_re"], Wv["s5_A_im"], s5_log_dt, Wv["s5_B_re"], Wv["s5_B_im"], Wv["s5_C_re"], Wv["s5_C_im"],
        h, _blk(L, 512, SUBLANES) // SUBLANES)
    dvec = s5_D

    for d_ in (W, M, V):
        d_["w_in"], _ = lax.optimization_barrier((d_["w_in"], win_token))
    g_win = _late_gather_wait(win_ss, win_rs, win_src, win_lands,
                              [tab, W["w_in"][0], M["w_in"][0], V["w_in"][0]], "w_in_gather_wait")
    g_win = fill(_late_gather_pair(g_win, "w_in_gather_pair")[0], own[0])
    w_full = jnp.moveaxis(g_win, 0, 1).reshape(D, 4 * nsh)
    w_main = w_full[:, :WM]
    w_low = jnp.pad(w_full[:, WM:], ((0, 0), (0, LANES - GLA_RANK)))
    late_ss, late_rs, late_src, late_lands, late_token = _late_gather_start(own[1:], g_win, "late_gather_start")
    proj_main, proj_low = _in_proj(h, w_main, w_low, late_token)
    y_pre, s_re, s_im = _s5_scan_fwd(proj_main, bbd_re, bbd_im, ct_re, ct_im, dvec, tab, ptab, DS)
    late = _late_gather_wait(late_ss, late_rs, late_src, late_lands, [y_pre], "late_gather_wait")
    late = _late_gather_pair(late, "late_gather_pair")
    g_glu, g_wout, g_gup = [fill(g, o) for g, o in zip(late, own[1:])]
    glu_w = g_glu.reshape(DS, DS)
    wout = g_wout.reshape(D, D)
    gup = jnp.moveaxis(g_gup, 0, 1).reshape(GLA_RANK, DK)
    gup_pad = jnp.pad(gup, ((0, LANES - GLA_RANK), (0, 0))).astype(BF16)
    ycat, t_pre = _s5_post_fwd(y_pre, proj_main, glu_w, s5_glu_b, DS)
    ycat, s_prev = _gla_fwd(proj_main, proj_low, gup_pad, gla_gate_bias, gla_norm_w, ycat, DS, DK, DV)
    mixed = _mm(ycat, wout, name="out_proj")
    loss11, d_mixed, dout, g_post_w = _post_fwd_bwd(mixed, xb, tgt, post_norm_w)

    d_ycat = _mm(d_mixed, wout, tb=True, name="out_proj_dx")
    g_wout_full = _mm(ycat, d_mixed, ta=True, out_dtype=BF16, name="out_proj_dw")
    d_ypre, d_s5, d_t, y1, g_glu_b = _s5_post_bwd(d_ycat, y_pre, proj_main, t_pre, glu_w, DS)
    g_glu_full = _mm(y1, d_t, ta=True, out_dtype=BF16, name="glu_dw")
    d_s5, g_D, gct_re, gct_im, gbbd_re, gbbd_im, gab_re, gab_im = _s5_scan_bwd(
        d_ypre, proj_main, s_re, s_im, bbd_re, bbd_im, ct_re, ct_im, dvec, tab, ptab, d_s5, DS)
    d_gla, d_a, g_norm_w, g_gate_bias = _gla_bwd(
        d_ycat, proj_main, proj_low, s_prev, gup_pad, gla_gate_bias, gla_norm_w, DS, DK, DV)
    d_low = _mm(d_a, gup_pad, tb=True, out_dtype=BF16, name="gate_dx")
    g_gup_pad = _mm(proj_low, d_a, ta=True, name="gate_dw")
    g_wmain, g_wlow = _in_proj_dw(h, d_s5, d_gla, d_low)

    g_win_full = jnp.concatenate([g_wmain, g_wlow[:, :GLA_RANK]], axis=1)
    gs = [jnp.moveaxis(g_win_full.reshape(D, 4, nsh), 1, 0),
          g_glu_full.reshape(4, DS // 4, DS),
          g_wout_full.reshape(4, D // 4, D),
          jnp.moveaxis(g_gup_pad[:GLA_RANK].reshape(GLA_RANK, 4, DK // 4), 1, 0)]
    c_arr = lax.axis_index("c").astype(jnp.int32).reshape(1)
    me_arr = chip.astype(jnp.int32).reshape(1)
    got = _pair_exchange(gs)
    pss = [_pair_add(g, r, c_arr, "grad_pair_add_" + n) for n, g, r in zip(sharded, gs, got)]
    send_sems, recv_sems, pss, lands, token = _chip_scatter_start(pss)

    dh = _in_proj_dx(d_s5, d_gla, d_low, w_main, w_low, token)
    grad_x, g_pre_w = _prenorm_bwd(xb, dh, dout, pre_norm_w)
    pss, rcv = _chip_scatter_wait(send_sems, recv_sems, pss, lands, g_pre_w)

    g_a, g_bc, g_ldt = _s5_prep_bwd(Wv["s5_A_re"], Wv["s5_A_im"], s5_log_dt, Wv["s5_B_re"], Wv["s5_B_im"],
                                    gbbd_re, gbbd_im, gct_re, gct_im, gab_re, gab_im)

    loss = lax.psum(loss11[0, 0], ("x", "y", "c"))

    g_vecs = jnp.concatenate([g_pre_w, g_post_w, g_D, g_glu_b, g_gate_bias, g_norm_w, g_ldt], axis=1)
    lanes_pad = -g_vecs.shape[1] % (8 * SUBLANES * LANES)
    g_vecs = jnp.pad(g_vecs, ((0, 0), (0, lanes_pad))).reshape(-1, LANES)
    r_vecs, r_a, r_bc = _allreduce_small([g_vecs, g_a, g_bc])
    outs4 = _adamw_small(r_vecs.reshape(1, -1), r_a, r_bc, [Wv[n] for n in small],
                         [view[n](M[n]) for n in small], [view[n](V[n]) for n in small])
    G_out, D_out, M_out, V_out = [{n: back[n](t) for n, t in zip(small, o)} for o in outs4]

    halves = [_chip_sum(p, r, me_arr, "grad_chip_sum_" + n) for n, p, r in zip(sharded, pss, rcv)]
    others = _pair_swap(halves)
    for n, g_own, g_other in zip(sharded, halves, others):
        g_, d_, m_, v_ = _adamw_sharded(W[n][0], g_own, g_other, M[n][0], V[n][0], c_arr, "adamw_" + n)
        G_out[n], D_out[n], M_out[n], V_out[n] = g_[None], d_[None], m_[None], v_[None]

    return (loss, grad_x[None], *[G_out[n] for n in names], *[D_out[n] for n in names],
            *[M_out[n] for n in names], *[V_out[n] for n in names])
```

```python
import functools
import math

import jax
import jax.numpy as jnp
from jax import lax
from jax.experimental import pallas as pl
from jax.experimental.pallas import tpu as pltpu

F32 = jnp.float32
BF16 = jnp.bfloat16
HI = lax.Precision.HIGHEST
MESH = pl.DeviceIdType.MESH

EPS = 1e-6
S5_GROUP = 16
S5_STATE = 64
GLA_HK = 128
GLA_HV = 256
GLA_RANK = 16
GLA_TAU = 16.0
GLA_CHUNK = 64
GLA_STEP_CHUNKS = 4
LANES = 128
SUBLANES = 8
S5_COLS = 128
S5_LANES = (S5_COLS // S5_GROUP) * S5_STATE

ADAM_LR = 0.001
ADAM_B1 = 0.9
ADAM_B2 = 0.999
ADAM_EPS = 1e-08
ADAM_WD = 0.01
ADAM_STEP = 10

GELU_K = math.sqrt(2.0 / math.pi)
GELU_C = 0.044715


def _blk(n, pref, unit=LANES):
    best = None
    b = unit
    while b <= min(n, pref):
        if n % b == 0:
            best = b
        b += unit
    return best if best is not None else n


def _dot(a, b, dn=(((1,), (0,)), ((), ()))):
    return lax.dot_general(a.astype(BF16), b.astype(BF16), dn, preferred_element_type=F32)


def _dot_hi(a, b, dn=(((1,), (0,)), ((), ()))):
    return lax.dot_general(a, b, dn, precision=HI, preferred_element_type=F32)


NN = (((1,), (0,)), ((), ()))
NT = (((1,), (1,)), ((), ()))
TN = (((0,), (0,)), ((), ()))


def _sigmoid(x):
    return 1.0 / (1.0 + jnp.exp(-x))


def _gelu(y):
    return 0.5 * y * (1.0 + jnp.tanh(GELU_K * (y + GELU_C * y * y * y)))


def _gelu_grad(y):
    th = jnp.tanh(GELU_K * (y + GELU_C * y * y * y))
    return 0.5 * (1.0 + th) + 0.5 * y * (1.0 - th * th) * GELU_K * (1.0 + 3.0 * GELU_C * y * y)


def _mm(a, b, *, name, ta=False, tb=False, out_dtype=F32, bm=1024, bn=1024, bk=2048):
    if ta:
        K, M = a.shape
    else:
        M, K = a.shape
    if tb:
        N, K2 = b.shape
    else:
        K2, N = b.shape
    assert K == K2, (a.shape, b.shape, ta, tb)
    bm, bn, bk = _blk(M, bm), _blk(N, bn), _blk(K, bk)
    nk = K // bk
    dn = (((0 if ta else 1,), (1 if tb else 0,)), ((), ()))

    def body(a_ref, b_ref, o_ref, *acc):
        if nk == 1:
            o_ref[...] = _dot(a_ref[...], b_ref[...], dn).astype(out_dtype)
            return
        acc_ref, = acc
        k = pl.program_id(2)

        @pl.when(k == 0)
        def _():
            acc_ref[...] = jnp.zeros_like(acc_ref)

        acc_ref[...] += _dot(a_ref[...], b_ref[...], dn)

        @pl.when(k == nk - 1)
        def _():
            o_ref[...] = acc_ref[...].astype(out_dtype)

    a_spec = pl.BlockSpec((bk, bm), lambda i, j, k: (k, i)) if ta else pl.BlockSpec((bm, bk), lambda i, j, k: (i, k))
    b_spec = pl.BlockSpec((bn, bk), lambda i, j, k: (j, k)) if tb else pl.BlockSpec((bk, bn), lambda i, j, k: (k, j))
    return pl.pallas_call(
        body,
        name=name,
        grid=(M // bm, N // bn, nk),
        in_specs=[a_spec, b_spec],
        out_specs=pl.BlockSpec((bm, bn), lambda i, j, k: (i, j)),
        out_shape=jax.ShapeDtypeStruct((M, N), out_dtype),
        scratch_shapes=[pltpu.VMEM((bm, bn), F32)] if nk > 1 else [],
        compiler_params=pltpu.CompilerParams(dimension_semantics=("parallel", "parallel", "arbitrary")),
    )(a, b)


def _in_proj(h, w_main, w_low, after):
    M, K = h.shape
    N = w_main.shape[1]
    bm, bn = _blk(M, 1024), _blk(N, 1024)

    def body(h_ref, w_ref, wl_ref, _after_ref, o_ref, ol_ref):
        hv = h_ref[...]
        o_ref[...] = _dot(hv, w_ref[...])

        @pl.when(pl.program_id(1) == 0)
        def _():
            ol_ref[...] = _dot(hv, wl_ref[...])

    return pl.pallas_call(
        body, name="in_proj", grid=(M // bm, N // bn),
        in_specs=[pl.BlockSpec((bm, K), lambda i, j: (i, 0)), pl.BlockSpec((K, bn), lambda i, j: (0, j)),
                  pl.BlockSpec((K, LANES), lambda i, j: (0, 0)), pl.BlockSpec(memory_space=pl.ANY)],
        out_specs=[pl.BlockSpec((bm, bn), lambda i, j: (i, j)), pl.BlockSpec((bm, LANES), lambda i, j: (i, 0))],
        out_shape=[jax.ShapeDtypeStruct((M, N), F32), jax.ShapeDtypeStruct((M, LANES), F32)],
        compiler_params=pltpu.CompilerParams(dimension_semantics=("parallel", "arbitrary")),
    )(h, w_main, w_low, after)


def _in_proj_dx(a1, a2, al, b, bl, after, *, bm=1024, bn=1024, bk=2048):
    M, K1 = a1.shape
    K2 = a2.shape[1]
    N = b.shape[0]
    bm, bn = _blk(M, bm), _blk(N, bn)
    bk = _blk(math.gcd(K1, K2), bk)
    nk1, nk = K1 // bk, (K1 + K2) // bk

    def body(a1_ref, a2_ref, al_ref, b_ref, bl_ref, _after_ref, o_ref, acc_ref):
        k = pl.program_id(2)

        @pl.when(k == 0)
        def _():
            acc_ref[...] = _dot(al_ref[...], bl_ref[...], NT)

        @pl.when(k < nk1)
        def _():
            acc_ref[...] += _dot(a1_ref[...], b_ref[...], NT)

        @pl.when(k >= nk1)
        def _():
            acc_ref[...] += _dot(a2_ref[...], b_ref[...], NT)

        @pl.when(k == nk - 1)
        def _():
            o_ref[...] = acc_ref[...]

    return pl.pallas_call(
        body, name="in_proj_dx", grid=(M // bm, N // bn, nk),
        in_specs=[pl.BlockSpec((bm, bk), lambda i, j, k: (i, jnp.minimum(k, nk1 - 1))),
                  pl.BlockSpec((bm, bk), lambda i, j, k: (i, jnp.maximum(k - nk1, 0))),
                  pl.BlockSpec((bm, LANES), lambda i, j, k: (i, 0)),
                  pl.BlockSpec((bn, bk), lambda i, j, k: (j, k)),
                  pl.BlockSpec((bn, LANES), lambda i, j, k: (j, 0)),
                  pl.BlockSpec(memory_space=pl.ANY)],
        out_specs=pl.BlockSpec((bm, bn), lambda i, j, k: (i, j)),
        out_shape=jax.ShapeDtypeStruct((M, N), F32),
        scratch_shapes=[pltpu.VMEM((bm, bn), F32)],
        compiler_params=pltpu.CompilerParams(dimension_semantics=("parallel", "parallel", "arbitrary")),
    )(a1, a2, al, b, bl, after)


def _in_proj_dw(a, b1, b2, bl, *, bm=1024, bn=1024, bk=2048):
    K, M = a.shape
    N1, N2 = b1.shape[1], b2.shape[1]
    bm, bk = _blk(M, bm), _blk(K, bk)
    bn = _blk(math.gcd(N1, N2), bn)
    nj1, nj = N1 // bn, (N1 + N2) // bn
    nk = K // bk

    def body(a_ref, b1_ref, b2_ref, bl_ref, o_ref, ol_ref, acc_ref, accl_ref):
        j = pl.program_id(1)
        k = pl.program_id(2)

        @pl.when(k == 0)
        def _():
            acc_ref[...] = jnp.zeros_like(acc_ref)

        @pl.when(j < nj1)
        def _():
            acc_ref[...] += _dot(a_ref[...], b1_ref[...], TN)

        @pl.when(j >= nj1)
        def _():
            acc_ref[...] += _dot(a_ref[...], b2_ref[...], TN)

        @pl.when(k == nk - 1)
        def _():
            o_ref[...] = acc_ref[...].astype(BF16)

        @pl.when(j == 0)
        def _():
            low = _dot(a_ref[...], bl_ref[...], TN)

            @pl.when(k == 0)
            def _():
                accl_ref[...] = low

            @pl.when(k > 0)
            def _():
                accl_ref[...] += low

            @pl.when(k == nk - 1)
            def _():
                ol_ref[...] = accl_ref[...].astype(BF16)

    return pl.pallas_call(
        body, name="in_proj_dw", grid=(M // bm, nj, nk),
        in_specs=[pl.BlockSpec((bk, bm), lambda i, j, k: (k, i)),
                  pl.BlockSpec((bk, bn), lambda i, j, k: (jnp.where(j < nj1, k, nk - 1), jnp.minimum(j, nj1 - 1))),
                  pl.BlockSpec((bk, bn), lambda i, j, k: (jnp.where(j >= nj1, k, 0), jnp.maximum(j - nj1, 0))),
                  pl.BlockSpec((bk, LANES), lambda i, j, k: (jnp.where(j == 0, k, nk - 1), 0))],
        out_specs=[pl.BlockSpec((bm, bn), lambda i, j, k: (i, j)), pl.BlockSpec((bm, LANES), lambda i, j, k: (i, 0))],
        out_shape=[jax.ShapeDtypeStruct((M, N1 + N2), BF16), jax.ShapeDtypeStruct((M, LANES), BF16)],
        scratch_shapes=[pltpu.VMEM((bm, bn), F32), pltpu.VMEM((bm, LANES), F32)],
        compiler_params=pltpu.CompilerParams(dimension_semantics=("parallel", "arbitrary", "arbitrary")),
    )(a, b1, b2, bl)


def _prenorm_fwd(x, w, after):
    L, D = x.shape
    tr = _blk(L, 256, SUBLANES)

    def body(x_ref, w_ref, _after_ref, h_ref):
        xv = x_ref[...]
        r = lax.rsqrt(jnp.mean(xv * xv, axis=-1, keepdims=True) + EPS)
        h_ref[...] = (xv * r * w_ref[...]).astype(BF16)

    return pl.pallas_call(
        body, name="prenorm_fwd", grid=(L // tr,),
        in_specs=[pl.BlockSpec((tr, D), lambda i: (i, 0)), pl.BlockSpec((1, D), lambda i: (0, 0)),
                  pl.BlockSpec(memory_space=pl.ANY)],
        out_specs=pl.BlockSpec((tr, D), lambda i: (i, 0)),
        out_shape=jax.ShapeDtypeStruct((L, D), BF16),
        compiler_params=pltpu.CompilerParams(dimension_semantics=("parallel",)),
    )(x, w, after)


def _post_fwd_bwd(mixed, x, target, w):
    L, D = x.shape
    tr = _blk(L, 256, SUBLANES)
    nsteps = L // tr

    def body(mx_ref, x_ref, t_ref, w_ref, loss_ref, dm_ref, dout_ref, gw_ref, acc_ref):
        i = pl.program_id(0)

        @pl.when(i == 0)
        def _():
            acc_ref[...] = jnp.zeros_like(acc_ref)
            gw_ref[...] = jnp.zeros_like(gw_ref)

        mx = mx_ref[...]
        wv = w_ref[...]
        r = lax.rsqrt(jnp.mean(mx * mx, axis=-1, keepdims=True) + EPS)
        n = mx * r
        err = x_ref[...] + n * wv - t_ref[...]
        acc_ref[...] += jnp.sum(err * err, axis=0, keepdims=True)
        dout = err * (1.0 / D)
        dout_ref[...] = dout
        gw_ref[...] += jnp.sum(dout * n, axis=0, keepdims=True)
        dn = dout * wv
        dm_ref[...] = (r * (dn - n * jnp.mean(dn * n, axis=-1, keepdims=True))).astype(BF16)

        @pl.when(i == nsteps - 1)
        def _():
            loss_ref[...] = jnp.sum(acc_ref[...], axis=-1, keepdims=True) * (0.5 / D)

    row = pl.BlockSpec((tr, D), lambda i: (i, 0))
    vec = pl.BlockSpec((1, D), lambda i: (0, 0))
    return pl.pallas_call(
        body, name="post_fwd_bwd", grid=(nsteps,),
        in_specs=[row, row, row, vec],
        out_specs=[pl.BlockSpec((1, 1), lambda i: (0, 0)), row, row, vec],
        out_shape=[jax.ShapeDtypeStruct((1, 1), F32), jax.ShapeDtypeStruct((L, D), BF16),
                   jax.ShapeDtypeStruct((L, D), F32), jax.ShapeDtypeStruct((1, D), F32)],
        scratch_shapes=[pltpu.VMEM((1, D), F32)],
        compiler_params=pltpu.CompilerParams(dimension_semantics=("arbitrary",)),
    )(mixed, x, target, w)


def _prenorm_bwd(x, dh, dout, w):
    L, D = x.shape
    tr = _blk(L, 256, SUBLANES)

    def body(x_ref, a_ref, dout_ref, w_ref, gx_ref, gw_ref):
        i = pl.program_id(0)

        @pl.when(i == 0)
        def _():
            gw_ref[...] = jnp.zeros_like(gw_ref)

        xv = x_ref[...]
        r = lax.rsqrt(jnp.mean(xv * xv, axis=-1, keepdims=True) + EPS)
        n = xv * r
        dh = a_ref[...]
        gw_ref[...] += jnp.sum(dh * n, axis=0, keepdims=True)
        dn = dh * w_ref[...]
        gx_ref[...] = dout_ref[...] + r * (dn - n * jnp.mean(dn * n, axis=-1, keepdims=True))

    row = pl.BlockSpec((tr, D), lambda i: (i, 0))
    vec = pl.BlockSpec((1, D), lambda i: (0, 0))
    return pl.pallas_call(
        body, name="prenorm_bwd", grid=(L // tr,),
        in_specs=[row, row, row, vec],
        out_specs=[row, vec],
        out_shape=[jax.ShapeDtypeStruct((L, D), F32), jax.ShapeDtypeStruct((1, D), F32)],
        compiler_params=pltpu.CompilerParams(dimension_semantics=("arbitrary",)),
    )(x, dh, dout, w)


def _s5_disc(a_re_raw, a_im, dt):
    a_re = jnp.minimum(a_re_raw, -1e-4)
    mag = jnp.exp(a_re * dt)
    ph = a_im * dt
    ab_re = mag * jnp.cos(ph)
    ab_im = mag * jnp.sin(ph)
    inv_n = 1.0 / (a_re * a_re + a_im * a_im)
    ia_re = a_re * inv_n
    ia_im = -a_im * inv_n
    n_re = ab_re - 1.0
    f_re = n_re * ia_re - ab_im * ia_im
    f_im = n_re * ia_im + ab_im * ia_re
    return a_re, ab_re, ab_im, f_re, f_im, ia_re, ia_im


def _iota2(shape, dim):
    return lax.broadcasted_iota(jnp.int32, shape, dim)


def _group_mask(rows, rows_per_group):
    shift = rows_per_group.bit_length() - 1
    return (_iota2((rows, S5_LANES), 0) >> shift) == (_iota2((rows, S5_LANES), 1) >> (S5_STATE.bit_length() - 1))


def _lane_tiler(dtype):
    return ((_iota2((S5_STATE, S5_LANES), 1) & (S5_STATE - 1)) == _iota2((S5_STATE, S5_LANES), 0)).astype(dtype)


def _row_to_col(row, n):
    eye = (_iota2((n, n), 0) == _iota2((n, n), 1)).astype(F32)
    return jnp.sum(eye * row, axis=1, keepdims=True)


def _group_repeat(G):
    return ((_iota2((G * S5_GROUP, G), 0) >> (S5_GROUP.bit_length() - 1)) == _iota2((G * S5_GROUP, G), 1)).astype(F32)


S5_TABS = 18


def _s5_prep_fwd(a_re, a_im, log_dt, b_re, b_im, c_re, c_im, after, seg):
    G, P = a_re.shape
    nb = G * S5_GROUP // S5_COLS
    g8 = S5_COLS // S5_GROUP
    assert seg & (seg - 1) == 0, seg

    def body(are_ref, aim_ref, ldt_ref, bre_ref, bim_ref, cre_ref, cim_ref, _after_ref,
             bbre_ref, bbim_ref, ctre_ref, ctim_ref, tab_ref, pt_ref):
        dt = jnp.exp(_row_to_col(ldt_ref[...], G))
        _, ab_re, ab_im, f_re, f_im, _, _ = _s5_disc(are_ref[...], aim_ref[...], dt)
        rep = _group_repeat(G)
        fx_re = _dot_hi(rep, f_re)
        fx_im = _dot_hi(rep, f_im)
        br, bi = bre_ref[...], bim_ref[...]
        bb_re = fx_re * br - fx_im * bi
        bb_im = fx_re * bi + fx_im * br
        tile_bf = _lane_tiler(BF16)
        mask = _group_mask(S5_COLS, S5_GROUP)
        for jb in range(nb):
            rs = slice(jb * S5_COLS, (jb + 1) * S5_COLS)
            for src, dst in ((bb_re[rs], bbre_ref), (bb_im[rs], bbim_ref), (cre_ref[rs, :], ctre_ref), (cim_ref[rs, :], ctim_ref)):
                dst[jb] = jnp.where(mask, _dot(src, tile_bf), 0.0).astype(BF16)

        tile_f = _lane_tiler(F32)
        mask8 = _group_mask(g8, 1)
        row = _iota2((SUBLANES, S5_LANES), 0)
        slab = (SUBLANES, S5_LANES)
        cmul = lambda p, q: (p[0] * q[0] - p[1] * q[1], p[0] * q[1] + p[1] * q[0])
        for jb in range(nb):
            gs = slice(jb * g8, (jb + 1) * g8)

            def lanes(m):
                v = jnp.sum(jnp.where(mask8, _dot_hi(m[gs], tile_f), 0.0), axis=0, keepdims=True)
                return jnp.broadcast_to(v, slab)

            a1 = (lanes(ab_re), lanes(ab_im))
            tab_ref[jb, 0], tab_ref[jb, 1] = a1

            def powers(i, p):
                off = pl.multiple_of(i * SUBLANES, SUBLANES)
                pt_ref[jb, 0, pl.ds(off, SUBLANES), :] = p[0]
                pt_ref[jb, 1, pl.ds(off, SUBLANES), :] = p[1]
                return cmul(p, a1)

            lax.fori_loop(0, seg, powers, a1)
            aseg = a1
            for _ in range(seg.bit_length() - 1):
                aseg = cmul(aseg, aseg)
            pw = [aseg]
            for _ in range(1, SUBLANES):
                pw.append(cmul(pw[-1], aseg))
            for lvl, k in enumerate((1, 2, 4)):
                tab_ref[jb, 2 + 2 * lvl] = jnp.where(row >= k, pw[k - 1][0], 0.0)
                tab_ref[jb, 3 + 2 * lvl] = jnp.where(row >= k, pw[k - 1][1], 0.0)
                tab_ref[jb, 10 + 2 * lvl] = jnp.where(row < SUBLANES - k, pw[k - 1][0], 0.0)
                tab_ref[jb, 11 + 2 * lvl] = jnp.where(row < SUBLANES - k, -pw[k - 1][1], 0.0)
            f_r = f_i = r_r = r_i = jnp.zeros(slab, F32)
            for i in range(SUBLANES):
                f_r = jnp.where(row == i, pw[i][0], f_r)
                f_i = jnp.where(row == i, pw[i][1], f_i)
                r_r = jnp.where(row == i, pw[SUBLANES - 1 - i][0], r_r)
                r_i = jnp.where(row == i, -pw[SUBLANES - 1 - i][1], r_i)
            tab_ref[jb, 8] = f_r
            tab_ref[jb, 9] = f_i
            tab_ref[jb, 16] = r_r
            tab_ref[jb, 17] = r_i

    vm = pl.BlockSpec(memory_space=pltpu.VMEM)
    bd = jax.ShapeDtypeStruct((nb, S5_COLS, S5_LANES), BF16)
    return pl.pallas_call(
        body, name="s5_prep_fwd",
        in_specs=[vm] * 7 + [pl.BlockSpec(memory_space=pl.ANY)], out_specs=[vm] * 6,
        out_shape=[bd, bd, bd, bd, jax.ShapeDtypeStruct((nb, S5_TABS, SUBLANES, S5_LANES), F32),
                   jax.ShapeDtypeStruct((nb, 2, seg * SUBLANES, S5_LANES), F32)],
    )(a_re, a_im, log_dt, b_re, b_im, c_re, c_im, after)


def _s5_prep_bwd(a_re, a_im, log_dt, b_re, b_im, gbb_re, gbb_im, gct_re, gct_im, gab_re, gab_im):
    G, P = a_re.shape
    nb = G * S5_GROUP // S5_COLS
    g8 = S5_COLS // S5_GROUP

    def body(are_ref, aim_ref, ldt_ref, bre_ref, bim_ref, gbr_ref, gbi_ref, gcr_ref, gci_ref, gar_ref, gai_ref,
             o_a, o_bc, o_ldt):
        dt = jnp.exp(_row_to_col(ldt_ref[...], G))
        a_raw = are_ref[...]
        a_imv = aim_ref[...]
        a_re_c, ab_re, ab_im, f_re, f_im, ia_re, ia_im = _s5_disc(a_raw, a_imv, dt)
        tile_f = _lane_tiler(F32)
        mask = _group_mask(S5_COLS, S5_GROUP)
        mask8 = _group_mask(g8, 1)
        for jb in range(nb):
            rs = slice(jb * S5_COLS, (jb + 1) * S5_COLS)
            gs = slice(jb * g8, (jb + 1) * g8)
            ls = slice(jb * S5_LANES, (jb + 1) * S5_LANES)
            for k, src in enumerate((gbr_ref, gbi_ref, gcr_ref, gci_ref)):
                o_bc[k, rs, :] = _dot_hi(jnp.where(mask, src[jb], 0.0), tile_f, NT)
            for k, src in enumerate((gar_ref, gai_ref)):
                o_a[k, gs, :] = _dot_hi(jnp.where(mask8, src[:, ls], 0.0), tile_f, NT)
        rep = _group_repeat(G)
        fx_re = _dot_hi(rep, f_re)
        fx_im = _dot_hi(rep, f_im)
        gbr, gbi = o_bc[0], o_bc[1]
        br, bi = bre_ref[...], bim_ref[...]
        o_bc[0] = fx_re * gbr + fx_im * gbi
        o_bc[1] = fx_re * gbi - fx_im * gbr
        gf_re = _dot_hi(rep, br * gbr + bi * gbi, TN)
        gf_im = _dot_hi(rep, br * gbi - bi * gbr, TN)
        gab_r = o_a[0] + ia_re * gf_re + ia_im * gf_im
        gab_i = o_a[1] + ia_re * gf_im - ia_im * gf_re
        q_re = f_re * ia_re - f_im * ia_im
        q_im = f_re * ia_im + f_im * ia_re
        ga_re = -(q_re * gf_re + q_im * gf_im)
        ga_im = -(q_re * gf_im - q_im * gf_re)
        gth_re = ab_re * gab_r + ab_im * gab_i
        gth_im = ab_re * gab_i - ab_im * gab_r
        ga_re = ga_re + dt * gth_re
        ga_im = ga_im + dt * gth_im
        gdt = jnp.sum(a_re_c * gth_re + a_imv * gth_im, axis=-1, keepdims=True)
        eye = (_iota2((G, G), 0) == _iota2((G, G), 1)).astype(F32)
        o_ldt[...] = jnp.sum(eye * (gdt * dt), axis=0, keepdims=True)
        slope = jnp.where(a_raw < -1e-4, 1.0, jnp.where(a_raw == -1e-4, 0.5, 0.0))
        o_a[0] = ga_re * slope
        o_a[1] = ga_im

    vm = pl.BlockSpec(memory_space=pltpu.VMEM)
    return pl.pallas_call(
        body, name="s5_prep_bwd",
        in_specs=[vm] * 11, out_specs=[vm] * 3,
        out_shape=[jax.ShapeDtypeStruct((2, G, P), F32), jax.ShapeDtypeStruct((4, G * S5_GROUP, P), F32),
                   jax.ShapeDtypeStruct((1, G), F32)],
    )(a_re, a_im, log_dt, b_re, b_im, gbb_re, gbb_im, gct_re, gct_im, gab_re, gab_im)


def _scan8(xr, xi, tab_ref, base, shifts):
    for lvl, sh in enumerate(shifts):
        mr = tab_ref[0, base + 2 * lvl]
        mi = tab_ref[0, base + 2 * lvl + 1]
        ar = pltpu.roll(xr, sh, 0)
        ai = pltpu.roll(xi, sh, 0)
        xr, xi = xr + mr * ar - mi * ai, xi + mr * ai + mi * ar
    return xr, xi


def _to_segments(src_ref, dst_ref, seg):
    for i in range(seg):
        dst_ref[i * SUBLANES:(i + 1) * SUBLANES, :] = src_ref[pl.ds(i, SUBLANES, stride=seg), :]


def _from_segments(src_ref, dst_ref, seg):
    for i in range(seg):
        dst_ref[pl.ds(i, SUBLANES, stride=seg), :] = src_ref[i * SUBLANES:(i + 1) * SUBLANES, :]


def _slab(i):
    return pl.ds(pl.multiple_of(i * SUBLANES, SUBLANES), SUBLANES)


def _s5_scan_fwd(proj_main, bbd_re, bbd_im, cbd_re, cbd_im, dvec, tab, ptab, DS):
    L = proj_main.shape[0]
    nb = DS // S5_COLS
    tb = _blk(L, 512, SUBLANES)
    nt = L // tb
    seg = tb // SUBLANES

    def body(u_ref, bre_ref, bim_ref, cre_ref, cim_ref, d_ref, tab_ref, pt_ref, y_ref, sre_ref, sim_ref,
             up_ref, yp_ref, car_ref):
        t = pl.program_id(1)

        @pl.when(t == 0)
        def _():
            car_ref[...] = jnp.zeros_like(car_ref)

        _to_segments(u_ref, up_ref, seg)
        up = up_ref[...]
        sre_ref[...] = _dot(up, bre_ref[0])
        sim_ref[...] = _dot(up, bim_ref[0])
        ar, ai = tab_ref[0, 0], tab_ref[0, 1]

        def pass1(i, x):
            xr = ar * x[0] - ai * x[1] + sre_ref[_slab(i), :]
            xi = ar * x[1] + ai * x[0] + sim_ref[_slab(i), :]
            sre_ref[_slab(i), :] = xr
            sim_ref[_slab(i), :] = xi
            return xr, xi

        zero = jnp.zeros((SUBLANES, S5_LANES), F32)
        er, ei = lax.fori_loop(0, seg, pass1, (zero, zero))
        cin_r, cin_i = car_ref[0], car_ref[1]
        sr, si = _scan8(er, ei, tab_ref, 2, (1, 2, 4))
        pr, pi = tab_ref[0, 8], tab_ref[0, 9]
        sr, si = sr + pr * cin_r - pi * cin_i, si + pr * cin_i + pi * cin_r
        row0 = _iota2((SUBLANES, S5_LANES), 0) == 0
        cr = jnp.where(row0, cin_r, pltpu.roll(sr, 1, 0))
        ci = jnp.where(row0, cin_i, pltpu.roll(si, 1, 0))
        car_ref[0] = jnp.broadcast_to(sr[SUBLANES - 1:SUBLANES, :], sr.shape)
        car_ref[1] = jnp.broadcast_to(si[SUBLANES - 1:SUBLANES, :], si.shape)

        def pass2(i, _):
            qr, qi = pt_ref[0, 0, _slab(i), :], pt_ref[0, 1, _slab(i), :]
            sre_ref[_slab(i), :] += qr * cr - qi * ci
            sim_ref[_slab(i), :] += qr * ci + qi * cr
            return 0

        lax.fori_loop(0, seg, pass2, 0, unroll=4)
        yp_ref[...] = _dot(sre_ref[...], cre_ref[0], NT) - _dot(sim_ref[...], cim_ref[0], NT) + d_ref[...] * up
        _from_segments(yp_ref, y_ref, seg)

    return pl.pallas_call(
        body, name="s5_scan_fwd", grid=(nb, nt),
        in_specs=[
            pl.BlockSpec((tb, S5_COLS), lambda j, t: (t, j)),
            pl.BlockSpec((1, S5_COLS, S5_LANES), lambda j, t: (j, 0, 0)),
            pl.BlockSpec((1, S5_COLS, S5_LANES), lambda j, t: (j, 0, 0)),
            pl.BlockSpec((1, S5_COLS, S5_LANES), lambda j, t: (j, 0, 0)),
            pl.BlockSpec((1, S5_COLS, S5_LANES), lambda j, t: (j, 0, 0)),
            pl.BlockSpec((1, S5_COLS), lambda j, t: (0, j)),
            pl.BlockSpec((1, S5_TABS, SUBLANES, S5_LANES), lambda j, t: (j, 0, 0, 0)),
            pl.BlockSpec((1, 2, tb, S5_LANES), lambda j, t: (j, 0, 0, 0)),
        ],
        out_specs=[
            pl.BlockSpec((tb, S5_COLS), lambda j, t: (t, j)),
            pl.BlockSpec((tb, S5_LANES), lambda j, t: (t, j)),
            pl.BlockSpec((tb, S5_LANES), lambda j, t: (t, j)),
        ],
        out_shape=[jax.ShapeDtypeStruct((L, DS), F32),
                   jax.ShapeDtypeStruct((L, nb * S5_LANES), F32),
                   jax.ShapeDtypeStruct((L, nb * S5_LANES), F32)],
        scratch_shapes=[pltpu.VMEM((tb, S5_COLS), F32), pltpu.VMEM((tb, S5_COLS), F32),
                        pltpu.VMEM((2, SUBLANES, S5_LANES), F32)],
        compiler_params=pltpu.CompilerParams(dimension_semantics=("parallel", "arbitrary")),
    )(proj_main, bbd_re, bbd_im, cbd_re, cbd_im, dvec, tab, ptab)


def _s5_scan_bwd(dy, proj_main, s_re, s_im, bbd_re, bbd_im, cbd_re, cbd_im, dvec, tab, ptab, d_s5, DS):
    L = proj_main.shape[0]
    nb = DS // S5_COLS
    tb = _blk(L, 512, SUBLANES)
    nt = L // tb
    seg = tb // SUBLANES
    tb8 = tb // SUBLANES

    def body(dy_ref, u_ref, sre_ref, sim_ref, pre_ref, pim_ref, bre_ref, bim_ref, cre_ref, cim_ref, d_ref, tab_ref, pt_ref,
             _ds5_ref, du_ref, gd_ref, gcre_ref, gcim_ref, gbre_ref, gbim_ref, gare_ref, gaim_ref,
             lre_ref, lim_ref, up_ref, dyp_ref, dup_ref, duo_ref, car_ref):
        t = pl.program_id(1)

        @pl.when(t == 0)
        def _():
            car_ref[...] = jnp.zeros_like(car_ref)
            gd_ref[...] = jnp.zeros_like(gd_ref)
            gcre_ref[...] = jnp.zeros_like(gcre_ref)
            gcim_ref[...] = jnp.zeros_like(gcim_ref)
            gbre_ref[...] = jnp.zeros_like(gbre_ref)
            gbim_ref[...] = jnp.zeros_like(gbim_ref)
            gare_ref[...] = jnp.zeros_like(gare_ref)
            gaim_ref[...] = jnp.zeros_like(gaim_ref)

        _to_segments(dy_ref, dyp_ref, seg)
        _to_segments(u_ref, up_ref, seg)
        dyv = dyp_ref[...]
        u = up_ref[...]
        gd_ref[...] += jnp.sum(dyv * u, axis=0, keepdims=True)
        lre_ref[...] = _dot(dyv, cre_ref[0])
        lim_ref[...] = -_dot(dyv, cim_ref[0])
        gcre_ref[0] += _dot(dyv, sre_ref[...], TN)
        gcim_ref[0] -= _dot(dyv, sim_ref[...], TN)
        ar, ai = tab_ref[0, 0], -tab_ref[0, 1]

        def pass1(k, x):
            i = seg - 1 - k
            xr = ar * x[0] - ai * x[1] + lre_ref[_slab(i), :]
            xi = ar * x[1] + ai * x[0] + lim_ref[_slab(i), :]
            lre_ref[_slab(i), :] = xr
            lim_ref[_slab(i), :] = xi
            return xr, xi

        zero = jnp.zeros((SUBLANES, S5_LANES), F32)
        er, ei = lax.fori_loop(0, seg, pass1, (zero, zero))
        cin_r, cin_i = car_ref[0], car_ref[1]
        lr, li = _scan8(er, ei, tab_ref, 10, (7, 6, 4))
        pr, pi = tab_ref[0, 16], tab_ref[0, 17]
        lr, li = lr + pr * cin_r - pi * cin_i, li + pr * cin_i + pi * cin_r
        rows = _iota2((SUBLANES, S5_LANES), 0)
        cr = jnp.where(rows == SUBLANES - 1, cin_r, pltpu.roll(lr, SUBLANES - 1, 0))
        ci = jnp.where(rows == SUBLANES - 1, cin_i, pltpu.roll(li, SUBLANES - 1, 0))
        car_ref[0] = jnp.broadcast_to(lr[0:1, :], lr.shape)
        car_ref[1] = jnp.broadcast_to(li[0:1, :], li.shape)

        first = (t == nt - 1).astype(F32)
        head_re = jnp.broadcast_to(pre_ref[SUBLANES - 1:SUBLANES, :], zero.shape) * (1.0 - first)
        head_im = jnp.broadcast_to(pim_ref[SUBLANES - 1:SUBLANES, :], zero.shape) * (1.0 - first)
        last = _slab(seg - 1)
        sp0_re = jnp.where(rows == 0, head_re, pltpu.roll(sre_ref[last, :], 1, 0))
        sp0_im = jnp.where(rows == 0, head_im, pltpu.roll(sim_ref[last, :], 1, 0))

        def pass2(i, acc):
            j = seg - 1 - i
            qr, qi = pt_ref[0, 0, _slab(j), :], -pt_ref[0, 1, _slab(j), :]
            xr = lre_ref[_slab(i), :] + qr * cr - qi * ci
            xi = lim_ref[_slab(i), :] + qr * ci + qi * cr
            lre_ref[_slab(i), :] = xr
            lim_ref[_slab(i), :] = xi
            prev = _slab(jnp.maximum(i - 1, 0))
            sp_re = jnp.where(i == 0, sp0_re, sre_ref[prev, :])
            sp_im = jnp.where(i == 0, sp0_im, sim_ref[prev, :])
            return acc[0] + sp_re * xr + sp_im * xi, acc[1] + sp_re * xi - sp_im * xr

        acc_re, acc_im = lax.fori_loop(0, seg, pass2, (zero, zero), unroll=2)
        gare_ref[...] += jnp.sum(acc_re, axis=0, keepdims=True)
        gaim_ref[...] += jnp.sum(acc_im, axis=0, keepdims=True)
        lre = lre_ref[...]
        lim = lim_ref[...]
        dup_ref[...] = dyv * d_ref[...] + _dot(lre, bre_ref[0], NT) + _dot(lim, bim_ref[0], NT)
        _from_segments(dup_ref, duo_ref, seg)
        du_ref[...] = duo_ref[...].astype(BF16)
        gbre_ref[0] += _dot(u, lre, TN)
        gbim_ref[0] += _dot(u, lim, TN)

    rt = lambda t: nt - 1 - t
    col = pl.BlockSpec((tb, S5_COLS), lambda j, t: (rt(t), j))
    st = pl.BlockSpec((tb, S5_LANES), lambda j, t: (rt(t), j))
    prev = pl.BlockSpec((SUBLANES, S5_LANES), lambda j, t: (jnp.maximum(rt(t) * tb8 - 1, 0), j))
    bmat = pl.BlockSpec((1, S5_COLS, S5_LANES), lambda j, t: (j, 0, 0))
    cmat = bmat
    return pl.pallas_call(
        body, name="s5_scan_bwd", grid=(nb, nt),
        in_specs=[col, col, st, st, prev, prev, bmat, bmat, cmat, cmat,
                  pl.BlockSpec((1, S5_COLS), lambda j, t: (0, j)),
                  pl.BlockSpec((1, S5_TABS, SUBLANES, S5_LANES), lambda j, t: (j, 0, 0, 0)),
                  pl.BlockSpec((1, 2, tb, S5_LANES), lambda j, t: (j, 0, 0, 0)),
                  pl.BlockSpec(memory_space=pl.ANY)],
        out_specs=[col, pl.BlockSpec((1, S5_COLS), lambda j, t: (0, j)), cmat, cmat, bmat, bmat,
                   pl.BlockSpec((1, S5_LANES), lambda j, t: (0, j)), pl.BlockSpec((1, S5_LANES), lambda j, t: (0, j))],
        input_output_aliases={13: 0},
        out_shape=[jax.ShapeDtypeStruct((L, 2 * DS), BF16), jax.ShapeDtypeStruct((1, DS), F32),
                   jax.ShapeDtypeStruct((nb, S5_COLS, S5_LANES), F32), jax.ShapeDtypeStruct((nb, S5_COLS, S5_LANES), F32),
                   jax.ShapeDtypeStruct((nb, S5_COLS, S5_LANES), F32), jax.ShapeDtypeStruct((nb, S5_COLS, S5_LANES), F32),
                   jax.ShapeDtypeStruct((1, nb * S5_LANES), F32), jax.ShapeDtypeStruct((1, nb * S5_LANES), F32)],
        scratch_shapes=[pltpu.VMEM((tb, S5_LANES), F32), pltpu.VMEM((tb, S5_LANES), F32)]
        + [pltpu.VMEM((tb, S5_COLS), F32)] * 4 + [pltpu.VMEM((2, SUBLANES, S5_LANES), F32)],
        compiler_params=pltpu.CompilerParams(dimension_semantics=("parallel", "arbitrary")),
    )(dy, proj_main, s_re, s_im, s_re, s_im, bbd_re, bbd_im, cbd_re, cbd_im, dvec, tab, ptab, d_s5)


def _s5_post_fwd(y_pre, proj_main, glu_w, glu_b, DS):
    L = y_pre.shape[0]
    tr = _blk(L, 256, SUBLANES)

    def body(y_ref, z_ref, w_ref, b_ref, o_ref, t_ref):
        y1 = _gelu(y_ref[...])
        t = _dot(y1, w_ref[...]) + b_ref[...]
        t_ref[...] = t
        z = z_ref[...]
        o_ref[...] = (y1 * _sigmoid(t) * (z * _sigmoid(z))).astype(BF16)

    row = pl.BlockSpec((tr, DS), lambda i: (i, 0))
    return pl.pallas_call(
        body, name="s5_post_fwd", grid=(L // tr,),
        in_specs=[row, pl.BlockSpec((tr, DS), lambda i: (i, 1)), pl.BlockSpec((DS, DS), lambda i: (0, 0)),
                  pl.BlockSpec((1, DS), lambda i: (0, 0))],
        out_specs=[row, row],
        out_shape=[jax.ShapeDtypeStruct((L, 2 * DS), BF16), jax.ShapeDtypeStruct((L, DS), F32)],
        compiler_params=pltpu.CompilerParams(dimension_semantics=("parallel",)),
    )(y_pre, proj_main, glu_w, glu_b)


def _s5_post_bwd(d_ycat, y_pre, proj_main, t_pre, glu_w, DS):
    L = y_pre.shape[0]
    tr = _blk(L, 256, SUBLANES)

    def body(dy_ref, y_ref, z_ref, t_ref, w_ref, dyp_ref, dz_ref, dt_ref, y1_ref, gb_ref):
        i = pl.program_id(0)

        @pl.when(i == 0)
        def _():
            gb_ref[...] = jnp.zeros_like(gb_ref)

        dy = dy_ref[...]
        yp = y_ref[...]
        z = z_ref[...]
        y1 = _gelu(yp)
        sg = _sigmoid(t_ref[...])
        sz = _sigmoid(z)
        c = y1 * sg
        d_c = dy * (z * sz)
        dz_ref[...] = (dy * c * (sz * (1.0 + z * (1.0 - sz)))).astype(BF16)
        d_t = d_c * y1 * sg * (1.0 - sg)
        gb_ref[...] += jnp.sum(d_t, axis=0, keepdims=True)
        dt_ref[...] = d_t.astype(BF16)
        y1_ref[...] = y1.astype(BF16)
        d_y1 = d_c * sg + _dot(d_t, w_ref[...], NT)
        dyp_ref[...] = d_y1 * _gelu_grad(yp)

    row = pl.BlockSpec((tr, DS), lambda i: (i, 0))
    return pl.pallas_call(
        body, name="s5_post_bwd", grid=(L // tr,),
        in_specs=[row, row, pl.BlockSpec((tr, DS), lambda i: (i, 1)), row, pl.BlockSpec((DS, DS), lambda i: (0, 0))],
        out_specs=[row, pl.BlockSpec((tr, DS), lambda i: (i, 1)), row, row, pl.BlockSpec((1, DS), lambda i: (0, 0))],
        out_shape=[jax.ShapeDtypeStruct((L, DS), F32), jax.ShapeDtypeStruct((L, 2 * DS), BF16),
                   jax.ShapeDtypeStruct((L, DS), BF16), jax.ShapeDtypeStruct((L, DS), BF16),
                   jax.ShapeDtypeStruct((1, DS), F32)],
        compiler_params=pltpu.CompilerParams(dimension_semantics=("arbitrary",)),
    )(d_ycat, y_pre, proj_main, t_pre, glu_w)


def _gla_gates(glow, gu_ref, gb_ref):
    a = _dot(glow, gu_ref[...]) + gb_ref[...]
    lg = (jnp.minimum(a, 0.0) - jnp.log(1.0 + jnp.exp(-jnp.abs(a)))) * (1.0 / GLA_TAU)
    ri = lax.broadcasted_iota(jnp.int32, (GLA_CHUNK, GLA_CHUNK), 0)
    ci = lax.broadcasted_iota(jnp.int32, (GLA_CHUNK, GLA_CHUNK), 1)
    b = _dot_hi((ri >= ci).astype(F32), lg)
    b_last = jnp.sum(lg, axis=0, keepdims=True)
    return a, b, b_last, ri >= ci


def _gla_specs(DS, DK, DV, c, cmap):
    return [
        pl.BlockSpec((c, DK), lambda n: (cmap(n), 2 * DS // DK)),
        pl.BlockSpec((c, DK), lambda n: (cmap(n), 2 * DS // DK + 1)),
        pl.BlockSpec((c, DV), lambda n: (cmap(n), (2 * DS + 2 * DK) // DV)),
        pl.BlockSpec((c, DV), lambda n: (cmap(n), (2 * DS + 2 * DK) // DV + 1)),
    ]


def _gla_fwd(proj_main, proj_low, gate_up_pad, gate_bias, norm_w, ycat, DS, DK, DV):
    L = proj_main.shape[0]
    nc = L // GLA_CHUNK
    cps = math.gcd(GLA_STEP_CHUNKS, nc)
    nh = DK // GLA_HK
    scale = GLA_HK ** -0.5

    def body(q_ref, k_ref, v_ref, z_ref, gl_ref, gu_ref, gb_ref, nw_ref, _yc_ref, y_ref, sp_ref, st_ref):
        n = pl.program_id(0)

        @pl.when(n == 0)
        def _():
            st_ref[...] = jnp.zeros_like(st_ref)

        pairs = [(sc, h) for sc in range(cps) for h in range(nh)]
        rows = lambda sc: slice(sc * GLA_CHUNK, (sc + 1) * GLA_CHUNK)
        kcol = lambda h: slice(h * GLA_HK, (h + 1) * GLA_HK)
        vcol = lambda h: slice(h * GLA_HV, (h + 1) * GLA_HV)
        gates = [_gla_gates(gl_ref[rows(sc), :], gu_ref, gb_ref) for sc in range(cps)]
        qe, dec, o_in, kv = {}, {}, {}, {}
        for sc, h in pairs:
            _, b, b_last, mask = gates[sc]
            bh, bl = b[:, kcol(h)], b_last[:, kcol(h)]
            qe[sc, h] = (q_ref[rows(sc), kcol(h)] * scale) * jnp.exp(bh)
            kh = k_ref[rows(sc), kcol(h)]
            vh = v_ref[rows(sc), vcol(h)]
            attn = jnp.where(mask, _dot(qe[sc, h], kh * jnp.exp(-bh), NT), 0.0)
            o_in[sc, h] = _dot(attn, vh)
            kv[sc, h] = _dot(vh, kh * jnp.exp(bl - bh), TN)
            dec[sc, h] = jnp.exp(bl)
        for sc, h in pairs:
            st = st_ref[h]
            sp_ref[sc, h] = st
            o = o_in[sc, h] + _dot(qe[sc, h], st, NT)
            st_ref[h] = dec[sc, h] * st + kv[sc, h]
            r = lax.rsqrt(jnp.mean(o * o, axis=-1, keepdims=True) + EPS)
            z = z_ref[rows(sc), vcol(h)]
            y_ref[rows(sc), vcol(h)] = (o * r * nw_ref[...] * (z * _sigmoid(z))).astype(BF16)

    c = cps * GLA_CHUNK
    return pl.pallas_call(
        body, name="gla_fwd", grid=(nc // cps,),
        in_specs=_gla_specs(DS, DK, DV, c, lambda n: n) + [
            pl.BlockSpec((c, LANES), lambda n: (n, 0)),
            pl.BlockSpec((LANES, DK), lambda n: (0, 0)),
            pl.BlockSpec((1, DK), lambda n: (0, 0)),
            pl.BlockSpec((1, GLA_HV), lambda n: (0, 0)),
            pl.BlockSpec(memory_space=pl.ANY),
        ],
        out_specs=[pl.BlockSpec((c, DV), lambda n: (n, DS // DV)),
                   pl.BlockSpec((cps, nh, GLA_HV, GLA_HK), lambda n: (n, 0, 0, 0))],
        input_output_aliases={8: 0},
        out_shape=[jax.ShapeDtypeStruct(ycat.shape, BF16), jax.ShapeDtypeStruct((nc, nh, GLA_HV, GLA_HK), F32)],
        scratch_shapes=[pltpu.VMEM((nh, GLA_HV, GLA_HK), F32)],
        compiler_params=pltpu.CompilerParams(dimension_semantics=("arbitrary",)),
    )(proj_main, proj_main, proj_main, proj_main, proj_low, gate_up_pad, gate_bias, norm_w, ycat)


def _gla_bwd(d_ycat, proj_main, proj_low, s_prev, gate_up_pad, gate_bias, norm_w, DS, DK, DV):
    L = proj_main.shape[0]
    nc = L // GLA_CHUNK
    cps = math.gcd(GLA_STEP_CHUNKS, nc)
    nh = DK // GLA_HK
    scale = GLA_HK ** -0.5

    def body(dy_ref, q_ref, k_ref, v_ref, z_ref, gl_ref, sp_ref, gu_ref, gb_ref, nw_ref,
             dg_ref, da_ref, gnw_ref, ggb_ref, dst_ref):
        n = pl.program_id(0)

        @pl.when(n == 0)
        def _():
            dst_ref[...] = jnp.zeros_like(dst_ref)
            gnw_ref[...] = jnp.zeros_like(gnw_ref)
            ggb_ref[...] = jnp.zeros_like(ggb_ref)

        last_row = lax.broadcasted_iota(jnp.int32, (GLA_CHUNK, GLA_HK), 0) == GLA_CHUNK - 1
        ri = lax.broadcasted_iota(jnp.int32, (GLA_CHUNK, GLA_CHUNK), 0)
        ci = lax.broadcasted_iota(jnp.int32, (GLA_CHUNK, GLA_CHUNK), 1)
        upper = (ci >= ri).astype(F32)
        nw = nw_ref[...]
        for sc in reversed(range(cps)):
            rs = slice(sc * GLA_CHUNK, (sc + 1) * GLA_CHUNK)
            a, b, b_last, mask = _gla_gates(gl_ref[rs, :], gu_ref, gb_ref)
            for h in range(nh):
                ks = slice(h * GLA_HK, (h + 1) * GLA_HK)
                vs = slice(h * GLA_HV, (h + 1) * GLA_HV)
                bh, bl = b[:, ks], b_last[:, ks]
                e = jnp.exp(bh)
                einv = jnp.exp(-bh)
                etail = jnp.exp(bl - bh)
                dec = jnp.exp(bl)
                qe = (q_ref[rs, ks] * scale) * e
                kh = k_ref[rs, ks]
                ke = kh * einv
                ktail = kh * etail
                vh = v_ref[rs, vs]
                st = sp_ref[sc, h]
                dst = dst_ref[h]
                attn = jnp.where(mask, _dot(qe, ke, NT), 0.0)
                o = _dot(attn, vh) + _dot(qe, st, NT)
                r = lax.rsqrt(jnp.mean(o * o, axis=-1, keepdims=True) + EPS)
                nrm = o * r
                z = z_ref[rs, vs]
                sz = _sigmoid(z)
                dy = dy_ref[rs, vs]
                dg_ref[rs, 2 * DK + DV + h * GLA_HV:2 * DK + DV + (h + 1) * GLA_HV] = (
                    dy * nrm * nw * (sz * (1.0 + z * (1.0 - sz)))).astype(BF16)
                d_on = dy * (z * sz)
                gnw_ref[...] += jnp.sum(d_on * nrm, axis=0, keepdims=True)
                d_n = d_on * nw
                d_o = r * (d_n - nrm * jnp.mean(d_n * nrm, axis=-1, keepdims=True))
                d_attn = jnp.where(mask, _dot(d_o, vh, NT), 0.0)
                dg_ref[rs, 2 * DK + h * GLA_HV:2 * DK + (h + 1) * GLA_HV] = (
                    _dot(attn, d_o, TN) + _dot(ktail, dst, NT)).astype(BF16)
                d_qe = _dot(d_attn, ke) + _dot(d_o, st)
                d_ke = _dot(d_attn, qe, TN)
                d_kt = _dot(vh, dst)
                d_dec = jnp.sum(dst * st, axis=0, keepdims=True)
                dst_ref[h] = dec * dst + _dot(d_o, qe, TN)
                dg_ref[rs, ks] = (d_qe * scale * e).astype(BF16)
                dg_ref[rs, DK + h * GLA_HK:DK + (h + 1) * GLA_HK] = (d_ke * einv + d_kt * etail).astype(BF16)
                d_bl = jnp.sum(d_kt * ktail, axis=0, keepdims=True) + d_dec * dec
                d_b = d_qe * qe - d_ke * ke - d_kt * ktail + jnp.where(last_row, d_bl, 0.0)
                d_lg = _dot_hi(upper, d_b)
                d_a = d_lg * (1.0 / GLA_TAU) * _sigmoid(-a[:, ks])
                ggb_ref[:, ks] += jnp.sum(d_a, axis=0, keepdims=True)
                da_ref[rs, ks] = d_a.astype(BF16)

    c = cps * GLA_CHUNK
    ns = nc // cps
    rn = lambda n: ns - 1 - n
    return pl.pallas_call(
        body, name="gla_bwd", grid=(ns,),
        in_specs=[pl.BlockSpec((c, DV), lambda n: (rn(n), DS // DV))] + _gla_specs(DS, DK, DV, c, rn) + [
            pl.BlockSpec((c, LANES), lambda n: (rn(n), 0)),
            pl.BlockSpec((cps, nh, GLA_HV, GLA_HK), lambda n: (rn(n), 0, 0, 0)),
            pl.BlockSpec((LANES, DK), lambda n: (0, 0)),
            pl.BlockSpec((1, DK), lambda n: (0, 0)),
            pl.BlockSpec((1, GLA_HV), lambda n: (0, 0)),
        ],
        out_specs=[pl.BlockSpec((c, 2 * DK + 2 * DV), lambda n: (rn(n), 0)),
                   pl.BlockSpec((c, DK), lambda n: (rn(n), 0)),
                   pl.BlockSpec((1, GLA_HV), lambda n: (0, 0)), pl.BlockSpec((1, DK), lambda n: (0, 0))],
        out_shape=[jax.ShapeDtypeStruct((L, 2 * DK + 2 * DV), BF16),
                   jax.ShapeDtypeStruct((L, DK), BF16),
                   jax.ShapeDtypeStruct((1, GLA_HV), F32), jax.ShapeDtypeStruct((1, DK), F32)],
        scratch_shapes=[pltpu.VMEM((nh, GLA_HV, GLA_HK), F32)],
        compiler_params=pltpu.CompilerParams(dimension_semantics=("arbitrary",)),
    )(d_ycat, proj_main, proj_main, proj_main, proj_main, proj_low, s_prev, gate_up_pad, gate_bias, norm_w)


def _adamw_math(w, g, m, v):
    c1 = 1.0 - ADAM_B1 ** ADAM_STEP
    c2 = 1.0 - ADAM_B2 ** ADAM_STEP
    m_ = ADAM_B1 * m + (1.0 - ADAM_B1) * g
    v_ = ADAM_B2 * v + (1.0 - ADAM_B2) * (g * g)
    return -ADAM_LR * ((m_ / c1) / (jnp.sqrt(v_ / c2) + ADAM_EPS) + ADAM_WD * w), m_, v_


def _adamw_small(g_row, g_a, g_bc, ws, ms, vs):
    n = len(ws)
    nvec = n - 6

    def body(*refs):
        grow_ref, ga_ref, gbc_ref = refs[:3]
        w_refs, m_refs, v_refs = refs[3:3 + n], refs[3 + n:3 + 2 * n], refs[3 + 2 * n:3 + 3 * n]
        outs = refs[3 + 3 * n:]
        off = 0
        for i in range(n):
            if i < nvec:
                width = ws[i].shape[1]
                g = grow_ref[:, off:off + width]
                off += width
            elif i < nvec + 2:
                g = ga_ref[i - nvec]
            else:
                g = gbc_ref[i - nvec - 2]
            d, m_, v_ = _adamw_math(w_refs[i][...], g, m_refs[i][...], v_refs[i][...])
            outs[i][...] = g
            outs[n + i][...] = d
            outs[2 * n + i][...] = m_
            outs[3 * n + i][...] = v_

    vm = pl.BlockSpec(memory_space=pltpu.VMEM)
    outs = pl.pallas_call(
        body, name="adamw_small",
        in_specs=[vm] * (3 + 3 * n), out_specs=[vm] * (4 * n),
        out_shape=[jax.ShapeDtypeStruct(w.shape, F32) for w in ws] * 4,
    )(g_row, g_a, g_bc, *ws, *ms, *vs)
    return [outs[k * n:(k + 1) * n] for k in range(4)]


def _my_pos():
    return lax.axis_index("x"), lax.axis_index("y"), lax.axis_index("c")


def _late_gather_copies(srcs, lands, send_sems, recv_sems):
    x, y, c = _my_pos()
    me = 2 * x + y
    copies = []
    for d in (1, 2, 3):
        to = (x ^ (d >> 1), y ^ (d & 1), c)
        for a in range(len(srcs)):
            hrows = srcs[a].shape[0] // 2
            rows = pl.ds(c * hrows, hrows)
            copies.append(pltpu.make_async_remote_copy(
                src_ref=srcs[a].at[rows, :], dst_ref=lands[a].at[me, rows, :], send_sem=send_sems.at[3 * a + d - 1],
                recv_sem=recv_sems.at[3 * a + d - 1], device_id=to, device_id_type=MESH))
    return copies


def _late_gather_start(shards, after, name):
    n = len(shards)

    def body(*refs):
        srcs, lands = refs[:n], refs[n:2 * n]
        send_sems, recv_sems = refs[2 * n + 1], refs[2 * n + 2]
        token = refs[-1]
        for cp in _late_gather_copies(srcs, lands, send_sems, recv_sems):
            cp.start()
        token[...] = jnp.zeros_like(token)

    hbm = pl.BlockSpec(memory_space=pltpu.HBM)
    sem = pl.BlockSpec(memory_space=pltpu.SEMAPHORE)
    outs = pl.pallas_call(
        body, name=name,
        in_specs=[hbm] * (2 * n) + [pl.BlockSpec(memory_space=pl.ANY)],
        out_specs=[sem, sem] + [hbm] * (2 * n) + [pl.BlockSpec(memory_space=pltpu.VMEM)],
        out_shape=[pltpu.SemaphoreType.DMA((3 * n,)), pltpu.SemaphoreType.DMA((3 * n,))]
        + [pltpu.HBM(s.shape, s.dtype) for s in shards]
        + [pltpu.HBM((4,) + s.shape, s.dtype) for s in shards]
        + [jax.ShapeDtypeStruct((SUBLANES, LANES), F32)],
        input_output_aliases={i: 2 + i for i in range(2 * n)},
        compiler_params=pltpu.CompilerParams(has_side_effects=pltpu.SideEffectType.DATAFLOW_SIDE_EFFECTING),
    )(*[pltpu.with_memory_space_constraint(s, pltpu.HBM) for s in shards],
      *[pltpu.with_memory_space_constraint(lax.empty((4,) + s.shape, s.dtype), pltpu.HBM) for s in shards], after)
    return outs[0], outs[1], outs[2:2 + n], outs[2 + n:2 + 2 * n], outs[-1]


def _late_gather_wait(send_sems, recv_sems, shards, lands, after, name):
    n = len(shards)

    def body(*refs):
        src_refs, land_refs = refs[:n], refs[n:2 * n]
        ssem, rsem = refs[2 * n], refs[2 * n + 1]
        for cp in _late_gather_copies(src_refs, land_refs, ssem, rsem):
            cp.wait_send()
            cp.wait_recv()

    hbm = pl.BlockSpec(memory_space=pltpu.HBM)
    sem = pl.BlockSpec(memory_space=pltpu.SEMAPHORE)
    outs = pl.pallas_call(
        body, name=name,
        in_specs=[hbm] * (2 * n) + [sem, sem] + [pl.BlockSpec(memory_space=pl.ANY)] * len(after),
        out_specs=[hbm] * (2 * n),
        out_shape=[pltpu.HBM(s.shape, s.dtype) for s in shards] + [pltpu.HBM(p.shape, p.dtype) for p in lands],
        input_output_aliases={i: i for i in range(2 * n)},
        compiler_params=pltpu.CompilerParams(has_side_effects=pltpu.SideEffectType.DATAFLOW_SIDE_EFFECTING),
    )(*shards, *lands, send_sems, recv_sems, *after)
    return outs[n:]


def _late_gather_pair(lands, name):
    n = len(lands)

    def body(*refs):
        outs = refs[n:2 * n]
        send_sems, recv_sems = refs[2 * n:]
        x, y, c = _my_pos()

        def copy(a, d, half):
            chip = 2 * (x ^ (d >> 1)) + (y ^ (d & 1))
            hrows = lands[a].shape[1] // 2
            sl = outs[a].at[chip, pl.ds(half * hrows, hrows), :]
            return pltpu.make_async_remote_copy(src_ref=sl, dst_ref=sl, send_sem=send_sems.at[3 * a + d - 1],
                                                recv_sem=recv_sems.at[3 * a + d - 1], device_id=(x, y, 1 - c),
                                                device_id_type=MESH)

        pairs = [(a, d) for d in (1, 2, 3) for a in range(n)]
        for a, d in pairs:
            copy(a, d, c).start()
        for a, d in pairs:
            copy(a, d, c).wait_send()
            copy(a, d, 1 - c).wait_recv()

    hbm = pl.BlockSpec(memory_space=pltpu.HBM)
    return pl.pallas_call(
        body, name=name, in_specs=[hbm] * n, out_specs=[hbm] * n,
        out_shape=[jax.ShapeDtypeStruct(p.shape, p.dtype) for p in lands],
        input_output_aliases={i: i for i in range(n)},
        scratch_shapes=[pltpu.SemaphoreType.DMA((3 * n,)), pltpu.SemaphoreType.DMA((3 * n,))],
    )(*lands)


def _pair_exchange(gs):
    n = len(gs)

    def body(*refs):
        ins, outs = refs[:n], refs[n:2 * n]
        send_sems, recv_sems = refs[2 * n:]
        x, y, c = _my_pos()
        sent = []
        for a in range(n):
            hrows = gs[a].shape[1] // 2
            cp = pltpu.make_async_remote_copy(
                src_ref=ins[a].at[:, pl.ds((1 - c) * hrows, hrows), :], dst_ref=outs[a], send_sem=send_sems.at[a],
                recv_sem=recv_sems.at[a], device_id=(x, y, 1 - c), device_id_type=MESH)
            cp.start()
            sent.append(cp)
        for cp in sent:
            cp.wait()

    hbm = pl.BlockSpec(memory_space=pltpu.HBM)
    return pl.pallas_call(
        body, name="grad_pair_exchange", in_specs=[hbm] * n, out_specs=[hbm] * n,
        out_shape=[jax.ShapeDtypeStruct((g.shape[0], g.shape[1] // 2, g.shape[2]), g.dtype) for g in gs],
        scratch_shapes=[pltpu.SemaphoreType.DMA((n,)), pltpu.SemaphoreType.DMA((n,))],
    )(*gs)


def _pair_add(g, got, c_arr, name):
    nk, rows2, cols = g.shape
    hrows = rows2 // 2
    tr = _blk(hrows, 256, 2 * SUBLANES)
    nb = hrows // tr

    def body(c_ref, a_ref, b_ref, o_ref):
        o_ref[...] = (a_ref[...].astype(F32) + b_ref[...].astype(F32)).astype(o_ref.dtype)

    return pl.pallas_call(
        body, name=name,
        grid_spec=pltpu.PrefetchScalarGridSpec(
            num_scalar_prefetch=1, grid=(nk, nb),
            in_specs=[pl.BlockSpec((1, tr, cols), lambda k, i, c_ref: (k, c_ref[0] * nb + i, 0)),
                      pl.BlockSpec((1, tr, cols), lambda k, i, c_ref: (k, i, 0))],
            out_specs=pl.BlockSpec((1, tr, cols), lambda k, i, c_ref: (k, i, 0))),
        out_shape=jax.ShapeDtypeStruct((nk, hrows, cols), g.dtype),
        compiler_params=pltpu.CompilerParams(dimension_semantics=("parallel", "parallel")),
    )(c_arr, g, got)


def _chip_scatter_copies(srcs, lands, send_sems, recv_sems):
    x, y, c = _my_pos()
    copies = []
    for d in (1, 2, 3):
        tx, ty = x ^ (d >> 1), y ^ (d & 1)
        for a in range(len(srcs)):
            copies.append(pltpu.make_async_remote_copy(
                src_ref=srcs[a].at[2 * tx + ty], dst_ref=lands[a].at[d - 1], send_sem=send_sems.at[3 * a + d - 1],
                recv_sem=recv_sems.at[3 * a + d - 1], device_id=(tx, ty, c), device_id_type=MESH))
    return copies


def _chip_scatter_start(pss):
    n = len(pss)

    def body(*refs):
        srcs, lands = refs[:n], refs[n:2 * n]
        send_sems, recv_sems = refs[2 * n], refs[2 * n + 1]
        token = refs[-1]
        for cp in _chip_scatter_copies(srcs, lands, send_sems, recv_sems):
            cp.start()
        token[...] = jnp.zeros_like(token)

    hbm = pl.BlockSpec(memory_space=pltpu.HBM)
    sem = pl.BlockSpec(memory_space=pltpu.SEMAPHORE)
    land_shapes = [(3,) + p.shape[1:] for p in pss]
    outs = pl.pallas_call(
        body, name="grad_chip_scatter_start",
        in_specs=[hbm] * (2 * n),
        out_specs=[sem, sem] + [hbm] * (2 * n) + [pl.BlockSpec(memory_space=pltpu.VMEM)],
        out_shape=[pltpu.SemaphoreType.DMA((3 * n,)), pltpu.SemaphoreType.DMA((3 * n,))]
        + [pltpu.HBM(p.shape, p.dtype) for p in pss]
        + [pltpu.HBM(s, p.dtype) for s, p in zip(land_shapes, pss)]
        + [jax.ShapeDtypeStruct((SUBLANES, LANES), F32)],
        input_output_aliases={i: 2 + i for i in range(2 * n)},
        compiler_params=pltpu.CompilerParams(has_side_effects=pltpu.SideEffectType.DATAFLOW_SIDE_EFFECTING),
    )(*[pltpu.with_memory_space_constraint(p, pltpu.HBM) for p in pss],
      *[pltpu.with_memory_space_constraint(lax.empty(s, p.dtype), pltpu.HBM) for s, p in zip(land_shapes, pss)])
    return outs[0], outs[1], outs[2:2 + n], outs[2 + n:2 + 2 * n], outs[-1]


def _chip_scatter_wait(send_sems, recv_sems, srcs, lands, after):
    n = len(srcs)

    def body(*refs):
        src_refs, land_refs = refs[:n], refs[n:2 * n]
        ssem, rsem = refs[2 * n], refs[2 * n + 1]
        for cp in _chip_scatter_copies(src_refs, land_refs, ssem, rsem):
            cp.wait_send()
            cp.wait_recv()

    hbm = pl.BlockSpec(memory_space=pltpu.HBM)
    sem = pl.BlockSpec(memory_space=pltpu.SEMAPHORE)
    outs = pl.pallas_call(
        body, name="grad_chip_scatter_wait",
        in_specs=[hbm] * (2 * n) + [sem, sem, pl.BlockSpec(memory_space=pl.ANY)],
        out_specs=[hbm] * (2 * n),
        out_shape=[pltpu.HBM(p.shape, p.dtype) for p in srcs] + [pltpu.HBM(p.shape, p.dtype) for p in lands],
        input_output_aliases={i: i for i in range(2 * n)},
        compiler_params=pltpu.CompilerParams(has_side_effects=pltpu.SideEffectType.DATAFLOW_SIDE_EFFECTING),
    )(*srcs, *lands, send_sems, recv_sems, after)
    return outs[:n], outs[n:]


def _chip_sum(ps, got, me_arr, name):
    _, hrows, cols = ps.shape
    tr = _blk(hrows, 256, 2 * SUBLANES)

    def body(me_ref, p_ref, g_ref, o_ref):
        acc = p_ref[0].astype(F32)
        for s in range(3):
            acc = acc + g_ref[s].astype(F32)
        o_ref[...] = acc

    return pl.pallas_call(
        body, name=name,
        grid_spec=pltpu.PrefetchScalarGridSpec(
            num_scalar_prefetch=1, grid=(hrows // tr,),
            in_specs=[pl.BlockSpec((1, tr, cols), lambda i, me_ref: (me_ref[0], i, 0)),
                      pl.BlockSpec((3, tr, cols), lambda i, me_ref: (0, i, 0))],
            out_specs=pl.BlockSpec((tr, cols), lambda i, me_ref: (i, 0))),
        out_shape=jax.ShapeDtypeStruct((hrows, cols), F32),
        compiler_params=pltpu.CompilerParams(dimension_semantics=("parallel",)),
    )(me_arr, ps, got)


def _pair_swap(halves):
    n = len(halves)

    def body(*refs):
        ins, outs = refs[:n], refs[n:2 * n]
        send_sems, recv_sems = refs[2 * n:]
        x, y, c = _my_pos()
        sent = []
        for a in range(n):
            cp = pltpu.make_async_remote_copy(src_ref=ins[a], dst_ref=outs[a], send_sem=send_sems.at[a], recv_sem=recv_sems.at[a],
                                              device_id=(x, y, 1 - c), device_id_type=MESH)
            cp.start()
            sent.append(cp)
        for cp in sent:
            cp.wait()

    hbm = pl.BlockSpec(memory_space=pltpu.HBM)
    return pl.pallas_call(
        body, name="grad_pair_swap", in_specs=[hbm] * n, out_specs=[hbm] * n,
        out_shape=[jax.ShapeDtypeStruct(h.shape, h.dtype) for h in halves],
        scratch_shapes=[pltpu.SemaphoreType.DMA((n,)), pltpu.SemaphoreType.DMA((n,))],
    )(*halves)


def _adamw_sharded(w, g_own, g_other, m, v, c_arr, name):
    R, C = w.shape
    hrows = R // 2
    tr = _blk(hrows, 256, SUBLANES)
    nbh = hrows // tr
    c1 = 1.0 - ADAM_B1 ** ADAM_STEP
    c2 = 1.0 - ADAM_B2 ** ADAM_STEP

    def body(c_ref, w_ref, go_ref, gx_ref, m_ref, v_ref, g_ref, d_ref, nm_ref, nv_ref):
        mine = (pl.program_id(0) // nbh) == c_ref[0]
        g_ = jnp.where(mine, go_ref[...], gx_ref[...])
        g_ref[...] = g_
        m_ = ADAM_B1 * m_ref[...] + (1.0 - ADAM_B1) * g_
        v_ = ADAM_B2 * v_ref[...] + (1.0 - ADAM_B2) * (g_ * g_)
        nm_ref[...] = m_
        nv_ref[...] = v_
        d_ref[...] = -ADAM_LR * ((m_ / c1) / (jnp.sqrt(v_ / c2) + ADAM_EPS) + ADAM_WD * w_ref[...])

    blk = pl.BlockSpec((tr, C), lambda i, c_ref: (i, 0))
    hblk = pl.BlockSpec((tr, C), lambda i, c_ref: (i % nbh, 0))
    sd = jax.ShapeDtypeStruct((R, C), F32)
    return pl.pallas_call(
        body, name=name,
        grid_spec=pltpu.PrefetchScalarGridSpec(
            num_scalar_prefetch=1, grid=(2 * nbh,),
            in_specs=[blk, hblk, hblk, blk, blk], out_specs=[blk] * 4),
        out_shape=[sd] * 4,
        compiler_params=pltpu.CompilerParams(dimension_semantics=("parallel",)),
    )(c_arr, w, g_own, g_other, m, v)


def _allreduce_small(arrs):
    n = len(arrs)
    rows = [a.shape[-2] // 8 for a in arrs]

    def piece(ref, a, p):
        start = p * rows[a]
        if rows[a] % SUBLANES == 0:
            start = pl.multiple_of(start, SUBLANES)
        return ref.at[..., pl.ds(start, rows[a]), :]

    def body(*refs):
        v_refs, o_refs, got_refs = refs[:n], refs[n:2 * n], refs[2 * n:3 * n]
        send_sems, recv_sems = refs[3 * n:]
        x, y, c = _my_pos()
        me = 4 * x + 2 * y + c

        def peer(d):
            return (x ^ (d >> 2), y ^ ((d >> 1) & 1), c ^ (d & 1))

        def lin(p):
            return 4 * p[0] + 2 * p[1] + p[2]

        sent = []
        for d in range(1, 8):
            to = peer(d)
            for a in range(n):
                cp = pltpu.make_async_remote_copy(
                    src_ref=piece(v_refs[a], a, lin(to)), dst_ref=got_refs[a].at[d],
                    send_sem=send_sems.at[0, d * n + a], recv_sem=recv_sems.at[0, d * n + a], device_id=to, device_id_type=MESH)
                cp.start()
                sent.append(cp)
        for a in range(n):
            acc = piece(v_refs[a], a, me)[...]
            for d in range(1, 8):
                sent[(d - 1) * n + a].wait_recv()
                acc = acc + got_refs[a][d]
            got_refs[a][0] = acc
            piece(o_refs[a], a, me)[...] = acc
        for d in range(1, 8):
            for a in range(n):
                cp = pltpu.make_async_remote_copy(
                    src_ref=got_refs[a].at[0], dst_ref=piece(o_refs[a], a, me),
                    send_sem=send_sems.at[1, d * n + a], recv_sem=recv_sems.at[1, d * n + a], device_id=peer(d), device_id_type=MESH)
                cp.start()
                sent.append(cp)
        for d in range(1, 8):
            for a in range(n):
                pltpu.make_async_remote_copy(
                    src_ref=got_refs[a].at[0], dst_ref=piece(o_refs[a], a, lin(peer(d))),
                    send_sem=send_sems.at[1, d * n + a], recv_sem=recv_sems.at[1, d * n + a], device_id=peer(d),
                    device_id_type=MESH).wait_recv()
        for cp in sent:
            cp.wait_send()

    vm = pl.BlockSpec(memory_space=pltpu.VMEM)
    return pl.pallas_call(
        body, name="allreduce_small", in_specs=[vm] * n, out_specs=[vm] * n,
        out_shape=[jax.ShapeDtypeStruct(a.shape, F32) for a in arrs],
        scratch_shapes=[pltpu.VMEM((8,) + a.shape[:-2] + (r, a.shape[-1]), F32) for a, r in zip(arrs, rows)]
        + [pltpu.SemaphoreType.DMA((2, 8 * n)), pltpu.SemaphoreType.DMA((2, 8 * n))],
    )(*arrs)


def kernel(x, pre_norm_w, w_in, s5_A_re, s5_A_im, s5_B_re, s5_B_im, s5_C_re, s5_C_im, s5_D, s5_log_dt, s5_glu_w, s5_glu_b, gla_gate_up, gla_gate_bias, gla_norm_w, w_out, post_norm_w, loss_target, m_pre_norm_w, m_w_in, m_s5_A_re, m_s5_A_im, m_s5_B_re, m_s5_B_im, m_s5_C_re, m_s5_C_im, m_s5_D, m_s5_log_dt, m_s5_glu_w, m_s5_glu_b, m_gla_gate_up, m_gla_gate_bias, m_gla_norm_w, m_w_out, m_post_norm_w, v_pre_norm_w, v_w_in, v_s5_A_re, v_s5_A_im, v_s5_B_re, v_s5_B_im, v_s5_C_re, v_s5_C_im, v_s5_D, v_s5_log_dt, v_s5_glu_w, v_s5_glu_b, v_gla_gate_up, v_gla_gate_bias, v_gla_norm_w, v_w_out, v_post_norm_w):
    names = ["pre_norm_w", "w_in", "s5_A_re", "s5_A_im", "s5_B_re", "s5_B_im", "s5_C_re", "s5_C_im", "s5_D", "s5_log_dt",
             "s5_glu_w", "s5_glu_b", "gla_gate_up", "gla_gate_bias", "gla_norm_w", "w_out", "post_norm_w"]
    W = dict(zip(names, (pre_norm_w, w_in, s5_A_re, s5_A_im, s5_B_re, s5_B_im, s5_C_re, s5_C_im, s5_D, s5_log_dt,
                         s5_glu_w, s5_glu_b, gla_gate_up, gla_gate_bias, gla_norm_w, w_out, post_norm_w)))
    M = dict(zip(names, (m_pre_norm_w, m_w_in, m_s5_A_re, m_s5_A_im, m_s5_B_re, m_s5_B_im, m_s5_C_re, m_s5_C_im, m_s5_D,
                         m_s5_log_dt, m_s5_glu_w, m_s5_glu_b, m_gla_gate_up, m_gla_gate_bias, m_gla_norm_w, m_w_out,
                         m_post_norm_w)))
    V = dict(zip(names, (v_pre_norm_w, v_w_in, v_s5_A_re, v_s5_A_im, v_s5_B_re, v_s5_B_im, v_s5_C_re, v_s5_C_im, v_s5_D,
                         v_s5_log_dt, v_s5_glu_w, v_s5_glu_b, v_gla_gate_up, v_gla_gate_bias, v_gla_norm_w, v_w_out,
                         v_post_norm_w)))
    sharded = ("w_in", "s5_glu_w", "w_out", "gla_gate_up")

    xb = x[0]
    tgt = loss_target[0]
    L, D = xb.shape
    DS = D // 2
    G = DS // S5_GROUP
    P = S5_STATE
    NB = DS // S5_COLS
    DV = D - DS
    DK = DV // 2
    WM = 2 * DS + 2 * DK + 2 * DV
    nsh = w_in.shape[2]

    chip = 2 * lax.axis_index("x") + lax.axis_index("y")
    own = [jnp.transpose(jnp.transpose(w_in[0]).astype(BF16)), s5_glu_w[0].astype(BF16), w_out[0].astype(BF16),
           gla_gate_up[0]]
    fill = lambda g, o: lax.dynamic_update_index_in_dim(g, o, chip, 0)
    win_ss, win_rs, win_src, win_lands, win_token = _late_gather_start(own[:1], pre_norm_w, "w_in_gather_start")
    h = _prenorm_fwd(xb, pre_norm_w, win_token)

    b_view = lambda t: jnp.transpose(t[0], (0, 2, 1)).reshape(G * S5_GROUP, P)
    b_back = lambda t: jnp.transpose(t.reshape(G, S5_GROUP, P), (0, 2, 1))[None]
    c_view = lambda t: t[0].reshape(G * S5_GROUP, P)
    c_back = lambda t: t.reshape(1, G, S5_GROUP, P)
    small = ["pre_norm_w", "post_norm_w", "s5_D", "s5_glu_b", "gla_gate_bias", "gla_norm_w", "s5_log_dt",
             "s5_A_re", "s5_A_im", "s5_B_re", "s5_B_im", "s5_C_re", "s5_C_im"]
    view = {n: (lambda t: t) for n in small[:7]}
    back = dict(view)
    view.update(s5_A_re=lambda t: t[0], s5_A_im=lambda t: t[0], s5_B_re=b_view, s5_B_im=b_view, s5_C_re=c_view, s5_C_im=c_view)
    back.update(s5_A_re=lambda t: t[None], s5_A_im=lambda t: t[None], s5_B_re=b_back, s5_B_im=b_back, s5_C_re=c_back,
                s5_C_im=c_back)
    Wv = {n: view[n](W[n]) for n in small}
    bbd_re, bbd_im, ct_re, ct_im, tab, ptab = _s5_prep_fwd(
        Wv["s5_A_re"], Wv["s5_A_im"], s5_log_dt, Wv["s5_B_re"], Wv["s5_B_im"], Wv["s5_C_re"], Wv["s5_C_im"],
        h, _blk(L, 512, SUBLANES) // SUBLANES)
    dvec = s5_D

    for d_ in (W, M, V):
        d_["w_in"], _ = lax.optimization_barrier((d_["w_in"], win_token))
    g_win = _late_gather_wait(win_ss, win_rs, win_src, win_lands,
                              [tab, W["w_in"][0], M["w_in"][0], V["w_in"][0]], "w_in_gather_wait")
    g_win = fill(_late_gather_pair(g_win, "w_in_gather_pair")[0], own[0])
    last = WM - 3 * nsh
    w_main = jnp.concatenate([g_win[0], g_win[1], g_win[2], g_win[3][:, :last]], axis=1)
    w_low = jnp.pad(g_win[3][:, last:], ((0, 0), (0, LANES - GLA_RANK)))
    late_ss, late_rs, late_src, late_lands, late_token = _late_gather_start(own[1:], g_win, "late_gather_start")
    proj_main, proj_low = _in_proj(h, w_main, w_low, late_token)
    y_pre, s_re, s_im = _s5_scan_fwd(proj_main, bbd_re, bbd_im, ct_re, ct_im, dvec, tab, ptab, DS)
    late = _late_gather_wait(late_ss, late_rs, late_src, late_lands, [y_pre], "late_gather_wait")
    late = _late_gather_pair(late, "late_gather_pair")
    g_glu, g_wout, g_gup = [fill(g, o) for g, o in zip(late, own[1:])]
    glu_w = g_glu.reshape(DS, DS)
    wout = g_wout.reshape(D, D)
    gup = jnp.moveaxis(g_gup, 0, 1).reshape(GLA_RANK, DK)
    gup_pad = jnp.pad(gup, ((0, LANES - GLA_RANK), (0, 0))).astype(BF16)
    ycat, t_pre = _s5_post_fwd(y_pre, proj_main, glu_w, s5_glu_b, DS)
    ycat, s_prev = _gla_fwd(proj_main, proj_low, gup_pad, gla_gate_bias, gla_norm_w, ycat, DS, DK, DV)
    mixed = _mm(ycat, wout, name="out_proj")
    loss11, d_mixed, dout, g_post_w = _post_fwd_bwd(mixed, xb, tgt, post_norm_w)

    d_ycat = _mm(d_mixed, wout, tb=True, name="out_proj_dx")
    g_wout_full = _mm(ycat, d_mixed, ta=True, out_dtype=BF16, name="out_proj_dw")
    d_ypre, d_s5, d_t, y1, g_glu_b = _s5_post_bwd(d_ycat, y_pre, proj_main, t_pre, glu_w, DS)
    g_glu_full = _mm(y1, d_t, ta=True, out_dtype=BF16, name="glu_dw")
    d_s5, g_D, gct_re, gct_im, gbbd_re, gbbd_im, gab_re, gab_im = _s5_scan_bwd(
        d_ypre, proj_main, s_re, s_im, bbd_re, bbd_im, ct_re, ct_im, dvec, tab, ptab, d_s5, DS)
    d_gla, d_a, g_norm_w, g_gate_bias = _gla_bwd(
        d_ycat, proj_main, proj_low, s_prev, gup_pad, gla_gate_bias, gla_norm_w, DS, DK, DV)
    d_low = _mm(d_a, gup_pad, tb=True, out_dtype=BF16, name="gate_dx")
    g_gup_pad = _mm(proj_low, d_a, ta=True, name="gate_dw")
    g_wmain, g_wlow = _in_proj_dw(h, d_s5, d_gla, d_low)

    g_win_sh = jnp.stack([g_wmain[:, :nsh], g_wmain[:, nsh:2 * nsh], g_wmain[:, 2 * nsh:3 * nsh],
                          jnp.concatenate([g_wmain[:, 3 * nsh:], g_wlow[:, :GLA_RANK]], axis=1)])
    gs = [g_win_sh,
          g_glu_full.reshape(4, DS // 4, DS),
          g_wout_full.reshape(4, D // 4, D),
          jnp.moveaxis(g_gup_pad[:GLA_RANK].reshape(GLA_RANK, 4, DK // 4), 1, 0)]
    c_arr = lax.axis_index("c").astype(jnp.int32).reshape(1)
    me_arr = chip.astype(jnp.int32).reshape(1)
    got = _pair_exchange(gs)
    pss = [_pair_add(g, r, c_arr, "grad_pair_add_" + n) for n, g, r in zip(sharded, gs, got)]
    send_sems, recv_sems, pss, lands, token = _chip_scatter_start(pss)

    dh = _in_proj_dx(d_s5, d_gla, d_low, w_main, w_low, token)
    grad_x, g_pre_w = _prenorm_bwd(xb, dh, dout, pre_norm_w)
    pss, rcv = _chip_scatter_wait(send_sems, recv_sems, pss, lands, g_pre_w)

    g_a, g_bc, g_ldt = _s5_prep_bwd(Wv["s5_A_re"], Wv["s5_A_im"], s5_log_dt, Wv["s5_B_re"], Wv["s5_B_im"],
                                    gbbd_re, gbbd_im, gct_re, gct_im, gab_re, gab_im)

    loss = lax.psum(loss11[0, 0], ("x", "y", "c"))

    g_vecs = jnp.concatenate([g_pre_w, g_post_w, g_D, g_glu_b, g_gate_bias, g_norm_w, g_ldt], axis=1)
    lanes_pad = -g_vecs.shape[1] % (8 * SUBLANES * LANES)
    g_vecs = jnp.pad(g_vecs, ((0, 0), (0, lanes_pad))).reshape(-1, LANES)
    r_vecs, r_a, r_bc = _allreduce_small([g_vecs, g_a, g_bc])
    outs4 = _adamw_small(r_vecs.reshape(1, -1), r_a, r_bc, [Wv[n] for n in small],
                         [view[n](M[n]) for n in small], [view[n](V[n]) for n in small])
    G_out, D_out, M_out, V_out = [{n: back[n](t) for n, t in zip(small, o)} for o in outs4]

    halves = [_chip_sum(p, r, me_arr, "grad_chip_sum_" + n) for n, p, r in zip(sharded, pss, rcv)]
    others = _pair_swap(halves)
    for n, g_own, g_other in zip(sharded, halves, others):
        g_, d_, m_, v_ = _adamw_sharded(W[n][0], g_own, g_other, M[n][0], V[n][0], c_arr, "adamw_" + n)
        G_out[n], D_out[n], M_out[n], V_out[n] = g_[None], d_[None], m_[None], v_[None]

    return (loss, grad_x[None], *[G_out[n] for n in names], *[D_out[n] for n in names],
            *[M_out[n] for n in names], *[V_out[n] for n in names])
```

```python
import functools
import math

import jax
import jax.numpy as jnp
from jax import lax
from jax.experimental import pallas as pl
from jax.experimental.pallas import tpu as pltpu

F32 = jnp.float32
BF16 = jnp.bfloat16
HI = lax.Precision.HIGHEST
MESH = pl.DeviceIdType.MESH

EPS = 1e-6
S5_GROUP = 16
S5_STATE = 64
GLA_HK = 128
GLA_HV = 256
GLA_RANK = 16
GLA_TAU = 16.0
GLA_CHUNK = 64
GLA_STEP_CHUNKS = 4
LANES = 128
SUBLANES = 8
S5_COLS = 128
S5_LANES = (S5_COLS // S5_GROUP) * S5_STATE

ADAM_LR = 0.001
ADAM_B1 = 0.9
ADAM_B2 = 0.999
ADAM_EPS = 1e-08
ADAM_WD = 0.01
ADAM_STEP = 10

GELU_K = math.sqrt(2.0 / math.pi)
GELU_C = 0.044715


def _blk(n, pref, unit=LANES):
    best = None
    b = unit
    while b <= min(n, pref):
        if n % b == 0:
            best = b
        b += unit
    return best if best is not None else n


def _dot(a, b, dn=(((1,), (0,)), ((), ()))):
    return lax.dot_general(a.astype(BF16), b.astype(BF16), dn, preferred_element_type=F32)


def _dot_hi(a, b, dn=(((1,), (0,)), ((), ()))):
    return lax.dot_general(a, b, dn, precision=HI, preferred_element_type=F32)


NN = (((1,), (0,)), ((), ()))
NT = (((1,), (1,)), ((), ()))
TN = (((0,), (0,)), ((), ()))


def _sigmoid(x):
    return 1.0 / (1.0 + jnp.exp(-x))


def _gelu(y):
    return 0.5 * y * (1.0 + jnp.tanh(GELU_K * (y + GELU_C * y * y * y)))


def _gelu_grad(y):
    th = jnp.tanh(GELU_K * (y + GELU_C * y * y * y))
    return 0.5 * (1.0 + th) + 0.5 * y * (1.0 - th * th) * GELU_K * (1.0 + 3.0 * GELU_C * y * y)


def _mm(a, b, *, name, ta=False, tb=False, out_dtype=F32, bm=1024, bn=1024, bk=2048):
    if ta:
        K, M = a.shape
    else:
        M, K = a.shape
    if tb:
        N, K2 = b.shape
    else:
        K2, N = b.shape
    assert K == K2, (a.shape, b.shape, ta, tb)
    bm, bn, bk = _blk(M, bm), _blk(N, bn), _blk(K, bk)
    nk = K // bk
    dn = (((0 if ta else 1,), (1 if tb else 0,)), ((), ()))

    def body(a_ref, b_ref, o_ref, *acc):
        if nk == 1:
            o_ref[...] = _dot(a_ref[...], b_ref[...], dn).astype(out_dtype)
            return
        acc_ref, = acc
        k = pl.program_id(2)

        @pl.when(k == 0)
        def _():
            acc_ref[...] = jnp.zeros_like(acc_ref)

        acc_ref[...] += _dot(a_ref[...], b_ref[...], dn)

        @pl.when(k == nk - 1)
        def _():
            o_ref[...] = acc_ref[...].astype(out_dtype)

    a_spec = pl.BlockSpec((bk, bm), lambda i, j, k: (k, i)) if ta else pl.BlockSpec((bm, bk), lambda i, j, k: (i, k))
    b_spec = pl.BlockSpec((bn, bk), lambda i, j, k: (j, k)) if tb else pl.BlockSpec((bk, bn), lambda i, j, k: (k, j))
    return pl.pallas_call(
        body,
        name=name,
        grid=(M // bm, N // bn, nk),
        in_specs=[a_spec, b_spec],
        out_specs=pl.BlockSpec((bm, bn), lambda i, j, k: (i, j)),
        out_shape=jax.ShapeDtypeStruct((M, N), out_dtype),
        scratch_shapes=[pltpu.VMEM((bm, bn), F32)] if nk > 1 else [],
        compiler_params=pltpu.CompilerParams(dimension_semantics=("parallel", "parallel", "arbitrary")),
    )(a, b)


def _in_proj(h, w_main, w_low, after):
    M, K = h.shape
    N = w_main.shape[1]
    bm, bn = _blk(M, 1024), _blk(N, 1024)

    def body(h_ref, w_ref, wl_ref, _after_ref, o_ref, ol_ref):
        hv = h_ref[...]
        o_ref[...] = _dot(hv, w_ref[...])

        @pl.when(pl.program_id(1) == 0)
        def _():
            ol_ref[...] = _dot(hv, wl_ref[...])

    return pl.pallas_call(
        body, name="in_proj", grid=(M // bm, N // bn),
        in_specs=[pl.BlockSpec((bm, K), lambda i, j: (i, 0)), pl.BlockSpec((K, bn), lambda i, j: (0, j)),
                  pl.BlockSpec((K, LANES), lambda i, j: (0, 0)), pl.BlockSpec(memory_space=pl.ANY)],
        out_specs=[pl.BlockSpec((bm, bn), lambda i, j: (i, j)), pl.BlockSpec((bm, LANES), lambda i, j: (i, 0))],
        out_shape=[jax.ShapeDtypeStruct((M, N), F32), jax.ShapeDtypeStruct((M, LANES), F32)],
        compiler_params=pltpu.CompilerParams(dimension_semantics=("parallel", "arbitrary")),
    )(h, w_main, w_low, after)


def _in_proj_dx(a1, a2, al, b, bl, after, *, bm=1024, bn=1024, bk=2048):
    M, K1 = a1.shape
    K2 = a2.shape[1]
    N = b.shape[0]
    bm, bn = _blk(M, bm), _blk(N, bn)
    bk = _blk(math.gcd(K1, K2), bk)
    nk1, nk = K1 // bk, (K1 + K2) // bk

    def body(a1_ref, a2_ref, al_ref, b_ref, bl_ref, _after_ref, o_ref, acc_ref):
        k = pl.program_id(2)

        @pl.when(k == 0)
        def _():
            acc_ref[...] = _dot(al_ref[...], bl_ref[...], NT)

        @pl.when(k < nk1)
        def _():
            acc_ref[...] += _dot(a1_ref[...], b_ref[...], NT)

        @pl.when(k >= nk1)
        def _():
            acc_ref[...] += _dot(a2_ref[...], b_ref[...], NT)

        @pl.when(k == nk - 1)
        def _():
            o_ref[...] = acc_ref[...]

    return pl.pallas_call(
        body, name="in_proj_dx", grid=(M // bm, N // bn, nk),
        in_specs=[pl.BlockSpec((bm, bk), lambda i, j, k: (i, jnp.minimum(k, nk1 - 1))),
                  pl.BlockSpec((bm, bk), lambda i, j, k: (i, jnp.maximum(k - nk1, 0))),
                  pl.BlockSpec((bm, LANES), lambda i, j, k: (i, 0)),
                  pl.BlockSpec((bn, bk), lambda i, j, k: (j, k)),
                  pl.BlockSpec((bn, LANES), lambda i, j, k: (j, 0)),
                  pl.BlockSpec(memory_space=pl.ANY)],
        out_specs=pl.BlockSpec((bm, bn), lambda i, j, k: (i, j)),
        out_shape=jax.ShapeDtypeStruct((M, N), F32),
        scratch_shapes=[pltpu.VMEM((bm, bn), F32)],
        compiler_params=pltpu.CompilerParams(dimension_semantics=("parallel", "parallel", "arbitrary")),
    )(a1, a2, al, b, bl, after)


def _in_proj_dw(a, b1, b2, bl, *, bm=1024, bn=1024, bk=2048):
    K, M = a.shape
    N1, N2 = b1.shape[1], b2.shape[1]
    bm, bk = _blk(M, bm), _blk(K, bk)
    bn = _blk(math.gcd(N1, N2), bn)
    nj1, nj = N1 // bn, (N1 + N2) // bn
    nk = K // bk

    def body(a_ref, b1_ref, b2_ref, bl_ref, o_ref, ol_ref, acc_ref, accl_ref):
        j = pl.program_id(1)
        k = pl.program_id(2)

        @pl.when(k == 0)
        def _():
            acc_ref[...] = jnp.zeros_like(acc_ref)

        @pl.when(j < nj1)
        def _():
            acc_ref[...] += _dot(a_ref[...], b1_ref[...], TN)

        @pl.when(j >= nj1)
        def _():
            acc_ref[...] += _dot(a_ref[...], b2_ref[...], TN)

        @pl.when(k == nk - 1)
        def _():
            o_ref[...] = acc_ref[...].astype(BF16)

        @pl.when(j == 0)
        def _():
            low = _dot(a_ref[...], bl_ref[...], TN)

            @pl.when(k == 0)
            def _():
                accl_ref[...] = low

            @pl.when(k > 0)
            def _():
                accl_ref[...] += low

            @pl.when(k == nk - 1)
            def _():
                ol_ref[...] = accl_ref[...].astype(BF16)

    return pl.pallas_call(
        body, name="in_proj_dw", grid=(M // bm, nj, nk),
        in_specs=[pl.BlockSpec((bk, bm), lambda i, j, k: (k, i)),
                  pl.BlockSpec((bk, bn), lambda i, j, k: (jnp.where(j < nj1, k, nk - 1), jnp.minimum(j, nj1 - 1))),
                  pl.BlockSpec((bk, bn), lambda i, j, k: (jnp.where(j >= nj1, k, 0), jnp.maximum(j - nj1, 0))),
                  pl.BlockSpec((bk, LANES), lambda i, j, k: (jnp.where(j == 0, k, nk - 1), 0))],
        out_specs=[pl.BlockSpec((bm, bn), lambda i, j, k: (i, j)), pl.BlockSpec((bm, LANES), lambda i, j, k: (i, 0))],
        out_shape=[jax.ShapeDtypeStruct((M, N1 + N2), BF16), jax.ShapeDtypeStruct((M, LANES), BF16)],
        scratch_shapes=[pltpu.VMEM((bm, bn), F32), pltpu.VMEM((bm, LANES), F32)],
        compiler_params=pltpu.CompilerParams(dimension_semantics=("parallel", "arbitrary", "arbitrary")),
    )(a, b1, b2, bl)


def _prenorm_fwd(x, w, after):
    L, D = x.shape
    tr = _blk(L, 256, SUBLANES)

    def body(x_ref, w_ref, _after_ref, h_ref):
        xv = x_ref[...]
        r = lax.rsqrt(jnp.mean(xv * xv, axis=-1, keepdims=True) + EPS)
        h_ref[...] = (xv * r * w_ref[...]).astype(BF16)

    return pl.pallas_call(
        body, name="prenorm_fwd", grid=(L // tr,),
        in_specs=[pl.BlockSpec((tr, D), lambda i: (i, 0)), pl.BlockSpec((1, D), lambda i: (0, 0)),
                  pl.BlockSpec(memory_space=pl.ANY)],
        out_specs=pl.BlockSpec((tr, D), lambda i: (i, 0)),
        out_shape=jax.ShapeDtypeStruct((L, D), BF16),
        compiler_params=pltpu.CompilerParams(dimension_semantics=("parallel",)),
    )(x, w, after)


def _post_fwd_bwd(mixed, x, target, w):
    L, D = x.shape
    tr = _blk(L, 256, SUBLANES)
    nsteps = L // tr

    def body(mx_ref, x_ref, t_ref, w_ref, loss_ref, dm_ref, dout_ref, gw_ref, acc_ref):
        i = pl.program_id(0)

        @pl.when(i == 0)
        def _():
            acc_ref[...] = jnp.zeros_like(acc_ref)
            gw_ref[...] = jnp.zeros_like(gw_ref)

        mx = mx_ref[...]
        wv = w_ref[...]
        r = lax.rsqrt(jnp.mean(mx * mx, axis=-1, keepdims=True) + EPS)
        n = mx * r
        err = x_ref[...] + n * wv - t_ref[...]
        acc_ref[...] += jnp.sum(err * err, axis=0, keepdims=True)
        dout = err * (1.0 / D)
        dout_ref[...] = dout
        gw_ref[...] += jnp.sum(dout * n, axis=0, keepdims=True)
        dn = dout * wv
        dm_ref[...] = (r * (dn - n * jnp.mean(dn * n, axis=-1, keepdims=True))).astype(BF16)

        @pl.when(i == nsteps - 1)
        def _():
            loss_ref[...] = jnp.sum(acc_ref[...], axis=-1, keepdims=True) * (0.5 / D)

    row = pl.BlockSpec((tr, D), lambda i: (i, 0))
    vec = pl.BlockSpec((1, D), lambda i: (0, 0))
    return pl.pallas_call(
        body, name="post_fwd_bwd", grid=(nsteps,),
        in_specs=[row, row, row, vec],
        out_specs=[pl.BlockSpec((1, 1), lambda i: (0, 0)), row, row, vec],
        out_shape=[jax.ShapeDtypeStruct((1, 1), F32), jax.ShapeDtypeStruct((L, D), BF16),
                   jax.ShapeDtypeStruct((L, D), F32), jax.ShapeDtypeStruct((1, D), F32)],
        scratch_shapes=[pltpu.VMEM((1, D), F32)],
        compiler_params=pltpu.CompilerParams(dimension_semantics=("arbitrary",)),
    )(mixed, x, target, w)


def _prenorm_bwd(x, dh, dout, w):
    L, D = x.shape
    tr = _blk(L, 256, SUBLANES)

    def body(x_ref, a_ref, dout_ref, w_ref, gx_ref, gw_ref):
        i = pl.program_id(0)

        @pl.when(i == 0)
        def _():
            gw_ref[...] = jnp.zeros_like(gw_ref)

        xv = x_ref[...]
        r = lax.rsqrt(jnp.mean(xv * xv, axis=-1, keepdims=True) + EPS)
        n = xv * r
        dh = a_ref[...]
        gw_ref[...] += jnp.sum(dh * n, axis=0, keepdims=True)
        dn = dh * w_ref[...]
        gx_ref[...] = dout_ref[...] + r * (dn - n * jnp.mean(dn * n, axis=-1, keepdims=True))

    row = pl.BlockSpec((tr, D), lambda i: (i, 0))
    vec = pl.BlockSpec((1, D), lambda i: (0, 0))
    return pl.pallas_call(
        body, name="prenorm_bwd", grid=(L // tr,),
        in_specs=[row, row, row, vec],
        out_specs=[row, vec],
        out_shape=[jax.ShapeDtypeStruct((L, D), F32), jax.ShapeDtypeStruct((1, D), F32)],
        compiler_params=pltpu.CompilerParams(dimension_semantics=("arbitrary",)),
    )(x, dh, dout, w)


def _s5_disc(a_re_raw, a_im, dt):
    a_re = jnp.minimum(a_re_raw, -1e-4)
    mag = jnp.exp(a_re * dt)
    ph = a_im * dt
    ab_re = mag * jnp.cos(ph)
    ab_im = mag * jnp.sin(ph)
    inv_n = 1.0 / (a_re * a_re + a_im * a_im)
    ia_re = a_re * inv_n
    ia_im = -a_im * inv_n
    n_re = ab_re - 1.0
    f_re = n_re * ia_re - ab_im * ia_im
    f_im = n_re * ia_im + ab_im * ia_re
    return a_re, ab_re, ab_im, f_re, f_im, ia_re, ia_im


def _iota2(shape, dim):
    return lax.broadcasted_iota(jnp.int32, shape, dim)


def _group_mask(rows, rows_per_group):
    shift = rows_per_group.bit_length() - 1
    return (_iota2((rows, S5_LANES), 0) >> shift) == (_iota2((rows, S5_LANES), 1) >> (S5_STATE.bit_length() - 1))


def _lane_tiler(dtype):
    return ((_iota2((S5_STATE, S5_LANES), 1) & (S5_STATE - 1)) == _iota2((S5_STATE, S5_LANES), 0)).astype(dtype)


def _row_to_col(row, n):
    eye = (_iota2((n, n), 0) == _iota2((n, n), 1)).astype(F32)
    return jnp.sum(eye * row, axis=1, keepdims=True)


def _group_repeat(G):
    return ((_iota2((G * S5_GROUP, G), 0) >> (S5_GROUP.bit_length() - 1)) == _iota2((G * S5_GROUP, G), 1)).astype(F32)


S5_TABS = 18


def _s5_prep_fwd(a_re, a_im, log_dt, b_re, b_im, c_re, c_im, after, seg):
    G, P = a_re.shape
    nb = G * S5_GROUP // S5_COLS
    g8 = S5_COLS // S5_GROUP
    assert seg & (seg - 1) == 0, seg

    def body(are_ref, aim_ref, ldt_ref, bre_ref, bim_ref, cre_ref, cim_ref, _after_ref,
             bbre_ref, bbim_ref, ctre_ref, ctim_ref, tab_ref, pt_ref):
        dt = jnp.exp(_row_to_col(ldt_ref[...], G))
        _, ab_re, ab_im, f_re, f_im, _, _ = _s5_disc(are_ref[...], aim_ref[...], dt)
        rep = _group_repeat(G)
        fx_re = _dot_hi(rep, f_re)
        fx_im = _dot_hi(rep, f_im)
        br, bi = bre_ref[...], bim_ref[...]
        bb_re = fx_re * br - fx_im * bi
        bb_im = fx_re * bi + fx_im * br
        tile_bf = _lane_tiler(BF16)
        mask = _group_mask(S5_COLS, S5_GROUP)
        for jb in range(nb):
            rs = slice(jb * S5_COLS, (jb + 1) * S5_COLS)
            for src, dst in ((bb_re[rs], bbre_ref), (bb_im[rs], bbim_ref), (cre_ref[rs, :], ctre_ref), (cim_ref[rs, :], ctim_ref)):
                dst[jb] = jnp.where(mask, _dot(src, tile_bf), 0.0).astype(BF16)

        tile_f = _lane_tiler(F32)
        mask8 = _group_mask(g8, 1)
        row = _iota2((SUBLANES, S5_LANES), 0)
        slab = (SUBLANES, S5_LANES)
        cmul = lambda p, q: (p[0] * q[0] - p[1] * q[1], p[0] * q[1] + p[1] * q[0])
        for jb in range(nb):
            gs = slice(jb * g8, (jb + 1) * g8)

            def lanes(m):
                v = jnp.sum(jnp.where(mask8, _dot_hi(m[gs], tile_f), 0.0), axis=0, keepdims=True)
                return jnp.broadcast_to(v, slab)

            a1 = (lanes(ab_re), lanes(ab_im))
            tab_ref[jb, 0], tab_ref[jb, 1] = a1

            def powers(i, p):
                off = pl.multiple_of(i * SUBLANES, SUBLANES)
                pt_ref[jb, 0, pl.ds(off, SUBLANES), :] = p[0]
                pt_ref[jb, 1, pl.ds(off, SUBLANES), :] = p[1]
                return cmul(p, a1)

            lax.fori_loop(0, seg, powers, a1)
            aseg = a1
            for _ in range(seg.bit_length() - 1):
                aseg = cmul(aseg, aseg)
            pw = [aseg]
            for _ in range(1, SUBLANES):
                pw.append(cmul(pw[-1], aseg))
            for lvl, k in enumerate((1, 2, 4)):
                tab_ref[jb, 2 + 2 * lvl] = jnp.where(row >= k, pw[k - 1][0], 0.0)
                tab_ref[jb, 3 + 2 * lvl] = jnp.where(row >= k, pw[k - 1][1], 0.0)
                tab_ref[jb, 10 + 2 * lvl] = jnp.where(row < SUBLANES - k, pw[k - 1][0], 0.0)
                tab_ref[jb, 11 + 2 * lvl] = jnp.where(row < SUBLANES - k, -pw[k - 1][1], 0.0)
            f_r = f_i = r_r = r_i = jnp.zeros(slab, F32)
            for i in range(SUBLANES):
                f_r = jnp.where(row == i, pw[i][0], f_r)
                f_i = jnp.where(row == i, pw[i][1], f_i)
                r_r = jnp.where(row == i, pw[SUBLANES - 1 - i][0], r_r)
                r_i = jnp.where(row == i, -pw[SUBLANES - 1 - i][1], r_i)
            tab_ref[jb, 8] = f_r
            tab_ref[jb, 9] = f_i
            tab_ref[jb, 16] = r_r
            tab_ref[jb, 17] = r_i

    vm = pl.BlockSpec(memory_space=pltpu.VMEM)
    bd = jax.ShapeDtypeStruct((nb, S5_COLS, S5_LANES), BF16)
    return pl.pallas_call(
        body, name="s5_prep_fwd",
        in_specs=[vm] * 7 + [pl.BlockSpec(memory_space=pl.ANY)], out_specs=[vm] * 6,
        out_shape=[bd, bd, bd, bd, jax.ShapeDtypeStruct((nb, S5_TABS, SUBLANES, S5_LANES), F32),
                   jax.ShapeDtypeStruct((nb, 2, seg * SUBLANES, S5_LANES), F32)],
    )(a_re, a_im, log_dt, b_re, b_im, c_re, c_im, after)


def _s5_prep_bwd(a_re, a_im, log_dt, b_re, b_im, gbb_re, gbb_im, gct_re, gct_im, gab_re, gab_im):
    G, P = a_re.shape
    nb = G * S5_GROUP // S5_COLS
    g8 = S5_COLS // S5_GROUP

    def body(are_ref, aim_ref, ldt_ref, bre_ref, bim_ref, gbr_ref, gbi_ref, gcr_ref, gci_ref, gar_ref, gai_ref,
             o_a, o_bc, o_ldt):
        dt = jnp.exp(_row_to_col(ldt_ref[...], G))
        a_raw = are_ref[...]
        a_imv = aim_ref[...]
        a_re_c, ab_re, ab_im, f_re, f_im, ia_re, ia_im = _s5_disc(a_raw, a_imv, dt)
        tile_f = _lane_tiler(F32)
        mask = _group_mask(S5_COLS, S5_GROUP)
        mask8 = _group_mask(g8, 1)
        for jb in range(nb):
            rs = slice(jb * S5_COLS, (jb + 1) * S5_COLS)
            gs = slice(jb * g8, (jb + 1) * g8)
            ls = slice(jb * S5_LANES, (jb + 1) * S5_LANES)
            for k, src in enumerate((gbr_ref, gbi_ref, gcr_ref, gci_ref)):
                o_bc[k, rs, :] = _dot_hi(jnp.where(mask, src[jb], 0.0), tile_f, NT)
            for k, src in enumerate((gar_ref, gai_ref)):
                o_a[k, gs, :] = _dot_hi(jnp.where(mask8, src[:, ls], 0.0), tile_f, NT)
        rep = _group_repeat(G)
        fx_re = _dot_hi(rep, f_re)
        fx_im = _dot_hi(rep, f_im)
        gbr, gbi = o_bc[0], o_bc[1]
        br, bi = bre_ref[...], bim_ref[...]
        o_bc[0] = fx_re * gbr + fx_im * gbi
        o_bc[1] = fx_re * gbi - fx_im * gbr
        gf_re = _dot_hi(rep, br * gbr + bi * gbi, TN)
        gf_im = _dot_hi(rep, br * gbi - bi * gbr, TN)
        gab_r = o_a[0] + ia_re * gf_re + ia_im * gf_im
        gab_i = o_a[1] + ia_re * gf_im - ia_im * gf_re
        q_re = f_re * ia_re - f_im * ia_im
        q_im = f_re * ia_im + f_im * ia_re
        ga_re = -(q_re * gf_re + q_im * gf_im)
        ga_im = -(q_re * gf_im - q_im * gf_re)
        gth_re = ab_re * gab_r + ab_im * gab_i
        gth_im = ab_re * gab_i - ab_im * gab_r
        ga_re = ga_re + dt * gth_re
        ga_im = ga_im + dt * gth_im
        gdt = jnp.sum(a_re_c * gth_re + a_imv * gth_im, axis=-1, keepdims=True)
        eye = (_iota2((G, G), 0) == _iota2((G, G), 1)).astype(F32)
        o_ldt[...] = jnp.sum(eye * (gdt * dt), axis=0, keepdims=True)
        slope = jnp.where(a_raw < -1e-4, 1.0, jnp.where(a_raw == -1e-4, 0.5, 0.0))
        o_a[0] = ga_re * slope
        o_a[1] = ga_im

    vm = pl.BlockSpec(memory_space=pltpu.VMEM)
    return pl.pallas_call(
        body, name="s5_prep_bwd",
        in_specs=[vm] * 11, out_specs=[vm] * 3,
        out_shape=[jax.ShapeDtypeStruct((2, G, P), F32), jax.ShapeDtypeStruct((4, G * S5_GROUP, P), F32),
                   jax.ShapeDtypeStruct((1, G), F32)],
    )(a_re, a_im, log_dt, b_re, b_im, gbb_re, gbb_im, gct_re, gct_im, gab_re, gab_im)


def _scan8(xr, xi, tab_ref, base, shifts):
    for lvl, sh in enumerate(shifts):
        mr = tab_ref[0, base + 2 * lvl]
        mi = tab_ref[0, base + 2 * lvl + 1]
        ar = pltpu.roll(xr, sh, 0)
        ai = pltpu.roll(xi, sh, 0)
        xr, xi = xr + mr * ar - mi * ai, xi + mr * ai + mi * ar
    return xr, xi


def _to_segments(src_ref, dst_ref, seg):
    for i in range(seg):
        dst_ref[i * SUBLANES:(i + 1) * SUBLANES, :] = src_ref[pl.ds(i, SUBLANES, stride=seg), :]


def _from_segments(src_ref, dst_ref, seg):
    for i in range(seg):
        dst_ref[pl.ds(i, SUBLANES, stride=seg), :] = src_ref[i * SUBLANES:(i + 1) * SUBLANES, :]


def _slab(i):
    return pl.ds(pl.multiple_of(i * SUBLANES, SUBLANES), SUBLANES)


def _s5_scan_fwd(proj_main, bbd_re, bbd_im, cbd_re, cbd_im, dvec, tab, ptab, DS):
    L = proj_main.shape[0]
    nb = DS // S5_COLS
    tb = _blk(L, 512, SUBLANES)
    nt = L // tb
    seg = tb // SUBLANES

    def body(u_ref, bre_ref, bim_ref, cre_ref, cim_ref, d_ref, tab_ref, pt_ref, y_ref, sre_ref, sim_ref,
             up_ref, yp_ref, car_ref):
        t = pl.program_id(1)

        @pl.when(t == 0)
        def _():
            car_ref[...] = jnp.zeros_like(car_ref)

        _to_segments(u_ref, up_ref, seg)
        up = up_ref[...]
        sre_ref[...] = _dot(up, bre_ref[0])
        sim_ref[...] = _dot(up, bim_ref[0])
        ar, ai = tab_ref[0, 0], tab_ref[0, 1]

        def pass1(i, x):
            xr = ar * x[0] - ai * x[1] + sre_ref[_slab(i), :]
            xi = ar * x[1] + ai * x[0] + sim_ref[_slab(i), :]
            sre_ref[_slab(i), :] = xr
            sim_ref[_slab(i), :] = xi
            return xr, xi

        zero = jnp.zeros((SUBLANES, S5_LANES), F32)
        er, ei = lax.fori_loop(0, seg, pass1, (zero, zero))
        cin_r, cin_i = car_ref[0], car_ref[1]
        sr, si = _scan8(er, ei, tab_ref, 2, (1, 2, 4))
        pr, pi = tab_ref[0, 8], tab_ref[0, 9]
        sr, si = sr + pr * cin_r - pi * cin_i, si + pr * cin_i + pi * cin_r
        row0 = _iota2((SUBLANES, S5_LANES), 0) == 0
        cr = jnp.where(row0, cin_r, pltpu.roll(sr, 1, 0))
        ci = jnp.where(row0, cin_i, pltpu.roll(si, 1, 0))
        car_ref[0] = jnp.broadcast_to(sr[SUBLANES - 1:SUBLANES, :], sr.shape)
        car_ref[1] = jnp.broadcast_to(si[SUBLANES - 1:SUBLANES, :], si.shape)

        def pass2(i, _):
            qr, qi = pt_ref[0, 0, _slab(i), :], pt_ref[0, 1, _slab(i), :]
            sre_ref[_slab(i), :] += qr * cr - qi * ci
            sim_ref[_slab(i), :] += qr * ci + qi * cr
            return 0

        lax.fori_loop(0, seg, pass2, 0, unroll=4)
        yp_ref[...] = _dot(sre_ref[...], cre_ref[0], NT) - _dot(sim_ref[...], cim_ref[0], NT) + d_ref[...] * up
        _from_segments(yp_ref, y_ref, seg)

    return pl.pallas_call(
        body, name="s5_scan_fwd", grid=(nb, nt),
        in_specs=[
            pl.BlockSpec((tb, S5_COLS), lambda j, t: (t, j)),
            pl.BlockSpec((1, S5_COLS, S5_LANES), lambda j, t: (j, 0, 0)),
            pl.BlockSpec((1, S5_COLS, S5_LANES), lambda j, t: (j, 0, 0)),
            pl.BlockSpec((1, S5_COLS, S5_LANES), lambda j, t: (j, 0, 0)),
            pl.BlockSpec((1, S5_COLS, S5_LANES), lambda j, t: (j, 0, 0)),
            pl.BlockSpec((1, S5_COLS), lambda j, t: (0, j)),
            pl.BlockSpec((1, S5_TABS, SUBLANES, S5_LANES), lambda j, t: (j, 0, 0, 0)),
            pl.BlockSpec((1, 2, tb, S5_LANES), lambda j, t: (j, 0, 0, 0)),
        ],
        out_specs=[
            pl.BlockSpec((tb, S5_COLS), lambda j, t: (t, j)),
            pl.BlockSpec((tb, S5_LANES), lambda j, t: (t, j)),
            pl.BlockSpec((tb, S5_LANES), lambda j, t: (t, j)),
        ],
        out_shape=[jax.ShapeDtypeStruct((L, DS), F32),
                   jax.ShapeDtypeStruct((L, nb * S5_LANES), F32),
                   jax.ShapeDtypeStruct((L, nb * S5_LANES), F32)],
        scratch_shapes=[pltpu.VMEM((tb, S5_COLS), F32), pltpu.VMEM((tb, S5_COLS), F32),
                        pltpu.VMEM((2, SUBLANES, S5_LANES), F32)],
        compiler_params=pltpu.CompilerParams(dimension_semantics=("parallel", "arbitrary")),
    )(proj_main, bbd_re, bbd_im, cbd_re, cbd_im, dvec, tab, ptab)


def _s5_scan_bwd(dy, proj_main, s_re, s_im, bbd_re, bbd_im, cbd_re, cbd_im, dvec, tab, ptab, d_s5, DS):
    L = proj_main.shape[0]
    nb = DS // S5_COLS
    tb = _blk(L, 512, SUBLANES)
    nt = L // tb
    seg = tb // SUBLANES
    tb8 = tb // SUBLANES

    def body(dy_ref, u_ref, sre_ref, sim_ref, pre_ref, pim_ref, bre_ref, bim_ref, cre_ref, cim_ref, d_ref, tab_ref, pt_ref,
             _ds5_ref, du_ref, gd_ref, gcre_ref, gcim_ref, gbre_ref, gbim_ref, gare_ref, gaim_ref,
             lre_ref, lim_ref, up_ref, dyp_ref, dup_ref, duo_ref, car_ref):
        t = pl.program_id(1)

        @pl.when(t == 0)
        def _():
            car_ref[...] = jnp.zeros_like(car_ref)
            gd_ref[...] = jnp.zeros_like(gd_ref)
            gcre_ref[...] = jnp.zeros_like(gcre_ref)
            gcim_ref[...] = jnp.zeros_like(gcim_ref)
            gbre_ref[...] = jnp.zeros_like(gbre_ref)
            gbim_ref[...] = jnp.zeros_like(gbim_ref)
            gare_ref[...] = jnp.zeros_like(gare_ref)
            gaim_ref[...] = jnp.zeros_like(gaim_ref)

        _to_segments(dy_ref, dyp_ref, seg)
        _to_segments(u_ref, up_ref, seg)
        dyv = dyp_ref[...]
        u = up_ref[...]
        gd_ref[...] += jnp.sum(dyv * u, axis=0, keepdims=True)
        lre_ref[...] = _dot(dyv, cre_ref[0])
        lim_ref[...] = -_dot(dyv, cim_ref[0])
        gcre_ref[0] += _dot(dyv, sre_ref[...], TN)
        gcim_ref[0] -= _dot(dyv, sim_ref[...], TN)
        ar, ai = tab_ref[0, 0], -tab_ref[0, 1]

        def pass1(k, x):
            i = seg - 1 - k
            xr = ar * x[0] - ai * x[1] + lre_ref[_slab(i), :]
            xi = ar * x[1] + ai * x[0] + lim_ref[_slab(i), :]
            lre_ref[_slab(i), :] = xr
            lim_ref[_slab(i), :] = xi
            return xr, xi

        zero = jnp.zeros((SUBLANES, S5_LANES), F32)
        er, ei = lax.fori_loop(0, seg, pass1, (zero, zero))
        cin_r, cin_i = car_ref[0], car_ref[1]
        lr, li = _scan8(er, ei, tab_ref, 10, (7, 6, 4))
        pr, pi = tab_ref[0, 16], tab_ref[0, 17]
        lr, li = lr + pr * cin_r - pi * cin_i, li + pr * cin_i + pi * cin_r
        rows = _iota2((SUBLANES, S5_LANES), 0)
        cr = jnp.where(rows == SUBLANES - 1, cin_r, pltpu.roll(lr, SUBLANES - 1, 0))
        ci = jnp.where(rows == SUBLANES - 1, cin_i, pltpu.roll(li, SUBLANES - 1, 0))
        car_ref[0] = jnp.broadcast_to(lr[0:1, :], lr.shape)
        car_ref[1] = jnp.broadcast_to(li[0:1, :], li.shape)

        first = (t == nt - 1).astype(F32)
        head_re = jnp.broadcast_to(pre_ref[SUBLANES - 1:SUBLANES, :], zero.shape) * (1.0 - first)
        head_im = jnp.broadcast_to(pim_ref[SUBLANES - 1:SUBLANES, :], zero.shape) * (1.0 - first)
        last = _slab(seg - 1)
        sp0_re = jnp.where(rows == 0, head_re, pltpu.roll(sre_ref[last, :], 1, 0))
        sp0_im = jnp.where(rows == 0, head_im, pltpu.roll(sim_ref[last, :], 1, 0))

        def pass2(i, acc):
            j = seg - 1 - i
            qr, qi = pt_ref[0, 0, _slab(j), :], -pt_ref[0, 1, _slab(j), :]
            xr = lre_ref[_slab(i), :] + qr * cr - qi * ci
            xi = lim_ref[_slab(i), :] + qr * ci + qi * cr
            lre_ref[_slab(i), :] = xr
            lim_ref[_slab(i), :] = xi
            prev = _slab(jnp.maximum(i - 1, 0))
            sp_re = jnp.where(i == 0, sp0_re, sre_ref[prev, :])
            sp_im = jnp.where(i == 0, sp0_im, sim_ref[prev, :])
            return acc[0] + sp_re * xr + sp_im * xi, acc[1] + sp_re * xi - sp_im * xr

        acc_re, acc_im = lax.fori_loop(0, seg, pass2, (zero, zero), unroll=2)
        gare_ref[...] += jnp.sum(acc_re, axis=0, keepdims=True)
        gaim_ref[...] += jnp.sum(acc_im, axis=0, keepdims=True)
        lre = lre_ref[...]
        lim = lim_ref[...]
        dup_ref[...] = dyv * d_ref[...] + _dot(lre, bre_ref[0], NT) + _dot(lim, bim_ref[0], NT)
        _from_segments(dup_ref, duo_ref, seg)
        du_ref[...] = duo_ref[...].astype(BF16)
        gbre_ref[0] += _dot(u, lre, TN)
        gbim_ref[0] += _dot(u, lim, TN)

    rt = lambda t: nt - 1 - t
    col = pl.BlockSpec((tb, S5_COLS), lambda j, t: (rt(t), j))
    st = pl.BlockSpec((tb, S5_LANES), lambda j, t: (rt(t), j))
    prev = pl.BlockSpec((SUBLANES, S5_LANES), lambda j, t: (jnp.maximum(rt(t) * tb8 - 1, 0), j))
    bmat = pl.BlockSpec((1, S5_COLS, S5_LANES), lambda j, t: (j, 0, 0))
    cmat = bmat
    return pl.pallas_call(
        body, name="s5_scan_bwd", grid=(nb, nt),
        in_specs=[col, col, st, st, prev, prev, bmat, bmat, cmat, cmat,
                  pl.BlockSpec((1, S5_COLS), lambda j, t: (0, j)),
                  pl.BlockSpec((1, S5_TABS, SUBLANES, S5_LANES), lambda j, t: (j, 0, 0, 0)),
                  pl.BlockSpec((1, 2, tb, S5_LANES), lambda j, t: (j, 0, 0, 0)),
                  pl.BlockSpec(memory_space=pl.ANY)],
        out_specs=[col, pl.BlockSpec((1, S5_COLS), lambda j, t: (0, j)), cmat, cmat, bmat, bmat,
                   pl.BlockSpec((1, S5_LANES), lambda j, t: (0, j)), pl.BlockSpec((1, S5_LANES), lambda j, t: (0, j))],
        input_output_aliases={13: 0},
        out_shape=[jax.ShapeDtypeStruct((L, 2 * DS), BF16), jax.ShapeDtypeStruct((1, DS), F32),
                   jax.ShapeDtypeStruct((nb, S5_COLS, S5_LANES), F32), jax.ShapeDtypeStruct((nb, S5_COLS, S5_LANES), F32),
                   jax.ShapeDtypeStruct((nb, S5_COLS, S5_LANES), F32), jax.ShapeDtypeStruct((nb, S5_COLS, S5_LANES), F32),
                   jax.ShapeDtypeStruct((1, nb * S5_LANES), F32), jax.ShapeDtypeStruct((1, nb * S5_LANES), F32)],
        scratch_shapes=[pltpu.VMEM((tb, S5_LANES), F32), pltpu.VMEM((tb, S5_LANES), F32)]
        + [pltpu.VMEM((tb, S5_COLS), F32)] * 4 + [pltpu.VMEM((2, SUBLANES, S5_LANES), F32)],
        compiler_params=pltpu.CompilerParams(dimension_semantics=("parallel", "arbitrary")),
    )(dy, proj_main, s_re, s_im, s_re, s_im, bbd_re, bbd_im, cbd_re, cbd_im, dvec, tab, ptab, d_s5)


def _s5_post_fwd(y_pre, proj_main, glu_w, glu_b, DS):
    L = y_pre.shape[0]
    tr = _blk(L, 256, SUBLANES)

    def body(y_ref, z_ref, w_ref, b_ref, o_ref, t_ref):
        y1 = _gelu(y_ref[...])
        t = _dot(y1, w_ref[...]) + b_ref[...]
        t_ref[...] = t
        z = z_ref[...]
        o_ref[...] = (y1 * _sigmoid(t) * (z * _sigmoid(z))).astype(BF16)

    row = pl.BlockSpec((tr, DS), lambda i: (i, 0))
    return pl.pallas_call(
        body, name="s5_post_fwd", grid=(L // tr,),
        in_specs=[row, pl.BlockSpec((tr, DS), lambda i: (i, 1)), pl.BlockSpec((DS, DS), lambda i: (0, 0)),
                  pl.BlockSpec((1, DS), lambda i: (0, 0))],
        out_specs=[row, row],
        out_shape=[jax.ShapeDtypeStruct((L, 2 * DS), BF16), jax.ShapeDtypeStruct((L, DS), F32)],
        compiler_params=pltpu.CompilerParams(dimension_semantics=("parallel",)),
    )(y_pre, proj_main, glu_w, glu_b)


def _s5_post_bwd(d_ycat, y_pre, proj_main, t_pre, glu_w, DS):
    L = y_pre.shape[0]
    tr = _blk(L, 256, SUBLANES)

    def body(dy_ref, y_ref, z_ref, t_ref, w_ref, dyp_ref, dz_ref, dt_ref, y1_ref, gb_ref):
        i = pl.program_id(0)

        @pl.when(i == 0)
        def _():
            gb_ref[...] = jnp.zeros_like(gb_ref)

        dy = dy_ref[...]
        yp = y_ref[...]
        z = z_ref[...]
        y1 = _gelu(yp)
        sg = _sigmoid(t_ref[...])
        sz = _sigmoid(z)
        c = y1 * sg
        d_c = dy * (z * sz)
        dz_ref[...] = (dy * c * (sz * (1.0 + z * (1.0 - sz)))).astype(BF16)
        d_t = d_c * y1 * sg * (1.0 - sg)
        gb_ref[...] += jnp.sum(d_t, axis=0, keepdims=True)
        dt_ref[...] = d_t.astype(BF16)
        y1_ref[...] = y1.astype(BF16)
        d_y1 = d_c * sg + _dot(d_t, w_ref[...], NT)
        dyp_ref[...] = d_y1 * _gelu_grad(yp)

    row = pl.BlockSpec((tr, DS), lambda i: (i, 0))
    return pl.pallas_call(
        body, name="s5_post_bwd", grid=(L // tr,),
        in_specs=[row, row, pl.BlockSpec((tr, DS), lambda i: (i, 1)), row, pl.BlockSpec((DS, DS), lambda i: (0, 0))],
        out_specs=[row, pl.BlockSpec((tr, DS), lambda i: (i, 1)), row, row, pl.BlockSpec((1, DS), lambda i: (0, 0))],
        out_shape=[jax.ShapeDtypeStruct((L, DS), F32), jax.ShapeDtypeStruct((L, 2 * DS), BF16),
                   jax.ShapeDtypeStruct((L, DS), BF16), jax.ShapeDtypeStruct((L, DS), BF16),
                   jax.ShapeDtypeStruct((1, DS), F32)],
        compiler_params=pltpu.CompilerParams(dimension_semantics=("arbitrary",)),
    )(d_ycat, y_pre, proj_main, t_pre, glu_w)


def _gla_gates(glow, gu_ref, gb_ref):
    a = _dot(glow, gu_ref[...]) + gb_ref[...]
    lg = (jnp.minimum(a, 0.0) - jnp.log(1.0 + jnp.exp(-jnp.abs(a)))) * (1.0 / GLA_TAU)
    ri = lax.broadcasted_iota(jnp.int32, (GLA_CHUNK, GLA_CHUNK), 0)
    ci = lax.broadcasted_iota(jnp.int32, (GLA_CHUNK, GLA_CHUNK), 1)
    b = _dot_hi((ri >= ci).astype(F32), lg)
    b_last = jnp.sum(lg, axis=0, keepdims=True)
    return a, b, b_last, ri >= ci


def _gla_specs(DS, DK, DV, c, cmap):
    return [
        pl.BlockSpec((c, DK), lambda n: (cmap(n), 2 * DS // DK)),
        pl.BlockSpec((c, DK), lambda n: (cmap(n), 2 * DS // DK + 1)),
        pl.BlockSpec((c, DV), lambda n: (cmap(n), (2 * DS + 2 * DK) // DV)),
        pl.BlockSpec((c, DV), lambda n: (cmap(n), (2 * DS + 2 * DK) // DV + 1)),
    ]


def _gla_fwd(proj_main, proj_low, gate_up_pad, gate_bias, norm_w, ycat, DS, DK, DV):
    L = proj_main.shape[0]
    nc = L // GLA_CHUNK
    cps = math.gcd(GLA_STEP_CHUNKS, nc)
    nh = DK // GLA_HK
    scale = GLA_HK ** -0.5

    def body(q_ref, k_ref, v_ref, z_ref, gl_ref, gu_ref, gb_ref, nw_ref, _yc_ref, y_ref, sp_ref, st_ref):
        n = pl.program_id(0)

        @pl.when(n == 0)
        def _():
            st_ref[...] = jnp.zeros_like(st_ref)

        pairs = [(sc, h) for sc in range(cps) for h in range(nh)]
        rows = lambda sc: slice(sc * GLA_CHUNK, (sc + 1) * GLA_CHUNK)
        kcol = lambda h: slice(h * GLA_HK, (h + 1) * GLA_HK)
        vcol = lambda h: slice(h * GLA_HV, (h + 1) * GLA_HV)
        gates = [_gla_gates(gl_ref[rows(sc), :], gu_ref, gb_ref) for sc in range(cps)]
        qe, dec, o_in, kv = {}, {}, {}, {}
        for sc, h in pairs:
            _, b, b_last, mask = gates[sc]
            bh, bl = b[:, kcol(h)], b_last[:, kcol(h)]
            qe[sc, h] = (q_ref[rows(sc), kcol(h)] * scale) * jnp.exp(bh)
            kh = k_ref[rows(sc), kcol(h)]
            vh = v_ref[rows(sc), vcol(h)]
            attn = jnp.where(mask, _dot(qe[sc, h], kh * jnp.exp(-bh), NT), 0.0)
            o_in[sc, h] = _dot(attn, vh)
            kv[sc, h] = _dot(vh, kh * jnp.exp(bl - bh), TN)
            dec[sc, h] = jnp.exp(bl)
        for sc, h in pairs:
            st = st_ref[h]
            sp_ref[sc, h] = st
            o = o_in[sc, h] + _dot(qe[sc, h], st, NT)
            st_ref[h] = dec[sc, h] * st + kv[sc, h]
            r = lax.rsqrt(jnp.mean(o * o, axis=-1, keepdims=True) + EPS)
            z = z_ref[rows(sc), vcol(h)]
            y_ref[rows(sc), vcol(h)] = (o * r * nw_ref[...] * (z * _sigmoid(z))).astype(BF16)

    c = cps * GLA_CHUNK
    return pl.pallas_call(
        body, name="gla_fwd", grid=(nc // cps,),
        in_specs=_gla_specs(DS, DK, DV, c, lambda n: n) + [
            pl.BlockSpec((c, LANES), lambda n: (n, 0)),
            pl.BlockSpec((LANES, DK), lambda n: (0, 0)),
            pl.BlockSpec((1, DK), lambda n: (0, 0)),
            pl.BlockSpec((1, GLA_HV), lambda n: (0, 0)),
            pl.BlockSpec(memory_space=pl.ANY),
        ],
        out_specs=[pl.BlockSpec((c, DV), lambda n: (n, DS // DV)),
                   pl.BlockSpec((cps, nh, GLA_HV, GLA_HK), lambda n: (n, 0, 0, 0))],
        input_output_aliases={8: 0},
        out_shape=[jax.ShapeDtypeStruct(ycat.shape, BF16), jax.ShapeDtypeStruct((nc, nh, GLA_HV, GLA_HK), F32)],
        scratch_shapes=[pltpu.VMEM((nh, GLA_HV, GLA_HK), F32)],
        compiler_params=pltpu.CompilerParams(dimension_semantics=("arbitrary",)),
    )(proj_main, proj_main, proj_main, proj_main, proj_low, gate_up_pad, gate_bias, norm_w, ycat)


def _gla_bwd(d_ycat, proj_main, proj_low, s_prev, gate_up_pad, gate_bias, norm_w, DS, DK, DV):
    L = proj_main.shape[0]
    nc = L // GLA_CHUNK
    cps = math.gcd(GLA_STEP_CHUNKS, nc)
    nh = DK // GLA_HK
    scale = GLA_HK ** -0.5

    def body(dy_ref, q_ref, k_ref, v_ref, z_ref, gl_ref, sp_ref, gu_ref, gb_ref, nw_ref,
             dg_ref, da_ref, gnw_ref, ggb_ref, dst_ref):
        n = pl.program_id(0)

        @pl.when(n == 0)
        def _():
            dst_ref[...] = jnp.zeros_like(dst_ref)
            gnw_ref[...] = jnp.zeros_like(gnw_ref)
            ggb_ref[...] = jnp.zeros_like(ggb_ref)

        last_row = lax.broadcasted_iota(jnp.int32, (GLA_CHUNK, GLA_HK), 0) == GLA_CHUNK - 1
        ri = lax.broadcasted_iota(jnp.int32, (GLA_CHUNK, GLA_CHUNK), 0)
        ci = lax.broadcasted_iota(jnp.int32, (GLA_CHUNK, GLA_CHUNK), 1)
        upper = (ci >= ri).astype(F32)
        nw = nw_ref[...]
        for sc in reversed(range(cps)):
            rs = slice(sc * GLA_CHUNK, (sc + 1) * GLA_CHUNK)
            a, b, b_last, mask = _gla_gates(gl_ref[rs, :], gu_ref, gb_ref)
            for h in range(nh):
                ks = slice(h * GLA_HK, (h + 1) * GLA_HK)
                vs = slice(h * GLA_HV, (h + 1) * GLA_HV)
                bh, bl = b[:, ks], b_last[:, ks]
                e = jnp.exp(bh)
                einv = jnp.exp(-bh)
                etail = jnp.exp(bl - bh)
                dec = jnp.exp(bl)
                qe = (q_ref[rs, ks] * scale) * e
                kh = k_ref[rs, ks]
                ke = kh * einv
                ktail = kh * etail
                vh = v_ref[rs, vs]
                st = sp_ref[sc, h]
                dst = dst_ref[h]
                attn = jnp.where(mask, _dot(qe, ke, NT), 0.0)
                o = _dot(attn, vh) + _dot(qe, st, NT)
                r = lax.rsqrt(jnp.mean(o * o, axis=-1, keepdims=True) + EPS)
                nrm = o * r
                z = z_ref[rs, vs]
                sz = _sigmoid(z)
                dy = dy_ref[rs, vs]
                dg_ref[rs, 2 * DK + DV + h * GLA_HV:2 * DK + DV + (h + 1) * GLA_HV] = (
                    dy * nrm * nw * (sz * (1.0 + z * (1.0 - sz)))).astype(BF16)
                d_on = dy * (z * sz)
                gnw_ref[...] += jnp.sum(d_on * nrm, axis=0, keepdims=True)
                d_n = d_on * nw
                d_o = r * (d_n - nrm * jnp.mean(d_n * nrm, axis=-1, keepdims=True))
                d_attn = jnp.where(mask, _dot(d_o, vh, NT), 0.0)
                dg_ref[rs, 2 * DK + h * GLA_HV:2 * DK + (h + 1) * GLA_HV] = (
                    _dot(attn, d_o, TN) + _dot(ktail, dst, NT)).astype(BF16)
                d_qe = _dot(d_attn, ke) + _dot(d_o, st)
                d_ke = _dot(d_attn, qe, TN)
                d_kt = _dot(vh, dst)
                d_dec = jnp.sum(dst * st, axis=0, keepdims=True)
                dst_ref[h] = dec * dst + _dot(d_o, qe, TN)
                dg_ref[rs, ks] = (d_qe * scale * e).astype(BF16)
                dg_ref[rs, DK + h * GLA_HK:DK + (h + 1) * GLA_HK] = (d_ke * einv + d_kt * etail).astype(BF16)
                d_bl = jnp.sum(d_kt * ktail, axis=0, keepdims=True) + d_dec * dec
                d_b = d_qe * qe - d_ke * ke - d_kt * ktail + jnp.where(last_row, d_bl, 0.0)
                d_lg = _dot_hi(upper, d_b)
                d_a = d_lg * (1.0 / GLA_TAU) * _sigmoid(-a[:, ks])
                ggb_ref[:, ks] += jnp.sum(d_a, axis=0, keepdims=True)
                da_ref[rs, ks] = d_a.astype(BF16)

    c = cps * GLA_CHUNK
    ns = nc // cps
    rn = lambda n: ns - 1 - n
    return pl.pallas_call(
        body, name="gla_bwd", grid=(ns,),
        in_specs=[pl.BlockSpec((c, DV), lambda n: (rn(n), DS // DV))] + _gla_specs(DS, DK, DV, c, rn) + [
            pl.BlockSpec((c, LANES), lambda n: (rn(n), 0)),
            pl.BlockSpec((cps, nh, GLA_HV, GLA_HK), lambda n: (rn(n), 0, 0, 0)),
            pl.BlockSpec((LANES, DK), lambda n: (0, 0)),
            pl.BlockSpec((1, DK), lambda n: (0, 0)),
            pl.BlockSpec((1, GLA_HV), lambda n: (0, 0)),
        ],
        out_specs=[pl.BlockSpec((c, 2 * DK + 2 * DV), lambda n: (rn(n), 0)),
                   pl.BlockSpec((c, DK), lambda n: (rn(n), 0)),
                   pl.BlockSpec((1, GLA_HV), lambda n: (0, 0)), pl.BlockSpec((1, DK), lambda n: (0, 0))],
        out_shape=[jax.ShapeDtypeStruct((L, 2 * DK + 2 * DV), BF16),
                   jax.ShapeDtypeStruct((L, DK), BF16),
                   jax.ShapeDtypeStruct((1, GLA_HV), F32), jax.ShapeDtypeStruct((1, DK), F32)],
        scratch_shapes=[pltpu.VMEM((nh, GLA_HV, GLA_HK), F32)],
        compiler_params=pltpu.CompilerParams(dimension_semantics=("arbitrary",)),
    )(d_ycat, proj_main, proj_main, proj_main, proj_main, proj_low, s_prev, gate_up_pad, gate_bias, norm_w)


def _adamw_math(w, g, m, v):
    c1 = 1.0 - ADAM_B1 ** ADAM_STEP
    c2 = 1.0 - ADAM_B2 ** ADAM_STEP
    m_ = ADAM_B1 * m + (1.0 - ADAM_B1) * g
    v_ = ADAM_B2 * v + (1.0 - ADAM_B2) * (g * g)
    return -ADAM_LR * ((m_ / c1) / (jnp.sqrt(v_ / c2) + ADAM_EPS) + ADAM_WD * w), m_, v_


def _adamw_small(g_row, g_a, g_bc, ws, ms, vs):
    n = len(ws)
    nvec = n - 6

    def body(*refs):
        grow_ref, ga_ref, gbc_ref = refs[:3]
        w_refs, m_refs, v_refs = refs[3:3 + n], refs[3 + n:3 + 2 * n], refs[3 + 2 * n:3 + 3 * n]
        outs = refs[3 + 3 * n:]
        off = 0
        for i in range(n):
            if i < nvec:
                width = ws[i].shape[1]
                g = grow_ref[:, off:off + width]
                off += width
            elif i < nvec + 2:
                g = ga_ref[i - nvec]
            else:
                g = gbc_ref[i - nvec - 2]
            d, m_, v_ = _adamw_math(w_refs[i][...], g, m_refs[i][...], v_refs[i][...])
            outs[i][...] = g
            outs[n + i][...] = d
            outs[2 * n + i][...] = m_
            outs[3 * n + i][...] = v_

    vm = pl.BlockSpec(memory_space=pltpu.VMEM)
    outs = pl.pallas_call(
        body, name="adamw_small",
        in_specs=[vm] * (3 + 3 * n), out_specs=[vm] * (4 * n),
        out_shape=[jax.ShapeDtypeStruct(w.shape, F32) for w in ws] * 4,
    )(g_row, g_a, g_bc, *ws, *ms, *vs)
    return [outs[k * n:(k + 1) * n] for k in range(4)]


def _my_pos():
    return lax.axis_index("x"), lax.axis_index("y"), lax.axis_index("c")


def _split_start(name, srcs, lands_sd, make_copies, ncopies, after):
    n, m = len(srcs), len(lands_sd)

    def body(*refs):
        send_sems, recv_sems = refs[n + m + len(after)], refs[n + m + len(after) + 1]
        for cp in make_copies(refs[:n], refs[n:n + m], send_sems, recv_sems):
            cp.start()
        refs[-1][...] = jnp.zeros_like(refs[-1])

    hbm = pl.BlockSpec(memory_space=pltpu.HBM)
    sem = pl.BlockSpec(memory_space=pltpu.SEMAPHORE)
    outs = pl.pallas_call(
        body, name=name,
        in_specs=[hbm] * (n + m) + [pl.BlockSpec(memory_space=pl.ANY)] * len(after),
        out_specs=[sem, sem] + [hbm] * (n + m) + [pl.BlockSpec(memory_space=pltpu.VMEM)],
        out_shape=[pltpu.SemaphoreType.DMA((ncopies,)), pltpu.SemaphoreType.DMA((ncopies,))]
        + [pltpu.HBM(s.shape, s.dtype) for s in srcs] + [pltpu.HBM(s.shape, s.dtype) for s in lands_sd]
        + [jax.ShapeDtypeStruct((SUBLANES, LANES), F32)],
        input_output_aliases={i: 2 + i for i in range(n + m)},
        compiler_params=pltpu.CompilerParams(has_side_effects=pltpu.SideEffectType.DATAFLOW_SIDE_EFFECTING),
    )(*[pltpu.with_memory_space_constraint(s, pltpu.HBM) for s in srcs],
      *[pltpu.with_memory_space_constraint(lax.empty(s.shape, s.dtype), pltpu.HBM) for s in lands_sd], *after)
    return outs[0], outs[1], outs[2:2 + n], outs[2 + n:2 + n + m], outs[-1]


def _split_wait(name, send_sems, recv_sems, srcs, lands, make_copies, after):
    n, m = len(srcs), len(lands)

    def body(*refs):
        for cp in make_copies(refs[:n], refs[n:n + m], refs[n + m], refs[n + m + 1]):
            cp.wait_send()
            cp.wait_recv()

    hbm = pl.BlockSpec(memory_space=pltpu.HBM)
    sem = pl.BlockSpec(memory_space=pltpu.SEMAPHORE)
    outs = pl.pallas_call(
        body, name=name,
        in_specs=[hbm] * (n + m) + [sem, sem] + [pl.BlockSpec(memory_space=pl.ANY)] * len(after),
        out_specs=[hbm] * (n + m),
        out_shape=[pltpu.HBM(s.shape, s.dtype) for s in srcs] + [pltpu.HBM(p.shape, p.dtype) for p in lands],
        input_output_aliases={i: i for i in range(n + m)},
        compiler_params=pltpu.CompilerParams(has_side_effects=pltpu.SideEffectType.DATAFLOW_SIDE_EFFECTING),
    )(*srcs, *lands, send_sems, recv_sems, *after)
    return outs[:n], outs[n:]


def _late_gather_copies(srcs, lands, send_sems, recv_sems):
    x, y, c = _my_pos()
    me = 2 * x + y
    copies = []
    for d in (1, 2, 3):
        to = (x ^ (d >> 1), y ^ (d & 1), c)
        for a in range(len(srcs)):
            hrows = srcs[a].shape[0] // 2
            rows = pl.ds(c * hrows, hrows)
            copies.append(pltpu.make_async_remote_copy(
                src_ref=srcs[a].at[rows, :], dst_ref=lands[a].at[me, rows, :], send_sem=send_sems.at[3 * a + d - 1],
                recv_sem=recv_sems.at[3 * a + d - 1], device_id=to, device_id_type=MESH))
    return copies


def _late_gather_start(shards, after, name):
    n = len(shards)

    def body(*refs):
        srcs, lands = refs[:n], refs[n:2 * n]
        send_sems, recv_sems = refs[2 * n + 1], refs[2 * n + 2]
        token = refs[-1]
        for cp in _late_gather_copies(srcs, lands, send_sems, recv_sems):
            cp.start()
        token[...] = jnp.zeros_like(token)

    hbm = pl.BlockSpec(memory_space=pltpu.HBM)
    sem = pl.BlockSpec(memory_space=pltpu.SEMAPHORE)
    outs = pl.pallas_call(
        body, name=name,
        in_specs=[hbm] * (2 * n) + [pl.BlockSpec(memory_space=pl.ANY)],
        out_specs=[sem, sem] + [hbm] * (2 * n) + [pl.BlockSpec(memory_space=pltpu.VMEM)],
        out_shape=[pltpu.SemaphoreType.DMA((3 * n,)), pltpu.SemaphoreType.DMA((3 * n,))]
        + [pltpu.HBM(s.shape, s.dtype) for s in shards]
        + [pltpu.HBM((4,) + s.shape, s.dtype) for s in shards]
        + [jax.ShapeDtypeStruct((SUBLANES, LANES), F32)],
        input_output_aliases={i: 2 + i for i in range(2 * n)},
        compiler_params=pltpu.CompilerParams(has_side_effects=pltpu.SideEffectType.DATAFLOW_SIDE_EFFECTING),
    )(*[pltpu.with_memory_space_constraint(s, pltpu.HBM) for s in shards],
      *[pltpu.with_memory_space_constraint(lax.empty((4,) + s.shape, s.dtype), pltpu.HBM) for s in shards], after)
    return outs[0], outs[1], outs[2:2 + n], outs[2 + n:2 + 2 * n], outs[-1]


def _late_gather_wait(send_sems, recv_sems, shards, lands, after, name):
    n = len(shards)

    def body(*refs):
        src_refs, land_refs = refs[:n], refs[n:2 * n]
        ssem, rsem = refs[2 * n], refs[2 * n + 1]
        for cp in _late_gather_copies(src_refs, land_refs, ssem, rsem):
            cp.wait_send()
            cp.wait_recv()

    hbm = pl.BlockSpec(memory_space=pltpu.HBM)
    sem = pl.BlockSpec(memory_space=pltpu.SEMAPHORE)
    outs = pl.pallas_call(
        body, name=name,
        in_specs=[hbm] * (2 * n) + [sem, sem] + [pl.BlockSpec(memory_space=pl.ANY)] * len(after),
        out_specs=[hbm] * (2 * n),
        out_shape=[pltpu.HBM(s.shape, s.dtype) for s in shards] + [pltpu.HBM(p.shape, p.dtype) for p in lands],
        input_output_aliases={i: i for i in range(2 * n)},
        compiler_params=pltpu.CompilerParams(has_side_effects=pltpu.SideEffectType.DATAFLOW_SIDE_EFFECTING),
    )(*shards, *lands, send_sems, recv_sems, *after)
    return outs[n:]


def _late_gather_pair(lands, name):
    n = len(lands)

    def body(*refs):
        outs = refs[n:2 * n]
        send_sems, recv_sems = refs[2 * n:]
        x, y, c = _my_pos()

        def copy(a, d, half):
            chip = 2 * (x ^ (d >> 1)) + (y ^ (d & 1))
            hrows = lands[a].shape[1] // 2
            sl = outs[a].at[chip, pl.ds(half * hrows, hrows), :]
            return pltpu.make_async_remote_copy(src_ref=sl, dst_ref=sl, send_sem=send_sems.at[3 * a + d - 1],
                                                recv_sem=recv_sems.at[3 * a + d - 1], device_id=(x, y, 1 - c),
                                                device_id_type=MESH)

        pairs = [(a, d) for d in (1, 2, 3) for a in range(n)]
        for a, d in pairs:
            copy(a, d, c).start()
        for a, d in pairs:
            copy(a, d, c).wait_send()
            copy(a, d, 1 - c).wait_recv()

    hbm = pl.BlockSpec(memory_space=pltpu.HBM)
    return pl.pallas_call(
        body, name=name, in_specs=[hbm] * n, out_specs=[hbm] * n,
        out_shape=[jax.ShapeDtypeStruct(p.shape, p.dtype) for p in lands],
        input_output_aliases={i: i for i in range(n)},
        scratch_shapes=[pltpu.SemaphoreType.DMA((3 * n,)), pltpu.SemaphoreType.DMA((3 * n,))],
    )(*lands)


def _pair_exchange(gs):
    n = len(gs)

    def body(*refs):
        ins, outs = refs[:n], refs[n:2 * n]
        send_sems, recv_sems = refs[2 * n:]
        x, y, c = _my_pos()
        sent = []
        for a in range(n):
            hrows = gs[a].shape[1] // 2
            cp = pltpu.make_async_remote_copy(
                src_ref=ins[a].at[:, pl.ds((1 - c) * hrows, hrows), :], dst_ref=outs[a], send_sem=send_sems.at[a],
                recv_sem=recv_sems.at[a], device_id=(x, y, 1 - c), device_id_type=MESH)
            cp.start()
            sent.append(cp)
        for cp in sent:
            cp.wait()

    hbm = pl.BlockSpec(memory_space=pltpu.HBM)
    return pl.pallas_call(
        body, name="grad_pair_exchange", in_specs=[hbm] * n, out_specs=[hbm] * n,
        out_shape=[jax.ShapeDtypeStruct((g.shape[0], g.shape[1] // 2, g.shape[2]), g.dtype) for g in gs],
        scratch_shapes=[pltpu.SemaphoreType.DMA((n,)), pltpu.SemaphoreType.DMA((n,))],
    )(*gs)


def _pair_add(g, got, c_arr, name):
    nk, rows2, cols = g.shape
    hrows = rows2 // 2
    tr = _blk(hrows, 256, 2 * SUBLANES)
    nb = hrows // tr

    def body(c_ref, a_ref, b_ref, o_ref):
        o_ref[...] = (a_ref[...].astype(F32) + b_ref[...].astype(F32)).astype(o_ref.dtype)

    return pl.pallas_call(
        body, name=name,
        grid_spec=pltpu.PrefetchScalarGridSpec(
            num_scalar_prefetch=1, grid=(nk, nb),
            in_specs=[pl.BlockSpec((1, tr, cols), lambda k, i, c_ref: (k, c_ref[0] * nb + i, 0)),
                      pl.BlockSpec((1, tr, cols), lambda k, i, c_ref: (k, i, 0))],
            out_specs=pl.BlockSpec((1, tr, cols), lambda k, i, c_ref: (k, i, 0))),
        out_shape=jax.ShapeDtypeStruct((nk, hrows, cols), g.dtype),
        compiler_params=pltpu.CompilerParams(dimension_semantics=("parallel", "parallel")),
    )(c_arr, g, got)


def _chip_scatter_copies(srcs, lands, send_sems, recv_sems):
    x, y, c = _my_pos()
    copies = []
    for d in (1, 2, 3):
        tx, ty = x ^ (d >> 1), y ^ (d & 1)
        for a in range(len(srcs)):
            copies.append(pltpu.make_async_remote_copy(
                src_ref=srcs[a].at[2 * tx + ty], dst_ref=lands[a].at[d - 1], send_sem=send_sems.at[3 * a + d - 1],
                recv_sem=recv_sems.at[3 * a + d - 1], device_id=(tx, ty, c), device_id_type=MESH))
    return copies


def _chip_scatter_start(pss):
    n = len(pss)

    def body(*refs):
        srcs, lands = refs[:n], refs[n:2 * n]
        send_sems, recv_sems = refs[2 * n], refs[2 * n + 1]
        token = refs[-1]
        for cp in _chip_scatter_copies(srcs, lands, send_sems, recv_sems):
            cp.start()
        token[...] = jnp.zeros_like(token)

    hbm = pl.BlockSpec(memory_space=pltpu.HBM)
    sem = pl.BlockSpec(memory_space=pltpu.SEMAPHORE)
    land_shapes = [(3,) + p.shape[1:] for p in pss]
    outs = pl.pallas_call(
        body, name="grad_chip_scatter_start",
        in_specs=[hbm] * (2 * n),
        out_specs=[sem, sem] + [hbm] * (2 * n) + [pl.BlockSpec(memory_space=pltpu.VMEM)],
        out_shape=[pltpu.SemaphoreType.DMA((3 * n,)), pltpu.SemaphoreType.DMA((3 * n,))]
        + [pltpu.HBM(p.shape, p.dtype) for p in pss]
        + [pltpu.HBM(s, p.dtype) for s, p in zip(land_shapes, pss)]
        + [jax.ShapeDtypeStruct((SUBLANES, LANES), F32)],
        input_output_aliases={i: 2 + i for i in range(2 * n)},
        compiler_params=pltpu.CompilerParams(has_side_effects=pltpu.SideEffectType.DATAFLOW_SIDE_EFFECTING),
    )(*[pltpu.with_memory_space_constraint(p, pltpu.HBM) for p in pss],
      *[pltpu.with_memory_space_constraint(lax.empty(s, p.dtype), pltpu.HBM) for s, p in zip(land_shapes, pss)])
    return outs[0], outs[1], outs[2:2 + n], outs[2 + n:2 + 2 * n], outs[-1]


def _chip_scatter_wait(send_sems, recv_sems, srcs, lands, after):
    n = len(srcs)

    def body(*refs):
        src_refs, land_refs = refs[:n], refs[n:2 * n]
        ssem, rsem = refs[2 * n], refs[2 * n + 1]
        for cp in _chip_scatter_copies(src_refs, land_refs, ssem, rsem):
            cp.wait_send()
            cp.wait_recv()

    hbm = pl.BlockSpec(memory_space=pltpu.HBM)
    sem = pl.BlockSpec(memory_space=pltpu.SEMAPHORE)
    outs = pl.pallas_call(
        body, name="grad_chip_scatter_wait",
        in_specs=[hbm] * (2 * n) + [sem, sem, pl.BlockSpec(memory_space=pl.ANY)],
        out_specs=[hbm] * (2 * n),
        out_shape=[pltpu.HBM(p.shape, p.dtype) for p in srcs] + [pltpu.HBM(p.shape, p.dtype) for p in lands],
        input_output_aliases={i: i for i in range(2 * n)},
        compiler_params=pltpu.CompilerParams(has_side_effects=pltpu.SideEffectType.DATAFLOW_SIDE_EFFECTING),
    )(*srcs, *lands, send_sems, recv_sems, after)
    return outs[:n], outs[n:]


def _chip_sum(ps, got, me_arr, name):
    _, hrows, cols = ps.shape
    tr = _blk(hrows, 256, 2 * SUBLANES)

    def body(me_ref, p_ref, g_ref, o_ref):
        acc = p_ref[0].astype(F32)
        for s in range(3):
            acc = acc + g_ref[s].astype(F32)
        o_ref[...] = acc

    return pl.pallas_call(
        body, name=name,
        grid_spec=pltpu.PrefetchScalarGridSpec(
            num_scalar_prefetch=1, grid=(hrows // tr,),
            in_specs=[pl.BlockSpec((1, tr, cols), lambda i, me_ref: (me_ref[0], i, 0)),
                      pl.BlockSpec((3, tr, cols), lambda i, me_ref: (0, i, 0))],
            out_specs=pl.BlockSpec((tr, cols), lambda i, me_ref: (i, 0))),
        out_shape=jax.ShapeDtypeStruct((hrows, cols), F32),
        compiler_params=pltpu.CompilerParams(dimension_semantics=("parallel",)),
    )(me_arr, ps, got)


def _pair_swap(halves):
    n = len(halves)

    def body(*refs):
        ins, outs = refs[:n], refs[n:2 * n]
        send_sems, recv_sems = refs[2 * n:]
        x, y, c = _my_pos()
        sent = []
        for a in range(n):
            cp = pltpu.make_async_remote_copy(src_ref=ins[a], dst_ref=outs[a], send_sem=send_sems.at[a], recv_sem=recv_sems.at[a],
                                              device_id=(x, y, 1 - c), device_id_type=MESH)
            cp.start()
            sent.append(cp)
        for cp in sent:
            cp.wait()

    hbm = pl.BlockSpec(memory_space=pltpu.HBM)
    return pl.pallas_call(
        body, name="grad_pair_swap", in_specs=[hbm] * n, out_specs=[hbm] * n,
        out_shape=[jax.ShapeDtypeStruct(h.shape, h.dtype) for h in halves],
        scratch_shapes=[pltpu.SemaphoreType.DMA((n,)), pltpu.SemaphoreType.DMA((n,))],
    )(*halves)


def _adamw_sharded(w, g_own, g_other, m, v, c_arr, after, name):
    R, C = w.shape
    hrows = R // 2
    tr = _blk(hrows, 256, SUBLANES)
    nbh = hrows // tr

    def body(c_ref, w_ref, go_ref, gx_ref, m_ref, v_ref, _after_ref, g_ref, d_ref, nm_ref, nv_ref):
        mine = (pl.program_id(0) // nbh) == c_ref[0]
        g_ = jnp.where(mine, go_ref[...], gx_ref[...])
        g_ref[...] = g_
        d_ref[...], nm_ref[...], nv_ref[...] = _adamw_math(w_ref[...], g_, m_ref[...], v_ref[...])

    blk = pl.BlockSpec((tr, C), lambda i, c_ref: (i, 0))
    hblk = pl.BlockSpec((tr, C), lambda i, c_ref: (i % nbh, 0))
    sd = jax.ShapeDtypeStruct((R, C), F32)
    return pl.pallas_call(
        body, name=name,
        grid_spec=pltpu.PrefetchScalarGridSpec(
            num_scalar_prefetch=1, grid=(2 * nbh,),
            in_specs=[blk, hblk, hblk, blk, blk, pl.BlockSpec(memory_space=pl.ANY)], out_specs=[blk] * 4),
        out_shape=[sd] * 4,
        compiler_params=pltpu.CompilerParams(dimension_semantics=("parallel",)),
    )(c_arr, w, g_own, g_other, m, v, after)


def _ar_piece(ref, rows, p):
    start = p * rows
    if rows % SUBLANES == 0:
        start = pl.multiple_of(start, SUBLANES)
    return ref.at[..., pl.ds(start, rows), :]


def _ar_peer(d):
    x, y, c = _my_pos()
    return (x ^ (d >> 2), y ^ ((d >> 1) & 1), c ^ (d & 1))


def _ar_lin(p):
    return 4 * p[0] + 2 * p[1] + p[2]


def _ar_scatter_copies(rows):
    def make(srcs, lands, send_sems, recv_sems):
        n = len(srcs)
        copies = []
        for d in range(1, 8):
            to = _ar_peer(d)
            for a in range(n):
                copies.append(pltpu.make_async_remote_copy(
                    src_ref=_ar_piece(srcs[a], rows[a], _ar_lin(to)), dst_ref=lands[a].at[d],
                    send_sem=send_sems.at[(d - 1) * n + a], recv_sem=recv_sems.at[(d - 1) * n + a], device_id=to,
                    device_id_type=MESH))
        return copies
    return make


def _ar_gather_copies(rows):
    def make(srcs, lands, send_sems, recv_sems):
        n = len(srcs)
        me = _ar_lin(_my_pos())
        copies = []
        for d in range(1, 8):
            for a in range(n):
                copies.append(pltpu.make_async_remote_copy(
                    src_ref=srcs[a], dst_ref=_ar_piece(lands[a], rows[a], me),
                    send_sem=send_sems.at[(d - 1) * n + a], recv_sem=recv_sems.at[(d - 1) * n + a], device_id=_ar_peer(d),
                    device_id_type=MESH))
        return copies
    return make


def _ar_sum(srcs, lands, rows):
    n = len(srcs)

    def body(*refs):
        me = _ar_lin(_my_pos())
        for a in range(n):
            acc = _ar_piece(refs[a], rows[a], me)[...]
            for d in range(1, 8):
                acc = acc + refs[n + a][d]
            refs[2 * n + a][...] = acc

    vm = pl.BlockSpec(memory_space=pltpu.VMEM)
    return pl.pallas_call(
        body, name="allreduce_sum", in_specs=[vm] * (2 * n), out_specs=[vm] * n,
        out_shape=[jax.ShapeDtypeStruct(p.shape[1:], F32) for p in lands],
    )(*srcs, *lands)


def kernel(x, pre_norm_w, w_in, s5_A_re, s5_A_im, s5_B_re, s5_B_im, s5_C_re, s5_C_im, s5_D, s5_log_dt, s5_glu_w, s5_glu_b, gla_gate_up, gla_gate_bias, gla_norm_w, w_out, post_norm_w, loss_target, m_pre_norm_w, m_w_in, m_s5_A_re, m_s5_A_im, m_s5_B_re, m_s5_B_im, m_s5_C_re, m_s5_C_im, m_s5_D, m_s5_log_dt, m_s5_glu_w, m_s5_glu_b, m_gla_gate_up, m_gla_gate_bias, m_gla_norm_w, m_w_out, m_post_norm_w, v_pre_norm_w, v_w_in, v_s5_A_re, v_s5_A_im, v_s5_B_re, v_s5_B_im, v_s5_C_re, v_s5_C_im, v_s5_D, v_s5_log_dt, v_s5_glu_w, v_s5_glu_b, v_gla_gate_up, v_gla_gate_bias, v_gla_norm_w, v_w_out, v_post_norm_w):
    names = ["pre_norm_w", "w_in", "s5_A_re", "s5_A_im", "s5_B_re", "s5_B_im", "s5_C_re", "s5_C_im", "s5_D", "s5_log_dt",
             "s5_glu_w", "s5_glu_b", "gla_gate_up", "gla_gate_bias", "gla_norm_w", "w_out", "post_norm_w"]
    W = dict(zip(names, (pre_norm_w, w_in, s5_A_re, s5_A_im, s5_B_re, s5_B_im, s5_C_re, s5_C_im, s5_D, s5_log_dt,
                         s5_glu_w, s5_glu_b, gla_gate_up, gla_gate_bias, gla_norm_w, w_out, post_norm_w)))
    M = dict(zip(names, (m_pre_norm_w, m_w_in, m_s5_A_re, m_s5_A_im, m_s5_B_re, m_s5_B_im, m_s5_C_re, m_s5_C_im, m_s5_D,
                         m_s5_log_dt, m_s5_glu_w, m_s5_glu_b, m_gla_gate_up, m_gla_gate_bias, m_gla_norm_w, m_w_out,
                         m_post_norm_w)))
    V = dict(zip(names, (v_pre_norm_w, v_w_in, v_s5_A_re, v_s5_A_im, v_s5_B_re, v_s5_B_im, v_s5_C_re, v_s5_C_im, v_s5_D,
                         v_s5_log_dt, v_s5_glu_w, v_s5_glu_b, v_gla_gate_up, v_gla_gate_bias, v_gla_norm_w, v_w_out,
                         v_post_norm_w)))
    sharded = ("w_in", "s5_glu_w", "w_out", "gla_gate_up")

    xb = x[0]
    tgt = loss_target[0]
    L, D = xb.shape
    DS = D // 2
    G = DS // S5_GROUP
    P = S5_STATE
    NB = DS // S5_COLS
    DV = D - DS
    DK = DV // 2
    WM = 2 * DS + 2 * DK + 2 * DV
    nsh = w_in.shape[2]

    chip = 2 * lax.axis_index("x") + lax.axis_index("y")
    own = [jnp.transpose(lax.optimization_barrier(jnp.transpose(w_in[0]).astype(BF16))), s5_glu_w[0].astype(BF16),
           w_out[0].astype(BF16), gla_gate_up[0]]
    fill = lambda g, o: lax.dynamic_update_index_in_dim(g, o, chip, 0)
    win_ss, win_rs, win_src, win_lands, win_token = _late_gather_start(own[:1], pre_norm_w, "w_in_gather_start")
    h = _prenorm_fwd(xb, pre_norm_w, win_token)

    b_view = lambda t: jnp.transpose(t[0], (0, 2, 1)).reshape(G * S5_GROUP, P)
    b_back = lambda t: jnp.transpose(t.reshape(G, S5_GROUP, P), (0, 2, 1))[None]
    c_view = lambda t: t[0].reshape(G * S5_GROUP, P)
    c_back = lambda t: t.reshape(1, G, S5_GROUP, P)
    small = ["pre_norm_w", "post_norm_w", "s5_D", "s5_glu_b", "gla_gate_bias", "gla_norm_w", "s5_log_dt",
             "s5_A_re", "s5_A_im", "s5_B_re", "s5_B_im", "s5_C_re", "s5_C_im"]
    view = {n: (lambda t: t) for n in small[:7]}
    back = dict(view)
    view.update(s5_A_re=lambda t: t[0], s5_A_im=lambda t: t[0], s5_B_re=b_view, s5_B_im=b_view, s5_C_re=c_view, s5_C_im=c_view)
    back.update(s5_A_re=lambda t: t[None], s5_A_im=lambda t: t[None], s5_B_re=b_back, s5_B_im=b_back, s5_C_re=c_back,
                s5_C_im=c_back)
    Wv = {n: view[n](W[n]) for n in small}
    bbd_re, bbd_im, ct_re, ct_im, tab, ptab = _s5_prep_fwd(
        Wv["s5_A_re"], Wv["s5_A_im"], s5_log_dt, Wv["s5_B_re"], Wv["s5_B_im"], Wv["s5_C_re"], Wv["s5_C_im"],
        h, _blk(L, 512, SUBLANES) // SUBLANES)
    dvec = s5_D

    for d_ in (W, M, V):
        d_["w_in"], _ = lax.optimization_barrier((d_["w_in"], win_token))
    g_win = _late_gather_wait(win_ss, win_rs, win_src, win_lands,
                              [tab, W["w_in"][0], M["w_in"][0], V["w_in"][0]], "w_in_gather_wait")
    g_win = fill(_late_gather_pair(g_win, "w_in_gather_pair")[0], own[0])
    last = WM - 3 * nsh
    w_main = jnp.concatenate([g_win[0], g_win[1], g_win[2], g_win[3][:, :last]], axis=1)
    w_low = jnp.pad(g_win[3][:, last:], ((0, 0), (0, LANES - GLA_RANK)))
    late_ss, late_rs, late_src, late_lands, late_token = _late_gather_start(own[1:], g_win, "late_gather_start")
    proj_main, proj_low = _in_proj(h, w_main, w_low, late_token)
    y_pre, s_re, s_im = _s5_scan_fwd(proj_main, bbd_re, bbd_im, ct_re, ct_im, dvec, tab, ptab, DS)
    late = _late_gather_wait(late_ss, late_rs, late_src, late_lands, [y_pre], "late_gather_wait")
    late = _late_gather_pair(late, "late_gather_pair")
    g_glu, g_wout, g_gup = [fill(g, o) for g, o in zip(late, own[1:])]
    glu_w = g_glu.reshape(DS, DS)
    wout = g_wout.reshape(D, D)
    gup = jnp.moveaxis(g_gup, 0, 1).reshape(GLA_RANK, DK)
    gup_pad = jnp.pad(gup, ((0, LANES - GLA_RANK), (0, 0))).astype(BF16)
    ycat, t_pre = _s5_post_fwd(y_pre, proj_main, glu_w, s5_glu_b, DS)
    ycat, s_prev = _gla_fwd(proj_main, proj_low, gup_pad, gla_gate_bias, gla_norm_w, ycat, DS, DK, DV)
    mixed = _mm(ycat, wout, name="out_proj")
    loss11, d_mixed, dout, g_post_w = _post_fwd_bwd(mixed, xb, tgt, post_norm_w)

    d_ycat = _mm(d_mixed, wout, tb=True, name="out_proj_dx")
    g_wout_full = _mm(ycat, d_mixed, ta=True, out_dtype=BF16, name="out_proj_dw")
    d_ypre, d_s5, d_t, y1, g_glu_b = _s5_post_bwd(d_ycat, y_pre, proj_main, t_pre, glu_w, DS)
    g_glu_full = _mm(y1, d_t, ta=True, out_dtype=BF16, name="glu_dw")
    d_s5, g_D, gct_re, gct_im, gbbd_re, gbbd_im, gab_re, gab_im = _s5_scan_bwd(
        d_ypre, proj_main, s_re, s_im, bbd_re, bbd_im, ct_re, ct_im, dvec, tab, ptab, d_s5, DS)
    d_gla, d_a, g_norm_w, g_gate_bias = _gla_bwd(
        d_ycat, proj_main, proj_low, s_prev, gup_pad, gla_gate_bias, gla_norm_w, DS, DK, DV)
    d_low = _mm(d_a, gup_pad, tb=True, out_dtype=BF16, name="gate_dx")
    g_gup_pad = _mm(proj_low, d_a, ta=True, name="gate_dw")
    g_wmain, g_wlow = _in_proj_dw(h, d_s5, d_gla, d_low)

    g_win_sh = jnp.stack([g_wmain[:, :nsh], g_wmain[:, nsh:2 * nsh], g_wmain[:, 2 * nsh:3 * nsh],
                          jnp.concatenate([g_wmain[:, 3 * nsh:], g_wlow[:, :GLA_RANK]], axis=1)])
    gs = [g_win_sh,
          g_glu_full.reshape(4, DS // 4, DS),
          g_wout_full.reshape(4, D // 4, D),
          jnp.moveaxis(g_gup_pad[:GLA_RANK].reshape(GLA_RANK, 4, DK // 4), 1, 0)]
    c_arr = lax.axis_index("c").astype(jnp.int32).reshape(1)
    me_arr = chip.astype(jnp.int32).reshape(1)
    got = _pair_exchange(gs)
    pss = [_pair_add(g, r, c_arr, "grad_pair_add_" + n) for n, g, r in zip(sharded, gs, got)]
    send_sems, recv_sems, pss, lands, token = _chip_scatter_start(pss)

    dh = _in_proj_dx(d_s5, d_gla, d_low, w_main, w_low, token)
    grad_x, g_pre_w = _prenorm_bwd(xb, dh, dout, pre_norm_w)

    g_a, g_bc, g_ldt = _s5_prep_bwd(Wv["s5_A_re"], Wv["s5_A_im"], s5_log_dt, Wv["s5_B_re"], Wv["s5_B_im"],
                                    gbbd_re, gbbd_im, gct_re, gct_im, gab_re, gab_im)

    loss = lax.psum(loss11[0, 0], ("x", "y", "c"))

    g_vecs = jnp.concatenate([g_pre_w, g_post_w, g_D, g_glu_b, g_gate_bias, g_norm_w, g_ldt], axis=1)
    lanes_pad = -g_vecs.shape[1] % (8 * SUBLANES * LANES)
    g_vecs = jnp.pad(g_vecs, ((0, 0), (0, lanes_pad))).reshape(-1, LANES)
    ar_srcs = [g_vecs, g_a, g_bc]
    ar_rows = [a.shape[-2] // 8 for a in ar_srcs]
    ar_lands = [jax.ShapeDtypeStruct((8,) + a.shape[:-2] + (r, a.shape[-1]), F32) for a, r in zip(ar_srcs, ar_rows)]
    ar_ss, ar_rs, ar_srcs, ar_got, ar_token = _split_start(
        "allreduce_scatter_start", ar_srcs, ar_lands, _ar_scatter_copies(ar_rows), 7 * len(ar_srcs), [])

    pss, rcv = _chip_scatter_wait(send_sems, recv_sems, pss, lands, ar_token)
    halves = [_chip_sum(p, r, me_arr, "grad_chip_sum_" + n) for n, p, r in zip(sharded, pss, rcv)]
    others = _pair_swap(halves)
    ar_srcs, ar_got = _split_wait("allreduce_scatter_wait", ar_ss, ar_rs, ar_srcs, ar_got, _ar_scatter_copies(ar_rows),
                                  [others[0]])
    ar_red = _ar_sum(ar_srcs, ar_got, ar_rows)
    ag_ss, ag_rs, ar_red, ag_full, ag_token = _split_start(
        "allreduce_gather_start", ar_red, [jax.ShapeDtypeStruct(a.shape, F32) for a in ar_srcs],
        _ar_gather_copies(ar_rows), 7 * len(ar_red), [])
    G_out, D_out, M_out, V_out = {}, {}, {}, {}
    for n, g_own, g_other in zip(sharded, halves, others):
        g_, d_, m_, v_ = _adamw_sharded(W[n][0], g_own, g_other, M[n][0], V[n][0], c_arr, ag_token, "adamw_" + n)
        G_out[n], D_out[n], M_out[n], V_out[n] = g_[None], d_[None], m_[None], v_[None]
    ar_red, ag_full = _split_wait("allreduce_gather_wait", ag_ss, ag_rs, ar_red, ag_full, _ar_gather_copies(ar_rows),
                                  [D_out[n] for n in sharded])
    me8 = 2 * chip + lax.axis_index("c")
    r_vecs, r_a, r_bc = [lax.dynamic_update_slice_in_dim(f, r, me8 * rw, axis=f.ndim - 2)
                         for f, r, rw in zip(ag_full, ar_red, ar_rows)]
    outs4 = _adamw_small(r_vecs.reshape(1, -1), r_a, r_bc, [Wv[n] for n in small],
                         [view[n](M[n]) for n in small], [view[n](V[n]) for n in small])
    for store, o in zip((G_out, D_out, M_out, V_out), outs4):
        store.update({n: back[n](t) for n, t in zip(small, o)})

    return (loss, grad_x[None], *[G_out[n] for n in names], *[D_out[n] for n in names],
            *[M_out[n] for n in names], *[V_out[n] for n in names])
```

```python
import functools
import math

import jax
import jax.numpy as jnp
from jax import lax
from jax.experimental import pallas as pl
from jax.experimental.pallas import tpu as pltpu

F32 = jnp.float32
BF16 = jnp.bfloat16
HI = lax.Precision.HIGHEST
MESH = pl.DeviceIdType.MESH

EPS = 1e-6
S5_GROUP = 16
S5_STATE = 64
GLA_HK = 128
GLA_HV = 256
GLA_RANK = 16
GLA_TAU = 16.0
GLA_CHUNK = 64
GLA_STEP_CHUNKS = 4
LANES = 128
SUBLANES = 8
S5_COLS = 128
S5_LANES = (S5_COLS // S5_GROUP) * S5_STATE

ADAM_LR = 0.001
ADAM_B1 = 0.9
ADAM_B2 = 0.999
ADAM_EPS = 1e-08
ADAM_WD = 0.01
ADAM_STEP = 10

GELU_K = math.sqrt(2.0 / math.pi)
GELU_C = 0.044715


def _blk(n, pref, unit=LANES):
    best = None
    b = unit
    while b <= min(n, pref):
        if n % b == 0:
            best = b
        b += unit
    return best if best is not None else n


def _dot(a, b, dn=(((1,), (0,)), ((), ()))):
    return lax.dot_general(a.astype(BF16), b.astype(BF16), dn, preferred_element_type=F32)


def _dot_hi(a, b, dn=(((1,), (0,)), ((), ()))):
    return lax.dot_general(a, b, dn, precision=HI, preferred_element_type=F32)


NN = (((1,), (0,)), ((), ()))
NT = (((1,), (1,)), ((), ()))
TN = (((0,), (0,)), ((), ()))


def _sigmoid(x):
    return 1.0 / (1.0 + jnp.exp(-x))


def _gelu(y):
    return 0.5 * y * (1.0 + jnp.tanh(GELU_K * (y + GELU_C * y * y * y)))


def _gelu_grad(y):
    th = jnp.tanh(GELU_K * (y + GELU_C * y * y * y))
    return 0.5 * (1.0 + th) + 0.5 * y * (1.0 - th * th) * GELU_K * (1.0 + 3.0 * GELU_C * y * y)


def _mm(a, b, *, name, ta=False, tb=False, out_dtype=F32, bm=1024, bn=1024, bk=2048):
    if ta:
        K, M = a.shape
    else:
        M, K = a.shape
    if tb:
        N, K2 = b.shape
    else:
        K2, N = b.shape
    assert K == K2, (a.shape, b.shape, ta, tb)
    bm, bn, bk = _blk(M, bm), _blk(N, bn), _blk(K, bk)
    nk = K // bk
    dn = (((0 if ta else 1,), (1 if tb else 0,)), ((), ()))

    def body(a_ref, b_ref, o_ref, *acc):
        if nk == 1:
            o_ref[...] = _dot(a_ref[...], b_ref[...], dn).astype(out_dtype)
            return
        acc_ref, = acc
        k = pl.program_id(2)

        @pl.when(k == 0)
        def _():
            acc_ref[...] = jnp.zeros_like(acc_ref)

        acc_ref[...] += _dot(a_ref[...], b_ref[...], dn)

        @pl.when(k == nk - 1)
        def _():
            o_ref[...] = acc_ref[...].astype(out_dtype)

    a_spec = pl.BlockSpec((bk, bm), lambda i, j, k: (k, i)) if ta else pl.BlockSpec((bm, bk), lambda i, j, k: (i, k))
    b_spec = pl.BlockSpec((bn, bk), lambda i, j, k: (j, k)) if tb else pl.BlockSpec((bk, bn), lambda i, j, k: (k, j))
    return pl.pallas_call(
        body,
        name=name,
        grid=(M // bm, N // bn, nk),
        in_specs=[a_spec, b_spec],
        out_specs=pl.BlockSpec((bm, bn), lambda i, j, k: (i, j)),
        out_shape=jax.ShapeDtypeStruct((M, N), out_dtype),
        scratch_shapes=[pltpu.VMEM((bm, bn), F32)] if nk > 1 else [],
        compiler_params=pltpu.CompilerParams(dimension_semantics=("parallel", "parallel", "arbitrary")),
    )(a, b)


def _in_proj(h, w_main, w_low, after):
    M, K = h.shape
    N = w_main.shape[1]
    bm, bn = _blk(M, 1024), _blk(N, 1024)

    def body(h_ref, w_ref, wl_ref, _after_ref, o_ref, ol_ref):
        hv = h_ref[...]
        o_ref[...] = _dot(hv, w_ref[...])

        @pl.when(pl.program_id(1) == 0)
        def _():
            ol_ref[...] = _dot(hv, wl_ref[...])

    return pl.pallas_call(
        body, name="in_proj", grid=(M // bm, N // bn),
        in_specs=[pl.BlockSpec((bm, K), lambda i, j: (i, 0)), pl.BlockSpec((K, bn), lambda i, j: (0, j)),
                  pl.BlockSpec((K, LANES), lambda i, j: (0, 0)), pl.BlockSpec(memory_space=pl.ANY)],
        out_specs=[pl.BlockSpec((bm, bn), lambda i, j: (i, j)), pl.BlockSpec((bm, LANES), lambda i, j: (i, 0))],
        out_shape=[jax.ShapeDtypeStruct((M, N), F32), jax.ShapeDtypeStruct((M, LANES), F32)],
        compiler_params=pltpu.CompilerParams(dimension_semantics=("parallel", "arbitrary")),
    )(h, w_main, w_low, after)


def _in_proj_dx(a1, a2, al, b, bl, after, *, bm=1024, bn=1024, bk=2048):
    M, K1 = a1.shape
    K2 = a2.shape[1]
    N = b.shape[0]
    bm, bn = _blk(M, bm), _blk(N, bn)
    bk = _blk(math.gcd(K1, K2), bk)
    nk1, nk = K1 // bk, (K1 + K2) // bk

    def body(a1_ref, a2_ref, al_ref, b_ref, bl_ref, _after_ref, o_ref, acc_ref):
        k = pl.program_id(2)

        @pl.when(k == 0)
        def _():
            acc_ref[...] = _dot(al_ref[...], bl_ref[...], NT)

        @pl.when(k < nk1)
        def _():
            acc_ref[...] += _dot(a1_ref[...], b_ref[...], NT)

        @pl.when(k >= nk1)
        def _():
            acc_ref[...] += _dot(a2_ref[...], b_ref[...], NT)

        @pl.when(k == nk - 1)
        def _():
            o_ref[...] = acc_ref[...]

    return pl.pallas_call(
        body, name="in_proj_dx", grid=(M // bm, N // bn, nk),
        in_specs=[pl.BlockSpec((bm, bk), lambda i, j, k: (i, jnp.minimum(k, nk1 - 1))),
                  pl.BlockSpec((bm, bk), lambda i, j, k: (i, jnp.maximum(k - nk1, 0))),
                  pl.BlockSpec((bm, LANES), lambda i, j, k: (i, 0)),
                  pl.BlockSpec((bn, bk), lambda i, j, k: (j, k)),
                  pl.BlockSpec((bn, LANES), lambda i, j, k: (j, 0)),
                  pl.BlockSpec(memory_space=pl.ANY)],
        out_specs=pl.BlockSpec((bm, bn), lambda i, j, k: (i, j)),
        out_shape=jax.ShapeDtypeStruct((M, N), F32),
        scratch_shapes=[pltpu.VMEM((bm, bn), F32)],
        compiler_params=pltpu.CompilerParams(dimension_semantics=("parallel", "parallel", "arbitrary")),
    )(a1, a2, al, b, bl, after)


def _in_proj_dw(a, b1, b2, bl, *, bm=1024, bn=1024, bk=2048):
    K, M = a.shape
    N1, N2 = b1.shape[1], b2.shape[1]
    bm, bk = _blk(M, bm), _blk(K, bk)
    bn = _blk(math.gcd(N1, N2), bn)
    nj1, nj = N1 // bn, (N1 + N2) // bn
    nk = K // bk

    def body(a_ref, b1_ref, b2_ref, bl_ref, o_ref, ol_ref, acc_ref, accl_ref):
        j = pl.program_id(1)
        k = pl.program_id(2)

        @pl.when(k == 0)
        def _():
            acc_ref[...] = jnp.zeros_like(acc_ref)

        @pl.when(j < nj1)
        def _():
            acc_ref[...] += _dot(a_ref[...], b1_ref[...], TN)

        @pl.when(j >= nj1)
        def _():
            acc_ref[...] += _dot(a_ref[...], b2_ref[...], TN)

        @pl.when(k == nk - 1)
        def _():
            o_ref[...] = acc_ref[...].astype(BF16)

        @pl.when(j == 0)
        def _():
            low = _dot(a_ref[...], bl_ref[...], TN)

            @pl.when(k == 0)
            def _():
                accl_ref[...] = low

            @pl.when(k > 0)
            def _():
                accl_ref[...] += low

            @pl.when(k == nk - 1)
            def _():
                ol_ref[...] = accl_ref[...].astype(BF16)

    return pl.pallas_call(
        body, name="in_proj_dw", grid=(M // bm, nj, nk),
        in_specs=[pl.BlockSpec((bk, bm), lambda i, j, k: (k, i)),
                  pl.BlockSpec((bk, bn), lambda i, j, k: (jnp.where(j < nj1, k, nk - 1), jnp.minimum(j, nj1 - 1))),
                  pl.BlockSpec((bk, bn), lambda i, j, k: (jnp.where(j >= nj1, k, 0), jnp.maximum(j - nj1, 0))),
                  pl.BlockSpec((bk, LANES), lambda i, j, k: (jnp.where(j == 0, k, nk - 1), 0))],
        out_specs=[pl.BlockSpec((bm, bn), lambda i, j, k: (i, j)), pl.BlockSpec((bm, LANES), lambda i, j, k: (i, 0))],
        out_shape=[jax.ShapeDtypeStruct((M, N1 + N2), BF16), jax.ShapeDtypeStruct((M, LANES), BF16)],
        scratch_shapes=[pltpu.VMEM((bm, bn), F32), pltpu.VMEM((bm, LANES), F32)],
        compiler_params=pltpu.CompilerParams(dimension_semantics=("parallel", "arbitrary", "arbitrary")),
    )(a, b1, b2, bl)


def _prenorm_fwd(x, w, after):
    L, D = x.shape
    tr = _blk(L, 256, SUBLANES)

    def body(x_ref, w_ref, _after_ref, h_ref):
        xv = x_ref[...]
        r = lax.rsqrt(jnp.mean(xv * xv, axis=-1, keepdims=True) + EPS)
        h_ref[...] = (xv * r * w_ref[...]).astype(BF16)

    return pl.pallas_call(
        body, name="prenorm_fwd", grid=(L // tr,),
        in_specs=[pl.BlockSpec((tr, D), lambda i: (i, 0)), pl.BlockSpec((1, D), lambda i: (0, 0)),
                  pl.BlockSpec(memory_space=pl.ANY)],
        out_specs=pl.BlockSpec((tr, D), lambda i: (i, 0)),
        out_shape=jax.ShapeDtypeStruct((L, D), BF16),
        compiler_params=pltpu.CompilerParams(dimension_semantics=("parallel",)),
    )(x, w, after)


def _post_fwd_bwd(mixed, x, target, w):
    L, D = x.shape
    tr = _blk(L, 256, SUBLANES)
    nsteps = L // tr

    def body(mx_ref, x_ref, t_ref, w_ref, loss_ref, dm_ref, dout_ref, gw_ref, acc_ref):
        i = pl.program_id(0)

        @pl.when(i == 0)
        def _():
            acc_ref[...] = jnp.zeros_like(acc_ref)
            gw_ref[...] = jnp.zeros_like(gw_ref)

        mx = mx_ref[...]
        wv = w_ref[...]
        r = lax.rsqrt(jnp.mean(mx * mx, axis=-1, keepdims=True) + EPS)
        n = mx * r
        err = x_ref[...] + n * wv - t_ref[...]
        acc_ref[...] += jnp.sum(err * err, axis=0, keepdims=True)
        dout = err * (1.0 / D)
        dout_ref[...] = dout
        gw_ref[...] += jnp.sum(dout * n, axis=0, keepdims=True)
        dn = dout * wv
        dm_ref[...] = (r * (dn - n * jnp.mean(dn * n, axis=-1, keepdims=True))).astype(BF16)

        @pl.when(i == nsteps - 1)
        def _():
            loss_ref[...] = jnp.sum(acc_ref[...], axis=-1, keepdims=True) * (0.5 / D)

    row = pl.BlockSpec((tr, D), lambda i: (i, 0))
    vec = pl.BlockSpec((1, D), lambda i: (0, 0))
    return pl.pallas_call(
        body, name="post_fwd_bwd", grid=(nsteps,),
        in_specs=[row, row, row, vec],
        out_specs=[pl.BlockSpec((1, 1), lambda i: (0, 0)), row, row, vec],
        out_shape=[jax.ShapeDtypeStruct((1, 1), F32), jax.ShapeDtypeStruct((L, D), BF16),
                   jax.ShapeDtypeStruct((L, D), F32), jax.ShapeDtypeStruct((1, D), F32)],
        scratch_shapes=[pltpu.VMEM((1, D), F32)],
        compiler_params=pltpu.CompilerParams(dimension_semantics=("arbitrary",)),
    )(mixed, x, target, w)


def _prenorm_bwd(x, dh, dout, w):
    L, D = x.shape
    tr = _blk(L, 256, SUBLANES)

    def body(x_ref, a_ref, dout_ref, w_ref, gx_ref, gw_ref):
        i = pl.program_id(0)

        @pl.when(i == 0)
        def _():
            gw_ref[...] = jnp.zeros_like(gw_ref)

        xv = x_ref[...]
        r = lax.rsqrt(jnp.mean(xv * xv, axis=-1, keepdims=True) + EPS)
        n = xv * r
        dh = a_ref[...]
        gw_ref[...] += jnp.sum(dh * n, axis=0, keepdims=True)
        dn = dh * w_ref[...]
        gx_ref[...] = dout_ref[...] + r * (dn - n * jnp.mean(dn * n, axis=-1, keepdims=True))

    row = pl.BlockSpec((tr, D), lambda i: (i, 0))
    vec = pl.BlockSpec((1, D), lambda i: (0, 0))
    return pl.pallas_call(
        body, name="prenorm_bwd", grid=(L // tr,),
        in_specs=[row, row, row, vec],
        out_specs=[row, vec],
        out_shape=[jax.ShapeDtypeStruct((L, D), F32), jax.ShapeDtypeStruct((1, D), F32)],
        compiler_params=pltpu.CompilerParams(dimension_semantics=("arbitrary",)),
    )(x, dh, dout, w)


def _s5_disc(a_re_raw, a_im, dt):
    a_re = jnp.minimum(a_re_raw, -1e-4)
    mag = jnp.exp(a_re * dt)
    ph = a_im * dt
    ab_re = mag * jnp.cos(ph)
    ab_im = mag * jnp.sin(ph)
    inv_n = 1.0 / (a_re * a_re + a_im * a_im)
    ia_re = a_re * inv_n
    ia_im = -a_im * inv_n
    n_re = ab_re - 1.0
    f_re = n_re * ia_re - ab_im * ia_im
    f_im = n_re * ia_im + ab_im * ia_re
    return a_re, ab_re, ab_im, f_re, f_im, ia_re, ia_im


def _iota2(shape, dim):
    return lax.broadcasted_iota(jnp.int32, shape, dim)


def _group_mask(rows, rows_per_group):
    shift = rows_per_group.bit_length() - 1
    return (_iota2((rows, S5_LANES), 0) >> shift) == (_iota2((rows, S5_LANES), 1) >> (S5_STATE.bit_length() - 1))


def _lane_tiler(dtype):
    return ((_iota2((S5_STATE, S5_LANES), 1) & (S5_STATE - 1)) == _iota2((S5_STATE, S5_LANES), 0)).astype(dtype)


def _row_to_col(row, n):
    eye = (_iota2((n, n), 0) == _iota2((n, n), 1)).astype(F32)
    return jnp.sum(eye * row, axis=1, keepdims=True)


def _group_repeat(G):
    return ((_iota2((G * S5_GROUP, G), 0) >> (S5_GROUP.bit_length() - 1)) == _iota2((G * S5_GROUP, G), 1)).astype(F32)


S5_TABS = 18


def _s5_prep_fwd(a_re, a_im, log_dt, b_re, b_im, c_re, c_im, after, seg):
    G, P = a_re.shape
    nb = G * S5_GROUP // S5_COLS
    g8 = S5_COLS // S5_GROUP
    assert seg & (seg - 1) == 0, seg

    def body(are_ref, aim_ref, ldt_ref, bre_ref, bim_ref, cre_ref, cim_ref, _after_ref,
             bbre_ref, bbim_ref, ctre_ref, ctim_ref, tab_ref, pt_ref):
        dt = jnp.exp(_row_to_col(ldt_ref[...], G))
        _, ab_re, ab_im, f_re, f_im, _, _ = _s5_disc(are_ref[...], aim_ref[...], dt)
        rep = _group_repeat(G)
        fx_re = _dot_hi(rep, f_re)
        fx_im = _dot_hi(rep, f_im)
        br, bi = bre_ref[...], bim_ref[...]
        bb_re = fx_re * br - fx_im * bi
        bb_im = fx_re * bi + fx_im * br
        tile_bf = _lane_tiler(BF16)
        mask = _group_mask(S5_COLS, S5_GROUP)
        for jb in range(nb):
            rs = slice(jb * S5_COLS, (jb + 1) * S5_COLS)
            for src, dst in ((bb_re[rs], bbre_ref), (bb_im[rs], bbim_ref), (cre_ref[rs, :], ctre_ref), (cim_ref[rs, :], ctim_ref)):
                dst[jb] = jnp.where(mask, _dot(src, tile_bf), 0.0).astype(BF16)

        tile_f = _lane_tiler(F32)
        mask8 = _group_mask(g8, 1)
        row = _iota2((SUBLANES, S5_LANES), 0)
        slab = (SUBLANES, S5_LANES)
        cmul = lambda p, q: (p[0] * q[0] - p[1] * q[1], p[0] * q[1] + p[1] * q[0])
        for jb in range(nb):
            gs = slice(jb * g8, (jb + 1) * g8)

            def lanes(m):
                v = jnp.sum(jnp.where(mask8, _dot_hi(m[gs], tile_f), 0.0), axis=0, keepdims=True)
                return jnp.broadcast_to(v, slab)

            a1 = (lanes(ab_re), lanes(ab_im))
            tab_ref[jb, 0], tab_ref[jb, 1] = a1

            def powers(i, p):
                off = pl.multiple_of(i * SUBLANES, SUBLANES)
                pt_ref[jb, 0, pl.ds(off, SUBLANES), :] = p[0]
                pt_ref[jb, 1, pl.ds(off, SUBLANES), :] = p[1]
                return cmul(p, a1)

            lax.fori_loop(0, seg, powers, a1)
            aseg = a1
            for _ in range(seg.bit_length() - 1):
                aseg = cmul(aseg, aseg)
            pw = [aseg]
            for _ in range(1, SUBLANES):
                pw.append(cmul(pw[-1], aseg))
            for lvl, k in enumerate((1, 2, 4)):
                tab_ref[jb, 2 + 2 * lvl] = jnp.where(row >= k, pw[k - 1][0], 0.0)
                tab_ref[jb, 3 + 2 * lvl] = jnp.where(row >= k, pw[k - 1][1], 0.0)
                tab_ref[jb, 10 + 2 * lvl] = jnp.where(row < SUBLANES - k, pw[k - 1][0], 0.0)
                tab_ref[jb, 11 + 2 * lvl] = jnp.where(row < SUBLANES - k, -pw[k - 1][1], 0.0)
            f_r = f_i = r_r = r_i = jnp.zeros(slab, F32)
            for i in range(SUBLANES):
                f_r = jnp.where(row == i, pw[i][0], f_r)
                f_i = jnp.where(row == i, pw[i][1], f_i)
                r_r = jnp.where(row == i, pw[SUBLANES - 1 - i][0], r_r)
                r_i = jnp.where(row == i, -pw[SUBLANES - 1 - i][1], r_i)
            tab_ref[jb, 8] = f_r
            tab_ref[jb, 9] = f_i
            tab_ref[jb, 16] = r_r
            tab_ref[jb, 17] = r_i

    vm = pl.BlockSpec(memory_space=pltpu.VMEM)
    bd = jax.ShapeDtypeStruct((nb, S5_COLS, S5_LANES), BF16)
    return pl.pallas_call(
        body, name="s5_prep_fwd",
        in_specs=[vm] * 7 + [pl.BlockSpec(memory_space=pl.ANY)], out_specs=[vm] * 6,
        out_shape=[bd, bd, bd, bd, jax.ShapeDtypeStruct((nb, S5_TABS, SUBLANES, S5_LANES), F32),
                   jax.ShapeDtypeStruct((nb, 2, seg * SUBLANES, S5_LANES), F32)],
    )(a_re, a_im, log_dt, b_re, b_im, c_re, c_im, after)


def _s5_prep_bwd(a_re, a_im, log_dt, b_re, b_im, gbb_re, gbb_im, gct_re, gct_im, gab_re, gab_im):
    G, P = a_re.shape
    nb = G * S5_GROUP // S5_COLS
    g8 = S5_COLS // S5_GROUP

    def body(are_ref, aim_ref, ldt_ref, bre_ref, bim_ref, gbr_ref, gbi_ref, gcr_ref, gci_ref, gar_ref, gai_ref,
             o_a, o_bc, o_ldt):
        dt = jnp.exp(_row_to_col(ldt_ref[...], G))
        a_raw = are_ref[...]
        a_imv = aim_ref[...]
        a_re_c, ab_re, ab_im, f_re, f_im, ia_re, ia_im = _s5_disc(a_raw, a_imv, dt)
        tile_f = _lane_tiler(F32)
        mask = _group_mask(S5_COLS, S5_GROUP)
        mask8 = _group_mask(g8, 1)
        for jb in range(nb):
            rs = slice(jb * S5_COLS, (jb + 1) * S5_COLS)
            gs = slice(jb * g8, (jb + 1) * g8)
            ls = slice(jb * S5_LANES, (jb + 1) * S5_LANES)
            for k, src in enumerate((gbr_ref, gbi_ref, gcr_ref, gci_ref)):
                o_bc[k, rs, :] = _dot_hi(jnp.where(mask, src[jb], 0.0), tile_f, NT)
            for k, src in enumerate((gar_ref, gai_ref)):
                o_a[k, gs, :] = _dot_hi(jnp.where(mask8, src[:, ls], 0.0), tile_f, NT)
        rep = _group_repeat(G)
        fx_re = _dot_hi(rep, f_re)
        fx_im = _dot_hi(rep, f_im)
        gbr, gbi = o_bc[0], o_bc[1]
        br, bi = bre_ref[...], bim_ref[...]
        o_bc[0] = fx_re * gbr + fx_im * gbi
        o_bc[1] = fx_re * gbi - fx_im * gbr
        gf_re = _dot_hi(rep, br * gbr + bi * gbi, TN)
        gf_im = _dot_hi(rep, br * gbi - bi * gbr, TN)
        gab_r = o_a[0] + ia_re * gf_re + ia_im * gf_im
        gab_i = o_a[1] + ia_re * gf_im - ia_im * gf_re
        q_re = f_re * ia_re - f_im * ia_im
        q_im = f_re * ia_im + f_im * ia_re
        ga_re = -(q_re * gf_re + q_im * gf_im)
        ga_im = -(q_re * gf_im - q_im * gf_re)
        gth_re = ab_re * gab_r + ab_im * gab_i
        gth_im = ab_re * gab_i - ab_im * gab_r
        ga_re = ga_re + dt * gth_re
        ga_im = ga_im + dt * gth_im
        gdt = jnp.sum(a_re_c * gth_re + a_imv * gth_im, axis=-1, keepdims=True)
        eye = (_iota2((G, G), 0) == _iota2((G, G), 1)).astype(F32)
        o_ldt[...] = jnp.sum(eye * (gdt * dt), axis=0, keepdims=True)
        slope = jnp.where(a_raw < -1e-4, 1.0, jnp.where(a_raw == -1e-4, 0.5, 0.0))
        o_a[0] = ga_re * slope
        o_a[1] = ga_im

    vm = pl.BlockSpec(memory_space=pltpu.VMEM)
    return pl.pallas_call(
        body, name="s5_prep_bwd",
        in_specs=[vm] * 11, out_specs=[vm] * 3,
        out_shape=[jax.ShapeDtypeStruct((2, G, P), F32), jax.ShapeDtypeStruct((4, G * S5_GROUP, P), F32),
                   jax.ShapeDtypeStruct((1, G), F32)],
    )(a_re, a_im, log_dt, b_re, b_im, gbb_re, gbb_im, gct_re, gct_im, gab_re, gab_im)


def _scan8(xr, xi, tab_ref, base, shifts):
    for lvl, sh in enumerate(shifts):
        mr = tab_ref[0, base + 2 * lvl]
        mi = tab_ref[0, base + 2 * lvl + 1]
        ar = pltpu.roll(xr, sh, 0)
        ai = pltpu.roll(xi, sh, 0)
        xr, xi = xr + mr * ar - mi * ai, xi + mr * ai + mi * ar
    return xr, xi


def _to_segments(src_ref, dst_ref, seg):
    for i in range(seg):
        dst_ref[i * SUBLANES:(i + 1) * SUBLANES, :] = src_ref[pl.ds(i, SUBLANES, stride=seg), :]


def _from_segments(src_ref, dst_ref, seg):
    for i in range(seg):
        dst_ref[pl.ds(i, SUBLANES, stride=seg), :] = src_ref[i * SUBLANES:(i + 1) * SUBLANES, :]


def _slab(i):
    return pl.ds(pl.multiple_of(i * SUBLANES, SUBLANES), SUBLANES)


def _s5_scan_fwd(proj_main, bbd_re, bbd_im, cbd_re, cbd_im, dvec, tab, ptab, DS):
    L = proj_main.shape[0]
    nb = DS // S5_COLS
    tb = _blk(L, 512, SUBLANES)
    nt = L // tb
    seg = tb // SUBLANES

    def body(u_ref, bre_ref, bim_ref, cre_ref, cim_ref, d_ref, tab_ref, pt_ref, y_ref, sre_ref, sim_ref,
             up_ref, yp_ref, car_ref):
        t = pl.program_id(1)

        @pl.when(t == 0)
        def _():
            car_ref[...] = jnp.zeros_like(car_ref)

        _to_segments(u_ref, up_ref, seg)
        up = up_ref[...]
        sre_ref[...] = _dot(up, bre_ref[0])
        sim_ref[...] = _dot(up, bim_ref[0])
        ar, ai = tab_ref[0, 0], tab_ref[0, 1]

        def pass1(i, x):
            xr = ar * x[0] - ai * x[1] + sre_ref[_slab(i), :]
            xi = ar * x[1] + ai * x[0] + sim_ref[_slab(i), :]
            sre_ref[_slab(i), :] = xr
            sim_ref[_slab(i), :] = xi
            return xr, xi

        zero = jnp.zeros((SUBLANES, S5_LANES), F32)
        er, ei = lax.fori_loop(0, seg, pass1, (zero, zero))
        cin_r, cin_i = car_ref[0], car_ref[1]
        sr, si = _scan8(er, ei, tab_ref, 2, (1, 2, 4))
        pr, pi = tab_ref[0, 8], tab_ref[0, 9]
        sr, si = sr + pr * cin_r - pi * cin_i, si + pr * cin_i + pi * cin_r
        row0 = _iota2((SUBLANES, S5_LANES), 0) == 0
        cr = jnp.where(row0, cin_r, pltpu.roll(sr, 1, 0))
        ci = jnp.where(row0, cin_i, pltpu.roll(si, 1, 0))
        car_ref[0] = jnp.broadcast_to(sr[SUBLANES - 1:SUBLANES, :], sr.shape)
        car_ref[1] = jnp.broadcast_to(si[SUBLANES - 1:SUBLANES, :], si.shape)

        def pass2(i, _):
            qr, qi = pt_ref[0, 0, _slab(i), :], pt_ref[0, 1, _slab(i), :]
            sre_ref[_slab(i), :] += qr * cr - qi * ci
            sim_ref[_slab(i), :] += qr * ci + qi * cr
            return 0

        lax.fori_loop(0, seg, pass2, 0, unroll=4)
        yp_ref[...] = _dot(sre_ref[...], cre_ref[0], NT) - _dot(sim_ref[...], cim_ref[0], NT) + d_ref[...] * up
        _from_segments(yp_ref, y_ref, seg)

    return pl.pallas_call(
        body, name="s5_scan_fwd", grid=(nb, nt),
        in_specs=[
            pl.BlockSpec((tb, S5_COLS), lambda j, t: (t, j)),
            pl.BlockSpec((1, S5_COLS, S5_LANES), lambda j, t: (j, 0, 0)),
            pl.BlockSpec((1, S5_COLS, S5_LANES), lambda j, t: (j, 0, 0)),
            pl.BlockSpec((1, S5_COLS, S5_LANES), lambda j, t: (j, 0, 0)),
            pl.BlockSpec((1, S5_COLS, S5_LANES), lambda j, t: (j, 0, 0)),
            pl.BlockSpec((1, S5_COLS), lambda j, t: (0, j)),
            pl.BlockSpec((1, S5_TABS, SUBLANES, S5_LANES), lambda j, t: (j, 0, 0, 0)),
            pl.BlockSpec((1, 2, tb, S5_LANES), lambda j, t: (j, 0, 0, 0)),
        ],
        out_specs=[
            pl.BlockSpec((tb, S5_COLS), lambda j, t: (t, j)),
            pl.BlockSpec((tb, S5_LANES), lambda j, t: (t, j)),
            pl.BlockSpec((tb, S5_LANES), lambda j, t: (t, j)),
        ],
        out_shape=[jax.ShapeDtypeStruct((L, DS), F32),
                   jax.ShapeDtypeStruct((L, nb * S5_LANES), F32),
                   jax.ShapeDtypeStruct((L, nb * S5_LANES), F32)],
        scratch_shapes=[pltpu.VMEM((tb, S5_COLS), F32), pltpu.VMEM((tb, S5_COLS), F32),
                        pltpu.VMEM((2, SUBLANES, S5_LANES), F32)],
        compiler_params=pltpu.CompilerParams(dimension_semantics=("parallel", "arbitrary")),
    )(proj_main, bbd_re, bbd_im, cbd_re, cbd_im, dvec, tab, ptab)


def _s5_scan_bwd(dy, proj_main, s_re, s_im, bbd_re, bbd_im, cbd_re, cbd_im, dvec, tab, ptab, d_s5, DS):
    L = proj_main.shape[0]
    nb = DS // S5_COLS
    tb = _blk(L, 512, SUBLANES)
    nt = L // tb
    seg = tb // SUBLANES
    tb8 = tb // SUBLANES

    def body(dy_ref, u_ref, sre_ref, sim_ref, pre_ref, pim_ref, bre_ref, bim_ref, cre_ref, cim_ref, d_ref, tab_ref, pt_ref,
             _ds5_ref, du_ref, gd_ref, gcre_ref, gcim_ref, gbre_ref, gbim_ref, gare_ref, gaim_ref,
             lre_ref, lim_ref, up_ref, dyp_ref, dup_ref, duo_ref, car_ref):
        t = pl.program_id(1)

        @pl.when(t == 0)
        def _():
            car_ref[...] = jnp.zeros_like(car_ref)
            gd_ref[...] = jnp.zeros_like(gd_ref)
            gcre_ref[...] = jnp.zeros_like(gcre_ref)
            gcim_ref[...] = jnp.zeros_like(gcim_ref)
            gbre_ref[...] = jnp.zeros_like(gbre_ref)
            gbim_ref[...] = jnp.zeros_like(gbim_ref)
            gare_ref[...] = jnp.zeros_like(gare_ref)
            gaim_ref[...] = jnp.zeros_like(gaim_ref)

        _to_segments(dy_ref, dyp_ref, seg)
        _to_segments(u_ref, up_ref, seg)
        dyv = dyp_ref[...]
        u = up_ref[...]
        gd_ref[...] += jnp.sum(dyv * u, axis=0, keepdims=True)
        lre_ref[...] = _dot(dyv, cre_ref[0])
        lim_ref[...] = -_dot(dyv, cim_ref[0])
        gcre_ref[0] += _dot(dyv, sre_ref[...], TN)
        gcim_ref[0] -= _dot(dyv, sim_ref[...], TN)
        ar, ai = tab_ref[0, 0], -tab_ref[0, 1]

        def pass1(k, x):
            i = seg - 1 - k
            xr = ar * x[0] - ai * x[1] + lre_ref[_slab(i), :]
            xi = ar * x[1] + ai * x[0] + lim_ref[_slab(i), :]
            lre_ref[_slab(i), :] = xr
            lim_ref[_slab(i), :] = xi
            return xr, xi

        zero = jnp.zeros((SUBLANES, S5_LANES), F32)
        er, ei = lax.fori_loop(0, seg, pass1, (zero, zero))
        cin_r, cin_i = car_ref[0], car_ref[1]
        lr, li = _scan8(er, ei, tab_ref, 10, (7, 6, 4))
        pr, pi = tab_ref[0, 16], tab_ref[0, 17]
        lr, li = lr + pr * cin_r - pi * cin_i, li + pr * cin_i + pi * cin_r
        rows = _iota2((SUBLANES, S5_LANES), 0)
        cr = jnp.where(rows == SUBLANES - 1, cin_r, pltpu.roll(lr, SUBLANES - 1, 0))
        ci = jnp.where(rows == SUBLANES - 1, cin_i, pltpu.roll(li, SUBLANES - 1, 0))
        car_ref[0] = jnp.broadcast_to(lr[0:1, :], lr.shape)
        car_ref[1] = jnp.broadcast_to(li[0:1, :], li.shape)

        first = (t == nt - 1).astype(F32)
        head_re = jnp.broadcast_to(pre_ref[SUBLANES - 1:SUBLANES, :], zero.shape) * (1.0 - first)
        head_im = jnp.broadcast_to(pim_ref[SUBLANES - 1:SUBLANES, :], zero.shape) * (1.0 - first)
        last = _slab(seg - 1)
        sp0_re = jnp.where(rows == 0, head_re, pltpu.roll(sre_ref[last, :], 1, 0))
        sp0_im = jnp.where(rows == 0, head_im, pltpu.roll(sim_ref[last, :], 1, 0))

        def pass2(i, acc):
            j = seg - 1 - i
            qr, qi = pt_ref[0, 0, _slab(j), :], -pt_ref[0, 1, _slab(j), :]
            xr = lre_ref[_slab(i), :] + qr * cr - qi * ci
            xi = lim_ref[_slab(i), :] + qr * ci + qi * cr
            lre_ref[_slab(i), :] = xr
            lim_ref[_slab(i), :] = xi
            prev = _slab(jnp.maximum(i - 1, 0))
            sp_re = jnp.where(i == 0, sp0_re, sre_ref[prev, :])
            sp_im = jnp.where(i == 0, sp0_im, sim_ref[prev, :])
            return acc[0] + sp_re * xr + sp_im * xi, acc[1] + sp_re * xi - sp_im * xr

        acc_re, acc_im = lax.fori_loop(0, seg, pass2, (zero, zero), unroll=2)
        gare_ref[...] += jnp.sum(acc_re, axis=0, keepdims=True)
        gaim_ref[...] += jnp.sum(acc_im, axis=0, keepdims=True)
        lre = lre_ref[...]
        lim = lim_ref[...]
        dup_ref[...] = dyv * d_ref[...] + _dot(lre, bre_ref[0], NT) + _dot(lim, bim_ref[0], NT)
        _from_segments(dup_ref, duo_ref, seg)
        du_ref[...] = duo_ref[...].astype(BF16)
        gbre_ref[0] += _dot(u, lre, TN)
        gbim_ref[0] += _dot(u, lim, TN)

    rt = lambda t: nt - 1 - t
    col = pl.BlockSpec((tb, S5_COLS), lambda j, t: (rt(t), j))
    st = pl.BlockSpec((tb, S5_LANES), lambda j, t: (rt(t), j))
    prev = pl.BlockSpec((SUBLANES, S5_LANES), lambda j, t: (jnp.maximum(rt(t) * tb8 - 1, 0), j))
    bmat = pl.BlockSpec((1, S5_COLS, S5_LANES), lambda j, t: (j, 0, 0))
    cmat = bmat
    return pl.pallas_call(
        body, name="s5_scan_bwd", grid=(nb, nt),
        in_specs=[col, col, st, st, prev, prev, bmat, bmat, cmat, cmat,
                  pl.BlockSpec((1, S5_COLS), lambda j, t: (0, j)),
                  pl.BlockSpec((1, S5_TABS, SUBLANES, S5_LANES), lambda j, t: (j, 0, 0, 0)),
                  pl.BlockSpec((1, 2, tb, S5_LANES), lambda j, t: (j, 0, 0, 0)),
                  pl.BlockSpec(memory_space=pl.ANY)],
        out_specs=[col, pl.BlockSpec((1, S5_COLS), lambda j, t: (0, j)), cmat, cmat, bmat, bmat,
                   pl.BlockSpec((1, S5_LANES), lambda j, t: (0, j)), pl.BlockSpec((1, S5_LANES), lambda j, t: (0, j))],
        input_output_aliases={13: 0},
        out_shape=[jax.ShapeDtypeStruct((L, 2 * DS), BF16), jax.ShapeDtypeStruct((1, DS), F32),
                   jax.ShapeDtypeStruct((nb, S5_COLS, S5_LANES), F32), jax.ShapeDtypeStruct((nb, S5_COLS, S5_LANES), F32),
                   jax.ShapeDtypeStruct((nb, S5_COLS, S5_LANES), F32), jax.ShapeDtypeStruct((nb, S5_COLS, S5_LANES), F32),
                   jax.ShapeDtypeStruct((1, nb * S5_LANES), F32), jax.ShapeDtypeStruct((1, nb * S5_LANES), F32)],
        scratch_shapes=[pltpu.VMEM((tb, S5_LANES), F32), pltpu.VMEM((tb, S5_LANES), F32)]
        + [pltpu.VMEM((tb, S5_COLS), F32)] * 4 + [pltpu.VMEM((2, SUBLANES, S5_LANES), F32)],
        compiler_params=pltpu.CompilerParams(dimension_semantics=("parallel", "arbitrary")),
    )(dy, proj_main, s_re, s_im, s_re, s_im, bbd_re, bbd_im, cbd_re, cbd_im, dvec, tab, ptab, d_s5)


def _s5_post_fwd(y_pre, proj_main, glu_w, glu_b, DS):
    L = y_pre.shape[0]
    tr = _blk(L, 256, SUBLANES)

    def body(y_ref, z_ref, w_ref, b_ref, o_ref, t_ref):
        y1 = _gelu(y_ref[...])
        t = _dot(y1, w_ref[...]) + b_ref[...]
        t_ref[...] = t
        z = z_ref[...]
        o_ref[...] = (y1 * _sigmoid(t) * (z * _sigmoid(z))).astype(BF16)

    row = pl.BlockSpec((tr, DS), lambda i: (i, 0))
    return pl.pallas_call(
        body, name="s5_post_fwd", grid=(L // tr,),
        in_specs=[row, pl.BlockSpec((tr, DS), lambda i: (i, 1)), pl.BlockSpec((DS, DS), lambda i: (0, 0)),
                  pl.BlockSpec((1, DS), lambda i: (0, 0))],
        out_specs=[row, row],
        out_shape=[jax.ShapeDtypeStruct((L, 2 * DS), BF16), jax.ShapeDtypeStruct((L, DS), F32)],
        compiler_params=pltpu.CompilerParams(dimension_semantics=("parallel",)),
    )(y_pre, proj_main, glu_w, glu_b)


def _s5_post_bwd(d_ycat, y_pre, proj_main, t_pre, glu_w, DS):
    L = y_pre.shape[0]
    tr = _blk(L, 256, SUBLANES)

    def body(dy_ref, y_ref, z_ref, t_ref, w_ref, dyp_ref, dz_ref, dt_ref, y1_ref, gb_ref):
        i = pl.program_id(0)

        @pl.when(i == 0)
        def _():
            gb_ref[...] = jnp.zeros_like(gb_ref)

        dy = dy_ref[...]
        yp = y_ref[...]
        z = z_ref[...]
        y1 = _gelu(yp)
        sg = _sigmoid(t_ref[...])
        sz = _sigmoid(z)
        c = y1 * sg
        d_c = dy * (z * sz)
        dz_ref[...] = (dy * c * (sz * (1.0 + z * (1.0 - sz)))).astype(BF16)
        d_t = d_c * y1 * sg * (1.0 - sg)
        gb_ref[...] += jnp.sum(d_t, axis=0, keepdims=True)
        dt_ref[...] = d_t.astype(BF16)
        y1_ref[...] = y1.astype(BF16)
        d_y1 = d_c * sg + _dot(d_t, w_ref[...], NT)
        dyp_ref[...] = d_y1 * _gelu_grad(yp)

    row = pl.BlockSpec((tr, DS), lambda i: (i, 0))
    return pl.pallas_call(
        body, name="s5_post_bwd", grid=(L // tr,),
        in_specs=[row, row, pl.BlockSpec((tr, DS), lambda i: (i, 1)), row, pl.BlockSpec((DS, DS), lambda i: (0, 0))],
        out_specs=[row, pl.BlockSpec((tr, DS), lambda i: (i, 1)), row, row, pl.BlockSpec((1, DS), lambda i: (0, 0))],
        out_shape=[jax.ShapeDtypeStruct((L, DS), F32), jax.ShapeDtypeStruct((L, 2 * DS), BF16),
                   jax.ShapeDtypeStruct((L, DS), BF16), jax.ShapeDtypeStruct((L, DS), BF16),
                   jax.ShapeDtypeStruct((1, DS), F32)],
        compiler_params=pltpu.CompilerParams(dimension_semantics=("arbitrary",)),
    )(d_ycat, y_pre, proj_main, t_pre, glu_w)


def _row_cumsum(x, reverse=False):
    n = x.shape[0]
    row = lax.broadcasted_iota(jnp.int32, x.shape, 0)
    k = 1
    while k < n:
        if reverse:
            x = x + jnp.where(row < n - k, pltpu.roll(x, n - k, 0), 0.0)
        else:
            x = x + jnp.where(row >= k, pltpu.roll(x, k, 0), 0.0)
        k *= 2
    return x


def _gla_gates(glow, gu_ref, gb_ref):
    a = _dot(glow, gu_ref[...]) + gb_ref[...]
    lg = (jnp.minimum(a, 0.0) - jnp.log(1.0 + jnp.exp(-jnp.abs(a)))) * (1.0 / GLA_TAU)
    ri = lax.broadcasted_iota(jnp.int32, (GLA_CHUNK, GLA_CHUNK), 0)
    ci = lax.broadcasted_iota(jnp.int32, (GLA_CHUNK, GLA_CHUNK), 1)
    b = _row_cumsum(lg)
    b_last = b[GLA_CHUNK - 1:GLA_CHUNK, :]
    return a, b, b_last, ri >= ci


def _gla_specs(DS, DK, DV, c, cmap):
    return [
        pl.BlockSpec((c, DK), lambda n: (cmap(n), 2 * DS // DK)),
        pl.BlockSpec((c, DK), lambda n: (cmap(n), 2 * DS // DK + 1)),
        pl.BlockSpec((c, DV), lambda n: (cmap(n), (2 * DS + 2 * DK) // DV)),
        pl.BlockSpec((c, DV), lambda n: (cmap(n), (2 * DS + 2 * DK) // DV + 1)),
    ]


def _gla_fwd(proj_main, proj_low, gate_up_pad, gate_bias, norm_w, ycat, DS, DK, DV):
    L = proj_main.shape[0]
    nc = L // GLA_CHUNK
    cps = math.gcd(GLA_STEP_CHUNKS, nc)
    nh = DK // GLA_HK
    scale = GLA_HK ** -0.5

    def body(q_ref, k_ref, v_ref, z_ref, gl_ref, gu_ref, gb_ref, nw_ref, _yc_ref, y_ref, sp_ref, st_ref):
        n = pl.program_id(0)

        @pl.when(n == 0)
        def _():
            st_ref[...] = jnp.zeros_like(st_ref)

        pairs = [(sc, h) for sc in range(cps) for h in range(nh)]
        rows = lambda sc: slice(sc * GLA_CHUNK, (sc + 1) * GLA_CHUNK)
        kcol = lambda h: slice(h * GLA_HK, (h + 1) * GLA_HK)
        vcol = lambda h: slice(h * GLA_HV, (h + 1) * GLA_HV)
        gates = [_gla_gates(gl_ref[rows(sc), :], gu_ref, gb_ref) for sc in range(cps)]
        qe, dec, o_in, kv = {}, {}, {}, {}
        for sc, h in pairs:
            _, b, b_last, mask = gates[sc]
            bh, bl = b[:, kcol(h)], b_last[:, kcol(h)]
            qe[sc, h] = (q_ref[rows(sc), kcol(h)] * scale) * jnp.exp(bh)
            kh = k_ref[rows(sc), kcol(h)]
            vh = v_ref[rows(sc), vcol(h)]
            attn = jnp.where(mask, _dot(qe[sc, h], kh * jnp.exp(-bh), NT), 0.0)
            o_in[sc, h] = _dot(attn, vh)
            kv[sc, h] = _dot(vh, kh * jnp.exp(bl - bh), TN)
            dec[sc, h] = jnp.exp(bl)
        for sc, h in pairs:
            st = st_ref[h]
            sp_ref[sc, h] = st
            o = o_in[sc, h] + _dot(qe[sc, h], st, NT)
            st_ref[h] = dec[sc, h] * st + kv[sc, h]
            r = lax.rsqrt(jnp.mean(o * o, axis=-1, keepdims=True) + EPS)
            z = z_ref[rows(sc), vcol(h)]
            y_ref[rows(sc), vcol(h)] = (o * r * nw_ref[...] * (z * _sigmoid(z))).astype(BF16)

    c = cps * GLA_CHUNK
    return pl.pallas_call(
        body, name="gla_fwd", grid=(nc // cps,),
        in_specs=_gla_specs(DS, DK, DV, c, lambda n: n) + [
            pl.BlockSpec((c, LANES), lambda n: (n, 0)),
            pl.BlockSpec((LANES, DK), lambda n: (0, 0)),
            pl.BlockSpec((1, DK), lambda n: (0, 0)),
            pl.BlockSpec((1, GLA_HV), lambda n: (0, 0)),
            pl.BlockSpec(memory_space=pl.ANY),
        ],
        out_specs=[pl.BlockSpec((c, DV), lambda n: (n, DS // DV)),
                   pl.BlockSpec((cps, nh, GLA_HV, GLA_HK), lambda n: (n, 0, 0, 0))],
        input_output_aliases={8: 0},
        out_shape=[jax.ShapeDtypeStruct(ycat.shape, BF16), jax.ShapeDtypeStruct((nc, nh, GLA_HV, GLA_HK), F32)],
        scratch_shapes=[pltpu.VMEM((nh, GLA_HV, GLA_HK), F32)],
        compiler_params=pltpu.CompilerParams(dimension_semantics=("arbitrary",)),
    )(proj_main, proj_main, proj_main, proj_main, proj_low, gate_up_pad, gate_bias, norm_w, ycat)


def _gla_bwd(d_ycat, proj_main, proj_low, s_prev, gate_up_pad, gate_bias, norm_w, DS, DK, DV):
    L = proj_main.shape[0]
    nc = L // GLA_CHUNK
    cps = math.gcd(GLA_STEP_CHUNKS, nc)
    nh = DK // GLA_HK
    scale = GLA_HK ** -0.5

    def body(dy_ref, q_ref, k_ref, v_ref, z_ref, gl_ref, sp_ref, gu_ref, gb_ref, nw_ref,
             dg_ref, da_ref, gnw_ref, ggb_ref, dst_ref):
        n = pl.program_id(0)

        @pl.when(n == 0)
        def _():
            dst_ref[...] = jnp.zeros_like(dst_ref)
            gnw_ref[...] = jnp.zeros_like(gnw_ref)
            ggb_ref[...] = jnp.zeros_like(ggb_ref)

        last_row = lax.broadcasted_iota(jnp.int32, (GLA_CHUNK, GLA_HK), 0) == GLA_CHUNK - 1
        nw = nw_ref[...]
        for sc in reversed(range(cps)):
            rs = slice(sc * GLA_CHUNK, (sc + 1) * GLA_CHUNK)
            a, b, b_last, mask = _gla_gates(gl_ref[rs, :], gu_ref, gb_ref)
            for h in range(nh):
                ks = slice(h * GLA_HK, (h + 1) * GLA_HK)
                vs = slice(h * GLA_HV, (h + 1) * GLA_HV)
                bh, bl = b[:, ks], b_last[:, ks]
                e = jnp.exp(bh)
                einv = jnp.exp(-bh)
                etail = jnp.exp(bl - bh)
                dec = jnp.exp(bl)
                qe = (q_ref[rs, ks] * scale) * e
                kh = k_ref[rs, ks]
                ke = kh * einv
                ktail = kh * etail
                vh = v_ref[rs, vs]
                st = sp_ref[sc, h]
                dst = dst_ref[h]
                attn = jnp.where(mask, _dot(qe, ke, NT), 0.0)
                o = _dot(attn, vh) + _dot(qe, st, NT)
                r = lax.rsqrt(jnp.mean(o * o, axis=-1, keepdims=True) + EPS)
                nrm = o * r
                z = z_ref[rs, vs]
                sz = _sigmoid(z)
                dy = dy_ref[rs, vs]
                dg_ref[rs, 2 * DK + DV + h * GLA_HV:2 * DK + DV + (h + 1) * GLA_HV] = (
                    dy * nrm * nw * (sz * (1.0 + z * (1.0 - sz)))).astype(BF16)
                d_on = dy * (z * sz)
                gnw_ref[...] += jnp.sum(d_on * nrm, axis=0, keepdims=True)
                d_n = d_on * nw
                d_o = r * (d_n - nrm * jnp.mean(d_n * nrm, axis=-1, keepdims=True))
                d_attn = jnp.where(mask, _dot(d_o, vh, NT), 0.0)
                dg_ref[rs, 2 * DK + h * GLA_HV:2 * DK + (h + 1) * GLA_HV] = (
                    _dot(attn, d_o, TN) + _dot(ktail, dst, NT)).astype(BF16)
                d_qe = _dot(d_attn, ke) + _dot(d_o, st)
                d_ke = _dot(d_attn, qe, TN)
                d_kt = _dot(vh, dst)
                d_dec = jnp.sum(dst * st, axis=0, keepdims=True)
                dst_ref[h] = dec * dst + _dot(d_o, qe, TN)
                dg_ref[rs, ks] = (d_qe * scale * e).astype(BF16)
                dg_ref[rs, DK + h * GLA_HK:DK + (h + 1) * GLA_HK] = (d_ke * einv + d_kt * etail).astype(BF16)
                d_bl = jnp.sum(d_kt * ktail, axis=0, keepdims=True) + d_dec * dec
                d_b = d_qe * qe - d_ke * ke - d_kt * ktail + jnp.where(last_row, d_bl, 0.0)
                d_lg = _row_cumsum(d_b, reverse=True)
                d_a = d_lg * (1.0 / GLA_TAU) * _sigmoid(-a[:, ks])
                ggb_ref[:, ks] += jnp.sum(d_a, axis=0, keepdims=True)
                da_ref[rs, ks] = d_a.astype(BF16)

    c = cps * GLA_CHUNK
    ns = nc // cps
    rn = lambda n: ns - 1 - n
    return pl.pallas_call(
        body, name="gla_bwd", grid=(ns,),
        in_specs=[pl.BlockSpec((c, DV), lambda n: (rn(n), DS // DV))] + _gla_specs(DS, DK, DV, c, rn) + [
            pl.BlockSpec((c, LANES), lambda n: (rn(n), 0)),
            pl.BlockSpec((cps, nh, GLA_HV, GLA_HK), lambda n: (rn(n), 0, 0, 0)),
            pl.BlockSpec((LANES, DK), lambda n: (0, 0)),
            pl.BlockSpec((1, DK), lambda n: (0, 0)),
            pl.BlockSpec((1, GLA_HV), lambda n: (0, 0)),
        ],
        out_specs=[pl.BlockSpec((c, 2 * DK + 2 * DV), lambda n: (rn(n), 0)),
                   pl.BlockSpec((c, DK), lambda n: (rn(n), 0)),
                   pl.BlockSpec((1, GLA_HV), lambda n: (0, 0)), pl.BlockSpec((1, DK), lambda n: (0, 0))],
        out_shape=[jax.ShapeDtypeStruct((L, 2 * DK + 2 * DV), BF16),
                   jax.ShapeDtypeStruct((L, DK), BF16),
                   jax.ShapeDtypeStruct((1, GLA_HV), F32), jax.ShapeDtypeStruct((1, DK), F32)],
        scratch_shapes=[pltpu.VMEM((nh, GLA_HV, GLA_HK), F32)],
        compiler_params=pltpu.CompilerParams(dimension_semantics=("arbitrary",)),
    )(d_ycat, proj_main, proj_main, proj_main, proj_main, proj_low, s_prev, gate_up_pad, gate_bias, norm_w)


def _adamw_math(w, g, m, v):
    c1 = 1.0 - ADAM_B1 ** ADAM_STEP
    c2 = 1.0 - ADAM_B2 ** ADAM_STEP
    m_ = ADAM_B1 * m + (1.0 - ADAM_B1) * g
    v_ = ADAM_B2 * v + (1.0 - ADAM_B2) * (g * g)
    return -ADAM_LR * ((m_ / c1) / (jnp.sqrt(v_ / c2) + ADAM_EPS) + ADAM_WD * w), m_, v_


def _adamw_small(g_row, g_a, g_bc, ws, ms, vs):
    n = len(ws)
    nvec = n - 6

    def body(*refs):
        grow_ref, ga_ref, gbc_ref = refs[:3]
        w_refs, m_refs, v_refs = refs[3:3 + n], refs[3 + n:3 + 2 * n], refs[3 + 2 * n:3 + 3 * n]
        outs = refs[3 + 3 * n:]
        off = 0
        for i in range(n):
            if i < nvec:
                width = ws[i].shape[1]
                g = grow_ref[:, off:off + width]
                off += width
            elif i < nvec + 2:
                g = ga_ref[i - nvec]
            else:
                g = gbc_ref[i - nvec - 2]
            d, m_, v_ = _adamw_math(w_refs[i][...], g, m_refs[i][...], v_refs[i][...])
            outs[i][...] = g
            outs[n + i][...] = d
            outs[2 * n + i][...] = m_
            outs[3 * n + i][...] = v_

    vm = pl.BlockSpec(memory_space=pltpu.VMEM)
    outs = pl.pallas_call(
        body, name="adamw_small",
        in_specs=[vm] * (3 + 3 * n), out_specs=[vm] * (4 * n),
        out_shape=[jax.ShapeDtypeStruct(w.shape, F32) for w in ws] * 4,
    )(g_row, g_a, g_bc, *ws, *ms, *vs)
    return [outs[k * n:(k + 1) * n] for k in range(4)]


def _my_pos():
    return lax.axis_index("x"), lax.axis_index("y"), lax.axis_index("c")


def _split_start(name, srcs, lands_sd, make_copies, ncopies, after):
    n, m = len(srcs), len(lands_sd)

    def body(*refs):
        send_sems, recv_sems = refs[n + m + len(after)], refs[n + m + len(after) + 1]
        for cp in make_copies(refs[:n], refs[n:n + m], send_sems, recv_sems):
            cp.start()
        refs[-1][...] = jnp.zeros_like(refs[-1])

    hbm = pl.BlockSpec(memory_space=pltpu.HBM)
    sem = pl.BlockSpec(memory_space=pltpu.SEMAPHORE)
    outs = pl.pallas_call(
        body, name=name,
        in_specs=[hbm] * (n + m) + [pl.BlockSpec(memory_space=pl.ANY)] * len(after),
        out_specs=[sem, sem] + [hbm] * (n + m) + [pl.BlockSpec(memory_space=pltpu.VMEM)],
        out_shape=[pltpu.SemaphoreType.DMA((ncopies,)), pltpu.SemaphoreType.DMA((ncopies,))]
        + [pltpu.HBM(s.shape, s.dtype) for s in srcs] + [pltpu.HBM(s.shape, s.dtype) for s in lands_sd]
        + [jax.ShapeDtypeStruct((SUBLANES, LANES), F32)],
        input_output_aliases={i: 2 + i for i in range(n + m)},
        compiler_params=pltpu.CompilerParams(has_side_effects=pltpu.SideEffectType.DATAFLOW_SIDE_EFFECTING),
    )(*[pltpu.with_memory_space_constraint(s, pltpu.HBM) for s in srcs],
      *[pltpu.with_memory_space_constraint(lax.empty(s.shape, s.dtype), pltpu.HBM) for s in lands_sd], *after)
    return outs[0], outs[1], outs[2:2 + n], outs[2 + n:2 + n + m], outs[-1]


def _split_wait(name, send_sems, recv_sems, srcs, lands, make_copies, after):
    n, m = len(srcs), len(lands)

    def body(*refs):
        for cp in make_copies(refs[:n], refs[n:n + m], refs[n + m], refs[n + m + 1]):
            cp.wait_send()
            cp.wait_recv()

    hbm = pl.BlockSpec(memory_space=pltpu.HBM)
    sem = pl.BlockSpec(memory_space=pltpu.SEMAPHORE)
    outs = pl.pallas_call(
        body, name=name,
        in_specs=[hbm] * (n + m) + [sem, sem] + [pl.BlockSpec(memory_space=pl.ANY)] * len(after),
        out_specs=[hbm] * (n + m),
        out_shape=[pltpu.HBM(s.shape, s.dtype) for s in srcs] + [pltpu.HBM(p.shape, p.dtype) for p in lands],
        input_output_aliases={i: i for i in range(n + m)},
        compiler_params=pltpu.CompilerParams(has_side_effects=pltpu.SideEffectType.DATAFLOW_SIDE_EFFECTING),
    )(*srcs, *lands, send_sems, recv_sems, *after)
    return outs[:n], outs[n:]


def _late_gather_copies(srcs, lands, send_sems, recv_sems):
    x, y, c = _my_pos()
    me = 2 * x + y
    copies = []
    for d in (1, 2, 3):
        to = (x ^ (d >> 1), y ^ (d & 1), c)
        for a in range(len(srcs)):
            hrows = srcs[a].shape[0] // 2
            rows = pl.ds(c * hrows, hrows)
            copies.append(pltpu.make_async_remote_copy(
                src_ref=srcs[a].at[rows, :], dst_ref=lands[a].at[me, rows, :], send_sem=send_sems.at[3 * a + d - 1],
                recv_sem=recv_sems.at[3 * a + d - 1], device_id=to, device_id_type=MESH))
    return copies


def _late_gather_start(shards, after, name):
    n = len(shards)

    def body(*refs):
        srcs, lands = refs[:n], refs[n:2 * n]
        send_sems, recv_sems = refs[2 * n + 1], refs[2 * n + 2]
        token = refs[-1]
        for cp in _late_gather_copies(srcs, lands, send_sems, recv_sems):
            cp.start()
        token[...] = jnp.zeros_like(token)

    hbm = pl.BlockSpec(memory_space=pltpu.HBM)
    sem = pl.BlockSpec(memory_space=pltpu.SEMAPHORE)
    outs = pl.pallas_call(
        body, name=name,
        in_specs=[hbm] * (2 * n) + [pl.BlockSpec(memory_space=pl.ANY)],
        out_specs=[sem, sem] + [hbm] * (2 * n) + [pl.BlockSpec(memory_space=pltpu.VMEM)],
        out_shape=[pltpu.SemaphoreType.DMA((3 * n,)), pltpu.SemaphoreType.DMA((3 * n,))]
        + [pltpu.HBM(s.shape, s.dtype) for s in shards]
        + [pltpu.HBM((4,) + s.shape, s.dtype) for s in shards]
        + [jax.ShapeDtypeStruct((SUBLANES, LANES), F32)],
        input_output_aliases={i: 2 + i for i in range(2 * n)},
        compiler_params=pltpu.CompilerParams(has_side_effects=pltpu.SideEffectType.DATAFLOW_SIDE_EFFECTING),
    )(*[pltpu.with_memory_space_constraint(s, pltpu.HBM) for s in shards],
      *[pltpu.with_memory_space_constraint(lax.empty((4,) + s.shape, s.dtype), pltpu.HBM) for s in shards], after)
    return outs[0], outs[1], outs[2:2 + n], outs[2 + n:2 + 2 * n], outs[-1]


def _late_gather_wait(send_sems, recv_sems, shards, lands, after, name):
    n = len(shards)

    def body(*refs):
        src_refs, land_refs = refs[:n], refs[n:2 * n]
        ssem, rsem = refs[2 * n], refs[2 * n + 1]
        for cp in _late_gather_copies(src_refs, land_refs, ssem, rsem):
            cp.wait_send()
            cp.wait_recv()

    hbm = pl.BlockSpec(memory_space=pltpu.HBM)
    sem = pl.BlockSpec(memory_space=pltpu.SEMAPHORE)
    outs = pl.pallas_call(
        body, name=name,
        in_specs=[hbm] * (2 * n) + [sem, sem] + [pl.BlockSpec(memory_space=pl.ANY)] * len(after),
        out_specs=[hbm] * (2 * n),
        out_shape=[pltpu.HBM(s.shape, s.dtype) for s in shards] + [pltpu.HBM(p.shape, p.dtype) for p in lands],
        input_output_aliases={i: i for i in range(2 * n)},
        compiler_params=pltpu.CompilerParams(has_side_effects=pltpu.SideEffectType.DATAFLOW_SIDE_EFFECTING),
    )(*shards, *lands, send_sems, recv_sems, *after)
    return outs[n:]


def _late_gather_pair(lands, name):
    n = len(lands)

    def body(*refs):
        outs = refs[n:2 * n]
        send_sems, recv_sems = refs[2 * n:]
        x, y, c = _my_pos()

        def copy(a, d, half):
            chip = 2 * (x ^ (d >> 1)) + (y ^ (d & 1))
            hrows = lands[a].shape[1] // 2
            sl = outs[a].at[chip, pl.ds(half * hrows, hrows), :]
            return pltpu.make_async_remote_copy(src_ref=sl, dst_ref=sl, send_sem=send_sems.at[3 * a + d - 1],
                                                recv_sem=recv_sems.at[3 * a + d - 1], device_id=(x, y, 1 - c),
                                                device_id_type=MESH)

        pairs = [(a, d) for d in (1, 2, 3) for a in range(n)]
        for a, d in pairs:
            copy(a, d, c).start()
        for a, d in pairs:
            copy(a, d, c).wait_send()
            copy(a, d, 1 - c).wait_recv()

    hbm = pl.BlockSpec(memory_space=pltpu.HBM)
    return pl.pallas_call(
        body, name=name, in_specs=[hbm] * n, out_specs=[hbm] * n,
        out_shape=[jax.ShapeDtypeStruct(p.shape, p.dtype) for p in lands],
        input_output_aliases={i: i for i in range(n)},
        scratch_shapes=[pltpu.SemaphoreType.DMA((3 * n,)), pltpu.SemaphoreType.DMA((3 * n,))],
    )(*lands)


def _pair_exchange(gs):
    n = len(gs)

    def body(*refs):
        ins, outs = refs[:n], refs[n:2 * n]
        send_sems, recv_sems = refs[2 * n:]
        x, y, c = _my_pos()
        sent = []
        for a in range(n):
            hrows = gs[a].shape[1] // 2
            cp = pltpu.make_async_remote_copy(
                src_ref=ins[a].at[:, pl.ds((1 - c) * hrows, hrows), :], dst_ref=outs[a], send_sem=send_sems.at[a],
                recv_sem=recv_sems.at[a], device_id=(x, y, 1 - c), device_id_type=MESH)
            cp.start()
            sent.append(cp)
        for cp in sent:
            cp.wait()

    hbm = pl.BlockSpec(memory_space=pltpu.HBM)
    return pl.pallas_call(
        body, name="grad_pair_exchange", in_specs=[hbm] * n, out_specs=[hbm] * n,
        out_shape=[jax.ShapeDtypeStruct((g.shape[0], g.shape[1] // 2, g.shape[2]), g.dtype) for g in gs],
        scratch_shapes=[pltpu.SemaphoreType.DMA((n,)), pltpu.SemaphoreType.DMA((n,))],
    )(*gs)


def _pair_add(g, got, c_arr, name):
    nk, rows2, cols = g.shape
    hrows = rows2 // 2
    tr = _blk(hrows, 256, 2 * SUBLANES)
    nb = hrows // tr

    def body(c_ref, a_ref, b_ref, o_ref):
        o_ref[...] = (a_ref[...].astype(F32) + b_ref[...].astype(F32)).astype(o_ref.dtype)

    return pl.pallas_call(
        body, name=name,
        grid_spec=pltpu.PrefetchScalarGridSpec(
            num_scalar_prefetch=1, grid=(nk, nb),
            in_specs=[pl.BlockSpec((1, tr, cols), lambda k, i, c_ref: (k, c_ref[0] * nb + i, 0)),
                      pl.BlockSpec((1, tr, cols), lambda k, i, c_ref: (k, i, 0))],
            out_specs=pl.BlockSpec((1, tr, cols), lambda k, i, c_ref: (k, i, 0))),
        out_shape=jax.ShapeDtypeStruct((nk, hrows, cols), g.dtype),
        compiler_params=pltpu.CompilerParams(dimension_semantics=("parallel", "parallel")),
    )(c_arr, g, got)


def _chip_scatter_copies(srcs, lands, send_sems, recv_sems):
    x, y, c = _my_pos()
    copies = []
    for d in (1, 2, 3):
        tx, ty = x ^ (d >> 1), y ^ (d & 1)
        for a in range(len(srcs)):
            copies.append(pltpu.make_async_remote_copy(
                src_ref=srcs[a].at[2 * tx + ty], dst_ref=lands[a].at[d - 1], send_sem=send_sems.at[3 * a + d - 1],
                recv_sem=recv_sems.at[3 * a + d - 1], device_id=(tx, ty, c), device_id_type=MESH))
    return copies


def _chip_scatter_start(pss):
    n = len(pss)

    def body(*refs):
        srcs, lands = refs[:n], refs[n:2 * n]
        send_sems, recv_sems = refs[2 * n], refs[2 * n + 1]
        token = refs[-1]
        for cp in _chip_scatter_copies(srcs, lands, send_sems, recv_sems):
            cp.start()
        token[...] = jnp.zeros_like(token)

    hbm = pl.BlockSpec(memory_space=pltpu.HBM)
    sem = pl.BlockSpec(memory_space=pltpu.SEMAPHORE)
    land_shapes = [(3,) + p.shape[1:] for p in pss]
    outs = pl.pallas_call(
        body, name="grad_chip_scatter_start",
        in_specs=[hbm] * (2 * n),
        out_specs=[sem, sem] + [hbm] * (2 * n) + [pl.BlockSpec(memory_space=pltpu.VMEM)],
        out_shape=[pltpu.SemaphoreType.DMA((3 * n,)), pltpu.SemaphoreType.DMA((3 * n,))]
        + [pltpu.HBM(p.shape, p.dtype) for p in pss]
        + [pltpu.HBM(s, p.dtype) for s, p in zip(land_shapes, pss)]
        + [jax.ShapeDtypeStruct((SUBLANES, LANES), F32)],
        input_output_aliases={i: 2 + i for i in range(2 * n)},
        compiler_params=pltpu.CompilerParams(has_side_effects=pltpu.SideEffectType.DATAFLOW_SIDE_EFFECTING),
    )(*[pltpu.with_memory_space_constraint(p, pltpu.HBM) for p in pss],
      *[pltpu.with_memory_space_constraint(lax.empty(s, p.dtype), pltpu.HBM) for s, p in zip(land_shapes, pss)])
    return outs[0], outs[1], outs[2:2 + n], outs[2 + n:2 + 2 * n], outs[-1]


def _chip_scatter_wait(send_sems, recv_sems, srcs, lands, after):
    n = len(srcs)

    def body(*refs):
        src_refs, land_refs = refs[:n], refs[n:2 * n]
        ssem, rsem = refs[2 * n], refs[2 * n + 1]
        for cp in _chip_scatter_copies(src_refs, land_refs, ssem, rsem):
            cp.wait_send()
            cp.wait_recv()

    hbm = pl.BlockSpec(memory_space=pltpu.HBM)
    sem = pl.BlockSpec(memory_space=pltpu.SEMAPHORE)
    outs = pl.pallas_call(
        body, name="grad_chip_scatter_wait",
        in_specs=[hbm] * (2 * n) + [sem, sem, pl.BlockSpec(memory_space=pl.ANY)],
        out_specs=[hbm] * (2 * n),
        out_shape=[pltpu.HBM(p.shape, p.dtype) for p in srcs] + [pltpu.HBM(p.shape, p.dtype) for p in lands],
        input_output_aliases={i: i for i in range(2 * n)},
        compiler_params=pltpu.CompilerParams(has_side_effects=pltpu.SideEffectType.DATAFLOW_SIDE_EFFECTING),
    )(*srcs, *lands, send_sems, recv_sems, after)
    return outs[:n], outs[n:]


def _chip_sum(ps, got, me_arr, name):
    _, hrows, cols = ps.shape
    tr = _blk(hrows, 256, 2 * SUBLANES)

    def body(me_ref, p_ref, g_ref, o_ref):
        acc = p_ref[0].astype(F32)
        for s in range(3):
            acc = acc + g_ref[s].astype(F32)
        o_ref[...] = acc

    return pl.pallas_call(
        body, name=name,
        grid_spec=pltpu.PrefetchScalarGridSpec(
            num_scalar_prefetch=1, grid=(hrows // tr,),
            in_specs=[pl.BlockSpec((1, tr, cols), lambda i, me_ref: (me_ref[0], i, 0)),
                      pl.BlockSpec((3, tr, cols), lambda i, me_ref: (0, i, 0))],
            out_specs=pl.BlockSpec((tr, cols), lambda i, me_ref: (i, 0))),
        out_shape=jax.ShapeDtypeStruct((hrows, cols), F32),
        compiler_params=pltpu.CompilerParams(dimension_semantics=("parallel",)),
    )(me_arr, ps, got)


def _pair_swap(halves):
    n = len(halves)

    def body(*refs):
        ins, outs = refs[:n], refs[n:2 * n]
        send_sems, recv_sems = refs[2 * n:]
        x, y, c = _my_pos()
        sent = []
        for a in range(n):
            cp = pltpu.make_async_remote_copy(src_ref=ins[a], dst_ref=outs[a], send_sem=send_sems.at[a], recv_sem=recv_sems.at[a],
                                              device_id=(x, y, 1 - c), device_id_type=MESH)
            cp.start()
            sent.append(cp)
        for cp in sent:
            cp.wait()

    hbm = pl.BlockSpec(memory_space=pltpu.HBM)
    return pl.pallas_call(
        body, name="grad_pair_swap", in_specs=[hbm] * n, out_specs=[hbm] * n,
        out_shape=[jax.ShapeDtypeStruct(h.shape, h.dtype) for h in halves],
        scratch_shapes=[pltpu.SemaphoreType.DMA((n,)), pltpu.SemaphoreType.DMA((n,))],
    )(*halves)


def _adamw_sharded(w, g_own, g_other, m, v, c_arr, after, name):
    R, C = w.shape
    hrows = R // 2
    tr = _blk(hrows, 256, SUBLANES)
    nbh = hrows // tr

    def body(c_ref, w_ref, go_ref, gx_ref, m_ref, v_ref, _after_ref, g_ref, d_ref, nm_ref, nv_ref):
        mine = (pl.program_id(0) // nbh) == c_ref[0]
        g_ = jnp.where(mine, go_ref[...], gx_ref[...])
        g_ref[...] = g_
        d_ref[...], nm_ref[...], nv_ref[...] = _adamw_math(w_ref[...], g_, m_ref[...], v_ref[...])

    blk = pl.BlockSpec((tr, C), lambda i, c_ref: (i, 0))
    hblk = pl.BlockSpec((tr, C), lambda i, c_ref: (i % nbh, 0))
    sd = jax.ShapeDtypeStruct((R, C), F32)
    return pl.pallas_call(
        body, name=name,
        grid_spec=pltpu.PrefetchScalarGridSpec(
            num_scalar_prefetch=1, grid=(2 * nbh,),
            in_specs=[blk, hblk, hblk, blk, blk, pl.BlockSpec(memory_space=pl.ANY)], out_specs=[blk] * 4),
        out_shape=[sd] * 4,
        compiler_params=pltpu.CompilerParams(dimension_semantics=("parallel",)),
    )(c_arr, w, g_own, g_other, m, v, after)


def _ar_piece(ref, rows, p):
    start = p * rows
    if rows % SUBLANES == 0:
        start = pl.multiple_of(start, SUBLANES)
    return ref.at[..., pl.ds(start, rows), :]


def _ar_peer(d):
    x, y, c = _my_pos()
    return (x ^ (d >> 2), y ^ ((d >> 1) & 1), c ^ (d & 1))


def _ar_lin(p):
    return 4 * p[0] + 2 * p[1] + p[2]


def _ar_scatter_copies(rows):
    def make(srcs, lands, send_sems, recv_sems):
        n = len(srcs)
        copies = []
        for d in range(1, 8):
            to = _ar_peer(d)
            for a in range(n):
                copies.append(pltpu.make_async_remote_copy(
                    src_ref=_ar_piece(srcs[a], rows[a], _ar_lin(to)), dst_ref=lands[a].at[d],
                    send_sem=send_sems.at[(d - 1) * n + a], recv_sem=recv_sems.at[(d - 1) * n + a], device_id=to,
                    device_id_type=MESH))
        return copies
    return make


def _ar_gather_copies(rows):
    def make(srcs, lands, send_sems, recv_sems):
        n = len(srcs)
        me = _ar_lin(_my_pos())
        copies = []
        for d in range(1, 8):
            for a in range(n):
                copies.append(pltpu.make_async_remote_copy(
                    src_ref=srcs[a], dst_ref=_ar_piece(lands[a], rows[a], me),
                    send_sem=send_sems.at[(d - 1) * n + a], recv_sem=recv_sems.at[(d - 1) * n + a], device_id=_ar_peer(d),
                    device_id_type=MESH))
        return copies
    return make


def _ar_sum(srcs, lands, rows):
    n = len(srcs)

    def body(*refs):
        me = _ar_lin(_my_pos())
        for a in range(n):
            acc = _ar_piece(refs[a], rows[a], me)[...]
            for d in range(1, 8):
                acc = acc + refs[n + a][d]
            refs[2 * n + a][...] = acc

    vm = pl.BlockSpec(memory_space=pltpu.VMEM)
    return pl.pallas_call(
        body, name="allreduce_sum", in_specs=[vm] * (2 * n), out_specs=[vm] * n,
        out_shape=[jax.ShapeDtypeStruct(p.shape[1:], F32) for p in lands],
    )(*srcs, *lands)


def kernel(x, pre_norm_w, w_in, s5_A_re, s5_A_im, s5_B_re, s5_B_im, s5_C_re, s5_C_im, s5_D, s5_log_dt, s5_glu_w, s5_glu_b, gla_gate_up, gla_gate_bias, gla_norm_w, w_out, post_norm_w, loss_target, m_pre_norm_w, m_w_in, m_s5_A_re, m_s5_A_im, m_s5_B_re, m_s5_B_im, m_s5_C_re, m_s5_C_im, m_s5_D, m_s5_log_dt, m_s5_glu_w, m_s5_glu_b, m_gla_gate_up, m_gla_gate_bias, m_gla_norm_w, m_w_out, m_post_norm_w, v_pre_norm_w, v_w_in, v_s5_A_re, v_s5_A_im, v_s5_B_re, v_s5_B_im, v_s5_C_re, v_s5_C_im, v_s5_D, v_s5_log_dt, v_s5_glu_w, v_s5_glu_b, v_gla_gate_up, v_gla_gate_bias, v_gla_norm_w, v_w_out, v_post_norm_w):
    names = ["pre_norm_w", "w_in", "s5_A_re", "s5_A_im", "s5_B_re", "s5_B_im", "s5_C_re", "s5_C_im", "s5_D", "s5_log_dt",
             "s5_glu_w", "s5_glu_b", "gla_gate_up", "gla_gate_bias", "gla_norm_w", "w_out", "post_norm_w"]
    W = dict(zip(names, (pre_norm_w, w_in, s5_A_re, s5_A_im, s5_B_re, s5_B_im, s5_C_re, s5_C_im, s5_D, s5_log_dt,
                         s5_glu_w, s5_glu_b, gla_gate_up, gla_gate_bias, gla_norm_w, w_out, post_norm_w)))
    M = dict(zip(names, (m_pre_norm_w, m_w_in, m_s5_A_re, m_s5_A_im, m_s5_B_re, m_s5_B_im, m_s5_C_re, m_s5_C_im, m_s5_D,
                         m_s5_log_dt, m_s5_glu_w, m_s5_glu_b, m_gla_gate_up, m_gla_gate_bias, m_gla_norm_w, m_w_out,
                         m_post_norm_w)))
    V = dict(zip(names, (v_pre_norm_w, v_w_in, v_s5_A_re, v_s5_A_im, v_s5_B_re, v_s5_B_im, v_s5_C_re, v_s5_C_im, v_s5_D,
                         v_s5_log_dt, v_s5_glu_w, v_s5_glu_b, v_gla_gate_up, v_gla_gate_bias, v_gla_norm_w, v_w_out,
                         v_post_norm_w)))
    sharded = ("w_in", "s5_glu_w", "w_out", "gla_gate_up")

    xb = x[0]
    tgt = loss_target[0]
    L, D = xb.shape
    DS = D // 2
    G = DS // S5_GROUP
    P = S5_STATE
    NB = DS // S5_COLS
    DV = D - DS
    DK = DV // 2
    WM = 2 * DS + 2 * DK + 2 * DV
    nsh = w_in.shape[2]

    chip = 2 * lax.axis_index("x") + lax.axis_index("y")
    own = [jnp.transpose(lax.optimization_barrier(jnp.transpose(w_in[0]).astype(BF16))), s5_glu_w[0].astype(BF16),
           w_out[0].astype(BF16), gla_gate_up[0]]
    fill = lambda g, o: lax.dynamic_update_index_in_dim(g, o, chip, 0)
    win_ss, win_rs, win_src, win_lands, win_token = _late_gather_start(own[:1], pre_norm_w, "w_in_gather_start")
    h = _prenorm_fwd(xb, pre_norm_w, win_token)

    b_view = lambda t: jnp.transpose(t[0], (0, 2, 1)).reshape(G * S5_GROUP, P)
    b_back = lambda t: jnp.transpose(t.reshape(G, S5_GROUP, P), (0, 2, 1))[None]
    c_view = lambda t: t[0].reshape(G * S5_GROUP, P)
    c_back = lambda t: t.reshape(1, G, S5_GROUP, P)
    small = ["pre_norm_w", "post_norm_w", "s5_D", "s5_glu_b", "gla_gate_bias", "gla_norm_w", "s5_log_dt",
             "s5_A_re", "s5_A_im", "s5_B_re", "s5_B_im", "s5_C_re", "s5_C_im"]
    view = {n: (lambda t: t) for n in small[:7]}
    back = dict(view)
    view.update(s5_A_re=lambda t: t[0], s5_A_im=lambda t: t[0], s5_B_re=b_view, s5_B_im=b_view, s5_C_re=c_view, s5_C_im=c_view)
    back.update(s5_A_re=lambda t: t[None], s5_A_im=lambda t: t[None], s5_B_re=b_back, s5_B_im=b_back, s5_C_re=c_back,
                s5_C_im=c_back)
    Wv = {n: view[n](W[n]) for n in small}
    bbd_re, bbd_im, ct_re, ct_im, tab, ptab = _s5_prep_fwd(
        Wv["s5_A_re"], Wv["s5_A_im"], s5_log_dt, Wv["s5_B_re"], Wv["s5_B_im"], Wv["s5_C_re"], Wv["s5_C_im"],
        h, _blk(L, 512, SUBLANES) // SUBLANES)
    dvec = s5_D

    for d_ in (W, M, V):
        d_["w_in"], _ = lax.optimization_barrier((d_["w_in"], win_token))
    g_win = _late_gather_wait(win_ss, win_rs, win_src, win_lands,
                              [tab, W["w_in"][0], M["w_in"][0], V["w_in"][0]], "w_in_gather_wait")
    g_win = fill(_late_gather_pair(g_win, "w_in_gather_pair")[0], own[0])
    last = WM - 3 * nsh
    w_main = jnp.concatenate([g_win[0], g_win[1], g_win[2], g_win[3][:, :last]], axis=1)
    w_low = jnp.pad(g_win[3][:, last:], ((0, 0), (0, LANES - GLA_RANK)))
    late_ss, late_rs, late_src, late_lands, late_token = _late_gather_start(own[1:], g_win, "late_gather_start")
    proj_main, proj_low = _in_proj(h, w_main, w_low, late_token)
    y_pre, s_re, s_im = _s5_scan_fwd(proj_main, bbd_re, bbd_im, ct_re, ct_im, dvec, tab, ptab, DS)
    late = _late_gather_wait(late_ss, late_rs, late_src, late_lands, [y_pre], "late_gather_wait")
    late = _late_gather_pair(late, "late_gather_pair")
    g_glu, g_wout, g_gup = [fill(g, o) for g, o in zip(late, own[1:])]
    glu_w = g_glu.reshape(DS, DS)
    wout = g_wout.reshape(D, D)
    gup = jnp.moveaxis(g_gup, 0, 1).reshape(GLA_RANK, DK)
    gup_pad = jnp.pad(gup, ((0, LANES - GLA_RANK), (0, 0))).astype(BF16)
    ycat, t_pre = _s5_post_fwd(y_pre, proj_main, glu_w, s5_glu_b, DS)
    ycat, s_prev = _gla_fwd(proj_main, proj_low, gup_pad, gla_gate_bias, gla_norm_w, ycat, DS, DK, DV)
    mixed = _mm(ycat, wout, name="out_proj")
    loss11, d_mixed, dout, g_post_w = _post_fwd_bwd(mixed, xb, tgt, post_norm_w)

    d_ycat = _mm(d_mixed, wout, tb=True, name="out_proj_dx")
    g_wout_full = _mm(ycat, d_mixed, ta=True, out_dtype=BF16, name="out_proj_dw")
    d_ypre, d_s5, d_t, y1, g_glu_b = _s5_post_bwd(d_ycat, y_pre, proj_main, t_pre, glu_w, DS)
    g_glu_full = _mm(y1, d_t, ta=True, out_dtype=BF16, name="glu_dw")
    d_s5, g_D, gct_re, gct_im, gbbd_re, gbbd_im, gab_re, gab_im = _s5_scan_bwd(
        d_ypre, proj_main, s_re, s_im, bbd_re, bbd_im, ct_re, ct_im, dvec, tab, ptab, d_s5, DS)
    d_gla, d_a, g_norm_w, g_gate_bias = _gla_bwd(
        d_ycat, proj_main, proj_low, s_prev, gup_pad, gla_gate_bias, gla_norm_w, DS, DK, DV)
    d_low = _mm(d_a, gup_pad, tb=True, out_dtype=BF16, name="gate_dx")
    g_gup_pad = _mm(proj_low, d_a, ta=True, name="gate_dw")
    g_wmain, g_wlow = _in_proj_dw(h, d_s5, d_gla, d_low)

    g_win_sh = jnp.stack([g_wmain[:, :nsh], g_wmain[:, nsh:2 * nsh], g_wmain[:, 2 * nsh:3 * nsh],
                          jnp.concatenate([g_wmain[:, 3 * nsh:], g_wlow[:, :GLA_RANK]], axis=1)])
    gs = [g_win_sh,
          g_glu_full.reshape(4, DS // 4, DS),
          g_wout_full.reshape(4, D // 4, D),
          jnp.moveaxis(g_gup_pad[:GLA_RANK].reshape(GLA_RANK, 4, DK // 4), 1, 0)]
    c_arr = lax.axis_index("c").astype(jnp.int32).reshape(1)
    me_arr = chip.astype(jnp.int32).reshape(1)
    got = _pair_exchange(gs)
    pss = [_pair_add(g, r, c_arr, "grad_pair_add_" + n) for n, g, r in zip(sharded, gs, got)]
    send_sems, recv_sems, pss, lands, token = _chip_scatter_start(pss)

    dh = _in_proj_dx(d_s5, d_gla, d_low, w_main, w_low, token)
    grad_x, g_pre_w = _prenorm_bwd(xb, dh, dout, pre_norm_w)

    g_a, g_bc, g_ldt = _s5_prep_bwd(Wv["s5_A_re"], Wv["s5_A_im"], s5_log_dt, Wv["s5_B_re"], Wv["s5_B_im"],
                                    gbbd_re, gbbd_im, gct_re, gct_im, gab_re, gab_im)

    g_vecs = jnp.concatenate([g_pre_w, g_post_w, g_D, g_glu_b, g_gate_bias, g_norm_w, g_ldt, loss11], axis=1)
    loss_at = g_vecs.shape[1] - 1
    lanes_pad = -g_vecs.shape[1] % (8 * SUBLANES * LANES)
    g_vecs = jnp.pad(g_vecs, ((0, 0), (0, lanes_pad))).reshape(-1, LANES)
    ar_srcs = [g_vecs, g_a, g_bc]
    ar_rows = [a.shape[-2] // 8 for a in ar_srcs]
    ar_lands = [jax.ShapeDtypeStruct((8,) + a.shape[:-2] + (r, a.shape[-1]), F32) for a, r in zip(ar_srcs, ar_rows)]
    ar_ss, ar_rs, ar_srcs, ar_got, ar_token = _split_start(
        "allreduce_scatter_start", ar_srcs, ar_lands, _ar_scatter_copies(ar_rows), 7 * len(ar_srcs), [])

    pss, rcv = _chip_scatter_wait(send_sems, recv_sems, pss, lands, ar_token)
    halves = [_chip_sum(p, r, me_arr, "grad_chip_sum_" + n) for n, p, r in zip(sharded, pss, rcv)]
    others = _pair_swap(halves)
    ar_srcs, ar_got = _split_wait("allreduce_scatter_wait", ar_ss, ar_rs, ar_srcs, ar_got, _ar_scatter_copies(ar_rows),
                                  [others[0]])
    ar_red = _ar_sum(ar_srcs, ar_got, ar_rows)
    ag_ss, ag_rs, ar_red, ag_full, ag_token = _split_start(
        "allreduce_gather_start", ar_red, [jax.ShapeDtypeStruct(a.shape, F32) for a in ar_srcs],
        _ar_gather_copies(ar_rows), 7 * len(ar_red), [])
    G_out, D_out, M_out, V_out = {}, {}, {}, {}
    for n, g_own, g_other in zip(sharded, halves, others):
        g_, d_, m_, v_ = _adamw_sharded(W[n][0], g_own, g_other, M[n][0], V[n][0], c_arr, ag_token, "adamw_" + n)
        G_out[n], D_out[n], M_out[n], V_out[n] = g_[None], d_[None], m_[None], v_[None]
    ar_red, ag_full = _split_wait("allreduce_gather_wait", ag_ss, ag_rs, ar_red, ag_full, _ar_gather_copies(ar_rows),
                                  [D_out[n] for n in sharded])
    me8 = 2 * chip + lax.axis_index("c")
    r_vecs, r_a, r_bc = [lax.dynamic_update_slice_in_dim(f, r, me8 * rw, axis=f.ndim - 2)
                         for f, r, rw in zip(ag_full, ar_red, ar_rows)]
    r_vecs = r_vecs.reshape(1, -1)
    loss = r_vecs[0, loss_at]
    outs4 = _adamw_small(r_vecs, r_a, r_bc, [Wv[n] for n in small],
                         [view[n](M[n]) for n in small], [view[n](V[n]) for n in small])
    for store, o in zip((G_out, D_out, M_out, V_out), outs4):
        store.update({n: back[n](t) for n, t in zip(small, o)})

    return (loss, grad_x[None], *[G_out[n] for n in names], *[D_out[n] for n in names],
            *[M_out[n] for n in names], *[V_out[n] for n in names])
```

```python
import functools
import math

import jax
import jax.numpy as jnp
from jax import lax
from jax.experimental import pallas as pl
from jax.experimental.pallas import tpu as pltpu

F32 = jnp.float32
BF16 = jnp.bfloat16
HI = lax.Precision.HIGHEST
MESH = pl.DeviceIdType.MESH

EPS = 1e-6
S5_GROUP = 16
S5_STATE = 64
GLA_HK = 128
GLA_HV = 256
GLA_RANK = 16
GLA_TAU = 16.0
GLA_CHUNK = 64
GLA_STEP_CHUNKS = 4
LANES = 128
SUBLANES = 8
S5_COLS = 128
S5_LANES = (S5_COLS // S5_GROUP) * S5_STATE

ADAM_LR = 0.001
ADAM_B1 = 0.9
ADAM_B2 = 0.999
ADAM_EPS = 1e-08
ADAM_WD = 0.01
ADAM_STEP = 10

GELU_K = math.sqrt(2.0 / math.pi)
GELU_C = 0.044715


def _blk(n, pref, unit=LANES):
    best = None
    b = unit
    while b <= min(n, pref):
        if n % b == 0:
            best = b
        b += unit
    return best if best is not None else n


def _dot(a, b, dn=(((1,), (0,)), ((), ()))):
    return lax.dot_general(a.astype(BF16), b.astype(BF16), dn, preferred_element_type=F32)


def _dot_hi(a, b, dn=(((1,), (0,)), ((), ()))):
    return lax.dot_general(a, b, dn, precision=HI, preferred_element_type=F32)


NN = (((1,), (0,)), ((), ()))
NT = (((1,), (1,)), ((), ()))
TN = (((0,), (0,)), ((), ()))


def _sigmoid(x):
    return 1.0 / (1.0 + jnp.exp(-x))


def _gelu(y):
    return 0.5 * y * (1.0 + jnp.tanh(GELU_K * (y + GELU_C * y * y * y)))


def _gelu_grad(y):
    th = jnp.tanh(GELU_K * (y + GELU_C * y * y * y))
    return 0.5 * (1.0 + th) + 0.5 * y * (1.0 - th * th) * GELU_K * (1.0 + 3.0 * GELU_C * y * y)


def _mm(a, b, *, name, ta=False, tb=False, out_dtype=F32, bm=1024, bn=1024, bk=2048):
    if ta:
        K, M = a.shape
    else:
        M, K = a.shape
    if tb:
        N, K2 = b.shape
    else:
        K2, N = b.shape
    assert K == K2, (a.shape, b.shape, ta, tb)
    bm, bn, bk = _blk(M, bm), _blk(N, bn), _blk(K, bk)
    nk = K // bk
    dn = (((0 if ta else 1,), (1 if tb else 0,)), ((), ()))

    def body(a_ref, b_ref, o_ref, *acc):
        if nk == 1:
            o_ref[...] = _dot(a_ref[...], b_ref[...], dn).astype(out_dtype)
            return
        acc_ref, = acc
        k = pl.program_id(2)

        @pl.when(k == 0)
        def _():
            acc_ref[...] = jnp.zeros_like(acc_ref)

        acc_ref[...] += _dot(a_ref[...], b_ref[...], dn)

        @pl.when(k == nk - 1)
        def _():
            o_ref[...] = acc_ref[...].astype(out_dtype)

    a_spec = pl.BlockSpec((bk, bm), lambda i, j, k: (k, i)) if ta else pl.BlockSpec((bm, bk), lambda i, j, k: (i, k))
    b_spec = pl.BlockSpec((bn, bk), lambda i, j, k: (j, k)) if tb else pl.BlockSpec((bk, bn), lambda i, j, k: (k, j))
    return pl.pallas_call(
        body,
        name=name,
        grid=(M // bm, N // bn, nk),
        in_specs=[a_spec, b_spec],
        out_specs=pl.BlockSpec((bm, bn), lambda i, j, k: (i, j)),
        out_shape=jax.ShapeDtypeStruct((M, N), out_dtype),
        scratch_shapes=[pltpu.VMEM((bm, bn), F32)] if nk > 1 else [],
        compiler_params=pltpu.CompilerParams(dimension_semantics=("parallel", "parallel", "arbitrary")),
    )(a, b)


def _in_proj(h, w_main, w_low, after):
    M, K = h.shape
    N = w_main.shape[1]
    bm, bn = _blk(M, 1024), _blk(N, 1024)

    def body(h_ref, w_ref, wl_ref, _after_ref, o_ref, ol_ref):
        hv = h_ref[...]
        o_ref[...] = _dot(hv, w_ref[...])

        @pl.when(pl.program_id(1) == 0)
        def _():
            ol_ref[...] = _dot(hv, wl_ref[...])

    return pl.pallas_call(
        body, name="in_proj", grid=(M // bm, N // bn),
        in_specs=[pl.BlockSpec((bm, K), lambda i, j: (i, 0)), pl.BlockSpec((K, bn), lambda i, j: (0, j)),
                  pl.BlockSpec((K, LANES), lambda i, j: (0, 0)), pl.BlockSpec(memory_space=pl.ANY)],
        out_specs=[pl.BlockSpec((bm, bn), lambda i, j: (i, j)), pl.BlockSpec((bm, LANES), lambda i, j: (i, 0))],
        out_shape=[jax.ShapeDtypeStruct((M, N), F32), jax.ShapeDtypeStruct((M, LANES), F32)],
        compiler_params=pltpu.CompilerParams(dimension_semantics=("parallel", "arbitrary")),
    )(h, w_main, w_low, after)


def _in_proj_dx(a1, a2, al, b, bl, after, *, bm=1024, bn=1024, bk=2048):
    M, K1 = a1.shape
    K2 = a2.shape[1]
    N = b.shape[0]
    bm, bn = _blk(M, bm), _blk(N, bn)
    bk = _blk(math.gcd(K1, K2), bk)
    nk1, nk = K1 // bk, (K1 + K2) // bk

    def body(a1_ref, a2_ref, al_ref, b_ref, bl_ref, _after_ref, o_ref, acc_ref):
        k = pl.program_id(2)

        @pl.when(k == 0)
        def _():
            acc_ref[...] = _dot(al_ref[...], bl_ref[...], NT)

        @pl.when(k < nk1)
        def _():
            acc_ref[...] += _dot(a1_ref[...], b_ref[...], NT)

        @pl.when(k >= nk1)
        def _():
            acc_ref[...] += _dot(a2_ref[...], b_ref[...], NT)

        @pl.when(k == nk - 1)
        def _():
            o_ref[...] = acc_ref[...]

    return pl.pallas_call(
        body, name="in_proj_dx", grid=(M // bm, N // bn, nk),
        in_specs=[pl.BlockSpec((bm, bk), lambda i, j, k: (i, jnp.minimum(k, nk1 - 1))),
                  pl.BlockSpec((bm, bk), lambda i, j, k: (i, jnp.maximum(k - nk1, 0))),
                  pl.BlockSpec((bm, LANES), lambda i, j, k: (i, 0)),
                  pl.BlockSpec((bn, bk), lambda i, j, k: (j, k)),
                  pl.BlockSpec((bn, LANES), lambda i, j, k: (j, 0)),
                  pl.BlockSpec(memory_space=pl.ANY)],
        out_specs=pl.BlockSpec((bm, bn), lambda i, j, k: (i, j)),
        out_shape=jax.ShapeDtypeStruct((M, N), F32),
        scratch_shapes=[pltpu.VMEM((bm, bn), F32)],
        compiler_params=pltpu.CompilerParams(dimension_semantics=("parallel", "parallel", "arbitrary")),
    )(a1, a2, al, b, bl, after)


def _in_proj_dw(a, b1, b2, bl, *, bm=1024, bn=1024, bk=2048):
    K, M = a.shape
    N1, N2 = b1.shape[1], b2.shape[1]
    bm, bk = _blk(M, bm), _blk(K, bk)
    bn = _blk(math.gcd(N1, N2), bn)
    nj1, nj = N1 // bn, (N1 + N2) // bn
    nk = K // bk

    def body(a_ref, b1_ref, b2_ref, bl_ref, o_ref, ol_ref, acc_ref, accl_ref):
        j = pl.program_id(1)
        k = pl.program_id(2)

        @pl.when(k == 0)
        def _():
            acc_ref[...] = jnp.zeros_like(acc_ref)

        @pl.when(j < nj1)
        def _():
            acc_ref[...] += _dot(a_ref[...], b1_ref[...], TN)

        @pl.when(j >= nj1)
        def _():
            acc_ref[...] += _dot(a_ref[...], b2_ref[...], TN)

        @pl.when(k == nk - 1)
        def _():
            o_ref[...] = acc_ref[...].astype(BF16)

        @pl.when(j == 0)
        def _():
            low = _dot(a_ref[...], bl_ref[...], TN)

            @pl.when(k == 0)
            def _():
                accl_ref[...] = low

            @pl.when(k > 0)
            def _():
                accl_ref[...] += low

            @pl.when(k == nk - 1)
            def _():
                ol_ref[...] = accl_ref[...].astype(BF16)

    return pl.pallas_call(
        body, name="in_proj_dw", grid=(M // bm, nj, nk),
        in_specs=[pl.BlockSpec((bk, bm), lambda i, j, k: (k, i)),
                  pl.BlockSpec((bk, bn), lambda i, j, k: (jnp.where(j < nj1, k, nk - 1), jnp.minimum(j, nj1 - 1))),
                  pl.BlockSpec((bk, bn), lambda i, j, k: (jnp.where(j >= nj1, k, 0), jnp.maximum(j - nj1, 0))),
                  pl.BlockSpec((bk, LANES), lambda i, j, k: (jnp.where(j == 0, k, nk - 1), 0))],
        out_specs=[pl.BlockSpec((bm, bn), lambda i, j, k: (i, j)), pl.BlockSpec((bm, LANES), lambda i, j, k: (i, 0))],
        out_shape=[jax.ShapeDtypeStruct((M, N1 + N2), BF16), jax.ShapeDtypeStruct((M, LANES), BF16)],
        scratch_shapes=[pltpu.VMEM((bm, bn), F32), pltpu.VMEM((bm, LANES), F32)],
        compiler_params=pltpu.CompilerParams(dimension_semantics=("parallel", "arbitrary", "arbitrary")),
    )(a, b1, b2, bl)


def _assemble_w_in(g, nsh, wm):
    _, R, nshp = g.shape
    nb_in = nshp // LANES
    nb_main = wm // LANES
    tr = _blk(R, 512, 2 * SUBLANES)
    plan = []
    for b in range(nb_main + 1):
        terms = []
        for k in range(g.shape[0]):
            for i in range(nb_in):
                delta = nsh * k + LANES * i - LANES * b
                lo, hi = max(0, -delta), min(LANES, LANES - delta, nsh - LANES * i)
                if abs(delta) < LANES and hi > lo:
                    terms.append((k, i, delta))
        plan.append(terms)
    deltas = sorted({d for terms in plan for _, _, d in terms if d})

    def body(g_ref, wm_ref, wl_ref):
        src = _iota2((LANES, LANES), 0)
        dst = _iota2((LANES, LANES), 1)
        shift = {d: (dst - src == d).astype(BF16) for d in deltas}
        for b, terms in enumerate(plan):
            acc = None
            for k, i, d in terms:
                blk = g_ref[k, :, LANES * i:LANES * (i + 1)]
                t = _dot(blk, shift[d]) if d else blk.astype(F32)
                acc = t if acc is None else acc + t
            if b < nb_main:
                wm_ref[:, LANES * b:LANES * (b + 1)] = acc.astype(BF16)
            else:
                wl_ref[...] = acc.astype(BF16)

    return pl.pallas_call(
        body, name="assemble_w_in", grid=(R // tr,),
        in_specs=[pl.BlockSpec((g.shape[0], tr, nshp), lambda r: (0, r, 0))],
        out_specs=[pl.BlockSpec((tr, wm), lambda r: (r, 0)), pl.BlockSpec((tr, LANES), lambda r: (r, 0))],
        out_shape=[jax.ShapeDtypeStruct((R, wm), BF16), jax.ShapeDtypeStruct((R, LANES), BF16)],
        compiler_params=pltpu.CompilerParams(dimension_semantics=("parallel",)),
    )(g)


def _prenorm_fwd(x, w, after):
    L, D = x.shape
    tr = _blk(L, 256, SUBLANES)

    def body(x_ref, w_ref, _after_ref, h_ref):
        xv = x_ref[...]
        r = lax.rsqrt(jnp.mean(xv * xv, axis=-1, keepdims=True) + EPS)
        h_ref[...] = (xv * r * w_ref[...]).astype(BF16)

    return pl.pallas_call(
        body, name="prenorm_fwd", grid=(L // tr,),
        in_specs=[pl.BlockSpec((tr, D), lambda i: (i, 0)), pl.BlockSpec((1, D), lambda i: (0, 0)),
                  pl.BlockSpec(memory_space=pl.ANY)],
        out_specs=pl.BlockSpec((tr, D), lambda i: (i, 0)),
        out_shape=jax.ShapeDtypeStruct((L, D), BF16),
        compiler_params=pltpu.CompilerParams(dimension_semantics=("parallel",)),
    )(x, w, after)


def _post_fwd_bwd(mixed, x, target, w):
    L, D = x.shape
    tr = _blk(L, 256, SUBLANES)
    nsteps = L // tr

    def body(mx_ref, x_ref, t_ref, w_ref, loss_ref, dm_ref, dout_ref, gw_ref, acc_ref):
        i = pl.program_id(0)

        @pl.when(i == 0)
        def _():
            acc_ref[...] = jnp.zeros_like(acc_ref)
            gw_ref[...] = jnp.zeros_like(gw_ref)

        mx = mx_ref[...]
        wv = w_ref[...]
        r = lax.rsqrt(jnp.mean(mx * mx, axis=-1, keepdims=True) + EPS)
        n = mx * r
        err = x_ref[...] + n * wv - t_ref[...]
        acc_ref[...] += jnp.sum(err * err, axis=0, keepdims=True)
        dout = err * (1.0 / D)
        dout_ref[...] = dout
        gw_ref[...] += jnp.sum(dout * n, axis=0, keepdims=True)
        dn = dout * wv
        dm_ref[...] = (r * (dn - n * jnp.mean(dn * n, axis=-1, keepdims=True))).astype(BF16)

        @pl.when(i == nsteps - 1)
        def _():
            loss_ref[...] = jnp.sum(acc_ref[...], axis=-1, keepdims=True) * (0.5 / D)

    row = pl.BlockSpec((tr, D), lambda i: (i, 0))
    vec = pl.BlockSpec((1, D), lambda i: (0, 0))
    return pl.pallas_call(
        body, name="post_fwd_bwd", grid=(nsteps,),
        in_specs=[row, row, row, vec],
        out_specs=[pl.BlockSpec((1, 1), lambda i: (0, 0)), row, row, vec],
        out_shape=[jax.ShapeDtypeStruct((1, 1), F32), jax.ShapeDtypeStruct((L, D), BF16),
                   jax.ShapeDtypeStruct((L, D), F32), jax.ShapeDtypeStruct((1, D), F32)],
        scratch_shapes=[pltpu.VMEM((1, D), F32)],
        compiler_params=pltpu.CompilerParams(dimension_semantics=("arbitrary",)),
    )(mixed, x, target, w)


def _prenorm_bwd(x, dh, dout, w):
    L, D = x.shape
    tr = _blk(L, 256, SUBLANES)

    def body(x_ref, a_ref, dout_ref, w_ref, gx_ref, gw_ref):
        i = pl.program_id(0)

        @pl.when(i == 0)
        def _():
            gw_ref[...] = jnp.zeros_like(gw_ref)

        xv = x_ref[...]
        r = lax.rsqrt(jnp.mean(xv * xv, axis=-1, keepdims=True) + EPS)
        n = xv * r
        dh = a_ref[...]
        gw_ref[...] += jnp.sum(dh * n, axis=0, keepdims=True)
        dn = dh * w_ref[...]
        gx_ref[...] = dout_ref[...] + r * (dn - n * jnp.mean(dn * n, axis=-1, keepdims=True))

    row = pl.BlockSpec((tr, D), lambda i: (i, 0))
    vec = pl.BlockSpec((1, D), lambda i: (0, 0))
    return pl.pallas_call(
        body, name="prenorm_bwd", grid=(L // tr,),
        in_specs=[row, row, row, vec],
        out_specs=[row, vec],
        out_shape=[jax.ShapeDtypeStruct((L, D), F32), jax.ShapeDtypeStruct((1, D), F32)],
        compiler_params=pltpu.CompilerParams(dimension_semantics=("arbitrary",)),
    )(x, dh, dout, w)


def _s5_disc(a_re_raw, a_im, dt):
    a_re = jnp.minimum(a_re_raw, -1e-4)
    mag = jnp.exp(a_re * dt)
    ph = a_im * dt
    ab_re = mag * jnp.cos(ph)
    ab_im = mag * jnp.sin(ph)
    inv_n = 1.0 / (a_re * a_re + a_im * a_im)
    ia_re = a_re * inv_n
    ia_im = -a_im * inv_n
    n_re = ab_re - 1.0
    f_re = n_re * ia_re - ab_im * ia_im
    f_im = n_re * ia_im + ab_im * ia_re
    return a_re, ab_re, ab_im, f_re, f_im, ia_re, ia_im


def _iota2(shape, dim):
    return lax.broadcasted_iota(jnp.int32, shape, dim)


def _group_mask(rows, rows_per_group):
    shift = rows_per_group.bit_length() - 1
    return (_iota2((rows, S5_LANES), 0) >> shift) == (_iota2((rows, S5_LANES), 1) >> (S5_STATE.bit_length() - 1))


def _lane_tiler(dtype):
    return ((_iota2((S5_STATE, S5_LANES), 1) & (S5_STATE - 1)) == _iota2((S5_STATE, S5_LANES), 0)).astype(dtype)


def _row_to_col(row, n):
    eye = (_iota2((n, n), 0) == _iota2((n, n), 1)).astype(F32)
    return jnp.sum(eye * row, axis=1, keepdims=True)


def _group_repeat(G):
    return ((_iota2((G * S5_GROUP, G), 0) >> (S5_GROUP.bit_length() - 1)) == _iota2((G * S5_GROUP, G), 1)).astype(F32)


S5_TABS = 18


def _s5_prep_fwd(a_re, a_im, log_dt, b_re, b_im, c_re, c_im, after, seg):
    G, P = a_re.shape
    nb = G * S5_GROUP // S5_COLS
    g8 = S5_COLS // S5_GROUP
    assert seg & (seg - 1) == 0, seg

    def body(are_ref, aim_ref, ldt_ref, bre_ref, bim_ref, cre_ref, cim_ref, _after_ref,
             bbre_ref, bbim_ref, ctre_ref, ctim_ref, tab_ref, pt_ref):
        dt = jnp.exp(_row_to_col(ldt_ref[...], G))
        _, ab_re, ab_im, f_re, f_im, _, _ = _s5_disc(are_ref[...], aim_ref[...], dt)
        rep = _group_repeat(G)
        fx_re = _dot_hi(rep, f_re)
        fx_im = _dot_hi(rep, f_im)
        br, bi = bre_ref[...], bim_ref[...]
        bb_re = fx_re * br - fx_im * bi
        bb_im = fx_re * bi + fx_im * br
        tile_bf = _lane_tiler(BF16)
        mask = _group_mask(S5_COLS, S5_GROUP)
        for jb in range(nb):
            rs = slice(jb * S5_COLS, (jb + 1) * S5_COLS)
            for src, dst in ((bb_re[rs], bbre_ref), (bb_im[rs], bbim_ref), (cre_ref[rs, :], ctre_ref), (cim_ref[rs, :], ctim_ref)):
                dst[jb] = jnp.where(mask, _dot(src, tile_bf), 0.0).astype(BF16)

        tile_f = _lane_tiler(F32)
        mask8 = _group_mask(g8, 1)
        row = _iota2((SUBLANES, S5_LANES), 0)
        slab = (SUBLANES, S5_LANES)
        cmul = lambda p, q: (p[0] * q[0] - p[1] * q[1], p[0] * q[1] + p[1] * q[0])
        for jb in range(nb):
            gs = slice(jb * g8, (jb + 1) * g8)

            def lanes(m):
                v = jnp.sum(jnp.where(mask8, _dot_hi(m[gs], tile_f), 0.0), axis=0, keepdims=True)
                return jnp.broadcast_to(v, slab)

            a1 = (lanes(ab_re), lanes(ab_im))
            tab_ref[jb, 0], tab_ref[jb, 1] = a1

            def powers(i, p):
                off = pl.multiple_of(i * SUBLANES, SUBLANES)
                pt_ref[jb, 0, pl.ds(off, SUBLANES), :] = p[0]
                pt_ref[jb, 1, pl.ds(off, SUBLANES), :] = p[1]
                return cmul(p, a1)

            lax.fori_loop(0, seg, powers, a1)
            aseg = a1
            for _ in range(seg.bit_length() - 1):
                aseg = cmul(aseg, aseg)
            pw = [aseg]
            for _ in range(1, SUBLANES):
                pw.append(cmul(pw[-1], aseg))
            for lvl, k in enumerate((1, 2, 4)):
                tab_ref[jb, 2 + 2 * lvl] = jnp.where(row >= k, pw[k - 1][0], 0.0)
                tab_ref[jb, 3 + 2 * lvl] = jnp.where(row >= k, pw[k - 1][1], 0.0)
                tab_ref[jb, 10 + 2 * lvl] = jnp.where(row < SUBLANES - k, pw[k - 1][0], 0.0)
                tab_ref[jb, 11 + 2 * lvl] = jnp.where(row < SUBLANES - k, -pw[k - 1][1], 0.0)
            f_r = f_i = r_r = r_i = jnp.zeros(slab, F32)
            for i in range(SUBLANES):
                f_r = jnp.where(row == i, pw[i][0], f_r)
                f_i = jnp.where(row == i, pw[i][1], f_i)
                r_r = jnp.where(row == i, pw[SUBLANES - 1 - i][0], r_r)
                r_i = jnp.where(row == i, -pw[SUBLANES - 1 - i][1], r_i)
            tab_ref[jb, 8] = f_r
            tab_ref[jb, 9] = f_i
            tab_ref[jb, 16] = r_r
            tab_ref[jb, 17] = r_i

    vm = pl.BlockSpec(memory_space=pltpu.VMEM)
    bd = jax.ShapeDtypeStruct((nb, S5_COLS, S5_LANES), BF16)
    return pl.pallas_call(
        body, name="s5_prep_fwd",
        in_specs=[vm] * 7 + [pl.BlockSpec(memory_space=pl.ANY)], out_specs=[vm] * 6,
        out_shape=[bd, bd, bd, bd, jax.ShapeDtypeStruct((nb, S5_TABS, SUBLANES, S5_LANES), F32),
                   jax.ShapeDtypeStruct((nb, 2, seg * SUBLANES, S5_LANES), F32)],
    )(a_re, a_im, log_dt, b_re, b_im, c_re, c_im, after)


def _s5_prep_bwd(a_re, a_im, log_dt, b_re, b_im, gbb_re, gbb_im, gct_re, gct_im, gab_re, gab_im):
    G, P = a_re.shape
    nb = G * S5_GROUP // S5_COLS
    g8 = S5_COLS // S5_GROUP

    def body(are_ref, aim_ref, ldt_ref, bre_ref, bim_ref, gbr_ref, gbi_ref, gcr_ref, gci_ref, gar_ref, gai_ref,
             o_a, o_bc, o_ldt):
        dt = jnp.exp(_row_to_col(ldt_ref[...], G))
        a_raw = are_ref[...]
        a_imv = aim_ref[...]
        a_re_c, ab_re, ab_im, f_re, f_im, ia_re, ia_im = _s5_disc(a_raw, a_imv, dt)
        tile_f = _lane_tiler(F32)
        mask = _group_mask(S5_COLS, S5_GROUP)
        mask8 = _group_mask(g8, 1)
        for jb in range(nb):
            rs = slice(jb * S5_COLS, (jb + 1) * S5_COLS)
            gs = slice(jb * g8, (jb + 1) * g8)
            ls = slice(jb * S5_LANES, (jb + 1) * S5_LANES)
            for k, src in enumerate((gbr_ref, gbi_ref, gcr_ref, gci_ref)):
                o_bc[k, rs, :] = _dot_hi(jnp.where(mask, src[jb], 0.0), tile_f, NT)
            for k, src in enumerate((gar_ref, gai_ref)):
                o_a[k, gs, :] = _dot_hi(jnp.where(mask8, src[:, ls], 0.0), tile_f, NT)
        rep = _group_repeat(G)
        fx_re = _dot_hi(rep, f_re)
        fx_im = _dot_hi(rep, f_im)
        gbr, gbi = o_bc[0], o_bc[1]
        br, bi = bre_ref[...], bim_ref[...]
        o_bc[0] = fx_re * gbr + fx_im * gbi
        o_bc[1] = fx_re * gbi - fx_im * gbr
        gf_re = _dot_hi(rep, br * gbr + bi * gbi, TN)
        gf_im = _dot_hi(rep, br * gbi - bi * gbr, TN)
        gab_r = o_a[0] + ia_re * gf_re + ia_im * gf_im
        gab_i = o_a[1] + ia_re * gf_im - ia_im * gf_re
        q_re = f_re * ia_re - f_im * ia_im
        q_im = f_re * ia_im + f_im * ia_re
        ga_re = -(q_re * gf_re + q_im * gf_im)
        ga_im = -(q_re * gf_im - q_im * gf_re)
        gth_re = ab_re * gab_r + ab_im * gab_i
        gth_im = ab_re * gab_i - ab_im * gab_r
        ga_re = ga_re + dt * gth_re
        ga_im = ga_im + dt * gth_im
        gdt = jnp.sum(a_re_c * gth_re + a_imv * gth_im, axis=-1, keepdims=True)
        eye = (_iota2((G, G), 0) == _iota2((G, G), 1)).astype(F32)
        o_ldt[...] = jnp.sum(eye * (gdt * dt), axis=0, keepdims=True)
        slope = jnp.where(a_raw < -1e-4, 1.0, jnp.where(a_raw == -1e-4, 0.5, 0.0))
        o_a[0] = ga_re * slope
        o_a[1] = ga_im

    vm = pl.BlockSpec(memory_space=pltpu.VMEM)
    return pl.pallas_call(
        body, name="s5_prep_bwd",
        in_specs=[vm] * 11, out_specs=[vm] * 3,
        out_shape=[jax.ShapeDtypeStruct((2, G, P), F32), jax.ShapeDtypeStruct((4, G * S5_GROUP, P), F32),
                   jax.ShapeDtypeStruct((1, G), F32)],
    )(a_re, a_im, log_dt, b_re, b_im, gbb_re, gbb_im, gct_re, gct_im, gab_re, gab_im)


def _scan8(xr, xi, tab_ref, base, shifts):
    for lvl, sh in enumerate(shifts):
        mr = tab_ref[0, base + 2 * lvl]
        mi = tab_ref[0, base + 2 * lvl + 1]
        ar = pltpu.roll(xr, sh, 0)
        ai = pltpu.roll(xi, sh, 0)
        xr, xi = xr + mr * ar - mi * ai, xi + mr * ai + mi * ar
    return xr, xi


def _to_segments(src_ref, dst_ref, seg):
    for i in range(seg):
        dst_ref[i * SUBLANES:(i + 1) * SUBLANES, :] = src_ref[pl.ds(i, SUBLANES, stride=seg), :]


def _from_segments(src_ref, dst_ref, seg):
    for i in range(seg):
        dst_ref[pl.ds(i, SUBLANES, stride=seg), :] = src_ref[i * SUBLANES:(i + 1) * SUBLANES, :]


def _slab(i):
    return pl.ds(pl.multiple_of(i * SUBLANES, SUBLANES), SUBLANES)


def _s5_scan_fwd(proj_main, bbd_re, bbd_im, cbd_re, cbd_im, dvec, tab, ptab, DS):
    L = proj_main.shape[0]
    nb = DS // S5_COLS
    tb = _blk(L, 512, SUBLANES)
    nt = L // tb
    seg = tb // SUBLANES

    def body(u_ref, bre_ref, bim_ref, cre_ref, cim_ref, d_ref, tab_ref, pt_ref, y_ref, sre_ref, sim_ref,
             up_ref, yp_ref, car_ref):
        t = pl.program_id(1)

        @pl.when(t == 0)
        def _():
            car_ref[...] = jnp.zeros_like(car_ref)

        _to_segments(u_ref, up_ref, seg)
        up = up_ref[...]
        sre_ref[...] = _dot(up, bre_ref[0])
        sim_ref[...] = _dot(up, bim_ref[0])
        ar, ai = tab_ref[0, 0], tab_ref[0, 1]

        def pass1(i, x):
            xr = ar * x[0] - ai * x[1] + sre_ref[_slab(i), :]
            xi = ar * x[1] + ai * x[0] + sim_ref[_slab(i), :]
            sre_ref[_slab(i), :] = xr
            sim_ref[_slab(i), :] = xi
            return xr, xi

        zero = jnp.zeros((SUBLANES, S5_LANES), F32)
        er, ei = lax.fori_loop(0, seg, pass1, (zero, zero))
        cin_r, cin_i = car_ref[0], car_ref[1]
        sr, si = _scan8(er, ei, tab_ref, 2, (1, 2, 4))
        pr, pi = tab_ref[0, 8], tab_ref[0, 9]
        sr, si = sr + pr * cin_r - pi * cin_i, si + pr * cin_i + pi * cin_r
        row0 = _iota2((SUBLANES, S5_LANES), 0) == 0
        cr = jnp.where(row0, cin_r, pltpu.roll(sr, 1, 0))
        ci = jnp.where(row0, cin_i, pltpu.roll(si, 1, 0))
        car_ref[0] = jnp.broadcast_to(sr[SUBLANES - 1:SUBLANES, :], sr.shape)
        car_ref[1] = jnp.broadcast_to(si[SUBLANES - 1:SUBLANES, :], si.shape)

        def pass2(i, _):
            qr, qi = pt_ref[0, 0, _slab(i), :], pt_ref[0, 1, _slab(i), :]
            sre_ref[_slab(i), :] += qr * cr - qi * ci
            sim_ref[_slab(i), :] += qr * ci + qi * cr
            return 0

        lax.fori_loop(0, seg, pass2, 0, unroll=4)
        yp_ref[...] = _dot(sre_ref[...], cre_ref[0], NT) - _dot(sim_ref[...], cim_ref[0], NT) + d_ref[...] * up
        _from_segments(yp_ref, y_ref, seg)

    return pl.pallas_call(
        body, name="s5_scan_fwd", grid=(nb, nt),
        in_specs=[
            pl.BlockSpec((tb, S5_COLS), lambda j, t: (t, j)),
            pl.BlockSpec((1, S5_COLS, S5_LANES), lambda j, t: (j, 0, 0)),
            pl.BlockSpec((1, S5_COLS, S5_LANES), lambda j, t: (j, 0, 0)),
            pl.BlockSpec((1, S5_COLS, S5_LANES), lambda j, t: (j, 0, 0)),
            pl.BlockSpec((1, S5_COLS, S5_LANES), lambda j, t: (j, 0, 0)),
            pl.BlockSpec((1, S5_COLS), lambda j, t: (0, j)),
            pl.BlockSpec((1, S5_TABS, SUBLANES, S5_LANES), lambda j, t: (j, 0, 0, 0)),
            pl.BlockSpec((1, 2, tb, S5_LANES), lambda j, t: (j, 0, 0, 0)),
        ],
        out_specs=[
            pl.BlockSpec((tb, S5_COLS), lambda j, t: (t, j)),
            pl.BlockSpec((tb, S5_LANES), lambda j, t: (t, j)),
            pl.BlockSpec((tb, S5_LANES), lambda j, t: (t, j)),
        ],
        out_shape=[jax.ShapeDtypeStruct((L, DS), F32),
                   jax.ShapeDtypeStruct((L, nb * S5_LANES), F32),
                   jax.ShapeDtypeStruct((L, nb * S5_LANES), F32)],
        scratch_shapes=[pltpu.VMEM((tb, S5_COLS), F32), pltpu.VMEM((tb, S5_COLS), F32),
                        pltpu.VMEM((2, SUBLANES, S5_LANES), F32)],
        compiler_params=pltpu.CompilerParams(dimension_semantics=("parallel", "arbitrary")),
    )(proj_main, bbd_re, bbd_im, cbd_re, cbd_im, dvec, tab, ptab)


def _s5_scan_bwd(dy, proj_main, s_re, s_im, bbd_re, bbd_im, cbd_re, cbd_im, dvec, tab, ptab, d_s5, DS):
    L = proj_main.shape[0]
    nb = DS // S5_COLS
    tb = _blk(L, 512, SUBLANES)
    nt = L // tb
    seg = tb // SUBLANES
    tb8 = tb // SUBLANES

    def body(dy_ref, u_ref, sre_ref, sim_ref, pre_ref, pim_ref, bre_ref, bim_ref, cre_ref, cim_ref, d_ref, tab_ref, pt_ref,
             _ds5_ref, du_ref, gd_ref, gcre_ref, gcim_ref, gbre_ref, gbim_ref, gare_ref, gaim_ref,
             lre_ref, lim_ref, up_ref, dyp_ref, dup_ref, duo_ref, car_ref):
        t = pl.program_id(1)

        @pl.when(t == 0)
        def _():
            car_ref[...] = jnp.zeros_like(car_ref)
            gd_ref[...] = jnp.zeros_like(gd_ref)
            gcre_ref[...] = jnp.zeros_like(gcre_ref)
            gcim_ref[...] = jnp.zeros_like(gcim_ref)
            gbre_ref[...] = jnp.zeros_like(gbre_ref)
            gbim_ref[...] = jnp.zeros_like(gbim_ref)
            gare_ref[...] = jnp.zeros_like(gare_ref)
            gaim_ref[...] = jnp.zeros_like(gaim_ref)

        _to_segments(dy_ref, dyp_ref, seg)
        _to_segments(u_ref, up_ref, seg)
        dyv = dyp_ref[...]
        u = up_ref[...]
        gd_ref[...] += jnp.sum(dyv * u, axis=0, keepdims=True)
        lre_ref[...] = _dot(dyv, cre_ref[0])
        lim_ref[...] = -_dot(dyv, cim_ref[0])
        gcre_ref[0] += _dot(dyv, sre_ref[...], TN)
        gcim_ref[0] -= _dot(dyv, sim_ref[...], TN)
        ar, ai = tab_ref[0, 0], -tab_ref[0, 1]

        def pass1(k, x):
            i = seg - 1 - k
            xr = ar * x[0] - ai * x[1] + lre_ref[_slab(i), :]
            xi = ar * x[1] + ai * x[0] + lim_ref[_slab(i), :]
            lre_ref[_slab(i), :] = xr
            lim_ref[_slab(i), :] = xi
            return xr, xi

        zero = jnp.zeros((SUBLANES, S5_LANES), F32)
        er, ei = lax.fori_loop(0, seg, pass1, (zero, zero))
        cin_r, cin_i = car_ref[0], car_ref[1]
        lr, li = _scan8(er, ei, tab_ref, 10, (7, 6, 4))
        pr, pi = tab_ref[0, 16], tab_ref[0, 17]
        lr, li = lr + pr * cin_r - pi * cin_i, li + pr * cin_i + pi * cin_r
        rows = _iota2((SUBLANES, S5_LANES), 0)
        cr = jnp.where(rows == SUBLANES - 1, cin_r, pltpu.roll(lr, SUBLANES - 1, 0))
        ci = jnp.where(rows == SUBLANES - 1, cin_i, pltpu.roll(li, SUBLANES - 1, 0))
        car_ref[0] = jnp.broadcast_to(lr[0:1, :], lr.shape)
        car_ref[1] = jnp.broadcast_to(li[0:1, :], li.shape)

        first = (t == nt - 1).astype(F32)
        head_re = jnp.broadcast_to(pre_ref[SUBLANES - 1:SUBLANES, :], zero.shape) * (1.0 - first)
        head_im = jnp.broadcast_to(pim_ref[SUBLANES - 1:SUBLANES, :], zero.shape) * (1.0 - first)
        last = _slab(seg - 1)
        sp0_re = jnp.where(rows == 0, head_re, pltpu.roll(sre_ref[last, :], 1, 0))
        sp0_im = jnp.where(rows == 0, head_im, pltpu.roll(sim_ref[last, :], 1, 0))

        def pass2(i, acc):
            j = seg - 1 - i
            qr, qi = pt_ref[0, 0, _slab(j), :], -pt_ref[0, 1, _slab(j), :]
            xr = lre_ref[_slab(i), :] + qr * cr - qi * ci
            xi = lim_ref[_slab(i), :] + qr * ci + qi * cr
            lre_ref[_slab(i), :] = xr
            lim_ref[_slab(i), :] = xi
            prev = _slab(jnp.maximum(i - 1, 0))
            sp_re = jnp.where(i == 0, sp0_re, sre_ref[prev, :])
            sp_im = jnp.where(i == 0, sp0_im, sim_ref[prev, :])
            return acc[0] + sp_re * xr + sp_im * xi, acc[1] + sp_re * xi - sp_im * xr

        acc_re, acc_im = lax.fori_loop(0, seg, pass2, (zero, zero), unroll=2)
        gare_ref[...] += jnp.sum(acc_re, axis=0, keepdims=True)
        gaim_ref[...] += jnp.sum(acc_im, axis=0, keepdims=True)
        lre = lre_ref[...]
        lim = lim_ref[...]
        dup_ref[...] = dyv * d_ref[...] + _dot(lre, bre_ref[0], NT) + _dot(lim, bim_ref[0], NT)
        _from_segments(dup_ref, duo_ref, seg)
        du_ref[...] = duo_ref[...].astype(BF16)
        gbre_ref[0] += _dot(u, lre, TN)
        gbim_ref[0] += _dot(u, lim, TN)

    rt = lambda t: nt - 1 - t
    col = pl.BlockSpec((tb, S5_COLS), lambda j, t: (rt(t), j))
    st = pl.BlockSpec((tb, S5_LANES), lambda j, t: (rt(t), j))
    prev = pl.BlockSpec((SUBLANES, S5_LANES), lambda j, t: (jnp.maximum(rt(t) * tb8 - 1, 0), j))
    bmat = pl.BlockSpec((1, S5_COLS, S5_LANES), lambda j, t: (j, 0, 0))
    cmat = bmat
    return pl.pallas_call(
        body, name="s5_scan_bwd", grid=(nb, nt),
        in_specs=[col, col, st, st, prev, prev, bmat, bmat, cmat, cmat,
                  pl.BlockSpec((1, S5_COLS), lambda j, t: (0, j)),
                  pl.BlockSpec((1, S5_TABS, SUBLANES, S5_LANES), lambda j, t: (j, 0, 0, 0)),
                  pl.BlockSpec((1, 2, tb, S5_LANES), lambda j, t: (j, 0, 0, 0)),
                  pl.BlockSpec(memory_space=pl.ANY)],
        out_specs=[col, pl.BlockSpec((1, S5_COLS), lambda j, t: (0, j)), cmat, cmat, bmat, bmat,
                   pl.BlockSpec((1, S5_LANES), lambda j, t: (0, j)), pl.BlockSpec((1, S5_LANES), lambda j, t: (0, j))],
        input_output_aliases={13: 0},
        out_shape=[jax.ShapeDtypeStruct((L, 2 * DS), BF16), jax.ShapeDtypeStruct((1, DS), F32),
                   jax.ShapeDtypeStruct((nb, S5_COLS, S5_LANES), F32), jax.ShapeDtypeStruct((nb, S5_COLS, S5_LANES), F32),
                   jax.ShapeDtypeStruct((nb, S5_COLS, S5_LANES), F32), jax.ShapeDtypeStruct((nb, S5_COLS, S5_LANES), F32),
                   jax.ShapeDtypeStruct((1, nb * S5_LANES), F32), jax.ShapeDtypeStruct((1, nb * S5_LANES), F32)],
        scratch_shapes=[pltpu.VMEM((tb, S5_LANES), F32), pltpu.VMEM((tb, S5_LANES), F32)]
        + [pltpu.VMEM((tb, S5_COLS), F32)] * 4 + [pltpu.VMEM((2, SUBLANES, S5_LANES), F32)],
        compiler_params=pltpu.CompilerParams(dimension_semantics=("parallel", "arbitrary")),
    )(dy, proj_main, s_re, s_im, s_re, s_im, bbd_re, bbd_im, cbd_re, cbd_im, dvec, tab, ptab, d_s5)


def _s5_post_fwd(y_pre, proj_main, glu_w, glu_b, DS):
    L = y_pre.shape[0]
    tr = _blk(L, 256, SUBLANES)

    def body(y_ref, z_ref, w_ref, b_ref, o_ref, t_ref):
        y1 = _gelu(y_ref[...])
        t = _dot(y1, w_ref[...]) + b_ref[...]
        t_ref[...] = t
        z = z_ref[...]
        o_ref[...] = (y1 * _sigmoid(t) * (z * _sigmoid(z))).astype(BF16)

    row = pl.BlockSpec((tr, DS), lambda i: (i, 0))
    return pl.pallas_call(
        body, name="s5_post_fwd", grid=(L // tr,),
        in_specs=[row, pl.BlockSpec((tr, DS), lambda i: (i, 1)), pl.BlockSpec((DS, DS), lambda i: (0, 0)),
                  pl.BlockSpec((1, DS), lambda i: (0, 0))],
        out_specs=[row, row],
        out_shape=[jax.ShapeDtypeStruct((L, 2 * DS), BF16), jax.ShapeDtypeStruct((L, DS), F32)],
        compiler_params=pltpu.CompilerParams(dimension_semantics=("parallel",)),
    )(y_pre, proj_main, glu_w, glu_b)


def _s5_post_bwd(d_ycat, y_pre, proj_main, t_pre, glu_w, DS):
    L = y_pre.shape[0]
    tr = _blk(L, 256, SUBLANES)

    def body(dy_ref, y_ref, z_ref, t_ref, w_ref, dyp_ref, dz_ref, dt_ref, y1_ref, gb_ref):
        i = pl.program_id(0)

        @pl.when(i == 0)
        def _():
            gb_ref[...] = jnp.zeros_like(gb_ref)

        dy = dy_ref[...]
        yp = y_ref[...]
        z = z_ref[...]
        y1 = _gelu(yp)
        sg = _sigmoid(t_ref[...])
        sz = _sigmoid(z)
        c = y1 * sg
        d_c = dy * (z * sz)
        dz_ref[...] = (dy * c * (sz * (1.0 + z * (1.0 - sz)))).astype(BF16)
        d_t = d_c * y1 * sg * (1.0 - sg)
        gb_ref[...] += jnp.sum(d_t, axis=0, keepdims=True)
        dt_ref[...] = d_t.astype(BF16)
        y1_ref[...] = y1.astype(BF16)
        d_y1 = d_c * sg + _dot(d_t, w_ref[...], NT)
        dyp_ref[...] = d_y1 * _gelu_grad(yp)

    row = pl.BlockSpec((tr, DS), lambda i: (i, 0))
    return pl.pallas_call(
        body, name="s5_post_bwd", grid=(L // tr,),
        in_specs=[row, row, pl.BlockSpec((tr, DS), lambda i: (i, 1)), row, pl.BlockSpec((DS, DS), lambda i: (0, 0))],
        out_specs=[row, pl.BlockSpec((tr, DS), lambda i: (i, 1)), row, row, pl.BlockSpec((1, DS), lambda i: (0, 0))],
        out_shape=[jax.ShapeDtypeStruct((L, DS), F32), jax.ShapeDtypeStruct((L, 2 * DS), BF16),
                   jax.ShapeDtypeStruct((L, DS), BF16), jax.ShapeDtypeStruct((L, DS), BF16),
                   jax.ShapeDtypeStruct((1, DS), F32)],
        compiler_params=pltpu.CompilerParams(dimension_semantics=("arbitrary",)),
    )(d_ycat, y_pre, proj_main, t_pre, glu_w)


def _row_cumsum(x, reverse=False):
    n = x.shape[0]
    row = lax.broadcasted_iota(jnp.int32, x.shape, 0)
    k = 1
    while k < n:
        if reverse:
            x = x + jnp.where(row < n - k, pltpu.roll(x, n - k, 0), 0.0)
        else:
            x = x + jnp.where(row >= k, pltpu.roll(x, k, 0), 0.0)
        k *= 2
    return x


def _gla_gates(glow, gu_ref, gb_ref):
    a = _dot(glow, gu_ref[...]) + gb_ref[...]
    lg = (jnp.minimum(a, 0.0) - jnp.log(1.0 + jnp.exp(-jnp.abs(a)))) * (1.0 / GLA_TAU)
    ri = lax.broadcasted_iota(jnp.int32, (GLA_CHUNK, GLA_CHUNK), 0)
    ci = lax.broadcasted_iota(jnp.int32, (GLA_CHUNK, GLA_CHUNK), 1)
    b = _row_cumsum(lg)
    b_last = b[GLA_CHUNK - 1:GLA_CHUNK, :]
    return a, b, b_last, ri >= ci


def _gla_specs(DS, DK, DV, c, cmap):
    return [
        pl.BlockSpec((c, DK), lambda n: (cmap(n), 2 * DS // DK)),
        pl.BlockSpec((c, DK), lambda n: (cmap(n), 2 * DS // DK + 1)),
        pl.BlockSpec((c, DV), lambda n: (cmap(n), (2 * DS + 2 * DK) // DV)),
        pl.BlockSpec((c, DV), lambda n: (cmap(n), (2 * DS + 2 * DK) // DV + 1)),
    ]


def _gla_fwd(proj_main, proj_low, gate_up_pad, gate_bias, norm_w, ycat, DS, DK, DV):
    L = proj_main.shape[0]
    nc = L // GLA_CHUNK
    cps = math.gcd(GLA_STEP_CHUNKS, nc)
    nh = DK // GLA_HK
    scale = GLA_HK ** -0.5

    def body(q_ref, k_ref, v_ref, z_ref, gl_ref, gu_ref, gb_ref, nw_ref, _yc_ref, y_ref, sp_ref, st_ref):
        n = pl.program_id(0)

        @pl.when(n == 0)
        def _():
            st_ref[...] = jnp.zeros_like(st_ref)

        pairs = [(sc, h) for sc in range(cps) for h in range(nh)]
        rows = lambda sc: slice(sc * GLA_CHUNK, (sc + 1) * GLA_CHUNK)
        kcol = lambda h: slice(h * GLA_HK, (h + 1) * GLA_HK)
        vcol = lambda h: slice(h * GLA_HV, (h + 1) * GLA_HV)
        gates = [_gla_gates(gl_ref[rows(sc), :], gu_ref, gb_ref) for sc in range(cps)]
        qe, dec, o_in, kv = {}, {}, {}, {}
        for sc, h in pairs:
            _, b, b_last, mask = gates[sc]
            bh, bl = b[:, kcol(h)], b_last[:, kcol(h)]
            qe[sc, h] = (q_ref[rows(sc), kcol(h)] * scale) * jnp.exp(bh)
            kh = k_ref[rows(sc), kcol(h)]
            vh = v_ref[rows(sc), vcol(h)]
            attn = jnp.where(mask, _dot(qe[sc, h], kh * jnp.exp(-bh), NT), 0.0)
            o_in[sc, h] = _dot(attn, vh)
            kv[sc, h] = _dot(vh, kh * jnp.exp(bl - bh), TN)
            dec[sc, h] = jnp.exp(bl)
        for sc, h in pairs:
            st = st_ref[h]
            sp_ref[sc, h] = st
            o = o_in[sc, h] + _dot(qe[sc, h], st, NT)
            st_ref[h] = dec[sc, h] * st + kv[sc, h]
            r = lax.rsqrt(jnp.mean(o * o, axis=-1, keepdims=True) + EPS)
            z = z_ref[rows(sc), vcol(h)]
            y_ref[rows(sc), vcol(h)] = (o * r * nw_ref[...] * (z * _sigmoid(z))).astype(BF16)

    c = cps * GLA_CHUNK
    return pl.pallas_call(
        body, name="gla_fwd", grid=(nc // cps,),
        in_specs=_gla_specs(DS, DK, DV, c, lambda n: n) + [
            pl.BlockSpec((c, LANES), lambda n: (n, 0)),
            pl.BlockSpec((LANES, DK), lambda n: (0, 0)),
            pl.BlockSpec((1, DK), lambda n: (0, 0)),
            pl.BlockSpec((1, GLA_HV), lambda n: (0, 0)),
            pl.BlockSpec(memory_space=pl.ANY),
        ],
        out_specs=[pl.BlockSpec((c, DV), lambda n: (n, DS // DV)),
                   pl.BlockSpec((cps, nh, GLA_HV, GLA_HK), lambda n: (n, 0, 0, 0))],
        input_output_aliases={8: 0},
        out_shape=[jax.ShapeDtypeStruct(ycat.shape, BF16), jax.ShapeDtypeStruct((nc, nh, GLA_HV, GLA_HK), F32)],
        scratch_shapes=[pltpu.VMEM((nh, GLA_HV, GLA_HK), F32)],
        compiler_params=pltpu.CompilerParams(dimension_semantics=("arbitrary",)),
    )(proj_main, proj_main, proj_main, proj_main, proj_low, gate_up_pad, gate_bias, norm_w, ycat)


def _gla_bwd(d_ycat, proj_main, proj_low, s_prev, gate_up_pad, gate_bias, norm_w, DS, DK, DV):
    L = proj_main.shape[0]
    nc = L // GLA_CHUNK
    cps = math.gcd(GLA_STEP_CHUNKS, nc)
    nh = DK // GLA_HK
    scale = GLA_HK ** -0.5

    def body(dy_ref, q_ref, k_ref, v_ref, z_ref, gl_ref, sp_ref, gu_ref, gb_ref, nw_ref,
             dg_ref, da_ref, gnw_ref, ggb_ref, dst_ref):
        n = pl.program_id(0)

        @pl.when(n == 0)
        def _():
            dst_ref[...] = jnp.zeros_like(dst_ref)
            gnw_ref[...] = jnp.zeros_like(gnw_ref)
            ggb_ref[...] = jnp.zeros_like(ggb_ref)

        last_row = lax.broadcasted_iota(jnp.int32, (GLA_CHUNK, GLA_HK), 0) == GLA_CHUNK - 1
        nw = nw_ref[...]
        for sc in reversed(range(cps)):
            rs = slice(sc * GLA_CHUNK, (sc + 1) * GLA_CHUNK)
            a, b, b_last, mask = _gla_gates(gl_ref[rs, :], gu_ref, gb_ref)
            for h in range(nh):
                ks = slice(h * GLA_HK, (h + 1) * GLA_HK)
                vs = slice(h * GLA_HV, (h + 1) * GLA_HV)
                bh, bl = b[:, ks], b_last[:, ks]
                e = jnp.exp(bh)
                einv = jnp.exp(-bh)
                etail = jnp.exp(bl - bh)
                dec = jnp.exp(bl)
                qe = (q_ref[rs, ks] * scale) * e
                kh = k_ref[rs, ks]
                ke = kh * einv
                ktail = kh * etail
                vh = v_ref[rs, vs]
                st = sp_ref[sc, h]
                dst = dst_ref[h]
                attn = jnp.where(mask, _dot(qe, ke, NT), 0.0)
                o = _dot(attn, vh) + _dot(qe, st, NT)
                r = lax.rsqrt(jnp.mean(o * o, axis=-1, keepdims=True) + EPS)
                nrm = o * r
                z = z_ref[rs, vs]
                sz = _sigmoid(z)
                dy = dy_ref[rs, vs]
                dg_ref[rs, 2 * DK + DV + h * GLA_HV:2 * DK + DV + (h + 1) * GLA_HV] = (
                    dy * nrm * nw * (sz * (1.0 + z * (1.0 - sz)))).astype(BF16)
                d_on = dy * (z * sz)
                gnw_ref[...] += jnp.sum(d_on * nrm, axis=0, keepdims=True)
                d_n = d_on * nw
                d_o = r * (d_n - nrm * jnp.mean(d_n * nrm, axis=-1, keepdims=True))
                d_attn = jnp.where(mask, _dot(d_o, vh, NT), 0.0)
                dg_ref[rs, 2 * DK + h * GLA_HV:2 * DK + (h + 1) * GLA_HV] = (
                    _dot(attn, d_o, TN) + _dot(ktail, dst, NT)).astype(BF16)
                d_qe = _dot(d_attn, ke) + _dot(d_o, st)
                d_ke = _dot(d_attn, qe, TN)
                d_kt = _dot(vh, dst)
                d_dec = jnp.sum(dst * st, axis=0, keepdims=True)
                dst_ref[h] = dec * dst + _dot(d_o, qe, TN)
                dg_ref[rs, ks] = (d_qe * scale * e).astype(BF16)
                dg_ref[rs, DK + h * GLA_HK:DK + (h + 1) * GLA_HK] = (d_ke * einv + d_kt * etail).astype(BF16)
                d_bl = jnp.sum(d_kt * ktail, axis=0, keepdims=True) + d_dec * dec
                d_b = d_qe * qe - d_ke * ke - d_kt * ktail + jnp.where(last_row, d_bl, 0.0)
                d_lg = _row_cumsum(d_b, reverse=True)
                d_a = d_lg * (1.0 / GLA_TAU) * _sigmoid(-a[:, ks])
                ggb_ref[:, ks] += jnp.sum(d_a, axis=0, keepdims=True)
                da_ref[rs, ks] = d_a.astype(BF16)

    c = cps * GLA_CHUNK
    ns = nc // cps
    rn = lambda n: ns - 1 - n
    return pl.pallas_call(
        body, name="gla_bwd", grid=(ns,),
        in_specs=[pl.BlockSpec((c, DV), lambda n: (rn(n), DS // DV))] + _gla_specs(DS, DK, DV, c, rn) + [
            pl.BlockSpec((c, LANES), lambda n: (rn(n), 0)),
            pl.BlockSpec((cps, nh, GLA_HV, GLA_HK), lambda n: (rn(n), 0, 0, 0)),
            pl.BlockSpec((LANES, DK), lambda n: (0, 0)),
            pl.BlockSpec((1, DK), lambda n: (0, 0)),
            pl.BlockSpec((1, GLA_HV), lambda n: (0, 0)),
        ],
        out_specs=[pl.BlockSpec((c, 2 * DK + 2 * DV), lambda n: (rn(n), 0)),
                   pl.BlockSpec((c, DK), lambda n: (rn(n), 0)),
                   pl.BlockSpec((1, GLA_HV), lambda n: (0, 0)), pl.BlockSpec((1, DK), lambda n: (0, 0))],
        out_shape=[jax.ShapeDtypeStruct((L, 2 * DK + 2 * DV), BF16),
                   jax.ShapeDtypeStruct((L, DK), BF16),
                   jax.ShapeDtypeStruct((1, GLA_HV), F32), jax.ShapeDtypeStruct((1, DK), F32)],
        scratch_shapes=[pltpu.VMEM((nh, GLA_HV, GLA_HK), F32)],
        compiler_params=pltpu.CompilerParams(dimension_semantics=("arbitrary",)),
    )(d_ycat, proj_main, proj_main, proj_main, proj_main, proj_low, s_prev, gate_up_pad, gate_bias, norm_w)


def _adamw_math(w, g, m, v):
    c1 = 1.0 - ADAM_B1 ** ADAM_STEP
    c2 = 1.0 - ADAM_B2 ** ADAM_STEP
    m_ = ADAM_B1 * m + (1.0 - ADAM_B1) * g
    v_ = ADAM_B2 * v + (1.0 - ADAM_B2) * (g * g)
    return -ADAM_LR * ((m_ / c1) / (jnp.sqrt(v_ / c2) + ADAM_EPS) + ADAM_WD * w), m_, v_


def _adamw_small(g_row, g_a, g_bc, ws, ms, vs):
    n = len(ws)
    nvec = n - 6

    def body(*refs):
        grow_ref, ga_ref, gbc_ref = refs[:3]
        w_refs, m_refs, v_refs = refs[3:3 + n], refs[3 + n:3 + 2 * n], refs[3 + 2 * n:3 + 3 * n]
        outs = refs[3 + 3 * n:]
        off = 0
        for i in range(n):
            if i < nvec:
                width = ws[i].shape[1]
                g = grow_ref[:, off:off + width]
                off += width
            elif i < nvec + 2:
                g = ga_ref[i - nvec]
            else:
                g = gbc_ref[i - nvec - 2]
            d, m_, v_ = _adamw_math(w_refs[i][...], g, m_refs[i][...], v_refs[i][...])
            outs[i][...] = g
            outs[n + i][...] = d
            outs[2 * n + i][...] = m_
            outs[3 * n + i][...] = v_

    vm = pl.BlockSpec(memory_space=pltpu.VMEM)
    outs = pl.pallas_call(
        body, name="adamw_small",
        in_specs=[vm] * (3 + 3 * n), out_specs=[vm] * (4 * n),
        out_shape=[jax.ShapeDtypeStruct(w.shape, F32) for w in ws] * 4,
    )(g_row, g_a, g_bc, *ws, *ms, *vs)
    return [outs[k * n:(k + 1) * n] for k in range(4)]


def _my_pos():
    return lax.axis_index("x"), lax.axis_index("y"), lax.axis_index("c")


def _split_start(name, srcs, lands_sd, make_copies, ncopies, after):
    n, m = len(srcs), len(lands_sd)

    def body(*refs):
        send_sems, recv_sems = refs[n + m + len(after)], refs[n + m + len(after) + 1]
        for cp in make_copies(refs[:n], refs[n:n + m], send_sems, recv_sems):
            cp.start()
        refs[-1][...] = jnp.zeros_like(refs[-1])

    hbm = pl.BlockSpec(memory_space=pltpu.HBM)
    sem = pl.BlockSpec(memory_space=pltpu.SEMAPHORE)
    outs = pl.pallas_call(
        body, name=name,
        in_specs=[hbm] * (n + m) + [pl.BlockSpec(memory_space=pl.ANY)] * len(after),
        out_specs=[sem, sem] + [hbm] * (n + m) + [pl.BlockSpec(memory_space=pltpu.VMEM)],
        out_shape=[pltpu.SemaphoreType.DMA((ncopies,)), pltpu.SemaphoreType.DMA((ncopies,))]
        + [pltpu.HBM(s.shape, s.dtype) for s in srcs] + [pltpu.HBM(s.shape, s.dtype) for s in lands_sd]
        + [jax.ShapeDtypeStruct((SUBLANES, LANES), F32)],
        input_output_aliases={i: 2 + i for i in range(n + m)},
        compiler_params=pltpu.CompilerParams(has_side_effects=pltpu.SideEffectType.DATAFLOW_SIDE_EFFECTING),
    )(*[pltpu.with_memory_space_constraint(s, pltpu.HBM) for s in srcs],
      *[pltpu.with_memory_space_constraint(lax.empty(s.shape, s.dtype), pltpu.HBM) for s in lands_sd], *after)
    return outs[0], outs[1], outs[2:2 + n], outs[2 + n:2 + n + m], outs[-1]


def _split_wait(name, send_sems, recv_sems, srcs, lands, make_copies, after):
    n, m = len(srcs), len(lands)

    def body(*refs):
        for cp in make_copies(refs[:n], refs[n:n + m], refs[n + m], refs[n + m + 1]):
            cp.wait_send()
            cp.wait_recv()

    hbm = pl.BlockSpec(memory_space=pltpu.HBM)
    sem = pl.BlockSpec(memory_space=pltpu.SEMAPHORE)
    outs = pl.pallas_call(
        body, name=name,
        in_specs=[hbm] * (n + m) + [sem, sem] + [pl.BlockSpec(memory_space=pl.ANY)] * len(after),
        out_specs=[hbm] * (n + m),
        out_shape=[pltpu.HBM(s.shape, s.dtype) for s in srcs] + [pltpu.HBM(p.shape, p.dtype) for p in lands],
        input_output_aliases={i: i for i in range(n + m)},
        compiler_params=pltpu.CompilerParams(has_side_effects=pltpu.SideEffectType.DATAFLOW_SIDE_EFFECTING),
    )(*srcs, *lands, send_sems, recv_sems, *after)
    return outs[:n], outs[n:]


def _late_gather_copies(srcs, lands, send_sems, recv_sems):
    x, y, c = _my_pos()
    me = 2 * x + y
    copies = []
    for d in (1, 2, 3):
        to = (x ^ (d >> 1), y ^ (d & 1), c)
        for a in range(len(srcs)):
            hrows = srcs[a].shape[0] // 2
            rows = pl.ds(c * hrows, hrows)
            copies.append(pltpu.make_async_remote_copy(
                src_ref=srcs[a].at[rows, :], dst_ref=lands[a].at[me, rows, :], send_sem=send_sems.at[3 * a + d - 1],
                recv_sem=recv_sems.at[3 * a + d - 1], device_id=to, device_id_type=MESH))
    return copies


def _late_gather_start(shards, after, name):
    n = len(shards)

    def body(*refs):
        srcs, lands = refs[:n], refs[n:2 * n]
        send_sems, recv_sems = refs[2 * n + 1], refs[2 * n + 2]
        token = refs[-1]
        for cp in _late_gather_copies(srcs, lands, send_sems, recv_sems):
            cp.start()
        token[...] = jnp.zeros_like(token)

    hbm = pl.BlockSpec(memory_space=pltpu.HBM)
    sem = pl.BlockSpec(memory_space=pltpu.SEMAPHORE)
    outs = pl.pallas_call(
        body, name=name,
        in_specs=[hbm] * (2 * n) + [pl.BlockSpec(memory_space=pl.ANY)],
        out_specs=[sem, sem] + [hbm] * (2 * n) + [pl.BlockSpec(memory_space=pltpu.VMEM)],
        out_shape=[pltpu.SemaphoreType.DMA((3 * n,)), pltpu.SemaphoreType.DMA((3 * n,))]
        + [pltpu.HBM(s.shape, s.dtype) for s in shards]
        + [pltpu.HBM((4,) + s.shape, s.dtype) for s in shards]
        + [jax.ShapeDtypeStruct((SUBLANES, LANES), F32)],
        input_output_aliases={i: 2 + i for i in range(2 * n)},
        compiler_params=pltpu.CompilerParams(has_side_effects=pltpu.SideEffectType.DATAFLOW_SIDE_EFFECTING),
    )(*[pltpu.with_memory_space_constraint(s, pltpu.HBM) for s in shards],
      *[pltpu.with_memory_space_constraint(lax.empty((4,) + s.shape, s.dtype), pltpu.HBM) for s in shards], after)
    return outs[0], outs[1], outs[2:2 + n], outs[2 + n:2 + 2 * n], outs[-1]


def _late_gather_wait(send_sems, recv_sems, shards, lands, after, name):
    n = len(shards)

    def body(*refs):
        src_refs, land_refs = refs[:n], refs[n:2 * n]
        ssem, rsem = refs[2 * n], refs[2 * n + 1]
        for cp in _late_gather_copies(src_refs, land_refs, ssem, rsem):
            cp.wait_send()
            cp.wait_recv()

    hbm = pl.BlockSpec(memory_space=pltpu.HBM)
    sem = pl.BlockSpec(memory_space=pltpu.SEMAPHORE)
    outs = pl.pallas_call(
        body, name=name,
        in_specs=[hbm] * (2 * n) + [sem, sem] + [pl.BlockSpec(memory_space=pl.ANY)] * len(after),
        out_specs=[hbm] * (2 * n),
        out_shape=[pltpu.HBM(s.shape, s.dtype) for s in shards] + [pltpu.HBM(p.shape, p.dtype) for p in lands],
        input_output_aliases={i: i for i in range(2 * n)},
        compiler_params=pltpu.CompilerParams(has_side_effects=pltpu.SideEffectType.DATAFLOW_SIDE_EFFECTING),
    )(*shards, *lands, send_sems, recv_sems, *after)
    return outs[n:]


def _late_gather_pair(lands, name):
    n = len(lands)

    def body(*refs):
        outs = refs[n:2 * n]
        send_sems, recv_sems = refs[2 * n:]
        x, y, c = _my_pos()

        def copy(a, d, half):
            chip = 2 * (x ^ (d >> 1)) + (y ^ (d & 1))
            hrows = lands[a].shape[1] // 2
            sl = outs[a].at[chip, pl.ds(half * hrows, hrows), :]
            return pltpu.make_async_remote_copy(src_ref=sl, dst_ref=sl, send_sem=send_sems.at[3 * a + d - 1],
                                                recv_sem=recv_sems.at[3 * a + d - 1], device_id=(x, y, 1 - c),
                                                device_id_type=MESH)

        pairs = [(a, d) for d in (1, 2, 3) for a in range(n)]
        for a, d in pairs:
            copy(a, d, c).start()
        for a, d in pairs:
            copy(a, d, c).wait_send()
            copy(a, d, 1 - c).wait_recv()

    hbm = pl.BlockSpec(memory_space=pltpu.HBM)
    return pl.pallas_call(
        body, name=name, in_specs=[hbm] * n, out_specs=[hbm] * n,
        out_shape=[jax.ShapeDtypeStruct(p.shape, p.dtype) for p in lands],
        input_output_aliases={i: i for i in range(n)},
        scratch_shapes=[pltpu.SemaphoreType.DMA((3 * n,)), pltpu.SemaphoreType.DMA((3 * n,))],
    )(*lands)


def _pair_exchange(gs):
    n = len(gs)

    def body(*refs):
        ins, outs = refs[:n], refs[n:2 * n]
        send_sems, recv_sems = refs[2 * n:]
        x, y, c = _my_pos()
        sent = []
        for a in range(n):
            hrows = gs[a].shape[1] // 2
            cp = pltpu.make_async_remote_copy(
                src_ref=ins[a].at[:, pl.ds((1 - c) * hrows, hrows), :], dst_ref=outs[a], send_sem=send_sems.at[a],
                recv_sem=recv_sems.at[a], device_id=(x, y, 1 - c), device_id_type=MESH)
            cp.start()
            sent.append(cp)
        for cp in sent:
            cp.wait()

    hbm = pl.BlockSpec(memory_space=pltpu.HBM)
    return pl.pallas_call(
        body, name="grad_pair_exchange", in_specs=[hbm] * n, out_specs=[hbm] * n,
        out_shape=[jax.ShapeDtypeStruct((g.shape[0], g.shape[1] // 2, g.shape[2]), g.dtype) for g in gs],
        scratch_shapes=[pltpu.SemaphoreType.DMA((n,)), pltpu.SemaphoreType.DMA((n,))],
    )(*gs)


def _pair_add(g, got, c_arr, name):
    nk, rows2, cols = g.shape
    hrows = rows2 // 2
    tr = _blk(hrows, 256, 2 * SUBLANES)
    nb = hrows // tr

    def body(c_ref, a_ref, b_ref, o_ref):
        o_ref[...] = (a_ref[...].astype(F32) + b_ref[...].astype(F32)).astype(o_ref.dtype)

    return pl.pallas_call(
        body, name=name,
        grid_spec=pltpu.PrefetchScalarGridSpec(
            num_scalar_prefetch=1, grid=(nk, nb),
            in_specs=[pl.BlockSpec((1, tr, cols), lambda k, i, c_ref: (k, c_ref[0] * nb + i, 0)),
                      pl.BlockSpec((1, tr, cols), lambda k, i, c_ref: (k, i, 0))],
            out_specs=pl.BlockSpec((1, tr, cols), lambda k, i, c_ref: (k, i, 0))),
        out_shape=jax.ShapeDtypeStruct((nk, hrows, cols), g.dtype),
        compiler_params=pltpu.CompilerParams(dimension_semantics=("parallel", "parallel")),
    )(c_arr, g, got)


def _chip_scatter_copies(srcs, lands, send_sems, recv_sems):
    x, y, c = _my_pos()
    copies = []
    for d in (1, 2, 3):
        tx, ty = x ^ (d >> 1), y ^ (d & 1)
        for a in range(len(srcs)):
            copies.append(pltpu.make_async_remote_copy(
                src_ref=srcs[a].at[2 * tx + ty], dst_ref=lands[a].at[d - 1], send_sem=send_sems.at[3 * a + d - 1],
                recv_sem=recv_sems.at[3 * a + d - 1], device_id=(tx, ty, c), device_id_type=MESH))
    return copies


def _chip_scatter_start(pss):
    n = len(pss)

    def body(*refs):
        srcs, lands = refs[:n], refs[n:2 * n]
        send_sems, recv_sems = refs[2 * n], refs[2 * n + 1]
        token = refs[-1]
        for cp in _chip_scatter_copies(srcs, lands, send_sems, recv_sems):
            cp.start()
        token[...] = jnp.zeros_like(token)

    hbm = pl.BlockSpec(memory_space=pltpu.HBM)
    sem = pl.BlockSpec(memory_space=pltpu.SEMAPHORE)
    land_shapes = [(3,) + p.shape[1:] for p in pss]
    outs = pl.pallas_call(
        body, name="grad_chip_scatter_start",
        in_specs=[hbm] * (2 * n),
        out_specs=[sem, sem] + [hbm] * (2 * n) + [pl.BlockSpec(memory_space=pltpu.VMEM)],
        out_shape=[pltpu.SemaphoreType.DMA((3 * n,)), pltpu.SemaphoreType.DMA((3 * n,))]
        + [pltpu.HBM(p.shape, p.dtype) for p in pss]
        + [pltpu.HBM(s, p.dtype) for s, p in zip(land_shapes, pss)]
        + [jax.ShapeDtypeStruct((SUBLANES, LANES), F32)],
        input_output_aliases={i: 2 + i for i in range(2 * n)},
        compiler_params=pltpu.CompilerParams(has_side_effects=pltpu.SideEffectType.DATAFLOW_SIDE_EFFECTING),
    )(*[pltpu.with_memory_space_constraint(p, pltpu.HBM) for p in pss],
      *[pltpu.with_memory_space_constraint(lax.empty(s, p.dtype), pltpu.HBM) for s, p in zip(land_shapes, pss)])
    return outs[0], outs[1], outs[2:2 + n], outs[2 + n:2 + 2 * n], outs[-1]


def _chip_scatter_wait(send_sems, recv_sems, srcs, lands, after):
    n = len(srcs)

    def body(*refs):
        src_refs, land_refs = refs[:n], refs[n:2 * n]
        ssem, rsem = refs[2 * n], refs[2 * n + 1]
        for cp in _chip_scatter_copies(src_refs, land_refs, ssem, rsem):
            cp.wait_send()
            cp.wait_recv()

    hbm = pl.BlockSpec(memory_space=pltpu.HBM)
    sem = pl.BlockSpec(memory_space=pltpu.SEMAPHORE)
    outs = pl.pallas_call(
        body, name="grad_chip_scatter_wait",
        in_specs=[hbm] * (2 * n) + [sem, sem, pl.BlockSpec(memory_space=pl.ANY)],
        out_specs=[hbm] * (2 * n),
        out_shape=[pltpu.HBM(p.shape, p.dtype) for p in srcs] + [pltpu.HBM(p.shape, p.dtype) for p in lands],
        input_output_aliases={i: i for i in range(2 * n)},
        compiler_params=pltpu.CompilerParams(has_side_effects=pltpu.SideEffectType.DATAFLOW_SIDE_EFFECTING),
    )(*srcs, *lands, send_sems, recv_sems, after)
    return outs[:n], outs[n:]


def _chip_sum(ps, got, me_arr, name):
    _, hrows, cols = ps.shape
    tr = _blk(hrows, 256, 2 * SUBLANES)

    def body(me_ref, p_ref, g_ref, o_ref):
        acc = p_ref[0].astype(F32)
        for s in range(3):
            acc = acc + g_ref[s].astype(F32)
        o_ref[...] = acc

    return pl.pallas_call(
        body, name=name,
        grid_spec=pltpu.PrefetchScalarGridSpec(
            num_scalar_prefetch=1, grid=(hrows // tr,),
            in_specs=[pl.BlockSpec((1, tr, cols), lambda i, me_ref: (me_ref[0], i, 0)),
                      pl.BlockSpec((3, tr, cols), lambda i, me_ref: (0, i, 0))],
            out_specs=pl.BlockSpec((tr, cols), lambda i, me_ref: (i, 0))),
        out_shape=jax.ShapeDtypeStruct((hrows, cols), F32),
        compiler_params=pltpu.CompilerParams(dimension_semantics=("parallel",)),
    )(me_arr, ps, got)


def _pair_swap(halves):
    n = len(halves)

    def body(*refs):
        ins, outs = refs[:n], refs[n:2 * n]
        send_sems, recv_sems = refs[2 * n:]
        x, y, c = _my_pos()
        sent = []
        for a in range(n):
            cp = pltpu.make_async_remote_copy(src_ref=ins[a], dst_ref=outs[a], send_sem=send_sems.at[a], recv_sem=recv_sems.at[a],
                                              device_id=(x, y, 1 - c), device_id_type=MESH)
            cp.start()
            sent.append(cp)
        for cp in sent:
            cp.wait()

    hbm = pl.BlockSpec(memory_space=pltpu.HBM)
    return pl.pallas_call(
        body, name="grad_pair_swap", in_specs=[hbm] * n, out_specs=[hbm] * n,
        out_shape=[jax.ShapeDtypeStruct(h.shape, h.dtype) for h in halves],
        scratch_shapes=[pltpu.SemaphoreType.DMA((n,)), pltpu.SemaphoreType.DMA((n,))],
    )(*halves)


def _adamw_sharded(w, g_own, g_other, m, v, c_arr, after, name):
    R, C = w.shape
    hrows = R // 2
    tr = _blk(hrows, 256, SUBLANES)
    nbh = hrows // tr

    def body(c_ref, w_ref, go_ref, gx_ref, m_ref, v_ref, _after_ref, g_ref, d_ref, nm_ref, nv_ref):
        mine = (pl.program_id(0) // nbh) == c_ref[0]
        g_ = jnp.where(mine, go_ref[...], gx_ref[...])
        g_ref[...] = g_
        d_ref[...], nm_ref[...], nv_ref[...] = _adamw_math(w_ref[...], g_, m_ref[...], v_ref[...])

    blk = pl.BlockSpec((tr, C), lambda i, c_ref: (i, 0))
    hblk = pl.BlockSpec((tr, C), lambda i, c_ref: (i % nbh, 0))
    sd = jax.ShapeDtypeStruct((R, C), F32)
    return pl.pallas_call(
        body, name=name,
        grid_spec=pltpu.PrefetchScalarGridSpec(
            num_scalar_prefetch=1, grid=(2 * nbh,),
            in_specs=[blk, hblk, hblk, blk, blk, pl.BlockSpec(memory_space=pl.ANY)], out_specs=[blk] * 4),
        out_shape=[sd] * 4,
        compiler_params=pltpu.CompilerParams(dimension_semantics=("parallel",)),
    )(c_arr, w, g_own, g_other, m, v, after)


def _ar_piece(ref, rows, p):
    start = p * rows
    if rows % SUBLANES == 0:
        start = pl.multiple_of(start, SUBLANES)
    return ref.at[..., pl.ds(start, rows), :]


def _ar_peer(d):
    x, y, c = _my_pos()
    return (x ^ (d >> 2), y ^ ((d >> 1) & 1), c ^ (d & 1))


def _ar_lin(p):
    return 4 * p[0] + 2 * p[1] + p[2]


def _ar_scatter_copies(rows):
    def make(srcs, lands, send_sems, recv_sems):
        n = len(srcs)
        copies = []
        for d in range(1, 8):
            to = _ar_peer(d)
            for a in range(n):
                copies.append(pltpu.make_async_remote_copy(
                    src_ref=_ar_piece(srcs[a], rows[a], _ar_lin(to)), dst_ref=lands[a].at[d],
                    send_sem=send_sems.at[(d - 1) * n + a], recv_sem=recv_sems.at[(d - 1) * n + a], device_id=to,
                    device_id_type=MESH))
        return copies
    return make


def _ar_gather_copies(rows):
    def make(srcs, lands, send_sems, recv_sems):
        n = len(srcs)
        me = _ar_lin(_my_pos())
        copies = []
        for d in range(1, 8):
            for a in range(n):
                copies.append(pltpu.make_async_remote_copy(
                    src_ref=srcs[a], dst_ref=_ar_piece(lands[a], rows[a], me),
                    send_sem=send_sems.at[(d - 1) * n + a], recv_sem=recv_sems.at[(d - 1) * n + a], device_id=_ar_peer(d),
                    device_id_type=MESH))
        return copies
    return make


def _ar_sum(srcs, lands, rows):
    n = len(srcs)

    def body(*refs):
        me = _ar_lin(_my_pos())
        for a in range(n):
            acc = _ar_piece(refs[a], rows[a], me)[...]
            for d in range(1, 8):
                acc = acc + refs[n + a][d]
            refs[2 * n + a][...] = acc

    vm = pl.BlockSpec(memory_space=pltpu.VMEM)
    return pl.pallas_call(
        body, name="allreduce_sum", in_specs=[vm] * (2 * n), out_specs=[vm] * n,
        out_shape=[jax.ShapeDtypeStruct(p.shape[1:], F32) for p in lands],
    )(*srcs, *lands)


def kernel(x, pre_norm_w, w_in, s5_A_re, s5_A_im, s5_B_re, s5_B_im, s5_C_re, s5_C_im, s5_D, s5_log_dt, s5_glu_w, s5_glu_b, gla_gate_up, gla_gate_bias, gla_norm_w, w_out, post_norm_w, loss_target, m_pre_norm_w, m_w_in, m_s5_A_re, m_s5_A_im, m_s5_B_re, m_s5_B_im, m_s5_C_re, m_s5_C_im, m_s5_D, m_s5_log_dt, m_s5_glu_w, m_s5_glu_b, m_gla_gate_up, m_gla_gate_bias, m_gla_norm_w, m_w_out, m_post_norm_w, v_pre_norm_w, v_w_in, v_s5_A_re, v_s5_A_im, v_s5_B_re, v_s5_B_im, v_s5_C_re, v_s5_C_im, v_s5_D, v_s5_log_dt, v_s5_glu_w, v_s5_glu_b, v_gla_gate_up, v_gla_gate_bias, v_gla_norm_w, v_w_out, v_post_norm_w):
    names = ["pre_norm_w", "w_in", "s5_A_re", "s5_A_im", "s5_B_re", "s5_B_im", "s5_C_re", "s5_C_im", "s5_D", "s5_log_dt",
             "s5_glu_w", "s5_glu_b", "gla_gate_up", "gla_gate_bias", "gla_norm_w", "w_out", "post_norm_w"]
    W = dict(zip(names, (pre_norm_w, w_in, s5_A_re, s5_A_im, s5_B_re, s5_B_im, s5_C_re, s5_C_im, s5_D, s5_log_dt,
                         s5_glu_w, s5_glu_b, gla_gate_up, gla_gate_bias, gla_norm_w, w_out, post_norm_w)))
    M = dict(zip(names, (m_pre_norm_w, m_w_in, m_s5_A_re, m_s5_A_im, m_s5_B_re, m_s5_B_im, m_s5_C_re, m_s5_C_im, m_s5_D,
                         m_s5_log_dt, m_s5_glu_w, m_s5_glu_b, m_gla_gate_up, m_gla_gate_bias, m_gla_norm_w, m_w_out,
                         m_post_norm_w)))
    V = dict(zip(names, (v_pre_norm_w, v_w_in, v_s5_A_re, v_s5_A_im, v_s5_B_re, v_s5_B_im, v_s5_C_re, v_s5_C_im, v_s5_D,
                         v_s5_log_dt, v_s5_glu_w, v_s5_glu_b, v_gla_gate_up, v_gla_gate_bias, v_gla_norm_w, v_w_out,
                         v_post_norm_w)))
    sharded = ("w_in", "s5_glu_w", "w_out", "gla_gate_up")

    xb = x[0]
    tgt = loss_target[0]
    L, D = xb.shape
    DS = D // 2
    G = DS // S5_GROUP
    P = S5_STATE
    NB = DS // S5_COLS
    DV = D - DS
    DK = DV // 2
    WM = 2 * DS + 2 * DK + 2 * DV
    nsh = w_in.shape[2]

    chip = 2 * lax.axis_index("x") + lax.axis_index("y")
    own = [jnp.pad(w_in[0].astype(BF16), ((0, 0), (0, -nsh % LANES))), s5_glu_w[0].astype(BF16),
           w_out[0].astype(BF16), gla_gate_up[0]]
    fill = lambda g, o: lax.dynamic_update_index_in_dim(g, o, chip, 0)
    win_ss, win_rs, win_src, win_lands, win_token = _late_gather_start(own[:1], pre_norm_w, "w_in_gather_start")
    h = _prenorm_fwd(xb, pre_norm_w, win_token)

    b_view = lambda t: jnp.transpose(t[0], (0, 2, 1)).reshape(G * S5_GROUP, P)
    b_back = lambda t: jnp.transpose(t.reshape(G, S5_GROUP, P), (0, 2, 1))[None]
    c_view = lambda t: t[0].reshape(G * S5_GROUP, P)
    c_back = lambda t: t.reshape(1, G, S5_GROUP, P)
    small = ["pre_norm_w", "post_norm_w", "s5_D", "s5_glu_b", "gla_gate_bias", "gla_norm_w", "s5_log_dt",
             "s5_A_re", "s5_A_im", "s5_B_re", "s5_B_im", "s5_C_re", "s5_C_im"]
    view = {n: (lambda t: t) for n in small[:7]}
    back = dict(view)
    view.update(s5_A_re=lambda t: t[0], s5_A_im=lambda t: t[0], s5_B_re=b_view, s5_B_im=b_view, s5_C_re=c_view, s5_C_im=c_view)
    back.update(s5_A_re=lambda t: t[None], s5_A_im=lambda t: t[None], s5_B_re=b_back, s5_B_im=b_back, s5_C_re=c_back,
                s5_C_im=c_back)
    Wv = {n: view[n](W[n]) for n in small}
    bbd_re, bbd_im, ct_re, ct_im, tab, ptab = _s5_prep_fwd(
        Wv["s5_A_re"], Wv["s5_A_im"], s5_log_dt, Wv["s5_B_re"], Wv["s5_B_im"], Wv["s5_C_re"], Wv["s5_C_im"],
        h, _blk(L, 512, SUBLANES) // SUBLANES)
    dvec = s5_D

    for d_ in (W, M, V):
        d_["w_in"], _ = lax.optimization_barrier((d_["w_in"], win_token))
    g_win = _late_gather_wait(win_ss, win_rs, win_src, win_lands,
                              [tab, W["w_in"][0], M["w_in"][0], V["w_in"][0]], "w_in_gather_wait")
    g_win = fill(_late_gather_pair(g_win, "w_in_gather_pair")[0], own[0])
    w_main, w_low = _assemble_w_in(g_win, nsh, WM)
    late_ss, late_rs, late_src, late_lands, late_token = _late_gather_start(own[1:], g_win, "late_gather_start")
    proj_main, proj_low = _in_proj(h, w_main, w_low, late_token)
    y_pre, s_re, s_im = _s5_scan_fwd(proj_main, bbd_re, bbd_im, ct_re, ct_im, dvec, tab, ptab, DS)
    late = _late_gather_wait(late_ss, late_rs, late_src, late_lands, [y_pre], "late_gather_wait")
    late = _late_gather_pair(late, "late_gather_pair")
    g_glu, g_wout, g_gup = [fill(g, o) for g, o in zip(late, own[1:])]
    glu_w = g_glu.reshape(DS, DS)
    wout = g_wout.reshape(D, D)
    gup = jnp.moveaxis(g_gup, 0, 1).reshape(GLA_RANK, DK)
    gup_pad = jnp.pad(gup, ((0, LANES - GLA_RANK), (0, 0))).astype(BF16)
    ycat, t_pre = _s5_post_fwd(y_pre, proj_main, glu_w, s5_glu_b, DS)
    ycat, s_prev = _gla_fwd(proj_main, proj_low, gup_pad, gla_gate_bias, gla_norm_w, ycat, DS, DK, DV)
    mixed = _mm(ycat, wout, name="out_proj")
    loss11, d_mixed, dout, g_post_w = _post_fwd_bwd(mixed, xb, tgt, post_norm_w)

    d_ycat = _mm(d_mixed, wout, tb=True, name="out_proj_dx")
    g_wout_full = _mm(ycat, d_mixed, ta=True, out_dtype=BF16, name="out_proj_dw")
    d_ypre, d_s5, d_t, y1, g_glu_b = _s5_post_bwd(d_ycat, y_pre, proj_main, t_pre, glu_w, DS)
    g_glu_full = _mm(y1, d_t, ta=True, out_dtype=BF16, name="glu_dw")
    d_s5, g_D, gct_re, gct_im, gbbd_re, gbbd_im, gab_re, gab_im = _s5_scan_bwd(
        d_ypre, proj_main, s_re, s_im, bbd_re, bbd_im, ct_re, ct_im, dvec, tab, ptab, d_s5, DS)
    d_gla, d_a, g_norm_w, g_gate_bias = _gla_bwd(
        d_ycat, proj_main, proj_low, s_prev, gup_pad, gla_gate_bias, gla_norm_w, DS, DK, DV)
    d_low = _mm(d_a, gup_pad, tb=True, out_dtype=BF16, name="gate_dx")
    g_gup_pad = _mm(proj_low, d_a, ta=True, name="gate_dw")
    g_wmain, g_wlow = _in_proj_dw(h, d_s5, d_gla, d_low)

    g_win_sh = jnp.stack([g_wmain[:, :nsh], g_wmain[:, nsh:2 * nsh], g_wmain[:, 2 * nsh:3 * nsh],
                          jnp.concatenate([g_wmain[:, 3 * nsh:], g_wlow[:, :GLA_RANK]], axis=1)])
    gs = [g_win_sh,
          g_glu_full.reshape(4, DS // 4, DS),
          g_wout_full.reshape(4, D // 4, D),
          jnp.moveaxis(g_gup_pad[:GLA_RANK].reshape(GLA_RANK, 4, DK // 4), 1, 0)]
    c_arr = lax.axis_index("c").astype(jnp.int32).reshape(1)
    me_arr = chip.astype(jnp.int32).reshape(1)
    got = _pair_exchange(gs)
    pss = [_pair_add(g, r, c_arr, "grad_pair_add_" + n) for n, g, r in zip(sharded, gs, got)]
    send_sems, recv_sems, pss, lands, token = _chip_scatter_start(pss)

    dh = _in_proj_dx(d_s5, d_gla, d_low, w_main, w_low, token)
    grad_x, g_pre_w = _prenorm_bwd(xb, dh, dout, pre_norm_w)

    g_a, g_bc, g_ldt = _s5_prep_bwd(Wv["s5_A_re"], Wv["s5_A_im"], s5_log_dt, Wv["s5_B_re"], Wv["s5_B_im"],
                                    gbbd_re, gbbd_im, gct_re, gct_im, gab_re, gab_im)

    g_vecs = jnp.concatenate([g_pre_w, g_post_w, g_D, g_glu_b, g_gate_bias, g_norm_w, g_ldt, loss11], axis=1)
    loss_at = g_vecs.shape[1] - 1
    lanes_pad = -g_vecs.shape[1] % (8 * SUBLANES * LANES)
    g_vecs = jnp.pad(g_vecs, ((0, 0), (0, lanes_pad))).reshape(-1, LANES)
    ar_srcs = [g_vecs, g_a, g_bc]
    ar_rows = [a.shape[-2] // 8 for a in ar_srcs]
    ar_lands = [jax.ShapeDtypeStruct((8,) + a.shape[:-2] + (r, a.shape[-1]), F32) for a, r in zip(ar_srcs, ar_rows)]
    ar_ss, ar_rs, ar_srcs, ar_got, ar_token = _split_start(
        "allreduce_scatter_start", ar_srcs, ar_lands, _ar_scatter_copies(ar_rows), 7 * len(ar_srcs), [])

    pss, rcv = _chip_scatter_wait(send_sems, recv_sems, pss, lands, ar_token)
    halves = [_chip_sum(p, r, me_arr, "grad_chip_sum_" + n) for n, p, r in zip(sharded, pss, rcv)]
    others = _pair_swap(halves)
    ar_srcs, ar_got = _split_wait("allreduce_scatter_wait", ar_ss, ar_rs, ar_srcs, ar_got, _ar_scatter_copies(ar_rows),
                                  [others[0]])
    ar_red = _ar_sum(ar_srcs, ar_got, ar_rows)
    ag_ss, ag_rs, ar_red, ag_full, ag_token = _split_start(
        "allreduce_gather_start", ar_red, [jax.ShapeDtypeStruct(a.shape, F32) for a in ar_srcs],
        _ar_gather_copies(ar_rows), 7 * len(ar_red), [])
    G_out, D_out, M_out, V_out = {}, {}, {}, {}
    for n, g_own, g_other in zip(sharded, halves, others):
        g_, d_, m_, v_ = _adamw_sharded(W[n][0], g_own, g_other, M[n][0], V[n][0], c_arr, ag_token, "adamw_" + n)
        G_out[n], D_out[n], M_out[n], V_out[n] = g_[None], d_[None], m_[None], v_[None]
    ar_red, ag_full = _split_wait("allreduce_gather_wait", ag_ss, ag_rs, ar_red, ag_full, _ar_gather_copies(ar_rows),
                                  [D_out[n] for n in sharded])
    me8 = 2 * chip + lax.axis_index("c")
    r_vecs, r_a, r_bc = [lax.dynamic_update_slice_in_dim(f, r, me8 * rw, axis=f.ndim - 2)
                         for f, r, rw in zip(ag_full, ar_red, ar_rows)]
    r_vecs = r_vecs.reshape(1, -1)
    loss = r_vecs[0, loss_at]
    outs4 = _adamw_small(r_vecs, r_a, r_bc, [Wv[n] for n in small],
                         [view[n](M[n]) for n in small], [view[n](V[n]) for n in small])
    for store, o in zip((G_out, D_out, M_out, V_out), outs4):
        store.update({n: back[n](t) for n, t in zip(small, o)})

    return (loss, grad_x[None], *[G_out[n] for n in names], *[D_out[n] for n in names],
            *[M_out[n] for n in names], *[V_out[n] for n in names])
```

```python
import functools
import math

import jax
import jax.numpy as jnp
from jax import lax
from jax.experimental import pallas as pl
from jax.experimental.pallas import tpu as pltpu

F32 = jnp.float32
BF16 = jnp.bfloat16
HI = lax.Precision.HIGHEST
MESH = pl.DeviceIdType.MESH

EPS = 1e-6
S5_GROUP = 16
S5_STATE = 64
GLA_HK = 128
GLA_HV = 256
GLA_RANK = 16
GLA_TAU = 16.0
GLA_CHUNK = 64
GLA_STEP_CHUNKS = 4
LANES = 128
SUBLANES = 8
S5_COLS = 128
S5_LANES = (S5_COLS // S5_GROUP) * S5_STATE

ADAM_LR = 0.001
ADAM_B1 = 0.9
ADAM_B2 = 0.999
ADAM_EPS = 1e-08
ADAM_WD = 0.01
ADAM_STEP = 10

GELU_K = math.sqrt(2.0 / math.pi)
GELU_C = 0.044715


def _blk(n, pref, unit=LANES):
    best = None
    b = unit
    while b <= min(n, pref):
        if n % b == 0:
            best = b
        b += unit
    return best if best is not None else n


def _dot(a, b, dn=(((1,), (0,)), ((), ()))):
    return lax.dot_general(a.astype(BF16), b.astype(BF16), dn, preferred_element_type=F32)


def _dot_hi(a, b, dn=(((1,), (0,)), ((), ()))):
    return lax.dot_general(a, b, dn, precision=HI, preferred_element_type=F32)


NN = (((1,), (0,)), ((), ()))
NT = (((1,), (1,)), ((), ()))
TN = (((0,), (0,)), ((), ()))


def _sigmoid(x):
    return 1.0 / (1.0 + jnp.exp(-x))


def _gelu(y):
    return 0.5 * y * (1.0 + jnp.tanh(GELU_K * (y + GELU_C * y * y * y)))


def _gelu_grad(y):
    th = jnp.tanh(GELU_K * (y + GELU_C * y * y * y))
    return 0.5 * (1.0 + th) + 0.5 * y * (1.0 - th * th) * GELU_K * (1.0 + 3.0 * GELU_C * y * y)


def _mm(a, b, *, name, ta=False, tb=False, out_dtype=F32, bm=1024, bn=1024, bk=2048):
    if ta:
        K, M = a.shape
    else:
        M, K = a.shape
    if tb:
        N, K2 = b.shape
    else:
        K2, N = b.shape
    assert K == K2, (a.shape, b.shape, ta, tb)
    bm, bn, bk = _blk(M, bm), _blk(N, bn), _blk(K, bk)
    nk = K // bk
    dn = (((0 if ta else 1,), (1 if tb else 0,)), ((), ()))

    def body(a_ref, b_ref, o_ref, *acc):
        if nk == 1:
            o_ref[...] = _dot(a_ref[...], b_ref[...], dn).astype(out_dtype)
            return
        acc_ref, = acc
        k = pl.program_id(2)

        @pl.when(k == 0)
        def _():
            acc_ref[...] = jnp.zeros_like(acc_ref)

        acc_ref[...] += _dot(a_ref[...], b_ref[...], dn)

        @pl.when(k == nk - 1)
        def _():
            o_ref[...] = acc_ref[...].astype(out_dtype)

    a_spec = pl.BlockSpec((bk, bm), lambda i, j, k: (k, i)) if ta else pl.BlockSpec((bm, bk), lambda i, j, k: (i, k))
    b_spec = pl.BlockSpec((bn, bk), lambda i, j, k: (j, k)) if tb else pl.BlockSpec((bk, bn), lambda i, j, k: (k, j))
    return pl.pallas_call(
        body,
        name=name,
        grid=(M // bm, N // bn, nk),
        in_specs=[a_spec, b_spec],
        out_specs=pl.BlockSpec((bm, bn), lambda i, j, k: (i, j)),
        out_shape=jax.ShapeDtypeStruct((M, N), out_dtype),
        scratch_shapes=[pltpu.VMEM((bm, bn), F32)] if nk > 1 else [],
        compiler_params=pltpu.CompilerParams(dimension_semantics=("parallel", "parallel", "arbitrary")),
    )(a, b)


def _in_proj(h, w_main, w_low, after):
    M, K = h.shape
    N = w_main.shape[1]
    bm, bn = _blk(M, 1024), _blk(N, 1024)

    def body(h_ref, w_ref, wl_ref, _after_ref, o_ref, ol_ref):
        hv = h_ref[...]
        o_ref[...] = _dot(hv, w_ref[...])

        @pl.when(pl.program_id(1) == 0)
        def _():
            ol_ref[...] = _dot(hv, wl_ref[...])

    return pl.pallas_call(
        body, name="in_proj", grid=(M // bm, N // bn),
        in_specs=[pl.BlockSpec((bm, K), lambda i, j: (i, 0)), pl.BlockSpec((K, bn), lambda i, j: (0, j)),
                  pl.BlockSpec((K, LANES), lambda i, j: (0, 0)), pl.BlockSpec(memory_space=pl.ANY)],
        out_specs=[pl.BlockSpec((bm, bn), lambda i, j: (i, j)), pl.BlockSpec((bm, LANES), lambda i, j: (i, 0))],
        out_shape=[jax.ShapeDtypeStruct((M, N), F32), jax.ShapeDtypeStruct((M, LANES), F32)],
        compiler_params=pltpu.CompilerParams(dimension_semantics=("parallel", "arbitrary")),
    )(h, w_main, w_low, after)


def _in_proj_dx(a1, a2, al, b, bl, after, *, bm=1024, bn=1024, bk=2048):
    M, K1 = a1.shape
    K2 = a2.shape[1]
    N = b.shape[0]
    bm, bn = _blk(M, bm), _blk(N, bn)
    bk = _blk(math.gcd(K1, K2), bk)
    nk1, nk = K1 // bk, (K1 + K2) // bk

    def body(a1_ref, a2_ref, al_ref, b_ref, bl_ref, _after_ref, o_ref, acc_ref):
        k = pl.program_id(2)

        @pl.when(k == 0)
        def _():
            acc_ref[...] = _dot(al_ref[...], bl_ref[...], NT)

        @pl.when(k < nk1)
        def _():
            acc_ref[...] += _dot(a1_ref[...], b_ref[...], NT)

        @pl.when(k >= nk1)
        def _():
            acc_ref[...] += _dot(a2_ref[...], b_ref[...], NT)

        @pl.when(k == nk - 1)
        def _():
            o_ref[...] = acc_ref[...]

    return pl.pallas_call(
        body, name="in_proj_dx", grid=(M // bm, N // bn, nk),
        in_specs=[pl.BlockSpec((bm, bk), lambda i, j, k: (i, jnp.minimum(k, nk1 - 1))),
                  pl.BlockSpec((bm, bk), lambda i, j, k: (i, jnp.maximum(k - nk1, 0))),
                  pl.BlockSpec((bm, LANES), lambda i, j, k: (i, 0)),
                  pl.BlockSpec((bn, bk), lambda i, j, k: (j, k)),
                  pl.BlockSpec((bn, LANES), lambda i, j, k: (j, 0)),
                  pl.BlockSpec(memory_space=pl.ANY)],
        out_specs=pl.BlockSpec((bm, bn), lambda i, j, k: (i, j)),
        out_shape=jax.ShapeDtypeStruct((M, N), F32),
        scratch_shapes=[pltpu.VMEM((bm, bn), F32)],
        compiler_params=pltpu.CompilerParams(dimension_semantics=("parallel", "parallel", "arbitrary")),
    )(a1, a2, al, b, bl, after)


def _in_proj_dw(a, b1, b2, bl, *, bm=1024, bn=1024, bk=2048):
    K, M = a.shape
    N1, N2 = b1.shape[1], b2.shape[1]
    bm, bk = _blk(M, bm), _blk(K, bk)
    bn = _blk(math.gcd(N1, N2), bn)
    nj1, nj = N1 // bn, (N1 + N2) // bn
    nk = K // bk

    def body(a_ref, b1_ref, b2_ref, bl_ref, o_ref, ol_ref, acc_ref, accl_ref):
        j = pl.program_id(1)
        k = pl.program_id(2)

        @pl.when(k == 0)
        def _():
            acc_ref[...] = jnp.zeros_like(acc_ref)

        @pl.when(j < nj1)
        def _():
            acc_ref[...] += _dot(a_ref[...], b1_ref[...], TN)

        @pl.when(j >= nj1)
        def _():
            acc_ref[...] += _dot(a_ref[...], b2_ref[...], TN)

        @pl.when(k == nk - 1)
        def _():
            o_ref[...] = acc_ref[...].astype(BF16)

        @pl.when(j == 0)
        def _():
            low = _dot(a_ref[...], bl_ref[...], TN)

            @pl.when(k == 0)
            def _():
                accl_ref[...] = low

            @pl.when(k > 0)
            def _():
                accl_ref[...] += low

            @pl.when(k == nk - 1)
            def _():
                ol_ref[...] = accl_ref[...].astype(BF16)

    return pl.pallas_call(
        body, name="in_proj_dw", grid=(M // bm, nj, nk),
        in_specs=[pl.BlockSpec((bk, bm), lambda i, j, k: (k, i)),
                  pl.BlockSpec((bk, bn), lambda i, j, k: (jnp.where(j < nj1, k, nk - 1), jnp.minimum(j, nj1 - 1))),
                  pl.BlockSpec((bk, bn), lambda i, j, k: (jnp.where(j >= nj1, k, 0), jnp.maximum(j - nj1, 0))),
                  pl.BlockSpec((bk, LANES), lambda i, j, k: (jnp.where(j == 0, k, nk - 1), 0))],
        out_specs=[pl.BlockSpec((bm, bn), lambda i, j, k: (i, j)), pl.BlockSpec((bm, LANES), lambda i, j, k: (i, 0))],
        out_shape=[jax.ShapeDtypeStruct((M, N1 + N2), BF16), jax.ShapeDtypeStruct((M, LANES), BF16)],
        scratch_shapes=[pltpu.VMEM((bm, bn), F32), pltpu.VMEM((bm, LANES), F32)],
        compiler_params=pltpu.CompilerParams(dimension_semantics=("parallel", "arbitrary", "arbitrary")),
    )(a, b1, b2, bl)


def _assemble_w_in(g, nsh, wm):
    _, R, nshp = g.shape
    nb_in = nshp // LANES
    nb_main = wm // LANES
    tr = _blk(R, 512, 2 * SUBLANES)
    plan = []
    for b in range(nb_main + 1):
        terms = []
        for k in range(g.shape[0]):
            for i in range(nb_in):
                delta = nsh * k + LANES * i - LANES * b
                lo, hi = max(0, -delta), min(LANES, LANES - delta, nsh - LANES * i)
                if abs(delta) < LANES and hi > lo:
                    terms.append((k, i, delta))
        plan.append(terms)
    deltas = sorted({d for terms in plan for _, _, d in terms if d})

    def body(g_ref, wm_ref, wl_ref):
        src = _iota2((LANES, LANES), 0)
        dst = _iota2((LANES, LANES), 1)
        shift = {d: (dst - src == d).astype(BF16) for d in deltas}
        for b, terms in enumerate(plan):
            acc = None
            for k, i, d in terms:
                blk = g_ref[k, :, LANES * i:LANES * (i + 1)]
                t = _dot(blk, shift[d]) if d else blk.astype(F32)
                acc = t if acc is None else acc + t
            if b < nb_main:
                wm_ref[:, LANES * b:LANES * (b + 1)] = acc.astype(BF16)
            else:
                wl_ref[...] = acc.astype(BF16)

    return pl.pallas_call(
        body, name="assemble_w_in", grid=(R // tr,),
        in_specs=[pl.BlockSpec((g.shape[0], tr, nshp), lambda r: (0, r, 0))],
        out_specs=[pl.BlockSpec((tr, wm), lambda r: (r, 0)), pl.BlockSpec((tr, LANES), lambda r: (r, 0))],
        out_shape=[jax.ShapeDtypeStruct((R, wm), BF16), jax.ShapeDtypeStruct((R, LANES), BF16)],
        compiler_params=pltpu.CompilerParams(dimension_semantics=("parallel",)),
    )(g)


def _split_w_in_grad(g_main, g_low, nsh):
    R, wm = g_main.shape
    nb_main = wm // LANES
    nb_out = -(-nsh // LANES)
    tr = _blk(R, 512, 2 * SUBLANES)
    plan = {}
    for k in range(4):
        for i in range(nb_out):
            width = min(LANES, nsh - LANES * i)
            terms = []
            for b in range(nb_main + 1):
                delta = LANES * b - (nsh * k + LANES * i)
                lo, hi = max(0, delta), min(width, LANES + delta)
                if abs(delta) < LANES and hi > lo:
                    terms.append((b, delta))
            plan[k, i] = (width, terms)
    deltas = sorted({d for _, terms in plan.values() for _, d in terms if d})

    def body(gm_ref, gl_ref, o_ref):
        src = _iota2((LANES, LANES), 0)
        dst = _iota2((LANES, LANES), 1)
        shift = {d: (dst - src == d).astype(BF16) for d in deltas}
        for (k, i), (width, terms) in plan.items():
            acc = None
            for b, d in terms:
                blk = gm_ref[:, LANES * b:LANES * (b + 1)] if b < nb_main else gl_ref[...]
                t = _dot(blk, shift[d]) if d else blk.astype(F32)
                acc = t if acc is None else acc + t
            o_ref[k, :, LANES * i:LANES * i + width] = acc[:, :width].astype(BF16)

    return pl.pallas_call(
        body, name="split_w_in_grad", grid=(R // tr,),
        in_specs=[pl.BlockSpec((tr, wm), lambda r: (r, 0)), pl.BlockSpec((tr, LANES), lambda r: (r, 0))],
        out_specs=pl.BlockSpec((4, tr, nsh), lambda r: (0, r, 0)),
        out_shape=jax.ShapeDtypeStruct((4, R, nsh), BF16),
        compiler_params=pltpu.CompilerParams(dimension_semantics=("parallel",)),
    )(g_main, g_low)


def _prenorm_fwd(x, w, after):
    L, D = x.shape
    tr = _blk(L, 256, SUBLANES)

    def body(x_ref, w_ref, _after_ref, h_ref):
        xv = x_ref[...]
        r = lax.rsqrt(jnp.mean(xv * xv, axis=-1, keepdims=True) + EPS)
        h_ref[...] = (xv * r * w_ref[...]).astype(BF16)

    return pl.pallas_call(
        body, name="prenorm_fwd", grid=(L // tr,),
        in_specs=[pl.BlockSpec((tr, D), lambda i: (i, 0)), pl.BlockSpec((1, D), lambda i: (0, 0)),
                  pl.BlockSpec(memory_space=pl.ANY)],
        out_specs=pl.BlockSpec((tr, D), lambda i: (i, 0)),
        out_shape=jax.ShapeDtypeStruct((L, D), BF16),
        compiler_params=pltpu.CompilerParams(dimension_semantics=("parallel",)),
    )(x, w, after)


def _post_fwd_bwd(mixed, x, target, w):
    L, D = x.shape
    tr = _blk(L, 256, SUBLANES)
    nsteps = L // tr

    def body(mx_ref, x_ref, t_ref, w_ref, loss_ref, dm_ref, dout_ref, gw_ref, acc_ref):
        i = pl.program_id(0)

        @pl.when(i == 0)
        def _():
            acc_ref[...] = jnp.zeros_like(acc_ref)
            gw_ref[...] = jnp.zeros_like(gw_ref)

        mx = mx_ref[...]
        wv = w_ref[...]
        r = lax.rsqrt(jnp.mean(mx * mx, axis=-1, keepdims=True) + EPS)
        n = mx * r
        err = x_ref[...] + n * wv - t_ref[...]
        acc_ref[...] += jnp.sum(err * err, axis=0, keepdims=True)
        dout = err * (1.0 / D)
        dout_ref[...] = dout
        gw_ref[...] += jnp.sum(dout * n, axis=0, keepdims=True)
        dn = dout * wv
        dm_ref[...] = (r * (dn - n * jnp.mean(dn * n, axis=-1, keepdims=True))).astype(BF16)

        @pl.when(i == nsteps - 1)
        def _():
            loss_ref[...] = jnp.sum(acc_ref[...], axis=-1, keepdims=True) * (0.5 / D)

    row = pl.BlockSpec((tr, D), lambda i: (i, 0))
    vec = pl.BlockSpec((1, D), lambda i: (0, 0))
    return pl.pallas_call(
        body, name="post_fwd_bwd", grid=(nsteps,),
        in_specs=[row, row, row, vec],
        out_specs=[pl.BlockSpec((1, 1), lambda i: (0, 0)), row, row, vec],
        out_shape=[jax.ShapeDtypeStruct((1, 1), F32), jax.ShapeDtypeStruct((L, D), BF16),
                   jax.ShapeDtypeStruct((L, D), F32), jax.ShapeDtypeStruct((1, D), F32)],
        scratch_shapes=[pltpu.VMEM((1, D), F32)],
        compiler_params=pltpu.CompilerParams(dimension_semantics=("arbitrary",)),
    )(mixed, x, target, w)


def _prenorm_bwd(x, dh, dout, w):
    L, D = x.shape
    tr = _blk(L, 256, SUBLANES)

    def body(x_ref, a_ref, dout_ref, w_ref, gx_ref, gw_ref):
        i = pl.program_id(0)

        @pl.when(i == 0)
        def _():
            gw_ref[...] = jnp.zeros_like(gw_ref)

        xv = x_ref[...]
        r = lax.rsqrt(jnp.mean(xv * xv, axis=-1, keepdims=True) + EPS)
        n = xv * r
        dh = a_ref[...]
        gw_ref[...] += jnp.sum(dh * n, axis=0, keepdims=True)
        dn = dh * w_ref[...]
        gx_ref[...] = dout_ref[...] + r * (dn - n * jnp.mean(dn * n, axis=-1, keepdims=True))

    row = pl.BlockSpec((tr, D), lambda i: (i, 0))
    vec = pl.BlockSpec((1, D), lambda i: (0, 0))
    return pl.pallas_call(
        body, name="prenorm_bwd", grid=(L // tr,),
        in_specs=[row, row, row, vec],
        out_specs=[row, vec],
        out_shape=[jax.ShapeDtypeStruct((L, D), F32), jax.ShapeDtypeStruct((1, D), F32)],
        compiler_params=pltpu.CompilerParams(dimension_semantics=("arbitrary",)),
    )(x, dh, dout, w)


def _s5_disc(a_re_raw, a_im, dt):
    a_re = jnp.minimum(a_re_raw, -1e-4)
    mag = jnp.exp(a_re * dt)
    ph = a_im * dt
    ab_re = mag * jnp.cos(ph)
    ab_im = mag * jnp.sin(ph)
    inv_n = 1.0 / (a_re * a_re + a_im * a_im)
    ia_re = a_re * inv_n
    ia_im = -a_im * inv_n
    n_re = ab_re - 1.0
    f_re = n_re * ia_re - ab_im * ia_im
    f_im = n_re * ia_im + ab_im * ia_re
    return a_re, ab_re, ab_im, f_re, f_im, ia_re, ia_im


def _iota2(shape, dim):
    return lax.broadcasted_iota(jnp.int32, shape, dim)


def _group_mask(rows, rows_per_group):
    shift = rows_per_group.bit_length() - 1
    return (_iota2((rows, S5_LANES), 0) >> shift) == (_iota2((rows, S5_LANES), 1) >> (S5_STATE.bit_length() - 1))


def _lane_tiler(dtype):
    return ((_iota2((S5_STATE, S5_LANES), 1) & (S5_STATE - 1)) == _iota2((S5_STATE, S5_LANES), 0)).astype(dtype)


def _row_to_col(row, n):
    eye = (_iota2((n, n), 0) == _iota2((n, n), 1)).astype(F32)
    return jnp.sum(eye * row, axis=1, keepdims=True)


def _group_repeat(G):
    return ((_iota2((G * S5_GROUP, G), 0) >> (S5_GROUP.bit_length() - 1)) == _iota2((G * S5_GROUP, G), 1)).astype(F32)


S5_TABS = 18


def _s5_prep_fwd(a_re, a_im, log_dt, b_re, b_im, c_re, c_im, after, seg):
    G, P = a_re.shape
    nb = G * S5_GROUP // S5_COLS
    g8 = S5_COLS // S5_GROUP
    assert seg & (seg - 1) == 0, seg

    def body(are_ref, aim_ref, ldt_ref, bre_ref, bim_ref, cre_ref, cim_ref, _after_ref,
             bbre_ref, bbim_ref, ctre_ref, ctim_ref, tab_ref, pt_ref):
        dt = jnp.exp(_row_to_col(ldt_ref[...], G))
        _, ab_re, ab_im, f_re, f_im, _, _ = _s5_disc(are_ref[...], aim_ref[...], dt)
        rep = _group_repeat(G)
        fx_re = _dot_hi(rep, f_re)
        fx_im = _dot_hi(rep, f_im)
        br, bi = bre_ref[...], bim_ref[...]
        bb_re = fx_re * br - fx_im * bi
        bb_im = fx_re * bi + fx_im * br
        tile_bf = _lane_tiler(BF16)
        mask = _group_mask(S5_COLS, S5_GROUP)
        for jb in range(nb):
            rs = slice(jb * S5_COLS, (jb + 1) * S5_COLS)
            for src, dst in ((bb_re[rs], bbre_ref), (bb_im[rs], bbim_ref), (cre_ref[rs, :], ctre_ref), (cim_ref[rs, :], ctim_ref)):
                dst[jb] = jnp.where(mask, _dot(src, tile_bf), 0.0).astype(BF16)

        tile_f = _lane_tiler(F32)
        mask8 = _group_mask(g8, 1)
        row = _iota2((SUBLANES, S5_LANES), 0)
        slab = (SUBLANES, S5_LANES)
        cmul = lambda p, q: (p[0] * q[0] - p[1] * q[1], p[0] * q[1] + p[1] * q[0])
        for jb in range(nb):
            gs = slice(jb * g8, (jb + 1) * g8)

            def lanes(m):
                v = jnp.sum(jnp.where(mask8, _dot_hi(m[gs], tile_f), 0.0), axis=0, keepdims=True)
                return jnp.broadcast_to(v, slab)

            a1 = (lanes(ab_re), lanes(ab_im))
            tab_ref[jb, 0], tab_ref[jb, 1] = a1

            def powers(i, p):
                off = pl.multiple_of(i * SUBLANES, SUBLANES)
                pt_ref[jb, 0, pl.ds(off, SUBLANES), :] = p[0]
                pt_ref[jb, 1, pl.ds(off, SUBLANES), :] = p[1]
                return cmul(p, a1)

            lax.fori_loop(0, seg, powers, a1)
            aseg = a1
            for _ in range(seg.bit_length() - 1):
                aseg = cmul(aseg, aseg)
            pw = [aseg]
            for _ in range(1, SUBLANES):
                pw.append(cmul(pw[-1], aseg))
            for lvl, k in enumerate((1, 2, 4)):
                tab_ref[jb, 2 + 2 * lvl] = jnp.where(row >= k, pw[k - 1][0], 0.0)
                tab_ref[jb, 3 + 2 * lvl] = jnp.where(row >= k, pw[k - 1][1], 0.0)
                tab_ref[jb, 10 + 2 * lvl] = jnp.where(row < SUBLANES - k, pw[k - 1][0], 0.0)
                tab_ref[jb, 11 + 2 * lvl] = jnp.where(row < SUBLANES - k, -pw[k - 1][1], 0.0)
            f_r = f_i = r_r = r_i = jnp.zeros(slab, F32)
            for i in range(SUBLANES):
                f_r = jnp.where(row == i, pw[i][0], f_r)
                f_i = jnp.where(row == i, pw[i][1], f_i)
                r_r = jnp.where(row == i, pw[SUBLANES - 1 - i][0], r_r)
                r_i = jnp.where(row == i, -pw[SUBLANES - 1 - i][1], r_i)
            tab_ref[jb, 8] = f_r
            tab_ref[jb, 9] = f_i
            tab_ref[jb, 16] = r_r
            tab_ref[jb, 17] = r_i

    vm = pl.BlockSpec(memory_space=pltpu.VMEM)
    bd = jax.ShapeDtypeStruct((nb, S5_COLS, S5_LANES), BF16)
    return pl.pallas_call(
        body, name="s5_prep_fwd",
        in_specs=[vm] * 7 + [pl.BlockSpec(memory_space=pl.ANY)], out_specs=[vm] * 6,
        out_shape=[bd, bd, bd, bd, jax.ShapeDtypeStruct((nb, S5_TABS, SUBLANES, S5_LANES), F32),
                   jax.ShapeDtypeStruct((nb, 2, seg * SUBLANES, S5_LANES), F32)],
    )(a_re, a_im, log_dt, b_re, b_im, c_re, c_im, after)


def _s5_prep_bwd(a_re, a_im, log_dt, b_re, b_im, gbb_re, gbb_im, gct_re, gct_im, gab_re, gab_im):
    G, P = a_re.shape
    nb = G * S5_GROUP // S5_COLS
    g8 = S5_COLS // S5_GROUP

    def body(are_ref, aim_ref, ldt_ref, bre_ref, bim_ref, gbr_ref, gbi_ref, gcr_ref, gci_ref, gar_ref, gai_ref,
             o_a, o_bc, o_ldt):
        dt = jnp.exp(_row_to_col(ldt_ref[...], G))
        a_raw = are_ref[...]
        a_imv = aim_ref[...]
        a_re_c, ab_re, ab_im, f_re, f_im, ia_re, ia_im = _s5_disc(a_raw, a_imv, dt)
        tile_f = _lane_tiler(F32)
        mask = _group_mask(S5_COLS, S5_GROUP)
        mask8 = _group_mask(g8, 1)
        for jb in range(nb):
            rs = slice(jb * S5_COLS, (jb + 1) * S5_COLS)
            gs = slice(jb * g8, (jb + 1) * g8)
            ls = slice(jb * S5_LANES, (jb + 1) * S5_LANES)
            for k, src in enumerate((gbr_ref, gbi_ref, gcr_ref, gci_ref)):
                o_bc[k, rs, :] = _dot_hi(jnp.where(mask, src[jb], 0.0), tile_f, NT)
            for k, src in enumerate((gar_ref, gai_ref)):
                o_a[k, gs, :] = _dot_hi(jnp.where(mask8, src[:, ls], 0.0), tile_f, NT)
        rep = _group_repeat(G)
        fx_re = _dot_hi(rep, f_re)
        fx_im = _dot_hi(rep, f_im)
        gbr, gbi = o_bc[0], o_bc[1]
        br, bi = bre_ref[...], bim_ref[...]
        o_bc[0] = fx_re * gbr + fx_im * gbi
        o_bc[1] = fx_re * gbi - fx_im * gbr
        gf_re = _dot_hi(rep, br * gbr + bi * gbi, TN)
        gf_im = _dot_hi(rep, br * gbi - bi * gbr, TN)
        gab_r = o_a[0] + ia_re * gf_re + ia_im * gf_im
        gab_i = o_a[1] + ia_re * gf_im - ia_im * gf_re
        q_re = f_re * ia_re - f_im * ia_im
        q_im = f_re * ia_im + f_im * ia_re
        ga_re = -(q_re * gf_re + q_im * gf_im)
        ga_im = -(q_re * gf_im - q_im * gf_re)
        gth_re = ab_re * gab_r + ab_im * gab_i
        gth_im = ab_re * gab_i - ab_im * gab_r
        ga_re = ga_re + dt * gth_re
        ga_im = ga_im + dt * gth_im
        gdt = jnp.sum(a_re_c * gth_re + a_imv * gth_im, axis=-1, keepdims=True)
        eye = (_iota2((G, G), 0) == _iota2((G, G), 1)).astype(F32)
        o_ldt[...] = jnp.sum(eye * (gdt * dt), axis=0, keepdims=True)
        slope = jnp.where(a_raw < -1e-4, 1.0, jnp.where(a_raw == -1e-4, 0.5, 0.0))
        o_a[0] = ga_re * slope
        o_a[1] = ga_im

    vm = pl.BlockSpec(memory_space=pltpu.VMEM)
    return pl.pallas_call(
        body, name="s5_prep_bwd",
        in_specs=[vm] * 11, out_specs=[vm] * 3,
        out_shape=[jax.ShapeDtypeStruct((2, G, P), F32), jax.ShapeDtypeStruct((4, G * S5_GROUP, P), F32),
                   jax.ShapeDtypeStruct((1, G), F32)],
    )(a_re, a_im, log_dt, b_re, b_im, gbb_re, gbb_im, gct_re, gct_im, gab_re, gab_im)


def _scan8(xr, xi, tab_ref, base, shifts):
    for lvl, sh in enumerate(shifts):
        mr = tab_ref[0, base + 2 * lvl]
        mi = tab_ref[0, base + 2 * lvl + 1]
        ar = pltpu.roll(xr, sh, 0)
        ai = pltpu.roll(xi, sh, 0)
        xr, xi = xr + mr * ar - mi * ai, xi + mr * ai + mi * ar
    return xr, xi


def _to_segments(src_ref, dst_ref, seg):
    for i in range(seg):
        dst_ref[i * SUBLANES:(i + 1) * SUBLANES, :] = src_ref[pl.ds(i, SUBLANES, stride=seg), :]


def _from_segments(src_ref, dst_ref, seg):
    for i in range(seg):
        dst_ref[pl.ds(i, SUBLANES, stride=seg), :] = src_ref[i * SUBLANES:(i + 1) * SUBLANES, :]


def _slab(i):
    return pl.ds(pl.multiple_of(i * SUBLANES, SUBLANES), SUBLANES)


def _s5_scan_fwd(proj_main, bbd_re, bbd_im, cbd_re, cbd_im, dvec, tab, ptab, DS):
    L = proj_main.shape[0]
    nb = DS // S5_COLS
    tb = _blk(L, 512, SUBLANES)
    nt = L // tb
    seg = tb // SUBLANES

    def body(u_ref, bre_ref, bim_ref, cre_ref, cim_ref, d_ref, tab_ref, pt_ref, y_ref, sre_ref, sim_ref,
             up_ref, yp_ref, car_ref):
        t = pl.program_id(1)

        @pl.when(t == 0)
        def _():
            car_ref[...] = jnp.zeros_like(car_ref)

        _to_segments(u_ref, up_ref, seg)
        up = up_ref[...]
        sre_ref[...] = _dot(up, bre_ref[0])
        sim_ref[...] = _dot(up, bim_ref[0])
        ar, ai = tab_ref[0, 0], tab_ref[0, 1]

        def pass1(i, x):
            xr = ar * x[0] - ai * x[1] + sre_ref[_slab(i), :]
            xi = ar * x[1] + ai * x[0] + sim_ref[_slab(i), :]
            sre_ref[_slab(i), :] = xr
            sim_ref[_slab(i), :] = xi
            return xr, xi

        zero = jnp.zeros((SUBLANES, S5_LANES), F32)
        er, ei = lax.fori_loop(0, seg, pass1, (zero, zero))
        cin_r, cin_i = car_ref[0], car_ref[1]
        sr, si = _scan8(er, ei, tab_ref, 2, (1, 2, 4))
        pr, pi = tab_ref[0, 8], tab_ref[0, 9]
        sr, si = sr + pr * cin_r - pi * cin_i, si + pr * cin_i + pi * cin_r
        row0 = _iota2((SUBLANES, S5_LANES), 0) == 0
        cr = jnp.where(row0, cin_r, pltpu.roll(sr, 1, 0))
        ci = jnp.where(row0, cin_i, pltpu.roll(si, 1, 0))
        car_ref[0] = jnp.broadcast_to(sr[SUBLANES - 1:SUBLANES, :], sr.shape)
        car_ref[1] = jnp.broadcast_to(si[SUBLANES - 1:SUBLANES, :], si.shape)

        def pass2(i, _):
            qr, qi = pt_ref[0, 0, _slab(i), :], pt_ref[0, 1, _slab(i), :]
            sre_ref[_slab(i), :] += qr * cr - qi * ci
            sim_ref[_slab(i), :] += qr * ci + qi * cr
            return 0

        lax.fori_loop(0, seg, pass2, 0, unroll=4)
        yp_ref[...] = _dot(sre_ref[...], cre_ref[0], NT) - _dot(sim_ref[...], cim_ref[0], NT) + d_ref[...] * up
        _from_segments(yp_ref, y_ref, seg)

    return pl.pallas_call(
        body, name="s5_scan_fwd", grid=(nb, nt),
        in_specs=[
            pl.BlockSpec((tb, S5_COLS), lambda j, t: (t, j)),
            pl.BlockSpec((1, S5_COLS, S5_LANES), lambda j, t: (j, 0, 0)),
            pl.BlockSpec((1, S5_COLS, S5_LANES), lambda j, t: (j, 0, 0)),
            pl.BlockSpec((1, S5_COLS, S5_LANES), lambda j, t: (j, 0, 0)),
            pl.BlockSpec((1, S5_COLS, S5_LANES), lambda j, t: (j, 0, 0)),
            pl.BlockSpec((1, S5_COLS), lambda j, t: (0, j)),
            pl.BlockSpec((1, S5_TABS, SUBLANES, S5_LANES), lambda j, t: (j, 0, 0, 0)),
            pl.BlockSpec((1, 2, tb, S5_LANES), lambda j, t: (j, 0, 0, 0)),
        ],
        out_specs=[
            pl.BlockSpec((tb, S5_COLS), lambda j, t: (t, j)),
            pl.BlockSpec((tb, S5_LANES), lambda j, t: (t, j)),
            pl.BlockSpec((tb, S5_LANES), lambda j, t: (t, j)),
        ],
        out_shape=[jax.ShapeDtypeStruct((L, DS), F32),
                   jax.ShapeDtypeStruct((L, nb * S5_LANES), F32),
                   jax.ShapeDtypeStruct((L, nb * S5_LANES), F32)],
        scratch_shapes=[pltpu.VMEM((tb, S5_COLS), F32), pltpu.VMEM((tb, S5_COLS), F32),
                        pltpu.VMEM((2, SUBLANES, S5_LANES), F32)],
        compiler_params=pltpu.CompilerParams(dimension_semantics=("parallel", "arbitrary")),
    )(proj_main, bbd_re, bbd_im, cbd_re, cbd_im, dvec, tab, ptab)


def _s5_scan_bwd(dy, proj_main, s_re, s_im, bbd_re, bbd_im, cbd_re, cbd_im, dvec, tab, ptab, d_s5, DS):
    L = proj_main.shape[0]
    nb = DS // S5_COLS
    tb = _blk(L, 512, SUBLANES)
    nt = L // tb
    seg = tb // SUBLANES
    tb8 = tb // SUBLANES

    def body(dy_ref, u_ref, sre_ref, sim_ref, pre_ref, pim_ref, bre_ref, bim_ref, cre_ref, cim_ref, d_ref, tab_ref, pt_ref,
             _ds5_ref, du_ref, gd_ref, gcre_ref, gcim_ref, gbre_ref, gbim_ref, gare_ref, gaim_ref,
             lre_ref, lim_ref, up_ref, dyp_ref, dup_ref, duo_ref, car_ref):
        t = pl.program_id(1)

        @pl.when(t == 0)
        def _():
            car_ref[...] = jnp.zeros_like(car_ref)
            gd_ref[...] = jnp.zeros_like(gd_ref)
            gcre_ref[...] = jnp.zeros_like(gcre_ref)
            gcim_ref[...] = jnp.zeros_like(gcim_ref)
            gbre_ref[...] = jnp.zeros_like(gbre_ref)
            gbim_ref[...] = jnp.zeros_like(gbim_ref)
            gare_ref[...] = jnp.zeros_like(gare_ref)
            gaim_ref[...] = jnp.zeros_like(gaim_ref)

        _to_segments(dy_ref, dyp_ref, seg)
        _to_segments(u_ref, up_ref, seg)
        dyv = dyp_ref[...]
        u = up_ref[...]
        gd_ref[...] += jnp.sum(dyv * u, axis=0, keepdims=True)
        lre_ref[...] = _dot(dyv, cre_ref[0])
        lim_ref[...] = -_dot(dyv, cim_ref[0])
        gcre_ref[0] += _dot(dyv, sre_ref[...], TN)
        gcim_ref[0] -= _dot(dyv, sim_ref[...], TN)
        ar, ai = tab_ref[0, 0], -tab_ref[0, 1]

        def pass1(k, x):
            i = seg - 1 - k
            xr = ar * x[0] - ai * x[1] + lre_ref[_slab(i), :]
            xi = ar * x[1] + ai * x[0] + lim_ref[_slab(i), :]
            lre_ref[_slab(i), :] = xr
            lim_ref[_slab(i), :] = xi
            return xr, xi

        zero = jnp.zeros((SUBLANES, S5_LANES), F32)
        er, ei = lax.fori_loop(0, seg, pass1, (zero, zero))
        cin_r, cin_i = car_ref[0], car_ref[1]
        lr, li = _scan8(er, ei, tab_ref, 10, (7, 6, 4))
        pr, pi = tab_ref[0, 16], tab_ref[0, 17]
        lr, li = lr + pr * cin_r - pi * cin_i, li + pr * cin_i + pi * cin_r
        rows = _iota2((SUBLANES, S5_LANES), 0)
        cr = jnp.where(rows == SUBLANES - 1, cin_r, pltpu.roll(lr, SUBLANES - 1, 0))
        ci = jnp.where(rows == SUBLANES - 1, cin_i, pltpu.roll(li, SUBLANES - 1, 0))
        car_ref[0] = jnp.broadcast_to(lr[0:1, :], lr.shape)
        car_ref[1] = jnp.broadcast_to(li[0:1, :], li.shape)

        first = (t == nt - 1).astype(F32)
        head_re = jnp.broadcast_to(pre_ref[SUBLANES - 1:SUBLANES, :], zero.shape) * (1.0 - first)
        head_im = jnp.broadcast_to(pim_ref[SUBLANES - 1:SUBLANES, :], zero.shape) * (1.0 - first)
        last = _slab(seg - 1)
        sp0_re = jnp.where(rows == 0, head_re, pltpu.roll(sre_ref[last, :], 1, 0))
        sp0_im = jnp.where(rows == 0, head_im, pltpu.roll(sim_ref[last, :], 1, 0))

        def fix(i, acc, sp_re, sp_im):
            j = seg - 1 - i
            qr, qi = pt_ref[0, 0, _slab(j), :], -pt_ref[0, 1, _slab(j), :]
            xr = lre_ref[_slab(i), :] + qr * cr - qi * ci
            xi = lim_ref[_slab(i), :] + qr * ci + qi * cr
            lre_ref[_slab(i), :] = xr
            lim_ref[_slab(i), :] = xi
            return acc[0] + sp_re * xr + sp_im * xi, acc[1] + sp_re * xi - sp_im * xr

        def pass2(i, acc):
            return fix(i, acc, sre_ref[_slab(i - 1), :], sim_ref[_slab(i - 1), :])

        acc_re, acc_im = lax.fori_loop(1, seg, pass2, fix(0, (zero, zero), sp0_re, sp0_im), unroll=2)
        gare_ref[...] += jnp.sum(acc_re, axis=0, keepdims=True)
        gaim_ref[...] += jnp.sum(acc_im, axis=0, keepdims=True)
        lre = lre_ref[...]
        lim = lim_ref[...]
        dup_ref[...] = dyv * d_ref[...] + _dot(lre, bre_ref[0], NT) + _dot(lim, bim_ref[0], NT)
        _from_segments(dup_ref, duo_ref, seg)
        du_ref[...] = duo_ref[...].astype(BF16)
        gbre_ref[0] += _dot(u, lre, TN)
        gbim_ref[0] += _dot(u, lim, TN)

    rt = lambda t: nt - 1 - t
    col = pl.BlockSpec((tb, S5_COLS), lambda j, t: (rt(t), j))
    st = pl.BlockSpec((tb, S5_LANES), lambda j, t: (rt(t), j))
    prev = pl.BlockSpec((SUBLANES, S5_LANES), lambda j, t: (jnp.maximum(rt(t) * tb8 - 1, 0), j))
    bmat = pl.BlockSpec((1, S5_COLS, S5_LANES), lambda j, t: (j, 0, 0))
    cmat = bmat
    return pl.pallas_call(
        body, name="s5_scan_bwd", grid=(nb, nt),
        in_specs=[col, col, st, st, prev, prev, bmat, bmat, cmat, cmat,
                  pl.BlockSpec((1, S5_COLS), lambda j, t: (0, j)),
                  pl.BlockSpec((1, S5_TABS, SUBLANES, S5_LANES), lambda j, t: (j, 0, 0, 0)),
                  pl.BlockSpec((1, 2, tb, S5_LANES), lambda j, t: (j, 0, 0, 0)),
                  pl.BlockSpec(memory_space=pl.ANY)],
        out_specs=[col, pl.BlockSpec((1, S5_COLS), lambda j, t: (0, j)), cmat, cmat, bmat, bmat,
                   pl.BlockSpec((1, S5_LANES), lambda j, t: (0, j)), pl.BlockSpec((1, S5_LANES), lambda j, t: (0, j))],
        input_output_aliases={13: 0},
        out_shape=[jax.ShapeDtypeStruct((L, 2 * DS), BF16), jax.ShapeDtypeStruct((1, DS), F32),
                   jax.ShapeDtypeStruct((nb, S5_COLS, S5_LANES), F32), jax.ShapeDtypeStruct((nb, S5_COLS, S5_LANES), F32),
                   jax.ShapeDtypeStruct((nb, S5_COLS, S5_LANES), F32), jax.ShapeDtypeStruct((nb, S5_COLS, S5_LANES), F32),
                   jax.ShapeDtypeStruct((1, nb * S5_LANES), F32), jax.ShapeDtypeStruct((1, nb * S5_LANES), F32)],
        scratch_shapes=[pltpu.VMEM((tb, S5_LANES), F32), pltpu.VMEM((tb, S5_LANES), F32)]
        + [pltpu.VMEM((tb, S5_COLS), F32)] * 4 + [pltpu.VMEM((2, SUBLANES, S5_LANES), F32)],
        compiler_params=pltpu.CompilerParams(dimension_semantics=("parallel", "arbitrary")),
    )(dy, proj_main, s_re, s_im, s_re, s_im, bbd_re, bbd_im, cbd_re, cbd_im, dvec, tab, ptab, d_s5)


def _s5_post_fwd(y_pre, proj_main, glu_w, glu_b, DS):
    L = y_pre.shape[0]
    tr = _blk(L, 256, SUBLANES)

    def body(y_ref, z_ref, w_ref, b_ref, o_ref, t_ref):
        y1 = _gelu(y_ref[...])
        t = _dot(y1, w_ref[...]) + b_ref[...]
        t_ref[...] = t
        z = z_ref[...]
        o_ref[...] = (y1 * _sigmoid(t) * (z * _sigmoid(z))).astype(BF16)

    row = pl.BlockSpec((tr, DS), lambda i: (i, 0))
    return pl.pallas_call(
        body, name="s5_post_fwd", grid=(L // tr,),
        in_specs=[row, pl.BlockSpec((tr, DS), lambda i: (i, 1)), pl.BlockSpec((DS, DS), lambda i: (0, 0)),
                  pl.BlockSpec((1, DS), lambda i: (0, 0))],
        out_specs=[row, row],
        out_shape=[jax.ShapeDtypeStruct((L, 2 * DS), BF16), jax.ShapeDtypeStruct((L, DS), F32)],
        compiler_params=pltpu.CompilerParams(dimension_semantics=("parallel",)),
    )(y_pre, proj_main, glu_w, glu_b)


def _s5_post_bwd(d_ycat, y_pre, proj_main, t_pre, glu_w, DS):
    L = y_pre.shape[0]
    tr = _blk(L, 256, SUBLANES)

    def body(dy_ref, y_ref, z_ref, t_ref, w_ref, dyp_ref, dz_ref, dt_ref, y1_ref, gb_ref):
        i = pl.program_id(0)

        @pl.when(i == 0)
        def _():
            gb_ref[...] = jnp.zeros_like(gb_ref)

        dy = dy_ref[...]
        yp = y_ref[...]
        z = z_ref[...]
        y1 = _gelu(yp)
        sg = _sigmoid(t_ref[...])
        sz = _sigmoid(z)
        c = y1 * sg
        d_c = dy * (z * sz)
        dz_ref[...] = (dy * c * (sz * (1.0 + z * (1.0 - sz)))).astype(BF16)
        d_t = d_c * y1 * sg * (1.0 - sg)
        gb_ref[...] += jnp.sum(d_t, axis=0, keepdims=True)
        dt_ref[...] = d_t.astype(BF16)
        y1_ref[...] = y1.astype(BF16)
        d_y1 = d_c * sg + _dot(d_t, w_ref[...], NT)
        dyp_ref[...] = d_y1 * _gelu_grad(yp)

    row = pl.BlockSpec((tr, DS), lambda i: (i, 0))
    return pl.pallas_call(
        body, name="s5_post_bwd", grid=(L // tr,),
        in_specs=[row, row, pl.BlockSpec((tr, DS), lambda i: (i, 1)), row, pl.BlockSpec((DS, DS), lambda i: (0, 0))],
        out_specs=[row, pl.BlockSpec((tr, DS), lambda i: (i, 1)), row, row, pl.BlockSpec((1, DS), lambda i: (0, 0))],
        out_shape=[jax.ShapeDtypeStruct((L, DS), F32), jax.ShapeDtypeStruct((L, 2 * DS), BF16),
                   jax.ShapeDtypeStruct((L, DS), BF16), jax.ShapeDtypeStruct((L, DS), BF16),
                   jax.ShapeDtypeStruct((1, DS), F32)],
        compiler_params=pltpu.CompilerParams(dimension_semantics=("arbitrary",)),
    )(d_ycat, y_pre, proj_main, t_pre, glu_w)


def _row_cumsum(x, reverse=False):
    n = x.shape[0]
    row = lax.broadcasted_iota(jnp.int32, x.shape, 0)
    k = 1
    while k < n:
        if reverse:
            x = x + jnp.where(row < n - k, pltpu.roll(x, n - k, 0), 0.0)
        else:
            x = x + jnp.where(row >= k, pltpu.roll(x, k, 0), 0.0)
        k *= 2
    return x


def _gla_gates(glow, gu_ref, gb_ref):
    a = _dot(glow, gu_ref[...]) + gb_ref[...]
    lg = (jnp.minimum(a, 0.0) - jnp.log(1.0 + jnp.exp(-jnp.abs(a)))) * (1.0 / GLA_TAU)
    ri = lax.broadcasted_iota(jnp.int32, (GLA_CHUNK, GLA_CHUNK), 0)
    ci = lax.broadcasted_iota(jnp.int32, (GLA_CHUNK, GLA_CHUNK), 1)
    b = _row_cumsum(lg)
    b_last = b[GLA_CHUNK - 1:GLA_CHUNK, :]
    return a, b, b_last, ri >= ci


def _gla_specs(DS, DK, DV, c, cmap):
    return [
        pl.BlockSpec((c, DK), lambda n: (cmap(n), 2 * DS // DK)),
        pl.BlockSpec((c, DK), lambda n: (cmap(n), 2 * DS // DK + 1)),
        pl.BlockSpec((c, DV), lambda n: (cmap(n), (2 * DS + 2 * DK) // DV)),
        pl.BlockSpec((c, DV), lambda n: (cmap(n), (2 * DS + 2 * DK) // DV + 1)),
    ]


def _gla_fwd(proj_main, proj_low, gate_up_pad, gate_bias, norm_w, ycat, DS, DK, DV):
    L = proj_main.shape[0]
    nc = L // GLA_CHUNK
    cps = math.gcd(GLA_STEP_CHUNKS, nc)
    nh = DK // GLA_HK
    scale = GLA_HK ** -0.5

    def body(q_ref, k_ref, v_ref, z_ref, gl_ref, gu_ref, gb_ref, nw_ref, _yc_ref, y_ref, sp_ref, st_ref):
        n = pl.program_id(0)

        @pl.when(n == 0)
        def _():
            st_ref[...] = jnp.zeros_like(st_ref)

        pairs = [(sc, h) for sc in range(cps) for h in range(nh)]
        rows = lambda sc: slice(sc * GLA_CHUNK, (sc + 1) * GLA_CHUNK)
        kcol = lambda h: slice(h * GLA_HK, (h + 1) * GLA_HK)
        vcol = lambda h: slice(h * GLA_HV, (h + 1) * GLA_HV)
        gates = [_gla_gates(gl_ref[rows(sc), :], gu_ref, gb_ref) for sc in range(cps)]
        qe, dec, o_in, kv = {}, {}, {}, {}
        for sc, h in pairs:
            _, b, b_last, mask = gates[sc]
            bh, bl = b[:, kcol(h)], b_last[:, kcol(h)]
            qe[sc, h] = (q_ref[rows(sc), kcol(h)] * scale) * jnp.exp(bh)
            kh = k_ref[rows(sc), kcol(h)]
            vh = v_ref[rows(sc), vcol(h)]
            attn = jnp.where(mask, _dot(qe[sc, h], kh * jnp.exp(-bh), NT), 0.0)
            o_in[sc, h] = _dot(attn, vh)
            kv[sc, h] = _dot(vh, kh * jnp.exp(bl - bh), TN)
            dec[sc, h] = jnp.exp(bl)
        for sc, h in pairs:
            st = st_ref[h]
            sp_ref[sc, h] = st
            o = o_in[sc, h] + _dot(qe[sc, h], st, NT)
            st_ref[h] = dec[sc, h] * st + kv[sc, h]
            r = lax.rsqrt(jnp.mean(o * o, axis=-1, keepdims=True) + EPS)
            z = z_ref[rows(sc), vcol(h)]
            y_ref[rows(sc), vcol(h)] = (o * r * nw_ref[...] * (z * _sigmoid(z))).astype(BF16)

    c = cps * GLA_CHUNK
    return pl.pallas_call(
        body, name="gla_fwd", grid=(nc // cps,),
        in_specs=_gla_specs(DS, DK, DV, c, lambda n: n) + [
            pl.BlockSpec((c, LANES), lambda n: (n, 0)),
            pl.BlockSpec((LANES, DK), lambda n: (0, 0)),
            pl.BlockSpec((1, DK), lambda n: (0, 0)),
            pl.BlockSpec((1, GLA_HV), lambda n: (0, 0)),
            pl.BlockSpec(memory_space=pl.ANY),
        ],
        out_specs=[pl.BlockSpec((c, DV), lambda n: (n, DS // DV)),
                   pl.BlockSpec((cps, nh, GLA_HV, GLA_HK), lambda n: (n, 0, 0, 0))],
        input_output_aliases={8: 0},
        out_shape=[jax.ShapeDtypeStruct(ycat.shape, BF16), jax.ShapeDtypeStruct((nc, nh, GLA_HV, GLA_HK), F32)],
        scratch_shapes=[pltpu.VMEM((nh, GLA_HV, GLA_HK), F32)],
        compiler_params=pltpu.CompilerParams(dimension_semantics=("arbitrary",)),
    )(proj_main, proj_main, proj_main, proj_main, proj_low, gate_up_pad, gate_bias, norm_w, ycat)


def _gla_bwd(d_ycat, proj_main, proj_low, s_prev, gate_up_pad, gate_bias, norm_w, DS, DK, DV):
    L = proj_main.shape[0]
    nc = L // GLA_CHUNK
    cps = math.gcd(GLA_STEP_CHUNKS, nc)
    nh = DK // GLA_HK
    scale = GLA_HK ** -0.5

    def body(dy_ref, q_ref, k_ref, v_ref, z_ref, gl_ref, sp_ref, gu_ref, gb_ref, nw_ref,
             dg_ref, da_ref, gnw_ref, ggb_ref, dst_ref):
        n = pl.program_id(0)

        @pl.when(n == 0)
        def _():
            dst_ref[...] = jnp.zeros_like(dst_ref)
            gnw_ref[...] = jnp.zeros_like(gnw_ref)
            ggb_ref[...] = jnp.zeros_like(ggb_ref)

        last_row = lax.broadcasted_iota(jnp.int32, (GLA_CHUNK, GLA_HK), 0) == GLA_CHUNK - 1
        nw = nw_ref[...]
        for sc in reversed(range(cps)):
            rs = slice(sc * GLA_CHUNK, (sc + 1) * GLA_CHUNK)
            a, b, b_last, mask = _gla_gates(gl_ref[rs, :], gu_ref, gb_ref)
            for h in range(nh):
                ks = slice(h * GLA_HK, (h + 1) * GLA_HK)
                vs = slice(h * GLA_HV, (h + 1) * GLA_HV)
                bh, bl = b[:, ks], b_last[:, ks]
                e = jnp.exp(bh)
                einv = jnp.exp(-bh)
                etail = jnp.exp(bl - bh)
                dec = jnp.exp(bl)
                qe = (q_ref[rs, ks] * scale) * e
                kh = k_ref[rs, ks]
                ke = kh * einv
                ktail = kh * etail
                vh = v_ref[rs, vs]
                st = sp_ref[sc, h]
                dst = dst_ref[h]
                attn = jnp.where(mask, _dot(qe, ke, NT), 0.0)
                o = _dot(attn, vh) + _dot(qe, st, NT)
                r = lax.rsqrt(jnp.mean(o * o, axis=-1, keepdims=True) + EPS)
                nrm = o * r
                z = z_ref[rs, vs]
                sz = _sigmoid(z)
                dy = dy_ref[rs, vs]
                dg_ref[rs, 2 * DK + DV + h * GLA_HV:2 * DK + DV + (h + 1) * GLA_HV] = (
                    dy * nrm * nw * (sz * (1.0 + z * (1.0 - sz)))).astype(BF16)
                d_on = dy * (z * sz)
                gnw_ref[...] += jnp.sum(d_on * nrm, axis=0, keepdims=True)
                d_n = d_on * nw
                d_o = r * (d_n - nrm * jnp.mean(d_n * nrm, axis=-1, keepdims=True))
                d_attn = jnp.where(mask, _dot(d_o, vh, NT), 0.0)
                dg_ref[rs, 2 * DK + h * GLA_HV:2 * DK + (h + 1) * GLA_HV] = (
                    _dot(attn, d_o, TN) + _dot(ktail, dst, NT)).astype(BF16)
                d_qe = _dot(d_attn, ke) + _dot(d_o, st)
                d_ke = _dot(d_attn, qe, TN)
                d_kt = _dot(vh, dst)
                d_dec = jnp.sum(dst * st, axis=0, keepdims=True)
                dst_ref[h] = dec * dst + _dot(d_o, qe, TN)
                dg_ref[rs, ks] = (d_qe * scale * e).astype(BF16)
                dg_ref[rs, DK + h * GLA_HK:DK + (h + 1) * GLA_HK] = (d_ke * einv + d_kt * etail).astype(BF16)
                d_bl = jnp.sum(d_kt * ktail, axis=0, keepdims=True) + d_dec * dec
                d_b = d_qe * qe - d_ke * ke - d_kt * ktail + jnp.where(last_row, d_bl, 0.0)
                d_lg = _row_cumsum(d_b, reverse=True)
                d_a = d_lg * (1.0 / GLA_TAU) * _sigmoid(-a[:, ks])
                ggb_ref[:, ks] += jnp.sum(d_a, axis=0, keepdims=True)
                da_ref[rs, ks] = d_a.astype(BF16)

    c = cps * GLA_CHUNK
    ns = nc // cps
    rn = lambda n: ns - 1 - n
    return pl.pallas_call(
        body, name="gla_bwd", grid=(ns,),
        in_specs=[pl.BlockSpec((c, DV), lambda n: (rn(n), DS // DV))] + _gla_specs(DS, DK, DV, c, rn) + [
            pl.BlockSpec((c, LANES), lambda n: (rn(n), 0)),
            pl.BlockSpec((cps, nh, GLA_HV, GLA_HK), lambda n: (rn(n), 0, 0, 0)),
            pl.BlockSpec((LANES, DK), lambda n: (0, 0)),
            pl.BlockSpec((1, DK), lambda n: (0, 0)),
            pl.BlockSpec((1, GLA_HV), lambda n: (0, 0)),
        ],
        out_specs=[pl.BlockSpec((c, 2 * DK + 2 * DV), lambda n: (rn(n), 0)),
                   pl.BlockSpec((c, DK), lambda n: (rn(n), 0)),
                   pl.BlockSpec((1, GLA_HV), lambda n: (0, 0)), pl.BlockSpec((1, DK), lambda n: (0, 0))],
        out_shape=[jax.ShapeDtypeStruct((L, 2 * DK + 2 * DV), BF16),
                   jax.ShapeDtypeStruct((L, DK), BF16),
                   jax.ShapeDtypeStruct((1, GLA_HV), F32), jax.ShapeDtypeStruct((1, DK), F32)],
        scratch_shapes=[pltpu.VMEM((nh, GLA_HV, GLA_HK), F32)],
        compiler_params=pltpu.CompilerParams(dimension_semantics=("arbitrary",)),
    )(d_ycat, proj_main, proj_main, proj_main, proj_main, proj_low, s_prev, gate_up_pad, gate_bias, norm_w)


def _adamw_math(w, g, m, v):
    c1 = 1.0 - ADAM_B1 ** ADAM_STEP
    c2 = 1.0 - ADAM_B2 ** ADAM_STEP
    m_ = ADAM_B1 * m + (1.0 - ADAM_B1) * g
    v_ = ADAM_B2 * v + (1.0 - ADAM_B2) * (g * g)
    return -ADAM_LR * ((m_ / c1) / (jnp.sqrt(v_ / c2) + ADAM_EPS) + ADAM_WD * w), m_, v_


def _adamw_small(g_row, g_a, g_bc, ws, ms, vs):
    n = len(ws)
    nvec = n - 6

    def body(*refs):
        grow_ref, ga_ref, gbc_ref = refs[:3]
        w_refs, m_refs, v_refs = refs[3:3 + n], refs[3 + n:3 + 2 * n], refs[3 + 2 * n:3 + 3 * n]
        outs = refs[3 + 3 * n:]
        off = 0
        for i in range(n):
            if i < nvec:
                width = ws[i].shape[1]
                g = grow_ref[:, off:off + width]
                off += width
            elif i < nvec + 2:
                g = ga_ref[i - nvec]
            else:
                g = gbc_ref[i - nvec - 2]
            d, m_, v_ = _adamw_math(w_refs[i][...], g, m_refs[i][...], v_refs[i][...])
            outs[i][...] = g
            outs[n + i][...] = d
            outs[2 * n + i][...] = m_
            outs[3 * n + i][...] = v_

    vm = pl.BlockSpec(memory_space=pltpu.VMEM)
    outs = pl.pallas_call(
        body, name="adamw_small",
        in_specs=[vm] * (3 + 3 * n), out_specs=[vm] * (4 * n),
        out_shape=[jax.ShapeDtypeStruct(w.shape, F32) for w in ws] * 4,
    )(g_row, g_a, g_bc, *ws, *ms, *vs)
    return [outs[k * n:(k + 1) * n] for k in range(4)]


def _my_pos():
    return lax.axis_index("x"), lax.axis_index("y"), lax.axis_index("c")


def _split_start(name, srcs, lands_sd, make_copies, ncopies, after):
    n, m = len(srcs), len(lands_sd)

    def body(*refs):
        send_sems, recv_sems = refs[n + m + len(after)], refs[n + m + len(after) + 1]
        for cp in make_copies(refs[:n], refs[n:n + m], send_sems, recv_sems):
            cp.start()
        refs[-1][...] = jnp.zeros_like(refs[-1])

    hbm = pl.BlockSpec(memory_space=pltpu.HBM)
    sem = pl.BlockSpec(memory_space=pltpu.SEMAPHORE)
    outs = pl.pallas_call(
        body, name=name,
        in_specs=[hbm] * (n + m) + [pl.BlockSpec(memory_space=pl.ANY)] * len(after),
        out_specs=[sem, sem] + [hbm] * (n + m) + [pl.BlockSpec(memory_space=pltpu.VMEM)],
        out_shape=[pltpu.SemaphoreType.DMA((ncopies,)), pltpu.SemaphoreType.DMA((ncopies,))]
        + [pltpu.HBM(s.shape, s.dtype) for s in srcs] + [pltpu.HBM(s.shape, s.dtype) for s in lands_sd]
        + [jax.ShapeDtypeStruct((SUBLANES, LANES), F32)],
        input_output_aliases={i: 2 + i for i in range(n + m)},
        compiler_params=pltpu.CompilerParams(has_side_effects=pltpu.SideEffectType.DATAFLOW_SIDE_EFFECTING),
    )(*[pltpu.with_memory_space_constraint(s, pltpu.HBM) for s in srcs],
      *[pltpu.with_memory_space_constraint(lax.empty(s.shape, s.dtype), pltpu.HBM) for s in lands_sd], *after)
    return outs[0], outs[1], outs[2:2 + n], outs[2 + n:2 + n + m], outs[-1]


def _split_wait(name, send_sems, recv_sems, srcs, lands, make_copies, after):
    n, m = len(srcs), len(lands)

    def body(*refs):
        for cp in make_copies(refs[:n], refs[n:n + m], refs[n + m], refs[n + m + 1]):
            cp.wait_send()
            cp.wait_recv()

    hbm = pl.BlockSpec(memory_space=pltpu.HBM)
    sem = pl.BlockSpec(memory_space=pltpu.SEMAPHORE)
    outs = pl.pallas_call(
        body, name=name,
        in_specs=[hbm] * (n + m) + [sem, sem] + [pl.BlockSpec(memory_space=pl.ANY)] * len(after),
        out_specs=[hbm] * (n + m),
        out_shape=[pltpu.HBM(s.shape, s.dtype) for s in srcs] + [pltpu.HBM(p.shape, p.dtype) for p in lands],
        input_output_aliases={i: i for i in range(n + m)},
        compiler_params=pltpu.CompilerParams(has_side_effects=pltpu.SideEffectType.DATAFLOW_SIDE_EFFECTING),
    )(*srcs, *lands, send_sems, recv_sems, *after)
    return outs[:n], outs[n:]


def _late_gather_copies(srcs, lands, send_sems, recv_sems):
    x, y, c = _my_pos()
    me = 2 * x + y
    copies = []
    for d in (1, 2, 3):
        to = (x ^ (d >> 1), y ^ (d & 1), c)
        for a in range(len(srcs)):
            hrows = srcs[a].shape[0] // 2
            rows = pl.ds(c * hrows, hrows)
            copies.append(pltpu.make_async_remote_copy(
                src_ref=srcs[a].at[rows, :], dst_ref=lands[a].at[me, rows, :], send_sem=send_sems.at[3 * a + d - 1],
                recv_sem=recv_sems.at[3 * a + d - 1], device_id=to, device_id_type=MESH))
    return copies


def _late_gather_start(shards, after, name):
    n = len(shards)

    def body(*refs):
        srcs, lands = refs[:n], refs[n:2 * n]
        send_sems, recv_sems = refs[2 * n + 1], refs[2 * n + 2]
        token = refs[-1]
        for cp in _late_gather_copies(srcs, lands, send_sems, recv_sems):
            cp.start()
        token[...] = jnp.zeros_like(token)

    hbm = pl.BlockSpec(memory_space=pltpu.HBM)
    sem = pl.BlockSpec(memory_space=pltpu.SEMAPHORE)
    outs = pl.pallas_call(
        body, name=name,
        in_specs=[hbm] * (2 * n) + [pl.BlockSpec(memory_space=pl.ANY)],
        out_specs=[sem, sem] + [hbm] * (2 * n) + [pl.BlockSpec(memory_space=pltpu.VMEM)],
        out_shape=[pltpu.SemaphoreType.DMA((3 * n,)), pltpu.SemaphoreType.DMA((3 * n,))]
        + [pltpu.HBM(s.shape, s.dtype) for s in shards]
        + [pltpu.HBM((4,) + s.shape, s.dtype) for s in shards]
        + [jax.ShapeDtypeStruct((SUBLANES, LANES), F32)],
        input_output_aliases={i: 2 + i for i in range(2 * n)},
        compiler_params=pltpu.CompilerParams(has_side_effects=pltpu.SideEffectType.DATAFLOW_SIDE_EFFECTING),
    )(*[pltpu.with_memory_space_constraint(s, pltpu.HBM) for s in shards],
      *[pltpu.with_memory_space_constraint(lax.empty((4,) + s.shape, s.dtype), pltpu.HBM) for s in shards], after)
    return outs[0], outs[1], outs[2:2 + n], outs[2 + n:2 + 2 * n], outs[-1]


def _late_gather_wait(send_sems, recv_sems, shards, lands, after, name):
    n = len(shards)

    def body(*refs):
        src_refs, land_refs = refs[:n], refs[n:2 * n]
        ssem, rsem = refs[2 * n], refs[2 * n + 1]
        for cp in _late_gather_copies(src_refs, land_refs, ssem, rsem):
            cp.wait_send()
            cp.wait_recv()

    hbm = pl.BlockSpec(memory_space=pltpu.HBM)
    sem = pl.BlockSpec(memory_space=pltpu.SEMAPHORE)
    outs = pl.pallas_call(
        body, name=name,
        in_specs=[hbm] * (2 * n) + [sem, sem] + [pl.BlockSpec(memory_space=pl.ANY)] * len(after),
        out_specs=[hbm] * (2 * n),
        out_shape=[pltpu.HBM(s.shape, s.dtype) for s in shards] + [pltpu.HBM(p.shape, p.dtype) for p in lands],
        input_output_aliases={i: i for i in range(2 * n)},
        compiler_params=pltpu.CompilerParams(has_side_effects=pltpu.SideEffectType.DATAFLOW_SIDE_EFFECTING),
    )(*shards, *lands, send_sems, recv_sems, *after)
    return outs[n:]


def _late_gather_pair(lands, name):
    n = len(lands)

    def body(*refs):
        outs = refs[n:2 * n]
        send_sems, recv_sems = refs[2 * n:]
        x, y, c = _my_pos()

        def copy(a, d, half):
            chip = 2 * (x ^ (d >> 1)) + (y ^ (d & 1))
            hrows = lands[a].shape[1] // 2
            sl = outs[a].at[chip, pl.ds(half * hrows, hrows), :]
            return pltpu.make_async_remote_copy(src_ref=sl, dst_ref=sl, send_sem=send_sems.at[3 * a + d - 1],
                                                recv_sem=recv_sems.at[3 * a + d - 1], device_id=(x, y, 1 - c),
                                                device_id_type=MESH)

        pairs = [(a, d) for d in (1, 2, 3) for a in range(n)]
        for a, d in pairs:
            copy(a, d, c).start()
        for a, d in pairs:
            copy(a, d, c).wait_send()
            copy(a, d, 1 - c).wait_recv()

    hbm = pl.BlockSpec(memory_space=pltpu.HBM)
    return pl.pallas_call(
        body, name=name, in_specs=[hbm] * n, out_specs=[hbm] * n,
        out_shape=[jax.ShapeDtypeStruct(p.shape, p.dtype) for p in lands],
        input_output_aliases={i: i for i in range(n)},
        scratch_shapes=[pltpu.SemaphoreType.DMA((3 * n,)), pltpu.SemaphoreType.DMA((3 * n,))],
    )(*lands)


def _pair_exchange(gs):
    n = len(gs)

    def body(*refs):
        ins, outs = refs[:n], refs[n:2 * n]
        send_sems, recv_sems = refs[2 * n:]
        x, y, c = _my_pos()
        sent = []
        for a in range(n):
            hrows = gs[a].shape[1] // 2
            cp = pltpu.make_async_remote_copy(
                src_ref=ins[a].at[:, pl.ds((1 - c) * hrows, hrows), :], dst_ref=outs[a], send_sem=send_sems.at[a],
                recv_sem=recv_sems.at[a], device_id=(x, y, 1 - c), device_id_type=MESH)
            cp.start()
            sent.append(cp)
        for cp in sent:
            cp.wait()

    hbm = pl.BlockSpec(memory_space=pltpu.HBM)
    return pl.pallas_call(
        body, name="grad_pair_exchange", in_specs=[hbm] * n, out_specs=[hbm] * n,
        out_shape=[jax.ShapeDtypeStruct((g.shape[0], g.shape[1] // 2, g.shape[2]), g.dtype) for g in gs],
        scratch_shapes=[pltpu.SemaphoreType.DMA((n,)), pltpu.SemaphoreType.DMA((n,))],
    )(*gs)


def _pair_add(g, got, c_arr, name):
    nk, rows2, cols = g.shape
    hrows = rows2 // 2
    tr = _blk(hrows, 256, 2 * SUBLANES)
    nb = hrows // tr

    def body(c_ref, a_ref, b_ref, o_ref):
        o_ref[...] = (a_ref[...].astype(F32) + b_ref[...].astype(F32)).astype(o_ref.dtype)

    return pl.pallas_call(
        body, name=name,
        grid_spec=pltpu.PrefetchScalarGridSpec(
            num_scalar_prefetch=1, grid=(nk, nb),
            in_specs=[pl.BlockSpec((1, tr, cols), lambda k, i, c_ref: (k, c_ref[0] * nb + i, 0)),
                      pl.BlockSpec((1, tr, cols), lambda k, i, c_ref: (k, i, 0))],
            out_specs=pl.BlockSpec((1, tr, cols), lambda k, i, c_ref: (k, i, 0))),
        out_shape=jax.ShapeDtypeStruct((nk, hrows, cols), g.dtype),
        compiler_params=pltpu.CompilerParams(dimension_semantics=("parallel", "parallel")),
    )(c_arr, g, got)


def _chip_scatter_copies(srcs, lands, send_sems, recv_sems):
    x, y, c = _my_pos()
    copies = []
    for d in (1, 2, 3):
        tx, ty = x ^ (d >> 1), y ^ (d & 1)
        for a in range(len(srcs)):
            copies.append(pltpu.make_async_remote_copy(
                src_ref=srcs[a].at[2 * tx + ty], dst_ref=lands[a].at[d - 1], send_sem=send_sems.at[3 * a + d - 1],
                recv_sem=recv_sems.at[3 * a + d - 1], device_id=(tx, ty, c), device_id_type=MESH))
    return copies


def _chip_scatter_start(pss):
    n = len(pss)

    def body(*refs):
        srcs, lands = refs[:n], refs[n:2 * n]
        send_sems, recv_sems = refs[2 * n], refs[2 * n + 1]
        token = refs[-1]
        for cp in _chip_scatter_copies(srcs, lands, send_sems, recv_sems):
            cp.start()
        token[...] = jnp.zeros_like(token)

    hbm = pl.BlockSpec(memory_space=pltpu.HBM)
    sem = pl.BlockSpec(memory_space=pltpu.SEMAPHORE)
    land_shapes = [(3,) + p.shape[1:] for p in pss]
    outs = pl.pallas_call(
        body, name="grad_chip_scatter_start",
        in_specs=[hbm] * (2 * n),
        out_specs=[sem, sem] + [hbm] * (2 * n) + [pl.BlockSpec(memory_space=pltpu.VMEM)],
        out_shape=[pltpu.SemaphoreType.DMA((3 * n,)), pltpu.SemaphoreType.DMA((3 * n,))]
        + [pltpu.HBM(p.shape, p.dtype) for p in pss]
        + [pltpu.HBM(s, p.dtype) for s, p in zip(land_shapes, pss)]
        + [jax.ShapeDtypeStruct((SUBLANES, LANES), F32)],
        input_output_aliases={i: 2 + i for i in range(2 * n)},
        compiler_params=pltpu.CompilerParams(has_side_effects=pltpu.SideEffectType.DATAFLOW_SIDE_EFFECTING),
    )(*[pltpu.with_memory_space_constraint(p, pltpu.HBM) for p in pss],
      *[pltpu.with_memory_space_constraint(lax.empty(s, p.dtype), pltpu.HBM) for s, p in zip(land_shapes, pss)])
    return outs[0], outs[1], outs[2:2 + n], outs[2 + n:2 + 2 * n], outs[-1]


def _chip_scatter_wait(send_sems, recv_sems, srcs, lands, after):
    n = len(srcs)

    def body(*refs):
        src_refs, land_refs = refs[:n], refs[n:2 * n]
        ssem, rsem = refs[2 * n], refs[2 * n + 1]
        for cp in _chip_scatter_copies(src_refs, land_refs, ssem, rsem):
            cp.wait_send()
            cp.wait_recv()

    hbm = pl.BlockSpec(memory_space=pltpu.HBM)
    sem = pl.BlockSpec(memory_space=pltpu.SEMAPHORE)
    outs = pl.pallas_call(
        body, name="grad_chip_scatter_wait",
        in_specs=[hbm] * (2 * n) + [sem, sem, pl.BlockSpec(memory_space=pl.ANY)],
        out_specs=[hbm] * (2 * n),
        out_shape=[pltpu.HBM(p.shape, p.dtype) for p in srcs] + [pltpu.HBM(p.shape, p.dtype) for p in lands],
        input_output_aliases={i: i for i in range(2 * n)},
        compiler_params=pltpu.CompilerParams(has_side_effects=pltpu.SideEffectType.DATAFLOW_SIDE_EFFECTING),
    )(*srcs, *lands, send_sems, recv_sems, after)
    return outs[:n], outs[n:]


def _chip_sum(ps, got, me_arr, name):
    _, hrows, cols = ps.shape
    tr = _blk(hrows, 256, 2 * SUBLANES)

    def body(me_ref, p_ref, g_ref, o_ref):
        acc = p_ref[0].astype(F32)
        for s in range(3):
            acc = acc + g_ref[s].astype(F32)
        o_ref[...] = acc

    return pl.pallas_call(
        body, name=name,
        grid_spec=pltpu.PrefetchScalarGridSpec(
            num_scalar_prefetch=1, grid=(hrows // tr,),
            in_specs=[pl.BlockSpec((1, tr, cols), lambda i, me_ref: (me_ref[0], i, 0)),
                      pl.BlockSpec((3, tr, cols), lambda i, me_ref: (0, i, 0))],
            out_specs=pl.BlockSpec((tr, cols), lambda i, me_ref: (i, 0))),
        out_shape=jax.ShapeDtypeStruct((hrows, cols), F32),
        compiler_params=pltpu.CompilerParams(dimension_semantics=("parallel",)),
    )(me_arr, ps, got)


def _pair_swap(halves):
    n = len(halves)

    def body(*refs):
        ins, outs = refs[:n], refs[n:2 * n]
        send_sems, recv_sems = refs[2 * n:]
        x, y, c = _my_pos()
        sent = []
        for a in range(n):
            cp = pltpu.make_async_remote_copy(src_ref=ins[a], dst_ref=outs[a], send_sem=send_sems.at[a], recv_sem=recv_sems.at[a],
                                              device_id=(x, y, 1 - c), device_id_type=MESH)
            cp.start()
            sent.append(cp)
        for cp in sent:
            cp.wait()

    hbm = pl.BlockSpec(memory_space=pltpu.HBM)
    return pl.pallas_call(
        body, name="grad_pair_swap", in_specs=[hbm] * n, out_specs=[hbm] * n,
        out_shape=[jax.ShapeDtypeStruct(h.shape, h.dtype) for h in halves],
        scratch_shapes=[pltpu.SemaphoreType.DMA((n,)), pltpu.SemaphoreType.DMA((n,))],
    )(*halves)


def _adamw_sharded(w, g_own, g_other, m, v, c_arr, after, name):
    R, C = w.shape
    hrows = R // 2
    tr = _blk(hrows, 256, SUBLANES)
    nbh = hrows // tr

    def body(c_ref, w_ref, go_ref, gx_ref, m_ref, v_ref, _after_ref, g_ref, d_ref, nm_ref, nv_ref):
        mine = (pl.program_id(0) // nbh) == c_ref[0]
        g_ = jnp.where(mine, go_ref[...], gx_ref[...])
        g_ref[...] = g_
        d_ref[...], nm_ref[...], nv_ref[...] = _adamw_math(w_ref[...], g_, m_ref[...], v_ref[...])

    blk = pl.BlockSpec((tr, C), lambda i, c_ref: (i, 0))
    hblk = pl.BlockSpec((tr, C), lambda i, c_ref: (i % nbh, 0))
    sd = jax.ShapeDtypeStruct((R, C), F32)
    return pl.pallas_call(
        body, name=name,
        grid_spec=pltpu.PrefetchScalarGridSpec(
            num_scalar_prefetch=1, grid=(2 * nbh,),
            in_specs=[blk, hblk, hblk, blk, blk, pl.BlockSpec(memory_space=pl.ANY)], out_specs=[blk] * 4),
        out_shape=[sd] * 4,
        compiler_params=pltpu.CompilerParams(dimension_semantics=("parallel",)),
    )(c_arr, w, g_own, g_other, m, v, after)


def _ar_piece(ref, rows, p):
    start = p * rows
    if rows % SUBLANES == 0:
        start = pl.multiple_of(start, SUBLANES)
    return ref.at[..., pl.ds(start, rows), :]


def _ar_peer(d):
    x, y, c = _my_pos()
    return (x ^ (d >> 2), y ^ ((d >> 1) & 1), c ^ (d & 1))


def _ar_lin(p):
    return 4 * p[0] + 2 * p[1] + p[2]


def _ar_scatter_copies(rows):
    def make(srcs, lands, send_sems, recv_sems):
        n = len(srcs)
        copies = []
        for d in range(1, 8):
            to = _ar_peer(d)
            for a in range(n):
                copies.append(pltpu.make_async_remote_copy(
                    src_ref=_ar_piece(srcs[a], rows[a], _ar_lin(to)), dst_ref=lands[a].at[d],
                    send_sem=send_sems.at[(d - 1) * n + a], recv_sem=recv_sems.at[(d - 1) * n + a], device_id=to,
                    device_id_type=MESH))
        return copies
    return make


def _ar_gather_copies(rows):
    def make(srcs, lands, send_sems, recv_sems):
        n = len(srcs)
        me = _ar_lin(_my_pos())
        copies = []
        for d in range(1, 8):
            for a in range(n):
                copies.append(pltpu.make_async_remote_copy(
                    src_ref=srcs[a], dst_ref=_ar_piece(lands[a], rows[a], me),
                    send_sem=send_sems.at[(d - 1) * n + a], recv_sem=recv_sems.at[(d - 1) * n + a], device_id=_ar_peer(d),
                    device_id_type=MESH))
        return copies
    return make


def _ar_sum(srcs, lands, rows):
    n = len(srcs)

    def body(*refs):
        me = _ar_lin(_my_pos())
        for a in range(n):
            acc = _ar_piece(refs[a], rows[a], me)[...]
            for d in range(1, 8):
                acc = acc + refs[n + a][d]
            refs[2 * n + a][...] = acc

    vm = pl.BlockSpec(memory_space=pltpu.VMEM)
    return pl.pallas_call(
        body, name="allreduce_sum", in_specs=[vm] * (2 * n), out_specs=[vm] * n,
        out_shape=[jax.ShapeDtypeStruct(p.shape[1:], F32) for p in lands],
    )(*srcs, *lands)


def kernel(x, pre_norm_w, w_in, s5_A_re, s5_A_im, s5_B_re, s5_B_im, s5_C_re, s5_C_im, s5_D, s5_log_dt, s5_glu_w, s5_glu_b, gla_gate_up, gla_gate_bias, gla_norm_w, w_out, post_norm_w, loss_target, m_pre_norm_w, m_w_in, m_s5_A_re, m_s5_A_im, m_s5_B_re, m_s5_B_im, m_s5_C_re, m_s5_C_im, m_s5_D, m_s5_log_dt, m_s5_glu_w, m_s5_glu_b, m_gla_gate_up, m_gla_gate_bias, m_gla_norm_w, m_w_out, m_post_norm_w, v_pre_norm_w, v_w_in, v_s5_A_re, v_s5_A_im, v_s5_B_re, v_s5_B_im, v_s5_C_re, v_s5_C_im, v_s5_D, v_s5_log_dt, v_s5_glu_w, v_s5_glu_b, v_gla_gate_up, v_gla_gate_bias, v_gla_norm_w, v_w_out, v_post_norm_w):
    names = ["pre_norm_w", "w_in", "s5_A_re", "s5_A_im", "s5_B_re", "s5_B_im", "s5_C_re", "s5_C_im", "s5_D", "s5_log_dt",
             "s5_glu_w", "s5_glu_b", "gla_gate_up", "gla_gate_bias", "gla_norm_w", "w_out", "post_norm_w"]
    W = dict(zip(names, (pre_norm_w, w_in, s5_A_re, s5_A_im, s5_B_re, s5_B_im, s5_C_re, s5_C_im, s5_D, s5_log_dt,
                         s5_glu_w, s5_glu_b, gla_gate_up, gla_gate_bias, gla_norm_w, w_out, post_norm_w)))
    M = dict(zip(names, (m_pre_norm_w, m_w_in, m_s5_A_re, m_s5_A_im, m_s5_B_re, m_s5_B_im, m_s5_C_re, m_s5_C_im, m_s5_D,
                         m_s5_log_dt, m_s5_glu_w, m_s5_glu_b, m_gla_gate_up, m_gla_gate_bias, m_gla_norm_w, m_w_out,
                         m_post_norm_w)))
    V = dict(zip(names, (v_pre_norm_w, v_w_in, v_s5_A_re, v_s5_A_im, v_s5_B_re, v_s5_B_im, v_s5_C_re, v_s5_C_im, v_s5_D,
                         v_s5_log_dt, v_s5_glu_w, v_s5_glu_b, v_gla_gate_up, v_gla_gate_bias, v_gla_norm_w, v_w_out,
                         v_post_norm_w)))
    sharded = ("w_in", "s5_glu_w", "w_out", "gla_gate_up")

    xb = x[0]
    tgt = loss_target[0]
    L, D = xb.shape
    DS = D // 2
    G = DS // S5_GROUP
    P = S5_STATE
    NB = DS // S5_COLS
    DV = D - DS
    DK = DV // 2
    WM = 2 * DS + 2 * DK + 2 * DV
    nsh = w_in.shape[2]

    chip = 2 * lax.axis_index("x") + lax.axis_index("y")
    own = [jnp.pad(w_in[0].astype(BF16), ((0, 0), (0, -nsh % LANES))), s5_glu_w[0].astype(BF16),
           w_out[0].astype(BF16), gla_gate_up[0]]
    fill = lambda g, o: lax.dynamic_update_index_in_dim(g, o, chip, 0)
    win_ss, win_rs, win_src, win_lands, win_token = _late_gather_start(own[:1], pre_norm_w, "w_in_gather_start")
    h = _prenorm_fwd(xb, pre_norm_w, win_token)

    b_view = lambda t: jnp.transpose(t[0], (0, 2, 1)).reshape(G * S5_GROUP, P)
    b_back = lambda t: jnp.transpose(t.reshape(G, S5_GROUP, P), (0, 2, 1))[None]
    c_view = lambda t: t[0].reshape(G * S5_GROUP, P)
    c_back = lambda t: t.reshape(1, G, S5_GROUP, P)
    small = ["pre_norm_w", "post_norm_w", "s5_D", "s5_glu_b", "gla_gate_bias", "gla_norm_w", "s5_log_dt",
             "s5_A_re", "s5_A_im", "s5_B_re", "s5_B_im", "s5_C_re", "s5_C_im"]
    view = {n: (lambda t: t) for n in small[:7]}
    back = dict(view)
    view.update(s5_A_re=lambda t: t[0], s5_A_im=lambda t: t[0], s5_B_re=b_view, s5_B_im=b_view, s5_C_re=c_view, s5_C_im=c_view)
    back.update(s5_A_re=lambda t: t[None], s5_A_im=lambda t: t[None], s5_B_re=b_back, s5_B_im=b_back, s5_C_re=c_back,
                s5_C_im=c_back)
    Wv = {n: view[n](W[n]) for n in small}
    bbd_re, bbd_im, ct_re, ct_im, tab, ptab = _s5_prep_fwd(
        Wv["s5_A_re"], Wv["s5_A_im"], s5_log_dt, Wv["s5_B_re"], Wv["s5_B_im"], Wv["s5_C_re"], Wv["s5_C_im"],
        h, _blk(L, 512, SUBLANES) // SUBLANES)
    dvec = s5_D

    for d_ in (W, M, V):
        d_["w_in"], _ = lax.optimization_barrier((d_["w_in"], win_token))
    g_win = _late_gather_wait(win_ss, win_rs, win_src, win_lands,
                              [tab, W["w_in"][0], M["w_in"][0], V["w_in"][0]], "w_in_gather_wait")
    g_win = fill(_late_gather_pair(g_win, "w_in_gather_pair")[0], own[0])
    w_main, w_low = _assemble_w_in(g_win, nsh, WM)
    late_ss, late_rs, late_src, late_lands, late_token = _late_gather_start(own[1:], g_win, "late_gather_start")
    proj_main, proj_low = _in_proj(h, w_main, w_low, late_token)
    y_pre, s_re, s_im = _s5_scan_fwd(proj_main, bbd_re, bbd_im, ct_re, ct_im, dvec, tab, ptab, DS)
    late = _late_gather_wait(late_ss, late_rs, late_src, late_lands, [y_pre], "late_gather_wait")
    late = _late_gather_pair(late, "late_gather_pair")
    g_glu, g_wout, g_gup = [fill(g, o) for g, o in zip(late, own[1:])]
    glu_w = g_glu.reshape(DS, DS)
    wout = g_wout.reshape(D, D)
    gup = jnp.moveaxis(g_gup, 0, 1).reshape(GLA_RANK, DK)
    gup_pad = jnp.pad(gup, ((0, LANES - GLA_RANK), (0, 0))).astype(BF16)
    ycat, t_pre = _s5_post_fwd(y_pre, proj_main, glu_w, s5_glu_b, DS)
    ycat, s_prev = _gla_fwd(proj_main, proj_low, gup_pad, gla_gate_bias, gla_norm_w, ycat, DS, DK, DV)
    mixed = _mm(ycat, wout, name="out_proj")
    loss11, d_mixed, dout, g_post_w = _post_fwd_bwd(mixed, xb, tgt, post_norm_w)

    d_ycat = _mm(d_mixed, wout, tb=True, name="out_proj_dx")
    g_wout_full = _mm(ycat, d_mixed, ta=True, out_dtype=BF16, name="out_proj_dw")
    d_ypre, d_s5, d_t, y1, g_glu_b = _s5_post_bwd(d_ycat, y_pre, proj_main, t_pre, glu_w, DS)
    g_glu_full = _mm(y1, d_t, ta=True, out_dtype=BF16, name="glu_dw")
    d_s5, g_D, gct_re, gct_im, gbbd_re, gbbd_im, gab_re, gab_im = _s5_scan_bwd(
        d_ypre, proj_main, s_re, s_im, bbd_re, bbd_im, ct_re, ct_im, dvec, tab, ptab, d_s5, DS)
    d_gla, d_a, g_norm_w, g_gate_bias = _gla_bwd(
        d_ycat, proj_main, proj_low, s_prev, gup_pad, gla_gate_bias, gla_norm_w, DS, DK, DV)
    d_low = _mm(d_a, gup_pad, tb=True, out_dtype=BF16, name="gate_dx")
    g_gup_pad = _mm(proj_low, d_a, ta=True, name="gate_dw")
    g_wmain, g_wlow = _in_proj_dw(h, d_s5, d_gla, d_low)

    gs = [_split_w_in_grad(g_wmain, g_wlow, nsh),
          g_glu_full.reshape(4, DS // 4, DS),
          g_wout_full.reshape(4, D // 4, D),
          jnp.moveaxis(g_gup_pad[:GLA_RANK].reshape(GLA_RANK, 4, DK // 4), 1, 0)]
    c_arr = lax.axis_index("c").astype(jnp.int32).reshape(1)
    me_arr = chip.astype(jnp.int32).reshape(1)
    got = _pair_exchange(gs)
    pss = [_pair_add(g, r, c_arr, "grad_pair_add_" + n) for n, g, r in zip(sharded, gs, got)]
    send_sems, recv_sems, pss, lands, token = _chip_scatter_start(pss)

    dh = _in_proj_dx(d_s5, d_gla, d_low, w_main, w_low, token)
    grad_x, g_pre_w = _prenorm_bwd(xb, dh, dout, pre_norm_w)

    g_a, g_bc, g_ldt = _s5_prep_bwd(Wv["s5_A_re"], Wv["s5_A_im"], s5_log_dt, Wv["s5_B_re"], Wv["s5_B_im"],
                                    gbbd_re, gbbd_im, gct_re, gct_im, gab_re, gab_im)

    g_vecs = jnp.concatenate([g_pre_w, g_post_w, g_D, g_glu_b, g_gate_bias, g_norm_w, g_ldt, loss11], axis=1)
    loss_at = g_vecs.shape[1] - 1
    lanes_pad = -g_vecs.shape[1] % (8 * SUBLANES * LANES)
    g_vecs = jnp.pad(g_vecs, ((0, 0), (0, lanes_pad))).reshape(-1, LANES)
    ar_srcs = [g_vecs, g_a, g_bc]
    ar_rows = [a.shape[-2] // 8 for a in ar_srcs]
    ar_lands = [jax.ShapeDtypeStruct((8,) + a.shape[:-2] + (r, a.shape[-1]), F32) for a, r in zip(ar_srcs, ar_rows)]
    ar_ss, ar_rs, ar_srcs, ar_got, ar_token = _split_start(
        "allreduce_scatter_start", ar_srcs, ar_lands, _ar_scatter_copies(ar_rows), 7 * len(ar_srcs), [])

    pss, rcv = _chip_scatter_wait(send_sems, recv_sems, pss, lands, ar_token)
    halves = [_chip_sum(p, r, me_arr, "grad_chip_sum_" + n) for n, p, r in zip(sharded, pss, rcv)]
    others = _pair_swap(halves)
    ar_srcs, ar_got = _split_wait("allreduce_scatter_wait", ar_ss, ar_rs, ar_srcs, ar_got, _ar_scatter_copies(ar_rows),
                                  [others[0]])
    ar_red = _ar_sum(ar_srcs, ar_got, ar_rows)
    ag_ss, ag_rs, ar_red, ag_full, ag_token = _split_start(
        "allreduce_gather_start", ar_red, [jax.ShapeDtypeStruct(a.shape, F32) for a in ar_srcs],
        _ar_gather_copies(ar_rows), 7 * len(ar_red), [])
    G_out, D_out, M_out, V_out = {}, {}, {}, {}
    for n, g_own, g_other in zip(sharded, halves, others):
        g_, d_, m_, v_ = _adamw_sharded(W[n][0], g_own, g_other, M[n][0], V[n][0], c_arr, ag_token, "adamw_" + n)
        G_out[n], D_out[n], M_out[n], V_out[n] = g_[None], d_[None], m_[None], v_[None]
    ar_red, ag_full = _split_wait("allreduce_gather_wait", ag_ss, ag_rs, ar_red, ag_full, _ar_gather_copies(ar_rows),
                                  [D_out[n] for n in sharded])
    me8 = 2 * chip + lax.axis_index("c")
    r_vecs, r_a, r_bc = [lax.dynamic_update_slice_in_dim(f, r, me8 * rw, axis=f.ndim - 2)
                         for f, r, rw in zip(ag_full, ar_red, ar_rows)]
    r_vecs = r_vecs.reshape(1, -1)
    loss = r_vecs[0, loss_at]
    outs4 = _adamw_small(r_vecs, r_a, r_bc, [Wv[n] for n in small],
                         [view[n](M[n]) for n in small], [view[n](V[n]) for n in small])
    for store, o in zip((G_out, D_out, M_out, V_out), outs4):
        store.update({n: back[n](t) for n, t in zip(small, o)})

    return (loss, grad_x[None], *[G_out[n] for n in names], *[D_out[n] for n in names],
            *[M_out[n] for n in names], *[V_out[n] for n in names])
```

```python
import functools
import math

import jax
import jax.numpy as jnp
from jax import lax
from jax.experimental import pallas as pl
from jax.experimental.pallas import tpu as pltpu

F32 = jnp.float32
BF16 = jnp.bfloat16
HI = lax.Precision.HIGHEST
MESH = pl.DeviceIdType.MESH

EPS = 1e-6
S5_GROUP = 16
S5_STATE = 64
GLA_HK = 128
GLA_HV = 256
GLA_RANK = 16
GLA_TAU = 16.0
GLA_CHUNK = 64
GLA_STEP_CHUNKS = 4
LANES = 128
SUBLANES = 8
S5_COLS = 128
S5_LANES = (S5_COLS // S5_GROUP) * S5_STATE
S5_TIME_BLOCK = 1024

ADAM_LR = 0.001
ADAM_B1 = 0.9
ADAM_B2 = 0.999
ADAM_EPS = 1e-08
ADAM_WD = 0.01
ADAM_STEP = 10

GELU_K = math.sqrt(2.0 / math.pi)
GELU_C = 0.044715


def _blk(n, pref, unit=LANES):
    best = None
    b = unit
    while b <= min(n, pref):
        if n % b == 0:
            best = b
        b += unit
    return best if best is not None else n


def _dot(a, b, dn=(((1,), (0,)), ((), ()))):
    return lax.dot_general(a.astype(BF16), b.astype(BF16), dn, preferred_element_type=F32)


def _dot_hi(a, b, dn=(((1,), (0,)), ((), ()))):
    return lax.dot_general(a, b, dn, precision=HI, preferred_element_type=F32)


NN = (((1,), (0,)), ((), ()))
NT = (((1,), (1,)), ((), ()))
TN = (((0,), (0,)), ((), ()))


def _sigmoid(x):
    return 1.0 / (1.0 + jnp.exp(-x))


def _gelu(y):
    return 0.5 * y * (1.0 + jnp.tanh(GELU_K * (y + GELU_C * y * y * y)))


def _gelu_grad(y):
    th = jnp.tanh(GELU_K * (y + GELU_C * y * y * y))
    return 0.5 * (1.0 + th) + 0.5 * y * (1.0 - th * th) * GELU_K * (1.0 + 3.0 * GELU_C * y * y)


def _mm(a, b, *, name, ta=False, tb=False, out_dtype=F32, bm=1024, bn=1024, bk=2048):
    if ta:
        K, M = a.shape
    else:
        M, K = a.shape
    if tb:
        N, K2 = b.shape
    else:
        K2, N = b.shape
    assert K == K2, (a.shape, b.shape, ta, tb)
    bm, bn, bk = _blk(M, bm), _blk(N, bn), _blk(K, bk)
    nk = K // bk
    dn = (((0 if ta else 1,), (1 if tb else 0,)), ((), ()))

    def body(a_ref, b_ref, o_ref, *acc):
        if nk == 1:
            o_ref[...] = _dot(a_ref[...], b_ref[...], dn).astype(out_dtype)
            return
        acc_ref, = acc
        k = pl.program_id(2)

        @pl.when(k == 0)
        def _():
            acc_ref[...] = jnp.zeros_like(acc_ref)

        acc_ref[...] += _dot(a_ref[...], b_ref[...], dn)

        @pl.when(k == nk - 1)
        def _():
            o_ref[...] = acc_ref[...].astype(out_dtype)

    a_spec = pl.BlockSpec((bk, bm), lambda i, j, k: (k, i)) if ta else pl.BlockSpec((bm, bk), lambda i, j, k: (i, k))
    b_spec = pl.BlockSpec((bn, bk), lambda i, j, k: (j, k)) if tb else pl.BlockSpec((bk, bn), lambda i, j, k: (k, j))
    return pl.pallas_call(
        body,
        name=name,
        grid=(M // bm, N // bn, nk),
        in_specs=[a_spec, b_spec],
        out_specs=pl.BlockSpec((bm, bn), lambda i, j, k: (i, j)),
        out_shape=jax.ShapeDtypeStruct((M, N), out_dtype),
        scratch_shapes=[pltpu.VMEM((bm, bn), F32)] if nk > 1 else [],
        compiler_params=pltpu.CompilerParams(dimension_semantics=("parallel", "parallel", "arbitrary")),
    )(a, b)


def _in_proj(h, w_main, w_low, after):
    M, K = h.shape
    N = w_main.shape[1]
    bm, bn = _blk(M, 1024), _blk(N, 1024)

    def body(h_ref, w_ref, wl_ref, _after_ref, o_ref, ol_ref):
        hv = h_ref[...]
        o_ref[...] = _dot(hv, w_ref[...])

        @pl.when(pl.program_id(1) == 0)
        def _():
            ol_ref[...] = _dot(hv, wl_ref[...])

    return pl.pallas_call(
        body, name="in_proj", grid=(M // bm, N // bn),
        in_specs=[pl.BlockSpec((bm, K), lambda i, j: (i, 0)), pl.BlockSpec((K, bn), lambda i, j: (0, j)),
                  pl.BlockSpec((K, LANES), lambda i, j: (0, 0)), pl.BlockSpec(memory_space=pl.ANY)],
        out_specs=[pl.BlockSpec((bm, bn), lambda i, j: (i, j)), pl.BlockSpec((bm, LANES), lambda i, j: (i, 0))],
        out_shape=[jax.ShapeDtypeStruct((M, N), F32), jax.ShapeDtypeStruct((M, LANES), F32)],
        compiler_params=pltpu.CompilerParams(dimension_semantics=("parallel", "arbitrary")),
    )(h, w_main, w_low, after)


def _in_proj_dx(a1, a2, al, b, bl, after, *, bm=1024, bn=1024, bk=2048):
    M, K1 = a1.shape
    K2 = a2.shape[1]
    N = b.shape[0]
    bm, bn = _blk(M, bm), _blk(N, bn)
    bk = _blk(math.gcd(K1, K2), bk)
    nk1, nk = K1 // bk, (K1 + K2) // bk

    def body(a1_ref, a2_ref, al_ref, b_ref, bl_ref, _after_ref, o_ref, acc_ref):
        k = pl.program_id(2)

        @pl.when(k == 0)
        def _():
            acc_ref[...] = _dot(al_ref[...], bl_ref[...], NT)

        @pl.when(k < nk1)
        def _():
            acc_ref[...] += _dot(a1_ref[...], b_ref[...], NT)

        @pl.when(k >= nk1)
        def _():
            acc_ref[...] += _dot(a2_ref[...], b_ref[...], NT)

        @pl.when(k == nk - 1)
        def _():
            o_ref[...] = acc_ref[...]

    return pl.pallas_call(
        body, name="in_proj_dx", grid=(M // bm, N // bn, nk),
        in_specs=[pl.BlockSpec((bm, bk), lambda i, j, k: (i, jnp.minimum(k, nk1 - 1))),
                  pl.BlockSpec((bm, bk), lambda i, j, k: (i, jnp.maximum(k - nk1, 0))),
                  pl.BlockSpec((bm, LANES), lambda i, j, k: (i, 0)),
                  pl.BlockSpec((bn, bk), lambda i, j, k: (j, k)),
                  pl.BlockSpec((bn, LANES), lambda i, j, k: (j, 0)),
                  pl.BlockSpec(memory_space=pl.ANY)],
        out_specs=pl.BlockSpec((bm, bn), lambda i, j, k: (i, j)),
        out_shape=jax.ShapeDtypeStruct((M, N), F32),
        scratch_shapes=[pltpu.VMEM((bm, bn), F32)],
        compiler_params=pltpu.CompilerParams(dimension_semantics=("parallel", "parallel", "arbitrary")),
    )(a1, a2, al, b, bl, after)


def _in_proj_dw(a, b1, b2, bl, *, bm=1024, bn=1024, bk=2048):
    K, M = a.shape
    N1, N2 = b1.shape[1], b2.shape[1]
    bm, bk = _blk(M, bm), _blk(K, bk)
    bn = _blk(math.gcd(N1, N2), bn)
    nj1, nj = N1 // bn, (N1 + N2) // bn
    nk = K // bk

    def body(a_ref, b1_ref, b2_ref, bl_ref, o_ref, ol_ref, acc_ref, accl_ref):
        j = pl.program_id(1)
        k = pl.program_id(2)

        @pl.when(k == 0)
        def _():
            acc_ref[...] = jnp.zeros_like(acc_ref)

        @pl.when(j < nj1)
        def _():
            acc_ref[...] += _dot(a_ref[...], b1_ref[...], TN)

        @pl.when(j >= nj1)
        def _():
            acc_ref[...] += _dot(a_ref[...], b2_ref[...], TN)

        @pl.when(k == nk - 1)
        def _():
            o_ref[...] = acc_ref[...].astype(BF16)

        @pl.when(j == 0)
        def _():
            low = _dot(a_ref[...], bl_ref[...], TN)

            @pl.when(k == 0)
            def _():
                accl_ref[...] = low

            @pl.when(k > 0)
            def _():
                accl_ref[...] += low

            @pl.when(k == nk - 1)
            def _():
                ol_ref[...] = accl_ref[...].astype(BF16)

    return pl.pallas_call(
        body, name="in_proj_dw", grid=(M // bm, nj, nk),
        in_specs=[pl.BlockSpec((bk, bm), lambda i, j, k: (k, i)),
                  pl.BlockSpec((bk, bn), lambda i, j, k: (jnp.where(j < nj1, k, nk - 1), jnp.minimum(j, nj1 - 1))),
                  pl.BlockSpec((bk, bn), lambda i, j, k: (jnp.where(j >= nj1, k, 0), jnp.maximum(j - nj1, 0))),
                  pl.BlockSpec((bk, LANES), lambda i, j, k: (jnp.where(j == 0, k, nk - 1), 0))],
        out_specs=[pl.BlockSpec((bm, bn), lambda i, j, k: (i, j)), pl.BlockSpec((bm, LANES), lambda i, j, k: (i, 0))],
        out_shape=[jax.ShapeDtypeStruct((M, N1 + N2), BF16), jax.ShapeDtypeStruct((M, LANES), BF16)],
        scratch_shapes=[pltpu.VMEM((bm, bn), F32), pltpu.VMEM((bm, LANES), F32)],
        compiler_params=pltpu.CompilerParams(dimension_semantics=("parallel", "arbitrary", "arbitrary")),
    )(a, b1, b2, bl)


def _assemble_w_in(g, nsh, wm):
    _, R, nshp = g.shape
    nb_in = nshp // LANES
    nb_main = wm // LANES
    tr = _blk(R, 512, 2 * SUBLANES)
    plan = []
    for b in range(nb_main + 1):
        terms = []
        for k in range(g.shape[0]):
            for i in range(nb_in):
                delta = nsh * k + LANES * i - LANES * b
                lo, hi = max(0, -delta), min(LANES, LANES - delta, nsh - LANES * i)
                if abs(delta) < LANES and hi > lo:
                    terms.append((k, i, delta))
        plan.append(terms)
    deltas = sorted({d for terms in plan for _, _, d in terms if d})

    def body(g_ref, wm_ref, wl_ref):
        src = _iota2((LANES, LANES), 0)
        dst = _iota2((LANES, LANES), 1)
        shift = {d: (dst - src == d).astype(BF16) for d in deltas}
        for b, terms in enumerate(plan):
            acc = None
            for k, i, d in terms:
                blk = g_ref[k, :, LANES * i:LANES * (i + 1)]
                t = _dot(blk, shift[d]) if d else blk.astype(F32)
                acc = t if acc is None else acc + t
            if b < nb_main:
                wm_ref[:, LANES * b:LANES * (b + 1)] = acc.astype(BF16)
            else:
                wl_ref[...] = acc.astype(BF16)

    return pl.pallas_call(
        body, name="assemble_w_in", grid=(R // tr,),
        in_specs=[pl.BlockSpec((g.shape[0], tr, nshp), lambda r: (0, r, 0))],
        out_specs=[pl.BlockSpec((tr, wm), lambda r: (r, 0)), pl.BlockSpec((tr, LANES), lambda r: (r, 0))],
        out_shape=[jax.ShapeDtypeStruct((R, wm), BF16), jax.ShapeDtypeStruct((R, LANES), BF16)],
        compiler_params=pltpu.CompilerParams(dimension_semantics=("parallel",)),
    )(g)


def _split_w_in_grad(g_main, g_low, nsh):
    R, wm = g_main.shape
    nb_main = wm // LANES
    nb_out = -(-nsh // LANES)
    tr = _blk(R, 512, 2 * SUBLANES)
    plan = {}
    for k in range(4):
        for i in range(nb_out):
            width = min(LANES, nsh - LANES * i)
            terms = []
            for b in range(nb_main + 1):
                delta = LANES * b - (nsh * k + LANES * i)
                lo, hi = max(0, delta), min(width, LANES + delta)
                if abs(delta) < LANES and hi > lo:
                    terms.append((b, delta))
            plan[k, i] = (width, terms)
    deltas = sorted({d for _, terms in plan.values() for _, d in terms if d})

    def body(gm_ref, gl_ref, o_ref):
        src = _iota2((LANES, LANES), 0)
        dst = _iota2((LANES, LANES), 1)
        shift = {d: (dst - src == d).astype(BF16) for d in deltas}
        for (k, i), (width, terms) in plan.items():
            acc = None
            for b, d in terms:
                blk = gm_ref[:, LANES * b:LANES * (b + 1)] if b < nb_main else gl_ref[...]
                t = _dot(blk, shift[d]) if d else blk.astype(F32)
                acc = t if acc is None else acc + t
            o_ref[k, :, LANES * i:LANES * i + width] = acc[:, :width].astype(BF16)

    return pl.pallas_call(
        body, name="split_w_in_grad", grid=(R // tr,),
        in_specs=[pl.BlockSpec((tr, wm), lambda r: (r, 0)), pl.BlockSpec((tr, LANES), lambda r: (r, 0))],
        out_specs=pl.BlockSpec((4, tr, nsh), lambda r: (0, r, 0)),
        out_shape=jax.ShapeDtypeStruct((4, R, nsh), BF16),
        compiler_params=pltpu.CompilerParams(dimension_semantics=("parallel",)),
    )(g_main, g_low)


def _prenorm_fwd(x, w, after):
    L, D = x.shape
    tr = _blk(L, 256, SUBLANES)

    def body(x_ref, w_ref, _after_ref, h_ref):
        xv = x_ref[...]
        r = lax.rsqrt(jnp.mean(xv * xv, axis=-1, keepdims=True) + EPS)
        h_ref[...] = (xv * r * w_ref[...]).astype(BF16)

    return pl.pallas_call(
        body, name="prenorm_fwd", grid=(L // tr,),
        in_specs=[pl.BlockSpec((tr, D), lambda i: (i, 0)), pl.BlockSpec((1, D), lambda i: (0, 0)),
                  pl.BlockSpec(memory_space=pl.ANY)],
        out_specs=pl.BlockSpec((tr, D), lambda i: (i, 0)),
        out_shape=jax.ShapeDtypeStruct((L, D), BF16),
        compiler_params=pltpu.CompilerParams(dimension_semantics=("parallel",)),
    )(x, w, after)


def _post_fwd_bwd(mixed, x, target, w):
    L, D = x.shape
    tr = _blk(L, 256, SUBLANES)
    nsteps = L // tr

    def body(mx_ref, x_ref, t_ref, w_ref, loss_ref, dm_ref, dout_ref, gw_ref, acc_ref):
        i = pl.program_id(0)

        @pl.when(i == 0)
        def _():
            acc_ref[...] = jnp.zeros_like(acc_ref)
            gw_ref[...] = jnp.zeros_like(gw_ref)

        mx = mx_ref[...]
        wv = w_ref[...]
        r = lax.rsqrt(jnp.mean(mx * mx, axis=-1, keepdims=True) + EPS)
        n = mx * r
        err = x_ref[...] + n * wv - t_ref[...]
        acc_ref[...] += jnp.sum(err * err, axis=0, keepdims=True)
        dout = err * (1.0 / D)
        dout_ref[...] = dout
        gw_ref[...] += jnp.sum(dout * n, axis=0, keepdims=True)
        dn = dout * wv
        dm_ref[...] = (r * (dn - n * jnp.mean(dn * n, axis=-1, keepdims=True))).astype(BF16)

        @pl.when(i == nsteps - 1)
        def _():
            loss_ref[...] = jnp.sum(acc_ref[...], axis=-1, keepdims=True) * (0.5 / D)

    row = pl.BlockSpec((tr, D), lambda i: (i, 0))
    vec = pl.BlockSpec((1, D), lambda i: (0, 0))
    return pl.pallas_call(
        body, name="post_fwd_bwd", grid=(nsteps,),
        in_specs=[row, row, row, vec],
        out_specs=[pl.BlockSpec((1, 1), lambda i: (0, 0)), row, row, vec],
        out_shape=[jax.ShapeDtypeStruct((1, 1), F32), jax.ShapeDtypeStruct((L, D), BF16),
                   jax.ShapeDtypeStruct((L, D), F32), jax.ShapeDtypeStruct((1, D), F32)],
        scratch_shapes=[pltpu.VMEM((1, D), F32)],
        compiler_params=pltpu.CompilerParams(dimension_semantics=("arbitrary",)),
    )(mixed, x, target, w)


def _prenorm_bwd(x, dh, dout, w):
    L, D = x.shape
    tr = _blk(L, 256, SUBLANES)

    def body(x_ref, a_ref, dout_ref, w_ref, gx_ref, gw_ref):
        i = pl.program_id(0)

        @pl.when(i == 0)
        def _():
            gw_ref[...] = jnp.zeros_like(gw_ref)

        xv = x_ref[...]
        r = lax.rsqrt(jnp.mean(xv * xv, axis=-1, keepdims=True) + EPS)
        n = xv * r
        dh = a_ref[...]
        gw_ref[...] += jnp.sum(dh * n, axis=0, keepdims=True)
        dn = dh * w_ref[...]
        gx_ref[...] = dout_ref[...] + r * (dn - n * jnp.mean(dn * n, axis=-1, keepdims=True))

    row = pl.BlockSpec((tr, D), lambda i: (i, 0))
    vec = pl.BlockSpec((1, D), lambda i: (0, 0))
    return pl.pallas_call(
        body, name="prenorm_bwd", grid=(L // tr,),
        in_specs=[row, row, row, vec],
        out_specs=[row, vec],
        out_shape=[jax.ShapeDtypeStruct((L, D), F32), jax.ShapeDtypeStruct((1, D), F32)],
        compiler_params=pltpu.CompilerParams(dimension_semantics=("arbitrary",)),
    )(x, dh, dout, w)


def _s5_disc(a_re_raw, a_im, dt):
    a_re = jnp.minimum(a_re_raw, -1e-4)
    mag = jnp.exp(a_re * dt)
    ph = a_im * dt
    ab_re = mag * jnp.cos(ph)
    ab_im = mag * jnp.sin(ph)
    inv_n = 1.0 / (a_re * a_re + a_im * a_im)
    ia_re = a_re * inv_n
    ia_im = -a_im * inv_n
    n_re = ab_re - 1.0
    f_re = n_re * ia_re - ab_im * ia_im
    f_im = n_re * ia_im + ab_im * ia_re
    return a_re, ab_re, ab_im, f_re, f_im, ia_re, ia_im


def _iota2(shape, dim):
    return lax.broadcasted_iota(jnp.int32, shape, dim)


def _group_mask(rows, rows_per_group):
    shift = rows_per_group.bit_length() - 1
    return (_iota2((rows, S5_LANES), 0) >> shift) == (_iota2((rows, S5_LANES), 1) >> (S5_STATE.bit_length() - 1))


def _lane_tiler(dtype):
    return ((_iota2((S5_STATE, S5_LANES), 1) & (S5_STATE - 1)) == _iota2((S5_STATE, S5_LANES), 0)).astype(dtype)


def _row_to_col(row, n):
    eye = (_iota2((n, n), 0) == _iota2((n, n), 1)).astype(F32)
    return jnp.sum(eye * row, axis=1, keepdims=True)


def _group_repeat(G):
    return ((_iota2((G * S5_GROUP, G), 0) >> (S5_GROUP.bit_length() - 1)) == _iota2((G * S5_GROUP, G), 1)).astype(F32)


S5_TABS = 18


def _s5_prep_fwd(a_re, a_im, log_dt, b_re, b_im, c_re, c_im, after, seg):
    G, P = a_re.shape
    nb = G * S5_GROUP // S5_COLS
    g8 = S5_COLS // S5_GROUP
    assert seg & (seg - 1) == 0, seg

    def body(are_ref, aim_ref, ldt_ref, bre_ref, bim_ref, cre_ref, cim_ref, _after_ref,
             bbre_ref, bbim_ref, ctre_ref, ctim_ref, tab_ref, pt_ref):
        dt = jnp.exp(_row_to_col(ldt_ref[...], G))
        _, ab_re, ab_im, f_re, f_im, _, _ = _s5_disc(are_ref[...], aim_ref[...], dt)
        rep = _group_repeat(G)
        fx_re = _dot_hi(rep, f_re)
        fx_im = _dot_hi(rep, f_im)
        br, bi = bre_ref[...], bim_ref[...]
        bb_re = fx_re * br - fx_im * bi
        bb_im = fx_re * bi + fx_im * br
        tile_bf = _lane_tiler(BF16)
        mask = _group_mask(S5_COLS, S5_GROUP)
        for jb in range(nb):
            rs = slice(jb * S5_COLS, (jb + 1) * S5_COLS)
            for src, dst in ((bb_re[rs], bbre_ref), (bb_im[rs], bbim_ref), (cre_ref[rs, :], ctre_ref), (cim_ref[rs, :], ctim_ref)):
                dst[jb] = jnp.where(mask, _dot(src, tile_bf), 0.0).astype(BF16)

        tile_f = _lane_tiler(F32)
        mask8 = _group_mask(g8, 1)
        row = _iota2((SUBLANES, S5_LANES), 0)
        slab = (SUBLANES, S5_LANES)
        cmul = lambda p, q: (p[0] * q[0] - p[1] * q[1], p[0] * q[1] + p[1] * q[0])
        for jb in range(nb):
            gs = slice(jb * g8, (jb + 1) * g8)

            def lanes(m):
                v = jnp.sum(jnp.where(mask8, _dot_hi(m[gs], tile_f), 0.0), axis=0, keepdims=True)
                return jnp.broadcast_to(v, slab)

            a1 = (lanes(ab_re), lanes(ab_im))
            tab_ref[jb, 0], tab_ref[jb, 1] = a1

            def powers(i, p):
                off = pl.multiple_of(i * SUBLANES, SUBLANES)
                pt_ref[jb, 0, pl.ds(off, SUBLANES), :] = p[0]
                pt_ref[jb, 1, pl.ds(off, SUBLANES), :] = p[1]
                return cmul(p, a1)

            lax.fori_loop(0, seg, powers, a1)
            aseg = a1
            for _ in range(seg.bit_length() - 1):
                aseg = cmul(aseg, aseg)
            pw = [aseg]
            for _ in range(1, SUBLANES):
                pw.append(cmul(pw[-1], aseg))
            for lvl, k in enumerate((1, 2, 4)):
                tab_ref[jb, 2 + 2 * lvl] = jnp.where(row >= k, pw[k - 1][0], 0.0)
                tab_ref[jb, 3 + 2 * lvl] = jnp.where(row >= k, pw[k - 1][1], 0.0)
                tab_ref[jb, 10 + 2 * lvl] = jnp.where(row < SUBLANES - k, pw[k - 1][0], 0.0)
                tab_ref[jb, 11 + 2 * lvl] = jnp.where(row < SUBLANES - k, -pw[k - 1][1], 0.0)
            f_r = f_i = r_r = r_i = jnp.zeros(slab, F32)
            for i in range(SUBLANES):
                f_r = jnp.where(row == i, pw[i][0], f_r)
                f_i = jnp.where(row == i, pw[i][1], f_i)
                r_r = jnp.where(row == i, pw[SUBLANES - 1 - i][0], r_r)
                r_i = jnp.where(row == i, -pw[SUBLANES - 1 - i][1], r_i)
            tab_ref[jb, 8] = f_r
            tab_ref[jb, 9] = f_i
            tab_ref[jb, 16] = r_r
            tab_ref[jb, 17] = r_i

    vm = pl.BlockSpec(memory_space=pltpu.VMEM)
    bd = jax.ShapeDtypeStruct((nb, S5_COLS, S5_LANES), BF16)
    return pl.pallas_call(
        body, name="s5_prep_fwd",
        in_specs=[vm] * 7 + [pl.BlockSpec(memory_space=pl.ANY)], out_specs=[vm] * 6,
        out_shape=[bd, bd, bd, bd, jax.ShapeDtypeStruct((nb, S5_TABS, SUBLANES, S5_LANES), F32),
                   jax.ShapeDtypeStruct((nb, 2, seg * SUBLANES, S5_LANES), F32)],
    )(a_re, a_im, log_dt, b_re, b_im, c_re, c_im, after)


def _s5_prep_bwd(a_re, a_im, log_dt, b_re, b_im, gbb_re, gbb_im, gct_re, gct_im, gab_re, gab_im):
    G, P = a_re.shape
    nb = G * S5_GROUP // S5_COLS
    g8 = S5_COLS // S5_GROUP

    def body(are_ref, aim_ref, ldt_ref, bre_ref, bim_ref, gbr_ref, gbi_ref, gcr_ref, gci_ref, gar_ref, gai_ref,
             o_a, o_bc, o_ldt):
        dt = jnp.exp(_row_to_col(ldt_ref[...], G))
        a_raw = are_ref[...]
        a_imv = aim_ref[...]
        a_re_c, ab_re, ab_im, f_re, f_im, ia_re, ia_im = _s5_disc(a_raw, a_imv, dt)
        tile_f = _lane_tiler(F32)
        mask = _group_mask(S5_COLS, S5_GROUP)
        mask8 = _group_mask(g8, 1)
        for jb in range(nb):
            rs = slice(jb * S5_COLS, (jb + 1) * S5_COLS)
            gs = slice(jb * g8, (jb + 1) * g8)
            ls = slice(jb * S5_LANES, (jb + 1) * S5_LANES)
            for k, src in enumerate((gbr_ref, gbi_ref, gcr_ref, gci_ref)):
                o_bc[k, rs, :] = _dot_hi(jnp.where(mask, src[jb], 0.0), tile_f, NT)
            for k, src in enumerate((gar_ref, gai_ref)):
                o_a[k, gs, :] = _dot_hi(jnp.where(mask8, src[:, ls], 0.0), tile_f, NT)
        rep = _group_repeat(G)
        fx_re = _dot_hi(rep, f_re)
        fx_im = _dot_hi(rep, f_im)
        gbr, gbi = o_bc[0], o_bc[1]
        br, bi = bre_ref[...], bim_ref[...]
        o_bc[0] = fx_re * gbr + fx_im * gbi
        o_bc[1] = fx_re * gbi - fx_im * gbr
        gf_re = _dot_hi(rep, br * gbr + bi * gbi, TN)
        gf_im = _dot_hi(rep, br * gbi - bi * gbr, TN)
        gab_r = o_a[0] + ia_re * gf_re + ia_im * gf_im
        gab_i = o_a[1] + ia_re * gf_im - ia_im * gf_re
        q_re = f_re * ia_re - f_im * ia_im
        q_im = f_re * ia_im + f_im * ia_re
        ga_re = -(q_re * gf_re + q_im * gf_im)
        ga_im = -(q_re * gf_im - q_im * gf_re)
        gth_re = ab_re * gab_r + ab_im * gab_i
        gth_im = ab_re * gab_i - ab_im * gab_r
        ga_re = ga_re + dt * gth_re
        ga_im = ga_im + dt * gth_im
        gdt = jnp.sum(a_re_c * gth_re + a_imv * gth_im, axis=-1, keepdims=True)
        eye = (_iota2((G, G), 0) == _iota2((G, G), 1)).astype(F32)
        o_ldt[...] = jnp.sum(eye * (gdt * dt), axis=0, keepdims=True)
        slope = jnp.where(a_raw < -1e-4, 1.0, jnp.where(a_raw == -1e-4, 0.5, 0.0))
        o_a[0] = ga_re * slope
        o_a[1] = ga_im

    vm = pl.BlockSpec(memory_space=pltpu.VMEM)
    return pl.pallas_call(
        body, name="s5_prep_bwd",
        in_specs=[vm] * 11, out_specs=[vm] * 3,
        out_shape=[jax.ShapeDtypeStruct((2, G, P), F32), jax.ShapeDtypeStruct((4, G * S5_GROUP, P), F32),
                   jax.ShapeDtypeStruct((1, G), F32)],
    )(a_re, a_im, log_dt, b_re, b_im, gbb_re, gbb_im, gct_re, gct_im, gab_re, gab_im)


def _scan8(xr, xi, tab_ref, base, shifts):
    for lvl, sh in enumerate(shifts):
        mr = tab_ref[0, base + 2 * lvl]
        mi = tab_ref[0, base + 2 * lvl + 1]
        ar = pltpu.roll(xr, sh, 0)
        ai = pltpu.roll(xi, sh, 0)
        xr, xi = xr + mr * ar - mi * ai, xi + mr * ai + mi * ar
    return xr, xi


def _to_segments(src_ref, dst_ref, seg):
    for i in range(seg):
        dst_ref[i * SUBLANES:(i + 1) * SUBLANES, :] = src_ref[pl.ds(i, SUBLANES, stride=seg), :]


def _from_segments(src_ref, dst_ref, seg):
    for i in range(seg):
        dst_ref[pl.ds(i, SUBLANES, stride=seg), :] = src_ref[i * SUBLANES:(i + 1) * SUBLANES, :]


def _slab(i):
    return pl.ds(pl.multiple_of(i * SUBLANES, SUBLANES), SUBLANES)


def _s5_scan_fwd(proj_main, bbd_re, bbd_im, cbd_re, cbd_im, dvec, tab, ptab, DS):
    L = proj_main.shape[0]
    nb = DS // S5_COLS
    tb = _blk(L, S5_TIME_BLOCK, SUBLANES)
    nt = L // tb
    seg = tb // SUBLANES

    def body(u_ref, bre_ref, bim_ref, cre_ref, cim_ref, d_ref, tab_ref, pt_ref, y_ref, sre_ref, sim_ref,
             up_ref, yp_ref, car_ref):
        t = pl.program_id(1)

        @pl.when(t == 0)
        def _():
            car_ref[...] = jnp.zeros_like(car_ref)

        _to_segments(u_ref, up_ref, seg)
        up = up_ref[...]
        sre_ref[...] = _dot(up, bre_ref[0])
        sim_ref[...] = _dot(up, bim_ref[0])
        ar, ai = tab_ref[0, 0], tab_ref[0, 1]

        def pass1(i, x):
            xr = ar * x[0] - ai * x[1] + sre_ref[_slab(i), :]
            xi = ar * x[1] + ai * x[0] + sim_ref[_slab(i), :]
            sre_ref[_slab(i), :] = xr
            sim_ref[_slab(i), :] = xi
            return xr, xi

        zero = jnp.zeros((SUBLANES, S5_LANES), F32)
        er, ei = lax.fori_loop(0, seg, pass1, (zero, zero))
        cin_r, cin_i = car_ref[0], car_ref[1]
        sr, si = _scan8(er, ei, tab_ref, 2, (1, 2, 4))
        pr, pi = tab_ref[0, 8], tab_ref[0, 9]
        sr, si = sr + pr * cin_r - pi * cin_i, si + pr * cin_i + pi * cin_r
        row0 = _iota2((SUBLANES, S5_LANES), 0) == 0
        cr = jnp.where(row0, cin_r, pltpu.roll(sr, 1, 0))
        ci = jnp.where(row0, cin_i, pltpu.roll(si, 1, 0))
        car_ref[0] = jnp.broadcast_to(sr[SUBLANES - 1:SUBLANES, :], sr.shape)
        car_ref[1] = jnp.broadcast_to(si[SUBLANES - 1:SUBLANES, :], si.shape)

        def pass2(i, _):
            qr, qi = pt_ref[0, 0, _slab(i), :], pt_ref[0, 1, _slab(i), :]
            sre_ref[_slab(i), :] += qr * cr - qi * ci
            sim_ref[_slab(i), :] += qr * ci + qi * cr
            return 0

        lax.fori_loop(0, seg, pass2, 0, unroll=4)
        yp_ref[...] = _dot(sre_ref[...], cre_ref[0], NT) - _dot(sim_ref[...], cim_ref[0], NT) + d_ref[...] * up
        _from_segments(yp_ref, y_ref, seg)

    return pl.pallas_call(
        body, name="s5_scan_fwd", grid=(nb, nt),
        in_specs=[
            pl.BlockSpec((tb, S5_COLS), lambda j, t: (t, j)),
            pl.BlockSpec((1, S5_COLS, S5_LANES), lambda j, t: (j, 0, 0)),
            pl.BlockSpec((1, S5_COLS, S5_LANES), lambda j, t: (j, 0, 0)),
            pl.BlockSpec((1, S5_COLS, S5_LANES), lambda j, t: (j, 0, 0)),
            pl.BlockSpec((1, S5_COLS, S5_LANES), lambda j, t: (j, 0, 0)),
            pl.BlockSpec((1, S5_COLS), lambda j, t: (0, j)),
            pl.BlockSpec((1, S5_TABS, SUBLANES, S5_LANES), lambda j, t: (j, 0, 0, 0)),
            pl.BlockSpec((1, 2, tb, S5_LANES), lambda j, t: (j, 0, 0, 0)),
        ],
        out_specs=[
            pl.BlockSpec((tb, S5_COLS), lambda j, t: (t, j)),
            pl.BlockSpec((tb, S5_LANES), lambda j, t: (t, j)),
            pl.BlockSpec((tb, S5_LANES), lambda j, t: (t, j)),
        ],
        out_shape=[jax.ShapeDtypeStruct((L, DS), F32),
                   jax.ShapeDtypeStruct((L, nb * S5_LANES), F32),
                   jax.ShapeDtypeStruct((L, nb * S5_LANES), F32)],
        scratch_shapes=[pltpu.VMEM((tb, S5_COLS), F32), pltpu.VMEM((tb, S5_COLS), F32),
                        pltpu.VMEM((2, SUBLANES, S5_LANES), F32)],
        compiler_params=pltpu.CompilerParams(dimension_semantics=("parallel", "arbitrary")),
    )(proj_main, bbd_re, bbd_im, cbd_re, cbd_im, dvec, tab, ptab)


def _s5_scan_bwd(dy, proj_main, s_re, s_im, bbd_re, bbd_im, cbd_re, cbd_im, dvec, tab, ptab, d_s5, DS):
    L = proj_main.shape[0]
    nb = DS // S5_COLS
    tb = _blk(L, S5_TIME_BLOCK, SUBLANES)
    nt = L // tb
    seg = tb // SUBLANES
    tb8 = tb // SUBLANES

    def body(dy_ref, u_ref, sre_ref, sim_ref, pre_ref, pim_ref, bre_ref, bim_ref, cre_ref, cim_ref, d_ref, tab_ref, pt_ref,
             _ds5_ref, du_ref, gd_ref, gcre_ref, gcim_ref, gbre_ref, gbim_ref, gare_ref, gaim_ref,
             lre_ref, lim_ref, up_ref, dyp_ref, dup_ref, duo_ref, car_ref):
        t = pl.program_id(1)

        @pl.when(t == 0)
        def _():
            car_ref[...] = jnp.zeros_like(car_ref)
            gd_ref[...] = jnp.zeros_like(gd_ref)
            gcre_ref[...] = jnp.zeros_like(gcre_ref)
            gcim_ref[...] = jnp.zeros_like(gcim_ref)
            gbre_ref[...] = jnp.zeros_like(gbre_ref)
            gbim_ref[...] = jnp.zeros_like(gbim_ref)
            gare_ref[...] = jnp.zeros_like(gare_ref)
            gaim_ref[...] = jnp.zeros_like(gaim_ref)

        _to_segments(dy_ref, dyp_ref, seg)
        _to_segments(u_ref, up_ref, seg)
        dyv = dyp_ref[...]
        u = up_ref[...]
        gd_ref[...] += jnp.sum(dyv * u, axis=0, keepdims=True)
        lre_ref[...] = _dot(dyv, cre_ref[0])
        lim_ref[...] = -_dot(dyv, cim_ref[0])
        gcre_ref[0] += _dot(dyv, sre_ref[...], TN)
        gcim_ref[0] -= _dot(dyv, sim_ref[...], TN)
        ar, ai = tab_ref[0, 0], -tab_ref[0, 1]

        def pass1(k, x):
            i = seg - 1 - k
            xr = ar * x[0] - ai * x[1] + lre_ref[_slab(i), :]
            xi = ar * x[1] + ai * x[0] + lim_ref[_slab(i), :]
            lre_ref[_slab(i), :] = xr
            lim_ref[_slab(i), :] = xi
            return xr, xi

        zero = jnp.zeros((SUBLANES, S5_LANES), F32)
        er, ei = lax.fori_loop(0, seg, pass1, (zero, zero))
        cin_r, cin_i = car_ref[0], car_ref[1]
        lr, li = _scan8(er, ei, tab_ref, 10, (7, 6, 4))
        pr, pi = tab_ref[0, 16], tab_ref[0, 17]
        lr, li = lr + pr * cin_r - pi * cin_i, li + pr * cin_i + pi * cin_r
        rows = _iota2((SUBLANES, S5_LANES), 0)
        cr = jnp.where(rows == SUBLANES - 1, cin_r, pltpu.roll(lr, SUBLANES - 1, 0))
        ci = jnp.where(rows == SUBLANES - 1, cin_i, pltpu.roll(li, SUBLANES - 1, 0))
        car_ref[0] = jnp.broadcast_to(lr[0:1, :], lr.shape)
        car_ref[1] = jnp.broadcast_to(li[0:1, :], li.shape)

        first = (t == nt - 1).astype(F32)
        head_re = jnp.broadcast_to(pre_ref[SUBLANES - 1:SUBLANES, :], zero.shape) * (1.0 - first)
        head_im = jnp.broadcast_to(pim_ref[SUBLANES - 1:SUBLANES, :], zero.shape) * (1.0 - first)
        last = _slab(seg - 1)
        sp0_re = jnp.where(rows == 0, head_re, pltpu.roll(sre_ref[last, :], 1, 0))
        sp0_im = jnp.where(rows == 0, head_im, pltpu.roll(sim_ref[last, :], 1, 0))

        def fix(i, acc, sp_re, sp_im):
            j = seg - 1 - i
            qr, qi = pt_ref[0, 0, _slab(j), :], -pt_ref[0, 1, _slab(j), :]
            xr = lre_ref[_slab(i), :] + qr * cr - qi * ci
            xi = lim_ref[_slab(i), :] + qr * ci + qi * cr
            lre_ref[_slab(i), :] = xr
            lim_ref[_slab(i), :] = xi
            return acc[0] + sp_re * xr + sp_im * xi, acc[1] + sp_re * xi - sp_im * xr

        def pass2(i, acc):
            return fix(i, acc, sre_ref[_slab(i - 1), :], sim_ref[_slab(i - 1), :])

        acc_re, acc_im = lax.fori_loop(1, seg, pass2, fix(0, (zero, zero), sp0_re, sp0_im), unroll=2)
        gare_ref[...] += jnp.sum(acc_re, axis=0, keepdims=True)
        gaim_ref[...] += jnp.sum(acc_im, axis=0, keepdims=True)
        lre = lre_ref[...]
        lim = lim_ref[...]
        dup_ref[...] = dyv * d_ref[...] + _dot(lre, bre_ref[0], NT) + _dot(lim, bim_ref[0], NT)
        _from_segments(dup_ref, duo_ref, seg)
        du_ref[...] = duo_ref[...].astype(BF16)
        gbre_ref[0] += _dot(u, lre, TN)
        gbim_ref[0] += _dot(u, lim, TN)

    rt = lambda t: nt - 1 - t
    col = pl.BlockSpec((tb, S5_COLS), lambda j, t: (rt(t), j))
    st = pl.BlockSpec((tb, S5_LANES), lambda j, t: (rt(t), j))
    prev = pl.BlockSpec((SUBLANES, S5_LANES), lambda j, t: (jnp.maximum(rt(t) * tb8 - 1, 0), j))
    bmat = pl.BlockSpec((1, S5_COLS, S5_LANES), lambda j, t: (j, 0, 0))
    cmat = bmat
    return pl.pallas_call(
        body, name="s5_scan_bwd", grid=(nb, nt),
        in_specs=[col, col, st, st, prev, prev, bmat, bmat, cmat, cmat,
                  pl.BlockSpec((1, S5_COLS), lambda j, t: (0, j)),
                  pl.BlockSpec((1, S5_TABS, SUBLANES, S5_LANES), lambda j, t: (j, 0, 0, 0)),
                  pl.BlockSpec((1, 2, tb, S5_LANES), lambda j, t: (j, 0, 0, 0)),
                  pl.BlockSpec(memory_space=pl.ANY)],
        out_specs=[col, pl.BlockSpec((1, S5_COLS), lambda j, t: (0, j)), cmat, cmat, bmat, bmat,
                   pl.BlockSpec((1, S5_LANES), lambda j, t: (0, j)), pl.BlockSpec((1, S5_LANES), lambda j, t: (0, j))],
        input_output_aliases={13: 0},
        out_shape=[jax.ShapeDtypeStruct((L, 2 * DS), BF16), jax.ShapeDtypeStruct((1, DS), F32),
                   jax.ShapeDtypeStruct((nb, S5_COLS, S5_LANES), F32), jax.ShapeDtypeStruct((nb, S5_COLS, S5_LANES), F32),
                   jax.ShapeDtypeStruct((nb, S5_COLS, S5_LANES), F32), jax.ShapeDtypeStruct((nb, S5_COLS, S5_LANES), F32),
                   jax.ShapeDtypeStruct((1, nb * S5_LANES), F32), jax.ShapeDtypeStruct((1, nb * S5_LANES), F32)],
        scratch_shapes=[pltpu.VMEM((tb, S5_LANES), F32), pltpu.VMEM((tb, S5_LANES), F32)]
        + [pltpu.VMEM((tb, S5_COLS), F32)] * 4 + [pltpu.VMEM((2, SUBLANES, S5_LANES), F32)],
        compiler_params=pltpu.CompilerParams(dimension_semantics=("parallel", "arbitrary")),
    )(dy, proj_main, s_re, s_im, s_re, s_im, bbd_re, bbd_im, cbd_re, cbd_im, dvec, tab, ptab, d_s5)


def _s5_post_fwd(y_pre, proj_main, glu_w, glu_b, DS):
    L = y_pre.shape[0]
    tr = _blk(L, 256, SUBLANES)

    def body(y_ref, z_ref, w_ref, b_ref, o_ref, t_ref):
        y1 = _gelu(y_ref[...])
        t = _dot(y1, w_ref[...]) + b_ref[...]
        t_ref[...] = t
        z = z_ref[...]
        o_ref[...] = (y1 * _sigmoid(t) * (z * _sigmoid(z))).astype(BF16)

    row = pl.BlockSpec((tr, DS), lambda i: (i, 0))
    return pl.pallas_call(
        body, name="s5_post_fwd", grid=(L // tr,),
        in_specs=[row, pl.BlockSpec((tr, DS), lambda i: (i, 1)), pl.BlockSpec((DS, DS), lambda i: (0, 0)),
                  pl.BlockSpec((1, DS), lambda i: (0, 0))],
        out_specs=[row, row],
        out_shape=[jax.ShapeDtypeStruct((L, 2 * DS), BF16), jax.ShapeDtypeStruct((L, DS), F32)],
        compiler_params=pltpu.CompilerParams(dimension_semantics=("parallel",)),
    )(y_pre, proj_main, glu_w, glu_b)


def _s5_post_bwd(d_ycat, y_pre, proj_main, t_pre, glu_w, DS):
    L = y_pre.shape[0]
    tr = _blk(L, 256, SUBLANES)

    def body(dy_ref, y_ref, z_ref, t_ref, w_ref, dyp_ref, dz_ref, dt_ref, y1_ref, gb_ref):
        i = pl.program_id(0)

        @pl.when(i == 0)
        def _():
            gb_ref[...] = jnp.zeros_like(gb_ref)

        dy = dy_ref[...]
        yp = y_ref[...]
        z = z_ref[...]
        y1 = _gelu(yp)
        sg = _sigmoid(t_ref[...])
        sz = _sigmoid(z)
        c = y1 * sg
        d_c = dy * (z * sz)
        dz_ref[...] = (dy * c * (sz * (1.0 + z * (1.0 - sz)))).astype(BF16)
        d_t = d_c * y1 * sg * (1.0 - sg)
        gb_ref[...] += jnp.sum(d_t, axis=0, keepdims=True)
        dt_ref[...] = d_t.astype(BF16)
        y1_ref[...] = y1.astype(BF16)
        d_y1 = d_c * sg + _dot(d_t, w_ref[...], NT)
        dyp_ref[...] = d_y1 * _gelu_grad(yp)

    row = pl.BlockSpec((tr, DS), lambda i: (i, 0))
    return pl.pallas_call(
        body, name="s5_post_bwd", grid=(L // tr,),
        in_specs=[row, row, pl.BlockSpec((tr, DS), lambda i: (i, 1)), row, pl.BlockSpec((DS, DS), lambda i: (0, 0))],
        out_specs=[row, pl.BlockSpec((tr, DS), lambda i: (i, 1)), row, row, pl.BlockSpec((1, DS), lambda i: (0, 0))],
        out_shape=[jax.ShapeDtypeStruct((L, DS), F32), jax.ShapeDtypeStruct((L, 2 * DS), BF16),
                   jax.ShapeDtypeStruct((L, DS), BF16), jax.ShapeDtypeStruct((L, DS), BF16),
                   jax.ShapeDtypeStruct((1, DS), F32)],
        compiler_params=pltpu.CompilerParams(dimension_semantics=("arbitrary",)),
    )(d_ycat, y_pre, proj_main, t_pre, glu_w)


def _row_cumsum(x, reverse=False):
    n = x.shape[0]
    row = lax.broadcasted_iota(jnp.int32, x.shape, 0)
    k = 1
    while k < n:
        if reverse:
            x = x + jnp.where(row < n - k, pltpu.roll(x, n - k, 0), 0.0)
        else:
            x = x + jnp.where(row >= k, pltpu.roll(x, k, 0), 0.0)
        k *= 2
    return x


def _gla_gates(glow, gu_ref, gb_ref):
    a = _dot(glow, gu_ref[...]) + gb_ref[...]
    lg = (jnp.minimum(a, 0.0) - jnp.log(1.0 + jnp.exp(-jnp.abs(a)))) * (1.0 / GLA_TAU)
    ri = lax.broadcasted_iota(jnp.int32, (GLA_CHUNK, GLA_CHUNK), 0)
    ci = lax.broadcasted_iota(jnp.int32, (GLA_CHUNK, GLA_CHUNK), 1)
    b = _row_cumsum(lg)
    b_last = b[GLA_CHUNK - 1:GLA_CHUNK, :]
    return a, b, b_last, ri >= ci


def _gla_specs(DS, DK, DV, c, cmap):
    return [
        pl.BlockSpec((c, DK), lambda n: (cmap(n), 2 * DS // DK)),
        pl.BlockSpec((c, DK), lambda n: (cmap(n), 2 * DS // DK + 1)),
        pl.BlockSpec((c, DV), lambda n: (cmap(n), (2 * DS + 2 * DK) // DV)),
        pl.BlockSpec((c, DV), lambda n: (cmap(n), (2 * DS + 2 * DK) // DV + 1)),
    ]


def _gla_fwd(proj_main, proj_low, gate_up_pad, gate_bias, norm_w, ycat, DS, DK, DV):
    L = proj_main.shape[0]
    nc = L // GLA_CHUNK
    cps = math.gcd(GLA_STEP_CHUNKS, nc)
    nh = DK // GLA_HK
    scale = GLA_HK ** -0.5

    def body(q_ref, k_ref, v_ref, z_ref, gl_ref, gu_ref, gb_ref, nw_ref, _yc_ref, y_ref, sp_ref, st_ref):
        n = pl.program_id(0)

        @pl.when(n == 0)
        def _():
            st_ref[...] = jnp.zeros_like(st_ref)

        pairs = [(sc, h) for sc in range(cps) for h in range(nh)]
        rows = lambda sc: slice(sc * GLA_CHUNK, (sc + 1) * GLA_CHUNK)
        kcol = lambda h: slice(h * GLA_HK, (h + 1) * GLA_HK)
        vcol = lambda h: slice(h * GLA_HV, (h + 1) * GLA_HV)
        gates = [_gla_gates(gl_ref[rows(sc), :], gu_ref, gb_ref) for sc in range(cps)]
        qe, dec, o_in, kv = {}, {}, {}, {}
        for sc, h in pairs:
            _, b, b_last, mask = gates[sc]
            bh, bl = b[:, kcol(h)], b_last[:, kcol(h)]
            qe[sc, h] = (q_ref[rows(sc), kcol(h)] * scale) * jnp.exp(bh)
            kh = k_ref[rows(sc), kcol(h)]
            vh = v_ref[rows(sc), vcol(h)]
            attn = jnp.where(mask, _dot(qe[sc, h], kh * jnp.exp(-bh), NT), 0.0)
            o_in[sc, h] = _dot(attn, vh)
            kv[sc, h] = _dot(vh, kh * jnp.exp(bl - bh), TN)
            dec[sc, h] = jnp.exp(bl)
        for sc, h in pairs:
            st = st_ref[h]
            sp_ref[sc, h] = st
            o = o_in[sc, h] + _dot(qe[sc, h], st, NT)
            st_ref[h] = dec[sc, h] * st + kv[sc, h]
            r = lax.rsqrt(jnp.mean(o * o, axis=-1, keepdims=True) + EPS)
            z = z_ref[rows(sc), vcol(h)]
            y_ref[rows(sc), vcol(h)] = (o * r * nw_ref[...] * (z * _sigmoid(z))).astype(BF16)

    c = cps * GLA_CHUNK
    return pl.pallas_call(
        body, name="gla_fwd", grid=(nc // cps,),
        in_specs=_gla_specs(DS, DK, DV, c, lambda n: n) + [
            pl.BlockSpec((c, LANES), lambda n: (n, 0)),
            pl.BlockSpec((LANES, DK), lambda n: (0, 0)),
            pl.BlockSpec((1, DK), lambda n: (0, 0)),
            pl.BlockSpec((1, GLA_HV), lambda n: (0, 0)),
            pl.BlockSpec(memory_space=pl.ANY),
        ],
        out_specs=[pl.BlockSpec((c, DV), lambda n: (n, DS // DV)),
                   pl.BlockSpec((cps, nh, GLA_HV, GLA_HK), lambda n: (n, 0, 0, 0))],
        input_output_aliases={8: 0},
        out_shape=[jax.ShapeDtypeStruct(ycat.shape, BF16), jax.ShapeDtypeStruct((nc, nh, GLA_HV, GLA_HK), F32)],
        scratch_shapes=[pltpu.VMEM((nh, GLA_HV, GLA_HK), F32)],
        compiler_params=pltpu.CompilerParams(dimension_semantics=("arbitrary",)),
    )(proj_main, proj_main, proj_main, proj_main, proj_low, gate_up_pad, gate_bias, norm_w, ycat)


def _gla_bwd(d_ycat, proj_main, proj_low, s_prev, gate_up_pad, gate_bias, norm_w, DS, DK, DV):
    L = proj_main.shape[0]
    nc = L // GLA_CHUNK
    cps = math.gcd(GLA_STEP_CHUNKS, nc)
    nh = DK // GLA_HK
    scale = GLA_HK ** -0.5

    def body(dy_ref, q_ref, k_ref, v_ref, z_ref, gl_ref, sp_ref, gu_ref, gb_ref, nw_ref,
             dg_ref, da_ref, gnw_ref, ggb_ref, dst_ref):
        n = pl.program_id(0)

        @pl.when(n == 0)
        def _():
            dst_ref[...] = jnp.zeros_like(dst_ref)
            gnw_ref[...] = jnp.zeros_like(gnw_ref)
            ggb_ref[...] = jnp.zeros_like(ggb_ref)

        last_row = lax.broadcasted_iota(jnp.int32, (GLA_CHUNK, GLA_HK), 0) == GLA_CHUNK - 1
        nw = nw_ref[...]
        for sc in reversed(range(cps)):
            rs = slice(sc * GLA_CHUNK, (sc + 1) * GLA_CHUNK)
            a, b, b_last, mask = _gla_gates(gl_ref[rs, :], gu_ref, gb_ref)
            for h in range(nh):
                ks = slice(h * GLA_HK, (h + 1) * GLA_HK)
                vs = slice(h * GLA_HV, (h + 1) * GLA_HV)
                bh, bl = b[:, ks], b_last[:, ks]
                e = jnp.exp(bh)
                einv = jnp.exp(-bh)
                etail = jnp.exp(bl - bh)
                dec = jnp.exp(bl)
                qe = (q_ref[rs, ks] * scale) * e
                kh = k_ref[rs, ks]
                ke = kh * einv
                ktail = kh * etail
                vh = v_ref[rs, vs]
                st = sp_ref[sc, h]
                dst = dst_ref[h]
                attn = jnp.where(mask, _dot(qe, ke, NT), 0.0)
                o = _dot(attn, vh) + _dot(qe, st, NT)
                r = lax.rsqrt(jnp.mean(o * o, axis=-1, keepdims=True) + EPS)
                nrm = o * r
                z = z_ref[rs, vs]
                sz = _sigmoid(z)
                dy = dy_ref[rs, vs]
                dg_ref[rs, 2 * DK + DV + h * GLA_HV:2 * DK + DV + (h + 1) * GLA_HV] = (
                    dy * nrm * nw * (sz * (1.0 + z * (1.0 - sz)))).astype(BF16)
                d_on = dy * (z * sz)
                gnw_ref[...] += jnp.sum(d_on * nrm, axis=0, keepdims=True)
                d_n = d_on * nw
                d_o = r * (d_n - nrm * jnp.mean(d_n * nrm, axis=-1, keepdims=True))
                d_attn = jnp.where(mask, _dot(d_o, vh, NT), 0.0)
                dg_ref[rs, 2 * DK + h * GLA_HV:2 * DK + (h + 1) * GLA_HV] = (
                    _dot(attn, d_o, TN) + _dot(ktail, dst, NT)).astype(BF16)
                d_qe = _dot(d_attn, ke) + _dot(d_o, st)
                d_ke = _dot(d_attn, qe, TN)
                d_kt = _dot(vh, dst)
                d_dec = jnp.sum(dst * st, axis=0, keepdims=True)
                dst_ref[h] = dec * dst + _dot(d_o, qe, TN)
                dg_ref[rs, ks] = (d_qe * scale * e).astype(BF16)
                dg_ref[rs, DK + h * GLA_HK:DK + (h + 1) * GLA_HK] = (d_ke * einv + d_kt * etail).astype(BF16)
                d_bl = jnp.sum(d_kt * ktail, axis=0, keepdims=True) + d_dec * dec
                d_b = d_qe * qe - d_ke * ke - d_kt * ktail + jnp.where(last_row, d_bl, 0.0)
                d_lg = _row_cumsum(d_b, reverse=True)
                d_a = d_lg * (1.0 / GLA_TAU) * _sigmoid(-a[:, ks])
                ggb_ref[:, ks] += jnp.sum(d_a, axis=0, keepdims=True)
                da_ref[rs, ks] = d_a.astype(BF16)

    c = cps * GLA_CHUNK
    ns = nc // cps
    rn = lambda n: ns - 1 - n
    return pl.pallas_call(
        body, name="gla_bwd", grid=(ns,),
        in_specs=[pl.BlockSpec((c, DV), lambda n: (rn(n), DS // DV))] + _gla_specs(DS, DK, DV, c, rn) + [
            pl.BlockSpec((c, LANES), lambda n: (rn(n), 0)),
            pl.BlockSpec((cps, nh, GLA_HV, GLA_HK), lambda n: (rn(n), 0, 0, 0)),
            pl.BlockSpec((LANES, DK), lambda n: (0, 0)),
            pl.BlockSpec((1, DK), lambda n: (0, 0)),
            pl.BlockSpec((1, GLA_HV), lambda n: (0, 0)),
        ],
        out_specs=[pl.BlockSpec((c, 2 * DK + 2 * DV), lambda n: (rn(n), 0)),
                   pl.BlockSpec((c, DK), lambda n: (rn(n), 0)),
                   pl.BlockSpec((1, GLA_HV), lambda n: (0, 0)), pl.BlockSpec((1, DK), lambda n: (0, 0))],
        out_shape=[jax.ShapeDtypeStruct((L, 2 * DK + 2 * DV), BF16),
                   jax.ShapeDtypeStruct((L, DK), BF16),
                   jax.ShapeDtypeStruct((1, GLA_HV), F32), jax.ShapeDtypeStruct((1, DK), F32)],
        scratch_shapes=[pltpu.VMEM((nh, GLA_HV, GLA_HK), F32)],
        compiler_params=pltpu.CompilerParams(dimension_semantics=("arbitrary",)),
    )(d_ycat, proj_main, proj_main, proj_main, proj_main, proj_low, s_prev, gate_up_pad, gate_bias, norm_w)


def _adamw_math(w, g, m, v):
    c1 = 1.0 - ADAM_B1 ** ADAM_STEP
    c2 = 1.0 - ADAM_B2 ** ADAM_STEP
    m_ = ADAM_B1 * m + (1.0 - ADAM_B1) * g
    v_ = ADAM_B2 * v + (1.0 - ADAM_B2) * (g * g)
    return -ADAM_LR * ((m_ / c1) / (jnp.sqrt(v_ / c2) + ADAM_EPS) + ADAM_WD * w), m_, v_


def _adamw_small(g_row, g_a, g_bc, ws, ms, vs):
    n = len(ws)
    nvec = n - 6

    def body(*refs):
        grow_ref, ga_ref, gbc_ref = refs[:3]
        w_refs, m_refs, v_refs = refs[3:3 + n], refs[3 + n:3 + 2 * n], refs[3 + 2 * n:3 + 3 * n]
        outs = refs[3 + 3 * n:]
        off = 0
        for i in range(n):
            if i < nvec:
                width = ws[i].shape[1]
                g = grow_ref[:, off:off + width]
                off += width
            elif i < nvec + 2:
                g = ga_ref[i - nvec]
            else:
                g = gbc_ref[i - nvec - 2]
            d, m_, v_ = _adamw_math(w_refs[i][...], g, m_refs[i][...], v_refs[i][...])
            outs[i][...] = g
            outs[n + i][...] = d
            outs[2 * n + i][...] = m_
            outs[3 * n + i][...] = v_

    vm = pl.BlockSpec(memory_space=pltpu.VMEM)
    outs = pl.pallas_call(
        body, name="adamw_small",
        in_specs=[vm] * (3 + 3 * n), out_specs=[vm] * (4 * n),
        out_shape=[jax.ShapeDtypeStruct(w.shape, F32) for w in ws] * 4,
    )(g_row, g_a, g_bc, *ws, *ms, *vs)
    return [outs[k * n:(k + 1) * n] for k in range(4)]


def _my_pos():
    return lax.axis_index("x"), lax.axis_index("y"), lax.axis_index("c")


def _split_start(name, srcs, lands_sd, make_copies, ncopies, after):
    n, m = len(srcs), len(lands_sd)

    def body(*refs):
        send_sems, recv_sems = refs[n + m + len(after)], refs[n + m + len(after) + 1]
        for cp in make_copies(refs[:n], refs[n:n + m], send_sems, recv_sems):
            cp.start()
        refs[-1][...] = jnp.zeros_like(refs[-1])

    hbm = pl.BlockSpec(memory_space=pltpu.HBM)
    sem = pl.BlockSpec(memory_space=pltpu.SEMAPHORE)
    outs = pl.pallas_call(
        body, name=name,
        in_specs=[hbm] * (n + m) + [pl.BlockSpec(memory_space=pl.ANY)] * len(after),
        out_specs=[sem, sem] + [hbm] * (n + m) + [pl.BlockSpec(memory_space=pltpu.VMEM)],
        out_shape=[pltpu.SemaphoreType.DMA((ncopies,)), pltpu.SemaphoreType.DMA((ncopies,))]
        + [pltpu.HBM(s.shape, s.dtype) for s in srcs] + [pltpu.HBM(s.shape, s.dtype) for s in lands_sd]
        + [jax.ShapeDtypeStruct((SUBLANES, LANES), F32)],
        input_output_aliases={i: 2 + i for i in range(n + m)},
        compiler_params=pltpu.CompilerParams(has_side_effects=pltpu.SideEffectType.DATAFLOW_SIDE_EFFECTING),
    )(*[pltpu.with_memory_space_constraint(s, pltpu.HBM) for s in srcs],
      *[pltpu.with_memory_space_constraint(lax.empty(s.shape, s.dtype), pltpu.HBM) for s in lands_sd], *after)
    return outs[0], outs[1], outs[2:2 + n], outs[2 + n:2 + n + m], outs[-1]


def _split_wait(name, send_sems, recv_sems, srcs, lands, make_copies, after):
    n, m = len(srcs), len(lands)

    def body(*refs):
        for cp in make_copies(refs[:n], refs[n:n + m], refs[n + m], refs[n + m + 1]):
            cp.wait_send()
            cp.wait_recv()

    hbm = pl.BlockSpec(memory_space=pltpu.HBM)
    sem = pl.BlockSpec(memory_space=pltpu.SEMAPHORE)
    outs = pl.pallas_call(
        body, name=name,
        in_specs=[hbm] * (n + m) + [sem, sem] + [pl.BlockSpec(memory_space=pl.ANY)] * len(after),
        out_specs=[hbm] * (n + m),
        out_shape=[pltpu.HBM(s.shape, s.dtype) for s in srcs] + [pltpu.HBM(p.shape, p.dtype) for p in lands],
        input_output_aliases={i: i for i in range(n + m)},
        compiler_params=pltpu.CompilerParams(has_side_effects=pltpu.SideEffectType.DATAFLOW_SIDE_EFFECTING),
    )(*srcs, *lands, send_sems, recv_sems, *after)
    return outs[:n], outs[n:]


def _late_gather_copies(srcs, lands, send_sems, recv_sems):
    x, y, c = _my_pos()
    me = 2 * x + y
    copies = []
    for d in (1, 2, 3):
        to = (x ^ (d >> 1), y ^ (d & 1), c)
        for a in range(len(srcs)):
            hrows = srcs[a].shape[0] // 2
            rows = pl.ds(c * hrows, hrows)
            copies.append(pltpu.make_async_remote_copy(
                src_ref=srcs[a].at[rows, :], dst_ref=lands[a].at[me, rows, :], send_sem=send_sems.at[3 * a + d - 1],
                recv_sem=recv_sems.at[3 * a + d - 1], device_id=to, device_id_type=MESH))
    return copies


def _late_gather_start(shards, after, name):
    n = len(shards)

    def body(*refs):
        srcs, lands = refs[:n], refs[n:2 * n]
        send_sems, recv_sems = refs[2 * n + 1], refs[2 * n + 2]
        token = refs[-1]
        for cp in _late_gather_copies(srcs, lands, send_sems, recv_sems):
            cp.start()
        token[...] = jnp.zeros_like(token)

    hbm = pl.BlockSpec(memory_space=pltpu.HBM)
    sem = pl.BlockSpec(memory_space=pltpu.SEMAPHORE)
    outs = pl.pallas_call(
        body, name=name,
        in_specs=[hbm] * (2 * n) + [pl.BlockSpec(memory_space=pl.ANY)],
        out_specs=[sem, sem] + [hbm] * (2 * n) + [pl.BlockSpec(memory_space=pltpu.VMEM)],
        out_shape=[pltpu.SemaphoreType.DMA((3 * n,)), pltpu.SemaphoreType.DMA((3 * n,))]
        + [pltpu.HBM(s.shape, s.dtype) for s in shards]
        + [pltpu.HBM((4,) + s.shape, s.dtype) for s in shards]
        + [jax.ShapeDtypeStruct((SUBLANES, LANES), F32)],
        input_output_aliases={i: 2 + i for i in range(2 * n)},
        compiler_params=pltpu.CompilerParams(has_side_effects=pltpu.SideEffectType.DATAFLOW_SIDE_EFFECTING),
    )(*[pltpu.with_memory_space_constraint(s, pltpu.HBM) for s in shards],
      *[pltpu.with_memory_space_constraint(lax.empty((4,) + s.shape, s.dtype), pltpu.HBM) for s in shards], after)
    return outs[0], outs[1], outs[2:2 + n], outs[2 + n:2 + 2 * n], outs[-1]


def _late_gather_wait(send_sems, recv_sems, shards, lands, after, name):
    n = len(shards)

    def body(*refs):
        src_refs, land_refs = refs[:n], refs[n:2 * n]
        ssem, rsem = refs[2 * n], refs[2 * n + 1]
        for cp in _late_gather_copies(src_refs, land_refs, ssem, rsem):
            cp.wait_send()
            cp.wait_recv()

    hbm = pl.BlockSpec(memory_space=pltpu.HBM)
    sem = pl.BlockSpec(memory_space=pltpu.SEMAPHORE)
    outs = pl.pallas_call(
        body, name=name,
        in_specs=[hbm] * (2 * n) + [sem, sem] + [pl.BlockSpec(memory_space=pl.ANY)] * len(after),
        out_specs=[hbm] * (2 * n),
        out_shape=[pltpu.HBM(s.shape, s.dtype) for s in shards] + [pltpu.HBM(p.shape, p.dtype) for p in lands],
        input_output_aliases={i: i for i in range(2 * n)},
        compiler_params=pltpu.CompilerParams(has_side_effects=pltpu.SideEffectType.DATAFLOW_SIDE_EFFECTING),
    )(*shards, *lands, send_sems, recv_sems, *after)
    return outs[n:]


def _late_gather_pair(lands, name):
    n = len(lands)

    def body(*refs):
        outs = refs[n:2 * n]
        send_sems, recv_sems = refs[2 * n:]
        x, y, c = _my_pos()

        def copy(a, d, half):
            chip = 2 * (x ^ (d >> 1)) + (y ^ (d & 1))
            hrows = lands[a].shape[1] // 2
            sl = outs[a].at[chip, pl.ds(half * hrows, hrows), :]
            return pltpu.make_async_remote_copy(src_ref=sl, dst_ref=sl, send_sem=send_sems.at[3 * a + d - 1],
                                                recv_sem=recv_sems.at[3 * a + d - 1], device_id=(x, y, 1 - c),
                                                device_id_type=MESH)

        pairs = [(a, d) for d in (1, 2, 3) for a in range(n)]
        for a, d in pairs:
            copy(a, d, c).start()
        for a, d in pairs:
            copy(a, d, c).wait_send()
            copy(a, d, 1 - c).wait_recv()

    hbm = pl.BlockSpec(memory_space=pltpu.HBM)
    return pl.pallas_call(
        body, name=name, in_specs=[hbm] * n, out_specs=[hbm] * n,
        out_shape=[jax.ShapeDtypeStruct(p.shape, p.dtype) for p in lands],
        input_output_aliases={i: i for i in range(n)},
        scratch_shapes=[pltpu.SemaphoreType.DMA((3 * n,)), pltpu.SemaphoreType.DMA((3 * n,))],
    )(*lands)


def _pair_exchange(gs):
    n = len(gs)

    def body(*refs):
        ins, outs = refs[:n], refs[n:2 * n]
        send_sems, recv_sems = refs[2 * n:]
        x, y, c = _my_pos()
        sent = []
        for a in range(n):
            hrows = gs[a].shape[1] // 2
            cp = pltpu.make_async_remote_copy(
                src_ref=ins[a].at[:, pl.ds((1 - c) * hrows, hrows), :], dst_ref=outs[a], send_sem=send_sems.at[a],
                recv_sem=recv_sems.at[a], device_id=(x, y, 1 - c), device_id_type=MESH)
            cp.start()
            sent.append(cp)
        for cp in sent:
            cp.wait()

    hbm = pl.BlockSpec(memory_space=pltpu.HBM)
    return pl.pallas_call(
        body, name="grad_pair_exchange", in_specs=[hbm] * n, out_specs=[hbm] * n,
        out_shape=[jax.ShapeDtypeStruct((g.shape[0], g.shape[1] // 2, g.shape[2]), g.dtype) for g in gs],
        scratch_shapes=[pltpu.SemaphoreType.DMA((n,)), pltpu.SemaphoreType.DMA((n,))],
    )(*gs)


def _pair_add(g, got, c_arr, name):
    nk, rows2, cols = g.shape
    hrows = rows2 // 2
    tr = _blk(hrows, 256, 2 * SUBLANES)
    nb = hrows // tr

    def body(c_ref, a_ref, b_ref, o_ref):
        o_ref[...] = (a_ref[...].astype(F32) + b_ref[...].astype(F32)).astype(o_ref.dtype)

    return pl.pallas_call(
        body, name=name,
        grid_spec=pltpu.PrefetchScalarGridSpec(
            num_scalar_prefetch=1, grid=(nk, nb),
            in_specs=[pl.BlockSpec((1, tr, cols), lambda k, i, c_ref: (k, c_ref[0] * nb + i, 0)),
                      pl.BlockSpec((1, tr, cols), lambda k, i, c_ref: (k, i, 0))],
            out_specs=pl.BlockSpec((1, tr, cols), lambda k, i, c_ref: (k, i, 0))),
        out_shape=jax.ShapeDtypeStruct((nk, hrows, cols), g.dtype),
        compiler_params=pltpu.CompilerParams(dimension_semantics=("parallel", "parallel")),
    )(c_arr, g, got)


def _chip_scatter_copies(srcs, lands, send_sems, recv_sems):
    x, y, c = _my_pos()
    copies = []
    for d in (1, 2, 3):
        tx, ty = x ^ (d >> 1), y ^ (d & 1)
        for a in range(len(srcs)):
            copies.append(pltpu.make_async_remote_copy(
                src_ref=srcs[a].at[2 * tx + ty], dst_ref=lands[a].at[d - 1], send_sem=send_sems.at[3 * a + d - 1],
                recv_sem=recv_sems.at[3 * a + d - 1], device_id=(tx, ty, c), device_id_type=MESH))
    return copies


def _chip_scatter_start(pss):
    n = len(pss)

    def body(*refs):
        srcs, lands = refs[:n], refs[n:2 * n]
        send_sems, recv_sems = refs[2 * n], refs[2 * n + 1]
        token = refs[-1]
        for cp in _chip_scatter_copies(srcs, lands, send_sems, recv_sems):
            cp.start()
        token[...] = jnp.zeros_like(token)

    hbm = pl.BlockSpec(memory_space=pltpu.HBM)
    sem = pl.BlockSpec(memory_space=pltpu.SEMAPHORE)
    land_shapes = [(3,) + p.shape[1:] for p in pss]
    outs = pl.pallas_call(
        body, name="grad_chip_scatter_start",
        in_specs=[hbm] * (2 * n),
        out_specs=[sem, sem] + [hbm] * (2 * n) + [pl.BlockSpec(memory_space=pltpu.VMEM)],
        out_shape=[pltpu.SemaphoreType.DMA((3 * n,)), pltpu.SemaphoreType.DMA((3 * n,))]
        + [pltpu.HBM(p.shape, p.dtype) for p in pss]
        + [pltpu.HBM(s, p.dtype) for s, p in zip(land_shapes, pss)]
        + [jax.ShapeDtypeStruct((SUBLANES, LANES), F32)],
        input_output_aliases={i: 2 + i for i in range(2 * n)},
        compiler_params=pltpu.CompilerParams(has_side_effects=pltpu.SideEffectType.DATAFLOW_SIDE_EFFECTING),
    )(*[pltpu.with_memory_space_constraint(p, pltpu.HBM) for p in pss],
      *[pltpu.with_memory_space_constraint(lax.empty(s, p.dtype), pltpu.HBM) for s, p in zip(land_shapes, pss)])
    return outs[0], outs[1], outs[2:2 + n], outs[2 + n:2 + 2 * n], outs[-1]


def _chip_scatter_wait(send_sems, recv_sems, srcs, lands, after):
    n = len(srcs)

    def body(*refs):
        src_refs, land_refs = refs[:n], refs[n:2 * n]
        ssem, rsem = refs[2 * n], refs[2 * n + 1]
        for cp in _chip_scatter_copies(src_refs, land_refs, ssem, rsem):
            cp.wait_send()
            cp.wait_recv()

    hbm = pl.BlockSpec(memory_space=pltpu.HBM)
    sem = pl.BlockSpec(memory_space=pltpu.SEMAPHORE)
    outs = pl.pallas_call(
        body, name="grad_chip_scatter_wait",
        in_specs=[hbm] * (2 * n) + [sem, sem, pl.BlockSpec(memory_space=pl.ANY)],
        out_specs=[hbm] * (2 * n),
        out_shape=[pltpu.HBM(p.shape, p.dtype) for p in srcs] + [pltpu.HBM(p.shape, p.dtype) for p in lands],
        input_output_aliases={i: i for i in range(2 * n)},
        compiler_params=pltpu.CompilerParams(has_side_effects=pltpu.SideEffectType.DATAFLOW_SIDE_EFFECTING),
    )(*srcs, *lands, send_sems, recv_sems, after)
    return outs[:n], outs[n:]


def _chip_sum(ps, got, me_arr, name):
    _, hrows, cols = ps.shape
    tr = _blk(hrows, 256, 2 * SUBLANES)

    def body(me_ref, p_ref, g_ref, o_ref):
        acc = p_ref[0].astype(F32)
        for s in range(3):
            acc = acc + g_ref[s].astype(F32)
        o_ref[...] = acc

    return pl.pallas_call(
        body, name=name,
        grid_spec=pltpu.PrefetchScalarGridSpec(
            num_scalar_prefetch=1, grid=(hrows // tr,),
            in_specs=[pl.BlockSpec((1, tr, cols), lambda i, me_ref: (me_ref[0], i, 0)),
                      pl.BlockSpec((3, tr, cols), lambda i, me_ref: (0, i, 0))],
            out_specs=pl.BlockSpec((tr, cols), lambda i, me_ref: (i, 0))),
        out_shape=jax.ShapeDtypeStruct((hrows, cols), F32),
        compiler_params=pltpu.CompilerParams(dimension_semantics=("parallel",)),
    )(me_arr, ps, got)


def _pair_swap(halves):
    n = len(halves)

    def body(*refs):
        ins, outs = refs[:n], refs[n:2 * n]
        send_sems, recv_sems = refs[2 * n:]
        x, y, c = _my_pos()
        sent = []
        for a in range(n):
            cp = pltpu.make_async_remote_copy(src_ref=ins[a], dst_ref=outs[a], send_sem=send_sems.at[a], recv_sem=recv_sems.at[a],
                                              device_id=(x, y, 1 - c), device_id_type=MESH)
            cp.start()
            sent.append(cp)
        for cp in sent:
            cp.wait()

    hbm = pl.BlockSpec(memory_space=pltpu.HBM)
    return pl.pallas_call(
        body, name="grad_pair_swap", in_specs=[hbm] * n, out_specs=[hbm] * n,
        out_shape=[jax.ShapeDtypeStruct(h.shape, h.dtype) for h in halves],
        scratch_shapes=[pltpu.SemaphoreType.DMA((n,)), pltpu.SemaphoreType.DMA((n,))],
    )(*halves)


def _adamw_sharded(w, g_own, g_other, m, v, c_arr, after, name):
    R, C = w.shape
    hrows = R // 2
    tr = _blk(hrows, 256, SUBLANES)
    nbh = hrows // tr

    def body(c_ref, w_ref, go_ref, gx_ref, m_ref, v_ref, _after_ref, g_ref, d_ref, nm_ref, nv_ref):
        mine = (pl.program_id(0) // nbh) == c_ref[0]
        g_ = jnp.where(mine, go_ref[...], gx_ref[...])
        g_ref[...] = g_
        d_ref[...], nm_ref[...], nv_ref[...] = _adamw_math(w_ref[...], g_, m_ref[...], v_ref[...])

    blk = pl.BlockSpec((tr, C), lambda i, c_ref: (i, 0))
    hblk = pl.BlockSpec((tr, C), lambda i, c_ref: (i % nbh, 0))
    sd = jax.ShapeDtypeStruct((R, C), F32)
    return pl.pallas_call(
        body, name=name,
        grid_spec=pltpu.PrefetchScalarGridSpec(
            num_scalar_prefetch=1, grid=(2 * nbh,),
            in_specs=[blk, hblk, hblk, blk, blk, pl.BlockSpec(memory_space=pl.ANY)], out_specs=[blk] * 4),
        out_shape=[sd] * 4,
        compiler_params=pltpu.CompilerParams(dimension_semantics=("parallel",)),
    )(c_arr, w, g_own, g_other, m, v, after)


def _ar_piece(ref, rows, p):
    start = p * rows
    if rows % SUBLANES == 0:
        start = pl.multiple_of(start, SUBLANES)
    return ref.at[..., pl.ds(start, rows), :]


def _ar_peer(d):
    x, y, c = _my_pos()
    return (x ^ (d >> 2), y ^ ((d >> 1) & 1), c ^ (d & 1))


def _ar_lin(p):
    return 4 * p[0] + 2 * p[1] + p[2]


def _ar_scatter_copies(rows):
    def make(srcs, lands, send_sems, recv_sems):
        n = len(srcs)
        copies = []
        for d in range(1, 8):
            to = _ar_peer(d)
            for a in range(n):
                copies.append(pltpu.make_async_remote_copy(
                    src_ref=_ar_piece(srcs[a], rows[a], _ar_lin(to)), dst_ref=lands[a].at[d],
                    send_sem=send_sems.at[(d - 1) * n + a], recv_sem=recv_sems.at[(d - 1) * n + a], device_id=to,
                    device_id_type=MESH))
        return copies
    return make


def _ar_gather_copies(rows):
    def make(srcs, lands, send_sems, recv_sems):
        n = len(srcs)
        me = _ar_lin(_my_pos())
        copies = []
        for d in range(1, 8):
            for a in range(n):
                copies.append(pltpu.make_async_remote_copy(
                    src_ref=srcs[a], dst_ref=_ar_piece(lands[a], rows[a], me),
                    send_sem=send_sems.at[(d - 1) * n + a], recv_sem=recv_sems.at[(d - 1) * n + a], device_id=_ar_peer(d),
                    device_id_type=MESH))
        return copies
    return make


def _ar_sum(srcs, lands, rows):
    n = len(srcs)

    def body(*refs):
        me = _ar_lin(_my_pos())
        for a in range(n):
            acc = _ar_piece(refs[a], rows[a], me)[...]
            for d in range(1, 8):
                acc = acc + refs[n + a][d]
            refs[2 * n + a][...] = acc

    vm = pl.BlockSpec(memory_space=pltpu.VMEM)
    return pl.pallas_call(
        body, name="allreduce_sum", in_specs=[vm] * (2 * n), out_specs=[vm] * n,
        out_shape=[jax.ShapeDtypeStruct(p.shape[1:], F32) for p in lands],
    )(*srcs, *lands)


def kernel(x, pre_norm_w, w_in, s5_A_re, s5_A_im, s5_B_re, s5_B_im, s5_C_re, s5_C_im, s5_D, s5_log_dt, s5_glu_w, s5_glu_b, gla_gate_up, gla_gate_bias, gla_norm_w, w_out, post_norm_w, loss_target, m_pre_norm_w, m_w_in, m_s5_A_re, m_s5_A_im, m_s5_B_re, m_s5_B_im, m_s5_C_re, m_s5_C_im, m_s5_D, m_s5_log_dt, m_s5_glu_w, m_s5_glu_b, m_gla_gate_up, m_gla_gate_bias, m_gla_norm_w, m_w_out, m_post_norm_w, v_pre_norm_w, v_w_in, v_s5_A_re, v_s5_A_im, v_s5_B_re, v_s5_B_im, v_s5_C_re, v_s5_C_im, v_s5_D, v_s5_log_dt, v_s5_glu_w, v_s5_glu_b, v_gla_gate_up, v_gla_gate_bias, v_gla_norm_w, v_w_out, v_post_norm_w):
    names = ["pre_norm_w", "w_in", "s5_A_re", "s5_A_im", "s5_B_re", "s5_B_im", "s5_C_re", "s5_C_im", "s5_D", "s5_log_dt",
             "s5_glu_w", "s5_glu_b", "gla_gate_up", "gla_gate_bias", "gla_norm_w", "w_out", "post_norm_w"]
    W = dict(zip(names, (pre_norm_w, w_in, s5_A_re, s5_A_im, s5_B_re, s5_B_im, s5_C_re, s5_C_im, s5_D, s5_log_dt,
                         s5_glu_w, s5_glu_b, gla_gate_up, gla_gate_bias, gla_norm_w, w_out, post_norm_w)))
    M = dict(zip(names, (m_pre_norm_w, m_w_in, m_s5_A_re, m_s5_A_im, m_s5_B_re, m_s5_B_im, m_s5_C_re, m_s5_C_im, m_s5_D,
                         m_s5_log_dt, m_s5_glu_w, m_s5_glu_b, m_gla_gate_up, m_gla_gate_bias, m_gla_norm_w, m_w_out,
                         m_post_norm_w)))
    V = dict(zip(names, (v_pre_norm_w, v_w_in, v_s5_A_re, v_s5_A_im, v_s5_B_re, v_s5_B_im, v_s5_C_re, v_s5_C_im, v_s5_D,
                         v_s5_log_dt, v_s5_glu_w, v_s5_glu_b, v_gla_gate_up, v_gla_gate_bias, v_gla_norm_w, v_w_out,
                         v_post_norm_w)))
    sharded = ("w_in", "s5_glu_w", "w_out", "gla_gate_up")

    xb = x[0]
    tgt = loss_target[0]
    L, D = xb.shape
    DS = D // 2
    G = DS // S5_GROUP
    P = S5_STATE
    NB = DS // S5_COLS
    DV = D - DS
    DK = DV // 2
    WM = 2 * DS + 2 * DK + 2 * DV
    nsh = w_in.shape[2]

    chip = 2 * lax.axis_index("x") + lax.axis_index("y")
    own = [jnp.pad(w_in[0].astype(BF16), ((0, 0), (0, -nsh % LANES))), s5_glu_w[0].astype(BF16),
           w_out[0].astype(BF16), gla_gate_up[0]]
    fill = lambda g, o: lax.dynamic_update_index_in_dim(g, o, chip, 0)
    win_ss, win_rs, win_src, win_lands, win_token = _late_gather_start(own[:1], pre_norm_w, "w_in_gather_start")
    h = _prenorm_fwd(xb, pre_norm_w, win_token)

    b_view = lambda t: jnp.transpose(t[0], (0, 2, 1)).reshape(G * S5_GROUP, P)
    b_back = lambda t: jnp.transpose(t.reshape(G, S5_GROUP, P), (0, 2, 1))[None]
    c_view = lambda t: t[0].reshape(G * S5_GROUP, P)
    c_back = lambda t: t.reshape(1, G, S5_GROUP, P)
    small = ["pre_norm_w", "post_norm_w", "s5_D", "s5_glu_b", "gla_gate_bias", "gla_norm_w", "s5_log_dt",
             "s5_A_re", "s5_A_im", "s5_B_re", "s5_B_im", "s5_C_re", "s5_C_im"]
    view = {n: (lambda t: t) for n in small[:7]}
    back = dict(view)
    view.update(s5_A_re=lambda t: t[0], s5_A_im=lambda t: t[0], s5_B_re=b_view, s5_B_im=b_view, s5_C_re=c_view, s5_C_im=c_view)
    back.update(s5_A_re=lambda t: t[None], s5_A_im=lambda t: t[None], s5_B_re=b_back, s5_B_im=b_back, s5_C_re=c_back,
                s5_C_im=c_back)
    Wv = {n: view[n](W[n]) for n in small}
    bbd_re, bbd_im, ct_re, ct_im, tab, ptab = _s5_prep_fwd(
        Wv["s5_A_re"], Wv["s5_A_im"], s5_log_dt, Wv["s5_B_re"], Wv["s5_B_im"], Wv["s5_C_re"], Wv["s5_C_im"],
        h, _blk(L, S5_TIME_BLOCK, SUBLANES) // SUBLANES)
    dvec = s5_D

    for d_ in (W, M, V):
        d_["w_in"], _ = lax.optimization_barrier((d_["w_in"], win_token))
    g_win = _late_gather_wait(win_ss, win_rs, win_src, win_lands,
                              [tab, W["w_in"][0], M["w_in"][0], V["w_in"][0]], "w_in_gather_wait")
    g_win = fill(_late_gather_pair(g_win, "w_in_gather_pair")[0], own[0])
    w_main, w_low = _assemble_w_in(g_win, nsh, WM)
    late_ss, late_rs, late_src, late_lands, late_token = _late_gather_start(own[1:], g_win, "late_gather_start")
    proj_main, proj_low = _in_proj(h, w_main, w_low, late_token)
    y_pre, s_re, s_im = _s5_scan_fwd(proj_main, bbd_re, bbd_im, ct_re, ct_im, dvec, tab, ptab, DS)
    late = _late_gather_wait(late_ss, late_rs, late_src, late_lands, [y_pre], "late_gather_wait")
    late = _late_gather_pair(late, "late_gather_pair")
    g_glu, g_wout, g_gup = [fill(g, o) for g, o in zip(late, own[1:])]
    glu_w = g_glu.reshape(DS, DS)
    wout = g_wout.reshape(D, D)
    gup = jnp.moveaxis(g_gup, 0, 1).reshape(GLA_RANK, DK)
    gup_pad = jnp.pad(gup, ((0, LANES - GLA_RANK), (0, 0))).astype(BF16)
    ycat, t_pre = _s5_post_fwd(y_pre, proj_main, glu_w, s5_glu_b, DS)
    ycat, s_prev = _gla_fwd(proj_main, proj_low, gup_pad, gla_gate_bias, gla_norm_w, ycat, DS, DK, DV)
    mixed = _mm(ycat, wout, name="out_proj")
    loss11, d_mixed, dout, g_post_w = _post_fwd_bwd(mixed, xb, tgt, post_norm_w)

    d_ycat = _mm(d_mixed, wout, tb=True, name="out_proj_dx")
    g_wout_full = _mm(ycat, d_mixed, ta=True, out_dtype=BF16, name="out_proj_dw")
    d_ypre, d_s5, d_t, y1, g_glu_b = _s5_post_bwd(d_ycat, y_pre, proj_main, t_pre, glu_w, DS)
    g_glu_full = _mm(y1, d_t, ta=True, out_dtype=BF16, name="glu_dw")
    d_s5, g_D, gct_re, gct_im, gbbd_re, gbbd_im, gab_re, gab_im = _s5_scan_bwd(
        d_ypre, proj_main, s_re, s_im, bbd_re, bbd_im, ct_re, ct_im, dvec, tab, ptab, d_s5, DS)
    d_gla, d_a, g_norm_w, g_gate_bias = _gla_bwd(
        d_ycat, proj_main, proj_low, s_prev, gup_pad, gla_gate_bias, gla_norm_w, DS, DK, DV)
    d_low = _mm(d_a, gup_pad, tb=True, out_dtype=BF16, name="gate_dx")
    g_gup_pad = _mm(proj_low, d_a, ta=True, name="gate_dw")
    g_wmain, g_wlow = _in_proj_dw(h, d_s5, d_gla, d_low)

    gs = [_split_w_in_grad(g_wmain, g_wlow, nsh),
          g_glu_full.reshape(4, DS // 4, DS),
          g_wout_full.reshape(4, D // 4, D),
          jnp.moveaxis(g_gup_pad[:GLA_RANK].reshape(GLA_RANK, 4, DK // 4), 1, 0)]
    c_arr = lax.axis_index("c").astype(jnp.int32).reshape(1)
    me_arr = chip.astype(jnp.int32).reshape(1)
    got = _pair_exchange(gs)
    pss = [_pair_add(g, r, c_arr, "grad_pair_add_" + n) for n, g, r in zip(sharded, gs, got)]
    send_sems, recv_sems, pss, lands, token = _chip_scatter_start(pss)

    dh = _in_proj_dx(d_s5, d_gla, d_low, w_main, w_low, token)
    grad_x, g_pre_w = _prenorm_bwd(xb, dh, dout, pre_norm_w)

    g_a, g_bc, g_ldt = _s5_prep_bwd(Wv["s5_A_re"], Wv["s5_A_im"], s5_log_dt, Wv["s5_B_re"], Wv["s5_B_im"],
                                    gbbd_re, gbbd_im, gct_re, gct_im, gab_re, gab_im)

    g_vecs = jnp.concatenate([g_pre_w, g_post_w, g_D, g_glu_b, g_gate_bias, g_norm_w, g_ldt, loss11], axis=1)
    loss_at = g_vecs.shape[1] - 1
    lanes_pad = -g_vecs.shape[1] % (8 * SUBLANES * LANES)
    g_vecs = jnp.pad(g_vecs, ((0, 0), (0, lanes_pad))).reshape(-1, LANES)
    ar_srcs = [g_vecs, g_a, g_bc]
    ar_rows = [a.shape[-2] // 8 for a in ar_srcs]
    ar_lands = [jax.ShapeDtypeStruct((8,) + a.shape[:-2] + (r, a.shape[-1]), F32) for a, r in zip(ar_srcs, ar_rows)]
    ar_ss, ar_rs, ar_srcs, ar_got, ar_token = _split_start(
        "allreduce_scatter_start", ar_srcs, ar_lands, _ar_scatter_copies(ar_rows), 7 * len(ar_srcs), [])

    pss, rcv = _chip_scatter_wait(send_sems, recv_sems, pss, lands, ar_token)
    halves = [_chip_sum(p, r, me_arr, "grad_chip_sum_" + n) for n, p, r in zip(sharded, pss, rcv)]
    others = _pair_swap(halves)
    ar_srcs, ar_got = _split_wait("allreduce_scatter_wait", ar_ss, ar_rs, ar_srcs, ar_got, _ar_scatter_copies(ar_rows),
                                  [others[0]])
    ar_red = _ar_sum(ar_srcs, ar_got, ar_rows)
    ag_ss, ag_rs, ar_red, ag_full, ag_token = _split_start(
        "allreduce_gather_start", ar_red, [jax.ShapeDtypeStruct(a.shape, F32) for a in ar_srcs],
        _ar_gather_copies(ar_rows), 7 * len(ar_red), [])
    G_out, D_out, M_out, V_out = {}, {}, {}, {}
    for n, g_own, g_other in zip(sharded, halves, others):
        g_, d_, m_, v_ = _adamw_sharded(W[n][0], g_own, g_other, M[n][0], V[n][0], c_arr, ag_token, "adamw_" + n)
        G_out[n], D_out[n], M_out[n], V_out[n] = g_[None], d_[None], m_[None], v_[None]
    ar_red, ag_full = _split_wait("allreduce_gather_wait", ag_ss, ag_rs, ar_red, ag_full, _ar_gather_copies(ar_rows),
                                  [D_out[n] for n in sharded])
    me8 = 2 * chip + lax.axis_index("c")
    r_vecs, r_a, r_bc = [lax.dynamic_update_slice_in_dim(f, r, me8 * rw, axis=f.ndim - 2)
                         for f, r, rw in zip(ag_full, ar_red, ar_rows)]
    r_vecs = r_vecs.reshape(1, -1)
    loss = r_vecs[0, loss_at]
    outs4 = _adamw_small(r_vecs, r_a, r_bc, [Wv[n] for n in small],
                         [view[n](M[n]) for n in small], [view[n](V[n]) for n in small])
    for store, o in zip((G_out, D_out, M_out, V_out), outs4):
        store.update({n: back[n](t) for n, t in zip(small, o)})

    return (loss, grad_x[None], *[G_out[n] for n in names], *[D_out[n] for n in names],
            *[M_out[n] for n in names], *[V_out[n] for n in names])
```

```python
import functools
import math

import jax
import jax.numpy as jnp
from jax import lax
from jax.experimental import pallas as pl
from jax.experimental.pallas import tpu as pltpu

F32 = jnp.float32
BF16 = jnp.bfloat16
HI = lax.Precision.HIGHEST
MESH = pl.DeviceIdType.MESH

EPS = 1e-6
S5_GROUP = 16
S5_STATE = 64
GLA_HK = 128
GLA_HV = 256
GLA_RANK = 16
GLA_TAU = 16.0
GLA_CHUNK = 64
GLA_STEP_CHUNKS = 4
LANES = 128
SUBLANES = 8
S5_COLS = 128
S5_LANES = (S5_COLS // S5_GROUP) * S5_STATE
S5_TIME_BLOCK = 1024

ADAM_LR = 0.001
ADAM_B1 = 0.9
ADAM_B2 = 0.999
ADAM_EPS = 1e-08
ADAM_WD = 0.01
ADAM_STEP = 10

GELU_K = math.sqrt(2.0 / math.pi)
GELU_C = 0.044715


def _blk(n, pref, unit=LANES):
    best = None
    b = unit
    while b <= min(n, pref):
        if n % b == 0:
            best = b
        b += unit
    return best if best is not None else n


def _dot(a, b, dn=(((1,), (0,)), ((), ()))):
    return lax.dot_general(a.astype(BF16), b.astype(BF16), dn, preferred_element_type=F32)


def _dot_hi(a, b, dn=(((1,), (0,)), ((), ()))):
    return lax.dot_general(a, b, dn, precision=HI, preferred_element_type=F32)


NN = (((1,), (0,)), ((), ()))
NT = (((1,), (1,)), ((), ()))
TN = (((0,), (0,)), ((), ()))


def _sigmoid(x):
    return 1.0 / (1.0 + jnp.exp(-x))


def _gelu(y):
    return 0.5 * y * (1.0 + jnp.tanh(GELU_K * (y + GELU_C * y * y * y)))


def _gelu_grad(y):
    th = jnp.tanh(GELU_K * (y + GELU_C * y * y * y))
    return 0.5 * (1.0 + th) + 0.5 * y * (1.0 - th * th) * GELU_K * (1.0 + 3.0 * GELU_C * y * y)


def _mm(a, b, *, name, ta=False, tb=False, out_dtype=F32, bm=1024, bn=1024, bk=2048, after=()):
    if ta:
        K, M = a.shape
    else:
        M, K = a.shape
    if tb:
        N, K2 = b.shape
    else:
        K2, N = b.shape
    assert K == K2, (a.shape, b.shape, ta, tb)
    bm, bn, bk = _blk(M, bm), _blk(N, bn), _blk(K, bk)
    nk = K // bk
    dn = (((0 if ta else 1,), (1 if tb else 0,)), ((), ()))

    def body(a_ref, b_ref, *rest):
        o_ref = rest[len(after)]
        if nk == 1:
            o_ref[...] = _dot(a_ref[...], b_ref[...], dn).astype(out_dtype)
            return
        acc_ref = rest[len(after) + 1]
        k = pl.program_id(2)

        @pl.when(k == 0)
        def _():
            acc_ref[...] = jnp.zeros_like(acc_ref)

        acc_ref[...] += _dot(a_ref[...], b_ref[...], dn)

        @pl.when(k == nk - 1)
        def _():
            o_ref[...] = acc_ref[...].astype(out_dtype)

    a_spec = pl.BlockSpec((bk, bm), lambda i, j, k: (k, i)) if ta else pl.BlockSpec((bm, bk), lambda i, j, k: (i, k))
    b_spec = pl.BlockSpec((bn, bk), lambda i, j, k: (j, k)) if tb else pl.BlockSpec((bk, bn), lambda i, j, k: (k, j))
    return pl.pallas_call(
        body,
        name=name,
        grid=(M // bm, N // bn, nk),
        in_specs=[a_spec, b_spec] + [pl.BlockSpec(memory_space=pl.ANY)] * len(after),
        out_specs=pl.BlockSpec((bm, bn), lambda i, j, k: (i, j)),
        out_shape=jax.ShapeDtypeStruct((M, N), out_dtype),
        scratch_shapes=[pltpu.VMEM((bm, bn), F32)] if nk > 1 else [],
        compiler_params=pltpu.CompilerParams(dimension_semantics=("parallel", "parallel", "arbitrary")),
    )(a, b, *after)


def _in_proj(h, w_main, w_low, after):
    M, K = h.shape
    N = w_main.shape[1]
    bm, bn = _blk(M, 1024), _blk(N, 1024)

    def body(h_ref, w_ref, wl_ref, _after_ref, o_ref, ol_ref):
        hv = h_ref[...]
        o_ref[...] = _dot(hv, w_ref[...])

        @pl.when(pl.program_id(1) == 0)
        def _():
            ol_ref[...] = _dot(hv, wl_ref[...])

    return pl.pallas_call(
        body, name="in_proj", grid=(M // bm, N // bn),
        in_specs=[pl.BlockSpec((bm, K), lambda i, j: (i, 0)), pl.BlockSpec((K, bn), lambda i, j: (0, j)),
                  pl.BlockSpec((K, LANES), lambda i, j: (0, 0)), pl.BlockSpec(memory_space=pl.ANY)],
        out_specs=[pl.BlockSpec((bm, bn), lambda i, j: (i, j)), pl.BlockSpec((bm, LANES), lambda i, j: (i, 0))],
        out_shape=[jax.ShapeDtypeStruct((M, N), F32), jax.ShapeDtypeStruct((M, LANES), F32)],
        compiler_params=pltpu.CompilerParams(dimension_semantics=("parallel", "arbitrary")),
    )(h, w_main, w_low, after)


def _in_proj_dx(a1, a2, al, b, bl, after, *, bm=1024, bn=1024, bk=2048):
    M, K1 = a1.shape
    K2 = a2.shape[1]
    N = b.shape[0]
    bm, bn = _blk(M, bm), _blk(N, bn)
    bk = _blk(math.gcd(K1, K2), bk)
    nk1, nk = K1 // bk, (K1 + K2) // bk

    def body(a1_ref, a2_ref, al_ref, b_ref, bl_ref, _after_ref, o_ref, acc_ref):
        k = pl.program_id(2)

        @pl.when(k == 0)
        def _():
            acc_ref[...] = _dot(al_ref[...], bl_ref[...], NT)

        @pl.when(k < nk1)
        def _():
            acc_ref[...] += _dot(a1_ref[...], b_ref[...], NT)

        @pl.when(k >= nk1)
        def _():
            acc_ref[...] += _dot(a2_ref[...], b_ref[...], NT)

        @pl.when(k == nk - 1)
        def _():
            o_ref[...] = acc_ref[...]

    return pl.pallas_call(
        body, name="in_proj_dx", grid=(M // bm, N // bn, nk),
        in_specs=[pl.BlockSpec((bm, bk), lambda i, j, k: (i, jnp.minimum(k, nk1 - 1))),
                  pl.BlockSpec((bm, bk), lambda i, j, k: (i, jnp.maximum(k - nk1, 0))),
                  pl.BlockSpec((bm, LANES), lambda i, j, k: (i, 0)),
                  pl.BlockSpec((bn, bk), lambda i, j, k: (j, k)),
                  pl.BlockSpec((bn, LANES), lambda i, j, k: (j, 0)),
                  pl.BlockSpec(memory_space=pl.ANY)],
        out_specs=pl.BlockSpec((bm, bn), lambda i, j, k: (i, j)),
        out_shape=jax.ShapeDtypeStruct((M, N), F32),
        scratch_shapes=[pltpu.VMEM((bm, bn), F32)],
        compiler_params=pltpu.CompilerParams(dimension_semantics=("parallel", "parallel", "arbitrary")),
    )(a1, a2, al, b, bl, after)


def _in_proj_dw(a, b1, b2, bl, *, bm=1024, bn=1024, bk=2048):
    K, M = a.shape
    N1, N2 = b1.shape[1], b2.shape[1]
    bm, bk = _blk(M, bm), _blk(K, bk)
    bn = _blk(math.gcd(N1, N2), bn)
    nj1, nj = N1 // bn, (N1 + N2) // bn
    nk = K // bk

    def body(a_ref, b1_ref, b2_ref, bl_ref, o_ref, ol_ref, acc_ref, accl_ref):
        j = pl.program_id(1)
        k = pl.program_id(2)

        @pl.when(k == 0)
        def _():
            acc_ref[...] = jnp.zeros_like(acc_ref)

        @pl.when(j < nj1)
        def _():
            acc_ref[...] += _dot(a_ref[...], b1_ref[...], TN)

        @pl.when(j >= nj1)
        def _():
            acc_ref[...] += _dot(a_ref[...], b2_ref[...], TN)

        @pl.when(k == nk - 1)
        def _():
            o_ref[...] = acc_ref[...].astype(BF16)

        @pl.when(j == 0)
        def _():
            low = _dot(a_ref[...], bl_ref[...], TN)

            @pl.when(k == 0)
            def _():
                accl_ref[...] = low

            @pl.when(k > 0)
            def _():
                accl_ref[...] += low

            @pl.when(k == nk - 1)
            def _():
                ol_ref[...] = accl_ref[...].astype(BF16)

    return pl.pallas_call(
        body, name="in_proj_dw", grid=(M // bm, nj, nk),
        in_specs=[pl.BlockSpec((bk, bm), lambda i, j, k: (k, i)),
                  pl.BlockSpec((bk, bn), lambda i, j, k: (jnp.where(j < nj1, k, nk - 1), jnp.minimum(j, nj1 - 1))),
                  pl.BlockSpec((bk, bn), lambda i, j, k: (jnp.where(j >= nj1, k, 0), jnp.maximum(j - nj1, 0))),
                  pl.BlockSpec((bk, LANES), lambda i, j, k: (jnp.where(j == 0, k, nk - 1), 0))],
        out_specs=[pl.BlockSpec((bm, bn), lambda i, j, k: (i, j)), pl.BlockSpec((bm, LANES), lambda i, j, k: (i, 0))],
        out_shape=[jax.ShapeDtypeStruct((M, N1 + N2), BF16), jax.ShapeDtypeStruct((M, LANES), BF16)],
        scratch_shapes=[pltpu.VMEM((bm, bn), F32), pltpu.VMEM((bm, LANES), F32)],
        compiler_params=pltpu.CompilerParams(dimension_semantics=("parallel", "arbitrary", "arbitrary")),
    )(a, b1, b2, bl)


def _assemble_w_in(g, nsh, wm):
    _, R, nshp = g.shape
    nb_in = nshp // LANES
    nb_main = wm // LANES
    tr = _blk(R, 512, 2 * SUBLANES)
    plan = []
    for b in range(nb_main + 1):
        terms = []
        for k in range(g.shape[0]):
            for i in range(nb_in):
                delta = nsh * k + LANES * i - LANES * b
                lo, hi = max(0, -delta), min(LANES, LANES - delta, nsh - LANES * i)
                if abs(delta) < LANES and hi > lo:
                    terms.append((k, i, delta))
        plan.append(terms)
    deltas = sorted({d for terms in plan for _, _, d in terms if d})

    def body(g_ref, wm_ref, wl_ref):
        src = _iota2((LANES, LANES), 0)
        dst = _iota2((LANES, LANES), 1)
        shift = {d: (dst - src == d).astype(BF16) for d in deltas}
        for b, terms in enumerate(plan):
            acc = None
            for k, i, d in terms:
                blk = g_ref[k, :, LANES * i:LANES * (i + 1)]
                t = _dot(blk, shift[d]) if d else blk.astype(F32)
                acc = t if acc is None else acc + t
            if b < nb_main:
                wm_ref[:, LANES * b:LANES * (b + 1)] = acc.astype(BF16)
            else:
                wl_ref[...] = acc.astype(BF16)

    return pl.pallas_call(
        body, name="assemble_w_in", grid=(R // tr,),
        in_specs=[pl.BlockSpec((g.shape[0], tr, nshp), lambda r: (0, r, 0))],
        out_specs=[pl.BlockSpec((tr, wm), lambda r: (r, 0)), pl.BlockSpec((tr, LANES), lambda r: (r, 0))],
        out_shape=[jax.ShapeDtypeStruct((R, wm), BF16), jax.ShapeDtypeStruct((R, LANES), BF16)],
        compiler_params=pltpu.CompilerParams(dimension_semantics=("parallel",)),
    )(g)


def _split_w_in_grad(g_main, g_low, nsh):
    R, wm = g_main.shape
    nb_main = wm // LANES
    nb_out = -(-nsh // LANES)
    tr = _blk(R, 512, 2 * SUBLANES)
    plan = {}
    for k in range(4):
        for i in range(nb_out):
            width = min(LANES, nsh - LANES * i)
            terms = []
            for b in range(nb_main + 1):
                delta = LANES * b - (nsh * k + LANES * i)
                lo, hi = max(0, delta), min(width, LANES + delta)
                if abs(delta) < LANES and hi > lo:
                    terms.append((b, delta))
            plan[k, i] = (width, terms)
    deltas = sorted({d for _, terms in plan.values() for _, d in terms if d})

    def body(gm_ref, gl_ref, o_ref):
        src = _iota2((LANES, LANES), 0)
        dst = _iota2((LANES, LANES), 1)
        shift = {d: (dst - src == d).astype(BF16) for d in deltas}
        for (k, i), (width, terms) in plan.items():
            acc = None
            for b, d in terms:
                blk = gm_ref[:, LANES * b:LANES * (b + 1)] if b < nb_main else gl_ref[...]
                t = _dot(blk, shift[d]) if d else blk.astype(F32)
                acc = t if acc is None else acc + t
            o_ref[k, :, LANES * i:LANES * i + width] = acc[:, :width].astype(BF16)

    return pl.pallas_call(
        body, name="split_w_in_grad", grid=(R // tr,),
        in_specs=[pl.BlockSpec((tr, wm), lambda r: (r, 0)), pl.BlockSpec((tr, LANES), lambda r: (r, 0))],
        out_specs=pl.BlockSpec((4, tr, nsh), lambda r: (0, r, 0)),
        out_shape=jax.ShapeDtypeStruct((4, R, nsh), BF16),
        compiler_params=pltpu.CompilerParams(dimension_semantics=("parallel",)),
    )(g_main, g_low)


def _prenorm_fwd(x, w, after):
    L, D = x.shape
    tr = _blk(L, 256, SUBLANES)

    def body(x_ref, w_ref, _after_ref, h_ref):
        xv = x_ref[...]
        r = lax.rsqrt(jnp.mean(xv * xv, axis=-1, keepdims=True) + EPS)
        h_ref[...] = (xv * r * w_ref[...]).astype(BF16)

    return pl.pallas_call(
        body, name="prenorm_fwd", grid=(L // tr,),
        in_specs=[pl.BlockSpec((tr, D), lambda i: (i, 0)), pl.BlockSpec((1, D), lambda i: (0, 0)),
                  pl.BlockSpec(memory_space=pl.ANY)],
        out_specs=pl.BlockSpec((tr, D), lambda i: (i, 0)),
        out_shape=jax.ShapeDtypeStruct((L, D), BF16),
        compiler_params=pltpu.CompilerParams(dimension_semantics=("parallel",)),
    )(x, w, after)


def _post_fwd_bwd(mixed, x, target, w):
    L, D = x.shape
    tr = _blk(L, 256, SUBLANES)
    nsteps = L // tr

    def body(mx_ref, x_ref, t_ref, w_ref, loss_ref, dm_ref, dout_ref, gw_ref, acc_ref):
        i = pl.program_id(0)

        @pl.when(i == 0)
        def _():
            acc_ref[...] = jnp.zeros_like(acc_ref)
            gw_ref[...] = jnp.zeros_like(gw_ref)

        mx = mx_ref[...]
        wv = w_ref[...]
        r = lax.rsqrt(jnp.mean(mx * mx, axis=-1, keepdims=True) + EPS)
        n = mx * r
        err = x_ref[...] + n * wv - t_ref[...]
        acc_ref[...] += jnp.sum(err * err, axis=0, keepdims=True)
        dout = err * (1.0 / D)
        dout_ref[...] = dout
        gw_ref[...] += jnp.sum(dout * n, axis=0, keepdims=True)
        dn = dout * wv
        dm_ref[...] = (r * (dn - n * jnp.mean(dn * n, axis=-1, keepdims=True))).astype(BF16)

        @pl.when(i == nsteps - 1)
        def _():
            loss_ref[...] = jnp.sum(acc_ref[...], axis=-1, keepdims=True) * (0.5 / D)

    row = pl.BlockSpec((tr, D), lambda i: (i, 0))
    vec = pl.BlockSpec((1, D), lambda i: (0, 0))
    return pl.pallas_call(
        body, name="post_fwd_bwd", grid=(nsteps,),
        in_specs=[row, row, row, vec],
        out_specs=[pl.BlockSpec((1, 1), lambda i: (0, 0)), row, row, vec],
        out_shape=[jax.ShapeDtypeStruct((1, 1), F32), jax.ShapeDtypeStruct((L, D), BF16),
                   jax.ShapeDtypeStruct((L, D), F32), jax.ShapeDtypeStruct((1, D), F32)],
        scratch_shapes=[pltpu.VMEM((1, D), F32)],
        compiler_params=pltpu.CompilerParams(dimension_semantics=("arbitrary",)),
    )(mixed, x, target, w)


def _prenorm_bwd(x, dh, dout, w):
    L, D = x.shape
    tr = _blk(L, 256, SUBLANES)

    def body(x_ref, a_ref, dout_ref, w_ref, gx_ref, gw_ref):
        i = pl.program_id(0)

        @pl.when(i == 0)
        def _():
            gw_ref[...] = jnp.zeros_like(gw_ref)

        xv = x_ref[...]
        r = lax.rsqrt(jnp.mean(xv * xv, axis=-1, keepdims=True) + EPS)
        n = xv * r
        dh = a_ref[...]
        gw_ref[...] += jnp.sum(dh * n, axis=0, keepdims=True)
        dn = dh * w_ref[...]
        gx_ref[...] = dout_ref[...] + r * (dn - n * jnp.mean(dn * n, axis=-1, keepdims=True))

    row = pl.BlockSpec((tr, D), lambda i: (i, 0))
    vec = pl.BlockSpec((1, D), lambda i: (0, 0))
    return pl.pallas_call(
        body, name="prenorm_bwd", grid=(L // tr,),
        in_specs=[row, row, row, vec],
        out_specs=[row, vec],
        out_shape=[jax.ShapeDtypeStruct((L, D), F32), jax.ShapeDtypeStruct((1, D), F32)],
        compiler_params=pltpu.CompilerParams(dimension_semantics=("arbitrary",)),
    )(x, dh, dout, w)


def _s5_disc(a_re_raw, a_im, dt):
    a_re = jnp.minimum(a_re_raw, -1e-4)
    mag = jnp.exp(a_re * dt)
    ph = a_im * dt
    ab_re = mag * jnp.cos(ph)
    ab_im = mag * jnp.sin(ph)
    inv_n = 1.0 / (a_re * a_re + a_im * a_im)
    ia_re = a_re * inv_n
    ia_im = -a_im * inv_n
    n_re = ab_re - 1.0
    f_re = n_re * ia_re - ab_im * ia_im
    f_im = n_re * ia_im + ab_im * ia_re
    return a_re, ab_re, ab_im, f_re, f_im, ia_re, ia_im


def _iota2(shape, dim):
    return lax.broadcasted_iota(jnp.int32, shape, dim)


def _group_mask(rows, rows_per_group):
    shift = rows_per_group.bit_length() - 1
    return (_iota2((rows, S5_LANES), 0) >> shift) == (_iota2((rows, S5_LANES), 1) >> (S5_STATE.bit_length() - 1))


def _lane_tiler(dtype):
    return ((_iota2((S5_STATE, S5_LANES), 1) & (S5_STATE - 1)) == _iota2((S5_STATE, S5_LANES), 0)).astype(dtype)


def _row_to_col(row, n):
    eye = (_iota2((n, n), 0) == _iota2((n, n), 1)).astype(F32)
    return jnp.sum(eye * row, axis=1, keepdims=True)


def _group_repeat(G):
    return ((_iota2((G * S5_GROUP, G), 0) >> (S5_GROUP.bit_length() - 1)) == _iota2((G * S5_GROUP, G), 1)).astype(F32)


S5_TABS = 18


def _s5_prep_fwd(a_re, a_im, log_dt, b_re, b_im, c_re, c_im, after, seg):
    G, P = a_re.shape
    nb = G * S5_GROUP // S5_COLS
    g8 = S5_COLS // S5_GROUP
    assert seg & (seg - 1) == 0, seg

    def body(are_ref, aim_ref, ldt_ref, bre_ref, bim_ref, cre_ref, cim_ref, _after_ref,
             bbre_ref, bbim_ref, ctre_ref, ctim_ref, tab_ref, pt_ref):
        dt = jnp.exp(_row_to_col(ldt_ref[...], G))
        _, ab_re, ab_im, f_re, f_im, _, _ = _s5_disc(are_ref[...], aim_ref[...], dt)
        rep = _group_repeat(G)
        fx_re = _dot_hi(rep, f_re)
        fx_im = _dot_hi(rep, f_im)
        br, bi = bre_ref[...], bim_ref[...]
        bb_re = fx_re * br - fx_im * bi
        bb_im = fx_re * bi + fx_im * br
        tile_bf = _lane_tiler(BF16)
        mask = _group_mask(S5_COLS, S5_GROUP)
        for jb in range(nb):
            rs = slice(jb * S5_COLS, (jb + 1) * S5_COLS)
            for src, dst in ((bb_re[rs], bbre_ref), (bb_im[rs], bbim_ref), (cre_ref[rs, :], ctre_ref), (cim_ref[rs, :], ctim_ref)):
                dst[jb] = jnp.where(mask, _dot(src, tile_bf), 0.0).astype(BF16)

        tile_f = _lane_tiler(F32)
        mask8 = _group_mask(g8, 1)
        row = _iota2((SUBLANES, S5_LANES), 0)
        slab = (SUBLANES, S5_LANES)
        cmul = lambda p, q: (p[0] * q[0] - p[1] * q[1], p[0] * q[1] + p[1] * q[0])
        for jb in range(nb):
            gs = slice(jb * g8, (jb + 1) * g8)

            def lanes(m):
                v = jnp.sum(jnp.where(mask8, _dot_hi(m[gs], tile_f), 0.0), axis=0, keepdims=True)
                return jnp.broadcast_to(v, slab)

            a1 = (lanes(ab_re), lanes(ab_im))
            tab_ref[jb, 0], tab_ref[jb, 1] = a1

            def powers(i, p):
                off = pl.multiple_of(i * SUBLANES, SUBLANES)
                pt_ref[jb, 0, pl.ds(off, SUBLANES), :] = p[0]
                pt_ref[jb, 1, pl.ds(off, SUBLANES), :] = p[1]
                return cmul(p, a1)

            lax.fori_loop(0, seg, powers, a1)
            aseg = a1
            for _ in range(seg.bit_length() - 1):
                aseg = cmul(aseg, aseg)
            pw = [aseg]
            for _ in range(1, SUBLANES):
                pw.append(cmul(pw[-1], aseg))
            for lvl, k in enumerate((1, 2, 4)):
                tab_ref[jb, 2 + 2 * lvl] = jnp.where(row >= k, pw[k - 1][0], 0.0)
                tab_ref[jb, 3 + 2 * lvl] = jnp.where(row >= k, pw[k - 1][1], 0.0)
                tab_ref[jb, 10 + 2 * lvl] = jnp.where(row < SUBLANES - k, pw[k - 1][0], 0.0)
                tab_ref[jb, 11 + 2 * lvl] = jnp.where(row < SUBLANES - k, -pw[k - 1][1], 0.0)
            f_r = f_i = r_r = r_i = jnp.zeros(slab, F32)
            for i in range(SUBLANES):
                f_r = jnp.where(row == i, pw[i][0], f_r)
                f_i = jnp.where(row == i, pw[i][1], f_i)
                r_r = jnp.where(row == i, pw[SUBLANES - 1 - i][0], r_r)
                r_i = jnp.where(row == i, -pw[SUBLANES - 1 - i][1], r_i)
            tab_ref[jb, 8] = f_r
            tab_ref[jb, 9] = f_i
            tab_ref[jb, 16] = r_r
            tab_ref[jb, 17] = r_i

    vm = pl.BlockSpec(memory_space=pltpu.VMEM)
    bd = jax.ShapeDtypeStruct((nb, S5_COLS, S5_LANES), BF16)
    return pl.pallas_call(
        body, name="s5_prep_fwd",
        in_specs=[vm] * 7 + [pl.BlockSpec(memory_space=pl.ANY)], out_specs=[vm] * 6,
        out_shape=[bd, bd, bd, bd, jax.ShapeDtypeStruct((nb, S5_TABS, SUBLANES, S5_LANES), F32),
                   jax.ShapeDtypeStruct((nb, 2, seg * SUBLANES, S5_LANES), F32)],
    )(a_re, a_im, log_dt, b_re, b_im, c_re, c_im, after)


def _s5_prep_bwd(a_re, a_im, log_dt, b_re, b_im, gbb_re, gbb_im, gct_re, gct_im, gab_re, gab_im):
    G, P = a_re.shape
    nb = G * S5_GROUP // S5_COLS
    g8 = S5_COLS // S5_GROUP

    def body(are_ref, aim_ref, ldt_ref, bre_ref, bim_ref, gbr_ref, gbi_ref, gcr_ref, gci_ref, gar_ref, gai_ref,
             o_a, o_bc, o_ldt):
        dt = jnp.exp(_row_to_col(ldt_ref[...], G))
        a_raw = are_ref[...]
        a_imv = aim_ref[...]
        a_re_c, ab_re, ab_im, f_re, f_im, ia_re, ia_im = _s5_disc(a_raw, a_imv, dt)
        tile_f = _lane_tiler(F32)
        mask = _group_mask(S5_COLS, S5_GROUP)
        mask8 = _group_mask(g8, 1)
        for jb in range(nb):
            rs = slice(jb * S5_COLS, (jb + 1) * S5_COLS)
            gs = slice(jb * g8, (jb + 1) * g8)
            ls = slice(jb * S5_LANES, (jb + 1) * S5_LANES)
            for k, src in enumerate((gbr_ref, gbi_ref, gcr_ref, gci_ref)):
                o_bc[k, rs, :] = _dot_hi(jnp.where(mask, src[jb], 0.0), tile_f, NT)
            for k, src in enumerate((gar_ref, gai_ref)):
                o_a[k, gs, :] = _dot_hi(jnp.where(mask8, src[:, ls], 0.0), tile_f, NT)
        rep = _group_repeat(G)
        fx_re = _dot_hi(rep, f_re)
        fx_im = _dot_hi(rep, f_im)
        gbr, gbi = o_bc[0], o_bc[1]
        br, bi = bre_ref[...], bim_ref[...]
        o_bc[0] = fx_re * gbr + fx_im * gbi
        o_bc[1] = fx_re * gbi - fx_im * gbr
        gf_re = _dot_hi(rep, br * gbr + bi * gbi, TN)
        gf_im = _dot_hi(rep, br * gbi - bi * gbr, TN)
        gab_r = o_a[0] + ia_re * gf_re + ia_im * gf_im
        gab_i = o_a[1] + ia_re * gf_im - ia_im * gf_re
        q_re = f_re * ia_re - f_im * ia_im
        q_im = f_re * ia_im + f_im * ia_re
        ga_re = -(q_re * gf_re + q_im * gf_im)
        ga_im = -(q_re * gf_im - q_im * gf_re)
        gth_re = ab_re * gab_r + ab_im * gab_i
        gth_im = ab_re * gab_i - ab_im * gab_r
        ga_re = ga_re + dt * gth_re
        ga_im = ga_im + dt * gth_im
        gdt = jnp.sum(a_re_c * gth_re + a_imv * gth_im, axis=-1, keepdims=True)
        eye = (_iota2((G, G), 0) == _iota2((G, G), 1)).astype(F32)
        o_ldt[...] = jnp.sum(eye * (gdt * dt), axis=0, keepdims=True)
        slope = jnp.where(a_raw < -1e-4, 1.0, jnp.where(a_raw == -1e-4, 0.5, 0.0))
        o_a[0] = ga_re * slope
        o_a[1] = ga_im

    vm = pl.BlockSpec(memory_space=pltpu.VMEM)
    return pl.pallas_call(
        body, name="s5_prep_bwd",
        in_specs=[vm] * 11, out_specs=[vm] * 3,
        out_shape=[jax.ShapeDtypeStruct((2, G, P), F32), jax.ShapeDtypeStruct((4, G * S5_GROUP, P), F32),
                   jax.ShapeDtypeStruct((1, G), F32)],
    )(a_re, a_im, log_dt, b_re, b_im, gbb_re, gbb_im, gct_re, gct_im, gab_re, gab_im)


def _scan8(xr, xi, tab_ref, base, shifts):
    for lvl, sh in enumerate(shifts):
        mr = tab_ref[0, base + 2 * lvl]
        mi = tab_ref[0, base + 2 * lvl + 1]
        ar = pltpu.roll(xr, sh, 0)
        ai = pltpu.roll(xi, sh, 0)
        xr, xi = xr + mr * ar - mi * ai, xi + mr * ai + mi * ar
    return xr, xi


def _to_segments(src_ref, dst_ref, seg):
    for i in range(seg):
        dst_ref[i * SUBLANES:(i + 1) * SUBLANES, :] = src_ref[pl.ds(i, SUBLANES, stride=seg), :]


def _from_segments(src_ref, dst_ref, seg):
    for i in range(seg):
        dst_ref[pl.ds(i, SUBLANES, stride=seg), :] = src_ref[i * SUBLANES:(i + 1) * SUBLANES, :]


def _slab(i):
    return pl.ds(pl.multiple_of(i * SUBLANES, SUBLANES), SUBLANES)


def _s5_scan_fwd(proj_main, bbd_re, bbd_im, cbd_re, cbd_im, dvec, tab, ptab, DS):
    L = proj_main.shape[0]
    nb = DS // S5_COLS
    tb = _blk(L, S5_TIME_BLOCK, SUBLANES)
    nt = L // tb
    seg = tb // SUBLANES

    def body(u_ref, bre_ref, bim_ref, cre_ref, cim_ref, d_ref, tab_ref, pt_ref, y_ref, sre_ref, sim_ref,
             up_ref, yp_ref, car_ref):
        t = pl.program_id(1)

        @pl.when(t == 0)
        def _():
            car_ref[...] = jnp.zeros_like(car_ref)

        _to_segments(u_ref, up_ref, seg)
        up = up_ref[...]
        sre_ref[...] = _dot(up, bre_ref[0])
        sim_ref[...] = _dot(up, bim_ref[0])
        ar, ai = tab_ref[0, 0], tab_ref[0, 1]

        def pass1(i, x):
            xr = ar * x[0] - ai * x[1] + sre_ref[_slab(i), :]
            xi = ar * x[1] + ai * x[0] + sim_ref[_slab(i), :]
            sre_ref[_slab(i), :] = xr
            sim_ref[_slab(i), :] = xi
            return xr, xi

        zero = jnp.zeros((SUBLANES, S5_LANES), F32)
        er, ei = lax.fori_loop(0, seg, pass1, (zero, zero))
        cin_r, cin_i = car_ref[0], car_ref[1]
        sr, si = _scan8(er, ei, tab_ref, 2, (1, 2, 4))
        pr, pi = tab_ref[0, 8], tab_ref[0, 9]
        sr, si = sr + pr * cin_r - pi * cin_i, si + pr * cin_i + pi * cin_r
        row0 = _iota2((SUBLANES, S5_LANES), 0) == 0
        cr = jnp.where(row0, cin_r, pltpu.roll(sr, 1, 0))
        ci = jnp.where(row0, cin_i, pltpu.roll(si, 1, 0))
        car_ref[0] = jnp.broadcast_to(sr[SUBLANES - 1:SUBLANES, :], sr.shape)
        car_ref[1] = jnp.broadcast_to(si[SUBLANES - 1:SUBLANES, :], si.shape)

        def pass2(i, _):
            qr, qi = pt_ref[0, 0, _slab(i), :], pt_ref[0, 1, _slab(i), :]
            sre_ref[_slab(i), :] += qr * cr - qi * ci
            sim_ref[_slab(i), :] += qr * ci + qi * cr
            return 0

        lax.fori_loop(0, seg, pass2, 0, unroll=4)
        yp_ref[...] = _dot(sre_ref[...], cre_ref[0], NT) - _dot(sim_ref[...], cim_ref[0], NT) + d_ref[...] * up
        _from_segments(yp_ref, y_ref, seg)

    return pl.pallas_call(
        body, name="s5_scan_fwd", grid=(nb, nt),
        in_specs=[
            pl.BlockSpec((tb, S5_COLS), lambda j, t: (t, j)),
            pl.BlockSpec((1, S5_COLS, S5_LANES), lambda j, t: (j, 0, 0)),
            pl.BlockSpec((1, S5_COLS, S5_LANES), lambda j, t: (j, 0, 0)),
            pl.BlockSpec((1, S5_COLS, S5_LANES), lambda j, t: (j, 0, 0)),
            pl.BlockSpec((1, S5_COLS, S5_LANES), lambda j, t: (j, 0, 0)),
            pl.BlockSpec((1, S5_COLS), lambda j, t: (0, j)),
            pl.BlockSpec((1, S5_TABS, SUBLANES, S5_LANES), lambda j, t: (j, 0, 0, 0)),
            pl.BlockSpec((1, 2, tb, S5_LANES), lambda j, t: (j, 0, 0, 0)),
        ],
        out_specs=[
            pl.BlockSpec((tb, S5_COLS), lambda j, t: (t, j)),
            pl.BlockSpec((tb, S5_LANES), lambda j, t: (t, j)),
            pl.BlockSpec((tb, S5_LANES), lambda j, t: (t, j)),
        ],
        out_shape=[jax.ShapeDtypeStruct((L, DS), F32),
                   jax.ShapeDtypeStruct((L, nb * S5_LANES), F32),
                   jax.ShapeDtypeStruct((L, nb * S5_LANES), F32)],
        scratch_shapes=[pltpu.VMEM((tb, S5_COLS), F32), pltpu.VMEM((tb, S5_COLS), F32),
                        pltpu.VMEM((2, SUBLANES, S5_LANES), F32)],
        compiler_params=pltpu.CompilerParams(dimension_semantics=("parallel", "arbitrary")),
    )(proj_main, bbd_re, bbd_im, cbd_re, cbd_im, dvec, tab, ptab)


def _s5_scan_bwd(dy, proj_main, s_re, s_im, bbd_re, bbd_im, cbd_re, cbd_im, dvec, tab, ptab, d_s5, DS):
    L = proj_main.shape[0]
    nb = DS // S5_COLS
    tb = _blk(L, S5_TIME_BLOCK, SUBLANES)
    nt = L // tb
    seg = tb // SUBLANES
    tb8 = tb // SUBLANES

    def body(dy_ref, u_ref, sre_ref, sim_ref, pre_ref, pim_ref, bre_ref, bim_ref, cre_ref, cim_ref, d_ref, tab_ref, pt_ref,
             _ds5_ref, du_ref, gd_ref, gcre_ref, gcim_ref, gbre_ref, gbim_ref, gare_ref, gaim_ref,
             lre_ref, lim_ref, up_ref, dyp_ref, dup_ref, duo_ref, car_ref):
        t = pl.program_id(1)

        @pl.when(t == 0)
        def _():
            car_ref[...] = jnp.zeros_like(car_ref)
            gd_ref[...] = jnp.zeros_like(gd_ref)
            gcre_ref[...] = jnp.zeros_like(gcre_ref)
            gcim_ref[...] = jnp.zeros_like(gcim_ref)
            gbre_ref[...] = jnp.zeros_like(gbre_ref)
            gbim_ref[...] = jnp.zeros_like(gbim_ref)
            gare_ref[...] = jnp.zeros_like(gare_ref)
            gaim_ref[...] = jnp.zeros_like(gaim_ref)

        _to_segments(dy_ref, dyp_ref, seg)
        _to_segments(u_ref, up_ref, seg)
        dyv = dyp_ref[...]
        u = up_ref[...]
        gd_ref[...] += jnp.sum(dyv * u, axis=0, keepdims=True)
        lre_ref[...] = _dot(dyv, cre_ref[0])
        lim_ref[...] = -_dot(dyv, cim_ref[0])
        gcre_ref[0] += _dot(dyv, sre_ref[...], TN)
        gcim_ref[0] -= _dot(dyv, sim_ref[...], TN)
        ar, ai = tab_ref[0, 0], -tab_ref[0, 1]

        def pass1(k, x):
            i = seg - 1 - k
            xr = ar * x[0] - ai * x[1] + lre_ref[_slab(i), :]
            xi = ar * x[1] + ai * x[0] + lim_ref[_slab(i), :]
            lre_ref[_slab(i), :] = xr
            lim_ref[_slab(i), :] = xi
            return xr, xi

        zero = jnp.zeros((SUBLANES, S5_LANES), F32)
        er, ei = lax.fori_loop(0, seg, pass1, (zero, zero))
        cin_r, cin_i = car_ref[0], car_ref[1]
        lr, li = _scan8(er, ei, tab_ref, 10, (7, 6, 4))
        pr, pi = tab_ref[0, 16], tab_ref[0, 17]
        lr, li = lr + pr * cin_r - pi * cin_i, li + pr * cin_i + pi * cin_r
        rows = _iota2((SUBLANES, S5_LANES), 0)
        cr = jnp.where(rows == SUBLANES - 1, cin_r, pltpu.roll(lr, SUBLANES - 1, 0))
        ci = jnp.where(rows == SUBLANES - 1, cin_i, pltpu.roll(li, SUBLANES - 1, 0))
        car_ref[0] = jnp.broadcast_to(lr[0:1, :], lr.shape)
        car_ref[1] = jnp.broadcast_to(li[0:1, :], li.shape)

        first = (t == nt - 1).astype(F32)
        head_re = jnp.broadcast_to(pre_ref[SUBLANES - 1:SUBLANES, :], zero.shape) * (1.0 - first)
        head_im = jnp.broadcast_to(pim_ref[SUBLANES - 1:SUBLANES, :], zero.shape) * (1.0 - first)
        last = _slab(seg - 1)
        sp0_re = jnp.where(rows == 0, head_re, pltpu.roll(sre_ref[last, :], 1, 0))
        sp0_im = jnp.where(rows == 0, head_im, pltpu.roll(sim_ref[last, :], 1, 0))

        def fix(i, acc, sp_re, sp_im):
            j = seg - 1 - i
            qr, qi = pt_ref[0, 0, _slab(j), :], -pt_ref[0, 1, _slab(j), :]
            xr = lre_ref[_slab(i), :] + qr * cr - qi * ci
            xi = lim_ref[_slab(i), :] + qr * ci + qi * cr
            lre_ref[_slab(i), :] = xr
            lim_ref[_slab(i), :] = xi
            return acc[0] + sp_re * xr + sp_im * xi, acc[1] + sp_re * xi - sp_im * xr

        def pass2(i, acc):
            prev = _slab(jnp.maximum(i - 1, 0))
            return fix(i, acc, sre_ref[prev, :], sim_ref[prev, :])

        acc_re, acc_im = lax.fori_loop(0, seg, pass2, (zero, zero), unroll=4)
        first_slab = _slab(0)
        d_re, d_im = sp0_re - sre_ref[first_slab, :], sp0_im - sim_ref[first_slab, :]
        x0r, x0i = lre_ref[first_slab, :], lim_ref[first_slab, :]
        acc_re = acc_re + d_re * x0r + d_im * x0i
        acc_im = acc_im + d_re * x0i - d_im * x0r
        gare_ref[...] += jnp.sum(acc_re, axis=0, keepdims=True)
        gaim_ref[...] += jnp.sum(acc_im, axis=0, keepdims=True)
        lre = lre_ref[...]
        lim = lim_ref[...]
        dup_ref[...] = dyv * d_ref[...] + _dot(lre, bre_ref[0], NT) + _dot(lim, bim_ref[0], NT)
        _from_segments(dup_ref, duo_ref, seg)
        du_ref[...] = duo_ref[...].astype(BF16)
        gbre_ref[0] += _dot(u, lre, TN)
        gbim_ref[0] += _dot(u, lim, TN)

    rt = lambda t: nt - 1 - t
    col = pl.BlockSpec((tb, S5_COLS), lambda j, t: (rt(t), j))
    st = pl.BlockSpec((tb, S5_LANES), lambda j, t: (rt(t), j))
    prev = pl.BlockSpec((SUBLANES, S5_LANES), lambda j, t: (jnp.maximum(rt(t) * tb8 - 1, 0), j))
    bmat = pl.BlockSpec((1, S5_COLS, S5_LANES), lambda j, t: (j, 0, 0))
    cmat = bmat
    return pl.pallas_call(
        body, name="s5_scan_bwd", grid=(nb, nt),
        in_specs=[col, col, st, st, prev, prev, bmat, bmat, cmat, cmat,
                  pl.BlockSpec((1, S5_COLS), lambda j, t: (0, j)),
                  pl.BlockSpec((1, S5_TABS, SUBLANES, S5_LANES), lambda j, t: (j, 0, 0, 0)),
                  pl.BlockSpec((1, 2, tb, S5_LANES), lambda j, t: (j, 0, 0, 0)),
                  pl.BlockSpec(memory_space=pl.ANY)],
        out_specs=[col, pl.BlockSpec((1, S5_COLS), lambda j, t: (0, j)), cmat, cmat, bmat, bmat,
                   pl.BlockSpec((1, S5_LANES), lambda j, t: (0, j)), pl.BlockSpec((1, S5_LANES), lambda j, t: (0, j))],
        input_output_aliases={13: 0},
        out_shape=[jax.ShapeDtypeStruct((L, 2 * DS), BF16), jax.ShapeDtypeStruct((1, DS), F32),
                   jax.ShapeDtypeStruct((nb, S5_COLS, S5_LANES), F32), jax.ShapeDtypeStruct((nb, S5_COLS, S5_LANES), F32),
                   jax.ShapeDtypeStruct((nb, S5_COLS, S5_LANES), F32), jax.ShapeDtypeStruct((nb, S5_COLS, S5_LANES), F32),
                   jax.ShapeDtypeStruct((1, nb * S5_LANES), F32), jax.ShapeDtypeStruct((1, nb * S5_LANES), F32)],
        scratch_shapes=[pltpu.VMEM((tb, S5_LANES), F32), pltpu.VMEM((tb, S5_LANES), F32)]
        + [pltpu.VMEM((tb, S5_COLS), F32)] * 4 + [pltpu.VMEM((2, SUBLANES, S5_LANES), F32)],
        compiler_params=pltpu.CompilerParams(dimension_semantics=("parallel", "arbitrary")),
    )(dy, proj_main, s_re, s_im, s_re, s_im, bbd_re, bbd_im, cbd_re, cbd_im, dvec, tab, ptab, d_s5)


def _s5_post_fwd(y_pre, proj_main, glu_w, glu_b, DS):
    L = y_pre.shape[0]
    tr = _blk(L, 256, SUBLANES)

    def body(y_ref, z_ref, w_ref, b_ref, o_ref, t_ref):
        y1 = _gelu(y_ref[...])
        t = _dot(y1, w_ref[...]) + b_ref[...]
        t_ref[...] = t
        z = z_ref[...]
        o_ref[...] = (y1 * _sigmoid(t) * (z * _sigmoid(z))).astype(BF16)

    row = pl.BlockSpec((tr, DS), lambda i: (i, 0))
    return pl.pallas_call(
        body, name="s5_post_fwd", grid=(L // tr,),
        in_specs=[row, pl.BlockSpec((tr, DS), lambda i: (i, 1)), pl.BlockSpec((DS, DS), lambda i: (0, 0)),
                  pl.BlockSpec((1, DS), lambda i: (0, 0))],
        out_specs=[row, row],
        out_shape=[jax.ShapeDtypeStruct((L, 2 * DS), BF16), jax.ShapeDtypeStruct((L, DS), F32)],
        compiler_params=pltpu.CompilerParams(dimension_semantics=("parallel",)),
    )(y_pre, proj_main, glu_w, glu_b)


def _s5_post_bwd(d_ycat, y_pre, proj_main, t_pre, glu_w, DS):
    L = y_pre.shape[0]
    tr = _blk(L, 256, SUBLANES)

    def body(dy_ref, y_ref, z_ref, t_ref, w_ref, dyp_ref, dz_ref, dt_ref, y1_ref, gb_ref):
        i = pl.program_id(0)

        @pl.when(i == 0)
        def _():
            gb_ref[...] = jnp.zeros_like(gb_ref)

        dy = dy_ref[...]
        yp = y_ref[...]
        z = z_ref[...]
        y1 = _gelu(yp)
        sg = _sigmoid(t_ref[...])
        sz = _sigmoid(z)
        c = y1 * sg
        d_c = dy * (z * sz)
        dz_ref[...] = (dy * c * (sz * (1.0 + z * (1.0 - sz)))).astype(BF16)
        d_t = d_c * y1 * sg * (1.0 - sg)
        gb_ref[...] += jnp.sum(d_t, axis=0, keepdims=True)
        dt_ref[...] = d_t.astype(BF16)
        y1_ref[...] = y1.astype(BF16)
        d_y1 = d_c * sg + _dot(d_t, w_ref[...], NT)
        dyp_ref[...] = d_y1 * _gelu_grad(yp)

    row = pl.BlockSpec((tr, DS), lambda i: (i, 0))
    return pl.pallas_call(
        body, name="s5_post_bwd", grid=(L // tr,),
        in_specs=[row, row, pl.BlockSpec((tr, DS), lambda i: (i, 1)), row, pl.BlockSpec((DS, DS), lambda i: (0, 0))],
        out_specs=[row, pl.BlockSpec((tr, DS), lambda i: (i, 1)), row, row, pl.BlockSpec((1, DS), lambda i: (0, 0))],
        out_shape=[jax.ShapeDtypeStruct((L, DS), F32), jax.ShapeDtypeStruct((L, 2 * DS), BF16),
                   jax.ShapeDtypeStruct((L, DS), BF16), jax.ShapeDtypeStruct((L, DS), BF16),
                   jax.ShapeDtypeStruct((1, DS), F32)],
        compiler_params=pltpu.CompilerParams(dimension_semantics=("arbitrary",)),
    )(d_ycat, y_pre, proj_main, t_pre, glu_w)


def _row_cumsum(x, reverse=False):
    n = x.shape[0]
    row = lax.broadcasted_iota(jnp.int32, x.shape, 0)
    k = 1
    while k < n:
        if reverse:
            x = x + jnp.where(row < n - k, pltpu.roll(x, n - k, 0), 0.0)
        else:
            x = x + jnp.where(row >= k, pltpu.roll(x, k, 0), 0.0)
        k *= 2
    return x


def _gla_gates(glow, gu_ref, gb_ref):
    a = _dot(glow, gu_ref[...]) + gb_ref[...]
    lg = (jnp.minimum(a, 0.0) - jnp.log(1.0 + jnp.exp(-jnp.abs(a)))) * (1.0 / GLA_TAU)
    ri = lax.broadcasted_iota(jnp.int32, (GLA_CHUNK, GLA_CHUNK), 0)
    ci = lax.broadcasted_iota(jnp.int32, (GLA_CHUNK, GLA_CHUNK), 1)
    b = _row_cumsum(lg)
    b_last = b[GLA_CHUNK - 1:GLA_CHUNK, :]
    return a, b, b_last, ri >= ci


def _gla_specs(DS, DK, DV, c, cmap):
    return [
        pl.BlockSpec((c, DK), lambda n: (cmap(n), 2 * DS // DK)),
        pl.BlockSpec((c, DK), lambda n: (cmap(n), 2 * DS // DK + 1)),
        pl.BlockSpec((c, DV), lambda n: (cmap(n), (2 * DS + 2 * DK) // DV)),
        pl.BlockSpec((c, DV), lambda n: (cmap(n), (2 * DS + 2 * DK) // DV + 1)),
    ]


def _gla_fwd(proj_main, proj_low, gate_up_pad, gate_bias, norm_w, ycat, DS, DK, DV):
    L = proj_main.shape[0]
    nc = L // GLA_CHUNK
    cps = math.gcd(GLA_STEP_CHUNKS, nc)
    nh = DK // GLA_HK
    scale = GLA_HK ** -0.5

    def body(q_ref, k_ref, v_ref, z_ref, gl_ref, gu_ref, gb_ref, nw_ref, _yc_ref, y_ref, sp_ref, st_ref):
        n = pl.program_id(0)

        @pl.when(n == 0)
        def _():
            st_ref[...] = jnp.zeros_like(st_ref)

        pairs = [(sc, h) for sc in range(cps) for h in range(nh)]
        rows = lambda sc: slice(sc * GLA_CHUNK, (sc + 1) * GLA_CHUNK)
        kcol = lambda h: slice(h * GLA_HK, (h + 1) * GLA_HK)
        vcol = lambda h: slice(h * GLA_HV, (h + 1) * GLA_HV)
        gates = [_gla_gates(gl_ref[rows(sc), :], gu_ref, gb_ref) for sc in range(cps)]
        qe, dec, o_in, kv = {}, {}, {}, {}
        for sc, h in pairs:
            _, b, b_last, mask = gates[sc]
            bh, bl = b[:, kcol(h)], b_last[:, kcol(h)]
            qe[sc, h] = (q_ref[rows(sc), kcol(h)] * scale) * jnp.exp(bh)
            kh = k_ref[rows(sc), kcol(h)]
            vh = v_ref[rows(sc), vcol(h)]
            attn = jnp.where(mask, _dot(qe[sc, h], kh * jnp.exp(-bh), NT), 0.0)
            o_in[sc, h] = _dot(attn, vh)
            kv[sc, h] = _dot(vh, kh * jnp.exp(bl - bh), TN)
            dec[sc, h] = jnp.exp(bl)
        for sc, h in pairs:
            st = st_ref[h]
            sp_ref[sc, h] = st
            o = o_in[sc, h] + _dot(qe[sc, h], st, NT)
            st_ref[h] = dec[sc, h] * st + kv[sc, h]
            r = lax.rsqrt(jnp.mean(o * o, axis=-1, keepdims=True) + EPS)
            z = z_ref[rows(sc), vcol(h)]
            y_ref[rows(sc), vcol(h)] = (o * r * nw_ref[...] * (z * _sigmoid(z))).astype(BF16)

    c = cps * GLA_CHUNK
    return pl.pallas_call(
        body, name="gla_fwd", grid=(nc // cps,),
        in_specs=_gla_specs(DS, DK, DV, c, lambda n: n) + [
            pl.BlockSpec((c, LANES), lambda n: (n, 0)),
            pl.BlockSpec((LANES, DK), lambda n: (0, 0)),
            pl.BlockSpec((1, DK), lambda n: (0, 0)),
            pl.BlockSpec((1, GLA_HV), lambda n: (0, 0)),
            pl.BlockSpec(memory_space=pl.ANY),
        ],
        out_specs=[pl.BlockSpec((c, DV), lambda n: (n, DS // DV)),
                   pl.BlockSpec((cps, nh, GLA_HV, GLA_HK), lambda n: (n, 0, 0, 0))],
        input_output_aliases={8: 0},
        out_shape=[jax.ShapeDtypeStruct(ycat.shape, BF16), jax.ShapeDtypeStruct((nc, nh, GLA_HV, GLA_HK), F32)],
        scratch_shapes=[pltpu.VMEM((nh, GLA_HV, GLA_HK), F32)],
        compiler_params=pltpu.CompilerParams(dimension_semantics=("arbitrary",)),
    )(proj_main, proj_main, proj_main, proj_main, proj_low, gate_up_pad, gate_bias, norm_w, ycat)


def _gla_bwd(d_ycat, proj_main, proj_low, s_prev, gate_up_pad, gate_bias, norm_w, DS, DK, DV):
    L = proj_main.shape[0]
    nc = L // GLA_CHUNK
    cps = math.gcd(GLA_STEP_CHUNKS, nc)
    nh = DK // GLA_HK
    scale = GLA_HK ** -0.5

    def body(dy_ref, q_ref, k_ref, v_ref, z_ref, gl_ref, sp_ref, gu_ref, gb_ref, nw_ref,
             dg_ref, da_ref, gnw_ref, ggb_ref, dst_ref):
        n = pl.program_id(0)

        @pl.when(n == 0)
        def _():
            dst_ref[...] = jnp.zeros_like(dst_ref)
            gnw_ref[...] = jnp.zeros_like(gnw_ref)
            ggb_ref[...] = jnp.zeros_like(ggb_ref)

        last_row = lax.broadcasted_iota(jnp.int32, (GLA_CHUNK, GLA_HK), 0) == GLA_CHUNK - 1
        nw = nw_ref[...]
        for sc in reversed(range(cps)):
            rs = slice(sc * GLA_CHUNK, (sc + 1) * GLA_CHUNK)
            a, b, b_last, mask = _gla_gates(gl_ref[rs, :], gu_ref, gb_ref)
            for h in range(nh):
                ks = slice(h * GLA_HK, (h + 1) * GLA_HK)
                vs = slice(h * GLA_HV, (h + 1) * GLA_HV)
                bh, bl = b[:, ks], b_last[:, ks]
                e = jnp.exp(bh)
                einv = jnp.exp(-bh)
                etail = jnp.exp(bl - bh)
                dec = jnp.exp(bl)
                qe = (q_ref[rs, ks] * scale) * e
                kh = k_ref[rs, ks]
                ke = kh * einv
                ktail = kh * etail
                vh = v_ref[rs, vs]
                st = sp_ref[sc, h]
                dst = dst_ref[h]
                attn = jnp.where(mask, _dot(qe, ke, NT), 0.0)
                o = _dot(attn, vh) + _dot(qe, st, NT)
                r = lax.rsqrt(jnp.mean(o * o, axis=-1, keepdims=True) + EPS)
                nrm = o * r
                z = z_ref[rs, vs]
                sz = _sigmoid(z)
                dy = dy_ref[rs, vs]
                dg_ref[rs, 2 * DK + DV + h * GLA_HV:2 * DK + DV + (h + 1) * GLA_HV] = (
                    dy * nrm * nw * (sz * (1.0 + z * (1.0 - sz)))).astype(BF16)
                d_on = dy * (z * sz)
                gnw_ref[...] += jnp.sum(d_on * nrm, axis=0, keepdims=True)
                d_n = d_on * nw
                d_o = r * (d_n - nrm * jnp.mean(d_n * nrm, axis=-1, keepdims=True))
                d_attn = jnp.where(mask, _dot(d_o, vh, NT), 0.0)
                dg_ref[rs, 2 * DK + h * GLA_HV:2 * DK + (h + 1) * GLA_HV] = (
                    _dot(attn, d_o, TN) + _dot(ktail, dst, NT)).astype(BF16)
                d_qe = _dot(d_attn, ke) + _dot(d_o, st)
                d_ke = _dot(d_attn, qe, TN)
                d_kt = _dot(vh, dst)
                d_dec = jnp.sum(dst * st, axis=0, keepdims=True)
                dst_ref[h] = dec * dst + _dot(d_o, qe, TN)
                dg_ref[rs, ks] = (d_qe * scale * e).astype(BF16)
                dg_ref[rs, DK + h * GLA_HK:DK + (h + 1) * GLA_HK] = (d_ke * einv + d_kt * etail).astype(BF16)
                d_bl = jnp.sum(d_kt * ktail, axis=0, keepdims=True) + d_dec * dec
                d_b = d_qe * qe - d_ke * ke - d_kt * ktail + jnp.where(last_row, d_bl, 0.0)
                d_lg = _row_cumsum(d_b, reverse=True)
                d_a = d_lg * (1.0 / GLA_TAU) * _sigmoid(-a[:, ks])
                ggb_ref[:, ks] += jnp.sum(d_a, axis=0, keepdims=True)
                da_ref[rs, ks] = d_a.astype(BF16)

    c = cps * GLA_CHUNK
    ns = nc // cps
    rn = lambda n: ns - 1 - n
    return pl.pallas_call(
        body, name="gla_bwd", grid=(ns,),
        in_specs=[pl.BlockSpec((c, DV), lambda n: (rn(n), DS // DV))] + _gla_specs(DS, DK, DV, c, rn) + [
            pl.BlockSpec((c, LANES), lambda n: (rn(n), 0)),
            pl.BlockSpec((cps, nh, GLA_HV, GLA_HK), lambda n: (rn(n), 0, 0, 0)),
            pl.BlockSpec((LANES, DK), lambda n: (0, 0)),
            pl.BlockSpec((1, DK), lambda n: (0, 0)),
            pl.BlockSpec((1, GLA_HV), lambda n: (0, 0)),
        ],
        out_specs=[pl.BlockSpec((c, 2 * DK + 2 * DV), lambda n: (rn(n), 0)),
                   pl.BlockSpec((c, DK), lambda n: (rn(n), 0)),
                   pl.BlockSpec((1, GLA_HV), lambda n: (0, 0)), pl.BlockSpec((1, DK), lambda n: (0, 0))],
        out_shape=[jax.ShapeDtypeStruct((L, 2 * DK + 2 * DV), BF16),
                   jax.ShapeDtypeStruct((L, DK), BF16),
                   jax.ShapeDtypeStruct((1, GLA_HV), F32), jax.ShapeDtypeStruct((1, DK), F32)],
        scratch_shapes=[pltpu.VMEM((nh, GLA_HV, GLA_HK), F32)],
        compiler_params=pltpu.CompilerParams(dimension_semantics=("arbitrary",)),
    )(d_ycat, proj_main, proj_main, proj_main, proj_main, proj_low, s_prev, gate_up_pad, gate_bias, norm_w)


def _adamw_math(w, g, m, v):
    c1 = 1.0 - ADAM_B1 ** ADAM_STEP
    c2 = 1.0 - ADAM_B2 ** ADAM_STEP
    m_ = ADAM_B1 * m + (1.0 - ADAM_B1) * g
    v_ = ADAM_B2 * v + (1.0 - ADAM_B2) * (g * g)
    return -ADAM_LR * ((m_ / c1) / (jnp.sqrt(v_ / c2) + ADAM_EPS) + ADAM_WD * w), m_, v_


def _adamw_small(g_row, g_a, g_bc, ws, ms, vs):
    n = len(ws)
    nvec = n - 6

    def body(*refs):
        grow_ref, ga_ref, gbc_ref = refs[:3]
        w_refs, m_refs, v_refs = refs[3:3 + n], refs[3 + n:3 + 2 * n], refs[3 + 2 * n:3 + 3 * n]
        outs = refs[3 + 3 * n:]
        off = 0
        for i in range(n):
            if i < nvec:
                width = ws[i].shape[1]
                g = grow_ref[:, off:off + width]
                off += width
            elif i < nvec + 2:
                g = ga_ref[i - nvec]
            else:
                g = gbc_ref[i - nvec - 2]
            d, m_, v_ = _adamw_math(w_refs[i][...], g, m_refs[i][...], v_refs[i][...])
            outs[i][...] = g
            outs[n + i][...] = d
            outs[2 * n + i][...] = m_
            outs[3 * n + i][...] = v_

    vm = pl.BlockSpec(memory_space=pltpu.VMEM)
    outs = pl.pallas_call(
        body, name="adamw_small",
        in_specs=[vm] * (3 + 3 * n), out_specs=[vm] * (4 * n),
        out_shape=[jax.ShapeDtypeStruct(w.shape, F32) for w in ws] * 4,
    )(g_row, g_a, g_bc, *ws, *ms, *vs)
    return [outs[k * n:(k + 1) * n] for k in range(4)]


def _my_pos():
    return lax.axis_index("x"), lax.axis_index("y"), lax.axis_index("c")


def _split_start(name, srcs, lands_sd, make_copies, ncopies, after):
    n, m = len(srcs), len(lands_sd)

    def body(*refs):
        send_sems, recv_sems = refs[n + m + len(after)], refs[n + m + len(after) + 1]
        for cp in make_copies(refs[:n], refs[n:n + m], send_sems, recv_sems):
            cp.start()
        refs[-1][...] = jnp.zeros_like(refs[-1])

    hbm = pl.BlockSpec(memory_space=pltpu.HBM)
    sem = pl.BlockSpec(memory_space=pltpu.SEMAPHORE)
    outs = pl.pallas_call(
        body, name=name,
        in_specs=[hbm] * (n + m) + [pl.BlockSpec(memory_space=pl.ANY)] * len(after),
        out_specs=[sem, sem] + [hbm] * (n + m) + [pl.BlockSpec(memory_space=pltpu.VMEM)],
        out_shape=[pltpu.SemaphoreType.DMA((ncopies,)), pltpu.SemaphoreType.DMA((ncopies,))]
        + [pltpu.HBM(s.shape, s.dtype) for s in srcs] + [pltpu.HBM(s.shape, s.dtype) for s in lands_sd]
        + [jax.ShapeDtypeStruct((SUBLANES, LANES), F32)],
        input_output_aliases={i: 2 + i for i in range(n + m)},
        compiler_params=pltpu.CompilerParams(has_side_effects=pltpu.SideEffectType.DATAFLOW_SIDE_EFFECTING),
    )(*[pltpu.with_memory_space_constraint(s, pltpu.HBM) for s in srcs],
      *[pltpu.with_memory_space_constraint(lax.empty(s.shape, s.dtype), pltpu.HBM) for s in lands_sd], *after)
    return outs[0], outs[1], outs[2:2 + n], outs[2 + n:2 + n + m], outs[-1]


def _split_wait(name, send_sems, recv_sems, srcs, lands, make_copies, after):
    n, m = len(srcs), len(lands)

    def body(*refs):
        for cp in make_copies(refs[:n], refs[n:n + m], refs[n + m], refs[n + m + 1]):
            cp.wait_send()
            cp.wait_recv()

    hbm = pl.BlockSpec(memory_space=pltpu.HBM)
    sem = pl.BlockSpec(memory_space=pltpu.SEMAPHORE)
    outs = pl.pallas_call(
        body, name=name,
        in_specs=[hbm] * (n + m) + [sem, sem] + [pl.BlockSpec(memory_space=pl.ANY)] * len(after),
        out_specs=[hbm] * (n + m),
        out_shape=[pltpu.HBM(s.shape, s.dtype) for s in srcs] + [pltpu.HBM(p.shape, p.dtype) for p in lands],
        input_output_aliases={i: i for i in range(n + m)},
        compiler_params=pltpu.CompilerParams(has_side_effects=pltpu.SideEffectType.DATAFLOW_SIDE_EFFECTING),
    )(*srcs, *lands, send_sems, recv_sems, *after)
    return outs[:n], outs[n:]


def _pair_half_copies(srcs, lands, send_sems, recv_sems):
    x, y, c = _my_pos()
    copies = []
    for a in range(len(srcs)):
        hrows = srcs[a].shape[1] // 2
        copies.append(pltpu.make_async_remote_copy(
            src_ref=srcs[a].at[:, pl.ds((1 - c) * hrows, hrows), :], dst_ref=lands[a], send_sem=send_sems.at[a],
            recv_sem=recv_sems.at[a], device_id=(x, y, 1 - c), device_id_type=MESH))
    return copies


def _late_gather_copies(srcs, lands, send_sems, recv_sems):
    x, y, c = _my_pos()
    me = 2 * x + y
    copies = []
    for d in (1, 2, 3):
        to = (x ^ (d >> 1), y ^ (d & 1), c)
        for a in range(len(srcs)):
            hrows = srcs[a].shape[0] // 2
            rows = pl.ds(c * hrows, hrows)
            copies.append(pltpu.make_async_remote_copy(
                src_ref=srcs[a].at[rows, :], dst_ref=lands[a].at[me, rows, :], send_sem=send_sems.at[3 * a + d - 1],
                recv_sem=recv_sems.at[3 * a + d - 1], device_id=to, device_id_type=MESH))
    return copies


def _late_gather_start(shards, after, name):
    n = len(shards)

    def body(*refs):
        srcs, lands = refs[:n], refs[n:2 * n]
        send_sems, recv_sems = refs[2 * n + 1], refs[2 * n + 2]
        token = refs[-1]
        for cp in _late_gather_copies(srcs, lands, send_sems, recv_sems):
            cp.start()
        token[...] = jnp.zeros_like(token)

    hbm = pl.BlockSpec(memory_space=pltpu.HBM)
    sem = pl.BlockSpec(memory_space=pltpu.SEMAPHORE)
    outs = pl.pallas_call(
        body, name=name,
        in_specs=[hbm] * (2 * n) + [pl.BlockSpec(memory_space=pl.ANY)],
        out_specs=[sem, sem] + [hbm] * (2 * n) + [pl.BlockSpec(memory_space=pltpu.VMEM)],
        out_shape=[pltpu.SemaphoreType.DMA((3 * n,)), pltpu.SemaphoreType.DMA((3 * n,))]
        + [pltpu.HBM(s.shape, s.dtype) for s in shards]
        + [pltpu.HBM((4,) + s.shape, s.dtype) for s in shards]
        + [jax.ShapeDtypeStruct((SUBLANES, LANES), F32)],
        input_output_aliases={i: 2 + i for i in range(2 * n)},
        compiler_params=pltpu.CompilerParams(has_side_effects=pltpu.SideEffectType.DATAFLOW_SIDE_EFFECTING),
    )(*[pltpu.with_memory_space_constraint(s, pltpu.HBM) for s in shards],
      *[pltpu.with_memory_space_constraint(lax.empty((4,) + s.shape, s.dtype), pltpu.HBM) for s in shards], after)
    return outs[0], outs[1], outs[2:2 + n], outs[2 + n:2 + 2 * n], outs[-1]


def _late_gather_wait(send_sems, recv_sems, shards, lands, after, name):
    n = len(shards)

    def body(*refs):
        src_refs, land_refs = refs[:n], refs[n:2 * n]
        ssem, rsem = refs[2 * n], refs[2 * n + 1]
        for cp in _late_gather_copies(src_refs, land_refs, ssem, rsem):
            cp.wait_send()
            cp.wait_recv()

    hbm = pl.BlockSpec(memory_space=pltpu.HBM)
    sem = pl.BlockSpec(memory_space=pltpu.SEMAPHORE)
    outs = pl.pallas_call(
        body, name=name,
        in_specs=[hbm] * (2 * n) + [sem, sem] + [pl.BlockSpec(memory_space=pl.ANY)] * len(after),
        out_specs=[hbm] * (2 * n),
        out_shape=[pltpu.HBM(s.shape, s.dtype) for s in shards] + [pltpu.HBM(p.shape, p.dtype) for p in lands],
        input_output_aliases={i: i for i in range(2 * n)},
        compiler_params=pltpu.CompilerParams(has_side_effects=pltpu.SideEffectType.DATAFLOW_SIDE_EFFECTING),
    )(*shards, *lands, send_sems, recv_sems, *after)
    return outs[n:]


def _late_gather_pair(lands, name):
    n = len(lands)

    def body(*refs):
        outs = refs[n:2 * n]
        send_sems, recv_sems = refs[2 * n:]
        x, y, c = _my_pos()

        def copy(a, d, half):
            chip = 2 * (x ^ (d >> 1)) + (y ^ (d & 1))
            hrows = lands[a].shape[1] // 2
            sl = outs[a].at[chip, pl.ds(half * hrows, hrows), :]
            return pltpu.make_async_remote_copy(src_ref=sl, dst_ref=sl, send_sem=send_sems.at[3 * a + d - 1],
                                                recv_sem=recv_sems.at[3 * a + d - 1], device_id=(x, y, 1 - c),
                                                device_id_type=MESH)

        pairs = [(a, d) for d in (1, 2, 3) for a in range(n)]
        for a, d in pairs:
            copy(a, d, c).start()
        for a, d in pairs:
            copy(a, d, c).wait_send()
            copy(a, d, 1 - c).wait_recv()

    hbm = pl.BlockSpec(memory_space=pltpu.HBM)
    return pl.pallas_call(
        body, name=name, in_specs=[hbm] * n, out_specs=[hbm] * n,
        out_shape=[jax.ShapeDtypeStruct(p.shape, p.dtype) for p in lands],
        input_output_aliases={i: i for i in range(n)},
        scratch_shapes=[pltpu.SemaphoreType.DMA((3 * n,)), pltpu.SemaphoreType.DMA((3 * n,))],
    )(*lands)


def _pair_exchange(gs):
    n = len(gs)

    def body(*refs):
        ins, outs = refs[:n], refs[n:2 * n]
        send_sems, recv_sems = refs[2 * n:]
        x, y, c = _my_pos()
        sent = []
        for a in range(n):
            hrows = gs[a].shape[1] // 2
            cp = pltpu.make_async_remote_copy(
                src_ref=ins[a].at[:, pl.ds((1 - c) * hrows, hrows), :], dst_ref=outs[a], send_sem=send_sems.at[a],
                recv_sem=recv_sems.at[a], device_id=(x, y, 1 - c), device_id_type=MESH)
            cp.start()
            sent.append(cp)
        for cp in sent:
            cp.wait()

    hbm = pl.BlockSpec(memory_space=pltpu.HBM)
    return pl.pallas_call(
        body, name="grad_pair_exchange", in_specs=[hbm] * n, out_specs=[hbm] * n,
        out_shape=[jax.ShapeDtypeStruct((g.shape[0], g.shape[1] // 2, g.shape[2]), g.dtype) for g in gs],
        scratch_shapes=[pltpu.SemaphoreType.DMA((n,)), pltpu.SemaphoreType.DMA((n,))],
    )(*gs)


def _pair_add(g, got, c_arr, name):
    nk, rows2, cols = g.shape
    hrows = rows2 // 2
    tr = _blk(hrows, 256, 2 * SUBLANES)
    nb = hrows // tr

    def body(c_ref, a_ref, b_ref, o_ref):
        o_ref[...] = (a_ref[...].astype(F32) + b_ref[...].astype(F32)).astype(o_ref.dtype)

    return pl.pallas_call(
        body, name=name,
        grid_spec=pltpu.PrefetchScalarGridSpec(
            num_scalar_prefetch=1, grid=(nk, nb),
            in_specs=[pl.BlockSpec((1, tr, cols), lambda k, i, c_ref: (k, c_ref[0] * nb + i, 0)),
                      pl.BlockSpec((1, tr, cols), lambda k, i, c_ref: (k, i, 0))],
            out_specs=pl.BlockSpec((1, tr, cols), lambda k, i, c_ref: (k, i, 0))),
        out_shape=jax.ShapeDtypeStruct((nk, hrows, cols), g.dtype),
        compiler_params=pltpu.CompilerParams(dimension_semantics=("parallel", "parallel")),
    )(c_arr, g, got)


def _chip_scatter_copies(srcs, lands, send_sems, recv_sems):
    x, y, c = _my_pos()
    copies = []
    for d in (1, 2, 3):
        tx, ty = x ^ (d >> 1), y ^ (d & 1)
        for a in range(len(srcs)):
            copies.append(pltpu.make_async_remote_copy(
                src_ref=srcs[a].at[2 * tx + ty], dst_ref=lands[a].at[d - 1], send_sem=send_sems.at[3 * a + d - 1],
                recv_sem=recv_sems.at[3 * a + d - 1], device_id=(tx, ty, c), device_id_type=MESH))
    return copies


def _chip_scatter_start(pss):
    n = len(pss)

    def body(*refs):
        srcs, lands = refs[:n], refs[n:2 * n]
        send_sems, recv_sems = refs[2 * n], refs[2 * n + 1]
        token = refs[-1]
        for cp in _chip_scatter_copies(srcs, lands, send_sems, recv_sems):
            cp.start()
        token[...] = jnp.zeros_like(token)

    hbm = pl.BlockSpec(memory_space=pltpu.HBM)
    sem = pl.BlockSpec(memory_space=pltpu.SEMAPHORE)
    land_shapes = [(3,) + p.shape[1:] for p in pss]
    outs = pl.pallas_call(
        body, name="grad_chip_scatter_start",
        in_specs=[hbm] * (2 * n),
        out_specs=[sem, sem] + [hbm] * (2 * n) + [pl.BlockSpec(memory_space=pltpu.VMEM)],
        out_shape=[pltpu.SemaphoreType.DMA((3 * n,)), pltpu.SemaphoreType.DMA((3 * n,))]
        + [pltpu.HBM(p.shape, p.dtype) for p in pss]
        + [pltpu.HBM(s, p.dtype) for s, p in zip(land_shapes, pss)]
        + [jax.ShapeDtypeStruct((SUBLANES, LANES), F32)],
        input_output_aliases={i: 2 + i for i in range(2 * n)},
        compiler_params=pltpu.CompilerParams(has_side_effects=pltpu.SideEffectType.DATAFLOW_SIDE_EFFECTING),
    )(*[pltpu.with_memory_space_constraint(p, pltpu.HBM) for p in pss],
      *[pltpu.with_memory_space_constraint(lax.empty(s, p.dtype), pltpu.HBM) for s, p in zip(land_shapes, pss)])
    return outs[0], outs[1], outs[2:2 + n], outs[2 + n:2 + 2 * n], outs[-1]


def _chip_scatter_wait(send_sems, recv_sems, srcs, lands, after):
    n = len(srcs)

    def body(*refs):
        src_refs, land_refs = refs[:n], refs[n:2 * n]
        ssem, rsem = refs[2 * n], refs[2 * n + 1]
        for cp in _chip_scatter_copies(src_refs, land_refs, ssem, rsem):
            cp.wait_send()
            cp.wait_recv()

    hbm = pl.BlockSpec(memory_space=pltpu.HBM)
    sem = pl.BlockSpec(memory_space=pltpu.SEMAPHORE)
    outs = pl.pallas_call(
        body, name="grad_chip_scatter_wait",
        in_specs=[hbm] * (2 * n) + [sem, sem, pl.BlockSpec(memory_space=pl.ANY)],
        out_specs=[hbm] * (2 * n),
        out_shape=[pltpu.HBM(p.shape, p.dtype) for p in srcs] + [pltpu.HBM(p.shape, p.dtype) for p in lands],
        input_output_aliases={i: i for i in range(2 * n)},
        compiler_params=pltpu.CompilerParams(has_side_effects=pltpu.SideEffectType.DATAFLOW_SIDE_EFFECTING),
    )(*srcs, *lands, send_sems, recv_sems, after)
    return outs[:n], outs[n:]


def _chip_sum(ps, got, me_arr, name):
    _, hrows, cols = ps.shape
    tr = _blk(hrows, 256, 2 * SUBLANES)

    def body(me_ref, p_ref, g_ref, o_ref):
        acc = p_ref[0].astype(F32)
        for s in range(3):
            acc = acc + g_ref[s].astype(F32)
        o_ref[...] = acc

    return pl.pallas_call(
        body, name=name,
        grid_spec=pltpu.PrefetchScalarGridSpec(
            num_scalar_prefetch=1, grid=(hrows // tr,),
            in_specs=[pl.BlockSpec((1, tr, cols), lambda i, me_ref: (me_ref[0], i, 0)),
                      pl.BlockSpec((3, tr, cols), lambda i, me_ref: (0, i, 0))],
            out_specs=pl.BlockSpec((tr, cols), lambda i, me_ref: (i, 0))),
        out_shape=jax.ShapeDtypeStruct((hrows, cols), F32),
        compiler_params=pltpu.CompilerParams(dimension_semantics=("parallel",)),
    )(me_arr, ps, got)


def _pair_swap(halves):
    n = len(halves)

    def body(*refs):
        ins, outs = refs[:n], refs[n:2 * n]
        send_sems, recv_sems = refs[2 * n:]
        x, y, c = _my_pos()
        sent = []
        for a in range(n):
            cp = pltpu.make_async_remote_copy(src_ref=ins[a], dst_ref=outs[a], send_sem=send_sems.at[a], recv_sem=recv_sems.at[a],
                                              device_id=(x, y, 1 - c), device_id_type=MESH)
            cp.start()
            sent.append(cp)
        for cp in sent:
            cp.wait()

    hbm = pl.BlockSpec(memory_space=pltpu.HBM)
    return pl.pallas_call(
        body, name="grad_pair_swap", in_specs=[hbm] * n, out_specs=[hbm] * n,
        out_shape=[jax.ShapeDtypeStruct(h.shape, h.dtype) for h in halves],
        scratch_shapes=[pltpu.SemaphoreType.DMA((n,)), pltpu.SemaphoreType.DMA((n,))],
    )(*halves)


def _adamw_sharded(w, g_own, g_other, m, v, c_arr, after, name):
    R, C = w.shape
    hrows = R // 2
    tr = _blk(hrows, 256, SUBLANES)
    nbh = hrows // tr

    def body(c_ref, w_ref, go_ref, gx_ref, m_ref, v_ref, _after_ref, g_ref, d_ref, nm_ref, nv_ref):
        mine = (pl.program_id(0) // nbh) == c_ref[0]
        g_ = jnp.where(mine, go_ref[...], gx_ref[...])
        g_ref[...] = g_
        d_ref[...], nm_ref[...], nv_ref[...] = _adamw_math(w_ref[...], g_, m_ref[...], v_ref[...])

    blk = pl.BlockSpec((tr, C), lambda i, c_ref: (i, 0))
    hblk = pl.BlockSpec((tr, C), lambda i, c_ref: (i % nbh, 0))
    sd = jax.ShapeDtypeStruct((R, C), F32)
    return pl.pallas_call(
        body, name=name,
        grid_spec=pltpu.PrefetchScalarGridSpec(
            num_scalar_prefetch=1, grid=(2 * nbh,),
            in_specs=[blk, hblk, hblk, blk, blk, pl.BlockSpec(memory_space=pl.ANY)], out_specs=[blk] * 4),
        out_shape=[sd] * 4,
        compiler_params=pltpu.CompilerParams(dimension_semantics=("parallel",)),
    )(c_arr, w, g_own, g_other, m, v, after)


def _ar_piece(ref, rows, p):
    start = p * rows
    if rows % SUBLANES == 0:
        start = pl.multiple_of(start, SUBLANES)
    return ref.at[..., pl.ds(start, rows), :]


def _ar_peer(d):
    x, y, c = _my_pos()
    return (x ^ (d >> 2), y ^ ((d >> 1) & 1), c ^ (d & 1))


def _ar_lin(p):
    return 4 * p[0] + 2 * p[1] + p[2]


def _ar_scatter_copies(rows):
    def make(srcs, lands, send_sems, recv_sems):
        n = len(srcs)
        copies = []
        for d in range(1, 8):
            to = _ar_peer(d)
            for a in range(n):
                copies.append(pltpu.make_async_remote_copy(
                    src_ref=_ar_piece(srcs[a], rows[a], _ar_lin(to)), dst_ref=lands[a].at[d],
                    send_sem=send_sems.at[(d - 1) * n + a], recv_sem=recv_sems.at[(d - 1) * n + a], device_id=to,
                    device_id_type=MESH))
        return copies
    return make


def _ar_gather_copies(rows):
    def make(srcs, lands, send_sems, recv_sems):
        n = len(srcs)
        me = _ar_lin(_my_pos())
        copies = []
        for d in range(1, 8):
            for a in range(n):
                copies.append(pltpu.make_async_remote_copy(
                    src_ref=srcs[a], dst_ref=_ar_piece(lands[a], rows[a], me),
                    send_sem=send_sems.at[(d - 1) * n + a], recv_sem=recv_sems.at[(d - 1) * n + a], device_id=_ar_peer(d),
                    device_id_type=MESH))
        return copies
    return make


def _ar_sum(srcs, lands, rows):
    n = len(srcs)

    def body(*refs):
        me = _ar_lin(_my_pos())
        for a in range(n):
            acc = _ar_piece(refs[a], rows[a], me)[...]
            for d in range(1, 8):
                acc = acc + refs[n + a][d]
            refs[2 * n + a][...] = acc

    vm = pl.BlockSpec(memory_space=pltpu.VMEM)
    return pl.pallas_call(
        body, name="allreduce_sum", in_specs=[vm] * (2 * n), out_specs=[vm] * n,
        out_shape=[jax.ShapeDtypeStruct(p.shape[1:], F32) for p in lands],
    )(*srcs, *lands)


def kernel(x, pre_norm_w, w_in, s5_A_re, s5_A_im, s5_B_re, s5_B_im, s5_C_re, s5_C_im, s5_D, s5_log_dt, s5_glu_w, s5_glu_b, gla_gate_up, gla_gate_bias, gla_norm_w, w_out, post_norm_w, loss_target, m_pre_norm_w, m_w_in, m_s5_A_re, m_s5_A_im, m_s5_B_re, m_s5_B_im, m_s5_C_re, m_s5_C_im, m_s5_D, m_s5_log_dt, m_s5_glu_w, m_s5_glu_b, m_gla_gate_up, m_gla_gate_bias, m_gla_norm_w, m_w_out, m_post_norm_w, v_pre_norm_w, v_w_in, v_s5_A_re, v_s5_A_im, v_s5_B_re, v_s5_B_im, v_s5_C_re, v_s5_C_im, v_s5_D, v_s5_log_dt, v_s5_glu_w, v_s5_glu_b, v_gla_gate_up, v_gla_gate_bias, v_gla_norm_w, v_w_out, v_post_norm_w):
    names = ["pre_norm_w", "w_in", "s5_A_re", "s5_A_im", "s5_B_re", "s5_B_im", "s5_C_re", "s5_C_im", "s5_D", "s5_log_dt",
             "s5_glu_w", "s5_glu_b", "gla_gate_up", "gla_gate_bias", "gla_norm_w", "w_out", "post_norm_w"]
    W = dict(zip(names, (pre_norm_w, w_in, s5_A_re, s5_A_im, s5_B_re, s5_B_im, s5_C_re, s5_C_im, s5_D, s5_log_dt,
                         s5_glu_w, s5_glu_b, gla_gate_up, gla_gate_bias, gla_norm_w, w_out, post_norm_w)))
    M = dict(zip(names, (m_pre_norm_w, m_w_in, m_s5_A_re, m_s5_A_im, m_s5_B_re, m_s5_B_im, m_s5_C_re, m_s5_C_im, m_s5_D,
                         m_s5_log_dt, m_s5_glu_w, m_s5_glu_b, m_gla_gate_up, m_gla_gate_bias, m_gla_norm_w, m_w_out,
                         m_post_norm_w)))
    V = dict(zip(names, (v_pre_norm_w, v_w_in, v_s5_A_re, v_s5_A_im, v_s5_B_re, v_s5_B_im, v_s5_C_re, v_s5_C_im, v_s5_D,
                         v_s5_log_dt, v_s5_glu_w, v_s5_glu_b, v_gla_gate_up, v_gla_gate_bias, v_gla_norm_w, v_w_out,
                         v_post_norm_w)))
    sharded = ("w_in", "s5_glu_w", "w_out", "gla_gate_up")

    xb = x[0]
    tgt = loss_target[0]
    L, D = xb.shape
    DS = D // 2
    G = DS // S5_GROUP
    P = S5_STATE
    NB = DS // S5_COLS
    DV = D - DS
    DK = DV // 2
    WM = 2 * DS + 2 * DK + 2 * DV
    nsh = w_in.shape[2]

    chip = 2 * lax.axis_index("x") + lax.axis_index("y")
    own = [jnp.pad(w_in[0].astype(BF16), ((0, 0), (0, -nsh % LANES))), s5_glu_w[0].astype(BF16),
           w_out[0].astype(BF16), gla_gate_up[0]]
    fill = lambda g, o: lax.dynamic_update_index_in_dim(g, o, chip, 0)
    win_ss, win_rs, win_src, win_lands, win_token = _late_gather_start(own[:1], pre_norm_w, "w_in_gather_start")
    h = _prenorm_fwd(xb, pre_norm_w, win_token)

    b_view = lambda t: jnp.transpose(t[0], (0, 2, 1)).reshape(G * S5_GROUP, P)
    b_back = lambda t: jnp.transpose(t.reshape(G, S5_GROUP, P), (0, 2, 1))[None]
    c_view = lambda t: t[0].reshape(G * S5_GROUP, P)
    c_back = lambda t: t.reshape(1, G, S5_GROUP, P)
    small = ["pre_norm_w", "post_norm_w", "s5_D", "s5_glu_b", "gla_gate_bias", "gla_norm_w", "s5_log_dt",
             "s5_A_re", "s5_A_im", "s5_B_re", "s5_B_im", "s5_C_re", "s5_C_im"]
    view = {n: (lambda t: t) for n in small[:7]}
    back = dict(view)
    view.update(s5_A_re=lambda t: t[0], s5_A_im=lambda t: t[0], s5_B_re=b_view, s5_B_im=b_view, s5_C_re=c_view, s5_C_im=c_view)
    back.update(s5_A_re=lambda t: t[None], s5_A_im=lambda t: t[None], s5_B_re=b_back, s5_B_im=b_back, s5_C_re=c_back,
                s5_C_im=c_back)
    Wv = {n: view[n](W[n]) for n in small}
    bbd_re, bbd_im, ct_re, ct_im, tab, ptab = _s5_prep_fwd(
        Wv["s5_A_re"], Wv["s5_A_im"], s5_log_dt, Wv["s5_B_re"], Wv["s5_B_im"], Wv["s5_C_re"], Wv["s5_C_im"],
        h, _blk(L, S5_TIME_BLOCK, SUBLANES) // SUBLANES)
    dvec = s5_D

    for d_ in (W, M, V):
        d_["w_in"], _ = lax.optimization_barrier((d_["w_in"], win_token))
    g_win = _late_gather_wait(win_ss, win_rs, win_src, win_lands,
                              [tab, W["w_in"][0], M["w_in"][0], V["w_in"][0]], "w_in_gather_wait")
    g_win = fill(_late_gather_pair(g_win, "w_in_gather_pair")[0], own[0])
    w_main, w_low = _assemble_w_in(g_win, nsh, WM)
    late_ss, late_rs, late_src, late_lands, late_token = _late_gather_start(own[1:], g_win, "late_gather_start")
    proj_main, proj_low = _in_proj(h, w_main, w_low, late_token)
    y_pre, s_re, s_im = _s5_scan_fwd(proj_main, bbd_re, bbd_im, ct_re, ct_im, dvec, tab, ptab, DS)
    late = _late_gather_wait(late_ss, late_rs, late_src, late_lands, [y_pre], "late_gather_wait")
    late = _late_gather_pair(late, "late_gather_pair")
    g_glu, g_wout, g_gup = [fill(g, o) for g, o in zip(late, own[1:])]
    glu_w = g_glu.reshape(DS, DS)
    wout = g_wout.reshape(D, D)
    gup = jnp.moveaxis(g_gup, 0, 1).reshape(GLA_RANK, DK)
    gup_pad = jnp.pad(gup, ((0, LANES - GLA_RANK), (0, 0))).astype(BF16)
    ycat, t_pre = _s5_post_fwd(y_pre, proj_main, glu_w, s5_glu_b, DS)
    ycat, s_prev = _gla_fwd(proj_main, proj_low, gup_pad, gla_gate_bias, gla_norm_w, ycat, DS, DK, DV)
    mixed = _mm(ycat, wout, name="out_proj")
    loss11, d_mixed, dout, g_post_w = _post_fwd_bwd(mixed, xb, tgt, post_norm_w)

    d_ycat = _mm(d_mixed, wout, tb=True, name="out_proj_dx")
    d_ypre, d_s5, d_t, y1, g_glu_b = _s5_post_bwd(d_ycat, y_pre, proj_main, t_pre, glu_w, DS)
    d_s5, g_D, gct_re, gct_im, gbbd_re, gbbd_im, gab_re, gab_im = _s5_scan_bwd(
        d_ypre, proj_main, s_re, s_im, bbd_re, bbd_im, ct_re, ct_im, dvec, tab, ptab, d_s5, DS)
    d_gla, d_a, g_norm_w, g_gate_bias = _gla_bwd(
        d_ycat, proj_main, proj_low, s_prev, gup_pad, gla_gate_bias, gla_norm_w, DS, DK, DV)
    d_low = _mm(d_a, gup_pad, tb=True, out_dtype=BF16, name="gate_dx")
    g_gup_pad = _mm(proj_low, d_a, ta=True, name="gate_dw")
    g_wmain, g_wlow = _in_proj_dw(h, d_s5, d_gla, d_low)

    g_win_sh = _split_w_in_grad(g_wmain, g_wlow, nsh)
    px_ss, px_rs, px_src, px_got, px_token = _split_start(
        "grad_pair_w_in_start", [g_win_sh], [jax.ShapeDtypeStruct((4, D // 2, nsh), BF16)], _pair_half_copies, 1, [])
    g_wout_full = _mm(ycat, d_mixed, ta=True, out_dtype=BF16, name="out_proj_dw", after=[px_token])
    g_glu_full = _mm(y1, d_t, ta=True, out_dtype=BF16, name="glu_dw", after=[px_token])
    px_src, px_got = _split_wait("grad_pair_w_in_wait", px_ss, px_rs, px_src, px_got, _pair_half_copies,
                                 [g_wout_full, g_glu_full])
    gs = [g_glu_full.reshape(4, DS // 4, DS), g_wout_full.reshape(4, D // 4, D),
          jnp.moveaxis(g_gup_pad[:GLA_RANK].reshape(GLA_RANK, 4, DK // 4), 1, 0)]
    c_arr = lax.axis_index("c").astype(jnp.int32).reshape(1)
    me_arr = chip.astype(jnp.int32).reshape(1)
    got = list(px_got) + list(_pair_exchange(gs))
    gs = list(px_src) + gs
    pss = [_pair_add(g, r, c_arr, "grad_pair_add_" + n) for n, g, r in zip(sharded, gs, got)]
    send_sems, recv_sems, pss, lands, token = _chip_scatter_start(pss)

    dh = _in_proj_dx(d_s5, d_gla, d_low, w_main, w_low, token)
    grad_x, g_pre_w = _prenorm_bwd(xb, dh, dout, pre_norm_w)

    g_a, g_bc, g_ldt = _s5_prep_bwd(Wv["s5_A_re"], Wv["s5_A_im"], s5_log_dt, Wv["s5_B_re"], Wv["s5_B_im"],
                                    gbbd_re, gbbd_im, gct_re, gct_im, gab_re, gab_im)

    g_vecs = jnp.concatenate([g_pre_w, g_post_w, g_D, g_glu_b, g_gate_bias, g_norm_w, g_ldt, loss11], axis=1)
    loss_at = g_vecs.shape[1] - 1
    lanes_pad = -g_vecs.shape[1] % (8 * SUBLANES * LANES)
    g_vecs = jnp.pad(g_vecs, ((0, 0), (0, lanes_pad))).reshape(-1, LANES)
    ar_srcs = [g_vecs, g_a, g_bc]
    ar_rows = [a.shape[-2] // 8 for a in ar_srcs]
    ar_lands = [jax.ShapeDtypeStruct((8,) + a.shape[:-2] + (r, a.shape[-1]), F32) for a, r in zip(ar_srcs, ar_rows)]
    ar_ss, ar_rs, ar_srcs, ar_got, ar_token = _split_start(
        "allreduce_scatter_start", ar_srcs, ar_lands, _ar_scatter_copies(ar_rows), 7 * len(ar_srcs), [])

    pss, rcv = _chip_scatter_wait(send_sems, recv_sems, pss, lands, ar_token)
    halves = [_chip_sum(p, r, me_arr, "grad_chip_sum_" + n) for n, p, r in zip(sharded, pss, rcv)]
    others = _pair_swap(halves)
    ar_srcs, ar_got = _split_wait("allreduce_scatter_wait", ar_ss, ar_rs, ar_srcs, ar_got, _ar_scatter_copies(ar_rows),
                                  [others[0]])
    ar_red = _ar_sum(ar_srcs, ar_got, ar_rows)
    ag_ss, ag_rs, ar_red, ag_full, ag_token = _split_start(
        "allreduce_gather_start", ar_red, [jax.ShapeDtypeStruct(a.shape, F32) for a in ar_srcs],
        _ar_gather_copies(ar_rows), 7 * len(ar_red), [])
    G_out, D_out, M_out, V_out = {}, {}, {}, {}
    for n, g_own, g_other in zip(sharded, halves, others):
        g_, d_, m_, v_ = _adamw_sharded(W[n][0], g_own, g_other, M[n][0], V[n][0], c_arr, ag_token, "adamw_" + n)
        G_out[n], D_out[n], M_out[n], V_out[n] = g_[None], d_[None], m_[None], v_[None]
    ar_red, ag_full = _split_wait("allreduce_gather_wait", ag_ss, ag_rs, ar_red, ag_full, _ar_gather_copies(ar_rows),
                                  [D_out[n] for n in sharded])
    me8 = 2 * chip + lax.axis_index("c")
    r_vecs, r_a, r_bc = [lax.dynamic_update_slice_in_dim(f, r, me8 * rw, axis=f.ndim - 2)
                         for f, r, rw in zip(ag_full, ar_red, ar_rows)]
    r_vecs = r_vecs.reshape(1, -1)
    loss = r_vecs[0, loss_at]
    outs4 = _adamw_small(r_vecs, r_a, r_bc, [Wv[n] for n in small],
                         [view[n](M[n]) for n in small], [view[n](V[n]) for n in small])
    for store, o in zip((G_out, D_out, M_out, V_out), outs4):
        store.update({n: back[n](t) for n, t in zip(small, o)})

    return (loss, grad_x[None], *[G_out[n] for n in names], *[D_out[n] for n in names],
            *[M_out[n] for n in names], *[V_out[n] for n in names])
```

```python
import functools
import math

import jax
import jax.numpy as jnp
from jax import lax
from jax.experimental import pallas as pl
from jax.experimental.pallas import tpu as pltpu

F32 = jnp.float32
BF16 = jnp.bfloat16
HI = lax.Precision.HIGHEST
MESH = pl.DeviceIdType.MESH

EPS = 1e-6
S5_GROUP = 16
S5_STATE = 64
GLA_HK = 128
GLA_HV = 256
GLA_RANK = 16
GLA_TAU = 16.0
GLA_CHUNK = 64
GLA_STEP_CHUNKS = 4
LANES = 128
SUBLANES = 8
S5_COLS = 128
S5_LANES = (S5_COLS // S5_GROUP) * S5_STATE
S5_TIME_BLOCK = 1024

ADAM_LR = 0.001
ADAM_B1 = 0.9
ADAM_B2 = 0.999
ADAM_EPS = 1e-08
ADAM_WD = 0.01
ADAM_STEP = 10

GELU_K = math.sqrt(2.0 / math.pi)
GELU_C = 0.044715


def _blk(n, pref, unit=LANES):
    best = None
    b = unit
    while b <= min(n, pref):
        if n % b == 0:
            best = b
        b += unit
    return best if best is not None else n


def _dot(a, b, dn=(((1,), (0,)), ((), ()))):
    return lax.dot_general(a.astype(BF16), b.astype(BF16), dn, preferred_element_type=F32)


def _dot_hi(a, b, dn=(((1,), (0,)), ((), ()))):
    return lax.dot_general(a, b, dn, precision=HI, preferred_element_type=F32)


NN = (((1,), (0,)), ((), ()))
NT = (((1,), (1,)), ((), ()))
TN = (((0,), (0,)), ((), ()))


def _sigmoid(x):
    return 1.0 / (1.0 + jnp.exp(-x))


def _gelu(y):
    return 0.5 * y * (1.0 + jnp.tanh(GELU_K * (y + GELU_C * y * y * y)))


def _gelu_grad(y):
    th = jnp.tanh(GELU_K * (y + GELU_C * y * y * y))
    return 0.5 * (1.0 + th) + 0.5 * y * (1.0 - th * th) * GELU_K * (1.0 + 3.0 * GELU_C * y * y)


def _mm(a, b, *, name, ta=False, tb=False, out_dtype=F32, bm=1024, bn=1024, bk=2048, after=()):
    if ta:
        K, M = a.shape
    else:
        M, K = a.shape
    if tb:
        N, K2 = b.shape
    else:
        K2, N = b.shape
    assert K == K2, (a.shape, b.shape, ta, tb)
    bm, bn, bk = _blk(M, bm), _blk(N, bn), _blk(K, bk)
    nk = K // bk
    dn = (((0 if ta else 1,), (1 if tb else 0,)), ((), ()))

    def body(a_ref, b_ref, *rest):
        o_ref = rest[len(after)]
        if nk == 1:
            o_ref[...] = _dot(a_ref[...], b_ref[...], dn).astype(out_dtype)
            return
        acc_ref = rest[len(after) + 1]
        k = pl.program_id(2)

        @pl.when(k == 0)
        def _():
            acc_ref[...] = jnp.zeros_like(acc_ref)

        acc_ref[...] += _dot(a_ref[...], b_ref[...], dn)

        @pl.when(k == nk - 1)
        def _():
            o_ref[...] = acc_ref[...].astype(out_dtype)

    a_spec = pl.BlockSpec((bk, bm), lambda i, j, k: (k, i)) if ta else pl.BlockSpec((bm, bk), lambda i, j, k: (i, k))
    b_spec = pl.BlockSpec((bn, bk), lambda i, j, k: (j, k)) if tb else pl.BlockSpec((bk, bn), lambda i, j, k: (k, j))
    return pl.pallas_call(
        body,
        name=name,
        grid=(M // bm, N // bn, nk),
        in_specs=[a_spec, b_spec] + [pl.BlockSpec(memory_space=pl.ANY)] * len(after),
        out_specs=pl.BlockSpec((bm, bn), lambda i, j, k: (i, j)),
        out_shape=jax.ShapeDtypeStruct((M, N), out_dtype),
        scratch_shapes=[pltpu.VMEM((bm, bn), F32)] if nk > 1 else [],
        compiler_params=pltpu.CompilerParams(dimension_semantics=("parallel", "parallel", "arbitrary")),
    )(a, b, *after)


def _in_proj(h, w_main, w_low, after):
    M, K = h.shape
    N = w_main.shape[1]
    bm, bn = _blk(M, 1024), _blk(N, 1024)

    def body(h_ref, w_ref, wl_ref, _after_ref, o_ref, ol_ref):
        hv = h_ref[...]
        o_ref[...] = _dot(hv, w_ref[...])

        @pl.when(pl.program_id(1) == 0)
        def _():
            ol_ref[...] = _dot(hv, wl_ref[...])

    return pl.pallas_call(
        body, name="in_proj", grid=(M // bm, N // bn),
        in_specs=[pl.BlockSpec((bm, K), lambda i, j: (i, 0)), pl.BlockSpec((K, bn), lambda i, j: (0, j)),
                  pl.BlockSpec((K, LANES), lambda i, j: (0, 0)), pl.BlockSpec(memory_space=pl.ANY)],
        out_specs=[pl.BlockSpec((bm, bn), lambda i, j: (i, j)), pl.BlockSpec((bm, LANES), lambda i, j: (i, 0))],
        out_shape=[jax.ShapeDtypeStruct((M, N), F32), jax.ShapeDtypeStruct((M, LANES), F32)],
        compiler_params=pltpu.CompilerParams(dimension_semantics=("parallel", "arbitrary")),
    )(h, w_main, w_low, after)


def _in_proj_dx(a1, a2, al, b, bl, after, *, bm=1024, bn=1024, bk=2048):
    M, K1 = a1.shape
    K2 = a2.shape[1]
    N = b.shape[0]
    bm, bn = _blk(M, bm), _blk(N, bn)
    bk = _blk(math.gcd(K1, K2), bk)
    nk1, nk = K1 // bk, (K1 + K2) // bk

    def body(a1_ref, a2_ref, al_ref, b_ref, bl_ref, _after_ref, o_ref, acc_ref):
        k = pl.program_id(2)

        @pl.when(k == 0)
        def _():
            acc_ref[...] = _dot(al_ref[...], bl_ref[...], NT)

        @pl.when(k < nk1)
        def _():
            acc_ref[...] += _dot(a1_ref[...], b_ref[...], NT)

        @pl.when(k >= nk1)
        def _():
            acc_ref[...] += _dot(a2_ref[...], b_ref[...], NT)

        @pl.when(k == nk - 1)
        def _():
            o_ref[...] = acc_ref[...]

    return pl.pallas_call(
        body, name="in_proj_dx", grid=(M // bm, N // bn, nk),
        in_specs=[pl.BlockSpec((bm, bk), lambda i, j, k: (i, jnp.minimum(k, nk1 - 1))),
                  pl.BlockSpec((bm, bk), lambda i, j, k: (i, jnp.maximum(k - nk1, 0))),
                  pl.BlockSpec((bm, LANES), lambda i, j, k: (i, 0)),
                  pl.BlockSpec((bn, bk), lambda i, j, k: (j, k)),
                  pl.BlockSpec((bn, LANES), lambda i, j, k: (j, 0)),
                  pl.BlockSpec(memory_space=pl.ANY)],
        out_specs=pl.BlockSpec((bm, bn), lambda i, j, k: (i, j)),
        out_shape=jax.ShapeDtypeStruct((M, N), F32),
        scratch_shapes=[pltpu.VMEM((bm, bn), F32)],
        compiler_params=pltpu.CompilerParams(dimension_semantics=("parallel", "parallel", "arbitrary")),
    )(a1, a2, al, b, bl, after)


def _in_proj_dw(a, b1, b2, bl, *, bm=1024, bn=1024, bk=2048):
    K, M = a.shape
    N1, N2 = b1.shape[1], b2.shape[1]
    bm, bk = _blk(M, bm), _blk(K, bk)
    bn = _blk(math.gcd(N1, N2), bn)
    nj1, nj = N1 // bn, (N1 + N2) // bn
    nk = K // bk

    def body(a_ref, b1_ref, b2_ref, bl_ref, o_ref, ol_ref, acc_ref, accl_ref):
        j = pl.program_id(1)
        k = pl.program_id(2)

        @pl.when(k == 0)
        def _():
            acc_ref[...] = jnp.zeros_like(acc_ref)

        @pl.when(j < nj1)
        def _():
            acc_ref[...] += _dot(a_ref[...], b1_ref[...], TN)

        @pl.when(j >= nj1)
        def _():
            acc_ref[...] += _dot(a_ref[...], b2_ref[...], TN)

        @pl.when(k == nk - 1)
        def _():
            o_ref[...] = acc_ref[...].astype(BF16)

        @pl.when(j == 0)
        def _():
            low = _dot(a_ref[...], bl_ref[...], TN)

            @pl.when(k == 0)
            def _():
                accl_ref[...] = low

            @pl.when(k > 0)
            def _():
                accl_ref[...] += low

            @pl.when(k == nk - 1)
            def _():
                ol_ref[...] = accl_ref[...].astype(BF16)

    return pl.pallas_call(
        body, name="in_proj_dw", grid=(M // bm, nj, nk),
        in_specs=[pl.BlockSpec((bk, bm), lambda i, j, k: (k, i)),
                  pl.BlockSpec((bk, bn), lambda i, j, k: (jnp.where(j < nj1, k, nk - 1), jnp.minimum(j, nj1 - 1))),
                  pl.BlockSpec((bk, bn), lambda i, j, k: (jnp.where(j >= nj1, k, 0), jnp.maximum(j - nj1, 0))),
                  pl.BlockSpec((bk, LANES), lambda i, j, k: (jnp.where(j == 0, k, nk - 1), 0))],
        out_specs=[pl.BlockSpec((bm, bn), lambda i, j, k: (i, j)), pl.BlockSpec((bm, LANES), lambda i, j, k: (i, 0))],
        out_shape=[jax.ShapeDtypeStruct((M, N1 + N2), BF16), jax.ShapeDtypeStruct((M, LANES), BF16)],
        scratch_shapes=[pltpu.VMEM((bm, bn), F32), pltpu.VMEM((bm, LANES), F32)],
        compiler_params=pltpu.CompilerParams(dimension_semantics=("parallel", "arbitrary", "arbitrary")),
    )(a, b1, b2, bl)


def _assemble_w_in(g, nsh, wm):
    _, R, nshp = g.shape
    nb_in = nshp // LANES
    nb_main = wm // LANES
    tr = _blk(R, 512, 2 * SUBLANES)
    plan = []
    for b in range(nb_main + 1):
        terms = []
        for k in range(g.shape[0]):
            for i in range(nb_in):
                delta = nsh * k + LANES * i - LANES * b
                lo, hi = max(0, -delta), min(LANES, LANES - delta, nsh - LANES * i)
                if abs(delta) < LANES and hi > lo:
                    terms.append((k, i, delta))
        plan.append(terms)
    deltas = sorted({d for terms in plan for _, _, d in terms if d})

    def body(g_ref, wm_ref, wl_ref):
        src = _iota2((LANES, LANES), 0)
        dst = _iota2((LANES, LANES), 1)
        shift = {d: (dst - src == d).astype(BF16) for d in deltas}
        for b, terms in enumerate(plan):
            acc = None
            for k, i, d in terms:
                blk = g_ref[k, :, LANES * i:LANES * (i + 1)]
                t = _dot(blk, shift[d]) if d else blk.astype(F32)
                acc = t if acc is None else acc + t
            if b < nb_main:
                wm_ref[:, LANES * b:LANES * (b + 1)] = acc.astype(BF16)
            else:
                wl_ref[...] = acc.astype(BF16)

    return pl.pallas_call(
        body, name="assemble_w_in", grid=(R // tr,),
        in_specs=[pl.BlockSpec((g.shape[0], tr, nshp), lambda r: (0, r, 0))],
        out_specs=[pl.BlockSpec((tr, wm), lambda r: (r, 0)), pl.BlockSpec((tr, LANES), lambda r: (r, 0))],
        out_shape=[jax.ShapeDtypeStruct((R, wm), BF16), jax.ShapeDtypeStruct((R, LANES), BF16)],
        compiler_params=pltpu.CompilerParams(dimension_semantics=("parallel",)),
    )(g)


def _split_w_in_grad(g_main, g_low, nsh):
    R, wm = g_main.shape
    nb_main = wm // LANES
    nb_out = -(-nsh // LANES)
    tr = _blk(R, 512, 2 * SUBLANES)
    plan = {}
    for k in range(4):
        for i in range(nb_out):
            width = min(LANES, nsh - LANES * i)
            terms = []
            for b in range(nb_main + 1):
                delta = LANES * b - (nsh * k + LANES * i)
                lo, hi = max(0, delta), min(width, LANES + delta)
                if abs(delta) < LANES and hi > lo:
                    terms.append((b, delta))
            plan[k, i] = (width, terms)
    deltas = sorted({d for _, terms in plan.values() for _, d in terms if d})

    def body(gm_ref, gl_ref, o_ref):
        src = _iota2((LANES, LANES), 0)
        dst = _iota2((LANES, LANES), 1)
        shift = {d: (dst - src == d).astype(BF16) for d in deltas}
        for (k, i), (width, terms) in plan.items():
            acc = None
            for b, d in terms:
                blk = gm_ref[:, LANES * b:LANES * (b + 1)] if b < nb_main else gl_ref[...]
                t = _dot(blk, shift[d]) if d else blk.astype(F32)
                acc = t if acc is None else acc + t
            o_ref[k, :, LANES * i:LANES * i + width] = acc[:, :width].astype(BF16)

    return pl.pallas_call(
        body, name="split_w_in_grad", grid=(R // tr,),
        in_specs=[pl.BlockSpec((tr, wm), lambda r: (r, 0)), pl.BlockSpec((tr, LANES), lambda r: (r, 0))],
        out_specs=pl.BlockSpec((4, tr, nsh), lambda r: (0, r, 0)),
        out_shape=jax.ShapeDtypeStruct((4, R, nsh), BF16),
        compiler_params=pltpu.CompilerParams(dimension_semantics=("parallel",)),
    )(g_main, g_low)


def _prenorm_fwd(x, w, after):
    L, D = x.shape
    tr = _blk(L, 256, SUBLANES)

    def body(x_ref, w_ref, _after_ref, h_ref):
        xv = x_ref[...]
        r = lax.rsqrt(jnp.mean(xv * xv, axis=-1, keepdims=True) + EPS)
        h_ref[...] = (xv * r * w_ref[...]).astype(BF16)

    return pl.pallas_call(
        body, name="prenorm_fwd", grid=(L // tr,),
        in_specs=[pl.BlockSpec((tr, D), lambda i: (i, 0)), pl.BlockSpec((1, D), lambda i: (0, 0)),
                  pl.BlockSpec(memory_space=pl.ANY)],
        out_specs=pl.BlockSpec((tr, D), lambda i: (i, 0)),
        out_shape=jax.ShapeDtypeStruct((L, D), BF16),
        compiler_params=pltpu.CompilerParams(dimension_semantics=("parallel",)),
    )(x, w, after)


def _post_fwd_bwd(mixed, x, target, w):
    L, D = x.shape
    tr = _blk(L, 256, SUBLANES)
    nsteps = L // tr

    def body(mx_ref, x_ref, t_ref, w_ref, loss_ref, dm_ref, dout_ref, gw_ref, acc_ref):
        i = pl.program_id(0)

        @pl.when(i == 0)
        def _():
            acc_ref[...] = jnp.zeros_like(acc_ref)
            gw_ref[...] = jnp.zeros_like(gw_ref)

        mx = mx_ref[...]
        wv = w_ref[...]
        r = lax.rsqrt(jnp.mean(mx * mx, axis=-1, keepdims=True) + EPS)
        n = mx * r
        err = x_ref[...] + n * wv - t_ref[...]
        acc_ref[...] += jnp.sum(err * err, axis=0, keepdims=True)
        dout = err * (1.0 / D)
        dout_ref[...] = dout
        gw_ref[...] += jnp.sum(dout * n, axis=0, keepdims=True)
        dn = dout * wv
        dm_ref[...] = (r * (dn - n * jnp.mean(dn * n, axis=-1, keepdims=True))).astype(BF16)

        @pl.when(i == nsteps - 1)
        def _():
            loss_ref[...] = jnp.sum(acc_ref[...], axis=-1, keepdims=True) * (0.5 / D)

    row = pl.BlockSpec((tr, D), lambda i: (i, 0))
    vec = pl.BlockSpec((1, D), lambda i: (0, 0))
    return pl.pallas_call(
        body, name="post_fwd_bwd", grid=(nsteps,),
        in_specs=[row, row, row, vec],
        out_specs=[pl.BlockSpec((1, 1), lambda i: (0, 0)), row, row, vec],
        out_shape=[jax.ShapeDtypeStruct((1, 1), F32), jax.ShapeDtypeStruct((L, D), BF16),
                   jax.ShapeDtypeStruct((L, D), F32), jax.ShapeDtypeStruct((1, D), F32)],
        scratch_shapes=[pltpu.VMEM((1, D), F32)],
        compiler_params=pltpu.CompilerParams(dimension_semantics=("arbitrary",)),
    )(mixed, x, target, w)


def _prenorm_bwd(x, dh, dout, w):
    L, D = x.shape
    tr = _blk(L, 256, SUBLANES)

    def body(x_ref, a_ref, dout_ref, w_ref, gx_ref, gw_ref):
        i = pl.program_id(0)

        @pl.when(i == 0)
        def _():
            gw_ref[...] = jnp.zeros_like(gw_ref)

        xv = x_ref[...]
        r = lax.rsqrt(jnp.mean(xv * xv, axis=-1, keepdims=True) + EPS)
        n = xv * r
        dh = a_ref[...]
        gw_ref[...] += jnp.sum(dh * n, axis=0, keepdims=True)
        dn = dh * w_ref[...]
        gx_ref[...] = dout_ref[...] + r * (dn - n * jnp.mean(dn * n, axis=-1, keepdims=True))

    row = pl.BlockSpec((tr, D), lambda i: (i, 0))
    vec = pl.BlockSpec((1, D), lambda i: (0, 0))
    return pl.pallas_call(
        body, name="prenorm_bwd", grid=(L // tr,),
        in_specs=[row, row, row, vec],
        out_specs=[row, vec],
        out_shape=[jax.ShapeDtypeStruct((L, D), F32), jax.ShapeDtypeStruct((1, D), F32)],
        compiler_params=pltpu.CompilerParams(dimension_semantics=("arbitrary",)),
    )(x, dh, dout, w)


def _s5_disc(a_re_raw, a_im, dt):
    a_re = jnp.minimum(a_re_raw, -1e-4)
    mag = jnp.exp(a_re * dt)
    ph = a_im * dt
    ab_re = mag * jnp.cos(ph)
    ab_im = mag * jnp.sin(ph)
    inv_n = 1.0 / (a_re * a_re + a_im * a_im)
    ia_re = a_re * inv_n
    ia_im = -a_im * inv_n
    n_re = ab_re - 1.0
    f_re = n_re * ia_re - ab_im * ia_im
    f_im = n_re * ia_im + ab_im * ia_re
    return a_re, ab_re, ab_im, f_re, f_im, ia_re, ia_im


def _iota2(shape, dim):
    return lax.broadcasted_iota(jnp.int32, shape, dim)


def _group_mask(rows, rows_per_group):
    shift = rows_per_group.bit_length() - 1
    return (_iota2((rows, S5_LANES), 0) >> shift) == (_iota2((rows, S5_LANES), 1) >> (S5_STATE.bit_length() - 1))


def _lane_tiler(dtype):
    return ((_iota2((S5_STATE, S5_LANES), 1) & (S5_STATE - 1)) == _iota2((S5_STATE, S5_LANES), 0)).astype(dtype)


def _row_to_col(row, n):
    eye = (_iota2((n, n), 0) == _iota2((n, n), 1)).astype(F32)
    return jnp.sum(eye * row, axis=1, keepdims=True)


def _group_repeat(G):
    return ((_iota2((G * S5_GROUP, G), 0) >> (S5_GROUP.bit_length() - 1)) == _iota2((G * S5_GROUP, G), 1)).astype(F32)


S5_TABS = 18


def _s5_prep_fwd(a_re, a_im, log_dt, b_re, b_im, c_re, c_im, after, seg):
    G, P = a_re.shape
    nb = G * S5_GROUP // S5_COLS
    g8 = S5_COLS // S5_GROUP
    assert seg & (seg - 1) == 0, seg

    def body(are_ref, aim_ref, ldt_ref, bre_ref, bim_ref, cre_ref, cim_ref, _after_ref,
             bbre_ref, bbim_ref, ctre_ref, ctim_ref, tab_ref, pt_ref):
        dt = jnp.exp(_row_to_col(ldt_ref[...], G))
        _, ab_re, ab_im, f_re, f_im, _, _ = _s5_disc(are_ref[...], aim_ref[...], dt)
        rep = _group_repeat(G)
        fx_re = _dot_hi(rep, f_re)
        fx_im = _dot_hi(rep, f_im)
        br, bi = bre_ref[...], bim_ref[...]
        bb_re = fx_re * br - fx_im * bi
        bb_im = fx_re * bi + fx_im * br
        tile_bf = _lane_tiler(BF16)
        mask = _group_mask(S5_COLS, S5_GROUP)
        for jb in range(nb):
            rs = slice(jb * S5_COLS, (jb + 1) * S5_COLS)
            for src, dst in ((bb_re[rs], bbre_ref), (bb_im[rs], bbim_ref), (cre_ref[rs, :], ctre_ref), (cim_ref[rs, :], ctim_ref)):
                dst[jb] = jnp.where(mask, _dot(src, tile_bf), 0.0).astype(BF16)

        tile_f = _lane_tiler(F32)
        mask8 = _group_mask(g8, 1)
        row = _iota2((SUBLANES, S5_LANES), 0)
        slab = (SUBLANES, S5_LANES)
        cmul = lambda p, q: (p[0] * q[0] - p[1] * q[1], p[0] * q[1] + p[1] * q[0])
        for jb in range(nb):
            gs = slice(jb * g8, (jb + 1) * g8)

            def lanes(m):
                v = jnp.sum(jnp.where(mask8, _dot_hi(m[gs], tile_f), 0.0), axis=0, keepdims=True)
                return jnp.broadcast_to(v, slab)

            a1 = (lanes(ab_re), lanes(ab_im))
            tab_ref[jb, 0], tab_ref[jb, 1] = a1

            def powers(i, p):
                off = pl.multiple_of(i * SUBLANES, SUBLANES)
                pt_ref[jb, 0, pl.ds(off, SUBLANES), :] = p[0]
                pt_ref[jb, 1, pl.ds(off, SUBLANES), :] = p[1]
                return cmul(p, a1)

            lax.fori_loop(0, seg, powers, a1)
            aseg = a1
            for _ in range(seg.bit_length() - 1):
                aseg = cmul(aseg, aseg)
            pw = [aseg]
            for _ in range(1, SUBLANES):
                pw.append(cmul(pw[-1], aseg))
            for lvl, k in enumerate((1, 2, 4)):
                tab_ref[jb, 2 + 2 * lvl] = jnp.where(row >= k, pw[k - 1][0], 0.0)
                tab_ref[jb, 3 + 2 * lvl] = jnp.where(row >= k, pw[k - 1][1], 0.0)
                tab_ref[jb, 10 + 2 * lvl] = jnp.where(row < SUBLANES - k, pw[k - 1][0], 0.0)
                tab_ref[jb, 11 + 2 * lvl] = jnp.where(row < SUBLANES - k, -pw[k - 1][1], 0.0)
            f_r = f_i = r_r = r_i = jnp.zeros(slab, F32)
            for i in range(SUBLANES):
                f_r = jnp.where(row == i, pw[i][0], f_r)
                f_i = jnp.where(row == i, pw[i][1], f_i)
                r_r = jnp.where(row == i, pw[SUBLANES - 1 - i][0], r_r)
                r_i = jnp.where(row == i, -pw[SUBLANES - 1 - i][1], r_i)
            tab_ref[jb, 8] = f_r
            tab_ref[jb, 9] = f_i
            tab_ref[jb, 16] = r_r
            tab_ref[jb, 17] = r_i

    vm = pl.BlockSpec(memory_space=pltpu.VMEM)
    bd = jax.ShapeDtypeStruct((nb, S5_COLS, S5_LANES), BF16)
    return pl.pallas_call(
        body, name="s5_prep_fwd",
        in_specs=[vm] * 7 + [pl.BlockSpec(memory_space=pl.ANY)], out_specs=[vm] * 6,
        out_shape=[bd, bd, bd, bd, jax.ShapeDtypeStruct((nb, S5_TABS, SUBLANES, S5_LANES), F32),
                   jax.ShapeDtypeStruct((nb, 2, seg * SUBLANES, S5_LANES), F32)],
    )(a_re, a_im, log_dt, b_re, b_im, c_re, c_im, after)


def _s5_prep_bwd(a_re, a_im, log_dt, b_re, b_im, gbb_re, gbb_im, gct_re, gct_im, gab_re, gab_im):
    G, P = a_re.shape
    nb = G * S5_GROUP // S5_COLS
    g8 = S5_COLS // S5_GROUP

    def body(are_ref, aim_ref, ldt_ref, bre_ref, bim_ref, gbr_ref, gbi_ref, gcr_ref, gci_ref, gar_ref, gai_ref,
             o_a, o_bc, o_ldt):
        dt = jnp.exp(_row_to_col(ldt_ref[...], G))
        a_raw = are_ref[...]
        a_imv = aim_ref[...]
        a_re_c, ab_re, ab_im, f_re, f_im, ia_re, ia_im = _s5_disc(a_raw, a_imv, dt)
        tile_f = _lane_tiler(F32)
        mask = _group_mask(S5_COLS, S5_GROUP)
        mask8 = _group_mask(g8, 1)
        for jb in range(nb):
            rs = slice(jb * S5_COLS, (jb + 1) * S5_COLS)
            gs = slice(jb * g8, (jb + 1) * g8)
            ls = slice(jb * S5_LANES, (jb + 1) * S5_LANES)
            for k, src in enumerate((gbr_ref, gbi_ref, gcr_ref, gci_ref)):
                o_bc[k, rs, :] = _dot_hi(jnp.where(mask, src[jb], 0.0), tile_f, NT)
            for k, src in enumerate((gar_ref, gai_ref)):
                o_a[k, gs, :] = _dot_hi(jnp.where(mask8, src[:, ls], 0.0), tile_f, NT)
        rep = _group_repeat(G)
        fx_re = _dot_hi(rep, f_re)
        fx_im = _dot_hi(rep, f_im)
        gbr, gbi = o_bc[0], o_bc[1]
        br, bi = bre_ref[...], bim_ref[...]
        o_bc[0] = fx_re * gbr + fx_im * gbi
        o_bc[1] = fx_re * gbi - fx_im * gbr
        gf_re = _dot_hi(rep, br * gbr + bi * gbi, TN)
        gf_im = _dot_hi(rep, br * gbi - bi * gbr, TN)
        gab_r = o_a[0] + ia_re * gf_re + ia_im * gf_im
        gab_i = o_a[1] + ia_re * gf_im - ia_im * gf_re
        q_re = f_re * ia_re - f_im * ia_im
        q_im = f_re * ia_im + f_im * ia_re
        ga_re = -(q_re * gf_re + q_im * gf_im)
        ga_im = -(q_re * gf_im - q_im * gf_re)
        gth_re = ab_re * gab_r + ab_im * gab_i
        gth_im = ab_re * gab_i - ab_im * gab_r
        ga_re = ga_re + dt * gth_re
        ga_im = ga_im + dt * gth_im
        gdt = jnp.sum(a_re_c * gth_re + a_imv * gth_im, axis=-1, keepdims=True)
        eye = (_iota2((G, G), 0) == _iota2((G, G), 1)).astype(F32)
        o_ldt[...] = jnp.sum(eye * (gdt * dt), axis=0, keepdims=True)
        slope = jnp.where(a_raw < -1e-4, 1.0, jnp.where(a_raw == -1e-4, 0.5, 0.0))
        o_a[0] = ga_re * slope
        o_a[1] = ga_im

    vm = pl.BlockSpec(memory_space=pltpu.VMEM)
    return pl.pallas_call(
        body, name="s5_prep_bwd",
        in_specs=[vm] * 11, out_specs=[vm] * 3,
        out_shape=[jax.ShapeDtypeStruct((2, G, P), F32), jax.ShapeDtypeStruct((4, G * S5_GROUP, P), F32),
                   jax.ShapeDtypeStruct((1, G), F32)],
    )(a_re, a_im, log_dt, b_re, b_im, gbb_re, gbb_im, gct_re, gct_im, gab_re, gab_im)


def _scan8(xr, xi, tab_ref, base, shifts):
    for lvl, sh in enumerate(shifts):
        mr = tab_ref[0, base + 2 * lvl]
        mi = tab_ref[0, base + 2 * lvl + 1]
        ar = pltpu.roll(xr, sh, 0)
        ai = pltpu.roll(xi, sh, 0)
        xr, xi = xr + mr * ar - mi * ai, xi + mr * ai + mi * ar
    return xr, xi


def _to_segments(src_ref, dst_ref, seg):
    for i in range(seg):
        dst_ref[i * SUBLANES:(i + 1) * SUBLANES, :] = src_ref[pl.ds(i, SUBLANES, stride=seg), :]


def _from_segments(src_ref, dst_ref, seg):
    for i in range(seg):
        dst_ref[pl.ds(i, SUBLANES, stride=seg), :] = src_ref[i * SUBLANES:(i + 1) * SUBLANES, :]


def _slab(i):
    return pl.ds(pl.multiple_of(i * SUBLANES, SUBLANES), SUBLANES)


def _s5_scan_fwd(proj_main, bbd_re, bbd_im, cbd_re, cbd_im, dvec, tab, ptab, DS):
    L = proj_main.shape[0]
    nb = DS // S5_COLS
    tb = _blk(L, S5_TIME_BLOCK, SUBLANES)
    nt = L // tb
    seg = tb // SUBLANES

    def body(u_ref, bre_ref, bim_ref, cre_ref, cim_ref, d_ref, tab_ref, pt_ref, y_ref, sre_ref, sim_ref,
             up_ref, yp_ref, car_ref):
        t = pl.program_id(1)

        @pl.when(t == 0)
        def _():
            car_ref[...] = jnp.zeros_like(car_ref)

        _to_segments(u_ref, up_ref, seg)
        up = up_ref[...]
        sre_ref[...] = _dot(up, bre_ref[0])
        sim_ref[...] = _dot(up, bim_ref[0])
        ar, ai = tab_ref[0, 0], tab_ref[0, 1]

        def pass1(i, x):
            xr = ar * x[0] - ai * x[1] + sre_ref[_slab(i), :]
            xi = ar * x[1] + ai * x[0] + sim_ref[_slab(i), :]
            sre_ref[_slab(i), :] = xr
            sim_ref[_slab(i), :] = xi
            return xr, xi

        zero = jnp.zeros((SUBLANES, S5_LANES), F32)
        er, ei = lax.fori_loop(0, seg, pass1, (zero, zero))
        cin_r, cin_i = car_ref[0], car_ref[1]
        sr, si = _scan8(er, ei, tab_ref, 2, (1, 2, 4))
        pr, pi = tab_ref[0, 8], tab_ref[0, 9]
        sr, si = sr + pr * cin_r - pi * cin_i, si + pr * cin_i + pi * cin_r
        row0 = _iota2((SUBLANES, S5_LANES), 0) == 0
        cr = jnp.where(row0, cin_r, pltpu.roll(sr, 1, 0))
        ci = jnp.where(row0, cin_i, pltpu.roll(si, 1, 0))
        car_ref[0] = jnp.broadcast_to(sr[SUBLANES - 1:SUBLANES, :], sr.shape)
        car_ref[1] = jnp.broadcast_to(si[SUBLANES - 1:SUBLANES, :], si.shape)

        def pass2(i, _):
            qr, qi = pt_ref[0, 0, _slab(i), :], pt_ref[0, 1, _slab(i), :]
            sre_ref[_slab(i), :] += qr * cr - qi * ci
            sim_ref[_slab(i), :] += qr * ci + qi * cr
            return 0

        lax.fori_loop(0, seg, pass2, 0, unroll=4)
        yp_ref[...] = _dot(sre_ref[...], cre_ref[0], NT) - _dot(sim_ref[...], cim_ref[0], NT) + d_ref[...] * up
        _from_segments(yp_ref, y_ref, seg)

    return pl.pallas_call(
        body, name="s5_scan_fwd", grid=(nb, nt),
        in_specs=[
            pl.BlockSpec((tb, S5_COLS), lambda j, t: (t, j)),
            pl.BlockSpec((1, S5_COLS, S5_LANES), lambda j, t: (j, 0, 0)),
            pl.BlockSpec((1, S5_COLS, S5_LANES), lambda j, t: (j, 0, 0)),
            pl.BlockSpec((1, S5_COLS, S5_LANES), lambda j, t: (j, 0, 0)),
            pl.BlockSpec((1, S5_COLS, S5_LANES), lambda j, t: (j, 0, 0)),
            pl.BlockSpec((1, S5_COLS), lambda j, t: (0, j)),
            pl.BlockSpec((1, S5_TABS, SUBLANES, S5_LANES), lambda j, t: (j, 0, 0, 0)),
            pl.BlockSpec((1, 2, tb, S5_LANES), lambda j, t: (j, 0, 0, 0)),
        ],
        out_specs=[
            pl.BlockSpec((tb, S5_COLS), lambda j, t: (t, j)),
            pl.BlockSpec((tb, S5_LANES), lambda j, t: (t, j)),
            pl.BlockSpec((tb, S5_LANES), lambda j, t: (t, j)),
        ],
        out_shape=[jax.ShapeDtypeStruct((L, DS), F32),
                   jax.ShapeDtypeStruct((L, nb * S5_LANES), F32),
                   jax.ShapeDtypeStruct((L, nb * S5_LANES), F32)],
        scratch_shapes=[pltpu.VMEM((tb, S5_COLS), F32), pltpu.VMEM((tb, S5_COLS), F32),
                        pltpu.VMEM((2, SUBLANES, S5_LANES), F32)],
        compiler_params=pltpu.CompilerParams(dimension_semantics=("parallel", "arbitrary")),
    )(proj_main, bbd_re, bbd_im, cbd_re, cbd_im, dvec, tab, ptab)


def _s5_scan_bwd(dy, proj_main, s_re, s_im, bbd_re, bbd_im, cbd_re, cbd_im, dvec, tab, ptab, d_s5, DS):
    L = proj_main.shape[0]
    nb = DS // S5_COLS
    tb = _blk(L, S5_TIME_BLOCK, SUBLANES)
    nt = L // tb
    seg = tb // SUBLANES
    tb8 = tb // SUBLANES

    def body(dy_ref, u_ref, sre_ref, sim_ref, pre_ref, pim_ref, bre_ref, bim_ref, cre_ref, cim_ref, d_ref, tab_ref, pt_ref,
             _ds5_ref, du_ref, gd_ref, gcre_ref, gcim_ref, gbre_ref, gbim_ref, gare_ref, gaim_ref,
             lre_ref, lim_ref, up_ref, dyp_ref, dup_ref, duo_ref, car_ref):
        t = pl.program_id(1)

        @pl.when(t == 0)
        def _():
            car_ref[...] = jnp.zeros_like(car_ref)
            gd_ref[...] = jnp.zeros_like(gd_ref)
            gcre_ref[...] = jnp.zeros_like(gcre_ref)
            gcim_ref[...] = jnp.zeros_like(gcim_ref)
            gbre_ref[...] = jnp.zeros_like(gbre_ref)
            gbim_ref[...] = jnp.zeros_like(gbim_ref)
            gare_ref[...] = jnp.zeros_like(gare_ref)
            gaim_ref[...] = jnp.zeros_like(gaim_ref)

        _to_segments(dy_ref, dyp_ref, seg)
        _to_segments(u_ref, up_ref, seg)
        dyv = dyp_ref[...]
        u = up_ref[...]
        gd_ref[...] += jnp.sum(dyv * u, axis=0, keepdims=True)
        lre_ref[...] = _dot(dyv, cre_ref[0])
        lim_ref[...] = -_dot(dyv, cim_ref[0])
        gcre_ref[0] += _dot(dyv, sre_ref[...], TN)
        gcim_ref[0] -= _dot(dyv, sim_ref[...], TN)
        ar, ai = tab_ref[0, 0], -tab_ref[0, 1]

        def pass1(k, x):
            i = seg - 1 - k
            xr = ar * x[0] - ai * x[1] + lre_ref[_slab(i), :]
            xi = ar * x[1] + ai * x[0] + lim_ref[_slab(i), :]
            lre_ref[_slab(i), :] = xr
            lim_ref[_slab(i), :] = xi
            return xr, xi

        zero = jnp.zeros((SUBLANES, S5_LANES), F32)
        er, ei = lax.fori_loop(0, seg, pass1, (zero, zero))
        cin_r, cin_i = car_ref[0], car_ref[1]
        lr, li = _scan8(er, ei, tab_ref, 10, (7, 6, 4))
        pr, pi = tab_ref[0, 16], tab_ref[0, 17]
        lr, li = lr + pr * cin_r - pi * cin_i, li + pr * cin_i + pi * cin_r
        rows = _iota2((SUBLANES, S5_LANES), 0)
        cr = jnp.where(rows == SUBLANES - 1, cin_r, pltpu.roll(lr, SUBLANES - 1, 0))
        ci = jnp.where(rows == SUBLANES - 1, cin_i, pltpu.roll(li, SUBLANES - 1, 0))
        car_ref[0] = jnp.broadcast_to(lr[0:1, :], lr.shape)
        car_ref[1] = jnp.broadcast_to(li[0:1, :], li.shape)

        first = (t == nt - 1).astype(F32)
        head_re = jnp.broadcast_to(pre_ref[SUBLANES - 1:SUBLANES, :], zero.shape) * (1.0 - first)
        head_im = jnp.broadcast_to(pim_ref[SUBLANES - 1:SUBLANES, :], zero.shape) * (1.0 - first)
        last = _slab(seg - 1)
        sp0_re = jnp.where(rows == 0, head_re, pltpu.roll(sre_ref[last, :], 1, 0))
        sp0_im = jnp.where(rows == 0, head_im, pltpu.roll(sim_ref[last, :], 1, 0))

        def fix(i, acc, sp_re, sp_im):
            j = seg - 1 - i
            qr, qi = pt_ref[0, 0, _slab(j), :], -pt_ref[0, 1, _slab(j), :]
            xr = lre_ref[_slab(i), :] + qr * cr - qi * ci
            xi = lim_ref[_slab(i), :] + qr * ci + qi * cr
            lre_ref[_slab(i), :] = xr
            lim_ref[_slab(i), :] = xi
            return acc[0] + sp_re * xr + sp_im * xi, acc[1] + sp_re * xi - sp_im * xr

        def pass2(i, acc):
            prev = _slab(jnp.maximum(i - 1, 0))
            return fix(i, acc, sre_ref[prev, :], sim_ref[prev, :])

        acc_re, acc_im = lax.fori_loop(0, seg, pass2, (zero, zero), unroll=4)
        first_slab = _slab(0)
        d_re, d_im = sp0_re - sre_ref[first_slab, :], sp0_im - sim_ref[first_slab, :]
        x0r, x0i = lre_ref[first_slab, :], lim_ref[first_slab, :]
        acc_re = acc_re + d_re * x0r + d_im * x0i
        acc_im = acc_im + d_re * x0i - d_im * x0r
        gare_ref[...] += jnp.sum(acc_re, axis=0, keepdims=True)
        gaim_ref[...] += jnp.sum(acc_im, axis=0, keepdims=True)
        lre = lre_ref[...]
        lim = lim_ref[...]
        dup_ref[...] = dyv * d_ref[...] + _dot(lre, bre_ref[0], NT) + _dot(lim, bim_ref[0], NT)
        _from_segments(dup_ref, duo_ref, seg)
        du_ref[...] = duo_ref[...].astype(BF16)
        gbre_ref[0] += _dot(u, lre, TN)
        gbim_ref[0] += _dot(u, lim, TN)

    rt = lambda t: nt - 1 - t
    col = pl.BlockSpec((tb, S5_COLS), lambda j, t: (rt(t), j))
    st = pl.BlockSpec((tb, S5_LANES), lambda j, t: (rt(t), j))
    prev = pl.BlockSpec((SUBLANES, S5_LANES), lambda j, t: (jnp.maximum(rt(t) * tb8 - 1, 0), j))
    bmat = pl.BlockSpec((1, S5_COLS, S5_LANES), lambda j, t: (j, 0, 0))
    cmat = bmat
    return pl.pallas_call(
        body, name="s5_scan_bwd", grid=(nb, nt),
        in_specs=[col, col, st, st, prev, prev, bmat, bmat, cmat, cmat,
                  pl.BlockSpec((1, S5_COLS), lambda j, t: (0, j)),
                  pl.BlockSpec((1, S5_TABS, SUBLANES, S5_LANES), lambda j, t: (j, 0, 0, 0)),
                  pl.BlockSpec((1, 2, tb, S5_LANES), lambda j, t: (j, 0, 0, 0)),
                  pl.BlockSpec(memory_space=pl.ANY)],
        out_specs=[col, pl.BlockSpec((1, S5_COLS), lambda j, t: (0, j)), cmat, cmat, bmat, bmat,
                   pl.BlockSpec((1, S5_LANES), lambda j, t: (0, j)), pl.BlockSpec((1, S5_LANES), lambda j, t: (0, j))],
        input_output_aliases={13: 0},
        out_shape=[jax.ShapeDtypeStruct((L, 2 * DS), BF16), jax.ShapeDtypeStruct((1, DS), F32),
                   jax.ShapeDtypeStruct((nb, S5_COLS, S5_LANES), F32), jax.ShapeDtypeStruct((nb, S5_COLS, S5_LANES), F32),
                   jax.ShapeDtypeStruct((nb, S5_COLS, S5_LANES), F32), jax.ShapeDtypeStruct((nb, S5_COLS, S5_LANES), F32),
                   jax.ShapeDtypeStruct((1, nb * S5_LANES), F32), jax.ShapeDtypeStruct((1, nb * S5_LANES), F32)],
        scratch_shapes=[pltpu.VMEM((tb, S5_LANES), F32), pltpu.VMEM((tb, S5_LANES), F32)]
        + [pltpu.VMEM((tb, S5_COLS), F32)] * 4 + [pltpu.VMEM((2, SUBLANES, S5_LANES), F32)],
        compiler_params=pltpu.CompilerParams(dimension_semantics=("parallel", "arbitrary")),
    )(dy, proj_main, s_re, s_im, s_re, s_im, bbd_re, bbd_im, cbd_re, cbd_im, dvec, tab, ptab, d_s5)


def _s5_post_fwd(y_pre, proj_main, glu_w, glu_b, DS):
    L = y_pre.shape[0]
    tr = _blk(L, 256, SUBLANES)

    def body(y_ref, z_ref, w_ref, b_ref, o_ref, t_ref):
        y1 = _gelu(y_ref[...])
        t = _dot(y1, w_ref[...]) + b_ref[...]
        t_ref[...] = t
        z = z_ref[...]
        o_ref[...] = (y1 * _sigmoid(t) * (z * _sigmoid(z))).astype(BF16)

    row = pl.BlockSpec((tr, DS), lambda i: (i, 0))
    return pl.pallas_call(
        body, name="s5_post_fwd", grid=(L // tr,),
        in_specs=[row, pl.BlockSpec((tr, DS), lambda i: (i, 1)), pl.BlockSpec((DS, DS), lambda i: (0, 0)),
                  pl.BlockSpec((1, DS), lambda i: (0, 0))],
        out_specs=[row, row],
        out_shape=[jax.ShapeDtypeStruct((L, 2 * DS), BF16), jax.ShapeDtypeStruct((L, DS), F32)],
        compiler_params=pltpu.CompilerParams(dimension_semantics=("parallel",)),
    )(y_pre, proj_main, glu_w, glu_b)


def _s5_post_bwd(d_ycat, y_pre, proj_main, t_pre, glu_w, DS):
    L = y_pre.shape[0]
    tr = _blk(L, 256, SUBLANES)

    def body(dy_ref, y_ref, z_ref, t_ref, w_ref, dyp_ref, dz_ref, dt_ref, y1_ref, gb_ref):
        i = pl.program_id(0)

        @pl.when(i == 0)
        def _():
            gb_ref[...] = jnp.zeros_like(gb_ref)

        dy = dy_ref[...]
        yp = y_ref[...]
        z = z_ref[...]
        y1 = _gelu(yp)
        sg = _sigmoid(t_ref[...])
        sz = _sigmoid(z)
        c = y1 * sg
        d_c = dy * (z * sz)
        dz_ref[...] = (dy * c * (sz * (1.0 + z * (1.0 - sz)))).astype(BF16)
        d_t = d_c * y1 * sg * (1.0 - sg)
        gb_ref[...] += jnp.sum(d_t, axis=0, keepdims=True)
        dt_ref[...] = d_t.astype(BF16)
        y1_ref[...] = y1.astype(BF16)
        d_y1 = d_c * sg + _dot(d_t, w_ref[...], NT)
        dyp_ref[...] = d_y1 * _gelu_grad(yp)

    row = pl.BlockSpec((tr, DS), lambda i: (i, 0))
    return pl.pallas_call(
        body, name="s5_post_bwd", grid=(L // tr,),
        in_specs=[row, row, pl.BlockSpec((tr, DS), lambda i: (i, 1)), row, pl.BlockSpec((DS, DS), lambda i: (0, 0))],
        out_specs=[row, pl.BlockSpec((tr, DS), lambda i: (i, 1)), row, row, pl.BlockSpec((1, DS), lambda i: (0, 0))],
        out_shape=[jax.ShapeDtypeStruct((L, DS), F32), jax.ShapeDtypeStruct((L, 2 * DS), BF16),
                   jax.ShapeDtypeStruct((L, DS), BF16), jax.ShapeDtypeStruct((L, DS), BF16),
                   jax.ShapeDtypeStruct((1, DS), F32)],
        compiler_params=pltpu.CompilerParams(dimension_semantics=("arbitrary",)),
    )(d_ycat, y_pre, proj_main, t_pre, glu_w)


def _row_cumsum(x, reverse=False):
    n = x.shape[0]
    row = lax.broadcasted_iota(jnp.int32, x.shape, 0)
    k = 1
    while k < n:
        if reverse:
            x = x + jnp.where(row < n - k, pltpu.roll(x, n - k, 0), 0.0)
        else:
            x = x + jnp.where(row >= k, pltpu.roll(x, k, 0), 0.0)
        k *= 2
    return x


def _gla_gates(glow, gu_ref, gb_ref):
    a = _dot(glow, gu_ref[...]) + gb_ref[...]
    lg = (jnp.minimum(a, 0.0) - jnp.log(1.0 + jnp.exp(-jnp.abs(a)))) * (1.0 / GLA_TAU)
    ri = lax.broadcasted_iota(jnp.int32, (GLA_CHUNK, GLA_CHUNK), 0)
    ci = lax.broadcasted_iota(jnp.int32, (GLA_CHUNK, GLA_CHUNK), 1)
    b = _row_cumsum(lg)
    b_last = b[GLA_CHUNK - 1:GLA_CHUNK, :]
    return a, b, b_last, ri >= ci


def _gla_specs(DS, DK, DV, c, cmap):
    return [
        pl.BlockSpec((c, DK), lambda n: (cmap(n), 2 * DS // DK)),
        pl.BlockSpec((c, DK), lambda n: (cmap(n), 2 * DS // DK + 1)),
        pl.BlockSpec((c, DV), lambda n: (cmap(n), (2 * DS + 2 * DK) // DV)),
        pl.BlockSpec((c, DV), lambda n: (cmap(n), (2 * DS + 2 * DK) // DV + 1)),
    ]


def _gla_fwd(proj_main, proj_low, gate_up_pad, gate_bias, norm_w, ycat, DS, DK, DV):
    L = proj_main.shape[0]
    nc = L // GLA_CHUNK
    cps = math.gcd(GLA_STEP_CHUNKS, nc)
    nh = DK // GLA_HK
    scale = GLA_HK ** -0.5

    def body(q_ref, k_ref, v_ref, z_ref, gl_ref, gu_ref, gb_ref, nw_ref, _yc_ref, y_ref, sp_ref, at_ref, o_ref, st_ref):
        n = pl.program_id(0)

        @pl.when(n == 0)
        def _():
            st_ref[...] = jnp.zeros_like(st_ref)

        pairs = [(sc, h) for sc in range(cps) for h in range(nh)]
        rows = lambda sc: slice(sc * GLA_CHUNK, (sc + 1) * GLA_CHUNK)
        kcol = lambda h: slice(h * GLA_HK, (h + 1) * GLA_HK)
        vcol = lambda h: slice(h * GLA_HV, (h + 1) * GLA_HV)
        gates = [_gla_gates(gl_ref[rows(sc), :], gu_ref, gb_ref) for sc in range(cps)]
        qe, dec, o_in, kv = {}, {}, {}, {}
        for sc, h in pairs:
            _, b, b_last, mask = gates[sc]
            bh, bl = b[:, kcol(h)], b_last[:, kcol(h)]
            qe[sc, h] = (q_ref[rows(sc), kcol(h)] * scale) * jnp.exp(bh)
            kh = k_ref[rows(sc), kcol(h)]
            vh = v_ref[rows(sc), vcol(h)]
            attn = jnp.where(mask, _dot(qe[sc, h], kh * jnp.exp(-bh), NT), 0.0).astype(BF16)
            at_ref[h, rows(sc), :] = attn
            o_in[sc, h] = _dot(attn, vh)
            kv[sc, h] = _dot(vh, kh * jnp.exp(bl - bh), TN)
            dec[sc, h] = jnp.exp(bl)
        for sc, h in pairs:
            st = st_ref[h]
            sp_ref[sc, h] = st
            o = o_in[sc, h] + _dot(qe[sc, h], st, NT)
            o_ref[rows(sc), vcol(h)] = o
            st_ref[h] = dec[sc, h] * st + kv[sc, h]
            r = lax.rsqrt(jnp.mean(o * o, axis=-1, keepdims=True) + EPS)
            z = z_ref[rows(sc), vcol(h)]
            y_ref[rows(sc), vcol(h)] = (o * r * nw_ref[...] * (z * _sigmoid(z))).astype(BF16)

    c = cps * GLA_CHUNK
    return pl.pallas_call(
        body, name="gla_fwd", grid=(nc // cps,),
        in_specs=_gla_specs(DS, DK, DV, c, lambda n: n) + [
            pl.BlockSpec((c, LANES), lambda n: (n, 0)),
            pl.BlockSpec((LANES, DK), lambda n: (0, 0)),
            pl.BlockSpec((1, DK), lambda n: (0, 0)),
            pl.BlockSpec((1, GLA_HV), lambda n: (0, 0)),
            pl.BlockSpec(memory_space=pl.ANY),
        ],
        out_specs=[pl.BlockSpec((c, DV), lambda n: (n, DS // DV)),
                   pl.BlockSpec((cps, nh, GLA_HV, GLA_HK), lambda n: (n, 0, 0, 0)),
                   pl.BlockSpec((nh, c, GLA_CHUNK), lambda n: (0, n, 0)),
                   pl.BlockSpec((c, DV), lambda n: (n, 0))],
        input_output_aliases={8: 0},
        out_shape=[jax.ShapeDtypeStruct(ycat.shape, BF16), jax.ShapeDtypeStruct((nc, nh, GLA_HV, GLA_HK), F32),
                   jax.ShapeDtypeStruct((nh, L, GLA_CHUNK), BF16), jax.ShapeDtypeStruct((L, DV), F32)],
        scratch_shapes=[pltpu.VMEM((nh, GLA_HV, GLA_HK), F32)],
        compiler_params=pltpu.CompilerParams(dimension_semantics=("arbitrary",)),
    )(proj_main, proj_main, proj_main, proj_main, proj_low, gate_up_pad, gate_bias, norm_w, ycat)


def _gla_bwd(d_ycat, proj_main, proj_low, s_prev, scores, o_pre, gate_up_pad, gate_bias, norm_w, DS, DK, DV):
    L = proj_main.shape[0]
    nc = L // GLA_CHUNK
    cps = math.gcd(GLA_STEP_CHUNKS, nc)
    nh = DK // GLA_HK
    scale = GLA_HK ** -0.5

    def body(dy_ref, q_ref, k_ref, v_ref, z_ref, gl_ref, sp_ref, at_ref, o_ref, gu_ref, gb_ref, nw_ref,
             dg_ref, da_ref, gnw_ref, ggb_ref, dst_ref):
        n = pl.program_id(0)

        @pl.when(n == 0)
        def _():
            dst_ref[...] = jnp.zeros_like(dst_ref)
            gnw_ref[...] = jnp.zeros_like(gnw_ref)
            ggb_ref[...] = jnp.zeros_like(ggb_ref)

        last_row = lax.broadcasted_iota(jnp.int32, (GLA_CHUNK, GLA_HK), 0) == GLA_CHUNK - 1
        nw = nw_ref[...]
        for sc in reversed(range(cps)):
            rs = slice(sc * GLA_CHUNK, (sc + 1) * GLA_CHUNK)
            a, b, b_last, mask = _gla_gates(gl_ref[rs, :], gu_ref, gb_ref)
            for h in range(nh):
                ks = slice(h * GLA_HK, (h + 1) * GLA_HK)
                vs = slice(h * GLA_HV, (h + 1) * GLA_HV)
                bh, bl = b[:, ks], b_last[:, ks]
                e = jnp.exp(bh)
                einv = jnp.exp(-bh)
                etail = jnp.exp(bl - bh)
                dec = jnp.exp(bl)
                qe = (q_ref[rs, ks] * scale) * e
                kh = k_ref[rs, ks]
                ke = kh * einv
                ktail = kh * etail
                vh = v_ref[rs, vs]
                st = sp_ref[sc, h]
                dst = dst_ref[h]
                attn = at_ref[h, rs, :]
                o = o_ref[rs, vs]
                r = lax.rsqrt(jnp.mean(o * o, axis=-1, keepdims=True) + EPS)
                nrm = o * r
                z = z_ref[rs, vs]
                sz = _sigmoid(z)
                dy = dy_ref[rs, vs]
                dg_ref[rs, 2 * DK + DV + h * GLA_HV:2 * DK + DV + (h + 1) * GLA_HV] = (
                    dy * nrm * nw * (sz * (1.0 + z * (1.0 - sz)))).astype(BF16)
                d_on = dy * (z * sz)
                gnw_ref[...] += jnp.sum(d_on * nrm, axis=0, keepdims=True)
                d_n = d_on * nw
                d_o = r * (d_n - nrm * jnp.mean(d_n * nrm, axis=-1, keepdims=True))
                d_attn = jnp.where(mask, _dot(d_o, vh, NT), 0.0)
                dg_ref[rs, 2 * DK + h * GLA_HV:2 * DK + (h + 1) * GLA_HV] = (
                    _dot(attn, d_o, TN) + _dot(ktail, dst, NT)).astype(BF16)
                d_qe = _dot(d_attn, ke) + _dot(d_o, st)
                d_ke = _dot(d_attn, qe, TN)
                d_kt = _dot(vh, dst)
                d_dec = jnp.sum(dst * st, axis=0, keepdims=True)
                dst_ref[h] = dec * dst + _dot(d_o, qe, TN)
                dg_ref[rs, ks] = (d_qe * scale * e).astype(BF16)
                dg_ref[rs, DK + h * GLA_HK:DK + (h + 1) * GLA_HK] = (d_ke * einv + d_kt * etail).astype(BF16)
                d_bl = jnp.sum(d_kt * ktail, axis=0, keepdims=True) + d_dec * dec
                d_b = d_qe * qe - d_ke * ke - d_kt * ktail + jnp.where(last_row, d_bl, 0.0)
                d_lg = _row_cumsum(d_b, reverse=True)
                d_a = d_lg * (1.0 / GLA_TAU) * _sigmoid(-a[:, ks])
                ggb_ref[:, ks] += jnp.sum(d_a, axis=0, keepdims=True)
                da_ref[rs, ks] = d_a.astype(BF16)

    c = cps * GLA_CHUNK
    ns = nc // cps
    rn = lambda n: ns - 1 - n
    return pl.pallas_call(
        body, name="gla_bwd", grid=(ns,),
        in_specs=[pl.BlockSpec((c, DV), lambda n: (rn(n), DS // DV))] + _gla_specs(DS, DK, DV, c, rn) + [
            pl.BlockSpec((c, LANES), lambda n: (rn(n), 0)),
            pl.BlockSpec((cps, nh, GLA_HV, GLA_HK), lambda n: (rn(n), 0, 0, 0)),
            pl.BlockSpec((nh, c, GLA_CHUNK), lambda n: (0, rn(n), 0)),
            pl.BlockSpec((c, DV), lambda n: (rn(n), 0)),
            pl.BlockSpec((LANES, DK), lambda n: (0, 0)),
            pl.BlockSpec((1, DK), lambda n: (0, 0)),
            pl.BlockSpec((1, GLA_HV), lambda n: (0, 0)),
        ],
        out_specs=[pl.BlockSpec((c, 2 * DK + 2 * DV), lambda n: (rn(n), 0)),
                   pl.BlockSpec((c, DK), lambda n: (rn(n), 0)),
                   pl.BlockSpec((1, GLA_HV), lambda n: (0, 0)), pl.BlockSpec((1, DK), lambda n: (0, 0))],
        out_shape=[jax.ShapeDtypeStruct((L, 2 * DK + 2 * DV), BF16),
                   jax.ShapeDtypeStruct((L, DK), BF16),
                   jax.ShapeDtypeStruct((1, GLA_HV), F32), jax.ShapeDtypeStruct((1, DK), F32)],
        scratch_shapes=[pltpu.VMEM((nh, GLA_HV, GLA_HK), F32)],
        compiler_params=pltpu.CompilerParams(dimension_semantics=("arbitrary",)),
    )(d_ycat, proj_main, proj_main, proj_main, proj_main, proj_low, s_prev, scores, o_pre, gate_up_pad, gate_bias, norm_w)


def _adamw_math(w, g, m, v):
    c1 = 1.0 - ADAM_B1 ** ADAM_STEP
    c2 = 1.0 - ADAM_B2 ** ADAM_STEP
    m_ = ADAM_B1 * m + (1.0 - ADAM_B1) * g
    v_ = ADAM_B2 * v + (1.0 - ADAM_B2) * (g * g)
    return -ADAM_LR * ((m_ / c1) / (jnp.sqrt(v_ / c2) + ADAM_EPS) + ADAM_WD * w), m_, v_


def _adamw_small(g_row, g_a, g_bc, ws, ms, vs):
    n = len(ws)
    nvec = n - 6

    def body(*refs):
        grow_ref, ga_ref, gbc_ref = refs[:3]
        w_refs, m_refs, v_refs = refs[3:3 + n], refs[3 + n:3 + 2 * n], refs[3 + 2 * n:3 + 3 * n]
        outs = refs[3 + 3 * n:]
        off = 0
        for i in range(n):
            if i < nvec:
                width = ws[i].shape[1]
                g = grow_ref[:, off:off + width]
                off += width
            elif i < nvec + 2:
                g = ga_ref[i - nvec]
            else:
                g = gbc_ref[i - nvec - 2]
            d, m_, v_ = _adamw_math(w_refs[i][...], g, m_refs[i][...], v_refs[i][...])
            outs[i][...] = g
            outs[n + i][...] = d
            outs[2 * n + i][...] = m_
            outs[3 * n + i][...] = v_

    vm = pl.BlockSpec(memory_space=pltpu.VMEM)
    outs = pl.pallas_call(
        body, name="adamw_small",
        in_specs=[vm] * (3 + 3 * n), out_specs=[vm] * (4 * n),
        out_shape=[jax.ShapeDtypeStruct(w.shape, F32) for w in ws] * 4,
    )(g_row, g_a, g_bc, *ws, *ms, *vs)
    return [outs[k * n:(k + 1) * n] for k in range(4)]


def _my_pos():
    return lax.axis_index("x"), lax.axis_index("y"), lax.axis_index("c")


def _split_start(name, srcs, lands_sd, make_copies, ncopies, after):
    n, m = len(srcs), len(lands_sd)

    def body(*refs):
        send_sems, recv_sems = refs[n + m + len(after)], refs[n + m + len(after) + 1]
        for cp in make_copies(refs[:n], refs[n:n + m], send_sems, recv_sems):
            cp.start()
        refs[-1][...] = jnp.zeros_like(refs[-1])

    hbm = pl.BlockSpec(memory_space=pltpu.HBM)
    sem = pl.BlockSpec(memory_space=pltpu.SEMAPHORE)
    outs = pl.pallas_call(
        body, name=name,
        in_specs=[hbm] * (n + m) + [pl.BlockSpec(memory_space=pl.ANY)] * len(after),
        out_specs=[sem, sem] + [hbm] * (n + m) + [pl.BlockSpec(memory_space=pltpu.VMEM)],
        out_shape=[pltpu.SemaphoreType.DMA((ncopies,)), pltpu.SemaphoreType.DMA((ncopies,))]
        + [pltpu.HBM(s.shape, s.dtype) for s in srcs] + [pltpu.HBM(s.shape, s.dtype) for s in lands_sd]
        + [jax.ShapeDtypeStruct((SUBLANES, LANES), F32)],
        input_output_aliases={i: 2 + i for i in range(n + m)},
        compiler_params=pltpu.CompilerParams(has_side_effects=pltpu.SideEffectType.DATAFLOW_SIDE_EFFECTING),
    )(*[pltpu.with_memory_space_constraint(s, pltpu.HBM) for s in srcs],
      *[pltpu.with_memory_space_constraint(lax.empty(s.shape, s.dtype), pltpu.HBM) for s in lands_sd], *after)
    return outs[0], outs[1], outs[2:2 + n], outs[2 + n:2 + n + m], outs[-1]


def _split_wait(name, send_sems, recv_sems, srcs, lands, make_copies, after):
    n, m = len(srcs), len(lands)

    def body(*refs):
        for cp in make_copies(refs[:n], refs[n:n + m], refs[n + m], refs[n + m + 1]):
            cp.wait_send()
            cp.wait_recv()

    hbm = pl.BlockSpec(memory_space=pltpu.HBM)
    sem = pl.BlockSpec(memory_space=pltpu.SEMAPHORE)
    outs = pl.pallas_call(
        body, name=name,
        in_specs=[hbm] * (n + m) + [sem, sem] + [pl.BlockSpec(memory_space=pl.ANY)] * len(after),
        out_specs=[hbm] * (n + m),
        out_shape=[pltpu.HBM(s.shape, s.dtype) for s in srcs] + [pltpu.HBM(p.shape, p.dtype) for p in lands],
        input_output_aliases={i: i for i in range(n + m)},
        compiler_params=pltpu.CompilerParams(has_side_effects=pltpu.SideEffectType.DATAFLOW_SIDE_EFFECTING),
    )(*srcs, *lands, send_sems, recv_sems, *after)
    return outs[:n], outs[n:]


def _pair_half_copies(srcs, lands, send_sems, recv_sems):
    x, y, c = _my_pos()
    copies = []
    for a in range(len(srcs)):
        hrows = srcs[a].shape[1] // 2
        copies.append(pltpu.make_async_remote_copy(
            src_ref=srcs[a].at[:, pl.ds((1 - c) * hrows, hrows), :], dst_ref=lands[a], send_sem=send_sems.at[a],
            recv_sem=recv_sems.at[a], device_id=(x, y, 1 - c), device_id_type=MESH))
    return copies


def _late_gather_copies(srcs, lands, send_sems, recv_sems):
    x, y, c = _my_pos()
    me = 2 * x + y
    copies = []
    for d in (1, 2, 3):
        to = (x ^ (d >> 1), y ^ (d & 1), c)
        for a in range(len(srcs)):
            hrows = srcs[a].shape[0] // 2
            rows = pl.ds(c * hrows, hrows)
            copies.append(pltpu.make_async_remote_copy(
                src_ref=srcs[a].at[rows, :], dst_ref=lands[a].at[me, rows, :], send_sem=send_sems.at[3 * a + d - 1],
                recv_sem=recv_sems.at[3 * a + d - 1], device_id=to, device_id_type=MESH))
    return copies


def _late_gather_start(shards, after, name):
    n = len(shards)

    def body(*refs):
        srcs, lands = refs[:n], refs[n:2 * n]
        send_sems, recv_sems = refs[2 * n + 1], refs[2 * n + 2]
        token = refs[-1]
        for cp in _late_gather_copies(srcs, lands, send_sems, recv_sems):
            cp.start()
        token[...] = jnp.zeros_like(token)

    hbm = pl.BlockSpec(memory_space=pltpu.HBM)
    sem = pl.BlockSpec(memory_space=pltpu.SEMAPHORE)
    outs = pl.pallas_call(
        body, name=name,
        in_specs=[hbm] * (2 * n) + [pl.BlockSpec(memory_space=pl.ANY)],
        out_specs=[sem, sem] + [hbm] * (2 * n) + [pl.BlockSpec(memory_space=pltpu.VMEM)],
        out_shape=[pltpu.SemaphoreType.DMA((3 * n,)), pltpu.SemaphoreType.DMA((3 * n,))]
        + [pltpu.HBM(s.shape, s.dtype) for s in shards]
        + [pltpu.HBM((4,) + s.shape, s.dtype) for s in shards]
        + [jax.ShapeDtypeStruct((SUBLANES, LANES), F32)],
        input_output_aliases={i: 2 + i for i in range(2 * n)},
        compiler_params=pltpu.CompilerParams(has_side_effects=pltpu.SideEffectType.DATAFLOW_SIDE_EFFECTING),
    )(*[pltpu.with_memory_space_constraint(s, pltpu.HBM) for s in shards],
      *[pltpu.with_memory_space_constraint(lax.empty((4,) + s.shape, s.dtype), pltpu.HBM) for s in shards], after)
    return outs[0], outs[1], outs[2:2 + n], outs[2 + n:2 + 2 * n], outs[-1]


def _late_gather_wait(send_sems, recv_sems, shards, lands, after, name):
    n = len(shards)

    def body(*refs):
        src_refs, land_refs = refs[:n], refs[n:2 * n]
        ssem, rsem = refs[2 * n], refs[2 * n + 1]
        for cp in _late_gather_copies(src_refs, land_refs, ssem, rsem):
            cp.wait_send()
            cp.wait_recv()

    hbm = pl.BlockSpec(memory_space=pltpu.HBM)
    sem = pl.BlockSpec(memory_space=pltpu.SEMAPHORE)
    outs = pl.pallas_call(
        body, name=name,
        in_specs=[hbm] * (2 * n) + [sem, sem] + [pl.BlockSpec(memory_space=pl.ANY)] * len(after),
        out_specs=[hbm] * (2 * n),
        out_shape=[pltpu.HBM(s.shape, s.dtype) for s in shards] + [pltpu.HBM(p.shape, p.dtype) for p in lands],
        input_output_aliases={i: i for i in range(2 * n)},
        compiler_params=pltpu.CompilerParams(has_side_effects=pltpu.SideEffectType.DATAFLOW_SIDE_EFFECTING),
    )(*shards, *lands, send_sems, recv_sems, *after)
    return outs[n:]


def _late_gather_pair(lands, name):
    n = len(lands)

    def body(*refs):
        outs = refs[n:2 * n]
        send_sems, recv_sems = refs[2 * n:]
        x, y, c = _my_pos()

        def copy(a, d, half):
            chip = 2 * (x ^ (d >> 1)) + (y ^ (d & 1))
            hrows = lands[a].shape[1] // 2
            sl = outs[a].at[chip, pl.ds(half * hrows, hrows), :]
            return pltpu.make_async_remote_copy(src_ref=sl, dst_ref=sl, send_sem=send_sems.at[3 * a + d - 1],
                                                recv_sem=recv_sems.at[3 * a + d - 1], device_id=(x, y, 1 - c),
                                                device_id_type=MESH)

        pairs = [(a, d) for d in (1, 2, 3) for a in range(n)]
        for a, d in pairs:
            copy(a, d, c).start()
        for a, d in pairs:
            copy(a, d, c).wait_send()
            copy(a, d, 1 - c).wait_recv()

    hbm = pl.BlockSpec(memory_space=pltpu.HBM)
    return pl.pallas_call(
        body, name=name, in_specs=[hbm] * n, out_specs=[hbm] * n,
        out_shape=[jax.ShapeDtypeStruct(p.shape, p.dtype) for p in lands],
        input_output_aliases={i: i for i in range(n)},
        scratch_shapes=[pltpu.SemaphoreType.DMA((3 * n,)), pltpu.SemaphoreType.DMA((3 * n,))],
    )(*lands)


def _pair_exchange(gs):
    n = len(gs)

    def body(*refs):
        ins, outs = refs[:n], refs[n:2 * n]
        send_sems, recv_sems = refs[2 * n:]
        x, y, c = _my_pos()
        sent = []
        for a in range(n):
            hrows = gs[a].shape[1] // 2
            cp = pltpu.make_async_remote_copy(
                src_ref=ins[a].at[:, pl.ds((1 - c) * hrows, hrows), :], dst_ref=outs[a], send_sem=send_sems.at[a],
                recv_sem=recv_sems.at[a], device_id=(x, y, 1 - c), device_id_type=MESH)
            cp.start()
            sent.append(cp)
        for cp in sent:
            cp.wait()

    hbm = pl.BlockSpec(memory_space=pltpu.HBM)
    return pl.pallas_call(
        body, name="grad_pair_exchange", in_specs=[hbm] * n, out_specs=[hbm] * n,
        out_shape=[jax.ShapeDtypeStruct((g.shape[0], g.shape[1] // 2, g.shape[2]), g.dtype) for g in gs],
        scratch_shapes=[pltpu.SemaphoreType.DMA((n,)), pltpu.SemaphoreType.DMA((n,))],
    )(*gs)


def _pair_add(g, got, c_arr, name):
    nk, rows2, cols = g.shape
    hrows = rows2 // 2
    tr = _blk(hrows, 256, 2 * SUBLANES)
    nb = hrows // tr

    def body(c_ref, a_ref, b_ref, o_ref):
        o_ref[...] = (a_ref[...].astype(F32) + b_ref[...].astype(F32)).astype(o_ref.dtype)

    return pl.pallas_call(
        body, name=name,
        grid_spec=pltpu.PrefetchScalarGridSpec(
            num_scalar_prefetch=1, grid=(nk, nb),
            in_specs=[pl.BlockSpec((1, tr, cols), lambda k, i, c_ref: (k, c_ref[0] * nb + i, 0)),
                      pl.BlockSpec((1, tr, cols), lambda k, i, c_ref: (k, i, 0))],
            out_specs=pl.BlockSpec((1, tr, cols), lambda k, i, c_ref: (k, i, 0))),
        out_shape=jax.ShapeDtypeStruct((nk, hrows, cols), g.dtype),
        compiler_params=pltpu.CompilerParams(dimension_semantics=("parallel", "parallel")),
    )(c_arr, g, got)


def _chip_scatter_copies(srcs, lands, send_sems, recv_sems):
    x, y, c = _my_pos()
    copies = []
    for d in (1, 2, 3):
        tx, ty = x ^ (d >> 1), y ^ (d & 1)
        for a in range(len(srcs)):
            copies.append(pltpu.make_async_remote_copy(
                src_ref=srcs[a].at[2 * tx + ty], dst_ref=lands[a].at[d - 1], send_sem=send_sems.at[3 * a + d - 1],
                recv_sem=recv_sems.at[3 * a + d - 1], device_id=(tx, ty, c), device_id_type=MESH))
    return copies


def _chip_scatter_start(pss):
    n = len(pss)

    def body(*refs):
        srcs, lands = refs[:n], refs[n:2 * n]
        send_sems, recv_sems = refs[2 * n], refs[2 * n + 1]
        token = refs[-1]
        for cp in _chip_scatter_copies(srcs, lands, send_sems, recv_sems):
            cp.start()
        token[...] = jnp.zeros_like(token)

    hbm = pl.BlockSpec(memory_space=pltpu.HBM)
    sem = pl.BlockSpec(memory_space=pltpu.SEMAPHORE)
    land_shapes = [(3,) + p.shape[1:] for p in pss]
    outs = pl.pallas_call(
        body, name="grad_chip_scatter_start",
        in_specs=[hbm] * (2 * n),
        out_specs=[sem, sem] + [hbm] * (2 * n) + [pl.BlockSpec(memory_space=pltpu.VMEM)],
        out_shape=[pltpu.SemaphoreType.DMA((3 * n,)), pltpu.SemaphoreType.DMA((3 * n,))]
        + [pltpu.HBM(p.shape, p.dtype) for p in pss]
        + [pltpu.HBM(s, p.dtype) for s, p in zip(land_shapes, pss)]
        + [jax.ShapeDtypeStruct((SUBLANES, LANES), F32)],
        input_output_aliases={i: 2 + i for i in range(2 * n)},
        compiler_params=pltpu.CompilerParams(has_side_effects=pltpu.SideEffectType.DATAFLOW_SIDE_EFFECTING),
    )(*[pltpu.with_memory_space_constraint(p, pltpu.HBM) for p in pss],
      *[pltpu.with_memory_space_constraint(lax.empty(s, p.dtype), pltpu.HBM) for s, p in zip(land_shapes, pss)])
    return outs[0], outs[1], outs[2:2 + n], outs[2 + n:2 + 2 * n], outs[-1]


def _chip_scatter_wait(send_sems, recv_sems, srcs, lands, after):
    n = len(srcs)

    def body(*refs):
        src_refs, land_refs = refs[:n], refs[n:2 * n]
        ssem, rsem = refs[2 * n], refs[2 * n + 1]
        for cp in _chip_scatter_copies(src_refs, land_refs, ssem, rsem):
            cp.wait_send()
            cp.wait_recv()

    hbm = pl.BlockSpec(memory_space=pltpu.HBM)
    sem = pl.BlockSpec(memory_space=pltpu.SEMAPHORE)
    outs = pl.pallas_call(
        body, name="grad_chip_scatter_wait",
        in_specs=[hbm] * (2 * n) + [sem, sem, pl.BlockSpec(memory_space=pl.ANY)],
        out_specs=[hbm] * (2 * n),
        out_shape=[pltpu.HBM(p.shape, p.dtype) for p in srcs] + [pltpu.HBM(p.shape, p.dtype) for p in lands],
        input_output_aliases={i: i for i in range(2 * n)},
        compiler_params=pltpu.CompilerParams(has_side_effects=pltpu.SideEffectType.DATAFLOW_SIDE_EFFECTING),
    )(*srcs, *lands, send_sems, recv_sems, after)
    return outs[:n], outs[n:]


def _chip_sum(ps, got, me_arr, name):
    _, hrows, cols = ps.shape
    tr = _blk(hrows, 256, 2 * SUBLANES)

    def body(me_ref, p_ref, g_ref, o_ref):
        acc = p_ref[0].astype(F32)
        for s in range(3):
            acc = acc + g_ref[s].astype(F32)
        o_ref[...] = acc

    return pl.pallas_call(
        body, name=name,
        grid_spec=pltpu.PrefetchScalarGridSpec(
            num_scalar_prefetch=1, grid=(hrows // tr,),
            in_specs=[pl.BlockSpec((1, tr, cols), lambda i, me_ref: (me_ref[0], i, 0)),
                      pl.BlockSpec((3, tr, cols), lambda i, me_ref: (0, i, 0))],
            out_specs=pl.BlockSpec((tr, cols), lambda i, me_ref: (i, 0))),
        out_shape=jax.ShapeDtypeStruct((hrows, cols), F32),
        compiler_params=pltpu.CompilerParams(dimension_semantics=("parallel",)),
    )(me_arr, ps, got)


def _pair_swap(halves):
    n = len(halves)

    def body(*refs):
        ins, outs = refs[:n], refs[n:2 * n]
        send_sems, recv_sems = refs[2 * n:]
        x, y, c = _my_pos()
        sent = []
        for a in range(n):
            cp = pltpu.make_async_remote_copy(src_ref=ins[a], dst_ref=outs[a], send_sem=send_sems.at[a], recv_sem=recv_sems.at[a],
                                              device_id=(x, y, 1 - c), device_id_type=MESH)
            cp.start()
            sent.append(cp)
        for cp in sent:
            cp.wait()

    hbm = pl.BlockSpec(memory_space=pltpu.HBM)
    return pl.pallas_call(
        body, name="grad_pair_swap", in_specs=[hbm] * n, out_specs=[hbm] * n,
        out_shape=[jax.ShapeDtypeStruct(h.shape, h.dtype) for h in halves],
        scratch_shapes=[pltpu.SemaphoreType.DMA((n,)), pltpu.SemaphoreType.DMA((n,))],
    )(*halves)


def _adamw_sharded(w, g_own, g_other, m, v, c_arr, after, name):
    R, C = w.shape
    hrows = R // 2
    tr = _blk(hrows, 256, SUBLANES)
    nbh = hrows // tr

    def body(c_ref, w_ref, go_ref, gx_ref, m_ref, v_ref, _after_ref, g_ref, d_ref, nm_ref, nv_ref):
        mine = (pl.program_id(0) // nbh) == c_ref[0]
        g_ = jnp.where(mine, go_ref[...], gx_ref[...])
        g_ref[...] = g_
        d_ref[...], nm_ref[...], nv_ref[...] = _adamw_math(w_ref[...], g_, m_ref[...], v_ref[...])

    blk = pl.BlockSpec((tr, C), lambda i, c_ref: (i, 0))
    hblk = pl.BlockSpec((tr, C), lambda i, c_ref: (i % nbh, 0))
    sd = jax.ShapeDtypeStruct((R, C), F32)
    return pl.pallas_call(
        body, name=name,
        grid_spec=pltpu.PrefetchScalarGridSpec(
            num_scalar_prefetch=1, grid=(2 * nbh,),
            in_specs=[blk, hblk, hblk, blk, blk, pl.BlockSpec(memory_space=pl.ANY)], out_specs=[blk] * 4),
        out_shape=[sd] * 4,
        compiler_params=pltpu.CompilerParams(dimension_semantics=("parallel",)),
    )(c_arr, w, g_own, g_other, m, v, after)


def _ar_piece(ref, rows, p):
    start = p * rows
    if rows % SUBLANES == 0:
        start = pl.multiple_of(start, SUBLANES)
    return ref.at[..., pl.ds(start, rows), :]


def _ar_peer(d):
    x, y, c = _my_pos()
    return (x ^ (d >> 2), y ^ ((d >> 1) & 1), c ^ (d & 1))


def _ar_lin(p):
    return 4 * p[0] + 2 * p[1] + p[2]


def _ar_scatter_copies(rows):
    def make(srcs, lands, send_sems, recv_sems):
        n = len(srcs)
        copies = []
        for d in range(1, 8):
            to = _ar_peer(d)
            for a in range(n):
                copies.append(pltpu.make_async_remote_copy(
                    src_ref=_ar_piece(srcs[a], rows[a], _ar_lin(to)), dst_ref=lands[a].at[d],
                    send_sem=send_sems.at[(d - 1) * n + a], recv_sem=recv_sems.at[(d - 1) * n + a], device_id=to,
                    device_id_type=MESH))
        return copies
    return make


def _ar_gather_copies(rows):
    def make(srcs, lands, send_sems, recv_sems):
        n = len(srcs)
        me = _ar_lin(_my_pos())
        copies = []
        for d in range(1, 8):
            for a in range(n):
                copies.append(pltpu.make_async_remote_copy(
                    src_ref=srcs[a], dst_ref=_ar_piece(lands[a], rows[a], me),
                    send_sem=send_sems.at[(d - 1) * n + a], recv_sem=recv_sems.at[(d - 1) * n + a], device_id=_ar_peer(d),
                    device_id_type=MESH))
        return copies
    return make


def _ar_sum(srcs, lands, rows):
    n = len(srcs)

    def body(*refs):
        me = _ar_lin(_my_pos())
        for a in range(n):
            acc = _ar_piece(refs[a], rows[a], me)[...]
            for d in range(1, 8):
                acc = acc + refs[n + a][d]
            refs[2 * n + a][...] = acc

    vm = pl.BlockSpec(memory_space=pltpu.VMEM)
    return pl.pallas_call(
        body, name="allreduce_sum", in_specs=[vm] * (2 * n), out_specs=[vm] * n,
        out_shape=[jax.ShapeDtypeStruct(p.shape[1:], F32) for p in lands],
    )(*srcs, *lands)


def kernel(x, pre_norm_w, w_in, s5_A_re, s5_A_im, s5_B_re, s5_B_im, s5_C_re, s5_C_im, s5_D, s5_log_dt, s5_glu_w, s5_glu_b, gla_gate_up, gla_gate_bias, gla_norm_w, w_out, post_norm_w, loss_target, m_pre_norm_w, m_w_in, m_s5_A_re, m_s5_A_im, m_s5_B_re, m_s5_B_im, m_s5_C_re, m_s5_C_im, m_s5_D, m_s5_log_dt, m_s5_glu_w, m_s5_glu_b, m_gla_gate_up, m_gla_gate_bias, m_gla_norm_w, m_w_out, m_post_norm_w, v_pre_norm_w, v_w_in, v_s5_A_re, v_s5_A_im, v_s5_B_re, v_s5_B_im, v_s5_C_re, v_s5_C_im, v_s5_D, v_s5_log_dt, v_s5_glu_w, v_s5_glu_b, v_gla_gate_up, v_gla_gate_bias, v_gla_norm_w, v_w_out, v_post_norm_w):
    names = ["pre_norm_w", "w_in", "s5_A_re", "s5_A_im", "s5_B_re", "s5_B_im", "s5_C_re", "s5_C_im", "s5_D", "s5_log_dt",
             "s5_glu_w", "s5_glu_b", "gla_gate_up", "gla_gate_bias", "gla_norm_w", "w_out", "post_norm_w"]
    W = dict(zip(names, (pre_norm_w, w_in, s5_A_re, s5_A_im, s5_B_re, s5_B_im, s5_C_re, s5_C_im, s5_D, s5_log_dt,
                         s5_glu_w, s5_glu_b, gla_gate_up, gla_gate_bias, gla_norm_w, w_out, post_norm_w)))
    M = dict(zip(names, (m_pre_norm_w, m_w_in, m_s5_A_re, m_s5_A_im, m_s5_B_re, m_s5_B_im, m_s5_C_re, m_s5_C_im, m_s5_D,
                         m_s5_log_dt, m_s5_glu_w, m_s5_glu_b, m_gla_gate_up, m_gla_gate_bias, m_gla_norm_w, m_w_out,
                         m_post_norm_w)))
    V = dict(zip(names, (v_pre_norm_w, v_w_in, v_s5_A_re, v_s5_A_im, v_s5_B_re, v_s5_B_im, v_s5_C_re, v_s5_C_im, v_s5_D,
                         v_s5_log_dt, v_s5_glu_w, v_s5_glu_b, v_gla_gate_up, v_gla_gate_bias, v_gla_norm_w, v_w_out,
                         v_post_norm_w)))
    sharded = ("w_in", "s5_glu_w", "w_out", "gla_gate_up")

    xb = x[0]
    tgt = loss_target[0]
    L, D = xb.shape
    DS = D // 2
    G = DS // S5_GROUP
    P = S5_STATE
    NB = DS // S5_COLS
    DV = D - DS
    DK = DV // 2
    WM = 2 * DS + 2 * DK + 2 * DV
    nsh = w_in.shape[2]

    chip = 2 * lax.axis_index("x") + lax.axis_index("y")
    own = [jnp.pad(w_in[0].astype(BF16), ((0, 0), (0, -nsh % LANES))), s5_glu_w[0].astype(BF16),
           w_out[0].astype(BF16), gla_gate_up[0]]
    fill = lambda g, o: lax.dynamic_update_index_in_dim(g, o, chip, 0)
    win_ss, win_rs, win_src, win_lands, win_token = _late_gather_start(own[:1], pre_norm_w, "w_in_gather_start")
    h = _prenorm_fwd(xb, pre_norm_w, win_token)

    b_view = lambda t: jnp.transpose(t[0], (0, 2, 1)).reshape(G * S5_GROUP, P)
    b_back = lambda t: jnp.transpose(t.reshape(G, S5_GROUP, P), (0, 2, 1))[None]
    c_view = lambda t: t[0].reshape(G * S5_GROUP, P)
    c_back = lambda t: t.reshape(1, G, S5_GROUP, P)
    small = ["pre_norm_w", "post_norm_w", "s5_D", "s5_glu_b", "gla_gate_bias", "gla_norm_w", "s5_log_dt",
             "s5_A_re", "s5_A_im", "s5_B_re", "s5_B_im", "s5_C_re", "s5_C_im"]
    view = {n: (lambda t: t) for n in small[:7]}
    back = dict(view)
    view.update(s5_A_re=lambda t: t[0], s5_A_im=lambda t: t[0], s5_B_re=b_view, s5_B_im=b_view, s5_C_re=c_view, s5_C_im=c_view)
    back.update(s5_A_re=lambda t: t[None], s5_A_im=lambda t: t[None], s5_B_re=b_back, s5_B_im=b_back, s5_C_re=c_back,
                s5_C_im=c_back)
    Wv = {n: view[n](W[n]) for n in small}
    bbd_re, bbd_im, ct_re, ct_im, tab, ptab = _s5_prep_fwd(
        Wv["s5_A_re"], Wv["s5_A_im"], s5_log_dt, Wv["s5_B_re"], Wv["s5_B_im"], Wv["s5_C_re"], Wv["s5_C_im"],
        h, _blk(L, S5_TIME_BLOCK, SUBLANES) // SUBLANES)
    dvec = s5_D

    for d_ in (W, M, V):
        d_["w_in"], _ = lax.optimization_barrier((d_["w_in"], win_token))
    g_win = _late_gather_wait(win_ss, win_rs, win_src, win_lands,
                              [tab, W["w_in"][0], M["w_in"][0], V["w_in"][0]], "w_in_gather_wait")
    g_win = fill(_late_gather_pair(g_win, "w_in_gather_pair")[0], own[0])
    w_main, w_low = _assemble_w_in(g_win, nsh, WM)
    late_ss, late_rs, late_src, late_lands, late_token = _late_gather_start(own[1:], g_win, "late_gather_start")
    proj_main, proj_low = _in_proj(h, w_main, w_low, late_token)
    y_pre, s_re, s_im = _s5_scan_fwd(proj_main, bbd_re, bbd_im, ct_re, ct_im, dvec, tab, ptab, DS)
    late = _late_gather_wait(late_ss, late_rs, late_src, late_lands, [y_pre], "late_gather_wait")
    late = _late_gather_pair(late, "late_gather_pair")
    g_glu, g_wout, g_gup = [fill(g, o) for g, o in zip(late, own[1:])]
    glu_w = g_glu.reshape(DS, DS)
    wout = g_wout.reshape(D, D)
    gup = jnp.moveaxis(g_gup, 0, 1).reshape(GLA_RANK, DK)
    gup_pad = jnp.pad(gup, ((0, LANES - GLA_RANK), (0, 0))).astype(BF16)
    ycat, t_pre = _s5_post_fwd(y_pre, proj_main, glu_w, s5_glu_b, DS)
    ycat, s_prev, gla_scores, gla_o = _gla_fwd(proj_main, proj_low, gup_pad, gla_gate_bias, gla_norm_w, ycat,
                                               DS, DK, DV)
    mixed = _mm(ycat, wout, name="out_proj")
    loss11, d_mixed, dout, g_post_w = _post_fwd_bwd(mixed, xb, tgt, post_norm_w)

    d_ycat = _mm(d_mixed, wout, tb=True, name="out_proj_dx")
    d_ypre, d_s5, d_t, y1, g_glu_b = _s5_post_bwd(d_ycat, y_pre, proj_main, t_pre, glu_w, DS)
    d_s5, g_D, gct_re, gct_im, gbbd_re, gbbd_im, gab_re, gab_im = _s5_scan_bwd(
        d_ypre, proj_main, s_re, s_im, bbd_re, bbd_im, ct_re, ct_im, dvec, tab, ptab, d_s5, DS)
    d_gla, d_a, g_norm_w, g_gate_bias = _gla_bwd(
        d_ycat, proj_main, proj_low, s_prev, gla_scores, gla_o, gup_pad, gla_gate_bias, gla_norm_w, DS, DK, DV)
    d_low = _mm(d_a, gup_pad, tb=True, out_dtype=BF16, name="gate_dx")
    g_gup_pad = _mm(proj_low, d_a, ta=True, name="gate_dw")
    g_wmain, g_wlow = _in_proj_dw(h, d_s5, d_gla, d_low)

    g_win_sh = _split_w_in_grad(g_wmain, g_wlow, nsh)
    px_ss, px_rs, px_src, px_got, px_token = _split_start(
        "grad_pair_w_in_start", [g_win_sh], [jax.ShapeDtypeStruct((4, D // 2, nsh), BF16)], _pair_half_copies, 1, [])
    g_wout_full = _mm(ycat, d_mixed, ta=True, out_dtype=BF16, name="out_proj_dw", after=[px_token])
    g_glu_full = _mm(y1, d_t, ta=True, out_dtype=BF16, name="glu_dw", after=[px_token])
    px_src, px_got = _split_wait("grad_pair_w_in_wait", px_ss, px_rs, px_src, px_got, _pair_half_copies,
                                 [g_wout_full, g_glu_full])
    gs = [g_glu_full.reshape(4, DS // 4, DS), g_wout_full.reshape(4, D // 4, D),
          jnp.moveaxis(g_gup_pad[:GLA_RANK].reshape(GLA_RANK, 4, DK // 4), 1, 0)]
    c_arr = lax.axis_index("c").astype(jnp.int32).reshape(1)
    me_arr = chip.astype(jnp.int32).reshape(1)
    got = list(px_got) + list(_pair_exchange(gs))
    gs = list(px_src) + gs
    pss = [_pair_add(g, r, c_arr, "grad_pair_add_" + n) for n, g, r in zip(sharded, gs, got)]
    send_sems, recv_sems, pss, lands, token = _chip_scatter_start(pss)

    dh = _in_proj_dx(d_s5, d_gla, d_low, w_main, w_low, token)
    grad_x, g_pre_w = _prenorm_bwd(xb, dh, dout, pre_norm_w)

    g_a, g_bc, g_ldt = _s5_prep_bwd(Wv["s5_A_re"], Wv["s5_A_im"], s5_log_dt, Wv["s5_B_re"], Wv["s5_B_im"],
                                    gbbd_re, gbbd_im, gct_re, gct_im, gab_re, gab_im)

    g_vecs = jnp.concatenate([g_pre_w, g_post_w, g_D, g_glu_b, g_gate_bias, g_norm_w, g_ldt, loss11], axis=1)
    loss_at = g_vecs.shape[1] - 1
    lanes_pad = -g_vecs.shape[1] % (8 * SUBLANES * LANES)
    g_vecs = jnp.pad(g_vecs, ((0, 0), (0, lanes_pad))).reshape(-1, LANES)
    ar_srcs = [g_vecs, g_a, g_bc]
    ar_rows = [a.shape[-2] // 8 for a in ar_srcs]
    ar_lands = [jax.ShapeDtypeStruct((8,) + a.shape[:-2] + (r, a.shape[-1]), F32) for a, r in zip(ar_srcs, ar_rows)]
    ar_ss, ar_rs, ar_srcs, ar_got, ar_token = _split_start(
        "allreduce_scatter_start", ar_srcs, ar_lands, _ar_scatter_copies(ar_rows), 7 * len(ar_srcs), [])

    pss, rcv = _chip_scatter_wait(send_sems, recv_sems, pss, lands, ar_token)
    halves = [_chip_sum(p, r, me_arr, "grad_chip_sum_" + n) for n, p, r in zip(sharded, pss, rcv)]
    others = _pair_swap(halves)
    ar_srcs, ar_got = _split_wait("allreduce_scatter_wait", ar_ss, ar_rs, ar_srcs, ar_got, _ar_scatter_copies(ar_rows),
                                  [others[0]])
    ar_red = _ar_sum(ar_srcs, ar_got, ar_rows)
    ag_ss, ag_rs, ar_red, ag_full, ag_token = _split_start(
        "allreduce_gather_start", ar_red, [jax.ShapeDtypeStruct(a.shape, F32) for a in ar_srcs],
        _ar_gather_copies(ar_rows), 7 * len(ar_red), [])
    G_out, D_out, M_out, V_out = {}, {}, {}, {}
    for n, g_own, g_other in zip(sharded, halves, others):
        g_, d_, m_, v_ = _adamw_sharded(W[n][0], g_own, g_other, M[n][0], V[n][0], c_arr, ag_token, "adamw_" + n)
        G_out[n], D_out[n], M_out[n], V_out[n] = g_[None], d_[None], m_[None], v_[None]
    ar_red, ag_full = _split_wait("allreduce_gather_wait", ag_ss, ag_rs, ar_red, ag_full, _ar_gather_copies(ar_rows),
                                  [D_out[n] for n in sharded])
    me8 = 2 * chip + lax.axis_index("c")
    r_vecs, r_a, r_bc = [lax.dynamic_update_slice_in_dim(f, r, me8 * rw, axis=f.ndim - 2)
                         for f, r, rw in zip(ag_full, ar_red, ar_rows)]
    r_vecs = r_vecs.reshape(1, -1)
    loss = r_vecs[0, loss_at]
    outs4 = _adamw_small(r_vecs, r_a, r_bc, [Wv[n] for n in small],
                         [view[n](M[n]) for n in small], [view[n](V[n]) for n in small])
    for store, o in zip((G_out, D_out, M_out, V_out), outs4):
        store.update({n: back[n](t) for n, t in zip(small, o)})

    return (loss, grad_x[None], *[G_out[n] for n in names], *[D_out[n] for n in names],
            *[M_out[n] for n in names], *[V_out[n] for n in names])
```

```python
import functools
import math

import jax
import jax.numpy as jnp
from jax import lax
from jax.experimental import pallas as pl
from jax.experimental.pallas import tpu as pltpu

F32 = jnp.float32
BF16 = jnp.bfloat16
HI = lax.Precision.HIGHEST
MESH = pl.DeviceIdType.MESH

EPS = 1e-6
S5_GROUP = 16
S5_STATE = 64
GLA_HK = 128
GLA_HV = 256
GLA_RANK = 16
GLA_TAU = 16.0
GLA_CHUNK = 64
GLA_STEP_CHUNKS = 4
LANES = 128
SUBLANES = 8
S5_COLS = 128
S5_LANES = (S5_COLS // S5_GROUP) * S5_STATE
S5_TIME_BLOCK = 1024
ROW_TILE = 512

ADAM_LR = 0.001
ADAM_B1 = 0.9
ADAM_B2 = 0.999
ADAM_EPS = 1e-08
ADAM_WD = 0.01
ADAM_STEP = 10

GELU_K = math.sqrt(2.0 / math.pi)
GELU_C = 0.044715


def _blk(n, pref, unit=LANES):
    best = None
    b = unit
    while b <= min(n, pref):
        if n % b == 0:
            best = b
        b += unit
    return best if best is not None else n


def _dot(a, b, dn=(((1,), (0,)), ((), ()))):
    return lax.dot_general(a.astype(BF16), b.astype(BF16), dn, preferred_element_type=F32)


def _dot_hi(a, b, dn=(((1,), (0,)), ((), ()))):
    return lax.dot_general(a, b, dn, precision=HI, preferred_element_type=F32)


NN = (((1,), (0,)), ((), ()))
NT = (((1,), (1,)), ((), ()))
TN = (((0,), (0,)), ((), ()))


def _sigmoid(x):
    return 1.0 / (1.0 + jnp.exp(-x))


def _gelu(y):
    return 0.5 * y * (1.0 + jnp.tanh(GELU_K * (y + GELU_C * y * y * y)))


def _gelu_grad(y):
    th = jnp.tanh(GELU_K * (y + GELU_C * y * y * y))
    return 0.5 * (1.0 + th) + 0.5 * y * (1.0 - th * th) * GELU_K * (1.0 + 3.0 * GELU_C * y * y)


def _mm(a, b, *, name, ta=False, tb=False, out_dtype=F32, bm=1024, bn=1024, bk=2048, after=()):
    if ta:
        K, M = a.shape
    else:
        M, K = a.shape
    if tb:
        N, K2 = b.shape
    else:
        K2, N = b.shape
    assert K == K2, (a.shape, b.shape, ta, tb)
    bm, bn, bk = _blk(M, bm), _blk(N, bn), _blk(K, bk)
    nk = K // bk
    dn = (((0 if ta else 1,), (1 if tb else 0,)), ((), ()))

    def body(a_ref, b_ref, *rest):
        o_ref = rest[len(after)]
        if nk == 1:
            o_ref[...] = _dot(a_ref[...], b_ref[...], dn).astype(out_dtype)
            return
        acc_ref = rest[len(after) + 1]
        k = pl.program_id(2)

        @pl.when(k == 0)
        def _():
            acc_ref[...] = jnp.zeros_like(acc_ref)

        acc_ref[...] += _dot(a_ref[...], b_ref[...], dn)

        @pl.when(k == nk - 1)
        def _():
            o_ref[...] = acc_ref[...].astype(out_dtype)

    a_spec = pl.BlockSpec((bk, bm), lambda i, j, k: (k, i)) if ta else pl.BlockSpec((bm, bk), lambda i, j, k: (i, k))
    b_spec = pl.BlockSpec((bn, bk), lambda i, j, k: (j, k)) if tb else pl.BlockSpec((bk, bn), lambda i, j, k: (k, j))
    return pl.pallas_call(
        body,
        name=name,
        grid=(M // bm, N // bn, nk),
        in_specs=[a_spec, b_spec] + [pl.BlockSpec(memory_space=pl.ANY)] * len(after),
        out_specs=pl.BlockSpec((bm, bn), lambda i, j, k: (i, j)),
        out_shape=jax.ShapeDtypeStruct((M, N), out_dtype),
        scratch_shapes=[pltpu.VMEM((bm, bn), F32)] if nk > 1 else [],
        compiler_params=pltpu.CompilerParams(dimension_semantics=("parallel", "parallel", "arbitrary")),
    )(a, b, *after)


def _in_proj(h, w_main, w_low, after):
    M, K = h.shape
    N = w_main.shape[1]
    bm, bn = _blk(M, 1024), _blk(N, 1024)

    def body(h_ref, w_ref, wl_ref, _after_ref, o_ref, ol_ref):
        hv = h_ref[...]
        o_ref[...] = _dot(hv, w_ref[...])

        @pl.when(pl.program_id(1) == 0)
        def _():
            ol_ref[...] = _dot(hv, wl_ref[...])

    return pl.pallas_call(
        body, name="in_proj", grid=(M // bm, N // bn),
        in_specs=[pl.BlockSpec((bm, K), lambda i, j: (i, 0)), pl.BlockSpec((K, bn), lambda i, j: (0, j)),
                  pl.BlockSpec((K, LANES), lambda i, j: (0, 0)), pl.BlockSpec(memory_space=pl.ANY)],
        out_specs=[pl.BlockSpec((bm, bn), lambda i, j: (i, j)), pl.BlockSpec((bm, LANES), lambda i, j: (i, 0))],
        out_shape=[jax.ShapeDtypeStruct((M, N), F32), jax.ShapeDtypeStruct((M, LANES), F32)],
        compiler_params=pltpu.CompilerParams(dimension_semantics=("parallel", "arbitrary")),
    )(h, w_main, w_low, after)


def _in_proj_dx(a1, a2, al, b, bl, after, *, bm=1024, bn=1024, bk=2048):
    M, K1 = a1.shape
    K2 = a2.shape[1]
    N = b.shape[0]
    bm, bn = _blk(M, bm), _blk(N, bn)
    bk = _blk(math.gcd(K1, K2), bk)
    nk1, nk = K1 // bk, (K1 + K2) // bk

    def body(a1_ref, a2_ref, al_ref, b_ref, bl_ref, _after_ref, o_ref, acc_ref):
        k = pl.program_id(2)

        @pl.when(k == 0)
        def _():
            acc_ref[...] = _dot(al_ref[...], bl_ref[...], NT)

        @pl.when(k < nk1)
        def _():
            acc_ref[...] += _dot(a1_ref[...], b_ref[...], NT)

        @pl.when(k >= nk1)
        def _():
            acc_ref[...] += _dot(a2_ref[...], b_ref[...], NT)

        @pl.when(k == nk - 1)
        def _():
            o_ref[...] = acc_ref[...]

    return pl.pallas_call(
        body, name="in_proj_dx", grid=(M // bm, N // bn, nk),
        in_specs=[pl.BlockSpec((bm, bk), lambda i, j, k: (i, jnp.minimum(k, nk1 - 1))),
                  pl.BlockSpec((bm, bk), lambda i, j, k: (i, jnp.maximum(k - nk1, 0))),
                  pl.BlockSpec((bm, LANES), lambda i, j, k: (i, 0)),
                  pl.BlockSpec((bn, bk), lambda i, j, k: (j, k)),
                  pl.BlockSpec((bn, LANES), lambda i, j, k: (j, 0)),
                  pl.BlockSpec(memory_space=pl.ANY)],
        out_specs=pl.BlockSpec((bm, bn), lambda i, j, k: (i, j)),
        out_shape=jax.ShapeDtypeStruct((M, N), F32),
        scratch_shapes=[pltpu.VMEM((bm, bn), F32)],
        compiler_params=pltpu.CompilerParams(dimension_semantics=("parallel", "parallel", "arbitrary")),
    )(a1, a2, al, b, bl, after)


def _in_proj_dw(a, b1, b2, bl, *, bm=1024, bn=1024, bk=2048):
    K, M = a.shape
    N1, N2 = b1.shape[1], b2.shape[1]
    bm, bk = _blk(M, bm), _blk(K, bk)
    bn = _blk(math.gcd(N1, N2), bn)
    nj1, nj = N1 // bn, (N1 + N2) // bn
    nk = K // bk

    def body(a_ref, b1_ref, b2_ref, bl_ref, o_ref, ol_ref, acc_ref, accl_ref):
        j = pl.program_id(1)
        k = pl.program_id(2)

        @pl.when(k == 0)
        def _():
            acc_ref[...] = jnp.zeros_like(acc_ref)

        @pl.when(j < nj1)
        def _():
            acc_ref[...] += _dot(a_ref[...], b1_ref[...], TN)

        @pl.when(j >= nj1)
        def _():
            acc_ref[...] += _dot(a_ref[...], b2_ref[...], TN)

        @pl.when(k == nk - 1)
        def _():
            o_ref[...] = acc_ref[...].astype(BF16)

        @pl.when(j == 0)
        def _():
            low = _dot(a_ref[...], bl_ref[...], TN)

            @pl.when(k == 0)
            def _():
                accl_ref[...] = low

            @pl.when(k > 0)
            def _():
                accl_ref[...] += low

            @pl.when(k == nk - 1)
            def _():
                ol_ref[...] = accl_ref[...].astype(BF16)

    return pl.pallas_call(
        body, name="in_proj_dw", grid=(M // bm, nj, nk),
        in_specs=[pl.BlockSpec((bk, bm), lambda i, j, k: (k, i)),
                  pl.BlockSpec((bk, bn), lambda i, j, k: (jnp.where(j < nj1, k, nk - 1), jnp.minimum(j, nj1 - 1))),
                  pl.BlockSpec((bk, bn), lambda i, j, k: (jnp.where(j >= nj1, k, 0), jnp.maximum(j - nj1, 0))),
                  pl.BlockSpec((bk, LANES), lambda i, j, k: (jnp.where(j == 0, k, nk - 1), 0))],
        out_specs=[pl.BlockSpec((bm, bn), lambda i, j, k: (i, j)), pl.BlockSpec((bm, LANES), lambda i, j, k: (i, 0))],
        out_shape=[jax.ShapeDtypeStruct((M, N1 + N2), BF16), jax.ShapeDtypeStruct((M, LANES), BF16)],
        scratch_shapes=[pltpu.VMEM((bm, bn), F32), pltpu.VMEM((bm, LANES), F32)],
        compiler_params=pltpu.CompilerParams(dimension_semantics=("parallel", "arbitrary", "arbitrary")),
    )(a, b1, b2, bl)


def _assemble_w_in(g, nsh, wm):
    _, R, nshp = g.shape
    nb_in = nshp // LANES
    nb_main = wm // LANES
    tr = _blk(R, 512, 2 * SUBLANES)
    plan = []
    for b in range(nb_main + 1):
        terms = []
        for k in range(g.shape[0]):
            for i in range(nb_in):
                delta = nsh * k + LANES * i - LANES * b
                lo, hi = max(0, -delta), min(LANES, LANES - delta, nsh - LANES * i)
                if abs(delta) < LANES and hi > lo:
                    terms.append((k, i, delta))
        plan.append(terms)
    deltas = sorted({d for terms in plan for _, _, d in terms if d})

    def body(g_ref, wm_ref, wl_ref):
        src = _iota2((LANES, LANES), 0)
        dst = _iota2((LANES, LANES), 1)
        shift = {d: (dst - src == d).astype(BF16) for d in deltas}
        for b, terms in enumerate(plan):
            acc = None
            for k, i, d in terms:
                blk = g_ref[k, :, LANES * i:LANES * (i + 1)]
                t = _dot(blk, shift[d]) if d else blk.astype(F32)
                acc = t if acc is None else acc + t
            if b < nb_main:
                wm_ref[:, LANES * b:LANES * (b + 1)] = acc.astype(BF16)
            else:
                wl_ref[...] = acc.astype(BF16)

    return pl.pallas_call(
        body, name="assemble_w_in", grid=(R // tr,),
        in_specs=[pl.BlockSpec((g.shape[0], tr, nshp), lambda r: (0, r, 0))],
        out_specs=[pl.BlockSpec((tr, wm), lambda r: (r, 0)), pl.BlockSpec((tr, LANES), lambda r: (r, 0))],
        out_shape=[jax.ShapeDtypeStruct((R, wm), BF16), jax.ShapeDtypeStruct((R, LANES), BF16)],
        compiler_params=pltpu.CompilerParams(dimension_semantics=("parallel",)),
    )(g)


def _split_w_in_grad(g_main, g_low, nsh):
    R, wm = g_main.shape
    nb_main = wm // LANES
    nb_out = -(-nsh // LANES)
    tr = _blk(R, 512, 2 * SUBLANES)
    plan = {}
    for k in range(4):
        for i in range(nb_out):
            width = min(LANES, nsh - LANES * i)
            terms = []
            for b in range(nb_main + 1):
                delta = LANES * b - (nsh * k + LANES * i)
                lo, hi = max(0, delta), min(width, LANES + delta)
                if abs(delta) < LANES and hi > lo:
                    terms.append((b, delta))
            plan[k, i] = (width, terms)
    deltas = sorted({d for _, terms in plan.values() for _, d in terms if d})

    def body(gm_ref, gl_ref, o_ref):
        src = _iota2((LANES, LANES), 0)
        dst = _iota2((LANES, LANES), 1)
        shift = {d: (dst - src == d).astype(BF16) for d in deltas}
        for (k, i), (width, terms) in plan.items():
            acc = None
            for b, d in terms:
                blk = gm_ref[:, LANES * b:LANES * (b + 1)] if b < nb_main else gl_ref[...]
                t = _dot(blk, shift[d]) if d else blk.astype(F32)
                acc = t if acc is None else acc + t
            o_ref[k, :, LANES * i:LANES * i + width] = acc[:, :width].astype(BF16)

    return pl.pallas_call(
        body, name="split_w_in_grad", grid=(R // tr,),
        in_specs=[pl.BlockSpec((tr, wm), lambda r: (r, 0)), pl.BlockSpec((tr, LANES), lambda r: (r, 0))],
        out_specs=pl.BlockSpec((4, tr, nsh), lambda r: (0, r, 0)),
        out_shape=jax.ShapeDtypeStruct((4, R, nsh), BF16),
        compiler_params=pltpu.CompilerParams(dimension_semantics=("parallel",)),
    )(g_main, g_low)


def _prenorm_fwd(x, w, after):
    L, D = x.shape
    tr = _blk(L, ROW_TILE, SUBLANES)

    def body(x_ref, w_ref, _after_ref, h_ref):
        xv = x_ref[...]
        r = lax.rsqrt(jnp.mean(xv * xv, axis=-1, keepdims=True) + EPS)
        h_ref[...] = (xv * r * w_ref[...]).astype(BF16)

    return pl.pallas_call(
        body, name="prenorm_fwd", grid=(L // tr,),
        in_specs=[pl.BlockSpec((tr, D), lambda i: (i, 0)), pl.BlockSpec((1, D), lambda i: (0, 0)),
                  pl.BlockSpec(memory_space=pl.ANY)],
        out_specs=pl.BlockSpec((tr, D), lambda i: (i, 0)),
        out_shape=jax.ShapeDtypeStruct((L, D), BF16),
        compiler_params=pltpu.CompilerParams(dimension_semantics=("parallel",)),
    )(x, w, after)


def _post_fwd_bwd(mixed, x, target, w):
    L, D = x.shape
    tr = _blk(L, ROW_TILE, SUBLANES)
    nsteps = L // tr

    def body(mx_ref, x_ref, t_ref, w_ref, loss_ref, dm_ref, dout_ref, gw_ref, acc_ref):
        i = pl.program_id(0)

        @pl.when(i == 0)
        def _():
            acc_ref[...] = jnp.zeros_like(acc_ref)
            gw_ref[...] = jnp.zeros_like(gw_ref)

        mx = mx_ref[...]
        wv = w_ref[...]
        r = lax.rsqrt(jnp.mean(mx * mx, axis=-1, keepdims=True) + EPS)
        n = mx * r
        err = x_ref[...] + n * wv - t_ref[...]
        acc_ref[...] += jnp.sum(err * err, axis=0, keepdims=True)
        dout = err * (1.0 / D)
        dout_ref[...] = dout
        gw_ref[...] += jnp.sum(dout * n, axis=0, keepdims=True)
        dn = dout * wv
        dm_ref[...] = (r * (dn - n * jnp.mean(dn * n, axis=-1, keepdims=True))).astype(BF16)

        @pl.when(i == nsteps - 1)
        def _():
            loss_ref[...] = jnp.sum(acc_ref[...], axis=-1, keepdims=True) * (0.5 / D)

    row = pl.BlockSpec((tr, D), lambda i: (i, 0))
    vec = pl.BlockSpec((1, D), lambda i: (0, 0))
    return pl.pallas_call(
        body, name="post_fwd_bwd", grid=(nsteps,),
        in_specs=[row, row, row, vec],
        out_specs=[pl.BlockSpec((1, 1), lambda i: (0, 0)), row, row, vec],
        out_shape=[jax.ShapeDtypeStruct((1, 1), F32), jax.ShapeDtypeStruct((L, D), BF16),
                   jax.ShapeDtypeStruct((L, D), F32), jax.ShapeDtypeStruct((1, D), F32)],
        scratch_shapes=[pltpu.VMEM((1, D), F32)],
        compiler_params=pltpu.CompilerParams(dimension_semantics=("arbitrary",)),
    )(mixed, x, target, w)


def _prenorm_bwd(x, dh, dout, w):
    L, D = x.shape
    tr = _blk(L, ROW_TILE, SUBLANES)

    def body(x_ref, a_ref, dout_ref, w_ref, gx_ref, gw_ref):
        i = pl.program_id(0)

        @pl.when(i == 0)
        def _():
            gw_ref[...] = jnp.zeros_like(gw_ref)

        xv = x_ref[...]
        r = lax.rsqrt(jnp.mean(xv * xv, axis=-1, keepdims=True) + EPS)
        n = xv * r
        dh = a_ref[...]
        gw_ref[...] += jnp.sum(dh * n, axis=0, keepdims=True)
        dn = dh * w_ref[...]
        gx_ref[...] = dout_ref[...] + r * (dn - n * jnp.mean(dn * n, axis=-1, keepdims=True))

    row = pl.BlockSpec((tr, D), lambda i: (i, 0))
    vec = pl.BlockSpec((1, D), lambda i: (0, 0))
    return pl.pallas_call(
        body, name="prenorm_bwd", grid=(L // tr,),
        in_specs=[row, row, row, vec],
        out_specs=[row, vec],
        out_shape=[jax.ShapeDtypeStruct((L, D), F32), jax.ShapeDtypeStruct((1, D), F32)],
        compiler_params=pltpu.CompilerParams(dimension_semantics=("arbitrary",)),
    )(x, dh, dout, w)


def _s5_disc(a_re_raw, a_im, dt):
    a_re = jnp.minimum(a_re_raw, -1e-4)
    mag = jnp.exp(a_re * dt)
    ph = a_im * dt
    ab_re = mag * jnp.cos(ph)
    ab_im = mag * jnp.sin(ph)
    inv_n = 1.0 / (a_re * a_re + a_im * a_im)
    ia_re = a_re * inv_n
    ia_im = -a_im * inv_n
    n_re = ab_re - 1.0
    f_re = n_re * ia_re - ab_im * ia_im
    f_im = n_re * ia_im + ab_im * ia_re
    return a_re, ab_re, ab_im, f_re, f_im, ia_re, ia_im


def _iota2(shape, dim):
    return lax.broadcasted_iota(jnp.int32, shape, dim)


def _group_mask(rows, rows_per_group):
    shift = rows_per_group.bit_length() - 1
    return (_iota2((rows, S5_LANES), 0) >> shift) == (_iota2((rows, S5_LANES), 1) >> (S5_STATE.bit_length() - 1))


def _lane_tiler(dtype):
    return ((_iota2((S5_STATE, S5_LANES), 1) & (S5_STATE - 1)) == _iota2((S5_STATE, S5_LANES), 0)).astype(dtype)


def _row_to_col(row, n):
    eye = (_iota2((n, n), 0) == _iota2((n, n), 1)).astype(F32)
    return jnp.sum(eye * row, axis=1, keepdims=True)


def _group_repeat(G):
    return ((_iota2((G * S5_GROUP, G), 0) >> (S5_GROUP.bit_length() - 1)) == _iota2((G * S5_GROUP, G), 1)).astype(F32)


S5_TABS = 18


def _s5_prep_fwd(a_re, a_im, log_dt, b_re, b_im, c_re, c_im, after, seg):
    G, P = a_re.shape
    nb = G * S5_GROUP // S5_COLS
    g8 = S5_COLS // S5_GROUP
    assert seg & (seg - 1) == 0, seg

    def body(are_ref, aim_ref, ldt_ref, bre_ref, bim_ref, cre_ref, cim_ref, _after_ref,
             bbre_ref, bbim_ref, ctre_ref, ctim_ref, tab_ref, pt_ref):
        dt = jnp.exp(_row_to_col(ldt_ref[...], G))
        _, ab_re, ab_im, f_re, f_im, _, _ = _s5_disc(are_ref[...], aim_ref[...], dt)
        rep = _group_repeat(G)
        fx_re = _dot_hi(rep, f_re)
        fx_im = _dot_hi(rep, f_im)
        br, bi = bre_ref[...], bim_ref[...]
        bb_re = fx_re * br - fx_im * bi
        bb_im = fx_re * bi + fx_im * br
        tile_bf = _lane_tiler(BF16)
        mask = _group_mask(S5_COLS, S5_GROUP)
        for jb in range(nb):
            rs = slice(jb * S5_COLS, (jb + 1) * S5_COLS)
            for src, dst in ((bb_re[rs], bbre_ref), (bb_im[rs], bbim_ref), (cre_ref[rs, :], ctre_ref), (cim_ref[rs, :], ctim_ref)):
                dst[jb] = jnp.where(mask, _dot(src, tile_bf), 0.0).astype(BF16)

        tile_f = _lane_tiler(F32)
        mask8 = _group_mask(g8, 1)
        row = _iota2((SUBLANES, S5_LANES), 0)
        slab = (SUBLANES, S5_LANES)
        cmul = lambda p, q: (p[0] * q[0] - p[1] * q[1], p[0] * q[1] + p[1] * q[0])
        for jb in range(nb):
            gs = slice(jb * g8, (jb + 1) * g8)

            def lanes(m):
                v = jnp.sum(jnp.where(mask8, _dot_hi(m[gs], tile_f), 0.0), axis=0, keepdims=True)
                return jnp.broadcast_to(v, slab)

            a1 = (lanes(ab_re), lanes(ab_im))
            tab_ref[jb, 0], tab_ref[jb, 1] = a1

            def powers(i, p):
                off = pl.multiple_of(i * SUBLANES, SUBLANES)
                pt_ref[jb, 0, pl.ds(off, SUBLANES), :] = p[0]
                pt_ref[jb, 1, pl.ds(off, SUBLANES), :] = p[1]
                return cmul(p, a1)

            lax.fori_loop(0, seg, powers, a1)
            aseg = a1
            for _ in range(seg.bit_length() - 1):
                aseg = cmul(aseg, aseg)
            pw = [aseg]
            for _ in range(1, SUBLANES):
                pw.append(cmul(pw[-1], aseg))
            for lvl, k in enumerate((1, 2, 4)):
                tab_ref[jb, 2 + 2 * lvl] = jnp.where(row >= k, pw[k - 1][0], 0.0)
                tab_ref[jb, 3 + 2 * lvl] = jnp.where(row >= k, pw[k - 1][1], 0.0)
                tab_ref[jb, 10 + 2 * lvl] = jnp.where(row < SUBLANES - k, pw[k - 1][0], 0.0)
                tab_ref[jb, 11 + 2 * lvl] = jnp.where(row < SUBLANES - k, -pw[k - 1][1], 0.0)
            f_r = f_i = r_r = r_i = jnp.zeros(slab, F32)
            for i in range(SUBLANES):
                f_r = jnp.where(row == i, pw[i][0], f_r)
                f_i = jnp.where(row == i, pw[i][1], f_i)
                r_r = jnp.where(row == i, pw[SUBLANES - 1 - i][0], r_r)
                r_i = jnp.where(row == i, -pw[SUBLANES - 1 - i][1], r_i)
            tab_ref[jb, 8] = f_r
            tab_ref[jb, 9] = f_i
            tab_ref[jb, 16] = r_r
            tab_ref[jb, 17] = r_i

    vm = pl.BlockSpec(memory_space=pltpu.VMEM)
    bd = jax.ShapeDtypeStruct((nb, S5_COLS, S5_LANES), BF16)
    return pl.pallas_call(
        body, name="s5_prep_fwd",
        in_specs=[vm] * 7 + [pl.BlockSpec(memory_space=pl.ANY)], out_specs=[vm] * 6,
        out_shape=[bd, bd, bd, bd, jax.ShapeDtypeStruct((nb, S5_TABS, SUBLANES, S5_LANES), F32),
                   jax.ShapeDtypeStruct((nb, 2, seg * SUBLANES, S5_LANES), F32)],
    )(a_re, a_im, log_dt, b_re, b_im, c_re, c_im, after)


def _s5_prep_bwd(a_re, a_im, log_dt, b_re, b_im, gbb_re, gbb_im, gct_re, gct_im, gab_re, gab_im):
    G, P = a_re.shape
    nb = G * S5_GROUP // S5_COLS
    g8 = S5_COLS // S5_GROUP

    def body(are_ref, aim_ref, ldt_ref, bre_ref, bim_ref, gbr_ref, gbi_ref, gcr_ref, gci_ref, gar_ref, gai_ref,
             o_a, o_bc, o_ldt):
        dt = jnp.exp(_row_to_col(ldt_ref[...], G))
        a_raw = are_ref[...]
        a_imv = aim_ref[...]
        a_re_c, ab_re, ab_im, f_re, f_im, ia_re, ia_im = _s5_disc(a_raw, a_imv, dt)
        tile_f = _lane_tiler(F32)
        mask = _group_mask(S5_COLS, S5_GROUP)
        mask8 = _group_mask(g8, 1)
        for jb in range(nb):
            rs = slice(jb * S5_COLS, (jb + 1) * S5_COLS)
            gs = slice(jb * g8, (jb + 1) * g8)
            ls = slice(jb * S5_LANES, (jb + 1) * S5_LANES)
            for k, src in enumerate((gbr_ref, gbi_ref, gcr_ref, gci_ref)):
                o_bc[k, rs, :] = _dot_hi(jnp.where(mask, src[jb], 0.0), tile_f, NT)
            for k, src in enumerate((gar_ref, gai_ref)):
                o_a[k, gs, :] = _dot_hi(jnp.where(mask8, src[:, ls], 0.0), tile_f, NT)
        rep = _group_repeat(G)
        fx_re = _dot_hi(rep, f_re)
        fx_im = _dot_hi(rep, f_im)
        gbr, gbi = o_bc[0], o_bc[1]
        br, bi = bre_ref[...], bim_ref[...]
        o_bc[0] = fx_re * gbr + fx_im * gbi
        o_bc[1] = fx_re * gbi - fx_im * gbr
        gf_re = _dot_hi(rep, br * gbr + bi * gbi, TN)
        gf_im = _dot_hi(rep, br * gbi - bi * gbr, TN)
        gab_r = o_a[0] + ia_re * gf_re + ia_im * gf_im
        gab_i = o_a[1] + ia_re * gf_im - ia_im * gf_re
        q_re = f_re * ia_re - f_im * ia_im
        q_im = f_re * ia_im + f_im * ia_re
        ga_re = -(q_re * gf_re + q_im * gf_im)
        ga_im = -(q_re * gf_im - q_im * gf_re)
        gth_re = ab_re * gab_r + ab_im * gab_i
        gth_im = ab_re * gab_i - ab_im * gab_r
        ga_re = ga_re + dt * gth_re
        ga_im = ga_im + dt * gth_im
        gdt = jnp.sum(a_re_c * gth_re + a_imv * gth_im, axis=-1, keepdims=True)
        eye = (_iota2((G, G), 0) == _iota2((G, G), 1)).astype(F32)
        o_ldt[...] = jnp.sum(eye * (gdt * dt), axis=0, keepdims=True)
        slope = jnp.where(a_raw < -1e-4, 1.0, jnp.where(a_raw == -1e-4, 0.5, 0.0))
        o_a[0] = ga_re * slope
        o_a[1] = ga_im

    vm = pl.BlockSpec(memory_space=pltpu.VMEM)
    return pl.pallas_call(
        body, name="s5_prep_bwd",
        in_specs=[vm] * 11, out_specs=[vm] * 3,
        out_shape=[jax.ShapeDtypeStruct((2, G, P), F32), jax.ShapeDtypeStruct((4, G * S5_GROUP, P), F32),
                   jax.ShapeDtypeStruct((1, G), F32)],
    )(a_re, a_im, log_dt, b_re, b_im, gbb_re, gbb_im, gct_re, gct_im, gab_re, gab_im)


def _scan8(xr, xi, tab_ref, base, shifts):
    for lvl, sh in enumerate(shifts):
        mr = tab_ref[0, base + 2 * lvl]
        mi = tab_ref[0, base + 2 * lvl + 1]
        ar = pltpu.roll(xr, sh, 0)
        ai = pltpu.roll(xi, sh, 0)
        xr, xi = xr + mr * ar - mi * ai, xi + mr * ai + mi * ar
    return xr, xi


def _to_segments(src_ref, dst_ref, seg):
    for i in range(seg):
        dst_ref[i * SUBLANES:(i + 1) * SUBLANES, :] = src_ref[pl.ds(i, SUBLANES, stride=seg), :]


def _from_segments(src_ref, dst_ref, seg):
    for i in range(seg):
        dst_ref[pl.ds(i, SUBLANES, stride=seg), :] = src_ref[i * SUBLANES:(i + 1) * SUBLANES, :]


def _slab(i):
    return pl.ds(pl.multiple_of(i * SUBLANES, SUBLANES), SUBLANES)


def _s5_scan_fwd(proj_main, bbd_re, bbd_im, cbd_re, cbd_im, dvec, tab, ptab, DS):
    L = proj_main.shape[0]
    nb = DS // S5_COLS
    tb = _blk(L, S5_TIME_BLOCK, SUBLANES)
    nt = L // tb
    seg = tb // SUBLANES

    def body(u_ref, bre_ref, bim_ref, cre_ref, cim_ref, d_ref, tab_ref, pt_ref, y_ref, sre_ref, sim_ref,
             up_ref, yp_ref, car_ref):
        t = pl.program_id(1)

        @pl.when(t == 0)
        def _():
            car_ref[...] = jnp.zeros_like(car_ref)

        _to_segments(u_ref, up_ref, seg)
        up = up_ref[...]
        sre_ref[...] = _dot(up, bre_ref[0])
        sim_ref[...] = _dot(up, bim_ref[0])
        ar, ai = tab_ref[0, 0], tab_ref[0, 1]

        def pass1(i, x):
            xr = ar * x[0] - ai * x[1] + sre_ref[_slab(i), :]
            xi = ar * x[1] + ai * x[0] + sim_ref[_slab(i), :]
            sre_ref[_slab(i), :] = xr
            sim_ref[_slab(i), :] = xi
            return xr, xi

        zero = jnp.zeros((SUBLANES, S5_LANES), F32)
        er, ei = lax.fori_loop(0, seg, pass1, (zero, zero))
        cin_r, cin_i = car_ref[0], car_ref[1]
        sr, si = _scan8(er, ei, tab_ref, 2, (1, 2, 4))
        pr, pi = tab_ref[0, 8], tab_ref[0, 9]
        sr, si = sr + pr * cin_r - pi * cin_i, si + pr * cin_i + pi * cin_r
        row0 = _iota2((SUBLANES, S5_LANES), 0) == 0
        cr = jnp.where(row0, cin_r, pltpu.roll(sr, 1, 0))
        ci = jnp.where(row0, cin_i, pltpu.roll(si, 1, 0))
        car_ref[0] = jnp.broadcast_to(sr[SUBLANES - 1:SUBLANES, :], sr.shape)
        car_ref[1] = jnp.broadcast_to(si[SUBLANES - 1:SUBLANES, :], si.shape)

        def pass2(i, _):
            qr, qi = pt_ref[0, 0, _slab(i), :], pt_ref[0, 1, _slab(i), :]
            sre_ref[_slab(i), :] += qr * cr - qi * ci
            sim_ref[_slab(i), :] += qr * ci + qi * cr
            return 0

        lax.fori_loop(0, seg, pass2, 0, unroll=4)
        yp_ref[...] = _dot(sre_ref[...], cre_ref[0], NT) - _dot(sim_ref[...], cim_ref[0], NT) + d_ref[...] * up
        _from_segments(yp_ref, y_ref, seg)

    return pl.pallas_call(
        body, name="s5_scan_fwd", grid=(nb, nt),
        in_specs=[
            pl.BlockSpec((tb, S5_COLS), lambda j, t: (t, j)),
            pl.BlockSpec((1, S5_COLS, S5_LANES), lambda j, t: (j, 0, 0)),
            pl.BlockSpec((1, S5_COLS, S5_LANES), lambda j, t: (j, 0, 0)),
            pl.BlockSpec((1, S5_COLS, S5_LANES), lambda j, t: (j, 0, 0)),
            pl.BlockSpec((1, S5_COLS, S5_LANES), lambda j, t: (j, 0, 0)),
            pl.BlockSpec((1, S5_COLS), lambda j, t: (0, j)),
            pl.BlockSpec((1, S5_TABS, SUBLANES, S5_LANES), lambda j, t: (j, 0, 0, 0)),
            pl.BlockSpec((1, 2, tb, S5_LANES), lambda j, t: (j, 0, 0, 0)),
        ],
        out_specs=[
            pl.BlockSpec((tb, S5_COLS), lambda j, t: (t, j)),
            pl.BlockSpec((tb, S5_LANES), lambda j, t: (t, j)),
            pl.BlockSpec((tb, S5_LANES), lambda j, t: (t, j)),
        ],
        out_shape=[jax.ShapeDtypeStruct((L, DS), F32),
                   jax.ShapeDtypeStruct((L, nb * S5_LANES), F32),
                   jax.ShapeDtypeStruct((L, nb * S5_LANES), F32)],
        scratch_shapes=[pltpu.VMEM((tb, S5_COLS), F32), pltpu.VMEM((tb, S5_COLS), F32),
                        pltpu.VMEM((2, SUBLANES, S5_LANES), F32)],
        compiler_params=pltpu.CompilerParams(dimension_semantics=("parallel", "arbitrary")),
    )(proj_main, bbd_re, bbd_im, cbd_re, cbd_im, dvec, tab, ptab)


def _s5_scan_bwd(dy, proj_main, s_re, s_im, bbd_re, bbd_im, cbd_re, cbd_im, dvec, tab, ptab, d_s5, DS):
    L = proj_main.shape[0]
    nb = DS // S5_COLS
    tb = _blk(L, S5_TIME_BLOCK, SUBLANES)
    nt = L // tb
    seg = tb // SUBLANES
    tb8 = tb // SUBLANES

    def body(dy_ref, u_ref, sre_ref, sim_ref, pre_ref, pim_ref, bre_ref, bim_ref, cre_ref, cim_ref, d_ref, tab_ref, pt_ref,
             _ds5_ref, du_ref, gd_ref, gcre_ref, gcim_ref, gbre_ref, gbim_ref, gare_ref, gaim_ref,
             lre_ref, lim_ref, up_ref, dyp_ref, dup_ref, duo_ref, car_ref):
        t = pl.program_id(1)

        @pl.when(t == 0)
        def _():
            car_ref[...] = jnp.zeros_like(car_ref)
            gd_ref[...] = jnp.zeros_like(gd_ref)
            gcre_ref[...] = jnp.zeros_like(gcre_ref)
            gcim_ref[...] = jnp.zeros_like(gcim_ref)
            gbre_ref[...] = jnp.zeros_like(gbre_ref)
            gbim_ref[...] = jnp.zeros_like(gbim_ref)
            gare_ref[...] = jnp.zeros_like(gare_ref)
            gaim_ref[...] = jnp.zeros_like(gaim_ref)

        _to_segments(dy_ref, dyp_ref, seg)
        _to_segments(u_ref, up_ref, seg)
        dyv = dyp_ref[...]
        u = up_ref[...]
        gd_ref[...] += jnp.sum(dyv * u, axis=0, keepdims=True)
        lre_ref[...] = _dot(dyv, cre_ref[0])
        lim_ref[...] = -_dot(dyv, cim_ref[0])
        gcre_ref[0] += _dot(dyv, sre_ref[...], TN)
        gcim_ref[0] -= _dot(dyv, sim_ref[...], TN)
        ar, ai = tab_ref[0, 0], -tab_ref[0, 1]

        def pass1(k, x):
            i = seg - 1 - k
            xr = ar * x[0] - ai * x[1] + lre_ref[_slab(i), :]
            xi = ar * x[1] + ai * x[0] + lim_ref[_slab(i), :]
            lre_ref[_slab(i), :] = xr
            lim_ref[_slab(i), :] = xi
            return xr, xi

        zero = jnp.zeros((SUBLANES, S5_LANES), F32)
        er, ei = lax.fori_loop(0, seg, pass1, (zero, zero))
        cin_r, cin_i = car_ref[0], car_ref[1]
        lr, li = _scan8(er, ei, tab_ref, 10, (7, 6, 4))
        pr, pi = tab_ref[0, 16], tab_ref[0, 17]
        lr, li = lr + pr * cin_r - pi * cin_i, li + pr * cin_i + pi * cin_r
        rows = _iota2((SUBLANES, S5_LANES), 0)
        cr = jnp.where(rows == SUBLANES - 1, cin_r, pltpu.roll(lr, SUBLANES - 1, 0))
        ci = jnp.where(rows == SUBLANES - 1, cin_i, pltpu.roll(li, SUBLANES - 1, 0))
        car_ref[0] = jnp.broadcast_to(lr[0:1, :], lr.shape)
        car_ref[1] = jnp.broadcast_to(li[0:1, :], li.shape)

        first = (t == nt - 1).astype(F32)
        head_re = jnp.broadcast_to(pre_ref[SUBLANES - 1:SUBLANES, :], zero.shape) * (1.0 - first)
        head_im = jnp.broadcast_to(pim_ref[SUBLANES - 1:SUBLANES, :], zero.shape) * (1.0 - first)
        last = _slab(seg - 1)
        sp0_re = jnp.where(rows == 0, head_re, pltpu.roll(sre_ref[last, :], 1, 0))
        sp0_im = jnp.where(rows == 0, head_im, pltpu.roll(sim_ref[last, :], 1, 0))

        def fix(i, acc, sp_re, sp_im):
            j = seg - 1 - i
            qr, qi = pt_ref[0, 0, _slab(j), :], -pt_ref[0, 1, _slab(j), :]
            xr = lre_ref[_slab(i), :] + qr * cr - qi * ci
            xi = lim_ref[_slab(i), :] + qr * ci + qi * cr
            lre_ref[_slab(i), :] = xr
            lim_ref[_slab(i), :] = xi
            return acc[0] + sp_re * xr + sp_im * xi, acc[1] + sp_re * xi - sp_im * xr

        def pass2(i, acc):
            prev = _slab(jnp.maximum(i - 1, 0))
            return fix(i, acc, sre_ref[prev, :], sim_ref[prev, :])

        acc_re, acc_im = lax.fori_loop(0, seg, pass2, (zero, zero), unroll=4)
        first_slab = _slab(0)
        d_re, d_im = sp0_re - sre_ref[first_slab, :], sp0_im - sim_ref[first_slab, :]
        x0r, x0i = lre_ref[first_slab, :], lim_ref[first_slab, :]
        acc_re = acc_re + d_re * x0r + d_im * x0i
        acc_im = acc_im + d_re * x0i - d_im * x0r
        gare_ref[...] += jnp.sum(acc_re, axis=0, keepdims=True)
        gaim_ref[...] += jnp.sum(acc_im, axis=0, keepdims=True)
        lre = lre_ref[...]
        lim = lim_ref[...]
        dup_ref[...] = dyv * d_ref[...] + _dot(lre, bre_ref[0], NT) + _dot(lim, bim_ref[0], NT)
        _from_segments(dup_ref, duo_ref, seg)
        du_ref[...] = duo_ref[...].astype(BF16)
        gbre_ref[0] += _dot(u, lre, TN)
        gbim_ref[0] += _dot(u, lim, TN)

    rt = lambda t: nt - 1 - t
    col = pl.BlockSpec((tb, S5_COLS), lambda j, t: (rt(t), j))
    st = pl.BlockSpec((tb, S5_LANES), lambda j, t: (rt(t), j))
    prev = pl.BlockSpec((SUBLANES, S5_LANES), lambda j, t: (jnp.maximum(rt(t) * tb8 - 1, 0), j))
    bmat = pl.BlockSpec((1, S5_COLS, S5_LANES), lambda j, t: (j, 0, 0))
    cmat = bmat
    return pl.pallas_call(
        body, name="s5_scan_bwd", grid=(nb, nt),
        in_specs=[col, col, st, st, prev, prev, bmat, bmat, cmat, cmat,
                  pl.BlockSpec((1, S5_COLS), lambda j, t: (0, j)),
                  pl.BlockSpec((1, S5_TABS, SUBLANES, S5_LANES), lambda j, t: (j, 0, 0, 0)),
                  pl.BlockSpec((1, 2, tb, S5_LANES), lambda j, t: (j, 0, 0, 0)),
                  pl.BlockSpec(memory_space=pl.ANY)],
        out_specs=[col, pl.BlockSpec((1, S5_COLS), lambda j, t: (0, j)), cmat, cmat, bmat, bmat,
                   pl.BlockSpec((1, S5_LANES), lambda j, t: (0, j)), pl.BlockSpec((1, S5_LANES), lambda j, t: (0, j))],
        input_output_aliases={13: 0},
        out_shape=[jax.ShapeDtypeStruct((L, 2 * DS), BF16), jax.ShapeDtypeStruct((1, DS), F32),
                   jax.ShapeDtypeStruct((nb, S5_COLS, S5_LANES), F32), jax.ShapeDtypeStruct((nb, S5_COLS, S5_LANES), F32),
                   jax.ShapeDtypeStruct((nb, S5_COLS, S5_LANES), F32), jax.ShapeDtypeStruct((nb, S5_COLS, S5_LANES), F32),
                   jax.ShapeDtypeStruct((1, nb * S5_LANES), F32), jax.ShapeDtypeStruct((1, nb * S5_LANES), F32)],
        scratch_shapes=[pltpu.VMEM((tb, S5_LANES), F32), pltpu.VMEM((tb, S5_LANES), F32)]
        + [pltpu.VMEM((tb, S5_COLS), F32)] * 4 + [pltpu.VMEM((2, SUBLANES, S5_LANES), F32)],
        compiler_params=pltpu.CompilerParams(dimension_semantics=("parallel", "arbitrary")),
    )(dy, proj_main, s_re, s_im, s_re, s_im, bbd_re, bbd_im, cbd_re, cbd_im, dvec, tab, ptab, d_s5)


def _s5_post_fwd(y_pre, proj_main, glu_w, glu_b, DS):
    L = y_pre.shape[0]
    tr = _blk(L, ROW_TILE, SUBLANES)

    def body(y_ref, z_ref, w_ref, b_ref, o_ref, t_ref):
        y1 = _gelu(y_ref[...])
        t = _dot(y1, w_ref[...]) + b_ref[...]
        t_ref[...] = t
        z = z_ref[...]
        o_ref[...] = (y1 * _sigmoid(t) * (z * _sigmoid(z))).astype(BF16)

    row = pl.BlockSpec((tr, DS), lambda i: (i, 0))
    return pl.pallas_call(
        body, name="s5_post_fwd", grid=(L // tr,),
        in_specs=[row, pl.BlockSpec((tr, DS), lambda i: (i, 1)), pl.BlockSpec((DS, DS), lambda i: (0, 0)),
                  pl.BlockSpec((1, DS), lambda i: (0, 0))],
        out_specs=[row, row],
        out_shape=[jax.ShapeDtypeStruct((L, 2 * DS), BF16), jax.ShapeDtypeStruct((L, DS), F32)],
        compiler_params=pltpu.CompilerParams(dimension_semantics=("parallel",)),
    )(y_pre, proj_main, glu_w, glu_b)


def _s5_post_bwd(d_ycat, y_pre, proj_main, t_pre, glu_w, DS):
    L = y_pre.shape[0]
    tr = _blk(L, ROW_TILE, SUBLANES)

    def body(dy_ref, y_ref, z_ref, t_ref, w_ref, dyp_ref, dz_ref, dt_ref, y1_ref, gb_ref):
        i = pl.program_id(0)

        @pl.when(i == 0)
        def _():
            gb_ref[...] = jnp.zeros_like(gb_ref)

        dy = dy_ref[...]
        yp = y_ref[...]
        z = z_ref[...]
        y1 = _gelu(yp)
        sg = _sigmoid(t_ref[...])
        sz = _sigmoid(z)
        c = y1 * sg
        d_c = dy * (z * sz)
        dz_ref[...] = (dy * c * (sz * (1.0 + z * (1.0 - sz)))).astype(BF16)
        d_t = d_c * y1 * sg * (1.0 - sg)
        gb_ref[...] += jnp.sum(d_t, axis=0, keepdims=True)
        dt_ref[...] = d_t.astype(BF16)
        y1_ref[...] = y1.astype(BF16)
        d_y1 = d_c * sg + _dot(d_t, w_ref[...], NT)
        dyp_ref[...] = d_y1 * _gelu_grad(yp)

    row = pl.BlockSpec((tr, DS), lambda i: (i, 0))
    return pl.pallas_call(
        body, name="s5_post_bwd", grid=(L // tr,),
        in_specs=[row, row, pl.BlockSpec((tr, DS), lambda i: (i, 1)), row, pl.BlockSpec((DS, DS), lambda i: (0, 0))],
        out_specs=[row, pl.BlockSpec((tr, DS), lambda i: (i, 1)), row, row, pl.BlockSpec((1, DS), lambda i: (0, 0))],
        out_shape=[jax.ShapeDtypeStruct((L, DS), F32), jax.ShapeDtypeStruct((L, 2 * DS), BF16),
                   jax.ShapeDtypeStruct((L, DS), BF16), jax.ShapeDtypeStruct((L, DS), BF16),
                   jax.ShapeDtypeStruct((1, DS), F32)],
        compiler_params=pltpu.CompilerParams(dimension_semantics=("arbitrary",)),
    )(d_ycat, y_pre, proj_main, t_pre, glu_w)


def _row_cumsum(x, reverse=False):
    n = x.shape[0]
    row = lax.broadcasted_iota(jnp.int32, x.shape, 0)
    k = 1
    while k < n:
        if reverse:
            x = x + jnp.where(row < n - k, pltpu.roll(x, n - k, 0), 0.0)
        else:
            x = x + jnp.where(row >= k, pltpu.roll(x, k, 0), 0.0)
        k *= 2
    return x


def _gla_gates(glow, gu_ref, gb_ref):
    a = _dot(glow, gu_ref[...]) + gb_ref[...]
    lg = (jnp.minimum(a, 0.0) - jnp.log(1.0 + jnp.exp(-jnp.abs(a)))) * (1.0 / GLA_TAU)
    ri = lax.broadcasted_iota(jnp.int32, (GLA_CHUNK, GLA_CHUNK), 0)
    ci = lax.broadcasted_iota(jnp.int32, (GLA_CHUNK, GLA_CHUNK), 1)
    b = _row_cumsum(lg)
    b_last = b[GLA_CHUNK - 1:GLA_CHUNK, :]
    return a, b, b_last, ri >= ci


def _gla_specs(DS, DK, DV, c, cmap):
    return [
        pl.BlockSpec((c, DK), lambda n: (cmap(n), 2 * DS // DK)),
        pl.BlockSpec((c, DK), lambda n: (cmap(n), 2 * DS // DK + 1)),
        pl.BlockSpec((c, DV), lambda n: (cmap(n), (2 * DS + 2 * DK) // DV)),
        pl.BlockSpec((c, DV), lambda n: (cmap(n), (2 * DS + 2 * DK) // DV + 1)),
    ]


def _gla_fwd(proj_main, proj_low, gate_up_pad, gate_bias, norm_w, ycat, DS, DK, DV):
    L = proj_main.shape[0]
    nc = L // GLA_CHUNK
    cps = math.gcd(GLA_STEP_CHUNKS, nc)
    nh = DK // GLA_HK
    scale = GLA_HK ** -0.5

    def body(q_ref, k_ref, v_ref, z_ref, gl_ref, gu_ref, gb_ref, nw_ref, _yc_ref, y_ref, sp_ref, at_ref, o_ref, st_ref):
        n = pl.program_id(0)

        @pl.when(n == 0)
        def _():
            st_ref[...] = jnp.zeros_like(st_ref)

        pairs = [(sc, h) for sc in range(cps) for h in range(nh)]
        rows = lambda sc: slice(sc * GLA_CHUNK, (sc + 1) * GLA_CHUNK)
        kcol = lambda h: slice(h * GLA_HK, (h + 1) * GLA_HK)
        vcol = lambda h: slice(h * GLA_HV, (h + 1) * GLA_HV)
        gates = [_gla_gates(gl_ref[rows(sc), :], gu_ref, gb_ref) for sc in range(cps)]
        qe, dec, o_in, kv = {}, {}, {}, {}
        for sc, h in pairs:
            _, b, b_last, mask = gates[sc]
            bh, bl = b[:, kcol(h)], b_last[:, kcol(h)]
            qe[sc, h] = (q_ref[rows(sc), kcol(h)] * scale) * jnp.exp(bh)
            kh = k_ref[rows(sc), kcol(h)]
            vh = v_ref[rows(sc), vcol(h)]
            attn = jnp.where(mask, _dot(qe[sc, h], kh * jnp.exp(-bh), NT), 0.0).astype(BF16)
            at_ref[h, rows(sc), :] = attn
            o_in[sc, h] = _dot(attn, vh)
            kv[sc, h] = _dot(vh, kh * jnp.exp(bl - bh), TN)
            dec[sc, h] = jnp.exp(bl)
        for sc, h in pairs:
            st = st_ref[h]
            sp_ref[sc, h] = st
            o = o_in[sc, h] + _dot(qe[sc, h], st, NT)
            o_ref[rows(sc), vcol(h)] = o
            st_ref[h] = dec[sc, h] * st + kv[sc, h]
            r = lax.rsqrt(jnp.mean(o * o, axis=-1, keepdims=True) + EPS)
            z = z_ref[rows(sc), vcol(h)]
            y_ref[rows(sc), vcol(h)] = (o * r * nw_ref[...] * (z * _sigmoid(z))).astype(BF16)

    c = cps * GLA_CHUNK
    return pl.pallas_call(
        body, name="gla_fwd", grid=(nc // cps,),
        in_specs=_gla_specs(DS, DK, DV, c, lambda n: n) + [
            pl.BlockSpec((c, LANES), lambda n: (n, 0)),
            pl.BlockSpec((LANES, DK), lambda n: (0, 0)),
            pl.BlockSpec((1, DK), lambda n: (0, 0)),
            pl.BlockSpec((1, GLA_HV), lambda n: (0, 0)),
            pl.BlockSpec(memory_space=pl.ANY),
        ],
        out_specs=[pl.BlockSpec((c, DV), lambda n: (n, DS // DV)),
                   pl.BlockSpec((cps, nh, GLA_HV, GLA_HK), lambda n: (n, 0, 0, 0)),
                   pl.BlockSpec((nh, c, GLA_CHUNK), lambda n: (0, n, 0)),
                   pl.BlockSpec((c, DV), lambda n: (n, 0))],
        input_output_aliases={8: 0},
        out_shape=[jax.ShapeDtypeStruct(ycat.shape, BF16), jax.ShapeDtypeStruct((nc, nh, GLA_HV, GLA_HK), F32),
                   jax.ShapeDtypeStruct((nh, L, GLA_CHUNK), BF16), jax.ShapeDtypeStruct((L, DV), F32)],
        scratch_shapes=[pltpu.VMEM((nh, GLA_HV, GLA_HK), F32)],
        compiler_params=pltpu.CompilerParams(dimension_semantics=("arbitrary",)),
    )(proj_main, proj_main, proj_main, proj_main, proj_low, gate_up_pad, gate_bias, norm_w, ycat)


def _gla_bwd(d_ycat, proj_main, proj_low, s_prev, scores, o_pre, gate_up_pad, gate_bias, norm_w, DS, DK, DV):
    L = proj_main.shape[0]
    nc = L // GLA_CHUNK
    cps = math.gcd(GLA_STEP_CHUNKS, nc)
    nh = DK // GLA_HK
    scale = GLA_HK ** -0.5

    def body(dy_ref, q_ref, k_ref, v_ref, z_ref, gl_ref, sp_ref, at_ref, o_ref, gu_ref, gb_ref, nw_ref,
             dg_ref, da_ref, gnw_ref, ggb_ref, dst_ref):
        n = pl.program_id(0)

        @pl.when(n == 0)
        def _():
            dst_ref[...] = jnp.zeros_like(dst_ref)
            gnw_ref[...] = jnp.zeros_like(gnw_ref)
            ggb_ref[...] = jnp.zeros_like(ggb_ref)

        last_row = lax.broadcasted_iota(jnp.int32, (GLA_CHUNK, GLA_HK), 0) == GLA_CHUNK - 1
        nw = nw_ref[...]
        for sc in reversed(range(cps)):
            rs = slice(sc * GLA_CHUNK, (sc + 1) * GLA_CHUNK)
            a, b, b_last, mask = _gla_gates(gl_ref[rs, :], gu_ref, gb_ref)
            for h in range(nh):
                ks = slice(h * GLA_HK, (h + 1) * GLA_HK)
                vs = slice(h * GLA_HV, (h + 1) * GLA_HV)
                bh, bl = b[:, ks], b_last[:, ks]
                e = jnp.exp(bh)
                einv = jnp.exp(-bh)
                etail = jnp.exp(bl - bh)
                dec = jnp.exp(bl)
                qe = (q_ref[rs, ks] * scale) * e
                kh = k_ref[rs, ks]
                ke = kh * einv
                ktail = kh * etail
                vh = v_ref[rs, vs]
                st = sp_ref[sc, h]
                dst = dst_ref[h]
                attn = at_ref[h, rs, :]
                o = o_ref[rs, vs]
                r = lax.rsqrt(jnp.mean(o * o, axis=-1, keepdims=True) + EPS)
                nrm = o * r
                z = z_ref[rs, vs]
                sz = _sigmoid(z)
                dy = dy_ref[rs, vs]
                dg_ref[rs, 2 * DK + DV + h * GLA_HV:2 * DK + DV + (h + 1) * GLA_HV] = (
                    dy * nrm * nw * (sz * (1.0 + z * (1.0 - sz)))).astype(BF16)
                d_on = dy * (z * sz)
                gnw_ref[...] += jnp.sum(d_on * nrm, axis=0, keepdims=True)
                d_n = d_on * nw
                d_o = r * (d_n - nrm * jnp.mean(d_n * nrm, axis=-1, keepdims=True))
                d_attn = jnp.where(mask, _dot(d_o, vh, NT), 0.0)
                dg_ref[rs, 2 * DK + h * GLA_HV:2 * DK + (h + 1) * GLA_HV] = (
                    _dot(attn, d_o, TN) + _dot(ktail, dst, NT)).astype(BF16)
                d_qe = _dot(d_attn, ke) + _dot(d_o, st)
                d_ke = _dot(d_attn, qe, TN)
                d_kt = _dot(vh, dst)
                d_dec = jnp.sum(dst * st, axis=0, keepdims=True)
                dst_ref[h] = dec * dst + _dot(d_o, qe, TN)
                dg_ref[rs, ks] = (d_qe * scale * e).astype(BF16)
                dg_ref[rs, DK + h * GLA_HK:DK + (h + 1) * GLA_HK] = (d_ke * einv + d_kt * etail).astype(BF16)
                d_bl = jnp.sum(d_kt * ktail, axis=0, keepdims=True) + d_dec * dec
                d_b = d_qe * qe - d_ke * ke - d_kt * ktail + jnp.where(last_row, d_bl, 0.0)
                d_lg = _row_cumsum(d_b, reverse=True)
                d_a = d_lg * (1.0 / GLA_TAU) * _sigmoid(-a[:, ks])
                ggb_ref[:, ks] += jnp.sum(d_a, axis=0, keepdims=True)
                da_ref[rs, ks] = d_a.astype(BF16)

    c = cps * GLA_CHUNK
    ns = nc // cps
    rn = lambda n: ns - 1 - n
    return pl.pallas_call(
        body, name="gla_bwd", grid=(ns,),
        in_specs=[pl.BlockSpec((c, DV), lambda n: (rn(n), DS // DV))] + _gla_specs(DS, DK, DV, c, rn) + [
            pl.BlockSpec((c, LANES), lambda n: (rn(n), 0)),
            pl.BlockSpec((cps, nh, GLA_HV, GLA_HK), lambda n: (rn(n), 0, 0, 0)),
            pl.BlockSpec((nh, c, GLA_CHUNK), lambda n: (0, rn(n), 0)),
            pl.BlockSpec((c, DV), lambda n: (rn(n), 0)),
            pl.BlockSpec((LANES, DK), lambda n: (0, 0)),
            pl.BlockSpec((1, DK), lambda n: (0, 0)),
            pl.BlockSpec((1, GLA_HV), lambda n: (0, 0)),
        ],
        out_specs=[pl.BlockSpec((c, 2 * DK + 2 * DV), lambda n: (rn(n), 0)),
                   pl.BlockSpec((c, DK), lambda n: (rn(n), 0)),
                   pl.BlockSpec((1, GLA_HV), lambda n: (0, 0)), pl.BlockSpec((1, DK), lambda n: (0, 0))],
        out_shape=[jax.ShapeDtypeStruct((L, 2 * DK + 2 * DV), BF16),
                   jax.ShapeDtypeStruct((L, DK), BF16),
                   jax.ShapeDtypeStruct((1, GLA_HV), F32), jax.ShapeDtypeStruct((1, DK), F32)],
        scratch_shapes=[pltpu.VMEM((nh, GLA_HV, GLA_HK), F32)],
        compiler_params=pltpu.CompilerParams(dimension_semantics=("arbitrary",)),
    )(d_ycat, proj_main, proj_main, proj_main, proj_main, proj_low, s_prev, scores, o_pre, gate_up_pad, gate_bias, norm_w)


def _adamw_math(w, g, m, v):
    c1 = 1.0 - ADAM_B1 ** ADAM_STEP
    c2 = 1.0 - ADAM_B2 ** ADAM_STEP
    m_ = ADAM_B1 * m + (1.0 - ADAM_B1) * g
    v_ = ADAM_B2 * v + (1.0 - ADAM_B2) * (g * g)
    return -ADAM_LR * ((m_ / c1) / (jnp.sqrt(v_ / c2) + ADAM_EPS) + ADAM_WD * w), m_, v_


def _adamw_small(g_row, g_a, g_bc, ws, ms, vs):
    n = len(ws)
    nvec = n - 6

    def body(*refs):
        grow_ref, ga_ref, gbc_ref = refs[:3]
        w_refs, m_refs, v_refs = refs[3:3 + n], refs[3 + n:3 + 2 * n], refs[3 + 2 * n:3 + 3 * n]
        outs = refs[3 + 3 * n:]
        off = 0
        for i in range(n):
            if i < nvec:
                width = ws[i].shape[1]
                g = grow_ref[:, off:off + width]
                off += width
            elif i < nvec + 2:
                g = ga_ref[i - nvec]
            else:
                g = gbc_ref[i - nvec - 2]
            d, m_, v_ = _adamw_math(w_refs[i][...], g, m_refs[i][...], v_refs[i][...])
            outs[i][...] = g
            outs[n + i][...] = d
            outs[2 * n + i][...] = m_
            outs[3 * n + i][...] = v_

    vm = pl.BlockSpec(memory_space=pltpu.VMEM)
    outs = pl.pallas_call(
        body, name="adamw_small",
        in_specs=[vm] * (3 + 3 * n), out_specs=[vm] * (4 * n),
        out_shape=[jax.ShapeDtypeStruct(w.shape, F32) for w in ws] * 4,
    )(g_row, g_a, g_bc, *ws, *ms, *vs)
    return [outs[k * n:(k + 1) * n] for k in range(4)]


def _my_pos():
    return lax.axis_index("x"), lax.axis_index("y"), lax.axis_index("c")


def _split_start(name, srcs, lands_sd, make_copies, ncopies, after):
    n, m = len(srcs), len(lands_sd)

    def body(*refs):
        send_sems, recv_sems = refs[n + m + len(after)], refs[n + m + len(after) + 1]
        for cp in make_copies(refs[:n], refs[n:n + m], send_sems, recv_sems):
            cp.start()
        refs[-1][...] = jnp.zeros_like(refs[-1])

    hbm = pl.BlockSpec(memory_space=pltpu.HBM)
    sem = pl.BlockSpec(memory_space=pltpu.SEMAPHORE)
    outs = pl.pallas_call(
        body, name=name,
        in_specs=[hbm] * (n + m) + [pl.BlockSpec(memory_space=pl.ANY)] * len(after),
        out_specs=[sem, sem] + [hbm] * (n + m) + [pl.BlockSpec(memory_space=pltpu.VMEM)],
        out_shape=[pltpu.SemaphoreType.DMA((ncopies,)), pltpu.SemaphoreType.DMA((ncopies,))]
        + [pltpu.HBM(s.shape, s.dtype) for s in srcs] + [pltpu.HBM(s.shape, s.dtype) for s in lands_sd]
        + [jax.ShapeDtypeStruct((SUBLANES, LANES), F32)],
        input_output_aliases={i: 2 + i for i in range(n + m)},
        compiler_params=pltpu.CompilerParams(has_side_effects=pltpu.SideEffectType.DATAFLOW_SIDE_EFFECTING),
    )(*[pltpu.with_memory_space_constraint(s, pltpu.HBM) for s in srcs],
      *[pltpu.with_memory_space_constraint(lax.empty(s.shape, s.dtype), pltpu.HBM) for s in lands_sd], *after)
    return outs[0], outs[1], outs[2:2 + n], outs[2 + n:2 + n + m], outs[-1]


def _split_wait(name, send_sems, recv_sems, srcs, lands, make_copies, after):
    n, m = len(srcs), len(lands)

    def body(*refs):
        for cp in make_copies(refs[:n], refs[n:n + m], refs[n + m], refs[n + m + 1]):
            cp.wait_send()
            cp.wait_recv()

    hbm = pl.BlockSpec(memory_space=pltpu.HBM)
    sem = pl.BlockSpec(memory_space=pltpu.SEMAPHORE)
    outs = pl.pallas_call(
        body, name=name,
        in_specs=[hbm] * (n + m) + [sem, sem] + [pl.BlockSpec(memory_space=pl.ANY)] * len(after),
        out_specs=[hbm] * (n + m),
        out_shape=[pltpu.HBM(s.shape, s.dtype) for s in srcs] + [pltpu.HBM(p.shape, p.dtype) for p in lands],
        input_output_aliases={i: i for i in range(n + m)},
        compiler_params=pltpu.CompilerParams(has_side_effects=pltpu.SideEffectType.DATAFLOW_SIDE_EFFECTING),
    )(*srcs, *lands, send_sems, recv_sems, *after)
    return outs[:n], outs[n:]


def _pair_half_copies(srcs, lands, send_sems, recv_sems):
    x, y, c = _my_pos()
    copies = []
    for a in range(len(srcs)):
        hrows = srcs[a].shape[1] // 2
        copies.append(pltpu.make_async_remote_copy(
            src_ref=srcs[a].at[:, pl.ds((1 - c) * hrows, hrows), :], dst_ref=lands[a], send_sem=send_sems.at[a],
            recv_sem=recv_sems.at[a], device_id=(x, y, 1 - c), device_id_type=MESH))
    return copies


def _late_gather_copies(srcs, lands, send_sems, recv_sems):
    x, y, c = _my_pos()
    me = 2 * x + y
    copies = []
    for d in (1, 2, 3):
        to = (x ^ (d >> 1), y ^ (d & 1), c)
        for a in range(len(srcs)):
            hrows = srcs[a].shape[0] // 2
            rows = pl.ds(c * hrows, hrows)
            copies.append(pltpu.make_async_remote_copy(
                src_ref=srcs[a].at[rows, :], dst_ref=lands[a].at[me, rows, :], send_sem=send_sems.at[3 * a + d - 1],
                recv_sem=recv_sems.at[3 * a + d - 1], device_id=to, device_id_type=MESH))
    return copies


def _late_gather_start(shards, after, name):
    lands = [jax.ShapeDtypeStruct((4,) + s.shape, s.dtype) for s in shards]
    return _split_start(name, shards, lands, _late_gather_copies, 3 * len(shards), [after])


def _late_gather_wait(send_sems, recv_sems, shards, lands, after, name):
    return _split_wait(name, send_sems, recv_sems, shards, lands, _late_gather_copies, after)[1]


def _late_gather_pair(lands, name):
    n = len(lands)

    def body(*refs):
        outs = refs[n:2 * n]
        send_sems, recv_sems = refs[2 * n:]
        x, y, c = _my_pos()

        def copy(a, d, half):
            chip = 2 * (x ^ (d >> 1)) + (y ^ (d & 1))
            hrows = lands[a].shape[1] // 2
            sl = outs[a].at[chip, pl.ds(half * hrows, hrows), :]
            return pltpu.make_async_remote_copy(src_ref=sl, dst_ref=sl, send_sem=send_sems.at[3 * a + d - 1],
                                                recv_sem=recv_sems.at[3 * a + d - 1], device_id=(x, y, 1 - c),
                                                device_id_type=MESH)

        pairs = [(a, d) for d in (1, 2, 3) for a in range(n)]
        for a, d in pairs:
            copy(a, d, c).start()
        for a, d in pairs:
            copy(a, d, c).wait_send()
            copy(a, d, 1 - c).wait_recv()

    hbm = pl.BlockSpec(memory_space=pltpu.HBM)
    return pl.pallas_call(
        body, name=name, in_specs=[hbm] * n, out_specs=[hbm] * n,
        out_shape=[jax.ShapeDtypeStruct(p.shape, p.dtype) for p in lands],
        input_output_aliases={i: i for i in range(n)},
        scratch_shapes=[pltpu.SemaphoreType.DMA((3 * n,)), pltpu.SemaphoreType.DMA((3 * n,))],
    )(*lands)


def _pair_exchange(gs):
    n = len(gs)

    def body(*refs):
        ins, outs = refs[:n], refs[n:2 * n]
        send_sems, recv_sems = refs[2 * n:]
        x, y, c = _my_pos()
        sent = []
        for a in range(n):
            hrows = gs[a].shape[1] // 2
            cp = pltpu.make_async_remote_copy(
                src_ref=ins[a].at[:, pl.ds((1 - c) * hrows, hrows), :], dst_ref=outs[a], send_sem=send_sems.at[a],
                recv_sem=recv_sems.at[a], device_id=(x, y, 1 - c), device_id_type=MESH)
            cp.start()
            sent.append(cp)
        for cp in sent:
            cp.wait()

    hbm = pl.BlockSpec(memory_space=pltpu.HBM)
    return pl.pallas_call(
        body, name="grad_pair_exchange", in_specs=[hbm] * n, out_specs=[hbm] * n,
        out_shape=[jax.ShapeDtypeStruct((g.shape[0], g.shape[1] // 2, g.shape[2]), g.dtype) for g in gs],
        scratch_shapes=[pltpu.SemaphoreType.DMA((n,)), pltpu.SemaphoreType.DMA((n,))],
    )(*gs)


def _pair_add(g, got, c_arr, name):
    nk, rows2, cols = g.shape
    hrows = rows2 // 2
    tr = _blk(hrows, 256, 2 * SUBLANES)
    nb = hrows // tr

    def body(c_ref, a_ref, b_ref, o_ref):
        o_ref[...] = (a_ref[...].astype(F32) + b_ref[...].astype(F32)).astype(o_ref.dtype)

    return pl.pallas_call(
        body, name=name,
        grid_spec=pltpu.PrefetchScalarGridSpec(
            num_scalar_prefetch=1, grid=(nk, nb),
            in_specs=[pl.BlockSpec((1, tr, cols), lambda k, i, c_ref: (k, c_ref[0] * nb + i, 0)),
                      pl.BlockSpec((1, tr, cols), lambda k, i, c_ref: (k, i, 0))],
            out_specs=pl.BlockSpec((1, tr, cols), lambda k, i, c_ref: (k, i, 0))),
        out_shape=jax.ShapeDtypeStruct((nk, hrows, cols), g.dtype),
        compiler_params=pltpu.CompilerParams(dimension_semantics=("parallel", "parallel")),
    )(c_arr, g, got)


def _chip_scatter_copies(srcs, lands, send_sems, recv_sems):
    x, y, c = _my_pos()
    copies = []
    for d in (1, 2, 3):
        tx, ty = x ^ (d >> 1), y ^ (d & 1)
        for a in range(len(srcs)):
            copies.append(pltpu.make_async_remote_copy(
                src_ref=srcs[a].at[2 * tx + ty], dst_ref=lands[a].at[d - 1], send_sem=send_sems.at[3 * a + d - 1],
                recv_sem=recv_sems.at[3 * a + d - 1], device_id=(tx, ty, c), device_id_type=MESH))
    return copies


def _chip_scatter_start(pss):
    lands = [jax.ShapeDtypeStruct((3,) + p.shape[1:], p.dtype) for p in pss]
    return _split_start("grad_chip_scatter_start", pss, lands, _chip_scatter_copies, 3 * len(pss), [])


def _chip_scatter_wait(send_sems, recv_sems, srcs, lands, after):
    return _split_wait("grad_chip_scatter_wait", send_sems, recv_sems, srcs, lands, _chip_scatter_copies, [after])


def _chip_sum(ps, got, me_arr, name):
    _, hrows, cols = ps.shape
    tr = _blk(hrows, 256, 2 * SUBLANES)

    def body(me_ref, p_ref, g_ref, o_ref):
        acc = p_ref[0].astype(F32)
        for s in range(3):
            acc = acc + g_ref[s].astype(F32)
        o_ref[...] = acc

    return pl.pallas_call(
        body, name=name,
        grid_spec=pltpu.PrefetchScalarGridSpec(
            num_scalar_prefetch=1, grid=(hrows // tr,),
            in_specs=[pl.BlockSpec((1, tr, cols), lambda i, me_ref: (me_ref[0], i, 0)),
                      pl.BlockSpec((3, tr, cols), lambda i, me_ref: (0, i, 0))],
            out_specs=pl.BlockSpec((tr, cols), lambda i, me_ref: (i, 0))),
        out_shape=jax.ShapeDtypeStruct((hrows, cols), F32),
        compiler_params=pltpu.CompilerParams(dimension_semantics=("parallel",)),
    )(me_arr, ps, got)


def _pair_swap(halves):
    n = len(halves)

    def body(*refs):
        ins, outs = refs[:n], refs[n:2 * n]
        send_sems, recv_sems = refs[2 * n:]
        x, y, c = _my_pos()
        sent = []
        for a in range(n):
            cp = pltpu.make_async_remote_copy(src_ref=ins[a], dst_ref=outs[a], send_sem=send_sems.at[a], recv_sem=recv_sems.at[a],
                                              device_id=(x, y, 1 - c), device_id_type=MESH)
            cp.start()
            sent.append(cp)
        for cp in sent:
            cp.wait()

    hbm = pl.BlockSpec(memory_space=pltpu.HBM)
    return pl.pallas_call(
        body, name="grad_pair_swap", in_specs=[hbm] * n, out_specs=[hbm] * n,
        out_shape=[jax.ShapeDtypeStruct(h.shape, h.dtype) for h in halves],
        scratch_shapes=[pltpu.SemaphoreType.DMA((n,)), pltpu.SemaphoreType.DMA((n,))],
    )(*halves)


def _adamw_sharded(w, g_own, g_other, m, v, c_arr, after, name):
    R, C = w.shape
    hrows = R // 2
    tr = _blk(hrows, 256, SUBLANES)
    nbh = hrows // tr

    def body(c_ref, w_ref, go_ref, gx_ref, m_ref, v_ref, _after_ref, g_ref, d_ref, nm_ref, nv_ref):
        mine = (pl.program_id(0) // nbh) == c_ref[0]
        g_ = jnp.where(mine, go_ref[...], gx_ref[...])
        g_ref[...] = g_
        d_ref[...], nm_ref[...], nv_ref[...] = _adamw_math(w_ref[...], g_, m_ref[...], v_ref[...])

    blk = pl.BlockSpec((tr, C), lambda i, c_ref: (i, 0))
    hblk = pl.BlockSpec((tr, C), lambda i, c_ref: (i % nbh, 0))
    sd = jax.ShapeDtypeStruct((R, C), F32)
    return pl.pallas_call(
        body, name=name,
        grid_spec=pltpu.PrefetchScalarGridSpec(
            num_scalar_prefetch=1, grid=(2 * nbh,),
            in_specs=[blk, hblk, hblk, blk, blk, pl.BlockSpec(memory_space=pl.ANY)], out_specs=[blk] * 4),
        out_shape=[sd] * 4,
        compiler_params=pltpu.CompilerParams(dimension_semantics=("parallel",)),
    )(c_arr, w, g_own, g_other, m, v, after)


def _ar_piece(ref, rows, p):
    start = p * rows
    if rows % SUBLANES == 0:
        start = pl.multiple_of(start, SUBLANES)
    return ref.at[..., pl.ds(start, rows), :]


def _ar_peer(d):
    x, y, c = _my_pos()
    return (x ^ (d >> 2), y ^ ((d >> 1) & 1), c ^ (d & 1))


def _ar_lin(p):
    return 4 * p[0] + 2 * p[1] + p[2]


def _ar_scatter_copies(rows):
    def make(srcs, lands, send_sems, recv_sems):
        n = len(srcs)
        copies = []
        for d in range(1, 8):
            to = _ar_peer(d)
            for a in range(n):
                copies.append(pltpu.make_async_remote_copy(
                    src_ref=_ar_piece(srcs[a], rows[a], _ar_lin(to)), dst_ref=lands[a].at[d],
                    send_sem=send_sems.at[(d - 1) * n + a], recv_sem=recv_sems.at[(d - 1) * n + a], device_id=to,
                    device_id_type=MESH))
        return copies
    return make


def _ar_gather_copies(rows):
    def make(srcs, lands, send_sems, recv_sems):
        n = len(srcs)
        me = _ar_lin(_my_pos())
        copies = []
        for d in range(1, 8):
            for a in range(n):
                copies.append(pltpu.make_async_remote_copy(
                    src_ref=srcs[a], dst_ref=_ar_piece(lands[a], rows[a], me),
                    send_sem=send_sems.at[(d - 1) * n + a], recv_sem=recv_sems.at[(d - 1) * n + a], device_id=_ar_peer(d),
                    device_id_type=MESH))
        return copies
    return make


def _ar_sum(srcs, lands, rows):
    n = len(srcs)

    def body(*refs):
        me = _ar_lin(_my_pos())
        for a in range(n):
            acc = _ar_piece(refs[a], rows[a], me)[...]
            for d in range(1, 8):
                acc = acc + refs[n + a][d]
            refs[2 * n + a][...] = acc

    vm = pl.BlockSpec(memory_space=pltpu.VMEM)
    return pl.pallas_call(
        body, name="allreduce_sum", in_specs=[vm] * (2 * n), out_specs=[vm] * n,
        out_shape=[jax.ShapeDtypeStruct(p.shape[1:], F32) for p in lands],
    )(*srcs, *lands)


def kernel(x, pre_norm_w, w_in, s5_A_re, s5_A_im, s5_B_re, s5_B_im, s5_C_re, s5_C_im, s5_D, s5_log_dt, s5_glu_w, s5_glu_b, gla_gate_up, gla_gate_bias, gla_norm_w, w_out, post_norm_w, loss_target, m_pre_norm_w, m_w_in, m_s5_A_re, m_s5_A_im, m_s5_B_re, m_s5_B_im, m_s5_C_re, m_s5_C_im, m_s5_D, m_s5_log_dt, m_s5_glu_w, m_s5_glu_b, m_gla_gate_up, m_gla_gate_bias, m_gla_norm_w, m_w_out, m_post_norm_w, v_pre_norm_w, v_w_in, v_s5_A_re, v_s5_A_im, v_s5_B_re, v_s5_B_im, v_s5_C_re, v_s5_C_im, v_s5_D, v_s5_log_dt, v_s5_glu_w, v_s5_glu_b, v_gla_gate_up, v_gla_gate_bias, v_gla_norm_w, v_w_out, v_post_norm_w):
    names = ["pre_norm_w", "w_in", "s5_A_re", "s5_A_im", "s5_B_re", "s5_B_im", "s5_C_re", "s5_C_im", "s5_D", "s5_log_dt",
             "s5_glu_w", "s5_glu_b", "gla_gate_up", "gla_gate_bias", "gla_norm_w", "w_out", "post_norm_w"]
    W = dict(zip(names, (pre_norm_w, w_in, s5_A_re, s5_A_im, s5_B_re, s5_B_im, s5_C_re, s5_C_im, s5_D, s5_log_dt,
                         s5_glu_w, s5_glu_b, gla_gate_up, gla_gate_bias, gla_norm_w, w_out, post_norm_w)))
    M = dict(zip(names, (m_pre_norm_w, m_w_in, m_s5_A_re, m_s5_A_im, m_s5_B_re, m_s5_B_im, m_s5_C_re, m_s5_C_im, m_s5_D,
                         m_s5_log_dt, m_s5_glu_w, m_s5_glu_b, m_gla_gate_up, m_gla_gate_bias, m_gla_norm_w, m_w_out,
                         m_post_norm_w)))
    V = dict(zip(names, (v_pre_norm_w, v_w_in, v_s5_A_re, v_s5_A_im, v_s5_B_re, v_s5_B_im, v_s5_C_re, v_s5_C_im, v_s5_D,
                         v_s5_log_dt, v_s5_glu_w, v_s5_glu_b, v_gla_gate_up, v_gla_gate_bias, v_gla_norm_w, v_w_out,
                         v_post_norm_w)))
    sharded = ("w_in", "s5_glu_w", "w_out", "gla_gate_up")

    xb = x[0]
    tgt = loss_target[0]
    L, D = xb.shape
    DS = D // 2
    G = DS // S5_GROUP
    P = S5_STATE
    NB = DS // S5_COLS
    DV = D - DS
    DK = DV // 2
    WM = 2 * DS + 2 * DK + 2 * DV
    nsh = w_in.shape[2]

    chip = 2 * lax.axis_index("x") + lax.axis_index("y")
    own = [jnp.pad(w_in[0].astype(BF16), ((0, 0), (0, -nsh % LANES))), s5_glu_w[0].astype(BF16),
           w_out[0].astype(BF16), gla_gate_up[0]]
    fill = lambda g, o: lax.dynamic_update_index_in_dim(g, o, chip, 0)
    win_ss, win_rs, win_src, win_lands, win_token = _late_gather_start(own[:1], pre_norm_w, "w_in_gather_start")
    h = _prenorm_fwd(xb, pre_norm_w, win_token)

    b_view = lambda t: jnp.transpose(t[0], (0, 2, 1)).reshape(G * S5_GROUP, P)
    b_back = lambda t: jnp.transpose(t.reshape(G, S5_GROUP, P), (0, 2, 1))[None]
    c_view = lambda t: t[0].reshape(G * S5_GROUP, P)
    c_back = lambda t: t.reshape(1, G, S5_GROUP, P)
    small = ["pre_norm_w", "post_norm_w", "s5_D", "s5_glu_b", "gla_gate_bias", "gla_norm_w", "s5_log_dt",
             "s5_A_re", "s5_A_im", "s5_B_re", "s5_B_im", "s5_C_re", "s5_C_im"]
    view = {n: (lambda t: t) for n in small[:7]}
    back = dict(view)
    view.update(s5_A_re=lambda t: t[0], s5_A_im=lambda t: t[0], s5_B_re=b_view, s5_B_im=b_view, s5_C_re=c_view, s5_C_im=c_view)
    back.update(s5_A_re=lambda t: t[None], s5_A_im=lambda t: t[None], s5_B_re=b_back, s5_B_im=b_back, s5_C_re=c_back,
                s5_C_im=c_back)
    Wv = {n: view[n](W[n]) for n in small}
    bbd_re, bbd_im, ct_re, ct_im, tab, ptab = _s5_prep_fwd(
        Wv["s5_A_re"], Wv["s5_A_im"], s5_log_dt, Wv["s5_B_re"], Wv["s5_B_im"], Wv["s5_C_re"], Wv["s5_C_im"],
        h, _blk(L, S5_TIME_BLOCK, SUBLANES) // SUBLANES)
    dvec = s5_D

    for d_ in (W, M, V):
        d_["w_in"], _ = lax.optimization_barrier((d_["w_in"], win_token))
    g_win = _late_gather_wait(win_ss, win_rs, win_src, win_lands,
                              [tab, W["w_in"][0], M["w_in"][0], V["w_in"][0]], "w_in_gather_wait")
    g_win = fill(_late_gather_pair(g_win, "w_in_gather_pair")[0], own[0])
    w_main, w_low = _assemble_w_in(g_win, nsh, WM)
    late_ss, late_rs, late_src, late_lands, late_token = _late_gather_start(own[1:], g_win, "late_gather_start")
    proj_main, proj_low = _in_proj(h, w_main, w_low, late_token)
    y_pre, s_re, s_im = _s5_scan_fwd(proj_main, bbd_re, bbd_im, ct_re, ct_im, dvec, tab, ptab, DS)
    late = _late_gather_wait(late_ss, late_rs, late_src, late_lands, [y_pre], "late_gather_wait")
    late = _late_gather_pair(late, "late_gather_pair")
    g_glu, g_wout, g_gup = [fill(g, o) for g, o in zip(late, own[1:])]
    glu_w = g_glu.reshape(DS, DS)
    wout = g_wout.reshape(D, D)
    gup = jnp.moveaxis(g_gup, 0, 1).reshape(GLA_RANK, DK)
    gup_pad = jnp.pad(gup, ((0, LANES - GLA_RANK), (0, 0))).astype(BF16)
    ycat, t_pre = _s5_post_fwd(y_pre, proj_main, glu_w, s5_glu_b, DS)
    ycat, s_prev, gla_scores, gla_o = _gla_fwd(proj_main, proj_low, gup_pad, gla_gate_bias, gla_norm_w, ycat,
                                               DS, DK, DV)
    mixed = _mm(ycat, wout, name="out_proj")
    loss11, d_mixed, dout, g_post_w = _post_fwd_bwd(mixed, xb, tgt, post_norm_w)

    d_ycat = _mm(d_mixed, wout, tb=True, name="out_proj_dx")
    d_ypre, d_s5, d_t, y1, g_glu_b = _s5_post_bwd(d_ycat, y_pre, proj_main, t_pre, glu_w, DS)
    d_s5, g_D, gct_re, gct_im, gbbd_re, gbbd_im, gab_re, gab_im = _s5_scan_bwd(
        d_ypre, proj_main, s_re, s_im, bbd_re, bbd_im, ct_re, ct_im, dvec, tab, ptab, d_s5, DS)
    d_gla, d_a, g_norm_w, g_gate_bias = _gla_bwd(
        d_ycat, proj_main, proj_low, s_prev, gla_scores, gla_o, gup_pad, gla_gate_bias, gla_norm_w, DS, DK, DV)
    d_low = _mm(d_a, gup_pad, tb=True, out_dtype=BF16, name="gate_dx")
    g_gup_pad = _mm(proj_low, d_a, ta=True, name="gate_dw")
    g_wmain, g_wlow = _in_proj_dw(h, d_s5, d_gla, d_low)

    g_win_sh = _split_w_in_grad(g_wmain, g_wlow, nsh)
    px_ss, px_rs, px_src, px_got, px_token = _split_start(
        "grad_pair_w_in_start", [g_win_sh], [jax.ShapeDtypeStruct((4, D // 2, nsh), BF16)], _pair_half_copies, 1, [])
    g_wout_full = _mm(ycat, d_mixed, ta=True, out_dtype=BF16, name="out_proj_dw", after=[px_token])
    g_glu_full = _mm(y1, d_t, ta=True, out_dtype=BF16, name="glu_dw", after=[px_token])
    px_src, px_got = _split_wait("grad_pair_w_in_wait", px_ss, px_rs, px_src, px_got, _pair_half_copies,
                                 [g_wout_full, g_glu_full])
    gs = [g_glu_full.reshape(4, DS // 4, DS), g_wout_full.reshape(4, D // 4, D),
          jnp.moveaxis(g_gup_pad[:GLA_RANK].reshape(GLA_RANK, 4, DK // 4), 1, 0)]
    c_arr = lax.axis_index("c").astype(jnp.int32).reshape(1)
    me_arr = chip.astype(jnp.int32).reshape(1)
    got = list(px_got) + list(_pair_exchange(gs))
    gs = list(px_src) + gs
    pss = [_pair_add(g, r, c_arr, "grad_pair_add_" + n) for n, g, r in zip(sharded, gs, got)]
    send_sems, recv_sems, pss, lands, token = _chip_scatter_start(pss)

    dh = _in_proj_dx(d_s5, d_gla, d_low, w_main, w_low, token)
    grad_x, g_pre_w = _prenorm_bwd(xb, dh, dout, pre_norm_w)

    g_a, g_bc, g_ldt = _s5_prep_bwd(Wv["s5_A_re"], Wv["s5_A_im"], s5_log_dt, Wv["s5_B_re"], Wv["s5_B_im"],
                                    gbbd_re, gbbd_im, gct_re, gct_im, gab_re, gab_im)

    g_vecs = jnp.concatenate([g_pre_w, g_post_w, g_D, g_glu_b, g_gate_bias, g_norm_w, g_ldt, loss11], axis=1)
    loss_at = g_vecs.shape[1] - 1
    lanes_pad = -g_vecs.shape[1] % (8 * SUBLANES * LANES)
    g_vecs = jnp.pad(g_vecs, ((0, 0), (0, lanes_pad))).reshape(-1, LANES)
    ar_srcs = [g_vecs, g_a, g_bc]
    ar_rows = [a.shape[-2] // 8 for a in ar_srcs]
    ar_lands = [jax.ShapeDtypeStruct((8,) + a.shape[:-2] + (r, a.shape[-1]), F32) for a, r in zip(ar_srcs, ar_rows)]
    ar_ss, ar_rs, ar_srcs, ar_got, ar_token = _split_start(
        "allreduce_scatter_start", ar_srcs, ar_lands, _ar_scatter_copies(ar_rows), 7 * len(ar_srcs), [])

    pss, rcv = _chip_scatter_wait(send_sems, recv_sems, pss, lands, ar_token)
    halves = [_chip_sum(p, r, me_arr, "grad_chip_sum_" + n) for n, p, r in zip(sharded, pss, rcv)]
    others = _pair_swap(halves)
    ar_srcs, ar_got = _split_wait("allreduce_scatter_wait", ar_ss, ar_rs, ar_srcs, ar_got, _ar_scatter_copies(ar_rows),
                                  [others[0]])
    ar_red = _ar_sum(ar_srcs, ar_got, ar_rows)
    ag_ss, ag_rs, ar_red, ag_full, ag_token = _split_start(
        "allreduce_gather_start", ar_red, [jax.ShapeDtypeStruct(a.shape, F32) for a in ar_srcs],
        _ar_gather_copies(ar_rows), 7 * len(ar_red), [])
    G_out, D_out, M_out, V_out = {}, {}, {}, {}
    for n, g_own, g_other in zip(sharded, halves, others):
        g_, d_, m_, v_ = _adamw_sharded(W[n][0], g_own, g_other, M[n][0], V[n][0], c_arr, ag_token, "adamw_" + n)
        G_out[n], D_out[n], M_out[n], V_out[n] = g_[None], d_[None], m_[None], v_[None]
    ar_red, ag_full = _split_wait("allreduce_gather_wait", ag_ss, ag_rs, ar_red, ag_full, _ar_gather_copies(ar_rows),
                                  [D_out[n] for n in sharded])
    me8 = 2 * chip + lax.axis_index("c")
    r_vecs, r_a, r_bc = [lax.dynamic_update_slice_in_dim(f, r, me8 * rw, axis=f.ndim - 2)
                         for f, r, rw in zip(ag_full, ar_red, ar_rows)]
    r_vecs = r_vecs.reshape(1, -1)
    loss = r_vecs[0, loss_at]
    outs4 = _adamw_small(r_vecs, r_a, r_bc, [Wv[n] for n in small],
                         [view[n](M[n]) for n in small], [view[n](V[n]) for n in small])
    for store, o in zip((G_out, D_out, M_out, V_out), outs4):
        store.update({n: back[n](t) for n, t in zip(small, o)})

    return (loss, grad_x[None], *[G_out[n] for n in names], *[D_out[n] for n in names],
            *[M_out[n] for n in names], *[V_out[n] for n in names])
```

```python
import functools
import math

import jax
import jax.numpy as jnp
from jax import lax
from jax.experimental import pallas as pl
from jax.experimental.pallas import tpu as pltpu

F32 = jnp.float32
BF16 = jnp.bfloat16
HI = lax.Precision.HIGHEST
MESH = pl.DeviceIdType.MESH

EPS = 1e-6
S5_GROUP = 16
S5_STATE = 64
GLA_HK = 128
GLA_HV = 256
GLA_RANK = 16
GLA_TAU = 16.0
GLA_CHUNK = 64
GLA_STEP_CHUNKS = 8
LANES = 128
SUBLANES = 8
S5_COLS = 128
S5_LANES = (S5_COLS // S5_GROUP) * S5_STATE
S5_TIME_BLOCK = 1024
ROW_TILE = 512

ADAM_LR = 0.001
ADAM_B1 = 0.9
ADAM_B2 = 0.999
ADAM_EPS = 1e-08
ADAM_WD = 0.01
ADAM_STEP = 10

GELU_K = math.sqrt(2.0 / math.pi)
GELU_C = 0.044715


def _blk(n, pref, unit=LANES):
    best = None
    b = unit
    while b <= min(n, pref):
        if n % b == 0:
            best = b
        b += unit
    return best if best is not None else n


def _dot(a, b, dn=(((1,), (0,)), ((), ()))):
    return lax.dot_general(a.astype(BF16), b.astype(BF16), dn, preferred_element_type=F32)


def _dot_hi(a, b, dn=(((1,), (0,)), ((), ()))):
    return lax.dot_general(a, b, dn, precision=HI, preferred_element_type=F32)


NN = (((1,), (0,)), ((), ()))
NT = (((1,), (1,)), ((), ()))
TN = (((0,), (0,)), ((), ()))


def _sigmoid(x):
    return 1.0 / (1.0 + jnp.exp(-x))


def _gelu(y):
    return 0.5 * y * (1.0 + jnp.tanh(GELU_K * (y + GELU_C * y * y * y)))


def _gelu_grad(y):
    th = jnp.tanh(GELU_K * (y + GELU_C * y * y * y))
    return 0.5 * (1.0 + th) + 0.5 * y * (1.0 - th * th) * GELU_K * (1.0 + 3.0 * GELU_C * y * y)


def _mm(a, b, *, name, ta=False, tb=False, out_dtype=F32, bm=1024, bn=1024, bk=2048, after=()):
    if ta:
        K, M = a.shape
    else:
        M, K = a.shape
    if tb:
        N, K2 = b.shape
    else:
        K2, N = b.shape
    assert K == K2, (a.shape, b.shape, ta, tb)
    bm, bn, bk = _blk(M, bm), _blk(N, bn), _blk(K, bk)
    nk = K // bk
    dn = (((0 if ta else 1,), (1 if tb else 0,)), ((), ()))

    def body(a_ref, b_ref, *rest):
        o_ref = rest[len(after)]
        if nk == 1:
            o_ref[...] = _dot(a_ref[...], b_ref[...], dn).astype(out_dtype)
            return
        acc_ref = rest[len(after) + 1]
        k = pl.program_id(2)

        @pl.when(k == 0)
        def _():
            acc_ref[...] = jnp.zeros_like(acc_ref)

        acc_ref[...] += _dot(a_ref[...], b_ref[...], dn)

        @pl.when(k == nk - 1)
        def _():
            o_ref[...] = acc_ref[...].astype(out_dtype)

    a_spec = pl.BlockSpec((bk, bm), lambda i, j, k: (k, i)) if ta else pl.BlockSpec((bm, bk), lambda i, j, k: (i, k))
    b_spec = pl.BlockSpec((bn, bk), lambda i, j, k: (j, k)) if tb else pl.BlockSpec((bk, bn), lambda i, j, k: (k, j))
    return pl.pallas_call(
        body,
        name=name,
        grid=(M // bm, N // bn, nk),
        in_specs=[a_spec, b_spec] + [pl.BlockSpec(memory_space=pl.ANY)] * len(after),
        out_specs=pl.BlockSpec((bm, bn), lambda i, j, k: (i, j)),
        out_shape=jax.ShapeDtypeStruct((M, N), out_dtype),
        scratch_shapes=[pltpu.VMEM((bm, bn), F32)] if nk > 1 else [],
        compiler_params=pltpu.CompilerParams(dimension_semantics=("parallel", "parallel", "arbitrary")),
    )(a, b, *after)


def _in_proj(h, w_main, w_low, after):
    M, K = h.shape
    N = w_main.shape[1]
    bm, bn = _blk(M, 1024), _blk(N, 1024)

    def body(h_ref, w_ref, wl_ref, _after_ref, o_ref, ol_ref):
        hv = h_ref[...]
        o_ref[...] = _dot(hv, w_ref[...])

        @pl.when(pl.program_id(1) == 0)
        def _():
            ol_ref[...] = _dot(hv, wl_ref[...])

    return pl.pallas_call(
        body, name="in_proj", grid=(M // bm, N // bn),
        in_specs=[pl.BlockSpec((bm, K), lambda i, j: (i, 0)), pl.BlockSpec((K, bn), lambda i, j: (0, j)),
                  pl.BlockSpec((K, LANES), lambda i, j: (0, 0)), pl.BlockSpec(memory_space=pl.ANY)],
        out_specs=[pl.BlockSpec((bm, bn), lambda i, j: (i, j)), pl.BlockSpec((bm, LANES), lambda i, j: (i, 0))],
        out_shape=[jax.ShapeDtypeStruct((M, N), F32), jax.ShapeDtypeStruct((M, LANES), F32)],
        compiler_params=pltpu.CompilerParams(dimension_semantics=("parallel", "arbitrary")),
    )(h, w_main, w_low, after)


def _in_proj_dx(a1, a2, al, b, bl, after, *, bm=1024, bn=1024, bk=2048):
    M, K1 = a1.shape
    K2 = a2.shape[1]
    N = b.shape[0]
    bm, bn = _blk(M, bm), _blk(N, bn)
    bk = _blk(math.gcd(K1, K2), bk)
    nk1, nk = K1 // bk, (K1 + K2) // bk

    def body(a1_ref, a2_ref, al_ref, b_ref, bl_ref, _after_ref, o_ref, acc_ref):
        k = pl.program_id(2)

        @pl.when(k == 0)
        def _():
            acc_ref[...] = _dot(al_ref[...], bl_ref[...], NT)

        @pl.when(k < nk1)
        def _():
            acc_ref[...] += _dot(a1_ref[...], b_ref[...], NT)

        @pl.when(k >= nk1)
        def _():
            acc_ref[...] += _dot(a2_ref[...], b_ref[...], NT)

        @pl.when(k == nk - 1)
        def _():
            o_ref[...] = acc_ref[...]

    return pl.pallas_call(
        body, name="in_proj_dx", grid=(M // bm, N // bn, nk),
        in_specs=[pl.BlockSpec((bm, bk), lambda i, j, k: (i, jnp.minimum(k, nk1 - 1))),
                  pl.BlockSpec((bm, bk), lambda i, j, k: (i, jnp.maximum(k - nk1, 0))),
                  pl.BlockSpec((bm, LANES), lambda i, j, k: (i, 0)),
                  pl.BlockSpec((bn, bk), lambda i, j, k: (j, k)),
                  pl.BlockSpec((bn, LANES), lambda i, j, k: (j, 0)),
                  pl.BlockSpec(memory_space=pl.ANY)],
        out_specs=pl.BlockSpec((bm, bn), lambda i, j, k: (i, j)),
        out_shape=jax.ShapeDtypeStruct((M, N), F32),
        scratch_shapes=[pltpu.VMEM((bm, bn), F32)],
        compiler_params=pltpu.CompilerParams(dimension_semantics=("parallel", "parallel", "arbitrary")),
    )(a1, a2, al, b, bl, after)


def _in_proj_dw(a, b1, b2, bl, *, bm=1024, bn=1024, bk=2048):
    K, M = a.shape
    N1, N2 = b1.shape[1], b2.shape[1]
    bm, bk = _blk(M, bm), _blk(K, bk)
    bn = _blk(math.gcd(N1, N2), bn)
    nj1, nj = N1 // bn, (N1 + N2) // bn
    nk = K // bk

    def body(a_ref, b1_ref, b2_ref, bl_ref, o_ref, ol_ref, acc_ref, accl_ref):
        j = pl.program_id(1)
        k = pl.program_id(2)

        @pl.when(k == 0)
        def _():
            acc_ref[...] = jnp.zeros_like(acc_ref)

        @pl.when(j < nj1)
        def _():
            acc_ref[...] += _dot(a_ref[...], b1_ref[...], TN)

        @pl.when(j >= nj1)
        def _():
            acc_ref[...] += _dot(a_ref[...], b2_ref[...], TN)

        @pl.when(k == nk - 1)
        def _():
            o_ref[...] = acc_ref[...].astype(BF16)

        @pl.when(j == 0)
        def _():
            low = _dot(a_ref[...], bl_ref[...], TN)

            @pl.when(k == 0)
            def _():
                accl_ref[...] = low

            @pl.when(k > 0)
            def _():
                accl_ref[...] += low

            @pl.when(k == nk - 1)
            def _():
                ol_ref[...] = accl_ref[...].astype(BF16)

    return pl.pallas_call(
        body, name="in_proj_dw", grid=(M // bm, nj, nk),
        in_specs=[pl.BlockSpec((bk, bm), lambda i, j, k: (k, i)),
                  pl.BlockSpec((bk, bn), lambda i, j, k: (jnp.where(j < nj1, k, nk - 1), jnp.minimum(j, nj1 - 1))),
                  pl.BlockSpec((bk, bn), lambda i, j, k: (jnp.where(j >= nj1, k, 0), jnp.maximum(j - nj1, 0))),
                  pl.BlockSpec((bk, LANES), lambda i, j, k: (jnp.where(j == 0, k, nk - 1), 0))],
        out_specs=[pl.BlockSpec((bm, bn), lambda i, j, k: (i, j)), pl.BlockSpec((bm, LANES), lambda i, j, k: (i, 0))],
        out_shape=[jax.ShapeDtypeStruct((M, N1 + N2), BF16), jax.ShapeDtypeStruct((M, LANES), BF16)],
        scratch_shapes=[pltpu.VMEM((bm, bn), F32), pltpu.VMEM((bm, LANES), F32)],
        compiler_params=pltpu.CompilerParams(dimension_semantics=("parallel", "arbitrary", "arbitrary")),
    )(a, b1, b2, bl)


def _assemble_w_in(g, nsh, wm):
    _, R, nshp = g.shape
    nb_in = nshp // LANES
    nb_main = wm // LANES
    tr = _blk(R, 512, 2 * SUBLANES)
    plan = []
    for b in range(nb_main + 1):
        terms = []
        for k in range(g.shape[0]):
            for i in range(nb_in):
                delta = nsh * k + LANES * i - LANES * b
                lo, hi = max(0, -delta), min(LANES, LANES - delta, nsh - LANES * i)
                if abs(delta) < LANES and hi > lo:
                    terms.append((k, i, delta))
        plan.append(terms)
    deltas = sorted({d for terms in plan for _, _, d in terms if d})

    def body(g_ref, wm_ref, wl_ref):
        src = _iota2((LANES, LANES), 0)
        dst = _iota2((LANES, LANES), 1)
        shift = {d: (dst - src == d).astype(BF16) for d in deltas}
        for b, terms in enumerate(plan):
            acc = None
            for k, i, d in terms:
                blk = g_ref[k, :, LANES * i:LANES * (i + 1)]
                t = _dot(blk, shift[d]) if d else blk.astype(F32)
                acc = t if acc is None else acc + t
            if b < nb_main:
                wm_ref[:, LANES * b:LANES * (b + 1)] = acc.astype(BF16)
            else:
                wl_ref[...] = acc.astype(BF16)

    return pl.pallas_call(
        body, name="assemble_w_in", grid=(R // tr,),
        in_specs=[pl.BlockSpec((g.shape[0], tr, nshp), lambda r: (0, r, 0))],
        out_specs=[pl.BlockSpec((tr, wm), lambda r: (r, 0)), pl.BlockSpec((tr, LANES), lambda r: (r, 0))],
        out_shape=[jax.ShapeDtypeStruct((R, wm), BF16), jax.ShapeDtypeStruct((R, LANES), BF16)],
        compiler_params=pltpu.CompilerParams(dimension_semantics=("parallel",)),
    )(g)


def _split_w_in_grad(g_main, g_low, nsh):
    R, wm = g_main.shape
    nb_main = wm // LANES
    nb_out = -(-nsh // LANES)
    tr = _blk(R, 512, 2 * SUBLANES)
    plan = {}
    for k in range(4):
        for i in range(nb_out):
            width = min(LANES, nsh - LANES * i)
            terms = []
            for b in range(nb_main + 1):
                delta = LANES * b - (nsh * k + LANES * i)
                lo, hi = max(0, delta), min(width, LANES + delta)
                if abs(delta) < LANES and hi > lo:
                    terms.append((b, delta))
            plan[k, i] = (width, terms)
    deltas = sorted({d for _, terms in plan.values() for _, d in terms if d})

    def body(gm_ref, gl_ref, o_ref):
        src = _iota2((LANES, LANES), 0)
        dst = _iota2((LANES, LANES), 1)
        shift = {d: (dst - src == d).astype(BF16) for d in deltas}
        for (k, i), (width, terms) in plan.items():
            acc = None
            for b, d in terms:
                blk = gm_ref[:, LANES * b:LANES * (b + 1)] if b < nb_main else gl_ref[...]
                t = _dot(blk, shift[d]) if d else blk.astype(F32)
                acc = t if acc is None else acc + t
            o_ref[k, :, LANES * i:LANES * i + width] = acc[:, :width].astype(BF16)

    return pl.pallas_call(
        body, name="split_w_in_grad", grid=(R // tr,),
        in_specs=[pl.BlockSpec((tr, wm), lambda r: (r, 0)), pl.BlockSpec((tr, LANES), lambda r: (r, 0))],
        out_specs=pl.BlockSpec((4, tr, nsh), lambda r: (0, r, 0)),
        out_shape=jax.ShapeDtypeStruct((4, R, nsh), BF16),
        compiler_params=pltpu.CompilerParams(dimension_semantics=("parallel",)),
    )(g_main, g_low)


def _prenorm_fwd(x, w, after):
    L, D = x.shape
    tr = _blk(L, ROW_TILE, SUBLANES)

    def body(x_ref, w_ref, _after_ref, h_ref):
        xv = x_ref[...]
        r = lax.rsqrt(jnp.mean(xv * xv, axis=-1, keepdims=True) + EPS)
        h_ref[...] = (xv * r * w_ref[...]).astype(BF16)

    return pl.pallas_call(
        body, name="prenorm_fwd", grid=(L // tr,),
        in_specs=[pl.BlockSpec((tr, D), lambda i: (i, 0)), pl.BlockSpec((1, D), lambda i: (0, 0)),
                  pl.BlockSpec(memory_space=pl.ANY)],
        out_specs=pl.BlockSpec((tr, D), lambda i: (i, 0)),
        out_shape=jax.ShapeDtypeStruct((L, D), BF16),
        compiler_params=pltpu.CompilerParams(dimension_semantics=("parallel",)),
    )(x, w, after)


def _post_fwd_bwd(mixed, x, target, w):
    L, D = x.shape
    tr = _blk(L, ROW_TILE, SUBLANES)
    nsteps = L // tr

    def body(mx_ref, x_ref, t_ref, w_ref, loss_ref, dm_ref, dout_ref, gw_ref, acc_ref):
        i = pl.program_id(0)

        @pl.when(i == 0)
        def _():
            acc_ref[...] = jnp.zeros_like(acc_ref)
            gw_ref[...] = jnp.zeros_like(gw_ref)

        mx = mx_ref[...]
        wv = w_ref[...]
        r = lax.rsqrt(jnp.mean(mx * mx, axis=-1, keepdims=True) + EPS)
        n = mx * r
        err = x_ref[...] + n * wv - t_ref[...]
        acc_ref[...] += jnp.sum(err * err, axis=0, keepdims=True)
        dout = err * (1.0 / D)
        dout_ref[...] = dout
        gw_ref[...] += jnp.sum(dout * n, axis=0, keepdims=True)
        dn = dout * wv
        dm_ref[...] = (r * (dn - n * jnp.mean(dn * n, axis=-1, keepdims=True))).astype(BF16)

        @pl.when(i == nsteps - 1)
        def _():
            loss_ref[...] = jnp.sum(acc_ref[...], axis=-1, keepdims=True) * (0.5 / D)

    row = pl.BlockSpec((tr, D), lambda i: (i, 0))
    vec = pl.BlockSpec((1, D), lambda i: (0, 0))
    return pl.pallas_call(
        body, name="post_fwd_bwd", grid=(nsteps,),
        in_specs=[row, row, row, vec],
        out_specs=[pl.BlockSpec((1, 1), lambda i: (0, 0)), row, row, vec],
        out_shape=[jax.ShapeDtypeStruct((1, 1), F32), jax.ShapeDtypeStruct((L, D), BF16),
                   jax.ShapeDtypeStruct((L, D), F32), jax.ShapeDtypeStruct((1, D), F32)],
        scratch_shapes=[pltpu.VMEM((1, D), F32)],
        compiler_params=pltpu.CompilerParams(dimension_semantics=("arbitrary",)),
    )(mixed, x, target, w)


def _prenorm_bwd(x, dh, dout, w):
    L, D = x.shape
    tr = _blk(L, ROW_TILE, SUBLANES)

    def body(x_ref, a_ref, dout_ref, w_ref, gx_ref, gw_ref):
        i = pl.program_id(0)

        @pl.when(i == 0)
        def _():
            gw_ref[...] = jnp.zeros_like(gw_ref)

        xv = x_ref[...]
        r = lax.rsqrt(jnp.mean(xv * xv, axis=-1, keepdims=True) + EPS)
        n = xv * r
        dh = a_ref[...]
        gw_ref[...] += jnp.sum(dh * n, axis=0, keepdims=True)
        dn = dh * w_ref[...]
        gx_ref[...] = dout_ref[...] + r * (dn - n * jnp.mean(dn * n, axis=-1, keepdims=True))

    row = pl.BlockSpec((tr, D), lambda i: (i, 0))
    vec = pl.BlockSpec((1, D), lambda i: (0, 0))
    return pl.pallas_call(
        body, name="prenorm_bwd", grid=(L // tr,),
        in_specs=[row, row, row, vec],
        out_specs=[row, vec],
        out_shape=[jax.ShapeDtypeStruct((L, D), F32), jax.ShapeDtypeStruct((1, D), F32)],
        compiler_params=pltpu.CompilerParams(dimension_semantics=("arbitrary",)),
    )(x, dh, dout, w)


def _s5_disc(a_re_raw, a_im, dt):
    a_re = jnp.minimum(a_re_raw, -1e-4)
    mag = jnp.exp(a_re * dt)
    ph = a_im * dt
    ab_re = mag * jnp.cos(ph)
    ab_im = mag * jnp.sin(ph)
    inv_n = 1.0 / (a_re * a_re + a_im * a_im)
    ia_re = a_re * inv_n
    ia_im = -a_im * inv_n
    n_re = ab_re - 1.0
    f_re = n_re * ia_re - ab_im * ia_im
    f_im = n_re * ia_im + ab_im * ia_re
    return a_re, ab_re, ab_im, f_re, f_im, ia_re, ia_im


def _iota2(shape, dim):
    return lax.broadcasted_iota(jnp.int32, shape, dim)


def _group_mask(rows, rows_per_group):
    shift = rows_per_group.bit_length() - 1
    return (_iota2((rows, S5_LANES), 0) >> shift) == (_iota2((rows, S5_LANES), 1) >> (S5_STATE.bit_length() - 1))


def _lane_tiler(dtype):
    return ((_iota2((S5_STATE, S5_LANES), 1) & (S5_STATE - 1)) == _iota2((S5_STATE, S5_LANES), 0)).astype(dtype)


def _row_to_col(row, n):
    eye = (_iota2((n, n), 0) == _iota2((n, n), 1)).astype(F32)
    return jnp.sum(eye * row, axis=1, keepdims=True)


def _group_repeat(G):
    return ((_iota2((G * S5_GROUP, G), 0) >> (S5_GROUP.bit_length() - 1)) == _iota2((G * S5_GROUP, G), 1)).astype(F32)


S5_TABS = 18


def _s5_prep_fwd(a_re, a_im, log_dt, b_re, b_im, c_re, c_im, after, seg):
    G, P = a_re.shape
    nb = G * S5_GROUP // S5_COLS
    g8 = S5_COLS // S5_GROUP
    assert seg & (seg - 1) == 0, seg

    def body(are_ref, aim_ref, ldt_ref, bre_ref, bim_ref, cre_ref, cim_ref, _after_ref,
             bbre_ref, bbim_ref, ctre_ref, ctim_ref, tab_ref, pt_ref):
        dt = jnp.exp(_row_to_col(ldt_ref[...], G))
        _, ab_re, ab_im, f_re, f_im, _, _ = _s5_disc(are_ref[...], aim_ref[...], dt)
        rep = _group_repeat(G)
        fx_re = _dot_hi(rep, f_re)
        fx_im = _dot_hi(rep, f_im)
        br, bi = bre_ref[...], bim_ref[...]
        bb_re = fx_re * br - fx_im * bi
        bb_im = fx_re * bi + fx_im * br
        tile_bf = _lane_tiler(BF16)
        mask = _group_mask(S5_COLS, S5_GROUP)
        for jb in range(nb):
            rs = slice(jb * S5_COLS, (jb + 1) * S5_COLS)
            for src, dst in ((bb_re[rs], bbre_ref), (bb_im[rs], bbim_ref), (cre_ref[rs, :], ctre_ref), (cim_ref[rs, :], ctim_ref)):
                dst[jb] = jnp.where(mask, _dot(src, tile_bf), 0.0).astype(BF16)

        tile_f = _lane_tiler(F32)
        mask8 = _group_mask(g8, 1)
        row = _iota2((SUBLANES, S5_LANES), 0)
        slab = (SUBLANES, S5_LANES)
        cmul = lambda p, q: (p[0] * q[0] - p[1] * q[1], p[0] * q[1] + p[1] * q[0])
        for jb in range(nb):
            gs = slice(jb * g8, (jb + 1) * g8)

            def lanes(m):
                v = jnp.sum(jnp.where(mask8, _dot_hi(m[gs], tile_f), 0.0), axis=0, keepdims=True)
                return jnp.broadcast_to(v, slab)

            a1 = (lanes(ab_re), lanes(ab_im))
            tab_ref[jb, 0], tab_ref[jb, 1] = a1

            def powers(k, p):
                s_re = s_im = jnp.zeros(slab, F32)
                for r in range(SUBLANES):
                    s_re = jnp.where(row == r, p[0], s_re)
                    s_im = jnp.where(row == r, p[1], s_im)
                    p = cmul(p, a1)
                pt_ref[jb, 0, _slab(k), :] = s_re
                pt_ref[jb, 1, _slab(k), :] = s_im
                return p

            lax.fori_loop(0, seg // SUBLANES, powers, a1)
            aseg = a1
            for _ in range(seg.bit_length() - 1):
                aseg = cmul(aseg, aseg)
            pw = [aseg]
            for _ in range(1, SUBLANES):
                pw.append(cmul(pw[-1], aseg))
            for lvl, k in enumerate((1, 2, 4)):
                tab_ref[jb, 2 + 2 * lvl] = jnp.where(row >= k, pw[k - 1][0], 0.0)
                tab_ref[jb, 3 + 2 * lvl] = jnp.where(row >= k, pw[k - 1][1], 0.0)
                tab_ref[jb, 10 + 2 * lvl] = jnp.where(row < SUBLANES - k, pw[k - 1][0], 0.0)
                tab_ref[jb, 11 + 2 * lvl] = jnp.where(row < SUBLANES - k, -pw[k - 1][1], 0.0)
            f_r = f_i = r_r = r_i = jnp.zeros(slab, F32)
            for i in range(SUBLANES):
                f_r = jnp.where(row == i, pw[i][0], f_r)
                f_i = jnp.where(row == i, pw[i][1], f_i)
                r_r = jnp.where(row == i, pw[SUBLANES - 1 - i][0], r_r)
                r_i = jnp.where(row == i, -pw[SUBLANES - 1 - i][1], r_i)
            tab_ref[jb, 8] = f_r
            tab_ref[jb, 9] = f_i
            tab_ref[jb, 16] = r_r
            tab_ref[jb, 17] = r_i

    vm = pl.BlockSpec(memory_space=pltpu.VMEM)
    bd = jax.ShapeDtypeStruct((nb, S5_COLS, S5_LANES), BF16)
    return pl.pallas_call(
        body, name="s5_prep_fwd",
        in_specs=[vm] * 7 + [pl.BlockSpec(memory_space=pl.ANY)], out_specs=[vm] * 6,
        out_shape=[bd, bd, bd, bd, jax.ShapeDtypeStruct((nb, S5_TABS, SUBLANES, S5_LANES), F32),
                   jax.ShapeDtypeStruct((nb, 2, seg, S5_LANES), F32)],
    )(a_re, a_im, log_dt, b_re, b_im, c_re, c_im, after)


def _s5_prep_bwd(a_re, a_im, log_dt, b_re, b_im, gbb_re, gbb_im, gct_re, gct_im, gab_re, gab_im):
    G, P = a_re.shape
    nb = G * S5_GROUP // S5_COLS
    g8 = S5_COLS // S5_GROUP

    def body(are_ref, aim_ref, ldt_ref, bre_ref, bim_ref, gbr_ref, gbi_ref, gcr_ref, gci_ref, gar_ref, gai_ref,
             o_a, o_bc, o_ldt):
        dt = jnp.exp(_row_to_col(ldt_ref[...], G))
        a_raw = are_ref[...]
        a_imv = aim_ref[...]
        a_re_c, ab_re, ab_im, f_re, f_im, ia_re, ia_im = _s5_disc(a_raw, a_imv, dt)
        tile_f = _lane_tiler(F32)
        mask = _group_mask(S5_COLS, S5_GROUP)
        mask8 = _group_mask(g8, 1)
        for jb in range(nb):
            rs = slice(jb * S5_COLS, (jb + 1) * S5_COLS)
            gs = slice(jb * g8, (jb + 1) * g8)
            ls = slice(jb * S5_LANES, (jb + 1) * S5_LANES)
            for k, src in enumerate((gbr_ref, gbi_ref, gcr_ref, gci_ref)):
                o_bc[k, rs, :] = _dot_hi(jnp.where(mask, src[jb], 0.0), tile_f, NT)
            for k, src in enumerate((gar_ref, gai_ref)):
                o_a[k, gs, :] = _dot_hi(jnp.where(mask8, src[:, ls], 0.0), tile_f, NT)
        rep = _group_repeat(G)
        fx_re = _dot_hi(rep, f_re)
        fx_im = _dot_hi(rep, f_im)
        gbr, gbi = o_bc[0], o_bc[1]
        br, bi = bre_ref[...], bim_ref[...]
        o_bc[0] = fx_re * gbr + fx_im * gbi
        o_bc[1] = fx_re * gbi - fx_im * gbr
        gf_re = _dot_hi(rep, br * gbr + bi * gbi, TN)
        gf_im = _dot_hi(rep, br * gbi - bi * gbr, TN)
        gab_r = o_a[0] + ia_re * gf_re + ia_im * gf_im
        gab_i = o_a[1] + ia_re * gf_im - ia_im * gf_re
        q_re = f_re * ia_re - f_im * ia_im
        q_im = f_re * ia_im + f_im * ia_re
        ga_re = -(q_re * gf_re + q_im * gf_im)
        ga_im = -(q_re * gf_im - q_im * gf_re)
        gth_re = ab_re * gab_r + ab_im * gab_i
        gth_im = ab_re * gab_i - ab_im * gab_r
        ga_re = ga_re + dt * gth_re
        ga_im = ga_im + dt * gth_im
        gdt = jnp.sum(a_re_c * gth_re + a_imv * gth_im, axis=-1, keepdims=True)
        eye = (_iota2((G, G), 0) == _iota2((G, G), 1)).astype(F32)
        o_ldt[...] = jnp.sum(eye * (gdt * dt), axis=0, keepdims=True)
        slope = jnp.where(a_raw < -1e-4, 1.0, jnp.where(a_raw == -1e-4, 0.5, 0.0))
        o_a[0] = ga_re * slope
        o_a[1] = ga_im

    vm = pl.BlockSpec(memory_space=pltpu.VMEM)
    return pl.pallas_call(
        body, name="s5_prep_bwd",
        in_specs=[vm] * 11, out_specs=[vm] * 3,
        out_shape=[jax.ShapeDtypeStruct((2, G, P), F32), jax.ShapeDtypeStruct((4, G * S5_GROUP, P), F32),
                   jax.ShapeDtypeStruct((1, G), F32)],
    )(a_re, a_im, log_dt, b_re, b_im, gbb_re, gbb_im, gct_re, gct_im, gab_re, gab_im)


def _scan8(xr, xi, tab_ref, base, shifts):
    for lvl, sh in enumerate(shifts):
        mr = tab_ref[0, base + 2 * lvl]
        mi = tab_ref[0, base + 2 * lvl + 1]
        ar = pltpu.roll(xr, sh, 0)
        ai = pltpu.roll(xi, sh, 0)
        xr, xi = xr + mr * ar - mi * ai, xi + mr * ai + mi * ar
    return xr, xi


def _to_segments(src_ref, dst_ref, seg):
    for i in range(seg):
        dst_ref[i * SUBLANES:(i + 1) * SUBLANES, :] = src_ref[pl.ds(i, SUBLANES, stride=seg), :]


def _from_segments(src_ref, dst_ref, seg):
    for i in range(seg):
        dst_ref[pl.ds(i, SUBLANES, stride=seg), :] = src_ref[i * SUBLANES:(i + 1) * SUBLANES, :]


def _slab(i):
    return pl.ds(pl.multiple_of(i * SUBLANES, SUBLANES), SUBLANES)


def _s5_scan_fwd(proj_main, bbd_re, bbd_im, cbd_re, cbd_im, dvec, tab, ptab, DS):
    L = proj_main.shape[0]
    nb = DS // S5_COLS
    tb = _blk(L, S5_TIME_BLOCK, SUBLANES)
    nt = L // tb
    seg = tb // SUBLANES

    def body(u_ref, bre_ref, bim_ref, cre_ref, cim_ref, d_ref, tab_ref, pt_ref, y_ref, sre_ref, sim_ref,
             up_ref, yp_ref, car_ref):
        t = pl.program_id(1)

        @pl.when(t == 0)
        def _():
            car_ref[...] = jnp.zeros_like(car_ref)

        _to_segments(u_ref, up_ref, seg)
        up = up_ref[...]
        sre_ref[...] = _dot(up, bre_ref[0])
        sim_ref[...] = _dot(up, bim_ref[0])
        ar, ai = tab_ref[0, 0], tab_ref[0, 1]

        def pass1(i, x):
            xr = ar * x[0] - ai * x[1] + sre_ref[_slab(i), :]
            xi = ar * x[1] + ai * x[0] + sim_ref[_slab(i), :]
            sre_ref[_slab(i), :] = xr
            sim_ref[_slab(i), :] = xi
            return xr, xi

        zero = jnp.zeros((SUBLANES, S5_LANES), F32)
        er, ei = lax.fori_loop(0, seg, pass1, (zero, zero))
        cin_r, cin_i = car_ref[0], car_ref[1]
        sr, si = _scan8(er, ei, tab_ref, 2, (1, 2, 4))
        pr, pi = tab_ref[0, 8], tab_ref[0, 9]
        sr, si = sr + pr * cin_r - pi * cin_i, si + pr * cin_i + pi * cin_r
        row0 = _iota2((SUBLANES, S5_LANES), 0) == 0
        cr = jnp.where(row0, cin_r, pltpu.roll(sr, 1, 0))
        ci = jnp.where(row0, cin_i, pltpu.roll(si, 1, 0))
        car_ref[0] = jnp.broadcast_to(sr[SUBLANES - 1:SUBLANES, :], sr.shape)
        car_ref[1] = jnp.broadcast_to(si[SUBLANES - 1:SUBLANES, :], si.shape)

        def pass2(i, _):
            qr, qi = pt_ref[0, 0, pl.ds(i, 1), :], pt_ref[0, 1, pl.ds(i, 1), :]
            sre_ref[_slab(i), :] += qr * cr - qi * ci
            sim_ref[_slab(i), :] += qr * ci + qi * cr
            return 0

        lax.fori_loop(0, seg, pass2, 0, unroll=4)
        yp_ref[...] = _dot(sre_ref[...], cre_ref[0], NT) - _dot(sim_ref[...], cim_ref[0], NT) + d_ref[...] * up
        _from_segments(yp_ref, y_ref, seg)

    return pl.pallas_call(
        body, name="s5_scan_fwd", grid=(nb, nt),
        in_specs=[
            pl.BlockSpec((tb, S5_COLS), lambda j, t: (t, j)),
            pl.BlockSpec((1, S5_COLS, S5_LANES), lambda j, t: (j, 0, 0)),
            pl.BlockSpec((1, S5_COLS, S5_LANES), lambda j, t: (j, 0, 0)),
            pl.BlockSpec((1, S5_COLS, S5_LANES), lambda j, t: (j, 0, 0)),
            pl.BlockSpec((1, S5_COLS, S5_LANES), lambda j, t: (j, 0, 0)),
            pl.BlockSpec((1, S5_COLS), lambda j, t: (0, j)),
            pl.BlockSpec((1, S5_TABS, SUBLANES, S5_LANES), lambda j, t: (j, 0, 0, 0)),
            pl.BlockSpec((1, 2, seg, S5_LANES), lambda j, t: (j, 0, 0, 0)),
        ],
        out_specs=[
            pl.BlockSpec((tb, S5_COLS), lambda j, t: (t, j)),
            pl.BlockSpec((tb, S5_LANES), lambda j, t: (t, j)),
            pl.BlockSpec((tb, S5_LANES), lambda j, t: (t, j)),
        ],
        out_shape=[jax.ShapeDtypeStruct((L, DS), F32),
                   jax.ShapeDtypeStruct((L, nb * S5_LANES), F32),
                   jax.ShapeDtypeStruct((L, nb * S5_LANES), F32)],
        scratch_shapes=[pltpu.VMEM((tb, S5_COLS), F32), pltpu.VMEM((tb, S5_COLS), F32),
                        pltpu.VMEM((2, SUBLANES, S5_LANES), F32)],
        compiler_params=pltpu.CompilerParams(dimension_semantics=("parallel", "arbitrary")),
    )(proj_main, bbd_re, bbd_im, cbd_re, cbd_im, dvec, tab, ptab)


def _s5_scan_bwd(dy, proj_main, s_re, s_im, bbd_re, bbd_im, cbd_re, cbd_im, dvec, tab, ptab, d_s5, DS):
    L = proj_main.shape[0]
    nb = DS // S5_COLS
    tb = _blk(L, S5_TIME_BLOCK, SUBLANES)
    nt = L // tb
    seg = tb // SUBLANES
    tb8 = tb // SUBLANES

    def body(dy_ref, u_ref, sre_ref, sim_ref, pre_ref, pim_ref, bre_ref, bim_ref, cre_ref, cim_ref, d_ref, tab_ref, pt_ref,
             _ds5_ref, du_ref, gd_ref, gcre_ref, gcim_ref, gbre_ref, gbim_ref, gare_ref, gaim_ref,
             lre_ref, lim_ref, up_ref, dyp_ref, dup_ref, duo_ref, car_ref):
        t = pl.program_id(1)

        @pl.when(t == 0)
        def _():
            car_ref[...] = jnp.zeros_like(car_ref)
            gd_ref[...] = jnp.zeros_like(gd_ref)
            gcre_ref[...] = jnp.zeros_like(gcre_ref)
            gcim_ref[...] = jnp.zeros_like(gcim_ref)
            gbre_ref[...] = jnp.zeros_like(gbre_ref)
            gbim_ref[...] = jnp.zeros_like(gbim_ref)
            gare_ref[...] = jnp.zeros_like(gare_ref)
            gaim_ref[...] = jnp.zeros_like(gaim_ref)

        _to_segments(dy_ref, dyp_ref, seg)
        _to_segments(u_ref, up_ref, seg)
        dyv = dyp_ref[...]
        u = up_ref[...]
        gd_ref[...] += jnp.sum(dyv * u, axis=0, keepdims=True)
        lre_ref[...] = _dot(dyv, cre_ref[0])
        lim_ref[...] = -_dot(dyv, cim_ref[0])
        gcre_ref[0] += _dot(dyv, sre_ref[...], TN)
        gcim_ref[0] -= _dot(dyv, sim_ref[...], TN)
        ar, ai = tab_ref[0, 0], -tab_ref[0, 1]

        def pass1(k, x):
            i = seg - 1 - k
            xr = ar * x[0] - ai * x[1] + lre_ref[_slab(i), :]
            xi = ar * x[1] + ai * x[0] + lim_ref[_slab(i), :]
            lre_ref[_slab(i), :] = xr
            lim_ref[_slab(i), :] = xi
            return xr, xi

        zero = jnp.zeros((SUBLANES, S5_LANES), F32)
        er, ei = lax.fori_loop(0, seg, pass1, (zero, zero))
        cin_r, cin_i = car_ref[0], car_ref[1]
        lr, li = _scan8(er, ei, tab_ref, 10, (7, 6, 4))
        pr, pi = tab_ref[0, 16], tab_ref[0, 17]
        lr, li = lr + pr * cin_r - pi * cin_i, li + pr * cin_i + pi * cin_r
        rows = _iota2((SUBLANES, S5_LANES), 0)
        cr = jnp.where(rows == SUBLANES - 1, cin_r, pltpu.roll(lr, SUBLANES - 1, 0))
        ci = jnp.where(rows == SUBLANES - 1, cin_i, pltpu.roll(li, SUBLANES - 1, 0))
        car_ref[0] = jnp.broadcast_to(lr[0:1, :], lr.shape)
        car_ref[1] = jnp.broadcast_to(li[0:1, :], li.shape)

        first = (t == nt - 1).astype(F32)
        head_re = jnp.broadcast_to(pre_ref[SUBLANES - 1:SUBLANES, :], zero.shape) * (1.0 - first)
        head_im = jnp.broadcast_to(pim_ref[SUBLANES - 1:SUBLANES, :], zero.shape) * (1.0 - first)
        last = _slab(seg - 1)
        sp0_re = jnp.where(rows == 0, head_re, pltpu.roll(sre_ref[last, :], 1, 0))
        sp0_im = jnp.where(rows == 0, head_im, pltpu.roll(sim_ref[last, :], 1, 0))

        def fix(i, acc, sp_re, sp_im):
            j = seg - 1 - i
            qr, qi = pt_ref[0, 0, pl.ds(j, 1), :], -pt_ref[0, 1, pl.ds(j, 1), :]
            xr = lre_ref[_slab(i), :] + qr * cr - qi * ci
            xi = lim_ref[_slab(i), :] + qr * ci + qi * cr
            lre_ref[_slab(i), :] = xr
            lim_ref[_slab(i), :] = xi
            return acc[0] + sp_re * xr + sp_im * xi, acc[1] + sp_re * xi - sp_im * xr

        def pass2(i, acc):
            prev = _slab(jnp.maximum(i - 1, 0))
            return fix(i, acc, sre_ref[prev, :], sim_ref[prev, :])

        acc_re, acc_im = lax.fori_loop(0, seg, pass2, (zero, zero), unroll=4)
        first_slab = _slab(0)
        d_re, d_im = sp0_re - sre_ref[first_slab, :], sp0_im - sim_ref[first_slab, :]
        x0r, x0i = lre_ref[first_slab, :], lim_ref[first_slab, :]
        acc_re = acc_re + d_re * x0r + d_im * x0i
        acc_im = acc_im + d_re * x0i - d_im * x0r
        gare_ref[...] += jnp.sum(acc_re, axis=0, keepdims=True)
        gaim_ref[...] += jnp.sum(acc_im, axis=0, keepdims=True)
        lre = lre_ref[...]
        lim = lim_ref[...]
        dup_ref[...] = dyv * d_ref[...] + _dot(lre, bre_ref[0], NT) + _dot(lim, bim_ref[0], NT)
        _from_segments(dup_ref, duo_ref, seg)
        du_ref[...] = duo_ref[...].astype(BF16)
        gbre_ref[0] += _dot(u, lre, TN)
        gbim_ref[0] += _dot(u, lim, TN)

    rt = lambda t: nt - 1 - t
    col = pl.BlockSpec((tb, S5_COLS), lambda j, t: (rt(t), j))
    st = pl.BlockSpec((tb, S5_LANES), lambda j, t: (rt(t), j))
    prev = pl.BlockSpec((SUBLANES, S5_LANES), lambda j, t: (jnp.maximum(rt(t) * tb8 - 1, 0), j))
    bmat = pl.BlockSpec((1, S5_COLS, S5_LANES), lambda j, t: (j, 0, 0))
    cmat = bmat
    return pl.pallas_call(
        body, name="s5_scan_bwd", grid=(nb, nt),
        in_specs=[col, col, st, st, prev, prev, bmat, bmat, cmat, cmat,
                  pl.BlockSpec((1, S5_COLS), lambda j, t: (0, j)),
                  pl.BlockSpec((1, S5_TABS, SUBLANES, S5_LANES), lambda j, t: (j, 0, 0, 0)),
                  pl.BlockSpec((1, 2, seg, S5_LANES), lambda j, t: (j, 0, 0, 0)),
                  pl.BlockSpec(memory_space=pl.ANY)],
        out_specs=[col, pl.BlockSpec((1, S5_COLS), lambda j, t: (0, j)), cmat, cmat, bmat, bmat,
                   pl.BlockSpec((1, S5_LANES), lambda j, t: (0, j)), pl.BlockSpec((1, S5_LANES), lambda j, t: (0, j))],
        input_output_aliases={13: 0},
        out_shape=[jax.ShapeDtypeStruct((L, 2 * DS), BF16), jax.ShapeDtypeStruct((1, DS), F32),
                   jax.ShapeDtypeStruct((nb, S5_COLS, S5_LANES), F32), jax.ShapeDtypeStruct((nb, S5_COLS, S5_LANES), F32),
                   jax.ShapeDtypeStruct((nb, S5_COLS, S5_LANES), F32), jax.ShapeDtypeStruct((nb, S5_COLS, S5_LANES), F32),
                   jax.ShapeDtypeStruct((1, nb * S5_LANES), F32), jax.ShapeDtypeStruct((1, nb * S5_LANES), F32)],
        scratch_shapes=[pltpu.VMEM((tb, S5_LANES), F32), pltpu.VMEM((tb, S5_LANES), F32)]
        + [pltpu.VMEM((tb, S5_COLS), F32)] * 4 + [pltpu.VMEM((2, SUBLANES, S5_LANES), F32)],
        compiler_params=pltpu.CompilerParams(dimension_semantics=("parallel", "arbitrary")),
    )(dy, proj_main, s_re, s_im, s_re, s_im, bbd_re, bbd_im, cbd_re, cbd_im, dvec, tab, ptab, d_s5)


def _s5_post_fwd(y_pre, proj_main, glu_w, glu_b, DS):
    L = y_pre.shape[0]
    tr = _blk(L, ROW_TILE, SUBLANES)

    def body(y_ref, z_ref, w_ref, b_ref, o_ref, t_ref):
        y1 = _gelu(y_ref[...])
        t = _dot(y1, w_ref[...]) + b_ref[...]
        t_ref[...] = t
        z = z_ref[...]
        o_ref[...] = (y1 * _sigmoid(t) * (z * _sigmoid(z))).astype(BF16)

    row = pl.BlockSpec((tr, DS), lambda i: (i, 0))
    return pl.pallas_call(
        body, name="s5_post_fwd", grid=(L // tr,),
        in_specs=[row, pl.BlockSpec((tr, DS), lambda i: (i, 1)), pl.BlockSpec((DS, DS), lambda i: (0, 0)),
                  pl.BlockSpec((1, DS), lambda i: (0, 0))],
        out_specs=[row, row],
        out_shape=[jax.ShapeDtypeStruct((L, 2 * DS), BF16), jax.ShapeDtypeStruct((L, DS), F32)],
        compiler_params=pltpu.CompilerParams(dimension_semantics=("parallel",)),
    )(y_pre, proj_main, glu_w, glu_b)


def _s5_post_bwd(d_ycat, y_pre, proj_main, t_pre, glu_w, DS):
    L = y_pre.shape[0]
    tr = _blk(L, ROW_TILE, SUBLANES)

    def body(dy_ref, y_ref, z_ref, t_ref, w_ref, dyp_ref, dz_ref, dt_ref, y1_ref, gb_ref):
        i = pl.program_id(0)

        @pl.when(i == 0)
        def _():
            gb_ref[...] = jnp.zeros_like(gb_ref)

        dy = dy_ref[...]
        yp = y_ref[...]
        z = z_ref[...]
        y1 = _gelu(yp)
        sg = _sigmoid(t_ref[...])
        sz = _sigmoid(z)
        c = y1 * sg
        d_c = dy * (z * sz)
        dz_ref[...] = (dy * c * (sz * (1.0 + z * (1.0 - sz)))).astype(BF16)
        d_t = d_c * y1 * sg * (1.0 - sg)
        gb_ref[...] += jnp.sum(d_t, axis=0, keepdims=True)
        dt_ref[...] = d_t.astype(BF16)
        y1_ref[...] = y1.astype(BF16)
        d_y1 = d_c * sg + _dot(d_t, w_ref[...], NT)
        dyp_ref[...] = d_y1 * _gelu_grad(yp)

    row = pl.BlockSpec((tr, DS), lambda i: (i, 0))
    return pl.pallas_call(
        body, name="s5_post_bwd", grid=(L // tr,),
        in_specs=[row, row, pl.BlockSpec((tr, DS), lambda i: (i, 1)), row, pl.BlockSpec((DS, DS), lambda i: (0, 0))],
        out_specs=[row, pl.BlockSpec((tr, DS), lambda i: (i, 1)), row, row, pl.BlockSpec((1, DS), lambda i: (0, 0))],
        out_shape=[jax.ShapeDtypeStruct((L, DS), F32), jax.ShapeDtypeStruct((L, 2 * DS), BF16),
                   jax.ShapeDtypeStruct((L, DS), BF16), jax.ShapeDtypeStruct((L, DS), BF16),
                   jax.ShapeDtypeStruct((1, DS), F32)],
        compiler_params=pltpu.CompilerParams(dimension_semantics=("arbitrary",)),
    )(d_ycat, y_pre, proj_main, t_pre, glu_w)


def _row_cumsum(x, reverse=False):
    n = x.shape[0]
    row = lax.broadcasted_iota(jnp.int32, x.shape, 0)
    k = 1
    while k < n:
        if reverse:
            x = x + jnp.where(row < n - k, pltpu.roll(x, n - k, 0), 0.0)
        else:
            x = x + jnp.where(row >= k, pltpu.roll(x, k, 0), 0.0)
        k *= 2
    return x


def _gla_gates(glow, gu_ref, gb_ref):
    a = _dot(glow, gu_ref[...]) + gb_ref[...]
    lg = (jnp.minimum(a, 0.0) - jnp.log(1.0 + jnp.exp(-jnp.abs(a)))) * (1.0 / GLA_TAU)
    ri = lax.broadcasted_iota(jnp.int32, (GLA_CHUNK, GLA_CHUNK), 0)
    ci = lax.broadcasted_iota(jnp.int32, (GLA_CHUNK, GLA_CHUNK), 1)
    b = _row_cumsum(lg)
    b_last = b[GLA_CHUNK - 1:GLA_CHUNK, :]
    return a, b, b_last, ri >= ci


def _gla_specs(DS, DK, DV, c, cmap):
    return [
        pl.BlockSpec((c, DK), lambda n: (cmap(n), 2 * DS // DK)),
        pl.BlockSpec((c, DK), lambda n: (cmap(n), 2 * DS // DK + 1)),
        pl.BlockSpec((c, DV), lambda n: (cmap(n), (2 * DS + 2 * DK) // DV)),
        pl.BlockSpec((c, DV), lambda n: (cmap(n), (2 * DS + 2 * DK) // DV + 1)),
    ]


def _gla_fwd(proj_main, proj_low, gate_up_pad, gate_bias, norm_w, ycat, DS, DK, DV):
    L = proj_main.shape[0]
    nc = L // GLA_CHUNK
    cps = math.gcd(GLA_STEP_CHUNKS, nc)
    nh = DK // GLA_HK
    scale = GLA_HK ** -0.5

    def body(q_ref, k_ref, v_ref, z_ref, gl_ref, gu_ref, gb_ref, nw_ref, _yc_ref, y_ref, sp_ref, at_ref, o_ref, st_ref):
        n = pl.program_id(0)

        @pl.when(n == 0)
        def _():
            st_ref[...] = jnp.zeros_like(st_ref)

        pairs = [(sc, h) for sc in range(cps) for h in range(nh)]
        rows = lambda sc: slice(sc * GLA_CHUNK, (sc + 1) * GLA_CHUNK)
        kcol = lambda h: slice(h * GLA_HK, (h + 1) * GLA_HK)
        vcol = lambda h: slice(h * GLA_HV, (h + 1) * GLA_HV)
        gates = [_gla_gates(gl_ref[rows(sc), :], gu_ref, gb_ref) for sc in range(cps)]
        qe, dec, o_in, kv = {}, {}, {}, {}
        for sc, h in pairs:
            _, b, b_last, mask = gates[sc]
            bh, bl = b[:, kcol(h)], b_last[:, kcol(h)]
            qe[sc, h] = (q_ref[rows(sc), kcol(h)] * scale) * jnp.exp(bh)
            kh = k_ref[rows(sc), kcol(h)]
            vh = v_ref[rows(sc), vcol(h)]
            attn = jnp.where(mask, _dot(qe[sc, h], kh * jnp.exp(-bh), NT), 0.0).astype(BF16)
            at_ref[h, rows(sc), :] = attn
            o_in[sc, h] = _dot(attn, vh)
            kv[sc, h] = _dot(vh, kh * jnp.exp(bl - bh), TN)
            dec[sc, h] = jnp.exp(bl)
        for sc, h in pairs:
            st = st_ref[h]
            sp_ref[sc, h] = st
            o = o_in[sc, h] + _dot(qe[sc, h], st, NT)
            o_ref[rows(sc), vcol(h)] = o
            st_ref[h] = dec[sc, h] * st + kv[sc, h]
            r = lax.rsqrt(jnp.mean(o * o, axis=-1, keepdims=True) + EPS)
            z = z_ref[rows(sc), vcol(h)]
            y_ref[rows(sc), vcol(h)] = (o * r * nw_ref[...] * (z * _sigmoid(z))).astype(BF16)

    c = cps * GLA_CHUNK
    return pl.pallas_call(
        body, name="gla_fwd", grid=(nc // cps,),
        in_specs=_gla_specs(DS, DK, DV, c, lambda n: n) + [
            pl.BlockSpec((c, LANES), lambda n: (n, 0)),
            pl.BlockSpec((LANES, DK), lambda n: (0, 0)),
            pl.BlockSpec((1, DK), lambda n: (0, 0)),
            pl.BlockSpec((1, GLA_HV), lambda n: (0, 0)),
            pl.BlockSpec(memory_space=pl.ANY),
        ],
        out_specs=[pl.BlockSpec((c, DV), lambda n: (n, DS // DV)),
                   pl.BlockSpec((cps, nh, GLA_HV, GLA_HK), lambda n: (n, 0, 0, 0)),
                   pl.BlockSpec((nh, c, GLA_CHUNK), lambda n: (0, n, 0)),
                   pl.BlockSpec((c, DV), lambda n: (n, 0))],
        input_output_aliases={8: 0},
        out_shape=[jax.ShapeDtypeStruct(ycat.shape, BF16), jax.ShapeDtypeStruct((nc, nh, GLA_HV, GLA_HK), F32),
                   jax.ShapeDtypeStruct((nh, L, GLA_CHUNK), BF16), jax.ShapeDtypeStruct((L, DV), F32)],
        scratch_shapes=[pltpu.VMEM((nh, GLA_HV, GLA_HK), F32)],
        compiler_params=pltpu.CompilerParams(dimension_semantics=("arbitrary",)),
    )(proj_main, proj_main, proj_main, proj_main, proj_low, gate_up_pad, gate_bias, norm_w, ycat)


def _gla_bwd(d_ycat, proj_main, proj_low, s_prev, scores, o_pre, gate_up_pad, gate_bias, norm_w, DS, DK, DV):
    L = proj_main.shape[0]
    nc = L // GLA_CHUNK
    cps = math.gcd(GLA_STEP_CHUNKS, nc)
    nh = DK // GLA_HK
    scale = GLA_HK ** -0.5

    def body(dy_ref, q_ref, k_ref, v_ref, z_ref, gl_ref, sp_ref, at_ref, o_ref, gu_ref, gb_ref, nw_ref,
             dg_ref, da_ref, gnw_ref, ggb_ref, dst_ref):
        n = pl.program_id(0)

        @pl.when(n == 0)
        def _():
            dst_ref[...] = jnp.zeros_like(dst_ref)
            gnw_ref[...] = jnp.zeros_like(gnw_ref)
            ggb_ref[...] = jnp.zeros_like(ggb_ref)

        last_row = lax.broadcasted_iota(jnp.int32, (GLA_CHUNK, GLA_HK), 0) == GLA_CHUNK - 1
        nw = nw_ref[...]
        for sc in reversed(range(cps)):
            rs = slice(sc * GLA_CHUNK, (sc + 1) * GLA_CHUNK)
            a, b, b_last, mask = _gla_gates(gl_ref[rs, :], gu_ref, gb_ref)
            for h in range(nh):
                ks = slice(h * GLA_HK, (h + 1) * GLA_HK)
                vs = slice(h * GLA_HV, (h + 1) * GLA_HV)
                bh, bl = b[:, ks], b_last[:, ks]
                e = jnp.exp(bh)
                einv = jnp.exp(-bh)
                etail = jnp.exp(bl - bh)
                dec = jnp.exp(bl)
                qe = (q_ref[rs, ks] * scale) * e
                kh = k_ref[rs, ks]
                ke = kh * einv
                ktail = kh * etail
                vh = v_ref[rs, vs]
                st = sp_ref[sc, h]
                dst = dst_ref[h]
                attn = at_ref[h, rs, :]
                o = o_ref[rs, vs]
                r = lax.rsqrt(jnp.mean(o * o, axis=-1, keepdims=True) + EPS)
                nrm = o * r
                z = z_ref[rs, vs]
                sz = _sigmoid(z)
                dy = dy_ref[rs, vs]
                dg_ref[rs, 2 * DK + DV + h * GLA_HV:2 * DK + DV + (h + 1) * GLA_HV] = (
                    dy * nrm * nw * (sz * (1.0 + z * (1.0 - sz)))).astype(BF16)
                d_on = dy * (z * sz)
                gnw_ref[...] += jnp.sum(d_on * nrm, axis=0, keepdims=True)
                d_n = d_on * nw
                d_o = r * (d_n - nrm * jnp.mean(d_n * nrm, axis=-1, keepdims=True))
                d_attn = jnp.where(mask, _dot(d_o, vh, NT), 0.0)
                dg_ref[rs, 2 * DK + h * GLA_HV:2 * DK + (h + 1) * GLA_HV] = (
                    _dot(attn, d_o, TN) + _dot(ktail, dst, NT)).astype(BF16)
                d_qe = _dot(d_attn, ke) + _dot(d_o, st)
                d_ke = _dot(d_attn, qe, TN)
                d_kt = _dot(vh, dst)
                d_dec = jnp.sum(dst * st, axis=0, keepdims=True)
                dst_ref[h] = dec * dst + _dot(d_o, qe, TN)
                dg_ref[rs, ks] = (d_qe * scale * e).astype(BF16)
                dg_ref[rs, DK + h * GLA_HK:DK + (h + 1) * GLA_HK] = (d_ke * einv + d_kt * etail).astype(BF16)
                d_bl = jnp.sum(d_kt * ktail, axis=0, keepdims=True) + d_dec * dec
                d_b = d_qe * qe - d_ke * ke - d_kt * ktail + jnp.where(last_row, d_bl, 0.0)
                d_lg = _row_cumsum(d_b, reverse=True)
                d_a = d_lg * (1.0 / GLA_TAU) * _sigmoid(-a[:, ks])
                ggb_ref[:, ks] += jnp.sum(d_a, axis=0, keepdims=True)
                da_ref[rs, ks] = d_a.astype(BF16)

    c = cps * GLA_CHUNK
    ns = nc // cps
    rn = lambda n: ns - 1 - n
    return pl.pallas_call(
        body, name="gla_bwd", grid=(ns,),
        in_specs=[pl.BlockSpec((c, DV), lambda n: (rn(n), DS // DV))] + _gla_specs(DS, DK, DV, c, rn) + [
            pl.BlockSpec((c, LANES), lambda n: (rn(n), 0)),
            pl.BlockSpec((cps, nh, GLA_HV, GLA_HK), lambda n: (rn(n), 0, 0, 0)),
            pl.BlockSpec((nh, c, GLA_CHUNK), lambda n: (0, rn(n), 0)),
            pl.BlockSpec((c, DV), lambda n: (rn(n), 0)),
            pl.BlockSpec((LANES, DK), lambda n: (0, 0)),
            pl.BlockSpec((1, DK), lambda n: (0, 0)),
            pl.BlockSpec((1, GLA_HV), lambda n: (0, 0)),
        ],
        out_specs=[pl.BlockSpec((c, 2 * DK + 2 * DV), lambda n: (rn(n), 0)),
                   pl.BlockSpec((c, DK), lambda n: (rn(n), 0)),
                   pl.BlockSpec((1, GLA_HV), lambda n: (0, 0)), pl.BlockSpec((1, DK), lambda n: (0, 0))],
        out_shape=[jax.ShapeDtypeStruct((L, 2 * DK + 2 * DV), BF16),
                   jax.ShapeDtypeStruct((L, DK), BF16),
                   jax.ShapeDtypeStruct((1, GLA_HV), F32), jax.ShapeDtypeStruct((1, DK), F32)],
        scratch_shapes=[pltpu.VMEM((nh, GLA_HV, GLA_HK), F32)],
        compiler_params=pltpu.CompilerParams(dimension_semantics=("arbitrary",)),
    )(d_ycat, proj_main, proj_main, proj_main, proj_main, proj_low, s_prev, scores, o_pre, gate_up_pad, gate_bias, norm_w)


def _adamw_math(w, g, m, v):
    c1 = 1.0 - ADAM_B1 ** ADAM_STEP
    c2 = 1.0 - ADAM_B2 ** ADAM_STEP
    m_ = ADAM_B1 * m + (1.0 - ADAM_B1) * g
    v_ = ADAM_B2 * v + (1.0 - ADAM_B2) * (g * g)
    return -ADAM_LR * ((m_ / c1) / (jnp.sqrt(v_ / c2) + ADAM_EPS) + ADAM_WD * w), m_, v_


def _adamw_small(g_row, g_a, g_bc, ws, ms, vs):
    n = len(ws)
    nvec = n - 6

    def body(*refs):
        grow_ref, ga_ref, gbc_ref = refs[:3]
        w_refs, m_refs, v_refs = refs[3:3 + n], refs[3 + n:3 + 2 * n], refs[3 + 2 * n:3 + 3 * n]
        outs = refs[3 + 3 * n:]
        off = 0
        for i in range(n):
            if i < nvec:
                width = ws[i].shape[1]
                g = grow_ref[:, off:off + width]
                off += width
            elif i < nvec + 2:
                g = ga_ref[i - nvec]
            else:
                g = gbc_ref[i - nvec - 2]
            d, m_, v_ = _adamw_math(w_refs[i][...], g, m_refs[i][...], v_refs[i][...])
            outs[i][...] = g
            outs[n + i][...] = d
            outs[2 * n + i][...] = m_
            outs[3 * n + i][...] = v_

    vm = pl.BlockSpec(memory_space=pltpu.VMEM)
    outs = pl.pallas_call(
        body, name="adamw_small",
        in_specs=[vm] * (3 + 3 * n), out_specs=[vm] * (4 * n),
        out_shape=[jax.ShapeDtypeStruct(w.shape, F32) for w in ws] * 4,
    )(g_row, g_a, g_bc, *ws, *ms, *vs)
    return [outs[k * n:(k + 1) * n] for k in range(4)]


def _my_pos():
    return lax.axis_index("x"), lax.axis_index("y"), lax.axis_index("c")


def _split_start(name, srcs, lands_sd, make_copies, ncopies, after):
    n, m = len(srcs), len(lands_sd)

    def body(*refs):
        send_sems, recv_sems = refs[n + m + len(after)], refs[n + m + len(after) + 1]
        for cp in make_copies(refs[:n], refs[n:n + m], send_sems, recv_sems):
            cp.start()
        refs[-1][...] = jnp.zeros_like(refs[-1])

    hbm = pl.BlockSpec(memory_space=pltpu.HBM)
    sem = pl.BlockSpec(memory_space=pltpu.SEMAPHORE)
    outs = pl.pallas_call(
        body, name=name,
        in_specs=[hbm] * (n + m) + [pl.BlockSpec(memory_space=pl.ANY)] * len(after),
        out_specs=[sem, sem] + [hbm] * (n + m) + [pl.BlockSpec(memory_space=pltpu.VMEM)],
        out_shape=[pltpu.SemaphoreType.DMA((ncopies,)), pltpu.SemaphoreType.DMA((ncopies,))]
        + [pltpu.HBM(s.shape, s.dtype) for s in srcs] + [pltpu.HBM(s.shape, s.dtype) for s in lands_sd]
        + [jax.ShapeDtypeStruct((SUBLANES, LANES), F32)],
        input_output_aliases={i: 2 + i for i in range(n + m)},
        compiler_params=pltpu.CompilerParams(has_side_effects=pltpu.SideEffectType.DATAFLOW_SIDE_EFFECTING),
    )(*[pltpu.with_memory_space_constraint(s, pltpu.HBM) for s in srcs],
      *[pltpu.with_memory_space_constraint(lax.empty(s.shape, s.dtype), pltpu.HBM) for s in lands_sd], *after)
    return outs[0], outs[1], outs[2:2 + n], outs[2 + n:2 + n + m], outs[-1]


def _split_wait(name, send_sems, recv_sems, srcs, lands, make_copies, after):
    n, m = len(srcs), len(lands)

    def body(*refs):
        for cp in make_copies(refs[:n], refs[n:n + m], refs[n + m], refs[n + m + 1]):
            cp.wait_send()
            cp.wait_recv()

    hbm = pl.BlockSpec(memory_space=pltpu.HBM)
    sem = pl.BlockSpec(memory_space=pltpu.SEMAPHORE)
    outs = pl.pallas_call(
        body, name=name,
        in_specs=[hbm] * (n + m) + [sem, sem] + [pl.BlockSpec(memory_space=pl.ANY)] * len(after),
        out_specs=[hbm] * (n + m),
        out_shape=[pltpu.HBM(s.shape, s.dtype) for s in srcs] + [pltpu.HBM(p.shape, p.dtype) for p in lands],
        input_output_aliases={i: i for i in range(n + m)},
        compiler_params=pltpu.CompilerParams(has_side_effects=pltpu.SideEffectType.DATAFLOW_SIDE_EFFECTING),
    )(*srcs, *lands, send_sems, recv_sems, *after)
    return outs[:n], outs[n:]


def _pair_half_copies(srcs, lands, send_sems, recv_sems):
    x, y, c = _my_pos()
    copies = []
    for a in range(len(srcs)):
        hrows = srcs[a].shape[1] // 2
        copies.append(pltpu.make_async_remote_copy(
            src_ref=srcs[a].at[:, pl.ds((1 - c) * hrows, hrows), :], dst_ref=lands[a], send_sem=send_sems.at[a],
            recv_sem=recv_sems.at[a], device_id=(x, y, 1 - c), device_id_type=MESH))
    return copies


def _late_gather_copies(srcs, lands, send_sems, recv_sems):
    x, y, c = _my_pos()
    me = 2 * x + y
    copies = []
    for d in (1, 2, 3):
        to = (x ^ (d >> 1), y ^ (d & 1), c)
        for a in range(len(srcs)):
            hrows = srcs[a].shape[0] // 2
            rows = pl.ds(c * hrows, hrows)
            copies.append(pltpu.make_async_remote_copy(
                src_ref=srcs[a].at[rows, :], dst_ref=lands[a].at[me, rows, :], send_sem=send_sems.at[3 * a + d - 1],
                recv_sem=recv_sems.at[3 * a + d - 1], device_id=to, device_id_type=MESH))
    return copies


def _late_gather_start(shards, after, name):
    lands = [jax.ShapeDtypeStruct((4,) + s.shape, s.dtype) for s in shards]
    return _split_start(name, shards, lands, _late_gather_copies, 3 * len(shards), [after])


def _late_gather_wait(send_sems, recv_sems, shards, lands, after, name):
    return _split_wait(name, send_sems, recv_sems, shards, lands, _late_gather_copies, after)[1]


def _late_gather_pair(lands, name):
    n = len(lands)

    def body(*refs):
        outs = refs[n:2 * n]
        send_sems, recv_sems = refs[2 * n:]
        x, y, c = _my_pos()

        def copy(a, d, half):
            chip = 2 * (x ^ (d >> 1)) + (y ^ (d & 1))
            hrows = lands[a].shape[1] // 2
            sl = outs[a].at[chip, pl.ds(half * hrows, hrows), :]
            return pltpu.make_async_remote_copy(src_ref=sl, dst_ref=sl, send_sem=send_sems.at[3 * a + d - 1],
                                                recv_sem=recv_sems.at[3 * a + d - 1], device_id=(x, y, 1 - c),
                                                device_id_type=MESH)

        pairs = [(a, d) for d in (1, 2, 3) for a in range(n)]
        for a, d in pairs:
            copy(a, d, c).start()
        for a, d in pairs:
            copy(a, d, c).wait_send()
            copy(a, d, 1 - c).wait_recv()

    hbm = pl.BlockSpec(memory_space=pltpu.HBM)
    return pl.pallas_call(
        body, name=name, in_specs=[hbm] * n, out_specs=[hbm] * n,
        out_shape=[jax.ShapeDtypeStruct(p.shape, p.dtype) for p in lands],
        input_output_aliases={i: i for i in range(n)},
        scratch_shapes=[pltpu.SemaphoreType.DMA((3 * n,)), pltpu.SemaphoreType.DMA((3 * n,))],
    )(*lands)


def _pair_exchange(gs):
    n = len(gs)

    def body(*refs):
        ins, outs = refs[:n], refs[n:2 * n]
        send_sems, recv_sems = refs[2 * n:]
        x, y, c = _my_pos()
        sent = []
        for a in range(n):
            hrows = gs[a].shape[1] // 2
            cp = pltpu.make_async_remote_copy(
                src_ref=ins[a].at[:, pl.ds((1 - c) * hrows, hrows), :], dst_ref=outs[a], send_sem=send_sems.at[a],
                recv_sem=recv_sems.at[a], device_id=(x, y, 1 - c), device_id_type=MESH)
            cp.start()
            sent.append(cp)
        for cp in sent:
            cp.wait()

    hbm = pl.BlockSpec(memory_space=pltpu.HBM)
    return pl.pallas_call(
        body, name="grad_pair_exchange", in_specs=[hbm] * n, out_specs=[hbm] * n,
        out_shape=[jax.ShapeDtypeStruct((g.shape[0], g.shape[1] // 2, g.shape[2]), g.dtype) for g in gs],
        scratch_shapes=[pltpu.SemaphoreType.DMA((n,)), pltpu.SemaphoreType.DMA((n,))],
    )(*gs)


def _pair_add(g, got, c_arr, name):
    nk, rows2, cols = g.shape
    hrows = rows2 // 2
    tr = _blk(hrows, 256, 2 * SUBLANES)
    nb = hrows // tr

    def body(c_ref, a_ref, b_ref, o_ref):
        o_ref[...] = (a_ref[...].astype(F32) + b_ref[...].astype(F32)).astype(o_ref.dtype)

    return pl.pallas_call(
        body, name=name,
        grid_spec=pltpu.PrefetchScalarGridSpec(
            num_scalar_prefetch=1, grid=(nk, nb),
            in_specs=[pl.BlockSpec((1, tr, cols), lambda k, i, c_ref: (k, c_ref[0] * nb + i, 0)),
                      pl.BlockSpec((1, tr, cols), lambda k, i, c_ref: (k, i, 0))],
            out_specs=pl.BlockSpec((1, tr, cols), lambda k, i, c_ref: (k, i, 0))),
        out_shape=jax.ShapeDtypeStruct((nk, hrows, cols), g.dtype),
        compiler_params=pltpu.CompilerParams(dimension_semantics=("parallel", "parallel")),
    )(c_arr, g, got)


def _chip_scatter_copies(srcs, lands, send_sems, recv_sems):
    x, y, c = _my_pos()
    copies = []
    for d in (1, 2, 3):
        tx, ty = x ^ (d >> 1), y ^ (d & 1)
        for a in range(len(srcs)):
            copies.append(pltpu.make_async_remote_copy(
                src_ref=srcs[a].at[2 * tx + ty], dst_ref=lands[a].at[d - 1], send_sem=send_sems.at[3 * a + d - 1],
                recv_sem=recv_sems.at[3 * a + d - 1], device_id=(tx, ty, c), device_id_type=MESH))
    return copies


def _chip_scatter_start(pss):
    lands = [jax.ShapeDtypeStruct((3,) + p.shape[1:], p.dtype) for p in pss]
    return _split_start("grad_chip_scatter_start", pss, lands, _chip_scatter_copies, 3 * len(pss), [])


def _chip_scatter_wait(send_sems, recv_sems, srcs, lands, after):
    return _split_wait("grad_chip_scatter_wait", send_sems, recv_sems, srcs, lands, _chip_scatter_copies, [after])


def _chip_sum(ps, got, me_arr, name):
    _, hrows, cols = ps.shape
    tr = _blk(hrows, 256, 2 * SUBLANES)

    def body(me_ref, p_ref, g_ref, o_ref):
        acc = p_ref[0].astype(F32)
        for s in range(3):
            acc = acc + g_ref[s].astype(F32)
        o_ref[...] = acc

    return pl.pallas_call(
        body, name=name,
        grid_spec=pltpu.PrefetchScalarGridSpec(
            num_scalar_prefetch=1, grid=(hrows // tr,),
            in_specs=[pl.BlockSpec((1, tr, cols), lambda i, me_ref: (me_ref[0], i, 0)),
                      pl.BlockSpec((3, tr, cols), lambda i, me_ref: (0, i, 0))],
            out_specs=pl.BlockSpec((tr, cols), lambda i, me_ref: (i, 0))),
        out_shape=jax.ShapeDtypeStruct((hrows, cols), F32),
        compiler_params=pltpu.CompilerParams(dimension_semantics=("parallel",)),
    )(me_arr, ps, got)


def _pair_swap(halves):
    n = len(halves)

    def body(*refs):
        ins, outs = refs[:n], refs[n:2 * n]
        send_sems, recv_sems = refs[2 * n:]
        x, y, c = _my_pos()
        sent = []
        for a in range(n):
            cp = pltpu.make_async_remote_copy(src_ref=ins[a], dst_ref=outs[a], send_sem=send_sems.at[a], recv_sem=recv_sems.at[a],
                                              device_id=(x, y, 1 - c), device_id_type=MESH)
            cp.start()
            sent.append(cp)
        for cp in sent:
            cp.wait()

    hbm = pl.BlockSpec(memory_space=pltpu.HBM)
    return pl.pallas_call(
        body, name="grad_pair_swap", in_specs=[hbm] * n, out_specs=[hbm] * n,
        out_shape=[jax.ShapeDtypeStruct(h.shape, h.dtype) for h in halves],
        scratch_shapes=[pltpu.SemaphoreType.DMA((n,)), pltpu.SemaphoreType.DMA((n,))],
    )(*halves)


def _adamw_sharded(w, g_own, g_other, m, v, c_arr, after, name):
    R, C = w.shape
    hrows = R // 2
    tr = _blk(hrows, 256, SUBLANES)
    nbh = hrows // tr

    def body(c_ref, w_ref, go_ref, gx_ref, m_ref, v_ref, _after_ref, g_ref, d_ref, nm_ref, nv_ref):
        mine = (pl.program_id(0) // nbh) == c_ref[0]
        g_ = jnp.where(mine, go_ref[...], gx_ref[...])
        g_ref[...] = g_
        d_ref[...], nm_ref[...], nv_ref[...] = _adamw_math(w_ref[...], g_, m_ref[...], v_ref[...])

    blk = pl.BlockSpec((tr, C), lambda i, c_ref: (i, 0))
    hblk = pl.BlockSpec((tr, C), lambda i, c_ref: (i % nbh, 0))
    sd = jax.ShapeDtypeStruct((R, C), F32)
    return pl.pallas_call(
        body, name=name,
        grid_spec=pltpu.PrefetchScalarGridSpec(
            num_scalar_prefetch=1, grid=(2 * nbh,),
            in_specs=[blk, hblk, hblk, blk, blk, pl.BlockSpec(memory_space=pl.ANY)], out_specs=[blk] * 4),
        out_shape=[sd] * 4,
        compiler_params=pltpu.CompilerParams(dimension_semantics=("parallel",)),
    )(c_arr, w, g_own, g_other, m, v, after)


def _ar_piece(ref, rows, p):
    start = p * rows
    if rows % SUBLANES == 0:
        start = pl.multiple_of(start, SUBLANES)
    return ref.at[..., pl.ds(start, rows), :]


def _ar_peer(d):
    x, y, c = _my_pos()
    return (x ^ (d >> 2), y ^ ((d >> 1) & 1), c ^ (d & 1))


def _ar_lin(p):
    return 4 * p[0] + 2 * p[1] + p[2]


def _ar_scatter_copies(rows):
    def make(srcs, lands, send_sems, recv_sems):
        n = len(srcs)
        copies = []
        for d in range(1, 8):
            to = _ar_peer(d)
            for a in range(n):
                copies.append(pltpu.make_async_remote_copy(
                    src_ref=_ar_piece(srcs[a], rows[a], _ar_lin(to)), dst_ref=lands[a].at[d],
                    send_sem=send_sems.at[(d - 1) * n + a], recv_sem=recv_sems.at[(d - 1) * n + a], device_id=to,
                    device_id_type=MESH))
        return copies
    return make


def _ar_gather_copies(rows):
    def make(srcs, lands, send_sems, recv_sems):
        n = len(srcs)
        me = _ar_lin(_my_pos())
        copies = []
        for d in range(1, 8):
            for a in range(n):
                copies.append(pltpu.make_async_remote_copy(
                    src_ref=srcs[a], dst_ref=_ar_piece(lands[a], rows[a], me),
                    send_sem=send_sems.at[(d - 1) * n + a], recv_sem=recv_sems.at[(d - 1) * n + a], device_id=_ar_peer(d),
                    device_id_type=MESH))
        return copies
    return make


def _ar_sum(srcs, lands, rows):
    n = len(srcs)

    def body(*refs):
        me = _ar_lin(_my_pos())
        for a in range(n):
            acc = _ar_piece(refs[a], rows[a], me)[...]
            for d in range(1, 8):
                acc = acc + refs[n + a][d]
            refs[2 * n + a][...] = acc

    vm = pl.BlockSpec(memory_space=pltpu.VMEM)
    return pl.pallas_call(
        body, name="allreduce_sum", in_specs=[vm] * (2 * n), out_specs=[vm] * n,
        out_shape=[jax.ShapeDtypeStruct(p.shape[1:], F32) for p in lands],
    )(*srcs, *lands)


def kernel(x, pre_norm_w, w_in, s5_A_re, s5_A_im, s5_B_re, s5_B_im, s5_C_re, s5_C_im, s5_D, s5_log_dt, s5_glu_w, s5_glu_b, gla_gate_up, gla_gate_bias, gla_norm_w, w_out, post_norm_w, loss_target, m_pre_norm_w, m_w_in, m_s5_A_re, m_s5_A_im, m_s5_B_re, m_s5_B_im, m_s5_C_re, m_s5_C_im, m_s5_D, m_s5_log_dt, m_s5_glu_w, m_s5_glu_b, m_gla_gate_up, m_gla_gate_bias, m_gla_norm_w, m_w_out, m_post_norm_w, v_pre_norm_w, v_w_in, v_s5_A_re, v_s5_A_im, v_s5_B_re, v_s5_B_im, v_s5_C_re, v_s5_C_im, v_s5_D, v_s5_log_dt, v_s5_glu_w, v_s5_glu_b, v_gla_gate_up, v_gla_gate_bias, v_gla_norm_w, v_w_out, v_post_norm_w):
    names = ["pre_norm_w", "w_in", "s5_A_re", "s5_A_im", "s5_B_re", "s5_B_im", "s5_C_re", "s5_C_im", "s5_D", "s5_log_dt",
             "s5_glu_w", "s5_glu_b", "gla_gate_up", "gla_gate_bias", "gla_norm_w", "w_out", "post_norm_w"]
    W = dict(zip(names, (pre_norm_w, w_in, s5_A_re, s5_A_im, s5_B_re, s5_B_im, s5_C_re, s5_C_im, s5_D, s5_log_dt,
                         s5_glu_w, s5_glu_b, gla_gate_up, gla_gate_bias, gla_norm_w, w_out, post_norm_w)))
    M = dict(zip(names, (m_pre_norm_w, m_w_in, m_s5_A_re, m_s5_A_im, m_s5_B_re, m_s5_B_im, m_s5_C_re, m_s5_C_im, m_s5_D,
                         m_s5_log_dt, m_s5_glu_w, m_s5_glu_b, m_gla_gate_up, m_gla_gate_bias, m_gla_norm_w, m_w_out,
                         m_post_norm_w)))
    V = dict(zip(names, (v_pre_norm_w, v_w_in, v_s5_A_re, v_s5_A_im, v_s5_B_re, v_s5_B_im, v_s5_C_re, v_s5_C_im, v_s5_D,
                         v_s5_log_dt, v_s5_glu_w, v_s5_glu_b, v_gla_gate_up, v_gla_gate_bias, v_gla_norm_w, v_w_out,
                         v_post_norm_w)))
    sharded = ("w_in", "s5_glu_w", "w_out", "gla_gate_up")

    xb = x[0]
    tgt = loss_target[0]
    L, D = xb.shape
    DS = D // 2
    G = DS // S5_GROUP
    P = S5_STATE
    NB = DS // S5_COLS
    DV = D - DS
    DK = DV // 2
    WM = 2 * DS + 2 * DK + 2 * DV
    nsh = w_in.shape[2]

    chip = 2 * lax.axis_index("x") + lax.axis_index("y")
    own = [jnp.pad(w_in[0].astype(BF16), ((0, 0), (0, -nsh % LANES))), s5_glu_w[0].astype(BF16),
           w_out[0].astype(BF16), gla_gate_up[0]]
    fill = lambda g, o: lax.dynamic_update_index_in_dim(g, o, chip, 0)
    win_ss, win_rs, win_src, win_lands, win_token = _late_gather_start(own[:1], pre_norm_w, "w_in_gather_start")
    h = _prenorm_fwd(xb, pre_norm_w, win_token)

    b_view = lambda t: jnp.transpose(t[0], (0, 2, 1)).reshape(G * S5_GROUP, P)
    b_back = lambda t: jnp.transpose(t.reshape(G, S5_GROUP, P), (0, 2, 1))[None]
    c_view = lambda t: t[0].reshape(G * S5_GROUP, P)
    c_back = lambda t: t.reshape(1, G, S5_GROUP, P)
    small = ["pre_norm_w", "post_norm_w", "s5_D", "s5_glu_b", "gla_gate_bias", "gla_norm_w", "s5_log_dt",
             "s5_A_re", "s5_A_im", "s5_B_re", "s5_B_im", "s5_C_re", "s5_C_im"]
    view = {n: (lambda t: t) for n in small[:7]}
    back = dict(view)
    view.update(s5_A_re=lambda t: t[0], s5_A_im=lambda t: t[0], s5_B_re=b_view, s5_B_im=b_view, s5_C_re=c_view, s5_C_im=c_view)
    back.update(s5_A_re=lambda t: t[None], s5_A_im=lambda t: t[None], s5_B_re=b_back, s5_B_im=b_back, s5_C_re=c_back,
                s5_C_im=c_back)
    Wv = {n: view[n](W[n]) for n in small}
    bbd_re, bbd_im, ct_re, ct_im, tab, ptab = _s5_prep_fwd(
        Wv["s5_A_re"], Wv["s5_A_im"], s5_log_dt, Wv["s5_B_re"], Wv["s5_B_im"], Wv["s5_C_re"], Wv["s5_C_im"],
        h, _blk(L, S5_TIME_BLOCK, SUBLANES) // SUBLANES)
    dvec = s5_D

    for d_ in (W, M, V):
        d_["w_in"], _ = lax.optimization_barrier((d_["w_in"], win_token))
    g_win = _late_gather_wait(win_ss, win_rs, win_src, win_lands,
                              [tab, W["w_in"][0], M["w_in"][0], V["w_in"][0]], "w_in_gather_wait")
    g_win = fill(_late_gather_pair(g_win, "w_in_gather_pair")[0], own[0])
    w_main, w_low = _assemble_w_in(g_win, nsh, WM)
    late_ss, late_rs, late_src, late_lands, late_token = _late_gather_start(own[1:], g_win, "late_gather_start")
    proj_main, proj_low = _in_proj(h, w_main, w_low, late_token)
    y_pre, s_re, s_im = _s5_scan_fwd(proj_main, bbd_re, bbd_im, ct_re, ct_im, dvec, tab, ptab, DS)
    late = _late_gather_wait(late_ss, late_rs, late_src, late_lands, [y_pre], "late_gather_wait")
    late = _late_gather_pair(late, "late_gather_pair")
    g_glu, g_wout, g_gup = [fill(g, o) for g, o in zip(late, own[1:])]
    glu_w = g_glu.reshape(DS, DS)
    wout = g_wout.reshape(D, D)
    gup = jnp.moveaxis(g_gup, 0, 1).reshape(GLA_RANK, DK)
    gup_pad = jnp.pad(gup, ((0, LANES - GLA_RANK), (0, 0))).astype(BF16)
    ycat, t_pre = _s5_post_fwd(y_pre, proj_main, glu_w, s5_glu_b, DS)
    ycat, s_prev, gla_scores, gla_o = _gla_fwd(proj_main, proj_low, gup_pad, gla_gate_bias, gla_norm_w, ycat,
                                               DS, DK, DV)
    mixed = _mm(ycat, wout, name="out_proj")
    loss11, d_mixed, dout, g_post_w = _post_fwd_bwd(mixed, xb, tgt, post_norm_w)

    d_ycat = _mm(d_mixed, wout, tb=True, name="out_proj_dx")
    d_ypre, d_s5, d_t, y1, g_glu_b = _s5_post_bwd(d_ycat, y_pre, proj_main, t_pre, glu_w, DS)
    d_s5, g_D, gct_re, gct_im, gbbd_re, gbbd_im, gab_re, gab_im = _s5_scan_bwd(
        d_ypre, proj_main, s_re, s_im, bbd_re, bbd_im, ct_re, ct_im, dvec, tab, ptab, d_s5, DS)
    d_gla, d_a, g_norm_w, g_gate_bias = _gla_bwd(
        d_ycat, proj_main, proj_low, s_prev, gla_scores, gla_o, gup_pad, gla_gate_bias, gla_norm_w, DS, DK, DV)
    d_low = _mm(d_a, gup_pad, tb=True, out_dtype=BF16, name="gate_dx")
    g_gup_pad = _mm(proj_low, d_a, ta=True, name="gate_dw")
    g_wmain, g_wlow = _in_proj_dw(h, d_s5, d_gla, d_low)

    g_win_sh = _split_w_in_grad(g_wmain, g_wlow, nsh)
    px_ss, px_rs, px_src, px_got, px_token = _split_start(
        "grad_pair_w_in_start", [g_win_sh], [jax.ShapeDtypeStruct((4, D // 2, nsh), BF16)], _pair_half_copies, 1, [])
    g_wout_full = _mm(ycat, d_mixed, ta=True, out_dtype=BF16, name="out_proj_dw", after=[px_token])
    g_glu_full = _mm(y1, d_t, ta=True, out_dtype=BF16, name="glu_dw", after=[px_token])
    px_src, px_got = _split_wait("grad_pair_w_in_wait", px_ss, px_rs, px_src, px_got, _pair_half_copies,
                                 [g_wout_full, g_glu_full])
    gs = [g_glu_full.reshape(4, DS // 4, DS), g_wout_full.reshape(4, D // 4, D),
          jnp.moveaxis(g_gup_pad[:GLA_RANK].reshape(GLA_RANK, 4, DK // 4), 1, 0)]
    c_arr = lax.axis_index("c").astype(jnp.int32).reshape(1)
    me_arr = chip.astype(jnp.int32).reshape(1)
    got = list(px_got) + list(_pair_exchange(gs))
    gs = list(px_src) + gs
    pss = [_pair_add(g, r, c_arr, "grad_pair_add_" + n) for n, g, r in zip(sharded, gs, got)]
    send_sems, recv_sems, pss, lands, token = _chip_scatter_start(pss)

    dh = _in_proj_dx(d_s5, d_gla, d_low, w_main, w_low, token)
    grad_x, g_pre_w = _prenorm_bwd(xb, dh, dout, pre_norm_w)

    g_a, g_bc, g_ldt = _s5_prep_bwd(Wv["s5_A_re"], Wv["s5_A_im"], s5_log_dt, Wv["s5_B_re"], Wv["s5_B_im"],
                                    gbbd_re, gbbd_im, gct_re, gct_im, gab_re, gab_im)

    g_vecs = jnp.concatenate([g_pre_w, g_post_w, g_D, g_glu_b, g_gate_bias, g_norm_w, g_ldt, loss11], axis=1)
    loss_at = g_vecs.shape[1] - 1
    lanes_pad = -g_vecs.shape[1] % (8 * SUBLANES * LANES)
    g_vecs = jnp.pad(g_vecs, ((0, 0), (0, lanes_pad))).reshape(-1, LANES)
    ar_srcs = [g_vecs, g_a, g_bc]
    ar_rows = [a.shape[-2] // 8 for a in ar_srcs]
    ar_lands = [jax.ShapeDtypeStruct((8,) + a.shape[:-2] + (r, a.shape[-1]), F32) for a, r in zip(ar_srcs, ar_rows)]
    ar_ss, ar_rs, ar_srcs, ar_got, ar_token = _split_start(
        "allreduce_scatter_start", ar_srcs, ar_lands, _ar_scatter_copies(ar_rows), 7 * len(ar_srcs), [])

    pss, rcv = _chip_scatter_wait(send_sems, recv_sems, pss, lands, ar_token)
    halves = [_chip_sum(p, r, me_arr, "grad_chip_sum_" + n) for n, p, r in zip(sharded, pss, rcv)]
    others = _pair_swap(halves)
    ar_srcs, ar_got = _split_wait("allreduce_scatter_wait", ar_ss, ar_rs, ar_srcs, ar_got, _ar_scatter_copies(ar_rows),
                                  [others[0]])
    ar_red = _ar_sum(ar_srcs, ar_got, ar_rows)
    ag_ss, ag_rs, ar_red, ag_full, ag_token = _split_start(
        "allreduce_gather_start", ar_red, [jax.ShapeDtypeStruct(a.shape, F32) for a in ar_srcs],
        _ar_gather_copies(ar_rows), 7 * len(ar_red), [])
    G_out, D_out, M_out, V_out = {}, {}, {}, {}
    for n, g_own, g_other in zip(sharded, halves, others):
        g_, d_, m_, v_ = _adamw_sharded(W[n][0], g_own, g_other, M[n][0], V[n][0], c_arr, ag_token, "adamw_" + n)
        G_out[n], D_out[n], M_out[n], V_out[n] = g_[None], d_[None], m_[None], v_[None]
    ar_red, ag_full = _split_wait("allreduce_gather_wait", ag_ss, ag_rs, ar_red, ag_full, _ar_gather_copies(ar_rows),
                                  [D_out[n] for n in sharded])
    me8 = 2 * chip + lax.axis_index("c")
    r_vecs, r_a, r_bc = [lax.dynamic_update_slice_in_dim(f, r, me8 * rw, axis=f.ndim - 2)
                         for f, r, rw in zip(ag_full, ar_red, ar_rows)]
    r_vecs = r_vecs.reshape(1, -1)
    loss = r_vecs[0, loss_at]
    outs4 = _adamw_small(r_vecs, r_a, r_bc, [Wv[n] for n in small],
                         [view[n](M[n]) for n in small], [view[n](V[n]) for n in small])
    for store, o in zip((G_out, D_out, M_out, V_out), outs4):
        store.update({n: back[n](t) for n, t in zip(small, o)})

    return (loss, grad_x[None], *[G_out[n] for n in names], *[D_out[n] for n in names],
            *[M_out[n] for n in names], *[V_out[n] for n in names])
```

```python
import functools
import math

import jax
import jax.numpy as jnp
from jax import lax
from jax.experimental import pallas as pl
from jax.experimental.pallas import tpu as pltpu

F32 = jnp.float32
BF16 = jnp.bfloat16
HI = lax.Precision.HIGHEST
MESH = pl.DeviceIdType.MESH

EPS = 1e-6
S5_GROUP = 16
S5_STATE = 64
GLA_HK = 128
GLA_HV = 256
GLA_RANK = 16
GLA_TAU = 16.0
GLA_CHUNK = 64
GLA_STEP_CHUNKS = 8
LANES = 128
SUBLANES = 8
S5_COLS = 128
S5_LANES = (S5_COLS // S5_GROUP) * S5_STATE
S5_TIME_BLOCK = 1024
ROW_TILE = 512

ADAM_LR = 0.001
ADAM_B1 = 0.9
ADAM_B2 = 0.999
ADAM_EPS = 1e-08
ADAM_WD = 0.01
ADAM_STEP = 10

GELU_K = math.sqrt(2.0 / math.pi)
GELU_C = 0.044715


def _blk(n, pref, unit=LANES):
    best = None
    b = unit
    while b <= min(n, pref):
        if n % b == 0:
            best = b
        b += unit
    return best if best is not None else n


def _dot(a, b, dn=(((1,), (0,)), ((), ()))):
    return lax.dot_general(a.astype(BF16), b.astype(BF16), dn, preferred_element_type=F32)


def _dot_hi(a, b, dn=(((1,), (0,)), ((), ()))):
    return lax.dot_general(a, b, dn, precision=HI, preferred_element_type=F32)


NN = (((1,), (0,)), ((), ()))
NT = (((1,), (1,)), ((), ()))
TN = (((0,), (0,)), ((), ()))


def _sigmoid(x):
    return 1.0 / (1.0 + jnp.exp(-x))


def _gelu(y):
    return 0.5 * y * (1.0 + jnp.tanh(GELU_K * (y + GELU_C * y * y * y)))


def _gelu_grad(y):
    th = jnp.tanh(GELU_K * (y + GELU_C * y * y * y))
    return 0.5 * (1.0 + th) + 0.5 * y * (1.0 - th * th) * GELU_K * (1.0 + 3.0 * GELU_C * y * y)


def _mm(a, b, *, name, ta=False, tb=False, out_dtype=F32, bm=1024, bn=1024, bk=2048, after=()):
    if ta:
        K, M = a.shape
    else:
        M, K = a.shape
    if tb:
        N, K2 = b.shape
    else:
        K2, N = b.shape
    assert K == K2, (a.shape, b.shape, ta, tb)
    bm, bn, bk = _blk(M, bm), _blk(N, bn), _blk(K, bk)
    nk = K // bk
    dn = (((0 if ta else 1,), (1 if tb else 0,)), ((), ()))

    def body(a_ref, b_ref, *rest):
        o_ref = rest[len(after)]
        if nk == 1:
            o_ref[...] = _dot(a_ref[...], b_ref[...], dn).astype(out_dtype)
            return
        acc_ref = rest[len(after) + 1]
        k = pl.program_id(2)

        @pl.when(k == 0)
        def _():
            acc_ref[...] = jnp.zeros_like(acc_ref)

        acc_ref[...] += _dot(a_ref[...], b_ref[...], dn)

        @pl.when(k == nk - 1)
        def _():
            o_ref[...] = acc_ref[...].astype(out_dtype)

    a_spec = pl.BlockSpec((bk, bm), lambda i, j, k: (k, i)) if ta else pl.BlockSpec((bm, bk), lambda i, j, k: (i, k))
    b_spec = pl.BlockSpec((bn, bk), lambda i, j, k: (j, k)) if tb else pl.BlockSpec((bk, bn), lambda i, j, k: (k, j))
    return pl.pallas_call(
        body,
        name=name,
        grid=(M // bm, N // bn, nk),
        in_specs=[a_spec, b_spec] + [pl.BlockSpec(memory_space=pl.ANY)] * len(after),
        out_specs=pl.BlockSpec((bm, bn), lambda i, j, k: (i, j)),
        out_shape=jax.ShapeDtypeStruct((M, N), out_dtype),
        scratch_shapes=[pltpu.VMEM((bm, bn), F32)] if nk > 1 else [],
        compiler_params=pltpu.CompilerParams(dimension_semantics=("parallel", "parallel", "arbitrary")),
    )(a, b, *after)


def _in_proj(h, w_main, w_low, after):
    M, K = h.shape
    N = w_main.shape[1]
    bm, bn = _blk(M, 1024), _blk(N, 1024)

    def body(h_ref, w_ref, wl_ref, _after_ref, o_ref, ol_ref):
        hv = h_ref[...]
        o_ref[...] = _dot(hv, w_ref[...])

        @pl.when(pl.program_id(1) == 0)
        def _():
            ol_ref[...] = _dot(hv, wl_ref[...])

    return pl.pallas_call(
        body, name="in_proj", grid=(M // bm, N // bn),
        in_specs=[pl.BlockSpec((bm, K), lambda i, j: (i, 0)), pl.BlockSpec((K, bn), lambda i, j: (0, j)),
                  pl.BlockSpec((K, LANES), lambda i, j: (0, 0)), pl.BlockSpec(memory_space=pl.ANY)],
        out_specs=[pl.BlockSpec((bm, bn), lambda i, j: (i, j)), pl.BlockSpec((bm, LANES), lambda i, j: (i, 0))],
        out_shape=[jax.ShapeDtypeStruct((M, N), F32), jax.ShapeDtypeStruct((M, LANES), F32)],
        compiler_params=pltpu.CompilerParams(dimension_semantics=("parallel", "arbitrary")),
    )(h, w_main, w_low, after)


def _in_proj_dx(a1, a2, al, b, bl, after, *, bm=1024, bn=1024, bk=2048):
    M, K1 = a1.shape
    K2 = a2.shape[1]
    N = b.shape[0]
    bm, bn = _blk(M, bm), _blk(N, bn)
    bk = _blk(math.gcd(K1, K2), bk)
    nk1, nk = K1 // bk, (K1 + K2) // bk

    def body(a1_ref, a2_ref, al_ref, b_ref, bl_ref, _after_ref, o_ref, acc_ref):
        k = pl.program_id(2)

        @pl.when(k == 0)
        def _():
            acc_ref[...] = _dot(al_ref[...], bl_ref[...], NT)

        @pl.when(k < nk1)
        def _():
            acc_ref[...] += _dot(a1_ref[...], b_ref[...], NT)

        @pl.when(k >= nk1)
        def _():
            acc_ref[...] += _dot(a2_ref[...], b_ref[...], NT)

        @pl.when(k == nk - 1)
        def _():
            o_ref[...] = acc_ref[...]

    return pl.pallas_call(
        body, name="in_proj_dx", grid=(M // bm, N // bn, nk),
        in_specs=[pl.BlockSpec((bm, bk), lambda i, j, k: (i, jnp.minimum(k, nk1 - 1))),
                  pl.BlockSpec((bm, bk), lambda i, j, k: (i, jnp.maximum(k - nk1, 0))),
                  pl.BlockSpec((bm, LANES), lambda i, j, k: (i, 0)),
                  pl.BlockSpec((bn, bk), lambda i, j, k: (j, k)),
                  pl.BlockSpec((bn, LANES), lambda i, j, k: (j, 0)),
                  pl.BlockSpec(memory_space=pl.ANY)],
        out_specs=pl.BlockSpec((bm, bn), lambda i, j, k: (i, j)),
        out_shape=jax.ShapeDtypeStruct((M, N), F32),
        scratch_shapes=[pltpu.VMEM((bm, bn), F32)],
        compiler_params=pltpu.CompilerParams(dimension_semantics=("parallel", "parallel", "arbitrary")),
    )(a1, a2, al, b, bl, after)


def _in_proj_dw(a, b1, b2, bl, *, bm=1024, bn=1024, bk=2048):
    K, M = a.shape
    N1, N2 = b1.shape[1], b2.shape[1]
    bm, bk = _blk(M, bm), _blk(K, bk)
    bn = _blk(math.gcd(N1, N2), bn)
    nj1, nj = N1 // bn, (N1 + N2) // bn
    nk = K // bk

    def body(a_ref, b1_ref, b2_ref, bl_ref, o_ref, ol_ref, acc_ref, accl_ref):
        j = pl.program_id(1)
        k = pl.program_id(2)

        @pl.when(k == 0)
        def _():
            acc_ref[...] = jnp.zeros_like(acc_ref)

        @pl.when(j < nj1)
        def _():
            acc_ref[...] += _dot(a_ref[...], b1_ref[...], TN)

        @pl.when(j >= nj1)
        def _():
            acc_ref[...] += _dot(a_ref[...], b2_ref[...], TN)

        @pl.when(k == nk - 1)
        def _():
            o_ref[...] = acc_ref[...].astype(BF16)

        @pl.when(j == 0)
        def _():
            low = _dot(a_ref[...], bl_ref[...], TN)

            @pl.when(k == 0)
            def _():
                accl_ref[...] = low

            @pl.when(k > 0)
            def _():
                accl_ref[...] += low

            @pl.when(k == nk - 1)
            def _():
                ol_ref[...] = accl_ref[...].astype(BF16)

    return pl.pallas_call(
        body, name="in_proj_dw", grid=(M // bm, nj, nk),
        in_specs=[pl.BlockSpec((bk, bm), lambda i, j, k: (k, i)),
                  pl.BlockSpec((bk, bn), lambda i, j, k: (jnp.where(j < nj1, k, nk - 1), jnp.minimum(j, nj1 - 1))),
                  pl.BlockSpec((bk, bn), lambda i, j, k: (jnp.where(j >= nj1, k, 0), jnp.maximum(j - nj1, 0))),
                  pl.BlockSpec((bk, LANES), lambda i, j, k: (jnp.where(j == 0, k, nk - 1), 0))],
        out_specs=[pl.BlockSpec((bm, bn), lambda i, j, k: (i, j)), pl.BlockSpec((bm, LANES), lambda i, j, k: (i, 0))],
        out_shape=[jax.ShapeDtypeStruct((M, N1 + N2), BF16), jax.ShapeDtypeStruct((M, LANES), BF16)],
        scratch_shapes=[pltpu.VMEM((bm, bn), F32), pltpu.VMEM((bm, LANES), F32)],
        compiler_params=pltpu.CompilerParams(dimension_semantics=("parallel", "arbitrary", "arbitrary")),
    )(a, b1, b2, bl)


def _assemble_w_in(g, nsh, wm):
    _, R, nshp = g.shape
    nb_in = nshp // LANES
    nb_main = wm // LANES
    tr = _blk(R, 512, 2 * SUBLANES)
    plan = []
    for b in range(nb_main + 1):
        terms = []
        for k in range(g.shape[0]):
            for i in range(nb_in):
                delta = nsh * k + LANES * i - LANES * b
                lo, hi = max(0, -delta), min(LANES, LANES - delta, nsh - LANES * i)
                if abs(delta) < LANES and hi > lo:
                    terms.append((k, i, delta))
        plan.append(terms)
    deltas = sorted({d for terms in plan for _, _, d in terms if d})

    def body(g_ref, wm_ref, wl_ref):
        src = _iota2((LANES, LANES), 0)
        dst = _iota2((LANES, LANES), 1)
        shift = {d: (dst - src == d).astype(BF16) for d in deltas}
        for b, terms in enumerate(plan):
            acc = None
            for k, i, d in terms:
                blk = g_ref[k, :, LANES * i:LANES * (i + 1)]
                t = _dot(blk, shift[d]) if d else blk.astype(F32)
                acc = t if acc is None else acc + t
            if b < nb_main:
                wm_ref[:, LANES * b:LANES * (b + 1)] = acc.astype(BF16)
            else:
                wl_ref[...] = acc.astype(BF16)

    return pl.pallas_call(
        body, name="assemble_w_in", grid=(R // tr,),
        in_specs=[pl.BlockSpec((g.shape[0], tr, nshp), lambda r: (0, r, 0))],
        out_specs=[pl.BlockSpec((tr, wm), lambda r: (r, 0)), pl.BlockSpec((tr, LANES), lambda r: (r, 0))],
        out_shape=[jax.ShapeDtypeStruct((R, wm), BF16), jax.ShapeDtypeStruct((R, LANES), BF16)],
        compiler_params=pltpu.CompilerParams(dimension_semantics=("parallel",)),
    )(g)


def _split_w_in_grad(g_main, g_low, nsh):
    R, wm = g_main.shape
    nb_main = wm // LANES
    nb_out = -(-nsh // LANES)
    tr = _blk(R, 512, 2 * SUBLANES)
    plan = {}
    for k in range(4):
        for i in range(nb_out):
            width = min(LANES, nsh - LANES * i)
            terms = []
            for b in range(nb_main + 1):
                delta = LANES * b - (nsh * k + LANES * i)
                lo, hi = max(0, delta), min(width, LANES + delta)
                if abs(delta) < LANES and hi > lo:
                    terms.append((b, delta))
            plan[k, i] = (width, terms)
    deltas = sorted({d for _, terms in plan.values() for _, d in terms if d})

    def body(gm_ref, gl_ref, o_ref):
        src = _iota2((LANES, LANES), 0)
        dst = _iota2((LANES, LANES), 1)
        shift = {d: (dst - src == d).astype(BF16) for d in deltas}
        for (k, i), (width, terms) in plan.items():
            acc = None
            for b, d in terms:
                blk = gm_ref[:, LANES * b:LANES * (b + 1)] if b < nb_main else gl_ref[...]
                t = _dot(blk, shift[d]) if d else blk.astype(F32)
                acc = t if acc is None else acc + t
            o_ref[k, :, LANES * i:LANES * i + width] = acc[:, :width].astype(BF16)

    return pl.pallas_call(
        body, name="split_w_in_grad", grid=(R // tr,),
        in_specs=[pl.BlockSpec((tr, wm), lambda r: (r, 0)), pl.BlockSpec((tr, LANES), lambda r: (r, 0))],
        out_specs=pl.BlockSpec((4, tr, nsh), lambda r: (0, r, 0)),
        out_shape=jax.ShapeDtypeStruct((4, R, nsh), BF16),
        compiler_params=pltpu.CompilerParams(dimension_semantics=("parallel",)),
    )(g_main, g_low)


def _prenorm_fwd(x, w, after):
    L, D = x.shape
    tr = _blk(L, ROW_TILE, SUBLANES)

    def body(x_ref, w_ref, _after_ref, h_ref):
        xv = x_ref[...]
        r = lax.rsqrt(jnp.mean(xv * xv, axis=-1, keepdims=True) + EPS)
        h_ref[...] = (xv * r * w_ref[...]).astype(BF16)

    return pl.pallas_call(
        body, name="prenorm_fwd", grid=(L // tr,),
        in_specs=[pl.BlockSpec((tr, D), lambda i: (i, 0)), pl.BlockSpec((1, D), lambda i: (0, 0)),
                  pl.BlockSpec(memory_space=pl.ANY)],
        out_specs=pl.BlockSpec((tr, D), lambda i: (i, 0)),
        out_shape=jax.ShapeDtypeStruct((L, D), BF16),
        compiler_params=pltpu.CompilerParams(dimension_semantics=("parallel",)),
    )(x, w, after)


def _post_fwd_bwd(mixed, x, target, w):
    L, D = x.shape
    tr = _blk(L, ROW_TILE, SUBLANES)
    nsteps = L // tr

    def body(mx_ref, x_ref, t_ref, w_ref, loss_ref, dm_ref, dout_ref, gw_ref, acc_ref):
        i = pl.program_id(0)

        @pl.when(i == 0)
        def _():
            acc_ref[...] = jnp.zeros_like(acc_ref)
            gw_ref[...] = jnp.zeros_like(gw_ref)

        mx = mx_ref[...]
        wv = w_ref[...]
        r = lax.rsqrt(jnp.mean(mx * mx, axis=-1, keepdims=True) + EPS)
        n = mx * r
        err = x_ref[...] + n * wv - t_ref[...]
        acc_ref[...] += jnp.sum(err * err, axis=0, keepdims=True)
        dout = err * (1.0 / D)
        dout_ref[...] = dout
        gw_ref[...] += jnp.sum(dout * n, axis=0, keepdims=True)
        dn = dout * wv
        dm_ref[...] = (r * (dn - n * jnp.mean(dn * n, axis=-1, keepdims=True))).astype(BF16)

        @pl.when(i == nsteps - 1)
        def _():
            loss_ref[...] = jnp.sum(acc_ref[...], axis=-1, keepdims=True) * (0.5 / D)

    row = pl.BlockSpec((tr, D), lambda i: (i, 0))
    vec = pl.BlockSpec((1, D), lambda i: (0, 0))
    return pl.pallas_call(
        body, name="post_fwd_bwd", grid=(nsteps,),
        in_specs=[row, row, row, vec],
        out_specs=[pl.BlockSpec((1, 1), lambda i: (0, 0)), row, row, vec],
        out_shape=[jax.ShapeDtypeStruct((1, 1), F32), jax.ShapeDtypeStruct((L, D), BF16),
                   jax.ShapeDtypeStruct((L, D), F32), jax.ShapeDtypeStruct((1, D), F32)],
        scratch_shapes=[pltpu.VMEM((1, D), F32)],
        compiler_params=pltpu.CompilerParams(dimension_semantics=("arbitrary",)),
    )(mixed, x, target, w)


def _prenorm_bwd(x, dh, dout, w):
    L, D = x.shape
    tr = _blk(L, ROW_TILE, SUBLANES)

    def body(x_ref, a_ref, dout_ref, w_ref, gx_ref, gw_ref):
        i = pl.program_id(0)

        @pl.when(i == 0)
        def _():
            gw_ref[...] = jnp.zeros_like(gw_ref)

        xv = x_ref[...]
        r = lax.rsqrt(jnp.mean(xv * xv, axis=-1, keepdims=True) + EPS)
        n = xv * r
        dh = a_ref[...]
        gw_ref[...] += jnp.sum(dh * n, axis=0, keepdims=True)
        dn = dh * w_ref[...]
        gx_ref[...] = dout_ref[...] + r * (dn - n * jnp.mean(dn * n, axis=-1, keepdims=True))

    row = pl.BlockSpec((tr, D), lambda i: (i, 0))
    vec = pl.BlockSpec((1, D), lambda i: (0, 0))
    return pl.pallas_call(
        body, name="prenorm_bwd", grid=(L // tr,),
        in_specs=[row, row, row, vec],
        out_specs=[row, vec],
        out_shape=[jax.ShapeDtypeStruct((L, D), F32), jax.ShapeDtypeStruct((1, D), F32)],
        compiler_params=pltpu.CompilerParams(dimension_semantics=("arbitrary",)),
    )(x, dh, dout, w)


def _s5_disc(a_re_raw, a_im, dt):
    a_re = jnp.minimum(a_re_raw, -1e-4)
    mag = jnp.exp(a_re * dt)
    ph = a_im * dt
    ab_re = mag * jnp.cos(ph)
    ab_im = mag * jnp.sin(ph)
    inv_n = 1.0 / (a_re * a_re + a_im * a_im)
    ia_re = a_re * inv_n
    ia_im = -a_im * inv_n
    n_re = ab_re - 1.0
    f_re = n_re * ia_re - ab_im * ia_im
    f_im = n_re * ia_im + ab_im * ia_re
    return a_re, ab_re, ab_im, f_re, f_im, ia_re, ia_im


def _iota2(shape, dim):
    return lax.broadcasted_iota(jnp.int32, shape, dim)


def _group_mask(rows, rows_per_group):
    shift = rows_per_group.bit_length() - 1
    return (_iota2((rows, S5_LANES), 0) >> shift) == (_iota2((rows, S5_LANES), 1) >> (S5_STATE.bit_length() - 1))


def _lane_tiler(dtype):
    return ((_iota2((S5_STATE, S5_LANES), 1) & (S5_STATE - 1)) == _iota2((S5_STATE, S5_LANES), 0)).astype(dtype)


def _row_to_col(row, n):
    eye = (_iota2((n, n), 0) == _iota2((n, n), 1)).astype(F32)
    return jnp.sum(eye * row, axis=1, keepdims=True)


def _group_repeat(G):
    return ((_iota2((G * S5_GROUP, G), 0) >> (S5_GROUP.bit_length() - 1)) == _iota2((G * S5_GROUP, G), 1)).astype(F32)


S5_TABS = 18


def _s5_prep_fwd(a_re, a_im, log_dt, b_re, b_im, c_re, c_im, after, seg):
    G, P = a_re.shape
    nb = G * S5_GROUP // S5_COLS
    g8 = S5_COLS // S5_GROUP
    assert seg & (seg - 1) == 0, seg

    def body(are_ref, aim_ref, ldt_ref, bre_ref, bim_ref, cre_ref, cim_ref, _after_ref,
             bbre_ref, bbim_ref, ctre_ref, ctim_ref, tab_ref, pt_ref):
        dt = jnp.exp(_row_to_col(ldt_ref[...], G))
        _, ab_re, ab_im, f_re, f_im, _, _ = _s5_disc(are_ref[...], aim_ref[...], dt)
        rep = _group_repeat(G)
        fx_re = _dot_hi(rep, f_re)
        fx_im = _dot_hi(rep, f_im)
        br, bi = bre_ref[...], bim_ref[...]
        bb_re = fx_re * br - fx_im * bi
        bb_im = fx_re * bi + fx_im * br
        tile_bf = _lane_tiler(BF16)
        mask = _group_mask(S5_COLS, S5_GROUP)
        for jb in range(nb):
            rs = slice(jb * S5_COLS, (jb + 1) * S5_COLS)
            for src, dst in ((bb_re[rs], bbre_ref), (bb_im[rs], bbim_ref), (cre_ref[rs, :], ctre_ref), (cim_ref[rs, :], ctim_ref)):
                dst[jb] = jnp.where(mask, _dot(src, tile_bf), 0.0).astype(BF16)

        tile_f = _lane_tiler(F32)
        mask8 = _group_mask(g8, 1)
        row = _iota2((SUBLANES, S5_LANES), 0)
        slab = (SUBLANES, S5_LANES)
        cmul = lambda p, q: (p[0] * q[0] - p[1] * q[1], p[0] * q[1] + p[1] * q[0])
        for jb in range(nb):
            gs = slice(jb * g8, (jb + 1) * g8)

            def lanes(m):
                v = jnp.sum(jnp.where(mask8, _dot_hi(m[gs], tile_f), 0.0), axis=0, keepdims=True)
                return jnp.broadcast_to(v, slab)

            a1 = (lanes(ab_re), lanes(ab_im))
            tab_ref[jb, 0], tab_ref[jb, 1] = a1

            def powers(k, p):
                s_re = s_im = jnp.zeros(slab, F32)
                for r in range(SUBLANES):
                    s_re = jnp.where(row == r, p[0], s_re)
                    s_im = jnp.where(row == r, p[1], s_im)
                    p = cmul(p, a1)
                pt_ref[jb, 0, _slab(k), :] = s_re
                pt_ref[jb, 1, _slab(k), :] = s_im
                return p

            lax.fori_loop(0, seg // SUBLANES, powers, a1)
            aseg = a1
            for _ in range(seg.bit_length() - 1):
                aseg = cmul(aseg, aseg)
            pw = [aseg]
            for _ in range(1, SUBLANES):
                pw.append(cmul(pw[-1], aseg))
            for lvl, k in enumerate((1, 2, 4)):
                tab_ref[jb, 2 + 2 * lvl] = jnp.where(row >= k, pw[k - 1][0], 0.0)
                tab_ref[jb, 3 + 2 * lvl] = jnp.where(row >= k, pw[k - 1][1], 0.0)
                tab_ref[jb, 10 + 2 * lvl] = jnp.where(row < SUBLANES - k, pw[k - 1][0], 0.0)
                tab_ref[jb, 11 + 2 * lvl] = jnp.where(row < SUBLANES - k, -pw[k - 1][1], 0.0)
            f_r = f_i = r_r = r_i = jnp.zeros(slab, F32)
            for i in range(SUBLANES):
                f_r = jnp.where(row == i, pw[i][0], f_r)
                f_i = jnp.where(row == i, pw[i][1], f_i)
                r_r = jnp.where(row == i, pw[SUBLANES - 1 - i][0], r_r)
                r_i = jnp.where(row == i, -pw[SUBLANES - 1 - i][1], r_i)
            tab_ref[jb, 8] = f_r
            tab_ref[jb, 9] = f_i
            tab_ref[jb, 16] = r_r
            tab_ref[jb, 17] = r_i

    vm = pl.BlockSpec(memory_space=pltpu.VMEM)
    bd = jax.ShapeDtypeStruct((nb, S5_COLS, S5_LANES), BF16)
    return pl.pallas_call(
        body, name="s5_prep_fwd",
        in_specs=[vm] * 7 + [pl.BlockSpec(memory_space=pl.ANY)], out_specs=[vm] * 6,
        out_shape=[bd, bd, bd, bd, jax.ShapeDtypeStruct((nb, S5_TABS, SUBLANES, S5_LANES), F32),
                   jax.ShapeDtypeStruct((nb, 2, seg, S5_LANES), F32)],
    )(a_re, a_im, log_dt, b_re, b_im, c_re, c_im, after)


def _s5_prep_bwd(a_re, a_im, log_dt, b_re, b_im, gbb_re, gbb_im, gct_re, gct_im, gab_re, gab_im):
    G, P = a_re.shape
    nb = G * S5_GROUP // S5_COLS
    g8 = S5_COLS // S5_GROUP

    def body(are_ref, aim_ref, ldt_ref, bre_ref, bim_ref, gbr_ref, gbi_ref, gcr_ref, gci_ref, gar_ref, gai_ref,
             o_a, o_bc, o_ldt):
        dt = jnp.exp(_row_to_col(ldt_ref[...], G))
        a_raw = are_ref[...]
        a_imv = aim_ref[...]
        a_re_c, ab_re, ab_im, f_re, f_im, ia_re, ia_im = _s5_disc(a_raw, a_imv, dt)
        tile_f = _lane_tiler(F32)
        mask = _group_mask(S5_COLS, S5_GROUP)
        mask8 = _group_mask(g8, 1)
        for jb in range(nb):
            rs = slice(jb * S5_COLS, (jb + 1) * S5_COLS)
            gs = slice(jb * g8, (jb + 1) * g8)
            ls = slice(jb * S5_LANES, (jb + 1) * S5_LANES)
            for k, src in enumerate((gbr_ref, gbi_ref, gcr_ref, gci_ref)):
                o_bc[k, rs, :] = _dot_hi(jnp.where(mask, src[jb], 0.0), tile_f, NT)
            for k, src in enumerate((gar_ref, gai_ref)):
                o_a[k, gs, :] = _dot_hi(jnp.where(mask8, src[:, ls], 0.0), tile_f, NT)
        rep = _group_repeat(G)
        fx_re = _dot_hi(rep, f_re)
        fx_im = _dot_hi(rep, f_im)
        gbr, gbi = o_bc[0], o_bc[1]
        br, bi = bre_ref[...], bim_ref[...]
        o_bc[0] = fx_re * gbr + fx_im * gbi
        o_bc[1] = fx_re * gbi - fx_im * gbr
        gf_re = _dot_hi(rep, br * gbr + bi * gbi, TN)
        gf_im = _dot_hi(rep, br * gbi - bi * gbr, TN)
        gab_r = o_a[0] + ia_re * gf_re + ia_im * gf_im
        gab_i = o_a[1] + ia_re * gf_im - ia_im * gf_re
        q_re = f_re * ia_re - f_im * ia_im
        q_im = f_re * ia_im + f_im * ia_re
        ga_re = -(q_re * gf_re + q_im * gf_im)
        ga_im = -(q_re * gf_im - q_im * gf_re)
        gth_re = ab_re * gab_r + ab_im * gab_i
        gth_im = ab_re * gab_i - ab_im * gab_r
        ga_re = ga_re + dt * gth_re
        ga_im = ga_im + dt * gth_im
        gdt = jnp.sum(a_re_c * gth_re + a_imv * gth_im, axis=-1, keepdims=True)
        eye = (_iota2((G, G), 0) == _iota2((G, G), 1)).astype(F32)
        o_ldt[...] = jnp.sum(eye * (gdt * dt), axis=0, keepdims=True)
        slope = jnp.where(a_raw < -1e-4, 1.0, jnp.where(a_raw == -1e-4, 0.5, 0.0))
        o_a[0] = ga_re * slope
        o_a[1] = ga_im

    vm = pl.BlockSpec(memory_space=pltpu.VMEM)
    return pl.pallas_call(
        body, name="s5_prep_bwd",
        in_specs=[vm] * 11, out_specs=[vm] * 3,
        out_shape=[jax.ShapeDtypeStruct((2, G, P), F32), jax.ShapeDtypeStruct((4, G * S5_GROUP, P), F32),
                   jax.ShapeDtypeStruct((1, G), F32)],
    )(a_re, a_im, log_dt, b_re, b_im, gbb_re, gbb_im, gct_re, gct_im, gab_re, gab_im)


def _scan8(xr, xi, tab_ref, base, shifts):
    for lvl, sh in enumerate(shifts):
        mr = tab_ref[0, base + 2 * lvl]
        mi = tab_ref[0, base + 2 * lvl + 1]
        ar = pltpu.roll(xr, sh, 0)
        ai = pltpu.roll(xi, sh, 0)
        xr, xi = xr + mr * ar - mi * ai, xi + mr * ai + mi * ar
    return xr, xi


def _to_segments(src_ref, dst_ref, seg):
    for i in range(seg):
        dst_ref[i * SUBLANES:(i + 1) * SUBLANES, :] = src_ref[pl.ds(i, SUBLANES, stride=seg), :]


def _from_segments(src_ref, dst_ref, seg):
    for i in range(seg):
        dst_ref[pl.ds(i, SUBLANES, stride=seg), :] = src_ref[i * SUBLANES:(i + 1) * SUBLANES, :]


def _slab(i):
    return pl.ds(pl.multiple_of(i * SUBLANES, SUBLANES), SUBLANES)


def _s5_scan_fwd(proj_main, bbd_re, bbd_im, cbd_re, cbd_im, dvec, tab, ptab, DS):
    L = proj_main.shape[0]
    nb = DS // S5_COLS
    tb = _blk(L, S5_TIME_BLOCK, SUBLANES)
    nt = L // tb
    seg = tb // SUBLANES

    def body(u_ref, bre_ref, bim_ref, cre_ref, cim_ref, d_ref, tab_ref, pt_ref, y_ref, sre_ref, sim_ref,
             up_ref, yp_ref, car_ref):
        t = pl.program_id(1)

        @pl.when(t == 0)
        def _():
            car_ref[...] = jnp.zeros_like(car_ref)

        _to_segments(u_ref, up_ref, seg)
        up = up_ref[...]
        sre_ref[...] = _dot(up, bre_ref[0])
        sim_ref[...] = _dot(up, bim_ref[0])
        ar, ai = tab_ref[0, 0], tab_ref[0, 1]

        def pass1(i, x):
            xr = ar * x[0] - ai * x[1] + sre_ref[_slab(i), :]
            xi = ar * x[1] + ai * x[0] + sim_ref[_slab(i), :]
            sre_ref[_slab(i), :] = xr
            sim_ref[_slab(i), :] = xi
            return xr, xi

        zero = jnp.zeros((SUBLANES, S5_LANES), F32)
        er, ei = lax.fori_loop(0, seg, pass1, (zero, zero))
        cin_r, cin_i = car_ref[0], car_ref[1]
        sr, si = _scan8(er, ei, tab_ref, 2, (1, 2, 4))
        pr, pi = tab_ref[0, 8], tab_ref[0, 9]
        sr, si = sr + pr * cin_r - pi * cin_i, si + pr * cin_i + pi * cin_r
        row0 = _iota2((SUBLANES, S5_LANES), 0) == 0
        cr = jnp.where(row0, cin_r, pltpu.roll(sr, 1, 0))
        ci = jnp.where(row0, cin_i, pltpu.roll(si, 1, 0))
        car_ref[0] = jnp.broadcast_to(sr[SUBLANES - 1:SUBLANES, :], sr.shape)
        car_ref[1] = jnp.broadcast_to(si[SUBLANES - 1:SUBLANES, :], si.shape)

        def pass2(i, _):
            qr, qi = pt_ref[0, 0, pl.ds(i, 1), :], pt_ref[0, 1, pl.ds(i, 1), :]
            sre_ref[_slab(i), :] += qr * cr - qi * ci
            sim_ref[_slab(i), :] += qr * ci + qi * cr
            return 0

        lax.fori_loop(0, seg, pass2, 0, unroll=4)
        yp_ref[...] = _dot(sre_ref[...], cre_ref[0], NT) - _dot(sim_ref[...], cim_ref[0], NT) + d_ref[...] * up
        _from_segments(yp_ref, y_ref, seg)

    return pl.pallas_call(
        body, name="s5_scan_fwd", grid=(nb, nt),
        in_specs=[
            pl.BlockSpec((tb, S5_COLS), lambda j, t: (t, j)),
            pl.BlockSpec((1, S5_COLS, S5_LANES), lambda j, t: (j, 0, 0)),
            pl.BlockSpec((1, S5_COLS, S5_LANES), lambda j, t: (j, 0, 0)),
            pl.BlockSpec((1, S5_COLS, S5_LANES), lambda j, t: (j, 0, 0)),
            pl.BlockSpec((1, S5_COLS, S5_LANES), lambda j, t: (j, 0, 0)),
            pl.BlockSpec((1, S5_COLS), lambda j, t: (0, j)),
            pl.BlockSpec((1, S5_TABS, SUBLANES, S5_LANES), lambda j, t: (j, 0, 0, 0)),
            pl.BlockSpec((1, 2, seg, S5_LANES), lambda j, t: (j, 0, 0, 0)),
        ],
        out_specs=[
            pl.BlockSpec((tb, S5_COLS), lambda j, t: (t, j)),
            pl.BlockSpec((tb, S5_LANES), lambda j, t: (t, j)),
            pl.BlockSpec((tb, S5_LANES), lambda j, t: (t, j)),
        ],
        out_shape=[jax.ShapeDtypeStruct((L, DS), F32),
                   jax.ShapeDtypeStruct((L, nb * S5_LANES), F32),
                   jax.ShapeDtypeStruct((L, nb * S5_LANES), F32)],
        scratch_shapes=[pltpu.VMEM((tb, S5_COLS), F32), pltpu.VMEM((tb, S5_COLS), F32),
                        pltpu.VMEM((2, SUBLANES, S5_LANES), F32)],
        compiler_params=pltpu.CompilerParams(dimension_semantics=("parallel", "arbitrary")),
    )(proj_main, bbd_re, bbd_im, cbd_re, cbd_im, dvec, tab, ptab)


def _s5_scan_bwd(dy, proj_main, s_re, s_im, bbd_re, bbd_im, cbd_re, cbd_im, dvec, tab, ptab, d_s5, DS):
    L = proj_main.shape[0]
    nb = DS // S5_COLS
    tb = _blk(L, S5_TIME_BLOCK, SUBLANES)
    nt = L // tb
    seg = tb // SUBLANES
    tb8 = tb // SUBLANES

    def body(dy_ref, u_ref, sre_ref, sim_ref, pre_ref, pim_ref, bre_ref, bim_ref, cre_ref, cim_ref, d_ref, tab_ref, pt_ref,
             _ds5_ref, du_ref, gd_ref, gcre_ref, gcim_ref, gbre_ref, gbim_ref, gare_ref, gaim_ref,
             lre_ref, lim_ref, up_ref, dyp_ref, dup_ref, duo_ref, car_ref):
        t = pl.program_id(1)

        @pl.when(t == 0)
        def _():
            car_ref[...] = jnp.zeros_like(car_ref)
            gd_ref[...] = jnp.zeros_like(gd_ref)
            gcre_ref[...] = jnp.zeros_like(gcre_ref)
            gcim_ref[...] = jnp.zeros_like(gcim_ref)
            gbre_ref[...] = jnp.zeros_like(gbre_ref)
            gbim_ref[...] = jnp.zeros_like(gbim_ref)
            gare_ref[...] = jnp.zeros_like(gare_ref)
            gaim_ref[...] = jnp.zeros_like(gaim_ref)

        _to_segments(dy_ref, dyp_ref, seg)
        _to_segments(u_ref, up_ref, seg)
        dyv = dyp_ref[...]
        u = up_ref[...]
        gd_ref[...] += jnp.sum(dyv * u, axis=0, keepdims=True)
        lre_ref[...] = _dot(dyv, cre_ref[0])
        lim_ref[...] = -_dot(dyv, cim_ref[0])
        gcre_ref[0] += _dot(dyv, sre_ref[...], TN)
        gcim_ref[0] -= _dot(dyv, sim_ref[...], TN)
        ar, ai = tab_ref[0, 0], -tab_ref[0, 1]

        def pass1(k, x):
            i = seg - 1 - k
            xr = ar * x[0] - ai * x[1] + lre_ref[_slab(i), :]
            xi = ar * x[1] + ai * x[0] + lim_ref[_slab(i), :]
            lre_ref[_slab(i), :] = xr
            lim_ref[_slab(i), :] = xi
            return xr, xi

        zero = jnp.zeros((SUBLANES, S5_LANES), F32)
        er, ei = lax.fori_loop(0, seg, pass1, (zero, zero))
        cin_r, cin_i = car_ref[0], car_ref[1]
        lr, li = _scan8(er, ei, tab_ref, 10, (7, 6, 4))
        pr, pi = tab_ref[0, 16], tab_ref[0, 17]
        lr, li = lr + pr * cin_r - pi * cin_i, li + pr * cin_i + pi * cin_r
        rows = _iota2((SUBLANES, S5_LANES), 0)
        cr = jnp.where(rows == SUBLANES - 1, cin_r, pltpu.roll(lr, SUBLANES - 1, 0))
        ci = jnp.where(rows == SUBLANES - 1, cin_i, pltpu.roll(li, SUBLANES - 1, 0))
        car_ref[0] = jnp.broadcast_to(lr[0:1, :], lr.shape)
        car_ref[1] = jnp.broadcast_to(li[0:1, :], li.shape)

        first = (t == nt - 1).astype(F32)
        head_re = jnp.broadcast_to(pre_ref[SUBLANES - 1:SUBLANES, :], zero.shape) * (1.0 - first)
        head_im = jnp.broadcast_to(pim_ref[SUBLANES - 1:SUBLANES, :], zero.shape) * (1.0 - first)
        last = _slab(seg - 1)
        sp0_re = jnp.where(rows == 0, head_re, pltpu.roll(sre_ref[last, :], 1, 0))
        sp0_im = jnp.where(rows == 0, head_im, pltpu.roll(sim_ref[last, :], 1, 0))

        def fix(i, acc, sp_re, sp_im):
            j = seg - 1 - i
            qr, qi = pt_ref[0, 0, pl.ds(j, 1), :], -pt_ref[0, 1, pl.ds(j, 1), :]
            xr = lre_ref[_slab(i), :] + qr * cr - qi * ci
            xi = lim_ref[_slab(i), :] + qr * ci + qi * cr
            lre_ref[_slab(i), :] = xr
            lim_ref[_slab(i), :] = xi
            return acc[0] + sp_re * xr + sp_im * xi, acc[1] + sp_re * xi - sp_im * xr

        def pass2(i, acc):
            prev = _slab(jnp.maximum(i - 1, 0))
            return fix(i, acc, sre_ref[prev, :], sim_ref[prev, :])

        acc_re, acc_im = lax.fori_loop(0, seg, pass2, (zero, zero), unroll=4)
        first_slab = _slab(0)
        d_re, d_im = sp0_re - sre_ref[first_slab, :], sp0_im - sim_ref[first_slab, :]
        x0r, x0i = lre_ref[first_slab, :], lim_ref[first_slab, :]
        acc_re = acc_re + d_re * x0r + d_im * x0i
        acc_im = acc_im + d_re * x0i - d_im * x0r
        gare_ref[...] += jnp.sum(acc_re, axis=0, keepdims=True)
        gaim_ref[...] += jnp.sum(acc_im, axis=0, keepdims=True)
        lre = lre_ref[...]
        lim = lim_ref[...]
        dup_ref[...] = dyv * d_ref[...] + _dot(lre, bre_ref[0], NT) + _dot(lim, bim_ref[0], NT)
        _from_segments(dup_ref, duo_ref, seg)
        du_ref[...] = duo_ref[...].astype(BF16)
        gbre_ref[0] += _dot(u, lre, TN)
        gbim_ref[0] += _dot(u, lim, TN)

    rt = lambda t: nt - 1 - t
    col = pl.BlockSpec((tb, S5_COLS), lambda j, t: (rt(t), j))
    st = pl.BlockSpec((tb, S5_LANES), lambda j, t: (rt(t), j))
    prev = pl.BlockSpec((SUBLANES, S5_LANES), lambda j, t: (jnp.maximum(rt(t) * tb8 - 1, 0), j))
    bmat = pl.BlockSpec((1, S5_COLS, S5_LANES), lambda j, t: (j, 0, 0))
    cmat = bmat
    return pl.pallas_call(
        body, name="s5_scan_bwd", grid=(nb, nt),
        in_specs=[col, col, st, st, prev, prev, bmat, bmat, cmat, cmat,
                  pl.BlockSpec((1, S5_COLS), lambda j, t: (0, j)),
                  pl.BlockSpec((1, S5_TABS, SUBLANES, S5_LANES), lambda j, t: (j, 0, 0, 0)),
                  pl.BlockSpec((1, 2, seg, S5_LANES), lambda j, t: (j, 0, 0, 0)),
                  pl.BlockSpec(memory_space=pl.ANY)],
        out_specs=[col, pl.BlockSpec((1, S5_COLS), lambda j, t: (0, j)), cmat, cmat, bmat, bmat,
                   pl.BlockSpec((1, S5_LANES), lambda j, t: (0, j)), pl.BlockSpec((1, S5_LANES), lambda j, t: (0, j))],
        input_output_aliases={13: 0},
        out_shape=[jax.ShapeDtypeStruct((L, 2 * DS), BF16), jax.ShapeDtypeStruct((1, DS), F32),
                   jax.ShapeDtypeStruct((nb, S5_COLS, S5_LANES), F32), jax.ShapeDtypeStruct((nb, S5_COLS, S5_LANES), F32),
                   jax.ShapeDtypeStruct((nb, S5_COLS, S5_LANES), F32), jax.ShapeDtypeStruct((nb, S5_COLS, S5_LANES), F32),
                   jax.ShapeDtypeStruct((1, nb * S5_LANES), F32), jax.ShapeDtypeStruct((1, nb * S5_LANES), F32)],
        scratch_shapes=[pltpu.VMEM((tb, S5_LANES), F32), pltpu.VMEM((tb, S5_LANES), F32)]
        + [pltpu.VMEM((tb, S5_COLS), F32)] * 4 + [pltpu.VMEM((2, SUBLANES, S5_LANES), F32)],
        compiler_params=pltpu.CompilerParams(dimension_semantics=("parallel", "arbitrary")),
    )(dy, proj_main, s_re, s_im, s_re, s_im, bbd_re, bbd_im, cbd_re, cbd_im, dvec, tab, ptab, d_s5)


def _s5_post_fwd(y_pre, proj_main, glu_w, glu_b, DS):
    L = y_pre.shape[0]
    tr = _blk(L, ROW_TILE, SUBLANES)

    def body(y_ref, z_ref, w_ref, b_ref, o_ref, t_ref):
        y1 = _gelu(y_ref[...])
        t = _dot(y1, w_ref[...]) + b_ref[...]
        t_ref[...] = t
        z = z_ref[...]
        o_ref[...] = (y1 * _sigmoid(t) * (z * _sigmoid(z))).astype(BF16)

    row = pl.BlockSpec((tr, DS), lambda i: (i, 0))
    return pl.pallas_call(
        body, name="s5_post_fwd", grid=(L // tr,),
        in_specs=[row, pl.BlockSpec((tr, DS), lambda i: (i, 1)), pl.BlockSpec((DS, DS), lambda i: (0, 0)),
                  pl.BlockSpec((1, DS), lambda i: (0, 0))],
        out_specs=[row, row],
        out_shape=[jax.ShapeDtypeStruct((L, 2 * DS), BF16), jax.ShapeDtypeStruct((L, DS), F32)],
        compiler_params=pltpu.CompilerParams(dimension_semantics=("parallel",)),
    )(y_pre, proj_main, glu_w, glu_b)


def _s5_post_bwd(d_ycat, y_pre, proj_main, t_pre, glu_w, DS):
    L = y_pre.shape[0]
    tr = _blk(L, ROW_TILE, SUBLANES)

    def body(dy_ref, y_ref, z_ref, t_ref, w_ref, dyp_ref, dz_ref, dt_ref, y1_ref, gb_ref):
        i = pl.program_id(0)

        @pl.when(i == 0)
        def _():
            gb_ref[...] = jnp.zeros_like(gb_ref)

        dy = dy_ref[...]
        yp = y_ref[...]
        z = z_ref[...]
        y1 = _gelu(yp)
        sg = _sigmoid(t_ref[...])
        sz = _sigmoid(z)
        c = y1 * sg
        d_c = dy * (z * sz)
        dz_ref[...] = (dy * c * (sz * (1.0 + z * (1.0 - sz)))).astype(BF16)
        d_t = d_c * y1 * sg * (1.0 - sg)
        gb_ref[...] += jnp.sum(d_t, axis=0, keepdims=True)
        dt_ref[...] = d_t.astype(BF16)
        y1_ref[...] = y1.astype(BF16)
        d_y1 = d_c * sg + _dot(d_t, w_ref[...], NT)
        dyp_ref[...] = d_y1 * _gelu_grad(yp)

    row = pl.BlockSpec((tr, DS), lambda i: (i, 0))
    return pl.pallas_call(
        body, name="s5_post_bwd", grid=(L // tr,),
        in_specs=[row, row, pl.BlockSpec((tr, DS), lambda i: (i, 1)), row, pl.BlockSpec((DS, DS), lambda i: (0, 0))],
        out_specs=[row, pl.BlockSpec((tr, DS), lambda i: (i, 1)), row, row, pl.BlockSpec((1, DS), lambda i: (0, 0))],
        out_shape=[jax.ShapeDtypeStruct((L, DS), F32), jax.ShapeDtypeStruct((L, 2 * DS), BF16),
                   jax.ShapeDtypeStruct((L, DS), BF16), jax.ShapeDtypeStruct((L, DS), BF16),
                   jax.ShapeDtypeStruct((1, DS), F32)],
        compiler_params=pltpu.CompilerParams(dimension_semantics=("arbitrary",)),
    )(d_ycat, y_pre, proj_main, t_pre, glu_w)


def _row_cumsum(x, reverse=False):
    n = x.shape[0]
    row = lax.broadcasted_iota(jnp.int32, x.shape, 0)
    k = 1
    while k < n:
        if reverse:
            x = x + jnp.where(row < n - k, pltpu.roll(x, n - k, 0), 0.0)
        else:
            x = x + jnp.where(row >= k, pltpu.roll(x, k, 0), 0.0)
        k *= 2
    return x


def _gla_gates(glow, gu_ref, gb_ref):
    a = _dot(glow, gu_ref[...]) + gb_ref[...]
    lg = (jnp.minimum(a, 0.0) - jnp.log(1.0 + jnp.exp(-jnp.abs(a)))) * (1.0 / GLA_TAU)
    ri = lax.broadcasted_iota(jnp.int32, (GLA_CHUNK, GLA_CHUNK), 0)
    ci = lax.broadcasted_iota(jnp.int32, (GLA_CHUNK, GLA_CHUNK), 1)
    b = _row_cumsum(lg)
    b_last = b[GLA_CHUNK - 1:GLA_CHUNK, :]
    return a, b, b_last, ri >= ci


def _gla_specs(DS, DK, DV, c, cmap):
    return [
        pl.BlockSpec((c, DK), lambda n: (cmap(n), 2 * DS // DK)),
        pl.BlockSpec((c, DK), lambda n: (cmap(n), 2 * DS // DK + 1)),
        pl.BlockSpec((c, DV), lambda n: (cmap(n), (2 * DS + 2 * DK) // DV)),
        pl.BlockSpec((c, DV), lambda n: (cmap(n), (2 * DS + 2 * DK) // DV + 1)),
    ]


def _gla_fwd(proj_main, proj_low, gate_up_pad, gate_bias, norm_w, ycat, DS, DK, DV):
    L = proj_main.shape[0]
    nc = L // GLA_CHUNK
    cps = math.gcd(GLA_STEP_CHUNKS, nc)
    nh = DK // GLA_HK
    scale = GLA_HK ** -0.5

    def body(q_ref, k_ref, v_ref, z_ref, gl_ref, gu_ref, gb_ref, nw_ref, _yc_ref, y_ref, sp_ref, at_ref, o_ref, st_ref):
        n = pl.program_id(0)

        @pl.when(n == 0)
        def _():
            st_ref[...] = jnp.zeros_like(st_ref)

        pairs = [(sc, h) for sc in range(cps) for h in range(nh)]
        rows = lambda sc: slice(sc * GLA_CHUNK, (sc + 1) * GLA_CHUNK)
        kcol = lambda h: slice(h * GLA_HK, (h + 1) * GLA_HK)
        vcol = lambda h: slice(h * GLA_HV, (h + 1) * GLA_HV)
        gates = [_gla_gates(gl_ref[rows(sc), :], gu_ref, gb_ref) for sc in range(cps)]
        qe, dec, o_in, kv = {}, {}, {}, {}
        for sc, h in pairs:
            _, b, b_last, mask = gates[sc]
            bh, bl = b[:, kcol(h)], b_last[:, kcol(h)]
            qe[sc, h] = (q_ref[rows(sc), kcol(h)] * scale) * jnp.exp(bh)
            kh = k_ref[rows(sc), kcol(h)]
            vh = v_ref[rows(sc), vcol(h)]
            attn = jnp.where(mask, _dot(qe[sc, h], kh * jnp.exp(-bh), NT), 0.0).astype(BF16)
            at_ref[h, rows(sc), :] = attn
            o_in[sc, h] = _dot(attn, vh)
            kv[sc, h] = _dot(vh, kh * jnp.exp(bl - bh), TN)
            dec[sc, h] = jnp.exp(bl)
        for sc, h in pairs:
            st = st_ref[h]
            sp_ref[sc, h] = st
            o = o_in[sc, h] + _dot(qe[sc, h], st, NT)
            o_ref[rows(sc), vcol(h)] = o
            st_ref[h] = dec[sc, h] * st + kv[sc, h]
            r = lax.rsqrt(jnp.mean(o * o, axis=-1, keepdims=True) + EPS)
            z = z_ref[rows(sc), vcol(h)]
            y_ref[rows(sc), vcol(h)] = (o * r * nw_ref[...] * (z * _sigmoid(z))).astype(BF16)

    c = cps * GLA_CHUNK
    return pl.pallas_call(
        body, name="gla_fwd", grid=(nc // cps,),
        in_specs=_gla_specs(DS, DK, DV, c, lambda n: n) + [
            pl.BlockSpec((c, LANES), lambda n: (n, 0)),
            pl.BlockSpec((LANES, DK), lambda n: (0, 0)),
            pl.BlockSpec((1, DK), lambda n: (0, 0)),
            pl.BlockSpec((1, GLA_HV), lambda n: (0, 0)),
            pl.BlockSpec(memory_space=pl.ANY),
        ],
        out_specs=[pl.BlockSpec((c, DV), lambda n: (n, DS // DV)),
                   pl.BlockSpec((cps, nh, GLA_HV, GLA_HK), lambda n: (n, 0, 0, 0)),
                   pl.BlockSpec((nh, c, GLA_CHUNK), lambda n: (0, n, 0)),
                   pl.BlockSpec((c, DV), lambda n: (n, 0))],
        input_output_aliases={8: 0},
        out_shape=[jax.ShapeDtypeStruct(ycat.shape, BF16), jax.ShapeDtypeStruct((nc, nh, GLA_HV, GLA_HK), F32),
                   jax.ShapeDtypeStruct((nh, L, GLA_CHUNK), BF16), jax.ShapeDtypeStruct((L, DV), F32)],
        scratch_shapes=[pltpu.VMEM((nh, GLA_HV, GLA_HK), F32)],
        compiler_params=pltpu.CompilerParams(dimension_semantics=("arbitrary",)),
    )(proj_main, proj_main, proj_main, proj_main, proj_low, gate_up_pad, gate_bias, norm_w, ycat)


def _gla_bwd(d_ycat, proj_main, proj_low, s_prev, scores, o_pre, gate_up_pad, gate_bias, norm_w, DS, DK, DV):
    L = proj_main.shape[0]
    nc = L // GLA_CHUNK
    cps = math.gcd(GLA_STEP_CHUNKS, nc)
    nh = DK // GLA_HK
    scale = GLA_HK ** -0.5

    def body(dy_ref, q_ref, k_ref, v_ref, z_ref, gl_ref, sp_ref, at_ref, o_ref, gu_ref, gb_ref, nw_ref,
             dg_ref, da_ref, gnw_ref, ggb_ref, dst_ref):
        n = pl.program_id(0)

        @pl.when(n == 0)
        def _():
            dst_ref[...] = jnp.zeros_like(dst_ref)
            gnw_ref[...] = jnp.zeros_like(gnw_ref)
            ggb_ref[...] = jnp.zeros_like(ggb_ref)

        last_row = lax.broadcasted_iota(jnp.int32, (GLA_CHUNK, GLA_HK), 0) == GLA_CHUNK - 1
        nw = nw_ref[...]
        for sc in reversed(range(cps)):
            rs = slice(sc * GLA_CHUNK, (sc + 1) * GLA_CHUNK)
            a, b, b_last, mask = _gla_gates(gl_ref[rs, :], gu_ref, gb_ref)
            for h in range(nh):
                ks = slice(h * GLA_HK, (h + 1) * GLA_HK)
                vs = slice(h * GLA_HV, (h + 1) * GLA_HV)
                bh, bl = b[:, ks], b_last[:, ks]
                e = jnp.exp(bh)
                einv = jnp.exp(-bh)
                etail = jnp.exp(bl - bh)
                dec = jnp.exp(bl)
                qe = (q_ref[rs, ks] * scale) * e
                kh = k_ref[rs, ks]
                ke = kh * einv
                ktail = kh * etail
                vh = v_ref[rs, vs]
                st = sp_ref[sc, h]
                dst = dst_ref[h]
                attn = at_ref[h, rs, :]
                o = o_ref[rs, vs]
                r = lax.rsqrt(jnp.mean(o * o, axis=-1, keepdims=True) + EPS)
                nrm = o * r
                z = z_ref[rs, vs]
                sz = _sigmoid(z)
                dy = dy_ref[rs, vs]
                dg_ref[rs, 2 * DK + DV + h * GLA_HV:2 * DK + DV + (h + 1) * GLA_HV] = (
                    dy * nrm * nw * (sz * (1.0 + z * (1.0 - sz)))).astype(BF16)
                d_on = dy * (z * sz)
                gnw_ref[...] += jnp.sum(d_on * nrm, axis=0, keepdims=True)
                d_n = d_on * nw
                d_o = r * (d_n - nrm * jnp.mean(d_n * nrm, axis=-1, keepdims=True))
                d_attn = jnp.where(mask, _dot(d_o, vh, NT), 0.0)
                dg_ref[rs, 2 * DK + h * GLA_HV:2 * DK + (h + 1) * GLA_HV] = (
                    _dot(attn, d_o, TN) + _dot(ktail, dst, NT)).astype(BF16)
                d_qe = _dot(d_attn, ke) + _dot(d_o, st)
                d_ke = _dot(d_attn, qe, TN)
                d_kt = _dot(vh, dst)
                d_dec = jnp.sum(dst * st, axis=0, keepdims=True)
                dst_ref[h] = dec * dst + _dot(d_o, qe, TN)
                dg_ref[rs, ks] = (d_qe * scale * e).astype(BF16)
                dg_ref[rs, DK + h * GLA_HK:DK + (h + 1) * GLA_HK] = (d_ke * einv + d_kt * etail).astype(BF16)
                d_bl = jnp.sum(d_kt * ktail, axis=0, keepdims=True) + d_dec * dec
                d_b = d_qe * qe - d_ke * ke - d_kt * ktail + jnp.where(last_row, d_bl, 0.0)
                d_lg = _row_cumsum(d_b, reverse=True)
                d_a = d_lg * (1.0 / GLA_TAU) * _sigmoid(-a[:, ks])
                ggb_ref[:, ks] += jnp.sum(d_a, axis=0, keepdims=True)
                da_ref[rs, ks] = d_a.astype(BF16)

    c = cps * GLA_CHUNK
    ns = nc // cps
    rn = lambda n: ns - 1 - n
    return pl.pallas_call(
        body, name="gla_bwd", grid=(ns,),
        in_specs=[pl.BlockSpec((c, DV), lambda n: (rn(n), DS // DV))] + _gla_specs(DS, DK, DV, c, rn) + [
            pl.BlockSpec((c, LANES), lambda n: (rn(n), 0)),
            pl.BlockSpec((cps, nh, GLA_HV, GLA_HK), lambda n: (rn(n), 0, 0, 0)),
            pl.BlockSpec((nh, c, GLA_CHUNK), lambda n: (0, rn(n), 0)),
            pl.BlockSpec((c, DV), lambda n: (rn(n), 0)),
            pl.BlockSpec((LANES, DK), lambda n: (0, 0)),
            pl.BlockSpec((1, DK), lambda n: (0, 0)),
            pl.BlockSpec((1, GLA_HV), lambda n: (0, 0)),
        ],
        out_specs=[pl.BlockSpec((c, 2 * DK + 2 * DV), lambda n: (rn(n), 0)),
                   pl.BlockSpec((c, DK), lambda n: (rn(n), 0)),
                   pl.BlockSpec((1, GLA_HV), lambda n: (0, 0)), pl.BlockSpec((1, DK), lambda n: (0, 0))],
        out_shape=[jax.ShapeDtypeStruct((L, 2 * DK + 2 * DV), BF16),
                   jax.ShapeDtypeStruct((L, DK), BF16),
                   jax.ShapeDtypeStruct((1, GLA_HV), F32), jax.ShapeDtypeStruct((1, DK), F32)],
        scratch_shapes=[pltpu.VMEM((nh, GLA_HV, GLA_HK), F32)],
        compiler_params=pltpu.CompilerParams(dimension_semantics=("arbitrary",)),
    )(d_ycat, proj_main, proj_main, proj_main, proj_main, proj_low, s_prev, scores, o_pre, gate_up_pad, gate_bias, norm_w)


def _adamw_math(w, g, m, v):
    c1 = 1.0 - ADAM_B1 ** ADAM_STEP
    c2 = 1.0 - ADAM_B2 ** ADAM_STEP
    m_ = ADAM_B1 * m + (1.0 - ADAM_B1) * g
    v_ = ADAM_B2 * v + (1.0 - ADAM_B2) * (g * g)
    return -ADAM_LR * ((m_ / c1) / (jnp.sqrt(v_ / c2) + ADAM_EPS) + ADAM_WD * w), m_, v_


def _adamw_small(g_row, g_a, g_bc, ws, ms, vs):
    n = len(ws)
    nvec = n - 6

    def body(*refs):
        grow_ref, ga_ref, gbc_ref = refs[:3]
        w_refs, m_refs, v_refs = refs[3:3 + n], refs[3 + n:3 + 2 * n], refs[3 + 2 * n:3 + 3 * n]
        outs = refs[3 + 3 * n:]
        off = 0
        for i in range(n):
            if i < nvec:
                width = ws[i].shape[1]
                g = grow_ref[:, off:off + width]
                off += width
            elif i < nvec + 2:
                g = ga_ref[i - nvec]
            else:
                g = gbc_ref[i - nvec - 2]
            d, m_, v_ = _adamw_math(w_refs[i][...], g, m_refs[i][...], v_refs[i][...])
            outs[i][...] = g
            outs[n + i][...] = d
            outs[2 * n + i][...] = m_
            outs[3 * n + i][...] = v_

    vm = pl.BlockSpec(memory_space=pltpu.VMEM)
    outs = pl.pallas_call(
        body, name="adamw_small",
        in_specs=[vm] * (3 + 3 * n), out_specs=[vm] * (4 * n),
        out_shape=[jax.ShapeDtypeStruct(w.shape, F32) for w in ws] * 4,
    )(g_row, g_a, g_bc, *ws, *ms, *vs)
    return [outs[k * n:(k + 1) * n] for k in range(4)]


def _my_pos():
    return lax.axis_index("x"), lax.axis_index("y"), lax.axis_index("c")


def _split_start(name, srcs, lands_sd, make_copies, ncopies, after):
    n, m = len(srcs), len(lands_sd)

    def body(*refs):
        send_sems, recv_sems = refs[n + m + len(after)], refs[n + m + len(after) + 1]
        for cp in make_copies(refs[:n], refs[n:n + m], send_sems, recv_sems):
            cp.start()
        refs[-1][...] = jnp.zeros_like(refs[-1])

    hbm = pl.BlockSpec(memory_space=pltpu.HBM)
    sem = pl.BlockSpec(memory_space=pltpu.SEMAPHORE)
    outs = pl.pallas_call(
        body, name=name,
        in_specs=[hbm] * (n + m) + [pl.BlockSpec(memory_space=pl.ANY)] * len(after),
        out_specs=[sem, sem] + [hbm] * (n + m) + [pl.BlockSpec(memory_space=pltpu.VMEM)],
        out_shape=[pltpu.SemaphoreType.DMA((ncopies,)), pltpu.SemaphoreType.DMA((ncopies,))]
        + [pltpu.HBM(s.shape, s.dtype) for s in srcs] + [pltpu.HBM(s.shape, s.dtype) for s in lands_sd]
        + [jax.ShapeDtypeStruct((SUBLANES, LANES), F32)],
        input_output_aliases={i: 2 + i for i in range(n + m)},
        compiler_params=pltpu.CompilerParams(has_side_effects=pltpu.SideEffectType.DATAFLOW_SIDE_EFFECTING),
    )(*[pltpu.with_memory_space_constraint(s, pltpu.HBM) for s in srcs],
      *[pltpu.with_memory_space_constraint(lax.empty(s.shape, s.dtype), pltpu.HBM) for s in lands_sd], *after)
    return outs[0], outs[1], outs[2:2 + n], outs[2 + n:2 + n + m], outs[-1]


def _split_wait(name, send_sems, recv_sems, srcs, lands, make_copies, after):
    n, m = len(srcs), len(lands)

    def body(*refs):
        for cp in make_copies(refs[:n], refs[n:n + m], refs[n + m], refs[n + m + 1]):
            cp.wait_send()
            cp.wait_recv()

    hbm = pl.BlockSpec(memory_space=pltpu.HBM)
    sem = pl.BlockSpec(memory_space=pltpu.SEMAPHORE)
    outs = pl.pallas_call(
        body, name=name,
        in_specs=[hbm] * (n + m) + [sem, sem] + [pl.BlockSpec(memory_space=pl.ANY)] * len(after),
        out_specs=[hbm] * (n + m),
        out_shape=[pltpu.HBM(s.shape, s.dtype) for s in srcs] + [pltpu.HBM(p.shape, p.dtype) for p in lands],
        input_output_aliases={i: i for i in range(n + m)},
        compiler_params=pltpu.CompilerParams(has_side_effects=pltpu.SideEffectType.DATAFLOW_SIDE_EFFECTING),
    )(*srcs, *lands, send_sems, recv_sems, *after)
    return outs[:n], outs[n:]


def _pair_half_copies(srcs, lands, send_sems, recv_sems):
    x, y, c = _my_pos()
    copies = []
    for a in range(len(srcs)):
        hrows = srcs[a].shape[1] // 2
        copies.append(pltpu.make_async_remote_copy(
            src_ref=srcs[a].at[:, pl.ds((1 - c) * hrows, hrows), :], dst_ref=lands[a], send_sem=send_sems.at[a],
            recv_sem=recv_sems.at[a], device_id=(x, y, 1 - c), device_id_type=MESH))
    return copies


def _late_gather_copies(srcs, lands, send_sems, recv_sems):
    x, y, c = _my_pos()
    me = 2 * x + y
    copies = []
    for d in (1, 2, 3):
        to = (x ^ (d >> 1), y ^ (d & 1), c)
        for a in range(len(srcs)):
            hrows = srcs[a].shape[0] // 2
            rows = pl.ds(c * hrows, hrows)
            copies.append(pltpu.make_async_remote_copy(
                src_ref=srcs[a].at[rows, :], dst_ref=lands[a].at[me, rows, :], send_sem=send_sems.at[3 * a + d - 1],
                recv_sem=recv_sems.at[3 * a + d - 1], device_id=to, device_id_type=MESH))
    return copies


def _late_gather_start(shards, after, name):
    lands = [jax.ShapeDtypeStruct((4,) + s.shape, s.dtype) for s in shards]
    return _split_start(name, shards, lands, _late_gather_copies, 3 * len(shards), [after])


def _late_gather_wait(send_sems, recv_sems, shards, lands, after, name):
    return _split_wait(name, send_sems, recv_sems, shards, lands, _late_gather_copies, after)[1]


def _late_gather_pair(lands, name):
    n = len(lands)

    def body(*refs):
        outs = refs[n:2 * n]
        send_sems, recv_sems = refs[2 * n:]
        x, y, c = _my_pos()

        def copy(a, d, half):
            chip = 2 * (x ^ (d >> 1)) + (y ^ (d & 1))
            hrows = lands[a].shape[1] // 2
            sl = outs[a].at[chip, pl.ds(half * hrows, hrows), :]
            return pltpu.make_async_remote_copy(src_ref=sl, dst_ref=sl, send_sem=send_sems.at[3 * a + d - 1],
                                                recv_sem=recv_sems.at[3 * a + d - 1], device_id=(x, y, 1 - c),
                                                device_id_type=MESH)

        pairs = [(a, d) for d in (1, 2, 3) for a in range(n)]
        for a, d in pairs:
            copy(a, d, c).start()
        for a, d in pairs:
            copy(a, d, c).wait_send()
            copy(a, d, 1 - c).wait_recv()

    hbm = pl.BlockSpec(memory_space=pltpu.HBM)
    return pl.pallas_call(
        body, name=name, in_specs=[hbm] * n, out_specs=[hbm] * n,
        out_shape=[jax.ShapeDtypeStruct(p.shape, p.dtype) for p in lands],
        input_output_aliases={i: i for i in range(n)},
        scratch_shapes=[pltpu.SemaphoreType.DMA((3 * n,)), pltpu.SemaphoreType.DMA((3 * n,))],
    )(*lands)


def _pair_add(g, got, c_arr, name):
    nk, rows2, cols = g.shape
    hrows = rows2 // 2
    tr = _blk(hrows, 256, 2 * SUBLANES)
    nb = hrows // tr

    def body(c_ref, a_ref, b_ref, o_ref):
        o_ref[...] = (a_ref[...].astype(F32) + b_ref[...].astype(F32)).astype(o_ref.dtype)

    return pl.pallas_call(
        body, name=name,
        grid_spec=pltpu.PrefetchScalarGridSpec(
            num_scalar_prefetch=1, grid=(nk, nb),
            in_specs=[pl.BlockSpec((1, tr, cols), lambda k, i, c_ref: (k, c_ref[0] * nb + i, 0)),
                      pl.BlockSpec((1, tr, cols), lambda k, i, c_ref: (k, i, 0))],
            out_specs=pl.BlockSpec((1, tr, cols), lambda k, i, c_ref: (k, i, 0))),
        out_shape=jax.ShapeDtypeStruct((nk, hrows, cols), g.dtype),
        compiler_params=pltpu.CompilerParams(dimension_semantics=("parallel", "parallel")),
    )(c_arr, g, got)


def _chip_scatter_copies(srcs, lands, send_sems, recv_sems):
    x, y, c = _my_pos()
    copies = []
    for d in (1, 2, 3):
        tx, ty = x ^ (d >> 1), y ^ (d & 1)
        for a in range(len(srcs)):
            copies.append(pltpu.make_async_remote_copy(
                src_ref=srcs[a].at[2 * tx + ty], dst_ref=lands[a].at[d - 1], send_sem=send_sems.at[3 * a + d - 1],
                recv_sem=recv_sems.at[3 * a + d - 1], device_id=(tx, ty, c), device_id_type=MESH))
    return copies


def _chip_scatter_start(pss):
    lands = [jax.ShapeDtypeStruct((3,) + p.shape[1:], p.dtype) for p in pss]
    return _split_start("grad_chip_scatter_start", pss, lands, _chip_scatter_copies, 3 * len(pss), [])


def _chip_scatter_wait(send_sems, recv_sems, srcs, lands, after):
    return _split_wait("grad_chip_scatter_wait", send_sems, recv_sems, srcs, lands, _chip_scatter_copies, [after])


def _chip_sum(ps, got, me_arr, name):
    _, hrows, cols = ps.shape
    tr = _blk(hrows, 256, 2 * SUBLANES)

    def body(me_ref, p_ref, g_ref, o_ref):
        acc = p_ref[0].astype(F32)
        for s in range(3):
            acc = acc + g_ref[s].astype(F32)
        o_ref[...] = acc

    return pl.pallas_call(
        body, name=name,
        grid_spec=pltpu.PrefetchScalarGridSpec(
            num_scalar_prefetch=1, grid=(hrows // tr,),
            in_specs=[pl.BlockSpec((1, tr, cols), lambda i, me_ref: (me_ref[0], i, 0)),
                      pl.BlockSpec((3, tr, cols), lambda i, me_ref: (0, i, 0))],
            out_specs=pl.BlockSpec((tr, cols), lambda i, me_ref: (i, 0))),
        out_shape=jax.ShapeDtypeStruct((hrows, cols), F32),
        compiler_params=pltpu.CompilerParams(dimension_semantics=("parallel",)),
    )(me_arr, ps, got)


def _pair_swap(halves):
    n = len(halves)

    def body(*refs):
        ins, outs = refs[:n], refs[n:2 * n]
        send_sems, recv_sems = refs[2 * n:]
        x, y, c = _my_pos()
        sent = []
        for a in range(n):
            cp = pltpu.make_async_remote_copy(src_ref=ins[a], dst_ref=outs[a], send_sem=send_sems.at[a], recv_sem=recv_sems.at[a],
                                              device_id=(x, y, 1 - c), device_id_type=MESH)
            cp.start()
            sent.append(cp)
        for cp in sent:
            cp.wait()

    hbm = pl.BlockSpec(memory_space=pltpu.HBM)
    return pl.pallas_call(
        body, name="grad_pair_swap", in_specs=[hbm] * n, out_specs=[hbm] * n,
        out_shape=[jax.ShapeDtypeStruct(h.shape, h.dtype) for h in halves],
        scratch_shapes=[pltpu.SemaphoreType.DMA((n,)), pltpu.SemaphoreType.DMA((n,))],
    )(*halves)


def _adamw_sharded(w, g_own, g_other, m, v, c_arr, after, name):
    R, C = w.shape
    hrows = R // 2
    tr = _blk(hrows, 256, SUBLANES)
    nbh = hrows // tr

    def body(c_ref, w_ref, go_ref, gx_ref, m_ref, v_ref, _after_ref, g_ref, d_ref, nm_ref, nv_ref):
        mine = (pl.program_id(0) // nbh) == c_ref[0]
        g_ = jnp.where(mine, go_ref[...], gx_ref[...])
        g_ref[...] = g_
        d_ref[...], nm_ref[...], nv_ref[...] = _adamw_math(w_ref[...], g_, m_ref[...], v_ref[...])

    blk = pl.BlockSpec((tr, C), lambda i, c_ref: (i, 0))
    hblk = pl.BlockSpec((tr, C), lambda i, c_ref: (i % nbh, 0))
    sd = jax.ShapeDtypeStruct((R, C), F32)
    return pl.pallas_call(
        body, name=name,
        grid_spec=pltpu.PrefetchScalarGridSpec(
            num_scalar_prefetch=1, grid=(2 * nbh,),
            in_specs=[blk, hblk, hblk, blk, blk, pl.BlockSpec(memory_space=pl.ANY)], out_specs=[blk] * 4),
        out_shape=[sd] * 4,
        compiler_params=pltpu.CompilerParams(dimension_semantics=("parallel",)),
    )(c_arr, w, g_own, g_other, m, v, after)


def _ar_piece(ref, rows, p):
    start = p * rows
    if rows % SUBLANES == 0:
        start = pl.multiple_of(start, SUBLANES)
    return ref.at[..., pl.ds(start, rows), :]


def _ar_peer(d):
    x, y, c = _my_pos()
    return (x ^ (d >> 2), y ^ ((d >> 1) & 1), c ^ (d & 1))


def _ar_lin(p):
    return 4 * p[0] + 2 * p[1] + p[2]


def _ar_scatter_copies(rows):
    def make(srcs, lands, send_sems, recv_sems):
        n = len(srcs)
        copies = []
        for d in range(1, 8):
            to = _ar_peer(d)
            for a in range(n):
                copies.append(pltpu.make_async_remote_copy(
                    src_ref=_ar_piece(srcs[a], rows[a], _ar_lin(to)), dst_ref=lands[a].at[d],
                    send_sem=send_sems.at[(d - 1) * n + a], recv_sem=recv_sems.at[(d - 1) * n + a], device_id=to,
                    device_id_type=MESH))
        return copies
    return make


def _ar_gather_copies(rows):
    def make(srcs, lands, send_sems, recv_sems):
        n = len(srcs)
        me = _ar_lin(_my_pos())
        copies = []
        for d in range(1, 8):
            for a in range(n):
                copies.append(pltpu.make_async_remote_copy(
                    src_ref=srcs[a], dst_ref=_ar_piece(lands[a], rows[a], me),
                    send_sem=send_sems.at[(d - 1) * n + a], recv_sem=recv_sems.at[(d - 1) * n + a], device_id=_ar_peer(d),
                    device_id_type=MESH))
        return copies
    return make


def _ar_sum(srcs, lands, rows):
    n = len(srcs)

    def body(*refs):
        me = _ar_lin(_my_pos())
        for a in range(n):
            acc = _ar_piece(refs[a], rows[a], me)[...]
            for d in range(1, 8):
                acc = acc + refs[n + a][d]
            refs[2 * n + a][...] = acc

    vm = pl.BlockSpec(memory_space=pltpu.VMEM)
    return pl.pallas_call(
        body, name="allreduce_sum", in_specs=[vm] * (2 * n), out_specs=[vm] * n,
        out_shape=[jax.ShapeDtypeStruct(p.shape[1:], F32) for p in lands],
    )(*srcs, *lands)


def kernel(x, pre_norm_w, w_in, s5_A_re, s5_A_im, s5_B_re, s5_B_im, s5_C_re, s5_C_im, s5_D, s5_log_dt, s5_glu_w, s5_glu_b, gla_gate_up, gla_gate_bias, gla_norm_w, w_out, post_norm_w, loss_target, m_pre_norm_w, m_w_in, m_s5_A_re, m_s5_A_im, m_s5_B_re, m_s5_B_im, m_s5_C_re, m_s5_C_im, m_s5_D, m_s5_log_dt, m_s5_glu_w, m_s5_glu_b, m_gla_gate_up, m_gla_gate_bias, m_gla_norm_w, m_w_out, m_post_norm_w, v_pre_norm_w, v_w_in, v_s5_A_re, v_s5_A_im, v_s5_B_re, v_s5_B_im, v_s5_C_re, v_s5_C_im, v_s5_D, v_s5_log_dt, v_s5_glu_w, v_s5_glu_b, v_gla_gate_up, v_gla_gate_bias, v_gla_norm_w, v_w_out, v_post_norm_w):
    names = ["pre_norm_w", "w_in", "s5_A_re", "s5_A_im", "s5_B_re", "s5_B_im", "s5_C_re", "s5_C_im", "s5_D", "s5_log_dt",
             "s5_glu_w", "s5_glu_b", "gla_gate_up", "gla_gate_bias", "gla_norm_w", "w_out", "post_norm_w"]
    W = dict(zip(names, (pre_norm_w, w_in, s5_A_re, s5_A_im, s5_B_re, s5_B_im, s5_C_re, s5_C_im, s5_D, s5_log_dt,
                         s5_glu_w, s5_glu_b, gla_gate_up, gla_gate_bias, gla_norm_w, w_out, post_norm_w)))
    M = dict(zip(names, (m_pre_norm_w, m_w_in, m_s5_A_re, m_s5_A_im, m_s5_B_re, m_s5_B_im, m_s5_C_re, m_s5_C_im, m_s5_D,
                         m_s5_log_dt, m_s5_glu_w, m_s5_glu_b, m_gla_gate_up, m_gla_gate_bias, m_gla_norm_w, m_w_out,
                         m_post_norm_w)))
    V = dict(zip(names, (v_pre_norm_w, v_w_in, v_s5_A_re, v_s5_A_im, v_s5_B_re, v_s5_B_im, v_s5_C_re, v_s5_C_im, v_s5_D,
                         v_s5_log_dt, v_s5_glu_w, v_s5_glu_b, v_gla_gate_up, v_gla_gate_bias, v_gla_norm_w, v_w_out,
                         v_post_norm_w)))
    sharded = ("w_in", "s5_glu_w", "w_out", "gla_gate_up")

    xb = x[0]
    tgt = loss_target[0]
    L, D = xb.shape
    DS = D // 2
    G = DS // S5_GROUP
    P = S5_STATE
    NB = DS // S5_COLS
    DV = D - DS
    DK = DV // 2
    WM = 2 * DS + 2 * DK + 2 * DV
    nsh = w_in.shape[2]

    chip = 2 * lax.axis_index("x") + lax.axis_index("y")
    own = [jnp.pad(w_in[0].astype(BF16), ((0, 0), (0, -nsh % LANES))), s5_glu_w[0].astype(BF16),
           w_out[0].astype(BF16), gla_gate_up[0]]
    fill = lambda g, o: lax.dynamic_update_index_in_dim(g, o, chip, 0)
    win_ss, win_rs, win_src, win_lands, win_token = _late_gather_start(own[:1], pre_norm_w, "w_in_gather_start")
    h = _prenorm_fwd(xb, pre_norm_w, win_token)

    b_view = lambda t: jnp.transpose(t[0], (0, 2, 1)).reshape(G * S5_GROUP, P)
    b_back = lambda t: jnp.transpose(t.reshape(G, S5_GROUP, P), (0, 2, 1))[None]
    c_view = lambda t: t[0].reshape(G * S5_GROUP, P)
    c_back = lambda t: t.reshape(1, G, S5_GROUP, P)
    small = ["pre_norm_w", "post_norm_w", "s5_D", "s5_glu_b", "gla_gate_bias", "gla_norm_w", "s5_log_dt",
             "s5_A_re", "s5_A_im", "s5_B_re", "s5_B_im", "s5_C_re", "s5_C_im"]
    view = {n: (lambda t: t) for n in small[:7]}
    back = dict(view)
    view.update(s5_A_re=lambda t: t[0], s5_A_im=lambda t: t[0], s5_B_re=b_view, s5_B_im=b_view, s5_C_re=c_view, s5_C_im=c_view)
    back.update(s5_A_re=lambda t: t[None], s5_A_im=lambda t: t[None], s5_B_re=b_back, s5_B_im=b_back, s5_C_re=c_back,
                s5_C_im=c_back)
    Wv = {n: view[n](W[n]) for n in small}
    bbd_re, bbd_im, ct_re, ct_im, tab, ptab = _s5_prep_fwd(
        Wv["s5_A_re"], Wv["s5_A_im"], s5_log_dt, Wv["s5_B_re"], Wv["s5_B_im"], Wv["s5_C_re"], Wv["s5_C_im"],
        h, _blk(L, S5_TIME_BLOCK, SUBLANES) // SUBLANES)
    dvec = s5_D

    for d_ in (W, M, V):
        d_["w_in"], _ = lax.optimization_barrier((d_["w_in"], win_token))
    g_win = _late_gather_wait(win_ss, win_rs, win_src, win_lands,
                              [tab, W["w_in"][0], M["w_in"][0], V["w_in"][0]], "w_in_gather_wait")
    g_win = fill(_late_gather_pair(g_win, "w_in_gather_pair")[0], own[0])
    w_main, w_low = _assemble_w_in(g_win, nsh, WM)
    late_ss, late_rs, late_src, late_lands, late_token = _late_gather_start(own[1:], g_win, "late_gather_start")
    proj_main, proj_low = _in_proj(h, w_main, w_low, late_token)
    y_pre, s_re, s_im = _s5_scan_fwd(proj_main, bbd_re, bbd_im, ct_re, ct_im, dvec, tab, ptab, DS)
    late = _late_gather_wait(late_ss, late_rs, late_src, late_lands, [y_pre], "late_gather_wait")
    late = _late_gather_pair(late, "late_gather_pair")
    g_glu, g_wout, g_gup = [fill(g, o) for g, o in zip(late, own[1:])]
    glu_w = g_glu.reshape(DS, DS)
    wout = g_wout.reshape(D, D)
    gup = jnp.moveaxis(g_gup, 0, 1).reshape(GLA_RANK, DK)
    gup_pad = jnp.pad(gup, ((0, LANES - GLA_RANK), (0, 0))).astype(BF16)
    ycat, t_pre = _s5_post_fwd(y_pre, proj_main, glu_w, s5_glu_b, DS)
    ycat, s_prev, gla_scores, gla_o = _gla_fwd(proj_main, proj_low, gup_pad, gla_gate_bias, gla_norm_w, ycat,
                                               DS, DK, DV)
    mixed = _mm(ycat, wout, name="out_proj")
    loss11, d_mixed, dout, g_post_w = _post_fwd_bwd(mixed, xb, tgt, post_norm_w)

    d_ycat = _mm(d_mixed, wout, tb=True, name="out_proj_dx")
    d_ypre, d_s5, d_t, y1, g_glu_b = _s5_post_bwd(d_ycat, y_pre, proj_main, t_pre, glu_w, DS)
    d_s5, g_D, gct_re, gct_im, gbbd_re, gbbd_im, gab_re, gab_im = _s5_scan_bwd(
        d_ypre, proj_main, s_re, s_im, bbd_re, bbd_im, ct_re, ct_im, dvec, tab, ptab, d_s5, DS)
    d_gla, d_a, g_norm_w, g_gate_bias = _gla_bwd(
        d_ycat, proj_main, proj_low, s_prev, gla_scores, gla_o, gup_pad, gla_gate_bias, gla_norm_w, DS, DK, DV)
    d_low = _mm(d_a, gup_pad, tb=True, out_dtype=BF16, name="gate_dx")
    g_gup_pad = _mm(proj_low, d_a, ta=True, name="gate_dw")
    g_wmain, g_wlow = _in_proj_dw(h, d_s5, d_gla, d_low)

    g_win_sh = _split_w_in_grad(g_wmain, g_wlow, nsh)
    px_ss, px_rs, px_src, px_got, px_token = _split_start(
        "grad_pair_w_in_start", [g_win_sh], [jax.ShapeDtypeStruct((4, D // 2, nsh), BF16)], _pair_half_copies, 1, [])
    g_wout_full = _mm(ycat, d_mixed, ta=True, out_dtype=BF16, name="out_proj_dw", after=[px_token])
    g_glu_full = _mm(y1, d_t, ta=True, out_dtype=BF16, name="glu_dw", after=[px_token])
    px_src, px_got = _split_wait("grad_pair_w_in_wait", px_ss, px_rs, px_src, px_got, _pair_half_copies,
                                 [g_wout_full, g_glu_full])
    gs = [g_glu_full.reshape(4, DS // 4, DS), g_wout_full.reshape(4, D // 4, D),
          jnp.moveaxis(g_gup_pad[:GLA_RANK].reshape(GLA_RANK, 4, DK // 4), 1, 0)]
    c_arr = lax.axis_index("c").astype(jnp.int32).reshape(1)
    me_arr = chip.astype(jnp.int32).reshape(1)
    sx_ss, sx_rs, gs, sx_got, _ = _split_start(
        "grad_pair_rest_start", gs, [jax.ShapeDtypeStruct((4, g.shape[1] // 2, g.shape[2]), g.dtype) for g in gs],
        _pair_half_copies, len(gs), [])
    ps_win = _pair_add(px_src[0], px_got[0], c_arr, "grad_pair_add_" + sharded[0])
    gs, sx_got = _split_wait("grad_pair_rest_wait", sx_ss, sx_rs, gs, sx_got, _pair_half_copies, [ps_win])
    pss = [ps_win] + [_pair_add(g, r, c_arr, "grad_pair_add_" + n) for n, g, r in zip(sharded[1:], gs, sx_got)]
    send_sems, recv_sems, pss, lands, token = _chip_scatter_start(pss)

    dh = _in_proj_dx(d_s5, d_gla, d_low, w_main, w_low, token)
    grad_x, g_pre_w = _prenorm_bwd(xb, dh, dout, pre_norm_w)

    g_a, g_bc, g_ldt = _s5_prep_bwd(Wv["s5_A_re"], Wv["s5_A_im"], s5_log_dt, Wv["s5_B_re"], Wv["s5_B_im"],
                                    gbbd_re, gbbd_im, gct_re, gct_im, gab_re, gab_im)

    g_vecs = jnp.concatenate([g_pre_w, g_post_w, g_D, g_glu_b, g_gate_bias, g_norm_w, g_ldt, loss11], axis=1)
    loss_at = g_vecs.shape[1] - 1
    lanes_pad = -g_vecs.shape[1] % (8 * SUBLANES * LANES)
    g_vecs = jnp.pad(g_vecs, ((0, 0), (0, lanes_pad))).reshape(-1, LANES)
    ar_srcs = [g_vecs, g_a, g_bc]
    ar_rows = [a.shape[-2] // 8 for a in ar_srcs]
    ar_lands = [jax.ShapeDtypeStruct((8,) + a.shape[:-2] + (r, a.shape[-1]), F32) for a, r in zip(ar_srcs, ar_rows)]
    ar_ss, ar_rs, ar_srcs, ar_got, ar_token = _split_start(
        "allreduce_scatter_start", ar_srcs, ar_lands, _ar_scatter_copies(ar_rows), 7 * len(ar_srcs), [])

    pss, rcv = _chip_scatter_wait(send_sems, recv_sems, pss, lands, ar_token)
    halves = [_chip_sum(p, r, me_arr, "grad_chip_sum_" + n) for n, p, r in zip(sharded, pss, rcv)]
    others = _pair_swap(halves)
    ar_srcs, ar_got = _split_wait("allreduce_scatter_wait", ar_ss, ar_rs, ar_srcs, ar_got, _ar_scatter_copies(ar_rows),
                                  [others[0]])
    ar_red = _ar_sum(ar_srcs, ar_got, ar_rows)
    ag_ss, ag_rs, ar_red, ag_full, ag_token = _split_start(
        "allreduce_gather_start", ar_red, [jax.ShapeDtypeStruct(a.shape, F32) for a in ar_srcs],
        _ar_gather_copies(ar_rows), 7 * len(ar_red), [])
    G_out, D_out, M_out, V_out = {}, {}, {}, {}
    for n, g_own, g_other in zip(sharded, halves, others):
        g_, d_, m_, v_ = _adamw_sharded(W[n][0], g_own, g_other, M[n][0], V[n][0], c_arr, ag_token, "adamw_" + n)
        G_out[n], D_out[n], M_out[n], V_out[n] = g_[None], d_[None], m_[None], v_[None]
    ar_red, ag_full = _split_wait("allreduce_gather_wait", ag_ss, ag_rs, ar_red, ag_full, _ar_gather_copies(ar_rows),
                                  [D_out[n] for n in sharded])
    me8 = 2 * chip + lax.axis_index("c")
    r_vecs, r_a, r_bc = [lax.dynamic_update_slice_in_dim(f, r, me8 * rw, axis=f.ndim - 2)
                         for f, r, rw in zip(ag_full, ar_red, ar_rows)]
    r_vecs = r_vecs.reshape(1, -1)
    loss = r_vecs[0, loss_at]
    outs4 = _adamw_small(r_vecs, r_a, r_bc, [Wv[n] for n in small],
                         [view[n](M[n]) for n in small], [view[n](V[n]) for n in small])
    for store, o in zip((G_out, D_out, M_out, V_out), outs4):
        store.update({n: back[n](t) for n, t in zip(small, o)})

    return (loss, grad_x[None], *[G_out[n] for n in names], *[D_out[n] for n in names],
            *[M_out[n] for n in names], *[V_out[n] for n in names])
```

```python
import functools
import math

import jax
import jax.numpy as jnp
from jax import lax
from jax.experimental import pallas as pl
from jax.experimental.pallas import tpu as pltpu

F32 = jnp.float32
BF16 = jnp.bfloat16
HI = lax.Precision.HIGHEST
MESH = pl.DeviceIdType.MESH

EPS = 1e-6
S5_GROUP = 16
S5_STATE = 64
GLA_HK = 128
GLA_HV = 256
GLA_RANK = 16
GLA_TAU = 16.0
GLA_CHUNK = 64
GLA_STEP_CHUNKS = 8
LANES = 128
SUBLANES = 8
S5_COLS = 128
S5_LANES = (S5_COLS // S5_GROUP) * S5_STATE
S5_TIME_BLOCK = 1024
ROW_TILE = 512

ADAM_LR = 0.001
ADAM_B1 = 0.9
ADAM_B2 = 0.999
ADAM_EPS = 1e-08
ADAM_WD = 0.01
ADAM_STEP = 10

GELU_K = math.sqrt(2.0 / math.pi)
GELU_C = 0.044715


def _blk(n, pref, unit=LANES):
    best = None
    b = unit
    while b <= min(n, pref):
        if n % b == 0:
            best = b
        b += unit
    return best if best is not None else n


def _dot(a, b, dn=(((1,), (0,)), ((), ()))):
    return lax.dot_general(a.astype(BF16), b.astype(BF16), dn, preferred_element_type=F32)


def _dot_hi(a, b, dn=(((1,), (0,)), ((), ()))):
    return lax.dot_general(a, b, dn, precision=HI, preferred_element_type=F32)


NN = (((1,), (0,)), ((), ()))
NT = (((1,), (1,)), ((), ()))
TN = (((0,), (0,)), ((), ()))


def _sigmoid(x):
    return 1.0 / (1.0 + jnp.exp(-x))


def _gelu(y):
    return 0.5 * y * (1.0 + jnp.tanh(GELU_K * (y + GELU_C * y * y * y)))


def _gelu_grad(y):
    th = jnp.tanh(GELU_K * (y + GELU_C * y * y * y))
    return 0.5 * (1.0 + th) + 0.5 * y * (1.0 - th * th) * GELU_K * (1.0 + 3.0 * GELU_C * y * y)


def _mm(a, b, *, name, ta=False, tb=False, out_dtype=F32, bm=1024, bn=1024, bk=2048, after=()):
    if ta:
        K, M = a.shape
    else:
        M, K = a.shape
    if tb:
        N, K2 = b.shape
    else:
        K2, N = b.shape
    assert K == K2, (a.shape, b.shape, ta, tb)
    bm, bn, bk = _blk(M, bm), _blk(N, bn), _blk(K, bk)
    nk = K // bk
    dn = (((0 if ta else 1,), (1 if tb else 0,)), ((), ()))

    def body(a_ref, b_ref, *rest):
        o_ref = rest[len(after)]
        if nk == 1:
            o_ref[...] = _dot(a_ref[...], b_ref[...], dn).astype(out_dtype)
            return
        acc_ref = rest[len(after) + 1]
        k = pl.program_id(2)

        @pl.when(k == 0)
        def _():
            acc_ref[...] = jnp.zeros_like(acc_ref)

        acc_ref[...] += _dot(a_ref[...], b_ref[...], dn)

        @pl.when(k == nk - 1)
        def _():
            o_ref[...] = acc_ref[...].astype(out_dtype)

    a_spec = pl.BlockSpec((bk, bm), lambda i, j, k: (k, i)) if ta else pl.BlockSpec((bm, bk), lambda i, j, k: (i, k))
    b_spec = pl.BlockSpec((bn, bk), lambda i, j, k: (j, k)) if tb else pl.BlockSpec((bk, bn), lambda i, j, k: (k, j))
    return pl.pallas_call(
        body,
        name=name,
        grid=(M // bm, N // bn, nk),
        in_specs=[a_spec, b_spec] + [pl.BlockSpec(memory_space=pl.ANY)] * len(after),
        out_specs=pl.BlockSpec((bm, bn), lambda i, j, k: (i, j)),
        out_shape=jax.ShapeDtypeStruct((M, N), out_dtype),
        scratch_shapes=[pltpu.VMEM((bm, bn), F32)] if nk > 1 else [],
        compiler_params=pltpu.CompilerParams(dimension_semantics=("parallel", "parallel", "arbitrary")),
    )(a, b, *after)


def _in_proj(h, w_main, w_low, after):
    M, K = h.shape
    N = w_main.shape[1]
    bm, bn = _blk(M, 1024), _blk(N, 1024)

    def body(h_ref, w_ref, wl_ref, _after_ref, o_ref, ol_ref):
        hv = h_ref[...]
        o_ref[...] = _dot(hv, w_ref[...])

        @pl.when(pl.program_id(1) == 0)
        def _():
            ol_ref[...] = _dot(hv, wl_ref[...])

    return pl.pallas_call(
        body, name="in_proj", grid=(M // bm, N // bn),
        in_specs=[pl.BlockSpec((bm, K), lambda i, j: (i, 0)), pl.BlockSpec((K, bn), lambda i, j: (0, j)),
                  pl.BlockSpec((K, LANES), lambda i, j: (0, 0)), pl.BlockSpec(memory_space=pl.ANY)],
        out_specs=[pl.BlockSpec((bm, bn), lambda i, j: (i, j)), pl.BlockSpec((bm, LANES), lambda i, j: (i, 0))],
        out_shape=[jax.ShapeDtypeStruct((M, N), F32), jax.ShapeDtypeStruct((M, LANES), F32)],
        compiler_params=pltpu.CompilerParams(dimension_semantics=("parallel", "arbitrary")),
    )(h, w_main, w_low, after)


def _in_proj_dx(a1, a2, al, b, bl, after, *, bm=512, bn=1024):
    M, K1 = a1.shape
    K2 = a2.shape[1]
    N = b.shape[0]
    bm, bn = _blk(M, bm, 2 * SUBLANES), _blk(N, bn)

    def body(a1_ref, a2_ref, al_ref, b_ref, bl_ref, _after_ref, o_ref):
        o_ref[...] = (_dot(a1_ref[...], b_ref[:, :K1], NT) + _dot(a2_ref[...], b_ref[:, K1:], NT)
                      + _dot(al_ref[...], bl_ref[...], NT))

    return pl.pallas_call(
        body, name="in_proj_dx", grid=(N // bn, M // bm),
        in_specs=[pl.BlockSpec((bm, K1), lambda j, i: (i, 0)),
                  pl.BlockSpec((bm, K2), lambda j, i: (i, 0)),
                  pl.BlockSpec((bm, LANES), lambda j, i: (i, 0)),
                  pl.BlockSpec((bn, K1 + K2), lambda j, i: (j, 0)),
                  pl.BlockSpec((bn, LANES), lambda j, i: (j, 0)),
                  pl.BlockSpec(memory_space=pl.ANY)],
        out_specs=pl.BlockSpec((bm, bn), lambda j, i: (i, j)),
        out_shape=jax.ShapeDtypeStruct((M, N), F32),
        compiler_params=pltpu.CompilerParams(dimension_semantics=("parallel", "parallel")),
    )(a1, a2, al, b, bl, after)


def _in_proj_dw(a, b1, b2, bl, *, bm=1024, bn=1024, bk=2048):
    K, M = a.shape
    N1, N2 = b1.shape[1], b2.shape[1]
    bm, bk = _blk(M, bm), _blk(K, bk)
    bn = _blk(math.gcd(N1, N2), bn)
    nj1, nj = N1 // bn, (N1 + N2) // bn
    nk = K // bk

    def body(a_ref, b1_ref, b2_ref, bl_ref, o_ref, ol_ref, acc_ref, accl_ref):
        j = pl.program_id(1)
        k = pl.program_id(2)

        @pl.when(k == 0)
        def _():
            acc_ref[...] = jnp.zeros_like(acc_ref)

        @pl.when(j < nj1)
        def _():
            acc_ref[...] += _dot(a_ref[...], b1_ref[...], TN)

        @pl.when(j >= nj1)
        def _():
            acc_ref[...] += _dot(a_ref[...], b2_ref[...], TN)

        @pl.when(k == nk - 1)
        def _():
            o_ref[...] = acc_ref[...].astype(BF16)

        @pl.when(j == 0)
        def _():
            low = _dot(a_ref[...], bl_ref[...], TN)

            @pl.when(k == 0)
            def _():
                accl_ref[...] = low

            @pl.when(k > 0)
            def _():
                accl_ref[...] += low

            @pl.when(k == nk - 1)
            def _():
                ol_ref[...] = accl_ref[...].astype(BF16)

    return pl.pallas_call(
        body, name="in_proj_dw", grid=(M // bm, nj, nk),
        in_specs=[pl.BlockSpec((bk, bm), lambda i, j, k: (k, i)),
                  pl.BlockSpec((bk, bn), lambda i, j, k: (jnp.where(j < nj1, k, nk - 1), jnp.minimum(j, nj1 - 1))),
                  pl.BlockSpec((bk, bn), lambda i, j, k: (jnp.where(j >= nj1, k, 0), jnp.maximum(j - nj1, 0))),
                  pl.BlockSpec((bk, LANES), lambda i, j, k: (jnp.where(j == 0, k, nk - 1), 0))],
        out_specs=[pl.BlockSpec((bm, bn), lambda i, j, k: (i, j)), pl.BlockSpec((bm, LANES), lambda i, j, k: (i, 0))],
        out_shape=[jax.ShapeDtypeStruct((M, N1 + N2), BF16), jax.ShapeDtypeStruct((M, LANES), BF16)],
        scratch_shapes=[pltpu.VMEM((bm, bn), F32), pltpu.VMEM((bm, LANES), F32)],
        compiler_params=pltpu.CompilerParams(dimension_semantics=("parallel", "arbitrary", "arbitrary")),
    )(a, b1, b2, bl)


def _assemble_w_in(g, nsh, wm):
    _, R, nshp = g.shape
    nb_in = nshp // LANES
    nb_main = wm // LANES
    tr = _blk(R, 512, 2 * SUBLANES)
    plan = []
    for b in range(nb_main + 1):
        terms = []
        for k in range(g.shape[0]):
            for i in range(nb_in):
                delta = nsh * k + LANES * i - LANES * b
                lo, hi = max(0, -delta), min(LANES, LANES - delta, nsh - LANES * i)
                if abs(delta) < LANES and hi > lo:
                    terms.append((k, i, delta))
        plan.append(terms)
    deltas = sorted({d for terms in plan for _, _, d in terms if d})

    def body(g_ref, wm_ref, wl_ref):
        src = _iota2((LANES, LANES), 0)
        dst = _iota2((LANES, LANES), 1)
        shift = {d: (dst - src == d).astype(BF16) for d in deltas}
        for b, terms in enumerate(plan):
            acc = None
            for k, i, d in terms:
                blk = g_ref[k, :, LANES * i:LANES * (i + 1)]
                t = _dot(blk, shift[d]) if d else blk.astype(F32)
                acc = t if acc is None else acc + t
            if b < nb_main:
                wm_ref[:, LANES * b:LANES * (b + 1)] = acc.astype(BF16)
            else:
                wl_ref[...] = acc.astype(BF16)

    return pl.pallas_call(
        body, name="assemble_w_in", grid=(R // tr,),
        in_specs=[pl.BlockSpec((g.shape[0], tr, nshp), lambda r: (0, r, 0))],
        out_specs=[pl.BlockSpec((tr, wm), lambda r: (r, 0)), pl.BlockSpec((tr, LANES), lambda r: (r, 0))],
        out_shape=[jax.ShapeDtypeStruct((R, wm), BF16), jax.ShapeDtypeStruct((R, LANES), BF16)],
        compiler_params=pltpu.CompilerParams(dimension_semantics=("parallel",)),
    )(g)


def _split_w_in_grad(g_main, g_low, nsh):
    R, wm = g_main.shape
    nb_main = wm // LANES
    nb_out = -(-nsh // LANES)
    tr = _blk(R, 512, 2 * SUBLANES)
    plan = {}
    for k in range(4):
        for i in range(nb_out):
            width = min(LANES, nsh - LANES * i)
            terms = []
            for b in range(nb_main + 1):
                delta = LANES * b - (nsh * k + LANES * i)
                lo, hi = max(0, delta), min(width, LANES + delta)
                if abs(delta) < LANES and hi > lo:
                    terms.append((b, delta))
            plan[k, i] = (width, terms)
    deltas = sorted({d for _, terms in plan.values() for _, d in terms if d})

    def body(gm_ref, gl_ref, o_ref):
        src = _iota2((LANES, LANES), 0)
        dst = _iota2((LANES, LANES), 1)
        shift = {d: (dst - src == d).astype(BF16) for d in deltas}
        for (k, i), (width, terms) in plan.items():
            acc = None
            for b, d in terms:
                blk = gm_ref[:, LANES * b:LANES * (b + 1)] if b < nb_main else gl_ref[...]
                t = _dot(blk, shift[d]) if d else blk.astype(F32)
                acc = t if acc is None else acc + t
            o_ref[k, :, LANES * i:LANES * i + width] = acc[:, :width].astype(BF16)

    return pl.pallas_call(
        body, name="split_w_in_grad", grid=(R // tr,),
        in_specs=[pl.BlockSpec((tr, wm), lambda r: (r, 0)), pl.BlockSpec((tr, LANES), lambda r: (r, 0))],
        out_specs=pl.BlockSpec((4, tr, nsh), lambda r: (0, r, 0)),
        out_shape=jax.ShapeDtypeStruct((4, R, nsh), BF16),
        compiler_params=pltpu.CompilerParams(dimension_semantics=("parallel",)),
    )(g_main, g_low)


def _prenorm_fwd(x, w, after):
    L, D = x.shape
    tr = _blk(L, ROW_TILE, SUBLANES)

    def body(x_ref, w_ref, _after_ref, h_ref):
        xv = x_ref[...]
        r = lax.rsqrt(jnp.mean(xv * xv, axis=-1, keepdims=True) + EPS)
        h_ref[...] = (xv * r * w_ref[...]).astype(BF16)

    return pl.pallas_call(
        body, name="prenorm_fwd", grid=(L // tr,),
        in_specs=[pl.BlockSpec((tr, D), lambda i: (i, 0)), pl.BlockSpec((1, D), lambda i: (0, 0)),
                  pl.BlockSpec(memory_space=pl.ANY)],
        out_specs=pl.BlockSpec((tr, D), lambda i: (i, 0)),
        out_shape=jax.ShapeDtypeStruct((L, D), BF16),
        compiler_params=pltpu.CompilerParams(dimension_semantics=("parallel",)),
    )(x, w, after)


def _post_fwd_bwd(mixed, x, target, w):
    L, D = x.shape
    tr = _blk(L, ROW_TILE, SUBLANES)
    nsteps = L // tr

    def body(mx_ref, x_ref, t_ref, w_ref, loss_ref, dm_ref, dout_ref, gw_ref, acc_ref):
        i = pl.program_id(0)

        @pl.when(i == 0)
        def _():
            acc_ref[...] = jnp.zeros_like(acc_ref)
            gw_ref[...] = jnp.zeros_like(gw_ref)

        mx = mx_ref[...]
        wv = w_ref[...]
        r = lax.rsqrt(jnp.mean(mx * mx, axis=-1, keepdims=True) + EPS)
        n = mx * r
        err = x_ref[...] + n * wv - t_ref[...]
        acc_ref[...] += jnp.sum(err * err, axis=0, keepdims=True)
        dout = err * (1.0 / D)
        dout_ref[...] = dout
        gw_ref[...] += jnp.sum(dout * n, axis=0, keepdims=True)
        dn = dout * wv
        dm_ref[...] = (r * (dn - n * jnp.mean(dn * n, axis=-1, keepdims=True))).astype(BF16)

        @pl.when(i == nsteps - 1)
        def _():
            loss_ref[...] = jnp.sum(acc_ref[...], axis=-1, keepdims=True) * (0.5 / D)

    row = pl.BlockSpec((tr, D), lambda i: (i, 0))
    vec = pl.BlockSpec((1, D), lambda i: (0, 0))
    return pl.pallas_call(
        body, name="post_fwd_bwd", grid=(nsteps,),
        in_specs=[row, row, row, vec],
        out_specs=[pl.BlockSpec((1, 1), lambda i: (0, 0)), row, row, vec],
        out_shape=[jax.ShapeDtypeStruct((1, 1), F32), jax.ShapeDtypeStruct((L, D), BF16),
                   jax.ShapeDtypeStruct((L, D), F32), jax.ShapeDtypeStruct((1, D), F32)],
        scratch_shapes=[pltpu.VMEM((1, D), F32)],
        compiler_params=pltpu.CompilerParams(dimension_semantics=("arbitrary",)),
    )(mixed, x, target, w)


def _prenorm_bwd(x, dh, dout, w):
    L, D = x.shape
    tr = _blk(L, ROW_TILE, SUBLANES)

    def body(x_ref, a_ref, dout_ref, w_ref, gx_ref, gw_ref):
        i = pl.program_id(0)

        @pl.when(i == 0)
        def _():
            gw_ref[...] = jnp.zeros_like(gw_ref)

        xv = x_ref[...]
        r = lax.rsqrt(jnp.mean(xv * xv, axis=-1, keepdims=True) + EPS)
        n = xv * r
        dh = a_ref[...]
        gw_ref[...] += jnp.sum(dh * n, axis=0, keepdims=True)
        dn = dh * w_ref[...]
        gx_ref[...] = dout_ref[...] + r * (dn - n * jnp.mean(dn * n, axis=-1, keepdims=True))

    row = pl.BlockSpec((tr, D), lambda i: (i, 0))
    vec = pl.BlockSpec((1, D), lambda i: (0, 0))
    return pl.pallas_call(
        body, name="prenorm_bwd", grid=(L // tr,),
        in_specs=[row, row, row, vec],
        out_specs=[row, vec],
        out_shape=[jax.ShapeDtypeStruct((L, D), F32), jax.ShapeDtypeStruct((1, D), F32)],
        compiler_params=pltpu.CompilerParams(dimension_semantics=("arbitrary",)),
    )(x, dh, dout, w)


def _s5_disc(a_re_raw, a_im, dt):
    a_re = jnp.minimum(a_re_raw, -1e-4)
    mag = jnp.exp(a_re * dt)
    ph = a_im * dt
    ab_re = mag * jnp.cos(ph)
    ab_im = mag * jnp.sin(ph)
    inv_n = 1.0 / (a_re * a_re + a_im * a_im)
    ia_re = a_re * inv_n
    ia_im = -a_im * inv_n
    n_re = ab_re - 1.0
    f_re = n_re * ia_re - ab_im * ia_im
    f_im = n_re * ia_im + ab_im * ia_re
    return a_re, ab_re, ab_im, f_re, f_im, ia_re, ia_im


def _iota2(shape, dim):
    return lax.broadcasted_iota(jnp.int32, shape, dim)


def _group_mask(rows, rows_per_group):
    shift = rows_per_group.bit_length() - 1
    return (_iota2((rows, S5_LANES), 0) >> shift) == (_iota2((rows, S5_LANES), 1) >> (S5_STATE.bit_length() - 1))


def _lane_tiler(dtype):
    return ((_iota2((S5_STATE, S5_LANES), 1) & (S5_STATE - 1)) == _iota2((S5_STATE, S5_LANES), 0)).astype(dtype)


def _row_to_col(row, n):
    eye = (_iota2((n, n), 0) == _iota2((n, n), 1)).astype(F32)
    return jnp.sum(eye * row, axis=1, keepdims=True)


def _group_repeat(G):
    return ((_iota2((G * S5_GROUP, G), 0) >> (S5_GROUP.bit_length() - 1)) == _iota2((G * S5_GROUP, G), 1)).astype(F32)


S5_TABS = 18


def _s5_prep_fwd(a_re, a_im, log_dt, b_re, b_im, c_re, c_im, after, seg):
    G, P = a_re.shape
    nb = G * S5_GROUP // S5_COLS
    g8 = S5_COLS // S5_GROUP
    assert seg & (seg - 1) == 0, seg

    def body(are_ref, aim_ref, ldt_ref, bre_ref, bim_ref, cre_ref, cim_ref, _after_ref,
             bbre_ref, bbim_ref, ctre_ref, ctim_ref, tab_ref, pt_ref):
        dt = jnp.exp(_row_to_col(ldt_ref[...], G))
        _, ab_re, ab_im, f_re, f_im, _, _ = _s5_disc(are_ref[...], aim_ref[...], dt)
        rep = _group_repeat(G)
        fx_re = _dot_hi(rep, f_re)
        fx_im = _dot_hi(rep, f_im)
        br, bi = bre_ref[...], bim_ref[...]
        bb_re = fx_re * br - fx_im * bi
        bb_im = fx_re * bi + fx_im * br
        tile_bf = _lane_tiler(BF16)
        mask = _group_mask(S5_COLS, S5_GROUP)
        for jb in range(nb):
            rs = slice(jb * S5_COLS, (jb + 1) * S5_COLS)
            for src, dst in ((bb_re[rs], bbre_ref), (bb_im[rs], bbim_ref), (cre_ref[rs, :], ctre_ref), (cim_ref[rs, :], ctim_ref)):
                dst[jb] = jnp.where(mask, _dot(src, tile_bf), 0.0).astype(BF16)

        tile_f = _lane_tiler(F32)
        mask8 = _group_mask(g8, 1)
        row = _iota2((SUBLANES, S5_LANES), 0)
        slab = (SUBLANES, S5_LANES)
        cmul = lambda p, q: (p[0] * q[0] - p[1] * q[1], p[0] * q[1] + p[1] * q[0])
        for jb in range(nb):
            gs = slice(jb * g8, (jb + 1) * g8)

            def lanes(m):
                v = jnp.sum(jnp.where(mask8, _dot_hi(m[gs], tile_f), 0.0), axis=0, keepdims=True)
                return jnp.broadcast_to(v, slab)

            a1 = (lanes(ab_re), lanes(ab_im))
            tab_ref[jb, 0], tab_ref[jb, 1] = a1

            def powers(k, p):
                s_re = s_im = jnp.zeros(slab, F32)
                for r in range(SUBLANES):
                    s_re = jnp.where(row == r, p[0], s_re)
                    s_im = jnp.where(row == r, p[1], s_im)
                    p = cmul(p, a1)
                pt_ref[jb, 0, _slab(k), :] = s_re
                pt_ref[jb, 1, _slab(k), :] = s_im
                return p

            lax.fori_loop(0, seg // SUBLANES, powers, a1)
            aseg = a1
            for _ in range(seg.bit_length() - 1):
                aseg = cmul(aseg, aseg)
            pw = [aseg]
            for _ in range(1, SUBLANES):
                pw.append(cmul(pw[-1], aseg))
            for lvl, k in enumerate((1, 2, 4)):
                tab_ref[jb, 2 + 2 * lvl] = jnp.where(row >= k, pw[k - 1][0], 0.0)
                tab_ref[jb, 3 + 2 * lvl] = jnp.where(row >= k, pw[k - 1][1], 0.0)
                tab_ref[jb, 10 + 2 * lvl] = jnp.where(row < SUBLANES - k, pw[k - 1][0], 0.0)
                tab_ref[jb, 11 + 2 * lvl] = jnp.where(row < SUBLANES - k, -pw[k - 1][1], 0.0)
            f_r = f_i = r_r = r_i = jnp.zeros(slab, F32)
            for i in range(SUBLANES):
                f_r = jnp.where(row == i, pw[i][0], f_r)
                f_i = jnp.where(row == i, pw[i][1], f_i)
                r_r = jnp.where(row == i, pw[SUBLANES - 1 - i][0], r_r)
                r_i = jnp.where(row == i, -pw[SUBLANES - 1 - i][1], r_i)
            tab_ref[jb, 8] = f_r
            tab_ref[jb, 9] = f_i
            tab_ref[jb, 16] = r_r
            tab_ref[jb, 17] = r_i

    vm = pl.BlockSpec(memory_space=pltpu.VMEM)
    bd = jax.ShapeDtypeStruct((nb, S5_COLS, S5_LANES), BF16)
    return pl.pallas_call(
        body, name="s5_prep_fwd",
        in_specs=[vm] * 7 + [pl.BlockSpec(memory_space=pl.ANY)], out_specs=[vm] * 6,
        out_shape=[bd, bd, bd, bd, jax.ShapeDtypeStruct((nb, S5_TABS, SUBLANES, S5_LANES), F32),
                   jax.ShapeDtypeStruct((nb, 2, seg, S5_LANES), F32)],
    )(a_re, a_im, log_dt, b_re, b_im, c_re, c_im, after)


def _s5_prep_bwd(a_re, a_im, log_dt, b_re, b_im, gbb_re, gbb_im, gct_re, gct_im, gab_re, gab_im):
    G, P = a_re.shape
    nb = G * S5_GROUP // S5_COLS
    g8 = S5_COLS // S5_GROUP

    def body(are_ref, aim_ref, ldt_ref, bre_ref, bim_ref, gbr_ref, gbi_ref, gcr_ref, gci_ref, gar_ref, gai_ref,
             o_a, o_bc, o_ldt):
        dt = jnp.exp(_row_to_col(ldt_ref[...], G))
        a_raw = are_ref[...]
        a_imv = aim_ref[...]
        a_re_c, ab_re, ab_im, f_re, f_im, ia_re, ia_im = _s5_disc(a_raw, a_imv, dt)
        tile_f = _lane_tiler(F32)
        mask = _group_mask(S5_COLS, S5_GROUP)
        mask8 = _group_mask(g8, 1)
        for jb in range(nb):
            rs = slice(jb * S5_COLS, (jb + 1) * S5_COLS)
            gs = slice(jb * g8, (jb + 1) * g8)
            ls = slice(jb * S5_LANES, (jb + 1) * S5_LANES)
            for k, src in enumerate((gbr_ref, gbi_ref, gcr_ref, gci_ref)):
                o_bc[k, rs, :] = _dot_hi(jnp.where(mask, src[jb], 0.0), tile_f, NT)
            for k, src in enumerate((gar_ref, gai_ref)):
                o_a[k, gs, :] = _dot_hi(jnp.where(mask8, src[:, ls], 0.0), tile_f, NT)
        rep = _group_repeat(G)
        fx_re = _dot_hi(rep, f_re)
        fx_im = _dot_hi(rep, f_im)
        gbr, gbi = o_bc[0], o_bc[1]
        br, bi = bre_ref[...], bim_ref[...]
        o_bc[0] = fx_re * gbr + fx_im * gbi
        o_bc[1] = fx_re * gbi - fx_im * gbr
        gf_re = _dot_hi(rep, br * gbr + bi * gbi, TN)
        gf_im = _dot_hi(rep, br * gbi - bi * gbr, TN)
        gab_r = o_a[0] + ia_re * gf_re + ia_im * gf_im
        gab_i = o_a[1] + ia_re * gf_im - ia_im * gf_re
        q_re = f_re * ia_re - f_im * ia_im
        q_im = f_re * ia_im + f_im * ia_re
        ga_re = -(q_re * gf_re + q_im * gf_im)
        ga_im = -(q_re * gf_im - q_im * gf_re)
        gth_re = ab_re * gab_r + ab_im * gab_i
        gth_im = ab_re * gab_i - ab_im * gab_r
        ga_re = ga_re + dt * gth_re
        ga_im = ga_im + dt * gth_im
        gdt = jnp.sum(a_re_c * gth_re + a_imv * gth_im, axis=-1, keepdims=True)
        eye = (_iota2((G, G), 0) == _iota2((G, G), 1)).astype(F32)
        o_ldt[...] = jnp.sum(eye * (gdt * dt), axis=0, keepdims=True)
        slope = jnp.where(a_raw < -1e-4, 1.0, jnp.where(a_raw == -1e-4, 0.5, 0.0))
        o_a[0] = ga_re * slope
        o_a[1] = ga_im

    vm = pl.BlockSpec(memory_space=pltpu.VMEM)
    return pl.pallas_call(
        body, name="s5_prep_bwd",
        in_specs=[vm] * 11, out_specs=[vm] * 3,
        out_shape=[jax.ShapeDtypeStruct((2, G, P), F32), jax.ShapeDtypeStruct((4, G * S5_GROUP, P), F32),
                   jax.ShapeDtypeStruct((1, G), F32)],
    )(a_re, a_im, log_dt, b_re, b_im, gbb_re, gbb_im, gct_re, gct_im, gab_re, gab_im)


def _scan8(xr, xi, tab_ref, base, shifts):
    for lvl, sh in enumerate(shifts):
        mr = tab_ref[0, base + 2 * lvl]
        mi = tab_ref[0, base + 2 * lvl + 1]
        ar = pltpu.roll(xr, sh, 0)
        ai = pltpu.roll(xi, sh, 0)
        xr, xi = xr + mr * ar - mi * ai, xi + mr * ai + mi * ar
    return xr, xi


def _to_segments(src_ref, dst_ref, seg):
    for i in range(seg):
        dst_ref[i * SUBLANES:(i + 1) * SUBLANES, :] = src_ref[pl.ds(i, SUBLANES, stride=seg), :]


def _from_segments(src_ref, dst_ref, seg):
    for i in range(seg):
        dst_ref[pl.ds(i, SUBLANES, stride=seg), :] = src_ref[i * SUBLANES:(i + 1) * SUBLANES, :]


def _slab(i):
    return pl.ds(pl.multiple_of(i * SUBLANES, SUBLANES), SUBLANES)


def _s5_scan_fwd(proj_main, bbd_re, bbd_im, cbd_re, cbd_im, dvec, tab, ptab, DS):
    L = proj_main.shape[0]
    nb = DS // S5_COLS
    tb = _blk(L, S5_TIME_BLOCK, SUBLANES)
    nt = L // tb
    seg = tb // SUBLANES

    def body(u_ref, bre_ref, bim_ref, cre_ref, cim_ref, d_ref, tab_ref, pt_ref, y_ref, sre_ref, sim_ref,
             up_ref, yp_ref, car_ref):
        t = pl.program_id(1)

        @pl.when(t == 0)
        def _():
            car_ref[...] = jnp.zeros_like(car_ref)

        _to_segments(u_ref, up_ref, seg)
        up = up_ref[...]
        sre_ref[...] = _dot(up, bre_ref[0])
        sim_ref[...] = _dot(up, bim_ref[0])
        ar, ai = tab_ref[0, 0], tab_ref[0, 1]

        def pass1(i, x):
            xr = ar * x[0] - ai * x[1] + sre_ref[_slab(i), :]
            xi = ar * x[1] + ai * x[0] + sim_ref[_slab(i), :]
            sre_ref[_slab(i), :] = xr
            sim_ref[_slab(i), :] = xi
            return xr, xi

        zero = jnp.zeros((SUBLANES, S5_LANES), F32)
        er, ei = lax.fori_loop(0, seg, pass1, (zero, zero))
        cin_r, cin_i = car_ref[0], car_ref[1]
        sr, si = _scan8(er, ei, tab_ref, 2, (1, 2, 4))
        pr, pi = tab_ref[0, 8], tab_ref[0, 9]
        sr, si = sr + pr * cin_r - pi * cin_i, si + pr * cin_i + pi * cin_r
        row0 = _iota2((SUBLANES, S5_LANES), 0) == 0
        cr = jnp.where(row0, cin_r, pltpu.roll(sr, 1, 0))
        ci = jnp.where(row0, cin_i, pltpu.roll(si, 1, 0))
        car_ref[0] = jnp.broadcast_to(sr[SUBLANES - 1:SUBLANES, :], sr.shape)
        car_ref[1] = jnp.broadcast_to(si[SUBLANES - 1:SUBLANES, :], si.shape)

        def pass2(i, _):
            qr, qi = pt_ref[0, 0, pl.ds(i, 1), :], pt_ref[0, 1, pl.ds(i, 1), :]
            sre_ref[_slab(i), :] += qr * cr - qi * ci
            sim_ref[_slab(i), :] += qr * ci + qi * cr
            return 0

        lax.fori_loop(0, seg, pass2, 0, unroll=4)
        yp_ref[...] = _dot(sre_ref[...], cre_ref[0], NT) - _dot(sim_ref[...], cim_ref[0], NT) + d_ref[...] * up
        _from_segments(yp_ref, y_ref, seg)

    return pl.pallas_call(
        body, name="s5_scan_fwd", grid=(nb, nt),
        in_specs=[
            pl.BlockSpec((tb, S5_COLS), lambda j, t: (t, j)),
            pl.BlockSpec((1, S5_COLS, S5_LANES), lambda j, t: (j, 0, 0)),
            pl.BlockSpec((1, S5_COLS, S5_LANES), lambda j, t: (j, 0, 0)),
            pl.BlockSpec((1, S5_COLS, S5_LANES), lambda j, t: (j, 0, 0)),
            pl.BlockSpec((1, S5_COLS, S5_LANES), lambda j, t: (j, 0, 0)),
            pl.BlockSpec((1, S5_COLS), lambda j, t: (0, j)),
            pl.BlockSpec((1, S5_TABS, SUBLANES, S5_LANES), lambda j, t: (j, 0, 0, 0)),
            pl.BlockSpec((1, 2, seg, S5_LANES), lambda j, t: (j, 0, 0, 0)),
        ],
        out_specs=[
            pl.BlockSpec((tb, S5_COLS), lambda j, t: (t, j)),
            pl.BlockSpec((tb, S5_LANES), lambda j, t: (t, j)),
            pl.BlockSpec((tb, S5_LANES), lambda j, t: (t, j)),
        ],
        out_shape=[jax.ShapeDtypeStruct((L, DS), F32),
                   jax.ShapeDtypeStruct((L, nb * S5_LANES), F32),
                   jax.ShapeDtypeStruct((L, nb * S5_LANES), F32)],
        scratch_shapes=[pltpu.VMEM((tb, S5_COLS), F32), pltpu.VMEM((tb, S5_COLS), F32),
                        pltpu.VMEM((2, SUBLANES, S5_LANES), F32)],
        compiler_params=pltpu.CompilerParams(dimension_semantics=("parallel", "arbitrary")),
    )(proj_main, bbd_re, bbd_im, cbd_re, cbd_im, dvec, tab, ptab)


def _s5_scan_bwd(dy, proj_main, s_re, s_im, bbd_re, bbd_im, cbd_re, cbd_im, dvec, tab, ptab, d_s5, DS):
    L = proj_main.shape[0]
    nb = DS // S5_COLS
    tb = _blk(L, S5_TIME_BLOCK, SUBLANES)
    nt = L // tb
    seg = tb // SUBLANES
    tb8 = tb // SUBLANES

    def body(dy_ref, u_ref, sre_ref, sim_ref, pre_ref, pim_ref, bre_ref, bim_ref, cre_ref, cim_ref, d_ref, tab_ref, pt_ref,
             _ds5_ref, du_ref, gd_ref, gcre_ref, gcim_ref, gbre_ref, gbim_ref, gare_ref, gaim_ref,
             lre_ref, lim_ref, up_ref, dyp_ref, dup_ref, duo_ref, car_ref):
        t = pl.program_id(1)

        @pl.when(t == 0)
        def _():
            car_ref[...] = jnp.zeros_like(car_ref)
            gd_ref[...] = jnp.zeros_like(gd_ref)
            gcre_ref[...] = jnp.zeros_like(gcre_ref)
            gcim_ref[...] = jnp.zeros_like(gcim_ref)
            gbre_ref[...] = jnp.zeros_like(gbre_ref)
            gbim_ref[...] = jnp.zeros_like(gbim_ref)
            gare_ref[...] = jnp.zeros_like(gare_ref)
            gaim_ref[...] = jnp.zeros_like(gaim_ref)

        _to_segments(dy_ref, dyp_ref, seg)
        _to_segments(u_ref, up_ref, seg)
        dyv = dyp_ref[...]
        u = up_ref[...]
        gd_ref[...] += jnp.sum(dyv * u, axis=0, keepdims=True)
        lre_ref[...] = _dot(dyv, cre_ref[0])
        lim_ref[...] = -_dot(dyv, cim_ref[0])
        gcre_ref[0] += _dot(dyv, sre_ref[...], TN)
        gcim_ref[0] -= _dot(dyv, sim_ref[...], TN)
        ar, ai = tab_ref[0, 0], -tab_ref[0, 1]

        def pass1(k, x):
            i = seg - 1 - k
            xr = ar * x[0] - ai * x[1] + lre_ref[_slab(i), :]
            xi = ar * x[1] + ai * x[0] + lim_ref[_slab(i), :]
            lre_ref[_slab(i), :] = xr
            lim_ref[_slab(i), :] = xi
            return xr, xi

        zero = jnp.zeros((SUBLANES, S5_LANES), F32)
        er, ei = lax.fori_loop(0, seg, pass1, (zero, zero))
        cin_r, cin_i = car_ref[0], car_ref[1]
        lr, li = _scan8(er, ei, tab_ref, 10, (7, 6, 4))
        pr, pi = tab_ref[0, 16], tab_ref[0, 17]
        lr, li = lr + pr * cin_r - pi * cin_i, li + pr * cin_i + pi * cin_r
        rows = _iota2((SUBLANES, S5_LANES), 0)
        cr = jnp.where(rows == SUBLANES - 1, cin_r, pltpu.roll(lr, SUBLANES - 1, 0))
        ci = jnp.where(rows == SUBLANES - 1, cin_i, pltpu.roll(li, SUBLANES - 1, 0))
        car_ref[0] = jnp.broadcast_to(lr[0:1, :], lr.shape)
        car_ref[1] = jnp.broadcast_to(li[0:1, :], li.shape)

        first = (t == nt - 1).astype(F32)
        head_re = jnp.broadcast_to(pre_ref[SUBLANES - 1:SUBLANES, :], zero.shape) * (1.0 - first)
        head_im = jnp.broadcast_to(pim_ref[SUBLANES - 1:SUBLANES, :], zero.shape) * (1.0 - first)
        last = _slab(seg - 1)
        sp0_re = jnp.where(rows == 0, head_re, pltpu.roll(sre_ref[last, :], 1, 0))
        sp0_im = jnp.where(rows == 0, head_im, pltpu.roll(sim_ref[last, :], 1, 0))

        def fix(i, acc, sp_re, sp_im):
            j = seg - 1 - i
            qr, qi = pt_ref[0, 0, pl.ds(j, 1), :], -pt_ref[0, 1, pl.ds(j, 1), :]
            xr = lre_ref[_slab(i), :] + qr * cr - qi * ci
            xi = lim_ref[_slab(i), :] + qr * ci + qi * cr
            lre_ref[_slab(i), :] = xr
            lim_ref[_slab(i), :] = xi
            return acc[0] + sp_re * xr + sp_im * xi, acc[1] + sp_re * xi - sp_im * xr

        def pass2(i, acc):
            prev = _slab(jnp.maximum(i - 1, 0))
            return fix(i, acc, sre_ref[prev, :], sim_ref[prev, :])

        acc_re, acc_im = lax.fori_loop(0, seg, pass2, (zero, zero), unroll=4)
        first_slab = _slab(0)
        d_re, d_im = sp0_re - sre_ref[first_slab, :], sp0_im - sim_ref[first_slab, :]
        x0r, x0i = lre_ref[first_slab, :], lim_ref[first_slab, :]
        acc_re = acc_re + d_re * x0r + d_im * x0i
        acc_im = acc_im + d_re * x0i - d_im * x0r
        gare_ref[...] += jnp.sum(acc_re, axis=0, keepdims=True)
        gaim_ref[...] += jnp.sum(acc_im, axis=0, keepdims=True)
        lre = lre_ref[...]
        lim = lim_ref[...]
        dup_ref[...] = dyv * d_ref[...] + _dot(lre, bre_ref[0], NT) + _dot(lim, bim_ref[0], NT)
        _from_segments(dup_ref, duo_ref, seg)
        du_ref[...] = duo_ref[...].astype(BF16)
        gbre_ref[0] += _dot(u, lre, TN)
        gbim_ref[0] += _dot(u, lim, TN)

    rt = lambda t: nt - 1 - t
    col = pl.BlockSpec((tb, S5_COLS), lambda j, t: (rt(t), j))
    st = pl.BlockSpec((tb, S5_LANES), lambda j, t: (rt(t), j))
    prev = pl.BlockSpec((SUBLANES, S5_LANES), lambda j, t: (jnp.maximum(rt(t) * tb8 - 1, 0), j))
    bmat = pl.BlockSpec((1, S5_COLS, S5_LANES), lambda j, t: (j, 0, 0))
    cmat = bmat
    return pl.pallas_call(
        body, name="s5_scan_bwd", grid=(nb, nt),
        in_specs=[col, col, st, st, prev, prev, bmat, bmat, cmat, cmat,
                  pl.BlockSpec((1, S5_COLS), lambda j, t: (0, j)),
                  pl.BlockSpec((1, S5_TABS, SUBLANES, S5_LANES), lambda j, t: (j, 0, 0, 0)),
                  pl.BlockSpec((1, 2, seg, S5_LANES), lambda j, t: (j, 0, 0, 0)),
                  pl.BlockSpec(memory_space=pl.ANY)],
        out_specs=[col, pl.BlockSpec((1, S5_COLS), lambda j, t: (0, j)), cmat, cmat, bmat, bmat,
                   pl.BlockSpec((1, S5_LANES), lambda j, t: (0, j)), pl.BlockSpec((1, S5_LANES), lambda j, t: (0, j))],
        input_output_aliases={13: 0},
        out_shape=[jax.ShapeDtypeStruct((L, 2 * DS), BF16), jax.ShapeDtypeStruct((1, DS), F32),
                   jax.ShapeDtypeStruct((nb, S5_COLS, S5_LANES), F32), jax.ShapeDtypeStruct((nb, S5_COLS, S5_LANES), F32),
                   jax.ShapeDtypeStruct((nb, S5_COLS, S5_LANES), F32), jax.ShapeDtypeStruct((nb, S5_COLS, S5_LANES), F32),
                   jax.ShapeDtypeStruct((1, nb * S5_LANES), F32), jax.ShapeDtypeStruct((1, nb * S5_LANES), F32)],
        scratch_shapes=[pltpu.VMEM((tb, S5_LANES), F32), pltpu.VMEM((tb, S5_LANES), F32)]
        + [pltpu.VMEM((tb, S5_COLS), F32)] * 4 + [pltpu.VMEM((2, SUBLANES, S5_LANES), F32)],
        compiler_params=pltpu.CompilerParams(dimension_semantics=("parallel", "arbitrary")),
    )(dy, proj_main, s_re, s_im, s_re, s_im, bbd_re, bbd_im, cbd_re, cbd_im, dvec, tab, ptab, d_s5)


def _s5_post_fwd(y_pre, proj_main, glu_w, glu_b, DS):
    L = y_pre.shape[0]
    tr = _blk(L, ROW_TILE, SUBLANES)

    def body(y_ref, z_ref, w_ref, b_ref, o_ref, t_ref):
        y1 = _gelu(y_ref[...])
        t = _dot(y1, w_ref[...]) + b_ref[...]
        t_ref[...] = t
        z = z_ref[...]
        o_ref[...] = (y1 * _sigmoid(t) * (z * _sigmoid(z))).astype(BF16)

    row = pl.BlockSpec((tr, DS), lambda i: (i, 0))
    return pl.pallas_call(
        body, name="s5_post_fwd", grid=(L // tr,),
        in_specs=[row, pl.BlockSpec((tr, DS), lambda i: (i, 1)), pl.BlockSpec((DS, DS), lambda i: (0, 0)),
                  pl.BlockSpec((1, DS), lambda i: (0, 0))],
        out_specs=[row, row],
        out_shape=[jax.ShapeDtypeStruct((L, 2 * DS), BF16), jax.ShapeDtypeStruct((L, DS), F32)],
        compiler_params=pltpu.CompilerParams(dimension_semantics=("parallel",)),
    )(y_pre, proj_main, glu_w, glu_b)


def _s5_post_bwd(d_ycat, y_pre, proj_main, t_pre, glu_w, DS):
    L = y_pre.shape[0]
    tr = _blk(L, ROW_TILE, SUBLANES)

    def body(dy_ref, y_ref, z_ref, t_ref, w_ref, dyp_ref, dz_ref, dt_ref, y1_ref, gb_ref):
        i = pl.program_id(0)

        @pl.when(i == 0)
        def _():
            gb_ref[...] = jnp.zeros_like(gb_ref)

        dy = dy_ref[...]
        yp = y_ref[...]
        z = z_ref[...]
        y1 = _gelu(yp)
        sg = _sigmoid(t_ref[...])
        sz = _sigmoid(z)
        c = y1 * sg
        d_c = dy * (z * sz)
        dz_ref[...] = (dy * c * (sz * (1.0 + z * (1.0 - sz)))).astype(BF16)
        d_t = d_c * y1 * sg * (1.0 - sg)
        gb_ref[...] += jnp.sum(d_t, axis=0, keepdims=True)
        dt_ref[...] = d_t.astype(BF16)
        y1_ref[...] = y1.astype(BF16)
        d_y1 = d_c * sg + _dot(d_t, w_ref[...], NT)
        dyp_ref[...] = d_y1 * _gelu_grad(yp)

    row = pl.BlockSpec((tr, DS), lambda i: (i, 0))
    return pl.pallas_call(
        body, name="s5_post_bwd", grid=(L // tr,),
        in_specs=[row, row, pl.BlockSpec((tr, DS), lambda i: (i, 1)), row, pl.BlockSpec((DS, DS), lambda i: (0, 0))],
        out_specs=[row, pl.BlockSpec((tr, DS), lambda i: (i, 1)), row, row, pl.BlockSpec((1, DS), lambda i: (0, 0))],
        out_shape=[jax.ShapeDtypeStruct((L, DS), F32), jax.ShapeDtypeStruct((L, 2 * DS), BF16),
                   jax.ShapeDtypeStruct((L, DS), BF16), jax.ShapeDtypeStruct((L, DS), BF16),
                   jax.ShapeDtypeStruct((1, DS), F32)],
        compiler_params=pltpu.CompilerParams(dimension_semantics=("arbitrary",)),
    )(d_ycat, y_pre, proj_main, t_pre, glu_w)


def _row_cumsum(x, reverse=False):
    n = x.shape[0]
    row = lax.broadcasted_iota(jnp.int32, x.shape, 0)
    k = 1
    while k < n:
        if reverse:
            x = x + jnp.where(row < n - k, pltpu.roll(x, n - k, 0), 0.0)
        else:
            x = x + jnp.where(row >= k, pltpu.roll(x, k, 0), 0.0)
        k *= 2
    return x


def _gla_gates(glow, gu_ref, gb_ref):
    a = _dot(glow, gu_ref[...]) + gb_ref[...]
    lg = (jnp.minimum(a, 0.0) - jnp.log(1.0 + jnp.exp(-jnp.abs(a)))) * (1.0 / GLA_TAU)
    ri = lax.broadcasted_iota(jnp.int32, (GLA_CHUNK, GLA_CHUNK), 0)
    ci = lax.broadcasted_iota(jnp.int32, (GLA_CHUNK, GLA_CHUNK), 1)
    b = _row_cumsum(lg)
    b_last = b[GLA_CHUNK - 1:GLA_CHUNK, :]
    return a, b, b_last, ri >= ci


def _gla_specs(DS, DK, DV, c, cmap):
    return [
        pl.BlockSpec((c, DK), lambda n: (cmap(n), 2 * DS // DK)),
        pl.BlockSpec((c, DK), lambda n: (cmap(n), 2 * DS // DK + 1)),
        pl.BlockSpec((c, DV), lambda n: (cmap(n), (2 * DS + 2 * DK) // DV)),
        pl.BlockSpec((c, DV), lambda n: (cmap(n), (2 * DS + 2 * DK) // DV + 1)),
    ]


def _gla_fwd(proj_main, proj_low, gate_up_pad, gate_bias, norm_w, ycat, DS, DK, DV):
    L = proj_main.shape[0]
    nc = L // GLA_CHUNK
    cps = math.gcd(GLA_STEP_CHUNKS, nc)
    nh = DK // GLA_HK
    scale = GLA_HK ** -0.5

    def body(q_ref, k_ref, v_ref, z_ref, gl_ref, gu_ref, gb_ref, nw_ref, _yc_ref, y_ref, sp_ref, at_ref, o_ref, st_ref):
        n = pl.program_id(0)

        @pl.when(n == 0)
        def _():
            st_ref[...] = jnp.zeros_like(st_ref)

        pairs = [(sc, h) for sc in range(cps) for h in range(nh)]
        rows = lambda sc: slice(sc * GLA_CHUNK, (sc + 1) * GLA_CHUNK)
        kcol = lambda h: slice(h * GLA_HK, (h + 1) * GLA_HK)
        vcol = lambda h: slice(h * GLA_HV, (h + 1) * GLA_HV)
        gates = [_gla_gates(gl_ref[rows(sc), :], gu_ref, gb_ref) for sc in range(cps)]
        qe, dec, o_in, kv = {}, {}, {}, {}
        for sc, h in pairs:
            _, b, b_last, mask = gates[sc]
            bh, bl = b[:, kcol(h)], b_last[:, kcol(h)]
            qe[sc, h] = (q_ref[rows(sc), kcol(h)] * scale) * jnp.exp(bh)
            kh = k_ref[rows(sc), kcol(h)]
            vh = v_ref[rows(sc), vcol(h)]
            attn = jnp.where(mask, _dot(qe[sc, h], kh * jnp.exp(-bh), NT), 0.0).astype(BF16)
            at_ref[h, rows(sc), :] = attn
            o_in[sc, h] = _dot(attn, vh)
            kv[sc, h] = _dot(vh, kh * jnp.exp(bl - bh), TN)
            dec[sc, h] = jnp.exp(bl)
        for sc, h in pairs:
            st = st_ref[h]
            sp_ref[sc, h] = st
            o = o_in[sc, h] + _dot(qe[sc, h], st, NT)
            o_ref[rows(sc), vcol(h)] = o
            st_ref[h] = dec[sc, h] * st + kv[sc, h]
            r = lax.rsqrt(jnp.mean(o * o, axis=-1, keepdims=True) + EPS)
            z = z_ref[rows(sc), vcol(h)]
            y_ref[rows(sc), vcol(h)] = (o * r * nw_ref[...] * (z * _sigmoid(z))).astype(BF16)

    c = cps * GLA_CHUNK
    return pl.pallas_call(
        body, name="gla_fwd", grid=(nc // cps,),
        in_specs=_gla_specs(DS, DK, DV, c, lambda n: n) + [
            pl.BlockSpec((c, LANES), lambda n: (n, 0)),
            pl.BlockSpec((LANES, DK), lambda n: (0, 0)),
            pl.BlockSpec((1, DK), lambda n: (0, 0)),
            pl.BlockSpec((1, GLA_HV), lambda n: (0, 0)),
            pl.BlockSpec(memory_space=pl.ANY),
        ],
        out_specs=[pl.BlockSpec((c, DV), lambda n: (n, DS // DV)),
                   pl.BlockSpec((cps, nh, GLA_HV, GLA_HK), lambda n: (n, 0, 0, 0)),
                   pl.BlockSpec((nh, c, GLA_CHUNK), lambda n: (0, n, 0)),
                   pl.BlockSpec((c, DV), lambda n: (n, 0))],
        input_output_aliases={8: 0},
        out_shape=[jax.ShapeDtypeStruct(ycat.shape, BF16), jax.ShapeDtypeStruct((nc, nh, GLA_HV, GLA_HK), F32),
                   jax.ShapeDtypeStruct((nh, L, GLA_CHUNK), BF16), jax.ShapeDtypeStruct((L, DV), F32)],
        scratch_shapes=[pltpu.VMEM((nh, GLA_HV, GLA_HK), F32)],
        compiler_params=pltpu.CompilerParams(dimension_semantics=("arbitrary",)),
    )(proj_main, proj_main, proj_main, proj_main, proj_low, gate_up_pad, gate_bias, norm_w, ycat)


def _gla_bwd(d_ycat, proj_main, proj_low, s_prev, scores, o_pre, gate_up_pad, gate_bias, norm_w, DS, DK, DV):
    L = proj_main.shape[0]
    nc = L // GLA_CHUNK
    cps = math.gcd(GLA_STEP_CHUNKS, nc)
    nh = DK // GLA_HK
    scale = GLA_HK ** -0.5

    def body(dy_ref, q_ref, k_ref, v_ref, z_ref, gl_ref, sp_ref, at_ref, o_ref, gu_ref, gb_ref, nw_ref,
             dg_ref, da_ref, gnw_ref, ggb_ref, dst_ref):
        n = pl.program_id(0)

        @pl.when(n == 0)
        def _():
            dst_ref[...] = jnp.zeros_like(dst_ref)
            gnw_ref[...] = jnp.zeros_like(gnw_ref)
            ggb_ref[...] = jnp.zeros_like(ggb_ref)

        last_row = lax.broadcasted_iota(jnp.int32, (GLA_CHUNK, GLA_HK), 0) == GLA_CHUNK - 1
        nw = nw_ref[...]
        for sc in reversed(range(cps)):
            rs = slice(sc * GLA_CHUNK, (sc + 1) * GLA_CHUNK)
            a, b, b_last, mask = _gla_gates(gl_ref[rs, :], gu_ref, gb_ref)
            for h in range(nh):
                ks = slice(h * GLA_HK, (h + 1) * GLA_HK)
                vs = slice(h * GLA_HV, (h + 1) * GLA_HV)
                bh, bl = b[:, ks], b_last[:, ks]
                e = jnp.exp(bh)
                einv = jnp.exp(-bh)
                etail = jnp.exp(bl - bh)
                dec = jnp.exp(bl)
                qe = (q_ref[rs, ks] * scale) * e
                kh = k_ref[rs, ks]
                ke = kh * einv
                ktail = kh * etail
                vh = v_ref[rs, vs]
                st = sp_ref[sc, h]
                dst = dst_ref[h]
                attn = at_ref[h, rs, :]
                o = o_ref[rs, vs]
                r = lax.rsqrt(jnp.mean(o * o, axis=-1, keepdims=True) + EPS)
                nrm = o * r
                z = z_ref[rs, vs]
                sz = _sigmoid(z)
                dy = dy_ref[rs, vs]
                dg_ref[rs, 2 * DK + DV + h * GLA_HV:2 * DK + DV + (h + 1) * GLA_HV] = (
                    dy * nrm * nw * (sz * (1.0 + z * (1.0 - sz)))).astype(BF16)
                d_on = dy * (z * sz)
                gnw_ref[...] += jnp.sum(d_on * nrm, axis=0, keepdims=True)
                d_n = d_on * nw
                d_o = r * (d_n - nrm * jnp.mean(d_n * nrm, axis=-1, keepdims=True))
                d_attn = jnp.where(mask, _dot(d_o, vh, NT), 0.0)
                dg_ref[rs, 2 * DK + h * GLA_HV:2 * DK + (h + 1) * GLA_HV] = (
                    _dot(attn, d_o, TN) + _dot(ktail, dst, NT)).astype(BF16)
                d_qe = _dot(d_attn, ke) + _dot(d_o, st)
                d_ke = _dot(d_attn, qe, TN)
                d_kt = _dot(vh, dst)
                d_dec = jnp.sum(dst * st, axis=0, keepdims=True)
                dst_ref[h] = dec * dst + _dot(d_o, qe, TN)
                dg_ref[rs, ks] = (d_qe * scale * e).astype(BF16)
                dg_ref[rs, DK + h * GLA_HK:DK + (h + 1) * GLA_HK] = (d_ke * einv + d_kt * etail).astype(BF16)
                d_bl = jnp.sum(d_kt * ktail, axis=0, keepdims=True) + d_dec * dec
                d_b = d_qe * qe - d_ke * ke - d_kt * ktail + jnp.where(last_row, d_bl, 0.0)
                d_lg = _row_cumsum(d_b, reverse=True)
                d_a = d_lg * (1.0 / GLA_TAU) * _sigmoid(-a[:, ks])
                ggb_ref[:, ks] += jnp.sum(d_a, axis=0, keepdims=True)
                da_ref[rs, ks] = d_a.astype(BF16)

    c = cps * GLA_CHUNK
    ns = nc // cps
    rn = lambda n: ns - 1 - n
    return pl.pallas_call(
        body, name="gla_bwd", grid=(ns,),
        in_specs=[pl.BlockSpec((c, DV), lambda n: (rn(n), DS // DV))] + _gla_specs(DS, DK, DV, c, rn) + [
            pl.BlockSpec((c, LANES), lambda n: (rn(n), 0)),
            pl.BlockSpec((cps, nh, GLA_HV, GLA_HK), lambda n: (rn(n), 0, 0, 0)),
            pl.BlockSpec((nh, c, GLA_CHUNK), lambda n: (0, rn(n), 0)),
            pl.BlockSpec((c, DV), lambda n: (rn(n), 0)),
            pl.BlockSpec((LANES, DK), lambda n: (0, 0)),
            pl.BlockSpec((1, DK), lambda n: (0, 0)),
            pl.BlockSpec((1, GLA_HV), lambda n: (0, 0)),
        ],
        out_specs=[pl.BlockSpec((c, 2 * DK + 2 * DV), lambda n: (rn(n), 0)),
                   pl.BlockSpec((c, DK), lambda n: (rn(n), 0)),
                   pl.BlockSpec((1, GLA_HV), lambda n: (0, 0)), pl.BlockSpec((1, DK), lambda n: (0, 0))],
        out_shape=[jax.ShapeDtypeStruct((L, 2 * DK + 2 * DV), BF16),
                   jax.ShapeDtypeStruct((L, DK), BF16),
                   jax.ShapeDtypeStruct((1, GLA_HV), F32), jax.ShapeDtypeStruct((1, DK), F32)],
        scratch_shapes=[pltpu.VMEM((nh, GLA_HV, GLA_HK), F32)],
        compiler_params=pltpu.CompilerParams(dimension_semantics=("arbitrary",)),
    )(d_ycat, proj_main, proj_main, proj_main, proj_main, proj_low, s_prev, scores, o_pre, gate_up_pad, gate_bias, norm_w)


def _adamw_math(w, g, m, v):
    c1 = 1.0 - ADAM_B1 ** ADAM_STEP
    c2 = 1.0 - ADAM_B2 ** ADAM_STEP
    m_ = ADAM_B1 * m + (1.0 - ADAM_B1) * g
    v_ = ADAM_B2 * v + (1.0 - ADAM_B2) * (g * g)
    return -ADAM_LR * ((m_ / c1) / (jnp.sqrt(v_ / c2) + ADAM_EPS) + ADAM_WD * w), m_, v_


def _adamw_small(g_row, g_a, g_bc, ws, ms, vs):
    n = len(ws)
    nvec = n - 6

    def body(*refs):
        grow_ref, ga_ref, gbc_ref = refs[:3]
        w_refs, m_refs, v_refs = refs[3:3 + n], refs[3 + n:3 + 2 * n], refs[3 + 2 * n:3 + 3 * n]
        outs = refs[3 + 3 * n:]
        off = 0
        for i in range(n):
            if i < nvec:
                width = ws[i].shape[1]
                g = grow_ref[:, off:off + width]
                off += width
            elif i < nvec + 2:
                g = ga_ref[i - nvec]
            else:
                g = gbc_ref[i - nvec - 2]
            d, m_, v_ = _adamw_math(w_refs[i][...], g, m_refs[i][...], v_refs[i][...])
            outs[i][...] = g
            outs[n + i][...] = d
            outs[2 * n + i][...] = m_
            outs[3 * n + i][...] = v_

    vm = pl.BlockSpec(memory_space=pltpu.VMEM)
    outs = pl.pallas_call(
        body, name="adamw_small",
        in_specs=[vm] * (3 + 3 * n), out_specs=[vm] * (4 * n),
        out_shape=[jax.ShapeDtypeStruct(w.shape, F32) for w in ws] * 4,
    )(g_row, g_a, g_bc, *ws, *ms, *vs)
    return [outs[k * n:(k + 1) * n] for k in range(4)]


def _my_pos():
    return lax.axis_index("x"), lax.axis_index("y"), lax.axis_index("c")


def _split_start(name, srcs, lands_sd, make_copies, ncopies, after):
    n, m = len(srcs), len(lands_sd)

    def body(*refs):
        send_sems, recv_sems = refs[n + m + len(after)], refs[n + m + len(after) + 1]
        for cp in make_copies(refs[:n], refs[n:n + m], send_sems, recv_sems):
            cp.start()
        refs[-1][...] = jnp.zeros_like(refs[-1])

    hbm = pl.BlockSpec(memory_space=pltpu.HBM)
    sem = pl.BlockSpec(memory_space=pltpu.SEMAPHORE)
    outs = pl.pallas_call(
        body, name=name,
        in_specs=[hbm] * (n + m) + [pl.BlockSpec(memory_space=pl.ANY)] * len(after),
        out_specs=[sem, sem] + [hbm] * (n + m) + [pl.BlockSpec(memory_space=pltpu.VMEM)],
        out_shape=[pltpu.SemaphoreType.DMA((ncopies,)), pltpu.SemaphoreType.DMA((ncopies,))]
        + [pltpu.HBM(s.shape, s.dtype) for s in srcs] + [pltpu.HBM(s.shape, s.dtype) for s in lands_sd]
        + [jax.ShapeDtypeStruct((SUBLANES, LANES), F32)],
        input_output_aliases={i: 2 + i for i in range(n + m)},
        compiler_params=pltpu.CompilerParams(has_side_effects=pltpu.SideEffectType.DATAFLOW_SIDE_EFFECTING),
    )(*[pltpu.with_memory_space_constraint(s, pltpu.HBM) for s in srcs],
      *[pltpu.with_memory_space_constraint(lax.empty(s.shape, s.dtype), pltpu.HBM) for s in lands_sd], *after)
    return outs[0], outs[1], outs[2:2 + n], outs[2 + n:2 + n + m], outs[-1]


def _split_wait(name, send_sems, recv_sems, srcs, lands, make_copies, after):
    n, m = len(srcs), len(lands)

    def body(*refs):
        for cp in make_copies(refs[:n], refs[n:n + m], refs[n + m], refs[n + m + 1]):
            cp.wait_send()
            cp.wait_recv()

    hbm = pl.BlockSpec(memory_space=pltpu.HBM)
    sem = pl.BlockSpec(memory_space=pltpu.SEMAPHORE)
    outs = pl.pallas_call(
        body, name=name,
        in_specs=[hbm] * (n + m) + [sem, sem] + [pl.BlockSpec(memory_space=pl.ANY)] * len(after),
        out_specs=[hbm] * (n + m),
        out_shape=[pltpu.HBM(s.shape, s.dtype) for s in srcs] + [pltpu.HBM(p.shape, p.dtype) for p in lands],
        input_output_aliases={i: i for i in range(n + m)},
        compiler_params=pltpu.CompilerParams(has_side_effects=pltpu.SideEffectType.DATAFLOW_SIDE_EFFECTING),
    )(*srcs, *lands, send_sems, recv_sems, *after)
    return outs[:n], outs[n:]


def _pair_half_copies(srcs, lands, send_sems, recv_sems):
    x, y, c = _my_pos()
    copies = []
    for a in range(len(srcs)):
        hrows = srcs[a].shape[1] // 2
        copies.append(pltpu.make_async_remote_copy(
            src_ref=srcs[a].at[:, pl.ds((1 - c) * hrows, hrows), :], dst_ref=lands[a], send_sem=send_sems.at[a],
            recv_sem=recv_sems.at[a], device_id=(x, y, 1 - c), device_id_type=MESH))
    return copies


def _late_gather_copies(srcs, lands, send_sems, recv_sems):
    x, y, c = _my_pos()
    me = 2 * x + y
    copies = []
    for d in (1, 2, 3):
        to = (x ^ (d >> 1), y ^ (d & 1), c)
        for a in range(len(srcs)):
            hrows = srcs[a].shape[0] // 2
            rows = pl.ds(c * hrows, hrows)
            copies.append(pltpu.make_async_remote_copy(
                src_ref=srcs[a].at[rows, :], dst_ref=lands[a].at[me, rows, :], send_sem=send_sems.at[3 * a + d - 1],
                recv_sem=recv_sems.at[3 * a + d - 1], device_id=to, device_id_type=MESH))
    return copies


def _late_gather_start(shards, after, name):
    lands = [jax.ShapeDtypeStruct((4,) + s.shape, s.dtype) for s in shards]
    return _split_start(name, shards, lands, _late_gather_copies, 3 * len(shards), [after])


def _late_gather_wait(send_sems, recv_sems, shards, lands, after, name):
    return _split_wait(name, send_sems, recv_sems, shards, lands, _late_gather_copies, after)[1]


def _late_gather_pair(lands, name):
    n = len(lands)

    def body(*refs):
        outs = refs[n:2 * n]
        send_sems, recv_sems = refs[2 * n:]
        x, y, c = _my_pos()

        def copy(a, d, half):
            chip = 2 * (x ^ (d >> 1)) + (y ^ (d & 1))
            hrows = lands[a].shape[1] // 2
            sl = outs[a].at[chip, pl.ds(half * hrows, hrows), :]
            return pltpu.make_async_remote_copy(src_ref=sl, dst_ref=sl, send_sem=send_sems.at[3 * a + d - 1],
                                                recv_sem=recv_sems.at[3 * a + d - 1], device_id=(x, y, 1 - c),
                                                device_id_type=MESH)

        pairs = [(a, d) for d in (1, 2, 3) for a in range(n)]
        for a, d in pairs:
            copy(a, d, c).start()
        for a, d in pairs:
            copy(a, d, c).wait_send()
            copy(a, d, 1 - c).wait_recv()

    hbm = pl.BlockSpec(memory_space=pltpu.HBM)
    return pl.pallas_call(
        body, name=name, in_specs=[hbm] * n, out_specs=[hbm] * n,
        out_shape=[jax.ShapeDtypeStruct(p.shape, p.dtype) for p in lands],
        input_output_aliases={i: i for i in range(n)},
        scratch_shapes=[pltpu.SemaphoreType.DMA((3 * n,)), pltpu.SemaphoreType.DMA((3 * n,))],
    )(*lands)


def _pair_exchange(gs):
    n = len(gs)

    def body(*refs):
        ins, outs = refs[:n], refs[n:2 * n]
        send_sems, recv_sems = refs[2 * n:]
        x, y, c = _my_pos()
        sent = []
        for a in range(n):
            hrows = gs[a].shape[1] // 2
            cp = pltpu.make_async_remote_copy(
                src_ref=ins[a].at[:, pl.ds((1 - c) * hrows, hrows), :], dst_ref=outs[a], send_sem=send_sems.at[a],
                recv_sem=recv_sems.at[a], device_id=(x, y, 1 - c), device_id_type=MESH)
            cp.start()
            sent.append(cp)
        for cp in sent:
            cp.wait()

    hbm = pl.BlockSpec(memory_space=pltpu.HBM)
    return pl.pallas_call(
        body, name="grad_pair_exchange", in_specs=[hbm] * n, out_specs=[hbm] * n,
        out_shape=[jax.ShapeDtypeStruct((g.shape[0], g.shape[1] // 2, g.shape[2]), g.dtype) for g in gs],
        scratch_shapes=[pltpu.SemaphoreType.DMA((n,)), pltpu.SemaphoreType.DMA((n,))],
    )(*gs)


def _pair_add(g, got, c_arr, name):
    nk, rows2, cols = g.shape
    hrows = rows2 // 2
    tr = _blk(hrows, 256, 2 * SUBLANES)
    nb = hrows // tr

    def body(c_ref, a_ref, b_ref, o_ref):
        o_ref[...] = (a_ref[...].astype(F32) + b_ref[...].astype(F32)).astype(o_ref.dtype)

    return pl.pallas_call(
        body, name=name,
        grid_spec=pltpu.PrefetchScalarGridSpec(
            num_scalar_prefetch=1, grid=(nk, nb),
            in_specs=[pl.BlockSpec((1, tr, cols), lambda k, i, c_ref: (k, c_ref[0] * nb + i, 0)),
                      pl.BlockSpec((1, tr, cols), lambda k, i, c_ref: (k, i, 0))],
            out_specs=pl.BlockSpec((1, tr, cols), lambda k, i, c_ref: (k, i, 0))),
        out_shape=jax.ShapeDtypeStruct((nk, hrows, cols), g.dtype),
        compiler_params=pltpu.CompilerParams(dimension_semantics=("parallel", "parallel")),
    )(c_arr, g, got)


def _chip_scatter_copies(srcs, lands, send_sems, recv_sems):
    x, y, c = _my_pos()
    copies = []
    for d in (1, 2, 3):
        tx, ty = x ^ (d >> 1), y ^ (d & 1)
        for a in range(len(srcs)):
            copies.append(pltpu.make_async_remote_copy(
                src_ref=srcs[a].at[2 * tx + ty], dst_ref=lands[a].at[d - 1], send_sem=send_sems.at[3 * a + d - 1],
                recv_sem=recv_sems.at[3 * a + d - 1], device_id=(tx, ty, c), device_id_type=MESH))
    return copies


def _chip_scatter_start(pss):
    lands = [jax.ShapeDtypeStruct((3,) + p.shape[1:], p.dtype) for p in pss]
    return _split_start("grad_chip_scatter_start", pss, lands, _chip_scatter_copies, 3 * len(pss), [])


def _chip_scatter_wait(send_sems, recv_sems, srcs, lands, after):
    return _split_wait("grad_chip_scatter_wait", send_sems, recv_sems, srcs, lands, _chip_scatter_copies, [after])


def _chip_sum(ps, got, me_arr, name):
    _, hrows, cols = ps.shape
    tr = _blk(hrows, 256, 2 * SUBLANES)

    def body(me_ref, p_ref, g_ref, o_ref):
        acc = p_ref[0].astype(F32)
        for s in range(3):
            acc = acc + g_ref[s].astype(F32)
        o_ref[...] = acc

    return pl.pallas_call(
        body, name=name,
        grid_spec=pltpu.PrefetchScalarGridSpec(
            num_scalar_prefetch=1, grid=(hrows // tr,),
            in_specs=[pl.BlockSpec((1, tr, cols), lambda i, me_ref: (me_ref[0], i, 0)),
                      pl.BlockSpec((3, tr, cols), lambda i, me_ref: (0, i, 0))],
            out_specs=pl.BlockSpec((tr, cols), lambda i, me_ref: (i, 0))),
        out_shape=jax.ShapeDtypeStruct((hrows, cols), F32),
        compiler_params=pltpu.CompilerParams(dimension_semantics=("parallel",)),
    )(me_arr, ps, got)


def _pair_swap(halves):
    n = len(halves)

    def body(*refs):
        ins, outs = refs[:n], refs[n:2 * n]
        send_sems, recv_sems = refs[2 * n:]
        x, y, c = _my_pos()
        sent = []
        for a in range(n):
            cp = pltpu.make_async_remote_copy(src_ref=ins[a], dst_ref=outs[a], send_sem=send_sems.at[a], recv_sem=recv_sems.at[a],
                                              device_id=(x, y, 1 - c), device_id_type=MESH)
            cp.start()
            sent.append(cp)
        for cp in sent:
            cp.wait()

    hbm = pl.BlockSpec(memory_space=pltpu.HBM)
    return pl.pallas_call(
        body, name="grad_pair_swap", in_specs=[hbm] * n, out_specs=[hbm] * n,
        out_shape=[jax.ShapeDtypeStruct(h.shape, h.dtype) for h in halves],
        scratch_shapes=[pltpu.SemaphoreType.DMA((n,)), pltpu.SemaphoreType.DMA((n,))],
    )(*halves)


def _adamw_sharded(w, g_own, g_other, m, v, c_arr, after, name):
    R, C = w.shape
    hrows = R // 2
    tr = _blk(hrows, 256, SUBLANES)
    nbh = hrows // tr

    def body(c_ref, w_ref, go_ref, gx_ref, m_ref, v_ref, _after_ref, g_ref, d_ref, nm_ref, nv_ref):
        mine = (pl.program_id(0) // nbh) == c_ref[0]
        g_ = jnp.where(mine, go_ref[...], gx_ref[...])
        g_ref[...] = g_
        d_ref[...], nm_ref[...], nv_ref[...] = _adamw_math(w_ref[...], g_, m_ref[...], v_ref[...])

    blk = pl.BlockSpec((tr, C), lambda i, c_ref: (i, 0))
    hblk = pl.BlockSpec((tr, C), lambda i, c_ref: (i % nbh, 0))
    sd = jax.ShapeDtypeStruct((R, C), F32)
    return pl.pallas_call(
        body, name=name,
        grid_spec=pltpu.PrefetchScalarGridSpec(
            num_scalar_prefetch=1, grid=(2 * nbh,),
            in_specs=[blk, hblk, hblk, blk, blk, pl.BlockSpec(memory_space=pl.ANY)], out_specs=[blk] * 4),
        out_shape=[sd] * 4,
        compiler_params=pltpu.CompilerParams(dimension_semantics=("parallel",)),
    )(c_arr, w, g_own, g_other, m, v, after)


def _ar_piece(ref, rows, p):
    start = p * rows
    if rows % SUBLANES == 0:
        start = pl.multiple_of(start, SUBLANES)
    return ref.at[..., pl.ds(start, rows), :]


def _ar_peer(d):
    x, y, c = _my_pos()
    return (x ^ (d >> 2), y ^ ((d >> 1) & 1), c ^ (d & 1))


def _ar_lin(p):
    return 4 * p[0] + 2 * p[1] + p[2]


def _ar_scatter_copies(rows):
    def make(srcs, lands, send_sems, recv_sems):
        n = len(srcs)
        copies = []
        for d in range(1, 8):
            to = _ar_peer(d)
            for a in range(n):
                copies.append(pltpu.make_async_remote_copy(
                    src_ref=_ar_piece(srcs[a], rows[a], _ar_lin(to)), dst_ref=lands[a].at[d],
                    send_sem=send_sems.at[(d - 1) * n + a], recv_sem=recv_sems.at[(d - 1) * n + a], device_id=to,
                    device_id_type=MESH))
        return copies
    return make


def _ar_gather_copies(rows):
    def make(srcs, lands, send_sems, recv_sems):
        n = len(srcs)
        me = _ar_lin(_my_pos())
        copies = []
        for d in range(1, 8):
            for a in range(n):
                copies.append(pltpu.make_async_remote_copy(
                    src_ref=srcs[a], dst_ref=_ar_piece(lands[a], rows[a], me),
                    send_sem=send_sems.at[(d - 1) * n + a], recv_sem=recv_sems.at[(d - 1) * n + a], device_id=_ar_peer(d),
                    device_id_type=MESH))
        return copies
    return make


def _ar_sum(srcs, lands, rows):
    n = len(srcs)

    def body(*refs):
        me = _ar_lin(_my_pos())
        for a in range(n):
            acc = _ar_piece(refs[a], rows[a], me)[...]
            for d in range(1, 8):
                acc = acc + refs[n + a][d]
            refs[2 * n + a][...] = acc

    vm = pl.BlockSpec(memory_space=pltpu.VMEM)
    return pl.pallas_call(
        body, name="allreduce_sum", in_specs=[vm] * (2 * n), out_specs=[vm] * n,
        out_shape=[jax.ShapeDtypeStruct(p.shape[1:], F32) for p in lands],
    )(*srcs, *lands)


def kernel(x, pre_norm_w, w_in, s5_A_re, s5_A_im, s5_B_re, s5_B_im, s5_C_re, s5_C_im, s5_D, s5_log_dt, s5_glu_w, s5_glu_b, gla_gate_up, gla_gate_bias, gla_norm_w, w_out, post_norm_w, loss_target, m_pre_norm_w, m_w_in, m_s5_A_re, m_s5_A_im, m_s5_B_re, m_s5_B_im, m_s5_C_re, m_s5_C_im, m_s5_D, m_s5_log_dt, m_s5_glu_w, m_s5_glu_b, m_gla_gate_up, m_gla_gate_bias, m_gla_norm_w, m_w_out, m_post_norm_w, v_pre_norm_w, v_w_in, v_s5_A_re, v_s5_A_im, v_s5_B_re, v_s5_B_im, v_s5_C_re, v_s5_C_im, v_s5_D, v_s5_log_dt, v_s5_glu_w, v_s5_glu_b, v_gla_gate_up, v_gla_gate_bias, v_gla_norm_w, v_w_out, v_post_norm_w):
    names = ["pre_norm_w", "w_in", "s5_A_re", "s5_A_im", "s5_B_re", "s5_B_im", "s5_C_re", "s5_C_im", "s5_D", "s5_log_dt",
             "s5_glu_w", "s5_glu_b", "gla_gate_up", "gla_gate_bias", "gla_norm_w", "w_out", "post_norm_w"]
    W = dict(zip(names, (pre_norm_w, w_in, s5_A_re, s5_A_im, s5_B_re, s5_B_im, s5_C_re, s5_C_im, s5_D, s5_log_dt,
                         s5_glu_w, s5_glu_b, gla_gate_up, gla_gate_bias, gla_norm_w, w_out, post_norm_w)))
    M = dict(zip(names, (m_pre_norm_w, m_w_in, m_s5_A_re, m_s5_A_im, m_s5_B_re, m_s5_B_im, m_s5_C_re, m_s5_C_im, m_s5_D,
                         m_s5_log_dt, m_s5_glu_w, m_s5_glu_b, m_gla_gate_up, m_gla_gate_bias, m_gla_norm_w, m_w_out,
                         m_post_norm_w)))
    V = dict(zip(names, (v_pre_norm_w, v_w_in, v_s5_A_re, v_s5_A_im, v_s5_B_re, v_s5_B_im, v_s5_C_re, v_s5_C_im, v_s5_D,
                         v_s5_log_dt, v_s5_glu_w, v_s5_glu_b, v_gla_gate_up, v_gla_gate_bias, v_gla_norm_w, v_w_out,
                         v_post_norm_w)))
    sharded = ("w_in", "s5_glu_w", "w_out", "gla_gate_up")

    xb = x[0]
    tgt = loss_target[0]
    L, D = xb.shape
    DS = D // 2
    G = DS // S5_GROUP
    P = S5_STATE
    NB = DS // S5_COLS
    DV = D - DS
    DK = DV // 2
    WM = 2 * DS + 2 * DK + 2 * DV
    nsh = w_in.shape[2]

    chip = 2 * lax.axis_index("x") + lax.axis_index("y")
    own = [jnp.pad(w_in[0].astype(BF16), ((0, 0), (0, -nsh % LANES))), s5_glu_w[0].astype(BF16),
           w_out[0].astype(BF16), gla_gate_up[0]]
    fill = lambda g, o: lax.dynamic_update_index_in_dim(g, o, chip, 0)
    win_ss, win_rs, win_src, win_lands, win_token = _late_gather_start(own[:1], pre_norm_w, "w_in_gather_start")
    h = _prenorm_fwd(xb, pre_norm_w, win_token)

    b_view = lambda t: jnp.transpose(t[0], (0, 2, 1)).reshape(G * S5_GROUP, P)
    b_back = lambda t: jnp.transpose(t.reshape(G, S5_GROUP, P), (0, 2, 1))[None]
    c_view = lambda t: t[0].reshape(G * S5_GROUP, P)
    c_back = lambda t: t.reshape(1, G, S5_GROUP, P)
    small = ["pre_norm_w", "post_norm_w", "s5_D", "s5_glu_b", "gla_gate_bias", "gla_norm_w", "s5_log_dt",
             "s5_A_re", "s5_A_im", "s5_B_re", "s5_B_im", "s5_C_re", "s5_C_im"]
    view = {n: (lambda t: t) for n in small[:7]}
    back = dict(view)
    view.update(s5_A_re=lambda t: t[0], s5_A_im=lambda t: t[0], s5_B_re=b_view, s5_B_im=b_view, s5_C_re=c_view, s5_C_im=c_view)
    back.update(s5_A_re=lambda t: t[None], s5_A_im=lambda t: t[None], s5_B_re=b_back, s5_B_im=b_back, s5_C_re=c_back,
                s5_C_im=c_back)
    Wv = {n: view[n](W[n]) for n in small}
    bbd_re, bbd_im, ct_re, ct_im, tab, ptab = _s5_prep_fwd(
        Wv["s5_A_re"], Wv["s5_A_im"], s5_log_dt, Wv["s5_B_re"], Wv["s5_B_im"], Wv["s5_C_re"], Wv["s5_C_im"],
        h, _blk(L, S5_TIME_BLOCK, SUBLANES) // SUBLANES)
    dvec = s5_D

    for d_ in (W, M, V):
        d_["w_in"], _ = lax.optimization_barrier((d_["w_in"], win_token))
    g_win = _late_gather_wait(win_ss, win_rs, win_src, win_lands,
                              [tab, W["w_in"][0], M["w_in"][0], V["w_in"][0]], "w_in_gather_wait")
    g_win = fill(_late_gather_pair(g_win, "w_in_gather_pair")[0], own[0])
    w_main, w_low = _assemble_w_in(g_win, nsh, WM)
    late_ss, late_rs, late_src, late_lands, late_token = _late_gather_start(own[1:], g_win, "late_gather_start")
    proj_main, proj_low = _in_proj(h, w_main, w_low, late_token)
    y_pre, s_re, s_im = _s5_scan_fwd(proj_main, bbd_re, bbd_im, ct_re, ct_im, dvec, tab, ptab, DS)
    late = _late_gather_wait(late_ss, late_rs, late_src, late_lands, [y_pre], "late_gather_wait")
    late = _late_gather_pair(late, "late_gather_pair")
    g_glu, g_wout, g_gup = [fill(g, o) for g, o in zip(late, own[1:])]
    glu_w = g_glu.reshape(DS, DS)
    wout = g_wout.reshape(D, D)
    gup = jnp.moveaxis(g_gup, 0, 1).reshape(GLA_RANK, DK)
    gup_pad = jnp.pad(gup, ((0, LANES - GLA_RANK), (0, 0))).astype(BF16)
    ycat, t_pre = _s5_post_fwd(y_pre, proj_main, glu_w, s5_glu_b, DS)
    ycat, s_prev, gla_scores, gla_o = _gla_fwd(proj_main, proj_low, gup_pad, gla_gate_bias, gla_norm_w, ycat,
                                               DS, DK, DV)
    mixed = _mm(ycat, wout, name="out_proj")
    loss11, d_mixed, dout, g_post_w = _post_fwd_bwd(mixed, xb, tgt, post_norm_w)

    d_ycat = _mm(d_mixed, wout, tb=True, name="out_proj_dx")
    d_ypre, d_s5, d_t, y1, g_glu_b = _s5_post_bwd(d_ycat, y_pre, proj_main, t_pre, glu_w, DS)
    d_s5, g_D, gct_re, gct_im, gbbd_re, gbbd_im, gab_re, gab_im = _s5_scan_bwd(
        d_ypre, proj_main, s_re, s_im, bbd_re, bbd_im, ct_re, ct_im, dvec, tab, ptab, d_s5, DS)
    d_gla, d_a, g_norm_w, g_gate_bias = _gla_bwd(
        d_ycat, proj_main, proj_low, s_prev, gla_scores, gla_o, gup_pad, gla_gate_bias, gla_norm_w, DS, DK, DV)
    d_low = _mm(d_a, gup_pad, tb=True, out_dtype=BF16, name="gate_dx")
    g_gup_pad = _mm(proj_low, d_a, ta=True, name="gate_dw")
    g_wmain, g_wlow = _in_proj_dw(h, d_s5, d_gla, d_low)

    g_win_sh = _split_w_in_grad(g_wmain, g_wlow, nsh)
    px_ss, px_rs, px_src, px_got, px_token = _split_start(
        "grad_pair_w_in_start", [g_win_sh], [jax.ShapeDtypeStruct((4, D // 2, nsh), BF16)], _pair_half_copies, 1, [])
    g_wout_full = _mm(ycat, d_mixed, ta=True, out_dtype=BF16, name="out_proj_dw", after=[px_token])
    g_glu_full = _mm(y1, d_t, ta=True, out_dtype=BF16, name="glu_dw", after=[px_token])
    px_src, px_got = _split_wait("grad_pair_w_in_wait", px_ss, px_rs, px_src, px_got, _pair_half_copies,
                                 [g_wout_full, g_glu_full])
    gs = [g_glu_full.reshape(4, DS // 4, DS), g_wout_full.reshape(4, D // 4, D),
          jnp.moveaxis(g_gup_pad[:GLA_RANK].reshape(GLA_RANK, 4, DK // 4), 1, 0)]
    c_arr = lax.axis_index("c").astype(jnp.int32).reshape(1)
    me_arr = chip.astype(jnp.int32).reshape(1)
    got = list(px_got) + list(_pair_exchange(gs))
    gs = list(px_src) + gs
    pss = [_pair_add(g, r, c_arr, "grad_pair_add_" + n) for n, g, r in zip(sharded, gs, got)]
    send_sems, recv_sems, pss, lands, token = _chip_scatter_start(pss)

    dh = _in_proj_dx(d_s5, d_gla, d_low, w_main, w_low, token)
    grad_x, g_pre_w = _prenorm_bwd(xb, dh, dout, pre_norm_w)

    g_a, g_bc, g_ldt = _s5_prep_bwd(Wv["s5_A_re"], Wv["s5_A_im"], s5_log_dt, Wv["s5_B_re"], Wv["s5_B_im"],
                                    gbbd_re, gbbd_im, gct_re, gct_im, gab_re, gab_im)

    g_vecs = jnp.concatenate([g_pre_w, g_post_w, g_D, g_glu_b, g_gate_bias, g_norm_w, g_ldt, loss11], axis=1)
    loss_at = g_vecs.shape[1] - 1
    lanes_pad = -g_vecs.shape[1] % (8 * SUBLANES * LANES)
    g_vecs = jnp.pad(g_vecs, ((0, 0), (0, lanes_pad))).reshape(-1, LANES)
    ar_srcs = [g_vecs, g_a, g_bc]
    ar_rows = [a.shape[-2] // 8 for a in ar_srcs]
    ar_lands = [jax.ShapeDtypeStruct((8,) + a.shape[:-2] + (r, a.shape[-1]), F32) for a, r in zip(ar_srcs, ar_rows)]
    ar_ss, ar_rs, ar_srcs, ar_got, ar_token = _split_start(
        "allreduce_scatter_start", ar_srcs, ar_lands, _ar_scatter_copies(ar_rows), 7 * len(ar_srcs), [])

    pss, rcv = _chip_scatter_wait(send_sems, recv_sems, pss, lands, ar_token)
    halves = [_chip_sum(p, r, me_arr, "grad_chip_sum_" + n) for n, p, r in zip(sharded, pss, rcv)]
    others = _pair_swap(halves)
    ar_srcs, ar_got = _split_wait("allreduce_scatter_wait", ar_ss, ar_rs, ar_srcs, ar_got, _ar_scatter_copies(ar_rows),
                                  [others[0]])
    ar_red = _ar_sum(ar_srcs, ar_got, ar_rows)
    ag_ss, ag_rs, ar_red, ag_full, ag_token = _split_start(
        "allreduce_gather_start", ar_red, [jax.ShapeDtypeStruct(a.shape, F32) for a in ar_srcs],
        _ar_gather_copies(ar_rows), 7 * len(ar_red), [])
    G_out, D_out, M_out, V_out = {}, {}, {}, {}
    for n, g_own, g_other in zip(sharded, halves, others):
        g_, d_, m_, v_ = _adamw_sharded(W[n][0], g_own, g_other, M[n][0], V[n][0], c_arr, ag_token, "adamw_" + n)
        G_out[n], D_out[n], M_out[n], V_out[n] = g_[None], d_[None], m_[None], v_[None]
    ar_red, ag_full = _split_wait("allreduce_gather_wait", ag_ss, ag_rs, ar_red, ag_full, _ar_gather_copies(ar_rows),
                                  [D_out[n] for n in sharded])
    me8 = 2 * chip + lax.axis_index("c")
    r_vecs, r_a, r_bc = [lax.dynamic_update_slice_in_dim(f, r, me8 * rw, axis=f.ndim - 2)
                         for f, r, rw in zip(ag_full, ar_red, ar_rows)]
    r_vecs = r_vecs.reshape(1, -1)
    loss = r_vecs[0, loss_at]
    outs4 = _adamw_small(r_vecs, r_a, r_bc, [Wv[n] for n in small],
                         [view[n](M[n]) for n in small], [view[n](V[n]) for n in small])
    for store, o in zip((G_out, D_out, M_out, V_out), outs4):
        store.update({n: back[n](t) for n, t in zip(small, o)})

    return (loss, grad_x[None], *[G_out[n] for n in names], *[D_out[n] for n in names],
            *[M_out[n] for n in names], *[V_out[n] for n in names])
```

```python
import functools
import math

import jax
import jax.numpy as jnp
from jax import lax
from jax.experimental import pallas as pl
from jax.experimental.pallas import tpu as pltpu

F32 = jnp.float32
BF16 = jnp.bfloat16
HI = lax.Precision.HIGHEST
MESH = pl.DeviceIdType.MESH

EPS = 1e-6
S5_GROUP = 16
S5_STATE = 64
GLA_HK = 128
GLA_HV = 256
GLA_RANK = 16
GLA_TAU = 16.0
GLA_CHUNK = 64
GLA_STEP_CHUNKS = 8
LANES = 128
SUBLANES = 8
S5_COLS = 128
S5_LANES = (S5_COLS // S5_GROUP) * S5_STATE
S5_TIME_BLOCK = 1024
ROW_TILE = 512

ADAM_LR = 0.001
ADAM_B1 = 0.9
ADAM_B2 = 0.999
ADAM_EPS = 1e-08
ADAM_WD = 0.01
ADAM_STEP = 10

GELU_K = math.sqrt(2.0 / math.pi)
GELU_C = 0.044715


def _blk(n, pref, unit=LANES):
    best = None
    b = unit
    while b <= min(n, pref):
        if n % b == 0:
            best = b
        b += unit
    return best if best is not None else n


def _dot(a, b, dn=(((1,), (0,)), ((), ()))):
    return lax.dot_general(a.astype(BF16), b.astype(BF16), dn, preferred_element_type=F32)


def _dot_hi(a, b, dn=(((1,), (0,)), ((), ()))):
    return lax.dot_general(a, b, dn, precision=HI, preferred_element_type=F32)


NN = (((1,), (0,)), ((), ()))
NT = (((1,), (1,)), ((), ()))
TN = (((0,), (0,)), ((), ()))


def _sigmoid(x):
    return 1.0 / (1.0 + jnp.exp(-x))


def _gelu(y):
    return 0.5 * y * (1.0 + jnp.tanh(GELU_K * (y + GELU_C * y * y * y)))


def _gelu_grad(y):
    th = jnp.tanh(GELU_K * (y + GELU_C * y * y * y))
    return 0.5 * (1.0 + th) + 0.5 * y * (1.0 - th * th) * GELU_K * (1.0 + 3.0 * GELU_C * y * y)


def _mm(a, b, *, name, ta=False, tb=False, out_dtype=F32, bm=1024, bn=1024, bk=2048, after=()):
    if ta:
        K, M = a.shape
    else:
        M, K = a.shape
    if tb:
        N, K2 = b.shape
    else:
        K2, N = b.shape
    assert K == K2, (a.shape, b.shape, ta, tb)
    bm, bn, bk = _blk(M, bm), _blk(N, bn), _blk(K, bk)
    nk = K // bk
    dn = (((0 if ta else 1,), (1 if tb else 0,)), ((), ()))

    def body(a_ref, b_ref, *rest):
        o_ref = rest[len(after)]
        if nk == 1:
            o_ref[...] = _dot(a_ref[...], b_ref[...], dn).astype(out_dtype)
            return
        acc_ref = rest[len(after) + 1]
        k = pl.program_id(2)

        @pl.when(k == 0)
        def _():
            acc_ref[...] = jnp.zeros_like(acc_ref)

        acc_ref[...] += _dot(a_ref[...], b_ref[...], dn)

        @pl.when(k == nk - 1)
        def _():
            o_ref[...] = acc_ref[...].astype(out_dtype)

    a_spec = pl.BlockSpec((bk, bm), lambda i, j, k: (k, i)) if ta else pl.BlockSpec((bm, bk), lambda i, j, k: (i, k))
    b_spec = pl.BlockSpec((bn, bk), lambda i, j, k: (j, k)) if tb else pl.BlockSpec((bk, bn), lambda i, j, k: (k, j))
    return pl.pallas_call(
        body,
        name=name,
        grid=(M // bm, N // bn, nk),
        in_specs=[a_spec, b_spec] + [pl.BlockSpec(memory_space=pl.ANY)] * len(after),
        out_specs=pl.BlockSpec((bm, bn), lambda i, j, k: (i, j)),
        out_shape=jax.ShapeDtypeStruct((M, N), out_dtype),
        scratch_shapes=[pltpu.VMEM((bm, bn), F32)] if nk > 1 else [],
        compiler_params=pltpu.CompilerParams(dimension_semantics=("parallel", "parallel", "arbitrary")),
    )(a, b, *after)


def _in_proj(h, w_main, w_low, after):
    M, K = h.shape
    N = w_main.shape[1]
    bm, bn = _blk(M, 1024), _blk(N, 1024)

    def body(h_ref, w_ref, wl_ref, _after_ref, o_ref, ol_ref):
        hv = h_ref[...]
        o_ref[...] = _dot(hv, w_ref[...])

        @pl.when(pl.program_id(1) == 0)
        def _():
            ol_ref[...] = _dot(hv, wl_ref[...])

    return pl.pallas_call(
        body, name="in_proj", grid=(M // bm, N // bn),
        in_specs=[pl.BlockSpec((bm, K), lambda i, j: (i, 0)), pl.BlockSpec((K, bn), lambda i, j: (0, j)),
                  pl.BlockSpec((K, LANES), lambda i, j: (0, 0)), pl.BlockSpec(memory_space=pl.ANY)],
        out_specs=[pl.BlockSpec((bm, bn), lambda i, j: (i, j)), pl.BlockSpec((bm, LANES), lambda i, j: (i, 0))],
        out_shape=[jax.ShapeDtypeStruct((M, N), F32), jax.ShapeDtypeStruct((M, LANES), F32)],
        compiler_params=pltpu.CompilerParams(dimension_semantics=("parallel", "arbitrary")),
    )(h, w_main, w_low, after)


def _in_proj_dx(a1, a2, al, b, bl, after, *, bm=512, bn=1024):
    M, K1 = a1.shape
    K2 = a2.shape[1]
    N = b.shape[0]
    bm, bn = _blk(M, bm, 2 * SUBLANES), _blk(N, bn)

    def body(a1_ref, a2_ref, al_ref, b_ref, bl_ref, _after_ref, o_ref):
        o_ref[...] = (_dot(a1_ref[...], b_ref[:, :K1], NT) + _dot(a2_ref[...], b_ref[:, K1:], NT)
                      + _dot(al_ref[...], bl_ref[...], NT))

    return pl.pallas_call(
        body, name="in_proj_dx", grid=(N // bn, M // bm),
        in_specs=[pl.BlockSpec((bm, K1), lambda j, i: (i, 0)),
                  pl.BlockSpec((bm, K2), lambda j, i: (i, 0)),
                  pl.BlockSpec((bm, LANES), lambda j, i: (i, 0)),
                  pl.BlockSpec((bn, K1 + K2), lambda j, i: (j, 0)),
                  pl.BlockSpec((bn, LANES), lambda j, i: (j, 0)),
                  pl.BlockSpec(memory_space=pl.ANY)],
        out_specs=pl.BlockSpec((bm, bn), lambda j, i: (i, j)),
        out_shape=jax.ShapeDtypeStruct((M, N), F32),
        compiler_params=pltpu.CompilerParams(dimension_semantics=("parallel", "parallel")),
    )(a1, a2, al, b, bl, after)


def _in_proj_dw(a, b1, b2, bl, *, bm=1024, bn=1024, bk=2048):
    K, M = a.shape
    N1, N2 = b1.shape[1], b2.shape[1]
    bm, bk = _blk(M, bm), _blk(K, bk)
    bn = _blk(math.gcd(N1, N2), bn)
    nj1, nj = N1 // bn, (N1 + N2) // bn
    nk = K // bk

    def body(a_ref, b1_ref, b2_ref, bl_ref, o_ref, ol_ref, acc_ref, accl_ref):
        j = pl.program_id(1)
        k = pl.program_id(2)

        @pl.when(k == 0)
        def _():
            acc_ref[...] = jnp.zeros_like(acc_ref)

        @pl.when(j < nj1)
        def _():
            acc_ref[...] += _dot(a_ref[...], b1_ref[...], TN)

        @pl.when(j >= nj1)
        def _():
            acc_ref[...] += _dot(a_ref[...], b2_ref[...], TN)

        @pl.when(k == nk - 1)
        def _():
            o_ref[...] = acc_ref[...].astype(BF16)

        @pl.when(j == 0)
        def _():
            low = _dot(a_ref[...], bl_ref[...], TN)

            @pl.when(k == 0)
            def _():
                accl_ref[...] = low

            @pl.when(k > 0)
            def _():
                accl_ref[...] += low

            @pl.when(k == nk - 1)
            def _():
                ol_ref[...] = accl_ref[...].astype(BF16)

    return pl.pallas_call(
        body, name="in_proj_dw", grid=(M // bm, nj, nk),
        in_specs=[pl.BlockSpec((bk, bm), lambda i, j, k: (k, i)),
                  pl.BlockSpec((bk, bn), lambda i, j, k: (jnp.where(j < nj1, k, nk - 1), jnp.minimum(j, nj1 - 1))),
                  pl.BlockSpec((bk, bn), lambda i, j, k: (jnp.where(j >= nj1, k, 0), jnp.maximum(j - nj1, 0))),
                  pl.BlockSpec((bk, LANES), lambda i, j, k: (jnp.where(j == 0, k, nk - 1), 0))],
        out_specs=[pl.BlockSpec((bm, bn), lambda i, j, k: (i, j)), pl.BlockSpec((bm, LANES), lambda i, j, k: (i, 0))],
        out_shape=[jax.ShapeDtypeStruct((M, N1 + N2), BF16), jax.ShapeDtypeStruct((M, LANES), BF16)],
        scratch_shapes=[pltpu.VMEM((bm, bn), F32), pltpu.VMEM((bm, LANES), F32)],
        compiler_params=pltpu.CompilerParams(dimension_semantics=("parallel", "arbitrary", "arbitrary")),
    )(a, b1, b2, bl)


def _assemble_w_in(g, nsh, wm):
    _, R, nshp = g.shape
    nb_in = nshp // LANES
    nb_main = wm // LANES
    tr = _blk(R, 512, 2 * SUBLANES)
    plan = []
    for b in range(nb_main + 1):
        terms = []
        for k in range(g.shape[0]):
            for i in range(nb_in):
                delta = nsh * k + LANES * i - LANES * b
                lo, hi = max(0, -delta), min(LANES, LANES - delta, nsh - LANES * i)
                if abs(delta) < LANES and hi > lo:
                    terms.append((k, i, delta))
        plan.append(terms)
    deltas = sorted({d for terms in plan for _, _, d in terms if d})

    def body(g_ref, wm_ref, wl_ref):
        src = _iota2((LANES, LANES), 0)
        dst = _iota2((LANES, LANES), 1)
        shift = {d: (dst - src == d).astype(BF16) for d in deltas}
        for b, terms in enumerate(plan):
            acc = None
            for k, i, d in terms:
                blk = g_ref[k, :, LANES * i:LANES * (i + 1)]
                t = _dot(blk, shift[d]) if d else blk.astype(F32)
                acc = t if acc is None else acc + t
            if b < nb_main:
                wm_ref[:, LANES * b:LANES * (b + 1)] = acc.astype(BF16)
            else:
                wl_ref[...] = acc.astype(BF16)

    return pl.pallas_call(
        body, name="assemble_w_in", grid=(R // tr,),
        in_specs=[pl.BlockSpec((g.shape[0], tr, nshp), lambda r: (0, r, 0))],
        out_specs=[pl.BlockSpec((tr, wm), lambda r: (r, 0)), pl.BlockSpec((tr, LANES), lambda r: (r, 0))],
        out_shape=[jax.ShapeDtypeStruct((R, wm), BF16), jax.ShapeDtypeStruct((R, LANES), BF16)],
        compiler_params=pltpu.CompilerParams(dimension_semantics=("parallel",)),
    )(g)


def _split_w_in_grad(g_main, g_low, nsh):
    R, wm = g_main.shape
    nb_main = wm // LANES
    nb_out = -(-nsh // LANES)
    tr = _blk(R, 512, 2 * SUBLANES)
    plan = {}
    for k in range(4):
        for i in range(nb_out):
            width = min(LANES, nsh - LANES * i)
            terms = []
            for b in range(nb_main + 1):
                delta = LANES * b - (nsh * k + LANES * i)
                lo, hi = max(0, delta), min(width, LANES + delta)
                if abs(delta) < LANES and hi > lo:
                    terms.append((b, delta))
            plan[k, i] = (width, terms)
    deltas = sorted({d for _, terms in plan.values() for _, d in terms if d})

    def body(gm_ref, gl_ref, o_ref):
        src = _iota2((LANES, LANES), 0)
        dst = _iota2((LANES, LANES), 1)
        shift = {d: (dst - src == d).astype(BF16) for d in deltas}
        for (k, i), (width, terms) in plan.items():
            acc = None
            for b, d in terms:
                blk = gm_ref[:, LANES * b:LANES * (b + 1)] if b < nb_main else gl_ref[...]
                t = _dot(blk, shift[d]) if d else blk.astype(F32)
                acc = t if acc is None else acc + t
            o_ref[k, :, LANES * i:LANES * i + width] = acc[:, :width].astype(BF16)

    return pl.pallas_call(
        body, name="split_w_in_grad", grid=(R // tr,),
        in_specs=[pl.BlockSpec((tr, wm), lambda r: (r, 0)), pl.BlockSpec((tr, LANES), lambda r: (r, 0))],
        out_specs=pl.BlockSpec((4, tr, nsh), lambda r: (0, r, 0)),
        out_shape=jax.ShapeDtypeStruct((4, R, nsh), BF16),
        compiler_params=pltpu.CompilerParams(dimension_semantics=("parallel",)),
    )(g_main, g_low)


def _prenorm_fwd(x, w, after):
    L, D = x.shape
    tr = _blk(L, ROW_TILE, SUBLANES)

    def body(x_ref, w_ref, _after_ref, h_ref):
        xv = x_ref[...]
        r = lax.rsqrt(jnp.mean(xv * xv, axis=-1, keepdims=True) + EPS)
        h_ref[...] = (xv * r * w_ref[...]).astype(BF16)

    return pl.pallas_call(
        body, name="prenorm_fwd", grid=(L // tr,),
        in_specs=[pl.BlockSpec((tr, D), lambda i: (i, 0)), pl.BlockSpec((1, D), lambda i: (0, 0)),
                  pl.BlockSpec(memory_space=pl.ANY)],
        out_specs=pl.BlockSpec((tr, D), lambda i: (i, 0)),
        out_shape=jax.ShapeDtypeStruct((L, D), BF16),
        compiler_params=pltpu.CompilerParams(dimension_semantics=("parallel",)),
    )(x, w, after)


def _post_fwd_bwd(mixed, x, target, w):
    L, D = x.shape
    tr = _blk(L, ROW_TILE, SUBLANES)
    nsteps = L // tr

    def body(mx_ref, x_ref, t_ref, w_ref, loss_ref, dm_ref, dout_ref, gw_ref, acc_ref):
        i = pl.program_id(0)

        @pl.when(i == 0)
        def _():
            acc_ref[...] = jnp.zeros_like(acc_ref)
            gw_ref[...] = jnp.zeros_like(gw_ref)

        mx = mx_ref[...]
        wv = w_ref[...]
        r = lax.rsqrt(jnp.mean(mx * mx, axis=-1, keepdims=True) + EPS)
        n = mx * r
        err = x_ref[...] + n * wv - t_ref[...]
        acc_ref[...] += jnp.sum(err * err, axis=0, keepdims=True)
        dout = err * (1.0 / D)
        dout_ref[...] = dout
        gw_ref[...] += jnp.sum(dout * n, axis=0, keepdims=True)
        dn = dout * wv
        dm_ref[...] = (r * (dn - n * jnp.mean(dn * n, axis=-1, keepdims=True))).astype(BF16)

        @pl.when(i == nsteps - 1)
        def _():
            loss_ref[...] = jnp.sum(acc_ref[...], axis=-1, keepdims=True) * (0.5 / D)

    row = pl.BlockSpec((tr, D), lambda i: (i, 0))
    vec = pl.BlockSpec((1, D), lambda i: (0, 0))
    return pl.pallas_call(
        body, name="post_fwd_bwd", grid=(nsteps,),
        in_specs=[row, row, row, vec],
        out_specs=[pl.BlockSpec((1, 1), lambda i: (0, 0)), row, row, vec],
        out_shape=[jax.ShapeDtypeStruct((1, 1), F32), jax.ShapeDtypeStruct((L, D), BF16),
                   jax.ShapeDtypeStruct((L, D), F32), jax.ShapeDtypeStruct((1, D), F32)],
        scratch_shapes=[pltpu.VMEM((1, D), F32)],
        compiler_params=pltpu.CompilerParams(dimension_semantics=("arbitrary",)),
    )(mixed, x, target, w)


def _prenorm_bwd(x, dh, dout, w):
    L, D = x.shape
    tr = _blk(L, ROW_TILE, SUBLANES)

    def body(x_ref, a_ref, dout_ref, w_ref, gx_ref, gw_ref):
        i = pl.program_id(0)

        @pl.when(i == 0)
        def _():
            gw_ref[...] = jnp.zeros_like(gw_ref)

        xv = x_ref[...]
        r = lax.rsqrt(jnp.mean(xv * xv, axis=-1, keepdims=True) + EPS)
        n = xv * r
        dh = a_ref[...]
        gw_ref[...] += jnp.sum(dh * n, axis=0, keepdims=True)
        dn = dh * w_ref[...]
        gx_ref[...] = dout_ref[...] + r * (dn - n * jnp.mean(dn * n, axis=-1, keepdims=True))

    row = pl.BlockSpec((tr, D), lambda i: (i, 0))
    vec = pl.BlockSpec((1, D), lambda i: (0, 0))
    return pl.pallas_call(
        body, name="prenorm_bwd", grid=(L // tr,),
        in_specs=[row, row, row, vec],
        out_specs=[row, vec],
        out_shape=[jax.ShapeDtypeStruct((L, D), F32), jax.ShapeDtypeStruct((1, D), F32)],
        compiler_params=pltpu.CompilerParams(dimension_semantics=("arbitrary",)),
    )(x, dh, dout, w)


def _s5_disc(a_re_raw, a_im, dt):
    a_re = jnp.minimum(a_re_raw, -1e-4)
    mag = jnp.exp(a_re * dt)
    ph = a_im * dt
    ab_re = mag * jnp.cos(ph)
    ab_im = mag * jnp.sin(ph)
    inv_n = 1.0 / (a_re * a_re + a_im * a_im)
    ia_re = a_re * inv_n
    ia_im = -a_im * inv_n
    n_re = ab_re - 1.0
    f_re = n_re * ia_re - ab_im * ia_im
    f_im = n_re * ia_im + ab_im * ia_re
    return a_re, ab_re, ab_im, f_re, f_im, ia_re, ia_im


def _iota2(shape, dim):
    return lax.broadcasted_iota(jnp.int32, shape, dim)


def _group_mask(rows, rows_per_group):
    shift = rows_per_group.bit_length() - 1
    return (_iota2((rows, S5_LANES), 0) >> shift) == (_iota2((rows, S5_LANES), 1) >> (S5_STATE.bit_length() - 1))


def _lane_tiler(dtype):
    return ((_iota2((S5_STATE, S5_LANES), 1) & (S5_STATE - 1)) == _iota2((S5_STATE, S5_LANES), 0)).astype(dtype)


def _row_to_col(row, n):
    eye = (_iota2((n, n), 0) == _iota2((n, n), 1)).astype(F32)
    return jnp.sum(eye * row, axis=1, keepdims=True)


def _group_repeat(G):
    return ((_iota2((G * S5_GROUP, G), 0) >> (S5_GROUP.bit_length() - 1)) == _iota2((G * S5_GROUP, G), 1)).astype(F32)


S5_TABS = 18


def _s5_prep_fwd(a_re, a_im, log_dt, b_re, b_im, c_re, c_im, after, seg):
    G, P = a_re.shape
    nb = G * S5_GROUP // S5_COLS
    g8 = S5_COLS // S5_GROUP
    assert seg & (seg - 1) == 0, seg

    def body(are_ref, aim_ref, ldt_ref, bre_ref, bim_ref, cre_ref, cim_ref, _after_ref,
             bbre_ref, bbim_ref, ctre_ref, ctim_ref, tab_ref, pt_ref):
        dt = jnp.exp(_row_to_col(ldt_ref[...], G))
        _, ab_re, ab_im, f_re, f_im, _, _ = _s5_disc(are_ref[...], aim_ref[...], dt)
        rep = _group_repeat(G)
        fx_re = _dot_hi(rep, f_re)
        fx_im = _dot_hi(rep, f_im)
        br, bi = bre_ref[...], bim_ref[...]
        bb_re = fx_re * br - fx_im * bi
        bb_im = fx_re * bi + fx_im * br
        tile_bf = _lane_tiler(BF16)
        mask = _group_mask(S5_COLS, S5_GROUP)
        for jb in range(nb):
            rs = slice(jb * S5_COLS, (jb + 1) * S5_COLS)
            for src, dst in ((bb_re[rs], bbre_ref), (bb_im[rs], bbim_ref), (cre_ref[rs, :], ctre_ref), (cim_ref[rs, :], ctim_ref)):
                dst[jb] = jnp.where(mask, _dot(src, tile_bf), 0.0).astype(BF16)

        tile_f = _lane_tiler(F32)
        mask8 = _group_mask(g8, 1)
        row = _iota2((SUBLANES, S5_LANES), 0)
        slab = (SUBLANES, S5_LANES)
        cmul = lambda p, q: (p[0] * q[0] - p[1] * q[1], p[0] * q[1] + p[1] * q[0])
        for jb in range(nb):
            gs = slice(jb * g8, (jb + 1) * g8)

            def lanes(m):
                v = jnp.sum(jnp.where(mask8, _dot_hi(m[gs], tile_f), 0.0), axis=0, keepdims=True)
                return jnp.broadcast_to(v, slab)

            a1 = (lanes(ab_re), lanes(ab_im))
            tab_ref[jb, 0], tab_ref[jb, 1] = a1

            def powers(k, p):
                s_re = s_im = jnp.zeros(slab, F32)
                for r in range(SUBLANES):
                    s_re = jnp.where(row == r, p[0], s_re)
                    s_im = jnp.where(row == r, p[1], s_im)
                    p = cmul(p, a1)
                pt_ref[jb, 0, _slab(k), :] = s_re
                pt_ref[jb, 1, _slab(k), :] = s_im
                return p

            lax.fori_loop(0, seg // SUBLANES, powers, a1)
            aseg = a1
            for _ in range(seg.bit_length() - 1):
                aseg = cmul(aseg, aseg)
            pw = [aseg]
            for _ in range(1, SUBLANES):
                pw.append(cmul(pw[-1], aseg))
            for lvl, k in enumerate((1, 2, 4)):
                tab_ref[jb, 2 + 2 * lvl] = jnp.where(row >= k, pw[k - 1][0], 0.0)
                tab_ref[jb, 3 + 2 * lvl] = jnp.where(row >= k, pw[k - 1][1], 0.0)
                tab_ref[jb, 10 + 2 * lvl] = jnp.where(row < SUBLANES - k, pw[k - 1][0], 0.0)
                tab_ref[jb, 11 + 2 * lvl] = jnp.where(row < SUBLANES - k, -pw[k - 1][1], 0.0)
            f_r = f_i = r_r = r_i = jnp.zeros(slab, F32)
            for i in range(SUBLANES):
                f_r = jnp.where(row == i, pw[i][0], f_r)
                f_i = jnp.where(row == i, pw[i][1], f_i)
                r_r = jnp.where(row == i, pw[SUBLANES - 1 - i][0], r_r)
                r_i = jnp.where(row == i, -pw[SUBLANES - 1 - i][1], r_i)
            tab_ref[jb, 8] = f_r
            tab_ref[jb, 9] = f_i
            tab_ref[jb, 16] = r_r
            tab_ref[jb, 17] = r_i

    vm = pl.BlockSpec(memory_space=pltpu.VMEM)
    bd = jax.ShapeDtypeStruct((nb, S5_COLS, S5_LANES), BF16)
    return pl.pallas_call(
        body, name="s5_prep_fwd",
        in_specs=[vm] * 7 + [pl.BlockSpec(memory_space=pl.ANY)], out_specs=[vm] * 6,
        out_shape=[bd, bd, bd, bd, jax.ShapeDtypeStruct((nb, S5_TABS, SUBLANES, S5_LANES), F32),
                   jax.ShapeDtypeStruct((nb, 2, seg, S5_LANES), F32)],
    )(a_re, a_im, log_dt, b_re, b_im, c_re, c_im, after)


def _s5_prep_bwd(a_re, a_im, log_dt, b_re, b_im, gbb_re, gbb_im, gct_re, gct_im, gab_re, gab_im):
    G, P = a_re.shape
    nb = G * S5_GROUP // S5_COLS
    g8 = S5_COLS // S5_GROUP

    def body(are_ref, aim_ref, ldt_ref, bre_ref, bim_ref, gbr_ref, gbi_ref, gcr_ref, gci_ref, gar_ref, gai_ref,
             o_a, o_bc, o_ldt):
        dt = jnp.exp(_row_to_col(ldt_ref[...], G))
        a_raw = are_ref[...]
        a_imv = aim_ref[...]
        a_re_c, ab_re, ab_im, f_re, f_im, ia_re, ia_im = _s5_disc(a_raw, a_imv, dt)
        tile_f = _lane_tiler(F32)
        mask = _group_mask(S5_COLS, S5_GROUP)
        mask8 = _group_mask(g8, 1)
        for jb in range(nb):
            rs = slice(jb * S5_COLS, (jb + 1) * S5_COLS)
            gs = slice(jb * g8, (jb + 1) * g8)
            ls = slice(jb * S5_LANES, (jb + 1) * S5_LANES)
            for k, src in enumerate((gbr_ref, gbi_ref, gcr_ref, gci_ref)):
                o_bc[k, rs, :] = _dot_hi(jnp.where(mask, src[jb], 0.0), tile_f, NT)
            for k, src in enumerate((gar_ref, gai_ref)):
                o_a[k, gs, :] = _dot_hi(jnp.where(mask8, src[:, ls], 0.0), tile_f, NT)
        rep = _group_repeat(G)
        fx_re = _dot_hi(rep, f_re)
        fx_im = _dot_hi(rep, f_im)
        gbr, gbi = o_bc[0], o_bc[1]
        br, bi = bre_ref[...], bim_ref[...]
        o_bc[0] = fx_re * gbr + fx_im * gbi
        o_bc[1] = fx_re * gbi - fx_im * gbr
        gf_re = _dot_hi(rep, br * gbr + bi * gbi, TN)
        gf_im = _dot_hi(rep, br * gbi - bi * gbr, TN)
        gab_r = o_a[0] + ia_re * gf_re + ia_im * gf_im
        gab_i = o_a[1] + ia_re * gf_im - ia_im * gf_re
        q_re = f_re * ia_re - f_im * ia_im
        q_im = f_re * ia_im + f_im * ia_re
        ga_re = -(q_re * gf_re + q_im * gf_im)
        ga_im = -(q_re * gf_im - q_im * gf_re)
        gth_re = ab_re * gab_r + ab_im * gab_i
        gth_im = ab_re * gab_i - ab_im * gab_r
        ga_re = ga_re + dt * gth_re
        ga_im = ga_im + dt * gth_im
        gdt = jnp.sum(a_re_c * gth_re + a_imv * gth_im, axis=-1, keepdims=True)
        eye = (_iota2((G, G), 0) == _iota2((G, G), 1)).astype(F32)
        o_ldt[...] = jnp.sum(eye * (gdt * dt), axis=0, keepdims=True)
        slope = jnp.where(a_raw < -1e-4, 1.0, jnp.where(a_raw == -1e-4, 0.5, 0.0))
        o_a[0] = ga_re * slope
        o_a[1] = ga_im

    vm = pl.BlockSpec(memory_space=pltpu.VMEM)
    return pl.pallas_call(
        body, name="s5_prep_bwd",
        in_specs=[vm] * 11, out_specs=[vm] * 3,
        out_shape=[jax.ShapeDtypeStruct((2, G, P), F32), jax.ShapeDtypeStruct((4, G * S5_GROUP, P), F32),
                   jax.ShapeDtypeStruct((1, G), F32)],
    )(a_re, a_im, log_dt, b_re, b_im, gbb_re, gbb_im, gct_re, gct_im, gab_re, gab_im)


def _scan8(xr, xi, tab_ref, base, shifts):
    for lvl, sh in enumerate(shifts):
        mr = tab_ref[0, base + 2 * lvl]
        mi = tab_ref[0, base + 2 * lvl + 1]
        ar = pltpu.roll(xr, sh, 0)
        ai = pltpu.roll(xi, sh, 0)
        xr, xi = xr + mr * ar - mi * ai, xi + mr * ai + mi * ar
    return xr, xi


def _to_segments(src_ref, dst_ref, seg):
    for i in range(seg):
        dst_ref[i * SUBLANES:(i + 1) * SUBLANES, :] = src_ref[pl.ds(i, SUBLANES, stride=seg), :]


def _from_segments(src_ref, dst_ref, seg):
    for i in range(seg):
        dst_ref[pl.ds(i, SUBLANES, stride=seg), :] = src_ref[i * SUBLANES:(i + 1) * SUBLANES, :]


def _slab(i):
    return pl.ds(pl.multiple_of(i * SUBLANES, SUBLANES), SUBLANES)


def _s5_scan_fwd(proj_main, bbd_re, bbd_im, cbd_re, cbd_im, dvec, tab, ptab, DS):
    L = proj_main.shape[0]
    nb = DS // S5_COLS
    tb = _blk(L, S5_TIME_BLOCK, SUBLANES)
    nt = L // tb
    seg = tb // SUBLANES

    def body(u_ref, bre_ref, bim_ref, cre_ref, cim_ref, d_ref, tab_ref, pt_ref, y_ref, sre_ref, sim_ref,
             up_ref, yp_ref, car_ref):
        t = pl.program_id(1)

        @pl.when(t == 0)
        def _():
            car_ref[...] = jnp.zeros_like(car_ref)

        _to_segments(u_ref, up_ref, seg)
        up = up_ref[...]
        sre_ref[...] = _dot(up, bre_ref[0])
        sim_ref[...] = _dot(up, bim_ref[0])
        ar, ai = tab_ref[0, 0], tab_ref[0, 1]

        def pass1(i, x):
            xr = ar * x[0] - ai * x[1] + sre_ref[_slab(i), :]
            xi = ar * x[1] + ai * x[0] + sim_ref[_slab(i), :]
            sre_ref[_slab(i), :] = xr
            sim_ref[_slab(i), :] = xi
            return xr, xi

        zero = jnp.zeros((SUBLANES, S5_LANES), F32)
        er, ei = lax.fori_loop(0, seg, pass1, (zero, zero))
        cin_r, cin_i = car_ref[0], car_ref[1]
        sr, si = _scan8(er, ei, tab_ref, 2, (1, 2, 4))
        pr, pi = tab_ref[0, 8], tab_ref[0, 9]
        sr, si = sr + pr * cin_r - pi * cin_i, si + pr * cin_i + pi * cin_r
        row0 = _iota2((SUBLANES, S5_LANES), 0) == 0
        cr = jnp.where(row0, cin_r, pltpu.roll(sr, 1, 0))
        ci = jnp.where(row0, cin_i, pltpu.roll(si, 1, 0))
        car_ref[0] = jnp.broadcast_to(sr[SUBLANES - 1:SUBLANES, :], sr.shape)
        car_ref[1] = jnp.broadcast_to(si[SUBLANES - 1:SUBLANES, :], si.shape)

        def pass2(i, _):
            qr, qi = pt_ref[0, 0, pl.ds(i, 1), :], pt_ref[0, 1, pl.ds(i, 1), :]
            sre_ref[_slab(i), :] += qr * cr - qi * ci
            sim_ref[_slab(i), :] += qr * ci + qi * cr
            return 0

        lax.fori_loop(0, seg, pass2, 0, unroll=4)
        yp_ref[...] = _dot(sre_ref[...], cre_ref[0], NT) - _dot(sim_ref[...], cim_ref[0], NT) + d_ref[...] * up
        _from_segments(yp_ref, y_ref, seg)

    return pl.pallas_call(
        body, name="s5_scan_fwd", grid=(nb, nt),
        in_specs=[
            pl.BlockSpec((tb, S5_COLS), lambda j, t: (t, j)),
            pl.BlockSpec((1, S5_COLS, S5_LANES), lambda j, t: (j, 0, 0)),
            pl.BlockSpec((1, S5_COLS, S5_LANES), lambda j, t: (j, 0, 0)),
            pl.BlockSpec((1, S5_COLS, S5_LANES), lambda j, t: (j, 0, 0)),
            pl.BlockSpec((1, S5_COLS, S5_LANES), lambda j, t: (j, 0, 0)),
            pl.BlockSpec((1, S5_COLS), lambda j, t: (0, j)),
            pl.BlockSpec((1, S5_TABS, SUBLANES, S5_LANES), lambda j, t: (j, 0, 0, 0)),
            pl.BlockSpec((1, 2, seg, S5_LANES), lambda j, t: (j, 0, 0, 0)),
        ],
        out_specs=[
            pl.BlockSpec((tb, S5_COLS), lambda j, t: (t, j)),
            pl.BlockSpec((tb, S5_LANES), lambda j, t: (t, j)),
            pl.BlockSpec((tb, S5_LANES), lambda j, t: (t, j)),
        ],
        out_shape=[jax.ShapeDtypeStruct((L, DS), F32),
                   jax.ShapeDtypeStruct((L, nb * S5_LANES), F32),
                   jax.ShapeDtypeStruct((L, nb * S5_LANES), F32)],
        scratch_shapes=[pltpu.VMEM((tb, S5_COLS), F32), pltpu.VMEM((tb, S5_COLS), F32),
                        pltpu.VMEM((2, SUBLANES, S5_LANES), F32)],
        compiler_params=pltpu.CompilerParams(dimension_semantics=("parallel", "arbitrary")),
    )(proj_main, bbd_re, bbd_im, cbd_re, cbd_im, dvec, tab, ptab)


def _s5_scan_bwd(dy, proj_main, s_re, s_im, bbd_re, bbd_im, cbd_re, cbd_im, dvec, tab, ptab, d_s5, DS):
    L = proj_main.shape[0]
    nb = DS // S5_COLS
    tb = _blk(L, S5_TIME_BLOCK, SUBLANES)
    nt = L // tb
    seg = tb // SUBLANES
    tb8 = tb // SUBLANES

    def body(dy_ref, u_ref, sre_ref, sim_ref, pre_ref, pim_ref, bre_ref, bim_ref, cre_ref, cim_ref, d_ref, tab_ref, pt_ref,
             _ds5_ref, du_ref, gd_ref, gcre_ref, gcim_ref, gbre_ref, gbim_ref, gare_ref, gaim_ref,
             lre_ref, lim_ref, up_ref, dyp_ref, dup_ref, duo_ref, car_ref):
        t = pl.program_id(1)

        @pl.when(t == 0)
        def _():
            car_ref[...] = jnp.zeros_like(car_ref)
            gd_ref[...] = jnp.zeros_like(gd_ref)
            gcre_ref[...] = jnp.zeros_like(gcre_ref)
            gcim_ref[...] = jnp.zeros_like(gcim_ref)
            gbre_ref[...] = jnp.zeros_like(gbre_ref)
            gbim_ref[...] = jnp.zeros_like(gbim_ref)
            gare_ref[...] = jnp.zeros_like(gare_ref)
            gaim_ref[...] = jnp.zeros_like(gaim_ref)

        _to_segments(dy_ref, dyp_ref, seg)
        _to_segments(u_ref, up_ref, seg)
        dyv = dyp_ref[...]
        u = up_ref[...]
        gd_ref[...] += jnp.sum(dyv * u, axis=0, keepdims=True)
        lre_ref[...] = _dot(dyv, cre_ref[0])
        lim_ref[...] = -_dot(dyv, cim_ref[0])
        gcre_ref[0] += _dot(dyv, sre_ref[...], TN)
        gcim_ref[0] -= _dot(dyv, sim_ref[...], TN)
        ar, ai = tab_ref[0, 0], -tab_ref[0, 1]

        def pass1(k, x):
            i = seg - 1 - k
            xr = ar * x[0] - ai * x[1] + lre_ref[_slab(i), :]
            xi = ar * x[1] + ai * x[0] + lim_ref[_slab(i), :]
            lre_ref[_slab(i), :] = xr
            lim_ref[_slab(i), :] = xi
            return xr, xi

        zero = jnp.zeros((SUBLANES, S5_LANES), F32)
        er, ei = lax.fori_loop(0, seg, pass1, (zero, zero))
        cin_r, cin_i = car_ref[0], car_ref[1]
        lr, li = _scan8(er, ei, tab_ref, 10, (7, 6, 4))
        pr, pi = tab_ref[0, 16], tab_ref[0, 17]
        lr, li = lr + pr * cin_r - pi * cin_i, li + pr * cin_i + pi * cin_r
        rows = _iota2((SUBLANES, S5_LANES), 0)
        cr = jnp.where(rows == SUBLANES - 1, cin_r, pltpu.roll(lr, SUBLANES - 1, 0))
        ci = jnp.where(rows == SUBLANES - 1, cin_i, pltpu.roll(li, SUBLANES - 1, 0))
        car_ref[0] = jnp.broadcast_to(lr[0:1, :], lr.shape)
        car_ref[1] = jnp.broadcast_to(li[0:1, :], li.shape)

        first = (t == nt - 1).astype(F32)
        head_re = jnp.broadcast_to(pre_ref[SUBLANES - 1:SUBLANES, :], zero.shape) * (1.0 - first)
        head_im = jnp.broadcast_to(pim_ref[SUBLANES - 1:SUBLANES, :], zero.shape) * (1.0 - first)
        last = _slab(seg - 1)
        sp0_re = jnp.where(rows == 0, head_re, pltpu.roll(sre_ref[last, :], 1, 0))
        sp0_im = jnp.where(rows == 0, head_im, pltpu.roll(sim_ref[last, :], 1, 0))

        def fix(i, acc, sp_re, sp_im):
            j = seg - 1 - i
            qr, qi = pt_ref[0, 0, pl.ds(j, 1), :], -pt_ref[0, 1, pl.ds(j, 1), :]
            xr = lre_ref[_slab(i), :] + qr * cr - qi * ci
            xi = lim_ref[_slab(i), :] + qr * ci + qi * cr
            lre_ref[_slab(i), :] = xr
            lim_ref[_slab(i), :] = xi
            return acc[0] + sp_re * xr + sp_im * xi, acc[1] + sp_re * xi - sp_im * xr

        def pass2(i, acc):
            prev = _slab(jnp.maximum(i - 1, 0))
            return fix(i, acc, sre_ref[prev, :], sim_ref[prev, :])

        acc_re, acc_im = lax.fori_loop(0, seg, pass2, (zero, zero), unroll=4)
        first_slab = _slab(0)
        d_re, d_im = sp0_re - sre_ref[first_slab, :], sp0_im - sim_ref[first_slab, :]
        x0r, x0i = lre_ref[first_slab, :], lim_ref[first_slab, :]
        acc_re = acc_re + d_re * x0r + d_im * x0i
        acc_im = acc_im + d_re * x0i - d_im * x0r
        gare_ref[...] += jnp.sum(acc_re, axis=0, keepdims=True)
        gaim_ref[...] += jnp.sum(acc_im, axis=0, keepdims=True)
        lre = lre_ref[...]
        lim = lim_ref[...]
        dup_ref[...] = dyv * d_ref[...] + _dot(lre, bre_ref[0], NT) + _dot(lim, bim_ref[0], NT)
        _from_segments(dup_ref, duo_ref, seg)
        du_ref[...] = duo_ref[...].astype(BF16)
        gbre_ref[0] += _dot(u, lre, TN)
        gbim_ref[0] += _dot(u, lim, TN)

    rt = lambda t: nt - 1 - t
    col = pl.BlockSpec((tb, S5_COLS), lambda j, t: (rt(t), j))
    st = pl.BlockSpec((tb, S5_LANES), lambda j, t: (rt(t), j))
    prev = pl.BlockSpec((SUBLANES, S5_LANES), lambda j, t: (jnp.maximum(rt(t) * tb8 - 1, 0), j))
    bmat = pl.BlockSpec((1, S5_COLS, S5_LANES), lambda j, t: (j, 0, 0))
    cmat = bmat
    return pl.pallas_call(
        body, name="s5_scan_bwd", grid=(nb, nt),
        in_specs=[col, col, st, st, prev, prev, bmat, bmat, cmat, cmat,
                  pl.BlockSpec((1, S5_COLS), lambda j, t: (0, j)),
                  pl.BlockSpec((1, S5_TABS, SUBLANES, S5_LANES), lambda j, t: (j, 0, 0, 0)),
                  pl.BlockSpec((1, 2, seg, S5_LANES), lambda j, t: (j, 0, 0, 0)),
                  pl.BlockSpec(memory_space=pl.ANY)],
        out_specs=[col, pl.BlockSpec((1, S5_COLS), lambda j, t: (0, j)), cmat, cmat, bmat, bmat,
                   pl.BlockSpec((1, S5_LANES), lambda j, t: (0, j)), pl.BlockSpec((1, S5_LANES), lambda j, t: (0, j))],
        input_output_aliases={13: 0},
        out_shape=[jax.ShapeDtypeStruct((L, 2 * DS), BF16), jax.ShapeDtypeStruct((1, DS), F32),
                   jax.ShapeDtypeStruct((nb, S5_COLS, S5_LANES), F32), jax.ShapeDtypeStruct((nb, S5_COLS, S5_LANES), F32),
                   jax.ShapeDtypeStruct((nb, S5_COLS, S5_LANES), F32), jax.ShapeDtypeStruct((nb, S5_COLS, S5_LANES), F32),
                   jax.ShapeDtypeStruct((1, nb * S5_LANES), F32), jax.ShapeDtypeStruct((1, nb * S5_LANES), F32)],
        scratch_shapes=[pltpu.VMEM((tb, S5_LANES), F32), pltpu.VMEM((tb, S5_LANES), F32)]
        + [pltpu.VMEM((tb, S5_COLS), F32)] * 4 + [pltpu.VMEM((2, SUBLANES, S5_LANES), F32)],
        compiler_params=pltpu.CompilerParams(dimension_semantics=("parallel", "arbitrary")),
    )(dy, proj_main, s_re, s_im, s_re, s_im, bbd_re, bbd_im, cbd_re, cbd_im, dvec, tab, ptab, d_s5)


def _s5_post_fwd(y_pre, proj_main, glu_w, glu_b, DS):
    L = y_pre.shape[0]
    tr = _blk(L, ROW_TILE, SUBLANES)

    def body(y_ref, z_ref, w_ref, b_ref, o_ref, t_ref):
        y1 = _gelu(y_ref[...])
        t = _dot(y1, w_ref[...]) + b_ref[...]
        t_ref[...] = t
        z = z_ref[...]
        o_ref[...] = (y1 * _sigmoid(t) * (z * _sigmoid(z))).astype(BF16)

    row = pl.BlockSpec((tr, DS), lambda i: (i, 0))
    return pl.pallas_call(
        body, name="s5_post_fwd", grid=(L // tr,),
        in_specs=[row, pl.BlockSpec((tr, DS), lambda i: (i, 1)), pl.BlockSpec((DS, DS), lambda i: (0, 0)),
                  pl.BlockSpec((1, DS), lambda i: (0, 0))],
        out_specs=[row, row],
        out_shape=[jax.ShapeDtypeStruct((L, 2 * DS), BF16), jax.ShapeDtypeStruct((L, DS), F32)],
        compiler_params=pltpu.CompilerParams(dimension_semantics=("parallel",)),
    )(y_pre, proj_main, glu_w, glu_b)


def _s5_post_bwd(d_ycat, y_pre, proj_main, t_pre, glu_w, DS):
    L = y_pre.shape[0]
    tr = _blk(L, ROW_TILE, SUBLANES)

    def body(dy_ref, y_ref, z_ref, t_ref, w_ref, dyp_ref, dz_ref, dt_ref, y1_ref, gb_ref):
        i = pl.program_id(0)

        @pl.when(i == 0)
        def _():
            gb_ref[...] = jnp.zeros_like(gb_ref)

        dy = dy_ref[...]
        yp = y_ref[...]
        z = z_ref[...]
        y1 = _gelu(yp)
        sg = _sigmoid(t_ref[...])
        sz = _sigmoid(z)
        c = y1 * sg
        d_c = dy * (z * sz)
        dz_ref[...] = (dy * c * (sz * (1.0 + z * (1.0 - sz)))).astype(BF16)
        d_t = d_c * y1 * sg * (1.0 - sg)
        gb_ref[...] += jnp.sum(d_t, axis=0, keepdims=True)
        dt_ref[...] = d_t.astype(BF16)
        y1_ref[...] = y1.astype(BF16)
        d_y1 = d_c * sg + _dot(d_t, w_ref[...], NT)
        dyp_ref[...] = d_y1 * _gelu_grad(yp)

    row = pl.BlockSpec((tr, DS), lambda i: (i, 0))
    return pl.pallas_call(
        body, name="s5_post_bwd", grid=(L // tr,),
        in_specs=[row, row, pl.BlockSpec((tr, DS), lambda i: (i, 1)), row, pl.BlockSpec((DS, DS), lambda i: (0, 0))],
        out_specs=[row, pl.BlockSpec((tr, DS), lambda i: (i, 1)), row, row, pl.BlockSpec((1, DS), lambda i: (0, 0))],
        out_shape=[jax.ShapeDtypeStruct((L, DS), F32), jax.ShapeDtypeStruct((L, 2 * DS), BF16),
                   jax.ShapeDtypeStruct((L, DS), BF16), jax.ShapeDtypeStruct((L, DS), BF16),
                   jax.ShapeDtypeStruct((1, DS), F32)],
        compiler_params=pltpu.CompilerParams(dimension_semantics=("arbitrary",)),
    )(d_ycat, y_pre, proj_main, t_pre, glu_w)


def _row_cumsum(x, reverse=False):
    n = x.shape[0]
    row = lax.broadcasted_iota(jnp.int32, x.shape, 0)
    k = 1
    while k < n:
        if reverse:
            x = x + jnp.where(row < n - k, pltpu.roll(x, n - k, 0), 0.0)
        else:
            x = x + jnp.where(row >= k, pltpu.roll(x, k, 0), 0.0)
        k *= 2
    return x


def _gla_gates(glow, gu_ref, gb_ref):
    a = _dot(glow, gu_ref[...]) + gb_ref[...]
    lg = (jnp.minimum(a, 0.0) - jnp.log(1.0 + jnp.exp(-jnp.abs(a)))) * (1.0 / GLA_TAU)
    ri = lax.broadcasted_iota(jnp.int32, (GLA_CHUNK, GLA_CHUNK), 0)
    ci = lax.broadcasted_iota(jnp.int32, (GLA_CHUNK, GLA_CHUNK), 1)
    b = _row_cumsum(lg)
    b_last = b[GLA_CHUNK - 1:GLA_CHUNK, :]
    return a, b, b_last, ri >= ci


def _gla_specs(DS, DK, DV, c, cmap):
    return [
        pl.BlockSpec((c, DK), lambda n: (cmap(n), 2 * DS // DK)),
        pl.BlockSpec((c, DK), lambda n: (cmap(n), 2 * DS // DK + 1)),
        pl.BlockSpec((c, DV), lambda n: (cmap(n), (2 * DS + 2 * DK) // DV)),
        pl.BlockSpec((c, DV), lambda n: (cmap(n), (2 * DS + 2 * DK) // DV + 1)),
    ]


def _gla_fwd(proj_main, proj_low, gate_up_pad, gate_bias, norm_w, ycat, DS, DK, DV):
    L = proj_main.shape[0]
    nc = L // GLA_CHUNK
    cps = math.gcd(GLA_STEP_CHUNKS, nc)
    nh = DK // GLA_HK
    scale = GLA_HK ** -0.5

    def body(q_ref, k_ref, v_ref, z_ref, gl_ref, gu_ref, gb_ref, nw_ref, _yc_ref, y_ref, sp_ref, at_ref, o_ref, st_ref):
        n = pl.program_id(0)

        @pl.when(n == 0)
        def _():
            st_ref[...] = jnp.zeros_like(st_ref)

        pairs = [(sc, h) for sc in range(cps) for h in range(nh)]
        rows = lambda sc: slice(sc * GLA_CHUNK, (sc + 1) * GLA_CHUNK)
        kcol = lambda h: slice(h * GLA_HK, (h + 1) * GLA_HK)
        vcol = lambda h: slice(h * GLA_HV, (h + 1) * GLA_HV)
        gates = [_gla_gates(gl_ref[rows(sc), :], gu_ref, gb_ref) for sc in range(cps)]
        qe, dec, o_in, kv = {}, {}, {}, {}
        for sc, h in pairs:
            _, b, b_last, mask = gates[sc]
            bh, bl = b[:, kcol(h)], b_last[:, kcol(h)]
            qe[sc, h] = (q_ref[rows(sc), kcol(h)] * scale) * jnp.exp(bh)
            kh = k_ref[rows(sc), kcol(h)]
            vh = v_ref[rows(sc), vcol(h)]
            attn = jnp.where(mask, _dot(qe[sc, h], kh * jnp.exp(-bh), NT), 0.0).astype(BF16)
            at_ref[h, rows(sc), :] = attn
            o_in[sc, h] = _dot(attn, vh)
            kv[sc, h] = _dot(vh, kh * jnp.exp(bl - bh), TN)
            dec[sc, h] = jnp.exp(bl)
        for sc, h in pairs:
            st = st_ref[h]
            sp_ref[sc, h] = st
            o = o_in[sc, h] + _dot(qe[sc, h], st, NT)
            o_ref[rows(sc), vcol(h)] = o
            st_ref[h] = dec[sc, h] * st + kv[sc, h]
            r = lax.rsqrt(jnp.mean(o * o, axis=-1, keepdims=True) + EPS)
            z = z_ref[rows(sc), vcol(h)]
            y_ref[rows(sc), vcol(h)] = (o * r * nw_ref[...] * (z * _sigmoid(z))).astype(BF16)

    c = cps * GLA_CHUNK
    return pl.pallas_call(
        body, name="gla_fwd", grid=(nc // cps,),
        in_specs=_gla_specs(DS, DK, DV, c, lambda n: n) + [
            pl.BlockSpec((c, LANES), lambda n: (n, 0)),
            pl.BlockSpec((LANES, DK), lambda n: (0, 0)),
            pl.BlockSpec((1, DK), lambda n: (0, 0)),
            pl.BlockSpec((1, GLA_HV), lambda n: (0, 0)),
            pl.BlockSpec(memory_space=pl.ANY),
        ],
        out_specs=[pl.BlockSpec((c, DV), lambda n: (n, DS // DV)),
                   pl.BlockSpec((cps, nh, GLA_HV, GLA_HK), lambda n: (n, 0, 0, 0)),
                   pl.BlockSpec((nh, c, GLA_CHUNK), lambda n: (0, n, 0)),
                   pl.BlockSpec((c, DV), lambda n: (n, 0))],
        input_output_aliases={8: 0},
        out_shape=[jax.ShapeDtypeStruct(ycat.shape, BF16), jax.ShapeDtypeStruct((nc, nh, GLA_HV, GLA_HK), F32),
                   jax.ShapeDtypeStruct((nh, L, GLA_CHUNK), BF16), jax.ShapeDtypeStruct((L, DV), F32)],
        scratch_shapes=[pltpu.VMEM((nh, GLA_HV, GLA_HK), F32)],
        compiler_params=pltpu.CompilerParams(dimension_semantics=("arbitrary",)),
    )(proj_main, proj_main, proj_main, proj_main, proj_low, gate_up_pad, gate_bias, norm_w, ycat)


def _gla_bwd(d_ycat, proj_main, proj_low, s_prev, scores, o_pre, gate_up_pad, gate_bias, norm_w, DS, DK, DV):
    L = proj_main.shape[0]
    nc = L // GLA_CHUNK
    cps = math.gcd(GLA_STEP_CHUNKS, nc)
    nh = DK // GLA_HK
    scale = GLA_HK ** -0.5

    def body(dy_ref, q_ref, k_ref, v_ref, z_ref, gl_ref, sp_ref, at_ref, o_ref, gu_ref, gb_ref, nw_ref,
             dg_ref, da_ref, gnw_ref, ggb_ref, dst_ref):
        n = pl.program_id(0)

        @pl.when(n == 0)
        def _():
            dst_ref[...] = jnp.zeros_like(dst_ref)
            gnw_ref[...] = jnp.zeros_like(gnw_ref)
            ggb_ref[...] = jnp.zeros_like(ggb_ref)

        last_row = lax.broadcasted_iota(jnp.int32, (GLA_CHUNK, GLA_HK), 0) == GLA_CHUNK - 1
        nw = nw_ref[...]
        for sc in reversed(range(cps)):
            rs = slice(sc * GLA_CHUNK, (sc + 1) * GLA_CHUNK)
            a, b, b_last, mask = _gla_gates(gl_ref[rs, :], gu_ref, gb_ref)
            for h in range(nh):
                ks = slice(h * GLA_HK, (h + 1) * GLA_HK)
                vs = slice(h * GLA_HV, (h + 1) * GLA_HV)
                bh, bl = b[:, ks], b_last[:, ks]
                e = jnp.exp(bh)
                einv = jnp.exp(-bh)
                etail = jnp.exp(bl - bh)
                dec = jnp.exp(bl)
                qe = (q_ref[rs, ks] * scale) * e
                kh = k_ref[rs, ks]
                ke = kh * einv
                ktail = kh * etail
                vh = v_ref[rs, vs]
                st = sp_ref[sc, h]
                dst = dst_ref[h]
                attn = at_ref[h, rs, :]
                o = o_ref[rs, vs]
                r = lax.rsqrt(jnp.mean(o * o, axis=-1, keepdims=True) + EPS)
                nrm = o * r
                z = z_ref[rs, vs]
                sz = _sigmoid(z)
                dy = dy_ref[rs, vs]
                dg_ref[rs, 2 * DK + DV + h * GLA_HV:2 * DK + DV + (h + 1) * GLA_HV] = (
                    dy * nrm * nw * (sz * (1.0 + z * (1.0 - sz)))).astype(BF16)
                d_on = dy * (z * sz)
                gnw_ref[...] += jnp.sum(d_on * nrm, axis=0, keepdims=True)
                d_n = d_on * nw
                d_o = r * (d_n - nrm * jnp.mean(d_n * nrm, axis=-1, keepdims=True))
                d_attn = jnp.where(mask, _dot(d_o, vh, NT), 0.0)
                dg_ref[rs, 2 * DK + h * GLA_HV:2 * DK + (h + 1) * GLA_HV] = (
                    _dot(attn, d_o, TN) + _dot(ktail, dst, NT)).astype(BF16)
                d_qe = _dot(d_attn, ke) + _dot(d_o, st)
                d_ke = _dot(d_attn, qe, TN)
                d_kt = _dot(vh, dst)
                d_dec = jnp.sum(dst * st, axis=0, keepdims=True)
                dst_ref[h] = dec * dst + _dot(d_o, qe, TN)
                dg_ref[rs, ks] = (d_qe * scale * e).astype(BF16)
                dg_ref[rs, DK + h * GLA_HK:DK + (h + 1) * GLA_HK] = (d_ke * einv + d_kt * etail).astype(BF16)
                d_bl = jnp.sum(d_kt * ktail, axis=0, keepdims=True) + d_dec * dec
                d_b = d_qe * qe - d_ke * ke - d_kt * ktail + jnp.where(last_row, d_bl, 0.0)
                d_lg = _row_cumsum(d_b, reverse=True)
                d_a = d_lg * (1.0 / GLA_TAU) * _sigmoid(-a[:, ks])
                ggb_ref[:, ks] += jnp.sum(d_a, axis=0, keepdims=True)
                da_ref[rs, ks] = d_a.astype(BF16)

    c = cps * GLA_CHUNK
    ns = nc // cps
    rn = lambda n: ns - 1 - n
    return pl.pallas_call(
        body, name="gla_bwd", grid=(ns,),
        in_specs=[pl.BlockSpec((c, DV), lambda n: (rn(n), DS // DV))] + _gla_specs(DS, DK, DV, c, rn) + [
            pl.BlockSpec((c, LANES), lambda n: (rn(n), 0)),
            pl.BlockSpec((cps, nh, GLA_HV, GLA_HK), lambda n: (rn(n), 0, 0, 0)),
            pl.BlockSpec((nh, c, GLA_CHUNK), lambda n: (0, rn(n), 0)),
            pl.BlockSpec((c, DV), lambda n: (rn(n), 0)),
            pl.BlockSpec((LANES, DK), lambda n: (0, 0)),
            pl.BlockSpec((1, DK), lambda n: (0, 0)),
            pl.BlockSpec((1, GLA_HV), lambda n: (0, 0)),
        ],
        out_specs=[pl.BlockSpec((c, 2 * DK + 2 * DV), lambda n: (rn(n), 0)),
                   pl.BlockSpec((c, DK), lambda n: (rn(n), 0)),
                   pl.BlockSpec((1, GLA_HV), lambda n: (0, 0)), pl.BlockSpec((1, DK), lambda n: (0, 0))],
        out_shape=[jax.ShapeDtypeStruct((L, 2 * DK + 2 * DV), BF16),
                   jax.ShapeDtypeStruct((L, DK), BF16),
                   jax.ShapeDtypeStruct((1, GLA_HV), F32), jax.ShapeDtypeStruct((1, DK), F32)],
        scratch_shapes=[pltpu.VMEM((nh, GLA_HV, GLA_HK), F32)],
        compiler_params=pltpu.CompilerParams(dimension_semantics=("arbitrary",)),
    )(d_ycat, proj_main, proj_main, proj_main, proj_main, proj_low, s_prev, scores, o_pre, gate_up_pad, gate_bias, norm_w)


def _adamw_math(w, g, m, v):
    c1 = 1.0 - ADAM_B1 ** ADAM_STEP
    c2 = 1.0 - ADAM_B2 ** ADAM_STEP
    m_ = ADAM_B1 * m + (1.0 - ADAM_B1) * g
    v_ = ADAM_B2 * v + (1.0 - ADAM_B2) * (g * g)
    return -ADAM_LR * ((m_ / c1) / (jnp.sqrt(v_ / c2) + ADAM_EPS) + ADAM_WD * w), m_, v_


def _adamw_small(g_row, g_a, g_bc, ws, ms, vs):
    n = len(ws)
    nvec = n - 6

    def body(*refs):
        grow_ref, ga_ref, gbc_ref = refs[:3]
        w_refs, m_refs, v_refs = refs[3:3 + n], refs[3 + n:3 + 2 * n], refs[3 + 2 * n:3 + 3 * n]
        outs = refs[3 + 3 * n:]
        off = 0
        for i in range(n):
            if i < nvec:
                width = ws[i].shape[1]
                g = grow_ref[:, off:off + width]
                off += width
            elif i < nvec + 2:
                g = ga_ref[i - nvec]
            else:
                g = gbc_ref[i - nvec - 2]
            d, m_, v_ = _adamw_math(w_refs[i][...], g, m_refs[i][...], v_refs[i][...])
            outs[i][...] = g
            outs[n + i][...] = d
            outs[2 * n + i][...] = m_
            outs[3 * n + i][...] = v_

    vm = pl.BlockSpec(memory_space=pltpu.VMEM)
    outs = pl.pallas_call(
        body, name="adamw_small",
        in_specs=[vm] * (3 + 3 * n), out_specs=[vm] * (4 * n),
        out_shape=[jax.ShapeDtypeStruct(w.shape, F32) for w in ws] * 4,
    )(g_row, g_a, g_bc, *ws, *ms, *vs)
    return [outs[k * n:(k + 1) * n] for k in range(4)]


def _my_pos():
    return lax.axis_index("x"), lax.axis_index("y"), lax.axis_index("c")


def _split_start(name, srcs, lands_sd, make_copies, ncopies, after):
    n, m = len(srcs), len(lands_sd)

    def body(*refs):
        send_sems, recv_sems = refs[n + m + len(after)], refs[n + m + len(after) + 1]
        for cp in make_copies(refs[:n], refs[n:n + m], send_sems, recv_sems):
            cp.start()
        refs[-1][...] = jnp.zeros_like(refs[-1])

    hbm = pl.BlockSpec(memory_space=pltpu.HBM)
    sem = pl.BlockSpec(memory_space=pltpu.SEMAPHORE)
    outs = pl.pallas_call(
        body, name=name,
        in_specs=[hbm] * (n + m) + [pl.BlockSpec(memory_space=pl.ANY)] * len(after),
        out_specs=[sem, sem] + [hbm] * (n + m) + [pl.BlockSpec(memory_space=pltpu.VMEM)],
        out_shape=[pltpu.SemaphoreType.DMA((ncopies,)), pltpu.SemaphoreType.DMA((ncopies,))]
        + [pltpu.HBM(s.shape, s.dtype) for s in srcs] + [pltpu.HBM(s.shape, s.dtype) for s in lands_sd]
        + [jax.ShapeDtypeStruct((SUBLANES, LANES), F32)],
        input_output_aliases={i: 2 + i for i in range(n + m)},
        compiler_params=pltpu.CompilerParams(has_side_effects=pltpu.SideEffectType.DATAFLOW_SIDE_EFFECTING),
    )(*[pltpu.with_memory_space_constraint(s, pltpu.HBM) for s in srcs],
      *[pltpu.with_memory_space_constraint(lax.empty(s.shape, s.dtype), pltpu.HBM) for s in lands_sd], *after)
    return outs[0], outs[1], outs[2:2 + n], outs[2 + n:2 + n + m], outs[-1]


def _split_wait(name, send_sems, recv_sems, srcs, lands, make_copies, after):
    n, m = len(srcs), len(lands)

    def body(*refs):
        for cp in make_copies(refs[:n], refs[n:n + m], refs[n + m], refs[n + m + 1]):
            cp.wait_send()
            cp.wait_recv()

    hbm = pl.BlockSpec(memory_space=pltpu.HBM)
    sem = pl.BlockSpec(memory_space=pltpu.SEMAPHORE)
    outs = pl.pallas_call(
        body, name=name,
        in_specs=[hbm] * (n + m) + [sem, sem] + [pl.BlockSpec(memory_space=pl.ANY)] * len(after),
        out_specs=[hbm] * (n + m),
        out_shape=[pltpu.HBM(s.shape, s.dtype) for s in srcs] + [pltpu.HBM(p.shape, p.dtype) for p in lands],
        input_output_aliases={i: i for i in range(n + m)},
        compiler_params=pltpu.CompilerParams(has_side_effects=pltpu.SideEffectType.DATAFLOW_SIDE_EFFECTING),
    )(*srcs, *lands, send_sems, recv_sems, *after)
    return outs[:n], outs[n:]


def _pair_half_copies(srcs, lands, send_sems, recv_sems):
    x, y, c = _my_pos()
    copies = []
    for a in range(len(srcs)):
        hrows = srcs[a].shape[1] // 2
        copies.append(pltpu.make_async_remote_copy(
            src_ref=srcs[a].at[:, pl.ds((1 - c) * hrows, hrows), :], dst_ref=lands[a], send_sem=send_sems.at[a],
            recv_sem=recv_sems.at[a], device_id=(x, y, 1 - c), device_id_type=MESH))
    return copies


def _late_gather_copies(srcs, lands, send_sems, recv_sems):
    x, y, c = _my_pos()
    me = 2 * x + y
    copies = []
    for d in (1, 2, 3):
        to = (x ^ (d >> 1), y ^ (d & 1), c)
        for a in range(len(srcs)):
            hrows = srcs[a].shape[0] // 2
            rows = pl.ds(c * hrows, hrows)
            copies.append(pltpu.make_async_remote_copy(
                src_ref=srcs[a].at[rows, :], dst_ref=lands[a].at[me, rows, :], send_sem=send_sems.at[3 * a + d - 1],
                recv_sem=recv_sems.at[3 * a + d - 1], device_id=to, device_id_type=MESH))
    return copies


def _late_gather_start(shards, after, name):
    lands = [jax.ShapeDtypeStruct((4,) + s.shape, s.dtype) for s in shards]
    return _split_start(name, shards, lands, _late_gather_copies, 3 * len(shards), [after])


def _late_gather_wait(send_sems, recv_sems, shards, lands, after, name):
    return _split_wait(name, send_sems, recv_sems, shards, lands, _late_gather_copies, after)[1]


def _late_gather_pair(lands, name):
    n = len(lands)

    def body(*refs):
        outs = refs[n:2 * n]
        send_sems, recv_sems = refs[2 * n:]
        x, y, c = _my_pos()

        def copy(a, d, half):
            chip = 2 * (x ^ (d >> 1)) + (y ^ (d & 1))
            hrows = lands[a].shape[1] // 2
            sl = outs[a].at[chip, pl.ds(half * hrows, hrows), :]
            return pltpu.make_async_remote_copy(src_ref=sl, dst_ref=sl, send_sem=send_sems.at[3 * a + d - 1],
                                                recv_sem=recv_sems.at[3 * a + d - 1], device_id=(x, y, 1 - c),
                                                device_id_type=MESH)

        pairs = [(a, d) for d in (1, 2, 3) for a in range(n)]
        for a, d in pairs:
            copy(a, d, c).start()
        for a, d in pairs:
            copy(a, d, c).wait_send()
            copy(a, d, 1 - c).wait_recv()

    hbm = pl.BlockSpec(memory_space=pltpu.HBM)
    return pl.pallas_call(
        body, name=name, in_specs=[hbm] * n, out_specs=[hbm] * n,
        out_shape=[jax.ShapeDtypeStruct(p.shape, p.dtype) for p in lands],
        input_output_aliases={i: i for i in range(n)},
        scratch_shapes=[pltpu.SemaphoreType.DMA((3 * n,)), pltpu.SemaphoreType.DMA((3 * n,))],
    )(*lands)


def _pair_exchange(gs):
    n = len(gs)

    def body(*refs):
        ins, outs = refs[:n], refs[n:2 * n]
        send_sems, recv_sems = refs[2 * n:]
        x, y, c = _my_pos()
        sent = []
        for a in range(n):
            hrows = gs[a].shape[1] // 2
            cp = pltpu.make_async_remote_copy(
                src_ref=ins[a].at[:, pl.ds((1 - c) * hrows, hrows), :], dst_ref=outs[a], send_sem=send_sems.at[a],
                recv_sem=recv_sems.at[a], device_id=(x, y, 1 - c), device_id_type=MESH)
            cp.start()
            sent.append(cp)
        for cp in sent:
            cp.wait()

    hbm = pl.BlockSpec(memory_space=pltpu.HBM)
    return pl.pallas_call(
        body, name="grad_pair_exchange", in_specs=[hbm] * n, out_specs=[hbm] * n,
        out_shape=[jax.ShapeDtypeStruct((g.shape[0], g.shape[1] // 2, g.shape[2]), g.dtype) for g in gs],
        scratch_shapes=[pltpu.SemaphoreType.DMA((n,)), pltpu.SemaphoreType.DMA((n,))],
    )(*gs)


def _pair_add(g, got, c_arr, name):
    nk, rows2, cols = g.shape
    hrows = rows2 // 2
    tr = _blk(hrows, 256, 2 * SUBLANES)
    nb = hrows // tr

    def body(c_ref, a_ref, b_ref, o_ref):
        o_ref[...] = (a_ref[...].astype(F32) + b_ref[...].astype(F32)).astype(o_ref.dtype)

    return pl.pallas_call(
        body, name=name,
        grid_spec=pltpu.PrefetchScalarGridSpec(
            num_scalar_prefetch=1, grid=(nk, nb),
            in_specs=[pl.BlockSpec((1, tr, cols), lambda k, i, c_ref: (k, c_ref[0] * nb + i, 0)),
                      pl.BlockSpec((1, tr, cols), lambda k, i, c_ref: (k, i, 0))],
            out_specs=pl.BlockSpec((1, tr, cols), lambda k, i, c_ref: (k, i, 0))),
        out_shape=jax.ShapeDtypeStruct((nk, hrows, cols), g.dtype),
        compiler_params=pltpu.CompilerParams(dimension_semantics=("parallel", "parallel")),
    )(c_arr, g, got)


def _chip_scatter_copies(srcs, lands, send_sems, recv_sems):
    x, y, c = _my_pos()
    copies = []
    for d in (1, 2, 3):
        tx, ty = x ^ (d >> 1), y ^ (d & 1)
        for a in range(len(srcs)):
            copies.append(pltpu.make_async_remote_copy(
                src_ref=srcs[a].at[2 * tx + ty], dst_ref=lands[a].at[d - 1], send_sem=send_sems.at[3 * a + d - 1],
                recv_sem=recv_sems.at[3 * a + d - 1], device_id=(tx, ty, c), device_id_type=MESH))
    return copies


def _chip_scatter_start(pss):
    lands = [jax.ShapeDtypeStruct((3,) + p.shape[1:], p.dtype) for p in pss]
    return _split_start("grad_chip_scatter_start", pss, lands, _chip_scatter_copies, 3 * len(pss), [])


def _chip_scatter_wait(send_sems, recv_sems, srcs, lands, after):
    return _split_wait("grad_chip_scatter_wait", send_sems, recv_sems, srcs, lands, _chip_scatter_copies, [after])


def _chip_sum(ps, got, me_arr, name):
    _, hrows, cols = ps.shape
    tr = _blk(hrows, 256, 2 * SUBLANES)

    def body(me_ref, p_ref, g_ref, o_ref):
        acc = p_ref[0].astype(F32)
        for s in range(3):
            acc = acc + g_ref[s].astype(F32)
        o_ref[...] = acc

    return pl.pallas_call(
        body, name=name,
        grid_spec=pltpu.PrefetchScalarGridSpec(
            num_scalar_prefetch=1, grid=(hrows // tr,),
            in_specs=[pl.BlockSpec((1, tr, cols), lambda i, me_ref: (me_ref[0], i, 0)),
                      pl.BlockSpec((3, tr, cols), lambda i, me_ref: (0, i, 0))],
            out_specs=pl.BlockSpec((tr, cols), lambda i, me_ref: (i, 0))),
        out_shape=jax.ShapeDtypeStruct((hrows, cols), F32),
        compiler_params=pltpu.CompilerParams(dimension_semantics=("parallel",)),
    )(me_arr, ps, got)


def _pair_swap(halves):
    n = len(halves)

    def body(*refs):
        ins, outs = refs[:n], refs[n:2 * n]
        send_sems, recv_sems = refs[2 * n:]
        x, y, c = _my_pos()
        sent = []
        for a in range(n):
            cp = pltpu.make_async_remote_copy(src_ref=ins[a], dst_ref=outs[a], send_sem=send_sems.at[a], recv_sem=recv_sems.at[a],
                                              device_id=(x, y, 1 - c), device_id_type=MESH)
            cp.start()
            sent.append(cp)
        for cp in sent:
            cp.wait()

    hbm = pl.BlockSpec(memory_space=pltpu.HBM)
    return pl.pallas_call(
        body, name="grad_pair_swap", in_specs=[hbm] * n, out_specs=[hbm] * n,
        out_shape=[jax.ShapeDtypeStruct(h.shape, h.dtype) for h in halves],
        scratch_shapes=[pltpu.SemaphoreType.DMA((n,)), pltpu.SemaphoreType.DMA((n,))],
    )(*halves)


def _adamw_sharded(w, g_own, g_other, m, v, c_arr, after, name):
    R, C = w.shape
    hrows = R // 2
    tr = _blk(hrows, 256, SUBLANES)
    nbh = hrows // tr

    def body(c_ref, w_ref, go_ref, gx_ref, m_ref, v_ref, _after_ref, g_ref, d_ref, nm_ref, nv_ref):
        mine = (pl.program_id(0) // nbh) == c_ref[0]
        g_ = jnp.where(mine, go_ref[...], gx_ref[...])
        g_ref[...] = g_
        d_ref[...], nm_ref[...], nv_ref[...] = _adamw_math(w_ref[...], g_, m_ref[...], v_ref[...])

    blk = pl.BlockSpec((tr, C), lambda i, c_ref: (i, 0))
    hblk = pl.BlockSpec((tr, C), lambda i, c_ref: (i % nbh, 0))
    sd = jax.ShapeDtypeStruct((R, C), F32)
    return pl.pallas_call(
        body, name=name,
        grid_spec=pltpu.PrefetchScalarGridSpec(
            num_scalar_prefetch=1, grid=(2 * nbh,),
            in_specs=[blk, hblk, hblk, blk, blk, pl.BlockSpec(memory_space=pl.ANY)], out_specs=[blk] * 4),
        out_shape=[sd] * 4,
        compiler_params=pltpu.CompilerParams(dimension_semantics=("parallel",)),
    )(c_arr, w, g_own, g_other, m, v, after)


def _ar_piece(ref, rows, p):
    start = p * rows
    if rows % SUBLANES == 0:
        start = pl.multiple_of(start, SUBLANES)
    return ref.at[..., pl.ds(start, rows), :]


def _ar_peer(d):
    x, y, c = _my_pos()
    return (x ^ (d >> 2), y ^ ((d >> 1) & 1), c ^ (d & 1))


def _ar_lin(p):
    return 4 * p[0] + 2 * p[1] + p[2]


def _ar_scatter_copies(rows):
    def make(srcs, lands, send_sems, recv_sems):
        n = len(srcs)
        copies = []
        for d in range(1, 8):
            to = _ar_peer(d)
            for a in range(n):
                copies.append(pltpu.make_async_remote_copy(
                    src_ref=_ar_piece(srcs[a], rows[a], _ar_lin(to)), dst_ref=lands[a].at[d],
                    send_sem=send_sems.at[(d - 1) * n + a], recv_sem=recv_sems.at[(d - 1) * n + a], device_id=to,
                    device_id_type=MESH))
        return copies
    return make


def _ar_gather_copies(rows):
    def make(srcs, lands, send_sems, recv_sems):
        n = len(srcs)
        me = _ar_lin(_my_pos())
        copies = []
        for d in range(1, 8):
            for a in range(n):
                copies.append(pltpu.make_async_remote_copy(
                    src_ref=srcs[a], dst_ref=_ar_piece(lands[a], rows[a], me),
                    send_sem=send_sems.at[(d - 1) * n + a], recv_sem=recv_sems.at[(d - 1) * n + a], device_id=_ar_peer(d),
                    device_id_type=MESH))
        return copies
    return make


def _ar_sum(srcs, lands, rows):
    n = len(srcs)

    def body(*refs):
        me = _ar_lin(_my_pos())
        for a in range(n):
            acc = _ar_piece(refs[a], rows[a], me)[...]
            for d in range(1, 8):
                acc = acc + refs[n + a][d]
            refs[2 * n + a][...] = acc

    vm = pl.BlockSpec(memory_space=pltpu.VMEM)
    return pl.pallas_call(
        body, name="allreduce_sum", in_specs=[vm] * (2 * n), out_specs=[vm] * n,
        out_shape=[jax.ShapeDtypeStruct(p.shape[1:], F32) for p in lands],
    )(*srcs, *lands)


def kernel(x, pre_norm_w, w_in, s5_A_re, s5_A_im, s5_B_re, s5_B_im, s5_C_re, s5_C_im, s5_D, s5_log_dt, s5_glu_w, s5_glu_b, gla_gate_up, gla_gate_bias, gla_norm_w, w_out, post_norm_w, loss_target, m_pre_norm_w, m_w_in, m_s5_A_re, m_s5_A_im, m_s5_B_re, m_s5_B_im, m_s5_C_re, m_s5_C_im, m_s5_D, m_s5_log_dt, m_s5_glu_w, m_s5_glu_b, m_gla_gate_up, m_gla_gate_bias, m_gla_norm_w, m_w_out, m_post_norm_w, v_pre_norm_w, v_w_in, v_s5_A_re, v_s5_A_im, v_s5_B_re, v_s5_B_im, v_s5_C_re, v_s5_C_im, v_s5_D, v_s5_log_dt, v_s5_glu_w, v_s5_glu_b, v_gla_gate_up, v_gla_gate_bias, v_gla_norm_w, v_w_out, v_post_norm_w):
    names = ["pre_norm_w", "w_in", "s5_A_re", "s5_A_im", "s5_B_re", "s5_B_im", "s5_C_re", "s5_C_im", "s5_D", "s5_log_dt",
             "s5_glu_w", "s5_glu_b", "gla_gate_up", "gla_gate_bias", "gla_norm_w", "w_out", "post_norm_w"]
    W = dict(zip(names, (pre_norm_w, w_in, s5_A_re, s5_A_im, s5_B_re, s5_B_im, s5_C_re, s5_C_im, s5_D, s5_log_dt,
                         s5_glu_w, s5_glu_b, gla_gate_up, gla_gate_bias, gla_norm_w, w_out, post_norm_w)))
    M = dict(zip(names, (m_pre_norm_w, m_w_in, m_s5_A_re, m_s5_A_im, m_s5_B_re, m_s5_B_im, m_s5_C_re, m_s5_C_im, m_s5_D,
                         m_s5_log_dt, m_s5_glu_w, m_s5_glu_b, m_gla_gate_up, m_gla_gate_bias, m_gla_norm_w, m_w_out,
                         m_post_norm_w)))
    V = dict(zip(names, (v_pre_norm_w, v_w_in, v_s5_A_re, v_s5_A_im, v_s5_B_re, v_s5_B_im, v_s5_C_re, v_s5_C_im, v_s5_D,
                         v_s5_log_dt, v_s5_glu_w, v_s5_glu_b, v_gla_gate_up, v_gla_gate_bias, v_gla_norm_w, v_w_out,
                         v_post_norm_w)))
    sharded = ("w_in", "s5_glu_w", "w_out", "gla_gate_up")

    xb = x[0]
    tgt = loss_target[0]
    L, D = xb.shape
    DS = D // 2
    G = DS // S5_GROUP
    P = S5_STATE
    NB = DS // S5_COLS
    DV = D - DS
    DK = DV // 2
    WM = 2 * DS + 2 * DK + 2 * DV
    nsh = w_in.shape[2]

    chip = 2 * lax.axis_index("x") + lax.axis_index("y")
    own = [jnp.pad(w_in[0].astype(BF16), ((0, 0), (0, -nsh % LANES))), s5_glu_w[0].astype(BF16),
           w_out[0].astype(BF16), gla_gate_up[0]]
    fill = lambda g, o: lax.dynamic_update_index_in_dim(g, o, chip, 0)
    win_ss, win_rs, win_src, win_lands, win_token = _late_gather_start(own[:1], pre_norm_w, "w_in_gather_start")
    h = _prenorm_fwd(xb, pre_norm_w, win_token)

    b_view = lambda t: jnp.transpose(t[0], (0, 2, 1)).reshape(G * S5_GROUP, P)
    b_back = lambda t: jnp.transpose(t.reshape(G, S5_GROUP, P), (0, 2, 1))[None]
    c_view = lambda t: t[0].reshape(G * S5_GROUP, P)
    c_back = lambda t: t.reshape(1, G, S5_GROUP, P)
    small = ["pre_norm_w", "post_norm_w", "s5_D", "s5_glu_b", "gla_gate_bias", "gla_norm_w", "s5_log_dt",
             "s5_A_re", "s5_A_im", "s5_B_re", "s5_B_im", "s5_C_re", "s5_C_im"]
    view = {n: (lambda t: t) for n in small[:7]}
    back = dict(view)
    view.update(s5_A_re=lambda t: t[0], s5_A_im=lambda t: t[0], s5_B_re=b_view, s5_B_im=b_view, s5_C_re=c_view, s5_C_im=c_view)
    back.update(s5_A_re=lambda t: t[None], s5_A_im=lambda t: t[None], s5_B_re=b_back, s5_B_im=b_back, s5_C_re=c_back,
                s5_C_im=c_back)
    Wv = {n: view[n](W[n]) for n in small}
    bbd_re, bbd_im, ct_re, ct_im, tab, ptab = _s5_prep_fwd(
        Wv["s5_A_re"], Wv["s5_A_im"], s5_log_dt, Wv["s5_B_re"], Wv["s5_B_im"], Wv["s5_C_re"], Wv["s5_C_im"],
        h, _blk(L, S5_TIME_BLOCK, SUBLANES) // SUBLANES)
    dvec = s5_D

    for d_ in (W, M, V):
        d_["w_in"], _ = lax.optimization_barrier((d_["w_in"], win_token))
    g_win = _late_gather_wait(win_ss, win_rs, win_src, win_lands,
                              [tab, W["w_in"][0], M["w_in"][0], V["w_in"][0]], "w_in_gather_wait")
    g_win = fill(_late_gather_pair(g_win, "w_in_gather_pair")[0], own[0])
    w_main, w_low = _assemble_w_in(g_win, nsh, WM)
    late_ss, late_rs, late_src, late_lands, late_token = _late_gather_start(own[1:], g_win, "late_gather_start")
    proj_main, proj_low = _in_proj(h, w_main, w_low, late_token)
    y_pre, s_re, s_im = _s5_scan_fwd(proj_main, bbd_re, bbd_im, ct_re, ct_im, dvec, tab, ptab, DS)
    late = _late_gather_wait(late_ss, late_rs, late_src, late_lands, [y_pre], "late_gather_wait")
    late = _late_gather_pair(late, "late_gather_pair")
    g_glu, g_wout, g_gup = [fill(g, o) for g, o in zip(late, own[1:])]
    glu_w = g_glu.reshape(DS, DS)
    wout = g_wout.reshape(D, D)
    gup = jnp.moveaxis(g_gup, 0, 1).reshape(GLA_RANK, DK)
    gup_pad = jnp.pad(gup, ((0, LANES - GLA_RANK), (0, 0))).astype(BF16)
    ycat, t_pre = _s5_post_fwd(y_pre, proj_main, glu_w, s5_glu_b, DS)
    ycat, s_prev, gla_scores, gla_o = _gla_fwd(proj_main, proj_low, gup_pad, gla_gate_bias, gla_norm_w, ycat,
                                               DS, DK, DV)
    mixed = _mm(ycat, wout, name="out_proj")
    loss11, d_mixed, dout, g_post_w = _post_fwd_bwd(mixed, xb, tgt, post_norm_w)

    d_ycat = _mm(d_mixed, wout, tb=True, name="out_proj_dx")
    d_ypre, d_s5, d_t, y1, g_glu_b = _s5_post_bwd(d_ycat, y_pre, proj_main, t_pre, glu_w, DS)
    d_s5, g_D, gct_re, gct_im, gbbd_re, gbbd_im, gab_re, gab_im = _s5_scan_bwd(
        d_ypre, proj_main, s_re, s_im, bbd_re, bbd_im, ct_re, ct_im, dvec, tab, ptab, d_s5, DS)
    d_gla, d_a, g_norm_w, g_gate_bias = _gla_bwd(
        d_ycat, proj_main, proj_low, s_prev, gla_scores, gla_o, gup_pad, gla_gate_bias, gla_norm_w, DS, DK, DV)
    d_low = _mm(d_a, gup_pad, tb=True, out_dtype=BF16, name="gate_dx")
    g_gup_pad = _mm(proj_low, d_a, ta=True, name="gate_dw")
    g_wmain, g_wlow = _in_proj_dw(h, d_s5, d_gla, d_low)

    g_win_sh = _split_w_in_grad(g_wmain, g_wlow, nsh)
    px_ss, px_rs, px_src, px_got, px_token = _split_start(
        "grad_pair_w_in_start", [g_win_sh], [jax.ShapeDtypeStruct((4, D // 2, nsh), BF16)], _pair_half_copies, 1, [])
    g_wout_full = _mm(ycat, d_mixed, ta=True, out_dtype=BF16, bk=L, name="out_proj_dw", after=[px_token])
    g_glu_full = _mm(y1, d_t, ta=True, out_dtype=BF16, bk=L, name="glu_dw", after=[px_token])
    px_src, px_got = _split_wait("grad_pair_w_in_wait", px_ss, px_rs, px_src, px_got, _pair_half_copies,
                                 [g_wout_full, g_glu_full])
    gs = [g_glu_full.reshape(4, DS // 4, DS), g_wout_full.reshape(4, D // 4, D),
          jnp.moveaxis(g_gup_pad[:GLA_RANK].reshape(GLA_RANK, 4, DK // 4), 1, 0)]
    c_arr = lax.axis_index("c").astype(jnp.int32).reshape(1)
    me_arr = chip.astype(jnp.int32).reshape(1)
    got = list(px_got) + list(_pair_exchange(gs))
    gs = list(px_src) + gs
    pss = [_pair_add(g, r, c_arr, "grad_pair_add_" + n) for n, g, r in zip(sharded, gs, got)]
    send_sems, recv_sems, pss, lands, token = _chip_scatter_start(pss)

    dh = _in_proj_dx(d_s5, d_gla, d_low, w_main, w_low, token)
    grad_x, g_pre_w = _prenorm_bwd(xb, dh, dout, pre_norm_w)

    g_a, g_bc, g_ldt = _s5_prep_bwd(Wv["s5_A_re"], Wv["s5_A_im"], s5_log_dt, Wv["s5_B_re"], Wv["s5_B_im"],
                                    gbbd_re, gbbd_im, gct_re, gct_im, gab_re, gab_im)

    g_vecs = jnp.concatenate([g_pre_w, g_post_w, g_D, g_glu_b, g_gate_bias, g_norm_w, g_ldt, loss11], axis=1)
    loss_at = g_vecs.shape[1] - 1
    lanes_pad = -g_vecs.shape[1] % (8 * SUBLANES * LANES)
    g_vecs = jnp.pad(g_vecs, ((0, 0), (0, lanes_pad))).reshape(-1, LANES)
    ar_srcs = [g_vecs, g_a, g_bc]
    ar_rows = [a.shape[-2] // 8 for a in ar_srcs]
    ar_lands = [jax.ShapeDtypeStruct((8,) + a.shape[:-2] + (r, a.shape[-1]), F32) for a, r in zip(ar_srcs, ar_rows)]
    ar_ss, ar_rs, ar_srcs, ar_got, ar_token = _split_start(
        "allreduce_scatter_start", ar_srcs, ar_lands, _ar_scatter_copies(ar_rows), 7 * len(ar_srcs), [])

    pss, rcv = _chip_scatter_wait(send_sems, recv_sems, pss, lands, ar_token)
    halves = [_chip_sum(p, r, me_arr, "grad_chip_sum_" + n) for n, p, r in zip(sharded, pss, rcv)]
    others = _pair_swap(halves)
    ar_srcs, ar_got = _split_wait("allreduce_scatter_wait", ar_ss, ar_rs, ar_srcs, ar_got, _ar_scatter_copies(ar_rows),
                                  [others[0]])
    ar_red = _ar_sum(ar_srcs, ar_got, ar_rows)
    ag_ss, ag_rs, ar_red, ag_full, ag_token = _split_start(
        "allreduce_gather_start", ar_red, [jax.ShapeDtypeStruct(a.shape, F32) for a in ar_srcs],
        _ar_gather_copies(ar_rows), 7 * len(ar_red), [])
    G_out, D_out, M_out, V_out = {}, {}, {}, {}
    for n, g_own, g_other in zip(sharded, halves, others):
        g_, d_, m_, v_ = _adamw_sharded(W[n][0], g_own, g_other, M[n][0], V[n][0], c_arr, ag_token, "adamw_" + n)
        G_out[n], D_out[n], M_out[n], V_out[n] = g_[None], d_[None], m_[None], v_[None]
    ar_red, ag_full = _split_wait("allreduce_gather_wait", ag_ss, ag_rs, ar_red, ag_full, _ar_gather_copies(ar_rows),
                                  [D_out[n] for n in sharded])
    me8 = 2 * chip + lax.axis_index("c")
    r_vecs, r_a, r_bc = [lax.dynamic_update_slice_in_dim(f, r, me8 * rw, axis=f.ndim - 2)
                         for f, r, rw in zip(ag_full, ar_red, ar_rows)]
    r_vecs = r_vecs.reshape(1, -1)
    loss = r_vecs[0, loss_at]
    outs4 = _adamw_small(r_vecs, r_a, r_bc, [Wv[n] for n in small],
                         [view[n](M[n]) for n in small], [view[n](V[n]) for n in small])
    for store, o in zip((G_out, D_out, M_out, V_out), outs4):
        store.update({n: back[n](t) for n, t in zip(small, o)})

    return (loss, grad_x[None], *[G_out[n] for n in names], *[D_out[n] for n in names],
            *[M_out[n] for n in names], *[V_out[n] for n in names])
```

```python
import functools
import math

import jax
import jax.numpy as jnp
from jax import lax
from jax.experimental import pallas as pl
from jax.experimental.pallas import tpu as pltpu

F32 = jnp.float32
BF16 = jnp.bfloat16
HI = lax.Precision.HIGHEST
MESH = pl.DeviceIdType.MESH

EPS = 1e-6
S5_GROUP = 16
S5_STATE = 64
GLA_HK = 128
GLA_HV = 256
GLA_RANK = 16
GLA_TAU = 16.0
GLA_CHUNK = 64
GLA_STEP_CHUNKS = 8
LANES = 128
SUBLANES = 8
S5_COLS = 128
S5_LANES = (S5_COLS // S5_GROUP) * S5_STATE
S5_TIME_BLOCK = 1024
ROW_TILE = 512

ADAM_LR = 0.001
ADAM_B1 = 0.9
ADAM_B2 = 0.999
ADAM_EPS = 1e-08
ADAM_WD = 0.01
ADAM_STEP = 10

GELU_K = math.sqrt(2.0 / math.pi)
GELU_C = 0.044715


def _blk(n, pref, unit=LANES):
    best = None
    b = unit
    while b <= min(n, pref):
        if n % b == 0:
            best = b
        b += unit
    return best if best is not None else n


def _dot(a, b, dn=(((1,), (0,)), ((), ()))):
    return lax.dot_general(a.astype(BF16), b.astype(BF16), dn, preferred_element_type=F32)


def _dot_hi(a, b, dn=(((1,), (0,)), ((), ()))):
    return lax.dot_general(a, b, dn, precision=HI, preferred_element_type=F32)


NN = (((1,), (0,)), ((), ()))
NT = (((1,), (1,)), ((), ()))
TN = (((0,), (0,)), ((), ()))


def _sigmoid(x):
    return 1.0 / (1.0 + jnp.exp(-x))


def _gelu(y):
    return 0.5 * y * (1.0 + jnp.tanh(GELU_K * (y + GELU_C * y * y * y)))


def _gelu_grad(y):
    th = jnp.tanh(GELU_K * (y + GELU_C * y * y * y))
    return 0.5 * (1.0 + th) + 0.5 * y * (1.0 - th * th) * GELU_K * (1.0 + 3.0 * GELU_C * y * y)


def _mm(a, b, *, name, ta=False, tb=False, out_dtype=F32, bm=1024, bn=1024, bk=2048, after=()):
    if ta:
        K, M = a.shape
    else:
        M, K = a.shape
    if tb:
        N, K2 = b.shape
    else:
        K2, N = b.shape
    assert K == K2, (a.shape, b.shape, ta, tb)
    bm, bn, bk = _blk(M, bm), _blk(N, bn), _blk(K, bk)
    nk = K // bk
    dn = (((0 if ta else 1,), (1 if tb else 0,)), ((), ()))

    def body(a_ref, b_ref, *rest):
        o_ref = rest[len(after)]
        if nk == 1:
            o_ref[...] = _dot(a_ref[...], b_ref[...], dn).astype(out_dtype)
            return
        acc_ref = rest[len(after) + 1]
        k = pl.program_id(2)

        @pl.when(k == 0)
        def _():
            acc_ref[...] = jnp.zeros_like(acc_ref)

        acc_ref[...] += _dot(a_ref[...], b_ref[...], dn)

        @pl.when(k == nk - 1)
        def _():
            o_ref[...] = acc_ref[...].astype(out_dtype)

    a_spec = pl.BlockSpec((bk, bm), lambda i, j, k: (k, i)) if ta else pl.BlockSpec((bm, bk), lambda i, j, k: (i, k))
    b_spec = pl.BlockSpec((bn, bk), lambda i, j, k: (j, k)) if tb else pl.BlockSpec((bk, bn), lambda i, j, k: (k, j))
    return pl.pallas_call(
        body,
        name=name,
        grid=(M // bm, N // bn, nk),
        in_specs=[a_spec, b_spec] + [pl.BlockSpec(memory_space=pl.ANY)] * len(after),
        out_specs=pl.BlockSpec((bm, bn), lambda i, j, k: (i, j)),
        out_shape=jax.ShapeDtypeStruct((M, N), out_dtype),
        scratch_shapes=[pltpu.VMEM((bm, bn), F32)] if nk > 1 else [],
        compiler_params=pltpu.CompilerParams(dimension_semantics=("parallel", "parallel", "arbitrary")),
    )(a, b, *after)


def _in_proj(h, w_main, w_low, after):
    M, K = h.shape
    N = w_main.shape[1]
    bm, bn = _blk(M, 1024), _blk(N, 1024)

    def body(h_ref, w_ref, wl_ref, _after_ref, o_ref, ol_ref):
        hv = h_ref[...]
        o_ref[...] = _dot(hv, w_ref[...])

        @pl.when(pl.program_id(1) == 0)
        def _():
            ol_ref[...] = _dot(hv, wl_ref[...])

    return pl.pallas_call(
        body, name="in_proj", grid=(M // bm, N // bn),
        in_specs=[pl.BlockSpec((bm, K), lambda i, j: (i, 0)), pl.BlockSpec((K, bn), lambda i, j: (0, j)),
                  pl.BlockSpec((K, LANES), lambda i, j: (0, 0)), pl.BlockSpec(memory_space=pl.ANY)],
        out_specs=[pl.BlockSpec((bm, bn), lambda i, j: (i, j)), pl.BlockSpec((bm, LANES), lambda i, j: (i, 0))],
        out_shape=[jax.ShapeDtypeStruct((M, N), F32), jax.ShapeDtypeStruct((M, LANES), F32)],
        compiler_params=pltpu.CompilerParams(dimension_semantics=("parallel", "arbitrary")),
    )(h, w_main, w_low, after)


def _in_proj_dx(a1, a2, al, b, bl, after, *, bm=512, bn=1024):
    M, K1 = a1.shape
    K2 = a2.shape[1]
    N = b.shape[0]
    bm, bn = _blk(M, bm, 2 * SUBLANES), _blk(N, bn)

    def body(a1_ref, a2_ref, al_ref, b_ref, bl_ref, _after_ref, o_ref):
        o_ref[...] = (_dot(a1_ref[...], b_ref[:, :K1], NT) + _dot(a2_ref[...], b_ref[:, K1:], NT)
                      + _dot(al_ref[...], bl_ref[...], NT))

    return pl.pallas_call(
        body, name="in_proj_dx", grid=(N // bn, M // bm),
        in_specs=[pl.BlockSpec((bm, K1), lambda j, i: (i, 0)),
                  pl.BlockSpec((bm, K2), lambda j, i: (i, 0)),
                  pl.BlockSpec((bm, LANES), lambda j, i: (i, 0)),
                  pl.BlockSpec((bn, K1 + K2), lambda j, i: (j, 0)),
                  pl.BlockSpec((bn, LANES), lambda j, i: (j, 0)),
                  pl.BlockSpec(memory_space=pl.ANY)],
        out_specs=pl.BlockSpec((bm, bn), lambda j, i: (i, j)),
        out_shape=jax.ShapeDtypeStruct((M, N), F32),
        compiler_params=pltpu.CompilerParams(dimension_semantics=("parallel", "parallel")),
    )(a1, a2, al, b, bl, after)


def _in_proj_dw(a, b1, b2, bl, *, bm=1024, bn=1024, bk=2048):
    K, M = a.shape
    N1, N2 = b1.shape[1], b2.shape[1]
    bm, bk = _blk(M, bm), _blk(K, bk)
    bn = _blk(math.gcd(N1, N2), bn)
    nj1, nj = N1 // bn, (N1 + N2) // bn
    nk = K // bk

    def body(a_ref, b1_ref, b2_ref, bl_ref, o_ref, ol_ref, acc_ref, accl_ref):
        j = pl.program_id(1)
        k = pl.program_id(2)

        @pl.when(k == 0)
        def _():
            acc_ref[...] = jnp.zeros_like(acc_ref)

        @pl.when(j < nj1)
        def _():
            acc_ref[...] += _dot(a_ref[...], b1_ref[...], TN)

        @pl.when(j >= nj1)
        def _():
            acc_ref[...] += _dot(a_ref[...], b2_ref[...], TN)

        @pl.when(k == nk - 1)
        def _():
            o_ref[...] = acc_ref[...].astype(BF16)

        @pl.when(j == 0)
        def _():
            low = _dot(a_ref[...], bl_ref[...], TN)

            @pl.when(k == 0)
            def _():
                accl_ref[...] = low

            @pl.when(k > 0)
            def _():
                accl_ref[...] += low

            @pl.when(k == nk - 1)
            def _():
                ol_ref[...] = accl_ref[...].astype(BF16)

    return pl.pallas_call(
        body, name="in_proj_dw", grid=(M // bm, nj, nk),
        in_specs=[pl.BlockSpec((bk, bm), lambda i, j, k: (k, i)),
                  pl.BlockSpec((bk, bn), lambda i, j, k: (jnp.where(j < nj1, k, nk - 1), jnp.minimum(j, nj1 - 1))),
                  pl.BlockSpec((bk, bn), lambda i, j, k: (jnp.where(j >= nj1, k, 0), jnp.maximum(j - nj1, 0))),
                  pl.BlockSpec((bk, LANES), lambda i, j, k: (jnp.where(j == 0, k, nk - 1), 0))],
        out_specs=[pl.BlockSpec((bm, bn), lambda i, j, k: (i, j)), pl.BlockSpec((bm, LANES), lambda i, j, k: (i, 0))],
        out_shape=[jax.ShapeDtypeStruct((M, N1 + N2), BF16), jax.ShapeDtypeStruct((M, LANES), BF16)],
        scratch_shapes=[pltpu.VMEM((bm, bn), F32), pltpu.VMEM((bm, LANES), F32)],
        compiler_params=pltpu.CompilerParams(dimension_semantics=("parallel", "arbitrary", "arbitrary")),
    )(a, b1, b2, bl)


def _assemble_w_in(g, nsh, wm):
    _, R, nshp = g.shape
    nb_in = nshp // LANES
    nb_main = wm // LANES
    tr = _blk(R, 512, 2 * SUBLANES)
    plan = []
    for b in range(nb_main + 1):
        terms = []
        for k in range(g.shape[0]):
            for i in range(nb_in):
                delta = nsh * k + LANES * i - LANES * b
                lo, hi = max(0, -delta), min(LANES, LANES - delta, nsh - LANES * i)
                if abs(delta) < LANES and hi > lo:
                    terms.append((k, i, delta))
        plan.append(terms)
    deltas = sorted({d for terms in plan for _, _, d in terms if d})

    def body(g_ref, wm_ref, wl_ref):
        src = _iota2((LANES, LANES), 0)
        dst = _iota2((LANES, LANES), 1)
        shift = {d: (dst - src == d).astype(BF16) for d in deltas}
        for b, terms in enumerate(plan):
            acc = None
            for k, i, d in terms:
                blk = g_ref[k, :, LANES * i:LANES * (i + 1)]
                t = _dot(blk, shift[d]) if d else blk.astype(F32)
                acc = t if acc is None else acc + t
            if b < nb_main:
                wm_ref[:, LANES * b:LANES * (b + 1)] = acc.astype(BF16)
            else:
                wl_ref[...] = acc.astype(BF16)

    return pl.pallas_call(
        body, name="assemble_w_in", grid=(R // tr,),
        in_specs=[pl.BlockSpec((g.shape[0], tr, nshp), lambda r: (0, r, 0))],
        out_specs=[pl.BlockSpec((tr, wm), lambda r: (r, 0)), pl.BlockSpec((tr, LANES), lambda r: (r, 0))],
        out_shape=[jax.ShapeDtypeStruct((R, wm), BF16), jax.ShapeDtypeStruct((R, LANES), BF16)],
        compiler_params=pltpu.CompilerParams(dimension_semantics=("parallel",)),
    )(g)


def _split_w_in_grad(g_main, g_low, nsh):
    R, wm = g_main.shape
    nb_main = wm // LANES
    nb_out = -(-nsh // LANES)
    tr = _blk(R, 512, 2 * SUBLANES)
    plan = {}
    for k in range(4):
        for i in range(nb_out):
            width = min(LANES, nsh - LANES * i)
            terms = []
            for b in range(nb_main + 1):
                delta = LANES * b - (nsh * k + LANES * i)
                lo, hi = max(0, delta), min(width, LANES + delta)
                if abs(delta) < LANES and hi > lo:
                    terms.append((b, delta))
            plan[k, i] = (width, terms)
    deltas = sorted({d for _, terms in plan.values() for _, d in terms if d})

    def body(gm_ref, gl_ref, o_ref):
        src = _iota2((LANES, LANES), 0)
        dst = _iota2((LANES, LANES), 1)
        shift = {d: (dst - src == d).astype(BF16) for d in deltas}
        for (k, i), (width, terms) in plan.items():
            acc = None
            for b, d in terms:
                blk = gm_ref[:, LANES * b:LANES * (b + 1)] if b < nb_main else gl_ref[...]
                t = _dot(blk, shift[d]) if d else blk.astype(F32)
                acc = t if acc is None else acc + t
            o_ref[k, :, LANES * i:LANES * i + width] = acc[:, :width].astype(BF16)

    return pl.pallas_call(
        body, name="split_w_in_grad", grid=(R // tr,),
        in_specs=[pl.BlockSpec((tr, wm), lambda r: (r, 0)), pl.BlockSpec((tr, LANES), lambda r: (r, 0))],
        out_specs=pl.BlockSpec((4, tr, nsh), lambda r: (0, r, 0)),
        out_shape=jax.ShapeDtypeStruct((4, R, nsh), BF16),
        compiler_params=pltpu.CompilerParams(dimension_semantics=("parallel",)),
    )(g_main, g_low)


def _prenorm_fwd(x, w, after):
    L, D = x.shape
    tr = _blk(L, ROW_TILE, SUBLANES)

    def body(x_ref, w_ref, _after_ref, h_ref):
        xv = x_ref[...]
        r = lax.rsqrt(jnp.mean(xv * xv, axis=-1, keepdims=True) + EPS)
        h_ref[...] = (xv * r * w_ref[...]).astype(BF16)

    return pl.pallas_call(
        body, name="prenorm_fwd", grid=(L // tr,),
        in_specs=[pl.BlockSpec((tr, D), lambda i: (i, 0)), pl.BlockSpec((1, D), lambda i: (0, 0)),
                  pl.BlockSpec(memory_space=pl.ANY)],
        out_specs=pl.BlockSpec((tr, D), lambda i: (i, 0)),
        out_shape=jax.ShapeDtypeStruct((L, D), BF16),
        compiler_params=pltpu.CompilerParams(dimension_semantics=("parallel",)),
    )(x, w, after)


def _post_fwd_bwd(mixed, x, target, w):
    L, D = x.shape
    tr = _blk(L, ROW_TILE, SUBLANES)
    nsteps = L // tr

    def body(mx_ref, x_ref, t_ref, w_ref, loss_ref, dm_ref, dout_ref, gw_ref, acc_ref):
        i = pl.program_id(0)

        @pl.when(i == 0)
        def _():
            acc_ref[...] = jnp.zeros_like(acc_ref)
            gw_ref[...] = jnp.zeros_like(gw_ref)

        mx = mx_ref[...]
        wv = w_ref[...]
        r = lax.rsqrt(jnp.mean(mx * mx, axis=-1, keepdims=True) + EPS)
        n = mx * r
        err = x_ref[...] + n * wv - t_ref[...]
        acc_ref[...] += jnp.sum(err * err, axis=0, keepdims=True)
        dout = err * (1.0 / D)
        dout_ref[...] = dout
        gw_ref[...] += jnp.sum(dout * n, axis=0, keepdims=True)
        dn = dout * wv
        dm_ref[...] = (r * (dn - n * jnp.mean(dn * n, axis=-1, keepdims=True))).astype(BF16)

        @pl.when(i == nsteps - 1)
        def _():
            loss_ref[...] = jnp.sum(acc_ref[...], axis=-1, keepdims=True) * (0.5 / D)

    row = pl.BlockSpec((tr, D), lambda i: (i, 0))
    vec = pl.BlockSpec((1, D), lambda i: (0, 0))
    return pl.pallas_call(
        body, name="post_fwd_bwd", grid=(nsteps,),
        in_specs=[row, row, row, vec],
        out_specs=[pl.BlockSpec((1, 1), lambda i: (0, 0)), row, row, vec],
        out_shape=[jax.ShapeDtypeStruct((1, 1), F32), jax.ShapeDtypeStruct((L, D), BF16),
                   jax.ShapeDtypeStruct((L, D), F32), jax.ShapeDtypeStruct((1, D), F32)],
        scratch_shapes=[pltpu.VMEM((1, D), F32)],
        compiler_params=pltpu.CompilerParams(dimension_semantics=("arbitrary",)),
    )(mixed, x, target, w)


def _prenorm_bwd(x, dh, dout, w):
    L, D = x.shape
    tr = _blk(L, ROW_TILE, SUBLANES)

    def body(x_ref, a_ref, dout_ref, w_ref, gx_ref, gw_ref):
        i = pl.program_id(0)

        @pl.when(i == 0)
        def _():
            gw_ref[...] = jnp.zeros_like(gw_ref)

        xv = x_ref[...]
        r = lax.rsqrt(jnp.mean(xv * xv, axis=-1, keepdims=True) + EPS)
        n = xv * r
        dh = a_ref[...]
        gw_ref[...] += jnp.sum(dh * n, axis=0, keepdims=True)
        dn = dh * w_ref[...]
        gx_ref[...] = dout_ref[...] + r * (dn - n * jnp.mean(dn * n, axis=-1, keepdims=True))

    row = pl.BlockSpec((tr, D), lambda i: (i, 0))
    vec = pl.BlockSpec((1, D), lambda i: (0, 0))
    return pl.pallas_call(
        body, name="prenorm_bwd", grid=(L // tr,),
        in_specs=[row, row, row, vec],
        out_specs=[row, vec],
        out_shape=[jax.ShapeDtypeStruct((L, D), F32), jax.ShapeDtypeStruct((1, D), F32)],
        compiler_params=pltpu.CompilerParams(dimension_semantics=("arbitrary",)),
    )(x, dh, dout, w)


def _s5_disc(a_re_raw, a_im, dt):
    a_re = jnp.minimum(a_re_raw, -1e-4)
    mag = jnp.exp(a_re * dt)
    ph = a_im * dt
    ab_re = mag * jnp.cos(ph)
    ab_im = mag * jnp.sin(ph)
    inv_n = 1.0 / (a_re * a_re + a_im * a_im)
    ia_re = a_re * inv_n
    ia_im = -a_im * inv_n
    n_re = ab_re - 1.0
    f_re = n_re * ia_re - ab_im * ia_im
    f_im = n_re * ia_im + ab_im * ia_re
    return a_re, ab_re, ab_im, f_re, f_im, ia_re, ia_im


def _iota2(shape, dim):
    return lax.broadcasted_iota(jnp.int32, shape, dim)


def _group_mask(rows, rows_per_group):
    shift = rows_per_group.bit_length() - 1
    return (_iota2((rows, S5_LANES), 0) >> shift) == (_iota2((rows, S5_LANES), 1) >> (S5_STATE.bit_length() - 1))


def _lane_tiler(dtype):
    return ((_iota2((S5_STATE, S5_LANES), 1) & (S5_STATE - 1)) == _iota2((S5_STATE, S5_LANES), 0)).astype(dtype)


def _row_to_col(row, n):
    eye = (_iota2((n, n), 0) == _iota2((n, n), 1)).astype(F32)
    return jnp.sum(eye * row, axis=1, keepdims=True)


def _group_repeat(G):
    return ((_iota2((G * S5_GROUP, G), 0) >> (S5_GROUP.bit_length() - 1)) == _iota2((G * S5_GROUP, G), 1)).astype(F32)


S5_TABS = 18


def _s5_prep_fwd(a_re, a_im, log_dt, b_re, b_im, c_re, c_im, after, seg):
    G, P = a_re.shape
    nb = G * S5_GROUP // S5_COLS
    g8 = S5_COLS // S5_GROUP
    assert seg & (seg - 1) == 0, seg

    def body(are_ref, aim_ref, ldt_ref, bre_ref, bim_ref, cre_ref, cim_ref, _after_ref,
             bbre_ref, bbim_ref, ctre_ref, ctim_ref, tab_ref, pt_ref):
        dt = jnp.exp(_row_to_col(ldt_ref[...], G))
        _, ab_re, ab_im, f_re, f_im, _, _ = _s5_disc(are_ref[...], aim_ref[...], dt)
        rep = _group_repeat(G)
        fx_re = _dot_hi(rep, f_re)
        fx_im = _dot_hi(rep, f_im)
        br, bi = bre_ref[...], bim_ref[...]
        bb_re = fx_re * br - fx_im * bi
        bb_im = fx_re * bi + fx_im * br
        tile_bf = _lane_tiler(BF16)
        mask = _group_mask(S5_COLS, S5_GROUP)
        for jb in range(nb):
            rs = slice(jb * S5_COLS, (jb + 1) * S5_COLS)
            for src, dst in ((bb_re[rs], bbre_ref), (bb_im[rs], bbim_ref), (cre_ref[rs, :], ctre_ref), (cim_ref[rs, :], ctim_ref)):
                dst[jb] = jnp.where(mask, _dot(src, tile_bf), 0.0).astype(BF16)

        tile_f = _lane_tiler(F32)
        mask8 = _group_mask(g8, 1)
        row = _iota2((SUBLANES, S5_LANES), 0)
        slab = (SUBLANES, S5_LANES)
        cmul = lambda p, q: (p[0] * q[0] - p[1] * q[1], p[0] * q[1] + p[1] * q[0])
        for jb in range(nb):
            gs = slice(jb * g8, (jb + 1) * g8)

            def lanes(m):
                v = jnp.sum(jnp.where(mask8, _dot_hi(m[gs], tile_f), 0.0), axis=0, keepdims=True)
                return jnp.broadcast_to(v, slab)

            a1 = (lanes(ab_re), lanes(ab_im))
            tab_ref[jb, 0], tab_ref[jb, 1] = a1

            def powers(k, p):
                s_re = s_im = jnp.zeros(slab, F32)
                for r in range(SUBLANES):
                    s_re = jnp.where(row == r, p[0], s_re)
                    s_im = jnp.where(row == r, p[1], s_im)
                    p = cmul(p, a1)
                pt_ref[jb, 0, _slab(k), :] = s_re
                pt_ref[jb, 1, _slab(k), :] = s_im
                return p

            lax.fori_loop(0, seg // SUBLANES, powers, a1)
            aseg = a1
            for _ in range(seg.bit_length() - 1):
                aseg = cmul(aseg, aseg)
            pw = [aseg]
            for _ in range(1, SUBLANES):
                pw.append(cmul(pw[-1], aseg))
            for lvl, k in enumerate((1, 2, 4)):
                tab_ref[jb, 2 + 2 * lvl] = jnp.where(row >= k, pw[k - 1][0], 0.0)
                tab_ref[jb, 3 + 2 * lvl] = jnp.where(row >= k, pw[k - 1][1], 0.0)
                tab_ref[jb, 10 + 2 * lvl] = jnp.where(row < SUBLANES - k, pw[k - 1][0], 0.0)
                tab_ref[jb, 11 + 2 * lvl] = jnp.where(row < SUBLANES - k, -pw[k - 1][1], 0.0)
            f_r = f_i = r_r = r_i = jnp.zeros(slab, F32)
            for i in range(SUBLANES):
                f_r = jnp.where(row == i, pw[i][0], f_r)
                f_i = jnp.where(row == i, pw[i][1], f_i)
                r_r = jnp.where(row == i, pw[SUBLANES - 1 - i][0], r_r)
                r_i = jnp.where(row == i, -pw[SUBLANES - 1 - i][1], r_i)
            tab_ref[jb, 8] = f_r
            tab_ref[jb, 9] = f_i
            tab_ref[jb, 16] = r_r
            tab_ref[jb, 17] = r_i

    vm = pl.BlockSpec(memory_space=pltpu.VMEM)
    bd = jax.ShapeDtypeStruct((nb, S5_COLS, S5_LANES), BF16)
    return pl.pallas_call(
        body, name="s5_prep_fwd",
        in_specs=[vm] * 7 + [pl.BlockSpec(memory_space=pl.ANY)], out_specs=[vm] * 6,
        out_shape=[bd, bd, bd, bd, jax.ShapeDtypeStruct((nb, S5_TABS, SUBLANES, S5_LANES), F32),
                   jax.ShapeDtypeStruct((nb, 2, seg, S5_LANES), F32)],
    )(a_re, a_im, log_dt, b_re, b_im, c_re, c_im, after)


def _s5_prep_bwd(a_re, a_im, log_dt, b_re, b_im, gbb_re, gbb_im, gct_re, gct_im, gab_re, gab_im):
    G, P = a_re.shape
    nb = G * S5_GROUP // S5_COLS
    g8 = S5_COLS // S5_GROUP

    def body(are_ref, aim_ref, ldt_ref, bre_ref, bim_ref, gbr_ref, gbi_ref, gcr_ref, gci_ref, gar_ref, gai_ref,
             o_a, o_bc, o_ldt):
        dt = jnp.exp(_row_to_col(ldt_ref[...], G))
        a_raw = are_ref[...]
        a_imv = aim_ref[...]
        a_re_c, ab_re, ab_im, f_re, f_im, ia_re, ia_im = _s5_disc(a_raw, a_imv, dt)
        tile_f = _lane_tiler(F32)
        mask = _group_mask(S5_COLS, S5_GROUP)
        mask8 = _group_mask(g8, 1)
        for jb in range(nb):
            rs = slice(jb * S5_COLS, (jb + 1) * S5_COLS)
            gs = slice(jb * g8, (jb + 1) * g8)
            ls = slice(jb * S5_LANES, (jb + 1) * S5_LANES)
            for k, src in enumerate((gbr_ref, gbi_ref, gcr_ref, gci_ref)):
                o_bc[k, rs, :] = _dot_hi(jnp.where(mask, src[jb], 0.0), tile_f, NT)
            for k, src in enumerate((gar_ref, gai_ref)):
                o_a[k, gs, :] = _dot_hi(jnp.where(mask8, src[:, ls], 0.0), tile_f, NT)
        rep = _group_repeat(G)
        fx_re = _dot_hi(rep, f_re)
        fx_im = _dot_hi(rep, f_im)
        gbr, gbi = o_bc[0], o_bc[1]
        br, bi = bre_ref[...], bim_ref[...]
        o_bc[0] = fx_re * gbr + fx_im * gbi
        o_bc[1] = fx_re * gbi - fx_im * gbr
        gf_re = _dot_hi(rep, br * gbr + bi * gbi, TN)
        gf_im = _dot_hi(rep, br * gbi - bi * gbr, TN)
        gab_r = o_a[0] + ia_re * gf_re + ia_im * gf_im
        gab_i = o_a[1] + ia_re * gf_im - ia_im * gf_re
        q_re = f_re * ia_re - f_im * ia_im
        q_im = f_re * ia_im + f_im * ia_re
        ga_re = -(q_re * gf_re + q_im * gf_im)
        ga_im = -(q_re * gf_im - q_im * gf_re)
        gth_re = ab_re * gab_r + ab_im * gab_i
        gth_im = ab_re * gab_i - ab_im * gab_r
        ga_re = ga_re + dt * gth_re
        ga_im = ga_im + dt * gth_im
        gdt = jnp.sum(a_re_c * gth_re + a_imv * gth_im, axis=-1, keepdims=True)
        eye = (_iota2((G, G), 0) == _iota2((G, G), 1)).astype(F32)
        o_ldt[...] = jnp.sum(eye * (gdt * dt), axis=0, keepdims=True)
        slope = jnp.where(a_raw < -1e-4, 1.0, jnp.where(a_raw == -1e-4, 0.5, 0.0))
        o_a[0] = ga_re * slope
        o_a[1] = ga_im

    vm = pl.BlockSpec(memory_space=pltpu.VMEM)
    return pl.pallas_call(
        body, name="s5_prep_bwd",
        in_specs=[vm] * 11, out_specs=[vm] * 3,
        out_shape=[jax.ShapeDtypeStruct((2, G, P), F32), jax.ShapeDtypeStruct((4, G * S5_GROUP, P), F32),
                   jax.ShapeDtypeStruct((1, G), F32)],
    )(a_re, a_im, log_dt, b_re, b_im, gbb_re, gbb_im, gct_re, gct_im, gab_re, gab_im)


def _scan8(xr, xi, tab_ref, base, shifts):
    for lvl, sh in enumerate(shifts):
        mr = tab_ref[0, base + 2 * lvl]
        mi = tab_ref[0, base + 2 * lvl + 1]
        ar = pltpu.roll(xr, sh, 0)
        ai = pltpu.roll(xi, sh, 0)
        xr, xi = xr + mr * ar - mi * ai, xi + mr * ai + mi * ar
    return xr, xi


def _to_segments(src_ref, dst_ref, seg):
    for i in range(seg):
        dst_ref[i * SUBLANES:(i + 1) * SUBLANES, :] = src_ref[pl.ds(i, SUBLANES, stride=seg), :]


def _from_segments(src_ref, dst_ref, seg):
    for i in range(seg):
        dst_ref[pl.ds(i, SUBLANES, stride=seg), :] = src_ref[i * SUBLANES:(i + 1) * SUBLANES, :]


def _slab(i):
    return pl.ds(pl.multiple_of(i * SUBLANES, SUBLANES), SUBLANES)


def _s5_scan_fwd(proj_main, bbd_re, bbd_im, cbd_re, cbd_im, dvec, tab, ptab, DS):
    L = proj_main.shape[0]
    nb = DS // S5_COLS
    tb = _blk(L, S5_TIME_BLOCK, SUBLANES)
    nt = L // tb
    seg = tb // SUBLANES

    def body(u_ref, bre_ref, bim_ref, cre_ref, cim_ref, d_ref, tab_ref, pt_ref, y_ref, sre_ref, sim_ref,
             up_ref, yp_ref, car_ref):
        t = pl.program_id(1)

        @pl.when(t == 0)
        def _():
            car_ref[...] = jnp.zeros_like(car_ref)

        _to_segments(u_ref, up_ref, seg)
        up = up_ref[...]
        sre_ref[...] = _dot(up, bre_ref[0])
        sim_ref[...] = _dot(up, bim_ref[0])
        ar, ai = tab_ref[0, 0], tab_ref[0, 1]

        def pass1(i, x):
            xr = ar * x[0] - ai * x[1] + sre_ref[_slab(i), :]
            xi = ar * x[1] + ai * x[0] + sim_ref[_slab(i), :]
            sre_ref[_slab(i), :] = xr
            sim_ref[_slab(i), :] = xi
            return xr, xi

        zero = jnp.zeros((SUBLANES, S5_LANES), F32)
        er, ei = lax.fori_loop(0, seg, pass1, (zero, zero))
        cin_r, cin_i = car_ref[0], car_ref[1]
        sr, si = _scan8(er, ei, tab_ref, 2, (1, 2, 4))
        pr, pi = tab_ref[0, 8], tab_ref[0, 9]
        sr, si = sr + pr * cin_r - pi * cin_i, si + pr * cin_i + pi * cin_r
        row0 = _iota2((SUBLANES, S5_LANES), 0) == 0
        cr = jnp.where(row0, cin_r, pltpu.roll(sr, 1, 0))
        ci = jnp.where(row0, cin_i, pltpu.roll(si, 1, 0))
        car_ref[0] = jnp.broadcast_to(sr[SUBLANES - 1:SUBLANES, :], sr.shape)
        car_ref[1] = jnp.broadcast_to(si[SUBLANES - 1:SUBLANES, :], si.shape)

        def pass2(i, _):
            qr, qi = pt_ref[0, 0, pl.ds(i, 1), :], pt_ref[0, 1, pl.ds(i, 1), :]
            sre_ref[_slab(i), :] += qr * cr - qi * ci
            sim_ref[_slab(i), :] += qr * ci + qi * cr
            return 0

        lax.fori_loop(0, seg, pass2, 0, unroll=4)
        yp_ref[...] = _dot(sre_ref[...], cre_ref[0], NT) - _dot(sim_ref[...], cim_ref[0], NT) + d_ref[...] * up
        _from_segments(yp_ref, y_ref, seg)

    return pl.pallas_call(
        body, name="s5_scan_fwd", grid=(nb, nt),
        in_specs=[
            pl.BlockSpec((tb, S5_COLS), lambda j, t: (t, j)),
            pl.BlockSpec((1, S5_COLS, S5_LANES), lambda j, t: (j, 0, 0)),
            pl.BlockSpec((1, S5_COLS, S5_LANES), lambda j, t: (j, 0, 0)),
            pl.BlockSpec((1, S5_COLS, S5_LANES), lambda j, t: (j, 0, 0)),
            pl.BlockSpec((1, S5_COLS, S5_LANES), lambda j, t: (j, 0, 0)),
            pl.BlockSpec((1, S5_COLS), lambda j, t: (0, j)),
            pl.BlockSpec((1, S5_TABS, SUBLANES, S5_LANES), lambda j, t: (j, 0, 0, 0)),
            pl.BlockSpec((1, 2, seg, S5_LANES), lambda j, t: (j, 0, 0, 0)),
        ],
        out_specs=[
            pl.BlockSpec((tb, S5_COLS), lambda j, t: (t, j)),
            pl.BlockSpec((tb, S5_LANES), lambda j, t: (t, j)),
            pl.BlockSpec((tb, S5_LANES), lambda j, t: (t, j)),
        ],
        out_shape=[jax.ShapeDtypeStruct((L, DS), F32),
                   jax.ShapeDtypeStruct((L, nb * S5_LANES), F32),
                   jax.ShapeDtypeStruct((L, nb * S5_LANES), F32)],
        scratch_shapes=[pltpu.VMEM((tb, S5_COLS), F32), pltpu.VMEM((tb, S5_COLS), F32),
                        pltpu.VMEM((2, SUBLANES, S5_LANES), F32)],
        compiler_params=pltpu.CompilerParams(dimension_semantics=("parallel", "arbitrary")),
    )(proj_main, bbd_re, bbd_im, cbd_re, cbd_im, dvec, tab, ptab)


def _s5_scan_bwd(dy, proj_main, s_re, s_im, bbd_re, bbd_im, cbd_re, cbd_im, dvec, tab, ptab, d_s5, DS):
    L = proj_main.shape[0]
    nb = DS // S5_COLS
    tb = _blk(L, S5_TIME_BLOCK, SUBLANES)
    nt = L // tb
    seg = tb // SUBLANES
    tb8 = tb // SUBLANES

    def body(dy_ref, u_ref, sre_ref, sim_ref, pre_ref, pim_ref, bre_ref, bim_ref, cre_ref, cim_ref, d_ref, tab_ref, pt_ref,
             _ds5_ref, du_ref, gd_ref, gcre_ref, gcim_ref, gbre_ref, gbim_ref, gare_ref, gaim_ref,
             lre_ref, lim_ref, up_ref, dyp_ref, dup_ref, duo_ref, car_ref):
        t = pl.program_id(1)

        @pl.when(t == 0)
        def _():
            car_ref[...] = jnp.zeros_like(car_ref)
            gd_ref[...] = jnp.zeros_like(gd_ref)
            gcre_ref[...] = jnp.zeros_like(gcre_ref)
            gcim_ref[...] = jnp.zeros_like(gcim_ref)
            gbre_ref[...] = jnp.zeros_like(gbre_ref)
            gbim_ref[...] = jnp.zeros_like(gbim_ref)
            gare_ref[...] = jnp.zeros_like(gare_ref)
            gaim_ref[...] = jnp.zeros_like(gaim_ref)

        _to_segments(dy_ref, dyp_ref, seg)
        _to_segments(u_ref, up_ref, seg)
        dyv = dyp_ref[...]
        u = up_ref[...]
        gd_ref[...] += jnp.sum(dyv * u, axis=0, keepdims=True)
        lre_ref[...] = _dot(dyv, cre_ref[0])
        lim_ref[...] = -_dot(dyv, cim_ref[0])
        gcre_ref[0] += _dot(dyv, sre_ref[...], TN)
        gcim_ref[0] -= _dot(dyv, sim_ref[...], TN)
        ar, ai = tab_ref[0, 0], -tab_ref[0, 1]

        def pass1(k, x):
            i = seg - 1 - k
            xr = ar * x[0] - ai * x[1] + lre_ref[_slab(i), :]
            xi = ar * x[1] + ai * x[0] + lim_ref[_slab(i), :]
            lre_ref[_slab(i), :] = xr
            lim_ref[_slab(i), :] = xi
            return xr, xi

        zero = jnp.zeros((SUBLANES, S5_LANES), F32)
        er, ei = lax.fori_loop(0, seg, pass1, (zero, zero))
        cin_r, cin_i = car_ref[0], car_ref[1]
        lr, li = _scan8(er, ei, tab_ref, 10, (7, 6, 4))
        pr, pi = tab_ref[0, 16], tab_ref[0, 17]
        lr, li = lr + pr * cin_r - pi * cin_i, li + pr * cin_i + pi * cin_r
        rows = _iota2((SUBLANES, S5_LANES), 0)
        cr = jnp.where(rows == SUBLANES - 1, cin_r, pltpu.roll(lr, SUBLANES - 1, 0))
        ci = jnp.where(rows == SUBLANES - 1, cin_i, pltpu.roll(li, SUBLANES - 1, 0))
        car_ref[0] = jnp.broadcast_to(lr[0:1, :], lr.shape)
        car_ref[1] = jnp.broadcast_to(li[0:1, :], li.shape)

        first = (t == nt - 1).astype(F32)
        head_re = jnp.broadcast_to(pre_ref[SUBLANES - 1:SUBLANES, :], zero.shape) * (1.0 - first)
        head_im = jnp.broadcast_to(pim_ref[SUBLANES - 1:SUBLANES, :], zero.shape) * (1.0 - first)
        last = _slab(seg - 1)
        sp0_re = jnp.where(rows == 0, head_re, pltpu.roll(sre_ref[last, :], 1, 0))
        sp0_im = jnp.where(rows == 0, head_im, pltpu.roll(sim_ref[last, :], 1, 0))

        def fix(i, acc, sp_re, sp_im):
            j = seg - 1 - i
            qr, qi = pt_ref[0, 0, pl.ds(j, 1), :], -pt_ref[0, 1, pl.ds(j, 1), :]
            xr = lre_ref[_slab(i), :] + qr * cr - qi * ci
            xi = lim_ref[_slab(i), :] + qr * ci + qi * cr
            lre_ref[_slab(i), :] = xr
            lim_ref[_slab(i), :] = xi
            return acc[0] + sp_re * xr + sp_im * xi, acc[1] + sp_re * xi - sp_im * xr

        def pass2(i, acc):
            prev = _slab(jnp.maximum(i - 1, 0))
            return fix(i, acc, sre_ref[prev, :], sim_ref[prev, :])

        acc_re, acc_im = lax.fori_loop(0, seg, pass2, (zero, zero), unroll=4)
        first_slab = _slab(0)
        d_re, d_im = sp0_re - sre_ref[first_slab, :], sp0_im - sim_ref[first_slab, :]
        x0r, x0i = lre_ref[first_slab, :], lim_ref[first_slab, :]
        acc_re = acc_re + d_re * x0r + d_im * x0i
        acc_im = acc_im + d_re * x0i - d_im * x0r
        gare_ref[...] += jnp.sum(acc_re, axis=0, keepdims=True)
        gaim_ref[...] += jnp.sum(acc_im, axis=0, keepdims=True)
        lre = lre_ref[...]
        lim = lim_ref[...]
        dup_ref[...] = dyv * d_ref[...] + _dot(lre, bre_ref[0], NT) + _dot(lim, bim_ref[0], NT)
        _from_segments(dup_ref, duo_ref, seg)
        du_ref[...] = duo_ref[...].astype(BF16)
        gbre_ref[0] += _dot(u, lre, TN)
        gbim_ref[0] += _dot(u, lim, TN)

    rt = lambda t: nt - 1 - t
    col = pl.BlockSpec((tb, S5_COLS), lambda j, t: (rt(t), j))
    st = pl.BlockSpec((tb, S5_LANES), lambda j, t: (rt(t), j))
    prev = pl.BlockSpec((SUBLANES, S5_LANES), lambda j, t: (jnp.maximum(rt(t) * tb8 - 1, 0), j))
    bmat = pl.BlockSpec((1, S5_COLS, S5_LANES), lambda j, t: (j, 0, 0))
    cmat = bmat
    return pl.pallas_call(
        body, name="s5_scan_bwd", grid=(nb, nt),
        in_specs=[col, col, st, st, prev, prev, bmat, bmat, cmat, cmat,
                  pl.BlockSpec((1, S5_COLS), lambda j, t: (0, j)),
                  pl.BlockSpec((1, S5_TABS, SUBLANES, S5_LANES), lambda j, t: (j, 0, 0, 0)),
                  pl.BlockSpec((1, 2, seg, S5_LANES), lambda j, t: (j, 0, 0, 0)),
                  pl.BlockSpec(memory_space=pl.ANY)],
        out_specs=[col, pl.BlockSpec((1, S5_COLS), lambda j, t: (0, j)), cmat, cmat, bmat, bmat,
                   pl.BlockSpec((1, S5_LANES), lambda j, t: (0, j)), pl.BlockSpec((1, S5_LANES), lambda j, t: (0, j))],
        input_output_aliases={13: 0},
        out_shape=[jax.ShapeDtypeStruct((L, 2 * DS), BF16), jax.ShapeDtypeStruct((1, DS), F32),
                   jax.ShapeDtypeStruct((nb, S5_COLS, S5_LANES), F32), jax.ShapeDtypeStruct((nb, S5_COLS, S5_LANES), F32),
                   jax.ShapeDtypeStruct((nb, S5_COLS, S5_LANES), F32), jax.ShapeDtypeStruct((nb, S5_COLS, S5_LANES), F32),
                   jax.ShapeDtypeStruct((1, nb * S5_LANES), F32), jax.ShapeDtypeStruct((1, nb * S5_LANES), F32)],
        scratch_shapes=[pltpu.VMEM((tb, S5_LANES), F32), pltpu.VMEM((tb, S5_LANES), F32)]
        + [pltpu.VMEM((tb, S5_COLS), F32)] * 4 + [pltpu.VMEM((2, SUBLANES, S5_LANES), F32)],
        compiler_params=pltpu.CompilerParams(dimension_semantics=("parallel", "arbitrary")),
    )(dy, proj_main, s_re, s_im, s_re, s_im, bbd_re, bbd_im, cbd_re, cbd_im, dvec, tab, ptab, d_s5)


def _s5_post_fwd(y_pre, proj_main, glu_w, glu_b, DS):
    L = y_pre.shape[0]
    tr = _blk(L, ROW_TILE, SUBLANES)

    def body(y_ref, z_ref, w_ref, b_ref, o_ref, t_ref):
        y1 = _gelu(y_ref[...])
        t = _dot(y1, w_ref[...]) + b_ref[...]
        t_ref[...] = t
        z = z_ref[...]
        o_ref[...] = (y1 * _sigmoid(t) * (z * _sigmoid(z))).astype(BF16)

    row = pl.BlockSpec((tr, DS), lambda i: (i, 0))
    return pl.pallas_call(
        body, name="s5_post_fwd", grid=(L // tr,),
        in_specs=[row, pl.BlockSpec((tr, DS), lambda i: (i, 1)), pl.BlockSpec((DS, DS), lambda i: (0, 0)),
                  pl.BlockSpec((1, DS), lambda i: (0, 0))],
        out_specs=[row, row],
        out_shape=[jax.ShapeDtypeStruct((L, 2 * DS), BF16), jax.ShapeDtypeStruct((L, DS), F32)],
        compiler_params=pltpu.CompilerParams(dimension_semantics=("parallel",)),
    )(y_pre, proj_main, glu_w, glu_b)


def _s5_post_bwd(d_ycat, y_pre, proj_main, t_pre, glu_w, DS):
    L = y_pre.shape[0]
    tr = _blk(L, ROW_TILE, SUBLANES)

    def body(dy_ref, y_ref, z_ref, t_ref, w_ref, dyp_ref, dz_ref, dt_ref, y1_ref, gb_ref):
        i = pl.program_id(0)

        @pl.when(i == 0)
        def _():
            gb_ref[...] = jnp.zeros_like(gb_ref)

        dy = dy_ref[...]
        yp = y_ref[...]
        z = z_ref[...]
        y1 = _gelu(yp)
        sg = _sigmoid(t_ref[...])
        sz = _sigmoid(z)
        c = y1 * sg
        d_c = dy * (z * sz)
        dz_ref[...] = (dy * c * (sz * (1.0 + z * (1.0 - sz)))).astype(BF16)
        d_t = d_c * y1 * sg * (1.0 - sg)
        gb_ref[...] += jnp.sum(d_t, axis=0, keepdims=True)
        dt_ref[...] = d_t.astype(BF16)
        y1_ref[...] = y1.astype(BF16)
        d_y1 = d_c * sg + _dot(d_t, w_ref[...], NT)
        dyp_ref[...] = d_y1 * _gelu_grad(yp)

    row = pl.BlockSpec((tr, DS), lambda i: (i, 0))
    return pl.pallas_call(
        body, name="s5_post_bwd", grid=(L // tr,),
        in_specs=[row, row, pl.BlockSpec((tr, DS), lambda i: (i, 1)), row, pl.BlockSpec((DS, DS), lambda i: (0, 0))],
        out_specs=[row, pl.BlockSpec((tr, DS), lambda i: (i, 1)), row, row, pl.BlockSpec((1, DS), lambda i: (0, 0))],
        out_shape=[jax.ShapeDtypeStruct((L, DS), F32), jax.ShapeDtypeStruct((L, 2 * DS), BF16),
                   jax.ShapeDtypeStruct((L, DS), BF16), jax.ShapeDtypeStruct((L, DS), BF16),
                   jax.ShapeDtypeStruct((1, DS), F32)],
        compiler_params=pltpu.CompilerParams(dimension_semantics=("arbitrary",)),
    )(d_ycat, y_pre, proj_main, t_pre, glu_w)


def _row_cumsum(x, reverse=False):
    n = x.shape[0]
    row = lax.broadcasted_iota(jnp.int32, x.shape, 0)
    k = 1
    while k < n:
        if reverse:
            x = x + jnp.where(row < n - k, pltpu.roll(x, n - k, 0), 0.0)
        else:
            x = x + jnp.where(row >= k, pltpu.roll(x, k, 0), 0.0)
        k *= 2
    return x


def _gla_gates(glow, gu_ref, gb_ref):
    a = _dot(glow, gu_ref[...]) + gb_ref[...]
    lg = (jnp.minimum(a, 0.0) - jnp.log(1.0 + jnp.exp(-jnp.abs(a)))) * (1.0 / GLA_TAU)
    ri = lax.broadcasted_iota(jnp.int32, (GLA_CHUNK, GLA_CHUNK), 0)
    ci = lax.broadcasted_iota(jnp.int32, (GLA_CHUNK, GLA_CHUNK), 1)
    b = _row_cumsum(lg)
    b_last = b[GLA_CHUNK - 1:GLA_CHUNK, :]
    return a, b, b_last, ri >= ci


def _gla_specs(DS, DK, DV, c, cmap):
    return [
        pl.BlockSpec((c, DK), lambda n: (cmap(n), 2 * DS // DK)),
        pl.BlockSpec((c, DK), lambda n: (cmap(n), 2 * DS // DK + 1)),
        pl.BlockSpec((c, DV), lambda n: (cmap(n), (2 * DS + 2 * DK) // DV)),
        pl.BlockSpec((c, DV), lambda n: (cmap(n), (2 * DS + 2 * DK) // DV + 1)),
    ]


def _gla_fwd(proj_main, proj_low, gate_up_pad, gate_bias, norm_w, ycat, DS, DK, DV):
    L = proj_main.shape[0]
    nc = L // GLA_CHUNK
    cps = math.gcd(GLA_STEP_CHUNKS, nc)
    nh = DK // GLA_HK
    scale = GLA_HK ** -0.5

    def body(q_ref, k_ref, v_ref, z_ref, gl_ref, gu_ref, gb_ref, nw_ref, _yc_ref, y_ref, sp_ref, at_ref, o_ref, st_ref):
        n = pl.program_id(0)

        @pl.when(n == 0)
        def _():
            st_ref[...] = jnp.zeros_like(st_ref)

        pairs = [(sc, h) for sc in range(cps) for h in range(nh)]
        rows = lambda sc: slice(sc * GLA_CHUNK, (sc + 1) * GLA_CHUNK)
        kcol = lambda h: slice(h * GLA_HK, (h + 1) * GLA_HK)
        vcol = lambda h: slice(h * GLA_HV, (h + 1) * GLA_HV)
        gates = [_gla_gates(gl_ref[rows(sc), :], gu_ref, gb_ref) for sc in range(cps)]
        qe, dec, o_in, kv = {}, {}, {}, {}
        for sc, h in pairs:
            _, b, b_last, mask = gates[sc]
            bh, bl = b[:, kcol(h)], b_last[:, kcol(h)]
            qe[sc, h] = (q_ref[rows(sc), kcol(h)] * scale) * jnp.exp(bh)
            kh = k_ref[rows(sc), kcol(h)]
            vh = v_ref[rows(sc), vcol(h)]
            attn = jnp.where(mask, _dot(qe[sc, h], kh * jnp.exp(-bh), NT), 0.0).astype(BF16)
            at_ref[h, rows(sc), :] = attn
            o_in[sc, h] = _dot(attn, vh)
            kv[sc, h] = _dot(vh, kh * jnp.exp(bl - bh), TN)
            dec[sc, h] = jnp.exp(bl)
        for sc, h in pairs:
            st = st_ref[h]
            sp_ref[sc, h] = st
            o = o_in[sc, h] + _dot(qe[sc, h], st, NT)
            o_ref[rows(sc), vcol(h)] = o
            st_ref[h] = dec[sc, h] * st + kv[sc, h]
            r = lax.rsqrt(jnp.mean(o * o, axis=-1, keepdims=True) + EPS)
            z = z_ref[rows(sc), vcol(h)]
            y_ref[rows(sc), vcol(h)] = (o * r * nw_ref[...] * (z * _sigmoid(z))).astype(BF16)

    c = cps * GLA_CHUNK
    return pl.pallas_call(
        body, name="gla_fwd", grid=(nc // cps,),
        in_specs=_gla_specs(DS, DK, DV, c, lambda n: n) + [
            pl.BlockSpec((c, LANES), lambda n: (n, 0)),
            pl.BlockSpec((LANES, DK), lambda n: (0, 0)),
            pl.BlockSpec((1, DK), lambda n: (0, 0)),
            pl.BlockSpec((1, GLA_HV), lambda n: (0, 0)),
            pl.BlockSpec(memory_space=pl.ANY),
        ],
        out_specs=[pl.BlockSpec((c, DV), lambda n: (n, DS // DV)),
                   pl.BlockSpec((cps, nh, GLA_HV, GLA_HK), lambda n: (n, 0, 0, 0)),
                   pl.BlockSpec((nh, c, GLA_CHUNK), lambda n: (0, n, 0)),
                   pl.BlockSpec((c, DV), lambda n: (n, 0))],
        input_output_aliases={8: 0},
        out_shape=[jax.ShapeDtypeStruct(ycat.shape, BF16), jax.ShapeDtypeStruct((nc, nh, GLA_HV, GLA_HK), F32),
                   jax.ShapeDtypeStruct((nh, L, GLA_CHUNK), BF16), jax.ShapeDtypeStruct((L, DV), F32)],
        scratch_shapes=[pltpu.VMEM((nh, GLA_HV, GLA_HK), F32)],
        compiler_params=pltpu.CompilerParams(dimension_semantics=("arbitrary",)),
    )(proj_main, proj_main, proj_main, proj_main, proj_low, gate_up_pad, gate_bias, norm_w, ycat)


def _gla_bwd(d_ycat, proj_main, proj_low, s_prev, scores, o_pre, gate_up_pad, gate_bias, norm_w, DS, DK, DV):
    L = proj_main.shape[0]
    nc = L // GLA_CHUNK
    cps = math.gcd(GLA_STEP_CHUNKS, nc)
    nh = DK // GLA_HK
    scale = GLA_HK ** -0.5

    def body(dy_ref, q_ref, k_ref, v_ref, z_ref, gl_ref, sp_ref, at_ref, o_ref, gu_ref, gb_ref, nw_ref,
             dg_ref, da_ref, gnw_ref, ggb_ref, dst_ref):
        n = pl.program_id(0)

        @pl.when(n == 0)
        def _():
            dst_ref[...] = jnp.zeros_like(dst_ref)
            gnw_ref[...] = jnp.zeros_like(gnw_ref)
            ggb_ref[...] = jnp.zeros_like(ggb_ref)

        last_row = lax.broadcasted_iota(jnp.int32, (GLA_CHUNK, GLA_HK), 0) == GLA_CHUNK - 1
        nw = nw_ref[...]
        for sc in reversed(range(cps)):
            rs = slice(sc * GLA_CHUNK, (sc + 1) * GLA_CHUNK)
            a, b, b_last, mask = _gla_gates(gl_ref[rs, :], gu_ref, gb_ref)
            for h in range(nh):
                ks = slice(h * GLA_HK, (h + 1) * GLA_HK)
                vs = slice(h * GLA_HV, (h + 1) * GLA_HV)
                bh, bl = b[:, ks], b_last[:, ks]
                e = jnp.exp(bh)
                einv = jnp.exp(-bh)
                etail = jnp.exp(bl - bh)
                dec = jnp.exp(bl)
                qe = (q_ref[rs, ks] * scale) * e
                kh = k_ref[rs, ks]
                ke = kh * einv
                ktail = kh * etail
                vh = v_ref[rs, vs]
                st = sp_ref[sc, h]
                dst = dst_ref[h]
                attn = at_ref[h, rs, :]
                o = o_ref[rs, vs]
                r = lax.rsqrt(jnp.mean(o * o, axis=-1, keepdims=True) + EPS)
                nrm = o * r
                z = z_ref[rs, vs]
                sz = _sigmoid(z)
                dy = dy_ref[rs, vs]
                dg_ref[rs, 2 * DK + DV + h * GLA_HV:2 * DK + DV + (h + 1) * GLA_HV] = (
                    dy * nrm * nw * (sz * (1.0 + z * (1.0 - sz)))).astype(BF16)
                d_on = dy * (z * sz)
                gnw_ref[...] += jnp.sum(d_on * nrm, axis=0, keepdims=True)
                d_n = d_on * nw
                d_o = r * (d_n - nrm * jnp.mean(d_n * nrm, axis=-1, keepdims=True))
                d_attn = jnp.where(mask, _dot(d_o, vh, NT), 0.0)
                dg_ref[rs, 2 * DK + h * GLA_HV:2 * DK + (h + 1) * GLA_HV] = (
                    _dot(attn, d_o, TN) + _dot(ktail, dst, NT)).astype(BF16)
                d_qe = _dot(d_attn, ke) + _dot(d_o, st)
                d_ke = _dot(d_attn, qe, TN)
                d_kt = _dot(vh, dst)
                d_dec = jnp.sum(dst * st, axis=0, keepdims=True)
                dst_ref[h] = dec * dst + _dot(d_o, qe, TN)
                dg_ref[rs, ks] = (d_qe * scale * e).astype(BF16)
                dg_ref[rs, DK + h * GLA_HK:DK + (h + 1) * GLA_HK] = (d_ke * einv + d_kt * etail).astype(BF16)
                d_bl = jnp.sum(d_kt * ktail, axis=0, keepdims=True) + d_dec * dec
                d_b = d_qe * qe - d_ke * ke - d_kt * ktail + jnp.where(last_row, d_bl, 0.0)
                d_lg = _row_cumsum(d_b, reverse=True)
                d_a = d_lg * (1.0 / GLA_TAU) * _sigmoid(-a[:, ks])
                ggb_ref[:, ks] += jnp.sum(d_a, axis=0, keepdims=True)
                da_ref[rs, ks] = d_a.astype(BF16)

    c = cps * GLA_CHUNK
    ns = nc // cps
    rn = lambda n: ns - 1 - n
    return pl.pallas_call(
        body, name="gla_bwd", grid=(ns,),
        in_specs=[pl.BlockSpec((c, DV), lambda n: (rn(n), DS // DV))] + _gla_specs(DS, DK, DV, c, rn) + [
            pl.BlockSpec((c, LANES), lambda n: (rn(n), 0)),
            pl.BlockSpec((cps, nh, GLA_HV, GLA_HK), lambda n: (rn(n), 0, 0, 0)),
            pl.BlockSpec((nh, c, GLA_CHUNK), lambda n: (0, rn(n), 0)),
            pl.BlockSpec((c, DV), lambda n: (rn(n), 0)),
            pl.BlockSpec((LANES, DK), lambda n: (0, 0)),
            pl.BlockSpec((1, DK), lambda n: (0, 0)),
            pl.BlockSpec((1, GLA_HV), lambda n: (0, 0)),
        ],
        out_specs=[pl.BlockSpec((c, 2 * DK + 2 * DV), lambda n: (rn(n), 0)),
                   pl.BlockSpec((c, DK), lambda n: (rn(n), 0)),
                   pl.BlockSpec((1, GLA_HV), lambda n: (0, 0)), pl.BlockSpec((1, DK), lambda n: (0, 0))],
        out_shape=[jax.ShapeDtypeStruct((L, 2 * DK + 2 * DV), BF16),
                   jax.ShapeDtypeStruct((L, DK), BF16),
                   jax.ShapeDtypeStruct((1, GLA_HV), F32), jax.ShapeDtypeStruct((1, DK), F32)],
        scratch_shapes=[pltpu.VMEM((nh, GLA_HV, GLA_HK), F32)],
        compiler_params=pltpu.CompilerParams(dimension_semantics=("arbitrary",)),
    )(d_ycat, proj_main, proj_main, proj_main, proj_main, proj_low, s_prev, scores, o_pre, gate_up_pad, gate_bias, norm_w)


def _adamw_math(w, g, m, v):
    c1 = 1.0 - ADAM_B1 ** ADAM_STEP
    c2 = 1.0 - ADAM_B2 ** ADAM_STEP
    m_ = ADAM_B1 * m + (1.0 - ADAM_B1) * g
    v_ = ADAM_B2 * v + (1.0 - ADAM_B2) * (g * g)
    return -ADAM_LR * ((m_ / c1) / (jnp.sqrt(v_ / c2) + ADAM_EPS) + ADAM_WD * w), m_, v_


def _adamw_small(g_row, g_a, g_bc, ws, ms, vs):
    n = len(ws)
    nvec = n - 6

    def body(*refs):
        grow_ref, ga_ref, gbc_ref = refs[:3]
        w_refs, m_refs, v_refs = refs[3:3 + n], refs[3 + n:3 + 2 * n], refs[3 + 2 * n:3 + 3 * n]
        outs = refs[3 + 3 * n:]
        off = 0
        for i in range(n):
            if i < nvec:
                width = ws[i].shape[1]
                g = grow_ref[:, off:off + width]
                off += width
            elif i < nvec + 2:
                g = ga_ref[i - nvec]
            else:
                g = gbc_ref[i - nvec - 2]
            d, m_, v_ = _adamw_math(w_refs[i][...], g, m_refs[i][...], v_refs[i][...])
            outs[i][...] = g
            outs[n + i][...] = d
            outs[2 * n + i][...] = m_
            outs[3 * n + i][...] = v_

    vm = pl.BlockSpec(memory_space=pltpu.VMEM)
    outs = pl.pallas_call(
        body, name="adamw_small",
        in_specs=[vm] * (3 + 3 * n), out_specs=[vm] * (4 * n),
        out_shape=[jax.ShapeDtypeStruct(w.shape, F32) for w in ws] * 4,
    )(g_row, g_a, g_bc, *ws, *ms, *vs)
    return [outs[k * n:(k + 1) * n] for k in range(4)]


def _my_pos():
    return lax.axis_index("x"), lax.axis_index("y"), lax.axis_index("c")


def _split_start(name, srcs, lands_sd, make_copies, ncopies, after):
    n, m = len(srcs), len(lands_sd)

    def body(*refs):
        send_sems, recv_sems = refs[n + m + len(after)], refs[n + m + len(after) + 1]
        for cp in make_copies(refs[:n], refs[n:n + m], send_sems, recv_sems):
            cp.start()
        refs[-1][...] = jnp.zeros_like(refs[-1])

    hbm = pl.BlockSpec(memory_space=pltpu.HBM)
    sem = pl.BlockSpec(memory_space=pltpu.SEMAPHORE)
    outs = pl.pallas_call(
        body, name=name,
        in_specs=[hbm] * (n + m) + [pl.BlockSpec(memory_space=pl.ANY)] * len(after),
        out_specs=[sem, sem] + [hbm] * (n + m) + [pl.BlockSpec(memory_space=pltpu.VMEM)],
        out_shape=[pltpu.SemaphoreType.DMA((ncopies,)), pltpu.SemaphoreType.DMA((ncopies,))]
        + [pltpu.HBM(s.shape, s.dtype) for s in srcs] + [pltpu.HBM(s.shape, s.dtype) for s in lands_sd]
        + [jax.ShapeDtypeStruct((SUBLANES, LANES), F32)],
        input_output_aliases={i: 2 + i for i in range(n + m)},
        compiler_params=pltpu.CompilerParams(has_side_effects=pltpu.SideEffectType.DATAFLOW_SIDE_EFFECTING),
    )(*[pltpu.with_memory_space_constraint(s, pltpu.HBM) for s in srcs],
      *[pltpu.with_memory_space_constraint(lax.empty(s.shape, s.dtype), pltpu.HBM) for s in lands_sd], *after)
    return outs[0], outs[1], outs[2:2 + n], outs[2 + n:2 + n + m], outs[-1]


def _split_wait(name, send_sems, recv_sems, srcs, lands, make_copies, after):
    n, m = len(srcs), len(lands)

    def body(*refs):
        for cp in make_copies(refs[:n], refs[n:n + m], refs[n + m], refs[n + m + 1]):
            cp.wait_send()
            cp.wait_recv()

    hbm = pl.BlockSpec(memory_space=pltpu.HBM)
    sem = pl.BlockSpec(memory_space=pltpu.SEMAPHORE)
    outs = pl.pallas_call(
        body, name=name,
        in_specs=[hbm] * (n + m) + [sem, sem] + [pl.BlockSpec(memory_space=pl.ANY)] * len(after),
        out_specs=[hbm] * (n + m),
        out_shape=[pltpu.HBM(s.shape, s.dtype) for s in srcs] + [pltpu.HBM(p.shape, p.dtype) for p in lands],
        input_output_aliases={i: i for i in range(n + m)},
        compiler_params=pltpu.CompilerParams(has_side_effects=pltpu.SideEffectType.DATAFLOW_SIDE_EFFECTING),
    )(*srcs, *lands, send_sems, recv_sems, *after)
    return outs[:n], outs[n:]


def _pair_half_copies(srcs, lands, send_sems, recv_sems):
    x, y, c = _my_pos()
    copies = []
    for a in range(len(srcs)):
        hrows = srcs[a].shape[1] // 2
        copies.append(pltpu.make_async_remote_copy(
            src_ref=srcs[a].at[:, pl.ds((1 - c) * hrows, hrows), :], dst_ref=lands[a], send_sem=send_sems.at[a],
            recv_sem=recv_sems.at[a], device_id=(x, y, 1 - c), device_id_type=MESH))
    return copies


def _late_gather_copies(srcs, lands, send_sems, recv_sems):
    x, y, c = _my_pos()
    me = 2 * x + y
    copies = []
    for d in (1, 2, 3):
        to = (x ^ (d >> 1), y ^ (d & 1), c)
        for a in range(len(srcs)):
            hrows = srcs[a].shape[0] // 2
            rows = pl.ds(c * hrows, hrows)
            copies.append(pltpu.make_async_remote_copy(
                src_ref=srcs[a].at[rows, :], dst_ref=lands[a].at[me, rows, :], send_sem=send_sems.at[3 * a + d - 1],
                recv_sem=recv_sems.at[3 * a + d - 1], device_id=to, device_id_type=MESH))
    return copies


def _late_gather_start(shards, after, name):
    lands = [jax.ShapeDtypeStruct((4,) + s.shape, s.dtype) for s in shards]
    return _split_start(name, shards, lands, _late_gather_copies, 3 * len(shards), [after])


def _late_gather_wait(send_sems, recv_sems, shards, lands, after, name):
    return _split_wait(name, send_sems, recv_sems, shards, lands, _late_gather_copies, after)[1]


def _late_gather_pair(lands, name):
    n = len(lands)

    def body(*refs):
        outs = refs[n:2 * n]
        send_sems, recv_sems = refs[2 * n:]
        x, y, c = _my_pos()

        def copy(a, d, half):
            chip = 2 * (x ^ (d >> 1)) + (y ^ (d & 1))
            hrows = lands[a].shape[1] // 2
            sl = outs[a].at[chip, pl.ds(half * hrows, hrows), :]
            return pltpu.make_async_remote_copy(src_ref=sl, dst_ref=sl, send_sem=send_sems.at[3 * a + d - 1],
                                                recv_sem=recv_sems.at[3 * a + d - 1], device_id=(x, y, 1 - c),
                                                device_id_type=MESH)

        pairs = [(a, d) for d in (1, 2, 3) for a in range(n)]
        for a, d in pairs:
            copy(a, d, c).start()
        for a, d in pairs:
            copy(a, d, c).wait_send()
            copy(a, d, 1 - c).wait_recv()

    hbm = pl.BlockSpec(memory_space=pltpu.HBM)
    return pl.pallas_call(
        body, name=name, in_specs=[hbm] * n, out_specs=[hbm] * n,
        out_shape=[jax.ShapeDtypeStruct(p.shape, p.dtype) for p in lands],
        input_output_aliases={i: i for i in range(n)},
        scratch_shapes=[pltpu.SemaphoreType.DMA((3 * n,)), pltpu.SemaphoreType.DMA((3 * n,))],
    )(*lands)


def _pair_exchange(gs):
    n = len(gs)

    def body(*refs):
        ins, outs = refs[:n], refs[n:2 * n]
        send_sems, recv_sems = refs[2 * n:]
        x, y, c = _my_pos()
        sent = []
        for a in range(n):
            hrows = gs[a].shape[1] // 2
            cp = pltpu.make_async_remote_copy(
                src_ref=ins[a].at[:, pl.ds((1 - c) * hrows, hrows), :], dst_ref=outs[a], send_sem=send_sems.at[a],
                recv_sem=recv_sems.at[a], device_id=(x, y, 1 - c), device_id_type=MESH)
            cp.start()
            sent.append(cp)
        for cp in sent:
            cp.wait()

    hbm = pl.BlockSpec(memory_space=pltpu.HBM)
    return pl.pallas_call(
        body, name="grad_pair_exchange", in_specs=[hbm] * n, out_specs=[hbm] * n,
        out_shape=[jax.ShapeDtypeStruct((g.shape[0], g.shape[1] // 2, g.shape[2]), g.dtype) for g in gs],
        scratch_shapes=[pltpu.SemaphoreType.DMA((n,)), pltpu.SemaphoreType.DMA((n,))],
    )(*gs)


def _pair_add(g, got, c_arr, name):
    nk, rows2, cols = g.shape
    hrows = rows2 // 2
    tr = _blk(hrows, 256, 2 * SUBLANES)
    nb = hrows // tr

    def body(c_ref, a_ref, b_ref, o_ref):
        o_ref[...] = (a_ref[...].astype(F32) + b_ref[...].astype(F32)).astype(o_ref.dtype)

    return pl.pallas_call(
        body, name=name,
        grid_spec=pltpu.PrefetchScalarGridSpec(
            num_scalar_prefetch=1, grid=(nk, nb),
            in_specs=[pl.BlockSpec((1, tr, cols), lambda k, i, c_ref: (k, c_ref[0] * nb + i, 0)),
                      pl.BlockSpec((1, tr, cols), lambda k, i, c_ref: (k, i, 0))],
            out_specs=pl.BlockSpec((1, tr, cols), lambda k, i, c_ref: (k, i, 0))),
        out_shape=jax.ShapeDtypeStruct((nk, hrows, cols), g.dtype),
        compiler_params=pltpu.CompilerParams(dimension_semantics=("parallel", "parallel")),
    )(c_arr, g, got)


def _chip_scatter_copies(srcs, lands, send_sems, recv_sems):
    x, y, c = _my_pos()
    copies = []
    for d in (1, 2, 3):
        tx, ty = x ^ (d >> 1), y ^ (d & 1)
        for a in range(len(srcs)):
            copies.append(pltpu.make_async_remote_copy(
                src_ref=srcs[a].at[2 * tx + ty], dst_ref=lands[a].at[d - 1], send_sem=send_sems.at[3 * a + d - 1],
                recv_sem=recv_sems.at[3 * a + d - 1], device_id=(tx, ty, c), device_id_type=MESH))
    return copies


def _chip_scatter_start(pss):
    lands = [jax.ShapeDtypeStruct((3,) + p.shape[1:], p.dtype) for p in pss]
    return _split_start("grad_chip_scatter_start", pss, lands, _chip_scatter_copies, 3 * len(pss), [])


def _chip_scatter_wait(send_sems, recv_sems, srcs, lands, after):
    return _split_wait("grad_chip_scatter_wait", send_sems, recv_sems, srcs, lands, _chip_scatter_copies, [after])


def _chip_sum(ps, got, me_arr, name):
    _, hrows, cols = ps.shape
    tr = _blk(hrows, 256, 2 * SUBLANES)

    def body(me_ref, p_ref, g_ref, o_ref):
        acc = p_ref[0].astype(F32)
        for s in range(3):
            acc = acc + g_ref[s].astype(F32)
        o_ref[...] = acc

    return pl.pallas_call(
        body, name=name,
        grid_spec=pltpu.PrefetchScalarGridSpec(
            num_scalar_prefetch=1, grid=(hrows // tr,),
            in_specs=[pl.BlockSpec((1, tr, cols), lambda i, me_ref: (me_ref[0], i, 0)),
                      pl.BlockSpec((3, tr, cols), lambda i, me_ref: (0, i, 0))],
            out_specs=pl.BlockSpec((tr, cols), lambda i, me_ref: (i, 0))),
        out_shape=jax.ShapeDtypeStruct((hrows, cols), F32),
        compiler_params=pltpu.CompilerParams(dimension_semantics=("parallel",)),
    )(me_arr, ps, got)


def _pair_swap(halves):
    n = len(halves)

    def body(*refs):
        ins, outs = refs[:n], refs[n:2 * n]
        send_sems, recv_sems = refs[2 * n:]
        x, y, c = _my_pos()
        sent = []
        for a in range(n):
            cp = pltpu.make_async_remote_copy(src_ref=ins[a], dst_ref=outs[a], send_sem=send_sems.at[a], recv_sem=recv_sems.at[a],
                                              device_id=(x, y, 1 - c), device_id_type=MESH)
            cp.start()
            sent.append(cp)
        for cp in sent:
            cp.wait()

    hbm = pl.BlockSpec(memory_space=pltpu.HBM)
    return pl.pallas_call(
        body, name="grad_pair_swap", in_specs=[hbm] * n, out_specs=[hbm] * n,
        out_shape=[jax.ShapeDtypeStruct(h.shape, h.dtype) for h in halves],
        scratch_shapes=[pltpu.SemaphoreType.DMA((n,)), pltpu.SemaphoreType.DMA((n,))],
    )(*halves)


def _adamw_sharded(w, g_own, g_other, m, v, c_arr, after, name):
    R, C = w.shape
    hrows = R // 2
    tr = _blk(hrows, 256, SUBLANES)
    nbh = hrows // tr

    def body(c_ref, w_ref, go_ref, gx_ref, m_ref, v_ref, _after_ref, g_ref, d_ref, nm_ref, nv_ref):
        mine = (pl.program_id(0) // nbh) == c_ref[0]
        g_ = jnp.where(mine, go_ref[...], gx_ref[...])
        g_ref[...] = g_
        d_ref[...], nm_ref[...], nv_ref[...] = _adamw_math(w_ref[...], g_, m_ref[...], v_ref[...])

    blk = pl.BlockSpec((tr, C), lambda i, c_ref: (i, 0))
    own_blk = pl.BlockSpec((tr, C), lambda i, c_ref: (jnp.where(i // nbh == c_ref[0], i % nbh, 0), 0))
    oth_blk = pl.BlockSpec((tr, C), lambda i, c_ref: (jnp.where(i // nbh == c_ref[0], 0, i % nbh), 0))
    sd = jax.ShapeDtypeStruct((R, C), F32)
    return pl.pallas_call(
        body, name=name,
        grid_spec=pltpu.PrefetchScalarGridSpec(
            num_scalar_prefetch=1, grid=(2 * nbh,),
            in_specs=[blk, own_blk, oth_blk, blk, blk, pl.BlockSpec(memory_space=pl.ANY)], out_specs=[blk] * 4),
        out_shape=[sd] * 4,
        compiler_params=pltpu.CompilerParams(dimension_semantics=("parallel",)),
    )(c_arr, w, g_own, g_other, m, v, after)


def _ar_piece(ref, rows, p):
    start = p * rows
    if rows % SUBLANES == 0:
        start = pl.multiple_of(start, SUBLANES)
    return ref.at[..., pl.ds(start, rows), :]


def _ar_peer(d):
    x, y, c = _my_pos()
    return (x ^ (d >> 2), y ^ ((d >> 1) & 1), c ^ (d & 1))


def _ar_lin(p):
    return 4 * p[0] + 2 * p[1] + p[2]


def _ar_scatter_copies(rows):
    def make(srcs, lands, send_sems, recv_sems):
        n = len(srcs)
        copies = []
        for d in range(1, 8):
            to = _ar_peer(d)
            for a in range(n):
                copies.append(pltpu.make_async_remote_copy(
                    src_ref=_ar_piece(srcs[a], rows[a], _ar_lin(to)), dst_ref=lands[a].at[d],
                    send_sem=send_sems.at[(d - 1) * n + a], recv_sem=recv_sems.at[(d - 1) * n + a], device_id=to,
                    device_id_type=MESH))
        return copies
    return make


def _ar_gather_copies(rows):
    def make(srcs, lands, send_sems, recv_sems):
        n = len(srcs)
        me = _ar_lin(_my_pos())
        copies = []
        for d in range(1, 8):
            for a in range(n):
                copies.append(pltpu.make_async_remote_copy(
                    src_ref=srcs[a], dst_ref=_ar_piece(lands[a], rows[a], me),
                    send_sem=send_sems.at[(d - 1) * n + a], recv_sem=recv_sems.at[(d - 1) * n + a], device_id=_ar_peer(d),
                    device_id_type=MESH))
        return copies
    return make


def _ar_sum(srcs, lands, rows):
    n = len(srcs)

    def body(*refs):
        me = _ar_lin(_my_pos())
        for a in range(n):
            acc = _ar_piece(refs[a], rows[a], me)[...]
            for d in range(1, 8):
                acc = acc + refs[n + a][d]
            refs[2 * n + a][...] = acc

    vm = pl.BlockSpec(memory_space=pltpu.VMEM)
    return pl.pallas_call(
        body, name="allreduce_sum", in_specs=[vm] * (2 * n), out_specs=[vm] * n,
        out_shape=[jax.ShapeDtypeStruct(p.shape[1:], F32) for p in lands],
    )(*srcs, *lands)


def kernel(x, pre_norm_w, w_in, s5_A_re, s5_A_im, s5_B_re, s5_B_im, s5_C_re, s5_C_im, s5_D, s5_log_dt, s5_glu_w, s5_glu_b, gla_gate_up, gla_gate_bias, gla_norm_w, w_out, post_norm_w, loss_target, m_pre_norm_w, m_w_in, m_s5_A_re, m_s5_A_im, m_s5_B_re, m_s5_B_im, m_s5_C_re, m_s5_C_im, m_s5_D, m_s5_log_dt, m_s5_glu_w, m_s5_glu_b, m_gla_gate_up, m_gla_gate_bias, m_gla_norm_w, m_w_out, m_post_norm_w, v_pre_norm_w, v_w_in, v_s5_A_re, v_s5_A_im, v_s5_B_re, v_s5_B_im, v_s5_C_re, v_s5_C_im, v_s5_D, v_s5_log_dt, v_s5_glu_w, v_s5_glu_b, v_gla_gate_up, v_gla_gate_bias, v_gla_norm_w, v_w_out, v_post_norm_w):
    names = ["pre_norm_w", "w_in", "s5_A_re", "s5_A_im", "s5_B_re", "s5_B_im", "s5_C_re", "s5_C_im", "s5_D", "s5_log_dt",
             "s5_glu_w", "s5_glu_b", "gla_gate_up", "gla_gate_bias", "gla_norm_w", "w_out", "post_norm_w"]
    W = dict(zip(names, (pre_norm_w, w_in, s5_A_re, s5_A_im, s5_B_re, s5_B_im, s5_C_re, s5_C_im, s5_D, s5_log_dt,
                         s5_glu_w, s5_glu_b, gla_gate_up, gla_gate_bias, gla_norm_w, w_out, post_norm_w)))
    M = dict(zip(names, (m_pre_norm_w, m_w_in, m_s5_A_re, m_s5_A_im, m_s5_B_re, m_s5_B_im, m_s5_C_re, m_s5_C_im, m_s5_D,
                         m_s5_log_dt, m_s5_glu_w, m_s5_glu_b, m_gla_gate_up, m_gla_gate_bias, m_gla_norm_w, m_w_out,
                         m_post_norm_w)))
    V = dict(zip(names, (v_pre_norm_w, v_w_in, v_s5_A_re, v_s5_A_im, v_s5_B_re, v_s5_B_im, v_s5_C_re, v_s5_C_im, v_s5_D,
                         v_s5_log_dt, v_s5_glu_w, v_s5_glu_b, v_gla_gate_up, v_gla_gate_bias, v_gla_norm_w, v_w_out,
                         v_post_norm_w)))
    sharded = ("w_in", "s5_glu_w", "w_out", "gla_gate_up")

    xb = x[0]
    tgt = loss_target[0]
    L, D = xb.shape
    DS = D // 2
    G = DS // S5_GROUP
    P = S5_STATE
    NB = DS // S5_COLS
    DV = D - DS
    DK = DV // 2
    WM = 2 * DS + 2 * DK + 2 * DV
    nsh = w_in.shape[2]

    chip = 2 * lax.axis_index("x") + lax.axis_index("y")
    own = [jnp.pad(w_in[0].astype(BF16), ((0, 0), (0, -nsh % LANES))), s5_glu_w[0].astype(BF16),
           w_out[0].astype(BF16), gla_gate_up[0]]
    fill = lambda g, o: lax.dynamic_update_index_in_dim(g, o, chip, 0)
    win_ss, win_rs, win_src, win_lands, win_token = _late_gather_start(own[:1], pre_norm_w, "w_in_gather_start")
    h = _prenorm_fwd(xb, pre_norm_w, win_token)

    b_view = lambda t: jnp.transpose(t[0], (0, 2, 1)).reshape(G * S5_GROUP, P)
    b_back = lambda t: jnp.transpose(t.reshape(G, S5_GROUP, P), (0, 2, 1))[None]
    c_view = lambda t: t[0].reshape(G * S5_GROUP, P)
    c_back = lambda t: t.reshape(1, G, S5_GROUP, P)
    small = ["pre_norm_w", "post_norm_w", "s5_D", "s5_glu_b", "gla_gate_bias", "gla_norm_w", "s5_log_dt",
             "s5_A_re", "s5_A_im", "s5_B_re", "s5_B_im", "s5_C_re", "s5_C_im"]
    view = {n: (lambda t: t) for n in small[:7]}
    back = dict(view)
    view.update(s5_A_re=lambda t: t[0], s5_A_im=lambda t: t[0], s5_B_re=b_view, s5_B_im=b_view, s5_C_re=c_view, s5_C_im=c_view)
    back.update(s5_A_re=lambda t: t[None], s5_A_im=lambda t: t[None], s5_B_re=b_back, s5_B_im=b_back, s5_C_re=c_back,
                s5_C_im=c_back)
    Wv = {n: view[n](W[n]) for n in small}
    bbd_re, bbd_im, ct_re, ct_im, tab, ptab = _s5_prep_fwd(
        Wv["s5_A_re"], Wv["s5_A_im"], s5_log_dt, Wv["s5_B_re"], Wv["s5_B_im"], Wv["s5_C_re"], Wv["s5_C_im"],
        h, _blk(L, S5_TIME_BLOCK, SUBLANES) // SUBLANES)
    dvec = s5_D

    for d_ in (W, M, V):
        d_["w_in"], _ = lax.optimization_barrier((d_["w_in"], win_token))
    g_win = _late_gather_wait(win_ss, win_rs, win_src, win_lands,
                              [tab, W["w_in"][0], M["w_in"][0], V["w_in"][0]], "w_in_gather_wait")
    g_win = fill(_late_gather_pair(g_win, "w_in_gather_pair")[0], own[0])
    w_main, w_low = _assemble_w_in(g_win, nsh, WM)
    late_ss, late_rs, late_src, late_lands, late_token = _late_gather_start(own[1:], g_win, "late_gather_start")
    proj_main, proj_low = _in_proj(h, w_main, w_low, late_token)
    y_pre, s_re, s_im = _s5_scan_fwd(proj_main, bbd_re, bbd_im, ct_re, ct_im, dvec, tab, ptab, DS)
    late = _late_gather_wait(late_ss, late_rs, late_src, late_lands, [y_pre], "late_gather_wait")
    late = _late_gather_pair(late, "late_gather_pair")
    g_glu, g_wout, g_gup = [fill(g, o) for g, o in zip(late, own[1:])]
    glu_w = g_glu.reshape(DS, DS)
    wout = g_wout.reshape(D, D)
    gup = jnp.moveaxis(g_gup, 0, 1).reshape(GLA_RANK, DK)
    gup_pad = jnp.pad(gup, ((0, LANES - GLA_RANK), (0, 0))).astype(BF16)
    ycat, t_pre = _s5_post_fwd(y_pre, proj_main, glu_w, s5_glu_b, DS)
    ycat, s_prev, gla_scores, gla_o = _gla_fwd(proj_main, proj_low, gup_pad, gla_gate_bias, gla_norm_w, ycat,
                                               DS, DK, DV)
    mixed = _mm(ycat, wout, name="out_proj")
    loss11, d_mixed, dout, g_post_w = _post_fwd_bwd(mixed, xb, tgt, post_norm_w)

    d_ycat = _mm(d_mixed, wout, tb=True, name="out_proj_dx")
    d_ypre, d_s5, d_t, y1, g_glu_b = _s5_post_bwd(d_ycat, y_pre, proj_main, t_pre, glu_w, DS)
    d_s5, g_D, gct_re, gct_im, gbbd_re, gbbd_im, gab_re, gab_im = _s5_scan_bwd(
        d_ypre, proj_main, s_re, s_im, bbd_re, bbd_im, ct_re, ct_im, dvec, tab, ptab, d_s5, DS)
    d_gla, d_a, g_norm_w, g_gate_bias = _gla_bwd(
        d_ycat, proj_main, proj_low, s_prev, gla_scores, gla_o, gup_pad, gla_gate_bias, gla_norm_w, DS, DK, DV)
    d_low = _mm(d_a, gup_pad, tb=True, out_dtype=BF16, name="gate_dx")
    g_gup_pad = _mm(proj_low, d_a, ta=True, name="gate_dw")
    g_wmain, g_wlow = _in_proj_dw(h, d_s5, d_gla, d_low)

    g_win_sh = _split_w_in_grad(g_wmain, g_wlow, nsh)
    px_ss, px_rs, px_src, px_got, px_token = _split_start(
        "grad_pair_w_in_start", [g_win_sh], [jax.ShapeDtypeStruct((4, D // 2, nsh), BF16)], _pair_half_copies, 1, [])
    g_wout_full = _mm(ycat, d_mixed, ta=True, out_dtype=BF16, name="out_proj_dw", after=[px_token])
    g_glu_full = _mm(y1, d_t, ta=True, out_dtype=BF16, name="glu_dw", after=[px_token])
    px_src, px_got = _split_wait("grad_pair_w_in_wait", px_ss, px_rs, px_src, px_got, _pair_half_copies,
                                 [g_wout_full, g_glu_full])
    gs = [g_glu_full.reshape(4, DS // 4, DS), g_wout_full.reshape(4, D // 4, D),
          jnp.moveaxis(g_gup_pad[:GLA_RANK].reshape(GLA_RANK, 4, DK // 4), 1, 0)]
    c_arr = lax.axis_index("c").astype(jnp.int32).reshape(1)
    me_arr = chip.astype(jnp.int32).reshape(1)
    got = list(px_got) + list(_pair_exchange(gs))
    gs = list(px_src) + gs
    pss = [_pair_add(g, r, c_arr, "grad_pair_add_" + n) for n, g, r in zip(sharded, gs, got)]
    send_sems, recv_sems, pss, lands, token = _chip_scatter_start(pss)

    dh = _in_proj_dx(d_s5, d_gla, d_low, w_main, w_low, token)
    grad_x, g_pre_w = _prenorm_bwd(xb, dh, dout, pre_norm_w)

    g_a, g_bc, g_ldt = _s5_prep_bwd(Wv["s5_A_re"], Wv["s5_A_im"], s5_log_dt, Wv["s5_B_re"], Wv["s5_B_im"],
                                    gbbd_re, gbbd_im, gct_re, gct_im, gab_re, gab_im)

    g_vecs = jnp.concatenate([g_pre_w, g_post_w, g_D, g_glu_b, g_gate_bias, g_norm_w, g_ldt, loss11], axis=1)
    loss_at = g_vecs.shape[1] - 1
    lanes_pad = -g_vecs.shape[1] % (8 * SUBLANES * LANES)
    g_vecs = jnp.pad(g_vecs, ((0, 0), (0, lanes_pad))).reshape(-1, LANES)
    ar_srcs = [g_vecs, g_a, g_bc]
    ar_rows = [a.shape[-2] // 8 for a in ar_srcs]
    ar_lands = [jax.ShapeDtypeStruct((8,) + a.shape[:-2] + (r, a.shape[-1]), F32) for a, r in zip(ar_srcs, ar_rows)]
    ar_ss, ar_rs, ar_srcs, ar_got, ar_token = _split_start(
        "allreduce_scatter_start", ar_srcs, ar_lands, _ar_scatter_copies(ar_rows), 7 * len(ar_srcs), [])

    pss, rcv = _chip_scatter_wait(send_sems, recv_sems, pss, lands, ar_token)
    halves = [_chip_sum(p, r, me_arr, "grad_chip_sum_" + n) for n, p, r in zip(sharded, pss, rcv)]
    others = _pair_swap(halves)
    ar_srcs, ar_got = _split_wait("allreduce_scatter_wait", ar_ss, ar_rs, ar_srcs, ar_got, _ar_scatter_copies(ar_rows),
                                  [others[0]])
    ar_red = _ar_sum(ar_srcs, ar_got, ar_rows)
    ag_ss, ag_rs, ar_red, ag_full, ag_token = _split_start(
        "allreduce_gather_start", ar_red, [jax.ShapeDtypeStruct(a.shape, F32) for a in ar_srcs],
        _ar_gather_copies(ar_rows), 7 * len(ar_red), [])
    G_out, D_out, M_out, V_out = {}, {}, {}, {}
    for n, g_own, g_other in zip(sharded, halves, others):
        g_, d_, m_, v_ = _adamw_sharded(W[n][0], g_own, g_other, M[n][0], V[n][0], c_arr, ag_token, "adamw_" + n)
        G_out[n], D_out[n], M_out[n], V_out[n] = g_[None], d_[None], m_[None], v_[None]
    ar_red, ag_full = _split_wait("allreduce_gather_wait", ag_ss, ag_rs, ar_red, ag_full, _ar_gather_copies(ar_rows),
                                  [D_out[n] for n in sharded])
    me8 = 2 * chip + lax.axis_index("c")
    r_vecs, r_a, r_bc = [lax.dynamic_update_slice_in_dim(f, r, me8 * rw, axis=f.ndim - 2)
                         for f, r, rw in zip(ag_full, ar_red, ar_rows)]
    r_vecs = r_vecs.reshape(1, -1)
    loss = r_vecs[0, loss_at]
    outs4 = _adamw_small(r_vecs, r_a, r_bc, [Wv[n] for n in small],
                         [view[n](M[n]) for n in small], [view[n](V[n]) for n in small])
    for store, o in zip((G_out, D_out, M_out, V_out), outs4):
        store.update({n: back[n](t) for n, t in zip(small, o)})

    return (loss, grad_x[None], *[G_out[n] for n in names], *[D_out[n] for n in names],
            *[M_out[n] for n in names], *[V_out[n] for n in names])
```
